```python
import jax, jax.numpy as jnp
from jax import lax
import numpy as np

D_MODEL = 1024
BATCH = 8
SEQ = 4096
DEPTH = 1

SSD_EXPAND = 2
SSD_D_INNER = SSD_EXPAND * D_MODEL
SSD_HEAD_DIM = 64
SSD_HEADS = SSD_D_INNER // SSD_HEAD_DIM
SSD_GROUPS = 4
SSD_HEADS_PER_GROUP = SSD_HEADS // SSD_GROUPS
SSD_STATE = 128
SSD_CONV = 4
SSD_CHUNK = 128
SSD_BC = SSD_GROUPS * SSD_STATE
SSD_XBC = SSD_D_INNER + 2 * SSD_BC
SSD_IN = SSD_D_INNER + SSD_XBC + SSD_HEADS

SGU_WIDTH = D_MODEL
SGU_GROUPS = 8
SGU_GROUP_DIM = SGU_WIDTH // SGU_GROUPS
SGU_CHUNK = 128
SGU_IN = 2 * SGU_WIDTH

N_BRANCH = 2
GATE_IN = N_BRANCH * D_MODEL
IN_COLS = SSD_IN + SGU_IN + GATE_IN
MIX_WIDTH = SSD_D_INNER + SGU_WIDTH

D_FF = 2816
FFN_CONV = 3

NORM_EPS = 1e-6
LN_EPS = 1e-5

kernel_name = "hybrid_ssd_sgu_convffn_block"


def rms_norm(x, w, eps=NORM_EPS):
    xf = x.astype(jnp.float32)
    y = xf * lax.rsqrt(jnp.mean(xf * xf, axis=-1, keepdims=True) + eps)
    return (y * w.astype(jnp.float32)).astype(x.dtype)


def layer_norm(x, w, b, eps=LN_EPS):
    xf = x.astype(jnp.float32)
    mu = jnp.mean(xf, axis=-1, keepdims=True)
    var = jnp.mean(jnp.square(xf - mu), axis=-1, keepdims=True)
    y = (xf - mu) * lax.rsqrt(var + eps) * w.astype(jnp.float32) + b.astype(jnp.float32)
    return y.astype(x.dtype)


def causal_dwconv(x, w, b):
    k = w.shape[0]
    s = x.shape[1]
    xp = jnp.pad(x, ((0, 0), (k - 1, 0), (0, 0)))
    y = b
    for i in range(k):
        y = y + w[i] * xp[:, i:i + s]
    return y


def segsum_exp(a_cum):
    t = a_cum.shape[-1]
    diff = a_cum[..., :, None] - a_cum[..., None, :]
    mask = jnp.tril(jnp.ones((t, t), dtype=bool))
    return jnp.exp(jnp.where(mask, diff, -jnp.inf))


def ssd_chunked(xh, dt, a, b_mat, c_mat):
    bsz, s = xh.shape[0], xh.shape[1]
    nc = s // SSD_CHUNK
    G, R, L, P, N = SSD_GROUPS, SSD_HEADS_PER_GROUP, SSD_CHUNK, SSD_HEAD_DIM, SSD_STATE
    xdt = (xh.astype(jnp.float32) * dt[..., None]).reshape(bsz, nc, L, G, R, P)
    adt = (dt * a).reshape(bsz, nc, L, G, R)
    a_cum = jnp.cumsum(jnp.transpose(adt, (0, 3, 4, 1, 2)), axis=-1)
    bm = b_mat.reshape(bsz, nc, L, G, N)
    cm = c_mat.reshape(bsz, nc, L, G, N)
    cb = jnp.einsum('bclgn,bcsgn->bgcls', cm, bm)
    decay = segsum_exp(a_cum)
    y_diag = jnp.einsum('bgcls,bgrcls,bcsgrp->bclgrp', cb, decay, xdt)
    decay_states = jnp.exp(a_cum[..., -1:] - a_cum)
    states = jnp.einsum('bcsgn,bgrcs,bcsgrp->bcgrpn', bm, decay_states, xdt)
    chunk_tot = jnp.pad(a_cum[..., -1], ((0, 0), (0, 0), (0, 0), (1, 0)))
    decay_chunk = segsum_exp(jnp.cumsum(chunk_tot, axis=-1))
    states = jnp.concatenate([jnp.zeros_like(states[:, :1]), states], axis=1)
    states = jnp.einsum('bgrzc,bcgrpn->bzgrpn', decay_chunk, states)[:, :-1]
    y_off = jnp.einsum('bclgn,bcgrpn,bgrcl->bclgrp', cm, states, jnp.exp(a_cum))
    return (y_diag + y_off).reshape(bsz, s, SSD_HEADS, P)


def ssd_branch(z, xbc, dt_raw, conv_w, conv_b, dt_bias, a_log, d_skip, norm_w):
    xbc = jax.nn.silu(causal_dwconv(xbc, conv_w, conv_b))
    xs, b_mat, c_mat = jnp.split(xbc, [SSD_D_INNER, SSD_D_INNER + SSD_BC], axis=-1)
    bsz, s = xs.shape[0], xs.shape[1]
    xh = xs.reshape(bsz, s, SSD_HEADS, SSD_HEAD_DIM)
    dt = jax.nn.softplus(dt_raw.astype(jnp.float32) + dt_bias.astype(jnp.float32))
    a = -jnp.exp(a_log.astype(jnp.float32))
    y = ssd_chunked(xh, dt, a,
                    b_mat.reshape(bsz, s, SSD_GROUPS, SSD_STATE),
                    c_mat.reshape(bsz, s, SSD_GROUPS, SSD_STATE))
    y = y + d_skip.astype(jnp.float32)[:, None] * xh.astype(jnp.float32)
    y = y.reshape(bsz, s, SSD_D_INNER).astype(xs.dtype)
    return rms_norm(y * jax.nn.silu(z), norm_w)


def sgu_branch(uv, ln_w, ln_b, w_spatial, b_spatial):
    uv = jax.nn.gelu(uv)
    u, v = jnp.split(uv, 2, axis=-1)
    v = layer_norm(v, ln_w, ln_b)
    bsz, s = v.shape[0], v.shape[1]
    nc = s // SGU_CHUNK
    vc = v.reshape(bsz, nc, SGU_CHUNK, SGU_GROUPS, SGU_GROUP_DIM)
    w_causal = w_spatial * jnp.tril(jnp.ones((SGU_CHUNK, SGU_CHUNK), w_spatial.dtype))
    mixed = jnp.einsum('gts,bcsgd->bctgd', w_causal, vc) + b_spatial.T[None, None, :, :, None]
    return u * mixed.reshape(bsz, s, SGU_WIDTH)


def _fwd_setup_inputs(seed: int = 0) -> dict:
    key = jax.random.key(seed)
    ks = jax.random.split(key, 24)
    f32 = jnp.float32

    def nrm(k, shape, scale):
        return jax.random.normal(k, shape, f32) * scale

    x = jax.random.normal(ks[0], (BATCH, SEQ, D_MODEL), f32)
    norm1_w = 1.0 + nrm(ks[1], (DEPTH, D_MODEL), 0.02)
    w_in = nrm(ks[2], (DEPTH, D_MODEL, IN_COLS), D_MODEL ** -0.5)
    b_gate = nrm(ks[3], (DEPTH, GATE_IN), 0.1)
    conv_a_w = nrm(ks[4], (DEPTH, SSD_CONV, SSD_XBC), SSD_CONV ** -0.5)
    conv_a_b = nrm(ks[5], (DEPTH, SSD_XBC), 0.02)
    dt0 = jnp.exp(jax.random.uniform(ks[6], (DEPTH, SSD_HEADS), f32,
                                     float(np.log(1e-3)), float(np.log(1e-1))))
    dt_bias = dt0 + jnp.log(-jnp.expm1(-dt0))
    a_log = jnp.log(jax.random.uniform(ks[7], (DEPTH, SSD_HEADS), f32, 1.0, 16.0))
    d_skip = 1.0 + nrm(ks[8], (DEPTH, SSD_HEADS), 0.1)
    ssd_norm_w = 1.0 + nrm(ks[9], (DEPTH, SSD_D_INNER), 0.02)
    uv_b = nrm(ks[10], (DEPTH, SGU_IN), 0.02)
    v_ln_w = 1.0 + nrm(ks[11], (DEPTH, SGU_WIDTH), 0.02)
    v_ln_b = nrm(ks[12], (DEPTH, SGU_WIDTH), 0.02)
    w_spatial = nrm(ks[13], (DEPTH, SGU_GROUPS, SGU_CHUNK, SGU_CHUNK), SGU_CHUNK ** -0.5)
    b_spatial = 1.0 + nrm(ks[14], (DEPTH, SGU_GROUPS, SGU_CHUNK), 0.1)
    w_branch = jnp.concatenate(
        [nrm(ks[15], (DEPTH, SSD_D_INNER, D_MODEL), SSD_D_INNER ** -0.5),
         nrm(ks[16], (DEPTH, SGU_WIDTH, D_MODEL), SGU_WIDTH ** -0.5)], axis=1)
    w_out = nrm(ks[17], (DEPTH, D_MODEL, D_MODEL), D_MODEL ** -0.5)
    norm2_w = 1.0 + nrm(ks[18], (DEPTH, D_MODEL), 0.02)
    w_up = nrm(ks[19], (DEPTH, D_MODEL, 2 * D_FF), D_MODEL ** -0.5)
    conv_f_w = nrm(ks[20], (DEPTH, FFN_CONV, 2 * D_FF), FFN_CONV ** -0.5)
    conv_f_b = nrm(ks[21], (DEPTH, 2 * D_FF), 0.02)
    w_down = nrm(ks[22], (DEPTH, D_FF, D_MODEL), D_FF ** -0.5)
    final_norm_w = 1.0 + nrm(ks[23], (D_MODEL,), 0.02)
    return {"x": x, "norm1_w": norm1_w, "w_in": w_in, "b_gate": b_gate,
            "conv_a_w": conv_a_w, "conv_a_b": conv_a_b, "dt_bias": dt_bias,
            "a_log": a_log, "d_skip": d_skip, "ssd_norm_w": ssd_norm_w,
            "uv_b": uv_b, "v_ln_w": v_ln_w, "v_ln_b": v_ln_b,
            "w_spatial": w_spatial, "b_spatial": b_spatial, "w_branch": w_branch,
            "w_out": w_out, "norm2_w": norm2_w, "w_up": w_up, "conv_f_w": conv_f_w,
            "conv_f_b": conv_f_b, "w_down": w_down, "final_norm_w": final_norm_w}


def _fwd_reference(x, norm1_w, w_in, b_gate, conv_a_w, conv_a_b, dt_bias, a_log, d_skip,
              ssd_norm_w, uv_b, v_ln_w, v_ln_b, w_spatial, b_spatial, w_branch, w_out,
              norm2_w, w_up, conv_f_w, conv_f_b, w_down, final_norm_w):
    h = x
    for l in range(DEPTH):
        n = rms_norm(h, norm1_w[l])
        proj = n @ w_in[l]
        z, xbc, dt_raw, uv, gates = jnp.split(
            proj, [SSD_D_INNER, SSD_D_INNER + SSD_XBC, SSD_IN, SSD_IN + SGU_IN], axis=-1)
        y_a = ssd_branch(z, xbc, dt_raw, conv_a_w[l], conv_a_b[l], dt_bias[l],
                         a_log[l], d_skip[l], ssd_norm_w[l])
        y_b = sgu_branch(uv + uv_b[l], v_ln_w[l], v_ln_b[l],
                         w_spatial[l], b_spatial[l])
        g_a, g_b = jnp.split(jax.nn.sigmoid(gates + b_gate[l]), 2, axis=-1)
        p_a = y_a @ w_branch[l, :SSD_D_INNER]
        p_b = y_b @ w_branch[l, SSD_D_INNER:]
        h = h + (g_a * p_a + g_b * p_b) @ w_out[l]
        n = rms_norm(h, norm2_w[l])
        up = causal_dwconv(n @ w_up[l], conv_f_w[l], conv_f_b[l])
        a_up, v_up = jnp.split(up, 2, axis=-1)
        h = h + (jax.nn.silu(a_up) * v_up) @ w_down[l]
    return rms_norm(h, final_norm_w)


import jax as _jax
import jax.numpy as _jnp

TWIN_FORMAT = 'train_step'
FWD_PARAMS = ['x', 'norm1_w', 'w_in', 'b_gate', 'conv_a_w', 'conv_a_b', 'dt_bias', 'a_log', 'd_skip', 'ssd_norm_w', 'uv_b', 'v_ln_w', 'v_ln_b', 'w_spatial', 'b_spatial', 'w_branch', 'w_out', 'norm2_w', 'w_up', 'conv_f_w', 'conv_f_b', 'w_down', 'final_norm_w']
TWIN_WEIGHTS = ['norm1_w', 'w_in', 'b_gate', 'conv_a_w', 'conv_a_b', 'dt_bias', 'a_log', 'd_skip', 'ssd_norm_w', 'uv_b', 'v_ln_w', 'v_ln_b', 'w_spatial', 'b_spatial', 'w_branch', 'w_out', 'norm2_w', 'w_up', 'conv_f_w', 'conv_f_b', 'w_down', 'final_norm_w']
TWIN_DIFF_INPUT = 'x'
TWIN_INPUTS = ['x', 'norm1_w', 'w_in', 'b_gate', 'conv_a_w', 'conv_a_b', 'dt_bias', 'a_log', 'd_skip', 'ssd_norm_w', 'uv_b', 'v_ln_w', 'v_ln_b', 'w_spatial', 'b_spatial', 'w_branch', 'w_out', 'norm2_w', 'w_up', 'conv_f_w', 'conv_f_b', 'w_down', 'final_norm_w', 'loss_target', 'm_norm1_w', 'm_w_in', 'm_b_gate', 'm_conv_a_w', 'm_conv_a_b', 'm_dt_bias', 'm_a_log', 'm_d_skip', 'm_ssd_norm_w', 'm_uv_b', 'm_v_ln_w', 'm_v_ln_b', 'm_w_spatial', 'm_b_spatial', 'm_w_branch', 'm_w_out', 'm_norm2_w', 'm_w_up', 'm_conv_f_w', 'm_conv_f_b', 'm_w_down', 'm_final_norm_w', 'v_norm1_w', 'v_w_in', 'v_b_gate', 'v_conv_a_w', 'v_conv_a_b', 'v_dt_bias', 'v_a_log', 'v_d_skip', 'v_ssd_norm_w', 'v_uv_b', 'v_v_ln_w', 'v_v_ln_b', 'v_w_spatial', 'v_b_spatial', 'v_w_branch', 'v_w_out', 'v_norm2_w', 'v_w_up', 'v_conv_f_w', 'v_conv_f_b', 'v_w_down', 'v_final_norm_w']
TWIN_OUTPUTS = ['loss', 'grad_x', 'grad_norm1_w', 'grad_w_in', 'grad_b_gate', 'grad_conv_a_w', 'grad_conv_a_b', 'grad_dt_bias', 'grad_a_log', 'grad_d_skip', 'grad_ssd_norm_w', 'grad_uv_b', 'grad_v_ln_w', 'grad_v_ln_b', 'grad_w_spatial', 'grad_b_spatial', 'grad_w_branch', 'grad_w_out', 'grad_norm2_w', 'grad_w_up', 'grad_conv_f_w', 'grad_conv_f_b', 'grad_w_down', 'grad_final_norm_w', 'delta_norm1_w', 'delta_w_in', 'delta_b_gate', 'delta_conv_a_w', 'delta_conv_a_b', 'delta_dt_bias', 'delta_a_log', 'delta_d_skip', 'delta_ssd_norm_w', 'delta_uv_b', 'delta_v_ln_w', 'delta_v_ln_b', 'delta_w_spatial', 'delta_b_spatial', 'delta_w_branch', 'delta_w_out', 'delta_norm2_w', 'delta_w_up', 'delta_conv_f_w', 'delta_conv_f_b', 'delta_w_down', 'delta_final_norm_w', 'new_m_norm1_w', 'new_m_w_in', 'new_m_b_gate', 'new_m_conv_a_w', 'new_m_conv_a_b', 'new_m_dt_bias', 'new_m_a_log', 'new_m_d_skip', 'new_m_ssd_norm_w', 'new_m_uv_b', 'new_m_v_ln_w', 'new_m_v_ln_b', 'new_m_w_spatial', 'new_m_b_spatial', 'new_m_w_branch', 'new_m_w_out', 'new_m_norm2_w', 'new_m_w_up', 'new_m_conv_f_w', 'new_m_conv_f_b', 'new_m_w_down', 'new_m_final_norm_w', 'new_v_norm1_w', 'new_v_w_in', 'new_v_b_gate', 'new_v_conv_a_w', 'new_v_conv_a_b', 'new_v_dt_bias', 'new_v_a_log', 'new_v_d_skip', 'new_v_ssd_norm_w', 'new_v_uv_b', 'new_v_v_ln_w', 'new_v_v_ln_b', 'new_v_w_spatial', 'new_v_b_spatial', 'new_v_w_branch', 'new_v_w_out', 'new_v_norm2_w', 'new_v_w_up', 'new_v_conv_f_w', 'new_v_conv_f_b', 'new_v_w_down', 'new_v_final_norm_w']
TWIN_LEAF_KINDS = {'loss': 'loss', 'grad_x': 'grad_x', 'grad_norm1_w': 'grad_w', 'grad_w_in': 'grad_w', 'grad_b_gate': 'grad_w', 'grad_conv_a_w': 'grad_w', 'grad_conv_a_b': 'grad_w', 'grad_dt_bias': 'grad_w', 'grad_a_log': 'grad_w', 'grad_d_skip': 'grad_w', 'grad_ssd_norm_w': 'grad_w', 'grad_uv_b': 'grad_w', 'grad_v_ln_w': 'grad_w', 'grad_v_ln_b': 'grad_w', 'grad_w_spatial': 'grad_w', 'grad_b_spatial': 'grad_w', 'grad_w_branch': 'grad_w', 'grad_w_out': 'grad_w', 'grad_norm2_w': 'grad_w', 'grad_w_up': 'grad_w', 'grad_conv_f_w': 'grad_w', 'grad_conv_f_b': 'grad_w', 'grad_w_down': 'grad_w', 'grad_final_norm_w': 'grad_w', 'delta_norm1_w': 'delta_w', 'delta_w_in': 'delta_w', 'delta_b_gate': 'delta_w', 'delta_conv_a_w': 'delta_w', 'delta_conv_a_b': 'delta_w', 'delta_dt_bias': 'delta_w', 'delta_a_log': 'delta_w', 'delta_d_skip': 'delta_w', 'delta_ssd_norm_w': 'delta_w', 'delta_uv_b': 'delta_w', 'delta_v_ln_w': 'delta_w', 'delta_v_ln_b': 'delta_w', 'delta_w_spatial': 'delta_w', 'delta_b_spatial': 'delta_w', 'delta_w_branch': 'delta_w', 'delta_w_out': 'delta_w', 'delta_norm2_w': 'delta_w', 'delta_w_up': 'delta_w', 'delta_conv_f_w': 'delta_w', 'delta_conv_f_b': 'delta_w', 'delta_w_down': 'delta_w', 'delta_final_norm_w': 'delta_w', 'new_m_norm1_w': 'new_m', 'new_m_w_in': 'new_m', 'new_m_b_gate': 'new_m', 'new_m_conv_a_w': 'new_m', 'new_m_conv_a_b': 'new_m', 'new_m_dt_bias': 'new_m', 'new_m_a_log': 'new_m', 'new_m_d_skip': 'new_m', 'new_m_ssd_norm_w': 'new_m', 'new_m_uv_b': 'new_m', 'new_m_v_ln_w': 'new_m', 'new_m_v_ln_b': 'new_m', 'new_m_w_spatial': 'new_m', 'new_m_b_spatial': 'new_m', 'new_m_w_branch': 'new_m', 'new_m_w_out': 'new_m', 'new_m_norm2_w': 'new_m', 'new_m_w_up': 'new_m', 'new_m_conv_f_w': 'new_m', 'new_m_conv_f_b': 'new_m', 'new_m_w_down': 'new_m', 'new_m_final_norm_w': 'new_m', 'new_v_norm1_w': 'new_v', 'new_v_w_in': 'new_v', 'new_v_b_gate': 'new_v', 'new_v_conv_a_w': 'new_v', 'new_v_conv_a_b': 'new_v', 'new_v_dt_bias': 'new_v', 'new_v_a_log': 'new_v', 'new_v_d_skip': 'new_v', 'new_v_ssd_norm_w': 'new_v', 'new_v_uv_b': 'new_v', 'new_v_v_ln_w': 'new_v', 'new_v_v_ln_b': 'new_v', 'new_v_w_spatial': 'new_v', 'new_v_b_spatial': 'new_v', 'new_v_w_branch': 'new_v', 'new_v_w_out': 'new_v', 'new_v_norm2_w': 'new_v', 'new_v_w_up': 'new_v', 'new_v_conv_f_w': 'new_v', 'new_v_conv_f_b': 'new_v', 'new_v_w_down': 'new_v', 'new_v_final_norm_w': 'new_v'}


def _forward(args):
    return _fwd_reference(*[args[k] for k in FWD_PARAMS])


def _output_shape():
    def fwd():
        inp = _fwd_setup_inputs(0)
        return _fwd_reference(*[inp[k] for k in FWD_PARAMS])
    out = _jax.eval_shape(fwd)
    return out.shape, out.dtype

N_MICROBATCH = 1
ADAM_LR = 0.001
ADAM_B1 = 0.9
ADAM_B2 = 0.999
ADAM_EPS = 1e-08
ADAM_WD = 0.01
ADAM_STEP = 10
PER_EXAMPLE_BATCH_AXIS = {'x': 0, 'loss_target': 0}
SHARED_INPUTS = []
_WEIGHT_DTYPES = {'norm1_w': _jnp.float32, 'w_in': _jnp.float32, 'b_gate': _jnp.float32, 'conv_a_w': _jnp.float32, 'conv_a_b': _jnp.float32, 'dt_bias': _jnp.float32, 'a_log': _jnp.float32, 'd_skip': _jnp.float32, 'ssd_norm_w': _jnp.float32, 'uv_b': _jnp.float32, 'v_ln_w': _jnp.float32, 'v_ln_b': _jnp.float32, 'w_spatial': _jnp.float32, 'b_spatial': _jnp.float32, 'w_branch': _jnp.float32, 'w_out': _jnp.float32, 'norm2_w': _jnp.float32, 'w_up': _jnp.float32, 'conv_f_w': _jnp.float32, 'conv_f_b': _jnp.float32, 'w_down': _jnp.float32, 'final_norm_w': _jnp.float32}
MOMENT_SCALE = {'norm1_w': 1.672057e-01, 'w_in': 5.432555e-02, 'b_gate': 3.123164e-02, 'conv_a_w': 5.312897e-02, 'conv_a_b': 7.574796e-02, 'dt_bias': 3.271506e-01, 'a_log': 2.290639e-01, 'd_skip': 3.833597e-01, 'ssd_norm_w': 6.519455e-02, 'uv_b': 6.161286e-02, 'v_ln_w': 4.113394e-02, 'v_ln_b': 3.943693e-02, 'w_spatial': 3.969078e-02, 'b_spatial': 5.590284e-02, 'w_branch': 8.203278e-02, 'w_out': 1.128322e-01, 'norm2_w': 1.136933e-01, 'w_up': 4.845221e-02, 'conv_f_w': 4.977002e-02, 'conv_f_b': 4.847134e-02, 'w_down': 7.915762e-02, 'final_norm_w': 3.201366e+01}


def _to_microbatches(a, axis):
    t = _jnp.moveaxis(a, axis, 0)
    t = t.reshape((N_MICROBATCH, t.shape[0] // N_MICROBATCH) + t.shape[1:])
    return _jnp.moveaxis(t, 1, axis + 1)


def setup_inputs(seed: int = 0) -> dict:
    inp = _fwd_setup_inputs(seed)
    key = _jax.random.fold_in(_jax.random.key(seed), 7919)
    shape, _ = _output_shape()
    out = dict(inp)
    out["loss_target"] = _jax.random.normal(_jax.random.fold_in(key, 0), shape, _jnp.float32)
    for i, name in enumerate(TWIN_WEIGHTS):
        w = inp[name].astype(_jnp.float32)
        if MOMENT_SCALE is None:
            s = _jnp.sqrt(_jnp.mean(_jnp.square(w)) + 1e-30)
        else:
            s = MOMENT_SCALE[name]
        km, kv = _jax.random.split(_jax.random.fold_in(key, i + 1))
        out[name] = w
        out["m_" + name] = s * _jax.random.normal(km, w.shape, _jnp.float32)
        out["v_" + name] = (s * s) * _jax.random.uniform(kv, w.shape, _jnp.float32, 0.5, 1.5)
    if N_MICROBATCH > 1:
        for name, axis in PER_EXAMPLE_BATCH_AXIS.items():
            out[name] = _to_microbatches(out[name], axis)
    return {'x': out['x'], 'norm1_w': out['norm1_w'], 'w_in': out['w_in'], 'b_gate': out['b_gate'], 'conv_a_w': out['conv_a_w'], 'conv_a_b': out['conv_a_b'], 'dt_bias': out['dt_bias'], 'a_log': out['a_log'], 'd_skip': out['d_skip'], 'ssd_norm_w': out['ssd_norm_w'], 'uv_b': out['uv_b'], 'v_ln_w': out['v_ln_w'], 'v_ln_b': out['v_ln_b'], 'w_spatial': out['w_spatial'], 'b_spatial': out['b_spatial'], 'w_branch': out['w_branch'], 'w_out': out['w_out'], 'norm2_w': out['norm2_w'], 'w_up': out['w_up'], 'conv_f_w': out['conv_f_w'], 'conv_f_b': out['conv_f_b'], 'w_down': out['w_down'], 'final_norm_w': out['final_norm_w'], 'loss_target': out['loss_target'], 'm_norm1_w': out['m_norm1_w'], 'm_w_in': out['m_w_in'], 'm_b_gate': out['m_b_gate'], 'm_conv_a_w': out['m_conv_a_w'], 'm_conv_a_b': out['m_conv_a_b'], 'm_dt_bias': out['m_dt_bias'], 'm_a_log': out['m_a_log'], 'm_d_skip': out['m_d_skip'], 'm_ssd_norm_w': out['m_ssd_norm_w'], 'm_uv_b': out['m_uv_b'], 'm_v_ln_w': out['m_v_ln_w'], 'm_v_ln_b': out['m_v_ln_b'], 'm_w_spatial': out['m_w_spatial'], 'm_b_spatial': out['m_b_spatial'], 'm_w_branch': out['m_w_branch'], 'm_w_out': out['m_w_out'], 'm_norm2_w': out['m_norm2_w'], 'm_w_up': out['m_w_up'], 'm_conv_f_w': out['m_conv_f_w'], 'm_conv_f_b': out['m_conv_f_b'], 'm_w_down': out['m_w_down'], 'm_final_norm_w': out['m_final_norm_w'], 'v_norm1_w': out['v_norm1_w'], 'v_w_in': out['v_w_in'], 'v_b_gate': out['v_b_gate'], 'v_conv_a_w': out['v_conv_a_w'], 'v_conv_a_b': out['v_conv_a_b'], 'v_dt_bias': out['v_dt_bias'], 'v_a_log': out['v_a_log'], 'v_d_skip': out['v_d_skip'], 'v_ssd_norm_w': out['v_ssd_norm_w'], 'v_uv_b': out['v_uv_b'], 'v_v_ln_w': out['v_v_ln_w'], 'v_v_ln_b': out['v_v_ln_b'], 'v_w_spatial': out['v_w_spatial'], 'v_b_spatial': out['v_b_spatial'], 'v_w_branch': out['v_w_branch'], 'v_w_out': out['v_w_out'], 'v_norm2_w': out['v_norm2_w'], 'v_w_up': out['v_w_up'], 'v_conv_f_w': out['v_conv_f_w'], 'v_conv_f_b': out['v_conv_f_b'], 'v_w_down': out['v_w_down'], 'v_final_norm_w': out['v_final_norm_w']}


def _loss(weights, diff, rest, loss_target):
    with _jax.named_scope("forward"):
        args = {**rest, TWIN_DIFF_INPUT: diff, **{k: w.astype(_WEIGHT_DTYPES[k]) for k, w in weights.items()}}
        y = _forward(args)
    with _jax.named_scope("loss_head"):
        err = _jnp.square(y.astype(_jnp.float32) - loss_target)
        return 0.5 * _jnp.sum(_jnp.mean(err, axis=-1)) if err.ndim else 0.5 * err


def _adamw(w, g, m, v):
    m = ADAM_B1 * m + (1.0 - ADAM_B1) * g
    v = ADAM_B2 * v + (1.0 - ADAM_B2) * _jnp.square(g)
    m_hat = m / (1.0 - ADAM_B1 ** ADAM_STEP)
    v_hat = v / (1.0 - ADAM_B2 ** ADAM_STEP)
    delta = -ADAM_LR * (m_hat / (_jnp.sqrt(v_hat) + ADAM_EPS) + ADAM_WD * w)
    return delta, m, v


def reference(x, norm1_w, w_in, b_gate, conv_a_w, conv_a_b, dt_bias, a_log, d_skip, ssd_norm_w, uv_b, v_ln_w, v_ln_b, w_spatial, b_spatial, w_branch, w_out, norm2_w, w_up, conv_f_w, conv_f_b, w_down, final_norm_w, loss_target, m_norm1_w, m_w_in, m_b_gate, m_conv_a_w, m_conv_a_b, m_dt_bias, m_a_log, m_d_skip, m_ssd_norm_w, m_uv_b, m_v_ln_w, m_v_ln_b, m_w_spatial, m_b_spatial, m_w_branch, m_w_out, m_norm2_w, m_w_up, m_conv_f_w, m_conv_f_b, m_w_down, m_final_norm_w, v_norm1_w, v_w_in, v_b_gate, v_conv_a_w, v_conv_a_b, v_dt_bias, v_a_log, v_d_skip, v_ssd_norm_w, v_uv_b, v_v_ln_w, v_v_ln_b, v_w_spatial, v_b_spatial, v_w_branch, v_w_out, v_norm2_w, v_w_up, v_conv_f_w, v_conv_f_b, v_w_down, v_final_norm_w):
    given = dict(x=x, norm1_w=norm1_w, w_in=w_in, b_gate=b_gate, conv_a_w=conv_a_w, conv_a_b=conv_a_b, dt_bias=dt_bias, a_log=a_log, d_skip=d_skip, ssd_norm_w=ssd_norm_w, uv_b=uv_b, v_ln_w=v_ln_w, v_ln_b=v_ln_b, w_spatial=w_spatial, b_spatial=b_spatial, w_branch=w_branch, w_out=w_out, norm2_w=norm2_w, w_up=w_up, conv_f_w=conv_f_w, conv_f_b=conv_f_b, w_down=w_down, final_norm_w=final_norm_w, loss_target=loss_target, m_norm1_w=m_norm1_w, m_w_in=m_w_in, m_b_gate=m_b_gate, m_conv_a_w=m_conv_a_w, m_conv_a_b=m_conv_a_b, m_dt_bias=m_dt_bias, m_a_log=m_a_log, m_d_skip=m_d_skip, m_ssd_norm_w=m_ssd_norm_w, m_uv_b=m_uv_b, m_v_ln_w=m_v_ln_w, m_v_ln_b=m_v_ln_b, m_w_spatial=m_w_spatial, m_b_spatial=m_b_spatial, m_w_branch=m_w_branch, m_w_out=m_w_out, m_norm2_w=m_norm2_w, m_w_up=m_w_up, m_conv_f_w=m_conv_f_w, m_conv_f_b=m_conv_f_b, m_w_down=m_w_down, m_final_norm_w=m_final_norm_w, v_norm1_w=v_norm1_w, v_w_in=v_w_in, v_b_gate=v_b_gate, v_conv_a_w=v_conv_a_w, v_conv_a_b=v_conv_a_b, v_dt_bias=v_dt_bias, v_a_log=v_a_log, v_d_skip=v_d_skip, v_ssd_norm_w=v_ssd_norm_w, v_uv_b=v_uv_b, v_v_ln_w=v_v_ln_w, v_v_ln_b=v_v_ln_b, v_w_spatial=v_w_spatial, v_b_spatial=v_b_spatial, v_w_branch=v_w_branch, v_w_out=v_w_out, v_norm2_w=v_norm2_w, v_w_up=v_w_up, v_conv_f_w=v_conv_f_w, v_conv_f_b=v_conv_f_b, v_w_down=v_w_down, v_final_norm_w=v_final_norm_w)
    weights = {n: given[n] for n in TWIN_WEIGHTS}
    shared = {n: given[n] for n in SHARED_INPUTS}
    per_example = {n: given[n] for n in ['x']}
    grad_fn = _jax.value_and_grad(_loss, argnums=(0, 1))

    def one_microbatch(ex, loss_target):
        ex = dict(ex)
        diff = ex.pop(TWIN_DIFF_INPUT)
        return grad_fn(weights, diff, {**shared, **ex}, loss_target)

    if N_MICROBATCH == 1:
        loss, (grad_w, grad_x) = one_microbatch(per_example, given["loss_target"])
    else:
        def body(carry, xs):
            loss_sum, grad_sum = carry
            l_k, (gw_k, gx_k) = one_microbatch(xs[0], xs[1])
            with _jax.named_scope("update"):
                return (loss_sum + l_k, _jax.tree.map(_jnp.add, grad_sum, gw_k)), gx_k

        init = (_jnp.zeros((), _jnp.float32), _jax.tree.map(_jnp.zeros_like, weights))
        (loss, grad_w), grad_x = _jax.lax.scan(body, init, (per_example, given["loss_target"]))
    with _jax.named_scope("update"):
        delta_w, new_m, new_v = {}, {}, {}
        for n in TWIN_WEIGHTS:
            delta_w[n], new_m[n], new_v[n] = _adamw(weights[n], grad_w[n], given["m_" + n], given["v_" + n])
    return (loss, grad_x, *[grad_w[n] for n in TWIN_WEIGHTS], *[delta_w[n] for n in TWIN_WEIGHTS],
            *[new_m[n] for n in TWIN_WEIGHTS], *[new_v[n] for n in TWIN_WEIGHTS])
```

```python
import functools

import jax
import jax.numpy as jnp
from jax import lax
from jax.experimental import pallas as pl
from jax.experimental.pallas import tpu as pltpu

F32, BF16 = jnp.float32, jnp.bfloat16
HIGHEST = lax.Precision.HIGHEST

D_MODEL = 1024
SSD_INNER = 2048
SSD_HEAD_DIM = 64
SSD_HEADS = 32
SSD_GROUPS = 4
SSD_STATE = 128
SSD_BC = SSD_GROUPS * SSD_STATE
SSD_XBC = SSD_INNER + 2 * SSD_BC
SSD_CONV = 4
CHUNK = 128
N_PAIRS = SSD_HEADS // 2
PAIRS_PER_GROUP = N_PAIRS // SSD_GROUPS
SGU_WIDTH = 1024
SGU_GROUPS = 8
D_FF = 2816
FFN_CONV = 3
NORM_EPS = 1e-6
LN_EPS = 1e-5
LANES = 128
DT_PAD = LANES

ADAM_LR, ADAM_B1, ADAM_B2, ADAM_EPS, ADAM_WD, ADAM_STEP = 0.001, 0.9, 0.999, 1e-08, 0.01, 10

N_DEV = 8
VMEM_LIMIT = 56 * 1024 * 1024
MESH = pl.DeviceIdType.MESH


def _params(n_grid, **kw):
    sem = dict(dimension_semantics=("arbitrary",) * n_grid) if n_grid else {}
    return pltpu.CompilerParams(vmem_limit_bytes=VMEM_LIMIT, **sem, **kw)


def _tile(n, pref):
    t = (min(pref, n) // LANES) * LANES
    while n % t:
        t -= LANES
    return t


def _row_tile(r, pref):
    for t in range(min(pref, r) // 16 * 16, 0, -16):
        if r % t == 0:
            return t
    return r


def _rows(tm, n, nt=None, rev=False, col=0):
    if rev:
        return pl.BlockSpec((tm, n), lambda i: (nt - 1 - i, col))
    return pl.BlockSpec((tm, n), lambda i: (i, col))


def _halo(tm, n, nt=None, rev=False):
    per = tm // 8
    if rev:
        return pl.BlockSpec((8, n), lambda i: (jnp.maximum((nt - 1 - i) * per - 1, 0), 0))
    return pl.BlockSpec((8, n), lambda i: (jnp.maximum(i * per - 1, 0), 0))


def _full(shape):
    nd = len(shape)
    return pl.BlockSpec(shape, lambda *_: (0,) * nd)


def _rms(x, w, eps=NORM_EPS):
    return x * lax.rsqrt(jnp.mean(x * x, axis=-1, keepdims=True) + eps) * w


def _layer_norm(x, w, b):
    mu = jnp.mean(x, axis=-1, keepdims=True)
    var = jnp.mean(jnp.square(x - mu), axis=-1, keepdims=True)
    return (x - mu) * lax.rsqrt(var + LN_EPS) * w + b


def _sigmoid(x):
    return 1.0 / (1.0 + jnp.exp(-x))


def _silu(x):
    return x * _sigmoid(x)


def _dsilu(x):
    s = _sigmoid(x)
    return s * (1.0 + x * (1.0 - s))


def _softplus(x):
    return jnp.maximum(x, 0.0) + jnp.log(1.0 + jnp.exp(-jnp.abs(x)))


def _gelu(x):
    return jax.nn.gelu(x)


def _dot(a, b):
    return jnp.dot(a, b, preferred_element_type=F32)


def _dot_nt(a, b):
    return lax.dot_general(a, b, (((1,), (1,)), ((), ())), preferred_element_type=F32)


def _dot_tn(a, b):
    return lax.dot_general(a, b, (((0,), (0,)), ((), ())), preferred_element_type=F32)


def _dot_split(p, e):
    hi = p.astype(BF16)
    lo = (p - hi.astype(F32)).astype(BF16)
    return _dot(hi, e) + _dot(lo, e)


def _colsum(x):
    return jnp.sum(x, axis=0, keepdims=True)


def _shift_down(x, halo, j):
    xs = pltpu.roll(x, j, 0)
    hs = pltpu.roll(halo, j, 0)
    r8 = lax.broadcasted_iota(jnp.int32, hs.shape, 0)
    return jnp.concatenate([jnp.where(r8 < j, hs, xs[:8]), xs[8:]], axis=0)


def _shift_up(x, nxt, j):
    n = x.shape[0]
    xs = pltpu.roll(x, n - j, 0)
    ns = pltpu.roll(nxt, 8 - j, 0)
    r8 = lax.broadcasted_iota(jnp.int32, ns.shape, 0)
    return jnp.concatenate([xs[:n - 8], jnp.where(r8 >= 8 - j, ns, xs[n - 8:])], axis=0)


def _causal_conv(x, halo, w, b):
    k = w.shape[0]
    shifted = [x] + [_shift_down(x, halo, j) for j in range(1, k)]
    y = b + w[k - 1:k, :] * x
    for j in range(1, k):
        y = y + w[k - 1 - j:k - j, :] * shifted[j]
    return y, shifted


def _anticausal_conv(dy, nxt, w):
    k = w.shape[0]
    dx = w[k - 1:k, :] * dy
    for j in range(1, k):
        dx = dx + w[k - 1 - j:k - j, :] * _shift_up(dy, nxt, j)
    return dx


def _conv_wgrad(dy, shifted):
    k = len(shifted)
    return jnp.concatenate([_colsum(dy * shifted[k - 1 - i]) for i in range(k)], axis=0)


def _mm(a, b, dims, name, acc=None, out_dtype=F32, tm=512, tn=512):
    if dims == "tn":
        k, m = a.shape
    else:
        m, k = a.shape
    n = b.shape[0] if dims == "nt" else b.shape[1]
    tm, tn = _tile(m, tm), _tile(n, tn)
    a_spec = pl.BlockSpec((k, tm), lambda j, i: (0, i)) if dims == "tn" else pl.BlockSpec((tm, k), lambda j, i: (i, 0))
    b_spec = pl.BlockSpec((tn, k), lambda j, i: (j, 0)) if dims == "nt" else pl.BlockSpec((k, tn), lambda j, i: (0, j))
    o_spec = pl.BlockSpec((tm, tn), lambda j, i: (i, j))
    dot = {"nn": _dot, "nt": _dot_nt, "tn": _dot_tn}[dims]

    def body(a_ref, b_ref, *rest):
        r = dot(a_ref[...], b_ref[...])
        if acc is not None:
            r = r + rest[0][...]
        rest[-1][...] = r.astype(out_dtype)

    ins, specs = [a, b], [a_spec, b_spec]
    if acc is not None:
        ins.append(acc)
        specs.append(o_spec)
    return pl.pallas_call(
        body, name=name, grid=(n // tn, m // tm), in_specs=specs, out_specs=o_spec,
        out_shape=jax.ShapeDtypeStruct((m, n), out_dtype), compiler_params=_params(2),
    )(*ins)


def _norm_fwd(x, w, name, tm=512):
    t, d = x.shape

    def body(x_ref, w_ref, o_ref):
        o_ref[...] = _rms(x_ref[...], w_ref[...]).astype(BF16)

    return pl.pallas_call(
        body, name=name, grid=(t // tm,), in_specs=[_rows(tm, d), _full((1, d))], out_specs=_rows(tm, d),
        out_shape=jax.ShapeDtypeStruct((t, d), BF16), compiler_params=_params(1),
    )(x, w)


def _conv_a_fwd(xbc, cw, cb, tm=256):
    t, c = xbc.shape

    def body(x_ref, h_ref, w_ref, b_ref, o_ref):
        halo = jnp.where(pl.program_id(0) > 0, h_ref[...], 0.0)
        y, _ = _causal_conv(x_ref[...], halo, w_ref[...], b_ref[...])
        o_ref[...] = _silu(y)

    return pl.pallas_call(
        body, name="conv_a_fwd", grid=(t // tm,),
        in_specs=[_rows(tm, c), _halo(tm, c), _full(cw.shape), _full((1, c))], out_specs=_rows(tm, c),
        out_shape=jax.ShapeDtypeStruct((t, c), F32), compiler_params=_params(1),
    )(xbc, xbc, cw, cb)


def _ssd_common(dtr, dtb, alog):
    row = lax.broadcasted_iota(jnp.int32, (CHUNK, CHUNK), 0)
    col = lax.broadcasted_iota(jnp.int32, (CHUNK, CHUNK), 1)
    causal = row >= col
    dt = _softplus(dtr + dtb)
    a = -jnp.exp(alog)
    acum = jnp.dot(causal.astype(F32), dt * a, precision=HIGHEST, preferred_element_type=F32)
    return dt, a, acum, acum.T, causal, col < SSD_HEAD_DIM, row


def _pair_terms(j, dt, acum, acum_t, causal, lane_lo):
    h0, h1 = 2 * j, 2 * j + 1
    ac0, ac1 = acum[:, h0:h0 + 1], acum[:, h1:h1 + 1]
    l0 = jnp.exp(jnp.where(causal, ac0 - acum_t[h0:h0 + 1, :], -jnp.inf))
    l1 = jnp.exp(jnp.where(causal, ac1 - acum_t[h1:h1 + 1, :], -jnp.inf))
    dtp = jnp.where(lane_lo, dt[:, h0:h0 + 1], dt[:, h1:h1 + 1])
    al0, al1 = acum[CHUNK - 1:CHUNK, h0:h0 + 1], acum[CHUNK - 1:CHUNK, h1:h1 + 1]
    ecol = jnp.where(lane_lo, jnp.exp(ac0), jnp.exp(ac1))
    dsr = jnp.where(lane_lo, jnp.exp(al0 - ac0), jnp.exp(al1 - ac1))
    elast = jnp.where(lane_lo[0:1], jnp.exp(al0), jnp.exp(al1))
    return l0, l1, dtp, ecol, dsr, elast


def _ssd_fwd(xc, dtr, z, dtb, alog, dsk, nw):
    t = xc.shape[0]
    nc = t // CHUNK

    def body(xs_ref, b_ref, c_ref, dtr_ref, z_ref, dtb_ref, alog_ref, dsk_ref, nw_ref, y_ref, ya_ref, sp_ref, s_scr):
        @pl.when(pl.program_id(0) == 0)
        def _():
            s_scr[...] = jnp.zeros_like(s_scr)

        dt, a, acum, acum_t, causal, lane_lo, _ = _ssd_common(dtr_ref[...], dtb_ref[...], alog_ref[...])
        dsk = dsk_ref[...]
        for g in range(SSD_GROUPS):
            gs = slice(g * SSD_STATE, (g + 1) * SSD_STATE)
            bg, cg = b_ref[:, gs].astype(BF16), c_ref[:, gs].astype(BF16)
            cb = _dot_nt(cg, bg)
            for pp in range(PAIRS_PER_GROUP):
                j = g * PAIRS_PER_GROUP + pp
                ps = slice(j * LANES, (j + 1) * LANES)
                x = xs_ref[:, ps]
                l0, l1, dtp, ecol, dsr, elast = _pair_terms(j, dt, acum, acum_t, causal, lane_lo)
                xdt = x * dtp
                xb = xdt.astype(BF16)
                zero = jnp.zeros_like(xb)
                yd = (_dot((cb * l0).astype(BF16), jnp.where(lane_lo, xb, zero))
                      + _dot((cb * l1).astype(BF16), jnp.where(lane_lo, zero, xb)))
                sp = s_scr[j]
                yo = ecol * _dot(cg, sp.astype(BF16))
                st = _dot_tn(bg, (xdt * dsr).astype(BF16))
                sp_ref[0, j] = sp
                s_scr[j] = elast * sp + st
                dskp = jnp.where(lane_lo[0:1], dsk[:, 2 * j:2 * j + 1], dsk[:, 2 * j + 1:2 * j + 2])
                y_ref[:, ps] = yd + yo + dskp * x
        ya_ref[...] = _rms(y_ref[...] * _silu(z_ref[...]), nw_ref[...]).astype(BF16)

    ck = lambda n, col=0: pl.BlockSpec((CHUNK, n), lambda c: (c, col))
    return pl.pallas_call(
        body, name="ssd_fwd", grid=(nc,),
        in_specs=[ck(SSD_INNER), ck(SSD_BC, SSD_INNER // SSD_BC), ck(SSD_BC, SSD_INNER // SSD_BC + 1), ck(DT_PAD),
                  ck(SSD_INNER), _full((1, DT_PAD)), _full((1, DT_PAD)), _full((1, DT_PAD)), _full((1, SSD_INNER))],
        out_specs=[ck(SSD_INNER), ck(SSD_INNER),
                   pl.BlockSpec((1, N_PAIRS, SSD_STATE, LANES), lambda c: (c, 0, 0, 0))],
        out_shape=[jax.ShapeDtypeStruct((t, SSD_INNER), F32), jax.ShapeDtypeStruct((t, SSD_INNER), BF16),
                   jax.ShapeDtypeStruct((nc, N_PAIRS, SSD_STATE, LANES), F32)],
        scratch_shapes=[pltpu.VMEM((N_PAIRS, SSD_STATE, LANES), F32)], compiler_params=_params(1),
    )(xc, xc, xc, dtr, z, dtb, alog, dsk, nw)


def _ssd_bwd(dya, y, z, xc, dtr, sprev, dtb, alog, dsk, nw, e_heads):
    t = xc.shape[0]
    nc = t // CHUNK

    def body(dya_ref, y_ref, z_ref, xs_ref, b_ref, c_ref, dtr_ref, sp_ref, dtb_ref, alog_ref, dsk_ref, nw_ref, e_ref,
             dz_ref, dxs_ref, db_ref, dc_ref, ddtr_ref, dnw_ref, ddtb_ref, dalog_ref, ddsk_ref, ds_scr):
        @pl.when(pl.program_id(0) == 0)
        def _():
            ds_scr[...] = jnp.zeros_like(ds_scr)
            for r in (dnw_ref, ddtb_ref, dalog_ref, ddsk_ref):
                r[...] = jnp.zeros_like(r)

        y = y_ref[...]
        _, gate_vjp = jax.vjp(lambda y_, z_, w_: _rms(y_ * _silu(z_), w_), y, z_ref[...], nw_ref[...])
        dy, dz, dnw = gate_vjp(dya_ref[...])
        dz_ref[...] = dz.astype(BF16)
        dnw_ref[...] += dnw

        dtr = dtr_ref[...]
        dt, a, acum, acum_t, causal, lane_lo, row = _ssd_common(dtr, dtb_ref[...], alog_ref[...])
        dsk = dsk_ref[...]
        p_a, p_dt, v_last = [], [], []
        col = lax.broadcasted_iota(jnp.int32, (CHUNK, CHUNK), 1)
        da_cols = jnp.zeros((CHUNK, CHUNK), F32)
        da_rows = jnp.zeros((CHUNK, CHUNK), F32)
        for g in range(SSD_GROUPS):
            gs = slice(g * SSD_STATE, (g + 1) * SSD_STATE)
            bg, cg = b_ref[:, gs].astype(BF16), c_ref[:, gs].astype(BF16)
            cb = _dot_nt(cg, bg)
            dcb = jnp.zeros((CHUNK, CHUNK), F32)
            dbg = jnp.zeros((CHUNK, SSD_STATE), F32)
            dcg = jnp.zeros((CHUNK, SSD_STATE), F32)
            for pp in range(PAIRS_PER_GROUP):
                j = g * PAIRS_PER_GROUP + pp
                ps = slice(j * LANES, (j + 1) * LANES)
                x = xs_ref[:, ps]
                l0, l1, dtp, ecol, dsr, elast = _pair_terms(j, dt, acum, acum_t, causal, lane_lo)
                xdt = x * dtp
                xb = xdt.astype(BF16)
                dskp = jnp.where(lane_lo[0:1], dsk[:, 2 * j:2 * j + 1], dsk[:, 2 * j + 1:2 * j + 2])
                dyp = dy[:, ps]
                dyb = dyp.astype(BF16)
                sp, dsn = sp_ref[0, j], ds_scr[j]
                spb, dsnb = sp.astype(BF16), dsn.astype(BF16)
                y_off = ecol * _dot(cg, spb)
                dw = (dyp * ecol).astype(BF16)
                dcg = dcg + _dot_nt(dw, spb)
                dsp = _dot_tn(cg, dw) + elast * dsn
                xd = xdt * dsr
                zd = _dot(bg, dsnb) * dsr
                dbg = dbg + _dot_nt(xd.astype(BF16), dsnb)
                dxdt = zd
                zero = jnp.zeros_like(xb)
                for h, lm, le in ((2 * j, lane_lo, l0), (2 * j + 1, jnp.logical_not(lane_lo), l1)):
                    dm = _dot_nt(jnp.where(lm, dyb, zero), jnp.where(lm, xb, zero))
                    dcb = dcb + dm * le
                    m = cb * le
                    dxdt = dxdt + jnp.where(lm, _dot_tn(m.astype(BF16), dyb), 0.0)
                    q = dm * m
                    da_cols = da_cols + jnp.where(col == h, jnp.sum(q, axis=1, keepdims=True), 0.0)
                    da_rows = da_rows + jnp.where(row == h, _colsum(q), 0.0)
                ds_scr[j] = dsp
                dxs_ref[:, ps] = dxdt * dtp + dskp * dyp
                p_a.append(dyp * y_off - xdt * zd)
                p_dt.append(dxdt * x)
                v_last.append(_colsum(zd * xdt) + elast * _colsum(dsn * sp))
            dcbb = dcb.astype(BF16)
            db_ref[:, gs] = dbg + _dot_tn(dcbb, cg)
            dc_ref[:, gs] = dcg + _dot(dcbb, bg)
        e = e_ref[...]
        rows8 = jnp.concatenate([jnp.concatenate(v_last, axis=1), _colsum(dy * xs_ref[...]),
                                 jnp.zeros((6, SSD_INNER), F32)], axis=0)
        r8 = _dot_split(rows8, e)
        da = (_dot_split(jnp.concatenate(p_a, axis=1), e) + jnp.where(row == CHUNK - 1, r8[0:1], 0.0)
              + da_cols - da_rows.T)
        ddsk_ref[...] += r8[1:2]
        dadt = jnp.dot((row <= col).astype(F32), da, precision=HIGHEST, preferred_element_type=F32)
        ddt = dadt * a + _dot_split(jnp.concatenate(p_dt, axis=1), e)
        dalog_ref[...] += _colsum(dadt * dt) * a
        ddtr = ddt * _sigmoid(dtr + dtb_ref[...])
        ddtr_ref[...] = ddtr
        ddtb_ref[...] += _colsum(ddtr)

    ck = lambda n, col=0: pl.BlockSpec((CHUNK, n), lambda c: (nc - 1 - c, col))
    acc = lambda n: _full((1, n))
    return pl.pallas_call(
        body, name="ssd_bwd", grid=(nc,),
        in_specs=[ck(SSD_INNER), ck(SSD_INNER), ck(SSD_INNER), ck(SSD_INNER), ck(SSD_BC, SSD_INNER // SSD_BC),
                  ck(SSD_BC, SSD_INNER // SSD_BC + 1), ck(DT_PAD),
                  pl.BlockSpec((1, N_PAIRS, SSD_STATE, LANES), lambda c: (nc - 1 - c, 0, 0, 0)),
                  acc(DT_PAD), acc(DT_PAD), acc(DT_PAD), acc(SSD_INNER), _full((SSD_INNER, LANES))],
        out_specs=[ck(SSD_INNER), ck(SSD_INNER), ck(SSD_BC), ck(SSD_BC), ck(DT_PAD),
                   acc(SSD_INNER), acc(DT_PAD), acc(DT_PAD), acc(DT_PAD)],
        out_shape=[jax.ShapeDtypeStruct((t, SSD_INNER), BF16), jax.ShapeDtypeStruct((t, SSD_INNER), F32),
                   jax.ShapeDtypeStruct((t, SSD_BC), F32), jax.ShapeDtypeStruct((t, SSD_BC), F32),
                   jax.ShapeDtypeStruct((t, DT_PAD), F32), jax.ShapeDtypeStruct((1, SSD_INNER), F32),
                   jax.ShapeDtypeStruct((1, DT_PAD), F32), jax.ShapeDtypeStruct((1, DT_PAD), F32),
                   jax.ShapeDtypeStruct((1, DT_PAD), F32)],
        scratch_shapes=[pltpu.VMEM((N_PAIRS, SSD_STATE, LANES), F32)], compiler_params=_params(1),
    )(dya, y, z, xc, xc, xc, dtr, sprev, dtb, alog, dsk, nw, e_heads)


def _sgu_act(uv, uvb, lnw, lnb):
    a = _gelu(uv + uvb)
    return a[:, :SGU_WIDTH], _layer_norm(a[:, SGU_WIDTH:], lnw, lnb)


def _sgu_weights(ws_ref):
    row = lax.broadcasted_iota(jnp.int32, (CHUNK, CHUNK), 0)
    col = lax.broadcasted_iota(jnp.int32, (CHUNK, CHUNK), 1)
    return [jnp.where(row >= col, ws_ref[g], 0.0).astype(BF16) for g in range(SGU_GROUPS)], row >= col


def _sgu_fwd(uv, uvb, lnw, lnb, ws, bs_t):
    t = uv.shape[0]

    def body(uv_ref, uvb_ref, lnw_ref, lnb_ref, ws_ref, bs_ref, o_ref):
        u, vn = _sgu_act(uv_ref[...], uvb_ref[...], lnw_ref[...], lnb_ref[...])
        wc, _ = _sgu_weights(ws_ref)
        bs = bs_ref[...]
        for g in range(SGU_GROUPS):
            gs = slice(g * LANES, (g + 1) * LANES)
            mixed = _dot(wc[g], vn[:, gs].astype(BF16)) + bs[:, g:g + 1]
            o_ref[:, gs] = (u[:, gs] * mixed).astype(BF16)

    return pl.pallas_call(
        body, name="sgu_fwd", grid=(t // CHUNK,),
        in_specs=[_rows(CHUNK, 2 * SGU_WIDTH), _full((1, 2 * SGU_WIDTH)), _full((1, SGU_WIDTH)), _full((1, SGU_WIDTH)),
                  _full(ws.shape), _full(bs_t.shape)],
        out_specs=_rows(CHUNK, SGU_WIDTH), out_shape=jax.ShapeDtypeStruct((t, SGU_WIDTH), BF16),
        compiler_params=_params(1),
    )(uv, uvb, lnw, lnb, ws, bs_t)


def _sgu_bwd(dyb, uv, uvb, lnw, lnb, ws, bs_t, e_groups):
    t = uv.shape[0]

    def body(dyb_ref, uv_ref, uvb_ref, lnw_ref, lnb_ref, ws_ref, bs_ref, e_ref,
             duv_ref, duvb_ref, dlnw_ref, dlnb_ref, dws_ref, dbs_ref):
        @pl.when(pl.program_id(0) == 0)
        def _():
            for r in (duvb_ref, dlnw_ref, dlnb_ref, dws_ref, dbs_ref):
                r[...] = jnp.zeros_like(r)

        (u, vn), act_vjp = jax.vjp(_sgu_act, uv_ref[...], uvb_ref[...], lnw_ref[...], lnb_ref[...])
        wc, causal = _sgu_weights(ws_ref)
        bs = bs_ref[...]
        dyb = dyb_ref[...]
        du, dvn, dmix = [], [], []
        for g in range(SGU_GROUPS):
            gs = slice(g * LANES, (g + 1) * LANES)
            vb = vn[:, gs].astype(BF16)
            mixed = _dot(wc[g], vb) + bs[:, g:g + 1]
            dm = dyb[:, gs] * u[:, gs]
            dmb = dm.astype(BF16)
            du.append(dyb[:, gs] * mixed)
            dvn.append(_dot_tn(wc[g], dmb))
            dws_ref[g] += jnp.where(causal, _dot_nt(dmb, vb), 0.0)
            dmix.append(dm)
        dbs_ref[...] += _dot_split(jnp.concatenate(dmix, axis=1), e_ref[...])
        duv, duvb, dlnw, dlnb = act_vjp((jnp.concatenate(du, axis=1), jnp.concatenate(dvn, axis=1)))
        duv_ref[...] = duv.astype(BF16)
        duvb_ref[...] += duvb
        dlnw_ref[...] += dlnw
        dlnb_ref[...] += dlnb

    return pl.pallas_call(
        body, name="sgu_bwd", grid=(t // CHUNK,),
        in_specs=[_rows(CHUNK, SGU_WIDTH), _rows(CHUNK, 2 * SGU_WIDTH), _full((1, 2 * SGU_WIDTH)),
                  _full((1, SGU_WIDTH)), _full((1, SGU_WIDTH)), _full(ws.shape), _full(bs_t.shape),
                  _full(e_groups.shape)],
        out_specs=[_rows(CHUNK, 2 * SGU_WIDTH), _full((1, 2 * SGU_WIDTH)), _full((1, SGU_WIDTH)),
                   _full((1, SGU_WIDTH)), _full(ws.shape), _full(bs_t.shape)],
        out_shape=[jax.ShapeDtypeStruct((t, 2 * SGU_WIDTH), BF16), jax.ShapeDtypeStruct((1, 2 * SGU_WIDTH), F32),
                   jax.ShapeDtypeStruct((1, SGU_WIDTH), F32), jax.ShapeDtypeStruct((1, SGU_WIDTH), F32),
                   jax.ShapeDtypeStruct(ws.shape, F32), jax.ShapeDtypeStruct(bs_t.shape, F32)],
        compiler_params=_params(1),
    )(dyb, uv, uvb, lnw, lnb, ws, bs_t, e_groups)


def _merge(gates, pa, pb, bg):
    s = _sigmoid(gates + bg)
    return s[:, :D_MODEL] * pa + s[:, D_MODEL:] * pb


def _merge_fwd(gates, pa, pb, bg, tm=256):
    t = gates.shape[0]

    def body(g_ref, pa_ref, pb_ref, bg_ref, o_ref):
        o_ref[...] = _merge(g_ref[...], pa_ref[...], pb_ref[...], bg_ref[...]).astype(BF16)

    return pl.pallas_call(
        body, name="merge_fwd", grid=(t // tm,),
        in_specs=[_rows(tm, 2 * D_MODEL), _rows(tm, D_MODEL), _rows(tm, D_MODEL), _full((1, 2 * D_MODEL))],
        out_specs=_rows(tm, D_MODEL), out_shape=jax.ShapeDtypeStruct((t, D_MODEL), BF16), compiler_params=_params(1),
    )(gates, pa, pb, bg)


def _merge_bwd(dmix, gates, pa, pb, bg, tm=256):
    t = gates.shape[0]

    def body(d_ref, g_ref, pa_ref, pb_ref, bg_ref, dg_ref, dpa_ref, dpb_ref, dbg_ref):
        @pl.when(pl.program_id(0) == 0)
        def _():
            dbg_ref[...] = jnp.zeros_like(dbg_ref)

        _, vjp = jax.vjp(_merge, g_ref[...], pa_ref[...], pb_ref[...], bg_ref[...])
        dg, dpa, dpb, dbg = vjp(d_ref[...])
        dg_ref[...] = dg.astype(BF16)
        dpa_ref[...] = dpa.astype(BF16)
        dpb_ref[...] = dpb.astype(BF16)
        dbg_ref[...] += dbg

    return pl.pallas_call(
        body, name="merge_bwd", grid=(t // tm,),
        in_specs=[_rows(tm, D_MODEL), _rows(tm, 2 * D_MODEL), _rows(tm, D_MODEL), _rows(tm, D_MODEL),
                  _full((1, 2 * D_MODEL))],
        out_specs=[_rows(tm, 2 * D_MODEL), _rows(tm, D_MODEL), _rows(tm, D_MODEL), _full((1, 2 * D_MODEL))],
        out_shape=[jax.ShapeDtypeStruct((t, 2 * D_MODEL), BF16), jax.ShapeDtypeStruct((t, D_MODEL), BF16),
                   jax.ShapeDtypeStruct((t, D_MODEL), BF16), jax.ShapeDtypeStruct((1, 2 * D_MODEL), F32)],
        compiler_params=_params(1),
    )(dmix, gates, pa, pb, bg)


def _residual_norm_fwd(x, o, w, tm=512):
    t, d = x.shape

    def body(x_ref, o_ref, w_ref, h_ref, n_ref):
        h = x_ref[...] + o_ref[...]
        h_ref[...] = h
        n_ref[...] = _rms(h, w_ref[...]).astype(BF16)

    return pl.pallas_call(
        body, name="residual_norm_fwd", grid=(t // tm,), in_specs=[_rows(tm, d), _rows(tm, d), _full((1, d))],
        out_specs=[_rows(tm, d), _rows(tm, d)],
        out_shape=[jax.ShapeDtypeStruct((t, d), F32), jax.ShapeDtypeStruct((t, d), BF16)], compiler_params=_params(1),
    )(x, o, w)


def _norm_bwd(dn, h, w, dres, name, tm=512):
    t, d = h.shape

    def body(dn_ref, h_ref, w_ref, dres_ref, dh_ref, dhb_ref, dw_ref):
        @pl.when(pl.program_id(0) == 0)
        def _():
            dw_ref[...] = jnp.zeros_like(dw_ref)

        _, vjp = jax.vjp(_rms, h_ref[...], w_ref[...])
        dh, dw = vjp(dn_ref[...])
        dh = dh + dres_ref[...]
        dh_ref[...] = dh
        dhb_ref[...] = dh.astype(BF16)
        dw_ref[...] += dw

    return pl.pallas_call(
        body, name=name, grid=(t // tm,), in_specs=[_rows(tm, d), _rows(tm, d), _full((1, d)), _rows(tm, d)],
        out_specs=[_rows(tm, d), _rows(tm, d), _full((1, d))],
        out_shape=[jax.ShapeDtypeStruct((t, d), F32), jax.ShapeDtypeStruct((t, d), BF16),
                   jax.ShapeDtypeStruct((1, d), F32)], compiler_params=_params(1),
    )(dn, h, w, dres)


def _conv_f_fwd(up, cw, cb, tm=128):
    t, c = up.shape

    def body(x_ref, h_ref, w_ref, b_ref, o_ref):
        halo = jnp.where(pl.program_id(0) > 0, h_ref[...], 0.0)
        y, _ = _causal_conv(x_ref[...], halo, w_ref[...], b_ref[...])
        o_ref[...] = (_silu(y[:, :D_FF]) * y[:, D_FF:]).astype(BF16)

    return pl.pallas_call(
        body, name="conv_f_fwd", grid=(t // tm,),
        in_specs=[_rows(tm, c), _halo(tm, c), _full(cw.shape), _full((1, c))], out_specs=_rows(tm, D_FF),
        out_shape=jax.ShapeDtypeStruct((t, D_FF), BF16), compiler_params=_params(1),
    )(up, up, cw, cb)


def _conv_f_bwd(dact, up, cw, cb, tm=128):
    t, c = up.shape
    nt = t // tm

    def body(d_ref, x_ref, h_ref, w_ref, b_ref, dx_ref, dw_ref, db_ref, nxt_scr):
        @pl.when(pl.program_id(0) == 0)
        def _():
            nxt_scr[...] = jnp.zeros_like(nxt_scr)
            dw_ref[...] = jnp.zeros_like(dw_ref)
            db_ref[...] = jnp.zeros_like(db_ref)

        halo = jnp.where(pl.program_id(0) < nt - 1, h_ref[...], 0.0)
        w = w_ref[...]
        y, shifted = _causal_conv(x_ref[...], halo, w, b_ref[...])
        a, v = y[:, :D_FF], y[:, D_FF:]
        d = d_ref[...]
        dy = jnp.concatenate([d * v * _dsilu(a), d * _silu(a)], axis=1)
        dx_ref[...] = _anticausal_conv(dy, nxt_scr[...], w).astype(BF16)
        nxt_scr[...] = dy[:8]
        dw_ref[...] += _conv_wgrad(dy, shifted)
        db_ref[...] += _colsum(dy)

    return pl.pallas_call(
        body, name="conv_f_bwd", grid=(nt,),
        in_specs=[_rows(tm, D_FF, nt, True), _rows(tm, c, nt, True), _halo(tm, c, nt, True), _full(cw.shape),
                  _full((1, c))],
        out_specs=[_rows(tm, c, nt, True), _full(cw.shape), _full((1, c))],
        out_shape=[jax.ShapeDtypeStruct((t, c), BF16), jax.ShapeDtypeStruct(cw.shape, F32),
                   jax.ShapeDtypeStruct((1, c), F32)],
        scratch_shapes=[pltpu.VMEM((8, c), F32)], compiler_params=_params(1),
    )(dact, up, up, cw, cb)


def _conv_a_bwd(dxs, db, dc, xbc, cw, cb, tm=256):
    t, c = xbc.shape
    nt = t // tm

    def body(dxs_ref, db_ref, dc_ref, x_ref, h_ref, w_ref, b_ref, dx_ref, dw_ref, dbias_ref, nxt_scr):
        @pl.when(pl.program_id(0) == 0)
        def _():
            nxt_scr[...] = jnp.zeros_like(nxt_scr)
            dw_ref[...] = jnp.zeros_like(dw_ref)
            dbias_ref[...] = jnp.zeros_like(dbias_ref)

        halo = jnp.where(pl.program_id(0) < nt - 1, h_ref[...], 0.0)
        w = w_ref[...]
        y, shifted = _causal_conv(x_ref[...], halo, w, b_ref[...])
        dy = jnp.concatenate([dxs_ref[...], db_ref[...], dc_ref[...]], axis=1) * _dsilu(y)
        dx_ref[...] = _anticausal_conv(dy, nxt_scr[...], w).astype(BF16)
        nxt_scr[...] = dy[:8]
        dw_ref[...] += _conv_wgrad(dy, shifted)
        dbias_ref[...] += _colsum(dy)

    return pl.pallas_call(
        body, name="conv_a_bwd", grid=(nt,),
        in_specs=[_rows(tm, SSD_INNER, nt, True), _rows(tm, SSD_BC, nt, True), _rows(tm, SSD_BC, nt, True),
                  _rows(tm, c, nt, True), _halo(tm, c, nt, True), _full(cw.shape), _full((1, c))],
        out_specs=[_rows(tm, c, nt, True), _full(cw.shape), _full((1, c))],
        out_shape=[jax.ShapeDtypeStruct((t, c), BF16), jax.ShapeDtypeStruct(cw.shape, F32),
                   jax.ShapeDtypeStruct((1, c), F32)],
        scratch_shapes=[pltpu.VMEM((8, c), F32)], compiler_params=_params(1),
    )(dxs, db, dc, xbc, xbc, cw, cb)


def _loss_head(h1, dn, w, target, tm=512):
    t, d = h1.shape

    def body(h_ref, dn_ref, w_ref, t_ref, loss_ref, dh_ref, dhb_ref, dw_ref):
        @pl.when(pl.program_id(0) == 0)
        def _():
            loss_ref[...] = jnp.zeros_like(loss_ref)
            dw_ref[...] = jnp.zeros_like(dw_ref)

        yf, vjp = jax.vjp(_rms, h_ref[...] + dn_ref[...], w_ref[...])
        err = yf - t_ref[...]
        loss_ref[...] += 0.5 * jnp.sum(jnp.mean(err * err, axis=-1, keepdims=True))
        dh, dw = vjp(err * (1.0 / d))
        dh_ref[...] = dh
        dhb_ref[...] = dh.astype(BF16)
        dw_ref[...] += dw

    return pl.pallas_call(
        body, name="loss_head", grid=(t // tm,),
        in_specs=[_rows(tm, d), _rows(tm, d), _full((1, d)), _rows(tm, d)],
        out_specs=[_full((8, LANES)), _rows(tm, d), _rows(tm, d), _full((1, d))],
        out_shape=[jax.ShapeDtypeStruct((8, LANES), F32), jax.ShapeDtypeStruct((t, d), F32),
                   jax.ShapeDtypeStruct((t, d), BF16), jax.ShapeDtypeStruct((1, d), F32)], compiler_params=_params(1),
    )(h1, dn, w, target)


def _pad_lanes(v, n=DT_PAD):
    return jnp.pad(v, ((0, 0), (0, n - v.shape[1])))


def _local_step(x, target, w, p):
    dtb, alog, dsk = _pad_lanes(p["dt_bias"]), _pad_lanes(p["a_log"]), _pad_lanes(p["d_skip"])
    bs_t = _pad_lanes(p["b_spatial"].T)
    e_heads = (jnp.arange(SSD_INNER)[:, None] // SSD_HEAD_DIM == jnp.arange(LANES)[None, :]).astype(BF16)
    e_groups = (jnp.arange(SGU_WIDTH)[:, None] // LANES == jnp.arange(LANES)[None, :]).astype(BF16)

    n1 = _norm_fwd(x, p["norm1_w"], "norm1_fwd")
    z = _mm(n1, w["z"], "nn", "proj_z")
    xbc = _mm(n1, w["xbc"], "nn", "proj_xbc")
    dtr = _mm(n1, w["dt"], "nn", "proj_dt")
    uv = _mm(n1, w["uv"], "nn", "proj_uv")
    gates = _mm(n1, w["gates"], "nn", "proj_gates")
    xc = _conv_a_fwd(xbc, w["conv_a"], p["conv_a_b"])
    y, ya, sprev = _ssd_fwd(xc, dtr, z, dtb, alog, dsk, p["ssd_norm_w"])
    yb = _sgu_fwd(uv, p["uv_b"], p["v_ln_w"], p["v_ln_b"], p["w_spatial"], bs_t)
    pa = _mm(ya, w["branch_a"], "nn", "branch_a")
    pb = _mm(yb, w["branch_b"], "nn", "branch_b")
    mix = _merge_fwd(gates, pa, pb, p["b_gate"])
    o = _mm(mix, w["out"], "nn", "out_proj")
    h1, n2 = _residual_norm_fwd(x, o, p["norm2_w"])
    up = _mm(n2, w["up"], "nn", "up_proj")
    act = _conv_f_fwd(up, w["conv_f"], p["conv_f_b"])
    dn = _mm(act, w["down"], "nn", "down_proj")
    loss, dh2, dh2b, g_final = _loss_head(h1, dn, p["final_norm_w"], target)

    g = {"final_norm_w": g_final}
    g["down"] = _mm(act, dh2b, "tn", "down_wgrad")
    dact = _mm(dh2b, w["down"], "nt", "down_dgrad")
    dup, g["conv_f"], g["conv_f_b"] = _conv_f_bwd(dact, up, w["conv_f"], p["conv_f_b"])
    g["up"] = _mm(n2, dup, "tn", "up_wgrad")
    dn2 = _mm(dup, w["up"], "nt", "up_dgrad")
    dh1, dh1b, g["norm2_w"] = _norm_bwd(dn2, h1, p["norm2_w"], dh2, "norm2_bwd")
    g["out"] = _mm(mix, dh1b, "tn", "out_wgrad")
    dmix = _mm(dh1b, w["out"], "nt", "out_dgrad")
    dgates, dpa, dpb, g["b_gate"] = _merge_bwd(dmix, gates, pa, pb, p["b_gate"])
    g["branch_a"] = _mm(ya, dpa, "tn", "branch_a_wgrad")
    g["branch_b"] = _mm(yb, dpb, "tn", "branch_b_wgrad")
    dya = _mm(dpa, w["branch_a"], "nt", "branch_a_dgrad")
    dyb = _mm(dpb, w["branch_b"], "nt", "branch_b_dgrad")
    duv, g["uv_b"], g["v_ln_w"], g["v_ln_b"], g["w_spatial"], dbs_t = _sgu_bwd(
        dyb, uv, p["uv_b"], p["v_ln_w"], p["v_ln_b"], p["w_spatial"], bs_t, e_groups)
    g["b_spatial"] = dbs_t[:, :SGU_GROUPS].T
    dz, dxs, db, dc, ddtr, g["ssd_norm_w"], ddtb, dalog, ddsk = _ssd_bwd(
        dya, y, z, xc, dtr, sprev, dtb, alog, dsk, p["ssd_norm_w"], e_heads)
    g["dt_bias"], g["a_log"], g["d_skip"] = ddtb[:, :SSD_HEADS], dalog[:, :SSD_HEADS], ddsk[:, :SSD_HEADS]
    dxbc, g["conv_a"], g["conv_a_b"] = _conv_a_bwd(dxs, db, dc, xbc, w["conv_a"], p["conv_a_b"])
    ddtrb = ddtr.astype(BF16)
    g["z"] = _mm(n1, dz, "tn", "z_wgrad")
    g["xbc"] = _mm(n1, dxbc, "tn", "xbc_wgrad")
    g["dt"] = _mm(n1, ddtrb, "tn", "dt_wgrad")
    g["uv"] = _mm(n1, duv, "tn", "uv_wgrad")
    g["gates"] = _mm(n1, dgates, "tn", "gates_wgrad")
    dn1 = _mm(dz, w["z"], "nt", "z_dgrad")
    dn1 = _mm(dxbc, w["xbc"], "nt", "xbc_dgrad", acc=dn1)
    dn1 = _mm(ddtrb, w["dt"], "nt", "dt_dgrad", acc=dn1)
    dn1 = _mm(duv, w["uv"], "nt", "uv_dgrad", acc=dn1)
    dn1 = _mm(dgates, w["gates"], "nt", "gates_dgrad", acc=dn1)
    gx, _, g["norm1_w"] = _norm_bwd(dn1, x, p["norm1_w"], dh1, "norm1_bwd")
    return loss, gx, g


def _place():
    return lax.axis_index("x"), lax.axis_index("y"), lax.axis_index("c")


def _other_chips(x, y):
    return [(1 - x, y), (x, 1 - y), (1 - x, 1 - y)]


def _all_gather(shards, name):
    n = len(shards)

    def body(*refs):
        ins, outs = refs[:n], refs[n:2 * n]
        send_sems, recv_sems, local_sems = refs[2 * n:]
        x, y, c = _place()
        me, sibling = (x, y, c), (x, y, 1 - c)
        chips = _other_chips(x, y)

        def copy(a, k, block, to, src=None):
            slot = outs[a].at[4 * block[0] + 2 * block[1] + block[2]]
            return pltpu.make_async_remote_copy(
                src_ref=slot if src is None else src, dst_ref=slot, send_sem=send_sems.at[7 * a + k],
                recv_sem=recv_sems.at[7 * a + k], device_id=to, device_id_type=MESH)

        started = []
        for a in range(n):
            mine = pltpu.make_async_copy(ins[a], outs[a].at[4 * x + 2 * y + c], local_sems.at[a])
            mine.start()
            started.append(mine)
        sends = []
        for a in range(n):
            sends.append(copy(a, 0, me, sibling, src=ins[a]))
            sends += [copy(a, 1 + j, me, (*chip, c), src=ins[a]) for j, chip in enumerate(chips)]
        for cp in sends:
            cp.start()
        for a in range(n):
            for j, chip in enumerate(chips):
                copy(a, 1 + j, (*chip, c), me).wait_recv()
                fwd = copy(a, 4 + j, (*chip, c), sibling)
                fwd.start()
                sends.append(fwd)
        for a in range(n):
            copy(a, 0, sibling, me).wait_recv()
            for j, chip in enumerate(chips):
                copy(a, 4 + j, (*chip, 1 - c), me).wait_recv()
        for cp in sends:
            cp.wait_send()
        for mine in started:
            mine.wait()

    any_spec = pl.BlockSpec(memory_space=pl.ANY)
    return pl.pallas_call(
        body, name=name, in_specs=[any_spec] * n, out_specs=[any_spec] * n,
        out_shape=[jax.ShapeDtypeStruct((N_DEV, *s.shape), s.dtype) for s in shards],
        scratch_shapes=[pltpu.SemaphoreType.DMA((7 * n,)), pltpu.SemaphoreType.DMA((7 * n,)),
                        pltpu.SemaphoreType.DMA((n,))],
    )(*shards)


def _exchange_cores(parts, name):
    n = len(parts)

    def body(*refs):
        ins, outs = refs[:n], refs[n:2 * n]
        send_sems, recv_sems = refs[2 * n:]
        x, y, c = _place()
        copies = []
        for a in range(n):
            for k in range(4):
                copies.append(pltpu.make_async_remote_copy(
                    src_ref=ins[a].at[2 * k + (1 - c)], dst_ref=outs[a].at[k], send_sem=send_sems.at[4 * a + k],
                    recv_sem=recv_sems.at[4 * a + k], device_id=(x, y, 1 - c), device_id_type=MESH))
        for cp in copies:
            cp.start()
        for cp in copies:
            cp.wait()

    any_spec = pl.BlockSpec(memory_space=pl.ANY)
    return pl.pallas_call(
        body, name=name, in_specs=[any_spec] * n, out_specs=[any_spec] * n,
        out_shape=[jax.ShapeDtypeStruct((4, *s.shape[1:]), s.dtype) for s in parts],
        scratch_shapes=[pltpu.SemaphoreType.DMA((4 * n,)), pltpu.SemaphoreType.DMA((4 * n,))],
    )(*parts)


def _exchange_chips(parts, name):
    n = len(parts)

    def body(*refs):
        ins, outs = refs[:n], refs[n:2 * n]
        send_sems, recv_sems = refs[2 * n:]
        x, y, c = _place()
        copies = []
        for a in range(n):
            for j, (cx, cy) in enumerate(_other_chips(x, y)):
                copies.append(pltpu.make_async_remote_copy(
                    src_ref=ins[a].at[2 * cx + cy], dst_ref=outs[a].at[j], send_sem=send_sems.at[3 * a + j],
                    recv_sem=recv_sems.at[3 * a + j], device_id=(cx, cy, c), device_id_type=MESH))
        for cp in copies:
            cp.start()
        for cp in copies:
            cp.wait()

    any_spec = pl.BlockSpec(memory_space=pl.ANY)
    return pl.pallas_call(
        body, name=name, in_specs=[any_spec] * n, out_specs=[any_spec] * n,
        out_shape=[jax.ShapeDtypeStruct((3, *s.shape[1:]), s.dtype) for s in parts],
        scratch_shapes=[pltpu.SemaphoreType.DMA((3 * n,)), pltpu.SemaphoreType.DMA((3 * n,))],
    )(*parts)


def _chip_sum(part, got, place, name, tr=256):
    _, r, c = part.shape
    tr = _row_tile(r, tr)

    def body(place_ref, p_ref, g_ref, q_ref, own_ref):
        s = p_ref[0].astype(F32) + g_ref[0].astype(F32)
        q_ref[0] = s.astype(BF16)

        @pl.when(pl.program_id(1) == place_ref[1])
        def _():
            own_ref[...] = s

    grid_spec = pltpu.PrefetchScalarGridSpec(
        num_scalar_prefetch=1, grid=(r // tr, 4),
        in_specs=[pl.BlockSpec((1, tr, c), lambda i, k, pr: (2 * k + pr[0], i, 0)),
                  pl.BlockSpec((1, tr, c), lambda i, k, pr: (k, i, 0))],
        out_specs=[pl.BlockSpec((1, tr, c), lambda i, k, pr: (k, i, 0)),
                   pl.BlockSpec((tr, c), lambda i, k, pr: (i, 0))])
    return pl.pallas_call(
        body, name=name, grid_spec=grid_spec,
        out_shape=[jax.ShapeDtypeStruct((4, r, c), BF16), jax.ShapeDtypeStruct((r, c), F32)],
        compiler_params=_params(2),
    )(place, part, got)


def _adamw(w, g, m, v):
    m = ADAM_B1 * m + (1.0 - ADAM_B1) * g
    v = ADAM_B2 * v + (1.0 - ADAM_B2) * jnp.square(g)
    m_hat = m / (1.0 - ADAM_B1 ** ADAM_STEP)
    v_hat = v / (1.0 - ADAM_B2 ** ADAM_STEP)
    return -ADAM_LR * (m_hat / (jnp.sqrt(v_hat) + ADAM_EPS) + ADAM_WD * w), m, v


def _sum_adamw(own, got, w, m, v, name, tr=256):
    r, c = w.shape
    tr = _row_tile(r, tr)

    def body(own_ref, got_ref, w_ref, m_ref, v_ref, g_ref, d_ref, nm_ref, nv_ref):
        g = own_ref[...]
        for j in range(3):
            g = g + got_ref[j].astype(F32)
        g_ref[...] = g
        d_ref[...], nm_ref[...], nv_ref[...] = _adamw(w_ref[...], g, m_ref[...], v_ref[...])

    blk = pl.BlockSpec((tr, c), lambda i: (i, 0))
    return pl.pallas_call(
        body, name=name, grid=(r // tr,),
        in_specs=[blk, pl.BlockSpec((3, tr, c), lambda i: (0, i, 0)), blk, blk, blk], out_specs=[blk] * 4,
        out_shape=[jax.ShapeDtypeStruct((r, c), F32)] * 4, compiler_params=_params(1),
    )(own, got, w, m, v)


def _sum_devices(parts, name):
    _, r, c = parts.shape
    tr = r

    def body(p_ref, o_ref):
        s = p_ref[0]
        for d in range(1, N_DEV):
            s = s + p_ref[d]
        o_ref[...] = s

    return pl.pallas_call(
        body, name=name, grid=(pl.cdiv(r, tr),), in_specs=[pl.BlockSpec((N_DEV, tr, c), lambda i: (0, i, 0))],
        out_specs=pl.BlockSpec((tr, c), lambda i: (i, 0)), out_shape=jax.ShapeDtypeStruct((r, c), F32),
        compiler_params=_params(1),
    )(parts)


def _adamw_call(w, g, m, v, name):
    r, c = w.shape
    tr = r

    def body(w_ref, g_ref, m_ref, v_ref, d_ref, nm_ref, nv_ref):
        d_ref[...], nm_ref[...], nv_ref[...] = _adamw(w_ref[...], g_ref[...], m_ref[...], v_ref[...])

    blk = pl.BlockSpec((tr, c), lambda i: (i, 0))
    return pl.pallas_call(
        body, name=name, grid=(pl.cdiv(r, tr),), in_specs=[blk] * 4, out_specs=[blk] * 3,
        out_shape=[jax.ShapeDtypeStruct((r, c), F32)] * 3, compiler_params=_params(1),
    )(w, g, m, v)


PACK_ROWS = 8


def _pack(arrays):
    parts = []
    for a in arrays:
        flat = a.reshape(-1)
        unit = PACK_ROWS * LANES
        parts.append(jnp.pad(flat, (0, -flat.shape[0] % unit)).reshape(-1, LANES))
    return jnp.concatenate(parts, axis=0)


def _unpack(pack, shapes):
    out, row = [], 0
    for s in shapes:
        size = 1
        for d in s:
            size *= d
        rows = -(-size // (PACK_ROWS * LANES)) * PACK_ROWS
        out.append(pack[row:row + rows].reshape(-1)[:size].reshape(s))
        row += rows
    return out


SMALL = ["norm1_w", "b_gate", "conv_a_b", "dt_bias", "a_log", "d_skip", "ssd_norm_w", "uv_b", "v_ln_w", "v_ln_b",
         "w_spatial", "b_spatial", "norm2_w", "conv_f_b", "final_norm_w"]
BIG = ["w_in", "w_branch", "w_out", "w_up", "w_down"]
WEIGHTS = ["norm1_w", "w_in", "b_gate", "conv_a_w", "conv_a_b", "dt_bias", "a_log", "d_skip", "ssd_norm_w", "uv_b",
           "v_ln_w", "v_ln_b", "w_spatial", "b_spatial", "w_branch", "w_out", "norm2_w", "w_up", "conv_f_w",
           "conv_f_b", "w_down", "final_norm_w"]
IN_SPLITS = [("z", 0, 2048), ("xbc", 2048, 5120), ("dt", 5120, 5152), ("uv", 5152, 7200), ("gates", 7200, 9248)]


def _columns_by_device(a):
    r = a.shape[0]
    return a.reshape(r, N_DEV, -1).transpose(1, 0, 2)


def _columns_from_devices(a):
    return a.transpose(1, 0, 2).reshape(a.shape[1], -1)


def kernel(x, norm1_w, w_in, b_gate, conv_a_w, conv_a_b, dt_bias, a_log, d_skip, ssd_norm_w, uv_b, v_ln_w, v_ln_b, w_spatial, b_spatial, w_branch, w_out, norm2_w, w_up, conv_f_w, conv_f_b, w_down, final_norm_w, loss_target, m_norm1_w, m_w_in, m_b_gate, m_conv_a_w, m_conv_a_b, m_dt_bias, m_a_log, m_d_skip, m_ssd_norm_w, m_uv_b, m_v_ln_w, m_v_ln_b, m_w_spatial, m_b_spatial, m_w_branch, m_w_out, m_norm2_w, m_w_up, m_conv_f_w, m_conv_f_b, m_w_down, m_final_norm_w, v_norm1_w, v_w_in, v_b_gate, v_conv_a_w, v_conv_a_b, v_dt_bias, v_a_log, v_d_skip, v_ssd_norm_w, v_uv_b, v_v_ln_w, v_v_ln_b, v_w_spatial, v_b_spatial, v_w_branch, v_w_out, v_norm2_w, v_w_up, v_conv_f_w, v_conv_f_b, v_w_down, v_final_norm_w):
    args = dict(locals())
    wts = {n: args[n] for n in WEIGHTS}
    mom = {n: args["m_" + n] for n in WEIGHTS}
    var = {n: args["v_" + n] for n in WEIGHTS}
    cx, cy, cc = _place()
    dev = 4 * cx + 2 * cy + cc
    place = jnp.stack([cc, 2 * cx + cy]).astype(jnp.int32)

    shards = [wts[n][0].astype(BF16) for n in BIG] + [conv_a_w[0], conv_f_w[0]]
    g_in, g_branch, g_out, g_up, g_down, g_conv_a, g_conv_f = _all_gather(shards, "gather_weights")
    w_in_full = _columns_from_devices(g_in)
    w = {name: w_in_full[:, lo:hi] for name, lo, hi in IN_SPLITS}
    w["dt"] = _pad_lanes(w["dt"])
    branch = g_branch.reshape(-1, D_MODEL)
    w["branch_a"], w["branch_b"] = branch[:SSD_INNER], branch[SSD_INNER:]
    w["out"] = g_out.reshape(-1, D_MODEL)
    w["up"] = _columns_from_devices(g_up)
    w["down"] = g_down.reshape(-1, D_MODEL)
    w["conv_a"] = _columns_from_devices(g_conv_a)
    w["conv_f"] = _columns_from_devices(g_conv_f)

    p = {n: wts[n][0] if wts[n].ndim > 2 else wts[n].reshape(1, -1) for n in SMALL}
    loss, gx, g = _local_step(x[0], loss_target[0], w, p)
    loss = lax.psum(loss[0, 0], ("x", "y", "c"))

    small_g = [g[n] for n in SMALL] + [g["conv_a"], g["conv_f"]]
    gathered, = _all_gather([_pack(small_g)], "gather_small_grads")
    small_sum = _unpack(_sum_devices(gathered, "sum_small_grads"), [a.shape for a in small_g])
    grads = {n: s.reshape(wts[n].shape) for n, s in zip(SMALL, small_sum[:len(SMALL)])}
    for n, s in (("conv_a_w", small_sum[-2]), ("conv_f_w", small_sum[-1])):
        cols = wts[n].shape[2]
        grads[n] = lax.dynamic_slice_in_dim(s, dev * cols, cols, axis=1)[None]

    g_in_full = jnp.concatenate([g[name][:, :hi - lo] for name, lo, hi in IN_SPLITS], axis=1)
    parts = [_columns_by_device(g_in_full), jnp.concatenate([g["branch_a"], g["branch_b"]], axis=0),
             g["out"], _columns_by_device(g["up"]), g["down"]]
    parts = [a.astype(BF16).reshape(N_DEV, *wts[n].shape[1:]) for n, a in zip(BIG, parts)]
    from_core = _exchange_cores(parts, "grads_to_other_core")
    sums = [_chip_sum(a, b, place, f"chip_sum_{n}") for n, a, b in zip(BIG, parts, from_core)]
    from_chips = _exchange_chips([s[0] for s in sums], "grads_to_other_chips")

    delta, new_m, new_v = {}, {}, {}
    for n, s, got in zip(BIG, sums, from_chips):
        gr, d, nm, nv = _sum_adamw(s[1], got, wts[n][0], mom[n][0], var[n][0], f"adamw_{n}")
        grads[n], delta[n], new_m[n], new_v[n] = gr[None], d[None], nm[None], nv[None]
    small_names = SMALL + ["conv_a_w", "conv_f_w"]
    packs = [_pack([t[n] for n in small_names]) for t in (wts, grads, mom, var)]
    outs = _adamw_call(*packs, "adamw_small")
    shapes = [wts[n].shape for n in small_names]
    for tgt, pack in zip((delta, new_m, new_v), outs):
        tgt.update(dict(zip(small_names, _unpack(pack, shapes))))

    return (loss, gx[None], *[grads[n] for n in WEIGHTS], *[delta[n] for n in WEIGHTS],
            *[new_m[n] for n in WEIGHTS], *[new_v[n] for n in WEIGHTS])
```

```python
import functools

import jax
import jax.numpy as jnp
from jax import lax
from jax.experimental import pallas as pl
from jax.experimental.pallas import tpu as pltpu

F32, BF16 = jnp.float32, jnp.bfloat16
HIGHEST = lax.Precision.HIGHEST

D_MODEL = 1024
SSD_INNER = 2048
SSD_HEAD_DIM = 64
SSD_HEADS = 32
SSD_GROUPS = 4
SSD_STATE = 128
SSD_BC = SSD_GROUPS * SSD_STATE
SSD_XBC = SSD_INNER + 2 * SSD_BC
SSD_CONV = 4
CHUNK = 128
N_PAIRS = SSD_HEADS // 2
PAIRS_PER_GROUP = N_PAIRS // SSD_GROUPS
SGU_WIDTH = 1024
SGU_GROUPS = 8
D_FF = 2816
FFN_CONV = 3
NORM_EPS = 1e-6
LN_EPS = 1e-5
LANES = 128
DT_PAD = LANES

ADAM_LR, ADAM_B1, ADAM_B2, ADAM_EPS, ADAM_WD, ADAM_STEP = 0.001, 0.9, 0.999, 1e-08, 0.01, 10

N_DEV = 8
VMEM_LIMIT = 56 * 1024 * 1024
MESH = pl.DeviceIdType.MESH


def _params(n_grid, **kw):
    sem = dict(dimension_semantics=("arbitrary",) * n_grid) if n_grid else {}
    return pltpu.CompilerParams(vmem_limit_bytes=VMEM_LIMIT, **sem, **kw)


def _tile(n, pref):
    t = (min(pref, n) // LANES) * LANES
    while n % t:
        t -= LANES
    return t


def _row_tile(r, pref):
    for t in range(min(pref, r) // 16 * 16, 0, -16):
        if r % t == 0:
            return t
    return r


def _rows(tm, n, nt=None, rev=False, col=0):
    if rev:
        return pl.BlockSpec((tm, n), lambda i: (nt - 1 - i, col))
    return pl.BlockSpec((tm, n), lambda i: (i, col))


def _halo(tm, n, nt=None, rev=False):
    per = tm // 8
    if rev:
        return pl.BlockSpec((8, n), lambda i: (jnp.maximum((nt - 1 - i) * per - 1, 0), 0))
    return pl.BlockSpec((8, n), lambda i: (jnp.maximum(i * per - 1, 0), 0))


def _full(shape):
    nd = len(shape)
    return pl.BlockSpec(shape, lambda *_: (0,) * nd)


def _rms(x, w, eps=NORM_EPS):
    return x * lax.rsqrt(jnp.mean(x * x, axis=-1, keepdims=True) + eps) * w


def _layer_norm(x, w, b):
    mu = jnp.mean(x, axis=-1, keepdims=True)
    var = jnp.mean(jnp.square(x - mu), axis=-1, keepdims=True)
    return (x - mu) * lax.rsqrt(var + LN_EPS) * w + b


def _sigmoid(x):
    return 1.0 / (1.0 + jnp.exp(-x))


def _silu(x):
    return x * _sigmoid(x)


def _dsilu(x):
    s = _sigmoid(x)
    return s * (1.0 + x * (1.0 - s))


def _softplus(x):
    return jnp.maximum(x, 0.0) + jnp.log(1.0 + jnp.exp(-jnp.abs(x)))


def _gelu(x):
    return jax.nn.gelu(x)


def _dot(a, b):
    return jnp.dot(a, b, preferred_element_type=F32)


def _dot_nt(a, b):
    return lax.dot_general(a, b, (((1,), (1,)), ((), ())), preferred_element_type=F32)


def _dot_tn(a, b):
    return lax.dot_general(a, b, (((0,), (0,)), ((), ())), preferred_element_type=F32)


def _dot_split(p, e):
    hi = p.astype(BF16)
    lo = (p - hi.astype(F32)).astype(BF16)
    return _dot(hi, e) + _dot(lo, e)


def _colsum(x):
    return jnp.sum(x, axis=0, keepdims=True)


def _shift_down(x, halo, j):
    xs = pltpu.roll(x, j, 0)
    hs = pltpu.roll(halo, j, 0)
    r8 = lax.broadcasted_iota(jnp.int32, hs.shape, 0)
    return jnp.concatenate([jnp.where(r8 < j, hs, xs[:8]), xs[8:]], axis=0)


def _shift_up(x, nxt, j):
    n = x.shape[0]
    xs = pltpu.roll(x, n - j, 0)
    ns = pltpu.roll(nxt, 8 - j, 0)
    r8 = lax.broadcasted_iota(jnp.int32, ns.shape, 0)
    return jnp.concatenate([xs[:n - 8], jnp.where(r8 >= 8 - j, ns, xs[n - 8:])], axis=0)


def _causal_conv(x, halo, w, b):
    k = w.shape[0]
    shifted = [x] + [_shift_down(x, halo, j) for j in range(1, k)]
    y = b + w[k - 1:k, :] * x
    for j in range(1, k):
        y = y + w[k - 1 - j:k - j, :] * shifted[j]
    return y, shifted


def _anticausal_conv(dy, nxt, w):
    k = w.shape[0]
    dx = w[k - 1:k, :] * dy
    for j in range(1, k):
        dx = dx + w[k - 1 - j:k - j, :] * _shift_up(dy, nxt, j)
    return dx


def _conv_wgrad(dy, shifted):
    k = len(shifted)
    return jnp.concatenate([_colsum(dy * shifted[k - 1 - i]) for i in range(k)], axis=0)


MM_TILE_PREF = 1408
MM_VMEM_BUDGET = 40 * 1024 * 1024


def _mm_tiles(m, n, k, out_bytes):
    tm, tn = _tile(m, MM_TILE_PREF), _tile(n, MM_TILE_PREF)
    need = lambda tm, tn: 2 * (2 * k * (tm + tn) + out_bytes * tm * tn)
    while need(tm, tn) > MM_VMEM_BUDGET:
        if tn >= tm and tn > LANES:
            tn = _tile(n, tn - LANES)
        else:
            tm = _tile(m, tm - LANES)
    return tm, tn


def _mm(a, b, dims, name, acc=None, out_dtype=F32):
    if dims == "tn":
        k, m = a.shape
    else:
        m, k = a.shape
    n = b.shape[0] if dims == "nt" else b.shape[1]
    tm, tn = _mm_tiles(m, n, k, 4 * (2 if acc is not None else 1))
    a_spec = pl.BlockSpec((k, tm), lambda j, i: (0, i)) if dims == "tn" else pl.BlockSpec((tm, k), lambda j, i: (i, 0))
    b_spec = pl.BlockSpec((tn, k), lambda j, i: (j, 0)) if dims == "nt" else pl.BlockSpec((k, tn), lambda j, i: (0, j))
    o_spec = pl.BlockSpec((tm, tn), lambda j, i: (i, j))
    dot = {"nn": _dot, "nt": _dot_nt, "tn": _dot_tn}[dims]

    def body(a_ref, b_ref, *rest):
        r = dot(a_ref[...], b_ref[...])
        if acc is not None:
            r = r + rest[0][...]
        rest[-1][...] = r.astype(out_dtype)

    ins, specs = [a, b], [a_spec, b_spec]
    if acc is not None:
        ins.append(acc)
        specs.append(o_spec)
    return pl.pallas_call(
        body, name=name, grid=(n // tn, m // tm), in_specs=specs, out_specs=o_spec,
        out_shape=jax.ShapeDtypeStruct((m, n), out_dtype), compiler_params=_params(2),
    )(*ins)


def _wgrad(a, d, name):
    return _mm(a, d, "tn", name, out_dtype=BF16)


def _norm_fwd(x, w, name, tm=512):
    t, d = x.shape

    def body(x_ref, w_ref, o_ref):
        o_ref[...] = _rms(x_ref[...], w_ref[...]).astype(BF16)

    return pl.pallas_call(
        body, name=name, grid=(t // tm,), in_specs=[_rows(tm, d), _full((1, d))], out_specs=_rows(tm, d),
        out_shape=jax.ShapeDtypeStruct((t, d), BF16), compiler_params=_params(1),
    )(x, w)


def _conv_a_fwd(xbc, cw, cb, tm=256):
    t, c = xbc.shape

    def body(x_ref, h_ref, w_ref, b_ref, o_ref):
        halo = jnp.where(pl.program_id(0) > 0, h_ref[...], 0.0)
        y, _ = _causal_conv(x_ref[...], halo, w_ref[...], b_ref[...])
        o_ref[...] = _silu(y)

    return pl.pallas_call(
        body, name="conv_a_fwd", grid=(t // tm,),
        in_specs=[_rows(tm, c), _halo(tm, c), _full(cw.shape), _full((1, c))], out_specs=_rows(tm, c),
        out_shape=jax.ShapeDtypeStruct((t, c), F32), compiler_params=_params(1),
    )(xbc, xbc, cw, cb)


def _ssd_common(dtr, dtb, alog):
    row = lax.broadcasted_iota(jnp.int32, (CHUNK, CHUNK), 0)
    col = lax.broadcasted_iota(jnp.int32, (CHUNK, CHUNK), 1)
    causal = row >= col
    dt = _softplus(dtr + dtb)
    a = -jnp.exp(alog)
    acum = jnp.dot(causal.astype(F32), dt * a, precision=HIGHEST, preferred_element_type=F32)
    return dt, a, acum, acum.T, causal, col < SSD_HEAD_DIM, row


def _pair_terms(j, dt, acum, acum_t, causal, lane_lo):
    h0, h1 = 2 * j, 2 * j + 1
    ac0, ac1 = acum[:, h0:h0 + 1], acum[:, h1:h1 + 1]
    l0 = jnp.exp(jnp.where(causal, ac0 - acum_t[h0:h0 + 1, :], -jnp.inf))
    l1 = jnp.exp(jnp.where(causal, ac1 - acum_t[h1:h1 + 1, :], -jnp.inf))
    dtp = jnp.where(lane_lo, dt[:, h0:h0 + 1], dt[:, h1:h1 + 1])
    al0, al1 = acum[CHUNK - 1:CHUNK, h0:h0 + 1], acum[CHUNK - 1:CHUNK, h1:h1 + 1]
    ecol = jnp.where(lane_lo, jnp.exp(ac0), jnp.exp(ac1))
    dsr = jnp.where(lane_lo, jnp.exp(al0 - ac0), jnp.exp(al1 - ac1))
    elast = jnp.where(lane_lo[0:1], jnp.exp(al0), jnp.exp(al1))
    return l0, l1, dtp, ecol, dsr, elast


def _ssd_fwd(xc, dtr, z, dtb, alog, dsk, nw):
    t = xc.shape[0]
    nc = t // CHUNK

    def body(xs_ref, b_ref, c_ref, dtr_ref, z_ref, dtb_ref, alog_ref, dsk_ref, nw_ref, y_ref, ya_ref, sp_ref, s_scr):
        @pl.when(pl.program_id(0) == 0)
        def _():
            s_scr[...] = jnp.zeros_like(s_scr)

        dt, a, acum, acum_t, causal, lane_lo, _ = _ssd_common(dtr_ref[...], dtb_ref[...], alog_ref[...])
        dsk = dsk_ref[...]
        for g in range(SSD_GROUPS):
            gs = slice(g * SSD_STATE, (g + 1) * SSD_STATE)
            bg, cg = b_ref[:, gs].astype(BF16), c_ref[:, gs].astype(BF16)
            cb = _dot_nt(cg, bg)
            for pp in range(PAIRS_PER_GROUP):
                j = g * PAIRS_PER_GROUP + pp
                ps = slice(j * LANES, (j + 1) * LANES)
                x = xs_ref[:, ps]
                l0, l1, dtp, ecol, dsr, elast = _pair_terms(j, dt, acum, acum_t, causal, lane_lo)
                xdt = x * dtp
                xb = xdt.astype(BF16)
                zero = jnp.zeros_like(xb)
                yd = (_dot((cb * l0).astype(BF16), jnp.where(lane_lo, xb, zero))
                      + _dot((cb * l1).astype(BF16), jnp.where(lane_lo, zero, xb)))
                sp = s_scr[j]
                yo = ecol * _dot(cg, sp.astype(BF16))
                st = _dot_tn(bg, (xdt * dsr).astype(BF16))
                sp_ref[0, j] = sp
                s_scr[j] = elast * sp + st
                dskp = jnp.where(lane_lo[0:1], dsk[:, 2 * j:2 * j + 1], dsk[:, 2 * j + 1:2 * j + 2])
                y_ref[:, ps] = yd + yo + dskp * x
        ya_ref[...] = _rms(y_ref[...] * _silu(z_ref[...]), nw_ref[...]).astype(BF16)

    ck = lambda n, col=0: pl.BlockSpec((CHUNK, n), lambda c: (c, col))
    return pl.pallas_call(
        body, name="ssd_fwd", grid=(nc,),
        in_specs=[ck(SSD_INNER), ck(SSD_BC, SSD_INNER // SSD_BC), ck(SSD_BC, SSD_INNER // SSD_BC + 1), ck(DT_PAD),
                  ck(SSD_INNER), _full((1, DT_PAD)), _full((1, DT_PAD)), _full((1, DT_PAD)), _full((1, SSD_INNER))],
        out_specs=[ck(SSD_INNER), ck(SSD_INNER),
                   pl.BlockSpec((1, N_PAIRS, SSD_STATE, LANES), lambda c: (c, 0, 0, 0))],
        out_shape=[jax.ShapeDtypeStruct((t, SSD_INNER), F32), jax.ShapeDtypeStruct((t, SSD_INNER), BF16),
                   jax.ShapeDtypeStruct((nc, N_PAIRS, SSD_STATE, LANES), F32)],
        scratch_shapes=[pltpu.VMEM((N_PAIRS, SSD_STATE, LANES), F32)], compiler_params=_params(1),
    )(xc, xc, xc, dtr, z, dtb, alog, dsk, nw)


def _ssd_bwd(dya, y, z, xc, dtr, sprev, dtb, alog, dsk, nw, e_heads):
    t = xc.shape[0]
    nc = t // CHUNK

    def body(dya_ref, y_ref, z_ref, xs_ref, b_ref, c_ref, dtr_ref, sp_ref, dtb_ref, alog_ref, dsk_ref, nw_ref, e_ref,
             dz_ref, dxs_ref, db_ref, dc_ref, ddtr_ref, dnw_ref, ddtb_ref, dalog_ref, ddsk_ref, ds_scr):
        @pl.when(pl.program_id(0) == 0)
        def _():
            ds_scr[...] = jnp.zeros_like(ds_scr)
            for r in (dnw_ref, ddtb_ref, dalog_ref, ddsk_ref):
                r[...] = jnp.zeros_like(r)

        y = y_ref[...]
        _, gate_vjp = jax.vjp(lambda y_, z_, w_: _rms(y_ * _silu(z_), w_), y, z_ref[...], nw_ref[...])
        dy, dz, dnw = gate_vjp(dya_ref[...])
        dz_ref[...] = dz.astype(BF16)
        dnw_ref[...] += dnw

        dtr = dtr_ref[...]
        dt, a, acum, acum_t, causal, lane_lo, row = _ssd_common(dtr, dtb_ref[...], alog_ref[...])
        dsk = dsk_ref[...]
        p_a, p_dt, v_last = [], [], []
        col = lax.broadcasted_iota(jnp.int32, (CHUNK, CHUNK), 1)
        da_cols = jnp.zeros((CHUNK, CHUNK), F32)
        da_rows = jnp.zeros((CHUNK, CHUNK), F32)
        for g in range(SSD_GROUPS):
            gs = slice(g * SSD_STATE, (g + 1) * SSD_STATE)
            bg, cg = b_ref[:, gs].astype(BF16), c_ref[:, gs].astype(BF16)
            cb = _dot_nt(cg, bg)
            dcb = jnp.zeros((CHUNK, CHUNK), F32)
            dbg = jnp.zeros((CHUNK, SSD_STATE), F32)
            dcg = jnp.zeros((CHUNK, SSD_STATE), F32)
            for pp in range(PAIRS_PER_GROUP):
                j = g * PAIRS_PER_GROUP + pp
                ps = slice(j * LANES, (j + 1) * LANES)
                x = xs_ref[:, ps]
                l0, l1, dtp, ecol, dsr, elast = _pair_terms(j, dt, acum, acum_t, causal, lane_lo)
                xdt = x * dtp
                xb = xdt.astype(BF16)
                dskp = jnp.where(lane_lo[0:1], dsk[:, 2 * j:2 * j + 1], dsk[:, 2 * j + 1:2 * j + 2])
                dyp = dy[:, ps]
                dyb = dyp.astype(BF16)
                sp, dsn = sp_ref[0, j], ds_scr[j]
                spb, dsnb = sp.astype(BF16), dsn.astype(BF16)
                y_off = ecol * _dot(cg, spb)
                dw = (dyp * ecol).astype(BF16)
                dcg = dcg + _dot_nt(dw, spb)
                dsp = _dot_tn(cg, dw) + elast * dsn
                xd = xdt * dsr
                zd = _dot(bg, dsnb) * dsr
                dbg = dbg + _dot_nt(xd.astype(BF16), dsnb)
                dxdt = zd
                zero = jnp.zeros_like(xb)
                for h, lm, le in ((2 * j, lane_lo, l0), (2 * j + 1, jnp.logical_not(lane_lo), l1)):
                    dm = _dot_nt(jnp.where(lm, dyb, zero), jnp.where(lm, xb, zero))
                    dcb = dcb + dm * le
                    m = cb * le
                    dxdt = dxdt + jnp.where(lm, _dot_tn(m.astype(BF16), dyb), 0.0)
                    q = dm * m
                    da_cols = da_cols + jnp.where(col == h, jnp.sum(q, axis=1, keepdims=True), 0.0)
                    da_rows = da_rows + jnp.where(row == h, _colsum(q), 0.0)
                ds_scr[j] = dsp
                dxs_ref[:, ps] = dxdt * dtp + dskp * dyp
                p_a.append(dyp * y_off - xdt * zd)
                p_dt.append(dxdt * x)
                v_last.append(_colsum(zd * xdt) + elast * _colsum(dsn * sp))
            dcbb = dcb.astype(BF16)
            db_ref[:, gs] = dbg + _dot_tn(dcbb, cg)
            dc_ref[:, gs] = dcg + _dot(dcbb, bg)
        e = e_ref[...]
        rows8 = jnp.concatenate([jnp.concatenate(v_last, axis=1), _colsum(dy * xs_ref[...]),
                                 jnp.zeros((6, SSD_INNER), F32)], axis=0)
        r8 = _dot_split(rows8, e)
        da = (_dot_split(jnp.concatenate(p_a, axis=1), e) + jnp.where(row == CHUNK - 1, r8[0:1], 0.0)
              + da_cols - da_rows.T)
        ddsk_ref[...] += r8[1:2]
        dadt = jnp.dot((row <= col).astype(F32), da, precision=HIGHEST, preferred_element_type=F32)
        ddt = dadt * a + _dot_split(jnp.concatenate(p_dt, axis=1), e)
        dalog_ref[...] += _colsum(dadt * dt) * a
        ddtr = ddt * _sigmoid(dtr + dtb_ref[...])
        ddtr_ref[...] = ddtr
        ddtb_ref[...] += _colsum(ddtr)

    ck = lambda n, col=0: pl.BlockSpec((CHUNK, n), lambda c: (nc - 1 - c, col))
    acc = lambda n: _full((1, n))
    return pl.pallas_call(
        body, name="ssd_bwd", grid=(nc,),
        in_specs=[ck(SSD_INNER), ck(SSD_INNER), ck(SSD_INNER), ck(SSD_INNER), ck(SSD_BC, SSD_INNER // SSD_BC),
                  ck(SSD_BC, SSD_INNER // SSD_BC + 1), ck(DT_PAD),
                  pl.BlockSpec((1, N_PAIRS, SSD_STATE, LANES), lambda c: (nc - 1 - c, 0, 0, 0)),
                  acc(DT_PAD), acc(DT_PAD), acc(DT_PAD), acc(SSD_INNER), _full((SSD_INNER, LANES))],
        out_specs=[ck(SSD_INNER), ck(SSD_INNER), ck(SSD_BC), ck(SSD_BC), ck(DT_PAD),
                   acc(SSD_INNER), acc(DT_PAD), acc(DT_PAD), acc(DT_PAD)],
        out_shape=[jax.ShapeDtypeStruct((t, SSD_INNER), BF16), jax.ShapeDtypeStruct((t, SSD_INNER), F32),
                   jax.ShapeDtypeStruct((t, SSD_BC), F32), jax.ShapeDtypeStruct((t, SSD_BC), F32),
                   jax.ShapeDtypeStruct((t, DT_PAD), F32), jax.ShapeDtypeStruct((1, SSD_INNER), F32),
                   jax.ShapeDtypeStruct((1, DT_PAD), F32), jax.ShapeDtypeStruct((1, DT_PAD), F32),
                   jax.ShapeDtypeStruct((1, DT_PAD), F32)],
        scratch_shapes=[pltpu.VMEM((N_PAIRS, SSD_STATE, LANES), F32)], compiler_params=_params(1),
    )(dya, y, z, xc, xc, xc, dtr, sprev, dtb, alog, dsk, nw, e_heads)


def _sgu_act(uv, uvb, lnw, lnb):
    a = _gelu(uv + uvb)
    return a[:, :SGU_WIDTH], _layer_norm(a[:, SGU_WIDTH:], lnw, lnb)


def _sgu_weights(ws_ref):
    row = lax.broadcasted_iota(jnp.int32, (CHUNK, CHUNK), 0)
    col = lax.broadcasted_iota(jnp.int32, (CHUNK, CHUNK), 1)
    return [jnp.where(row >= col, ws_ref[g], 0.0).astype(BF16) for g in range(SGU_GROUPS)], row >= col


def _sgu_fwd(uv, uvb, lnw, lnb, ws, bs_t):
    t = uv.shape[0]

    def body(uv_ref, uvb_ref, lnw_ref, lnb_ref, ws_ref, bs_ref, o_ref):
        u, vn = _sgu_act(uv_ref[...], uvb_ref[...], lnw_ref[...], lnb_ref[...])
        wc, _ = _sgu_weights(ws_ref)
        bs = bs_ref[...]
        for g in range(SGU_GROUPS):
            gs = slice(g * LANES, (g + 1) * LANES)
            mixed = _dot(wc[g], vn[:, gs].astype(BF16)) + bs[:, g:g + 1]
            o_ref[:, gs] = (u[:, gs] * mixed).astype(BF16)

    return pl.pallas_call(
        body, name="sgu_fwd", grid=(t // CHUNK,),
        in_specs=[_rows(CHUNK, 2 * SGU_WIDTH), _full((1, 2 * SGU_WIDTH)), _full((1, SGU_WIDTH)), _full((1, SGU_WIDTH)),
                  _full(ws.shape), _full(bs_t.shape)],
        out_specs=_rows(CHUNK, SGU_WIDTH), out_shape=jax.ShapeDtypeStruct((t, SGU_WIDTH), BF16),
        compiler_params=_params(1),
    )(uv, uvb, lnw, lnb, ws, bs_t)


def _sgu_bwd(dyb, uv, uvb, lnw, lnb, ws, bs_t, e_groups):
    t = uv.shape[0]

    def body(dyb_ref, uv_ref, uvb_ref, lnw_ref, lnb_ref, ws_ref, bs_ref, e_ref,
             duv_ref, duvb_ref, dlnw_ref, dlnb_ref, dws_ref, dbs_ref):
        @pl.when(pl.program_id(0) == 0)
        def _():
            for r in (duvb_ref, dlnw_ref, dlnb_ref, dws_ref, dbs_ref):
                r[...] = jnp.zeros_like(r)

        (u, vn), act_vjp = jax.vjp(_sgu_act, uv_ref[...], uvb_ref[...], lnw_ref[...], lnb_ref[...])
        wc, causal = _sgu_weights(ws_ref)
        bs = bs_ref[...]
        dyb = dyb_ref[...]
        du, dvn, dmix = [], [], []
        for g in range(SGU_GROUPS):
            gs = slice(g * LANES, (g + 1) * LANES)
            vb = vn[:, gs].astype(BF16)
            mixed = _dot(wc[g], vb) + bs[:, g:g + 1]
            dm = dyb[:, gs] * u[:, gs]
            dmb = dm.astype(BF16)
            du.append(dyb[:, gs] * mixed)
            dvn.append(_dot_tn(wc[g], dmb))
            dws_ref[g] += jnp.where(causal, _dot_nt(dmb, vb), 0.0)
            dmix.append(dm)
        dbs_ref[...] += _dot_split(jnp.concatenate(dmix, axis=1), e_ref[...])
        duv, duvb, dlnw, dlnb = act_vjp((jnp.concatenate(du, axis=1), jnp.concatenate(dvn, axis=1)))
        duv_ref[...] = duv.astype(BF16)
        duvb_ref[...] += duvb
        dlnw_ref[...] += dlnw
        dlnb_ref[...] += dlnb

    return pl.pallas_call(
        body, name="sgu_bwd", grid=(t // CHUNK,),
        in_specs=[_rows(CHUNK, SGU_WIDTH), _rows(CHUNK, 2 * SGU_WIDTH), _full((1, 2 * SGU_WIDTH)),
                  _full((1, SGU_WIDTH)), _full((1, SGU_WIDTH)), _full(ws.shape), _full(bs_t.shape),
                  _full(e_groups.shape)],
        out_specs=[_rows(CHUNK, 2 * SGU_WIDTH), _full((1, 2 * SGU_WIDTH)), _full((1, SGU_WIDTH)),
                   _full((1, SGU_WIDTH)), _full(ws.shape), _full(bs_t.shape)],
        out_shape=[jax.ShapeDtypeStruct((t, 2 * SGU_WIDTH), BF16), jax.ShapeDtypeStruct((1, 2 * SGU_WIDTH), F32),
                   jax.ShapeDtypeStruct((1, SGU_WIDTH), F32), jax.ShapeDtypeStruct((1, SGU_WIDTH), F32),
                   jax.ShapeDtypeStruct(ws.shape, F32), jax.ShapeDtypeStruct(bs_t.shape, F32)],
        compiler_params=_params(1),
    )(dyb, uv, uvb, lnw, lnb, ws, bs_t, e_groups)


def _merge(gates, pa, pb, bg):
    s = _sigmoid(gates + bg)
    return s[:, :D_MODEL] * pa + s[:, D_MODEL:] * pb


def _merge_fwd(gates, pa, pb, bg, tm=256):
    t = gates.shape[0]

    def body(g_ref, pa_ref, pb_ref, bg_ref, o_ref):
        o_ref[...] = _merge(g_ref[...], pa_ref[...], pb_ref[...], bg_ref[...]).astype(BF16)

    return pl.pallas_call(
        body, name="merge_fwd", grid=(t // tm,),
        in_specs=[_rows(tm, 2 * D_MODEL), _rows(tm, D_MODEL), _rows(tm, D_MODEL), _full((1, 2 * D_MODEL))],
        out_specs=_rows(tm, D_MODEL), out_shape=jax.ShapeDtypeStruct((t, D_MODEL), BF16), compiler_params=_params(1),
    )(gates, pa, pb, bg)


def _merge_bwd(dmix, gates, pa, pb, bg, tm=256):
    t = gates.shape[0]

    def body(d_ref, g_ref, pa_ref, pb_ref, bg_ref, dg_ref, dpa_ref, dpb_ref, dbg_ref):
        @pl.when(pl.program_id(0) == 0)
        def _():
            dbg_ref[...] = jnp.zeros_like(dbg_ref)

        _, vjp = jax.vjp(_merge, g_ref[...], pa_ref[...], pb_ref[...], bg_ref[...])
        dg, dpa, dpb, dbg = vjp(d_ref[...])
        dg_ref[...] = dg.astype(BF16)
        dpa_ref[...] = dpa.astype(BF16)
        dpb_ref[...] = dpb.astype(BF16)
        dbg_ref[...] += dbg

    return pl.pallas_call(
        body, name="merge_bwd", grid=(t // tm,),
        in_specs=[_rows(tm, D_MODEL), _rows(tm, 2 * D_MODEL), _rows(tm, D_MODEL), _rows(tm, D_MODEL),
                  _full((1, 2 * D_MODEL))],
        out_specs=[_rows(tm, 2 * D_MODEL), _rows(tm, D_MODEL), _rows(tm, D_MODEL), _full((1, 2 * D_MODEL))],
        out_shape=[jax.ShapeDtypeStruct((t, 2 * D_MODEL), BF16), jax.ShapeDtypeStruct((t, D_MODEL), BF16),
                   jax.ShapeDtypeStruct((t, D_MODEL), BF16), jax.ShapeDtypeStruct((1, 2 * D_MODEL), F32)],
        compiler_params=_params(1),
    )(dmix, gates, pa, pb, bg)


def _residual_norm_fwd(x, o, w, tm=512):
    t, d = x.shape

    def body(x_ref, o_ref, w_ref, h_ref, n_ref):
        h = x_ref[...] + o_ref[...]
        h_ref[...] = h
        n_ref[...] = _rms(h, w_ref[...]).astype(BF16)

    return pl.pallas_call(
        body, name="residual_norm_fwd", grid=(t // tm,), in_specs=[_rows(tm, d), _rows(tm, d), _full((1, d))],
        out_specs=[_rows(tm, d), _rows(tm, d)],
        out_shape=[jax.ShapeDtypeStruct((t, d), F32), jax.ShapeDtypeStruct((t, d), BF16)], compiler_params=_params(1),
    )(x, o, w)


def _norm_bwd(dn, h, w, dres, name, tm=512):
    t, d = h.shape

    def body(dn_ref, h_ref, w_ref, dres_ref, dh_ref, dhb_ref, dw_ref):
        @pl.when(pl.program_id(0) == 0)
        def _():
            dw_ref[...] = jnp.zeros_like(dw_ref)

        _, vjp = jax.vjp(_rms, h_ref[...], w_ref[...])
        dh, dw = vjp(dn_ref[...])
        dh = dh + dres_ref[...]
        dh_ref[...] = dh
        dhb_ref[...] = dh.astype(BF16)
        dw_ref[...] += dw

    return pl.pallas_call(
        body, name=name, grid=(t // tm,), in_specs=[_rows(tm, d), _rows(tm, d), _full((1, d)), _rows(tm, d)],
        out_specs=[_rows(tm, d), _rows(tm, d), _full((1, d))],
        out_shape=[jax.ShapeDtypeStruct((t, d), F32), jax.ShapeDtypeStruct((t, d), BF16),
                   jax.ShapeDtypeStruct((1, d), F32)], compiler_params=_params(1),
    )(dn, h, w, dres)


def _conv_f_fwd(up, cw, cb, tm=128):
    t, c = up.shape

    def body(x_ref, h_ref, w_ref, b_ref, o_ref):
        halo = jnp.where(pl.program_id(0) > 0, h_ref[...], 0.0)
        y, _ = _causal_conv(x_ref[...], halo, w_ref[...], b_ref[...])
        o_ref[...] = (_silu(y[:, :D_FF]) * y[:, D_FF:]).astype(BF16)

    return pl.pallas_call(
        body, name="conv_f_fwd", grid=(t // tm,),
        in_specs=[_rows(tm, c), _halo(tm, c), _full(cw.shape), _full((1, c))], out_specs=_rows(tm, D_FF),
        out_shape=jax.ShapeDtypeStruct((t, D_FF), BF16), compiler_params=_params(1),
    )(up, up, cw, cb)


def _conv_f_bwd(dact, up, cw, cb, tm=128):
    t, c = up.shape
    nt = t // tm

    def body(d_ref, x_ref, h_ref, w_ref, b_ref, dx_ref, dw_ref, db_ref, nxt_scr):
        @pl.when(pl.program_id(0) == 0)
        def _():
            nxt_scr[...] = jnp.zeros_like(nxt_scr)
            dw_ref[...] = jnp.zeros_like(dw_ref)
            db_ref[...] = jnp.zeros_like(db_ref)

        halo = jnp.where(pl.program_id(0) < nt - 1, h_ref[...], 0.0)
        w = w_ref[...]
        y, shifted = _causal_conv(x_ref[...], halo, w, b_ref[...])
        a, v = y[:, :D_FF], y[:, D_FF:]
        d = d_ref[...]
        dy = jnp.concatenate([d * v * _dsilu(a), d * _silu(a)], axis=1)
        dx_ref[...] = _anticausal_conv(dy, nxt_scr[...], w).astype(BF16)
        nxt_scr[...] = dy[:8]
        dw_ref[...] += _conv_wgrad(dy, shifted)
        db_ref[...] += _colsum(dy)

    return pl.pallas_call(
        body, name="conv_f_bwd", grid=(nt,),
        in_specs=[_rows(tm, D_FF, nt, True), _rows(tm, c, nt, True), _halo(tm, c, nt, True), _full(cw.shape),
                  _full((1, c))],
        out_specs=[_rows(tm, c, nt, True), _full(cw.shape), _full((1, c))],
        out_shape=[jax.ShapeDtypeStruct((t, c), BF16), jax.ShapeDtypeStruct(cw.shape, F32),
                   jax.ShapeDtypeStruct((1, c), F32)],
        scratch_shapes=[pltpu.VMEM((8, c), F32)], compiler_params=_params(1),
    )(dact, up, up, cw, cb)


def _conv_a_bwd(dxs, db, dc, xbc, cw, cb, tm=256):
    t, c = xbc.shape
    nt = t // tm

    def body(dxs_ref, db_ref, dc_ref, x_ref, h_ref, w_ref, b_ref, dx_ref, dw_ref, dbias_ref, nxt_scr):
        @pl.when(pl.program_id(0) == 0)
        def _():
            nxt_scr[...] = jnp.zeros_like(nxt_scr)
            dw_ref[...] = jnp.zeros_like(dw_ref)
            dbias_ref[...] = jnp.zeros_like(dbias_ref)

        halo = jnp.where(pl.program_id(0) < nt - 1, h_ref[...], 0.0)
        w = w_ref[...]
        y, shifted = _causal_conv(x_ref[...], halo, w, b_ref[...])
        dy = jnp.concatenate([dxs_ref[...], db_ref[...], dc_ref[...]], axis=1) * _dsilu(y)
        dx_ref[...] = _anticausal_conv(dy, nxt_scr[...], w).astype(BF16)
        nxt_scr[...] = dy[:8]
        dw_ref[...] += _conv_wgrad(dy, shifted)
        dbias_ref[...] += _colsum(dy)

    return pl.pallas_call(
        body, name="conv_a_bwd", grid=(nt,),
        in_specs=[_rows(tm, SSD_INNER, nt, True), _rows(tm, SSD_BC, nt, True), _rows(tm, SSD_BC, nt, True),
                  _rows(tm, c, nt, True), _halo(tm, c, nt, True), _full(cw.shape), _full((1, c))],
        out_specs=[_rows(tm, c, nt, True), _full(cw.shape), _full((1, c))],
        out_shape=[jax.ShapeDtypeStruct((t, c), BF16), jax.ShapeDtypeStruct(cw.shape, F32),
                   jax.ShapeDtypeStruct((1, c), F32)],
        scratch_shapes=[pltpu.VMEM((8, c), F32)], compiler_params=_params(1),
    )(dxs, db, dc, xbc, xbc, cw, cb)


def _loss_head(h1, dn, w, target, tm=512):
    t, d = h1.shape

    def body(h_ref, dn_ref, w_ref, t_ref, loss_ref, dh_ref, dhb_ref, dw_ref):
        @pl.when(pl.program_id(0) == 0)
        def _():
            loss_ref[...] = jnp.zeros_like(loss_ref)
            dw_ref[...] = jnp.zeros_like(dw_ref)

        yf, vjp = jax.vjp(_rms, h_ref[...] + dn_ref[...], w_ref[...])
        err = yf - t_ref[...]
        loss_ref[...] += 0.5 * jnp.sum(jnp.mean(err * err, axis=-1, keepdims=True))
        dh, dw = vjp(err * (1.0 / d))
        dh_ref[...] = dh
        dhb_ref[...] = dh.astype(BF16)
        dw_ref[...] += dw

    return pl.pallas_call(
        body, name="loss_head", grid=(t // tm,),
        in_specs=[_rows(tm, d), _rows(tm, d), _full((1, d)), _rows(tm, d)],
        out_specs=[_full((8, LANES)), _rows(tm, d), _rows(tm, d), _full((1, d))],
        out_shape=[jax.ShapeDtypeStruct((8, LANES), F32), jax.ShapeDtypeStruct((t, d), F32),
                   jax.ShapeDtypeStruct((t, d), BF16), jax.ShapeDtypeStruct((1, d), F32)], compiler_params=_params(1),
    )(h1, dn, w, target)


def _pad_lanes(v, n=DT_PAD):
    return jnp.pad(v, ((0, 0), (0, n - v.shape[1])))


def _local_step(x, target, w, p):
    dtb, alog, dsk = _pad_lanes(p["dt_bias"]), _pad_lanes(p["a_log"]), _pad_lanes(p["d_skip"])
    bs_t = _pad_lanes(p["b_spatial"].T)
    e_heads = (jnp.arange(SSD_INNER)[:, None] // SSD_HEAD_DIM == jnp.arange(LANES)[None, :]).astype(BF16)
    e_groups = (jnp.arange(SGU_WIDTH)[:, None] // LANES == jnp.arange(LANES)[None, :]).astype(BF16)

    n1 = _norm_fwd(x, p["norm1_w"], "norm1_fwd")
    z = _mm(n1, w["z"], "nn", "proj_z")
    xbc = _mm(n1, w["xbc"], "nn", "proj_xbc")
    dtr = _mm(n1, w["dt"], "nn", "proj_dt")
    uv = _mm(n1, w["uv"], "nn", "proj_uv")
    gates = _mm(n1, w["gates"], "nn", "proj_gates")
    xc = _conv_a_fwd(xbc, w["conv_a"], p["conv_a_b"])
    y, ya, sprev = _ssd_fwd(xc, dtr, z, dtb, alog, dsk, p["ssd_norm_w"])
    yb = _sgu_fwd(uv, p["uv_b"], p["v_ln_w"], p["v_ln_b"], p["w_spatial"], bs_t)
    pa = _mm(ya, w["branch_a"], "nn", "branch_a")
    pb = _mm(yb, w["branch_b"], "nn", "branch_b")
    mix = _merge_fwd(gates, pa, pb, p["b_gate"])
    o = _mm(mix, w["out"], "nn", "out_proj")
    h1, n2 = _residual_norm_fwd(x, o, p["norm2_w"])
    up = _mm(n2, w["up"], "nn", "up_proj")
    act = _conv_f_fwd(up, w["conv_f"], p["conv_f_b"])
    dn = _mm(act, w["down"], "nn", "down_proj")
    loss, dh2, dh2b, g_final = _loss_head(h1, dn, p["final_norm_w"], target)

    g = {"final_norm_w": g_final}
    g["down"] = _wgrad(act, dh2b, "down_wgrad")
    dact = _mm(dh2b, w["down"], "nt", "down_dgrad")
    dup, g["conv_f"], g["conv_f_b"] = _conv_f_bwd(dact, up, w["conv_f"], p["conv_f_b"])
    g["up"] = _wgrad(n2, dup, "up_wgrad")
    dn2 = _mm(dup, w["up"], "nt", "up_dgrad")
    dh1, dh1b, g["norm2_w"] = _norm_bwd(dn2, h1, p["norm2_w"], dh2, "norm2_bwd")
    g["out"] = _wgrad(mix, dh1b, "out_wgrad")
    dmix = _mm(dh1b, w["out"], "nt", "out_dgrad")
    dgates, dpa, dpb, g["b_gate"] = _merge_bwd(dmix, gates, pa, pb, p["b_gate"])
    g["branch_a"] = _wgrad(ya, dpa, "branch_a_wgrad")
    g["branch_b"] = _wgrad(yb, dpb, "branch_b_wgrad")
    dya = _mm(dpa, w["branch_a"], "nt", "branch_a_dgrad")
    dyb = _mm(dpb, w["branch_b"], "nt", "branch_b_dgrad")
    duv, g["uv_b"], g["v_ln_w"], g["v_ln_b"], g["w_spatial"], dbs_t = _sgu_bwd(
        dyb, uv, p["uv_b"], p["v_ln_w"], p["v_ln_b"], p["w_spatial"], bs_t, e_groups)
    g["b_spatial"] = dbs_t[:, :SGU_GROUPS].T
    dz, dxs, db, dc, ddtr, g["ssd_norm_w"], ddtb, dalog, ddsk = _ssd_bwd(
        dya, y, z, xc, dtr, sprev, dtb, alog, dsk, p["ssd_norm_w"], e_heads)
    g["dt_bias"], g["a_log"], g["d_skip"] = ddtb[:, :SSD_HEADS], dalog[:, :SSD_HEADS], ddsk[:, :SSD_HEADS]
    dxbc, g["conv_a"], g["conv_a_b"] = _conv_a_bwd(dxs, db, dc, xbc, w["conv_a"], p["conv_a_b"])
    ddtrb = ddtr.astype(BF16)
    for name, d in (("z", dz), ("xbc", dxbc), ("dt", ddtrb), ("uv", duv), ("gates", dgates)):
        g[name] = _wgrad(n1, d, name + "_wgrad")
    dn1 = _mm(dz, w["z"], "nt", "z_dgrad")
    dn1 = _mm(dxbc, w["xbc"], "nt", "xbc_dgrad", acc=dn1)
    dn1 = _mm(ddtrb, w["dt"], "nt", "dt_dgrad", acc=dn1)
    dn1 = _mm(duv, w["uv"], "nt", "uv_dgrad", acc=dn1)
    dn1 = _mm(dgates, w["gates"], "nt", "gates_dgrad", acc=dn1)
    gx, _, g["norm1_w"] = _norm_bwd(dn1, x, p["norm1_w"], dh1, "norm1_bwd")
    return loss, gx, g


def _place():
    return lax.axis_index("x"), lax.axis_index("y"), lax.axis_index("c")


def _other_chips(x, y):
    return [(1 - x, y), (x, 1 - y), (1 - x, 1 - y)]


def _all_gather(shards, name):
    n = len(shards)

    def body(*refs):
        ins, outs = refs[:n], refs[n:2 * n]
        send_sems, recv_sems, local_sems = refs[2 * n:]
        x, y, c = _place()
        me, sibling = (x, y, c), (x, y, 1 - c)
        chips = _other_chips(x, y)

        def copy(a, k, block, to, src=None):
            slot = outs[a].at[4 * block[0] + 2 * block[1] + block[2]]
            return pltpu.make_async_remote_copy(
                src_ref=slot if src is None else src, dst_ref=slot, send_sem=send_sems.at[7 * a + k],
                recv_sem=recv_sems.at[7 * a + k], device_id=to, device_id_type=MESH)

        started = []
        for a in range(n):
            mine = pltpu.make_async_copy(ins[a], outs[a].at[4 * x + 2 * y + c], local_sems.at[a])
            mine.start()
            started.append(mine)
        sends = []
        for a in range(n):
            sends.append(copy(a, 0, me, sibling, src=ins[a]))
            sends += [copy(a, 1 + j, me, (*chip, c), src=ins[a]) for j, chip in enumerate(chips)]
        for cp in sends:
            cp.start()
        for a in range(n):
            for j, chip in enumerate(chips):
                copy(a, 1 + j, (*chip, c), me).wait_recv()
                fwd = copy(a, 4 + j, (*chip, c), sibling)
                fwd.start()
                sends.append(fwd)
        for a in range(n):
            copy(a, 0, sibling, me).wait_recv()
            for j, chip in enumerate(chips):
                copy(a, 4 + j, (*chip, 1 - c), me).wait_recv()
        for cp in sends:
            cp.wait_send()
        for mine in started:
            mine.wait()

    any_spec = pl.BlockSpec(memory_space=pl.ANY)
    return pl.pallas_call(
        body, name=name, in_specs=[any_spec] * n, out_specs=[any_spec] * n,
        out_shape=[jax.ShapeDtypeStruct((N_DEV, *s.shape), s.dtype) for s in shards],
        scratch_shapes=[pltpu.SemaphoreType.DMA((7 * n,)), pltpu.SemaphoreType.DMA((7 * n,)),
                        pltpu.SemaphoreType.DMA((n,))],
    )(*shards)


def _exchange_cores(parts, name):
    n = len(parts)

    def body(*refs):
        ins, outs = refs[:n], refs[n:2 * n]
        send_sems, recv_sems = refs[2 * n:]
        x, y, c = _place()
        copies = []
        for a in range(n):
            for k in range(4):
                copies.append(pltpu.make_async_remote_copy(
                    src_ref=ins[a].at[2 * k + (1 - c)], dst_ref=outs[a].at[k], send_sem=send_sems.at[4 * a + k],
                    recv_sem=recv_sems.at[4 * a + k], device_id=(x, y, 1 - c), device_id_type=MESH))
        for cp in copies:
            cp.start()
        for cp in copies:
            cp.wait()

    any_spec = pl.BlockSpec(memory_space=pl.ANY)
    return pl.pallas_call(
        body, name=name, in_specs=[any_spec] * n, out_specs=[any_spec] * n,
        out_shape=[jax.ShapeDtypeStruct((4, *s.shape[1:]), s.dtype) for s in parts],
        scratch_shapes=[pltpu.SemaphoreType.DMA((4 * n,)), pltpu.SemaphoreType.DMA((4 * n,))],
    )(*parts)


def _exchange_chips(parts, name):
    n = len(parts)

    def body(*refs):
        ins, outs = refs[:n], refs[n:2 * n]
        send_sems, recv_sems = refs[2 * n:]
        x, y, c = _place()
        copies = []
        for a in range(n):
            for j, (cx, cy) in enumerate(_other_chips(x, y)):
                copies.append(pltpu.make_async_remote_copy(
                    src_ref=ins[a].at[2 * cx + cy], dst_ref=outs[a].at[j], send_sem=send_sems.at[3 * a + j],
                    recv_sem=recv_sems.at[3 * a + j], device_id=(cx, cy, c), device_id_type=MESH))
        for cp in copies:
            cp.start()
        for cp in copies:
            cp.wait()

    any_spec = pl.BlockSpec(memory_space=pl.ANY)
    return pl.pallas_call(
        body, name=name, in_specs=[any_spec] * n, out_specs=[any_spec] * n,
        out_shape=[jax.ShapeDtypeStruct((3, *s.shape[1:]), s.dtype) for s in parts],
        scratch_shapes=[pltpu.SemaphoreType.DMA((3 * n,)), pltpu.SemaphoreType.DMA((3 * n,))],
    )(*parts)


def _chip_sum(part, got, place, name, tr=256):
    _, r, c = part.shape
    tr = _row_tile(r, tr)

    def body(place_ref, p_ref, g_ref, q_ref, own_ref):
        s = p_ref[0].astype(F32) + g_ref[0].astype(F32)
        q_ref[0] = s.astype(BF16)

        @pl.when(pl.program_id(1) == place_ref[1])
        def _():
            own_ref[...] = s

    grid_spec = pltpu.PrefetchScalarGridSpec(
        num_scalar_prefetch=1, grid=(r // tr, 4),
        in_specs=[pl.BlockSpec((1, tr, c), lambda i, k, pr: (2 * k + pr[0], i, 0)),
                  pl.BlockSpec((1, tr, c), lambda i, k, pr: (k, i, 0))],
        out_specs=[pl.BlockSpec((1, tr, c), lambda i, k, pr: (k, i, 0)),
                   pl.BlockSpec((tr, c), lambda i, k, pr: (i, 0))])
    return pl.pallas_call(
        body, name=name, grid_spec=grid_spec,
        out_shape=[jax.ShapeDtypeStruct((4, r, c), BF16), jax.ShapeDtypeStruct((r, c), F32)],
        compiler_params=_params(2),
    )(place, part, got)


def _adamw(w, g, m, v):
    m = ADAM_B1 * m + (1.0 - ADAM_B1) * g
    v = ADAM_B2 * v + (1.0 - ADAM_B2) * jnp.square(g)
    m_hat = m / (1.0 - ADAM_B1 ** ADAM_STEP)
    v_hat = v / (1.0 - ADAM_B2 ** ADAM_STEP)
    return -ADAM_LR * (m_hat / (jnp.sqrt(v_hat) + ADAM_EPS) + ADAM_WD * w), m, v


def _sum_adamw(own, got, w, m, v, name, tr=256):
    r, c = w.shape
    tr = _row_tile(r, tr)

    def body(own_ref, got_ref, w_ref, m_ref, v_ref, g_ref, d_ref, nm_ref, nv_ref):
        g = own_ref[...]
        for j in range(3):
            g = g + got_ref[j].astype(F32)
        g_ref[...] = g
        d_ref[...], nm_ref[...], nv_ref[...] = _adamw(w_ref[...], g, m_ref[...], v_ref[...])

    blk = pl.BlockSpec((tr, c), lambda i: (i, 0))
    return pl.pallas_call(
        body, name=name, grid=(r // tr,),
        in_specs=[blk, pl.BlockSpec((3, tr, c), lambda i: (0, i, 0)), blk, blk, blk], out_specs=[blk] * 4,
        out_shape=[jax.ShapeDtypeStruct((r, c), F32)] * 4, compiler_params=_params(1),
    )(own, got, w, m, v)


def _sum_devices(parts, name):
    _, r, c = parts.shape
    tr = r

    def body(p_ref, o_ref):
        s = p_ref[0]
        for d in range(1, N_DEV):
            s = s + p_ref[d]
        o_ref[...] = s

    return pl.pallas_call(
        body, name=name, grid=(pl.cdiv(r, tr),), in_specs=[pl.BlockSpec((N_DEV, tr, c), lambda i: (0, i, 0))],
        out_specs=pl.BlockSpec((tr, c), lambda i: (i, 0)), out_shape=jax.ShapeDtypeStruct((r, c), F32),
        compiler_params=_params(1),
    )(parts)


def _adamw_call(w, g, m, v, name):
    r, c = w.shape
    tr = r

    def body(w_ref, g_ref, m_ref, v_ref, d_ref, nm_ref, nv_ref):
        d_ref[...], nm_ref[...], nv_ref[...] = _adamw(w_ref[...], g_ref[...], m_ref[...], v_ref[...])

    blk = pl.BlockSpec((tr, c), lambda i: (i, 0))
    return pl.pallas_call(
        body, name=name, grid=(pl.cdiv(r, tr),), in_specs=[blk] * 4, out_specs=[blk] * 3,
        out_shape=[jax.ShapeDtypeStruct((r, c), F32)] * 3, compiler_params=_params(1),
    )(w, g, m, v)


PACK_ROWS = 8


def _pack(arrays):
    parts = []
    for a in arrays:
        flat = a.reshape(-1)
        unit = PACK_ROWS * LANES
        parts.append(jnp.pad(flat, (0, -flat.shape[0] % unit)).reshape(-1, LANES))
    return jnp.concatenate(parts, axis=0)


def _unpack(pack, shapes):
    out, row = [], 0
    for s in shapes:
        size = 1
        for d in s:
            size *= d
        rows = -(-size // (PACK_ROWS * LANES)) * PACK_ROWS
        out.append(pack[row:row + rows].reshape(-1)[:size].reshape(s))
        row += rows
    return out


SMALL = ["norm1_w", "b_gate", "conv_a_b", "dt_bias", "a_log", "d_skip", "ssd_norm_w", "uv_b", "v_ln_w", "v_ln_b",
         "w_spatial", "b_spatial", "norm2_w", "conv_f_b", "final_norm_w"]
BIG = ["w_in", "w_branch", "w_out", "w_up", "w_down"]
WEIGHTS = ["norm1_w", "w_in", "b_gate", "conv_a_w", "conv_a_b", "dt_bias", "a_log", "d_skip", "ssd_norm_w", "uv_b",
           "v_ln_w", "v_ln_b", "w_spatial", "b_spatial", "w_branch", "w_out", "norm2_w", "w_up", "conv_f_w",
           "conv_f_b", "w_down", "final_norm_w"]
IN_SPLITS = [("z", 0, 2048), ("xbc", 2048, 5120), ("dt", 5120, 5152), ("uv", 5152, 7200), ("gates", 7200, 9248)]


def _columns_by_device(a):
    r = a.shape[0]
    return a.reshape(r, N_DEV, -1).transpose(1, 0, 2)


def _columns_from_devices(a):
    return a.transpose(1, 0, 2).reshape(a.shape[1], -1)


def kernel(x, norm1_w, w_in, b_gate, conv_a_w, conv_a_b, dt_bias, a_log, d_skip, ssd_norm_w, uv_b, v_ln_w, v_ln_b, w_spatial, b_spatial, w_branch, w_out, norm2_w, w_up, conv_f_w, conv_f_b, w_down, final_norm_w, loss_target, m_norm1_w, m_w_in, m_b_gate, m_conv_a_w, m_conv_a_b, m_dt_bias, m_a_log, m_d_skip, m_ssd_norm_w, m_uv_b, m_v_ln_w, m_v_ln_b, m_w_spatial, m_b_spatial, m_w_branch, m_w_out, m_norm2_w, m_w_up, m_conv_f_w, m_conv_f_b, m_w_down, m_final_norm_w, v_norm1_w, v_w_in, v_b_gate, v_conv_a_w, v_conv_a_b, v_dt_bias, v_a_log, v_d_skip, v_ssd_norm_w, v_uv_b, v_v_ln_w, v_v_ln_b, v_w_spatial, v_b_spatial, v_w_branch, v_w_out, v_norm2_w, v_w_up, v_conv_f_w, v_conv_f_b, v_w_down, v_final_norm_w):
    args = dict(locals())
    wts = {n: args[n] for n in WEIGHTS}
    mom = {n: args["m_" + n] for n in WEIGHTS}
    var = {n: args["v_" + n] for n in WEIGHTS}
    cx, cy, cc = _place()
    dev = 4 * cx + 2 * cy + cc
    place = jnp.stack([cc, 2 * cx + cy]).astype(jnp.int32)

    shards = [wts[n][0].astype(BF16) for n in BIG] + [conv_a_w[0], conv_f_w[0]]
    g_in, g_branch, g_out, g_up, g_down, g_conv_a, g_conv_f = _all_gather(shards, "gather_weights")
    w_in_full = _columns_from_devices(g_in)
    w = {name: w_in_full[:, lo:hi] for name, lo, hi in IN_SPLITS}
    w["dt"] = _pad_lanes(w["dt"])
    branch = g_branch.reshape(-1, D_MODEL)
    w["branch_a"], w["branch_b"] = branch[:SSD_INNER], branch[SSD_INNER:]
    w["out"] = g_out.reshape(-1, D_MODEL)
    w["up"] = _columns_from_devices(g_up)
    w["down"] = g_down.reshape(-1, D_MODEL)
    w["conv_a"] = _columns_from_devices(g_conv_a)
    w["conv_f"] = _columns_from_devices(g_conv_f)

    p = {n: wts[n][0] if wts[n].ndim > 2 else wts[n].reshape(1, -1) for n in SMALL}
    loss, gx, g = _local_step(x[0], loss_target[0], w, p)
    loss = lax.psum(loss[0, 0], ("x", "y", "c"))

    small_g = [g[n] for n in SMALL] + [g["conv_a"], g["conv_f"]]
    gathered, = _all_gather([_pack(small_g)], "gather_small_grads")
    small_sum = _unpack(_sum_devices(gathered, "sum_small_grads"), [a.shape for a in small_g])
    grads = {n: s.reshape(wts[n].shape) for n, s in zip(SMALL, small_sum[:len(SMALL)])}
    for n, s in (("conv_a_w", small_sum[-2]), ("conv_f_w", small_sum[-1])):
        cols = wts[n].shape[2]
        grads[n] = lax.dynamic_slice_in_dim(s, dev * cols, cols, axis=1)[None]

    g_in_full = jnp.concatenate([g[name][:, :hi - lo] for name, lo, hi in IN_SPLITS], axis=1)
    parts = [_columns_by_device(g_in_full), jnp.concatenate([g["branch_a"], g["branch_b"]], axis=0),
             g["out"], _columns_by_device(g["up"]), g["down"]]
    parts = [a.astype(BF16).reshape(N_DEV, *wts[n].shape[1:]) for n, a in zip(BIG, parts)]
    from_core = _exchange_cores(parts, "grads_to_other_core")
    sums = [_chip_sum(a, b, place, f"chip_sum_{n}") for n, a, b in zip(BIG, parts, from_core)]
    from_chips = _exchange_chips([s[0] for s in sums], "grads_to_other_chips")

    delta, new_m, new_v = {}, {}, {}
    for n, s, got in zip(BIG, sums, from_chips):
        gr, d, nm, nv = _sum_adamw(s[1], got, wts[n][0], mom[n][0], var[n][0], f"adamw_{n}")
        grads[n], delta[n], new_m[n], new_v[n] = gr[None], d[None], nm[None], nv[None]
    small_names = SMALL + ["conv_a_w", "conv_f_w"]
    packs = [_pack([t[n] for n in small_names]) for t in (wts, grads, mom, var)]
    outs = _adamw_call(*packs, "adamw_small")
    shapes = [wts[n].shape for n in small_names]
    for tgt, pack in zip((delta, new_m, new_v), outs):
        tgt.update(dict(zip(small_names, _unpack(pack, shapes))))

    return (loss, gx[None], *[grads[n] for n in WEIGHTS], *[delta[n] for n in WEIGHTS],
            *[new_m[n] for n in WEIGHTS], *[new_v[n] for n in WEIGHTS])
```

```python
import functools

import jax
import jax.numpy as jnp
from jax import lax
from jax.experimental import pallas as pl
from jax.experimental.pallas import tpu as pltpu

F32, BF16 = jnp.float32, jnp.bfloat16
HIGHEST = lax.Precision.HIGHEST

D_MODEL = 1024
SSD_INNER = 2048
SSD_HEAD_DIM = 64
SSD_HEADS = 32
SSD_GROUPS = 4
SSD_STATE = 128
SSD_BC = SSD_GROUPS * SSD_STATE
SSD_XBC = SSD_INNER + 2 * SSD_BC
SSD_CONV = 4
CHUNK = 128
N_PAIRS = SSD_HEADS // 2
PAIRS_PER_GROUP = N_PAIRS // SSD_GROUPS
SGU_WIDTH = 1024
SGU_GROUPS = 8
D_FF = 2816
FFN_CONV = 3
NORM_EPS = 1e-6
LN_EPS = 1e-5
LANES = 128
DT_PAD = LANES

ADAM_LR, ADAM_B1, ADAM_B2, ADAM_EPS, ADAM_WD, ADAM_STEP = 0.001, 0.9, 0.999, 1e-08, 0.01, 10

N_DEV = 8
VMEM_LIMIT = 56 * 1024 * 1024
MESH = pl.DeviceIdType.MESH


def _params(n_grid, **kw):
    sem = dict(dimension_semantics=("arbitrary",) * n_grid) if n_grid else {}
    return pltpu.CompilerParams(vmem_limit_bytes=VMEM_LIMIT, **sem, **kw)


def _tile(n, pref):
    t = (min(pref, n) // LANES) * LANES
    while n % t:
        t -= LANES
    return t


def _row_tile(r, pref):
    for t in range(min(pref, r) // 16 * 16, 0, -16):
        if r % t == 0:
            return t
    return r


def _rows(tm, n, nt=None, rev=False, col=0):
    if rev:
        return pl.BlockSpec((tm, n), lambda i: (nt - 1 - i, col))
    return pl.BlockSpec((tm, n), lambda i: (i, col))


def _halo(tm, n, nt=None, rev=False):
    per = tm // 8
    if rev:
        return pl.BlockSpec((8, n), lambda i: (jnp.maximum((nt - 1 - i) * per - 1, 0), 0))
    return pl.BlockSpec((8, n), lambda i: (jnp.maximum(i * per - 1, 0), 0))


def _full(shape):
    nd = len(shape)
    return pl.BlockSpec(shape, lambda *_: (0,) * nd)


def _rms(x, w, eps=NORM_EPS):
    return x * lax.rsqrt(jnp.mean(x * x, axis=-1, keepdims=True) + eps) * w


def _layer_norm(x, w, b):
    mu = jnp.mean(x, axis=-1, keepdims=True)
    var = jnp.mean(jnp.square(x - mu), axis=-1, keepdims=True)
    return (x - mu) * lax.rsqrt(var + LN_EPS) * w + b


def _sigmoid(x):
    return 1.0 / (1.0 + jnp.exp(-x))


def _silu(x):
    return x * _sigmoid(x)


def _dsilu(x):
    s = _sigmoid(x)
    return s * (1.0 + x * (1.0 - s))


def _softplus(x):
    return jnp.maximum(x, 0.0) + jnp.log(1.0 + jnp.exp(-jnp.abs(x)))


def _gelu(x):
    return jax.nn.gelu(x)


def _dot(a, b):
    return jnp.dot(a, b, preferred_element_type=F32)


def _dot_nt(a, b):
    return lax.dot_general(a, b, (((1,), (1,)), ((), ())), preferred_element_type=F32)


def _dot_tn(a, b):
    return lax.dot_general(a, b, (((0,), (0,)), ((), ())), preferred_element_type=F32)


def _dot_split(p, e):
    hi = p.astype(BF16)
    lo = (p - hi.astype(F32)).astype(BF16)
    return _dot(hi, e) + _dot(lo, e)


def _colsum(x):
    return jnp.sum(x, axis=0, keepdims=True)


def _shift_down(x, halo, j):
    xs = pltpu.roll(x, j, 0)
    hs = pltpu.roll(halo, j, 0)
    r8 = lax.broadcasted_iota(jnp.int32, hs.shape, 0)
    return jnp.concatenate([jnp.where(r8 < j, hs, xs[:8]), xs[8:]], axis=0)


def _shift_up(x, nxt, j):
    n = x.shape[0]
    xs = pltpu.roll(x, n - j, 0)
    ns = pltpu.roll(nxt, 8 - j, 0)
    r8 = lax.broadcasted_iota(jnp.int32, ns.shape, 0)
    return jnp.concatenate([xs[:n - 8], jnp.where(r8 >= 8 - j, ns, xs[n - 8:])], axis=0)


def _causal_conv(x, halo, w, b):
    k = w.shape[0]
    shifted = [x] + [_shift_down(x, halo, j) for j in range(1, k)]
    y = b + w[k - 1:k, :] * x
    for j in range(1, k):
        y = y + w[k - 1 - j:k - j, :] * shifted[j]
    return y, shifted


def _anticausal_conv(dy, nxt, w):
    k = w.shape[0]
    dx = w[k - 1:k, :] * dy
    for j in range(1, k):
        dx = dx + w[k - 1 - j:k - j, :] * _shift_up(dy, nxt, j)
    return dx


def _conv_wgrad(dy, shifted):
    k = len(shifted)
    return jnp.concatenate([_colsum(dy * shifted[k - 1 - i]) for i in range(k)], axis=0)


MM_TILE_PREF = 1408
MM_VMEM_BUDGET = 40 * 1024 * 1024


def _mm_tiles(m, n, k, out_bytes):
    tm, tn = _tile(m, MM_TILE_PREF), _tile(n, MM_TILE_PREF)
    need = lambda tm, tn: 2 * (2 * k * (tm + tn) + out_bytes * tm * tn)
    while need(tm, tn) > MM_VMEM_BUDGET:
        if tn >= tm and tn > LANES:
            tn = _tile(n, tn - LANES)
        else:
            tm = _tile(m, tm - LANES)
    return tm, tn


def _mm(a, b, dims, name, acc=None, out_dtype=F32):
    if dims == "tn":
        k, m = a.shape
    else:
        m, k = a.shape
    n = b.shape[0] if dims == "nt" else b.shape[1]
    tm, tn = _mm_tiles(m, n, k, 4 * (2 if acc is not None else 1))
    a_spec = pl.BlockSpec((k, tm), lambda j, i: (0, i)) if dims == "tn" else pl.BlockSpec((tm, k), lambda j, i: (i, 0))
    b_spec = pl.BlockSpec((tn, k), lambda j, i: (j, 0)) if dims == "nt" else pl.BlockSpec((k, tn), lambda j, i: (0, j))
    o_spec = pl.BlockSpec((tm, tn), lambda j, i: (i, j))
    dot = {"nn": _dot, "nt": _dot_nt, "tn": _dot_tn}[dims]

    def body(a_ref, b_ref, *rest):
        r = dot(a_ref[...], b_ref[...])
        if acc is not None:
            r = r + rest[0][...]
        rest[-1][...] = r.astype(out_dtype)

    ins, specs = [a, b], [a_spec, b_spec]
    if acc is not None:
        ins.append(acc)
        specs.append(o_spec)
    return pl.pallas_call(
        body, name=name, grid=(n // tn, m // tm), in_specs=specs, out_specs=o_spec,
        out_shape=jax.ShapeDtypeStruct((m, n), out_dtype), compiler_params=_params(2),
    )(*ins)


def _wgrad(a, d, name):
    return _mm(a, d, "tn", name, out_dtype=BF16)


def _norm_fwd(x, w, name, after=None, tm=512):
    t, d = x.shape

    def body(x_ref, w_ref, *rest):
        rest[-1][...] = _rms(x_ref[...], w_ref[...]).astype(BF16)

    extra, extra_specs = ([after], [_full(after.shape)]) if after is not None else ([], [])
    return pl.pallas_call(
        body, name=name, grid=(t // tm,), in_specs=[_rows(tm, d), _full((1, d))] + extra_specs,
        out_specs=_rows(tm, d), out_shape=jax.ShapeDtypeStruct((t, d), BF16), compiler_params=_params(1),
    )(x, w, *extra)


def _conv_a_fwd(xbc, cw, cb, tm=256):
    t, c = xbc.shape

    def body(x_ref, h_ref, w_ref, b_ref, o_ref):
        halo = jnp.where(pl.program_id(0) > 0, h_ref[...], 0.0)
        y, _ = _causal_conv(x_ref[...], halo, w_ref[...], b_ref[...])
        o_ref[...] = _silu(y)

    return pl.pallas_call(
        body, name="conv_a_fwd", grid=(t // tm,),
        in_specs=[_rows(tm, c), _halo(tm, c), _full(cw.shape), _full((1, c))], out_specs=_rows(tm, c),
        out_shape=jax.ShapeDtypeStruct((t, c), F32), compiler_params=_params(1),
    )(xbc, xbc, cw, cb)


def _ssd_common(dtr, dtb, alog):
    row = lax.broadcasted_iota(jnp.int32, (CHUNK, CHUNK), 0)
    col = lax.broadcasted_iota(jnp.int32, (CHUNK, CHUNK), 1)
    causal = row >= col
    dt = _softplus(dtr + dtb)
    a = -jnp.exp(alog)
    acum = jnp.dot(causal.astype(F32), dt * a, precision=HIGHEST, preferred_element_type=F32)
    return dt, a, acum, acum.T, causal, col < SSD_HEAD_DIM, row


def _pair_terms(j, dt, acum, acum_t, causal, lane_lo):
    h0, h1 = 2 * j, 2 * j + 1
    ac0, ac1 = acum[:, h0:h0 + 1], acum[:, h1:h1 + 1]
    l0 = jnp.exp(jnp.where(causal, ac0 - acum_t[h0:h0 + 1, :], -jnp.inf))
    l1 = jnp.exp(jnp.where(causal, ac1 - acum_t[h1:h1 + 1, :], -jnp.inf))
    dtp = jnp.where(lane_lo, dt[:, h0:h0 + 1], dt[:, h1:h1 + 1])
    al0, al1 = acum[CHUNK - 1:CHUNK, h0:h0 + 1], acum[CHUNK - 1:CHUNK, h1:h1 + 1]
    ecol = jnp.where(lane_lo, jnp.exp(ac0), jnp.exp(ac1))
    dsr = jnp.where(lane_lo, jnp.exp(al0 - ac0), jnp.exp(al1 - ac1))
    elast = jnp.where(lane_lo[0:1], jnp.exp(al0), jnp.exp(al1))
    return l0, l1, dtp, ecol, dsr, elast


def _ssd_fwd(xc, dtr, z, dtb, alog, dsk, nw):
    t = xc.shape[0]
    nc = t // CHUNK

    def body(xs_ref, b_ref, c_ref, dtr_ref, z_ref, dtb_ref, alog_ref, dsk_ref, nw_ref, y_ref, ya_ref, sp_ref, s_scr):
        @pl.when(pl.program_id(0) == 0)
        def _():
            s_scr[...] = jnp.zeros_like(s_scr)

        dt, a, acum, acum_t, causal, lane_lo, _ = _ssd_common(dtr_ref[...], dtb_ref[...], alog_ref[...])
        dsk = dsk_ref[...]
        for g in range(SSD_GROUPS):
            gs = slice(g * SSD_STATE, (g + 1) * SSD_STATE)
            bg, cg = b_ref[:, gs].astype(BF16), c_ref[:, gs].astype(BF16)
            cb = _dot_nt(cg, bg)
            for pp in range(PAIRS_PER_GROUP):
                j = g * PAIRS_PER_GROUP + pp
                ps = slice(j * LANES, (j + 1) * LANES)
                x = xs_ref[:, ps]
                l0, l1, dtp, ecol, dsr, elast = _pair_terms(j, dt, acum, acum_t, causal, lane_lo)
                xdt = x * dtp
                xb = xdt.astype(BF16)
                zero = jnp.zeros_like(xb)
                yd = (_dot((cb * l0).astype(BF16), jnp.where(lane_lo, xb, zero))
                      + _dot((cb * l1).astype(BF16), jnp.where(lane_lo, zero, xb)))
                sp = s_scr[j]
                yo = ecol * _dot(cg, sp.astype(BF16))
                st = _dot_tn(bg, (xdt * dsr).astype(BF16))
                sp_ref[0, j] = sp
                s_scr[j] = elast * sp + st
                dskp = jnp.where(lane_lo[0:1], dsk[:, 2 * j:2 * j + 1], dsk[:, 2 * j + 1:2 * j + 2])
                y_ref[:, ps] = yd + yo + dskp * x
        ya_ref[...] = _rms(y_ref[...] * _silu(z_ref[...]), nw_ref[...]).astype(BF16)

    ck = lambda n, col=0: pl.BlockSpec((CHUNK, n), lambda c: (c, col))
    return pl.pallas_call(
        body, name="ssd_fwd", grid=(nc,),
        in_specs=[ck(SSD_INNER), ck(SSD_BC, SSD_INNER // SSD_BC), ck(SSD_BC, SSD_INNER // SSD_BC + 1), ck(DT_PAD),
                  ck(SSD_INNER), _full((1, DT_PAD)), _full((1, DT_PAD)), _full((1, DT_PAD)), _full((1, SSD_INNER))],
        out_specs=[ck(SSD_INNER), ck(SSD_INNER),
                   pl.BlockSpec((1, N_PAIRS, SSD_STATE, LANES), lambda c: (c, 0, 0, 0))],
        out_shape=[jax.ShapeDtypeStruct((t, SSD_INNER), F32), jax.ShapeDtypeStruct((t, SSD_INNER), BF16),
                   jax.ShapeDtypeStruct((nc, N_PAIRS, SSD_STATE, LANES), F32)],
        scratch_shapes=[pltpu.VMEM((N_PAIRS, SSD_STATE, LANES), F32)], compiler_params=_params(1),
    )(xc, xc, xc, dtr, z, dtb, alog, dsk, nw)


def _ssd_bwd(dya, y, z, xc, dtr, sprev, dtb, alog, dsk, nw, e_heads):
    t = xc.shape[0]
    nc = t // CHUNK

    def body(dya_ref, y_ref, z_ref, xs_ref, b_ref, c_ref, dtr_ref, sp_ref, dtb_ref, alog_ref, dsk_ref, nw_ref, e_ref,
             dz_ref, dxs_ref, db_ref, dc_ref, ddtr_ref, dnw_ref, ddtb_ref, dalog_ref, ddsk_ref, ds_scr):
        @pl.when(pl.program_id(0) == 0)
        def _():
            ds_scr[...] = jnp.zeros_like(ds_scr)
            for r in (dnw_ref, ddtb_ref, dalog_ref, ddsk_ref):
                r[...] = jnp.zeros_like(r)

        y = y_ref[...]
        _, gate_vjp = jax.vjp(lambda y_, z_, w_: _rms(y_ * _silu(z_), w_), y, z_ref[...], nw_ref[...])
        dy, dz, dnw = gate_vjp(dya_ref[...])
        dz_ref[...] = dz.astype(BF16)
        dnw_ref[...] += dnw

        dtr = dtr_ref[...]
        dt, a, acum, acum_t, causal, lane_lo, row = _ssd_common(dtr, dtb_ref[...], alog_ref[...])
        dsk = dsk_ref[...]
        p_a, p_dt, v_last = [], [], []
        col = lax.broadcasted_iota(jnp.int32, (CHUNK, CHUNK), 1)
        da_cols = jnp.zeros((CHUNK, CHUNK), F32)
        da_rows = jnp.zeros((CHUNK, CHUNK), F32)
        for g in range(SSD_GROUPS):
            gs = slice(g * SSD_STATE, (g + 1) * SSD_STATE)
            bg, cg = b_ref[:, gs].astype(BF16), c_ref[:, gs].astype(BF16)
            cb = _dot_nt(cg, bg)
            dcb = jnp.zeros((CHUNK, CHUNK), F32)
            dbg = jnp.zeros((CHUNK, SSD_STATE), F32)
            dcg = jnp.zeros((CHUNK, SSD_STATE), F32)
            for pp in range(PAIRS_PER_GROUP):
                j = g * PAIRS_PER_GROUP + pp
                ps = slice(j * LANES, (j + 1) * LANES)
                x = xs_ref[:, ps]
                l0, l1, dtp, ecol, dsr, elast = _pair_terms(j, dt, acum, acum_t, causal, lane_lo)
                xdt = x * dtp
                xb = xdt.astype(BF16)
                dskp = jnp.where(lane_lo[0:1], dsk[:, 2 * j:2 * j + 1], dsk[:, 2 * j + 1:2 * j + 2])
                dyp = dy[:, ps]
                dyb = dyp.astype(BF16)
                sp, dsn = sp_ref[0, j], ds_scr[j]
                spb, dsnb = sp.astype(BF16), dsn.astype(BF16)
                y_off = ecol * _dot(cg, spb)
                dw = (dyp * ecol).astype(BF16)
                dcg = dcg + _dot_nt(dw, spb)
                dsp = _dot_tn(cg, dw) + elast * dsn
                xd = xdt * dsr
                zd = _dot(bg, dsnb) * dsr
                dbg = dbg + _dot_nt(xd.astype(BF16), dsnb)
                dxdt = zd
                zero = jnp.zeros_like(xb)
                for h, lm, le in ((2 * j, lane_lo, l0), (2 * j + 1, jnp.logical_not(lane_lo), l1)):
                    dm = _dot_nt(jnp.where(lm, dyb, zero), jnp.where(lm, xb, zero))
                    dcb = dcb + dm * le
                    m = cb * le
                    dxdt = dxdt + jnp.where(lm, _dot_tn(m.astype(BF16), dyb), 0.0)
                    q = dm * m
                    da_cols = da_cols + jnp.where(col == h, jnp.sum(q, axis=1, keepdims=True), 0.0)
                    da_rows = da_rows + jnp.where(row == h, _colsum(q), 0.0)
                ds_scr[j] = dsp
                dxs_ref[:, ps] = dxdt * dtp + dskp * dyp
                p_a.append(dyp * y_off - xdt * zd)
                p_dt.append(dxdt * x)
                v_last.append(_colsum(zd * xdt) + elast * _colsum(dsn * sp))
            dcbb = dcb.astype(BF16)
            db_ref[:, gs] = dbg + _dot_tn(dcbb, cg)
            dc_ref[:, gs] = dcg + _dot(dcbb, bg)
        e = e_ref[...]
        rows8 = jnp.concatenate([jnp.concatenate(v_last, axis=1), _colsum(dy * xs_ref[...]),
                                 jnp.zeros((6, SSD_INNER), F32)], axis=0)
        r8 = _dot_split(rows8, e)
        da = (_dot_split(jnp.concatenate(p_a, axis=1), e) + jnp.where(row == CHUNK - 1, r8[0:1], 0.0)
              + da_cols - da_rows.T)
        ddsk_ref[...] += r8[1:2]
        dadt = jnp.dot((row <= col).astype(F32), da, precision=HIGHEST, preferred_element_type=F32)
        ddt = dadt * a + _dot_split(jnp.concatenate(p_dt, axis=1), e)
        dalog_ref[...] += _colsum(dadt * dt) * a
        ddtr = ddt * _sigmoid(dtr + dtb_ref[...])
        ddtr_ref[...] = ddtr
        ddtb_ref[...] += _colsum(ddtr)

    ck = lambda n, col=0: pl.BlockSpec((CHUNK, n), lambda c: (nc - 1 - c, col))
    acc = lambda n: _full((1, n))
    return pl.pallas_call(
        body, name="ssd_bwd", grid=(nc,),
        in_specs=[ck(SSD_INNER), ck(SSD_INNER), ck(SSD_INNER), ck(SSD_INNER), ck(SSD_BC, SSD_INNER // SSD_BC),
                  ck(SSD_BC, SSD_INNER // SSD_BC + 1), ck(DT_PAD),
                  pl.BlockSpec((1, N_PAIRS, SSD_STATE, LANES), lambda c: (nc - 1 - c, 0, 0, 0)),
                  acc(DT_PAD), acc(DT_PAD), acc(DT_PAD), acc(SSD_INNER), _full((SSD_INNER, LANES))],
        out_specs=[ck(SSD_INNER), ck(SSD_INNER), ck(SSD_BC), ck(SSD_BC), ck(DT_PAD),
                   acc(SSD_INNER), acc(DT_PAD), acc(DT_PAD), acc(DT_PAD)],
        out_shape=[jax.ShapeDtypeStruct((t, SSD_INNER), BF16), jax.ShapeDtypeStruct((t, SSD_INNER), F32),
                   jax.ShapeDtypeStruct((t, SSD_BC), F32), jax.ShapeDtypeStruct((t, SSD_BC), F32),
                   jax.ShapeDtypeStruct((t, DT_PAD), F32), jax.ShapeDtypeStruct((1, SSD_INNER), F32),
                   jax.ShapeDtypeStruct((1, DT_PAD), F32), jax.ShapeDtypeStruct((1, DT_PAD), F32),
                   jax.ShapeDtypeStruct((1, DT_PAD), F32)],
        scratch_shapes=[pltpu.VMEM((N_PAIRS, SSD_STATE, LANES), F32)], compiler_params=_params(1),
    )(dya, y, z, xc, xc, xc, dtr, sprev, dtb, alog, dsk, nw, e_heads)


def _sgu_act(uv, uvb, lnw, lnb):
    a = _gelu(uv + uvb)
    return a[:, :SGU_WIDTH], _layer_norm(a[:, SGU_WIDTH:], lnw, lnb)


def _sgu_weights(ws_ref):
    row = lax.broadcasted_iota(jnp.int32, (CHUNK, CHUNK), 0)
    col = lax.broadcasted_iota(jnp.int32, (CHUNK, CHUNK), 1)
    return [jnp.where(row >= col, ws_ref[g], 0.0).astype(BF16) for g in range(SGU_GROUPS)], row >= col


def _sgu_fwd(uv, uvb, lnw, lnb, ws, bs_t):
    t = uv.shape[0]

    def body(uv_ref, uvb_ref, lnw_ref, lnb_ref, ws_ref, bs_ref, o_ref):
        u, vn = _sgu_act(uv_ref[...], uvb_ref[...], lnw_ref[...], lnb_ref[...])
        wc, _ = _sgu_weights(ws_ref)
        bs = bs_ref[...]
        for g in range(SGU_GROUPS):
            gs = slice(g * LANES, (g + 1) * LANES)
            mixed = _dot(wc[g], vn[:, gs].astype(BF16)) + bs[:, g:g + 1]
            o_ref[:, gs] = (u[:, gs] * mixed).astype(BF16)

    return pl.pallas_call(
        body, name="sgu_fwd", grid=(t // CHUNK,),
        in_specs=[_rows(CHUNK, 2 * SGU_WIDTH), _full((1, 2 * SGU_WIDTH)), _full((1, SGU_WIDTH)), _full((1, SGU_WIDTH)),
                  _full(ws.shape), _full(bs_t.shape)],
        out_specs=_rows(CHUNK, SGU_WIDTH), out_shape=jax.ShapeDtypeStruct((t, SGU_WIDTH), BF16),
        compiler_params=_params(1),
    )(uv, uvb, lnw, lnb, ws, bs_t)


def _sgu_bwd(dyb, uv, uvb, lnw, lnb, ws, bs_t, e_groups):
    t = uv.shape[0]

    def body(dyb_ref, uv_ref, uvb_ref, lnw_ref, lnb_ref, ws_ref, bs_ref, e_ref,
             duv_ref, duvb_ref, dlnw_ref, dlnb_ref, dws_ref, dbs_ref):
        @pl.when(pl.program_id(0) == 0)
        def _():
            for r in (duvb_ref, dlnw_ref, dlnb_ref, dws_ref, dbs_ref):
                r[...] = jnp.zeros_like(r)

        (u, vn), act_vjp = jax.vjp(_sgu_act, uv_ref[...], uvb_ref[...], lnw_ref[...], lnb_ref[...])
        wc, causal = _sgu_weights(ws_ref)
        bs = bs_ref[...]
        dyb = dyb_ref[...]
        du, dvn, dmix = [], [], []
        for g in range(SGU_GROUPS):
            gs = slice(g * LANES, (g + 1) * LANES)
            vb = vn[:, gs].astype(BF16)
            mixed = _dot(wc[g], vb) + bs[:, g:g + 1]
            dm = dyb[:, gs] * u[:, gs]
            dmb = dm.astype(BF16)
            du.append(dyb[:, gs] * mixed)
            dvn.append(_dot_tn(wc[g], dmb))
            dws_ref[g] += jnp.where(causal, _dot_nt(dmb, vb), 0.0)
            dmix.append(dm)
        dbs_ref[...] += _dot_split(jnp.concatenate(dmix, axis=1), e_ref[...])
        duv, duvb, dlnw, dlnb = act_vjp((jnp.concatenate(du, axis=1), jnp.concatenate(dvn, axis=1)))
        duv_ref[...] = duv.astype(BF16)
        duvb_ref[...] += duvb
        dlnw_ref[...] += dlnw
        dlnb_ref[...] += dlnb

    return pl.pallas_call(
        body, name="sgu_bwd", grid=(t // CHUNK,),
        in_specs=[_rows(CHUNK, SGU_WIDTH), _rows(CHUNK, 2 * SGU_WIDTH), _full((1, 2 * SGU_WIDTH)),
                  _full((1, SGU_WIDTH)), _full((1, SGU_WIDTH)), _full(ws.shape), _full(bs_t.shape),
                  _full(e_groups.shape)],
        out_specs=[_rows(CHUNK, 2 * SGU_WIDTH), _full((1, 2 * SGU_WIDTH)), _full((1, SGU_WIDTH)),
                   _full((1, SGU_WIDTH)), _full(ws.shape), _full(bs_t.shape)],
        out_shape=[jax.ShapeDtypeStruct((t, 2 * SGU_WIDTH), BF16), jax.ShapeDtypeStruct((1, 2 * SGU_WIDTH), F32),
                   jax.ShapeDtypeStruct((1, SGU_WIDTH), F32), jax.ShapeDtypeStruct((1, SGU_WIDTH), F32),
                   jax.ShapeDtypeStruct(ws.shape, F32), jax.ShapeDtypeStruct(bs_t.shape, F32)],
        compiler_params=_params(1),
    )(dyb, uv, uvb, lnw, lnb, ws, bs_t, e_groups)


def _merge(gates, pa, pb, bg):
    s = _sigmoid(gates + bg)
    return s[:, :D_MODEL] * pa + s[:, D_MODEL:] * pb


def _merge_fwd(gates, pa, pb, bg, tm=256):
    t = gates.shape[0]

    def body(g_ref, pa_ref, pb_ref, bg_ref, o_ref):
        o_ref[...] = _merge(g_ref[...], pa_ref[...], pb_ref[...], bg_ref[...]).astype(BF16)

    return pl.pallas_call(
        body, name="merge_fwd", grid=(t // tm,),
        in_specs=[_rows(tm, 2 * D_MODEL), _rows(tm, D_MODEL), _rows(tm, D_MODEL), _full((1, 2 * D_MODEL))],
        out_specs=_rows(tm, D_MODEL), out_shape=jax.ShapeDtypeStruct((t, D_MODEL), BF16), compiler_params=_params(1),
    )(gates, pa, pb, bg)


def _merge_bwd(dmix, gates, pa, pb, bg, tm=256):
    t = gates.shape[0]

    def body(d_ref, g_ref, pa_ref, pb_ref, bg_ref, dg_ref, dpa_ref, dpb_ref, dbg_ref):
        @pl.when(pl.program_id(0) == 0)
        def _():
            dbg_ref[...] = jnp.zeros_like(dbg_ref)

        _, vjp = jax.vjp(_merge, g_ref[...], pa_ref[...], pb_ref[...], bg_ref[...])
        dg, dpa, dpb, dbg = vjp(d_ref[...])
        dg_ref[...] = dg.astype(BF16)
        dpa_ref[...] = dpa.astype(BF16)
        dpb_ref[...] = dpb.astype(BF16)
        dbg_ref[...] += dbg

    return pl.pallas_call(
        body, name="merge_bwd", grid=(t // tm,),
        in_specs=[_rows(tm, D_MODEL), _rows(tm, 2 * D_MODEL), _rows(tm, D_MODEL), _rows(tm, D_MODEL),
                  _full((1, 2 * D_MODEL))],
        out_specs=[_rows(tm, 2 * D_MODEL), _rows(tm, D_MODEL), _rows(tm, D_MODEL), _full((1, 2 * D_MODEL))],
        out_shape=[jax.ShapeDtypeStruct((t, 2 * D_MODEL), BF16), jax.ShapeDtypeStruct((t, D_MODEL), BF16),
                   jax.ShapeDtypeStruct((t, D_MODEL), BF16), jax.ShapeDtypeStruct((1, 2 * D_MODEL), F32)],
        compiler_params=_params(1),
    )(dmix, gates, pa, pb, bg)


def _residual_norm_fwd(x, o, w, tm=512):
    t, d = x.shape

    def body(x_ref, o_ref, w_ref, h_ref, n_ref):
        h = x_ref[...] + o_ref[...]
        h_ref[...] = h
        n_ref[...] = _rms(h, w_ref[...]).astype(BF16)

    return pl.pallas_call(
        body, name="residual_norm_fwd", grid=(t // tm,), in_specs=[_rows(tm, d), _rows(tm, d), _full((1, d))],
        out_specs=[_rows(tm, d), _rows(tm, d)],
        out_shape=[jax.ShapeDtypeStruct((t, d), F32), jax.ShapeDtypeStruct((t, d), BF16)], compiler_params=_params(1),
    )(x, o, w)


def _norm_bwd(dn, h, w, dres, name, tm=512):
    t, d = h.shape

    def body(dn_ref, h_ref, w_ref, dres_ref, dh_ref, dhb_ref, dw_ref):
        @pl.when(pl.program_id(0) == 0)
        def _():
            dw_ref[...] = jnp.zeros_like(dw_ref)

        _, vjp = jax.vjp(_rms, h_ref[...], w_ref[...])
        dh, dw = vjp(dn_ref[...])
        dh = dh + dres_ref[...]
        dh_ref[...] = dh
        dhb_ref[...] = dh.astype(BF16)
        dw_ref[...] += dw

    return pl.pallas_call(
        body, name=name, grid=(t // tm,), in_specs=[_rows(tm, d), _rows(tm, d), _full((1, d)), _rows(tm, d)],
        out_specs=[_rows(tm, d), _rows(tm, d), _full((1, d))],
        out_shape=[jax.ShapeDtypeStruct((t, d), F32), jax.ShapeDtypeStruct((t, d), BF16),
                   jax.ShapeDtypeStruct((1, d), F32)], compiler_params=_params(1),
    )(dn, h, w, dres)


def _conv_f_fwd(up, cw, cb, tm=128):
    t, c = up.shape

    def body(x_ref, h_ref, w_ref, b_ref, o_ref):
        halo = jnp.where(pl.program_id(0) > 0, h_ref[...], 0.0)
        y, _ = _causal_conv(x_ref[...], halo, w_ref[...], b_ref[...])
        o_ref[...] = (_silu(y[:, :D_FF]) * y[:, D_FF:]).astype(BF16)

    return pl.pallas_call(
        body, name="conv_f_fwd", grid=(t // tm,),
        in_specs=[_rows(tm, c), _halo(tm, c), _full(cw.shape), _full((1, c))], out_specs=_rows(tm, D_FF),
        out_shape=jax.ShapeDtypeStruct((t, D_FF), BF16), compiler_params=_params(1),
    )(up, up, cw, cb)


def _conv_f_bwd(dact, up, cw, cb, tm=128):
    t, c = up.shape
    nt = t // tm

    def body(d_ref, x_ref, h_ref, w_ref, b_ref, dx_ref, dw_ref, db_ref, nxt_scr):
        @pl.when(pl.program_id(0) == 0)
        def _():
            nxt_scr[...] = jnp.zeros_like(nxt_scr)
            dw_ref[...] = jnp.zeros_like(dw_ref)
            db_ref[...] = jnp.zeros_like(db_ref)

        halo = jnp.where(pl.program_id(0) < nt - 1, h_ref[...], 0.0)
        w = w_ref[...]
        y, shifted = _causal_conv(x_ref[...], halo, w, b_ref[...])
        a, v = y[:, :D_FF], y[:, D_FF:]
        d = d_ref[...]
        dy = jnp.concatenate([d * v * _dsilu(a), d * _silu(a)], axis=1)
        dx_ref[...] = _anticausal_conv(dy, nxt_scr[...], w).astype(BF16)
        nxt_scr[...] = dy[:8]
        dw_ref[...] += _conv_wgrad(dy, shifted)
        db_ref[...] += _colsum(dy)

    return pl.pallas_call(
        body, name="conv_f_bwd", grid=(nt,),
        in_specs=[_rows(tm, D_FF, nt, True), _rows(tm, c, nt, True), _halo(tm, c, nt, True), _full(cw.shape),
                  _full((1, c))],
        out_specs=[_rows(tm, c, nt, True), _full(cw.shape), _full((1, c))],
        out_shape=[jax.ShapeDtypeStruct((t, c), BF16), jax.ShapeDtypeStruct(cw.shape, F32),
                   jax.ShapeDtypeStruct((1, c), F32)],
        scratch_shapes=[pltpu.VMEM((8, c), F32)], compiler_params=_params(1),
    )(dact, up, up, cw, cb)


def _conv_a_bwd(dxs, db, dc, xbc, cw, cb, tm=256):
    t, c = xbc.shape
    nt = t // tm

    def body(dxs_ref, db_ref, dc_ref, x_ref, h_ref, w_ref, b_ref, dx_ref, dw_ref, dbias_ref, nxt_scr):
        @pl.when(pl.program_id(0) == 0)
        def _():
            nxt_scr[...] = jnp.zeros_like(nxt_scr)
            dw_ref[...] = jnp.zeros_like(dw_ref)
            dbias_ref[...] = jnp.zeros_like(dbias_ref)

        halo = jnp.where(pl.program_id(0) < nt - 1, h_ref[...], 0.0)
        w = w_ref[...]
        y, shifted = _causal_conv(x_ref[...], halo, w, b_ref[...])
        dy = jnp.concatenate([dxs_ref[...], db_ref[...], dc_ref[...]], axis=1) * _dsilu(y)
        dx_ref[...] = _anticausal_conv(dy, nxt_scr[...], w).astype(BF16)
        nxt_scr[...] = dy[:8]
        dw_ref[...] += _conv_wgrad(dy, shifted)
        dbias_ref[...] += _colsum(dy)

    return pl.pallas_call(
        body, name="conv_a_bwd", grid=(nt,),
        in_specs=[_rows(tm, SSD_INNER, nt, True), _rows(tm, SSD_BC, nt, True), _rows(tm, SSD_BC, nt, True),
                  _rows(tm, c, nt, True), _halo(tm, c, nt, True), _full(cw.shape), _full((1, c))],
        out_specs=[_rows(tm, c, nt, True), _full(cw.shape), _full((1, c))],
        out_shape=[jax.ShapeDtypeStruct((t, c), BF16), jax.ShapeDtypeStruct(cw.shape, F32),
                   jax.ShapeDtypeStruct((1, c), F32)],
        scratch_shapes=[pltpu.VMEM((8, c), F32)], compiler_params=_params(1),
    )(dxs, db, dc, xbc, xbc, cw, cb)


def _loss_head(h1, dn, w, target, tm=512):
    t, d = h1.shape

    def body(h_ref, dn_ref, w_ref, t_ref, loss_ref, dh_ref, dhb_ref, dw_ref):
        @pl.when(pl.program_id(0) == 0)
        def _():
            loss_ref[...] = jnp.zeros_like(loss_ref)
            dw_ref[...] = jnp.zeros_like(dw_ref)

        yf, vjp = jax.vjp(_rms, h_ref[...] + dn_ref[...], w_ref[...])
        err = yf - t_ref[...]
        loss_ref[...] += 0.5 * jnp.sum(jnp.mean(err * err, axis=-1, keepdims=True))
        dh, dw = vjp(err * (1.0 / d))
        dh_ref[...] = dh
        dhb_ref[...] = dh.astype(BF16)
        dw_ref[...] += dw

    return pl.pallas_call(
        body, name="loss_head", grid=(t // tm,),
        in_specs=[_rows(tm, d), _rows(tm, d), _full((1, d)), _rows(tm, d)],
        out_specs=[_full((8, LANES)), _rows(tm, d), _rows(tm, d), _full((1, d))],
        out_shape=[jax.ShapeDtypeStruct((8, LANES), F32), jax.ShapeDtypeStruct((t, d), F32),
                   jax.ShapeDtypeStruct((t, d), BF16), jax.ShapeDtypeStruct((1, d), F32)], compiler_params=_params(1),
    )(h1, dn, w, target)


def _pad_lanes(v, n=DT_PAD):
    return jnp.pad(v, ((0, 0), (0, n - v.shape[1])))


def _local_step(x, target, w, p, after=None, late_weights=None):
    dtb, alog, dsk = _pad_lanes(p["dt_bias"]), _pad_lanes(p["a_log"]), _pad_lanes(p["d_skip"])
    bs_t = _pad_lanes(p["b_spatial"].T)
    e_heads = (jnp.arange(SSD_INNER)[:, None] // SSD_HEAD_DIM == jnp.arange(LANES)[None, :]).astype(BF16)
    e_groups = (jnp.arange(SGU_WIDTH)[:, None] // LANES == jnp.arange(LANES)[None, :]).astype(BF16)

    n1 = _norm_fwd(x, p["norm1_w"], "norm1_fwd", after=after)
    z = _mm(n1, w["z"], "nn", "proj_z")
    xbc = _mm(n1, w["xbc"], "nn", "proj_xbc")
    dtr = _mm(n1, w["dt"], "nn", "proj_dt")
    uv = _mm(n1, w["uv"], "nn", "proj_uv")
    gates = _mm(n1, w["gates"], "nn", "proj_gates")
    xc = _conv_a_fwd(xbc, w["conv_a"], p["conv_a_b"])
    y, ya, sprev = _ssd_fwd(xc, dtr, z, dtb, alog, dsk, p["ssd_norm_w"])
    yb = _sgu_fwd(uv, p["uv_b"], p["v_ln_w"], p["v_ln_b"], p["w_spatial"], bs_t)
    if late_weights is not None:
        w = {**w, **late_weights(yb)}
    pa = _mm(ya, w["branch_a"], "nn", "branch_a")
    pb = _mm(yb, w["branch_b"], "nn", "branch_b")
    mix = _merge_fwd(gates, pa, pb, p["b_gate"])
    o = _mm(mix, w["out"], "nn", "out_proj")
    h1, n2 = _residual_norm_fwd(x, o, p["norm2_w"])
    up = _mm(n2, w["up"], "nn", "up_proj")
    act = _conv_f_fwd(up, w["conv_f"], p["conv_f_b"])
    dn = _mm(act, w["down"], "nn", "down_proj")
    loss, dh2, dh2b, g_final = _loss_head(h1, dn, p["final_norm_w"], target)

    g = {"final_norm_w": g_final}
    g["down"] = _wgrad(act, dh2b, "down_wgrad")
    dact = _mm(dh2b, w["down"], "nt", "down_dgrad")
    dup, g["conv_f"], g["conv_f_b"] = _conv_f_bwd(dact, up, w["conv_f"], p["conv_f_b"])
    g["up"] = _wgrad(n2, dup, "up_wgrad")
    dn2 = _mm(dup, w["up"], "nt", "up_dgrad")
    dh1, dh1b, g["norm2_w"] = _norm_bwd(dn2, h1, p["norm2_w"], dh2, "norm2_bwd")
    g["out"] = _wgrad(mix, dh1b, "out_wgrad")
    dmix = _mm(dh1b, w["out"], "nt", "out_dgrad")
    dgates, dpa, dpb, g["b_gate"] = _merge_bwd(dmix, gates, pa, pb, p["b_gate"])
    g["branch_a"] = _wgrad(ya, dpa, "branch_a_wgrad")
    g["branch_b"] = _wgrad(yb, dpb, "branch_b_wgrad")
    dya = _mm(dpa, w["branch_a"], "nt", "branch_a_dgrad")
    dyb = _mm(dpb, w["branch_b"], "nt", "branch_b_dgrad")
    duv, g["uv_b"], g["v_ln_w"], g["v_ln_b"], g["w_spatial"], dbs_t = _sgu_bwd(
        dyb, uv, p["uv_b"], p["v_ln_w"], p["v_ln_b"], p["w_spatial"], bs_t, e_groups)
    g["b_spatial"] = dbs_t[:, :SGU_GROUPS].T
    dz, dxs, db, dc, ddtr, g["ssd_norm_w"], ddtb, dalog, ddsk = _ssd_bwd(
        dya, y, z, xc, dtr, sprev, dtb, alog, dsk, p["ssd_norm_w"], e_heads)
    g["dt_bias"], g["a_log"], g["d_skip"] = ddtb[:, :SSD_HEADS], dalog[:, :SSD_HEADS], ddsk[:, :SSD_HEADS]
    dxbc, g["conv_a"], g["conv_a_b"] = _conv_a_bwd(dxs, db, dc, xbc, w["conv_a"], p["conv_a_b"])
    ddtrb = ddtr.astype(BF16)
    for name, d in (("z", dz), ("xbc", dxbc), ("dt", ddtrb), ("uv", duv), ("gates", dgates)):
        g[name] = _wgrad(n1, d, name + "_wgrad")
    dn1 = _mm(dz, w["z"], "nt", "z_dgrad")
    dn1 = _mm(dxbc, w["xbc"], "nt", "xbc_dgrad", acc=dn1)
    dn1 = _mm(ddtrb, w["dt"], "nt", "dt_dgrad", acc=dn1)
    dn1 = _mm(duv, w["uv"], "nt", "uv_dgrad", acc=dn1)
    dn1 = _mm(dgates, w["gates"], "nt", "gates_dgrad", acc=dn1)
    gx, _, g["norm1_w"] = _norm_bwd(dn1, x, p["norm1_w"], dh1, "norm1_bwd")
    return loss, gx, g


def _place():
    return lax.axis_index("x"), lax.axis_index("y"), lax.axis_index("c")


def _other_chips(x, y):
    return [(1 - x, y), (x, 1 - y), (1 - x, 1 - y)]


def _all_gather(shards, name):
    n = len(shards)

    def body(*refs):
        ins, outs = refs[:n], refs[n:2 * n]
        send_sems, recv_sems, local_sems = refs[2 * n:]
        x, y, c = _place()
        me, sibling = (x, y, c), (x, y, 1 - c)
        chips = _other_chips(x, y)

        def copy(a, k, block, to, src=None):
            slot = outs[a].at[4 * block[0] + 2 * block[1] + block[2]]
            return pltpu.make_async_remote_copy(
                src_ref=slot if src is None else src, dst_ref=slot, send_sem=send_sems.at[7 * a + k],
                recv_sem=recv_sems.at[7 * a + k], device_id=to, device_id_type=MESH)

        started = []
        for a in range(n):
            mine = pltpu.make_async_copy(ins[a], outs[a].at[4 * x + 2 * y + c], local_sems.at[a])
            mine.start()
            started.append(mine)
        sends = []
        for a in range(n):
            sends.append(copy(a, 0, me, sibling, src=ins[a]))
            sends += [copy(a, 1 + j, me, (*chip, c), src=ins[a]) for j, chip in enumerate(chips)]
        for cp in sends:
            cp.start()
        for a in range(n):
            for j, chip in enumerate(chips):
                copy(a, 1 + j, (*chip, c), me).wait_recv()
                fwd = copy(a, 4 + j, (*chip, c), sibling)
                fwd.start()
                sends.append(fwd)
        for a in range(n):
            copy(a, 0, sibling, me).wait_recv()
            for j, chip in enumerate(chips):
                copy(a, 4 + j, (*chip, 1 - c), me).wait_recv()
        for cp in sends:
            cp.wait_send()
        for mine in started:
            mine.wait()

    any_spec = pl.BlockSpec(memory_space=pl.ANY)
    return pl.pallas_call(
        body, name=name, in_specs=[any_spec] * n, out_specs=[any_spec] * n,
        out_shape=[jax.ShapeDtypeStruct((N_DEV, *s.shape), s.dtype) for s in shards],
        scratch_shapes=[pltpu.SemaphoreType.DMA((7 * n,)), pltpu.SemaphoreType.DMA((7 * n,)),
                        pltpu.SemaphoreType.DMA((n,))],
    )(*shards)


HBM_SPEC = pl.BlockSpec(memory_space=pltpu.HBM)
SEM_SPEC = pl.BlockSpec(memory_space=pltpu.SEMAPHORE)
ANY_SPEC = pl.BlockSpec(memory_space=pl.ANY)
DATAFLOW = pltpu.SideEffectType.DATAFLOW_SIDE_EFFECTING
N_PEERS = N_DEV - 1


def _peers(x, y, c):
    out = []
    for r in range(1, N_DEV):
        fx, fy, fc = r >> 2 & 1, r >> 1 & 1, r & 1
        out.append(((1 - x) if fx else x, (1 - y) if fy else y, (1 - c) if fc else c))
    return out


def _gather_copies(srcs, lands, send_sems, recv_sems, sending):
    x, y, c = _place()
    copies = []
    for a, (src, land) in enumerate(zip(srcs, lands)):
        for j, (px, py, pc) in enumerate(_peers(x, y, c)):
            slot = 4 * x + 2 * y + c if sending else 4 * px + 2 * py + pc
            copies.append(pltpu.make_async_remote_copy(
                src_ref=src, dst_ref=land.at[slot], send_sem=send_sems.at[N_PEERS * a + j],
                recv_sem=recv_sems.at[N_PEERS * a + j], device_id=(px, py, pc), device_id_type=MESH))
    return copies


def _gather_start(shards, after, name):
    n = len(shards)

    def body(*refs):
        srcs, lands = refs[:n], refs[n:2 * n]
        send_sems, recv_sems = refs[2 * n + 1:2 * n + 3]
        token = refs[-1]
        for cp in _gather_copies(srcs, lands, send_sems, recv_sems, sending=True):
            cp.start()
        token[...] = jnp.zeros_like(token)

    lands = [lax.empty((N_DEV, *s.shape), s.dtype) for s in shards]
    hbm = lambda a: pltpu.with_memory_space_constraint(a, pltpu.HBM)
    out = pl.pallas_call(
        body, name=name,
        out_shape=(pltpu.SemaphoreType.DMA((N_PEERS * n,)), pltpu.SemaphoreType.DMA((N_PEERS * n,)),
                   *[pltpu.HBM(a.shape, a.dtype) for a in (*shards, *lands)], jax.ShapeDtypeStruct((8, LANES), F32)),
        in_specs=[HBM_SPEC] * (2 * n) + [ANY_SPEC],
        out_specs=(SEM_SPEC, SEM_SPEC, *[HBM_SPEC] * (2 * n), pl.BlockSpec(memory_space=pltpu.VMEM)),
        input_output_aliases={i: 2 + i for i in range(2 * n)},
        compiler_params=pltpu.CompilerParams(has_side_effects=DATAFLOW),
    )(*[hbm(a) for a in (*shards, *lands)], after)
    return out[0], out[1], out[2:2 + n], out[2 + n:2 + 2 * n], out[-1]


def _gather_wait(send_sems, recv_sems, shards, lands, after, name):
    n = len(shards)

    def body(*refs):
        srcs, lands_ = refs[:n], refs[n:2 * n]
        send, recv = refs[2 * n:2 * n + 2]
        for cp in _gather_copies(srcs, lands_, send, recv, sending=False):
            cp.wait_send()
            cp.wait_recv()

    out = pl.pallas_call(
        body, name=name, out_shape=tuple(pltpu.HBM(a.shape, a.dtype) for a in (*shards, *lands)),
        in_specs=[HBM_SPEC] * (2 * n) + [SEM_SPEC, SEM_SPEC, ANY_SPEC], out_specs=tuple([HBM_SPEC] * (2 * n)),
        input_output_aliases={i: i for i in range(2 * n)},
        compiler_params=pltpu.CompilerParams(has_side_effects=DATAFLOW),
    )(*shards, *lands, send_sems, recv_sems, after)
    return out[n:]


def _exchange_cores(parts, name):
    n = len(parts)

    def body(*refs):
        ins, outs = refs[:n], refs[n:2 * n]
        send_sems, recv_sems = refs[2 * n:]
        x, y, c = _place()
        copies = []
        for a in range(n):
            for k in range(4):
                copies.append(pltpu.make_async_remote_copy(
                    src_ref=ins[a].at[2 * k + (1 - c)], dst_ref=outs[a].at[k], send_sem=send_sems.at[4 * a + k],
                    recv_sem=recv_sems.at[4 * a + k], device_id=(x, y, 1 - c), device_id_type=MESH))
        for cp in copies:
            cp.start()
        for cp in copies:
            cp.wait()

    any_spec = pl.BlockSpec(memory_space=pl.ANY)
    return pl.pallas_call(
        body, name=name, in_specs=[any_spec] * n, out_specs=[any_spec] * n,
        out_shape=[jax.ShapeDtypeStruct((4, *s.shape[1:]), s.dtype) for s in parts],
        scratch_shapes=[pltpu.SemaphoreType.DMA((4 * n,)), pltpu.SemaphoreType.DMA((4 * n,))],
    )(*parts)


def _exchange_chips(parts, name):
    n = len(parts)

    def body(*refs):
        ins, outs = refs[:n], refs[n:2 * n]
        send_sems, recv_sems = refs[2 * n:]
        x, y, c = _place()
        copies = []
        for a in range(n):
            for j, (cx, cy) in enumerate(_other_chips(x, y)):
                copies.append(pltpu.make_async_remote_copy(
                    src_ref=ins[a].at[2 * cx + cy], dst_ref=outs[a].at[j], send_sem=send_sems.at[3 * a + j],
                    recv_sem=recv_sems.at[3 * a + j], device_id=(cx, cy, c), device_id_type=MESH))
        for cp in copies:
            cp.start()
        for cp in copies:
            cp.wait()

    any_spec = pl.BlockSpec(memory_space=pl.ANY)
    return pl.pallas_call(
        body, name=name, in_specs=[any_spec] * n, out_specs=[any_spec] * n,
        out_shape=[jax.ShapeDtypeStruct((3, *s.shape[1:]), s.dtype) for s in parts],
        scratch_shapes=[pltpu.SemaphoreType.DMA((3 * n,)), pltpu.SemaphoreType.DMA((3 * n,))],
    )(*parts)


def _chip_sum(part, got, place, name, tr=256):
    _, r, c = part.shape
    tr = _row_tile(r, tr)

    def body(place_ref, p_ref, g_ref, q_ref, own_ref):
        s = p_ref[0].astype(F32) + g_ref[0].astype(F32)
        q_ref[0] = s.astype(BF16)

        @pl.when(pl.program_id(1) == place_ref[1])
        def _():
            own_ref[...] = s

    grid_spec = pltpu.PrefetchScalarGridSpec(
        num_scalar_prefetch=1, grid=(r // tr, 4),
        in_specs=[pl.BlockSpec((1, tr, c), lambda i, k, pr: (2 * k + pr[0], i, 0)),
                  pl.BlockSpec((1, tr, c), lambda i, k, pr: (k, i, 0))],
        out_specs=[pl.BlockSpec((1, tr, c), lambda i, k, pr: (k, i, 0)),
                   pl.BlockSpec((tr, c), lambda i, k, pr: (i, 0))])
    return pl.pallas_call(
        body, name=name, grid_spec=grid_spec,
        out_shape=[jax.ShapeDtypeStruct((4, r, c), BF16), jax.ShapeDtypeStruct((r, c), F32)],
        compiler_params=_params(2),
    )(place, part, got)


def _adamw(w, g, m, v):
    m = ADAM_B1 * m + (1.0 - ADAM_B1) * g
    v = ADAM_B2 * v + (1.0 - ADAM_B2) * jnp.square(g)
    m_hat = m / (1.0 - ADAM_B1 ** ADAM_STEP)
    v_hat = v / (1.0 - ADAM_B2 ** ADAM_STEP)
    return -ADAM_LR * (m_hat / (jnp.sqrt(v_hat) + ADAM_EPS) + ADAM_WD * w), m, v


def _sum_adamw(own, got, w, m, v, name, tr=256):
    r, c = w.shape
    tr = _row_tile(r, tr)

    def body(own_ref, got_ref, w_ref, m_ref, v_ref, g_ref, d_ref, nm_ref, nv_ref):
        g = own_ref[...]
        for j in range(3):
            g = g + got_ref[j].astype(F32)
        g_ref[...] = g
        d_ref[...], nm_ref[...], nv_ref[...] = _adamw(w_ref[...], g, m_ref[...], v_ref[...])

    blk = pl.BlockSpec((tr, c), lambda i: (i, 0))
    return pl.pallas_call(
        body, name=name, grid=(r // tr,),
        in_specs=[blk, pl.BlockSpec((3, tr, c), lambda i: (0, i, 0)), blk, blk, blk], out_specs=[blk] * 4,
        out_shape=[jax.ShapeDtypeStruct((r, c), F32)] * 4, compiler_params=_params(1),
    )(own, got, w, m, v)


def _sum_devices(parts, name):
    _, r, c = parts.shape
    tr = r

    def body(p_ref, o_ref):
        s = p_ref[0]
        for d in range(1, N_DEV):
            s = s + p_ref[d]
        o_ref[...] = s

    return pl.pallas_call(
        body, name=name, grid=(pl.cdiv(r, tr),), in_specs=[pl.BlockSpec((N_DEV, tr, c), lambda i: (0, i, 0))],
        out_specs=pl.BlockSpec((tr, c), lambda i: (i, 0)), out_shape=jax.ShapeDtypeStruct((r, c), F32),
        compiler_params=_params(1),
    )(parts)


def _adamw_call(w, g, m, v, name):
    r, c = w.shape
    tr = r

    def body(w_ref, g_ref, m_ref, v_ref, d_ref, nm_ref, nv_ref):
        d_ref[...], nm_ref[...], nv_ref[...] = _adamw(w_ref[...], g_ref[...], m_ref[...], v_ref[...])

    blk = pl.BlockSpec((tr, c), lambda i: (i, 0))
    return pl.pallas_call(
        body, name=name, grid=(pl.cdiv(r, tr),), in_specs=[blk] * 4, out_specs=[blk] * 3,
        out_shape=[jax.ShapeDtypeStruct((r, c), F32)] * 3, compiler_params=_params(1),
    )(w, g, m, v)


PACK_ROWS = 8


def _pack(arrays):
    parts = []
    for a in arrays:
        flat = a.reshape(-1)
        unit = PACK_ROWS * LANES
        parts.append(jnp.pad(flat, (0, -flat.shape[0] % unit)).reshape(-1, LANES))
    return jnp.concatenate(parts, axis=0)


def _unpack(pack, shapes):
    out, row = [], 0
    for s in shapes:
        size = 1
        for d in s:
            size *= d
        rows = -(-size // (PACK_ROWS * LANES)) * PACK_ROWS
        out.append(pack[row:row + rows].reshape(-1)[:size].reshape(s))
        row += rows
    return out


SMALL = ["norm1_w", "b_gate", "conv_a_b", "dt_bias", "a_log", "d_skip", "ssd_norm_w", "uv_b", "v_ln_w", "v_ln_b",
         "w_spatial", "b_spatial", "norm2_w", "conv_f_b", "final_norm_w"]
BIG = ["w_in", "w_branch", "w_out", "w_up", "w_down"]
WEIGHTS = ["norm1_w", "w_in", "b_gate", "conv_a_w", "conv_a_b", "dt_bias", "a_log", "d_skip", "ssd_norm_w", "uv_b",
           "v_ln_w", "v_ln_b", "w_spatial", "b_spatial", "w_branch", "w_out", "norm2_w", "w_up", "conv_f_w",
           "conv_f_b", "w_down", "final_norm_w"]
IN_SPLITS = [("z", 0, 2048), ("xbc", 2048, 5120), ("dt", 5120, 5152), ("uv", 5152, 7200), ("gates", 7200, 9248)]


def _columns_by_device(a):
    r = a.shape[0]
    return a.reshape(r, N_DEV, -1).transpose(1, 0, 2)


def _columns_from_devices(a):
    return a.transpose(1, 0, 2).reshape(a.shape[1], -1)


def kernel(x, norm1_w, w_in, b_gate, conv_a_w, conv_a_b, dt_bias, a_log, d_skip, ssd_norm_w, uv_b, v_ln_w, v_ln_b, w_spatial, b_spatial, w_branch, w_out, norm2_w, w_up, conv_f_w, conv_f_b, w_down, final_norm_w, loss_target, m_norm1_w, m_w_in, m_b_gate, m_conv_a_w, m_conv_a_b, m_dt_bias, m_a_log, m_d_skip, m_ssd_norm_w, m_uv_b, m_v_ln_w, m_v_ln_b, m_w_spatial, m_b_spatial, m_w_branch, m_w_out, m_norm2_w, m_w_up, m_conv_f_w, m_conv_f_b, m_w_down, m_final_norm_w, v_norm1_w, v_w_in, v_b_gate, v_conv_a_w, v_conv_a_b, v_dt_bias, v_a_log, v_d_skip, v_ssd_norm_w, v_uv_b, v_v_ln_w, v_v_ln_b, v_w_spatial, v_b_spatial, v_w_branch, v_w_out, v_norm2_w, v_w_up, v_conv_f_w, v_conv_f_b, v_w_down, v_final_norm_w):
    args = dict(locals())
    wts = {n: args[n] for n in WEIGHTS}
    mom = {n: args["m_" + n] for n in WEIGHTS}
    var = {n: args["v_" + n] for n in WEIGHTS}
    cx, cy, cc = _place()
    dev = 4 * cx + 2 * cy + cc
    place = jnp.stack([cc, 2 * cx + cy]).astype(jnp.int32)

    g_in, g_conv_a, g_conv_f = _all_gather([w_in[0].astype(BF16), conv_a_w[0], conv_f_w[0]], "gather_w_in")
    late = [wts[n][0].astype(BF16) for n in BIG[1:]]
    send_sems, recv_sems, late, lands, token = _gather_start(late, g_in, "gather_late_start")
    w_in_full = _columns_from_devices(g_in)
    w = {name: w_in_full[:, lo:hi] for name, lo, hi in IN_SPLITS}
    w["dt"] = _pad_lanes(w["dt"])
    w["conv_a"] = _columns_from_devices(g_conv_a)
    w["conv_f"] = _columns_from_devices(g_conv_f)

    def late_weights(after):
        got = _gather_wait(send_sems, recv_sems, late, lands, after, "gather_late_wait")
        g_branch, g_out, g_up, g_down = [lax.dynamic_update_index_in_dim(land, mine, dev, 0)
                                         for land, mine in zip(got, late)]
        branch = g_branch.reshape(-1, D_MODEL)
        return {"branch_a": branch[:SSD_INNER], "branch_b": branch[SSD_INNER:], "out": g_out.reshape(-1, D_MODEL),
                "up": _columns_from_devices(g_up), "down": g_down.reshape(-1, D_MODEL)}

    p = {n: wts[n][0] if wts[n].ndim > 2 else wts[n].reshape(1, -1) for n in SMALL}
    loss, gx, g = _local_step(x[0], loss_target[0], w, p, after=token, late_weights=late_weights)
    loss = lax.psum(loss[0, 0], ("x", "y", "c"))

    small_g = [g[n] for n in SMALL] + [g["conv_a"], g["conv_f"]]
    gathered, = _all_gather([_pack(small_g)], "gather_small_grads")
    small_sum = _unpack(_sum_devices(gathered, "sum_small_grads"), [a.shape for a in small_g])
    grads = {n: s.reshape(wts[n].shape) for n, s in zip(SMALL, small_sum[:len(SMALL)])}
    for n, s in (("conv_a_w", small_sum[-2]), ("conv_f_w", small_sum[-1])):
        cols = wts[n].shape[2]
        grads[n] = lax.dynamic_slice_in_dim(s, dev * cols, cols, axis=1)[None]

    g_in_full = jnp.concatenate([g[name][:, :hi - lo] for name, lo, hi in IN_SPLITS], axis=1)
    parts = [_columns_by_device(g_in_full), jnp.concatenate([g["branch_a"], g["branch_b"]], axis=0),
             g["out"], _columns_by_device(g["up"]), g["down"]]
    parts = [a.astype(BF16).reshape(N_DEV, *wts[n].shape[1:]) for n, a in zip(BIG, parts)]
    from_core = _exchange_cores(parts, "grads_to_other_core")
    sums = [_chip_sum(a, b, place, f"chip_sum_{n}") for n, a, b in zip(BIG, parts, from_core)]
    from_chips = _exchange_chips([s[0] for s in sums], "grads_to_other_chips")

    delta, new_m, new_v = {}, {}, {}
    for n, s, got in zip(BIG, sums, from_chips):
        gr, d, nm, nv = _sum_adamw(s[1], got, wts[n][0], mom[n][0], var[n][0], f"adamw_{n}")
        grads[n], delta[n], new_m[n], new_v[n] = gr[None], d[None], nm[None], nv[None]
    small_names = SMALL + ["conv_a_w", "conv_f_w"]
    packs = [_pack([t[n] for n in small_names]) for t in (wts, grads, mom, var)]
    outs = _adamw_call(*packs, "adamw_small")
    shapes = [wts[n].shape for n in small_names]
    for tgt, pack in zip((delta, new_m, new_v), outs):
        tgt.update(dict(zip(small_names, _unpack(pack, shapes))))

    return (loss, gx[None], *[grads[n] for n in WEIGHTS], *[delta[n] for n in WEIGHTS],
            *[new_m[n] for n in WEIGHTS], *[new_v[n] for n in WEIGHTS])
```

```python
import functools

import jax
import jax.numpy as jnp
from jax import lax
from jax.experimental import pallas as pl
from jax.experimental.pallas import tpu as pltpu

F32, BF16 = jnp.float32, jnp.bfloat16
HIGHEST = lax.Precision.HIGHEST

D_MODEL = 1024
SSD_INNER = 2048
SSD_HEAD_DIM = 64
SSD_HEADS = 32
SSD_GROUPS = 4
SSD_STATE = 128
SSD_BC = SSD_GROUPS * SSD_STATE
SSD_XBC = SSD_INNER + 2 * SSD_BC
SSD_CONV = 4
CHUNK = 128
N_PAIRS = SSD_HEADS // 2
PAIRS_PER_GROUP = N_PAIRS // SSD_GROUPS
SGU_WIDTH = 1024
SGU_GROUPS = 8
D_FF = 2816
FFN_CONV = 3
NORM_EPS = 1e-6
LN_EPS = 1e-5
LANES = 128
DT_PAD = LANES

ADAM_LR, ADAM_B1, ADAM_B2, ADAM_EPS, ADAM_WD, ADAM_STEP = 0.001, 0.9, 0.999, 1e-08, 0.01, 10

N_DEV = 8
VMEM_LIMIT = 56 * 1024 * 1024
MESH = pl.DeviceIdType.MESH


def _params(n_grid, **kw):
    sem = dict(dimension_semantics=("arbitrary",) * n_grid) if n_grid else {}
    return pltpu.CompilerParams(vmem_limit_bytes=VMEM_LIMIT, **sem, **kw)


def _tile(n, pref):
    t = (min(pref, n) // LANES) * LANES
    while n % t:
        t -= LANES
    return t


def _row_tile(r, pref):
    for t in range(min(pref, r) // 16 * 16, 0, -16):
        if r % t == 0:
            return t
    return r


def _rows(tm, n, nt=None, rev=False, col=0):
    if rev:
        return pl.BlockSpec((tm, n), lambda i: (nt - 1 - i, col))
    return pl.BlockSpec((tm, n), lambda i: (i, col))


def _halo(tm, n, nt=None, rev=False):
    per = tm // 8
    if rev:
        return pl.BlockSpec((8, n), lambda i: (jnp.maximum((nt - 1 - i) * per - 1, 0), 0))
    return pl.BlockSpec((8, n), lambda i: (jnp.maximum(i * per - 1, 0), 0))


def _full(shape):
    nd = len(shape)
    return pl.BlockSpec(shape, lambda *_: (0,) * nd)


def _rms(x, w, eps=NORM_EPS):
    return x * lax.rsqrt(jnp.mean(x * x, axis=-1, keepdims=True) + eps) * w


def _layer_norm(x, w, b):
    mu = jnp.mean(x, axis=-1, keepdims=True)
    var = jnp.mean(jnp.square(x - mu), axis=-1, keepdims=True)
    return (x - mu) * lax.rsqrt(var + LN_EPS) * w + b


def _sigmoid(x):
    return 1.0 / (1.0 + jnp.exp(-x))


def _silu(x):
    return x * _sigmoid(x)


def _dsilu(x):
    s = _sigmoid(x)
    return s * (1.0 + x * (1.0 - s))


def _softplus(x):
    return jnp.maximum(x, 0.0) + jnp.log(1.0 + jnp.exp(-jnp.abs(x)))


def _gelu(x):
    return jax.nn.gelu(x)


def _dot(a, b):
    return jnp.dot(a, b, preferred_element_type=F32)


def _dot_nt(a, b):
    return lax.dot_general(a, b, (((1,), (1,)), ((), ())), preferred_element_type=F32)


def _dot_tn(a, b):
    return lax.dot_general(a, b, (((0,), (0,)), ((), ())), preferred_element_type=F32)


def _dot_split(p, e):
    hi = p.astype(BF16)
    lo = (p - hi.astype(F32)).astype(BF16)
    return _dot(hi, e) + _dot(lo, e)


def _colsum(x):
    return jnp.sum(x, axis=0, keepdims=True)


def _shift_down(x, halo, j):
    xs = pltpu.roll(x, j, 0)
    hs = pltpu.roll(halo, j, 0)
    r8 = lax.broadcasted_iota(jnp.int32, hs.shape, 0)
    return jnp.concatenate([jnp.where(r8 < j, hs, xs[:8]), xs[8:]], axis=0)


def _shift_up(x, nxt, j):
    n = x.shape[0]
    xs = pltpu.roll(x, n - j, 0)
    ns = pltpu.roll(nxt, 8 - j, 0)
    r8 = lax.broadcasted_iota(jnp.int32, ns.shape, 0)
    return jnp.concatenate([xs[:n - 8], jnp.where(r8 >= 8 - j, ns, xs[n - 8:])], axis=0)


def _causal_conv(x, halo, w, b):
    k = w.shape[0]
    shifted = [x] + [_shift_down(x, halo, j) for j in range(1, k)]
    y = b + w[k - 1:k, :] * x
    for j in range(1, k):
        y = y + w[k - 1 - j:k - j, :] * shifted[j]
    return y, shifted


def _anticausal_conv(dy, nxt, w):
    k = w.shape[0]
    dx = w[k - 1:k, :] * dy
    for j in range(1, k):
        dx = dx + w[k - 1 - j:k - j, :] * _shift_up(dy, nxt, j)
    return dx


def _conv_wgrad(dy, shifted):
    k = len(shifted)
    return jnp.concatenate([_colsum(dy * shifted[k - 1 - i]) for i in range(k)], axis=0)


MM_TILE_PREF = 1408
MM_VMEM_BUDGET = 40 * 1024 * 1024


def _mm_tiles(m, n, k, out_bytes):
    tm, tn = _tile(m, MM_TILE_PREF), _tile(n, MM_TILE_PREF)
    need = lambda tm, tn: 2 * (2 * k * (tm + tn) + out_bytes * tm * tn)
    while need(tm, tn) > MM_VMEM_BUDGET:
        if tn >= tm and tn > LANES:
            tn = _tile(n, tn - LANES)
        else:
            tm = _tile(m, tm - LANES)
    return tm, tn


def _mm(a, b, dims, name, acc=None, out_dtype=F32, after=None):
    if dims == "tn":
        k, m = a.shape
    else:
        m, k = a.shape
    n = b.shape[0] if dims == "nt" else b.shape[1]
    tm, tn = _mm_tiles(m, n, k, 4 * (2 if acc is not None else 1))
    a_spec = pl.BlockSpec((k, tm), lambda j, i: (0, i)) if dims == "tn" else pl.BlockSpec((tm, k), lambda j, i: (i, 0))
    b_spec = pl.BlockSpec((tn, k), lambda j, i: (j, 0)) if dims == "nt" else pl.BlockSpec((k, tn), lambda j, i: (0, j))
    o_spec = pl.BlockSpec((tm, tn), lambda j, i: (i, j))
    dot = {"nn": _dot, "nt": _dot_nt, "tn": _dot_tn}[dims]

    def body(a_ref, b_ref, *rest):
        r = dot(a_ref[...], b_ref[...])
        if acc is not None:
            r = r + rest[0][...]
        rest[-1][...] = r.astype(out_dtype)

    ins, specs = [a, b], [a_spec, b_spec]
    if acc is not None:
        ins.append(acc)
        specs.append(o_spec)
    if after is not None:
        ins.append(after)
        specs.append(pl.BlockSpec(memory_space=pl.ANY))
    return pl.pallas_call(
        body, name=name, grid=(n // tn, m // tm), in_specs=specs, out_specs=o_spec,
        out_shape=jax.ShapeDtypeStruct((m, n), out_dtype), compiler_params=_params(2),
    )(*ins)


def _wgrad(a, d, name):
    return _mm(a, d, "tn", name, out_dtype=BF16)


def _norm_fwd(x, w, name, after=None, tm=512):
    t, d = x.shape

    def body(x_ref, w_ref, *rest):
        rest[-1][...] = _rms(x_ref[...], w_ref[...]).astype(BF16)

    extra, extra_specs = ([after], [_full(after.shape)]) if after is not None else ([], [])
    return pl.pallas_call(
        body, name=name, grid=(t // tm,), in_specs=[_rows(tm, d), _full((1, d))] + extra_specs,
        out_specs=_rows(tm, d), out_shape=jax.ShapeDtypeStruct((t, d), BF16), compiler_params=_params(1),
    )(x, w, *extra)


def _conv_a_fwd(xbc, cw, cb, tm=256):
    t, c = xbc.shape

    def body(x_ref, h_ref, w_ref, b_ref, o_ref):
        halo = jnp.where(pl.program_id(0) > 0, h_ref[...], 0.0)
        y, _ = _causal_conv(x_ref[...], halo, w_ref[...], b_ref[...])
        o_ref[...] = _silu(y)

    return pl.pallas_call(
        body, name="conv_a_fwd", grid=(t // tm,),
        in_specs=[_rows(tm, c), _halo(tm, c), _full(cw.shape), _full((1, c))], out_specs=_rows(tm, c),
        out_shape=jax.ShapeDtypeStruct((t, c), F32), compiler_params=_params(1),
    )(xbc, xbc, cw, cb)


def _ssd_common(dtr, dtb, alog):
    row = lax.broadcasted_iota(jnp.int32, (CHUNK, CHUNK), 0)
    col = lax.broadcasted_iota(jnp.int32, (CHUNK, CHUNK), 1)
    causal = row >= col
    dt = _softplus(dtr + dtb)
    a = -jnp.exp(alog)
    acum = jnp.dot(causal.astype(F32), dt * a, precision=HIGHEST, preferred_element_type=F32)
    return dt, a, acum, acum.T, causal, col < SSD_HEAD_DIM, row


def _pair_terms(j, dt, acum, acum_t, causal, lane_lo):
    h0, h1 = 2 * j, 2 * j + 1
    ac0, ac1 = acum[:, h0:h0 + 1], acum[:, h1:h1 + 1]
    l0 = jnp.exp(jnp.where(causal, ac0 - acum_t[h0:h0 + 1, :], -jnp.inf))
    l1 = jnp.exp(jnp.where(causal, ac1 - acum_t[h1:h1 + 1, :], -jnp.inf))
    dtp = jnp.where(lane_lo, dt[:, h0:h0 + 1], dt[:, h1:h1 + 1])
    al0, al1 = acum[CHUNK - 1:CHUNK, h0:h0 + 1], acum[CHUNK - 1:CHUNK, h1:h1 + 1]
    ecol = jnp.where(lane_lo, jnp.exp(ac0), jnp.exp(ac1))
    dsr = jnp.where(lane_lo, jnp.exp(al0 - ac0), jnp.exp(al1 - ac1))
    elast = jnp.where(lane_lo[0:1], jnp.exp(al0), jnp.exp(al1))
    return l0, l1, dtp, ecol, dsr, elast


def _ssd_fwd(xc, dtr, z, dtb, alog, dsk, nw):
    t = xc.shape[0]
    nc = t // CHUNK

    def body(xs_ref, b_ref, c_ref, dtr_ref, z_ref, dtb_ref, alog_ref, dsk_ref, nw_ref, y_ref, ya_ref, sp_ref, s_scr):
        @pl.when(pl.program_id(0) == 0)
        def _():
            s_scr[...] = jnp.zeros_like(s_scr)

        dt, a, acum, acum_t, causal, lane_lo, _ = _ssd_common(dtr_ref[...], dtb_ref[...], alog_ref[...])
        dsk = dsk_ref[...]
        for g in range(SSD_GROUPS):
            gs = slice(g * SSD_STATE, (g + 1) * SSD_STATE)
            bg, cg = b_ref[:, gs].astype(BF16), c_ref[:, gs].astype(BF16)
            cb = _dot_nt(cg, bg)
            for pp in range(PAIRS_PER_GROUP):
                j = g * PAIRS_PER_GROUP + pp
                ps = slice(j * LANES, (j + 1) * LANES)
                x = xs_ref[:, ps]
                l0, l1, dtp, ecol, dsr, elast = _pair_terms(j, dt, acum, acum_t, causal, lane_lo)
                xdt = x * dtp
                xb = xdt.astype(BF16)
                zero = jnp.zeros_like(xb)
                yd = (_dot((cb * l0).astype(BF16), jnp.where(lane_lo, xb, zero))
                      + _dot((cb * l1).astype(BF16), jnp.where(lane_lo, zero, xb)))
                sp = s_scr[j]
                yo = ecol * _dot(cg, sp.astype(BF16))
                st = _dot_tn(bg, (xdt * dsr).astype(BF16))
                sp_ref[0, j] = sp
                s_scr[j] = elast * sp + st
                dskp = jnp.where(lane_lo[0:1], dsk[:, 2 * j:2 * j + 1], dsk[:, 2 * j + 1:2 * j + 2])
                y_ref[:, ps] = yd + yo + dskp * x
        ya_ref[...] = _rms(y_ref[...] * _silu(z_ref[...]), nw_ref[...]).astype(BF16)

    ck = lambda n, col=0: pl.BlockSpec((CHUNK, n), lambda c: (c, col))
    return pl.pallas_call(
        body, name="ssd_fwd", grid=(nc,),
        in_specs=[ck(SSD_INNER), ck(SSD_BC, SSD_INNER // SSD_BC), ck(SSD_BC, SSD_INNER // SSD_BC + 1), ck(DT_PAD),
                  ck(SSD_INNER), _full((1, DT_PAD)), _full((1, DT_PAD)), _full((1, DT_PAD)), _full((1, SSD_INNER))],
        out_specs=[ck(SSD_INNER), ck(SSD_INNER),
                   pl.BlockSpec((1, N_PAIRS, SSD_STATE, LANES), lambda c: (c, 0, 0, 0))],
        out_shape=[jax.ShapeDtypeStruct((t, SSD_INNER), F32), jax.ShapeDtypeStruct((t, SSD_INNER), BF16),
                   jax.ShapeDtypeStruct((nc, N_PAIRS, SSD_STATE, LANES), F32)],
        scratch_shapes=[pltpu.VMEM((N_PAIRS, SSD_STATE, LANES), F32)], compiler_params=_params(1),
    )(xc, xc, xc, dtr, z, dtb, alog, dsk, nw)


def _ssd_bwd(dya, y, z, xc, dtr, sprev, dtb, alog, dsk, nw, e_heads):
    t = xc.shape[0]
    nc = t // CHUNK

    def body(dya_ref, y_ref, z_ref, xs_ref, b_ref, c_ref, dtr_ref, sp_ref, dtb_ref, alog_ref, dsk_ref, nw_ref, e_ref,
             dz_ref, dxs_ref, db_ref, dc_ref, ddtr_ref, dnw_ref, ddtb_ref, dalog_ref, ddsk_ref, ds_scr):
        @pl.when(pl.program_id(0) == 0)
        def _():
            ds_scr[...] = jnp.zeros_like(ds_scr)
            for r in (dnw_ref, ddtb_ref, dalog_ref, ddsk_ref):
                r[...] = jnp.zeros_like(r)

        y = y_ref[...]
        _, gate_vjp = jax.vjp(lambda y_, z_, w_: _rms(y_ * _silu(z_), w_), y, z_ref[...], nw_ref[...])
        dy, dz, dnw = gate_vjp(dya_ref[...])
        dz_ref[...] = dz.astype(BF16)
        dnw_ref[...] += dnw

        dtr = dtr_ref[...]
        dt, a, acum, acum_t, causal, lane_lo, row = _ssd_common(dtr, dtb_ref[...], alog_ref[...])
        dsk = dsk_ref[...]
        p_a, p_dt, v_last = [], [], []
        col = lax.broadcasted_iota(jnp.int32, (CHUNK, CHUNK), 1)
        da_cols = jnp.zeros((CHUNK, CHUNK), F32)
        da_rows = jnp.zeros((CHUNK, CHUNK), F32)
        for g in range(SSD_GROUPS):
            gs = slice(g * SSD_STATE, (g + 1) * SSD_STATE)
            bg, cg = b_ref[:, gs].astype(BF16), c_ref[:, gs].astype(BF16)
            cb = _dot_nt(cg, bg)
            dcb = jnp.zeros((CHUNK, CHUNK), F32)
            dbg = jnp.zeros((CHUNK, SSD_STATE), F32)
            dcg = jnp.zeros((CHUNK, SSD_STATE), F32)
            for pp in range(PAIRS_PER_GROUP):
                j = g * PAIRS_PER_GROUP + pp
                ps = slice(j * LANES, (j + 1) * LANES)
                x = xs_ref[:, ps]
                l0, l1, dtp, ecol, dsr, elast = _pair_terms(j, dt, acum, acum_t, causal, lane_lo)
                xdt = x * dtp
                xb = xdt.astype(BF16)
                dskp = jnp.where(lane_lo[0:1], dsk[:, 2 * j:2 * j + 1], dsk[:, 2 * j + 1:2 * j + 2])
                dyp = dy[:, ps]
                dyb = dyp.astype(BF16)
                sp, dsn = sp_ref[0, j], ds_scr[j]
                spb, dsnb = sp.astype(BF16), dsn.astype(BF16)
                y_off = ecol * _dot(cg, spb)
                dw = (dyp * ecol).astype(BF16)
                dcg = dcg + _dot_nt(dw, spb)
                dsp = _dot_tn(cg, dw) + elast * dsn
                xd = xdt * dsr
                zd = _dot(bg, dsnb) * dsr
                dbg = dbg + _dot_nt(xd.astype(BF16), dsnb)
                dxdt = zd
                zero = jnp.zeros_like(xb)
                for h, lm, le in ((2 * j, lane_lo, l0), (2 * j + 1, jnp.logical_not(lane_lo), l1)):
                    dm = _dot_nt(jnp.where(lm, dyb, zero), jnp.where(lm, xb, zero))
                    dcb = dcb + dm * le
                    m = cb * le
                    dxdt = dxdt + jnp.where(lm, _dot_tn(m.astype(BF16), dyb), 0.0)
                    q = dm * m
                    da_cols = da_cols + jnp.where(col == h, jnp.sum(q, axis=1, keepdims=True), 0.0)
                    da_rows = da_rows + jnp.where(row == h, _colsum(q), 0.0)
                ds_scr[j] = dsp
                dxs_ref[:, ps] = dxdt * dtp + dskp * dyp
                p_a.append(dyp * y_off - xdt * zd)
                p_dt.append(dxdt * x)
                v_last.append(_colsum(zd * xdt) + elast * _colsum(dsn * sp))
            dcbb = dcb.astype(BF16)
            db_ref[:, gs] = dbg + _dot_tn(dcbb, cg)
            dc_ref[:, gs] = dcg + _dot(dcbb, bg)
        e = e_ref[...]
        rows8 = jnp.concatenate([jnp.concatenate(v_last, axis=1), _colsum(dy * xs_ref[...]),
                                 jnp.zeros((6, SSD_INNER), F32)], axis=0)
        r8 = _dot_split(rows8, e)
        da = (_dot_split(jnp.concatenate(p_a, axis=1), e) + jnp.where(row == CHUNK - 1, r8[0:1], 0.0)
              + da_cols - da_rows.T)
        ddsk_ref[...] += r8[1:2]
        dadt = jnp.dot((row <= col).astype(F32), da, precision=HIGHEST, preferred_element_type=F32)
        ddt = dadt * a + _dot_split(jnp.concatenate(p_dt, axis=1), e)
        dalog_ref[...] += _colsum(dadt * dt) * a
        ddtr = ddt * _sigmoid(dtr + dtb_ref[...])
        ddtr_ref[...] = ddtr
        ddtb_ref[...] += _colsum(ddtr)

    ck = lambda n, col=0: pl.BlockSpec((CHUNK, n), lambda c: (nc - 1 - c, col))
    acc = lambda n: _full((1, n))
    return pl.pallas_call(
        body, name="ssd_bwd", grid=(nc,),
        in_specs=[ck(SSD_INNER), ck(SSD_INNER), ck(SSD_INNER), ck(SSD_INNER), ck(SSD_BC, SSD_INNER // SSD_BC),
                  ck(SSD_BC, SSD_INNER // SSD_BC + 1), ck(DT_PAD),
                  pl.BlockSpec((1, N_PAIRS, SSD_STATE, LANES), lambda c: (nc - 1 - c, 0, 0, 0)),
                  acc(DT_PAD), acc(DT_PAD), acc(DT_PAD), acc(SSD_INNER), _full((SSD_INNER, LANES))],
        out_specs=[ck(SSD_INNER), ck(SSD_INNER), ck(SSD_BC), ck(SSD_BC), ck(DT_PAD),
                   acc(SSD_INNER), acc(DT_PAD), acc(DT_PAD), acc(DT_PAD)],
        out_shape=[jax.ShapeDtypeStruct((t, SSD_INNER), BF16), jax.ShapeDtypeStruct((t, SSD_INNER), F32),
                   jax.ShapeDtypeStruct((t, SSD_BC), F32), jax.ShapeDtypeStruct((t, SSD_BC), F32),
                   jax.ShapeDtypeStruct((t, DT_PAD), F32), jax.ShapeDtypeStruct((1, SSD_INNER), F32),
                   jax.ShapeDtypeStruct((1, DT_PAD), F32), jax.ShapeDtypeStruct((1, DT_PAD), F32),
                   jax.ShapeDtypeStruct((1, DT_PAD), F32)],
        scratch_shapes=[pltpu.VMEM((N_PAIRS, SSD_STATE, LANES), F32)], compiler_params=_params(1),
    )(dya, y, z, xc, xc, xc, dtr, sprev, dtb, alog, dsk, nw, e_heads)


def _sgu_act(uv, uvb, lnw, lnb):
    a = _gelu(uv + uvb)
    return a[:, :SGU_WIDTH], _layer_norm(a[:, SGU_WIDTH:], lnw, lnb)


def _sgu_weights(ws_ref):
    row = lax.broadcasted_iota(jnp.int32, (CHUNK, CHUNK), 0)
    col = lax.broadcasted_iota(jnp.int32, (CHUNK, CHUNK), 1)
    return [jnp.where(row >= col, ws_ref[g], 0.0).astype(BF16) for g in range(SGU_GROUPS)], row >= col


def _sgu_fwd(uv, uvb, lnw, lnb, ws, bs_t):
    t = uv.shape[0]

    def body(uv_ref, uvb_ref, lnw_ref, lnb_ref, ws_ref, bs_ref, o_ref):
        u, vn = _sgu_act(uv_ref[...], uvb_ref[...], lnw_ref[...], lnb_ref[...])
        wc, _ = _sgu_weights(ws_ref)
        bs = bs_ref[...]
        for g in range(SGU_GROUPS):
            gs = slice(g * LANES, (g + 1) * LANES)
            mixed = _dot(wc[g], vn[:, gs].astype(BF16)) + bs[:, g:g + 1]
            o_ref[:, gs] = (u[:, gs] * mixed).astype(BF16)

    return pl.pallas_call(
        body, name="sgu_fwd", grid=(t // CHUNK,),
        in_specs=[_rows(CHUNK, 2 * SGU_WIDTH), _full((1, 2 * SGU_WIDTH)), _full((1, SGU_WIDTH)), _full((1, SGU_WIDTH)),
                  _full(ws.shape), _full(bs_t.shape)],
        out_specs=_rows(CHUNK, SGU_WIDTH), out_shape=jax.ShapeDtypeStruct((t, SGU_WIDTH), BF16),
        compiler_params=_params(1),
    )(uv, uvb, lnw, lnb, ws, bs_t)


def _sgu_bwd(dyb, uv, uvb, lnw, lnb, ws, bs_t, e_groups):
    t = uv.shape[0]

    def body(dyb_ref, uv_ref, uvb_ref, lnw_ref, lnb_ref, ws_ref, bs_ref, e_ref,
             duv_ref, duvb_ref, dlnw_ref, dlnb_ref, dws_ref, dbs_ref):
        @pl.when(pl.program_id(0) == 0)
        def _():
            for r in (duvb_ref, dlnw_ref, dlnb_ref, dws_ref, dbs_ref):
                r[...] = jnp.zeros_like(r)

        (u, vn), act_vjp = jax.vjp(_sgu_act, uv_ref[...], uvb_ref[...], lnw_ref[...], lnb_ref[...])
        wc, causal = _sgu_weights(ws_ref)
        bs = bs_ref[...]
        dyb = dyb_ref[...]
        du, dvn, dmix = [], [], []
        for g in range(SGU_GROUPS):
            gs = slice(g * LANES, (g + 1) * LANES)
            vb = vn[:, gs].astype(BF16)
            mixed = _dot(wc[g], vb) + bs[:, g:g + 1]
            dm = dyb[:, gs] * u[:, gs]
            dmb = dm.astype(BF16)
            du.append(dyb[:, gs] * mixed)
            dvn.append(_dot_tn(wc[g], dmb))
            dws_ref[g] += jnp.where(causal, _dot_nt(dmb, vb), 0.0)
            dmix.append(dm)
        dbs_ref[...] += _dot_split(jnp.concatenate(dmix, axis=1), e_ref[...])
        duv, duvb, dlnw, dlnb = act_vjp((jnp.concatenate(du, axis=1), jnp.concatenate(dvn, axis=1)))
        duv_ref[...] = duv.astype(BF16)
        duvb_ref[...] += duvb
        dlnw_ref[...] += dlnw
        dlnb_ref[...] += dlnb

    return pl.pallas_call(
        body, name="sgu_bwd", grid=(t // CHUNK,),
        in_specs=[_rows(CHUNK, SGU_WIDTH), _rows(CHUNK, 2 * SGU_WIDTH), _full((1, 2 * SGU_WIDTH)),
                  _full((1, SGU_WIDTH)), _full((1, SGU_WIDTH)), _full(ws.shape), _full(bs_t.shape),
                  _full(e_groups.shape)],
        out_specs=[_rows(CHUNK, 2 * SGU_WIDTH), _full((1, 2 * SGU_WIDTH)), _full((1, SGU_WIDTH)),
                   _full((1, SGU_WIDTH)), _full(ws.shape), _full(bs_t.shape)],
        out_shape=[jax.ShapeDtypeStruct((t, 2 * SGU_WIDTH), BF16), jax.ShapeDtypeStruct((1, 2 * SGU_WIDTH), F32),
                   jax.ShapeDtypeStruct((1, SGU_WIDTH), F32), jax.ShapeDtypeStruct((1, SGU_WIDTH), F32),
                   jax.ShapeDtypeStruct(ws.shape, F32), jax.ShapeDtypeStruct(bs_t.shape, F32)],
        compiler_params=_params(1),
    )(dyb, uv, uvb, lnw, lnb, ws, bs_t, e_groups)


def _merge(gates, pa, pb, bg):
    s = _sigmoid(gates + bg)
    return s[:, :D_MODEL] * pa + s[:, D_MODEL:] * pb


def _merge_fwd(gates, pa, pb, bg, tm=256):
    t = gates.shape[0]

    def body(g_ref, pa_ref, pb_ref, bg_ref, o_ref):
        o_ref[...] = _merge(g_ref[...], pa_ref[...], pb_ref[...], bg_ref[...]).astype(BF16)

    return pl.pallas_call(
        body, name="merge_fwd", grid=(t // tm,),
        in_specs=[_rows(tm, 2 * D_MODEL), _rows(tm, D_MODEL), _rows(tm, D_MODEL), _full((1, 2 * D_MODEL))],
        out_specs=_rows(tm, D_MODEL), out_shape=jax.ShapeDtypeStruct((t, D_MODEL), BF16), compiler_params=_params(1),
    )(gates, pa, pb, bg)


def _merge_bwd(dmix, gates, pa, pb, bg, tm=256):
    t = gates.shape[0]

    def body(d_ref, g_ref, pa_ref, pb_ref, bg_ref, dg_ref, dpa_ref, dpb_ref, dbg_ref):
        @pl.when(pl.program_id(0) == 0)
        def _():
            dbg_ref[...] = jnp.zeros_like(dbg_ref)

        _, vjp = jax.vjp(_merge, g_ref[...], pa_ref[...], pb_ref[...], bg_ref[...])
        dg, dpa, dpb, dbg = vjp(d_ref[...])
        dg_ref[...] = dg.astype(BF16)
        dpa_ref[...] = dpa.astype(BF16)
        dpb_ref[...] = dpb.astype(BF16)
        dbg_ref[...] += dbg

    return pl.pallas_call(
        body, name="merge_bwd", grid=(t // tm,),
        in_specs=[_rows(tm, D_MODEL), _rows(tm, 2 * D_MODEL), _rows(tm, D_MODEL), _rows(tm, D_MODEL),
                  _full((1, 2 * D_MODEL))],
        out_specs=[_rows(tm, 2 * D_MODEL), _rows(tm, D_MODEL), _rows(tm, D_MODEL), _full((1, 2 * D_MODEL))],
        out_shape=[jax.ShapeDtypeStruct((t, 2 * D_MODEL), BF16), jax.ShapeDtypeStruct((t, D_MODEL), BF16),
                   jax.ShapeDtypeStruct((t, D_MODEL), BF16), jax.ShapeDtypeStruct((1, 2 * D_MODEL), F32)],
        compiler_params=_params(1),
    )(dmix, gates, pa, pb, bg)


def _residual_norm_fwd(x, o, w, tm=512):
    t, d = x.shape

    def body(x_ref, o_ref, w_ref, h_ref, n_ref):
        h = x_ref[...] + o_ref[...]
        h_ref[...] = h
        n_ref[...] = _rms(h, w_ref[...]).astype(BF16)

    return pl.pallas_call(
        body, name="residual_norm_fwd", grid=(t // tm,), in_specs=[_rows(tm, d), _rows(tm, d), _full((1, d))],
        out_specs=[_rows(tm, d), _rows(tm, d)],
        out_shape=[jax.ShapeDtypeStruct((t, d), F32), jax.ShapeDtypeStruct((t, d), BF16)], compiler_params=_params(1),
    )(x, o, w)


def _norm_bwd(dn, h, w, dres, name, tm=512):
    t, d = h.shape

    def body(dn_ref, h_ref, w_ref, dres_ref, dh_ref, dhb_ref, dw_ref):
        @pl.when(pl.program_id(0) == 0)
        def _():
            dw_ref[...] = jnp.zeros_like(dw_ref)

        _, vjp = jax.vjp(_rms, h_ref[...], w_ref[...])
        dh, dw = vjp(dn_ref[...])
        dh = dh + dres_ref[...]
        dh_ref[...] = dh
        dhb_ref[...] = dh.astype(BF16)
        dw_ref[...] += dw

    return pl.pallas_call(
        body, name=name, grid=(t // tm,), in_specs=[_rows(tm, d), _rows(tm, d), _full((1, d)), _rows(tm, d)],
        out_specs=[_rows(tm, d), _rows(tm, d), _full((1, d))],
        out_shape=[jax.ShapeDtypeStruct((t, d), F32), jax.ShapeDtypeStruct((t, d), BF16),
                   jax.ShapeDtypeStruct((1, d), F32)], compiler_params=_params(1),
    )(dn, h, w, dres)


def _conv_f_fwd(up, cw, cb, tm=128):
    t, c = up.shape

    def body(x_ref, h_ref, w_ref, b_ref, o_ref):
        halo = jnp.where(pl.program_id(0) > 0, h_ref[...], 0.0)
        y, _ = _causal_conv(x_ref[...], halo, w_ref[...], b_ref[...])
        o_ref[...] = (_silu(y[:, :D_FF]) * y[:, D_FF:]).astype(BF16)

    return pl.pallas_call(
        body, name="conv_f_fwd", grid=(t // tm,),
        in_specs=[_rows(tm, c), _halo(tm, c), _full(cw.shape), _full((1, c))], out_specs=_rows(tm, D_FF),
        out_shape=jax.ShapeDtypeStruct((t, D_FF), BF16), compiler_params=_params(1),
    )(up, up, cw, cb)


def _conv_f_bwd(dact, up, cw, cb, tm=128):
    t, c = up.shape
    nt = t // tm

    def body(d_ref, x_ref, h_ref, w_ref, b_ref, dx_ref, dw_ref, db_ref, nxt_scr):
        @pl.when(pl.program_id(0) == 0)
        def _():
            nxt_scr[...] = jnp.zeros_like(nxt_scr)
            dw_ref[...] = jnp.zeros_like(dw_ref)
            db_ref[...] = jnp.zeros_like(db_ref)

        halo = jnp.where(pl.program_id(0) < nt - 1, h_ref[...], 0.0)
        w = w_ref[...]
        y, shifted = _causal_conv(x_ref[...], halo, w, b_ref[...])
        a, v = y[:, :D_FF], y[:, D_FF:]
        d = d_ref[...]
        dy = jnp.concatenate([d * v * _dsilu(a), d * _silu(a)], axis=1)
        dx_ref[...] = _anticausal_conv(dy, nxt_scr[...], w).astype(BF16)
        nxt_scr[...] = dy[:8]
        dw_ref[...] += _conv_wgrad(dy, shifted)
        db_ref[...] += _colsum(dy)

    return pl.pallas_call(
        body, name="conv_f_bwd", grid=(nt,),
        in_specs=[_rows(tm, D_FF, nt, True), _rows(tm, c, nt, True), _halo(tm, c, nt, True), _full(cw.shape),
                  _full((1, c))],
        out_specs=[_rows(tm, c, nt, True), _full(cw.shape), _full((1, c))],
        out_shape=[jax.ShapeDtypeStruct((t, c), BF16), jax.ShapeDtypeStruct(cw.shape, F32),
                   jax.ShapeDtypeStruct((1, c), F32)],
        scratch_shapes=[pltpu.VMEM((8, c), F32)], compiler_params=_params(1),
    )(dact, up, up, cw, cb)


def _conv_a_bwd(dxs, db, dc, xbc, cw, cb, tm=256):
    t, c = xbc.shape
    nt = t // tm

    def body(dxs_ref, db_ref, dc_ref, x_ref, h_ref, w_ref, b_ref, dx_ref, dw_ref, dbias_ref, nxt_scr):
        @pl.when(pl.program_id(0) == 0)
        def _():
            nxt_scr[...] = jnp.zeros_like(nxt_scr)
            dw_ref[...] = jnp.zeros_like(dw_ref)
            dbias_ref[...] = jnp.zeros_like(dbias_ref)

        halo = jnp.where(pl.program_id(0) < nt - 1, h_ref[...], 0.0)
        w = w_ref[...]
        y, shifted = _causal_conv(x_ref[...], halo, w, b_ref[...])
        dy = jnp.concatenate([dxs_ref[...], db_ref[...], dc_ref[...]], axis=1) * _dsilu(y)
        dx_ref[...] = _anticausal_conv(dy, nxt_scr[...], w).astype(BF16)
        nxt_scr[...] = dy[:8]
        dw_ref[...] += _conv_wgrad(dy, shifted)
        dbias_ref[...] += _colsum(dy)

    return pl.pallas_call(
        body, name="conv_a_bwd", grid=(nt,),
        in_specs=[_rows(tm, SSD_INNER, nt, True), _rows(tm, SSD_BC, nt, True), _rows(tm, SSD_BC, nt, True),
                  _rows(tm, c, nt, True), _halo(tm, c, nt, True), _full(cw.shape), _full((1, c))],
        out_specs=[_rows(tm, c, nt, True), _full(cw.shape), _full((1, c))],
        out_shape=[jax.ShapeDtypeStruct((t, c), BF16), jax.ShapeDtypeStruct(cw.shape, F32),
                   jax.ShapeDtypeStruct((1, c), F32)],
        scratch_shapes=[pltpu.VMEM((8, c), F32)], compiler_params=_params(1),
    )(dxs, db, dc, xbc, xbc, cw, cb)


def _loss_head(h1, dn, w, target, tm=512):
    t, d = h1.shape

    def body(h_ref, dn_ref, w_ref, t_ref, loss_ref, dh_ref, dhb_ref, dw_ref):
        @pl.when(pl.program_id(0) == 0)
        def _():
            loss_ref[...] = jnp.zeros_like(loss_ref)
            dw_ref[...] = jnp.zeros_like(dw_ref)

        yf, vjp = jax.vjp(_rms, h_ref[...] + dn_ref[...], w_ref[...])
        err = yf - t_ref[...]
        loss_ref[...] += 0.5 * jnp.sum(jnp.mean(err * err, axis=-1, keepdims=True))
        dh, dw = vjp(err * (1.0 / d))
        dh_ref[...] = dh
        dhb_ref[...] = dh.astype(BF16)
        dw_ref[...] += dw

    return pl.pallas_call(
        body, name="loss_head", grid=(t // tm,),
        in_specs=[_rows(tm, d), _rows(tm, d), _full((1, d)), _rows(tm, d)],
        out_specs=[_full((8, LANES)), _rows(tm, d), _rows(tm, d), _full((1, d))],
        out_shape=[jax.ShapeDtypeStruct((8, LANES), F32), jax.ShapeDtypeStruct((t, d), F32),
                   jax.ShapeDtypeStruct((t, d), BF16), jax.ShapeDtypeStruct((1, d), F32)], compiler_params=_params(1),
    )(h1, dn, w, target)


def _pad_lanes(v, n=DT_PAD):
    return jnp.pad(v, ((0, 0), (0, n - v.shape[1])))


def _local_step(x, target, w, p, after=None, late_weights=None, on_grad=None):
    dtb, alog, dsk = _pad_lanes(p["dt_bias"]), _pad_lanes(p["a_log"]), _pad_lanes(p["d_skip"])
    bs_t = _pad_lanes(p["b_spatial"].T)
    e_heads = (jnp.arange(SSD_INNER)[:, None] // SSD_HEAD_DIM == jnp.arange(LANES)[None, :]).astype(BF16)
    e_groups = (jnp.arange(SGU_WIDTH)[:, None] // LANES == jnp.arange(LANES)[None, :]).astype(BF16)

    n1 = _norm_fwd(x, p["norm1_w"], "norm1_fwd", after=after)
    z = _mm(n1, w["z"], "nn", "proj_z")
    xbc = _mm(n1, w["xbc"], "nn", "proj_xbc")
    dtr = _mm(n1, w["dt"], "nn", "proj_dt")
    uv = _mm(n1, w["uv"], "nn", "proj_uv")
    gates = _mm(n1, w["gates"], "nn", "proj_gates")
    xc = _conv_a_fwd(xbc, w["conv_a"], p["conv_a_b"])
    y, ya, sprev = _ssd_fwd(xc, dtr, z, dtb, alog, dsk, p["ssd_norm_w"])
    yb = _sgu_fwd(uv, p["uv_b"], p["v_ln_w"], p["v_ln_b"], p["w_spatial"], bs_t)
    if late_weights is not None:
        w = {**w, **late_weights(ya, yb)}
    pa = _mm(ya, w["branch_a"], "nn", "branch_a")
    pb = _mm(yb, w["branch_b"], "nn", "branch_b")
    mix = _merge_fwd(gates, pa, pb, p["b_gate"])
    o = _mm(mix, w["out"], "nn", "out_proj")
    h1, n2 = _residual_norm_fwd(x, o, p["norm2_w"])
    up = _mm(n2, w["up"], "nn", "up_proj")
    act = _conv_f_fwd(up, w["conv_f"], p["conv_f_b"])
    dn = _mm(act, w["down"], "nn", "down_proj")
    loss, dh2, dh2b, g_final = _loss_head(h1, dn, p["final_norm_w"], target)

    on_grad = on_grad or (lambda name, grads: None)
    g = {"final_norm_w": g_final}
    g["down"] = _wgrad(act, dh2b, "down_wgrad")
    tok = on_grad("w_down", g)
    dact = _mm(dh2b, w["down"], "nt", "down_dgrad", after=tok)
    dup, g["conv_f"], g["conv_f_b"] = _conv_f_bwd(dact, up, w["conv_f"], p["conv_f_b"])
    g["up"] = _wgrad(n2, dup, "up_wgrad")
    tok = on_grad("w_up", g)
    dn2 = _mm(dup, w["up"], "nt", "up_dgrad", after=tok)
    dh1, dh1b, g["norm2_w"] = _norm_bwd(dn2, h1, p["norm2_w"], dh2, "norm2_bwd")
    g["out"] = _wgrad(mix, dh1b, "out_wgrad")
    tok = on_grad("w_out", g)
    dmix = _mm(dh1b, w["out"], "nt", "out_dgrad", after=tok)
    dgates, dpa, dpb, g["b_gate"] = _merge_bwd(dmix, gates, pa, pb, p["b_gate"])
    g["branch_a"] = _wgrad(ya, dpa, "branch_a_wgrad")
    g["branch_b"] = _wgrad(yb, dpb, "branch_b_wgrad")
    tok = on_grad("w_branch", g)
    dya = _mm(dpa, w["branch_a"], "nt", "branch_a_dgrad", after=tok)
    dyb = _mm(dpb, w["branch_b"], "nt", "branch_b_dgrad", after=tok)
    duv, g["uv_b"], g["v_ln_w"], g["v_ln_b"], g["w_spatial"], dbs_t = _sgu_bwd(
        dyb, uv, p["uv_b"], p["v_ln_w"], p["v_ln_b"], p["w_spatial"], bs_t, e_groups)
    g["b_spatial"] = dbs_t[:, :SGU_GROUPS].T
    dz, dxs, db, dc, ddtr, g["ssd_norm_w"], ddtb, dalog, ddsk = _ssd_bwd(
        dya, y, z, xc, dtr, sprev, dtb, alog, dsk, p["ssd_norm_w"], e_heads)
    g["dt_bias"], g["a_log"], g["d_skip"] = ddtb[:, :SSD_HEADS], dalog[:, :SSD_HEADS], ddsk[:, :SSD_HEADS]
    dxbc, g["conv_a"], g["conv_a_b"] = _conv_a_bwd(dxs, db, dc, xbc, w["conv_a"], p["conv_a_b"])
    ddtrb = ddtr.astype(BF16)
    for name, d in (("z", dz), ("xbc", dxbc), ("dt", ddtrb), ("uv", duv), ("gates", dgates)):
        g[name] = _wgrad(n1, d, name + "_wgrad")
    tok = on_grad("w_in", g)
    dn1 = _mm(dz, w["z"], "nt", "z_dgrad", after=tok)
    dn1 = _mm(dxbc, w["xbc"], "nt", "xbc_dgrad", acc=dn1)
    dn1 = _mm(ddtrb, w["dt"], "nt", "dt_dgrad", acc=dn1)
    dn1 = _mm(duv, w["uv"], "nt", "uv_dgrad", acc=dn1)
    dn1 = _mm(dgates, w["gates"], "nt", "gates_dgrad", acc=dn1)
    gx, _, g["norm1_w"] = _norm_bwd(dn1, x, p["norm1_w"], dh1, "norm1_bwd")
    return loss, gx, g


def _place():
    return lax.axis_index("x"), lax.axis_index("y"), lax.axis_index("c")


def _other_chips(x, y):
    return [(1 - x, y), (x, 1 - y), (1 - x, 1 - y)]


def _all_gather(shards, name):
    n = len(shards)

    def body(*refs):
        ins, outs = refs[:n], refs[n:2 * n]
        send_sems, recv_sems, local_sems = refs[2 * n:]
        x, y, c = _place()
        me, sibling = (x, y, c), (x, y, 1 - c)
        chips = _other_chips(x, y)

        def copy(a, k, block, to, src=None):
            slot = outs[a].at[4 * block[0] + 2 * block[1] + block[2]]
            return pltpu.make_async_remote_copy(
                src_ref=slot if src is None else src, dst_ref=slot, send_sem=send_sems.at[7 * a + k],
                recv_sem=recv_sems.at[7 * a + k], device_id=to, device_id_type=MESH)

        started = []
        for a in range(n):
            mine = pltpu.make_async_copy(ins[a], outs[a].at[4 * x + 2 * y + c], local_sems.at[a])
            mine.start()
            started.append(mine)
        sends = []
        for a in range(n):
            sends.append(copy(a, 0, me, sibling, src=ins[a]))
            sends += [copy(a, 1 + j, me, (*chip, c), src=ins[a]) for j, chip in enumerate(chips)]
        for cp in sends:
            cp.start()
        for a in range(n):
            for j, chip in enumerate(chips):
                copy(a, 1 + j, (*chip, c), me).wait_recv()
                fwd = copy(a, 4 + j, (*chip, c), sibling)
                fwd.start()
                sends.append(fwd)
        for a in range(n):
            copy(a, 0, sibling, me).wait_recv()
            for j, chip in enumerate(chips):
                copy(a, 4 + j, (*chip, 1 - c), me).wait_recv()
        for cp in sends:
            cp.wait_send()
        for mine in started:
            mine.wait()

    any_spec = pl.BlockSpec(memory_space=pl.ANY)
    return pl.pallas_call(
        body, name=name, in_specs=[any_spec] * n, out_specs=[any_spec] * n,
        out_shape=[jax.ShapeDtypeStruct((N_DEV, *s.shape), s.dtype) for s in shards],
        scratch_shapes=[pltpu.SemaphoreType.DMA((7 * n,)), pltpu.SemaphoreType.DMA((7 * n,)),
                        pltpu.SemaphoreType.DMA((n,))],
    )(*shards)


HBM_SPEC = pl.BlockSpec(memory_space=pltpu.HBM)
SEM_SPEC = pl.BlockSpec(memory_space=pltpu.SEMAPHORE)
ANY_SPEC = pl.BlockSpec(memory_space=pl.ANY)
DATAFLOW = pltpu.SideEffectType.DATAFLOW_SIDE_EFFECTING
N_PEERS = N_DEV - 1


def _peers(x, y, c):
    out = []
    for r in range(1, N_DEV):
        fx, fy, fc = r >> 2 & 1, r >> 1 & 1, r & 1
        out.append(((1 - x) if fx else x, (1 - y) if fy else y, (1 - c) if fc else c))
    return out


def _gather_copies(srcs, lands, send_sems, recv_sems, sending):
    x, y, c = _place()
    copies = []
    for a, (src, land) in enumerate(zip(srcs, lands)):
        for j, (px, py, pc) in enumerate(_peers(x, y, c)):
            slot = 4 * x + 2 * y + c if sending else 4 * px + 2 * py + pc
            copies.append(pltpu.make_async_remote_copy(
                src_ref=src, dst_ref=land.at[slot], send_sem=send_sems.at[N_PEERS * a + j],
                recv_sem=recv_sems.at[N_PEERS * a + j], device_id=(px, py, pc), device_id_type=MESH))
    return copies


def _gather_start(shards, after, name):
    n = len(shards)

    def body(*refs):
        srcs, lands = refs[:n], refs[n:2 * n]
        send_sems, recv_sems = refs[2 * n + 1:2 * n + 3]
        token = refs[-1]
        for cp in _gather_copies(srcs, lands, send_sems, recv_sems, sending=True):
            cp.start()
        token[...] = jnp.zeros_like(token)

    lands = [lax.empty((N_DEV, *s.shape), s.dtype) for s in shards]
    hbm = lambda a: pltpu.with_memory_space_constraint(a, pltpu.HBM)
    out = pl.pallas_call(
        body, name=name,
        out_shape=(pltpu.SemaphoreType.DMA((N_PEERS * n,)), pltpu.SemaphoreType.DMA((N_PEERS * n,)),
                   *[pltpu.HBM(a.shape, a.dtype) for a in (*shards, *lands)], jax.ShapeDtypeStruct((8, LANES), F32)),
        in_specs=[HBM_SPEC] * (2 * n) + [ANY_SPEC],
        out_specs=(SEM_SPEC, SEM_SPEC, *[HBM_SPEC] * (2 * n), pl.BlockSpec(memory_space=pltpu.VMEM)),
        input_output_aliases={i: 2 + i for i in range(2 * n)},
        compiler_params=pltpu.CompilerParams(has_side_effects=DATAFLOW),
    )(*[hbm(a) for a in (*shards, *lands)], after)
    return out[0], out[1], out[2:2 + n], out[2 + n:2 + 2 * n], out[-1]


def _gather_wait(send_sems, recv_sems, shards, lands, after, name):
    n = len(shards)
    after = tuple(after)

    def body(*refs):
        srcs, lands_ = refs[:n], refs[n:2 * n]
        send, recv = refs[2 * n:2 * n + 2]
        for cp in _gather_copies(srcs, lands_, send, recv, sending=False):
            cp.wait_send()
            cp.wait_recv()

    out = pl.pallas_call(
        body, name=name, out_shape=tuple(pltpu.HBM(a.shape, a.dtype) for a in (*shards, *lands)),
        in_specs=[HBM_SPEC] * (2 * n) + [SEM_SPEC, SEM_SPEC] + [ANY_SPEC] * len(after),
        out_specs=tuple([HBM_SPEC] * (2 * n)), input_output_aliases={i: i for i in range(2 * n)},
        compiler_params=pltpu.CompilerParams(has_side_effects=DATAFLOW),
    )(*shards, *lands, send_sems, recv_sems, *after)
    return out[n:]


def _chip_copies(src, land, send_sems, recv_sems):
    x, y, c = _place()
    return [pltpu.make_async_remote_copy(
        src_ref=src.at[2 * cx + cy], dst_ref=land.at[j], send_sem=send_sems.at[j], recv_sem=recv_sems.at[j],
        device_id=(cx, cy, c), device_id_type=MESH) for j, (cx, cy) in enumerate(_other_chips(x, y))]


def _chips_start(q, name):
    def body(q_ref, land_ref, send_sems, recv_sems, q_thru, land_thru, token):
        for cp in _chip_copies(q_ref, land_ref, send_sems, recv_sems):
            cp.start()
        token[...] = jnp.zeros_like(token)

    land = lax.empty((3, *q.shape[1:]), q.dtype)
    return pl.pallas_call(
        body, name=name,
        out_shape=(pltpu.SemaphoreType.DMA((3,)), pltpu.SemaphoreType.DMA((3,)), pltpu.HBM(q.shape, q.dtype),
                   pltpu.HBM(land.shape, land.dtype), jax.ShapeDtypeStruct((8, LANES), F32)),
        in_specs=[HBM_SPEC, HBM_SPEC],
        out_specs=(SEM_SPEC, SEM_SPEC, HBM_SPEC, HBM_SPEC, pl.BlockSpec(memory_space=pltpu.VMEM)),
        input_output_aliases={0: 2, 1: 3}, compiler_params=pltpu.CompilerParams(has_side_effects=DATAFLOW),
    )(pltpu.with_memory_space_constraint(q, pltpu.HBM), pltpu.with_memory_space_constraint(land, pltpu.HBM))


def _chips_wait(send_sems, recv_sems, q, land, after, name):
    def body(q_ref, land_ref, send, recv, after_ref, q_out, land_out):
        for cp in _chip_copies(q_ref, land_ref, send, recv):
            cp.wait_send()
            cp.wait_recv()

    return pl.pallas_call(
        body, name=name, out_shape=(pltpu.HBM(q.shape, q.dtype), pltpu.HBM(land.shape, land.dtype)),
        in_specs=[HBM_SPEC, HBM_SPEC, SEM_SPEC, SEM_SPEC, ANY_SPEC], out_specs=(HBM_SPEC, HBM_SPEC),
        input_output_aliases={0: 0, 1: 1}, compiler_params=pltpu.CompilerParams(has_side_effects=DATAFLOW),
    )(q, land, send_sems, recv_sems, after)[1]


def _exchange_cores(parts, name):
    n = len(parts)

    def body(*refs):
        ins, outs = refs[:n], refs[n:2 * n]
        send_sems, recv_sems = refs[2 * n:]
        x, y, c = _place()
        copies = []
        for a in range(n):
            for k in range(4):
                copies.append(pltpu.make_async_remote_copy(
                    src_ref=ins[a].at[2 * k + (1 - c)], dst_ref=outs[a].at[k], send_sem=send_sems.at[4 * a + k],
                    recv_sem=recv_sems.at[4 * a + k], device_id=(x, y, 1 - c), device_id_type=MESH))
        for cp in copies:
            cp.start()
        for cp in copies:
            cp.wait()

    any_spec = pl.BlockSpec(memory_space=pl.ANY)
    return pl.pallas_call(
        body, name=name, in_specs=[any_spec] * n, out_specs=[any_spec] * n,
        out_shape=[jax.ShapeDtypeStruct((4, *s.shape[1:]), s.dtype) for s in parts],
        scratch_shapes=[pltpu.SemaphoreType.DMA((4 * n,)), pltpu.SemaphoreType.DMA((4 * n,))],
    )(*parts)


def _chip_sum(part, got, place, name, tr=256):
    _, r, c = part.shape
    tr = _row_tile(r, tr)

    def body(place_ref, p_ref, g_ref, q_ref, own_ref):
        s = p_ref[0].astype(F32) + g_ref[0].astype(F32)
        q_ref[0] = s.astype(BF16)

        @pl.when(pl.program_id(1) == place_ref[1])
        def _():
            own_ref[...] = s

    grid_spec = pltpu.PrefetchScalarGridSpec(
        num_scalar_prefetch=1, grid=(r // tr, 4),
        in_specs=[pl.BlockSpec((1, tr, c), lambda i, k, pr: (2 * k + pr[0], i, 0)),
                  pl.BlockSpec((1, tr, c), lambda i, k, pr: (k, i, 0))],
        out_specs=[pl.BlockSpec((1, tr, c), lambda i, k, pr: (k, i, 0)),
                   pl.BlockSpec((tr, c), lambda i, k, pr: (i, 0))])
    return pl.pallas_call(
        body, name=name, grid_spec=grid_spec,
        out_shape=[jax.ShapeDtypeStruct((4, r, c), BF16), jax.ShapeDtypeStruct((r, c), F32)],
        compiler_params=_params(2),
    )(place, part, got)


def _adamw(w, g, m, v):
    m = ADAM_B1 * m + (1.0 - ADAM_B1) * g
    v = ADAM_B2 * v + (1.0 - ADAM_B2) * jnp.square(g)
    m_hat = m / (1.0 - ADAM_B1 ** ADAM_STEP)
    v_hat = v / (1.0 - ADAM_B2 ** ADAM_STEP)
    return -ADAM_LR * (m_hat / (jnp.sqrt(v_hat) + ADAM_EPS) + ADAM_WD * w), m, v


def _sum_adamw(own, got, w, m, v, name, tr=256):
    r, c = w.shape
    tr = _row_tile(r, tr)

    def body(own_ref, got_ref, w_ref, m_ref, v_ref, g_ref, d_ref, nm_ref, nv_ref):
        g = own_ref[...]
        for j in range(3):
            g = g + got_ref[j].astype(F32)
        g_ref[...] = g
        d_ref[...], nm_ref[...], nv_ref[...] = _adamw(w_ref[...], g, m_ref[...], v_ref[...])

    blk = pl.BlockSpec((tr, c), lambda i: (i, 0))
    return pl.pallas_call(
        body, name=name, grid=(r // tr,),
        in_specs=[blk, pl.BlockSpec((3, tr, c), lambda i: (0, i, 0)), blk, blk, blk], out_specs=[blk] * 4,
        out_shape=[jax.ShapeDtypeStruct((r, c), F32)] * 4, compiler_params=_params(1),
    )(own, got, w, m, v)


def _sum_devices(parts, name):
    _, r, c = parts.shape
    tr = r

    def body(p_ref, o_ref):
        s = p_ref[0]
        for d in range(1, N_DEV):
            s = s + p_ref[d]
        o_ref[...] = s

    return pl.pallas_call(
        body, name=name, grid=(pl.cdiv(r, tr),), in_specs=[pl.BlockSpec((N_DEV, tr, c), lambda i: (0, i, 0))],
        out_specs=pl.BlockSpec((tr, c), lambda i: (i, 0)), out_shape=jax.ShapeDtypeStruct((r, c), F32),
        compiler_params=_params(1),
    )(parts)


def _adamw_call(w, g, m, v, name):
    r, c = w.shape
    tr = r

    def body(w_ref, g_ref, m_ref, v_ref, d_ref, nm_ref, nv_ref):
        d_ref[...], nm_ref[...], nv_ref[...] = _adamw(w_ref[...], g_ref[...], m_ref[...], v_ref[...])

    blk = pl.BlockSpec((tr, c), lambda i: (i, 0))
    return pl.pallas_call(
        body, name=name, grid=(pl.cdiv(r, tr),), in_specs=[blk] * 4, out_specs=[blk] * 3,
        out_shape=[jax.ShapeDtypeStruct((r, c), F32)] * 3, compiler_params=_params(1),
    )(w, g, m, v)


PACK_ROWS = 8


def _pack(arrays):
    parts = []
    for a in arrays:
        flat = a.reshape(-1)
        unit = PACK_ROWS * LANES
        parts.append(jnp.pad(flat, (0, -flat.shape[0] % unit)).reshape(-1, LANES))
    return jnp.concatenate(parts, axis=0)


def _unpack(pack, shapes):
    out, row = [], 0
    for s in shapes:
        size = 1
        for d in s:
            size *= d
        rows = -(-size // (PACK_ROWS * LANES)) * PACK_ROWS
        out.append(pack[row:row + rows].reshape(-1)[:size].reshape(s))
        row += rows
    return out


SMALL = ["norm1_w", "b_gate", "conv_a_b", "dt_bias", "a_log", "d_skip", "ssd_norm_w", "uv_b", "v_ln_w", "v_ln_b",
         "w_spatial", "b_spatial", "norm2_w", "conv_f_b", "final_norm_w"]
BIG = ["w_in", "w_branch", "w_out", "w_up", "w_down"]
WEIGHTS = ["norm1_w", "w_in", "b_gate", "conv_a_w", "conv_a_b", "dt_bias", "a_log", "d_skip", "ssd_norm_w", "uv_b",
           "v_ln_w", "v_ln_b", "w_spatial", "b_spatial", "w_branch", "w_out", "norm2_w", "w_up", "conv_f_w",
           "conv_f_b", "w_down", "final_norm_w"]
IN_SPLITS = [("z", 0, 2048), ("xbc", 2048, 5120), ("dt", 5120, 5152), ("uv", 5152, 7200), ("gates", 7200, 9248)]


def _columns_by_device(a):
    r = a.shape[0]
    return a.reshape(r, N_DEV, -1).transpose(1, 0, 2)


def _columns_from_devices(a):
    return a.transpose(1, 0, 2).reshape(a.shape[1], -1)


def kernel(x, norm1_w, w_in, b_gate, conv_a_w, conv_a_b, dt_bias, a_log, d_skip, ssd_norm_w, uv_b, v_ln_w, v_ln_b, w_spatial, b_spatial, w_branch, w_out, norm2_w, w_up, conv_f_w, conv_f_b, w_down, final_norm_w, loss_target, m_norm1_w, m_w_in, m_b_gate, m_conv_a_w, m_conv_a_b, m_dt_bias, m_a_log, m_d_skip, m_ssd_norm_w, m_uv_b, m_v_ln_w, m_v_ln_b, m_w_spatial, m_b_spatial, m_w_branch, m_w_out, m_norm2_w, m_w_up, m_conv_f_w, m_conv_f_b, m_w_down, m_final_norm_w, v_norm1_w, v_w_in, v_b_gate, v_conv_a_w, v_conv_a_b, v_dt_bias, v_a_log, v_d_skip, v_ssd_norm_w, v_uv_b, v_v_ln_w, v_v_ln_b, v_w_spatial, v_b_spatial, v_w_branch, v_w_out, v_norm2_w, v_w_up, v_conv_f_w, v_conv_f_b, v_w_down, v_final_norm_w):
    args = dict(locals())
    wts = {n: args[n] for n in WEIGHTS}
    mom = {n: args["m_" + n] for n in WEIGHTS}
    var = {n: args["v_" + n] for n in WEIGHTS}
    cx, cy, cc = _place()
    dev = 4 * cx + 2 * cy + cc
    place = jnp.stack([cc, 2 * cx + cy]).astype(jnp.int32)

    g_in, g_conv_a, g_conv_f = _all_gather([w_in[0].astype(BF16), conv_a_w[0], conv_f_w[0]], "gather_w_in")
    late = [wts[n][0].astype(BF16) for n in BIG[1:]]
    send_sems, recv_sems, late, lands, token = _gather_start(late, g_in, "gather_late_start")
    w_in_full = _columns_from_devices(g_in)
    w = {name: w_in_full[:, lo:hi] for name, lo, hi in IN_SPLITS}
    w["dt"] = _pad_lanes(w["dt"])
    w["conv_a"] = _columns_from_devices(g_conv_a)
    w["conv_f"] = _columns_from_devices(g_conv_f)

    def late_weights(*after):
        got = _gather_wait(send_sems, recv_sems, late, lands, after, "gather_late_wait")
        g_branch, g_out, g_up, g_down = [lax.dynamic_update_index_in_dim(land, mine, dev, 0)
                                         for land, mine in zip(got, late)]
        branch = g_branch.reshape(-1, D_MODEL)
        return {"branch_a": branch[:SSD_INNER], "branch_b": branch[SSD_INNER:], "out": g_out.reshape(-1, D_MODEL),
                "up": _columns_from_devices(g_up), "down": g_down.reshape(-1, D_MODEL)}

    in_flight = {}

    def on_grad(n, g):
        part = {"w_in": lambda: _columns_by_device(jnp.concatenate(
                    [g[name][:, :hi - lo] for name, lo, hi in IN_SPLITS], axis=1)),
                "w_branch": lambda: jnp.concatenate([g["branch_a"], g["branch_b"]], axis=0),
                "w_out": lambda: g["out"], "w_up": lambda: _columns_by_device(g["up"]), "w_down": lambda: g["down"]}[n]()
        part = part.reshape(N_DEV, *wts[n].shape[1:])
        from_core, = _exchange_cores([part], f"to_other_core_{n}")
        q, own = _chip_sum(part, from_core, place, f"chip_sum_{n}")
        send, recv, q, land, tok = _chips_start(q, f"to_other_chips_start_{n}")
        in_flight[n] = (own, send, recv, q, land)
        return tok

    p = {n: wts[n][0] if wts[n].ndim > 2 else wts[n].reshape(1, -1) for n in SMALL}
    loss, gx, g = _local_step(x[0], loss_target[0], w, p, after=token, late_weights=late_weights, on_grad=on_grad)
    loss = lax.psum(loss[0, 0], ("x", "y", "c"))

    small_g = [g[n] for n in SMALL] + [g["conv_a"], g["conv_f"]]
    gathered, = _all_gather([_pack(small_g)], "gather_small_grads")
    small_total = _sum_devices(gathered, "sum_small_grads")
    small_sum = _unpack(small_total, [a.shape for a in small_g])
    grads = {n: s.reshape(wts[n].shape) for n, s in zip(SMALL, small_sum[:len(SMALL)])}
    for n, s in (("conv_a_w", small_sum[-2]), ("conv_f_w", small_sum[-1])):
        cols = wts[n].shape[2]
        grads[n] = lax.dynamic_slice_in_dim(s, dev * cols, cols, axis=1)[None]

    delta, new_m, new_v = {}, {}, {}
    small_names = SMALL + ["conv_a_w", "conv_f_w"]
    packs = [_pack([t[n] for n in small_names]) for t in (wts, grads, mom, var)]
    outs = _adamw_call(*packs, "adamw_small")
    shapes = [wts[n].shape for n in small_names]
    for tgt, pack in zip((delta, new_m, new_v), outs):
        tgt.update(dict(zip(small_names, _unpack(pack, shapes))))
    after = small_total
    for n in ("w_down", "w_up", "w_out", "w_branch", "w_in"):
        own, send, recv, q, land = in_flight[n]
        got = _chips_wait(send, recv, q, land, after, f"to_other_chips_wait_{n}")
        gr, d, nm, nv = _sum_adamw(own, got, wts[n][0], mom[n][0], var[n][0], f"adamw_{n}")
        grads[n], delta[n], new_m[n], new_v[n] = gr[None], d[None], nm[None], nv[None]
        after = d

    return (loss, gx[None], *[grads[n] for n in WEIGHTS], *[delta[n] for n in WEIGHTS],
            *[new_m[n] for n in WEIGHTS], *[new_v[n] for n in WEIGHTS])
```

```python
import functools

import jax
import jax.numpy as jnp
from jax import lax
from jax.experimental import pallas as pl
from jax.experimental.pallas import tpu as pltpu

F32, BF16 = jnp.float32, jnp.bfloat16
HIGHEST = lax.Precision.HIGHEST

D_MODEL = 1024
SSD_INNER = 2048
SSD_HEAD_DIM = 64
SSD_HEADS = 32
SSD_GROUPS = 4
SSD_STATE = 128
SSD_BC = SSD_GROUPS * SSD_STATE
SSD_XBC = SSD_INNER + 2 * SSD_BC
SSD_CONV = 4
CHUNK = 128
N_PAIRS = SSD_HEADS // 2
PAIRS_PER_GROUP = N_PAIRS // SSD_GROUPS
SGU_WIDTH = 1024
SGU_GROUPS = 8
D_FF = 2816
FFN_CONV = 3
NORM_EPS = 1e-6
LN_EPS = 1e-5
LANES = 128
DT_PAD = LANES

ADAM_LR, ADAM_B1, ADAM_B2, ADAM_EPS, ADAM_WD, ADAM_STEP = 0.001, 0.9, 0.999, 1e-08, 0.01, 10

N_DEV = 8
VMEM_LIMIT = 56 * 1024 * 1024
MESH = pl.DeviceIdType.MESH


def _params(n_grid, **kw):
    sem = dict(dimension_semantics=("arbitrary",) * n_grid) if n_grid else {}
    return pltpu.CompilerParams(vmem_limit_bytes=VMEM_LIMIT, **sem, **kw)


def _tile(n, pref):
    t = (min(pref, n) // LANES) * LANES
    while n % t:
        t -= LANES
    return t


def _row_tile(r, pref):
    for t in range(min(pref, r) // 16 * 16, 0, -16):
        if r % t == 0:
            return t
    return r


def _tile2d(r, c, rows):
    if r % 16 == 0:
        return _row_tile(r, rows), c
    return r, _tile(c, 2 * LANES)


def _rows(tm, n, nt=None, rev=False, col=0):
    if rev:
        return pl.BlockSpec((tm, n), lambda i: (nt - 1 - i, col))
    return pl.BlockSpec((tm, n), lambda i: (i, col))


def _halo(tm, n, nt=None, rev=False):
    per = tm // 8
    if rev:
        return pl.BlockSpec((8, n), lambda i: (jnp.maximum((nt - 1 - i) * per - 1, 0), 0))
    return pl.BlockSpec((8, n), lambda i: (jnp.maximum(i * per - 1, 0), 0))


def _full(shape):
    nd = len(shape)
    return pl.BlockSpec(shape, lambda *_: (0,) * nd)


def _rms(x, w, eps=NORM_EPS):
    return x * lax.rsqrt(jnp.mean(x * x, axis=-1, keepdims=True) + eps) * w


def _layer_norm(x, w, b):
    mu = jnp.mean(x, axis=-1, keepdims=True)
    var = jnp.mean(jnp.square(x - mu), axis=-1, keepdims=True)
    return (x - mu) * lax.rsqrt(var + LN_EPS) * w + b


def _sigmoid(x):
    return 1.0 / (1.0 + jnp.exp(-x))


def _silu(x):
    return x * _sigmoid(x)


def _dsilu(x):
    s = _sigmoid(x)
    return s * (1.0 + x * (1.0 - s))


def _softplus(x):
    return jnp.maximum(x, 0.0) + jnp.log(1.0 + jnp.exp(-jnp.abs(x)))


def _gelu(x):
    return jax.nn.gelu(x)


def _dot(a, b):
    return jnp.dot(a, b, preferred_element_type=F32)


def _dot_nt(a, b):
    return lax.dot_general(a, b, (((1,), (1,)), ((), ())), preferred_element_type=F32)


def _dot_tn(a, b):
    return lax.dot_general(a, b, (((0,), (0,)), ((), ())), preferred_element_type=F32)


def _dot_split(p, e):
    hi = p.astype(BF16)
    lo = (p - hi.astype(F32)).astype(BF16)
    return _dot(hi, e) + _dot(lo, e)


def _colsum(x):
    return jnp.sum(x, axis=0, keepdims=True)


def _shift_down(x, halo, j):
    xs = pltpu.roll(x, j, 0)
    hs = pltpu.roll(halo, j, 0)
    r8 = lax.broadcasted_iota(jnp.int32, hs.shape, 0)
    return jnp.concatenate([jnp.where(r8 < j, hs, xs[:8]), xs[8:]], axis=0)


def _shift_up(x, nxt, j):
    n = x.shape[0]
    xs = pltpu.roll(x, n - j, 0)
    ns = pltpu.roll(nxt, 8 - j, 0)
    r8 = lax.broadcasted_iota(jnp.int32, ns.shape, 0)
    return jnp.concatenate([xs[:n - 8], jnp.where(r8 >= 8 - j, ns, xs[n - 8:])], axis=0)


def _causal_conv(x, halo, w, b):
    k = w.shape[0]
    shifted = [x] + [_shift_down(x, halo, j) for j in range(1, k)]
    y = b + w[k - 1:k, :] * x
    for j in range(1, k):
        y = y + w[k - 1 - j:k - j, :] * shifted[j]
    return y, shifted


def _anticausal_conv(dy, nxt, w):
    k = w.shape[0]
    dx = w[k - 1:k, :] * dy
    for j in range(1, k):
        dx = dx + w[k - 1 - j:k - j, :] * _shift_up(dy, nxt, j)
    return dx


def _conv_wgrad(dy, shifted):
    k = len(shifted)
    return jnp.concatenate([_colsum(dy * shifted[k - 1 - i]) for i in range(k)], axis=0)


MM_TILE_PREF = 1408
MM_VMEM_BUDGET = 40 * 1024 * 1024


def _mm_tiles(m, n, k, out_bytes):
    tm, tn = _tile(m, MM_TILE_PREF), _tile(n, MM_TILE_PREF)
    need = lambda tm, tn: 2 * (2 * k * (tm + tn) + out_bytes * tm * tn)
    while need(tm, tn) > MM_VMEM_BUDGET:
        if tn >= tm and tn > LANES:
            tn = _tile(n, tn - LANES)
        else:
            tm = _tile(m, tm - LANES)
    return tm, tn


def _mm(a, b, dims, name, acc=None, out_dtype=F32, after=None):
    if dims == "tn":
        k, m = a.shape
    else:
        m, k = a.shape
    n = b.shape[0] if dims == "nt" else b.shape[1]
    tm, tn = _mm_tiles(m, n, k, 4 * (2 if acc is not None else 1))
    a_spec = pl.BlockSpec((k, tm), lambda j, i: (0, i)) if dims == "tn" else pl.BlockSpec((tm, k), lambda j, i: (i, 0))
    b_spec = pl.BlockSpec((tn, k), lambda j, i: (j, 0)) if dims == "nt" else pl.BlockSpec((k, tn), lambda j, i: (0, j))
    o_spec = pl.BlockSpec((tm, tn), lambda j, i: (i, j))
    dot = {"nn": _dot, "nt": _dot_nt, "tn": _dot_tn}[dims]

    def body(a_ref, b_ref, *rest):
        r = dot(a_ref[...], b_ref[...])
        if acc is not None:
            r = r + rest[0][...]
        rest[-1][...] = r.astype(out_dtype)

    ins, specs = [a, b], [a_spec, b_spec]
    if acc is not None:
        ins.append(acc)
        specs.append(o_spec)
    if after is not None:
        ins.append(after)
        specs.append(pl.BlockSpec(memory_space=pl.ANY))
    return pl.pallas_call(
        body, name=name, grid=(n // tn, m // tm), in_specs=specs, out_specs=o_spec,
        out_shape=jax.ShapeDtypeStruct((m, n), out_dtype), compiler_params=_params(2),
    )(*ins)


def _wgrad(a, d, name):
    return _mm(a, d, "tn", name, out_dtype=BF16)


def _norm_fwd(x, w, name, after=None, tm=512):
    t, d = x.shape

    def body(x_ref, w_ref, *rest):
        rest[-1][...] = _rms(x_ref[...], w_ref[...]).astype(BF16)

    extra, extra_specs = ([after], [_full(after.shape)]) if after is not None else ([], [])
    return pl.pallas_call(
        body, name=name, grid=(t // tm,), in_specs=[_rows(tm, d), _full((1, d))] + extra_specs,
        out_specs=_rows(tm, d), out_shape=jax.ShapeDtypeStruct((t, d), BF16), compiler_params=_params(1),
    )(x, w, *extra)


def _conv_a_fwd(xbc, cw, cb, tm=256):
    t, c = xbc.shape

    def body(x_ref, h_ref, w_ref, b_ref, o_ref):
        halo = jnp.where(pl.program_id(0) > 0, h_ref[...], 0.0)
        y, _ = _causal_conv(x_ref[...], halo, w_ref[...], b_ref[...])
        o_ref[...] = _silu(y)

    return pl.pallas_call(
        body, name="conv_a_fwd", grid=(t // tm,),
        in_specs=[_rows(tm, c), _halo(tm, c), _full(cw.shape), _full((1, c))], out_specs=_rows(tm, c),
        out_shape=jax.ShapeDtypeStruct((t, c), F32), compiler_params=_params(1),
    )(xbc, xbc, cw, cb)


def _ssd_common(dtr, dtb, alog):
    row = lax.broadcasted_iota(jnp.int32, (CHUNK, CHUNK), 0)
    col = lax.broadcasted_iota(jnp.int32, (CHUNK, CHUNK), 1)
    causal = row >= col
    dt = _softplus(dtr + dtb)
    a = -jnp.exp(alog)
    acum = jnp.dot(causal.astype(F32), dt * a, precision=HIGHEST, preferred_element_type=F32)
    return dt, a, acum, acum.T, causal, col < SSD_HEAD_DIM, row


def _pair_terms(j, dt, acum, acum_t, causal, lane_lo):
    h0, h1 = 2 * j, 2 * j + 1
    ac0, ac1 = acum[:, h0:h0 + 1], acum[:, h1:h1 + 1]
    l0 = jnp.exp(jnp.where(causal, ac0 - acum_t[h0:h0 + 1, :], -jnp.inf))
    l1 = jnp.exp(jnp.where(causal, ac1 - acum_t[h1:h1 + 1, :], -jnp.inf))
    dtp = jnp.where(lane_lo, dt[:, h0:h0 + 1], dt[:, h1:h1 + 1])
    al0, al1 = acum[CHUNK - 1:CHUNK, h0:h0 + 1], acum[CHUNK - 1:CHUNK, h1:h1 + 1]
    ecol = jnp.where(lane_lo, jnp.exp(ac0), jnp.exp(ac1))
    dsr = jnp.where(lane_lo, jnp.exp(al0 - ac0), jnp.exp(al1 - ac1))
    elast = jnp.where(lane_lo[0:1], jnp.exp(al0), jnp.exp(al1))
    return l0, l1, dtp, ecol, dsr, elast


def _ssd_fwd(xc, dtr, z, dtb, alog, dsk, nw):
    t = xc.shape[0]
    nc = t // CHUNK

    def body(xs_ref, b_ref, c_ref, dtr_ref, z_ref, dtb_ref, alog_ref, dsk_ref, nw_ref, y_ref, ya_ref, sp_ref, s_scr):
        @pl.when(pl.program_id(0) == 0)
        def _():
            s_scr[...] = jnp.zeros_like(s_scr)

        dt, a, acum, acum_t, causal, lane_lo, _ = _ssd_common(dtr_ref[...], dtb_ref[...], alog_ref[...])
        dsk = dsk_ref[...]
        for g in range(SSD_GROUPS):
            gs = slice(g * SSD_STATE, (g + 1) * SSD_STATE)
            bg, cg = b_ref[:, gs].astype(BF16), c_ref[:, gs].astype(BF16)
            cb = _dot_nt(cg, bg)
            for pp in range(PAIRS_PER_GROUP):
                j = g * PAIRS_PER_GROUP + pp
                ps = slice(j * LANES, (j + 1) * LANES)
                x = xs_ref[:, ps]
                l0, l1, dtp, ecol, dsr, elast = _pair_terms(j, dt, acum, acum_t, causal, lane_lo)
                xdt = x * dtp
                xb = xdt.astype(BF16)
                zero = jnp.zeros_like(xb)
                yd = (_dot((cb * l0).astype(BF16), jnp.where(lane_lo, xb, zero))
                      + _dot((cb * l1).astype(BF16), jnp.where(lane_lo, zero, xb)))
                sp = s_scr[j]
                yo = ecol * _dot(cg, sp.astype(BF16))
                st = _dot_tn(bg, (xdt * dsr).astype(BF16))
                sp_ref[0, j] = sp
                s_scr[j] = elast * sp + st
                dskp = jnp.where(lane_lo[0:1], dsk[:, 2 * j:2 * j + 1], dsk[:, 2 * j + 1:2 * j + 2])
                y_ref[:, ps] = yd + yo + dskp * x
        ya_ref[...] = _rms(y_ref[...] * _silu(z_ref[...]), nw_ref[...]).astype(BF16)

    ck = lambda n, col=0: pl.BlockSpec((CHUNK, n), lambda c: (c, col))
    return pl.pallas_call(
        body, name="ssd_fwd", grid=(nc,),
        in_specs=[ck(SSD_INNER), ck(SSD_BC, SSD_INNER // SSD_BC), ck(SSD_BC, SSD_INNER // SSD_BC + 1), ck(DT_PAD),
                  ck(SSD_INNER), _full((1, DT_PAD)), _full((1, DT_PAD)), _full((1, DT_PAD)), _full((1, SSD_INNER))],
        out_specs=[ck(SSD_INNER), ck(SSD_INNER),
                   pl.BlockSpec((1, N_PAIRS, SSD_STATE, LANES), lambda c: (c, 0, 0, 0))],
        out_shape=[jax.ShapeDtypeStruct((t, SSD_INNER), F32), jax.ShapeDtypeStruct((t, SSD_INNER), BF16),
                   jax.ShapeDtypeStruct((nc, N_PAIRS, SSD_STATE, LANES), F32)],
        scratch_shapes=[pltpu.VMEM((N_PAIRS, SSD_STATE, LANES), F32)], compiler_params=_params(1),
    )(xc, xc, xc, dtr, z, dtb, alog, dsk, nw)


def _ssd_bwd(dya, y, z, xc, dtr, sprev, dtb, alog, dsk, nw, e_heads):
    t = xc.shape[0]
    nc = t // CHUNK

    def body(dya_ref, y_ref, z_ref, xs_ref, b_ref, c_ref, dtr_ref, sp_ref, dtb_ref, alog_ref, dsk_ref, nw_ref, e_ref,
             dz_ref, dxs_ref, db_ref, dc_ref, ddtr_ref, dnw_ref, ddtb_ref, dalog_ref, ddsk_ref, ds_scr):
        @pl.when(pl.program_id(0) == 0)
        def _():
            ds_scr[...] = jnp.zeros_like(ds_scr)
            for r in (dnw_ref, ddtb_ref, dalog_ref, ddsk_ref):
                r[...] = jnp.zeros_like(r)

        y = y_ref[...]
        _, gate_vjp = jax.vjp(lambda y_, z_, w_: _rms(y_ * _silu(z_), w_), y, z_ref[...], nw_ref[...])
        dy, dz, dnw = gate_vjp(dya_ref[...])
        dz_ref[...] = dz.astype(BF16)
        dnw_ref[...] += dnw

        dtr = dtr_ref[...]
        dt, a, acum, acum_t, causal, lane_lo, row = _ssd_common(dtr, dtb_ref[...], alog_ref[...])
        dsk = dsk_ref[...]
        p_a, p_dt, v_last = [], [], []
        col = lax.broadcasted_iota(jnp.int32, (CHUNK, CHUNK), 1)
        da_cols = jnp.zeros((CHUNK, CHUNK), F32)
        da_rows = jnp.zeros((CHUNK, CHUNK), F32)
        for g in range(SSD_GROUPS):
            gs = slice(g * SSD_STATE, (g + 1) * SSD_STATE)
            bg, cg = b_ref[:, gs].astype(BF16), c_ref[:, gs].astype(BF16)
            cb = _dot_nt(cg, bg)
            dcb = jnp.zeros((CHUNK, CHUNK), F32)
            dbg = jnp.zeros((CHUNK, SSD_STATE), F32)
            dcg = jnp.zeros((CHUNK, SSD_STATE), F32)
            for pp in range(PAIRS_PER_GROUP):
                j = g * PAIRS_PER_GROUP + pp
                ps = slice(j * LANES, (j + 1) * LANES)
                x = xs_ref[:, ps]
                l0, l1, dtp, ecol, dsr, elast = _pair_terms(j, dt, acum, acum_t, causal, lane_lo)
                xdt = x * dtp
                xb = xdt.astype(BF16)
                dskp = jnp.where(lane_lo[0:1], dsk[:, 2 * j:2 * j + 1], dsk[:, 2 * j + 1:2 * j + 2])
                dyp = dy[:, ps]
                dyb = dyp.astype(BF16)
                sp, dsn = sp_ref[0, j], ds_scr[j]
                spb, dsnb = sp.astype(BF16), dsn.astype(BF16)
                y_off = ecol * _dot(cg, spb)
                dw = (dyp * ecol).astype(BF16)
                dcg = dcg + _dot_nt(dw, spb)
                dsp = _dot_tn(cg, dw) + elast * dsn
                xd = xdt * dsr
                zd = _dot(bg, dsnb) * dsr
                dbg = dbg + _dot_nt(xd.astype(BF16), dsnb)
                dxdt = zd
                zero = jnp.zeros_like(xb)
                for h, lm, le in ((2 * j, lane_lo, l0), (2 * j + 1, jnp.logical_not(lane_lo), l1)):
                    dm = _dot_nt(jnp.where(lm, dyb, zero), jnp.where(lm, xb, zero))
                    dcb = dcb + dm * le
                    m = cb * le
                    dxdt = dxdt + jnp.where(lm, _dot_tn(m.astype(BF16), dyb), 0.0)
                    q = dm * m
                    da_cols = da_cols + jnp.where(col == h, jnp.sum(q, axis=1, keepdims=True), 0.0)
                    da_rows = da_rows + jnp.where(row == h, _colsum(q), 0.0)
                ds_scr[j] = dsp
                dxs_ref[:, ps] = dxdt * dtp + dskp * dyp
                p_a.append(dyp * y_off - xdt * zd)
                p_dt.append(dxdt * x)
                v_last.append(_colsum(zd * xdt) + elast * _colsum(dsn * sp))
            dcbb = dcb.astype(BF16)
            db_ref[:, gs] = dbg + _dot_tn(dcbb, cg)
            dc_ref[:, gs] = dcg + _dot(dcbb, bg)
        e = e_ref[...]
        rows8 = jnp.concatenate([jnp.concatenate(v_last, axis=1), _colsum(dy * xs_ref[...]),
                                 jnp.zeros((6, SSD_INNER), F32)], axis=0)
        r8 = _dot_split(rows8, e)
        da = (_dot_split(jnp.concatenate(p_a, axis=1), e) + jnp.where(row == CHUNK - 1, r8[0:1], 0.0)
              + da_cols - da_rows.T)
        ddsk_ref[...] += r8[1:2]
        dadt = jnp.dot((row <= col).astype(F32), da, precision=HIGHEST, preferred_element_type=F32)
        ddt = dadt * a + _dot_split(jnp.concatenate(p_dt, axis=1), e)
        dalog_ref[...] += _colsum(dadt * dt) * a
        ddtr = ddt * _sigmoid(dtr + dtb_ref[...])
        ddtr_ref[...] = ddtr
        ddtb_ref[...] += _colsum(ddtr)

    ck = lambda n, col=0: pl.BlockSpec((CHUNK, n), lambda c: (nc - 1 - c, col))
    acc = lambda n: _full((1, n))
    return pl.pallas_call(
        body, name="ssd_bwd", grid=(nc,),
        in_specs=[ck(SSD_INNER), ck(SSD_INNER), ck(SSD_INNER), ck(SSD_INNER), ck(SSD_BC, SSD_INNER // SSD_BC),
                  ck(SSD_BC, SSD_INNER // SSD_BC + 1), ck(DT_PAD),
                  pl.BlockSpec((1, N_PAIRS, SSD_STATE, LANES), lambda c: (nc - 1 - c, 0, 0, 0)),
                  acc(DT_PAD), acc(DT_PAD), acc(DT_PAD), acc(SSD_INNER), _full((SSD_INNER, LANES))],
        out_specs=[ck(SSD_INNER), ck(SSD_INNER), ck(SSD_BC), ck(SSD_BC), ck(DT_PAD),
                   acc(SSD_INNER), acc(DT_PAD), acc(DT_PAD), acc(DT_PAD)],
        out_shape=[jax.ShapeDtypeStruct((t, SSD_INNER), BF16), jax.ShapeDtypeStruct((t, SSD_INNER), F32),
                   jax.ShapeDtypeStruct((t, SSD_BC), F32), jax.ShapeDtypeStruct((t, SSD_BC), F32),
                   jax.ShapeDtypeStruct((t, DT_PAD), F32), jax.ShapeDtypeStruct((1, SSD_INNER), F32),
                   jax.ShapeDtypeStruct((1, DT_PAD), F32), jax.ShapeDtypeStruct((1, DT_PAD), F32),
                   jax.ShapeDtypeStruct((1, DT_PAD), F32)],
        scratch_shapes=[pltpu.VMEM((N_PAIRS, SSD_STATE, LANES), F32)], compiler_params=_params(1),
    )(dya, y, z, xc, xc, xc, dtr, sprev, dtb, alog, dsk, nw, e_heads)


def _sgu_act(uv, uvb, lnw, lnb):
    a = _gelu(uv + uvb)
    return a[:, :SGU_WIDTH], _layer_norm(a[:, SGU_WIDTH:], lnw, lnb)


def _sgu_weights(ws_ref):
    row = lax.broadcasted_iota(jnp.int32, (CHUNK, CHUNK), 0)
    col = lax.broadcasted_iota(jnp.int32, (CHUNK, CHUNK), 1)
    return [jnp.where(row >= col, ws_ref[g], 0.0).astype(BF16) for g in range(SGU_GROUPS)], row >= col


def _sgu_fwd(uv, uvb, lnw, lnb, ws, bs_t):
    t = uv.shape[0]

    def body(uv_ref, uvb_ref, lnw_ref, lnb_ref, ws_ref, bs_ref, o_ref):
        u, vn = _sgu_act(uv_ref[...], uvb_ref[...], lnw_ref[...], lnb_ref[...])
        wc, _ = _sgu_weights(ws_ref)
        bs = bs_ref[...]
        for g in range(SGU_GROUPS):
            gs = slice(g * LANES, (g + 1) * LANES)
            mixed = _dot(wc[g], vn[:, gs].astype(BF16)) + bs[:, g:g + 1]
            o_ref[:, gs] = (u[:, gs] * mixed).astype(BF16)

    return pl.pallas_call(
        body, name="sgu_fwd", grid=(t // CHUNK,),
        in_specs=[_rows(CHUNK, 2 * SGU_WIDTH), _full((1, 2 * SGU_WIDTH)), _full((1, SGU_WIDTH)), _full((1, SGU_WIDTH)),
                  _full(ws.shape), _full(bs_t.shape)],
        out_specs=_rows(CHUNK, SGU_WIDTH), out_shape=jax.ShapeDtypeStruct((t, SGU_WIDTH), BF16),
        compiler_params=_params(1),
    )(uv, uvb, lnw, lnb, ws, bs_t)


def _sgu_bwd(dyb, uv, uvb, lnw, lnb, ws, bs_t, e_groups):
    t = uv.shape[0]

    def body(dyb_ref, uv_ref, uvb_ref, lnw_ref, lnb_ref, ws_ref, bs_ref, e_ref,
             duv_ref, duvb_ref, dlnw_ref, dlnb_ref, dws_ref, dbs_ref):
        @pl.when(pl.program_id(0) == 0)
        def _():
            for r in (duvb_ref, dlnw_ref, dlnb_ref, dws_ref, dbs_ref):
                r[...] = jnp.zeros_like(r)

        (u, vn), act_vjp = jax.vjp(_sgu_act, uv_ref[...], uvb_ref[...], lnw_ref[...], lnb_ref[...])
        wc, causal = _sgu_weights(ws_ref)
        bs = bs_ref[...]
        dyb = dyb_ref[...]
        du, dvn, dmix = [], [], []
        for g in range(SGU_GROUPS):
            gs = slice(g * LANES, (g + 1) * LANES)
            vb = vn[:, gs].astype(BF16)
            mixed = _dot(wc[g], vb) + bs[:, g:g + 1]
            dm = dyb[:, gs] * u[:, gs]
            dmb = dm.astype(BF16)
            du.append(dyb[:, gs] * mixed)
            dvn.append(_dot_tn(wc[g], dmb))
            dws_ref[g] += jnp.where(causal, _dot_nt(dmb, vb), 0.0)
            dmix.append(dm)
        dbs_ref[...] += _dot_split(jnp.concatenate(dmix, axis=1), e_ref[...])
        duv, duvb, dlnw, dlnb = act_vjp((jnp.concatenate(du, axis=1), jnp.concatenate(dvn, axis=1)))
        duv_ref[...] = duv.astype(BF16)
        duvb_ref[...] += duvb
        dlnw_ref[...] += dlnw
        dlnb_ref[...] += dlnb

    return pl.pallas_call(
        body, name="sgu_bwd", grid=(t // CHUNK,),
        in_specs=[_rows(CHUNK, SGU_WIDTH), _rows(CHUNK, 2 * SGU_WIDTH), _full((1, 2 * SGU_WIDTH)),
                  _full((1, SGU_WIDTH)), _full((1, SGU_WIDTH)), _full(ws.shape), _full(bs_t.shape),
                  _full(e_groups.shape)],
        out_specs=[_rows(CHUNK, 2 * SGU_WIDTH), _full((1, 2 * SGU_WIDTH)), _full((1, SGU_WIDTH)),
                   _full((1, SGU_WIDTH)), _full(ws.shape), _full(bs_t.shape)],
        out_shape=[jax.ShapeDtypeStruct((t, 2 * SGU_WIDTH), BF16), jax.ShapeDtypeStruct((1, 2 * SGU_WIDTH), F32),
                   jax.ShapeDtypeStruct((1, SGU_WIDTH), F32), jax.ShapeDtypeStruct((1, SGU_WIDTH), F32),
                   jax.ShapeDtypeStruct(ws.shape, F32), jax.ShapeDtypeStruct(bs_t.shape, F32)],
        compiler_params=_params(1),
    )(dyb, uv, uvb, lnw, lnb, ws, bs_t, e_groups)


def _merge(gates, pa, pb, bg):
    s = _sigmoid(gates + bg)
    return s[:, :D_MODEL] * pa + s[:, D_MODEL:] * pb


def _merge_fwd(gates, pa, pb, bg, tm=256):
    t = gates.shape[0]

    def body(g_ref, pa_ref, pb_ref, bg_ref, o_ref):
        o_ref[...] = _merge(g_ref[...], pa_ref[...], pb_ref[...], bg_ref[...]).astype(BF16)

    return pl.pallas_call(
        body, name="merge_fwd", grid=(t // tm,),
        in_specs=[_rows(tm, 2 * D_MODEL), _rows(tm, D_MODEL), _rows(tm, D_MODEL), _full((1, 2 * D_MODEL))],
        out_specs=_rows(tm, D_MODEL), out_shape=jax.ShapeDtypeStruct((t, D_MODEL), BF16), compiler_params=_params(1),
    )(gates, pa, pb, bg)


def _merge_bwd(dmix, gates, pa, pb, bg, tm=256):
    t = gates.shape[0]

    def body(d_ref, g_ref, pa_ref, pb_ref, bg_ref, dg_ref, dpa_ref, dpb_ref, dbg_ref):
        @pl.when(pl.program_id(0) == 0)
        def _():
            dbg_ref[...] = jnp.zeros_like(dbg_ref)

        _, vjp = jax.vjp(_merge, g_ref[...], pa_ref[...], pb_ref[...], bg_ref[...])
        dg, dpa, dpb, dbg = vjp(d_ref[...])
        dg_ref[...] = dg.astype(BF16)
        dpa_ref[...] = dpa.astype(BF16)
        dpb_ref[...] = dpb.astype(BF16)
        dbg_ref[...] += dbg

    return pl.pallas_call(
        body, name="merge_bwd", grid=(t // tm,),
        in_specs=[_rows(tm, D_MODEL), _rows(tm, 2 * D_MODEL), _rows(tm, D_MODEL), _rows(tm, D_MODEL),
                  _full((1, 2 * D_MODEL))],
        out_specs=[_rows(tm, 2 * D_MODEL), _rows(tm, D_MODEL), _rows(tm, D_MODEL), _full((1, 2 * D_MODEL))],
        out_shape=[jax.ShapeDtypeStruct((t, 2 * D_MODEL), BF16), jax.ShapeDtypeStruct((t, D_MODEL), BF16),
                   jax.ShapeDtypeStruct((t, D_MODEL), BF16), jax.ShapeDtypeStruct((1, 2 * D_MODEL), F32)],
        compiler_params=_params(1),
    )(dmix, gates, pa, pb, bg)


def _residual_norm_fwd(x, o, w, tm=512):
    t, d = x.shape

    def body(x_ref, o_ref, w_ref, h_ref, n_ref):
        h = x_ref[...] + o_ref[...]
        h_ref[...] = h
        n_ref[...] = _rms(h, w_ref[...]).astype(BF16)

    return pl.pallas_call(
        body, name="residual_norm_fwd", grid=(t // tm,), in_specs=[_rows(tm, d), _rows(tm, d), _full((1, d))],
        out_specs=[_rows(tm, d), _rows(tm, d)],
        out_shape=[jax.ShapeDtypeStruct((t, d), F32), jax.ShapeDtypeStruct((t, d), BF16)], compiler_params=_params(1),
    )(x, o, w)


def _norm_bwd(dn, h, w, dres, name, tm=512):
    t, d = h.shape

    def body(dn_ref, h_ref, w_ref, dres_ref, dh_ref, dhb_ref, dw_ref):
        @pl.when(pl.program_id(0) == 0)
        def _():
            dw_ref[...] = jnp.zeros_like(dw_ref)

        _, vjp = jax.vjp(_rms, h_ref[...], w_ref[...])
        dh, dw = vjp(dn_ref[...])
        dh = dh + dres_ref[...]
        dh_ref[...] = dh
        dhb_ref[...] = dh.astype(BF16)
        dw_ref[...] += dw

    return pl.pallas_call(
        body, name=name, grid=(t // tm,), in_specs=[_rows(tm, d), _rows(tm, d), _full((1, d)), _rows(tm, d)],
        out_specs=[_rows(tm, d), _rows(tm, d), _full((1, d))],
        out_shape=[jax.ShapeDtypeStruct((t, d), F32), jax.ShapeDtypeStruct((t, d), BF16),
                   jax.ShapeDtypeStruct((1, d), F32)], compiler_params=_params(1),
    )(dn, h, w, dres)


def _conv_f_fwd(up, cw, cb, tm=128):
    t, c = up.shape

    def body(x_ref, h_ref, w_ref, b_ref, o_ref):
        halo = jnp.where(pl.program_id(0) > 0, h_ref[...], 0.0)
        y, _ = _causal_conv(x_ref[...], halo, w_ref[...], b_ref[...])
        o_ref[...] = (_silu(y[:, :D_FF]) * y[:, D_FF:]).astype(BF16)

    return pl.pallas_call(
        body, name="conv_f_fwd", grid=(t // tm,),
        in_specs=[_rows(tm, c), _halo(tm, c), _full(cw.shape), _full((1, c))], out_specs=_rows(tm, D_FF),
        out_shape=jax.ShapeDtypeStruct((t, D_FF), BF16), compiler_params=_params(1),
    )(up, up, cw, cb)


def _conv_f_bwd(dact, up, cw, cb, tm=128):
    t, c = up.shape
    nt = t // tm

    def body(d_ref, x_ref, h_ref, w_ref, b_ref, dx_ref, dw_ref, db_ref, nxt_scr):
        @pl.when(pl.program_id(0) == 0)
        def _():
            nxt_scr[...] = jnp.zeros_like(nxt_scr)
            dw_ref[...] = jnp.zeros_like(dw_ref)
            db_ref[...] = jnp.zeros_like(db_ref)

        halo = jnp.where(pl.program_id(0) < nt - 1, h_ref[...], 0.0)
        w = w_ref[...]
        y, shifted = _causal_conv(x_ref[...], halo, w, b_ref[...])
        a, v = y[:, :D_FF], y[:, D_FF:]
        d = d_ref[...]
        dy = jnp.concatenate([d * v * _dsilu(a), d * _silu(a)], axis=1)
        dx_ref[...] = _anticausal_conv(dy, nxt_scr[...], w).astype(BF16)
        nxt_scr[...] = dy[:8]
        dw_ref[...] += _conv_wgrad(dy, shifted)
        db_ref[...] += _colsum(dy)

    return pl.pallas_call(
        body, name="conv_f_bwd", grid=(nt,),
        in_specs=[_rows(tm, D_FF, nt, True), _rows(tm, c, nt, True), _halo(tm, c, nt, True), _full(cw.shape),
                  _full((1, c))],
        out_specs=[_rows(tm, c, nt, True), _full(cw.shape), _full((1, c))],
        out_shape=[jax.ShapeDtypeStruct((t, c), BF16), jax.ShapeDtypeStruct(cw.shape, F32),
                   jax.ShapeDtypeStruct((1, c), F32)],
        scratch_shapes=[pltpu.VMEM((8, c), F32)], compiler_params=_params(1),
    )(dact, up, up, cw, cb)


def _conv_a_bwd(dxs, db, dc, xbc, cw, cb, tm=256):
    t, c = xbc.shape
    nt = t // tm

    def body(dxs_ref, db_ref, dc_ref, x_ref, h_ref, w_ref, b_ref, dx_ref, dw_ref, dbias_ref, nxt_scr):
        @pl.when(pl.program_id(0) == 0)
        def _():
            nxt_scr[...] = jnp.zeros_like(nxt_scr)
            dw_ref[...] = jnp.zeros_like(dw_ref)
            dbias_ref[...] = jnp.zeros_like(dbias_ref)

        halo = jnp.where(pl.program_id(0) < nt - 1, h_ref[...], 0.0)
        w = w_ref[...]
        y, shifted = _causal_conv(x_ref[...], halo, w, b_ref[...])
        dy = jnp.concatenate([dxs_ref[...], db_ref[...], dc_ref[...]], axis=1) * _dsilu(y)
        dx_ref[...] = _anticausal_conv(dy, nxt_scr[...], w).astype(BF16)
        nxt_scr[...] = dy[:8]
        dw_ref[...] += _conv_wgrad(dy, shifted)
        dbias_ref[...] += _colsum(dy)

    return pl.pallas_call(
        body, name="conv_a_bwd", grid=(nt,),
        in_specs=[_rows(tm, SSD_INNER, nt, True), _rows(tm, SSD_BC, nt, True), _rows(tm, SSD_BC, nt, True),
                  _rows(tm, c, nt, True), _halo(tm, c, nt, True), _full(cw.shape), _full((1, c))],
        out_specs=[_rows(tm, c, nt, True), _full(cw.shape), _full((1, c))],
        out_shape=[jax.ShapeDtypeStruct((t, c), BF16), jax.ShapeDtypeStruct(cw.shape, F32),
                   jax.ShapeDtypeStruct((1, c), F32)],
        scratch_shapes=[pltpu.VMEM((8, c), F32)], compiler_params=_params(1),
    )(dxs, db, dc, xbc, xbc, cw, cb)


def _loss_head(h1, dn, w, target, tm=512):
    t, d = h1.shape

    def body(h_ref, dn_ref, w_ref, t_ref, loss_ref, dh_ref, dhb_ref, dw_ref):
        @pl.when(pl.program_id(0) == 0)
        def _():
            loss_ref[...] = jnp.zeros_like(loss_ref)
            dw_ref[...] = jnp.zeros_like(dw_ref)

        yf, vjp = jax.vjp(_rms, h_ref[...] + dn_ref[...], w_ref[...])
        err = yf - t_ref[...]
        loss_ref[...] += 0.5 * jnp.sum(jnp.mean(err * err, axis=-1, keepdims=True))
        dh, dw = vjp(err * (1.0 / d))
        dh_ref[...] = dh
        dhb_ref[...] = dh.astype(BF16)
        dw_ref[...] += dw

    return pl.pallas_call(
        body, name="loss_head", grid=(t // tm,),
        in_specs=[_rows(tm, d), _rows(tm, d), _full((1, d)), _rows(tm, d)],
        out_specs=[_full((8, LANES)), _rows(tm, d), _rows(tm, d), _full((1, d))],
        out_shape=[jax.ShapeDtypeStruct((8, LANES), F32), jax.ShapeDtypeStruct((t, d), F32),
                   jax.ShapeDtypeStruct((t, d), BF16), jax.ShapeDtypeStruct((1, d), F32)], compiler_params=_params(1),
    )(h1, dn, w, target)


def _pad_lanes(v, n=DT_PAD):
    return jnp.pad(v, ((0, 0), (0, n - v.shape[1])))


def _local_step(x, target, w, p, after=None, late_weights=None, on_grad=None):
    dtb, alog, dsk = _pad_lanes(p["dt_bias"]), _pad_lanes(p["a_log"]), _pad_lanes(p["d_skip"])
    bs_t = _pad_lanes(p["b_spatial"].T)
    e_heads = (jnp.arange(SSD_INNER)[:, None] // SSD_HEAD_DIM == jnp.arange(LANES)[None, :]).astype(BF16)
    e_groups = (jnp.arange(SGU_WIDTH)[:, None] // LANES == jnp.arange(LANES)[None, :]).astype(BF16)

    n1 = _norm_fwd(x, p["norm1_w"], "norm1_fwd", after=after)
    z = _mm(n1, w["z"], "nt", "proj_z")
    xbc = _mm(n1, w["xbc"], "nt", "proj_xbc")
    dtr = _mm(n1, w["dt"], "nt", "proj_dt")
    uv = _mm(n1, w["uv"], "nt", "proj_uv")
    gates = _mm(n1, w["gates"], "nt", "proj_gates")
    xc = _conv_a_fwd(xbc, w["conv_a"], p["conv_a_b"])
    y, ya, sprev = _ssd_fwd(xc, dtr, z, dtb, alog, dsk, p["ssd_norm_w"])
    yb = _sgu_fwd(uv, p["uv_b"], p["v_ln_w"], p["v_ln_b"], p["w_spatial"], bs_t)
    if late_weights is not None:
        w = {**w, **late_weights(ya, yb)}
    pa = _mm(ya, w["branch_a"], "nn", "branch_a")
    pb = _mm(yb, w["branch_b"], "nn", "branch_b")
    mix = _merge_fwd(gates, pa, pb, p["b_gate"])
    o = _mm(mix, w["out"], "nn", "out_proj")
    h1, n2 = _residual_norm_fwd(x, o, p["norm2_w"])
    up = _mm(n2, w["up"], "nt", "up_proj")
    act = _conv_f_fwd(up, w["conv_f"], p["conv_f_b"])
    dn = _mm(act, w["down"], "nn", "down_proj")
    loss, dh2, dh2b, g_final = _loss_head(h1, dn, p["final_norm_w"], target)

    on_grad = on_grad or (lambda name, grads: None)
    g = {"final_norm_w": g_final}
    g["down"] = _wgrad(act, dh2b, "down_wgrad")
    tok = on_grad("w_down", g)
    dact = _mm(dh2b, w["down"], "nt", "down_dgrad", after=tok)
    dup, g["conv_f"], g["conv_f_b"] = _conv_f_bwd(dact, up, w["conv_f"], p["conv_f_b"])
    g["up"] = _wgrad(dup, n2, "up_wgrad")
    tok = on_grad("w_up", g)
    dn2 = _mm(dup, w["up"], "nn", "up_dgrad", after=tok)
    dh1, dh1b, g["norm2_w"] = _norm_bwd(dn2, h1, p["norm2_w"], dh2, "norm2_bwd")
    g["out"] = _wgrad(mix, dh1b, "out_wgrad")
    tok = on_grad("w_out", g)
    dmix = _mm(dh1b, w["out"], "nt", "out_dgrad", after=tok)
    dgates, dpa, dpb, g["b_gate"] = _merge_bwd(dmix, gates, pa, pb, p["b_gate"])
    g["branch_a"] = _wgrad(ya, dpa, "branch_a_wgrad")
    g["branch_b"] = _wgrad(yb, dpb, "branch_b_wgrad")
    tok = on_grad("w_branch", g)
    dya = _mm(dpa, w["branch_a"], "nt", "branch_a_dgrad", after=tok)
    dyb = _mm(dpb, w["branch_b"], "nt", "branch_b_dgrad", after=tok)
    duv, g["uv_b"], g["v_ln_w"], g["v_ln_b"], g["w_spatial"], dbs_t = _sgu_bwd(
        dyb, uv, p["uv_b"], p["v_ln_w"], p["v_ln_b"], p["w_spatial"], bs_t, e_groups)
    g["b_spatial"] = dbs_t[:, :SGU_GROUPS].T
    dz, dxs, db, dc, ddtr, g["ssd_norm_w"], ddtb, dalog, ddsk = _ssd_bwd(
        dya, y, z, xc, dtr, sprev, dtb, alog, dsk, p["ssd_norm_w"], e_heads)
    g["dt_bias"], g["a_log"], g["d_skip"] = ddtb[:, :SSD_HEADS], dalog[:, :SSD_HEADS], ddsk[:, :SSD_HEADS]
    dxbc, g["conv_a"], g["conv_a_b"] = _conv_a_bwd(dxs, db, dc, xbc, w["conv_a"], p["conv_a_b"])
    ddtrb = ddtr.astype(BF16)
    for name, d in (("z", dz), ("xbc", dxbc), ("dt", ddtrb), ("uv", duv), ("gates", dgates)):
        g[name] = _wgrad(d, n1, name + "_wgrad")
    tok = on_grad("w_in", g)
    dn1 = _mm(dz, w["z"], "nn", "z_dgrad", after=tok)
    dn1 = _mm(dxbc, w["xbc"], "nn", "xbc_dgrad", acc=dn1)
    dn1 = _mm(ddtrb, w["dt"], "nn", "dt_dgrad", acc=dn1)
    dn1 = _mm(duv, w["uv"], "nn", "uv_dgrad", acc=dn1)
    dn1 = _mm(dgates, w["gates"], "nn", "gates_dgrad", acc=dn1)
    gx, _, g["norm1_w"] = _norm_bwd(dn1, x, p["norm1_w"], dh1, "norm1_bwd")
    return loss, gx, g


def _place():
    return lax.axis_index("x"), lax.axis_index("y"), lax.axis_index("c")


def _other_chips(x, y):
    return [(1 - x, y), (x, 1 - y), (1 - x, 1 - y)]


def _all_gather(shards, name):
    n = len(shards)

    def body(*refs):
        ins, outs = refs[:n], refs[n:2 * n]
        send_sems, recv_sems, local_sems = refs[2 * n:]
        x, y, c = _place()
        me, sibling = (x, y, c), (x, y, 1 - c)
        chips = _other_chips(x, y)

        def copy(a, k, block, to, src=None):
            slot = outs[a].at[4 * block[0] + 2 * block[1] + block[2]]
            return pltpu.make_async_remote_copy(
                src_ref=slot if src is None else src, dst_ref=slot, send_sem=send_sems.at[7 * a + k],
                recv_sem=recv_sems.at[7 * a + k], device_id=to, device_id_type=MESH)

        started = []
        for a in range(n):
            mine = pltpu.make_async_copy(ins[a], outs[a].at[4 * x + 2 * y + c], local_sems.at[a])
            mine.start()
            started.append(mine)
        sends = []
        for a in range(n):
            sends.append(copy(a, 0, me, sibling, src=ins[a]))
            sends += [copy(a, 1 + j, me, (*chip, c), src=ins[a]) for j, chip in enumerate(chips)]
        for cp in sends:
            cp.start()
        for a in range(n):
            for j, chip in enumerate(chips):
                copy(a, 1 + j, (*chip, c), me).wait_recv()
                fwd = copy(a, 4 + j, (*chip, c), sibling)
                fwd.start()
                sends.append(fwd)
        for a in range(n):
            copy(a, 0, sibling, me).wait_recv()
            for j, chip in enumerate(chips):
                copy(a, 4 + j, (*chip, 1 - c), me).wait_recv()
        for cp in sends:
            cp.wait_send()
        for mine in started:
            mine.wait()

    any_spec = pl.BlockSpec(memory_space=pl.ANY)
    return pl.pallas_call(
        body, name=name, in_specs=[any_spec] * n, out_specs=[any_spec] * n,
        out_shape=[jax.ShapeDtypeStruct((N_DEV, *s.shape), s.dtype) for s in shards],
        scratch_shapes=[pltpu.SemaphoreType.DMA((7 * n,)), pltpu.SemaphoreType.DMA((7 * n,)),
                        pltpu.SemaphoreType.DMA((n,))],
    )(*shards)


HBM_SPEC = pl.BlockSpec(memory_space=pltpu.HBM)
SEM_SPEC = pl.BlockSpec(memory_space=pltpu.SEMAPHORE)
ANY_SPEC = pl.BlockSpec(memory_space=pl.ANY)
DATAFLOW = pltpu.SideEffectType.DATAFLOW_SIDE_EFFECTING
N_PEERS = N_DEV - 1


def _peers(x, y, c):
    out = []
    for r in range(1, N_DEV):
        fx, fy, fc = r >> 2 & 1, r >> 1 & 1, r & 1
        out.append(((1 - x) if fx else x, (1 - y) if fy else y, (1 - c) if fc else c))
    return out


def _gather_copies(srcs, lands, send_sems, recv_sems, sending):
    x, y, c = _place()
    copies = []
    for a, (src, land) in enumerate(zip(srcs, lands)):
        for j, (px, py, pc) in enumerate(_peers(x, y, c)):
            slot = 4 * x + 2 * y + c if sending else 4 * px + 2 * py + pc
            copies.append(pltpu.make_async_remote_copy(
                src_ref=src, dst_ref=land.at[slot], send_sem=send_sems.at[N_PEERS * a + j],
                recv_sem=recv_sems.at[N_PEERS * a + j], device_id=(px, py, pc), device_id_type=MESH))
    return copies


def _gather_start(shards, after, name):
    n = len(shards)

    def body(*refs):
        srcs, lands = refs[:n], refs[n:2 * n]
        send_sems, recv_sems = refs[2 * n + 1:2 * n + 3]
        token = refs[-1]
        for cp in _gather_copies(srcs, lands, send_sems, recv_sems, sending=True):
            cp.start()
        token[...] = jnp.zeros_like(token)

    lands = [lax.empty((N_DEV, *s.shape), s.dtype) for s in shards]
    hbm = lambda a: pltpu.with_memory_space_constraint(a, pltpu.HBM)
    out = pl.pallas_call(
        body, name=name,
        out_shape=(pltpu.SemaphoreType.DMA((N_PEERS * n,)), pltpu.SemaphoreType.DMA((N_PEERS * n,)),
                   *[pltpu.HBM(a.shape, a.dtype) for a in (*shards, *lands)], jax.ShapeDtypeStruct((8, LANES), F32)),
        in_specs=[HBM_SPEC] * (2 * n) + [ANY_SPEC],
        out_specs=(SEM_SPEC, SEM_SPEC, *[HBM_SPEC] * (2 * n), pl.BlockSpec(memory_space=pltpu.VMEM)),
        input_output_aliases={i: 2 + i for i in range(2 * n)},
        compiler_params=pltpu.CompilerParams(has_side_effects=DATAFLOW),
    )(*[hbm(a) for a in (*shards, *lands)], after)
    return out[0], out[1], out[2:2 + n], out[2 + n:2 + 2 * n], out[-1]


def _gather_wait(send_sems, recv_sems, shards, lands, after, name):
    n = len(shards)
    after = tuple(after)

    def body(*refs):
        srcs, lands_ = refs[:n], refs[n:2 * n]
        send, recv = refs[2 * n:2 * n + 2]
        for cp in _gather_copies(srcs, lands_, send, recv, sending=False):
            cp.wait_send()
            cp.wait_recv()

    out = pl.pallas_call(
        body, name=name, out_shape=tuple(pltpu.HBM(a.shape, a.dtype) for a in (*shards, *lands)),
        in_specs=[HBM_SPEC] * (2 * n) + [SEM_SPEC, SEM_SPEC] + [ANY_SPEC] * len(after),
        out_specs=tuple([HBM_SPEC] * (2 * n)), input_output_aliases={i: i for i in range(2 * n)},
        compiler_params=pltpu.CompilerParams(has_side_effects=DATAFLOW),
    )(*shards, *lands, send_sems, recv_sems, *after)
    return out[n:]


def _chip_copies(src, land, send_sems, recv_sems):
    x, y, c = _place()
    return [pltpu.make_async_remote_copy(
        src_ref=src.at[2 * cx + cy], dst_ref=land.at[j], send_sem=send_sems.at[j], recv_sem=recv_sems.at[j],
        device_id=(cx, cy, c), device_id_type=MESH) for j, (cx, cy) in enumerate(_other_chips(x, y))]


def _chips_start(q, name):
    def body(q_ref, land_ref, send_sems, recv_sems, q_thru, land_thru, token):
        for cp in _chip_copies(q_ref, land_ref, send_sems, recv_sems):
            cp.start()
        token[...] = jnp.zeros_like(token)

    land = lax.empty((3, *q.shape[1:]), q.dtype)
    return pl.pallas_call(
        body, name=name,
        out_shape=(pltpu.SemaphoreType.DMA((3,)), pltpu.SemaphoreType.DMA((3,)), pltpu.HBM(q.shape, q.dtype),
                   pltpu.HBM(land.shape, land.dtype), jax.ShapeDtypeStruct((8, LANES), F32)),
        in_specs=[HBM_SPEC, HBM_SPEC],
        out_specs=(SEM_SPEC, SEM_SPEC, HBM_SPEC, HBM_SPEC, pl.BlockSpec(memory_space=pltpu.VMEM)),
        input_output_aliases={0: 2, 1: 3}, compiler_params=pltpu.CompilerParams(has_side_effects=DATAFLOW),
    )(pltpu.with_memory_space_constraint(q, pltpu.HBM), pltpu.with_memory_space_constraint(land, pltpu.HBM))


def _chips_wait(send_sems, recv_sems, q, land, after, name):
    def body(q_ref, land_ref, send, recv, after_ref, q_out, land_out):
        for cp in _chip_copies(q_ref, land_ref, send, recv):
            cp.wait_send()
            cp.wait_recv()

    return pl.pallas_call(
        body, name=name, out_shape=(pltpu.HBM(q.shape, q.dtype), pltpu.HBM(land.shape, land.dtype)),
        in_specs=[HBM_SPEC, HBM_SPEC, SEM_SPEC, SEM_SPEC, ANY_SPEC], out_specs=(HBM_SPEC, HBM_SPEC),
        input_output_aliases={0: 0, 1: 1}, compiler_params=pltpu.CompilerParams(has_side_effects=DATAFLOW),
    )(q, land, send_sems, recv_sems, after)[1]


def _exchange_cores(parts, name):
    n = len(parts)

    def body(*refs):
        ins, outs = refs[:n], refs[n:2 * n]
        send_sems, recv_sems = refs[2 * n:]
        x, y, c = _place()
        copies = []
        for a in range(n):
            for k in range(4):
                copies.append(pltpu.make_async_remote_copy(
                    src_ref=ins[a].at[2 * k + (1 - c)], dst_ref=outs[a].at[k], send_sem=send_sems.at[4 * a + k],
                    recv_sem=recv_sems.at[4 * a + k], device_id=(x, y, 1 - c), device_id_type=MESH))
        for cp in copies:
            cp.start()
        for cp in copies:
            cp.wait()

    any_spec = pl.BlockSpec(memory_space=pl.ANY)
    return pl.pallas_call(
        body, name=name, in_specs=[any_spec] * n, out_specs=[any_spec] * n,
        out_shape=[jax.ShapeDtypeStruct((4, *s.shape[1:]), s.dtype) for s in parts],
        scratch_shapes=[pltpu.SemaphoreType.DMA((4 * n,)), pltpu.SemaphoreType.DMA((4 * n,))],
    )(*parts)


def _chip_sum(part, got, place, name, tr=256):
    _, r, c = part.shape
    tr, tc = _tile2d(r, c, tr)

    def body(place_ref, p_ref, g_ref, q_ref, own_ref):
        s = p_ref[0].astype(F32) + g_ref[0].astype(F32)
        q_ref[0] = s.astype(BF16)

        @pl.when(pl.program_id(2) == place_ref[1])
        def _():
            own_ref[...] = s

    grid_spec = pltpu.PrefetchScalarGridSpec(
        num_scalar_prefetch=1, grid=(r // tr, c // tc, 4),
        in_specs=[pl.BlockSpec((1, tr, tc), lambda i, j, k, pr: (2 * k + pr[0], i, j)),
                  pl.BlockSpec((1, tr, tc), lambda i, j, k, pr: (k, i, j))],
        out_specs=[pl.BlockSpec((1, tr, tc), lambda i, j, k, pr: (k, i, j)),
                   pl.BlockSpec((tr, tc), lambda i, j, k, pr: (i, j))])
    return pl.pallas_call(
        body, name=name, grid_spec=grid_spec,
        out_shape=[jax.ShapeDtypeStruct((4, r, c), BF16), jax.ShapeDtypeStruct((r, c), F32)],
        compiler_params=_params(3),
    )(place, part, got)


def _adamw(w, g, m, v):
    m = ADAM_B1 * m + (1.0 - ADAM_B1) * g
    v = ADAM_B2 * v + (1.0 - ADAM_B2) * jnp.square(g)
    m_hat = m / (1.0 - ADAM_B1 ** ADAM_STEP)
    v_hat = v / (1.0 - ADAM_B2 ** ADAM_STEP)
    return -ADAM_LR * (m_hat / (jnp.sqrt(v_hat) + ADAM_EPS) + ADAM_WD * w), m, v


def _sum_adamw(own, got, w, m, v, name, tr=256):
    r, c = w.shape
    tr, tc = _tile2d(r, c, tr)

    def body(own_ref, got_ref, w_ref, m_ref, v_ref, g_ref, d_ref, nm_ref, nv_ref):
        g = own_ref[...]
        for j in range(3):
            g = g + got_ref[j].astype(F32)
        g_ref[...] = g
        d_ref[...], nm_ref[...], nv_ref[...] = _adamw(w_ref[...], g, m_ref[...], v_ref[...])

    blk = pl.BlockSpec((tr, tc), lambda i, j: (i, j))
    return pl.pallas_call(
        body, name=name, grid=(r // tr, c // tc),
        in_specs=[blk, pl.BlockSpec((3, tr, tc), lambda i, j: (0, i, j)), blk, blk, blk], out_specs=[blk] * 4,
        out_shape=[jax.ShapeDtypeStruct((r, c), F32)] * 4, compiler_params=_params(2),
    )(own, got, w, m, v)


def _sum_devices(parts, name):
    _, r, c = parts.shape
    tr = r

    def body(p_ref, o_ref):
        s = p_ref[0]
        for d in range(1, N_DEV):
            s = s + p_ref[d]
        o_ref[...] = s

    return pl.pallas_call(
        body, name=name, grid=(pl.cdiv(r, tr),), in_specs=[pl.BlockSpec((N_DEV, tr, c), lambda i: (0, i, 0))],
        out_specs=pl.BlockSpec((tr, c), lambda i: (i, 0)), out_shape=jax.ShapeDtypeStruct((r, c), F32),
        compiler_params=_params(1),
    )(parts)


def _adamw_call(w, g, m, v, name):
    r, c = w.shape
    tr = r

    def body(w_ref, g_ref, m_ref, v_ref, d_ref, nm_ref, nv_ref):
        d_ref[...], nm_ref[...], nv_ref[...] = _adamw(w_ref[...], g_ref[...], m_ref[...], v_ref[...])

    blk = pl.BlockSpec((tr, c), lambda i: (i, 0))
    return pl.pallas_call(
        body, name=name, grid=(pl.cdiv(r, tr),), in_specs=[blk] * 4, out_specs=[blk] * 3,
        out_shape=[jax.ShapeDtypeStruct((r, c), F32)] * 3, compiler_params=_params(1),
    )(w, g, m, v)


PACK_ROWS = 8


def _pack(arrays):
    parts = []
    for a in arrays:
        flat = a.reshape(-1)
        unit = PACK_ROWS * LANES
        parts.append(jnp.pad(flat, (0, -flat.shape[0] % unit)).reshape(-1, LANES))
    return jnp.concatenate(parts, axis=0)


def _unpack(pack, shapes):
    out, row = [], 0
    for s in shapes:
        size = 1
        for d in s:
            size *= d
        rows = -(-size // (PACK_ROWS * LANES)) * PACK_ROWS
        out.append(pack[row:row + rows].reshape(-1)[:size].reshape(s))
        row += rows
    return out


SMALL = ["norm1_w", "b_gate", "conv_a_b", "dt_bias", "a_log", "d_skip", "ssd_norm_w", "uv_b", "v_ln_w", "v_ln_b",
         "w_spatial", "b_spatial", "norm2_w", "conv_f_b", "final_norm_w"]
BIG = ["w_in", "w_branch", "w_out", "w_up", "w_down"]
TRANSPOSED = ("w_in", "w_up")
WEIGHTS = ["norm1_w", "w_in", "b_gate", "conv_a_w", "conv_a_b", "dt_bias", "a_log", "d_skip", "ssd_norm_w", "uv_b",
           "v_ln_w", "v_ln_b", "w_spatial", "b_spatial", "w_branch", "w_out", "norm2_w", "w_up", "conv_f_w",
           "conv_f_b", "w_down", "final_norm_w"]
IN_SPLITS = [("z", 0, 2048), ("xbc", 2048, 5120), ("dt", 5120, 5152), ("uv", 5152, 7200), ("gates", 7200, 9248)]


def _columns_from_devices(a):
    return a.transpose(1, 0, 2).reshape(a.shape[1], -1)


def kernel(x, norm1_w, w_in, b_gate, conv_a_w, conv_a_b, dt_bias, a_log, d_skip, ssd_norm_w, uv_b, v_ln_w, v_ln_b, w_spatial, b_spatial, w_branch, w_out, norm2_w, w_up, conv_f_w, conv_f_b, w_down, final_norm_w, loss_target, m_norm1_w, m_w_in, m_b_gate, m_conv_a_w, m_conv_a_b, m_dt_bias, m_a_log, m_d_skip, m_ssd_norm_w, m_uv_b, m_v_ln_w, m_v_ln_b, m_w_spatial, m_b_spatial, m_w_branch, m_w_out, m_norm2_w, m_w_up, m_conv_f_w, m_conv_f_b, m_w_down, m_final_norm_w, v_norm1_w, v_w_in, v_b_gate, v_conv_a_w, v_conv_a_b, v_dt_bias, v_a_log, v_d_skip, v_ssd_norm_w, v_uv_b, v_v_ln_w, v_v_ln_b, v_w_spatial, v_b_spatial, v_w_branch, v_w_out, v_norm2_w, v_w_up, v_conv_f_w, v_conv_f_b, v_w_down, v_final_norm_w):
    args = dict(locals())
    wts = {n: args[n] for n in WEIGHTS}
    mom = {n: args["m_" + n] for n in WEIGHTS}
    var = {n: args["v_" + n] for n in WEIGHTS}
    cx, cy, cc = _place()
    dev = 4 * cx + 2 * cy + cc
    place = jnp.stack([cc, 2 * cx + cy]).astype(jnp.int32)

    def shard2d(n, a):
        return a[0].T if n in TRANSPOSED else a[0]

    def unshard(n, b):
        return (b.T if n in TRANSPOSED else b)[None]

    g_in, g_conv_a, g_conv_f = _all_gather(
        [shard2d("w_in", w_in).astype(BF16), conv_a_w[0], conv_f_w[0]], "gather_w_in")
    late = [shard2d(n, wts[n]).astype(BF16) for n in BIG[1:]]
    send_sems, recv_sems, late, lands, token = _gather_start(late, g_in, "gather_late_start")
    w_in_rows = g_in.reshape(-1, D_MODEL)
    w = {name: w_in_rows[lo:hi] for name, lo, hi in IN_SPLITS}
    w["dt"] = jnp.pad(w["dt"], ((0, DT_PAD - SSD_HEADS), (0, 0)))
    w["conv_a"] = _columns_from_devices(g_conv_a)
    w["conv_f"] = _columns_from_devices(g_conv_f)

    def late_weights(*after):
        got = _gather_wait(send_sems, recv_sems, late, lands, after, "gather_late_wait")
        g_branch, g_out, g_up, g_down = [lax.dynamic_update_index_in_dim(land, mine, dev, 0).reshape(-1, D_MODEL)
                                         for land, mine in zip(got, late)]
        return {"branch_a": g_branch[:SSD_INNER], "branch_b": g_branch[SSD_INNER:], "out": g_out, "up": g_up,
                "down": g_down}

    in_flight = {}

    def on_grad(n, g):
        part = {"w_in": lambda: jnp.concatenate([g[name][:hi - lo] for name, lo, hi in IN_SPLITS], axis=0),
                "w_branch": lambda: jnp.concatenate([g["branch_a"], g["branch_b"]], axis=0),
                "w_out": lambda: g["out"], "w_up": lambda: g["up"], "w_down": lambda: g["down"]}[n]()
        part = part.reshape(N_DEV, -1, D_MODEL)
        from_core, = _exchange_cores([part], f"to_other_core_{n}")
        q, own = _chip_sum(part, from_core, place, f"chip_sum_{n}")
        send, recv, q, land, tok = _chips_start(q, f"to_other_chips_start_{n}")
        in_flight[n] = (own, send, recv, q, land)
        return tok

    p = {n: wts[n][0] if wts[n].ndim > 2 else wts[n].reshape(1, -1) for n in SMALL}
    loss, gx, g = _local_step(x[0], loss_target[0], w, p, after=token, late_weights=late_weights, on_grad=on_grad)
    loss = lax.psum(loss[0, 0], ("x", "y", "c"))

    small_g = [g[n] for n in SMALL] + [g["conv_a"], g["conv_f"]]
    s_send, s_recv, s_mine, s_land, _ = _gather_start([_pack(small_g)], gx, "gather_small_start")

    grads, delta, new_m, new_v = {}, {}, {}, {}

    def big_adamw(n, after):
        own, send, recv, q, land = in_flight[n]
        got = _chips_wait(send, recv, q, land, after, f"to_other_chips_wait_{n}")
        out = _sum_adamw(own, got, shard2d(n, wts[n]), shard2d(n, mom[n]), shard2d(n, var[n]), f"adamw_{n}")
        grads[n], delta[n], new_m[n], new_v[n] = [unshard(n, o) for o in out]
        return out[1]

    after = gx
    for n in ("w_down", "w_up", "w_out", "w_branch"):
        after = big_adamw(n, after)
    gathered, = _gather_wait(s_send, s_recv, s_mine, s_land, [after], "gather_small_wait")
    gathered = lax.dynamic_update_index_in_dim(gathered, s_mine[0], dev, 0)
    small_sum = _unpack(_sum_devices(gathered, "sum_small_grads"), [a.shape for a in small_g])
    grads.update({n: s.reshape(wts[n].shape) for n, s in zip(SMALL, small_sum[:len(SMALL)])})
    for n, s in (("conv_a_w", small_sum[-2]), ("conv_f_w", small_sum[-1])):
        cols = wts[n].shape[2]
        grads[n] = lax.dynamic_slice_in_dim(s, dev * cols, cols, axis=1)[None]
    small_names = SMALL + ["conv_a_w", "conv_f_w"]
    packs = [_pack([t[n] for n in small_names]) for t in (wts, grads, mom, var)]
    outs = _adamw_call(*packs, "adamw_small")
    shapes = [wts[n].shape for n in small_names]
    for tgt, pack in zip((delta, new_m, new_v), outs):
        tgt.update(dict(zip(small_names, _unpack(pack, shapes))))
    big_adamw("w_in", outs[0])

    return (loss, gx[None], *[grads[n] for n in WEIGHTS], *[delta[n] for n in WEIGHTS],
            *[new_m[n] for n in WEIGHTS], *[new_v[n] for n in WEIGHTS])
```

```python
import functools

import jax
import jax.numpy as jnp
from jax import lax
from jax.experimental import pallas as pl
from jax.experimental.pallas import tpu as pltpu

F32, BF16 = jnp.float32, jnp.bfloat16
HIGHEST = lax.Precision.HIGHEST

D_MODEL = 1024
SSD_INNER = 2048
SSD_HEAD_DIM = 64
SSD_HEADS = 32
SSD_GROUPS = 4
SSD_STATE = 128
SSD_BC = SSD_GROUPS * SSD_STATE
SSD_XBC = SSD_INNER + 2 * SSD_BC
SSD_CONV = 4
CHUNK = 128
N_PAIRS = SSD_HEADS // 2
PAIRS_PER_GROUP = N_PAIRS // SSD_GROUPS
SGU_WIDTH = 1024
SGU_GROUPS = 8
D_FF = 2816
FFN_CONV = 3
NORM_EPS = 1e-6
LN_EPS = 1e-5
LANES = 128
DT_PAD = LANES

ADAM_LR, ADAM_B1, ADAM_B2, ADAM_EPS, ADAM_WD, ADAM_STEP = 0.001, 0.9, 0.999, 1e-08, 0.01, 10

N_DEV = 8
VMEM_LIMIT = 56 * 1024 * 1024
MESH = pl.DeviceIdType.MESH


def _params(n_grid, **kw):
    sem = dict(dimension_semantics=("arbitrary",) * n_grid) if n_grid else {}
    return pltpu.CompilerParams(vmem_limit_bytes=VMEM_LIMIT, **sem, **kw)


def _tile(n, pref):
    t = (min(pref, n) // LANES) * LANES
    while n % t:
        t -= LANES
    return t


def _row_tile(r, pref):
    for t in range(min(pref, r) // 16 * 16, 0, -16):
        if r % t == 0:
            return t
    return r


def _tile2d(r, c, rows):
    if r % 16 == 0:
        return _row_tile(r, rows), c
    return r, _tile(c, 2 * LANES)


def _rows(tm, n, nt=None, rev=False, col=0):
    if rev:
        return pl.BlockSpec((tm, n), lambda i: (nt - 1 - i, col))
    return pl.BlockSpec((tm, n), lambda i: (i, col))


def _halo(tm, n, nt=None, rev=False):
    per = tm // 8
    if rev:
        return pl.BlockSpec((8, n), lambda i: (jnp.maximum((nt - 1 - i) * per - 1, 0), 0))
    return pl.BlockSpec((8, n), lambda i: (jnp.maximum(i * per - 1, 0), 0))


def _full(shape):
    nd = len(shape)
    return pl.BlockSpec(shape, lambda *_: (0,) * nd)


def _rms(x, w, eps=NORM_EPS):
    return x * lax.rsqrt(jnp.mean(x * x, axis=-1, keepdims=True) + eps) * w


def _layer_norm(x, w, b):
    mu = jnp.mean(x, axis=-1, keepdims=True)
    var = jnp.mean(jnp.square(x - mu), axis=-1, keepdims=True)
    return (x - mu) * lax.rsqrt(var + LN_EPS) * w + b


def _sigmoid(x):
    return 1.0 / (1.0 + jnp.exp(-x))


def _silu(x):
    return x * _sigmoid(x)


def _dsilu(x):
    s = _sigmoid(x)
    return s * (1.0 + x * (1.0 - s))


def _softplus(x):
    return jnp.maximum(x, 0.0) + jnp.log(1.0 + jnp.exp(-jnp.abs(x)))


def _gelu(x):
    return jax.nn.gelu(x)


def _dot(a, b):
    return jnp.dot(a, b, preferred_element_type=F32)


def _dot_nt(a, b):
    return lax.dot_general(a, b, (((1,), (1,)), ((), ())), preferred_element_type=F32)


def _dot_tn(a, b):
    return lax.dot_general(a, b, (((0,), (0,)), ((), ())), preferred_element_type=F32)


def _dot_split(p, e):
    hi = p.astype(BF16)
    lo = (p - hi.astype(F32)).astype(BF16)
    return _dot(hi, e) + _dot(lo, e)


def _colsum(x):
    return jnp.sum(x, axis=0, keepdims=True)


def _shift_down(x, halo, j):
    xs = pltpu.roll(x, j, 0)
    hs = pltpu.roll(halo, j, 0)
    r8 = lax.broadcasted_iota(jnp.int32, hs.shape, 0)
    return jnp.concatenate([jnp.where(r8 < j, hs, xs[:8]), xs[8:]], axis=0)


def _shift_up(x, nxt, j):
    n = x.shape[0]
    xs = pltpu.roll(x, n - j, 0)
    ns = pltpu.roll(nxt, 8 - j, 0)
    r8 = lax.broadcasted_iota(jnp.int32, ns.shape, 0)
    return jnp.concatenate([xs[:n - 8], jnp.where(r8 >= 8 - j, ns, xs[n - 8:])], axis=0)


def _causal_conv(x, halo, w, b):
    k = w.shape[0]
    shifted = [x] + [_shift_down(x, halo, j) for j in range(1, k)]
    y = b + w[k - 1:k, :] * x
    for j in range(1, k):
        y = y + w[k - 1 - j:k - j, :] * shifted[j]
    return y, shifted


def _anticausal_conv(dy, nxt, w):
    k = w.shape[0]
    dx = w[k - 1:k, :] * dy
    for j in range(1, k):
        dx = dx + w[k - 1 - j:k - j, :] * _shift_up(dy, nxt, j)
    return dx


def _conv_wgrad(dy, shifted):
    k = len(shifted)
    return jnp.concatenate([_colsum(dy * shifted[k - 1 - i]) for i in range(k)], axis=0)


MM_TILE_PREF = 1408
MM_VMEM_BUDGET = 40 * 1024 * 1024


def _mm_tiles(m, n, k, out_bytes):
    tm, tn = _tile(m, MM_TILE_PREF), _tile(n, MM_TILE_PREF)
    need = lambda tm, tn: 2 * (2 * k * (tm + tn) + out_bytes * tm * tn)
    while need(tm, tn) > MM_VMEM_BUDGET:
        if tn >= tm and tn > LANES:
            tn = _tile(n, tn - LANES)
        else:
            tm = _tile(m, tm - LANES)
    return tm, tn


def _mm(a, b, dims, name, acc=None, out_dtype=F32, after=None):
    if dims == "tn":
        k, m = a.shape
    else:
        m, k = a.shape
    n = b.shape[0] if dims == "nt" else b.shape[1]
    tm, tn = _mm_tiles(m, n, k, 4 * (2 if acc is not None else 1))
    a_spec = pl.BlockSpec((k, tm), lambda j, i: (0, i)) if dims == "tn" else pl.BlockSpec((tm, k), lambda j, i: (i, 0))
    b_spec = pl.BlockSpec((tn, k), lambda j, i: (j, 0)) if dims == "nt" else pl.BlockSpec((k, tn), lambda j, i: (0, j))
    o_spec = pl.BlockSpec((tm, tn), lambda j, i: (i, j))
    dot = {"nn": _dot, "nt": _dot_nt, "tn": _dot_tn}[dims]

    def body(a_ref, b_ref, *rest):
        r = dot(a_ref[...], b_ref[...])
        if acc is not None:
            r = r + rest[0][...]
        rest[-1][...] = r.astype(out_dtype)

    ins, specs = [a, b], [a_spec, b_spec]
    if acc is not None:
        ins.append(acc)
        specs.append(o_spec)
    if after is not None:
        ins.append(after)
        specs.append(pl.BlockSpec(memory_space=pl.ANY))
    return pl.pallas_call(
        body, name=name, grid=(n // tn, m // tm), in_specs=specs, out_specs=o_spec,
        out_shape=jax.ShapeDtypeStruct((m, n), out_dtype), compiler_params=_params(2),
    )(*ins)


def _wgrad(a, d, name):
    return _mm(a, d, "tn", name, out_dtype=BF16)


def _norm_fwd(x, w, name, after=None, tm=512):
    t, d = x.shape

    def body(x_ref, w_ref, *rest):
        rest[-1][...] = _rms(x_ref[...], w_ref[...]).astype(BF16)

    extra, extra_specs = ([after], [_full(after.shape)]) if after is not None else ([], [])
    return pl.pallas_call(
        body, name=name, grid=(t // tm,), in_specs=[_rows(tm, d), _full((1, d))] + extra_specs,
        out_specs=_rows(tm, d), out_shape=jax.ShapeDtypeStruct((t, d), BF16), compiler_params=_params(1),
    )(x, w, *extra)


def _conv_a_fwd(xbc, cw, cb, tm=256):
    t, c = xbc.shape

    def body(x_ref, h_ref, w_ref, b_ref, o_ref):
        halo = jnp.where(pl.program_id(0) > 0, h_ref[...], 0.0)
        y, _ = _causal_conv(x_ref[...], halo, w_ref[...], b_ref[...])
        o_ref[...] = _silu(y)

    return pl.pallas_call(
        body, name="conv_a_fwd", grid=(t // tm,),
        in_specs=[_rows(tm, c), _halo(tm, c), _full(cw.shape), _full((1, c))], out_specs=_rows(tm, c),
        out_shape=jax.ShapeDtypeStruct((t, c), F32), compiler_params=_params(1),
    )(xbc, xbc, cw, cb)


def _ssd_common(dtr, dtb, alog):
    row = lax.broadcasted_iota(jnp.int32, (CHUNK, CHUNK), 0)
    col = lax.broadcasted_iota(jnp.int32, (CHUNK, CHUNK), 1)
    causal = row >= col
    dt = _softplus(dtr + dtb)
    a = -jnp.exp(alog)
    acum = jnp.dot(causal.astype(F32), dt * a, precision=HIGHEST, preferred_element_type=F32)
    return dt, a, acum, acum.T, causal, col < SSD_HEAD_DIM, row


def _pair_terms(j, dt, acum, acum_t, causal, lane_lo):
    h0, h1 = 2 * j, 2 * j + 1
    ac0, ac1 = acum[:, h0:h0 + 1], acum[:, h1:h1 + 1]
    l0 = jnp.exp(jnp.where(causal, ac0 - acum_t[h0:h0 + 1, :], -jnp.inf))
    l1 = jnp.exp(jnp.where(causal, ac1 - acum_t[h1:h1 + 1, :], -jnp.inf))
    dtp = jnp.where(lane_lo, dt[:, h0:h0 + 1], dt[:, h1:h1 + 1])
    al0, al1 = acum[CHUNK - 1:CHUNK, h0:h0 + 1], acum[CHUNK - 1:CHUNK, h1:h1 + 1]
    ecol = jnp.where(lane_lo, jnp.exp(ac0), jnp.exp(ac1))
    dsr = jnp.where(lane_lo, jnp.exp(al0 - ac0), jnp.exp(al1 - ac1))
    elast = jnp.where(lane_lo[0:1], jnp.exp(al0), jnp.exp(al1))
    return l0, l1, dtp, ecol, dsr, elast


def _ssd_fwd(xc, dtr, z, dtb, alog, dsk, nw):
    t = xc.shape[0]
    nc = t // CHUNK

    def body(xs_ref, b_ref, c_ref, dtr_ref, z_ref, dtb_ref, alog_ref, dsk_ref, nw_ref, y_ref, ya_ref, sp_ref, s_scr):
        @pl.when(pl.program_id(0) == 0)
        def _():
            s_scr[...] = jnp.zeros_like(s_scr)

        dt, a, acum, acum_t, causal, lane_lo, _ = _ssd_common(dtr_ref[...], dtb_ref[...], alog_ref[...])
        dsk = dsk_ref[...]
        for g in range(SSD_GROUPS):
            gs = slice(g * SSD_STATE, (g + 1) * SSD_STATE)
            bg, cg = b_ref[:, gs].astype(BF16), c_ref[:, gs].astype(BF16)
            cb = _dot_nt(cg, bg)
            for pp in range(PAIRS_PER_GROUP):
                j = g * PAIRS_PER_GROUP + pp
                ps = slice(j * LANES, (j + 1) * LANES)
                x = xs_ref[:, ps]
                l0, l1, dtp, ecol, dsr, elast = _pair_terms(j, dt, acum, acum_t, causal, lane_lo)
                xdt = x * dtp
                xb = xdt.astype(BF16)
                zero = jnp.zeros_like(xb)
                yd = (_dot((cb * l0).astype(BF16), jnp.where(lane_lo, xb, zero))
                      + _dot((cb * l1).astype(BF16), jnp.where(lane_lo, zero, xb)))
                sp = s_scr[j]
                yo = ecol * _dot(cg, sp.astype(BF16))
                st = _dot_tn(bg, (xdt * dsr).astype(BF16))
                sp_ref[0, j] = sp
                s_scr[j] = elast * sp + st
                dskp = jnp.where(lane_lo[0:1], dsk[:, 2 * j:2 * j + 1], dsk[:, 2 * j + 1:2 * j + 2])
                y_ref[:, ps] = yd + yo + dskp * x
        ya_ref[...] = _rms(y_ref[...] * _silu(z_ref[...]), nw_ref[...]).astype(BF16)

    ck = lambda n, col=0: pl.BlockSpec((CHUNK, n), lambda c: (c, col))
    return pl.pallas_call(
        body, name="ssd_fwd", grid=(nc,),
        in_specs=[ck(SSD_INNER), ck(SSD_BC, SSD_INNER // SSD_BC), ck(SSD_BC, SSD_INNER // SSD_BC + 1), ck(DT_PAD),
                  ck(SSD_INNER), _full((1, DT_PAD)), _full((1, DT_PAD)), _full((1, DT_PAD)), _full((1, SSD_INNER))],
        out_specs=[ck(SSD_INNER), ck(SSD_INNER),
                   pl.BlockSpec((1, N_PAIRS, SSD_STATE, LANES), lambda c: (c, 0, 0, 0))],
        out_shape=[jax.ShapeDtypeStruct((t, SSD_INNER), F32), jax.ShapeDtypeStruct((t, SSD_INNER), BF16),
                   jax.ShapeDtypeStruct((nc, N_PAIRS, SSD_STATE, LANES), F32)],
        scratch_shapes=[pltpu.VMEM((N_PAIRS, SSD_STATE, LANES), F32)], compiler_params=_params(1),
    )(xc, xc, xc, dtr, z, dtb, alog, dsk, nw)


def _ssd_bwd(dya, y, z, xc, dtr, sprev, dtb, alog, dsk, nw, e_heads):
    t = xc.shape[0]
    nc = t // CHUNK

    def body(dya_ref, y_ref, z_ref, xs_ref, b_ref, c_ref, dtr_ref, sp_ref, dtb_ref, alog_ref, dsk_ref, nw_ref, e_ref,
             dz_ref, dxs_ref, db_ref, dc_ref, ddtr_ref, dnw_ref, ddtb_ref, dalog_ref, ddsk_ref, ds_scr):
        @pl.when(pl.program_id(0) == 0)
        def _():
            ds_scr[...] = jnp.zeros_like(ds_scr)
            for r in (dnw_ref, ddtb_ref, dalog_ref, ddsk_ref):
                r[...] = jnp.zeros_like(r)

        y = y_ref[...]
        _, gate_vjp = jax.vjp(lambda y_, z_, w_: _rms(y_ * _silu(z_), w_), y, z_ref[...], nw_ref[...])
        dy, dz, dnw = gate_vjp(dya_ref[...])
        dz_ref[...] = dz.astype(BF16)
        dnw_ref[...] += dnw

        dtr = dtr_ref[...]
        dt, a, acum, acum_t, causal, lane_lo, row = _ssd_common(dtr, dtb_ref[...], alog_ref[...])
        dsk = dsk_ref[...]
        p_a, p_dt, v_last = [], [], []
        col = lax.broadcasted_iota(jnp.int32, (CHUNK, CHUNK), 1)
        da_cols = jnp.zeros((CHUNK, CHUNK), F32)
        da_rows = jnp.zeros((CHUNK, CHUNK), F32)
        for g in range(SSD_GROUPS):
            gs = slice(g * SSD_STATE, (g + 1) * SSD_STATE)
            bg, cg = b_ref[:, gs].astype(BF16), c_ref[:, gs].astype(BF16)
            cb = _dot_nt(cg, bg)
            dcb = jnp.zeros((CHUNK, CHUNK), F32)
            dbg = jnp.zeros((CHUNK, SSD_STATE), F32)
            dcg = jnp.zeros((CHUNK, SSD_STATE), F32)
            for pp in range(PAIRS_PER_GROUP):
                j = g * PAIRS_PER_GROUP + pp
                ps = slice(j * LANES, (j + 1) * LANES)
                x = xs_ref[:, ps]
                l0, l1, dtp, ecol, dsr, elast = _pair_terms(j, dt, acum, acum_t, causal, lane_lo)
                xdt = x * dtp
                xb = xdt.astype(BF16)
                dskp = jnp.where(lane_lo[0:1], dsk[:, 2 * j:2 * j + 1], dsk[:, 2 * j + 1:2 * j + 2])
                dyp = dy[:, ps]
                dyb = dyp.astype(BF16)
                sp, dsn = sp_ref[0, j], ds_scr[j]
                spb, dsnb = sp.astype(BF16), dsn.astype(BF16)
                y_off = ecol * _dot(cg, spb)
                dw = (dyp * ecol).astype(BF16)
                dcg = dcg + _dot_nt(dw, spb)
                dsp = _dot_tn(cg, dw) + elast * dsn
                xd = xdt * dsr
                zd = _dot(bg, dsnb) * dsr
                dbg = dbg + _dot_nt(xd.astype(BF16), dsnb)
                dxdt = zd
                zero = jnp.zeros_like(xb)
                for h, lm, le in ((2 * j, lane_lo, l0), (2 * j + 1, jnp.logical_not(lane_lo), l1)):
                    dm = _dot_nt(jnp.where(lm, dyb, zero), jnp.where(lm, xb, zero))
                    dcb = dcb + dm * le
                    m = cb * le
                    dxdt = dxdt + jnp.where(lm, _dot_tn(m.astype(BF16), dyb), 0.0)
                    q = dm * m
                    da_cols = da_cols + jnp.where(col == h, jnp.sum(q, axis=1, keepdims=True), 0.0)
                    da_rows = da_rows + jnp.where(row == h, _colsum(q), 0.0)
                ds_scr[j] = dsp
                dxs_ref[:, ps] = dxdt * dtp + dskp * dyp
                p_a.append(dyp * y_off - xdt * zd)
                p_dt.append(dxdt * x)
                v_last.append(_colsum(zd * xdt) + elast * _colsum(dsn * sp))
            dcbb = dcb.astype(BF16)
            db_ref[:, gs] = dbg + _dot_tn(dcbb, cg)
            dc_ref[:, gs] = dcg + _dot(dcbb, bg)
        e = e_ref[...]
        rows8 = jnp.concatenate([jnp.concatenate(v_last, axis=1), _colsum(dy * xs_ref[...]),
                                 jnp.zeros((6, SSD_INNER), F32)], axis=0)
        r8 = _dot_split(rows8, e)
        da = (_dot_split(jnp.concatenate(p_a, axis=1), e) + jnp.where(row == CHUNK - 1, r8[0:1], 0.0)
              + da_cols - da_rows.T)
        ddsk_ref[...] += r8[1:2]
        dadt = jnp.dot((row <= col).astype(F32), da, precision=HIGHEST, preferred_element_type=F32)
        ddt = dadt * a + _dot_split(jnp.concatenate(p_dt, axis=1), e)
        dalog_ref[...] += _colsum(dadt * dt) * a
        ddtr = ddt * _sigmoid(dtr + dtb_ref[...])
        ddtr_ref[...] = ddtr
        ddtb_ref[...] += _colsum(ddtr)

    ck = lambda n, col=0: pl.BlockSpec((CHUNK, n), lambda c: (nc - 1 - c, col))
    acc = lambda n: _full((1, n))
    return pl.pallas_call(
        body, name="ssd_bwd", grid=(nc,),
        in_specs=[ck(SSD_INNER), ck(SSD_INNER), ck(SSD_INNER), ck(SSD_INNER), ck(SSD_BC, SSD_INNER // SSD_BC),
                  ck(SSD_BC, SSD_INNER // SSD_BC + 1), ck(DT_PAD),
                  pl.BlockSpec((1, N_PAIRS, SSD_STATE, LANES), lambda c: (nc - 1 - c, 0, 0, 0)),
                  acc(DT_PAD), acc(DT_PAD), acc(DT_PAD), acc(SSD_INNER), _full((SSD_INNER, LANES))],
        out_specs=[ck(SSD_INNER), ck(SSD_INNER), ck(SSD_BC), ck(SSD_BC), ck(DT_PAD),
                   acc(SSD_INNER), acc(DT_PAD), acc(DT_PAD), acc(DT_PAD)],
        out_shape=[jax.ShapeDtypeStruct((t, SSD_INNER), BF16), jax.ShapeDtypeStruct((t, SSD_INNER), F32),
                   jax.ShapeDtypeStruct((t, SSD_BC), F32), jax.ShapeDtypeStruct((t, SSD_BC), F32),
                   jax.ShapeDtypeStruct((t, DT_PAD), F32), jax.ShapeDtypeStruct((1, SSD_INNER), F32),
                   jax.ShapeDtypeStruct((1, DT_PAD), F32), jax.ShapeDtypeStruct((1, DT_PAD), F32),
                   jax.ShapeDtypeStruct((1, DT_PAD), F32)],
        scratch_shapes=[pltpu.VMEM((N_PAIRS, SSD_STATE, LANES), F32)], compiler_params=_params(1),
    )(dya, y, z, xc, xc, xc, dtr, sprev, dtb, alog, dsk, nw, e_heads)


def _sgu_act(uv, uvb, lnw, lnb):
    a = _gelu(uv + uvb)
    return a[:, :SGU_WIDTH], _layer_norm(a[:, SGU_WIDTH:], lnw, lnb)


def _sgu_weights(ws_ref):
    row = lax.broadcasted_iota(jnp.int32, (CHUNK, CHUNK), 0)
    col = lax.broadcasted_iota(jnp.int32, (CHUNK, CHUNK), 1)
    return [jnp.where(row >= col, ws_ref[g], 0.0).astype(BF16) for g in range(SGU_GROUPS)], row >= col


def _sgu_fwd(uv, uvb, lnw, lnb, ws, bs_t):
    t = uv.shape[0]

    def body(uv_ref, uvb_ref, lnw_ref, lnb_ref, ws_ref, bs_ref, o_ref):
        u, vn = _sgu_act(uv_ref[...], uvb_ref[...], lnw_ref[...], lnb_ref[...])
        wc, _ = _sgu_weights(ws_ref)
        bs = bs_ref[...]
        for g in range(SGU_GROUPS):
            gs = slice(g * LANES, (g + 1) * LANES)
            mixed = _dot(wc[g], vn[:, gs].astype(BF16)) + bs[:, g:g + 1]
            o_ref[:, gs] = (u[:, gs] * mixed).astype(BF16)

    return pl.pallas_call(
        body, name="sgu_fwd", grid=(t // CHUNK,),
        in_specs=[_rows(CHUNK, 2 * SGU_WIDTH), _full((1, 2 * SGU_WIDTH)), _full((1, SGU_WIDTH)), _full((1, SGU_WIDTH)),
                  _full(ws.shape), _full(bs_t.shape)],
        out_specs=_rows(CHUNK, SGU_WIDTH), out_shape=jax.ShapeDtypeStruct((t, SGU_WIDTH), BF16),
        compiler_params=_params(1),
    )(uv, uvb, lnw, lnb, ws, bs_t)


def _sgu_bwd(dyb, uv, uvb, lnw, lnb, ws, bs_t, e_groups):
    t = uv.shape[0]

    def body(dyb_ref, uv_ref, uvb_ref, lnw_ref, lnb_ref, ws_ref, bs_ref, e_ref,
             duv_ref, duvb_ref, dlnw_ref, dlnb_ref, dws_ref, dbs_ref):
        @pl.when(pl.program_id(0) == 0)
        def _():
            for r in (duvb_ref, dlnw_ref, dlnb_ref, dws_ref, dbs_ref):
                r[...] = jnp.zeros_like(r)

        (u, vn), act_vjp = jax.vjp(_sgu_act, uv_ref[...], uvb_ref[...], lnw_ref[...], lnb_ref[...])
        wc, causal = _sgu_weights(ws_ref)
        bs = bs_ref[...]
        dyb = dyb_ref[...]
        du, dvn, dmix = [], [], []
        for g in range(SGU_GROUPS):
            gs = slice(g * LANES, (g + 1) * LANES)
            vb = vn[:, gs].astype(BF16)
            mixed = _dot(wc[g], vb) + bs[:, g:g + 1]
            dm = dyb[:, gs] * u[:, gs]
            dmb = dm.astype(BF16)
            du.append(dyb[:, gs] * mixed)
            dvn.append(_dot_tn(wc[g], dmb))
            dws_ref[g] += jnp.where(causal, _dot_nt(dmb, vb), 0.0)
            dmix.append(dm)
        dbs_ref[...] += _dot_split(jnp.concatenate(dmix, axis=1), e_ref[...])
        duv, duvb, dlnw, dlnb = act_vjp((jnp.concatenate(du, axis=1), jnp.concatenate(dvn, axis=1)))
        duv_ref[...] = duv.astype(BF16)
        duvb_ref[...] += duvb
        dlnw_ref[...] += dlnw
        dlnb_ref[...] += dlnb

    return pl.pallas_call(
        body, name="sgu_bwd", grid=(t // CHUNK,),
        in_specs=[_rows(CHUNK, SGU_WIDTH), _rows(CHUNK, 2 * SGU_WIDTH), _full((1, 2 * SGU_WIDTH)),
                  _full((1, SGU_WIDTH)), _full((1, SGU_WIDTH)), _full(ws.shape), _full(bs_t.shape),
                  _full(e_groups.shape)],
        out_specs=[_rows(CHUNK, 2 * SGU_WIDTH), _full((1, 2 * SGU_WIDTH)), _full((1, SGU_WIDTH)),
                   _full((1, SGU_WIDTH)), _full(ws.shape), _full(bs_t.shape)],
        out_shape=[jax.ShapeDtypeStruct((t, 2 * SGU_WIDTH), BF16), jax.ShapeDtypeStruct((1, 2 * SGU_WIDTH), F32),
                   jax.ShapeDtypeStruct((1, SGU_WIDTH), F32), jax.ShapeDtypeStruct((1, SGU_WIDTH), F32),
                   jax.ShapeDtypeStruct(ws.shape, F32), jax.ShapeDtypeStruct(bs_t.shape, F32)],
        compiler_params=_params(1),
    )(dyb, uv, uvb, lnw, lnb, ws, bs_t, e_groups)


def _merge(gates, pa, pb, bg):
    s = _sigmoid(gates + bg)
    return s[:, :D_MODEL] * pa + s[:, D_MODEL:] * pb


def _merge_fwd(gates, pa, pb, bg, tm=256):
    t = gates.shape[0]

    def body(g_ref, pa_ref, pb_ref, bg_ref, o_ref):
        o_ref[...] = _merge(g_ref[...], pa_ref[...], pb_ref[...], bg_ref[...]).astype(BF16)

    return pl.pallas_call(
        body, name="merge_fwd", grid=(t // tm,),
        in_specs=[_rows(tm, 2 * D_MODEL), _rows(tm, D_MODEL), _rows(tm, D_MODEL), _full((1, 2 * D_MODEL))],
        out_specs=_rows(tm, D_MODEL), out_shape=jax.ShapeDtypeStruct((t, D_MODEL), BF16), compiler_params=_params(1),
    )(gates, pa, pb, bg)


def _merge_bwd(dmix, gates, pa, pb, bg, tm=256):
    t = gates.shape[0]

    def body(d_ref, g_ref, pa_ref, pb_ref, bg_ref, dg_ref, dpa_ref, dpb_ref, dbg_ref):
        @pl.when(pl.program_id(0) == 0)
        def _():
            dbg_ref[...] = jnp.zeros_like(dbg_ref)

        _, vjp = jax.vjp(_merge, g_ref[...], pa_ref[...], pb_ref[...], bg_ref[...])
        dg, dpa, dpb, dbg = vjp(d_ref[...])
        dg_ref[...] = dg.astype(BF16)
        dpa_ref[...] = dpa.astype(BF16)
        dpb_ref[...] = dpb.astype(BF16)
        dbg_ref[...] += dbg

    return pl.pallas_call(
        body, name="merge_bwd", grid=(t // tm,),
        in_specs=[_rows(tm, D_MODEL), _rows(tm, 2 * D_MODEL), _rows(tm, D_MODEL), _rows(tm, D_MODEL),
                  _full((1, 2 * D_MODEL))],
        out_specs=[_rows(tm, 2 * D_MODEL), _rows(tm, D_MODEL), _rows(tm, D_MODEL), _full((1, 2 * D_MODEL))],
        out_shape=[jax.ShapeDtypeStruct((t, 2 * D_MODEL), BF16), jax.ShapeDtypeStruct((t, D_MODEL), BF16),
                   jax.ShapeDtypeStruct((t, D_MODEL), BF16), jax.ShapeDtypeStruct((1, 2 * D_MODEL), F32)],
        compiler_params=_params(1),
    )(dmix, gates, pa, pb, bg)


def _residual_norm_fwd(x, o, w, tm=512):
    t, d = x.shape

    def body(x_ref, o_ref, w_ref, h_ref, n_ref):
        h = x_ref[...] + o_ref[...]
        h_ref[...] = h
        n_ref[...] = _rms(h, w_ref[...]).astype(BF16)

    return pl.pallas_call(
        body, name="residual_norm_fwd", grid=(t // tm,), in_specs=[_rows(tm, d), _rows(tm, d), _full((1, d))],
        out_specs=[_rows(tm, d), _rows(tm, d)],
        out_shape=[jax.ShapeDtypeStruct((t, d), F32), jax.ShapeDtypeStruct((t, d), BF16)], compiler_params=_params(1),
    )(x, o, w)


def _norm_bwd(dn, h, w, dres, name, tm=512):
    t, d = h.shape

    def body(dn_ref, h_ref, w_ref, dres_ref, dh_ref, dhb_ref, dw_ref):
        @pl.when(pl.program_id(0) == 0)
        def _():
            dw_ref[...] = jnp.zeros_like(dw_ref)

        _, vjp = jax.vjp(_rms, h_ref[...], w_ref[...])
        dh, dw = vjp(dn_ref[...])
        dh = dh + dres_ref[...]
        dh_ref[...] = dh
        dhb_ref[...] = dh.astype(BF16)
        dw_ref[...] += dw

    return pl.pallas_call(
        body, name=name, grid=(t // tm,), in_specs=[_rows(tm, d), _rows(tm, d), _full((1, d)), _rows(tm, d)],
        out_specs=[_rows(tm, d), _rows(tm, d), _full((1, d))],
        out_shape=[jax.ShapeDtypeStruct((t, d), F32), jax.ShapeDtypeStruct((t, d), BF16),
                   jax.ShapeDtypeStruct((1, d), F32)], compiler_params=_params(1),
    )(dn, h, w, dres)


def _conv_f_fwd(up, cw, cb, tm=128):
    t, c = up.shape

    def body(x_ref, h_ref, w_ref, b_ref, o_ref):
        halo = jnp.where(pl.program_id(0) > 0, h_ref[...], 0.0)
        y, _ = _causal_conv(x_ref[...], halo, w_ref[...], b_ref[...])
        o_ref[...] = (_silu(y[:, :D_FF]) * y[:, D_FF:]).astype(BF16)

    return pl.pallas_call(
        body, name="conv_f_fwd", grid=(t // tm,),
        in_specs=[_rows(tm, c), _halo(tm, c), _full(cw.shape), _full((1, c))], out_specs=_rows(tm, D_FF),
        out_shape=jax.ShapeDtypeStruct((t, D_FF), BF16), compiler_params=_params(1),
    )(up, up, cw, cb)


def _conv_f_bwd(dact, up, cw, cb, tm=128):
    t, c = up.shape
    nt = t // tm

    def body(d_ref, x_ref, h_ref, w_ref, b_ref, dx_ref, dw_ref, db_ref, nxt_scr):
        @pl.when(pl.program_id(0) == 0)
        def _():
            nxt_scr[...] = jnp.zeros_like(nxt_scr)
            dw_ref[...] = jnp.zeros_like(dw_ref)
            db_ref[...] = jnp.zeros_like(db_ref)

        halo = jnp.where(pl.program_id(0) < nt - 1, h_ref[...], 0.0)
        w = w_ref[...]
        y, shifted = _causal_conv(x_ref[...], halo, w, b_ref[...])
        a, v = y[:, :D_FF], y[:, D_FF:]
        d = d_ref[...]
        dy = jnp.concatenate([d * v * _dsilu(a), d * _silu(a)], axis=1)
        dx_ref[...] = _anticausal_conv(dy, nxt_scr[...], w).astype(BF16)
        nxt_scr[...] = dy[:8]
        dw_ref[...] += _conv_wgrad(dy, shifted)
        db_ref[...] += _colsum(dy)

    return pl.pallas_call(
        body, name="conv_f_bwd", grid=(nt,),
        in_specs=[_rows(tm, D_FF, nt, True), _rows(tm, c, nt, True), _halo(tm, c, nt, True), _full(cw.shape),
                  _full((1, c))],
        out_specs=[_rows(tm, c, nt, True), _full(cw.shape), _full((1, c))],
        out_shape=[jax.ShapeDtypeStruct((t, c), BF16), jax.ShapeDtypeStruct(cw.shape, F32),
                   jax.ShapeDtypeStruct((1, c), F32)],
        scratch_shapes=[pltpu.VMEM((8, c), F32)], compiler_params=_params(1),
    )(dact, up, up, cw, cb)


def _conv_a_bwd(dxs, db, dc, xbc, cw, cb, tm=256):
    t, c = xbc.shape
    nt = t // tm

    def body(dxs_ref, db_ref, dc_ref, x_ref, h_ref, w_ref, b_ref, dx_ref, dw_ref, dbias_ref, nxt_scr):
        @pl.when(pl.program_id(0) == 0)
        def _():
            nxt_scr[...] = jnp.zeros_like(nxt_scr)
            dw_ref[...] = jnp.zeros_like(dw_ref)
            dbias_ref[...] = jnp.zeros_like(dbias_ref)

        halo = jnp.where(pl.program_id(0) < nt - 1, h_ref[...], 0.0)
        w = w_ref[...]
        y, shifted = _causal_conv(x_ref[...], halo, w, b_ref[...])
        dy = jnp.concatenate([dxs_ref[...], db_ref[...], dc_ref[...]], axis=1) * _dsilu(y)
        dx_ref[...] = _anticausal_conv(dy, nxt_scr[...], w).astype(BF16)
        nxt_scr[...] = dy[:8]
        dw_ref[...] += _conv_wgrad(dy, shifted)
        dbias_ref[...] += _colsum(dy)

    return pl.pallas_call(
        body, name="conv_a_bwd", grid=(nt,),
        in_specs=[_rows(tm, SSD_INNER, nt, True), _rows(tm, SSD_BC, nt, True), _rows(tm, SSD_BC, nt, True),
                  _rows(tm, c, nt, True), _halo(tm, c, nt, True), _full(cw.shape), _full((1, c))],
        out_specs=[_rows(tm, c, nt, True), _full(cw.shape), _full((1, c))],
        out_shape=[jax.ShapeDtypeStruct((t, c), BF16), jax.ShapeDtypeStruct(cw.shape, F32),
                   jax.ShapeDtypeStruct((1, c), F32)],
        scratch_shapes=[pltpu.VMEM((8, c), F32)], compiler_params=_params(1),
    )(dxs, db, dc, xbc, xbc, cw, cb)


def _loss_head(h1, dn, w, target, tm=512):
    t, d = h1.shape

    def body(h_ref, dn_ref, w_ref, t_ref, loss_ref, dh_ref, dhb_ref, dw_ref):
        @pl.when(pl.program_id(0) == 0)
        def _():
            loss_ref[...] = jnp.zeros_like(loss_ref)
            dw_ref[...] = jnp.zeros_like(dw_ref)

        yf, vjp = jax.vjp(_rms, h_ref[...] + dn_ref[...], w_ref[...])
        err = yf - t_ref[...]
        loss_ref[...] += 0.5 * jnp.sum(jnp.mean(err * err, axis=-1, keepdims=True))
        dh, dw = vjp(err * (1.0 / d))
        dh_ref[...] = dh
        dhb_ref[...] = dh.astype(BF16)
        dw_ref[...] += dw

    return pl.pallas_call(
        body, name="loss_head", grid=(t // tm,),
        in_specs=[_rows(tm, d), _rows(tm, d), _full((1, d)), _rows(tm, d)],
        out_specs=[_full((8, LANES)), _rows(tm, d), _rows(tm, d), _full((1, d))],
        out_shape=[jax.ShapeDtypeStruct((8, LANES), F32), jax.ShapeDtypeStruct((t, d), F32),
                   jax.ShapeDtypeStruct((t, d), BF16), jax.ShapeDtypeStruct((1, d), F32)], compiler_params=_params(1),
    )(h1, dn, w, target)


def _pad_lanes(v, n=DT_PAD):
    return jnp.pad(v, ((0, 0), (0, n - v.shape[1])))


def _local_step(x, target, w, p, after=None, late_weights=None, on_grad=None):
    dtb, alog, dsk = _pad_lanes(p["dt_bias"]), _pad_lanes(p["a_log"]), _pad_lanes(p["d_skip"])
    bs_t = _pad_lanes(p["b_spatial"].T)
    e_heads = (jnp.arange(SSD_INNER)[:, None] // SSD_HEAD_DIM == jnp.arange(LANES)[None, :]).astype(BF16)
    e_groups = (jnp.arange(SGU_WIDTH)[:, None] // LANES == jnp.arange(LANES)[None, :]).astype(BF16)

    n1 = _norm_fwd(x, p["norm1_w"], "norm1_fwd", after=after)
    z = _mm(n1, w["z"], "nt", "proj_z")
    xbc = _mm(n1, w["xbc"], "nt", "proj_xbc")
    dtr = _mm(n1, w["dt"], "nt", "proj_dt")
    uv = _mm(n1, w["uv"], "nt", "proj_uv")
    gates = _mm(n1, w["gates"], "nt", "proj_gates")
    xc = _conv_a_fwd(xbc, w["conv_a"], p["conv_a_b"])
    y, ya, sprev = _ssd_fwd(xc, dtr, z, dtb, alog, dsk, p["ssd_norm_w"])
    yb = _sgu_fwd(uv, p["uv_b"], p["v_ln_w"], p["v_ln_b"], p["w_spatial"], bs_t)
    if late_weights is not None:
        w = {**w, **late_weights(ya, yb)}
    pa = _mm(ya, w["branch_a"], "nn", "branch_a")
    pb = _mm(yb, w["branch_b"], "nn", "branch_b")
    mix = _merge_fwd(gates, pa, pb, p["b_gate"])
    o = _mm(mix, w["out"], "nn", "out_proj")
    h1, n2 = _residual_norm_fwd(x, o, p["norm2_w"])
    up = _mm(n2, w["up"], "nt", "up_proj")
    act = _conv_f_fwd(up, w["conv_f"], p["conv_f_b"])
    dn = _mm(act, w["down"], "nn", "down_proj")
    loss, dh2, dh2b, g_final = _loss_head(h1, dn, p["final_norm_w"], target)

    on_grad = on_grad or (lambda name, grads: None)
    g = {"final_norm_w": g_final}
    g["down"] = _wgrad(act, dh2b, "down_wgrad")
    tok = on_grad("w_down", g)
    dact = _mm(dh2b, w["down"], "nt", "down_dgrad", after=tok)
    dup, g["conv_f"], g["conv_f_b"] = _conv_f_bwd(dact, up, w["conv_f"], p["conv_f_b"])
    g["up"] = _wgrad(dup, n2, "up_wgrad")
    tok = on_grad("w_up", g)
    dn2 = _mm(dup, w["up"], "nn", "up_dgrad", after=tok)
    dh1, dh1b, g["norm2_w"] = _norm_bwd(dn2, h1, p["norm2_w"], dh2, "norm2_bwd")
    g["out"] = _wgrad(mix, dh1b, "out_wgrad")
    tok = on_grad("w_out", g)
    dmix = _mm(dh1b, w["out"], "nt", "out_dgrad", after=tok)
    dgates, dpa, dpb, g["b_gate"] = _merge_bwd(dmix, gates, pa, pb, p["b_gate"])
    g["branch_a"] = _wgrad(ya, dpa, "branch_a_wgrad")
    g["branch_b"] = _wgrad(yb, dpb, "branch_b_wgrad")
    tok = on_grad("w_branch", g)
    dya = _mm(dpa, w["branch_a"], "nt", "branch_a_dgrad", after=tok)
    dyb = _mm(dpb, w["branch_b"], "nt", "branch_b_dgrad", after=tok)
    duv, g["uv_b"], g["v_ln_w"], g["v_ln_b"], g["w_spatial"], dbs_t = _sgu_bwd(
        dyb, uv, p["uv_b"], p["v_ln_w"], p["v_ln_b"], p["w_spatial"], bs_t, e_groups)
    g["b_spatial"] = dbs_t[:, :SGU_GROUPS].T
    dz, dxs, db, dc, ddtr, g["ssd_norm_w"], ddtb, dalog, ddsk = _ssd_bwd(
        dya, y, z, xc, dtr, sprev, dtb, alog, dsk, p["ssd_norm_w"], e_heads)
    g["dt_bias"], g["a_log"], g["d_skip"] = ddtb, dalog, ddsk
    dxbc, g["conv_a"], g["conv_a_b"] = _conv_a_bwd(dxs, db, dc, xbc, w["conv_a"], p["conv_a_b"])
    ddtrb = ddtr.astype(BF16)
    for name, d in (("z", dz), ("xbc", dxbc), ("dt", ddtrb), ("uv", duv), ("gates", dgates)):
        g[name] = _wgrad(d, n1, name + "_wgrad")
    tok = on_grad("w_in", g)
    dn1 = _mm(dz, w["z"], "nn", "z_dgrad", after=tok)
    dn1 = _mm(dxbc, w["xbc"], "nn", "xbc_dgrad", acc=dn1)
    dn1 = _mm(ddtrb, w["dt"], "nn", "dt_dgrad", acc=dn1)
    dn1 = _mm(duv, w["uv"], "nn", "uv_dgrad", acc=dn1)
    dn1 = _mm(dgates, w["gates"], "nn", "gates_dgrad", acc=dn1)
    gx, _, g["norm1_w"] = _norm_bwd(dn1, x, p["norm1_w"], dh1, "norm1_bwd")
    return loss, gx, g


def _place():
    return lax.axis_index("x"), lax.axis_index("y"), lax.axis_index("c")


def _other_chips(x, y):
    return [(1 - x, y), (x, 1 - y), (1 - x, 1 - y)]


def _all_gather(shards, name):
    n = len(shards)

    def body(*refs):
        ins, outs = refs[:n], refs[n:2 * n]
        send_sems, recv_sems, local_sems = refs[2 * n:]
        x, y, c = _place()
        me, sibling = (x, y, c), (x, y, 1 - c)
        chips = _other_chips(x, y)

        def copy(a, k, block, to, src=None):
            slot = outs[a].at[4 * block[0] + 2 * block[1] + block[2]]
            return pltpu.make_async_remote_copy(
                src_ref=slot if src is None else src, dst_ref=slot, send_sem=send_sems.at[7 * a + k],
                recv_sem=recv_sems.at[7 * a + k], device_id=to, device_id_type=MESH)

        started = []
        for a in range(n):
            mine = pltpu.make_async_copy(ins[a], outs[a].at[4 * x + 2 * y + c], local_sems.at[a])
            mine.start()
            started.append(mine)
        sends = []
        for a in range(n):
            sends.append(copy(a, 0, me, sibling, src=ins[a]))
            sends += [copy(a, 1 + j, me, (*chip, c), src=ins[a]) for j, chip in enumerate(chips)]
        for cp in sends:
            cp.start()
        for a in range(n):
            for j, chip in enumerate(chips):
                copy(a, 1 + j, (*chip, c), me).wait_recv()
                fwd = copy(a, 4 + j, (*chip, c), sibling)
                fwd.start()
                sends.append(fwd)
        for a in range(n):
            copy(a, 0, sibling, me).wait_recv()
            for j, chip in enumerate(chips):
                copy(a, 4 + j, (*chip, 1 - c), me).wait_recv()
        for cp in sends:
            cp.wait_send()
        for mine in started:
            mine.wait()

    any_spec = pl.BlockSpec(memory_space=pl.ANY)
    return pl.pallas_call(
        body, name=name, in_specs=[any_spec] * n, out_specs=[any_spec] * n,
        out_shape=[jax.ShapeDtypeStruct((N_DEV, *s.shape), s.dtype) for s in shards],
        scratch_shapes=[pltpu.SemaphoreType.DMA((7 * n,)), pltpu.SemaphoreType.DMA((7 * n,)),
                        pltpu.SemaphoreType.DMA((n,))],
    )(*shards)


HBM_SPEC = pl.BlockSpec(memory_space=pltpu.HBM)
SEM_SPEC = pl.BlockSpec(memory_space=pltpu.SEMAPHORE)
ANY_SPEC = pl.BlockSpec(memory_space=pl.ANY)
DATAFLOW = pltpu.SideEffectType.DATAFLOW_SIDE_EFFECTING
N_PEERS = N_DEV - 1


def _peers(x, y, c):
    out = []
    for r in range(1, N_DEV):
        fx, fy, fc = r >> 2 & 1, r >> 1 & 1, r & 1
        out.append(((1 - x) if fx else x, (1 - y) if fy else y, (1 - c) if fc else c))
    return out


def _gather_copies(srcs, lands, send_sems, recv_sems, sending):
    x, y, c = _place()
    copies = []
    for a, (src, land) in enumerate(zip(srcs, lands)):
        for j, (px, py, pc) in enumerate(_peers(x, y, c)):
            slot = 4 * x + 2 * y + c if sending else 4 * px + 2 * py + pc
            copies.append(pltpu.make_async_remote_copy(
                src_ref=src, dst_ref=land.at[slot], send_sem=send_sems.at[N_PEERS * a + j],
                recv_sem=recv_sems.at[N_PEERS * a + j], device_id=(px, py, pc), device_id_type=MESH))
    return copies


def _gather_start(shards, after, name):
    n = len(shards)

    def body(*refs):
        srcs, lands = refs[:n], refs[n:2 * n]
        send_sems, recv_sems = refs[2 * n + 1:2 * n + 3]
        token = refs[-1]
        for cp in _gather_copies(srcs, lands, send_sems, recv_sems, sending=True):
            cp.start()
        token[...] = jnp.zeros_like(token)

    lands = [lax.empty((N_DEV, *s.shape), s.dtype) for s in shards]
    hbm = lambda a: pltpu.with_memory_space_constraint(a, pltpu.HBM)
    out = pl.pallas_call(
        body, name=name,
        out_shape=(pltpu.SemaphoreType.DMA((N_PEERS * n,)), pltpu.SemaphoreType.DMA((N_PEERS * n,)),
                   *[pltpu.HBM(a.shape, a.dtype) for a in (*shards, *lands)], jax.ShapeDtypeStruct((8, LANES), F32)),
        in_specs=[HBM_SPEC] * (2 * n) + [ANY_SPEC],
        out_specs=(SEM_SPEC, SEM_SPEC, *[HBM_SPEC] * (2 * n), pl.BlockSpec(memory_space=pltpu.VMEM)),
        input_output_aliases={i: 2 + i for i in range(2 * n)},
        compiler_params=pltpu.CompilerParams(has_side_effects=DATAFLOW),
    )(*[hbm(a) for a in (*shards, *lands)], after)
    return out[0], out[1], out[2:2 + n], out[2 + n:2 + 2 * n], out[-1]


def _gather_wait(send_sems, recv_sems, shards, lands, after, name):
    n = len(shards)
    after = tuple(after)

    def body(*refs):
        srcs, lands_ = refs[:n], refs[n:2 * n]
        send, recv = refs[2 * n:2 * n + 2]
        for cp in _gather_copies(srcs, lands_, send, recv, sending=False):
            cp.wait_send()
            cp.wait_recv()

    out = pl.pallas_call(
        body, name=name, out_shape=tuple(pltpu.HBM(a.shape, a.dtype) for a in (*shards, *lands)),
        in_specs=[HBM_SPEC] * (2 * n) + [SEM_SPEC, SEM_SPEC] + [ANY_SPEC] * len(after),
        out_specs=tuple([HBM_SPEC] * (2 * n)), input_output_aliases={i: i for i in range(2 * n)},
        compiler_params=pltpu.CompilerParams(has_side_effects=DATAFLOW),
    )(*shards, *lands, send_sems, recv_sems, *after)
    return out[n:]


def _chip_copies(src, land, send_sems, recv_sems):
    x, y, c = _place()
    return [pltpu.make_async_remote_copy(
        src_ref=src.at[2 * cx + cy], dst_ref=land.at[j], send_sem=send_sems.at[j], recv_sem=recv_sems.at[j],
        device_id=(cx, cy, c), device_id_type=MESH) for j, (cx, cy) in enumerate(_other_chips(x, y))]


def _chips_start(q, name):
    def body(q_ref, land_ref, send_sems, recv_sems, q_thru, land_thru, token):
        for cp in _chip_copies(q_ref, land_ref, send_sems, recv_sems):
            cp.start()
        token[...] = jnp.zeros_like(token)

    land = lax.empty((3, *q.shape[1:]), q.dtype)
    return pl.pallas_call(
        body, name=name,
        out_shape=(pltpu.SemaphoreType.DMA((3,)), pltpu.SemaphoreType.DMA((3,)), pltpu.HBM(q.shape, q.dtype),
                   pltpu.HBM(land.shape, land.dtype), jax.ShapeDtypeStruct((8, LANES), F32)),
        in_specs=[HBM_SPEC, HBM_SPEC],
        out_specs=(SEM_SPEC, SEM_SPEC, HBM_SPEC, HBM_SPEC, pl.BlockSpec(memory_space=pltpu.VMEM)),
        input_output_aliases={0: 2, 1: 3}, compiler_params=pltpu.CompilerParams(has_side_effects=DATAFLOW),
    )(pltpu.with_memory_space_constraint(q, pltpu.HBM), pltpu.with_memory_space_constraint(land, pltpu.HBM))


def _chips_wait(send_sems, recv_sems, q, land, after, name):
    def body(q_ref, land_ref, send, recv, after_ref, q_out, land_out):
        for cp in _chip_copies(q_ref, land_ref, send, recv):
            cp.wait_send()
            cp.wait_recv()

    return pl.pallas_call(
        body, name=name, out_shape=(pltpu.HBM(q.shape, q.dtype), pltpu.HBM(land.shape, land.dtype)),
        in_specs=[HBM_SPEC, HBM_SPEC, SEM_SPEC, SEM_SPEC, ANY_SPEC], out_specs=(HBM_SPEC, HBM_SPEC),
        input_output_aliases={0: 0, 1: 1}, compiler_params=pltpu.CompilerParams(has_side_effects=DATAFLOW),
    )(q, land, send_sems, recv_sems, after)[1]


def _exchange_cores(parts, name):
    n = len(parts)

    def body(*refs):
        ins, outs = refs[:n], refs[n:2 * n]
        send_sems, recv_sems = refs[2 * n:]
        x, y, c = _place()
        copies = []
        for a in range(n):
            for k in range(4):
                copies.append(pltpu.make_async_remote_copy(
                    src_ref=ins[a].at[2 * k + (1 - c)], dst_ref=outs[a].at[k], send_sem=send_sems.at[4 * a + k],
                    recv_sem=recv_sems.at[4 * a + k], device_id=(x, y, 1 - c), device_id_type=MESH))
        for cp in copies:
            cp.start()
        for cp in copies:
            cp.wait()

    any_spec = pl.BlockSpec(memory_space=pl.ANY)
    return pl.pallas_call(
        body, name=name, in_specs=[any_spec] * n, out_specs=[any_spec] * n,
        out_shape=[jax.ShapeDtypeStruct((4, *s.shape[1:]), s.dtype) for s in parts],
        scratch_shapes=[pltpu.SemaphoreType.DMA((4 * n,)), pltpu.SemaphoreType.DMA((4 * n,))],
    )(*parts)


def _chip_sum(part, got, place, name, tr=256):
    _, r, c = part.shape
    tr, tc = _tile2d(r, c, tr)

    def body(place_ref, p_ref, g_ref, q_ref, own_ref):
        s = p_ref[0].astype(F32) + g_ref[0].astype(F32)
        q_ref[0] = s.astype(BF16)

        @pl.when(pl.program_id(2) == place_ref[1])
        def _():
            own_ref[...] = s

    grid_spec = pltpu.PrefetchScalarGridSpec(
        num_scalar_prefetch=1, grid=(r // tr, c // tc, 4),
        in_specs=[pl.BlockSpec((1, tr, tc), lambda i, j, k, pr: (2 * k + pr[0], i, j)),
                  pl.BlockSpec((1, tr, tc), lambda i, j, k, pr: (k, i, j))],
        out_specs=[pl.BlockSpec((1, tr, tc), lambda i, j, k, pr: (k, i, j)),
                   pl.BlockSpec((tr, tc), lambda i, j, k, pr: (i, j))])
    return pl.pallas_call(
        body, name=name, grid_spec=grid_spec,
        out_shape=[jax.ShapeDtypeStruct((4, r, c), BF16), jax.ShapeDtypeStruct((r, c), F32)],
        compiler_params=_params(3),
    )(place, part, got)


def _adamw(w, g, m, v):
    m = ADAM_B1 * m + (1.0 - ADAM_B1) * g
    v = ADAM_B2 * v + (1.0 - ADAM_B2) * jnp.square(g)
    m_hat = m / (1.0 - ADAM_B1 ** ADAM_STEP)
    v_hat = v / (1.0 - ADAM_B2 ** ADAM_STEP)
    return -ADAM_LR * (m_hat / (jnp.sqrt(v_hat) + ADAM_EPS) + ADAM_WD * w), m, v


def _sum_adamw(own, got, w, m, v, name, tr=256):
    r, c = own.shape
    if w.ndim == 3:
        tr, tc = r, LANES
        wblk = pl.BlockSpec((tr, 1, tc), lambda i, j: (i, 0, j))
    else:
        tr, tc = _tile2d(r, c, tr)
        wblk = pl.BlockSpec((tr, tc), lambda i, j: (i, j))

    def body(own_ref, got_ref, w_ref, m_ref, v_ref, g_ref, d_ref, nm_ref, nv_ref):
        g = own_ref[...]
        for j in range(3):
            g = g + got_ref[j].astype(F32)
        two_d = lambda ref: ref[...].reshape(tr, tc)
        delta, nm, nv = _adamw(two_d(w_ref), g, two_d(m_ref), two_d(v_ref))
        for ref, val in ((g_ref, g), (d_ref, delta), (nm_ref, nm), (nv_ref, nv)):
            ref[...] = val.reshape(ref.shape)

    blk = pl.BlockSpec((tr, tc), lambda i, j: (i, j))
    return pl.pallas_call(
        body, name=name, grid=(r // tr, c // tc),
        in_specs=[blk, pl.BlockSpec((3, tr, tc), lambda i, j: (0, i, j)), wblk, wblk, wblk], out_specs=[wblk] * 4,
        out_shape=[jax.ShapeDtypeStruct(w.shape, F32)] * 4, compiler_params=_params(2),
    )(own, got, w, m, v)


VECTORS = ["norm1_w", "b_gate", "conv_a_b", "dt_bias", "a_log", "d_skip", "ssd_norm_w", "uv_b", "v_ln_w", "v_ln_b",
           "norm2_w", "conv_f_b", "final_norm_w"]
SMALL_ORDER = VECTORS + ["w_spatial", "b_spatial", "conv_a_w", "conv_f_w"]


def _small_adamw(gathered, w, m, v):
    sizes = {n: w[n].shape[1] for n in VECTORS}
    offs, off = {}, 0
    for n in VECTORS:
        offs[n] = off
        off += -(-sizes[n] // LANES) * LANES
    k = len(SMALL_ORDER)

    def body(*refs):
        row_ref, ws_ref, bs_ref, ca_ref, cf_ref = refs[:5]
        w_refs, m_refs, v_refs = (dict(zip(SMALL_ORDER, refs[5 + i * k:5 + (i + 1) * k])) for i in range(3))
        outs = refs[5 + 3 * k:]
        x, y, c = _place()
        dev = 4 * x + 2 * y + c

        def total(ref):
            s = ref[0]
            for d in range(1, N_DEV):
                s = s + ref[d]
            return s

        row = total(row_ref)
        grads = {n: row[:, offs[n]:offs[n] + sizes[n]] for n in VECTORS}
        grads["w_spatial"], grads["b_spatial"] = total(ws_ref), total(bs_ref)
        for n, ref in (("conv_a_w", ca_ref), ("conv_f_w", cf_ref)):
            whole, cols = total(ref), w_refs[n].shape[1]
            mine = whole[:, :cols]
            for d in range(1, N_DEV):
                mine = jnp.where(dev == d, whole[:, d * cols:(d + 1) * cols], mine)
            grads[n] = mine
        for i, n in enumerate(SMALL_ORDER):
            outs[4 * i][...] = grads[n]
            outs[4 * i + 1][...], outs[4 * i + 2][...], outs[4 * i + 3][...] = _adamw(
                w_refs[n][...], grads[n], m_refs[n][...], v_refs[n][...])

    out = pl.pallas_call(
        body, name="adamw_small",
        out_shape=[jax.ShapeDtypeStruct(w[n].shape, F32) for n in SMALL_ORDER for _ in range(4)],
        compiler_params=_params(0),
    )(*gathered, *[t[n] for t in (w, m, v) for n in SMALL_ORDER])
    return [dict(zip(SMALL_ORDER, out[j::4])) for j in range(4)]


SMALL = ["norm1_w", "b_gate", "conv_a_b", "dt_bias", "a_log", "d_skip", "ssd_norm_w", "uv_b", "v_ln_w", "v_ln_b",
         "w_spatial", "b_spatial", "norm2_w", "conv_f_b", "final_norm_w"]
BIG = ["w_in", "w_branch", "w_out", "w_up", "w_down"]
TRANSPOSED = ("w_in", "w_up")
WEIGHTS = ["norm1_w", "w_in", "b_gate", "conv_a_w", "conv_a_b", "dt_bias", "a_log", "d_skip", "ssd_norm_w", "uv_b",
           "v_ln_w", "v_ln_b", "w_spatial", "b_spatial", "w_branch", "w_out", "norm2_w", "w_up", "conv_f_w",
           "conv_f_b", "w_down", "final_norm_w"]
IN_SPLITS = [("z", 0, 2048), ("xbc", 2048, 5120), ("dt", 5120, 5152), ("uv", 5152, 7200), ("gates", 7200, 9248)]


def _columns_from_devices(a):
    return a.transpose(1, 0, 2).reshape(a.shape[1], -1)


def kernel(x, norm1_w, w_in, b_gate, conv_a_w, conv_a_b, dt_bias, a_log, d_skip, ssd_norm_w, uv_b, v_ln_w, v_ln_b, w_spatial, b_spatial, w_branch, w_out, norm2_w, w_up, conv_f_w, conv_f_b, w_down, final_norm_w, loss_target, m_norm1_w, m_w_in, m_b_gate, m_conv_a_w, m_conv_a_b, m_dt_bias, m_a_log, m_d_skip, m_ssd_norm_w, m_uv_b, m_v_ln_w, m_v_ln_b, m_w_spatial, m_b_spatial, m_w_branch, m_w_out, m_norm2_w, m_w_up, m_conv_f_w, m_conv_f_b, m_w_down, m_final_norm_w, v_norm1_w, v_w_in, v_b_gate, v_conv_a_w, v_conv_a_b, v_dt_bias, v_a_log, v_d_skip, v_ssd_norm_w, v_uv_b, v_v_ln_w, v_v_ln_b, v_w_spatial, v_b_spatial, v_w_branch, v_w_out, v_norm2_w, v_w_up, v_conv_f_w, v_conv_f_b, v_w_down, v_final_norm_w):
    args = dict(locals())
    wts = {n: args[n] for n in WEIGHTS}
    mom = {n: args["m_" + n] for n in WEIGHTS}
    var = {n: args["v_" + n] for n in WEIGHTS}
    cx, cy, cc = _place()
    dev = 4 * cx + 2 * cy + cc
    place = jnp.stack([cc, 2 * cx + cy]).astype(jnp.int32)

    def shard2d(n, a):
        return a[0].T if n in TRANSPOSED else a[0]

    def unshard(n, b):
        return (b.T if n in TRANSPOSED else b)[None]

    g_in, g_conv_a, g_conv_f = _all_gather(
        [shard2d("w_in", w_in).astype(BF16), conv_a_w[0], conv_f_w[0]], "gather_w_in")
    late = [shard2d(n, wts[n]).astype(BF16) for n in BIG[1:]]
    send_sems, recv_sems, late, lands, token = _gather_start(late, g_in, "gather_late_start")
    w_in_rows = g_in.reshape(-1, D_MODEL)
    w = {name: w_in_rows[lo:hi] for name, lo, hi in IN_SPLITS}
    w["dt"] = jnp.pad(w["dt"], ((0, DT_PAD - SSD_HEADS), (0, 0)))
    w["conv_a"] = _columns_from_devices(g_conv_a)
    w["conv_f"] = _columns_from_devices(g_conv_f)

    def late_weights(*after):
        got = _gather_wait(send_sems, recv_sems, late, lands, after, "gather_late_wait")
        g_branch, g_out, g_up, g_down = [lax.dynamic_update_index_in_dim(land, mine, dev, 0).reshape(-1, D_MODEL)
                                         for land, mine in zip(got, late)]
        return {"branch_a": g_branch[:SSD_INNER], "branch_b": g_branch[SSD_INNER:], "out": g_out, "up": g_up,
                "down": g_down}

    in_flight = {}

    def on_grad(n, g):
        part = {"w_in": lambda: jnp.concatenate([g[name][:hi - lo] for name, lo, hi in IN_SPLITS], axis=0),
                "w_branch": lambda: jnp.concatenate([g["branch_a"], g["branch_b"]], axis=0),
                "w_out": lambda: g["out"], "w_up": lambda: g["up"], "w_down": lambda: g["down"]}[n]()
        part = part.reshape(N_DEV, -1, D_MODEL)
        from_core, = _exchange_cores([part], f"to_other_core_{n}")
        q, own = _chip_sum(part, from_core, place, f"chip_sum_{n}")
        send, recv, q, land, tok = _chips_start(q, f"to_other_chips_start_{n}")
        in_flight[n] = (own, send, recv, q, land)
        return tok

    p = {n: wts[n][0] if wts[n].ndim > 2 else wts[n].reshape(1, -1) for n in SMALL}
    loss, gx, g = _local_step(x[0], loss_target[0], w, p, after=token, late_weights=late_weights, on_grad=on_grad)
    loss = lax.psum(loss[0, 0], ("x", "y", "c"))

    small_g = [jnp.concatenate([g[n] for n in VECTORS], axis=1), g["w_spatial"], g["b_spatial"], g["conv_a"],
               g["conv_f"]]
    s_send, s_recv, s_mine, s_land, _ = _gather_start(small_g, gx, "gather_small_start")

    grads, delta, new_m, new_v = {}, {}, {}, {}

    def big_adamw(n, after):
        own, send, recv, q, land = in_flight[n]
        got = _chips_wait(send, recv, q, land, after, f"to_other_chips_wait_{n}")
        view = (lambda a: a.transpose(2, 0, 1)) if n == "w_in" else (lambda a: shard2d(n, a))
        back = (lambda b: b.transpose(1, 2, 0)) if n == "w_in" else (lambda b: unshard(n, b))
        out = _sum_adamw(own, got, view(wts[n]), view(mom[n]), view(var[n]), f"adamw_{n}")
        grads[n], delta[n], new_m[n], new_v[n] = [back(o) for o in out]
        return out[1]

    after = gx
    for n in ("w_down", "w_up", "w_out", "w_branch", "w_in"):
        after = big_adamw(n, after)
    gathered = _gather_wait(s_send, s_recv, s_mine, s_land, [after], "gather_small_wait")
    gathered = [lax.dynamic_update_index_in_dim(land, mine, dev, 0) for land, mine in zip(gathered, s_mine)]
    small = [{n: t[n][0] if t[n].ndim > 2 else t[n].reshape(1, -1) for n in SMALL_ORDER} for t in (wts, mom, var)]
    for tgt, out in zip((grads, delta, new_m, new_v), _small_adamw(gathered, *small)):
        tgt.update({n: out[n].reshape(wts[n].shape) for n in SMALL_ORDER})

    return (loss, gx[None], *[grads[n] for n in WEIGHTS], *[delta[n] for n in WEIGHTS],
            *[new_m[n] for n in WEIGHTS], *[new_v[n] for n in WEIGHTS])
```

```python
import functools

import jax
import jax.numpy as jnp
from jax import lax
from jax.experimental import pallas as pl
from jax.experimental.pallas import tpu as pltpu

F32, BF16 = jnp.float32, jnp.bfloat16
HIGHEST = lax.Precision.HIGHEST

D_MODEL = 1024
SSD_INNER = 2048
SSD_HEAD_DIM = 64
SSD_HEADS = 32
SSD_GROUPS = 4
SSD_STATE = 128
SSD_BC = SSD_GROUPS * SSD_STATE
SSD_XBC = SSD_INNER + 2 * SSD_BC
SSD_CONV = 4
CHUNK = 128
N_PAIRS = SSD_HEADS // 2
PAIRS_PER_GROUP = N_PAIRS // SSD_GROUPS
SGU_WIDTH = 1024
SGU_GROUPS = 8
D_FF = 2816
FFN_CONV = 3
NORM_EPS = 1e-6
LN_EPS = 1e-5
LANES = 128
DT_PAD = LANES

ADAM_LR, ADAM_B1, ADAM_B2, ADAM_EPS, ADAM_WD, ADAM_STEP = 0.001, 0.9, 0.999, 1e-08, 0.01, 10

N_DEV = 8
VMEM_LIMIT = 56 * 1024 * 1024
MESH = pl.DeviceIdType.MESH


def _params(n_grid, **kw):
    sem = dict(dimension_semantics=("arbitrary",) * n_grid) if n_grid else {}
    return pltpu.CompilerParams(vmem_limit_bytes=VMEM_LIMIT, **sem, **kw)


def _tile(n, pref):
    t = (min(pref, n) // LANES) * LANES
    while n % t:
        t -= LANES
    return t


def _row_tile(r, pref):
    for t in range(min(pref, r) // 16 * 16, 0, -16):
        if r % t == 0:
            return t
    return r


def _tile2d(r, c, rows):
    if r % 16 == 0:
        return _row_tile(r, rows), c
    return r, _tile(c, 2 * LANES)


def _rows(tm, n, nt=None, rev=False, col=0):
    if rev:
        return pl.BlockSpec((tm, n), lambda i: (nt - 1 - i, col))
    return pl.BlockSpec((tm, n), lambda i: (i, col))


def _halo(tm, n, nt=None, rev=False):
    per = tm // 8
    if rev:
        return pl.BlockSpec((8, n), lambda i: (jnp.maximum((nt - 1 - i) * per - 1, 0), 0))
    return pl.BlockSpec((8, n), lambda i: (jnp.maximum(i * per - 1, 0), 0))


def _full(shape):
    nd = len(shape)
    return pl.BlockSpec(shape, lambda *_: (0,) * nd)


def _rms(x, w, eps=NORM_EPS):
    return x * lax.rsqrt(jnp.mean(x * x, axis=-1, keepdims=True) + eps) * w


def _layer_norm(x, w, b):
    mu = jnp.mean(x, axis=-1, keepdims=True)
    var = jnp.mean(jnp.square(x - mu), axis=-1, keepdims=True)
    return (x - mu) * lax.rsqrt(var + LN_EPS) * w + b


def _sigmoid(x):
    return 1.0 / (1.0 + jnp.exp(-x))


def _silu(x):
    return x * _sigmoid(x)


def _dsilu(x):
    s = _sigmoid(x)
    return s * (1.0 + x * (1.0 - s))


def _softplus(x):
    return jnp.maximum(x, 0.0) + jnp.log(1.0 + jnp.exp(-jnp.abs(x)))


def _gelu(x):
    return jax.nn.gelu(x)


def _dot(a, b):
    return jnp.dot(a, b, preferred_element_type=F32)


def _dot_nt(a, b):
    return lax.dot_general(a, b, (((1,), (1,)), ((), ())), preferred_element_type=F32)


def _dot_tn(a, b):
    return lax.dot_general(a, b, (((0,), (0,)), ((), ())), preferred_element_type=F32)


def _dot_split(p, e):
    hi = p.astype(BF16)
    lo = (p - hi.astype(F32)).astype(BF16)
    return _dot(hi, e) + _dot(lo, e)


def _colsum(x):
    return jnp.sum(x, axis=0, keepdims=True)


def _shift_down(x, halo, j):
    xs = pltpu.roll(x, j, 0)
    hs = pltpu.roll(halo, j, 0)
    r8 = lax.broadcasted_iota(jnp.int32, hs.shape, 0)
    return jnp.concatenate([jnp.where(r8 < j, hs, xs[:8]), xs[8:]], axis=0)


def _shift_up(x, nxt, j):
    n = x.shape[0]
    xs = pltpu.roll(x, n - j, 0)
    ns = pltpu.roll(nxt, 8 - j, 0)
    r8 = lax.broadcasted_iota(jnp.int32, ns.shape, 0)
    return jnp.concatenate([xs[:n - 8], jnp.where(r8 >= 8 - j, ns, xs[n - 8:])], axis=0)


def _causal_conv(x, halo, w, b):
    k = w.shape[0]
    y = b + w[k - 1:k, :] * x
    for j in range(1, k):
        y = y + w[k - 1 - j:k - j, :] * _shift_down(x, halo, j)
    return y


def _causal_conv_bwd(dy, nxt, x, w):
    k = w.shape[0]
    dx = w[k - 1:k, :] * dy
    dw = [_colsum(dy * x)]
    for j in range(1, k):
        dyj = _shift_up(dy, nxt, j)
        dx = dx + w[k - 1 - j:k - j, :] * dyj
        dw.append(_colsum(dyj * x))
    return dx, jnp.concatenate(dw[::-1], axis=0)


MM_TILE_PREF = 1408
MM_VMEM_BUDGET = 40 * 1024 * 1024


def _mm_tiles(m, n, k, out_bytes):
    tm, tn = _tile(m, MM_TILE_PREF), _tile(n, MM_TILE_PREF)
    need = lambda tm, tn: 2 * (2 * k * (tm + tn) + out_bytes * tm * tn)
    while need(tm, tn) > MM_VMEM_BUDGET:
        if tn >= tm and tn > LANES:
            tn = _tile(n, tn - LANES)
        else:
            tm = _tile(m, tm - LANES)
    return tm, tn


def _mm(a, b, dims, name, acc=None, out_dtype=F32, after=None):
    if dims == "tn":
        k, m = a.shape
    else:
        m, k = a.shape
    n = b.shape[0] if dims == "nt" else b.shape[1]
    tm, tn = _mm_tiles(m, n, k, 4 * (2 if acc is not None else 1))
    a_spec = pl.BlockSpec((k, tm), lambda j, i: (0, i)) if dims == "tn" else pl.BlockSpec((tm, k), lambda j, i: (i, 0))
    b_spec = pl.BlockSpec((tn, k), lambda j, i: (j, 0)) if dims == "nt" else pl.BlockSpec((k, tn), lambda j, i: (0, j))
    o_spec = pl.BlockSpec((tm, tn), lambda j, i: (i, j))
    dot = {"nn": _dot, "nt": _dot_nt, "tn": _dot_tn}[dims]

    def body(a_ref, b_ref, *rest):
        r = dot(a_ref[...], b_ref[...])
        if acc is not None:
            r = r + rest[0][...]
        rest[-1][...] = r.astype(out_dtype)

    ins, specs = [a, b], [a_spec, b_spec]
    if acc is not None:
        ins.append(acc)
        specs.append(o_spec)
    if after is not None:
        ins.append(after)
        specs.append(pl.BlockSpec(memory_space=pl.ANY))
    return pl.pallas_call(
        body, name=name, grid=(n // tn, m // tm), in_specs=specs, out_specs=o_spec,
        out_shape=jax.ShapeDtypeStruct((m, n), out_dtype), compiler_params=_params(2),
    )(*ins)


def _wgrad(a, d, name):
    return _mm(a, d, "tn", name, out_dtype=BF16)


def _norm_fwd(x, w, name, after=None, tm=512):
    t, d = x.shape

    def body(x_ref, w_ref, *rest):
        rest[-1][...] = _rms(x_ref[...], w_ref[...]).astype(BF16)

    extra, extra_specs = ([after], [_full(after.shape)]) if after is not None else ([], [])
    return pl.pallas_call(
        body, name=name, grid=(t // tm,), in_specs=[_rows(tm, d), _full((1, d))] + extra_specs,
        out_specs=_rows(tm, d), out_shape=jax.ShapeDtypeStruct((t, d), BF16), compiler_params=_params(1),
    )(x, w, *extra)


def _conv_a_fwd(xbc, cw, cb, tm=256):
    t, c = xbc.shape

    def body(x_ref, h_ref, w_ref, b_ref, o_ref, y_ref):
        halo = jnp.where(pl.program_id(0) > 0, h_ref[...], 0.0)
        y = _causal_conv(x_ref[...], halo, w_ref[...], b_ref[...])
        y_ref[...] = y
        o_ref[...] = _silu(y)

    return pl.pallas_call(
        body, name="conv_a_fwd", grid=(t // tm,),
        in_specs=[_rows(tm, c), _halo(tm, c), _full(cw.shape), _full((1, c))], out_specs=[_rows(tm, c)] * 2,
        out_shape=[jax.ShapeDtypeStruct((t, c), F32)] * 2, compiler_params=_params(1),
    )(xbc, xbc, cw, cb)


def _ssd_common(dtr, dtb, alog):
    row = lax.broadcasted_iota(jnp.int32, (CHUNK, CHUNK), 0)
    col = lax.broadcasted_iota(jnp.int32, (CHUNK, CHUNK), 1)
    causal = row >= col
    dt = _softplus(dtr + dtb)
    a = -jnp.exp(alog)
    acum = jnp.dot(causal.astype(F32), dt * a, precision=HIGHEST, preferred_element_type=F32)
    return dt, a, acum, acum.T, causal, col < SSD_HEAD_DIM, row


def _pair_terms(j, dt, acum, acum_t, causal, lane_lo):
    h0, h1 = 2 * j, 2 * j + 1
    ac0, ac1 = acum[:, h0:h0 + 1], acum[:, h1:h1 + 1]
    l0 = jnp.exp(jnp.where(causal, ac0 - acum_t[h0:h0 + 1, :], -jnp.inf))
    l1 = jnp.exp(jnp.where(causal, ac1 - acum_t[h1:h1 + 1, :], -jnp.inf))
    dtp = jnp.where(lane_lo, dt[:, h0:h0 + 1], dt[:, h1:h1 + 1])
    al0, al1 = acum[CHUNK - 1:CHUNK, h0:h0 + 1], acum[CHUNK - 1:CHUNK, h1:h1 + 1]
    ecol = jnp.where(lane_lo, jnp.exp(ac0), jnp.exp(ac1))
    dsr = jnp.where(lane_lo, jnp.exp(al0 - ac0), jnp.exp(al1 - ac1))
    elast = jnp.where(lane_lo[0:1], jnp.exp(al0), jnp.exp(al1))
    return l0, l1, dtp, ecol, dsr, elast


def _ssd_fwd(xc, dtr, z, dtb, alog, dsk, nw):
    t = xc.shape[0]
    nc = t // CHUNK

    def body(xs_ref, b_ref, c_ref, dtr_ref, z_ref, dtb_ref, alog_ref, dsk_ref, nw_ref, y_ref, ya_ref, sp_ref, s_scr):
        @pl.when(pl.program_id(0) == 0)
        def _():
            s_scr[...] = jnp.zeros_like(s_scr)

        dt, a, acum, acum_t, causal, lane_lo, _ = _ssd_common(dtr_ref[...], dtb_ref[...], alog_ref[...])
        dsk = dsk_ref[...]
        for g in range(SSD_GROUPS):
            gs = slice(g * SSD_STATE, (g + 1) * SSD_STATE)
            bg, cg = b_ref[:, gs].astype(BF16), c_ref[:, gs].astype(BF16)
            cb = _dot_nt(cg, bg)
            for pp in range(PAIRS_PER_GROUP):
                j = g * PAIRS_PER_GROUP + pp
                ps = slice(j * LANES, (j + 1) * LANES)
                x = xs_ref[:, ps]
                l0, l1, dtp, ecol, dsr, elast = _pair_terms(j, dt, acum, acum_t, causal, lane_lo)
                xdt = x * dtp
                xb = xdt.astype(BF16)
                zero = jnp.zeros_like(xb)
                yd = (_dot((cb * l0).astype(BF16), jnp.where(lane_lo, xb, zero))
                      + _dot((cb * l1).astype(BF16), jnp.where(lane_lo, zero, xb)))
                sp = s_scr[j]
                yo = ecol * _dot(cg, sp.astype(BF16))
                st = _dot_tn(bg, (xdt * dsr).astype(BF16))
                sp_ref[0, j] = sp
                s_scr[j] = elast * sp + st
                dskp = jnp.where(lane_lo[0:1], dsk[:, 2 * j:2 * j + 1], dsk[:, 2 * j + 1:2 * j + 2])
                y_ref[:, ps] = yd + yo + dskp * x
        ya_ref[...] = _rms(y_ref[...] * _silu(z_ref[...]), nw_ref[...]).astype(BF16)

    ck = lambda n, col=0: pl.BlockSpec((CHUNK, n), lambda c: (c, col))
    return pl.pallas_call(
        body, name="ssd_fwd", grid=(nc,),
        in_specs=[ck(SSD_INNER), ck(SSD_BC, SSD_INNER // SSD_BC), ck(SSD_BC, SSD_INNER // SSD_BC + 1), ck(DT_PAD),
                  ck(SSD_INNER), _full((1, DT_PAD)), _full((1, DT_PAD)), _full((1, DT_PAD)), _full((1, SSD_INNER))],
        out_specs=[ck(SSD_INNER), ck(SSD_INNER),
                   pl.BlockSpec((1, N_PAIRS, SSD_STATE, LANES), lambda c: (c, 0, 0, 0))],
        out_shape=[jax.ShapeDtypeStruct((t, SSD_INNER), F32), jax.ShapeDtypeStruct((t, SSD_INNER), BF16),
                   jax.ShapeDtypeStruct((nc, N_PAIRS, SSD_STATE, LANES), F32)],
        scratch_shapes=[pltpu.VMEM((N_PAIRS, SSD_STATE, LANES), F32)], compiler_params=_params(1),
    )(xc, xc, xc, dtr, z, dtb, alog, dsk, nw)


def _ssd_bwd(dya, y, z, xc, dtr, sprev, dtb, alog, dsk, nw, e_heads):
    t = xc.shape[0]
    nc = t // CHUNK

    def body(dya_ref, y_ref, z_ref, xs_ref, b_ref, c_ref, dtr_ref, sp_ref, dtb_ref, alog_ref, dsk_ref, nw_ref, e_ref,
             dz_ref, dxs_ref, db_ref, dc_ref, ddtr_ref, dnw_ref, ddtb_ref, dalog_ref, ddsk_ref, ds_scr):
        @pl.when(pl.program_id(0) == 0)
        def _():
            ds_scr[...] = jnp.zeros_like(ds_scr)
            for r in (dnw_ref, ddtb_ref, dalog_ref, ddsk_ref):
                r[...] = jnp.zeros_like(r)

        y = y_ref[...]
        _, gate_vjp = jax.vjp(lambda y_, z_, w_: _rms(y_ * _silu(z_), w_), y, z_ref[...], nw_ref[...])
        dy, dz, dnw = gate_vjp(dya_ref[...])
        dz_ref[...] = dz.astype(BF16)
        dnw_ref[...] += dnw

        dtr = dtr_ref[...]
        dt, a, acum, acum_t, causal, lane_lo, row = _ssd_common(dtr, dtb_ref[...], alog_ref[...])
        dsk = dsk_ref[...]
        p_a, p_dt, v_last = [], [], []
        col = lax.broadcasted_iota(jnp.int32, (CHUNK, CHUNK), 1)
        da_cols = jnp.zeros((CHUNK, CHUNK), F32)
        da_rows = jnp.zeros((CHUNK, CHUNK), F32)
        for g in range(SSD_GROUPS):
            gs = slice(g * SSD_STATE, (g + 1) * SSD_STATE)
            bg, cg = b_ref[:, gs].astype(BF16), c_ref[:, gs].astype(BF16)
            cb = _dot_nt(cg, bg)
            dcb = jnp.zeros((CHUNK, CHUNK), F32)
            dbg = jnp.zeros((CHUNK, SSD_STATE), F32)
            dcg = jnp.zeros((CHUNK, SSD_STATE), F32)
            for pp in range(PAIRS_PER_GROUP):
                j = g * PAIRS_PER_GROUP + pp
                ps = slice(j * LANES, (j + 1) * LANES)
                x = xs_ref[:, ps]
                l0, l1, dtp, ecol, dsr, elast = _pair_terms(j, dt, acum, acum_t, causal, lane_lo)
                xdt = x * dtp
                xb = xdt.astype(BF16)
                dskp = jnp.where(lane_lo[0:1], dsk[:, 2 * j:2 * j + 1], dsk[:, 2 * j + 1:2 * j + 2])
                dyp = dy[:, ps]
                dyb = dyp.astype(BF16)
                sp, dsn = sp_ref[0, j], ds_scr[j]
                spb, dsnb = sp.astype(BF16), dsn.astype(BF16)
                y_off = ecol * _dot(cg, spb)
                dw = (dyp * ecol).astype(BF16)
                dcg = dcg + _dot_nt(dw, spb)
                dsp = _dot_tn(cg, dw) + elast * dsn
                xd = xdt * dsr
                zd = _dot(bg, dsnb) * dsr
                dbg = dbg + _dot_nt(xd.astype(BF16), dsnb)
                dxdt = zd
                zero = jnp.zeros_like(xb)
                for h, lm, le in ((2 * j, lane_lo, l0), (2 * j + 1, jnp.logical_not(lane_lo), l1)):
                    dm = _dot_nt(jnp.where(lm, dyb, zero), jnp.where(lm, xb, zero))
                    dcb = dcb + dm * le
                    m = cb * le
                    dxdt = dxdt + jnp.where(lm, _dot_tn(m.astype(BF16), dyb), 0.0)
                    q = dm * m
                    da_cols = da_cols + jnp.where(col == h, jnp.sum(q, axis=1, keepdims=True), 0.0)
                    da_rows = da_rows + jnp.where(row == h, _colsum(q), 0.0)
                ds_scr[j] = dsp
                dxs_ref[:, ps] = dxdt * dtp + dskp * dyp
                p_a.append(dyp * y_off - xdt * zd)
                p_dt.append(dxdt * x)
                v_last.append(_colsum(zd * xdt) + elast * _colsum(dsn * sp))
            dcbb = dcb.astype(BF16)
            db_ref[:, gs] = dbg + _dot_tn(dcbb, cg)
            dc_ref[:, gs] = dcg + _dot(dcbb, bg)
        e = e_ref[...]
        rows8 = jnp.concatenate([jnp.concatenate(v_last, axis=1), _colsum(dy * xs_ref[...]),
                                 jnp.zeros((6, SSD_INNER), F32)], axis=0)
        r8 = _dot_split(rows8, e)
        da = (_dot_split(jnp.concatenate(p_a, axis=1), e) + jnp.where(row == CHUNK - 1, r8[0:1], 0.0)
              + da_cols - da_rows.T)
        ddsk_ref[...] += r8[1:2]
        dadt = jnp.dot((row <= col).astype(F32), da, precision=HIGHEST, preferred_element_type=F32)
        ddt = dadt * a + _dot_split(jnp.concatenate(p_dt, axis=1), e)
        dalog_ref[...] += _colsum(dadt * dt) * a
        ddtr = ddt * _sigmoid(dtr + dtb_ref[...])
        ddtr_ref[...] = ddtr
        ddtb_ref[...] += _colsum(ddtr)

    ck = lambda n, col=0: pl.BlockSpec((CHUNK, n), lambda c: (nc - 1 - c, col))
    acc = lambda n: _full((1, n))
    return pl.pallas_call(
        body, name="ssd_bwd", grid=(nc,),
        in_specs=[ck(SSD_INNER), ck(SSD_INNER), ck(SSD_INNER), ck(SSD_INNER), ck(SSD_BC, SSD_INNER // SSD_BC),
                  ck(SSD_BC, SSD_INNER // SSD_BC + 1), ck(DT_PAD),
                  pl.BlockSpec((1, N_PAIRS, SSD_STATE, LANES), lambda c: (nc - 1 - c, 0, 0, 0)),
                  acc(DT_PAD), acc(DT_PAD), acc(DT_PAD), acc(SSD_INNER), _full((SSD_INNER, LANES))],
        out_specs=[ck(SSD_INNER), ck(SSD_INNER), ck(SSD_BC), ck(SSD_BC), ck(DT_PAD),
                   acc(SSD_INNER), acc(DT_PAD), acc(DT_PAD), acc(DT_PAD)],
        out_shape=[jax.ShapeDtypeStruct((t, SSD_INNER), BF16), jax.ShapeDtypeStruct((t, SSD_INNER), F32),
                   jax.ShapeDtypeStruct((t, SSD_BC), F32), jax.ShapeDtypeStruct((t, SSD_BC), F32),
                   jax.ShapeDtypeStruct((t, DT_PAD), F32), jax.ShapeDtypeStruct((1, SSD_INNER), F32),
                   jax.ShapeDtypeStruct((1, DT_PAD), F32), jax.ShapeDtypeStruct((1, DT_PAD), F32),
                   jax.ShapeDtypeStruct((1, DT_PAD), F32)],
        scratch_shapes=[pltpu.VMEM((N_PAIRS, SSD_STATE, LANES), F32)], compiler_params=_params(1),
    )(dya, y, z, xc, xc, xc, dtr, sprev, dtb, alog, dsk, nw, e_heads)


def _sgu_act(uv, uvb, lnw, lnb):
    a = _gelu(uv + uvb)
    return a[:, :SGU_WIDTH], _layer_norm(a[:, SGU_WIDTH:], lnw, lnb)


def _sgu_weights(ws_ref):
    row = lax.broadcasted_iota(jnp.int32, (CHUNK, CHUNK), 0)
    col = lax.broadcasted_iota(jnp.int32, (CHUNK, CHUNK), 1)
    return [jnp.where(row >= col, ws_ref[g], 0.0).astype(BF16) for g in range(SGU_GROUPS)], row >= col


def _sgu_fwd(uv, uvb, lnw, lnb, ws, bs_t):
    t = uv.shape[0]

    def body(uv_ref, uvb_ref, lnw_ref, lnb_ref, ws_ref, bs_ref, o_ref):
        u, vn = _sgu_act(uv_ref[...], uvb_ref[...], lnw_ref[...], lnb_ref[...])
        wc, _ = _sgu_weights(ws_ref)
        bs = bs_ref[...]
        for g in range(SGU_GROUPS):
            gs = slice(g * LANES, (g + 1) * LANES)
            mixed = _dot(wc[g], vn[:, gs].astype(BF16)) + bs[:, g:g + 1]
            o_ref[:, gs] = (u[:, gs] * mixed).astype(BF16)

    return pl.pallas_call(
        body, name="sgu_fwd", grid=(t // CHUNK,),
        in_specs=[_rows(CHUNK, 2 * SGU_WIDTH), _full((1, 2 * SGU_WIDTH)), _full((1, SGU_WIDTH)), _full((1, SGU_WIDTH)),
                  _full(ws.shape), _full(bs_t.shape)],
        out_specs=_rows(CHUNK, SGU_WIDTH), out_shape=jax.ShapeDtypeStruct((t, SGU_WIDTH), BF16),
        compiler_params=_params(1),
    )(uv, uvb, lnw, lnb, ws, bs_t)


def _sgu_bwd(dyb, uv, uvb, lnw, lnb, ws, bs_t, e_groups):
    t = uv.shape[0]

    def body(dyb_ref, uv_ref, uvb_ref, lnw_ref, lnb_ref, ws_ref, bs_ref, e_ref,
             duv_ref, duvb_ref, dlnw_ref, dlnb_ref, dws_ref, dbs_ref):
        @pl.when(pl.program_id(0) == 0)
        def _():
            for r in (duvb_ref, dlnw_ref, dlnb_ref, dws_ref, dbs_ref):
                r[...] = jnp.zeros_like(r)

        (u, vn), act_vjp = jax.vjp(_sgu_act, uv_ref[...], uvb_ref[...], lnw_ref[...], lnb_ref[...])
        wc, causal = _sgu_weights(ws_ref)
        bs = bs_ref[...]
        dyb = dyb_ref[...]
        du, dvn, dmix = [], [], []
        for g in range(SGU_GROUPS):
            gs = slice(g * LANES, (g + 1) * LANES)
            vb = vn[:, gs].astype(BF16)
            mixed = _dot(wc[g], vb) + bs[:, g:g + 1]
            dm = dyb[:, gs] * u[:, gs]
            dmb = dm.astype(BF16)
            du.append(dyb[:, gs] * mixed)
            dvn.append(_dot_tn(wc[g], dmb))
            dws_ref[g] += jnp.where(causal, _dot_nt(dmb, vb), 0.0)
            dmix.append(dm)
        dbs_ref[...] += _dot_split(jnp.concatenate(dmix, axis=1), e_ref[...])
        duv, duvb, dlnw, dlnb = act_vjp((jnp.concatenate(du, axis=1), jnp.concatenate(dvn, axis=1)))
        duv_ref[...] = duv.astype(BF16)
        duvb_ref[...] += duvb
        dlnw_ref[...] += dlnw
        dlnb_ref[...] += dlnb

    return pl.pallas_call(
        body, name="sgu_bwd", grid=(t // CHUNK,),
        in_specs=[_rows(CHUNK, SGU_WIDTH), _rows(CHUNK, 2 * SGU_WIDTH), _full((1, 2 * SGU_WIDTH)),
                  _full((1, SGU_WIDTH)), _full((1, SGU_WIDTH)), _full(ws.shape), _full(bs_t.shape),
                  _full(e_groups.shape)],
        out_specs=[_rows(CHUNK, 2 * SGU_WIDTH), _full((1, 2 * SGU_WIDTH)), _full((1, SGU_WIDTH)),
                   _full((1, SGU_WIDTH)), _full(ws.shape), _full(bs_t.shape)],
        out_shape=[jax.ShapeDtypeStruct((t, 2 * SGU_WIDTH), BF16), jax.ShapeDtypeStruct((1, 2 * SGU_WIDTH), F32),
                   jax.ShapeDtypeStruct((1, SGU_WIDTH), F32), jax.ShapeDtypeStruct((1, SGU_WIDTH), F32),
                   jax.ShapeDtypeStruct(ws.shape, F32), jax.ShapeDtypeStruct(bs_t.shape, F32)],
        compiler_params=_params(1),
    )(dyb, uv, uvb, lnw, lnb, ws, bs_t, e_groups)


def _merge(gates, pa, pb, bg):
    s = _sigmoid(gates + bg)
    return s[:, :D_MODEL] * pa + s[:, D_MODEL:] * pb


def _merge_fwd(gates, pa, pb, bg, tm=256):
    t = gates.shape[0]

    def body(g_ref, pa_ref, pb_ref, bg_ref, o_ref):
        o_ref[...] = _merge(g_ref[...], pa_ref[...], pb_ref[...], bg_ref[...]).astype(BF16)

    return pl.pallas_call(
        body, name="merge_fwd", grid=(t // tm,),
        in_specs=[_rows(tm, 2 * D_MODEL), _rows(tm, D_MODEL), _rows(tm, D_MODEL), _full((1, 2 * D_MODEL))],
        out_specs=_rows(tm, D_MODEL), out_shape=jax.ShapeDtypeStruct((t, D_MODEL), BF16), compiler_params=_params(1),
    )(gates, pa, pb, bg)


def _merge_bwd(dmix, gates, pa, pb, bg, tm=256):
    t = gates.shape[0]

    def body(d_ref, g_ref, pa_ref, pb_ref, bg_ref, dg_ref, dpa_ref, dpb_ref, dbg_ref):
        @pl.when(pl.program_id(0) == 0)
        def _():
            dbg_ref[...] = jnp.zeros_like(dbg_ref)

        _, vjp = jax.vjp(_merge, g_ref[...], pa_ref[...], pb_ref[...], bg_ref[...])
        dg, dpa, dpb, dbg = vjp(d_ref[...])
        dg_ref[...] = dg.astype(BF16)
        dpa_ref[...] = dpa.astype(BF16)
        dpb_ref[...] = dpb.astype(BF16)
        dbg_ref[...] += dbg

    return pl.pallas_call(
        body, name="merge_bwd", grid=(t // tm,),
        in_specs=[_rows(tm, D_MODEL), _rows(tm, 2 * D_MODEL), _rows(tm, D_MODEL), _rows(tm, D_MODEL),
                  _full((1, 2 * D_MODEL))],
        out_specs=[_rows(tm, 2 * D_MODEL), _rows(tm, D_MODEL), _rows(tm, D_MODEL), _full((1, 2 * D_MODEL))],
        out_shape=[jax.ShapeDtypeStruct((t, 2 * D_MODEL), BF16), jax.ShapeDtypeStruct((t, D_MODEL), BF16),
                   jax.ShapeDtypeStruct((t, D_MODEL), BF16), jax.ShapeDtypeStruct((1, 2 * D_MODEL), F32)],
        compiler_params=_params(1),
    )(dmix, gates, pa, pb, bg)


def _residual_norm_fwd(x, o, w, tm=512):
    t, d = x.shape

    def body(x_ref, o_ref, w_ref, h_ref, n_ref):
        h = x_ref[...] + o_ref[...]
        h_ref[...] = h
        n_ref[...] = _rms(h, w_ref[...]).astype(BF16)

    return pl.pallas_call(
        body, name="residual_norm_fwd", grid=(t // tm,), in_specs=[_rows(tm, d), _rows(tm, d), _full((1, d))],
        out_specs=[_rows(tm, d), _rows(tm, d)],
        out_shape=[jax.ShapeDtypeStruct((t, d), F32), jax.ShapeDtypeStruct((t, d), BF16)], compiler_params=_params(1),
    )(x, o, w)


def _norm_bwd(dn, h, w, dres, name, tm=512):
    t, d = h.shape

    def body(dn_ref, h_ref, w_ref, dres_ref, dh_ref, dhb_ref, dw_ref):
        @pl.when(pl.program_id(0) == 0)
        def _():
            dw_ref[...] = jnp.zeros_like(dw_ref)

        _, vjp = jax.vjp(_rms, h_ref[...], w_ref[...])
        dh, dw = vjp(dn_ref[...])
        dh = dh + dres_ref[...]
        dh_ref[...] = dh
        dhb_ref[...] = dh.astype(BF16)
        dw_ref[...] += dw

    return pl.pallas_call(
        body, name=name, grid=(t // tm,), in_specs=[_rows(tm, d), _rows(tm, d), _full((1, d)), _rows(tm, d)],
        out_specs=[_rows(tm, d), _rows(tm, d), _full((1, d))],
        out_shape=[jax.ShapeDtypeStruct((t, d), F32), jax.ShapeDtypeStruct((t, d), BF16),
                   jax.ShapeDtypeStruct((1, d), F32)], compiler_params=_params(1),
    )(dn, h, w, dres)


def _conv_f_fwd(up, cw, cb, tm=128):
    t, c = up.shape

    def body(x_ref, h_ref, w_ref, b_ref, o_ref, y_ref):
        halo = jnp.where(pl.program_id(0) > 0, h_ref[...], 0.0)
        y = _causal_conv(x_ref[...], halo, w_ref[...], b_ref[...])
        y_ref[...] = y
        o_ref[...] = (_silu(y[:, :D_FF]) * y[:, D_FF:]).astype(BF16)

    return pl.pallas_call(
        body, name="conv_f_fwd", grid=(t // tm,),
        in_specs=[_rows(tm, c), _halo(tm, c), _full(cw.shape), _full((1, c))],
        out_specs=[_rows(tm, D_FF), _rows(tm, c)],
        out_shape=[jax.ShapeDtypeStruct((t, D_FF), BF16), jax.ShapeDtypeStruct((t, c), F32)],
        compiler_params=_params(1),
    )(up, up, cw, cb)


def _conv_f_bwd(dact, y, up, cw, tm=128):
    t, c = up.shape
    nt = t // tm

    def body(d_ref, y_ref, x_ref, w_ref, dx_ref, dw_ref, db_ref, nxt_scr):
        @pl.when(pl.program_id(0) == 0)
        def _():
            nxt_scr[...] = jnp.zeros_like(nxt_scr)
            dw_ref[...] = jnp.zeros_like(dw_ref)
            db_ref[...] = jnp.zeros_like(db_ref)

        a, v = y_ref[:, :D_FF], y_ref[:, D_FF:]
        d = d_ref[...]
        dy = jnp.concatenate([d * v * _dsilu(a), d * _silu(a)], axis=1)
        dx, dw = _causal_conv_bwd(dy, nxt_scr[...], x_ref[...], w_ref[...])
        dx_ref[...] = dx.astype(BF16)
        nxt_scr[...] = dy[:8]
        dw_ref[...] += dw
        db_ref[...] += _colsum(dy)

    return pl.pallas_call(
        body, name="conv_f_bwd", grid=(nt,),
        in_specs=[_rows(tm, D_FF, nt, True), _rows(tm, c, nt, True), _rows(tm, c, nt, True), _full(cw.shape)],
        out_specs=[_rows(tm, c, nt, True), _full(cw.shape), _full((1, c))],
        out_shape=[jax.ShapeDtypeStruct((t, c), BF16), jax.ShapeDtypeStruct(cw.shape, F32),
                   jax.ShapeDtypeStruct((1, c), F32)],
        scratch_shapes=[pltpu.VMEM((8, c), F32)], compiler_params=_params(1),
    )(dact, y, up, cw)


def _conv_a_bwd(dxs, db, dc, y, xbc, cw, tm=256):
    t, c = xbc.shape
    nt = t // tm

    def body(dxs_ref, db_ref, dc_ref, y_ref, x_ref, w_ref, dx_ref, dw_ref, dbias_ref, nxt_scr):
        @pl.when(pl.program_id(0) == 0)
        def _():
            nxt_scr[...] = jnp.zeros_like(nxt_scr)
            dw_ref[...] = jnp.zeros_like(dw_ref)
            dbias_ref[...] = jnp.zeros_like(dbias_ref)

        dy = jnp.concatenate([dxs_ref[...], db_ref[...], dc_ref[...]], axis=1) * _dsilu(y_ref[...])
        dx, dw = _causal_conv_bwd(dy, nxt_scr[...], x_ref[...], w_ref[...])
        dx_ref[...] = dx.astype(BF16)
        nxt_scr[...] = dy[:8]
        dw_ref[...] += dw
        dbias_ref[...] += _colsum(dy)

    return pl.pallas_call(
        body, name="conv_a_bwd", grid=(nt,),
        in_specs=[_rows(tm, SSD_INNER, nt, True), _rows(tm, SSD_BC, nt, True), _rows(tm, SSD_BC, nt, True),
                  _rows(tm, c, nt, True), _rows(tm, c, nt, True), _full(cw.shape)],
        out_specs=[_rows(tm, c, nt, True), _full(cw.shape), _full((1, c))],
        out_shape=[jax.ShapeDtypeStruct((t, c), BF16), jax.ShapeDtypeStruct(cw.shape, F32),
                   jax.ShapeDtypeStruct((1, c), F32)],
        scratch_shapes=[pltpu.VMEM((8, c), F32)], compiler_params=_params(1),
    )(dxs, db, dc, y, xbc, cw)


def _loss_head(h1, dn, w, target, tm=512):
    t, d = h1.shape

    def body(h_ref, dn_ref, w_ref, t_ref, loss_ref, dh_ref, dhb_ref, dw_ref):
        @pl.when(pl.program_id(0) == 0)
        def _():
            loss_ref[...] = jnp.zeros_like(loss_ref)
            dw_ref[...] = jnp.zeros_like(dw_ref)

        yf, vjp = jax.vjp(_rms, h_ref[...] + dn_ref[...], w_ref[...])
        err = yf - t_ref[...]
        loss_ref[...] += 0.5 * jnp.sum(jnp.mean(err * err, axis=-1, keepdims=True))
        dh, dw = vjp(err * (1.0 / d))
        dh_ref[...] = dh
        dhb_ref[...] = dh.astype(BF16)
        dw_ref[...] += dw

    return pl.pallas_call(
        body, name="loss_head", grid=(t // tm,),
        in_specs=[_rows(tm, d), _rows(tm, d), _full((1, d)), _rows(tm, d)],
        out_specs=[_full((8, LANES)), _rows(tm, d), _rows(tm, d), _full((1, d))],
        out_shape=[jax.ShapeDtypeStruct((8, LANES), F32), jax.ShapeDtypeStruct((t, d), F32),
                   jax.ShapeDtypeStruct((t, d), BF16), jax.ShapeDtypeStruct((1, d), F32)], compiler_params=_params(1),
    )(h1, dn, w, target)


def _pad_lanes(v, n=DT_PAD):
    return jnp.pad(v, ((0, 0), (0, n - v.shape[1])))


def _local_step(x, target, w, p, after=None, late_weights=None, on_grad=None):
    dtb, alog, dsk = _pad_lanes(p["dt_bias"]), _pad_lanes(p["a_log"]), _pad_lanes(p["d_skip"])
    bs_t = _pad_lanes(p["b_spatial"].T)
    e_heads = (jnp.arange(SSD_INNER)[:, None] // SSD_HEAD_DIM == jnp.arange(LANES)[None, :]).astype(BF16)
    e_groups = (jnp.arange(SGU_WIDTH)[:, None] // LANES == jnp.arange(LANES)[None, :]).astype(BF16)

    n1 = _norm_fwd(x, p["norm1_w"], "norm1_fwd", after=after)
    z = _mm(n1, w["z"], "nt", "proj_z")
    xbc = _mm(n1, w["xbc"], "nt", "proj_xbc")
    dtr = _mm(n1, w["dt"], "nt", "proj_dt")
    uv = _mm(n1, w["uv"], "nt", "proj_uv")
    gates = _mm(n1, w["gates"], "nt", "proj_gates")
    xc, conv_a_out = _conv_a_fwd(xbc, w["conv_a"], p["conv_a_b"])
    y, ya, sprev = _ssd_fwd(xc, dtr, z, dtb, alog, dsk, p["ssd_norm_w"])
    yb = _sgu_fwd(uv, p["uv_b"], p["v_ln_w"], p["v_ln_b"], p["w_spatial"], bs_t)
    if late_weights is not None:
        w = {**w, **late_weights(ya, yb)}
    pa = _mm(ya, w["branch_a"], "nn", "branch_a")
    pb = _mm(yb, w["branch_b"], "nn", "branch_b")
    mix = _merge_fwd(gates, pa, pb, p["b_gate"])
    o = _mm(mix, w["out"], "nn", "out_proj")
    h1, n2 = _residual_norm_fwd(x, o, p["norm2_w"])
    up = _mm(n2, w["up"], "nt", "up_proj")
    act, conv_f_out = _conv_f_fwd(up, w["conv_f"], p["conv_f_b"])
    dn = _mm(act, w["down"], "nn", "down_proj")
    loss, dh2, dh2b, g_final = _loss_head(h1, dn, p["final_norm_w"], target)

    on_grad = on_grad or (lambda name, grads: None)
    g = {"final_norm_w": g_final}
    g["down"] = _wgrad(act, dh2b, "down_wgrad")
    tok = on_grad("w_down", g)
    dact = _mm(dh2b, w["down"], "nt", "down_dgrad", after=tok)
    dup, g["conv_f"], g["conv_f_b"] = _conv_f_bwd(dact, conv_f_out, up, w["conv_f"])
    g["up"] = _wgrad(dup, n2, "up_wgrad")
    tok = on_grad("w_up", g)
    dn2 = _mm(dup, w["up"], "nn", "up_dgrad", after=tok)
    dh1, dh1b, g["norm2_w"] = _norm_bwd(dn2, h1, p["norm2_w"], dh2, "norm2_bwd")
    g["out"] = _wgrad(mix, dh1b, "out_wgrad")
    tok = on_grad("w_out", g)
    dmix = _mm(dh1b, w["out"], "nt", "out_dgrad", after=tok)
    dgates, dpa, dpb, g["b_gate"] = _merge_bwd(dmix, gates, pa, pb, p["b_gate"])
    g["branch_a"] = _wgrad(ya, dpa, "branch_a_wgrad")
    g["branch_b"] = _wgrad(yb, dpb, "branch_b_wgrad")
    tok = on_grad("w_branch", g)
    dya = _mm(dpa, w["branch_a"], "nt", "branch_a_dgrad", after=tok)
    dyb = _mm(dpb, w["branch_b"], "nt", "branch_b_dgrad", after=tok)
    duv, g["uv_b"], g["v_ln_w"], g["v_ln_b"], g["w_spatial"], dbs_t = _sgu_bwd(
        dyb, uv, p["uv_b"], p["v_ln_w"], p["v_ln_b"], p["w_spatial"], bs_t, e_groups)
    g["b_spatial"] = dbs_t[:, :SGU_GROUPS].T
    dz, dxs, db, dc, ddtr, g["ssd_norm_w"], ddtb, dalog, ddsk = _ssd_bwd(
        dya, y, z, xc, dtr, sprev, dtb, alog, dsk, p["ssd_norm_w"], e_heads)
    g["dt_bias"], g["a_log"], g["d_skip"] = ddtb, dalog, ddsk
    dxbc, g["conv_a"], g["conv_a_b"] = _conv_a_bwd(dxs, db, dc, conv_a_out, xbc, w["conv_a"])
    ddtrb = ddtr.astype(BF16)
    for name, d in (("z", dz), ("xbc", dxbc), ("dt", ddtrb), ("uv", duv), ("gates", dgates)):
        g[name] = _wgrad(d, n1, name + "_wgrad")
    tok = on_grad("w_in", g)
    dn1 = _mm(dz, w["z"], "nn", "z_dgrad", after=tok)
    dn1 = _mm(dxbc, w["xbc"], "nn", "xbc_dgrad", acc=dn1)
    dn1 = _mm(ddtrb, w["dt"], "nn", "dt_dgrad", acc=dn1)
    dn1 = _mm(duv, w["uv"], "nn", "uv_dgrad", acc=dn1)
    dn1 = _mm(dgates, w["gates"], "nn", "gates_dgrad", acc=dn1)
    gx, _, g["norm1_w"] = _norm_bwd(dn1, x, p["norm1_w"], dh1, "norm1_bwd")
    return loss, gx, g


def _place():
    return lax.axis_index("x"), lax.axis_index("y"), lax.axis_index("c")


def _other_chips(x, y):
    return [(1 - x, y), (x, 1 - y), (1 - x, 1 - y)]


def _all_gather(shards, name):
    n = len(shards)

    def body(*refs):
        ins, outs = refs[:n], refs[n:2 * n]
        send_sems, recv_sems, local_sems = refs[2 * n:]
        x, y, c = _place()
        me, sibling = (x, y, c), (x, y, 1 - c)
        chips = _other_chips(x, y)

        def copy(a, k, block, to, src=None):
            slot = outs[a].at[4 * block[0] + 2 * block[1] + block[2]]
            return pltpu.make_async_remote_copy(
                src_ref=slot if src is None else src, dst_ref=slot, send_sem=send_sems.at[7 * a + k],
                recv_sem=recv_sems.at[7 * a + k], device_id=to, device_id_type=MESH)

        started = []
        for a in range(n):
            mine = pltpu.make_async_copy(ins[a], outs[a].at[4 * x + 2 * y + c], local_sems.at[a])
            mine.start()
            started.append(mine)
        sends = []
        for a in range(n):
            sends.append(copy(a, 0, me, sibling, src=ins[a]))
            sends += [copy(a, 1 + j, me, (*chip, c), src=ins[a]) for j, chip in enumerate(chips)]
        for cp in sends:
            cp.start()
        for a in range(n):
            for j, chip in enumerate(chips):
                copy(a, 1 + j, (*chip, c), me).wait_recv()
                fwd = copy(a, 4 + j, (*chip, c), sibling)
                fwd.start()
                sends.append(fwd)
        for a in range(n):
            copy(a, 0, sibling, me).wait_recv()
            for j, chip in enumerate(chips):
                copy(a, 4 + j, (*chip, 1 - c), me).wait_recv()
        for cp in sends:
            cp.wait_send()
        for mine in started:
            mine.wait()

    any_spec = pl.BlockSpec(memory_space=pl.ANY)
    return pl.pallas_call(
        body, name=name, in_specs=[any_spec] * n, out_specs=[any_spec] * n,
        out_shape=[jax.ShapeDtypeStruct((N_DEV, *s.shape), s.dtype) for s in shards],
        scratch_shapes=[pltpu.SemaphoreType.DMA((7 * n,)), pltpu.SemaphoreType.DMA((7 * n,)),
                        pltpu.SemaphoreType.DMA((n,))],
    )(*shards)


HBM_SPEC = pl.BlockSpec(memory_space=pltpu.HBM)
SEM_SPEC = pl.BlockSpec(memory_space=pltpu.SEMAPHORE)
ANY_SPEC = pl.BlockSpec(memory_space=pl.ANY)
DATAFLOW = pltpu.SideEffectType.DATAFLOW_SIDE_EFFECTING
N_PEERS = N_DEV - 1


def _peers(x, y, c):
    out = []
    for r in range(1, N_DEV):
        fx, fy, fc = r >> 2 & 1, r >> 1 & 1, r & 1
        out.append(((1 - x) if fx else x, (1 - y) if fy else y, (1 - c) if fc else c))
    return out


def _gather_copies(srcs, lands, send_sems, recv_sems, sending):
    x, y, c = _place()
    copies = []
    for a, (src, land) in enumerate(zip(srcs, lands)):
        for j, (px, py, pc) in enumerate(_peers(x, y, c)):
            slot = 4 * x + 2 * y + c if sending else 4 * px + 2 * py + pc
            copies.append(pltpu.make_async_remote_copy(
                src_ref=src, dst_ref=land.at[slot], send_sem=send_sems.at[N_PEERS * a + j],
                recv_sem=recv_sems.at[N_PEERS * a + j], device_id=(px, py, pc), device_id_type=MESH))
    return copies


def _gather_start(shards, after, name):
    n = len(shards)

    def body(*refs):
        srcs, lands = refs[:n], refs[n:2 * n]
        send_sems, recv_sems = refs[2 * n + 1:2 * n + 3]
        token = refs[-1]
        for cp in _gather_copies(srcs, lands, send_sems, recv_sems, sending=True):
            cp.start()
        token[...] = jnp.zeros_like(token)

    lands = [lax.empty((N_DEV, *s.shape), s.dtype) for s in shards]
    hbm = lambda a: pltpu.with_memory_space_constraint(a, pltpu.HBM)
    out = pl.pallas_call(
        body, name=name,
        out_shape=(pltpu.SemaphoreType.DMA((N_PEERS * n,)), pltpu.SemaphoreType.DMA((N_PEERS * n,)),
                   *[pltpu.HBM(a.shape, a.dtype) for a in (*shards, *lands)], jax.ShapeDtypeStruct((8, LANES), F32)),
        in_specs=[HBM_SPEC] * (2 * n) + [ANY_SPEC],
        out_specs=(SEM_SPEC, SEM_SPEC, *[HBM_SPEC] * (2 * n), pl.BlockSpec(memory_space=pltpu.VMEM)),
        input_output_aliases={i: 2 + i for i in range(2 * n)},
        compiler_params=pltpu.CompilerParams(has_side_effects=DATAFLOW),
    )(*[hbm(a) for a in (*shards, *lands)], after)
    return out[0], out[1], out[2:2 + n], out[2 + n:2 + 2 * n], out[-1]


def _gather_wait(send_sems, recv_sems, shards, lands, after, name):
    n = len(shards)
    after = tuple(after)

    def body(*refs):
        srcs, lands_ = refs[:n], refs[n:2 * n]
        send, recv = refs[2 * n:2 * n + 2]
        for cp in _gather_copies(srcs, lands_, send, recv, sending=False):
            cp.wait_send()
            cp.wait_recv()

    out = pl.pallas_call(
        body, name=name, out_shape=tuple(pltpu.HBM(a.shape, a.dtype) for a in (*shards, *lands)),
        in_specs=[HBM_SPEC] * (2 * n) + [SEM_SPEC, SEM_SPEC] + [ANY_SPEC] * len(after),
        out_specs=tuple([HBM_SPEC] * (2 * n)), input_output_aliases={i: i for i in range(2 * n)},
        compiler_params=pltpu.CompilerParams(has_side_effects=DATAFLOW),
    )(*shards, *lands, send_sems, recv_sems, *after)
    return out[n:]


def _chip_copies(src, land, send_sems, recv_sems):
    x, y, c = _place()
    return [pltpu.make_async_remote_copy(
        src_ref=src.at[2 * cx + cy], dst_ref=land.at[j], send_sem=send_sems.at[j], recv_sem=recv_sems.at[j],
        device_id=(cx, cy, c), device_id_type=MESH) for j, (cx, cy) in enumerate(_other_chips(x, y))]


def _chips_start(q, name):
    def body(q_ref, land_ref, send_sems, recv_sems, q_thru, land_thru, token):
        for cp in _chip_copies(q_ref, land_ref, send_sems, recv_sems):
            cp.start()
        token[...] = jnp.zeros_like(token)

    land = lax.empty((3, *q.shape[1:]), q.dtype)
    return pl.pallas_call(
        body, name=name,
        out_shape=(pltpu.SemaphoreType.DMA((3,)), pltpu.SemaphoreType.DMA((3,)), pltpu.HBM(q.shape, q.dtype),
                   pltpu.HBM(land.shape, land.dtype), jax.ShapeDtypeStruct((8, LANES), F32)),
        in_specs=[HBM_SPEC, HBM_SPEC],
        out_specs=(SEM_SPEC, SEM_SPEC, HBM_SPEC, HBM_SPEC, pl.BlockSpec(memory_space=pltpu.VMEM)),
        input_output_aliases={0: 2, 1: 3}, compiler_params=pltpu.CompilerParams(has_side_effects=DATAFLOW),
    )(pltpu.with_memory_space_constraint(q, pltpu.HBM), pltpu.with_memory_space_constraint(land, pltpu.HBM))


def _chips_wait(send_sems, recv_sems, q, land, after, name):
    def body(q_ref, land_ref, send, recv, after_ref, q_out, land_out):
        for cp in _chip_copies(q_ref, land_ref, send, recv):
            cp.wait_send()
            cp.wait_recv()

    return pl.pallas_call(
        body, name=name, out_shape=(pltpu.HBM(q.shape, q.dtype), pltpu.HBM(land.shape, land.dtype)),
        in_specs=[HBM_SPEC, HBM_SPEC, SEM_SPEC, SEM_SPEC, ANY_SPEC], out_specs=(HBM_SPEC, HBM_SPEC),
        input_output_aliases={0: 0, 1: 1}, compiler_params=pltpu.CompilerParams(has_side_effects=DATAFLOW),
    )(q, land, send_sems, recv_sems, after)[1]


def _exchange_cores(parts, name):
    n = len(parts)

    def body(*refs):
        ins, outs = refs[:n], refs[n:2 * n]
        send_sems, recv_sems = refs[2 * n:]
        x, y, c = _place()
        copies = []
        for a in range(n):
            for k in range(4):
                copies.append(pltpu.make_async_remote_copy(
                    src_ref=ins[a].at[2 * k + (1 - c)], dst_ref=outs[a].at[k], send_sem=send_sems.at[4 * a + k],
                    recv_sem=recv_sems.at[4 * a + k], device_id=(x, y, 1 - c), device_id_type=MESH))
        for cp in copies:
            cp.start()
        for cp in copies:
            cp.wait()

    any_spec = pl.BlockSpec(memory_space=pl.ANY)
    return pl.pallas_call(
        body, name=name, in_specs=[any_spec] * n, out_specs=[any_spec] * n,
        out_shape=[jax.ShapeDtypeStruct((4, *s.shape[1:]), s.dtype) for s in parts],
        scratch_shapes=[pltpu.SemaphoreType.DMA((4 * n,)), pltpu.SemaphoreType.DMA((4 * n,))],
    )(*parts)


def _chip_sum(part, got, place, name, tr=256):
    _, r, c = part.shape
    tr, tc = _tile2d(r, c, tr)

    def body(place_ref, p_ref, g_ref, q_ref, own_ref):
        s = p_ref[0].astype(F32) + g_ref[0].astype(F32)
        q_ref[0] = s.astype(BF16)

        @pl.when(pl.program_id(2) == place_ref[1])
        def _():
            own_ref[...] = s

    grid_spec = pltpu.PrefetchScalarGridSpec(
        num_scalar_prefetch=1, grid=(r // tr, c // tc, 4),
        in_specs=[pl.BlockSpec((1, tr, tc), lambda i, j, k, pr: (2 * k + pr[0], i, j)),
                  pl.BlockSpec((1, tr, tc), lambda i, j, k, pr: (k, i, j))],
        out_specs=[pl.BlockSpec((1, tr, tc), lambda i, j, k, pr: (k, i, j)),
                   pl.BlockSpec((tr, tc), lambda i, j, k, pr: (i, j))])
    return pl.pallas_call(
        body, name=name, grid_spec=grid_spec,
        out_shape=[jax.ShapeDtypeStruct((4, r, c), BF16), jax.ShapeDtypeStruct((r, c), F32)],
        compiler_params=_params(3),
    )(place, part, got)


def _adamw(w, g, m, v):
    m = ADAM_B1 * m + (1.0 - ADAM_B1) * g
    v = ADAM_B2 * v + (1.0 - ADAM_B2) * jnp.square(g)
    m_hat = m / (1.0 - ADAM_B1 ** ADAM_STEP)
    v_hat = v / (1.0 - ADAM_B2 ** ADAM_STEP)
    return -ADAM_LR * (m_hat / (jnp.sqrt(v_hat) + ADAM_EPS) + ADAM_WD * w), m, v


def _sum_adamw(own, got, w, m, v, name, tr=256):
    r, c = own.shape
    if w.ndim == 3:
        tr, tc = r, 4 * LANES
        wblk = pl.BlockSpec((tr, 1, tc), lambda i, j: (i, 0, j))
    else:
        tr, tc = _tile2d(r, c, tr)
        wblk = pl.BlockSpec((tr, tc), lambda i, j: (i, j))

    def body(own_ref, got_ref, w_ref, m_ref, v_ref, g_ref, d_ref, nm_ref, nv_ref):
        g = own_ref[...]
        for j in range(3):
            g = g + got_ref[j].astype(F32)
        two_d = lambda ref: ref[...].reshape(tr, tc)
        delta, nm, nv = _adamw(two_d(w_ref), g, two_d(m_ref), two_d(v_ref))
        for ref, val in ((g_ref, g), (d_ref, delta), (nm_ref, nm), (nv_ref, nv)):
            ref[...] = val.reshape(ref.shape)

    blk = pl.BlockSpec((tr, tc), lambda i, j: (i, j))
    return pl.pallas_call(
        body, name=name, grid=(r // tr, c // tc),
        in_specs=[blk, pl.BlockSpec((3, tr, tc), lambda i, j: (0, i, j)), wblk, wblk, wblk], out_specs=[wblk] * 4,
        out_shape=[jax.ShapeDtypeStruct(w.shape, F32)] * 4, compiler_params=_params(2),
    )(own, got, w, m, v)


VECTORS = ["norm1_w", "b_gate", "conv_a_b", "dt_bias", "a_log", "d_skip", "ssd_norm_w", "uv_b", "v_ln_w", "v_ln_b",
           "norm2_w", "conv_f_b", "final_norm_w"]
SMALL_ORDER = VECTORS + ["w_spatial", "b_spatial", "conv_a_w", "conv_f_w"]


def _small_adamw(gathered, w, m, v):
    sizes = {n: w[n].shape[1] for n in VECTORS}
    offs, off = {}, 0
    for n in VECTORS:
        offs[n] = off
        off += -(-sizes[n] // LANES) * LANES
    k = len(SMALL_ORDER)

    def body(*refs):
        row_ref, ws_ref, bs_ref, ca_ref, cf_ref = refs[:5]
        w_refs, m_refs, v_refs = (dict(zip(SMALL_ORDER, refs[5 + i * k:5 + (i + 1) * k])) for i in range(3))
        outs = refs[5 + 3 * k:]
        x, y, c = _place()
        dev = 4 * x + 2 * y + c

        def total(ref):
            s = ref[0]
            for d in range(1, N_DEV):
                s = s + ref[d]
            return s

        row = total(row_ref)
        grads = {n: row[:, offs[n]:offs[n] + sizes[n]] for n in VECTORS}
        grads["w_spatial"], grads["b_spatial"] = total(ws_ref), total(bs_ref)
        for n, ref in (("conv_a_w", ca_ref), ("conv_f_w", cf_ref)):
            whole, cols = total(ref), w_refs[n].shape[1]
            mine = whole[:, :cols]
            for d in range(1, N_DEV):
                mine = jnp.where(dev == d, whole[:, d * cols:(d + 1) * cols], mine)
            grads[n] = mine
        for i, n in enumerate(SMALL_ORDER):
            outs[4 * i][...] = grads[n]
            outs[4 * i + 1][...], outs[4 * i + 2][...], outs[4 * i + 3][...] = _adamw(
                w_refs[n][...], grads[n], m_refs[n][...], v_refs[n][...])

    out = pl.pallas_call(
        body, name="adamw_small",
        out_shape=[jax.ShapeDtypeStruct(w[n].shape, F32) for n in SMALL_ORDER for _ in range(4)],
        compiler_params=_params(0),
    )(*gathered, *[t[n] for t in (w, m, v) for n in SMALL_ORDER])
    return [dict(zip(SMALL_ORDER, out[j::4])) for j in range(4)]


SMALL = ["norm1_w", "b_gate", "conv_a_b", "dt_bias", "a_log", "d_skip", "ssd_norm_w", "uv_b", "v_ln_w", "v_ln_b",
         "w_spatial", "b_spatial", "norm2_w", "conv_f_b", "final_norm_w"]
BIG = ["w_in", "w_branch", "w_out", "w_up", "w_down"]
TRANSPOSED = ("w_in", "w_up")
WEIGHTS = ["norm1_w", "w_in", "b_gate", "conv_a_w", "conv_a_b", "dt_bias", "a_log", "d_skip", "ssd_norm_w", "uv_b",
           "v_ln_w", "v_ln_b", "w_spatial", "b_spatial", "w_branch", "w_out", "norm2_w", "w_up", "conv_f_w",
           "conv_f_b", "w_down", "final_norm_w"]
IN_SPLITS = [("z", 0, 2048), ("xbc", 2048, 5120), ("dt", 5120, 5152), ("uv", 5152, 7200), ("gates", 7200, 9248)]


def _columns_from_devices(a):
    return a.transpose(1, 0, 2).reshape(a.shape[1], -1)


def kernel(x, norm1_w, w_in, b_gate, conv_a_w, conv_a_b, dt_bias, a_log, d_skip, ssd_norm_w, uv_b, v_ln_w, v_ln_b, w_spatial, b_spatial, w_branch, w_out, norm2_w, w_up, conv_f_w, conv_f_b, w_down, final_norm_w, loss_target, m_norm1_w, m_w_in, m_b_gate, m_conv_a_w, m_conv_a_b, m_dt_bias, m_a_log, m_d_skip, m_ssd_norm_w, m_uv_b, m_v_ln_w, m_v_ln_b, m_w_spatial, m_b_spatial, m_w_branch, m_w_out, m_norm2_w, m_w_up, m_conv_f_w, m_conv_f_b, m_w_down, m_final_norm_w, v_norm1_w, v_w_in, v_b_gate, v_conv_a_w, v_conv_a_b, v_dt_bias, v_a_log, v_d_skip, v_ssd_norm_w, v_uv_b, v_v_ln_w, v_v_ln_b, v_w_spatial, v_b_spatial, v_w_branch, v_w_out, v_norm2_w, v_w_up, v_conv_f_w, v_conv_f_b, v_w_down, v_final_norm_w):
    args = dict(locals())
    wts = {n: args[n] for n in WEIGHTS}
    mom = {n: args["m_" + n] for n in WEIGHTS}
    var = {n: args["v_" + n] for n in WEIGHTS}
    cx, cy, cc = _place()
    dev = 4 * cx + 2 * cy + cc
    place = jnp.stack([cc, 2 * cx + cy]).astype(jnp.int32)

    def shard2d(n, a):
        return a[0].T if n in TRANSPOSED else a[0]

    def unshard(n, b):
        return (b.T if n in TRANSPOSED else b)[None]

    g_in, g_conv_a, g_conv_f = _all_gather(
        [shard2d("w_in", w_in).astype(BF16), conv_a_w[0], conv_f_w[0]], "gather_w_in")
    late = [shard2d(n, wts[n]).astype(BF16) for n in BIG[1:]]
    send_sems, recv_sems, late, lands, token = _gather_start(late, g_in, "gather_late_start")
    w_in_rows = g_in.reshape(-1, D_MODEL)
    w = {name: w_in_rows[lo:hi] for name, lo, hi in IN_SPLITS}
    w["dt"] = jnp.pad(w["dt"], ((0, DT_PAD - SSD_HEADS), (0, 0)))
    w["conv_a"] = _columns_from_devices(g_conv_a)
    w["conv_f"] = _columns_from_devices(g_conv_f)

    def late_weights(*after):
        got = _gather_wait(send_sems, recv_sems, late, lands, after, "gather_late_wait")
        g_branch, g_out, g_up, g_down = [lax.dynamic_update_index_in_dim(land, mine, dev, 0).reshape(-1, D_MODEL)
                                         for land, mine in zip(got, late)]
        return {"branch_a": g_branch[:SSD_INNER], "branch_b": g_branch[SSD_INNER:], "out": g_out, "up": g_up,
                "down": g_down}

    in_flight = {}

    def on_grad(n, g):
        part = {"w_in": lambda: jnp.concatenate([g[name][:hi - lo] for name, lo, hi in IN_SPLITS], axis=0),
                "w_branch": lambda: jnp.concatenate([g["branch_a"], g["branch_b"]], axis=0),
                "w_out": lambda: g["out"], "w_up": lambda: g["up"], "w_down": lambda: g["down"]}[n]()
        part = part.reshape(N_DEV, -1, D_MODEL)
        from_core, = _exchange_cores([part], f"to_other_core_{n}")
        q, own = _chip_sum(part, from_core, place, f"chip_sum_{n}")
        send, recv, q, land, tok = _chips_start(q, f"to_other_chips_start_{n}")
        in_flight[n] = (own, send, recv, q, land)
        return tok

    p = {n: wts[n][0] if wts[n].ndim > 2 else wts[n].reshape(1, -1) for n in SMALL}
    loss, gx, g = _local_step(x[0], loss_target[0], w, p, after=token, late_weights=late_weights, on_grad=on_grad)
    loss = lax.psum(loss[0, 0], ("x", "y", "c"))

    small_g = [jnp.concatenate([g[n] for n in VECTORS], axis=1), g["w_spatial"], g["b_spatial"], g["conv_a"],
               g["conv_f"]]
    s_send, s_recv, s_mine, s_land, _ = _gather_start(small_g, gx, "gather_small_start")

    grads, delta, new_m, new_v = {}, {}, {}, {}

    def big_adamw(n, after):
        own, send, recv, q, land = in_flight[n]
        got = _chips_wait(send, recv, q, land, after, f"to_other_chips_wait_{n}")
        view = (lambda a: a.transpose(2, 0, 1)) if n == "w_in" else (lambda a: shard2d(n, a))
        back = (lambda b: b.transpose(1, 2, 0)) if n == "w_in" else (lambda b: unshard(n, b))
        out = _sum_adamw(own, got, view(wts[n]), view(mom[n]), view(var[n]), f"adamw_{n}")
        grads[n], delta[n], new_m[n], new_v[n] = [back(o) for o in out]
        return out[1]

    after = gx
    for n in ("w_down", "w_up", "w_out", "w_branch", "w_in"):
        after = big_adamw(n, after)
    gathered = _gather_wait(s_send, s_recv, s_mine, s_land, [after], "gather_small_wait")
    gathered = [lax.dynamic_update_index_in_dim(land, mine, dev, 0) for land, mine in zip(gathered, s_mine)]
    small = [{n: t[n][0] if t[n].ndim > 2 else t[n].reshape(1, -1) for n in SMALL_ORDER} for t in (wts, mom, var)]
    for tgt, out in zip((grads, delta, new_m, new_v), _small_adamw(gathered, *small)):
        tgt.update({n: out[n].reshape(wts[n].shape) for n in SMALL_ORDER})

    return (loss, gx[None], *[grads[n] for n in WEIGHTS], *[delta[n] for n in WEIGHTS],
            *[new_m[n] for n in WEIGHTS], *[new_v[n] for n in WEIGHTS])
```

```python
import functools

import jax
import jax.numpy as jnp
from jax import lax
from jax.experimental import pallas as pl
from jax.experimental.pallas import tpu as pltpu

F32, BF16 = jnp.float32, jnp.bfloat16
HIGHEST = lax.Precision.HIGHEST

D_MODEL = 1024
SSD_INNER = 2048
SSD_HEAD_DIM = 64
SSD_HEADS = 32
SSD_GROUPS = 4
SSD_STATE = 128
SSD_BC = SSD_GROUPS * SSD_STATE
SSD_XBC = SSD_INNER + 2 * SSD_BC
SSD_CONV = 4
CHUNK = 128
N_PAIRS = SSD_HEADS // 2
PAIRS_PER_GROUP = N_PAIRS // SSD_GROUPS
SGU_WIDTH = 1024
SGU_GROUPS = 8
D_FF = 2816
FFN_CONV = 3
NORM_EPS = 1e-6
LN_EPS = 1e-5
LANES = 128
DT_PAD = LANES

ADAM_LR, ADAM_B1, ADAM_B2, ADAM_EPS, ADAM_WD, ADAM_STEP = 0.001, 0.9, 0.999, 1e-08, 0.01, 10

N_DEV = 8
VMEM_LIMIT = 56 * 1024 * 1024
MESH = pl.DeviceIdType.MESH


def _params(n_grid, **kw):
    sem = dict(dimension_semantics=("arbitrary",) * n_grid) if n_grid else {}
    return pltpu.CompilerParams(vmem_limit_bytes=VMEM_LIMIT, **sem, **kw)


def _tile(n, pref):
    t = (min(pref, n) // LANES) * LANES
    while n % t:
        t -= LANES
    return t


def _row_tile(r, pref):
    for t in range(min(pref, r) // 16 * 16, 0, -16):
        if r % t == 0:
            return t
    return r


def _tile2d(r, c, rows):
    if r % 16 == 0:
        return _row_tile(r, rows), c
    return r, _tile(c, 2 * LANES)


def _rows(tm, n, nt=None, rev=False, col=0):
    if rev:
        return pl.BlockSpec((tm, n), lambda i: (nt - 1 - i, col))
    return pl.BlockSpec((tm, n), lambda i: (i, col))


def _halo(tm, n, nt=None, rev=False):
    per = tm // 8
    if rev:
        return pl.BlockSpec((8, n), lambda i: (jnp.maximum((nt - 1 - i) * per - 1, 0), 0))
    return pl.BlockSpec((8, n), lambda i: (jnp.maximum(i * per - 1, 0), 0))


def _full(shape):
    nd = len(shape)
    return pl.BlockSpec(shape, lambda *_: (0,) * nd)


def _rms(x, w, eps=NORM_EPS):
    return x * lax.rsqrt(jnp.mean(x * x, axis=-1, keepdims=True) + eps) * w


def _layer_norm(x, w, b):
    mu = jnp.mean(x, axis=-1, keepdims=True)
    var = jnp.mean(jnp.square(x - mu), axis=-1, keepdims=True)
    return (x - mu) * lax.rsqrt(var + LN_EPS) * w + b


def _sigmoid(x):
    return 1.0 / (1.0 + jnp.exp(-x))


def _silu(x):
    return x * _sigmoid(x)


def _dsilu(x):
    s = _sigmoid(x)
    return s * (1.0 + x * (1.0 - s))


def _softplus(x):
    return jnp.maximum(x, 0.0) + jnp.log(1.0 + jnp.exp(-jnp.abs(x)))


def _gelu(x):
    return jax.nn.gelu(x)


def _dot(a, b):
    return jnp.dot(a, b, preferred_element_type=F32)


def _dot_nt(a, b):
    return lax.dot_general(a, b, (((1,), (1,)), ((), ())), preferred_element_type=F32)


def _dot_tn(a, b):
    return lax.dot_general(a, b, (((0,), (0,)), ((), ())), preferred_element_type=F32)


def _dot_split(p, e):
    hi = p.astype(BF16)
    lo = (p - hi.astype(F32)).astype(BF16)
    return _dot(hi, e) + _dot(lo, e)


def _colsum(x):
    return jnp.sum(x, axis=0, keepdims=True)


def _shift_down(x, halo, j):
    xs = pltpu.roll(x, j, 0)
    hs = pltpu.roll(halo, j, 0)
    r8 = lax.broadcasted_iota(jnp.int32, hs.shape, 0)
    return jnp.concatenate([jnp.where(r8 < j, hs, xs[:8]), xs[8:]], axis=0)


def _shift_up(x, nxt, j):
    n = x.shape[0]
    xs = pltpu.roll(x, n - j, 0)
    ns = pltpu.roll(nxt, 8 - j, 0)
    r8 = lax.broadcasted_iota(jnp.int32, ns.shape, 0)
    return jnp.concatenate([xs[:n - 8], jnp.where(r8 >= 8 - j, ns, xs[n - 8:])], axis=0)


def _causal_conv(x, halo, w, b):
    k = w.shape[0]
    y = b + w[k - 1:k, :] * x
    for j in range(1, k):
        y = y + w[k - 1 - j:k - j, :] * _shift_down(x, halo, j)
    return y


def _causal_conv_bwd(dy, nxt, x, w):
    k = w.shape[0]
    dx = w[k - 1:k, :] * dy
    dw = [_colsum(dy * x)]
    for j in range(1, k):
        dyj = _shift_up(dy, nxt, j)
        dx = dx + w[k - 1 - j:k - j, :] * dyj
        dw.append(_colsum(dyj * x))
    return dx, jnp.concatenate(dw[::-1], axis=0)


MM_TILE_PREF = 1408
MM_VMEM_BUDGET = 40 * 1024 * 1024


def _mm_tiles(m, n, k, out_bytes):
    tm, tn = _tile(m, MM_TILE_PREF), _tile(n, MM_TILE_PREF)
    need = lambda tm, tn: 2 * (2 * k * (tm + tn) + out_bytes * tm * tn)
    while need(tm, tn) > MM_VMEM_BUDGET:
        if tn >= tm and tn > LANES:
            tn = _tile(n, tn - LANES)
        else:
            tm = _tile(m, tm - LANES)
    return tm, tn


def _mm(a, b, dims, name, acc=None, out_dtype=F32, after=None):
    if dims == "tn":
        k, m = a.shape
    else:
        m, k = a.shape
    n = b.shape[0] if dims == "nt" else b.shape[1]
    tm, tn = _mm_tiles(m, n, k, 4 * (2 if acc is not None else 1))
    a_spec = pl.BlockSpec((k, tm), lambda j, i: (0, i)) if dims == "tn" else pl.BlockSpec((tm, k), lambda j, i: (i, 0))
    b_spec = pl.BlockSpec((tn, k), lambda j, i: (j, 0)) if dims == "nt" else pl.BlockSpec((k, tn), lambda j, i: (0, j))
    o_spec = pl.BlockSpec((tm, tn), lambda j, i: (i, j))
    dot = {"nn": _dot, "nt": _dot_nt, "tn": _dot_tn}[dims]

    def body(a_ref, b_ref, *rest):
        r = dot(a_ref[...], b_ref[...])
        if acc is not None:
            r = r + rest[0][...]
        rest[-1][...] = r.astype(out_dtype)

    ins, specs = [a, b], [a_spec, b_spec]
    if acc is not None:
        ins.append(acc)
        specs.append(o_spec)
    if after is not None:
        ins.append(after)
        specs.append(pl.BlockSpec(memory_space=pl.ANY))
    return pl.pallas_call(
        body, name=name, grid=(n // tn, m // tm), in_specs=specs, out_specs=o_spec,
        out_shape=jax.ShapeDtypeStruct((m, n), out_dtype), compiler_params=_params(2),
    )(*ins)


def _mm_rows(a, b, dims, name, fn, rows=(), fulls=(), row_outs=(), acc_outs=(), after=None):
    m, k = a.shape
    n = b.shape[0] if dims == "nt" else b.shape[1]
    per_row = 2 * k + 8 * n + sum(4 * r.shape[1] for r in rows) + sum(c * jnp.dtype(d).itemsize for c, d in row_outs)
    tm = _tile(m, 1024)
    while 2 * tm * per_row + 4 * k * n > MM_VMEM_BUDGET:
        tm = _tile(m, tm - LANES)
    dot = _dot_nt if dims == "nt" else _dot
    n_in = 2 + len(rows) + len(fulls) + (after is not None)

    def body(*refs):
        ins, outs = refs[:n_in], refs[n_in:]
        row_refs, acc_refs = outs[:len(row_outs)], outs[len(row_outs):]

        @pl.when(pl.program_id(0) == 0)
        def _():
            for r in acc_refs:
                r[...] = jnp.zeros_like(r)

        new_rows, incs = fn(dot(ins[0][...], ins[1][...]), *[r[...] for r in ins[2:2 + len(rows) + len(fulls)]])
        for r, val in zip(row_refs, new_rows):
            r[...] = val.astype(r.dtype)
        for r, inc in zip(acc_refs, incs):
            r[...] += inc

    b_spec = _full(b.shape)
    extra, extra_specs = ([after], [pl.BlockSpec(memory_space=pl.ANY)]) if after is not None else ([], [])
    return pl.pallas_call(
        body, name=name, grid=(m // tm,),
        in_specs=[_rows(tm, k), b_spec] + [_rows(tm, r.shape[1]) for r in rows] + [_full(f.shape) for f in fulls]
        + extra_specs,
        out_specs=[_rows(tm, c) for c, _ in row_outs] + [_full(s) for s in acc_outs],
        out_shape=[jax.ShapeDtypeStruct((m, c), d) for c, d in row_outs]
        + [jax.ShapeDtypeStruct(s, F32) for s in acc_outs],
        compiler_params=_params(1),
    )(a, b, *rows, *fulls, *extra)


def _residual_norm(o, x, w):
    h = x + o
    return (h, _rms(h, w)), ()


def _norm_backward(dn, h, dres, w):
    _, vjp = jax.vjp(_rms, h, w)
    dh, dw = vjp(dn)
    dh = dh + dres
    return (dh, dh), (dw,)


def _loss_and_grad(dn, h1, target, w):
    yf, vjp = jax.vjp(_rms, h1 + dn, w)
    err = yf - target
    loss = 0.5 * jnp.sum(jnp.mean(err * err, axis=-1, keepdims=True))
    dh, dw = vjp(err * (1.0 / err.shape[-1]))
    return (dh, dh), (jnp.full((8, LANES), loss, F32), dw)


def _wgrad(a, d, name):
    return _mm(a, d, "tn", name, out_dtype=BF16)


def _norm_fwd(x, w, name, after=None, tm=512):
    t, d = x.shape

    def body(x_ref, w_ref, *rest):
        rest[-1][...] = _rms(x_ref[...], w_ref[...]).astype(BF16)

    extra, extra_specs = ([after], [_full(after.shape)]) if after is not None else ([], [])
    return pl.pallas_call(
        body, name=name, grid=(t // tm,), in_specs=[_rows(tm, d), _full((1, d))] + extra_specs,
        out_specs=_rows(tm, d), out_shape=jax.ShapeDtypeStruct((t, d), BF16), compiler_params=_params(1),
    )(x, w, *extra)


def _conv_a_fwd(xbc, cw, cb, tm=256):
    t, c = xbc.shape

    def body(x_ref, h_ref, w_ref, b_ref, o_ref, y_ref):
        halo = jnp.where(pl.program_id(0) > 0, h_ref[...], 0.0)
        y = _causal_conv(x_ref[...], halo, w_ref[...], b_ref[...])
        y_ref[...] = y
        o_ref[...] = _silu(y)

    return pl.pallas_call(
        body, name="conv_a_fwd", grid=(t // tm,),
        in_specs=[_rows(tm, c), _halo(tm, c), _full(cw.shape), _full((1, c))], out_specs=[_rows(tm, c)] * 2,
        out_shape=[jax.ShapeDtypeStruct((t, c), F32)] * 2, compiler_params=_params(1),
    )(xbc, xbc, cw, cb)


def _ssd_common(dtr, dtb, alog):
    row = lax.broadcasted_iota(jnp.int32, (CHUNK, CHUNK), 0)
    col = lax.broadcasted_iota(jnp.int32, (CHUNK, CHUNK), 1)
    causal = row >= col
    dt = _softplus(dtr + dtb)
    a = -jnp.exp(alog)
    acum = jnp.dot(causal.astype(F32), dt * a, precision=HIGHEST, preferred_element_type=F32)
    return dt, a, acum, acum.T, causal, col < SSD_HEAD_DIM, row


def _pair_terms(j, dt, acum, acum_t, causal, lane_lo):
    h0, h1 = 2 * j, 2 * j + 1
    ac0, ac1 = acum[:, h0:h0 + 1], acum[:, h1:h1 + 1]
    l0 = jnp.exp(jnp.where(causal, ac0 - acum_t[h0:h0 + 1, :], -jnp.inf))
    l1 = jnp.exp(jnp.where(causal, ac1 - acum_t[h1:h1 + 1, :], -jnp.inf))
    dtp = jnp.where(lane_lo, dt[:, h0:h0 + 1], dt[:, h1:h1 + 1])
    al0, al1 = acum[CHUNK - 1:CHUNK, h0:h0 + 1], acum[CHUNK - 1:CHUNK, h1:h1 + 1]
    ecol = jnp.where(lane_lo, jnp.exp(ac0), jnp.exp(ac1))
    dsr = jnp.where(lane_lo, jnp.exp(al0 - ac0), jnp.exp(al1 - ac1))
    elast = jnp.where(lane_lo[0:1], jnp.exp(al0), jnp.exp(al1))
    return l0, l1, dtp, ecol, dsr, elast


def _ssd_fwd(xc, dtr, z, dtb, alog, dsk, nw):
    t = xc.shape[0]
    nc = t // CHUNK

    def body(xs_ref, b_ref, c_ref, dtr_ref, z_ref, dtb_ref, alog_ref, dsk_ref, nw_ref, y_ref, ya_ref, sp_ref, s_scr):
        @pl.when(pl.program_id(0) == 0)
        def _():
            s_scr[...] = jnp.zeros_like(s_scr)

        dt, a, acum, acum_t, causal, lane_lo, _ = _ssd_common(dtr_ref[...], dtb_ref[...], alog_ref[...])
        dsk = dsk_ref[...]
        for g in range(SSD_GROUPS):
            gs = slice(g * SSD_STATE, (g + 1) * SSD_STATE)
            bg, cg = b_ref[:, gs].astype(BF16), c_ref[:, gs].astype(BF16)
            cb = _dot_nt(cg, bg)
            for pp in range(PAIRS_PER_GROUP):
                j = g * PAIRS_PER_GROUP + pp
                ps = slice(j * LANES, (j + 1) * LANES)
                x = xs_ref[:, ps]
                l0, l1, dtp, ecol, dsr, elast = _pair_terms(j, dt, acum, acum_t, causal, lane_lo)
                xdt = x * dtp
                xb = xdt.astype(BF16)
                zero = jnp.zeros_like(xb)
                yd = (_dot((cb * l0).astype(BF16), jnp.where(lane_lo, xb, zero))
                      + _dot((cb * l1).astype(BF16), jnp.where(lane_lo, zero, xb)))
                sp = s_scr[j]
                yo = ecol * _dot(cg, sp.astype(BF16))
                st = _dot_tn(bg, (xdt * dsr).astype(BF16))
                sp_ref[0, j] = sp
                s_scr[j] = elast * sp + st
                dskp = jnp.where(lane_lo[0:1], dsk[:, 2 * j:2 * j + 1], dsk[:, 2 * j + 1:2 * j + 2])
                y_ref[:, ps] = yd + yo + dskp * x
        ya_ref[...] = _rms(y_ref[...] * _silu(z_ref[...]), nw_ref[...]).astype(BF16)

    ck = lambda n, col=0: pl.BlockSpec((CHUNK, n), lambda c: (c, col))
    return pl.pallas_call(
        body, name="ssd_fwd", grid=(nc,),
        in_specs=[ck(SSD_INNER), ck(SSD_BC, SSD_INNER // SSD_BC), ck(SSD_BC, SSD_INNER // SSD_BC + 1), ck(DT_PAD),
                  ck(SSD_INNER), _full((1, DT_PAD)), _full((1, DT_PAD)), _full((1, DT_PAD)), _full((1, SSD_INNER))],
        out_specs=[ck(SSD_INNER), ck(SSD_INNER),
                   pl.BlockSpec((1, N_PAIRS, SSD_STATE, LANES), lambda c: (c, 0, 0, 0))],
        out_shape=[jax.ShapeDtypeStruct((t, SSD_INNER), F32), jax.ShapeDtypeStruct((t, SSD_INNER), BF16),
                   jax.ShapeDtypeStruct((nc, N_PAIRS, SSD_STATE, LANES), F32)],
        scratch_shapes=[pltpu.VMEM((N_PAIRS, SSD_STATE, LANES), F32)], compiler_params=_params(1),
    )(xc, xc, xc, dtr, z, dtb, alog, dsk, nw)


def _ssd_bwd(dya, y, z, xc, dtr, sprev, dtb, alog, dsk, nw, e_heads):
    t = xc.shape[0]
    nc = t // CHUNK

    def body(dya_ref, y_ref, z_ref, xs_ref, b_ref, c_ref, dtr_ref, sp_ref, dtb_ref, alog_ref, dsk_ref, nw_ref, e_ref,
             dz_ref, dxs_ref, db_ref, dc_ref, ddtr_ref, dnw_ref, ddtb_ref, dalog_ref, ddsk_ref, ds_scr):
        @pl.when(pl.program_id(0) == 0)
        def _():
            ds_scr[...] = jnp.zeros_like(ds_scr)
            for r in (dnw_ref, ddtb_ref, dalog_ref, ddsk_ref):
                r[...] = jnp.zeros_like(r)

        y = y_ref[...]
        _, gate_vjp = jax.vjp(lambda y_, z_, w_: _rms(y_ * _silu(z_), w_), y, z_ref[...], nw_ref[...])
        dy, dz, dnw = gate_vjp(dya_ref[...])
        dz_ref[...] = dz.astype(BF16)
        dnw_ref[...] += dnw

        dtr = dtr_ref[...]
        dt, a, acum, acum_t, causal, lane_lo, row = _ssd_common(dtr, dtb_ref[...], alog_ref[...])
        dsk = dsk_ref[...]
        p_a, p_dt, v_last = [], [], []
        col = lax.broadcasted_iota(jnp.int32, (CHUNK, CHUNK), 1)
        da_cols = jnp.zeros((CHUNK, CHUNK), F32)
        da_rows = jnp.zeros((CHUNK, CHUNK), F32)
        for g in range(SSD_GROUPS):
            gs = slice(g * SSD_STATE, (g + 1) * SSD_STATE)
            bg, cg = b_ref[:, gs].astype(BF16), c_ref[:, gs].astype(BF16)
            cb = _dot_nt(cg, bg)
            dcb = jnp.zeros((CHUNK, CHUNK), F32)
            dbg = jnp.zeros((CHUNK, SSD_STATE), F32)
            dcg = jnp.zeros((CHUNK, SSD_STATE), F32)
            for pp in range(PAIRS_PER_GROUP):
                j = g * PAIRS_PER_GROUP + pp
                ps = slice(j * LANES, (j + 1) * LANES)
                x = xs_ref[:, ps]
                l0, l1, dtp, ecol, dsr, elast = _pair_terms(j, dt, acum, acum_t, causal, lane_lo)
                xdt = x * dtp
                xb = xdt.astype(BF16)
                dskp = jnp.where(lane_lo[0:1], dsk[:, 2 * j:2 * j + 1], dsk[:, 2 * j + 1:2 * j + 2])
                dyp = dy[:, ps]
                dyb = dyp.astype(BF16)
                sp, dsn = sp_ref[0, j], ds_scr[j]
                spb, dsnb = sp.astype(BF16), dsn.astype(BF16)
                y_off = ecol * _dot(cg, spb)
                dw = (dyp * ecol).astype(BF16)
                dcg = dcg + _dot_nt(dw, spb)
                dsp = _dot_tn(cg, dw) + elast * dsn
                xd = xdt * dsr
                zd = _dot(bg, dsnb) * dsr
                dbg = dbg + _dot_nt(xd.astype(BF16), dsnb)
                dxdt = zd
                zero = jnp.zeros_like(xb)
                for h, lm, le in ((2 * j, lane_lo, l0), (2 * j + 1, jnp.logical_not(lane_lo), l1)):
                    dm = _dot_nt(jnp.where(lm, dyb, zero), jnp.where(lm, xb, zero))
                    dcb = dcb + dm * le
                    m = cb * le
                    dxdt = dxdt + jnp.where(lm, _dot_tn(m.astype(BF16), dyb), 0.0)
                    q = dm * m
                    da_cols = da_cols + jnp.where(col == h, jnp.sum(q, axis=1, keepdims=True), 0.0)
                    da_rows = da_rows + jnp.where(row == h, _colsum(q), 0.0)
                ds_scr[j] = dsp
                dxs_ref[:, ps] = dxdt * dtp + dskp * dyp
                p_a.append(dyp * y_off - xdt * zd)
                p_dt.append(dxdt * x)
                v_last.append(_colsum(zd * xdt) + elast * _colsum(dsn * sp))
            dcbb = dcb.astype(BF16)
            db_ref[:, gs] = dbg + _dot_tn(dcbb, cg)
            dc_ref[:, gs] = dcg + _dot(dcbb, bg)
        e = e_ref[...]
        rows8 = jnp.concatenate([jnp.concatenate(v_last, axis=1), _colsum(dy * xs_ref[...]),
                                 jnp.zeros((6, SSD_INNER), F32)], axis=0)
        r8 = _dot_split(rows8, e)
        da = (_dot_split(jnp.concatenate(p_a, axis=1), e) + jnp.where(row == CHUNK - 1, r8[0:1], 0.0)
              + da_cols - da_rows.T)
        ddsk_ref[...] += r8[1:2]
        dadt = jnp.dot((row <= col).astype(F32), da, precision=HIGHEST, preferred_element_type=F32)
        ddt = dadt * a + _dot_split(jnp.concatenate(p_dt, axis=1), e)
        dalog_ref[...] += _colsum(dadt * dt) * a
        ddtr = ddt * _sigmoid(dtr + dtb_ref[...])
        ddtr_ref[...] = ddtr
        ddtb_ref[...] += _colsum(ddtr)

    ck = lambda n, col=0: pl.BlockSpec((CHUNK, n), lambda c: (nc - 1 - c, col))
    acc = lambda n: _full((1, n))
    return pl.pallas_call(
        body, name="ssd_bwd", grid=(nc,),
        in_specs=[ck(SSD_INNER), ck(SSD_INNER), ck(SSD_INNER), ck(SSD_INNER), ck(SSD_BC, SSD_INNER // SSD_BC),
                  ck(SSD_BC, SSD_INNER // SSD_BC + 1), ck(DT_PAD),
                  pl.BlockSpec((1, N_PAIRS, SSD_STATE, LANES), lambda c: (nc - 1 - c, 0, 0, 0)),
                  acc(DT_PAD), acc(DT_PAD), acc(DT_PAD), acc(SSD_INNER), _full((SSD_INNER, LANES))],
        out_specs=[ck(SSD_INNER), ck(SSD_INNER), ck(SSD_BC), ck(SSD_BC), ck(DT_PAD),
                   acc(SSD_INNER), acc(DT_PAD), acc(DT_PAD), acc(DT_PAD)],
        out_shape=[jax.ShapeDtypeStruct((t, SSD_INNER), BF16), jax.ShapeDtypeStruct((t, SSD_INNER), F32),
                   jax.ShapeDtypeStruct((t, SSD_BC), F32), jax.ShapeDtypeStruct((t, SSD_BC), F32),
                   jax.ShapeDtypeStruct((t, DT_PAD), F32), jax.ShapeDtypeStruct((1, SSD_INNER), F32),
                   jax.ShapeDtypeStruct((1, DT_PAD), F32), jax.ShapeDtypeStruct((1, DT_PAD), F32),
                   jax.ShapeDtypeStruct((1, DT_PAD), F32)],
        scratch_shapes=[pltpu.VMEM((N_PAIRS, SSD_STATE, LANES), F32)], compiler_params=_params(1),
    )(dya, y, z, xc, xc, xc, dtr, sprev, dtb, alog, dsk, nw, e_heads)


def _sgu_act(uv, uvb, lnw, lnb):
    a = _gelu(uv + uvb)
    return a[:, :SGU_WIDTH], _layer_norm(a[:, SGU_WIDTH:], lnw, lnb)


def _sgu_weights(ws_ref):
    row = lax.broadcasted_iota(jnp.int32, (CHUNK, CHUNK), 0)
    col = lax.broadcasted_iota(jnp.int32, (CHUNK, CHUNK), 1)
    return [jnp.where(row >= col, ws_ref[g], 0.0).astype(BF16) for g in range(SGU_GROUPS)], row >= col


def _sgu_fwd(uv, uvb, lnw, lnb, ws, bs_t):
    t = uv.shape[0]

    def body(uv_ref, uvb_ref, lnw_ref, lnb_ref, ws_ref, bs_ref, o_ref):
        u, vn = _sgu_act(uv_ref[...], uvb_ref[...], lnw_ref[...], lnb_ref[...])
        wc, _ = _sgu_weights(ws_ref)
        bs = bs_ref[...]
        for g in range(SGU_GROUPS):
            gs = slice(g * LANES, (g + 1) * LANES)
            mixed = _dot(wc[g], vn[:, gs].astype(BF16)) + bs[:, g:g + 1]
            o_ref[:, gs] = (u[:, gs] * mixed).astype(BF16)

    return pl.pallas_call(
        body, name="sgu_fwd", grid=(t // CHUNK,),
        in_specs=[_rows(CHUNK, 2 * SGU_WIDTH), _full((1, 2 * SGU_WIDTH)), _full((1, SGU_WIDTH)), _full((1, SGU_WIDTH)),
                  _full(ws.shape), _full(bs_t.shape)],
        out_specs=_rows(CHUNK, SGU_WIDTH), out_shape=jax.ShapeDtypeStruct((t, SGU_WIDTH), BF16),
        compiler_params=_params(1),
    )(uv, uvb, lnw, lnb, ws, bs_t)


def _sgu_bwd(dyb, uv, uvb, lnw, lnb, ws, bs_t, e_groups):
    t = uv.shape[0]

    def body(dyb_ref, uv_ref, uvb_ref, lnw_ref, lnb_ref, ws_ref, bs_ref, e_ref,
             duv_ref, duvb_ref, dlnw_ref, dlnb_ref, dws_ref, dbs_ref):
        @pl.when(pl.program_id(0) == 0)
        def _():
            for r in (duvb_ref, dlnw_ref, dlnb_ref, dws_ref, dbs_ref):
                r[...] = jnp.zeros_like(r)

        (u, vn), act_vjp = jax.vjp(_sgu_act, uv_ref[...], uvb_ref[...], lnw_ref[...], lnb_ref[...])
        wc, causal = _sgu_weights(ws_ref)
        bs = bs_ref[...]
        dyb = dyb_ref[...]
        du, dvn, dmix = [], [], []
        for g in range(SGU_GROUPS):
            gs = slice(g * LANES, (g + 1) * LANES)
            vb = vn[:, gs].astype(BF16)
            mixed = _dot(wc[g], vb) + bs[:, g:g + 1]
            dm = dyb[:, gs] * u[:, gs]
            dmb = dm.astype(BF16)
            du.append(dyb[:, gs] * mixed)
            dvn.append(_dot_tn(wc[g], dmb))
            dws_ref[g] += jnp.where(causal, _dot_nt(dmb, vb), 0.0)
            dmix.append(dm)
        dbs_ref[...] += _dot_split(jnp.concatenate(dmix, axis=1), e_ref[...])
        duv, duvb, dlnw, dlnb = act_vjp((jnp.concatenate(du, axis=1), jnp.concatenate(dvn, axis=1)))
        duv_ref[...] = duv.astype(BF16)
        duvb_ref[...] += duvb
        dlnw_ref[...] += dlnw
        dlnb_ref[...] += dlnb

    return pl.pallas_call(
        body, name="sgu_bwd", grid=(t // CHUNK,),
        in_specs=[_rows(CHUNK, SGU_WIDTH), _rows(CHUNK, 2 * SGU_WIDTH), _full((1, 2 * SGU_WIDTH)),
                  _full((1, SGU_WIDTH)), _full((1, SGU_WIDTH)), _full(ws.shape), _full(bs_t.shape),
                  _full(e_groups.shape)],
        out_specs=[_rows(CHUNK, 2 * SGU_WIDTH), _full((1, 2 * SGU_WIDTH)), _full((1, SGU_WIDTH)),
                   _full((1, SGU_WIDTH)), _full(ws.shape), _full(bs_t.shape)],
        out_shape=[jax.ShapeDtypeStruct((t, 2 * SGU_WIDTH), BF16), jax.ShapeDtypeStruct((1, 2 * SGU_WIDTH), F32),
                   jax.ShapeDtypeStruct((1, SGU_WIDTH), F32), jax.ShapeDtypeStruct((1, SGU_WIDTH), F32),
                   jax.ShapeDtypeStruct(ws.shape, F32), jax.ShapeDtypeStruct(bs_t.shape, F32)],
        compiler_params=_params(1),
    )(dyb, uv, uvb, lnw, lnb, ws, bs_t, e_groups)


def _merge(gates, pa, pb, bg):
    s = _sigmoid(gates + bg)
    return s[:, :D_MODEL] * pa + s[:, D_MODEL:] * pb


def _merge_fwd(gates, pa, pb, bg, tm=256):
    t = gates.shape[0]

    def body(g_ref, pa_ref, pb_ref, bg_ref, o_ref):
        o_ref[...] = _merge(g_ref[...], pa_ref[...], pb_ref[...], bg_ref[...]).astype(BF16)

    return pl.pallas_call(
        body, name="merge_fwd", grid=(t // tm,),
        in_specs=[_rows(tm, 2 * D_MODEL), _rows(tm, D_MODEL), _rows(tm, D_MODEL), _full((1, 2 * D_MODEL))],
        out_specs=_rows(tm, D_MODEL), out_shape=jax.ShapeDtypeStruct((t, D_MODEL), BF16), compiler_params=_params(1),
    )(gates, pa, pb, bg)


def _merge_bwd(dmix, gates, pa, pb, bg, tm=256):
    t = gates.shape[0]

    def body(d_ref, g_ref, pa_ref, pb_ref, bg_ref, dg_ref, dpa_ref, dpb_ref, dbg_ref):
        @pl.when(pl.program_id(0) == 0)
        def _():
            dbg_ref[...] = jnp.zeros_like(dbg_ref)

        _, vjp = jax.vjp(_merge, g_ref[...], pa_ref[...], pb_ref[...], bg_ref[...])
        dg, dpa, dpb, dbg = vjp(d_ref[...])
        dg_ref[...] = dg.astype(BF16)
        dpa_ref[...] = dpa.astype(BF16)
        dpb_ref[...] = dpb.astype(BF16)
        dbg_ref[...] += dbg

    return pl.pallas_call(
        body, name="merge_bwd", grid=(t // tm,),
        in_specs=[_rows(tm, D_MODEL), _rows(tm, 2 * D_MODEL), _rows(tm, D_MODEL), _rows(tm, D_MODEL),
                  _full((1, 2 * D_MODEL))],
        out_specs=[_rows(tm, 2 * D_MODEL), _rows(tm, D_MODEL), _rows(tm, D_MODEL), _full((1, 2 * D_MODEL))],
        out_shape=[jax.ShapeDtypeStruct((t, 2 * D_MODEL), BF16), jax.ShapeDtypeStruct((t, D_MODEL), BF16),
                   jax.ShapeDtypeStruct((t, D_MODEL), BF16), jax.ShapeDtypeStruct((1, 2 * D_MODEL), F32)],
        compiler_params=_params(1),
    )(dmix, gates, pa, pb, bg)


def _conv_f_fwd(up, cw, cb, tm=128):
    t, c = up.shape

    def body(x_ref, h_ref, w_ref, b_ref, o_ref, y_ref):
        halo = jnp.where(pl.program_id(0) > 0, h_ref[...], 0.0)
        y = _causal_conv(x_ref[...], halo, w_ref[...], b_ref[...])
        y_ref[...] = y
        o_ref[...] = (_silu(y[:, :D_FF]) * y[:, D_FF:]).astype(BF16)

    return pl.pallas_call(
        body, name="conv_f_fwd", grid=(t // tm,),
        in_specs=[_rows(tm, c), _halo(tm, c), _full(cw.shape), _full((1, c))],
        out_specs=[_rows(tm, D_FF), _rows(tm, c)],
        out_shape=[jax.ShapeDtypeStruct((t, D_FF), BF16), jax.ShapeDtypeStruct((t, c), F32)],
        compiler_params=_params(1),
    )(up, up, cw, cb)


def _conv_f_bwd(dact, y, up, cw, tm=128):
    t, c = up.shape
    nt = t // tm

    def body(d_ref, y_ref, x_ref, w_ref, dx_ref, dw_ref, db_ref, nxt_scr):
        @pl.when(pl.program_id(0) == 0)
        def _():
            nxt_scr[...] = jnp.zeros_like(nxt_scr)
            dw_ref[...] = jnp.zeros_like(dw_ref)
            db_ref[...] = jnp.zeros_like(db_ref)

        a, v = y_ref[:, :D_FF], y_ref[:, D_FF:]
        d = d_ref[...]
        dy = jnp.concatenate([d * v * _dsilu(a), d * _silu(a)], axis=1)
        dx, dw = _causal_conv_bwd(dy, nxt_scr[...], x_ref[...], w_ref[...])
        dx_ref[...] = dx.astype(BF16)
        nxt_scr[...] = dy[:8]
        dw_ref[...] += dw
        db_ref[...] += _colsum(dy)

    return pl.pallas_call(
        body, name="conv_f_bwd", grid=(nt,),
        in_specs=[_rows(tm, D_FF, nt, True), _rows(tm, c, nt, True), _rows(tm, c, nt, True), _full(cw.shape)],
        out_specs=[_rows(tm, c, nt, True), _full(cw.shape), _full((1, c))],
        out_shape=[jax.ShapeDtypeStruct((t, c), BF16), jax.ShapeDtypeStruct(cw.shape, F32),
                   jax.ShapeDtypeStruct((1, c), F32)],
        scratch_shapes=[pltpu.VMEM((8, c), F32)], compiler_params=_params(1),
    )(dact, y, up, cw)


def _conv_a_bwd(dxs, db, dc, y, xbc, cw, tm=256):
    t, c = xbc.shape
    nt = t // tm

    def body(dxs_ref, db_ref, dc_ref, y_ref, x_ref, w_ref, dx_ref, dw_ref, dbias_ref, nxt_scr):
        @pl.when(pl.program_id(0) == 0)
        def _():
            nxt_scr[...] = jnp.zeros_like(nxt_scr)
            dw_ref[...] = jnp.zeros_like(dw_ref)
            dbias_ref[...] = jnp.zeros_like(dbias_ref)

        dy = jnp.concatenate([dxs_ref[...], db_ref[...], dc_ref[...]], axis=1) * _dsilu(y_ref[...])
        dx, dw = _causal_conv_bwd(dy, nxt_scr[...], x_ref[...], w_ref[...])
        dx_ref[...] = dx.astype(BF16)
        nxt_scr[...] = dy[:8]
        dw_ref[...] += dw
        dbias_ref[...] += _colsum(dy)

    return pl.pallas_call(
        body, name="conv_a_bwd", grid=(nt,),
        in_specs=[_rows(tm, SSD_INNER, nt, True), _rows(tm, SSD_BC, nt, True), _rows(tm, SSD_BC, nt, True),
                  _rows(tm, c, nt, True), _rows(tm, c, nt, True), _full(cw.shape)],
        out_specs=[_rows(tm, c, nt, True), _full(cw.shape), _full((1, c))],
        out_shape=[jax.ShapeDtypeStruct((t, c), BF16), jax.ShapeDtypeStruct(cw.shape, F32),
                   jax.ShapeDtypeStruct((1, c), F32)],
        scratch_shapes=[pltpu.VMEM((8, c), F32)], compiler_params=_params(1),
    )(dxs, db, dc, y, xbc, cw)


def _pad_lanes(v, n=DT_PAD):
    return jnp.pad(v, ((0, 0), (0, n - v.shape[1])))


def _local_step(x, target, w, p, after=None, late_weights=None, on_grad=None):
    dtb, alog, dsk = _pad_lanes(p["dt_bias"]), _pad_lanes(p["a_log"]), _pad_lanes(p["d_skip"])
    bs_t = _pad_lanes(p["b_spatial"].T)
    e_heads = (jnp.arange(SSD_INNER)[:, None] // SSD_HEAD_DIM == jnp.arange(LANES)[None, :]).astype(BF16)
    e_groups = (jnp.arange(SGU_WIDTH)[:, None] // LANES == jnp.arange(LANES)[None, :]).astype(BF16)

    n1 = _norm_fwd(x, p["norm1_w"], "norm1_fwd", after=after)
    z = _mm(n1, w["z"], "nt", "proj_z")
    xbc = _mm(n1, w["xbc"], "nt", "proj_xbc")
    dtr = _mm(n1, w["dt"], "nt", "proj_dt")
    uv = _mm(n1, w["uv"], "nt", "proj_uv")
    gates = _mm(n1, w["gates"], "nt", "proj_gates")
    xc, conv_a_out = _conv_a_fwd(xbc, w["conv_a"], p["conv_a_b"])
    y, ya, sprev = _ssd_fwd(xc, dtr, z, dtb, alog, dsk, p["ssd_norm_w"])
    yb = _sgu_fwd(uv, p["uv_b"], p["v_ln_w"], p["v_ln_b"], p["w_spatial"], bs_t)
    if late_weights is not None:
        w = {**w, **late_weights(ya, yb)}
    pa = _mm(ya, w["branch_a"], "nn", "branch_a")
    pb = _mm(yb, w["branch_b"], "nn", "branch_b")
    mix = _merge_fwd(gates, pa, pb, p["b_gate"])
    wide = [(D_MODEL, F32), (D_MODEL, BF16)]
    h1, n2 = _mm_rows(mix, w["out"], "nn", "out_proj", _residual_norm, rows=[x], fulls=[p["norm2_w"]], row_outs=wide)
    up = _mm(n2, w["up"], "nt", "up_proj")
    act, conv_f_out = _conv_f_fwd(up, w["conv_f"], p["conv_f_b"])
    dh2, dh2b, loss, g_final = _mm_rows(
        act, w["down"], "nn", "down_proj", _loss_and_grad, rows=[h1, target], fulls=[p["final_norm_w"]],
        row_outs=wide, acc_outs=[(8, LANES), (1, D_MODEL)])

    on_grad = on_grad or (lambda name, grads: None)
    g = {"final_norm_w": g_final}
    g["down"] = _wgrad(act, dh2b, "down_wgrad")
    tok = on_grad("w_down", g)
    dact = _mm(dh2b, w["down"], "nt", "down_dgrad", after=tok)
    dup, g["conv_f"], g["conv_f_b"] = _conv_f_bwd(dact, conv_f_out, up, w["conv_f"])
    g["up"] = _wgrad(dup, n2, "up_wgrad")
    tok = on_grad("w_up", g)
    dh1, dh1b, g["norm2_w"] = _mm_rows(
        dup, w["up"], "nn", "up_dgrad", _norm_backward, rows=[h1, dh2], fulls=[p["norm2_w"]], row_outs=wide,
        acc_outs=[(1, D_MODEL)], after=tok)
    g["out"] = _wgrad(mix, dh1b, "out_wgrad")
    tok = on_grad("w_out", g)
    dmix = _mm(dh1b, w["out"], "nt", "out_dgrad", after=tok)
    dgates, dpa, dpb, g["b_gate"] = _merge_bwd(dmix, gates, pa, pb, p["b_gate"])
    g["branch_a"] = _wgrad(ya, dpa, "branch_a_wgrad")
    g["branch_b"] = _wgrad(yb, dpb, "branch_b_wgrad")
    tok = on_grad("w_branch", g)
    dya = _mm(dpa, w["branch_a"], "nt", "branch_a_dgrad", after=tok)
    dyb = _mm(dpb, w["branch_b"], "nt", "branch_b_dgrad", after=tok)
    duv, g["uv_b"], g["v_ln_w"], g["v_ln_b"], g["w_spatial"], dbs_t = _sgu_bwd(
        dyb, uv, p["uv_b"], p["v_ln_w"], p["v_ln_b"], p["w_spatial"], bs_t, e_groups)
    g["b_spatial"] = dbs_t[:, :SGU_GROUPS].T
    dz, dxs, db, dc, ddtr, g["ssd_norm_w"], ddtb, dalog, ddsk = _ssd_bwd(
        dya, y, z, xc, dtr, sprev, dtb, alog, dsk, p["ssd_norm_w"], e_heads)
    g["dt_bias"], g["a_log"], g["d_skip"] = ddtb, dalog, ddsk
    dxbc, g["conv_a"], g["conv_a_b"] = _conv_a_bwd(dxs, db, dc, conv_a_out, xbc, w["conv_a"])
    ddtrb = ddtr.astype(BF16)
    for name, d in (("z", dz), ("xbc", dxbc), ("dt", ddtrb), ("uv", duv), ("gates", dgates)):
        g[name] = _wgrad(d, n1, name + "_wgrad")
    tok = on_grad("w_in", g)
    dn1 = _mm(dz, w["z"], "nn", "z_dgrad", after=tok)
    dn1 = _mm(dxbc, w["xbc"], "nn", "xbc_dgrad", acc=dn1)
    dn1 = _mm(ddtrb, w["dt"], "nn", "dt_dgrad", acc=dn1)
    dn1 = _mm(duv, w["uv"], "nn", "uv_dgrad", acc=dn1)
    gx, g["norm1_w"] = _mm_rows(
        dgates, w["gates"], "nn", "gates_dgrad",
        lambda r, so_far, h, dres, w_: tuple(t[:1] for t in _norm_backward(r + so_far, h, dres, w_)),
        rows=[dn1, x, dh1], fulls=[p["norm1_w"]], row_outs=wide[:1], acc_outs=[(1, D_MODEL)])
    return loss, gx, g


def _place():
    return lax.axis_index("x"), lax.axis_index("y"), lax.axis_index("c")


def _other_chips(x, y):
    return [(1 - x, y), (x, 1 - y), (1 - x, 1 - y)]


def _all_gather(shards, name):
    n = len(shards)

    def body(*refs):
        ins, outs = refs[:n], refs[n:2 * n]
        send_sems, recv_sems, local_sems = refs[2 * n:]
        x, y, c = _place()
        me, sibling = (x, y, c), (x, y, 1 - c)
        chips = _other_chips(x, y)

        def copy(a, k, block, to, src=None):
            slot = outs[a].at[4 * block[0] + 2 * block[1] + block[2]]
            return pltpu.make_async_remote_copy(
                src_ref=slot if src is None else src, dst_ref=slot, send_sem=send_sems.at[7 * a + k],
                recv_sem=recv_sems.at[7 * a + k], device_id=to, device_id_type=MESH)

        started = []
        for a in range(n):
            mine = pltpu.make_async_copy(ins[a], outs[a].at[4 * x + 2 * y + c], local_sems.at[a])
            mine.start()
            started.append(mine)
        sends = []
        for a in range(n):
            sends.append(copy(a, 0, me, sibling, src=ins[a]))
            sends += [copy(a, 1 + j, me, (*chip, c), src=ins[a]) for j, chip in enumerate(chips)]
        for cp in sends:
            cp.start()
        for a in range(n):
            for j, chip in enumerate(chips):
                copy(a, 1 + j, (*chip, c), me).wait_recv()
                fwd = copy(a, 4 + j, (*chip, c), sibling)
                fwd.start()
                sends.append(fwd)
        for a in range(n):
            copy(a, 0, sibling, me).wait_recv()
            for j, chip in enumerate(chips):
                copy(a, 4 + j, (*chip, 1 - c), me).wait_recv()
        for cp in sends:
            cp.wait_send()
        for mine in started:
            mine.wait()

    any_spec = pl.BlockSpec(memory_space=pl.ANY)
    return pl.pallas_call(
        body, name=name, in_specs=[any_spec] * n, out_specs=[any_spec] * n,
        out_shape=[jax.ShapeDtypeStruct((N_DEV, *s.shape), s.dtype) for s in shards],
        scratch_shapes=[pltpu.SemaphoreType.DMA((7 * n,)), pltpu.SemaphoreType.DMA((7 * n,)),
                        pltpu.SemaphoreType.DMA((n,))],
    )(*shards)


HBM_SPEC = pl.BlockSpec(memory_space=pltpu.HBM)
SEM_SPEC = pl.BlockSpec(memory_space=pltpu.SEMAPHORE)
ANY_SPEC = pl.BlockSpec(memory_space=pl.ANY)
DATAFLOW = pltpu.SideEffectType.DATAFLOW_SIDE_EFFECTING
N_PEERS = N_DEV - 1


def _peers(x, y, c):
    out = []
    for r in range(1, N_DEV):
        fx, fy, fc = r >> 2 & 1, r >> 1 & 1, r & 1
        out.append(((1 - x) if fx else x, (1 - y) if fy else y, (1 - c) if fc else c))
    return out


def _gather_copies(srcs, lands, send_sems, recv_sems, sending):
    x, y, c = _place()
    copies = []
    for a, (src, land) in enumerate(zip(srcs, lands)):
        for j, (px, py, pc) in enumerate(_peers(x, y, c)):
            slot = 4 * x + 2 * y + c if sending else 4 * px + 2 * py + pc
            copies.append(pltpu.make_async_remote_copy(
                src_ref=src, dst_ref=land.at[slot], send_sem=send_sems.at[N_PEERS * a + j],
                recv_sem=recv_sems.at[N_PEERS * a + j], device_id=(px, py, pc), device_id_type=MESH))
    return copies


def _gather_start(shards, after, name):
    n = len(shards)

    def body(*refs):
        srcs, lands = refs[:n], refs[n:2 * n]
        send_sems, recv_sems = refs[2 * n + 1:2 * n + 3]
        token = refs[-1]
        for cp in _gather_copies(srcs, lands, send_sems, recv_sems, sending=True):
            cp.start()
        token[...] = jnp.zeros_like(token)

    lands = [lax.empty((N_DEV, *s.shape), s.dtype) for s in shards]
    hbm = lambda a: pltpu.with_memory_space_constraint(a, pltpu.HBM)
    out = pl.pallas_call(
        body, name=name,
        out_shape=(pltpu.SemaphoreType.DMA((N_PEERS * n,)), pltpu.SemaphoreType.DMA((N_PEERS * n,)),
                   *[pltpu.HBM(a.shape, a.dtype) for a in (*shards, *lands)], jax.ShapeDtypeStruct((8, LANES), F32)),
        in_specs=[HBM_SPEC] * (2 * n) + [ANY_SPEC],
        out_specs=(SEM_SPEC, SEM_SPEC, *[HBM_SPEC] * (2 * n), pl.BlockSpec(memory_space=pltpu.VMEM)),
        input_output_aliases={i: 2 + i for i in range(2 * n)},
        compiler_params=pltpu.CompilerParams(has_side_effects=DATAFLOW),
    )(*[hbm(a) for a in (*shards, *lands)], after)
    return out[0], out[1], out[2:2 + n], out[2 + n:2 + 2 * n], out[-1]


def _gather_wait(send_sems, recv_sems, shards, lands, after, name):
    n = len(shards)
    after = tuple(after)

    def body(*refs):
        srcs, lands_ = refs[:n], refs[n:2 * n]
        send, recv = refs[2 * n:2 * n + 2]
        for cp in _gather_copies(srcs, lands_, send, recv, sending=False):
            cp.wait_send()
            cp.wait_recv()

    out = pl.pallas_call(
        body, name=name, out_shape=tuple(pltpu.HBM(a.shape, a.dtype) for a in (*shards, *lands)),
        in_specs=[HBM_SPEC] * (2 * n) + [SEM_SPEC, SEM_SPEC] + [ANY_SPEC] * len(after),
        out_specs=tuple([HBM_SPEC] * (2 * n)), input_output_aliases={i: i for i in range(2 * n)},
        compiler_params=pltpu.CompilerParams(has_side_effects=DATAFLOW),
    )(*shards, *lands, send_sems, recv_sems, *after)
    return out[n:]


def _chip_copies(src, land, send_sems, recv_sems):
    x, y, c = _place()
    return [pltpu.make_async_remote_copy(
        src_ref=src.at[2 * cx + cy], dst_ref=land.at[j], send_sem=send_sems.at[j], recv_sem=recv_sems.at[j],
        device_id=(cx, cy, c), device_id_type=MESH) for j, (cx, cy) in enumerate(_other_chips(x, y))]


def _chips_start(q, name):
    def body(q_ref, land_ref, send_sems, recv_sems, q_thru, land_thru, token):
        for cp in _chip_copies(q_ref, land_ref, send_sems, recv_sems):
            cp.start()
        token[...] = jnp.zeros_like(token)

    land = lax.empty((3, *q.shape[1:]), q.dtype)
    return pl.pallas_call(
        body, name=name,
        out_shape=(pltpu.SemaphoreType.DMA((3,)), pltpu.SemaphoreType.DMA((3,)), pltpu.HBM(q.shape, q.dtype),
                   pltpu.HBM(land.shape, land.dtype), jax.ShapeDtypeStruct((8, LANES), F32)),
        in_specs=[HBM_SPEC, HBM_SPEC],
        out_specs=(SEM_SPEC, SEM_SPEC, HBM_SPEC, HBM_SPEC, pl.BlockSpec(memory_space=pltpu.VMEM)),
        input_output_aliases={0: 2, 1: 3}, compiler_params=pltpu.CompilerParams(has_side_effects=DATAFLOW),
    )(pltpu.with_memory_space_constraint(q, pltpu.HBM), pltpu.with_memory_space_constraint(land, pltpu.HBM))


def _chips_wait(send_sems, recv_sems, q, land, after, name):
    def body(q_ref, land_ref, send, recv, after_ref, q_out, land_out):
        for cp in _chip_copies(q_ref, land_ref, send, recv):
            cp.wait_send()
            cp.wait_recv()

    return pl.pallas_call(
        body, name=name, out_shape=(pltpu.HBM(q.shape, q.dtype), pltpu.HBM(land.shape, land.dtype)),
        in_specs=[HBM_SPEC, HBM_SPEC, SEM_SPEC, SEM_SPEC, ANY_SPEC], out_specs=(HBM_SPEC, HBM_SPEC),
        input_output_aliases={0: 0, 1: 1}, compiler_params=pltpu.CompilerParams(has_side_effects=DATAFLOW),
    )(q, land, send_sems, recv_sems, after)[1]


def _exchange_cores(parts, name):
    n = len(parts)

    def body(*refs):
        ins, outs = refs[:n], refs[n:2 * n]
        send_sems, recv_sems = refs[2 * n:]
        x, y, c = _place()
        copies = []
        for a in range(n):
            for k in range(4):
                copies.append(pltpu.make_async_remote_copy(
                    src_ref=ins[a].at[2 * k + (1 - c)], dst_ref=outs[a].at[k], send_sem=send_sems.at[4 * a + k],
                    recv_sem=recv_sems.at[4 * a + k], device_id=(x, y, 1 - c), device_id_type=MESH))
        for cp in copies:
            cp.start()
        for cp in copies:
            cp.wait()

    any_spec = pl.BlockSpec(memory_space=pl.ANY)
    return pl.pallas_call(
        body, name=name, in_specs=[any_spec] * n, out_specs=[any_spec] * n,
        out_shape=[jax.ShapeDtypeStruct((4, *s.shape[1:]), s.dtype) for s in parts],
        scratch_shapes=[pltpu.SemaphoreType.DMA((4 * n,)), pltpu.SemaphoreType.DMA((4 * n,))],
    )(*parts)


def _chip_sum(part, got, place, name, tr=256):
    _, r, c = part.shape
    tr, tc = _tile2d(r, c, tr)

    def body(place_ref, p_ref, g_ref, q_ref, own_ref):
        s = p_ref[0].astype(F32) + g_ref[0].astype(F32)
        q_ref[0] = s.astype(BF16)

        @pl.when(pl.program_id(2) == place_ref[1])
        def _():
            own_ref[...] = s

    grid_spec = pltpu.PrefetchScalarGridSpec(
        num_scalar_prefetch=1, grid=(r // tr, c // tc, 4),
        in_specs=[pl.BlockSpec((1, tr, tc), lambda i, j, k, pr: (2 * k + pr[0], i, j)),
                  pl.BlockSpec((1, tr, tc), lambda i, j, k, pr: (k, i, j))],
        out_specs=[pl.BlockSpec((1, tr, tc), lambda i, j, k, pr: (k, i, j)),
                   pl.BlockSpec((tr, tc), lambda i, j, k, pr: (i, j))])
    return pl.pallas_call(
        body, name=name, grid_spec=grid_spec,
        out_shape=[jax.ShapeDtypeStruct((4, r, c), BF16), jax.ShapeDtypeStruct((r, c), F32)],
        compiler_params=_params(3),
    )(place, part, got)


def _adamw(w, g, m, v):
    m = ADAM_B1 * m + (1.0 - ADAM_B1) * g
    v = ADAM_B2 * v + (1.0 - ADAM_B2) * jnp.square(g)
    m_hat = m / (1.0 - ADAM_B1 ** ADAM_STEP)
    v_hat = v / (1.0 - ADAM_B2 ** ADAM_STEP)
    return -ADAM_LR * (m_hat / (jnp.sqrt(v_hat) + ADAM_EPS) + ADAM_WD * w), m, v


def _sum_adamw(own, got, w, m, v, name, tr=256):
    r, c = own.shape
    if w.ndim == 3:
        tr, tc = r, 4 * LANES
        wblk = pl.BlockSpec((tr, 1, tc), lambda i, j: (i, 0, j))
    else:
        tr, tc = _tile2d(r, c, tr)
        wblk = pl.BlockSpec((tr, tc), lambda i, j: (i, j))

    def body(own_ref, got_ref, w_ref, m_ref, v_ref, g_ref, d_ref, nm_ref, nv_ref):
        g = own_ref[...]
        for j in range(3):
            g = g + got_ref[j].astype(F32)
        two_d = lambda ref: ref[...].reshape(tr, tc)
        delta, nm, nv = _adamw(two_d(w_ref), g, two_d(m_ref), two_d(v_ref))
        for ref, val in ((g_ref, g), (d_ref, delta), (nm_ref, nm), (nv_ref, nv)):
            ref[...] = val.reshape(ref.shape)

    blk = pl.BlockSpec((tr, tc), lambda i, j: (i, j))
    return pl.pallas_call(
        body, name=name, grid=(r // tr, c // tc),
        in_specs=[blk, pl.BlockSpec((3, tr, tc), lambda i, j: (0, i, j)), wblk, wblk, wblk], out_specs=[wblk] * 4,
        out_shape=[jax.ShapeDtypeStruct(w.shape, F32)] * 4, compiler_params=_params(2),
    )(own, got, w, m, v)


VECTORS = ["norm1_w", "b_gate", "conv_a_b", "dt_bias", "a_log", "d_skip", "ssd_norm_w", "uv_b", "v_ln_w", "v_ln_b",
           "norm2_w", "conv_f_b", "final_norm_w"]
SMALL_ORDER = VECTORS + ["w_spatial", "b_spatial", "conv_a_w", "conv_f_w"]


def _small_adamw(gathered, w, m, v):
    sizes = {n: w[n].shape[1] for n in VECTORS}
    offs, off = {}, 0
    for n in VECTORS:
        offs[n] = off
        off += -(-sizes[n] // LANES) * LANES
    loss_off = off
    k = len(SMALL_ORDER)

    def body(*refs):
        row_ref, ws_ref, bs_ref, ca_ref, cf_ref = refs[:5]
        w_refs, m_refs, v_refs = (dict(zip(SMALL_ORDER, refs[5 + i * k:5 + (i + 1) * k])) for i in range(3))
        outs = refs[5 + 3 * k:]
        x, y, c = _place()
        dev = 4 * x + 2 * y + c

        def total(ref):
            s = ref[0]
            for d in range(1, N_DEV):
                s = s + ref[d]
            return s

        row = total(row_ref)
        grads = {n: row[:, offs[n]:offs[n] + sizes[n]] for n in VECTORS}
        grads["w_spatial"], grads["b_spatial"] = total(ws_ref), total(bs_ref)
        for n, ref in (("conv_a_w", ca_ref), ("conv_f_w", cf_ref)):
            whole, cols = total(ref), w_refs[n].shape[1]
            mine = whole[:, :cols]
            for d in range(1, N_DEV):
                mine = jnp.where(dev == d, whole[:, d * cols:(d + 1) * cols], mine)
            grads[n] = mine
        for i, n in enumerate(SMALL_ORDER):
            outs[4 * i][...] = grads[n]
            outs[4 * i + 1][...], outs[4 * i + 2][...], outs[4 * i + 3][...] = _adamw(
                w_refs[n][...], grads[n], m_refs[n][...], v_refs[n][...])
        outs[4 * k][...] = row[:, loss_off:loss_off + LANES]

    out = pl.pallas_call(
        body, name="adamw_small",
        out_shape=[jax.ShapeDtypeStruct(w[n].shape, F32) for n in SMALL_ORDER for _ in range(4)]
        + [jax.ShapeDtypeStruct((1, LANES), F32)],
        compiler_params=_params(0),
    )(*gathered, *[t[n] for t in (w, m, v) for n in SMALL_ORDER])
    return [dict(zip(SMALL_ORDER, out[j:4 * k:4])) for j in range(4)] + [out[4 * k]]


SMALL = ["norm1_w", "b_gate", "conv_a_b", "dt_bias", "a_log", "d_skip", "ssd_norm_w", "uv_b", "v_ln_w", "v_ln_b",
         "w_spatial", "b_spatial", "norm2_w", "conv_f_b", "final_norm_w"]
BIG = ["w_in", "w_branch", "w_out", "w_up", "w_down"]
TRANSPOSED = ("w_in", "w_up")
WEIGHTS = ["norm1_w", "w_in", "b_gate", "conv_a_w", "conv_a_b", "dt_bias", "a_log", "d_skip", "ssd_norm_w", "uv_b",
           "v_ln_w", "v_ln_b", "w_spatial", "b_spatial", "w_branch", "w_out", "norm2_w", "w_up", "conv_f_w",
           "conv_f_b", "w_down", "final_norm_w"]
IN_SPLITS = [("z", 0, 2048), ("xbc", 2048, 5120), ("dt", 5120, 5152), ("uv", 5152, 7200), ("gates", 7200, 9248)]


def _columns_from_devices(a):
    return a.transpose(1, 0, 2).reshape(a.shape[1], -1)


def kernel(x, norm1_w, w_in, b_gate, conv_a_w, conv_a_b, dt_bias, a_log, d_skip, ssd_norm_w, uv_b, v_ln_w, v_ln_b, w_spatial, b_spatial, w_branch, w_out, norm2_w, w_up, conv_f_w, conv_f_b, w_down, final_norm_w, loss_target, m_norm1_w, m_w_in, m_b_gate, m_conv_a_w, m_conv_a_b, m_dt_bias, m_a_log, m_d_skip, m_ssd_norm_w, m_uv_b, m_v_ln_w, m_v_ln_b, m_w_spatial, m_b_spatial, m_w_branch, m_w_out, m_norm2_w, m_w_up, m_conv_f_w, m_conv_f_b, m_w_down, m_final_norm_w, v_norm1_w, v_w_in, v_b_gate, v_conv_a_w, v_conv_a_b, v_dt_bias, v_a_log, v_d_skip, v_ssd_norm_w, v_uv_b, v_v_ln_w, v_v_ln_b, v_w_spatial, v_b_spatial, v_w_branch, v_w_out, v_norm2_w, v_w_up, v_conv_f_w, v_conv_f_b, v_w_down, v_final_norm_w):
    args = dict(locals())
    wts = {n: args[n] for n in WEIGHTS}
    mom = {n: args["m_" + n] for n in WEIGHTS}
    var = {n: args["v_" + n] for n in WEIGHTS}
    cx, cy, cc = _place()
    dev = 4 * cx + 2 * cy + cc
    place = jnp.stack([cc, 2 * cx + cy]).astype(jnp.int32)

    def shard2d(n, a):
        return a[0].T if n in TRANSPOSED else a[0]

    def unshard(n, b):
        return (b.T if n in TRANSPOSED else b)[None]

    g_in, g_conv_a, g_conv_f = _all_gather(
        [shard2d("w_in", w_in).astype(BF16), conv_a_w[0], conv_f_w[0]], "gather_w_in")
    late = [shard2d(n, wts[n]).astype(BF16) for n in BIG[1:]]
    send_sems, recv_sems, late, lands, token = _gather_start(late, g_in, "gather_late_start")
    w_in_rows = g_in.reshape(-1, D_MODEL)
    w = {name: w_in_rows[lo:hi] for name, lo, hi in IN_SPLITS}
    w["dt"] = jnp.pad(w["dt"], ((0, DT_PAD - SSD_HEADS), (0, 0)))
    w["conv_a"] = _columns_from_devices(g_conv_a)
    w["conv_f"] = _columns_from_devices(g_conv_f)

    def late_weights(*after):
        got = _gather_wait(send_sems, recv_sems, late, lands, after, "gather_late_wait")
        g_branch, g_out, g_up, g_down = [lax.dynamic_update_index_in_dim(land, mine, dev, 0).reshape(-1, D_MODEL)
                                         for land, mine in zip(got, late)]
        return {"branch_a": g_branch[:SSD_INNER], "branch_b": g_branch[SSD_INNER:], "out": g_out, "up": g_up,
                "down": g_down}

    in_flight = {}

    def on_grad(n, g):
        part = {"w_in": lambda: jnp.concatenate([g[name][:hi - lo] for name, lo, hi in IN_SPLITS], axis=0),
                "w_branch": lambda: jnp.concatenate([g["branch_a"], g["branch_b"]], axis=0),
                "w_out": lambda: g["out"], "w_up": lambda: g["up"], "w_down": lambda: g["down"]}[n]()
        part = part.reshape(N_DEV, -1, D_MODEL)
        from_core, = _exchange_cores([part], f"to_other_core_{n}")
        q, own = _chip_sum(part, from_core, place, f"chip_sum_{n}")
        send, recv, q, land, tok = _chips_start(q, f"to_other_chips_start_{n}")
        in_flight[n] = (own, send, recv, q, land)
        return tok

    p = {n: wts[n][0] if wts[n].ndim > 2 else wts[n].reshape(1, -1) for n in SMALL}
    loss, gx, g = _local_step(x[0], loss_target[0], w, p, after=token, late_weights=late_weights, on_grad=on_grad)

    small_g = [jnp.concatenate([g[n] for n in VECTORS] + [loss[:1]], axis=1), g["w_spatial"], g["b_spatial"],
               g["conv_a"], g["conv_f"]]
    s_send, s_recv, s_mine, s_land, _ = _gather_start(small_g, gx, "gather_small_start")

    grads, delta, new_m, new_v = {}, {}, {}, {}

    def big_adamw(n, after):
        own, send, recv, q, land = in_flight[n]
        got = _chips_wait(send, recv, q, land, after, f"to_other_chips_wait_{n}")
        view = (lambda a: a.transpose(2, 0, 1)) if n == "w_in" else (lambda a: shard2d(n, a))
        back = (lambda b: b.transpose(1, 2, 0)) if n == "w_in" else (lambda b: unshard(n, b))
        out = _sum_adamw(own, got, view(wts[n]), view(mom[n]), view(var[n]), f"adamw_{n}")
        grads[n], delta[n], new_m[n], new_v[n] = [back(o) for o in out]
        return out[1]

    after = gx
    for n in ("w_down", "w_up", "w_out", "w_branch", "w_in"):
        after = big_adamw(n, after)
    gathered = _gather_wait(s_send, s_recv, s_mine, s_land, [after], "gather_small_wait")
    gathered = [lax.dynamic_update_index_in_dim(land, mine, dev, 0) for land, mine in zip(gathered, s_mine)]
    small = [{n: t[n][0] if t[n].ndim > 2 else t[n].reshape(1, -1) for n in SMALL_ORDER} for t in (wts, mom, var)]
    *outs, loss = _small_adamw(gathered, *small)
    for tgt, out in zip((grads, delta, new_m, new_v), outs):
        tgt.update({n: out[n].reshape(wts[n].shape) for n in SMALL_ORDER})
    loss = loss[0, 0]

    return (loss, gx[None], *[grads[n] for n in WEIGHTS], *[delta[n] for n in WEIGHTS],
            *[new_m[n] for n in WEIGHTS], *[new_v[n] for n in WEIGHTS])
```

```python
import functools

import jax
import jax.numpy as jnp
from jax import lax
from jax.experimental import pallas as pl
from jax.experimental.pallas import tpu as pltpu

F32, BF16 = jnp.float32, jnp.bfloat16
HIGHEST = lax.Precision.HIGHEST

D_MODEL = 1024
SSD_INNER = 2048
SSD_HEAD_DIM = 64
SSD_HEADS = 32
SSD_GROUPS = 4
SSD_STATE = 128
SSD_BC = SSD_GROUPS * SSD_STATE
SSD_XBC = SSD_INNER + 2 * SSD_BC
SSD_CONV = 4
CHUNK = 128
N_PAIRS = SSD_HEADS // 2
PAIRS_PER_GROUP = N_PAIRS // SSD_GROUPS
SGU_WIDTH = 1024
SGU_GROUPS = 8
D_FF = 2816
FFN_CONV = 3
NORM_EPS = 1e-6
LN_EPS = 1e-5
LANES = 128
DT_PAD = LANES

ADAM_LR, ADAM_B1, ADAM_B2, ADAM_EPS, ADAM_WD, ADAM_STEP = 0.001, 0.9, 0.999, 1e-08, 0.01, 10

N_DEV = 8
VMEM_LIMIT = 56 * 1024 * 1024
MESH = pl.DeviceIdType.MESH


def _params(n_grid, **kw):
    sem = dict(dimension_semantics=("arbitrary",) * n_grid) if n_grid else {}
    return pltpu.CompilerParams(vmem_limit_bytes=VMEM_LIMIT, **sem, **kw)


def _tile(n, pref):
    t = (min(pref, n) // LANES) * LANES
    while n % t:
        t -= LANES
    return t


def _row_tile(r, pref):
    for t in range(min(pref, r) // 16 * 16, 0, -16):
        if r % t == 0:
            return t
    return r


def _tile2d(r, c, rows):
    if r % 16 == 0:
        return _row_tile(r, rows), c
    return r, _tile(c, 2 * LANES)


def _rows(tm, n, nt=None, rev=False, col=0):
    if rev:
        return pl.BlockSpec((tm, n), lambda i: (nt - 1 - i, col))
    return pl.BlockSpec((tm, n), lambda i: (i, col))


def _halo(tm, n, nt=None, rev=False):
    per = tm // 8
    if rev:
        return pl.BlockSpec((8, n), lambda i: (jnp.maximum((nt - 1 - i) * per - 1, 0), 0))
    return pl.BlockSpec((8, n), lambda i: (jnp.maximum(i * per - 1, 0), 0))


def _full(shape):
    nd = len(shape)
    return pl.BlockSpec(shape, lambda *_: (0,) * nd)


def _rms(x, w, eps=NORM_EPS):
    return x * lax.rsqrt(jnp.mean(x * x, axis=-1, keepdims=True) + eps) * w


def _layer_norm(x, w, b):
    mu = jnp.mean(x, axis=-1, keepdims=True)
    var = jnp.mean(jnp.square(x - mu), axis=-1, keepdims=True)
    return (x - mu) * lax.rsqrt(var + LN_EPS) * w + b


def _sigmoid(x):
    return 1.0 / (1.0 + jnp.exp(-x))


def _silu(x):
    return x * _sigmoid(x)


def _dsilu(x):
    s = _sigmoid(x)
    return s * (1.0 + x * (1.0 - s))


def _softplus(x):
    return jnp.maximum(x, 0.0) + jnp.log(1.0 + jnp.exp(-jnp.abs(x)))


def _gelu(x):
    return jax.nn.gelu(x)


def _dot(a, b):
    return jnp.dot(a, b, preferred_element_type=F32)


def _dot_nt(a, b):
    return lax.dot_general(a, b, (((1,), (1,)), ((), ())), preferred_element_type=F32)


def _dot_tn(a, b):
    return lax.dot_general(a, b, (((0,), (0,)), ((), ())), preferred_element_type=F32)


def _dot_split(p, e):
    hi = p.astype(BF16)
    lo = (p - hi.astype(F32)).astype(BF16)
    return _dot(hi, e) + _dot(lo, e)


def _colsum(x):
    return jnp.sum(x, axis=0, keepdims=True)


def _shift_down(x, halo, j):
    xs = pltpu.roll(x, j, 0)
    hs = pltpu.roll(halo, j, 0)
    r8 = lax.broadcasted_iota(jnp.int32, hs.shape, 0)
    return jnp.concatenate([jnp.where(r8 < j, hs, xs[:8]), xs[8:]], axis=0)


def _shift_up(x, nxt, j):
    n = x.shape[0]
    xs = pltpu.roll(x, n - j, 0)
    ns = pltpu.roll(nxt, 8 - j, 0)
    r8 = lax.broadcasted_iota(jnp.int32, ns.shape, 0)
    return jnp.concatenate([xs[:n - 8], jnp.where(r8 >= 8 - j, ns, xs[n - 8:])], axis=0)


def _causal_conv(x, halo, w, b):
    k = w.shape[0]
    y = b + w[k - 1:k, :] * x
    for j in range(1, k):
        y = y + w[k - 1 - j:k - j, :] * _shift_down(x, halo, j)
    return y


def _causal_conv_bwd(dy, nxt, x, w):
    k = w.shape[0]
    dx = w[k - 1:k, :] * dy
    dw = [_colsum(dy * x)]
    for j in range(1, k):
        dyj = _shift_up(dy, nxt, j)
        dx = dx + w[k - 1 - j:k - j, :] * dyj
        dw.append(_colsum(dyj * x))
    return dx, jnp.concatenate(dw[::-1], axis=0)


MM_TILE_PREF = 1408
MM_VMEM_BUDGET = 40 * 1024 * 1024


def _mm_tiles(m, n, k, out_bytes):
    tm, tn = _tile(m, MM_TILE_PREF), _tile(n, MM_TILE_PREF)
    need = lambda tm, tn: 2 * (2 * k * (tm + tn) + out_bytes * tm * tn)
    while need(tm, tn) > MM_VMEM_BUDGET:
        if tn >= tm and tn > LANES:
            tn = _tile(n, tn - LANES)
        else:
            tm = _tile(m, tm - LANES)
    return tm, tn


def _mm(a, b, dims, name, acc=None, out_dtype=F32, after=None):
    if dims == "tn":
        k, m = a.shape
    else:
        m, k = a.shape
    n = b.shape[0] if dims == "nt" else b.shape[1]
    tm, tn = _mm_tiles(m, n, k, 4 * (2 if acc is not None else 1))
    a_spec = pl.BlockSpec((k, tm), lambda j, i: (0, i)) if dims == "tn" else pl.BlockSpec((tm, k), lambda j, i: (i, 0))
    b_spec = pl.BlockSpec((tn, k), lambda j, i: (j, 0)) if dims == "nt" else pl.BlockSpec((k, tn), lambda j, i: (0, j))
    o_spec = pl.BlockSpec((tm, tn), lambda j, i: (i, j))
    dot = {"nn": _dot, "nt": _dot_nt, "tn": _dot_tn}[dims]

    def body(a_ref, b_ref, *rest):
        r = dot(a_ref[...], b_ref[...])
        if acc is not None:
            r = r + rest[0][...]
        rest[-1][...] = r.astype(out_dtype)

    ins, specs = [a, b], [a_spec, b_spec]
    if acc is not None:
        ins.append(acc)
        specs.append(o_spec)
    if after is not None:
        ins.append(after)
        specs.append(pl.BlockSpec(memory_space=pl.ANY))
    return pl.pallas_call(
        body, name=name, grid=(n // tn, m // tm), in_specs=specs, out_specs=o_spec,
        out_shape=jax.ShapeDtypeStruct((m, n), out_dtype), compiler_params=_params(2),
    )(*ins)


def _mm_rows(a, b, dims, name, fn, rows=(), fulls=(), row_outs=(), acc_outs=(), after=None):
    m, k = a.shape
    n = b.shape[0] if dims == "nt" else b.shape[1]
    per_row = 2 * k + 8 * n + sum(4 * r.shape[1] for r in rows) + sum(c * jnp.dtype(d).itemsize for c, d in row_outs)
    tm = _tile(m, 1024)
    while 2 * tm * per_row + 4 * k * n > MM_VMEM_BUDGET:
        tm = _tile(m, tm - LANES)
    dot = _dot_nt if dims == "nt" else _dot
    n_in = 2 + len(rows) + len(fulls) + (after is not None)

    def body(*refs):
        ins, outs = refs[:n_in], refs[n_in:]
        row_refs, acc_refs = outs[:len(row_outs)], outs[len(row_outs):]

        @pl.when(pl.program_id(0) == 0)
        def _():
            for r in acc_refs:
                r[...] = jnp.zeros_like(r)

        new_rows, incs = fn(dot(ins[0][...], ins[1][...]), *[r[...] for r in ins[2:2 + len(rows) + len(fulls)]])
        for r, val in zip(row_refs, new_rows):
            r[...] = val.astype(r.dtype)
        for r, inc in zip(acc_refs, incs):
            r[...] += inc

    b_spec = _full(b.shape)
    extra, extra_specs = ([after], [pl.BlockSpec(memory_space=pl.ANY)]) if after is not None else ([], [])
    return pl.pallas_call(
        body, name=name, grid=(m // tm,),
        in_specs=[_rows(tm, k), b_spec] + [_rows(tm, r.shape[1]) for r in rows] + [_full(f.shape) for f in fulls]
        + extra_specs,
        out_specs=[_rows(tm, c) for c, _ in row_outs] + [_full(s) for s in acc_outs],
        out_shape=[jax.ShapeDtypeStruct((m, c), d) for c, d in row_outs]
        + [jax.ShapeDtypeStruct(s, F32) for s in acc_outs],
        compiler_params=_params(1),
    )(a, b, *rows, *fulls, *extra)


def _residual_norm(o, x, w):
    h = x + o
    return (h, _rms(h, w)), ()


def _norm_backward(dn, h, dres, w):
    _, vjp = jax.vjp(_rms, h, w)
    dh, dw = vjp(dn)
    dh = dh + dres
    return (dh, dh), (dw,)


def _loss_and_grad(dn, h1, target, w):
    yf, vjp = jax.vjp(_rms, h1 + dn, w)
    err = yf - target
    loss = 0.5 * jnp.sum(jnp.mean(err * err, axis=-1, keepdims=True))
    dh, dw = vjp(err * (1.0 / err.shape[-1]))
    return (dh, dh), (jnp.full((8, LANES), loss, F32), dw)


def _wgrad(a, d, name, after=None):
    return _mm(a, d, "tn", name, out_dtype=BF16, after=after)


def _norm_fwd(x, w, name, after=None, tm=512):
    t, d = x.shape

    def body(x_ref, w_ref, *rest):
        rest[-1][...] = _rms(x_ref[...], w_ref[...]).astype(BF16)

    extra, extra_specs = ([after], [_full(after.shape)]) if after is not None else ([], [])
    return pl.pallas_call(
        body, name=name, grid=(t // tm,), in_specs=[_rows(tm, d), _full((1, d))] + extra_specs,
        out_specs=_rows(tm, d), out_shape=jax.ShapeDtypeStruct((t, d), BF16), compiler_params=_params(1),
    )(x, w, *extra)


def _conv_a_fwd(xbc, cw, cb, tm=256):
    t, c = xbc.shape

    def body(x_ref, h_ref, w_ref, b_ref, o_ref, y_ref):
        halo = jnp.where(pl.program_id(0) > 0, h_ref[...], 0.0)
        y = _causal_conv(x_ref[...], halo, w_ref[...], b_ref[...])
        y_ref[...] = y
        o_ref[...] = _silu(y)

    return pl.pallas_call(
        body, name="conv_a_fwd", grid=(t // tm,),
        in_specs=[_rows(tm, c), _halo(tm, c), _full(cw.shape), _full((1, c))], out_specs=[_rows(tm, c)] * 2,
        out_shape=[jax.ShapeDtypeStruct((t, c), F32)] * 2, compiler_params=_params(1),
    )(xbc, xbc, cw, cb)


def _ssd_common(dtr, dtb, alog, e_t):
    row = lax.broadcasted_iota(jnp.int32, (CHUNK, CHUNK), 0)
    col = lax.broadcasted_iota(jnp.int32, (CHUNK, CHUNK), 1)
    causal = row >= col
    dt = _softplus(dtr + dtb)
    a = -jnp.exp(alog)
    acum = jnp.dot(causal.astype(F32), dt * a, precision=HIGHEST, preferred_element_type=F32)
    spread = lambda v: _dot_split(v, e_t)
    return dict(dt=dt, a=a, acum=acum, acum_t=acum.T, causal=causal, row=row, col=col, lane_lo=col < SSD_HEAD_DIM,
                dt_x=spread(dt), ecol_x=spread(jnp.exp(acum)), dsr_x=spread(jnp.exp(acum[CHUNK - 1:CHUNK, :] - acum)))


def _head_decay(c, h, transposed=False):
    d = c["acum"][:, h:h + 1] - c["acum_t"][h:h + 1, :]
    if transposed:
        return jnp.exp(jnp.where(c["row"] <= c["col"], -d, -jnp.inf))
    return jnp.exp(jnp.where(c["causal"], d, -jnp.inf))


def _ssd_fwd(xc, dtr, z, dtb, alog, dsk, nw, e_t):
    t = xc.shape[0]
    nc = t // CHUNK

    def body(xs_ref, b_ref, c_ref, dtr_ref, z_ref, dtb_ref, alog_ref, dsk_ref, nw_ref, et_ref,
             y_ref, ya_ref, sp_ref, s_scr):
        @pl.when(pl.program_id(0) == 0)
        def _():
            s_scr[...] = jnp.zeros_like(s_scr)

        c = _ssd_common(dtr_ref[...], dtb_ref[...], alog_ref[...], et_ref[...])
        lane_lo = c["lane_lo"]
        dsk = dsk_ref[...]
        for g in range(SSD_GROUPS):
            gs = slice(g * SSD_STATE, (g + 1) * SSD_STATE)
            bg_t, cg = b_ref[:, gs].T.astype(BF16), c_ref[:, gs].astype(BF16)
            cb = _dot(cg, bg_t)
            for pp in range(PAIRS_PER_GROUP):
                j = g * PAIRS_PER_GROUP + pp
                ps = slice(j * LANES, (j + 1) * LANES)
                x = xs_ref[:, ps]
                ecol, dsr = c["ecol_x"][:, ps], c["dsr_x"][:, ps]
                xdt = x * c["dt_x"][:, ps]
                xb = xdt.astype(BF16)
                zero = jnp.zeros_like(xb)
                yd = (_dot((cb * _head_decay(c, 2 * j)).astype(BF16), jnp.where(lane_lo, xb, zero))
                      + _dot((cb * _head_decay(c, 2 * j + 1)).astype(BF16), jnp.where(lane_lo, zero, xb)))
                sp = s_scr[j]
                yo = ecol * _dot(cg, sp.astype(BF16))
                st = _dot(bg_t, (xdt * dsr).astype(BF16))
                sp_ref[0, j] = sp
                s_scr[j] = ecol[CHUNK - 1:CHUNK] * sp + st
                dskp = jnp.where(lane_lo[0:1], dsk[:, 2 * j:2 * j + 1], dsk[:, 2 * j + 1:2 * j + 2])
                y_ref[:, ps] = yd + yo + dskp * x
        ya_ref[...] = _rms(y_ref[...] * _silu(z_ref[...]), nw_ref[...]).astype(BF16)

    ck = lambda n, col=0: pl.BlockSpec((CHUNK, n), lambda c: (c, col))
    return pl.pallas_call(
        body, name="ssd_fwd", grid=(nc,),
        in_specs=[ck(SSD_INNER), ck(SSD_BC, SSD_INNER // SSD_BC), ck(SSD_BC, SSD_INNER // SSD_BC + 1), ck(DT_PAD),
                  ck(SSD_INNER), _full((1, DT_PAD)), _full((1, DT_PAD)), _full((1, DT_PAD)), _full((1, SSD_INNER)),
                  _full(e_t.shape)],
        out_specs=[ck(SSD_INNER), ck(SSD_INNER),
                   pl.BlockSpec((1, N_PAIRS, SSD_STATE, LANES), lambda c: (c, 0, 0, 0))],
        out_shape=[jax.ShapeDtypeStruct((t, SSD_INNER), F32), jax.ShapeDtypeStruct((t, SSD_INNER), BF16),
                   jax.ShapeDtypeStruct((nc, N_PAIRS, SSD_STATE, LANES), F32)],
        scratch_shapes=[pltpu.VMEM((N_PAIRS, SSD_STATE, LANES), F32)], compiler_params=_params(1),
    )(xc, xc, xc, dtr, z, dtb, alog, dsk, nw, e_t)


def _ssd_bwd(dya, y, z, xc, dtr, sprev, dtb, alog, dsk, nw, e_heads, e_t):
    t = xc.shape[0]
    nc = t // CHUNK

    def body(dya_ref, y_ref, z_ref, xs_ref, b_ref, c_ref, dtr_ref, sp_ref, dtb_ref, alog_ref, dsk_ref, nw_ref, e_ref,
             et_ref, dz_ref, dxs_ref, db_ref, dc_ref, ddtr_ref, dnw_ref, ddtb_ref, dalog_ref, ddsk_ref, ds_scr):
        @pl.when(pl.program_id(0) == 0)
        def _():
            ds_scr[...] = jnp.zeros_like(ds_scr)
            for r in (dnw_ref, ddtb_ref, dalog_ref, ddsk_ref):
                r[...] = jnp.zeros_like(r)

        y = y_ref[...]
        _, gate_vjp = jax.vjp(lambda y_, z_, w_: _rms(y_ * _silu(z_), w_), y, z_ref[...], nw_ref[...])
        dy, dz, dnw = gate_vjp(dya_ref[...])
        dz_ref[...] = dz.astype(BF16)
        dnw_ref[...] += dnw

        dtr = dtr_ref[...]
        c = _ssd_common(dtr, dtb_ref[...], alog_ref[...], et_ref[...])
        dt, a, lane_lo, row, col = c["dt"], c["a"], c["lane_lo"], c["row"], c["col"]
        dsk = dsk_ref[...]
        p_a, p_dt, v_last = [], [], []
        da_cols = jnp.zeros((CHUNK, CHUNK), F32)
        da_rows = jnp.zeros((CHUNK, CHUNK), F32)
        for g in range(SSD_GROUPS):
            gs = slice(g * SSD_STATE, (g + 1) * SSD_STATE)
            bg, cg = b_ref[:, gs].astype(BF16), c_ref[:, gs].astype(BF16)
            bg_t, cg_t = b_ref[:, gs].T.astype(BF16), c_ref[:, gs].T.astype(BF16)
            cb, cb_t = _dot(cg, bg_t), _dot(bg, cg_t)
            dcb = jnp.zeros((CHUNK, CHUNK), F32)
            dbg = jnp.zeros((CHUNK, SSD_STATE), F32)
            dcg = jnp.zeros((CHUNK, SSD_STATE), F32)
            for pp in range(PAIRS_PER_GROUP):
                j = g * PAIRS_PER_GROUP + pp
                ps = slice(j * LANES, (j + 1) * LANES)
                x = xs_ref[:, ps]
                dtp, ecol, dsr = c["dt_x"][:, ps], c["ecol_x"][:, ps], c["dsr_x"][:, ps]
                elast = ecol[CHUNK - 1:CHUNK]
                xdt = x * dtp
                xb = xdt.astype(BF16)
                dskp = jnp.where(lane_lo[0:1], dsk[:, 2 * j:2 * j + 1], dsk[:, 2 * j + 1:2 * j + 2])
                dyp = dy[:, ps]
                dyb = dyp.astype(BF16)
                sp, dsn = sp_ref[0, j], ds_scr[j]
                spb, dsnb = sp.astype(BF16), dsn.astype(BF16)
                y_off = ecol * _dot(cg, spb)
                dw = (dyp * ecol).astype(BF16)
                dcg = dcg + _dot_nt(dw, spb)
                dsp = _dot(cg_t, dw) + elast * dsn
                xd = xdt * dsr
                zd = _dot(bg, dsnb) * dsr
                dbg = dbg + _dot_nt(xd.astype(BF16), dsnb)
                dxdt = zd
                zero = jnp.zeros_like(xb)
                for h, lm in ((2 * j, lane_lo), (2 * j + 1, jnp.logical_not(lane_lo))):
                    le = _head_decay(c, h)
                    dm = _dot_nt(jnp.where(lm, dyb, zero), jnp.where(lm, xb, zero))
                    dcb = dcb + dm * le
                    m = cb * le
                    m_t = (cb_t * _head_decay(c, h, transposed=True)).astype(BF16)
                    dxdt = dxdt + jnp.where(lm, _dot(m_t, dyb), 0.0)
                    q = dm * m
                    da_cols = da_cols + jnp.where(col == h, jnp.sum(q, axis=1, keepdims=True), 0.0)
                    da_rows = da_rows + jnp.where(row == h, _colsum(q), 0.0)
                ds_scr[j] = dsp
                dxs_ref[:, ps] = dxdt * dtp + dskp * dyp
                p_a.append(dyp * y_off - xdt * zd)
                p_dt.append(dxdt * x)
                v_last.append(_colsum(zd * xdt) + elast * _colsum(dsn * sp))
            dcbb = dcb.astype(BF16)
            db_ref[:, gs] = dbg + _dot_tn(dcbb, cg)
            dc_ref[:, gs] = dcg + _dot(dcbb, bg)
        e = e_ref[...]
        rows8 = jnp.concatenate([jnp.concatenate(v_last, axis=1), _colsum(dy * xs_ref[...]),
                                 jnp.zeros((6, SSD_INNER), F32)], axis=0)
        r8 = _dot_split(rows8, e)
        da = (_dot_split(jnp.concatenate(p_a, axis=1), e) + jnp.where(row == CHUNK - 1, r8[0:1], 0.0)
              + da_cols - da_rows.T)
        ddsk_ref[...] += r8[1:2]
        dadt = jnp.dot((row <= col).astype(F32), da, precision=HIGHEST, preferred_element_type=F32)
        ddt = dadt * a + _dot_split(jnp.concatenate(p_dt, axis=1), e)
        dalog_ref[...] += _colsum(dadt * dt) * a
        ddtr = ddt * _sigmoid(dtr + dtb_ref[...])
        ddtr_ref[...] = ddtr
        ddtb_ref[...] += _colsum(ddtr)

    ck = lambda n, col=0: pl.BlockSpec((CHUNK, n), lambda c: (nc - 1 - c, col))
    acc = lambda n: _full((1, n))
    return pl.pallas_call(
        body, name="ssd_bwd", grid=(nc,),
        in_specs=[ck(SSD_INNER), ck(SSD_INNER), ck(SSD_INNER), ck(SSD_INNER), ck(SSD_BC, SSD_INNER // SSD_BC),
                  ck(SSD_BC, SSD_INNER // SSD_BC + 1), ck(DT_PAD),
                  pl.BlockSpec((1, N_PAIRS, SSD_STATE, LANES), lambda c: (nc - 1 - c, 0, 0, 0)),
                  acc(DT_PAD), acc(DT_PAD), acc(DT_PAD), acc(SSD_INNER), _full((SSD_INNER, LANES)),
                  _full((LANES, SSD_INNER))],
        out_specs=[ck(SSD_INNER), ck(SSD_INNER), ck(SSD_BC), ck(SSD_BC), ck(DT_PAD),
                   acc(SSD_INNER), acc(DT_PAD), acc(DT_PAD), acc(DT_PAD)],
        out_shape=[jax.ShapeDtypeStruct((t, SSD_INNER), BF16), jax.ShapeDtypeStruct((t, SSD_INNER), F32),
                   jax.ShapeDtypeStruct((t, SSD_BC), F32), jax.ShapeDtypeStruct((t, SSD_BC), F32),
                   jax.ShapeDtypeStruct((t, DT_PAD), F32), jax.ShapeDtypeStruct((1, SSD_INNER), F32),
                   jax.ShapeDtypeStruct((1, DT_PAD), F32), jax.ShapeDtypeStruct((1, DT_PAD), F32),
                   jax.ShapeDtypeStruct((1, DT_PAD), F32)],
        scratch_shapes=[pltpu.VMEM((N_PAIRS, SSD_STATE, LANES), F32)], compiler_params=_params(1),
    )(dya, y, z, xc, xc, xc, dtr, sprev, dtb, alog, dsk, nw, e_heads, e_t)


def _sgu_act(uv, uvb, lnw, lnb):
    a = _gelu(uv + uvb)
    return a[:, :SGU_WIDTH], _layer_norm(a[:, SGU_WIDTH:], lnw, lnb)


def _sgu_weights(ws_ref):
    row = lax.broadcasted_iota(jnp.int32, (CHUNK, CHUNK), 0)
    col = lax.broadcasted_iota(jnp.int32, (CHUNK, CHUNK), 1)
    return [jnp.where(row >= col, ws_ref[g], 0.0).astype(BF16) for g in range(SGU_GROUPS)], row >= col


def _sgu_fwd(uv, uvb, lnw, lnb, ws, bs_t):
    t = uv.shape[0]

    def body(uv_ref, uvb_ref, lnw_ref, lnb_ref, ws_ref, bs_ref, o_ref):
        u, vn = _sgu_act(uv_ref[...], uvb_ref[...], lnw_ref[...], lnb_ref[...])
        wc, _ = _sgu_weights(ws_ref)
        bs = bs_ref[...]
        for g in range(SGU_GROUPS):
            gs = slice(g * LANES, (g + 1) * LANES)
            mixed = _dot(wc[g], vn[:, gs].astype(BF16)) + bs[:, g:g + 1]
            o_ref[:, gs] = (u[:, gs] * mixed).astype(BF16)

    return pl.pallas_call(
        body, name="sgu_fwd", grid=(t // CHUNK,),
        in_specs=[_rows(CHUNK, 2 * SGU_WIDTH), _full((1, 2 * SGU_WIDTH)), _full((1, SGU_WIDTH)), _full((1, SGU_WIDTH)),
                  _full(ws.shape), _full(bs_t.shape)],
        out_specs=_rows(CHUNK, SGU_WIDTH), out_shape=jax.ShapeDtypeStruct((t, SGU_WIDTH), BF16),
        compiler_params=_params(1),
    )(uv, uvb, lnw, lnb, ws, bs_t)


def _sgu_bwd(dyb, uv, uvb, lnw, lnb, ws, bs_t, e_groups):
    t = uv.shape[0]

    def body(dyb_ref, uv_ref, uvb_ref, lnw_ref, lnb_ref, ws_ref, bs_ref, e_ref,
             duv_ref, duvb_ref, dlnw_ref, dlnb_ref, dws_ref, dbs_ref):
        @pl.when(pl.program_id(0) == 0)
        def _():
            for r in (duvb_ref, dlnw_ref, dlnb_ref, dws_ref, dbs_ref):
                r[...] = jnp.zeros_like(r)

        (u, vn), act_vjp = jax.vjp(_sgu_act, uv_ref[...], uvb_ref[...], lnw_ref[...], lnb_ref[...])
        wc, causal = _sgu_weights(ws_ref)
        bs = bs_ref[...]
        dyb = dyb_ref[...]
        du, dvn, dmix = [], [], []
        for g in range(SGU_GROUPS):
            gs = slice(g * LANES, (g + 1) * LANES)
            vb = vn[:, gs].astype(BF16)
            mixed = _dot(wc[g], vb) + bs[:, g:g + 1]
            dm = dyb[:, gs] * u[:, gs]
            dmb = dm.astype(BF16)
            du.append(dyb[:, gs] * mixed)
            dvn.append(_dot_tn(wc[g], dmb))
            dws_ref[g] += jnp.where(causal, _dot_nt(dmb, vb), 0.0)
            dmix.append(dm)
        dbs_ref[...] += _dot_split(jnp.concatenate(dmix, axis=1), e_ref[...])
        duv, duvb, dlnw, dlnb = act_vjp((jnp.concatenate(du, axis=1), jnp.concatenate(dvn, axis=1)))
        duv_ref[...] = duv.astype(BF16)
        duvb_ref[...] += duvb
        dlnw_ref[...] += dlnw
        dlnb_ref[...] += dlnb

    return pl.pallas_call(
        body, name="sgu_bwd", grid=(t // CHUNK,),
        in_specs=[_rows(CHUNK, SGU_WIDTH), _rows(CHUNK, 2 * SGU_WIDTH), _full((1, 2 * SGU_WIDTH)),
                  _full((1, SGU_WIDTH)), _full((1, SGU_WIDTH)), _full(ws.shape), _full(bs_t.shape),
                  _full(e_groups.shape)],
        out_specs=[_rows(CHUNK, 2 * SGU_WIDTH), _full((1, 2 * SGU_WIDTH)), _full((1, SGU_WIDTH)),
                   _full((1, SGU_WIDTH)), _full(ws.shape), _full(bs_t.shape)],
        out_shape=[jax.ShapeDtypeStruct((t, 2 * SGU_WIDTH), BF16), jax.ShapeDtypeStruct((1, 2 * SGU_WIDTH), F32),
                   jax.ShapeDtypeStruct((1, SGU_WIDTH), F32), jax.ShapeDtypeStruct((1, SGU_WIDTH), F32),
                   jax.ShapeDtypeStruct(ws.shape, F32), jax.ShapeDtypeStruct(bs_t.shape, F32)],
        compiler_params=_params(1),
    )(dyb, uv, uvb, lnw, lnb, ws, bs_t, e_groups)


def _merge(gates, pa, pb, bg):
    s = _sigmoid(gates + bg)
    return s[:, :D_MODEL] * pa + s[:, D_MODEL:] * pb


def _merge_fwd(gates, pa, pb, bg, tm=256):
    t = gates.shape[0]

    def body(g_ref, pa_ref, pb_ref, bg_ref, o_ref):
        o_ref[...] = _merge(g_ref[...], pa_ref[...], pb_ref[...], bg_ref[...]).astype(BF16)

    return pl.pallas_call(
        body, name="merge_fwd", grid=(t // tm,),
        in_specs=[_rows(tm, 2 * D_MODEL), _rows(tm, D_MODEL), _rows(tm, D_MODEL), _full((1, 2 * D_MODEL))],
        out_specs=_rows(tm, D_MODEL), out_shape=jax.ShapeDtypeStruct((t, D_MODEL), BF16), compiler_params=_params(1),
    )(gates, pa, pb, bg)


def _merge_bwd(dmix, gates, pa, pb, bg, tm=256):
    t = gates.shape[0]

    def body(d_ref, g_ref, pa_ref, pb_ref, bg_ref, dg_ref, dpa_ref, dpb_ref, dbg_ref):
        @pl.when(pl.program_id(0) == 0)
        def _():
            dbg_ref[...] = jnp.zeros_like(dbg_ref)

        _, vjp = jax.vjp(_merge, g_ref[...], pa_ref[...], pb_ref[...], bg_ref[...])
        dg, dpa, dpb, dbg = vjp(d_ref[...])
        dg_ref[...] = dg.astype(BF16)
        dpa_ref[...] = dpa.astype(BF16)
        dpb_ref[...] = dpb.astype(BF16)
        dbg_ref[...] += dbg

    return pl.pallas_call(
        body, name="merge_bwd", grid=(t // tm,),
        in_specs=[_rows(tm, D_MODEL), _rows(tm, 2 * D_MODEL), _rows(tm, D_MODEL), _rows(tm, D_MODEL),
                  _full((1, 2 * D_MODEL))],
        out_specs=[_rows(tm, 2 * D_MODEL), _rows(tm, D_MODEL), _rows(tm, D_MODEL), _full((1, 2 * D_MODEL))],
        out_shape=[jax.ShapeDtypeStruct((t, 2 * D_MODEL), BF16), jax.ShapeDtypeStruct((t, D_MODEL), BF16),
                   jax.ShapeDtypeStruct((t, D_MODEL), BF16), jax.ShapeDtypeStruct((1, 2 * D_MODEL), F32)],
        compiler_params=_params(1),
    )(dmix, gates, pa, pb, bg)


def _conv_f_fwd(up, cw, cb, tm=128):
    t, c = up.shape

    def body(x_ref, h_ref, w_ref, b_ref, o_ref, y_ref):
        halo = jnp.where(pl.program_id(0) > 0, h_ref[...], 0.0)
        y = _causal_conv(x_ref[...], halo, w_ref[...], b_ref[...])
        y_ref[...] = y
        o_ref[...] = (_silu(y[:, :D_FF]) * y[:, D_FF:]).astype(BF16)

    return pl.pallas_call(
        body, name="conv_f_fwd", grid=(t // tm,),
        in_specs=[_rows(tm, c), _halo(tm, c), _full(cw.shape), _full((1, c))],
        out_specs=[_rows(tm, D_FF), _rows(tm, c)],
        out_shape=[jax.ShapeDtypeStruct((t, D_FF), BF16), jax.ShapeDtypeStruct((t, c), F32)],
        compiler_params=_params(1),
    )(up, up, cw, cb)


def _conv_f_bwd(dact, y, up, cw, tm=128):
    t, c = up.shape
    nt = t // tm

    def body(d_ref, y_ref, x_ref, w_ref, dx_ref, dw_ref, db_ref, nxt_scr):
        @pl.when(pl.program_id(0) == 0)
        def _():
            nxt_scr[...] = jnp.zeros_like(nxt_scr)
            dw_ref[...] = jnp.zeros_like(dw_ref)
            db_ref[...] = jnp.zeros_like(db_ref)

        a, v = y_ref[:, :D_FF], y_ref[:, D_FF:]
        d = d_ref[...]
        dy = jnp.concatenate([d * v * _dsilu(a), d * _silu(a)], axis=1)
        dx, dw = _causal_conv_bwd(dy, nxt_scr[...], x_ref[...], w_ref[...])
        dx_ref[...] = dx.astype(BF16)
        nxt_scr[...] = dy[:8]
        dw_ref[...] += dw
        db_ref[...] += _colsum(dy)

    return pl.pallas_call(
        body, name="conv_f_bwd", grid=(nt,),
        in_specs=[_rows(tm, D_FF, nt, True), _rows(tm, c, nt, True), _rows(tm, c, nt, True), _full(cw.shape)],
        out_specs=[_rows(tm, c, nt, True), _full(cw.shape), _full((1, c))],
        out_shape=[jax.ShapeDtypeStruct((t, c), BF16), jax.ShapeDtypeStruct(cw.shape, F32),
                   jax.ShapeDtypeStruct((1, c), F32)],
        scratch_shapes=[pltpu.VMEM((8, c), F32)], compiler_params=_params(1),
    )(dact, y, up, cw)


def _conv_a_bwd(dxs, db, dc, y, xbc, cw, tm=256):
    t, c = xbc.shape
    nt = t // tm

    def body(dxs_ref, db_ref, dc_ref, y_ref, x_ref, w_ref, dx_ref, dw_ref, dbias_ref, nxt_scr):
        @pl.when(pl.program_id(0) == 0)
        def _():
            nxt_scr[...] = jnp.zeros_like(nxt_scr)
            dw_ref[...] = jnp.zeros_like(dw_ref)
            dbias_ref[...] = jnp.zeros_like(dbias_ref)

        dy = jnp.concatenate([dxs_ref[...], db_ref[...], dc_ref[...]], axis=1) * _dsilu(y_ref[...])
        dx, dw = _causal_conv_bwd(dy, nxt_scr[...], x_ref[...], w_ref[...])
        dx_ref[...] = dx.astype(BF16)
        nxt_scr[...] = dy[:8]
        dw_ref[...] += dw
        dbias_ref[...] += _colsum(dy)

    return pl.pallas_call(
        body, name="conv_a_bwd", grid=(nt,),
        in_specs=[_rows(tm, SSD_INNER, nt, True), _rows(tm, SSD_BC, nt, True), _rows(tm, SSD_BC, nt, True),
                  _rows(tm, c, nt, True), _rows(tm, c, nt, True), _full(cw.shape)],
        out_specs=[_rows(tm, c, nt, True), _full(cw.shape), _full((1, c))],
        out_shape=[jax.ShapeDtypeStruct((t, c), BF16), jax.ShapeDtypeStruct(cw.shape, F32),
                   jax.ShapeDtypeStruct((1, c), F32)],
        scratch_shapes=[pltpu.VMEM((8, c), F32)], compiler_params=_params(1),
    )(dxs, db, dc, y, xbc, cw)


def _pad_lanes(v, n=DT_PAD):
    return jnp.pad(v, ((0, 0), (0, n - v.shape[1])))


def _local_step(x, target, w, p, after=None, late_weights=None, on_grad=None, on_small=None):
    dtb, alog, dsk = _pad_lanes(p["dt_bias"]), _pad_lanes(p["a_log"]), _pad_lanes(p["d_skip"])
    bs_t = _pad_lanes(p["b_spatial"].T)
    e_heads = (jnp.arange(SSD_INNER)[:, None] // SSD_HEAD_DIM == jnp.arange(LANES)[None, :]).astype(BF16)
    e_heads_t = (jnp.arange(LANES)[:, None] == jnp.arange(SSD_INNER)[None, :] // SSD_HEAD_DIM).astype(BF16)
    e_groups = (jnp.arange(SGU_WIDTH)[:, None] // LANES == jnp.arange(LANES)[None, :]).astype(BF16)

    n1 = _norm_fwd(x, p["norm1_w"], "norm1_fwd", after=after)
    z = _mm(n1, w["z"], "nt", "proj_z")
    xbc = _mm(n1, w["xbc"], "nt", "proj_xbc")
    dtr = _mm(n1, w["dt"], "nt", "proj_dt")
    uv = _mm(n1, w["uv"], "nt", "proj_uv")
    gates = _mm(n1, w["gates"], "nt", "proj_gates")
    xc, conv_a_out = _conv_a_fwd(xbc, w["conv_a"], p["conv_a_b"])
    y, ya, sprev = _ssd_fwd(xc, dtr, z, dtb, alog, dsk, p["ssd_norm_w"], e_heads_t)
    yb = _sgu_fwd(uv, p["uv_b"], p["v_ln_w"], p["v_ln_b"], p["w_spatial"], bs_t)
    if late_weights is not None:
        w = {**w, **late_weights(ya, yb)}
    pa = _mm(ya, w["branch_a"], "nn", "branch_a")
    pb = _mm(yb, w["branch_b"], "nn", "branch_b")
    mix = _merge_fwd(gates, pa, pb, p["b_gate"])
    wide = [(D_MODEL, F32), (D_MODEL, BF16)]
    h1, n2 = _mm_rows(mix, w["out"], "nn", "out_proj", _residual_norm, rows=[x], fulls=[p["norm2_w"]], row_outs=wide)
    up = _mm(n2, w["up"], "nt", "up_proj")
    act, conv_f_out = _conv_f_fwd(up, w["conv_f"], p["conv_f_b"])
    dh2, dh2b, loss, g_final = _mm_rows(
        act, w["down"], "nn", "down_proj", _loss_and_grad, rows=[h1, target], fulls=[p["final_norm_w"]],
        row_outs=wide, acc_outs=[(8, LANES), (1, D_MODEL)])

    on_grad = on_grad or (lambda name, grads: None)
    g = {"final_norm_w": g_final}
    g["down"] = _wgrad(act, dh2b, "down_wgrad")
    tok = on_grad("w_down", g)
    dact = _mm(dh2b, w["down"], "nt", "down_dgrad", after=tok)
    dup, g["conv_f"], g["conv_f_b"] = _conv_f_bwd(dact, conv_f_out, up, w["conv_f"])
    g["up"] = _wgrad(dup, n2, "up_wgrad")
    tok = on_grad("w_up", g)
    dh1, dh1b, g["norm2_w"] = _mm_rows(
        dup, w["up"], "nn", "up_dgrad", _norm_backward, rows=[h1, dh2], fulls=[p["norm2_w"]], row_outs=wide,
        acc_outs=[(1, D_MODEL)], after=tok)
    g["out"] = _wgrad(mix, dh1b, "out_wgrad")
    tok = on_grad("w_out", g)
    dmix = _mm(dh1b, w["out"], "nt", "out_dgrad", after=tok)
    dgates, dpa, dpb, g["b_gate"] = _merge_bwd(dmix, gates, pa, pb, p["b_gate"])
    g["branch_a"] = _wgrad(ya, dpa, "branch_a_wgrad")
    g["branch_b"] = _wgrad(yb, dpb, "branch_b_wgrad")
    tok = on_grad("w_branch", g)
    dya = _mm(dpa, w["branch_a"], "nt", "branch_a_dgrad", after=tok)
    dyb = _mm(dpb, w["branch_b"], "nt", "branch_b_dgrad", after=tok)
    duv, g["uv_b"], g["v_ln_w"], g["v_ln_b"], g["w_spatial"], dbs_t = _sgu_bwd(
        dyb, uv, p["uv_b"], p["v_ln_w"], p["v_ln_b"], p["w_spatial"], bs_t, e_groups)
    g["b_spatial"] = dbs_t[:, :SGU_GROUPS].T
    dz, dxs, db, dc, ddtr, g["ssd_norm_w"], ddtb, dalog, ddsk = _ssd_bwd(
        dya, y, z, xc, dtr, sprev, dtb, alog, dsk, p["ssd_norm_w"], e_heads, e_heads_t)
    g["dt_bias"], g["a_log"], g["d_skip"] = ddtb, dalog, ddsk
    dxbc, g["conv_a"], g["conv_a_b"] = _conv_a_bwd(dxs, db, dc, conv_a_out, xbc, w["conv_a"])
    tok = on_small(g, loss) if on_small else None
    ddtrb = ddtr.astype(BF16)
    for name, d in (("z", dz), ("xbc", dxbc), ("dt", ddtrb), ("uv", duv), ("gates", dgates)):
        g[name] = _wgrad(d, n1, name + "_wgrad", after=tok)
    tok = on_grad("w_in", g)
    dn1 = _mm(dz, w["z"], "nn", "z_dgrad", after=tok)
    dn1 = _mm(dxbc, w["xbc"], "nn", "xbc_dgrad", acc=dn1)
    dn1 = _mm(ddtrb, w["dt"], "nn", "dt_dgrad", acc=dn1)
    dn1 = _mm(duv, w["uv"], "nn", "uv_dgrad", acc=dn1)
    gx, g["norm1_w"] = _mm_rows(
        dgates, w["gates"], "nn", "gates_dgrad",
        lambda r, so_far, h, dres, w_: tuple(t[:1] for t in _norm_backward(r + so_far, h, dres, w_)),
        rows=[dn1, x, dh1], fulls=[p["norm1_w"]], row_outs=wide[:1], acc_outs=[(1, D_MODEL)])
    return loss, gx, g


def _place():
    return lax.axis_index("x"), lax.axis_index("y"), lax.axis_index("c")


def _other_chips(x, y):
    return [(1 - x, y), (x, 1 - y), (1 - x, 1 - y)]


def _all_gather(shards, name):
    n = len(shards)

    def body(*refs):
        ins, outs = refs[:n], refs[n:2 * n]
        send_sems, recv_sems, local_sems = refs[2 * n:]
        x, y, c = _place()
        me, sibling = (x, y, c), (x, y, 1 - c)
        chips = _other_chips(x, y)

        def copy(a, k, block, to, src=None):
            slot = outs[a].at[4 * block[0] + 2 * block[1] + block[2]]
            return pltpu.make_async_remote_copy(
                src_ref=slot if src is None else src, dst_ref=slot, send_sem=send_sems.at[7 * a + k],
                recv_sem=recv_sems.at[7 * a + k], device_id=to, device_id_type=MESH)

        started = []
        for a in range(n):
            mine = pltpu.make_async_copy(ins[a], outs[a].at[4 * x + 2 * y + c], local_sems.at[a])
            mine.start()
            started.append(mine)
        sends = []
        for a in range(n):
            sends.append(copy(a, 0, me, sibling, src=ins[a]))
            sends += [copy(a, 1 + j, me, (*chip, c), src=ins[a]) for j, chip in enumerate(chips)]
        for cp in sends:
            cp.start()
        for a in range(n):
            for j, chip in enumerate(chips):
                copy(a, 1 + j, (*chip, c), me).wait_recv()
                fwd = copy(a, 4 + j, (*chip, c), sibling)
                fwd.start()
                sends.append(fwd)
        for a in range(n):
            copy(a, 0, sibling, me).wait_recv()
            for j, chip in enumerate(chips):
                copy(a, 4 + j, (*chip, 1 - c), me).wait_recv()
        for cp in sends:
            cp.wait_send()
        for mine in started:
            mine.wait()

    any_spec = pl.BlockSpec(memory_space=pl.ANY)
    return pl.pallas_call(
        body, name=name, in_specs=[any_spec] * n, out_specs=[any_spec] * n,
        out_shape=[jax.ShapeDtypeStruct((N_DEV, *s.shape), s.dtype) for s in shards],
        scratch_shapes=[pltpu.SemaphoreType.DMA((7 * n,)), pltpu.SemaphoreType.DMA((7 * n,)),
                        pltpu.SemaphoreType.DMA((n,))],
    )(*shards)


HBM_SPEC = pl.BlockSpec(memory_space=pltpu.HBM)
SEM_SPEC = pl.BlockSpec(memory_space=pltpu.SEMAPHORE)
ANY_SPEC = pl.BlockSpec(memory_space=pl.ANY)
DATAFLOW = pltpu.SideEffectType.DATAFLOW_SIDE_EFFECTING
N_PEERS = N_DEV - 1


def _peers(x, y, c):
    out = []
    for r in range(1, N_DEV):
        fx, fy, fc = r >> 2 & 1, r >> 1 & 1, r & 1
        out.append(((1 - x) if fx else x, (1 - y) if fy else y, (1 - c) if fc else c))
    return out


def _gather_copies(srcs, lands, send_sems, recv_sems, sending):
    x, y, c = _place()
    copies = []
    for a, (src, land) in enumerate(zip(srcs, lands)):
        for j, (px, py, pc) in enumerate(_peers(x, y, c)):
            slot = 4 * x + 2 * y + c if sending else 4 * px + 2 * py + pc
            copies.append(pltpu.make_async_remote_copy(
                src_ref=src, dst_ref=land.at[slot], send_sem=send_sems.at[N_PEERS * a + j],
                recv_sem=recv_sems.at[N_PEERS * a + j], device_id=(px, py, pc), device_id_type=MESH))
    return copies


def _gather_start(shards, after, name):
    n = len(shards)

    def body(*refs):
        srcs, lands = refs[:n], refs[n:2 * n]
        send_sems, recv_sems = refs[2 * n + 1:2 * n + 3]
        token = refs[-1]
        for cp in _gather_copies(srcs, lands, send_sems, recv_sems, sending=True):
            cp.start()
        token[...] = jnp.zeros_like(token)

    lands = [lax.empty((N_DEV, *s.shape), s.dtype) for s in shards]
    hbm = lambda a: pltpu.with_memory_space_constraint(a, pltpu.HBM)
    out = pl.pallas_call(
        body, name=name,
        out_shape=(pltpu.SemaphoreType.DMA((N_PEERS * n,)), pltpu.SemaphoreType.DMA((N_PEERS * n,)),
                   *[pltpu.HBM(a.shape, a.dtype) for a in (*shards, *lands)], jax.ShapeDtypeStruct((8, LANES), F32)),
        in_specs=[HBM_SPEC] * (2 * n) + [ANY_SPEC],
        out_specs=(SEM_SPEC, SEM_SPEC, *[HBM_SPEC] * (2 * n), pl.BlockSpec(memory_space=pltpu.VMEM)),
        input_output_aliases={i: 2 + i for i in range(2 * n)},
        compiler_params=pltpu.CompilerParams(has_side_effects=DATAFLOW),
    )(*[hbm(a) for a in (*shards, *lands)], after)
    return out[0], out[1], out[2:2 + n], out[2 + n:2 + 2 * n], out[-1]


def _gather_wait(send_sems, recv_sems, shards, lands, after, name):
    n = len(shards)
    after = tuple(after)

    def body(*refs):
        srcs, lands_ = refs[:n], refs[n:2 * n]
        send, recv = refs[2 * n:2 * n + 2]
        for cp in _gather_copies(srcs, lands_, send, recv, sending=False):
            cp.wait_send()
            cp.wait_recv()

    out = pl.pallas_call(
        body, name=name, out_shape=tuple(pltpu.HBM(a.shape, a.dtype) for a in (*shards, *lands)),
        in_specs=[HBM_SPEC] * (2 * n) + [SEM_SPEC, SEM_SPEC] + [ANY_SPEC] * len(after),
        out_specs=tuple([HBM_SPEC] * (2 * n)), input_output_aliases={i: i for i in range(2 * n)},
        compiler_params=pltpu.CompilerParams(has_side_effects=DATAFLOW),
    )(*shards, *lands, send_sems, recv_sems, *after)
    return out[n:]


def _chip_copies(src, land, send_sems, recv_sems):
    x, y, c = _place()
    return [pltpu.make_async_remote_copy(
        src_ref=src.at[2 * cx + cy], dst_ref=land.at[j], send_sem=send_sems.at[j], recv_sem=recv_sems.at[j],
        device_id=(cx, cy, c), device_id_type=MESH) for j, (cx, cy) in enumerate(_other_chips(x, y))]


def _chips_start(q, name):
    def body(q_ref, land_ref, send_sems, recv_sems, q_thru, land_thru, token):
        for cp in _chip_copies(q_ref, land_ref, send_sems, recv_sems):
            cp.start()
        token[...] = jnp.zeros_like(token)

    land = lax.empty((3, *q.shape[1:]), q.dtype)
    return pl.pallas_call(
        body, name=name,
        out_shape=(pltpu.SemaphoreType.DMA((3,)), pltpu.SemaphoreType.DMA((3,)), pltpu.HBM(q.shape, q.dtype),
                   pltpu.HBM(land.shape, land.dtype), jax.ShapeDtypeStruct((8, LANES), F32)),
        in_specs=[HBM_SPEC, HBM_SPEC],
        out_specs=(SEM_SPEC, SEM_SPEC, HBM_SPEC, HBM_SPEC, pl.BlockSpec(memory_space=pltpu.VMEM)),
        input_output_aliases={0: 2, 1: 3}, compiler_params=pltpu.CompilerParams(has_side_effects=DATAFLOW),
    )(pltpu.with_memory_space_constraint(q, pltpu.HBM), pltpu.with_memory_space_constraint(land, pltpu.HBM))


def _chips_wait(send_sems, recv_sems, q, land, after, name):
    def body(q_ref, land_ref, send, recv, after_ref, q_out, land_out):
        for cp in _chip_copies(q_ref, land_ref, send, recv):
            cp.wait_send()
            cp.wait_recv()

    return pl.pallas_call(
        body, name=name, out_shape=(pltpu.HBM(q.shape, q.dtype), pltpu.HBM(land.shape, land.dtype)),
        in_specs=[HBM_SPEC, HBM_SPEC, SEM_SPEC, SEM_SPEC, ANY_SPEC], out_specs=(HBM_SPEC, HBM_SPEC),
        input_output_aliases={0: 0, 1: 1}, compiler_params=pltpu.CompilerParams(has_side_effects=DATAFLOW),
    )(q, land, send_sems, recv_sems, after)[1]


def _exchange_cores(parts, name):
    n = len(parts)

    def body(*refs):
        ins, outs = refs[:n], refs[n:2 * n]
        send_sems, recv_sems = refs[2 * n:]
        x, y, c = _place()
        copies = []
        for a in range(n):
            for k in range(4):
                copies.append(pltpu.make_async_remote_copy(
                    src_ref=ins[a].at[2 * k + (1 - c)], dst_ref=outs[a].at[k], send_sem=send_sems.at[4 * a + k],
                    recv_sem=recv_sems.at[4 * a + k], device_id=(x, y, 1 - c), device_id_type=MESH))
        for cp in copies:
            cp.start()
        for cp in copies:
            cp.wait()

    any_spec = pl.BlockSpec(memory_space=pl.ANY)
    return pl.pallas_call(
        body, name=name, in_specs=[any_spec] * n, out_specs=[any_spec] * n,
        out_shape=[jax.ShapeDtypeStruct((4, *s.shape[1:]), s.dtype) for s in parts],
        scratch_shapes=[pltpu.SemaphoreType.DMA((4 * n,)), pltpu.SemaphoreType.DMA((4 * n,))],
    )(*parts)


def _chip_sum(part, got, place, name, tr=256):
    _, r, c = part.shape
    tr, tc = _tile2d(r, c, tr)

    def body(place_ref, p_ref, g_ref, q_ref, own_ref):
        s = p_ref[0].astype(F32) + g_ref[0].astype(F32)
        q_ref[0] = s.astype(BF16)

        @pl.when(pl.program_id(2) == place_ref[1])
        def _():
            own_ref[...] = s

    grid_spec = pltpu.PrefetchScalarGridSpec(
        num_scalar_prefetch=1, grid=(r // tr, c // tc, 4),
        in_specs=[pl.BlockSpec((1, tr, tc), lambda i, j, k, pr: (2 * k + pr[0], i, j)),
                  pl.BlockSpec((1, tr, tc), lambda i, j, k, pr: (k, i, j))],
        out_specs=[pl.BlockSpec((1, tr, tc), lambda i, j, k, pr: (k, i, j)),
                   pl.BlockSpec((tr, tc), lambda i, j, k, pr: (i, j))])
    return pl.pallas_call(
        body, name=name, grid_spec=grid_spec,
        out_shape=[jax.ShapeDtypeStruct((4, r, c), BF16), jax.ShapeDtypeStruct((r, c), F32)],
        compiler_params=_params(3),
    )(place, part, got)


def _adamw(w, g, m, v):
    m = ADAM_B1 * m + (1.0 - ADAM_B1) * g
    v = ADAM_B2 * v + (1.0 - ADAM_B2) * jnp.square(g)
    m_hat = m / (1.0 - ADAM_B1 ** ADAM_STEP)
    v_hat = v / (1.0 - ADAM_B2 ** ADAM_STEP)
    return -ADAM_LR * (m_hat / (jnp.sqrt(v_hat) + ADAM_EPS) + ADAM_WD * w), m, v


def _sum_adamw(own, got, w, m, v, name, tr=256):
    r, c = own.shape
    if w.ndim == 3:
        tr, tc = r, 4 * LANES
        wblk = pl.BlockSpec((tr, 1, tc), lambda i, j: (i, 0, j))
    else:
        tr, tc = _tile2d(r, c, tr)
        wblk = pl.BlockSpec((tr, tc), lambda i, j: (i, j))

    def body(own_ref, got_ref, w_ref, m_ref, v_ref, g_ref, d_ref, nm_ref, nv_ref):
        g = own_ref[...]
        for j in range(3):
            g = g + got_ref[j].astype(F32)
        two_d = lambda ref: ref[...].reshape(tr, tc)
        delta, nm, nv = _adamw(two_d(w_ref), g, two_d(m_ref), two_d(v_ref))
        for ref, val in ((g_ref, g), (d_ref, delta), (nm_ref, nm), (nv_ref, nv)):
            ref[...] = val.reshape(ref.shape)

    blk = pl.BlockSpec((tr, tc), lambda i, j: (i, j))
    return pl.pallas_call(
        body, name=name, grid=(r // tr, c // tc),
        in_specs=[blk, pl.BlockSpec((3, tr, tc), lambda i, j: (0, i, j)), wblk, wblk, wblk], out_specs=[wblk] * 4,
        out_shape=[jax.ShapeDtypeStruct(w.shape, F32)] * 4, compiler_params=_params(2),
    )(own, got, w, m, v)


VECTORS = ["norm1_w", "b_gate", "conv_a_b", "dt_bias", "a_log", "d_skip", "ssd_norm_w", "uv_b", "v_ln_w", "v_ln_b",
           "norm2_w", "conv_f_b", "final_norm_w"]
SMALL_ORDER = VECTORS + ["w_spatial", "b_spatial", "conv_a_w", "conv_f_w"]


ROW_VECTORS = VECTORS[1:]


def _small_adamw(gathered, w, m, v):
    sizes = {n: w[n].shape[1] for n in ROW_VECTORS}
    offs, off = {}, 0
    for n in ROW_VECTORS:
        offs[n] = off
        off += -(-sizes[n] // LANES) * LANES
    loss_off = off
    k = len(SMALL_ORDER)
    n_g = len(gathered)

    def body(*refs):
        row_ref, ws_ref, bs_ref, ca_ref, cf_ref, n1_ref = refs[:n_g]
        w_refs, m_refs, v_refs = (dict(zip(SMALL_ORDER, refs[n_g + i * k:n_g + (i + 1) * k])) for i in range(3))
        outs = refs[n_g + 3 * k:]
        x, y, c = _place()
        dev = 4 * x + 2 * y + c

        def total(ref):
            s = ref[0]
            for d in range(1, N_DEV):
                s = s + ref[d]
            return s

        row = total(row_ref)
        grads = {n: row[:, offs[n]:offs[n] + sizes[n]] for n in ROW_VECTORS}
        grads["norm1_w"], grads["w_spatial"], grads["b_spatial"] = total(n1_ref), total(ws_ref), total(bs_ref)
        for n, ref in (("conv_a_w", ca_ref), ("conv_f_w", cf_ref)):
            whole, cols = total(ref), w_refs[n].shape[1]
            mine = whole[:, :cols]
            for d in range(1, N_DEV):
                mine = jnp.where(dev == d, whole[:, d * cols:(d + 1) * cols], mine)
            grads[n] = mine
        for i, n in enumerate(SMALL_ORDER):
            outs[4 * i][...] = grads[n]
            outs[4 * i + 1][...], outs[4 * i + 2][...], outs[4 * i + 3][...] = _adamw(
                w_refs[n][...], grads[n], m_refs[n][...], v_refs[n][...])
        outs[4 * k][...] = row[:, loss_off:loss_off + LANES]

    out = pl.pallas_call(
        body, name="adamw_small",
        out_shape=[jax.ShapeDtypeStruct(w[n].shape, F32) for n in SMALL_ORDER for _ in range(4)]
        + [jax.ShapeDtypeStruct((1, LANES), F32)],
        compiler_params=_params(0),
    )(*gathered, *[t[n] for t in (w, m, v) for n in SMALL_ORDER])
    return [dict(zip(SMALL_ORDER, out[j:4 * k:4])) for j in range(4)] + [out[4 * k]]


SMALL = ["norm1_w", "b_gate", "conv_a_b", "dt_bias", "a_log", "d_skip", "ssd_norm_w", "uv_b", "v_ln_w", "v_ln_b",
         "w_spatial", "b_spatial", "norm2_w", "conv_f_b", "final_norm_w"]
BIG = ["w_in", "w_branch", "w_out", "w_up", "w_down"]
TRANSPOSED = ("w_in", "w_up")
WEIGHTS = ["norm1_w", "w_in", "b_gate", "conv_a_w", "conv_a_b", "dt_bias", "a_log", "d_skip", "ssd_norm_w", "uv_b",
           "v_ln_w", "v_ln_b", "w_spatial", "b_spatial", "w_branch", "w_out", "norm2_w", "w_up", "conv_f_w",
           "conv_f_b", "w_down", "final_norm_w"]
IN_SPLITS = [("z", 0, 2048), ("xbc", 2048, 5120), ("dt", 5120, 5152), ("uv", 5152, 7200), ("gates", 7200, 9248)]


def _columns_from_devices(a):
    return a.transpose(1, 0, 2).reshape(a.shape[1], -1)


def kernel(x, norm1_w, w_in, b_gate, conv_a_w, conv_a_b, dt_bias, a_log, d_skip, ssd_norm_w, uv_b, v_ln_w, v_ln_b, w_spatial, b_spatial, w_branch, w_out, norm2_w, w_up, conv_f_w, conv_f_b, w_down, final_norm_w, loss_target, m_norm1_w, m_w_in, m_b_gate, m_conv_a_w, m_conv_a_b, m_dt_bias, m_a_log, m_d_skip, m_ssd_norm_w, m_uv_b, m_v_ln_w, m_v_ln_b, m_w_spatial, m_b_spatial, m_w_branch, m_w_out, m_norm2_w, m_w_up, m_conv_f_w, m_conv_f_b, m_w_down, m_final_norm_w, v_norm1_w, v_w_in, v_b_gate, v_conv_a_w, v_conv_a_b, v_dt_bias, v_a_log, v_d_skip, v_ssd_norm_w, v_uv_b, v_v_ln_w, v_v_ln_b, v_w_spatial, v_b_spatial, v_w_branch, v_w_out, v_norm2_w, v_w_up, v_conv_f_w, v_conv_f_b, v_w_down, v_final_norm_w):
    args = dict(locals())
    wts = {n: args[n] for n in WEIGHTS}
    mom = {n: args["m_" + n] for n in WEIGHTS}
    var = {n: args["v_" + n] for n in WEIGHTS}
    cx, cy, cc = _place()
    dev = 4 * cx + 2 * cy + cc
    place = jnp.stack([cc, 2 * cx + cy]).astype(jnp.int32)

    def shard2d(n, a):
        return a[0].T if n in TRANSPOSED else a[0]

    def unshard(n, b):
        return (b.T if n in TRANSPOSED else b)[None]

    g_in, g_conv_a, g_conv_f = _all_gather(
        [shard2d("w_in", w_in).astype(BF16), conv_a_w[0], conv_f_w[0]], "gather_w_in")
    late = [shard2d(n, wts[n]).astype(BF16) for n in BIG[1:]]
    send_sems, recv_sems, late, lands, token = _gather_start(late, g_in, "gather_late_start")
    w_in_rows = g_in.reshape(-1, D_MODEL)
    w = {name: w_in_rows[lo:hi] for name, lo, hi in IN_SPLITS}
    w["dt"] = jnp.pad(w["dt"], ((0, DT_PAD - SSD_HEADS), (0, 0)))
    w["conv_a"] = _columns_from_devices(g_conv_a)
    w["conv_f"] = _columns_from_devices(g_conv_f)

    def late_weights(*after):
        got = _gather_wait(send_sems, recv_sems, late, lands, after, "gather_late_wait")
        g_branch, g_out, g_up, g_down = [lax.dynamic_update_index_in_dim(land, mine, dev, 0).reshape(-1, D_MODEL)
                                         for land, mine in zip(got, late)]
        return {"branch_a": g_branch[:SSD_INNER], "branch_b": g_branch[SSD_INNER:], "out": g_out, "up": g_up,
                "down": g_down}

    in_flight = {}

    def on_grad(n, g):
        part = {"w_in": lambda: jnp.concatenate([g[name][:hi - lo] for name, lo, hi in IN_SPLITS], axis=0),
                "w_branch": lambda: jnp.concatenate([g["branch_a"], g["branch_b"]], axis=0),
                "w_out": lambda: g["out"], "w_up": lambda: g["up"], "w_down": lambda: g["down"]}[n]()
        part = part.reshape(N_DEV, -1, D_MODEL)
        from_core, = _exchange_cores([part], f"to_other_core_{n}")
        q, own = _chip_sum(part, from_core, place, f"chip_sum_{n}")
        send, recv, q, land, tok = _chips_start(q, f"to_other_chips_start_{n}")
        in_flight[n] = (own, send, recv, q, land)
        return tok

    p = {n: wts[n][0] if wts[n].ndim > 2 else wts[n].reshape(1, -1) for n in SMALL}
    small_flight = []

    def on_small(g, loss):
        arrays = [jnp.concatenate([g[n] for n in ROW_VECTORS] + [loss[:1]], axis=1), g["w_spatial"], g["b_spatial"],
                  g["conv_a"], g["conv_f"]]
        *flight, tok = _gather_start(arrays, g["conv_a"], "gather_small_start")
        small_flight.append(flight)
        return tok

    loss, gx, g = _local_step(x[0], loss_target[0], w, p, after=token, late_weights=late_weights, on_grad=on_grad,
                              on_small=on_small)
    *flight, _ = _gather_start([g["norm1_w"]], gx, "gather_norm1_start")
    small_flight.append(flight)

    grads, delta, new_m, new_v = {}, {}, {}, {}

    def big_adamw(n, after):
        own, send, recv, q, land = in_flight[n]
        got = _chips_wait(send, recv, q, land, after, f"to_other_chips_wait_{n}")
        view = (lambda a: a.transpose(2, 0, 1)) if n == "w_in" else (lambda a: shard2d(n, a))
        back = (lambda b: b.transpose(1, 2, 0)) if n == "w_in" else (lambda b: unshard(n, b))
        out = _sum_adamw(own, got, view(wts[n]), view(mom[n]), view(var[n]), f"adamw_{n}")
        grads[n], delta[n], new_m[n], new_v[n] = [back(o) for o in out]
        return out[1]

    after = gx
    for n in ("w_down", "w_up", "w_out", "w_branch", "w_in"):
        after = big_adamw(n, after)
    gathered = []
    for (send, recv, mine, land), name in zip(small_flight, ("gather_small_wait", "gather_norm1_wait")):
        got = _gather_wait(send, recv, mine, land, [after], name)
        gathered += [lax.dynamic_update_index_in_dim(full, own, dev, 0) for full, own in zip(got, mine)]
    small = [{n: t[n][0] if t[n].ndim > 2 else t[n].reshape(1, -1) for n in SMALL_ORDER} for t in (wts, mom, var)]
    *outs, loss = _small_adamw(gathered, *small)
    for tgt, out in zip((grads, delta, new_m, new_v), outs):
        tgt.update({n: out[n].reshape(wts[n].shape) for n in SMALL_ORDER})
    loss = loss[0, 0]

    return (loss, gx[None], *[grads[n] for n in WEIGHTS], *[delta[n] for n in WEIGHTS],
            *[new_m[n] for n in WEIGHTS], *[new_v[n] for n in WEIGHTS])
```

```python
import functools

import jax
import jax.numpy as jnp
from jax import lax
from jax.experimental import pallas as pl
from jax.experimental.pallas import tpu as pltpu

F32, BF16 = jnp.float32, jnp.bfloat16
HIGHEST = lax.Precision.HIGHEST

D_MODEL = 1024
SSD_INNER = 2048
SSD_HEAD_DIM = 64
SSD_HEADS = 32
SSD_GROUPS = 4
SSD_STATE = 128
SSD_BC = SSD_GROUPS * SSD_STATE
SSD_XBC = SSD_INNER + 2 * SSD_BC
SSD_CONV = 4
CHUNK = 128
N_PAIRS = SSD_HEADS // 2
PAIRS_PER_GROUP = N_PAIRS // SSD_GROUPS
SGU_WIDTH = 1024
SGU_GROUPS = 8
D_FF = 2816
FFN_CONV = 3
NORM_EPS = 1e-6
LN_EPS = 1e-5
LANES = 128
DT_PAD = LANES

ADAM_LR, ADAM_B1, ADAM_B2, ADAM_EPS, ADAM_WD, ADAM_STEP = 0.001, 0.9, 0.999, 1e-08, 0.01, 10

N_DEV = 8
VMEM_LIMIT = 56 * 1024 * 1024
MESH = pl.DeviceIdType.MESH


def _params(n_grid, **kw):
    sem = dict(dimension_semantics=("arbitrary",) * n_grid) if n_grid else {}
    return pltpu.CompilerParams(vmem_limit_bytes=VMEM_LIMIT, **sem, **kw)


def _tile(n, pref):
    t = (min(pref, n) // LANES) * LANES
    while n % t:
        t -= LANES
    return t


def _row_tile(r, pref):
    for t in range(min(pref, r) // 16 * 16, 0, -16):
        if r % t == 0:
            return t
    return r


def _tile2d(r, c, rows):
    if r % 16 == 0:
        return _row_tile(r, rows), c
    return r, _tile(c, 2 * LANES)


def _rows(tm, n, nt=None, rev=False, col=0):
    if rev:
        return pl.BlockSpec((tm, n), lambda i: (nt - 1 - i, col))
    return pl.BlockSpec((tm, n), lambda i: (i, col))


def _halo(tm, n, nt=None, rev=False, col=0):
    per = tm // 8
    if rev:
        return pl.BlockSpec((8, n), lambda i: (jnp.maximum((nt - 1 - i) * per - 1, 0), col))
    return pl.BlockSpec((8, n), lambda i: (jnp.maximum(i * per - 1, 0), col))


def _into(into, in_index, out_index):
    if into is None:
        return [], [], {}
    return [into], [pl.BlockSpec(memory_space=pl.ANY)], dict(input_output_aliases={in_index: out_index})


def _full(shape):
    nd = len(shape)
    return pl.BlockSpec(shape, lambda *_: (0,) * nd)


def _rms(x, w, eps=NORM_EPS):
    return x * lax.rsqrt(jnp.mean(x * x, axis=-1, keepdims=True) + eps) * w


def _layer_norm(x, w, b):
    mu = jnp.mean(x, axis=-1, keepdims=True)
    var = jnp.mean(jnp.square(x - mu), axis=-1, keepdims=True)
    return (x - mu) * lax.rsqrt(var + LN_EPS) * w + b


def _sigmoid(x):
    return 1.0 / (1.0 + jnp.exp(-x))


def _silu(x):
    return x * _sigmoid(x)


def _dsilu(x):
    s = _sigmoid(x)
    return s * (1.0 + x * (1.0 - s))


def _softplus(x):
    return jnp.maximum(x, 0.0) + jnp.log(1.0 + jnp.exp(-jnp.abs(x)))


def _gelu(x):
    return jax.nn.gelu(x)


def _dot(a, b):
    return jnp.dot(a, b, preferred_element_type=F32)


def _dot_nt(a, b):
    return lax.dot_general(a, b, (((1,), (1,)), ((), ())), preferred_element_type=F32)


def _dot_tn(a, b):
    return lax.dot_general(a, b, (((0,), (0,)), ((), ())), preferred_element_type=F32)


def _dot_split(p, e):
    hi = p.astype(BF16)
    lo = (p - hi.astype(F32)).astype(BF16)
    return _dot(hi, e) + _dot(lo, e)


def _colsum(x):
    return jnp.sum(x, axis=0, keepdims=True)


def _shift_down(x, halo, j):
    xs = pltpu.roll(x, j, 0)
    hs = pltpu.roll(halo, j, 0)
    r8 = lax.broadcasted_iota(jnp.int32, hs.shape, 0)
    return jnp.concatenate([jnp.where(r8 < j, hs, xs[:8]), xs[8:]], axis=0)


def _shift_up(x, nxt, j):
    n = x.shape[0]
    xs = pltpu.roll(x, n - j, 0)
    ns = pltpu.roll(nxt, 8 - j, 0)
    r8 = lax.broadcasted_iota(jnp.int32, ns.shape, 0)
    return jnp.concatenate([xs[:n - 8], jnp.where(r8 >= 8 - j, ns, xs[n - 8:])], axis=0)


def _causal_conv(x, halo, w, b):
    k = w.shape[0]
    y = b + w[k - 1:k, :] * x
    for j in range(1, k):
        y = y + w[k - 1 - j:k - j, :] * _shift_down(x, halo, j)
    return y


def _causal_conv_bwd(dy, nxt, x, w):
    k = w.shape[0]
    dx = w[k - 1:k, :] * dy
    dw = [_colsum(dy * x)]
    for j in range(1, k):
        dyj = _shift_up(dy, nxt, j)
        dx = dx + w[k - 1 - j:k - j, :] * dyj
        dw.append(_colsum(dyj * x))
    return dx, jnp.concatenate(dw[::-1], axis=0)


MM_TILE_PREF = 1408
MM_VMEM_BUDGET = 40 * 1024 * 1024
MM_WHOLE_K = 6144


def _mm_tiles(m, n, k, out_bytes):
    tm, tn = _tile(m, MM_TILE_PREF), _tile(n, MM_TILE_PREF)
    need = lambda tm, tn: 2 * (2 * k * (tm + tn) + out_bytes * tm * tn)
    while need(tm, tn) > MM_VMEM_BUDGET:
        if tn >= tm and tn > LANES:
            tn = _tile(n, tn - LANES)
        else:
            tm = _tile(m, tm - LANES)
    return tm, tn


def _mm(a, b, dims, name, acc=None, out_dtype=F32, after=None):
    if dims == "tn":
        k, m = a.shape
    else:
        m, k = a.shape
    n = b.shape[0] if dims == "nt" else b.shape[1]
    tm, tn = _mm_tiles(m, n, k, 4 * (2 if acc is not None else 1))
    a_spec = pl.BlockSpec((k, tm), lambda j, i: (0, i)) if dims == "tn" else pl.BlockSpec((tm, k), lambda j, i: (i, 0))
    b_spec = pl.BlockSpec((tn, k), lambda j, i: (j, 0)) if dims == "nt" else pl.BlockSpec((k, tn), lambda j, i: (0, j))
    o_spec = pl.BlockSpec((tm, tn), lambda j, i: (i, j))
    dot = {"nn": _dot, "nt": _dot_nt, "tn": _dot_tn}[dims]

    def body(a_ref, b_ref, *rest):
        r = dot(a_ref[...], b_ref[...])
        if acc is not None:
            r = r + rest[0][...]
        rest[-1][...] = r.astype(out_dtype)

    ins, specs = [a, b], [a_spec, b_spec]
    if acc is not None:
        ins.append(acc)
        specs.append(o_spec)
    if after is not None:
        ins.append(after)
        specs.append(pl.BlockSpec(memory_space=pl.ANY))
    return pl.pallas_call(
        body, name=name, grid=(n // tn, m // tm), in_specs=specs, out_specs=o_spec,
        out_shape=jax.ShapeDtypeStruct((m, n), out_dtype), compiler_params=_params(2),
    )(*ins)


def _mm_rows(a, b, dims, name, fn, rows=(), fulls=(), row_outs=(), acc_outs=(), after=None):
    m, k = a.shape
    n = b.shape[0] if dims == "nt" else b.shape[1]
    tk = k if (dims == "nt" or k <= MM_WHOLE_K) else _tile(k, 1024)
    nk = k // tk
    per_row = (2 * tk + 8 * n + sum(4 * r.shape[1] for r in rows)
               + sum(c * jnp.dtype(d).itemsize for c, d in row_outs))
    tm = _tile(m, 1024)
    while 2 * tm * per_row + 4 * tk * n > MM_VMEM_BUDGET:
        tm = _tile(m, tm - LANES)
    dot = _dot_nt if dims == "nt" else _dot
    n_in = 2 + len(rows) + len(fulls) + (after is not None)
    n_out = len(row_outs) + len(acc_outs)

    def body(*refs):
        ins, outs, scratch = refs[:n_in], refs[n_in:n_in + n_out], refs[n_in + n_out:]
        row_refs, acc_refs = outs[:len(row_outs)], outs[len(row_outs):]
        step = pl.program_id(1)

        @pl.when(jnp.logical_and(pl.program_id(0) == 0, step == 0))
        def _():
            for r in acc_refs:
                r[...] = jnp.zeros_like(r)

        part = dot(ins[0][...], ins[1][...])
        if nk > 1:
            part_ref, = scratch

            @pl.when(step == 0)
            def _():
                part_ref[...] = part

            @pl.when(step > 0)
            def _():
                part_ref[...] += part

        @pl.when(step == nk - 1)
        def _():
            result = part_ref[...] if nk > 1 else part
            new_rows, incs = fn(result, *[r[...] for r in ins[2:2 + len(rows) + len(fulls)]])
            for r, val in zip(row_refs, new_rows):
                r[...] = val.astype(r.dtype)
            for r, inc in zip(acc_refs, incs):
                r[...] += inc

    tile_rows = lambda c: pl.BlockSpec((tm, c), lambda i, s: (i, 0))
    b_spec = pl.BlockSpec((tk, n), lambda i, s: (s, 0)) if dims == "nn" else _full(b.shape)
    extra, extra_specs = ([after], [pl.BlockSpec(memory_space=pl.ANY)]) if after is not None else ([], [])
    return pl.pallas_call(
        body, name=name, grid=(m // tm, nk),
        in_specs=[pl.BlockSpec((tm, tk), lambda i, s: (i, s)), b_spec] + [tile_rows(r.shape[1]) for r in rows]
        + [_full(f.shape) for f in fulls] + extra_specs,
        out_specs=[tile_rows(c) for c, _ in row_outs] + [_full(s) for s in acc_outs],
        out_shape=[jax.ShapeDtypeStruct((m, c), d) for c, d in row_outs]
        + [jax.ShapeDtypeStruct(s, F32) for s in acc_outs],
        scratch_shapes=[pltpu.VMEM((tm, n), F32)] if nk > 1 else [],
        compiler_params=_params(2),
    )(a, b, *rows, *fulls, *extra)


def _residual_norm(o, x, w):
    h = x + o
    return (h, _rms(h, w)), ()


def _norm_backward(dn, h, dres, w):
    _, vjp = jax.vjp(_rms, h, w)
    dh, dw = vjp(dn)
    dh = dh + dres
    return (dh, dh), (dw,)


def _loss_and_grad(dn, h1, target, w):
    yf, vjp = jax.vjp(_rms, h1 + dn, w)
    err = yf - target
    loss = 0.5 * jnp.sum(jnp.mean(err * err, axis=-1, keepdims=True))
    dh, dw = vjp(err * (1.0 / err.shape[-1]))
    return (dh, dh), (jnp.full((8, LANES), loss, F32), dw)


def _wgrad(a, d, name, after=None):
    return _mm(a, d, "tn", name, out_dtype=BF16, after=after)


def _norm_fwd(x, w, name, after=None, tm=512):
    t, d = x.shape

    def body(x_ref, w_ref, *rest):
        rest[-1][...] = _rms(x_ref[...], w_ref[...]).astype(BF16)

    extra, extra_specs = ([after], [_full(after.shape)]) if after is not None else ([], [])
    return pl.pallas_call(
        body, name=name, grid=(t // tm,), in_specs=[_rows(tm, d), _full((1, d))] + extra_specs,
        out_specs=_rows(tm, d), out_shape=jax.ShapeDtypeStruct((t, d), BF16), compiler_params=_params(1),
    )(x, w, *extra)


def _conv_a_fwd(xbc, cw, cb, tm=256, col=0):
    t, c = xbc.shape[0], cw.shape[1]

    def body(x_ref, h_ref, w_ref, b_ref, o_ref, y_ref):
        halo = jnp.where(pl.program_id(0) > 0, h_ref[...], 0.0)
        y = _causal_conv(x_ref[...], halo, w_ref[...], b_ref[...])
        y_ref[...] = y
        o_ref[...] = _silu(y)

    return pl.pallas_call(
        body, name="conv_a_fwd", grid=(t // tm,),
        in_specs=[_rows(tm, c, col=col), _halo(tm, c, col=col), _full(cw.shape), _full((1, c))],
        out_specs=[_rows(tm, c)] * 2, out_shape=[jax.ShapeDtypeStruct((t, c), F32)] * 2, compiler_params=_params(1),
    )(xbc, xbc, cw, cb)


def _ssd_common(dtr, dtb, alog, e_t):
    row = lax.broadcasted_iota(jnp.int32, (CHUNK, CHUNK), 0)
    col = lax.broadcasted_iota(jnp.int32, (CHUNK, CHUNK), 1)
    causal = row >= col
    dt = _softplus(dtr + dtb)
    a = -jnp.exp(alog)
    acum = jnp.dot(causal.astype(F32), dt * a, precision=HIGHEST, preferred_element_type=F32)
    spread = lambda v: _dot_split(v, e_t)
    return dict(dt=dt, a=a, acum=acum, acum_t=acum.T, causal=causal, row=row, col=col, lane_lo=col < SSD_HEAD_DIM,
                dt_x=spread(dt), ecol_x=spread(jnp.exp(acum)), dsr_x=spread(jnp.exp(acum[CHUNK - 1:CHUNK, :] - acum)))


def _head_decay(c, h, transposed=False):
    d = c["acum"][:, h:h + 1] - c["acum_t"][h:h + 1, :]
    if transposed:
        return jnp.exp(jnp.where(c["row"] <= c["col"], -d, -jnp.inf))
    return jnp.exp(jnp.where(c["causal"], d, -jnp.inf))


def _ssd_fwd(xc, dtr, z, dtb, alog, dsk, nw, e_t, z_col=0):
    t = xc.shape[0]
    nc = t // CHUNK

    def body(xs_ref, b_ref, c_ref, dtr_ref, z_ref, dtb_ref, alog_ref, dsk_ref, nw_ref, et_ref,
             y_ref, ya_ref, sp_ref, s_scr):
        @pl.when(pl.program_id(0) == 0)
        def _():
            s_scr[...] = jnp.zeros_like(s_scr)

        c = _ssd_common(dtr_ref[...], dtb_ref[...], alog_ref[...], et_ref[...])
        lane_lo = c["lane_lo"]
        dsk = dsk_ref[...]
        for g in range(SSD_GROUPS):
            gs = slice(g * SSD_STATE, (g + 1) * SSD_STATE)
            bg_t, cg = b_ref[:, gs].T.astype(BF16), c_ref[:, gs].astype(BF16)
            cb = _dot(cg, bg_t)
            for pp in range(PAIRS_PER_GROUP):
                j = g * PAIRS_PER_GROUP + pp
                ps = slice(j * LANES, (j + 1) * LANES)
                x = xs_ref[:, ps]
                ecol, dsr = c["ecol_x"][:, ps], c["dsr_x"][:, ps]
                xdt = x * c["dt_x"][:, ps]
                xb = xdt.astype(BF16)
                zero = jnp.zeros_like(xb)
                yd = (_dot((cb * _head_decay(c, 2 * j)).astype(BF16), jnp.where(lane_lo, xb, zero))
                      + _dot((cb * _head_decay(c, 2 * j + 1)).astype(BF16), jnp.where(lane_lo, zero, xb)))
                sp = s_scr[j]
                yo = ecol * _dot(cg, sp.astype(BF16))
                st = _dot(bg_t, (xdt * dsr).astype(BF16))
                sp_ref[0, j] = sp
                s_scr[j] = ecol[CHUNK - 1:CHUNK] * sp + st
                dskp = jnp.where(lane_lo[0:1], dsk[:, 2 * j:2 * j + 1], dsk[:, 2 * j + 1:2 * j + 2])
                y_ref[:, ps] = yd + yo + dskp * x
        ya_ref[...] = _rms(y_ref[...] * _silu(z_ref[...]), nw_ref[...]).astype(BF16)

    ck = lambda n, col=0: pl.BlockSpec((CHUNK, n), lambda c: (c, col))
    return pl.pallas_call(
        body, name="ssd_fwd", grid=(nc,),
        in_specs=[ck(SSD_INNER), ck(SSD_BC, SSD_INNER // SSD_BC), ck(SSD_BC, SSD_INNER // SSD_BC + 1), ck(DT_PAD),
                  ck(SSD_INNER, z_col), _full((1, DT_PAD)), _full((1, DT_PAD)), _full((1, DT_PAD)),
                  _full((1, SSD_INNER)), _full(e_t.shape)],
        out_specs=[ck(SSD_INNER), ck(SSD_INNER),
                   pl.BlockSpec((1, N_PAIRS, SSD_STATE, LANES), lambda c: (c, 0, 0, 0))],
        out_shape=[jax.ShapeDtypeStruct((t, SSD_INNER), F32), jax.ShapeDtypeStruct((t, SSD_INNER), BF16),
                   jax.ShapeDtypeStruct((nc, N_PAIRS, SSD_STATE, LANES), F32)],
        scratch_shapes=[pltpu.VMEM((N_PAIRS, SSD_STATE, LANES), F32)], compiler_params=_params(1),
    )(xc, xc, xc, dtr, z, dtb, alog, dsk, nw, e_t)


def _ssd_bwd(dya, y, z, xc, dtr, sprev, dtb, alog, dsk, nw, e_heads, e_t, z_col=0, into=None):
    t = xc.shape[0]
    nc = t // CHUNK
    more, more_specs, alias = _into(into, 14, 0)

    def body(dya_ref, y_ref, z_ref, xs_ref, b_ref, c_ref, dtr_ref, sp_ref, dtb_ref, alog_ref, dsk_ref, nw_ref, e_ref,
             et_ref, *rest):
        dz_ref, dxs_ref, db_ref, dc_ref, ddtr_ref, dnw_ref, ddtb_ref, dalog_ref, ddsk_ref, ds_scr = rest[len(more):]

        @pl.when(pl.program_id(0) == 0)
        def _():
            ds_scr[...] = jnp.zeros_like(ds_scr)
            for r in (dnw_ref, ddtb_ref, dalog_ref, ddsk_ref):
                r[...] = jnp.zeros_like(r)

        y = y_ref[...]
        _, gate_vjp = jax.vjp(lambda y_, z_, w_: _rms(y_ * _silu(z_), w_), y, z_ref[...], nw_ref[...])
        dy, dz, dnw = gate_vjp(dya_ref[...])
        dz_ref[...] = dz.astype(BF16)
        dnw_ref[...] += dnw

        dtr = dtr_ref[...]
        c = _ssd_common(dtr, dtb_ref[...], alog_ref[...], et_ref[...])
        dt, a, lane_lo, row, col = c["dt"], c["a"], c["lane_lo"], c["row"], c["col"]
        dsk = dsk_ref[...]
        p_a, p_dt, v_last = [], [], []
        da_cols = jnp.zeros((CHUNK, CHUNK), F32)
        da_rows = jnp.zeros((CHUNK, CHUNK), F32)
        for g in range(SSD_GROUPS):
            gs = slice(g * SSD_STATE, (g + 1) * SSD_STATE)
            bg, cg = b_ref[:, gs].astype(BF16), c_ref[:, gs].astype(BF16)
            bg_t, cg_t = b_ref[:, gs].T.astype(BF16), c_ref[:, gs].T.astype(BF16)
            cb, cb_t = _dot(cg, bg_t), _dot(bg, cg_t)
            dcb = jnp.zeros((CHUNK, CHUNK), F32)
            dbg = jnp.zeros((CHUNK, SSD_STATE), F32)
            dcg = jnp.zeros((CHUNK, SSD_STATE), F32)
            for pp in range(PAIRS_PER_GROUP):
                j = g * PAIRS_PER_GROUP + pp
                ps = slice(j * LANES, (j + 1) * LANES)
                x = xs_ref[:, ps]
                dtp, ecol, dsr = c["dt_x"][:, ps], c["ecol_x"][:, ps], c["dsr_x"][:, ps]
                elast = ecol[CHUNK - 1:CHUNK]
                xdt = x * dtp
                xb = xdt.astype(BF16)
                dskp = jnp.where(lane_lo[0:1], dsk[:, 2 * j:2 * j + 1], dsk[:, 2 * j + 1:2 * j + 2])
                dyp = dy[:, ps]
                dyb = dyp.astype(BF16)
                sp, dsn = sp_ref[0, j], ds_scr[j]
                spb, dsnb = sp.astype(BF16), dsn.astype(BF16)
                y_off = ecol * _dot(cg, spb)
                dw = (dyp * ecol).astype(BF16)
                dcg = dcg + _dot_nt(dw, spb)
                dsp = _dot(cg_t, dw) + elast * dsn
                xd = xdt * dsr
                zd = _dot(bg, dsnb) * dsr
                dbg = dbg + _dot_nt(xd.astype(BF16), dsnb)
                dxdt = zd
                zero = jnp.zeros_like(xb)
                for h, lm in ((2 * j, lane_lo), (2 * j + 1, jnp.logical_not(lane_lo))):
                    le = _head_decay(c, h)
                    dm = _dot_nt(jnp.where(lm, dyb, zero), jnp.where(lm, xb, zero))
                    dcb = dcb + dm * le
                    m = cb * le
                    m_t = (cb_t * _head_decay(c, h, transposed=True)).astype(BF16)
                    dxdt = dxdt + jnp.where(lm, _dot(m_t, dyb), 0.0)
                    q = dm * m
                    da_cols = da_cols + jnp.where(col == h, jnp.sum(q, axis=1, keepdims=True), 0.0)
                    da_rows = da_rows + jnp.where(row == h, _colsum(q), 0.0)
                ds_scr[j] = dsp
                dxs_ref[:, ps] = dxdt * dtp + dskp * dyp
                p_a.append(dyp * y_off - xdt * zd)
                p_dt.append(dxdt * x)
                v_last.append(_colsum(zd * xdt) + elast * _colsum(dsn * sp))
            dcbb = dcb.astype(BF16)
            db_ref[:, gs] = dbg + _dot_tn(dcbb, cg)
            dc_ref[:, gs] = dcg + _dot(dcbb, bg)
        e = e_ref[...]
        rows8 = jnp.concatenate([jnp.concatenate(v_last, axis=1), _colsum(dy * xs_ref[...]),
                                 jnp.zeros((6, SSD_INNER), F32)], axis=0)
        r8 = _dot_split(rows8, e)
        da = (_dot_split(jnp.concatenate(p_a, axis=1), e) + jnp.where(row == CHUNK - 1, r8[0:1], 0.0)
              + da_cols - da_rows.T)
        ddsk_ref[...] += r8[1:2]
        dadt = jnp.dot((row <= col).astype(F32), da, precision=HIGHEST, preferred_element_type=F32)
        ddt = dadt * a + _dot_split(jnp.concatenate(p_dt, axis=1), e)
        dalog_ref[...] += _colsum(dadt * dt) * a
        ddtr = ddt * _sigmoid(dtr + dtb_ref[...])
        ddtr_ref[...] = ddtr
        ddtb_ref[...] += _colsum(ddtr)

    ck = lambda n, col=0: pl.BlockSpec((CHUNK, n), lambda c: (nc - 1 - c, col))
    acc = lambda n: _full((1, n))
    return pl.pallas_call(
        body, name="ssd_bwd", grid=(nc,),
        in_specs=[ck(SSD_INNER), ck(SSD_INNER), ck(SSD_INNER, z_col), ck(SSD_INNER), ck(SSD_BC, SSD_INNER // SSD_BC),
                  ck(SSD_BC, SSD_INNER // SSD_BC + 1), ck(DT_PAD),
                  pl.BlockSpec((1, N_PAIRS, SSD_STATE, LANES), lambda c: (nc - 1 - c, 0, 0, 0)),
                  acc(DT_PAD), acc(DT_PAD), acc(DT_PAD), acc(SSD_INNER), _full((SSD_INNER, LANES)),
                  _full((LANES, SSD_INNER))] + more_specs,
        out_specs=[ck(SSD_INNER, z_col if into is not None else 0), ck(SSD_INNER), ck(SSD_BC), ck(SSD_BC), ck(DT_PAD),
                   acc(SSD_INNER), acc(DT_PAD), acc(DT_PAD), acc(DT_PAD)],
        out_shape=[jax.ShapeDtypeStruct(into.shape if into is not None else (t, SSD_INNER), BF16),
                   jax.ShapeDtypeStruct((t, SSD_INNER), F32),
                   jax.ShapeDtypeStruct((t, SSD_BC), F32), jax.ShapeDtypeStruct((t, SSD_BC), F32),
                   jax.ShapeDtypeStruct((t, DT_PAD), F32), jax.ShapeDtypeStruct((1, SSD_INNER), F32),
                   jax.ShapeDtypeStruct((1, DT_PAD), F32), jax.ShapeDtypeStruct((1, DT_PAD), F32),
                   jax.ShapeDtypeStruct((1, DT_PAD), F32)],
        scratch_shapes=[pltpu.VMEM((N_PAIRS, SSD_STATE, LANES), F32)], compiler_params=_params(1), **alias,
    )(dya, y, z, xc, xc, xc, dtr, sprev, dtb, alog, dsk, nw, e_heads, e_t, *more)


def _sgu_act(uv, uvb, lnw, lnb):
    a = _gelu(uv + uvb)
    return a[:, :SGU_WIDTH], _layer_norm(a[:, SGU_WIDTH:], lnw, lnb)


def _sgu_weights(ws_ref):
    row = lax.broadcasted_iota(jnp.int32, (CHUNK, CHUNK), 0)
    col = lax.broadcasted_iota(jnp.int32, (CHUNK, CHUNK), 1)
    return [jnp.where(row >= col, ws_ref[g], 0.0).astype(BF16) for g in range(SGU_GROUPS)], row >= col


def _sgu_fwd(uv, uvb, lnw, lnb, ws, bs_t, col=0):
    t = uv.shape[0]

    def body(uv_ref, uvb_ref, lnw_ref, lnb_ref, ws_ref, bs_ref, o_ref):
        u, vn = _sgu_act(uv_ref[...], uvb_ref[...], lnw_ref[...], lnb_ref[...])
        wc, _ = _sgu_weights(ws_ref)
        bs = bs_ref[...]
        for g in range(SGU_GROUPS):
            gs = slice(g * LANES, (g + 1) * LANES)
            mixed = _dot(wc[g], vn[:, gs].astype(BF16)) + bs[:, g:g + 1]
            o_ref[:, gs] = (u[:, gs] * mixed).astype(BF16)

    return pl.pallas_call(
        body, name="sgu_fwd", grid=(t // CHUNK,),
        in_specs=[_rows(CHUNK, 2 * SGU_WIDTH, col=col), _full((1, 2 * SGU_WIDTH)), _full((1, SGU_WIDTH)), _full((1, SGU_WIDTH)),
                  _full(ws.shape), _full(bs_t.shape)],
        out_specs=_rows(CHUNK, SGU_WIDTH), out_shape=jax.ShapeDtypeStruct((t, SGU_WIDTH), BF16),
        compiler_params=_params(1),
    )(uv, uvb, lnw, lnb, ws, bs_t)


def _sgu_bwd(dyb, uv, uvb, lnw, lnb, ws, bs_t, e_groups, col=0, into=None):
    t = uv.shape[0]
    more, more_specs, alias = _into(into, 8, 0)

    def body(dyb_ref, uv_ref, uvb_ref, lnw_ref, lnb_ref, ws_ref, bs_ref, e_ref, *rest):
        duv_ref, duvb_ref, dlnw_ref, dlnb_ref, dws_ref, dbs_ref = rest[len(more):]

        @pl.when(pl.program_id(0) == 0)
        def _():
            for r in (duvb_ref, dlnw_ref, dlnb_ref, dws_ref, dbs_ref):
                r[...] = jnp.zeros_like(r)

        (u, vn), act_vjp = jax.vjp(_sgu_act, uv_ref[...], uvb_ref[...], lnw_ref[...], lnb_ref[...])
        wc, causal = _sgu_weights(ws_ref)
        bs = bs_ref[...]
        dyb = dyb_ref[...]
        du, dvn, dmix = [], [], []
        for g in range(SGU_GROUPS):
            gs = slice(g * LANES, (g + 1) * LANES)
            vb = vn[:, gs].astype(BF16)
            mixed = _dot(wc[g], vb) + bs[:, g:g + 1]
            dm = dyb[:, gs] * u[:, gs]
            dmb = dm.astype(BF16)
            du.append(dyb[:, gs] * mixed)
            dvn.append(_dot_tn(wc[g], dmb))
            dws_ref[g] += jnp.where(causal, _dot_nt(dmb, vb), 0.0)
            dmix.append(dm)
        dbs_ref[...] += _dot_split(jnp.concatenate(dmix, axis=1), e_ref[...])
        duv, duvb, dlnw, dlnb = act_vjp((jnp.concatenate(du, axis=1), jnp.concatenate(dvn, axis=1)))
        duv_ref[...] = duv.astype(BF16)
        duvb_ref[...] += duvb
        dlnw_ref[...] += dlnw
        dlnb_ref[...] += dlnb

    return pl.pallas_call(
        body, name="sgu_bwd", grid=(t // CHUNK,),
        in_specs=[_rows(CHUNK, SGU_WIDTH), _rows(CHUNK, 2 * SGU_WIDTH, col=col), _full((1, 2 * SGU_WIDTH)),
                  _full((1, SGU_WIDTH)), _full((1, SGU_WIDTH)), _full(ws.shape), _full(bs_t.shape),
                  _full(e_groups.shape)] + more_specs,
        out_specs=[_rows(CHUNK, 2 * SGU_WIDTH, col=col if into is not None else 0), _full((1, 2 * SGU_WIDTH)),
                   _full((1, SGU_WIDTH)), _full((1, SGU_WIDTH)), _full(ws.shape), _full(bs_t.shape)],
        out_shape=[jax.ShapeDtypeStruct(into.shape if into is not None else (t, 2 * SGU_WIDTH), BF16),
                   jax.ShapeDtypeStruct((1, 2 * SGU_WIDTH), F32),
                   jax.ShapeDtypeStruct((1, SGU_WIDTH), F32), jax.ShapeDtypeStruct((1, SGU_WIDTH), F32),
                   jax.ShapeDtypeStruct(ws.shape, F32), jax.ShapeDtypeStruct(bs_t.shape, F32)],
        compiler_params=_params(1), **alias,
    )(dyb, uv, uvb, lnw, lnb, ws, bs_t, e_groups, *more)


def _merge(gates, pa, pb, bg):
    s = _sigmoid(gates + bg)
    return s[:, :D_MODEL] * pa + s[:, D_MODEL:] * pb


def _merge_fwd(gates, pa, pb, bg, tm=256, col=0):
    t = gates.shape[0]

    def body(g_ref, pa_ref, pb_ref, bg_ref, o_ref):
        o_ref[...] = _merge(g_ref[...], pa_ref[...], pb_ref[...], bg_ref[...]).astype(BF16)

    return pl.pallas_call(
        body, name="merge_fwd", grid=(t // tm,),
        in_specs=[_rows(tm, 2 * D_MODEL, col=col), _rows(tm, D_MODEL), _rows(tm, D_MODEL), _full((1, 2 * D_MODEL))],
        out_specs=_rows(tm, D_MODEL), out_shape=jax.ShapeDtypeStruct((t, D_MODEL), BF16), compiler_params=_params(1),
    )(gates, pa, pb, bg)


def _merge_bwd(dmix, gates, pa, pb, bg, tm=256, col=0, into=None):
    t = gates.shape[0]
    more, more_specs, alias = _into(into, 5, 0)

    def body(d_ref, g_ref, pa_ref, pb_ref, bg_ref, *rest):
        dg_ref, dpa_ref, dpb_ref, dbg_ref = rest[len(more):]

        @pl.when(pl.program_id(0) == 0)
        def _():
            dbg_ref[...] = jnp.zeros_like(dbg_ref)

        _, vjp = jax.vjp(_merge, g_ref[...], pa_ref[...], pb_ref[...], bg_ref[...])
        dg, dpa, dpb, dbg = vjp(d_ref[...])
        dg_ref[...] = dg.astype(BF16)
        dpa_ref[...] = dpa.astype(BF16)
        dpb_ref[...] = dpb.astype(BF16)
        dbg_ref[...] += dbg

    return pl.pallas_call(
        body, name="merge_bwd", grid=(t // tm,),
        in_specs=[_rows(tm, D_MODEL), _rows(tm, 2 * D_MODEL, col=col), _rows(tm, D_MODEL), _rows(tm, D_MODEL),
                  _full((1, 2 * D_MODEL))] + more_specs,
        out_specs=[_rows(tm, 2 * D_MODEL, col=col if into is not None else 0), _rows(tm, D_MODEL),
                   _rows(tm, D_MODEL), _full((1, 2 * D_MODEL))],
        out_shape=[jax.ShapeDtypeStruct(into.shape if into is not None else (t, 2 * D_MODEL), BF16),
                   jax.ShapeDtypeStruct((t, D_MODEL), BF16), jax.ShapeDtypeStruct((t, D_MODEL), BF16),
                   jax.ShapeDtypeStruct((1, 2 * D_MODEL), F32)],
        compiler_params=_params(1), **alias,
    )(dmix, gates, pa, pb, bg, *more)


def _conv_f_fwd(up, cw, cb, tm=128):
    t, c = up.shape

    def body(x_ref, h_ref, w_ref, b_ref, o_ref, y_ref):
        halo = jnp.where(pl.program_id(0) > 0, h_ref[...], 0.0)
        y = _causal_conv(x_ref[...], halo, w_ref[...], b_ref[...])
        y_ref[...] = y
        o_ref[...] = (_silu(y[:, :D_FF]) * y[:, D_FF:]).astype(BF16)

    return pl.pallas_call(
        body, name="conv_f_fwd", grid=(t // tm,),
        in_specs=[_rows(tm, c), _halo(tm, c), _full(cw.shape), _full((1, c))],
        out_specs=[_rows(tm, D_FF), _rows(tm, c)],
        out_shape=[jax.ShapeDtypeStruct((t, D_FF), BF16), jax.ShapeDtypeStruct((t, c), F32)],
        compiler_params=_params(1),
    )(up, up, cw, cb)


def _conv_f_bwd(dact, y, up, cw, tm=128):
    t, c = up.shape
    nt = t // tm

    def body(d_ref, y_ref, x_ref, w_ref, dx_ref, dw_ref, db_ref, nxt_scr):
        @pl.when(pl.program_id(0) == 0)
        def _():
            nxt_scr[...] = jnp.zeros_like(nxt_scr)
            dw_ref[...] = jnp.zeros_like(dw_ref)
            db_ref[...] = jnp.zeros_like(db_ref)

        a, v = y_ref[:, :D_FF], y_ref[:, D_FF:]
        d = d_ref[...]
        dy = jnp.concatenate([d * v * _dsilu(a), d * _silu(a)], axis=1)
        dx, dw = _causal_conv_bwd(dy, nxt_scr[...], x_ref[...], w_ref[...])
        dx_ref[...] = dx.astype(BF16)
        nxt_scr[...] = dy[:8]
        dw_ref[...] += dw
        db_ref[...] += _colsum(dy)

    return pl.pallas_call(
        body, name="conv_f_bwd", grid=(nt,),
        in_specs=[_rows(tm, D_FF, nt, True), _rows(tm, c, nt, True), _rows(tm, c, nt, True), _full(cw.shape)],
        out_specs=[_rows(tm, c, nt, True), _full(cw.shape), _full((1, c))],
        out_shape=[jax.ShapeDtypeStruct((t, c), BF16), jax.ShapeDtypeStruct(cw.shape, F32),
                   jax.ShapeDtypeStruct((1, c), F32)],
        scratch_shapes=[pltpu.VMEM((8, c), F32)], compiler_params=_params(1),
    )(dact, y, up, cw)


def _conv_a_bwd(dxs, db, dc, y, xbc, cw, tm=256, col=0, into=None):
    t, c = xbc.shape[0], cw.shape[1]
    nt = t // tm
    more, more_specs, alias = _into(into, 6, 0)

    def body(dxs_ref, db_ref, dc_ref, y_ref, x_ref, w_ref, *rest):
        dx_ref, dw_ref, dbias_ref, nxt_scr = rest[len(more):]

        @pl.when(pl.program_id(0) == 0)
        def _():
            nxt_scr[...] = jnp.zeros_like(nxt_scr)
            dw_ref[...] = jnp.zeros_like(dw_ref)
            dbias_ref[...] = jnp.zeros_like(dbias_ref)

        dy = jnp.concatenate([dxs_ref[...], db_ref[...], dc_ref[...]], axis=1) * _dsilu(y_ref[...])
        dx, dw = _causal_conv_bwd(dy, nxt_scr[...], x_ref[...], w_ref[...])
        dx_ref[...] = dx.astype(BF16)
        nxt_scr[...] = dy[:8]
        dw_ref[...] += dw
        dbias_ref[...] += _colsum(dy)

    return pl.pallas_call(
        body, name="conv_a_bwd", grid=(nt,),
        in_specs=[_rows(tm, SSD_INNER, nt, True), _rows(tm, SSD_BC, nt, True), _rows(tm, SSD_BC, nt, True),
                  _rows(tm, c, nt, True), _rows(tm, c, nt, True, col), _full(cw.shape)] + more_specs,
        out_specs=[_rows(tm, c, nt, True, col if into is not None else 0), _full(cw.shape), _full((1, c))],
        out_shape=[jax.ShapeDtypeStruct(into.shape if into is not None else (t, c), BF16),
                   jax.ShapeDtypeStruct(cw.shape, F32), jax.ShapeDtypeStruct((1, c), F32)],
        scratch_shapes=[pltpu.VMEM((8, c), F32)], compiler_params=_params(1), **alias,
    )(dxs, db, dc, y, xbc, cw, *more)


def _pad_lanes(v, n=DT_PAD):
    return jnp.pad(v, ((0, 0), (0, n - v.shape[1])))


def _local_step(x, target, w, p, after=None, late_weights=None, on_grad=None, on_small=None):
    dtb, alog, dsk = _pad_lanes(p["dt_bias"]), _pad_lanes(p["a_log"]), _pad_lanes(p["d_skip"])
    bs_t = _pad_lanes(p["b_spatial"].T)
    e_heads = (jnp.arange(SSD_INNER)[:, None] // SSD_HEAD_DIM == jnp.arange(LANES)[None, :]).astype(BF16)
    e_heads_t = (jnp.arange(LANES)[:, None] == jnp.arange(SSD_INNER)[None, :] // SSD_HEAD_DIM).astype(BF16)
    e_groups = (jnp.arange(SGU_WIDTH)[:, None] // LANES == jnp.arange(LANES)[None, :]).astype(BF16)

    n1 = _norm_fwd(x, p["norm1_w"], "norm1_fwd", after=after)
    proj = _mm(n1, w["in"], "nt", "proj_in")
    dtr = _mm(n1, w["dt"], "nt", "proj_dt")
    xc, conv_a_out = _conv_a_fwd(proj, w["conv_a"], p["conv_a_b"], col=IN_BLOCK["xbc"])
    y, ya, sprev = _ssd_fwd(xc, dtr, proj, dtb, alog, dsk, p["ssd_norm_w"], e_heads_t, z_col=IN_BLOCK["z"])
    yb = _sgu_fwd(proj, p["uv_b"], p["v_ln_w"], p["v_ln_b"], p["w_spatial"], bs_t, col=IN_BLOCK["uv"])
    if late_weights is not None:
        w = {**w, **late_weights(ya, yb)}
    pa = _mm(ya, w["branch_a"], "nn", "branch_a")
    pb = _mm(yb, w["branch_b"], "nn", "branch_b")
    mix = _merge_fwd(proj, pa, pb, p["b_gate"], col=IN_BLOCK["gates"])
    wide = [(D_MODEL, F32), (D_MODEL, BF16)]
    h1, n2 = _mm_rows(mix, w["out"], "nn", "out_proj", _residual_norm, rows=[x], fulls=[p["norm2_w"]], row_outs=wide)
    up = _mm(n2, w["up"], "nt", "up_proj")
    act, conv_f_out = _conv_f_fwd(up, w["conv_f"], p["conv_f_b"])
    dh2, dh2b, loss, g_final = _mm_rows(
        act, w["down"], "nn", "down_proj", _loss_and_grad, rows=[h1, target], fulls=[p["final_norm_w"]],
        row_outs=wide, acc_outs=[(8, LANES), (1, D_MODEL)])

    on_grad = on_grad or (lambda name, grads: None)
    g = {"final_norm_w": g_final}
    g["down"] = _wgrad(act, dh2b, "down_wgrad")
    tok = on_grad("w_down", g)
    dact = _mm(dh2b, w["down"], "nt", "down_dgrad", after=tok)
    dup, g["conv_f"], g["conv_f_b"] = _conv_f_bwd(dact, conv_f_out, up, w["conv_f"])
    g["up"] = _wgrad(dup, n2, "up_wgrad")
    tok = on_grad("w_up", g)
    dh1, dh1b, g["norm2_w"] = _mm_rows(
        dup, w["up"], "nn", "up_dgrad", _norm_backward, rows=[h1, dh2], fulls=[p["norm2_w"]], row_outs=wide,
        acc_outs=[(1, D_MODEL)], after=tok)
    g["out"] = _wgrad(mix, dh1b, "out_wgrad")
    tok = on_grad("w_out", g)
    dmix = _mm(dh1b, w["out"], "nt", "out_dgrad", after=tok)
    dproj = lax.empty(proj.shape, BF16)
    dproj, dpa, dpb, g["b_gate"] = _merge_bwd(dmix, proj, pa, pb, p["b_gate"], col=IN_BLOCK["gates"], into=dproj)
    g["branch_a"] = _wgrad(ya, dpa, "branch_a_wgrad")
    g["branch_b"] = _wgrad(yb, dpb, "branch_b_wgrad")
    tok = on_grad("w_branch", g)
    dya = _mm(dpa, w["branch_a"], "nt", "branch_a_dgrad", after=tok)
    dyb = _mm(dpb, w["branch_b"], "nt", "branch_b_dgrad", after=tok)
    dproj, g["uv_b"], g["v_ln_w"], g["v_ln_b"], g["w_spatial"], dbs_t = _sgu_bwd(
        dyb, proj, p["uv_b"], p["v_ln_w"], p["v_ln_b"], p["w_spatial"], bs_t, e_groups, col=IN_BLOCK["uv"],
        into=dproj)
    g["b_spatial"] = dbs_t[:, :SGU_GROUPS].T
    dproj, dxs, db, dc, ddtr, g["ssd_norm_w"], ddtb, dalog, ddsk = _ssd_bwd(
        dya, y, proj, xc, dtr, sprev, dtb, alog, dsk, p["ssd_norm_w"], e_heads, e_heads_t, z_col=IN_BLOCK["z"],
        into=dproj)
    g["dt_bias"], g["a_log"], g["d_skip"] = ddtb, dalog, ddsk
    dproj, g["conv_a"], g["conv_a_b"] = _conv_a_bwd(dxs, db, dc, conv_a_out, proj, w["conv_a"],
                                                    col=IN_BLOCK["xbc"], into=dproj)
    tok = on_small(g, loss) if on_small else None
    ddtrb = ddtr.astype(BF16)
    g["in"] = _wgrad(dproj, n1, "in_wgrad", after=tok)
    g["dt"] = _wgrad(ddtrb, n1, "dt_wgrad")
    tok = on_grad("w_in", g)
    dn1 = _mm(ddtrb, w["dt"], "nn", "dt_dgrad", after=tok)
    gx, g["norm1_w"] = _mm_rows(
        dproj, w["in"], "nn", "in_dgrad",
        lambda r, so_far, h, dres, w_: tuple(t[:1] for t in _norm_backward(r + so_far, h, dres, w_)),
        rows=[dn1, x, dh1], fulls=[p["norm1_w"]], row_outs=wide[:1], acc_outs=[(1, D_MODEL)])
    return loss, gx, g


def _place():
    return lax.axis_index("x"), lax.axis_index("y"), lax.axis_index("c")


def _other_chips(x, y):
    return [(1 - x, y), (x, 1 - y), (1 - x, 1 - y)]


def _all_gather(shards, name):
    n = len(shards)

    def body(*refs):
        ins, outs = refs[:n], refs[n:2 * n]
        send_sems, recv_sems, local_sems = refs[2 * n:]
        x, y, c = _place()
        me, sibling = (x, y, c), (x, y, 1 - c)
        chips = _other_chips(x, y)

        def copy(a, k, block, to, src=None):
            slot = outs[a].at[4 * block[0] + 2 * block[1] + block[2]]
            return pltpu.make_async_remote_copy(
                src_ref=slot if src is None else src, dst_ref=slot, send_sem=send_sems.at[7 * a + k],
                recv_sem=recv_sems.at[7 * a + k], device_id=to, device_id_type=MESH)

        started = []
        for a in range(n):
            mine = pltpu.make_async_copy(ins[a], outs[a].at[4 * x + 2 * y + c], local_sems.at[a])
            mine.start()
            started.append(mine)
        sends = []
        for a in range(n):
            sends.append(copy(a, 0, me, sibling, src=ins[a]))
            sends += [copy(a, 1 + j, me, (*chip, c), src=ins[a]) for j, chip in enumerate(chips)]
        for cp in sends:
            cp.start()
        for a in range(n):
            for j, chip in enumerate(chips):
                copy(a, 1 + j, (*chip, c), me).wait_recv()
                fwd = copy(a, 4 + j, (*chip, c), sibling)
                fwd.start()
                sends.append(fwd)
        for a in range(n):
            copy(a, 0, sibling, me).wait_recv()
            for j, chip in enumerate(chips):
                copy(a, 4 + j, (*chip, 1 - c), me).wait_recv()
        for cp in sends:
            cp.wait_send()
        for mine in started:
            mine.wait()

    any_spec = pl.BlockSpec(memory_space=pl.ANY)
    return pl.pallas_call(
        body, name=name, in_specs=[any_spec] * n, out_specs=[any_spec] * n,
        out_shape=[jax.ShapeDtypeStruct((N_DEV, *s.shape), s.dtype) for s in shards],
        scratch_shapes=[pltpu.SemaphoreType.DMA((7 * n,)), pltpu.SemaphoreType.DMA((7 * n,)),
                        pltpu.SemaphoreType.DMA((n,))],
    )(*shards)


HBM_SPEC = pl.BlockSpec(memory_space=pltpu.HBM)
SEM_SPEC = pl.BlockSpec(memory_space=pltpu.SEMAPHORE)
ANY_SPEC = pl.BlockSpec(memory_space=pl.ANY)
DATAFLOW = pltpu.SideEffectType.DATAFLOW_SIDE_EFFECTING
N_PEERS = N_DEV - 1


def _peers(x, y, c):
    out = []
    for r in range(1, N_DEV):
        fx, fy, fc = r >> 2 & 1, r >> 1 & 1, r & 1
        out.append(((1 - x) if fx else x, (1 - y) if fy else y, (1 - c) if fc else c))
    return out


def _gather_copies(srcs, lands, send_sems, recv_sems, sending):
    x, y, c = _place()
    copies = []
    for a, (src, land) in enumerate(zip(srcs, lands)):
        for j, (px, py, pc) in enumerate(_peers(x, y, c)):
            slot = 4 * x + 2 * y + c if sending else 4 * px + 2 * py + pc
            copies.append(pltpu.make_async_remote_copy(
                src_ref=src, dst_ref=land.at[slot], send_sem=send_sems.at[N_PEERS * a + j],
                recv_sem=recv_sems.at[N_PEERS * a + j], device_id=(px, py, pc), device_id_type=MESH))
    return copies


def _gather_start(shards, after, name):
    n = len(shards)

    def body(*refs):
        srcs, lands = refs[:n], refs[n:2 * n]
        send_sems, recv_sems = refs[2 * n + 1:2 * n + 3]
        token = refs[-1]
        for cp in _gather_copies(srcs, lands, send_sems, recv_sems, sending=True):
            cp.start()
        token[...] = jnp.zeros_like(token)

    lands = [lax.empty((N_DEV, *s.shape), s.dtype) for s in shards]
    hbm = lambda a: pltpu.with_memory_space_constraint(a, pltpu.HBM)
    out = pl.pallas_call(
        body, name=name,
        out_shape=(pltpu.SemaphoreType.DMA((N_PEERS * n,)), pltpu.SemaphoreType.DMA((N_PEERS * n,)),
                   *[pltpu.HBM(a.shape, a.dtype) for a in (*shards, *lands)], jax.ShapeDtypeStruct((8, LANES), F32)),
        in_specs=[HBM_SPEC] * (2 * n) + [ANY_SPEC],
        out_specs=(SEM_SPEC, SEM_SPEC, *[HBM_SPEC] * (2 * n), pl.BlockSpec(memory_space=pltpu.VMEM)),
        input_output_aliases={i: 2 + i for i in range(2 * n)},
        compiler_params=pltpu.CompilerParams(has_side_effects=DATAFLOW),
    )(*[hbm(a) for a in (*shards, *lands)], after)
    return out[0], out[1], out[2:2 + n], out[2 + n:2 + 2 * n], out[-1]


def _gather_wait(send_sems, recv_sems, shards, lands, after, name):
    n = len(shards)
    after = tuple(after)

    def body(*refs):
        srcs, lands_ = refs[:n], refs[n:2 * n]
        send, recv = refs[2 * n:2 * n + 2]
        for cp in _gather_copies(srcs, lands_, send, recv, sending=False):
            cp.wait_send()
            cp.wait_recv()

    out = pl.pallas_call(
        body, name=name, out_shape=tuple(pltpu.HBM(a.shape, a.dtype) for a in (*shards, *lands)),
        in_specs=[HBM_SPEC] * (2 * n) + [SEM_SPEC, SEM_SPEC] + [ANY_SPEC] * len(after),
        out_specs=tuple([HBM_SPEC] * (2 * n)), input_output_aliases={i: i for i in range(2 * n)},
        compiler_params=pltpu.CompilerParams(has_side_effects=DATAFLOW),
    )(*shards, *lands, send_sems, recv_sems, *after)
    return out[n:]


def _chip_copies(src, land, send_sems, recv_sems):
    x, y, c = _place()
    return [pltpu.make_async_remote_copy(
        src_ref=src.at[2 * cx + cy], dst_ref=land.at[j], send_sem=send_sems.at[j], recv_sem=recv_sems.at[j],
        device_id=(cx, cy, c), device_id_type=MESH) for j, (cx, cy) in enumerate(_other_chips(x, y))]


def _chips_start(q, name):
    def body(q_ref, land_ref, send_sems, recv_sems, q_thru, land_thru, token):
        for cp in _chip_copies(q_ref, land_ref, send_sems, recv_sems):
            cp.start()
        token[...] = jnp.zeros_like(token)

    land = lax.empty((3, *q.shape[1:]), q.dtype)
    return pl.pallas_call(
        body, name=name,
        out_shape=(pltpu.SemaphoreType.DMA((3,)), pltpu.SemaphoreType.DMA((3,)), pltpu.HBM(q.shape, q.dtype),
                   pltpu.HBM(land.shape, land.dtype), jax.ShapeDtypeStruct((8, LANES), F32)),
        in_specs=[HBM_SPEC, HBM_SPEC],
        out_specs=(SEM_SPEC, SEM_SPEC, HBM_SPEC, HBM_SPEC, pl.BlockSpec(memory_space=pltpu.VMEM)),
        input_output_aliases={0: 2, 1: 3}, compiler_params=pltpu.CompilerParams(has_side_effects=DATAFLOW),
    )(pltpu.with_memory_space_constraint(q, pltpu.HBM), pltpu.with_memory_space_constraint(land, pltpu.HBM))


def _chips_wait(send_sems, recv_sems, q, land, after, name):
    def body(q_ref, land_ref, send, recv, after_ref, q_out, land_out):
        for cp in _chip_copies(q_ref, land_ref, send, recv):
            cp.wait_send()
            cp.wait_recv()

    return pl.pallas_call(
        body, name=name, out_shape=(pltpu.HBM(q.shape, q.dtype), pltpu.HBM(land.shape, land.dtype)),
        in_specs=[HBM_SPEC, HBM_SPEC, SEM_SPEC, SEM_SPEC, ANY_SPEC], out_specs=(HBM_SPEC, HBM_SPEC),
        input_output_aliases={0: 0, 1: 1}, compiler_params=pltpu.CompilerParams(has_side_effects=DATAFLOW),
    )(q, land, send_sems, recv_sems, after)[1]


def _exchange_cores(parts, name):
    n = len(parts)

    def body(*refs):
        ins, outs = refs[:n], refs[n:2 * n]
        send_sems, recv_sems = refs[2 * n:]
        x, y, c = _place()
        copies = []
        for a in range(n):
            for k in range(4):
                copies.append(pltpu.make_async_remote_copy(
                    src_ref=ins[a].at[2 * k + (1 - c)], dst_ref=outs[a].at[k], send_sem=send_sems.at[4 * a + k],
                    recv_sem=recv_sems.at[4 * a + k], device_id=(x, y, 1 - c), device_id_type=MESH))
        for cp in copies:
            cp.start()
        for cp in copies:
            cp.wait()

    any_spec = pl.BlockSpec(memory_space=pl.ANY)
    return pl.pallas_call(
        body, name=name, in_specs=[any_spec] * n, out_specs=[any_spec] * n,
        out_shape=[jax.ShapeDtypeStruct((4, *s.shape[1:]), s.dtype) for s in parts],
        scratch_shapes=[pltpu.SemaphoreType.DMA((4 * n,)), pltpu.SemaphoreType.DMA((4 * n,))],
    )(*parts)


def _chip_sum(part, got, place, name, tr=256):
    _, r, c = part.shape
    tr, tc = _tile2d(r, c, tr)

    def body(place_ref, p_ref, g_ref, q_ref, own_ref):
        s = p_ref[0].astype(F32) + g_ref[0].astype(F32)
        q_ref[0] = s.astype(BF16)

        @pl.when(pl.program_id(2) == place_ref[1])
        def _():
            own_ref[...] = s

    grid_spec = pltpu.PrefetchScalarGridSpec(
        num_scalar_prefetch=1, grid=(r // tr, c // tc, 4),
        in_specs=[pl.BlockSpec((1, tr, tc), lambda i, j, k, pr: (2 * k + pr[0], i, j)),
                  pl.BlockSpec((1, tr, tc), lambda i, j, k, pr: (k, i, j))],
        out_specs=[pl.BlockSpec((1, tr, tc), lambda i, j, k, pr: (k, i, j)),
                   pl.BlockSpec((tr, tc), lambda i, j, k, pr: (i, j))])
    return pl.pallas_call(
        body, name=name, grid_spec=grid_spec,
        out_shape=[jax.ShapeDtypeStruct((4, r, c), BF16), jax.ShapeDtypeStruct((r, c), F32)],
        compiler_params=_params(3),
    )(place, part, got)


def _adamw(w, g, m, v):
    m = ADAM_B1 * m + (1.0 - ADAM_B1) * g
    v = ADAM_B2 * v + (1.0 - ADAM_B2) * jnp.square(g)
    m_hat = m / (1.0 - ADAM_B1 ** ADAM_STEP)
    v_hat = v / (1.0 - ADAM_B2 ** ADAM_STEP)
    return -ADAM_LR * (m_hat / (jnp.sqrt(v_hat) + ADAM_EPS) + ADAM_WD * w), m, v


def _sum_adamw(own, got, w, m, v, name, tr=256):
    r, c = own.shape
    if w.ndim == 3:
        tr, tc = r, 4 * LANES
        wblk = pl.BlockSpec((tr, 1, tc), lambda i, j: (i, 0, j))
    else:
        tr, tc = _tile2d(r, c, tr)
        wblk = pl.BlockSpec((tr, tc), lambda i, j: (i, j))

    def body(own_ref, got_ref, w_ref, m_ref, v_ref, g_ref, d_ref, nm_ref, nv_ref):
        g = own_ref[...]
        for j in range(3):
            g = g + got_ref[j].astype(F32)
        two_d = lambda ref: ref[...].reshape(tr, tc)
        delta, nm, nv = _adamw(two_d(w_ref), g, two_d(m_ref), two_d(v_ref))
        for ref, val in ((g_ref, g), (d_ref, delta), (nm_ref, nm), (nv_ref, nv)):
            ref[...] = val.reshape(ref.shape)

    blk = pl.BlockSpec((tr, tc), lambda i, j: (i, j))
    return pl.pallas_call(
        body, name=name, grid=(r // tr, c // tc),
        in_specs=[blk, pl.BlockSpec((3, tr, tc), lambda i, j: (0, i, j)), wblk, wblk, wblk], out_specs=[wblk] * 4,
        out_shape=[jax.ShapeDtypeStruct(w.shape, F32)] * 4, compiler_params=_params(2),
    )(own, got, w, m, v)


VECTORS = ["norm1_w", "b_gate", "conv_a_b", "dt_bias", "a_log", "d_skip", "ssd_norm_w", "uv_b", "v_ln_w", "v_ln_b",
           "norm2_w", "conv_f_b", "final_norm_w"]
SMALL_ORDER = VECTORS + ["w_spatial", "b_spatial", "conv_a_w", "conv_f_w"]


ROW_VECTORS = VECTORS[1:]


def _small_adamw(gathered, w, m, v):
    sizes = {n: w[n].shape[1] for n in ROW_VECTORS}
    offs, off = {}, 0
    for n in ROW_VECTORS:
        offs[n] = off
        off += -(-sizes[n] // LANES) * LANES
    loss_off = off
    k = len(SMALL_ORDER)
    n_g = len(gathered)

    def body(*refs):
        row_ref, ws_ref, bs_ref, ca_ref, cf_ref, n1_ref = refs[:n_g]
        w_refs, m_refs, v_refs = (dict(zip(SMALL_ORDER, refs[n_g + i * k:n_g + (i + 1) * k])) for i in range(3))
        outs = refs[n_g + 3 * k:]
        x, y, c = _place()
        dev = 4 * x + 2 * y + c

        def total(ref):
            s = ref[0]
            for d in range(1, N_DEV):
                s = s + ref[d]
            return s

        row = total(row_ref)
        grads = {n: row[:, offs[n]:offs[n] + sizes[n]] for n in ROW_VECTORS}
        grads["norm1_w"], grads["w_spatial"], grads["b_spatial"] = total(n1_ref), total(ws_ref), total(bs_ref)
        for n, ref in (("conv_a_w", ca_ref), ("conv_f_w", cf_ref)):
            whole, cols = total(ref), w_refs[n].shape[1]
            mine = whole[:, :cols]
            for d in range(1, N_DEV):
                mine = jnp.where(dev == d, whole[:, d * cols:(d + 1) * cols], mine)
            grads[n] = mine
        for i, n in enumerate(SMALL_ORDER):
            outs[4 * i][...] = grads[n]
            outs[4 * i + 1][...], outs[4 * i + 2][...], outs[4 * i + 3][...] = _adamw(
                w_refs[n][...], grads[n], m_refs[n][...], v_refs[n][...])
        outs[4 * k][...] = row[:, loss_off:loss_off + LANES]

    out = pl.pallas_call(
        body, name="adamw_small",
        out_shape=[jax.ShapeDtypeStruct(w[n].shape, F32) for n in SMALL_ORDER for _ in range(4)]
        + [jax.ShapeDtypeStruct((1, LANES), F32)],
        compiler_params=_params(0),
    )(*gathered, *[t[n] for t in (w, m, v) for n in SMALL_ORDER])
    return [dict(zip(SMALL_ORDER, out[j:4 * k:4])) for j in range(4)] + [out[4 * k]]


SMALL = ["norm1_w", "b_gate", "conv_a_b", "dt_bias", "a_log", "d_skip", "ssd_norm_w", "uv_b", "v_ln_w", "v_ln_b",
         "w_spatial", "b_spatial", "norm2_w", "conv_f_b", "final_norm_w"]
BIG = ["w_in", "w_branch", "w_out", "w_up", "w_down"]
TRANSPOSED = ("w_in", "w_up")
WEIGHTS = ["norm1_w", "w_in", "b_gate", "conv_a_w", "conv_a_b", "dt_bias", "a_log", "d_skip", "ssd_norm_w", "uv_b",
           "v_ln_w", "v_ln_b", "w_spatial", "b_spatial", "w_branch", "w_out", "norm2_w", "w_up", "conv_f_w",
           "conv_f_b", "w_down", "final_norm_w"]
IN_SPLITS = [("z", 0, 2048), ("xbc", 2048, 5120), ("dt", 5120, 5152), ("uv", 5152, 7200), ("gates", 7200, 9248)]
IN_ORDER = ["z", "uv", "gates", "xbc"]
IN_BLOCK = {"z": 0, "uv": 1, "gates": 2, "xbc": 2}


def _in_rows(flat):
    span = {n: (lo, hi) for n, lo, hi in IN_SPLITS}
    return jnp.concatenate([flat[slice(*span[n])] for n in IN_ORDER], axis=0), flat[slice(*span["dt"])]


def _canonical_rows(main, dt):
    span = {n: (lo, hi) for n, lo, hi in IN_SPLITS}
    part, off = {"dt": dt[:span["dt"][1] - span["dt"][0]]}, 0
    for n in IN_ORDER:
        width = span[n][1] - span[n][0]
        part[n] = main[off:off + width]
        off += width
    return jnp.concatenate([part[n] for n, _, _ in IN_SPLITS], axis=0)


def _columns_from_devices(a):
    return a.transpose(1, 0, 2).reshape(a.shape[1], -1)


def kernel(x, norm1_w, w_in, b_gate, conv_a_w, conv_a_b, dt_bias, a_log, d_skip, ssd_norm_w, uv_b, v_ln_w, v_ln_b, w_spatial, b_spatial, w_branch, w_out, norm2_w, w_up, conv_f_w, conv_f_b, w_down, final_norm_w, loss_target, m_norm1_w, m_w_in, m_b_gate, m_conv_a_w, m_conv_a_b, m_dt_bias, m_a_log, m_d_skip, m_ssd_norm_w, m_uv_b, m_v_ln_w, m_v_ln_b, m_w_spatial, m_b_spatial, m_w_branch, m_w_out, m_norm2_w, m_w_up, m_conv_f_w, m_conv_f_b, m_w_down, m_final_norm_w, v_norm1_w, v_w_in, v_b_gate, v_conv_a_w, v_conv_a_b, v_dt_bias, v_a_log, v_d_skip, v_ssd_norm_w, v_uv_b, v_v_ln_w, v_v_ln_b, v_w_spatial, v_b_spatial, v_w_branch, v_w_out, v_norm2_w, v_w_up, v_conv_f_w, v_conv_f_b, v_w_down, v_final_norm_w):
    args = dict(locals())
    wts = {n: args[n] for n in WEIGHTS}
    mom = {n: args["m_" + n] for n in WEIGHTS}
    var = {n: args["v_" + n] for n in WEIGHTS}
    cx, cy, cc = _place()
    dev = 4 * cx + 2 * cy + cc
    place = jnp.stack([cc, 2 * cx + cy]).astype(jnp.int32)

    def shard2d(n, a):
        return a[0].T if n in TRANSPOSED else a[0]

    def unshard(n, b):
        return (b.T if n in TRANSPOSED else b)[None]

    g_in, g_conv_a, g_conv_f = _all_gather(
        [shard2d("w_in", w_in).astype(BF16), conv_a_w[0], conv_f_w[0]], "gather_w_in")
    late = [shard2d(n, wts[n]).astype(BF16) for n in BIG[1:]]
    send_sems, recv_sems, late, lands, token = _gather_start(late, g_in, "gather_late_start")
    w = {}
    w["in"], dt_rows = _in_rows(g_in.reshape(-1, D_MODEL))
    w["dt"] = jnp.pad(dt_rows, ((0, DT_PAD - SSD_HEADS), (0, 0)))
    w["conv_a"] = _columns_from_devices(g_conv_a)
    w["conv_f"] = _columns_from_devices(g_conv_f)

    def late_weights(*after):
        got = _gather_wait(send_sems, recv_sems, late, lands, after, "gather_late_wait")
        g_branch, g_out, g_up, g_down = [lax.dynamic_update_index_in_dim(land, mine, dev, 0).reshape(-1, D_MODEL)
                                         for land, mine in zip(got, late)]
        return {"branch_a": g_branch[:SSD_INNER], "branch_b": g_branch[SSD_INNER:], "out": g_out, "up": g_up,
                "down": g_down}

    in_flight = {}

    def on_grad(n, g):
        part = {"w_in": lambda: _canonical_rows(g["in"], g["dt"]),
                "w_branch": lambda: jnp.concatenate([g["branch_a"], g["branch_b"]], axis=0),
                "w_out": lambda: g["out"], "w_up": lambda: g["up"], "w_down": lambda: g["down"]}[n]()
        part = part.reshape(N_DEV, -1, D_MODEL)
        from_core, = _exchange_cores([part], f"to_other_core_{n}")
        q, own = _chip_sum(part, from_core, place, f"chip_sum_{n}")
        send, recv, q, land, tok = _chips_start(q, f"to_other_chips_start_{n}")
        in_flight[n] = (own, send, recv, q, land)
        return tok

    p = {n: wts[n][0] if wts[n].ndim > 2 else wts[n].reshape(1, -1) for n in SMALL}
    small_flight = []

    def on_small(g, loss):
        arrays = [jnp.concatenate([g[n] for n in ROW_VECTORS] + [loss[:1]], axis=1), g["w_spatial"], g["b_spatial"],
                  g["conv_a"], g["conv_f"]]
        *flight, tok = _gather_start(arrays, g["conv_a"], "gather_small_start")
        small_flight.append(flight)
        return tok

    loss, gx, g = _local_step(x[0], loss_target[0], w, p, after=token, late_weights=late_weights, on_grad=on_grad,
                              on_small=on_small)
    *flight, _ = _gather_start([g["norm1_w"]], gx, "gather_norm1_start")
    small_flight.append(flight)

    grads, delta, new_m, new_v = {}, {}, {}, {}

    def big_adamw(n, after):
        own, send, recv, q, land = in_flight[n]
        got = _chips_wait(send, recv, q, land, after, f"to_other_chips_wait_{n}")
        view = (lambda a: a.transpose(2, 0, 1)) if n == "w_in" else (lambda a: shard2d(n, a))
        back = (lambda b: b.transpose(1, 2, 0)) if n == "w_in" else (lambda b: unshard(n, b))
        out = _sum_adamw(own, got, view(wts[n]), view(mom[n]), view(var[n]), f"adamw_{n}")
        grads[n], delta[n], new_m[n], new_v[n] = [back(o) for o in out]
        return out[1]

    after = gx
    for n in ("w_down", "w_up", "w_out", "w_branch", "w_in"):
        after = big_adamw(n, after)
    gathered = []
    for (send, recv, mine, land), name in zip(small_flight, ("gather_small_wait", "gather_norm1_wait")):
        got = _gather_wait(send, recv, mine, land, [after], name)
        gathered += [lax.dynamic_update_index_in_dim(full, own, dev, 0) for full, own in zip(got, mine)]
    small = [{n: t[n][0] if t[n].ndim > 2 else t[n].reshape(1, -1) for n in SMALL_ORDER} for t in (wts, mom, var)]
    *outs, loss = _small_adamw(gathered, *small)
    for tgt, out in zip((grads, delta, new_m, new_v), outs):
        tgt.update({n: out[n].reshape(wts[n].shape) for n in SMALL_ORDER})
    loss = loss[0, 0]

    return (loss, gx[None], *[grads[n] for n in WEIGHTS], *[delta[n] for n in WEIGHTS],
            *[new_m[n] for n in WEIGHTS], *[new_v[n] for n in WEIGHTS])
```

```python
import functools

import jax
import jax.numpy as jnp
from jax import lax
from jax.experimental import pallas as pl
from jax.experimental.pallas import tpu as pltpu

F32, BF16 = jnp.float32, jnp.bfloat16
HIGHEST = lax.Precision.HIGHEST

D_MODEL = 1024
SSD_INNER = 2048
SSD_HEAD_DIM = 64
SSD_HEADS = 32
SSD_GROUPS = 4
SSD_STATE = 128
SSD_BC = SSD_GROUPS * SSD_STATE
SSD_XBC = SSD_INNER + 2 * SSD_BC
SSD_CONV = 4
CHUNK = 128
N_PAIRS = SSD_HEADS // 2
PAIRS_PER_GROUP = N_PAIRS // SSD_GROUPS
SGU_WIDTH = 1024
SGU_GROUPS = 8
D_FF = 2816
FFN_CONV = 3
NORM_EPS = 1e-6
LN_EPS = 1e-5
LANES = 128
DT_PAD = LANES

ADAM_LR, ADAM_B1, ADAM_B2, ADAM_EPS, ADAM_WD, ADAM_STEP = 0.001, 0.9, 0.999, 1e-08, 0.01, 10

N_DEV = 8
VMEM_LIMIT = 56 * 1024 * 1024
MESH = pl.DeviceIdType.MESH


def _params(n_grid, **kw):
    sem = dict(dimension_semantics=("arbitrary",) * n_grid) if n_grid else {}
    return pltpu.CompilerParams(vmem_limit_bytes=VMEM_LIMIT, **sem, **kw)


def _tile(n, pref):
    t = (min(pref, n) // LANES) * LANES
    while n % t:
        t -= LANES
    return t


def _row_tile(r, pref):
    for t in range(min(pref, r) // 16 * 16, 0, -16):
        if r % t == 0:
            return t
    return r


def _tile2d(r, c, rows):
    if r % 16 == 0:
        return _row_tile(r, rows), c
    return r, _tile(c, 2 * LANES)


def _rows(tm, n, nt=None, rev=False, col=0):
    if rev:
        return pl.BlockSpec((tm, n), lambda i: (nt - 1 - i, col))
    return pl.BlockSpec((tm, n), lambda i: (i, col))


def _halo(tm, n, nt=None, rev=False, col=0):
    per = tm // 8
    if rev:
        return pl.BlockSpec((8, n), lambda i: (jnp.maximum((nt - 1 - i) * per - 1, 0), col))
    return pl.BlockSpec((8, n), lambda i: (jnp.maximum(i * per - 1, 0), col))


def _into(into, in_index, out_index):
    if into is None:
        return [], [], {}
    return [into], [pl.BlockSpec(memory_space=pl.ANY)], dict(input_output_aliases={in_index: out_index})


def _full(shape):
    nd = len(shape)
    return pl.BlockSpec(shape, lambda *_: (0,) * nd)


def _rms(x, w, eps=NORM_EPS):
    return x * lax.rsqrt(jnp.mean(x * x, axis=-1, keepdims=True) + eps) * w


def _layer_norm(x, w, b):
    mu = jnp.mean(x, axis=-1, keepdims=True)
    var = jnp.mean(jnp.square(x - mu), axis=-1, keepdims=True)
    return (x - mu) * lax.rsqrt(var + LN_EPS) * w + b


def _sigmoid(x):
    return 1.0 / (1.0 + jnp.exp(-x))


def _silu(x):
    return x * _sigmoid(x)


def _dsilu(x):
    s = _sigmoid(x)
    return s * (1.0 + x * (1.0 - s))


def _softplus(x):
    return jnp.maximum(x, 0.0) + jnp.log(1.0 + jnp.exp(-jnp.abs(x)))


def _gelu(x):
    return jax.nn.gelu(x)


def _dot(a, b):
    return jnp.dot(a, b, preferred_element_type=F32)


def _dot_nt(a, b):
    return lax.dot_general(a, b, (((1,), (1,)), ((), ())), preferred_element_type=F32)


def _dot_tn(a, b):
    return lax.dot_general(a, b, (((0,), (0,)), ((), ())), preferred_element_type=F32)


def _dot_split(p, e):
    hi = p.astype(BF16)
    lo = (p - hi.astype(F32)).astype(BF16)
    return _dot(hi, e) + _dot(lo, e)


def _colsum(x):
    return jnp.sum(x, axis=0, keepdims=True)


def _shift_down(x, halo, j):
    xs = pltpu.roll(x, j, 0)
    hs = pltpu.roll(halo, j, 0)
    r8 = lax.broadcasted_iota(jnp.int32, hs.shape, 0)
    return jnp.concatenate([jnp.where(r8 < j, hs, xs[:8]), xs[8:]], axis=0)


def _shift_up(x, nxt, j):
    n = x.shape[0]
    xs = pltpu.roll(x, n - j, 0)
    ns = pltpu.roll(nxt, 8 - j, 0)
    r8 = lax.broadcasted_iota(jnp.int32, ns.shape, 0)
    return jnp.concatenate([xs[:n - 8], jnp.where(r8 >= 8 - j, ns, xs[n - 8:])], axis=0)


def _causal_conv(x, halo, w, b):
    k = w.shape[0]
    y = b + w[k - 1:k, :] * x
    for j in range(1, k):
        y = y + w[k - 1 - j:k - j, :] * _shift_down(x, halo, j)
    return y


def _causal_conv_bwd(dy, nxt, x, w):
    k = w.shape[0]
    dx = w[k - 1:k, :] * dy
    dw = [_colsum(dy * x)]
    for j in range(1, k):
        dyj = _shift_up(dy, nxt, j)
        dx = dx + w[k - 1 - j:k - j, :] * dyj
        dw.append(_colsum(dyj * x))
    return dx, jnp.concatenate(dw[::-1], axis=0)


MM_TILE_PREF = 1408
MM_VMEM_BUDGET = 40 * 1024 * 1024
MM_WHOLE_K = 6144


def _mm_tiles(m, n, k, out_bytes):
    tm, tn = _tile(m, MM_TILE_PREF), _tile(n, MM_TILE_PREF)
    need = lambda tm, tn: 2 * (2 * k * (tm + tn) + out_bytes * tm * tn)
    while need(tm, tn) > MM_VMEM_BUDGET:
        if tn >= tm and tn > LANES:
            tn = _tile(n, tn - LANES)
        else:
            tm = _tile(m, tm - LANES)
    return tm, tn


def _mm(a, b, dims, name, acc=None, out_dtype=F32, after=None):
    if dims == "tn":
        k, m = a.shape
    else:
        m, k = a.shape
    n = b.shape[0] if dims == "nt" else b.shape[1]
    tm, tn = _mm_tiles(m, n, k, 4 * (2 if acc is not None else 1))
    a_spec = pl.BlockSpec((k, tm), lambda j, i: (0, i)) if dims == "tn" else pl.BlockSpec((tm, k), lambda j, i: (i, 0))
    b_spec = pl.BlockSpec((tn, k), lambda j, i: (j, 0)) if dims == "nt" else pl.BlockSpec((k, tn), lambda j, i: (0, j))
    o_spec = pl.BlockSpec((tm, tn), lambda j, i: (i, j))
    dot = {"nn": _dot, "nt": _dot_nt, "tn": _dot_tn}[dims]

    def body(a_ref, b_ref, *rest):
        r = dot(a_ref[...], b_ref[...])
        if acc is not None:
            r = r + rest[0][...]
        rest[-1][...] = r.astype(out_dtype)

    ins, specs = [a, b], [a_spec, b_spec]
    if acc is not None:
        ins.append(acc)
        specs.append(o_spec)
    if after is not None:
        ins.append(after)
        specs.append(pl.BlockSpec(memory_space=pl.ANY))
    return pl.pallas_call(
        body, name=name, grid=(n // tn, m // tm), in_specs=specs, out_specs=o_spec,
        out_shape=jax.ShapeDtypeStruct((m, n), out_dtype), compiler_params=_params(2),
    )(*ins)


def _mm_rows(a, b, dims, name, fn, rows=(), fulls=(), row_outs=(), acc_outs=(), after=None):
    m, k = a.shape
    n = b.shape[0] if dims == "nt" else b.shape[1]
    tk = k if (dims == "nt" or k <= MM_WHOLE_K) else _tile(k, 1024)
    nk = k // tk
    per_row = (2 * tk + 8 * n + sum(4 * r.shape[1] for r in rows)
               + sum(c * jnp.dtype(d).itemsize for c, d in row_outs))
    tm = _tile(m, 1024)
    while 2 * tm * per_row + 4 * tk * n > MM_VMEM_BUDGET:
        tm = _tile(m, tm - LANES)
    dot = _dot_nt if dims == "nt" else _dot
    n_in = 2 + len(rows) + len(fulls) + (after is not None)
    n_out = len(row_outs) + len(acc_outs)

    def body(*refs):
        ins, outs, scratch = refs[:n_in], refs[n_in:n_in + n_out], refs[n_in + n_out:]
        row_refs, acc_refs = outs[:len(row_outs)], outs[len(row_outs):]
        step = pl.program_id(1)

        @pl.when(jnp.logical_and(pl.program_id(0) == 0, step == 0))
        def _():
            for r in acc_refs:
                r[...] = jnp.zeros_like(r)

        part = dot(ins[0][...], ins[1][...])
        if nk > 1:
            part_ref, = scratch

            @pl.when(step == 0)
            def _():
                part_ref[...] = part

            @pl.when(step > 0)
            def _():
                part_ref[...] += part

        @pl.when(step == nk - 1)
        def _():
            result = part_ref[...] if nk > 1 else part
            new_rows, incs = fn(result, *[r[...] for r in ins[2:2 + len(rows) + len(fulls)]])
            for r, val in zip(row_refs, new_rows):
                r[...] = val.astype(r.dtype)
            for r, inc in zip(acc_refs, incs):
                r[...] += inc

    tile_rows = lambda c: pl.BlockSpec((tm, c), lambda i, s: (i, 0))
    b_spec = pl.BlockSpec((tk, n), lambda i, s: (s, 0)) if dims == "nn" else _full(b.shape)
    extra, extra_specs = ([after], [pl.BlockSpec(memory_space=pl.ANY)]) if after is not None else ([], [])
    return pl.pallas_call(
        body, name=name, grid=(m // tm, nk),
        in_specs=[pl.BlockSpec((tm, tk), lambda i, s: (i, s)), b_spec] + [tile_rows(r.shape[1]) for r in rows]
        + [_full(f.shape) for f in fulls] + extra_specs,
        out_specs=[tile_rows(c) for c, _ in row_outs] + [_full(s) for s in acc_outs],
        out_shape=[jax.ShapeDtypeStruct((m, c), d) for c, d in row_outs]
        + [jax.ShapeDtypeStruct(s, F32) for s in acc_outs],
        scratch_shapes=[pltpu.VMEM((tm, n), F32)] if nk > 1 else [],
        compiler_params=_params(2),
    )(a, b, *rows, *fulls, *extra)


def _residual_norm(o, x, w):
    h = x + o
    return (h, _rms(h, w)), ()


def _norm_backward(dn, h, dres, w):
    _, vjp = jax.vjp(_rms, h, w)
    dh, dw = vjp(dn)
    dh = dh + dres
    return (dh, dh), (dw,)


def _loss_and_grad(dn, h1, target, w):
    yf, vjp = jax.vjp(_rms, h1 + dn, w)
    err = yf - target
    loss = 0.5 * jnp.sum(jnp.mean(err * err, axis=-1, keepdims=True))
    dh, dw = vjp(err * (1.0 / err.shape[-1]))
    return (dh, dh), (jnp.full((8, LANES), loss, F32), dw)


def _wgrad(a, d, name, after=None):
    return _mm(a, d, "tn", name, out_dtype=BF16, after=after)


def _norm_fwd(x, w, name, after=None, tm=512):
    t, d = x.shape

    def body(x_ref, w_ref, *rest):
        rest[-1][...] = _rms(x_ref[...], w_ref[...]).astype(BF16)

    extra, extra_specs = ([after], [_full(after.shape)]) if after is not None else ([], [])
    return pl.pallas_call(
        body, name=name, grid=(t // tm,), in_specs=[_rows(tm, d), _full((1, d))] + extra_specs,
        out_specs=_rows(tm, d), out_shape=jax.ShapeDtypeStruct((t, d), BF16), compiler_params=_params(1),
    )(x, w, *extra)


def _conv_a_fwd(xbc, cw, cb, tm=256, col=0):
    t, c = xbc.shape[0], cw.shape[1]

    def body(x_ref, h_ref, w_ref, b_ref, o_ref, y_ref):
        halo = jnp.where(pl.program_id(0) > 0, h_ref[...], 0.0)
        y = _causal_conv(x_ref[...], halo, w_ref[...], b_ref[...])
        y_ref[...] = y
        o_ref[...] = _silu(y)

    return pl.pallas_call(
        body, name="conv_a_fwd", grid=(t // tm,),
        in_specs=[_rows(tm, c, col=col), _halo(tm, c, col=col), _full(cw.shape), _full((1, c))],
        out_specs=[_rows(tm, c)] * 2, out_shape=[jax.ShapeDtypeStruct((t, c), F32)] * 2, compiler_params=_params(1),
    )(xbc, xbc, cw, cb)


def _ssd_common(dtr, dtb, alog, e_t):
    row = lax.broadcasted_iota(jnp.int32, (CHUNK, CHUNK), 0)
    col = lax.broadcasted_iota(jnp.int32, (CHUNK, CHUNK), 1)
    causal = row >= col
    dt = _softplus(dtr + dtb)
    a = -jnp.exp(alog)
    acum = jnp.dot(causal.astype(F32), dt * a, precision=HIGHEST, preferred_element_type=F32)
    spread = lambda v: _dot_split(v, e_t)
    return dict(dt=dt, a=a, acum=acum, acum_t=acum.T, causal=causal, row=row, col=col, lane_lo=col < SSD_HEAD_DIM,
                dt_x=spread(dt), ecol_x=spread(jnp.exp(acum)), dsr_x=spread(jnp.exp(acum[CHUNK - 1:CHUNK, :] - acum)))


def _head_decay(c, h, transposed=False):
    d = c["acum"][:, h:h + 1] - c["acum_t"][h:h + 1, :]
    if transposed:
        return jnp.exp(jnp.where(c["row"] <= c["col"], -d, -jnp.inf))
    return jnp.exp(jnp.where(c["causal"], d, -jnp.inf))


def _ssd_fwd(xc, dtr, z, dtb, alog, dsk, nw, e_t, z_col=0):
    t = xc.shape[0]
    nc = t // CHUNK

    def body(xs_ref, b_ref, c_ref, dtr_ref, z_ref, dtb_ref, alog_ref, dsk_ref, nw_ref, et_ref,
             y_ref, ya_ref, sp_ref, s_scr):
        @pl.when(pl.program_id(0) == 0)
        def _():
            s_scr[...] = jnp.zeros_like(s_scr)

        c = _ssd_common(dtr_ref[...], dtb_ref[...], alog_ref[...], et_ref[...])
        lane_lo = c["lane_lo"]
        dsk = dsk_ref[...]
        for g in range(SSD_GROUPS):
            gs = slice(g * SSD_STATE, (g + 1) * SSD_STATE)
            bg_t, cg = b_ref[:, gs].T.astype(BF16), c_ref[:, gs].astype(BF16)
            cb = _dot(cg, bg_t)
            for pp in range(PAIRS_PER_GROUP):
                j = g * PAIRS_PER_GROUP + pp
                ps = slice(j * LANES, (j + 1) * LANES)
                x = xs_ref[:, ps]
                ecol, dsr = c["ecol_x"][:, ps], c["dsr_x"][:, ps]
                xdt = x * c["dt_x"][:, ps]
                xb = xdt.astype(BF16)
                zero = jnp.zeros_like(xb)
                yd = (_dot((cb * _head_decay(c, 2 * j)).astype(BF16), jnp.where(lane_lo, xb, zero))
                      + _dot((cb * _head_decay(c, 2 * j + 1)).astype(BF16), jnp.where(lane_lo, zero, xb)))
                sp = s_scr[j]
                yo = ecol * _dot(cg, sp.astype(BF16))
                st = _dot(bg_t, (xdt * dsr).astype(BF16))
                sp_ref[0, j] = sp
                s_scr[j] = ecol[CHUNK - 1:CHUNK] * sp + st
                dskp = jnp.where(lane_lo[0:1], dsk[:, 2 * j:2 * j + 1], dsk[:, 2 * j + 1:2 * j + 2])
                y_ref[:, ps] = yd + yo + dskp * x
        ya_ref[...] = _rms(y_ref[...] * _silu(z_ref[...]), nw_ref[...]).astype(BF16)

    ck = lambda n, col=0: pl.BlockSpec((CHUNK, n), lambda c: (c, col))
    return pl.pallas_call(
        body, name="ssd_fwd", grid=(nc,),
        in_specs=[ck(SSD_INNER), ck(SSD_BC, SSD_INNER // SSD_BC), ck(SSD_BC, SSD_INNER // SSD_BC + 1), ck(DT_PAD),
                  ck(SSD_INNER, z_col), _full((1, DT_PAD)), _full((1, DT_PAD)), _full((1, DT_PAD)),
                  _full((1, SSD_INNER)), _full(e_t.shape)],
        out_specs=[ck(SSD_INNER), ck(SSD_INNER),
                   pl.BlockSpec((1, N_PAIRS, SSD_STATE, LANES), lambda c: (c, 0, 0, 0))],
        out_shape=[jax.ShapeDtypeStruct((t, SSD_INNER), F32), jax.ShapeDtypeStruct((t, SSD_INNER), BF16),
                   jax.ShapeDtypeStruct((nc, N_PAIRS, SSD_STATE, LANES), F32)],
        scratch_shapes=[pltpu.VMEM((N_PAIRS, SSD_STATE, LANES), F32)], compiler_params=_params(1),
    )(xc, xc, xc, dtr, z, dtb, alog, dsk, nw, e_t)


def _ssd_bwd(dya, y, z, xc, dtr, sprev, dtb, alog, dsk, nw, e_heads, e_t, z_col=0, into=None):
    t = xc.shape[0]
    nc = t // CHUNK
    more, more_specs, alias = _into(into, 14, 0)

    def body(dya_ref, y_ref, z_ref, xs_ref, b_ref, c_ref, dtr_ref, sp_ref, dtb_ref, alog_ref, dsk_ref, nw_ref, e_ref,
             et_ref, *rest):
        dz_ref, dxs_ref, db_ref, dc_ref, ddtr_ref, dnw_ref, ddtb_ref, dalog_ref, ddsk_ref, ds_scr = rest[len(more):]

        @pl.when(pl.program_id(0) == 0)
        def _():
            ds_scr[...] = jnp.zeros_like(ds_scr)
            for r in (dnw_ref, ddtb_ref, dalog_ref, ddsk_ref):
                r[...] = jnp.zeros_like(r)

        y = y_ref[...]
        _, gate_vjp = jax.vjp(lambda y_, z_, w_: _rms(y_ * _silu(z_), w_), y, z_ref[...], nw_ref[...])
        dy, dz, dnw = gate_vjp(dya_ref[...])
        dz_ref[...] = dz.astype(BF16)
        dnw_ref[...] += dnw

        dtr = dtr_ref[...]
        c = _ssd_common(dtr, dtb_ref[...], alog_ref[...], et_ref[...])
        dt, a, lane_lo, row, col = c["dt"], c["a"], c["lane_lo"], c["row"], c["col"]
        dsk = dsk_ref[...]
        p_a, p_dt, v_last = [], [], []
        da_cols = jnp.zeros((CHUNK, CHUNK), F32)
        da_rows = jnp.zeros((CHUNK, CHUNK), F32)
        for g in range(SSD_GROUPS):
            gs = slice(g * SSD_STATE, (g + 1) * SSD_STATE)
            bg, cg = b_ref[:, gs].astype(BF16), c_ref[:, gs].astype(BF16)
            bg_t, cg_t = b_ref[:, gs].T.astype(BF16), c_ref[:, gs].T.astype(BF16)
            cb, cb_t = _dot(cg, bg_t), _dot(bg, cg_t)
            dcb = jnp.zeros((CHUNK, CHUNK), F32)
            dbg = jnp.zeros((CHUNK, SSD_STATE), F32)
            dcg = jnp.zeros((CHUNK, SSD_STATE), F32)
            for pp in range(PAIRS_PER_GROUP):
                j = g * PAIRS_PER_GROUP + pp
                ps = slice(j * LANES, (j + 1) * LANES)
                x = xs_ref[:, ps]
                dtp, ecol, dsr = c["dt_x"][:, ps], c["ecol_x"][:, ps], c["dsr_x"][:, ps]
                elast = ecol[CHUNK - 1:CHUNK]
                xdt = x * dtp
                xb = xdt.astype(BF16)
                dskp = jnp.where(lane_lo[0:1], dsk[:, 2 * j:2 * j + 1], dsk[:, 2 * j + 1:2 * j + 2])
                dyp = dy[:, ps]
                dyb = dyp.astype(BF16)
                sp, dsn = sp_ref[0, j], ds_scr[j]
                spb, dsnb = sp.astype(BF16), dsn.astype(BF16)
                y_off = ecol * _dot(cg, spb)
                dw = (dyp * ecol).astype(BF16)
                dcg = dcg + _dot_nt(dw, spb)
                dsp = _dot(cg_t, dw) + elast * dsn
                xd = xdt * dsr
                zd = _dot(bg, dsnb) * dsr
                dbg = dbg + _dot_nt(xd.astype(BF16), dsnb)
                dxdt = zd
                zero = jnp.zeros_like(xb)
                for h, lm in ((2 * j, lane_lo), (2 * j + 1, jnp.logical_not(lane_lo))):
                    le = _head_decay(c, h)
                    dm = _dot_nt(jnp.where(lm, dyb, zero), jnp.where(lm, xb, zero))
                    dcb = dcb + dm * le
                    m = cb * le
                    m_t = (cb_t * _head_decay(c, h, transposed=True)).astype(BF16)
                    dxdt = dxdt + jnp.where(lm, _dot(m_t, dyb), 0.0)
                    q = dm * m
                    da_cols = da_cols + jnp.where(col == h, jnp.sum(q, axis=1, keepdims=True), 0.0)
                    da_rows = da_rows + jnp.where(row == h, _colsum(q), 0.0)
                ds_scr[j] = dsp
                dxs_ref[:, ps] = dxdt * dtp + dskp * dyp
                p_a.append(dyp * y_off - xdt * zd)
                p_dt.append(dxdt * x)
                v_last.append(_colsum(zd * xdt) + elast * _colsum(dsn * sp))
            dcbb = dcb.astype(BF16)
            db_ref[:, gs] = dbg + _dot_tn(dcbb, cg)
            dc_ref[:, gs] = dcg + _dot(dcbb, bg)
        e = e_ref[...]
        rows8 = jnp.concatenate([jnp.concatenate(v_last, axis=1), _colsum(dy * xs_ref[...]),
                                 jnp.zeros((6, SSD_INNER), F32)], axis=0)
        r8 = _dot_split(rows8, e)
        da = (_dot_split(jnp.concatenate(p_a, axis=1), e) + jnp.where(row == CHUNK - 1, r8[0:1], 0.0)
              + da_cols - da_rows.T)
        ddsk_ref[...] += r8[1:2]
        dadt = jnp.dot((row <= col).astype(F32), da, precision=HIGHEST, preferred_element_type=F32)
        ddt = dadt * a + _dot_split(jnp.concatenate(p_dt, axis=1), e)
        dalog_ref[...] += _colsum(dadt * dt) * a
        ddtr = ddt * _sigmoid(dtr + dtb_ref[...])
        ddtr_ref[...] = ddtr
        ddtb_ref[...] += _colsum(ddtr)

    ck = lambda n, col=0: pl.BlockSpec((CHUNK, n), lambda c: (nc - 1 - c, col))
    acc = lambda n: _full((1, n))
    return pl.pallas_call(
        body, name="ssd_bwd", grid=(nc,),
        in_specs=[ck(SSD_INNER), ck(SSD_INNER), ck(SSD_INNER, z_col), ck(SSD_INNER), ck(SSD_BC, SSD_INNER // SSD_BC),
                  ck(SSD_BC, SSD_INNER // SSD_BC + 1), ck(DT_PAD),
                  pl.BlockSpec((1, N_PAIRS, SSD_STATE, LANES), lambda c: (nc - 1 - c, 0, 0, 0)),
                  acc(DT_PAD), acc(DT_PAD), acc(DT_PAD), acc(SSD_INNER), _full((SSD_INNER, LANES)),
                  _full((LANES, SSD_INNER))] + more_specs,
        out_specs=[ck(SSD_INNER, z_col if into is not None else 0), ck(SSD_INNER), ck(SSD_BC), ck(SSD_BC), ck(DT_PAD),
                   acc(SSD_INNER), acc(DT_PAD), acc(DT_PAD), acc(DT_PAD)],
        out_shape=[jax.ShapeDtypeStruct(into.shape if into is not None else (t, SSD_INNER), BF16),
                   jax.ShapeDtypeStruct((t, SSD_INNER), F32),
                   jax.ShapeDtypeStruct((t, SSD_BC), F32), jax.ShapeDtypeStruct((t, SSD_BC), F32),
                   jax.ShapeDtypeStruct((t, DT_PAD), F32), jax.ShapeDtypeStruct((1, SSD_INNER), F32),
                   jax.ShapeDtypeStruct((1, DT_PAD), F32), jax.ShapeDtypeStruct((1, DT_PAD), F32),
                   jax.ShapeDtypeStruct((1, DT_PAD), F32)],
        scratch_shapes=[pltpu.VMEM((N_PAIRS, SSD_STATE, LANES), F32)], compiler_params=_params(1), **alias,
    )(dya, y, z, xc, xc, xc, dtr, sprev, dtb, alog, dsk, nw, e_heads, e_t, *more)


def _sgu_act(uv, uvb, lnw, lnb):
    a = _gelu(uv + uvb)
    return a[:, :SGU_WIDTH], _layer_norm(a[:, SGU_WIDTH:], lnw, lnb)


def _sgu_weights(ws_ref):
    row = lax.broadcasted_iota(jnp.int32, (CHUNK, CHUNK), 0)
    col = lax.broadcasted_iota(jnp.int32, (CHUNK, CHUNK), 1)
    return [jnp.where(row >= col, ws_ref[g], 0.0).astype(BF16) for g in range(SGU_GROUPS)], row >= col


def _sgu_fwd(uv, uvb, lnw, lnb, ws, bs_t, col=0):
    t = uv.shape[0]

    def body(uv_ref, uvb_ref, lnw_ref, lnb_ref, ws_ref, bs_ref, o_ref):
        u, vn = _sgu_act(uv_ref[...], uvb_ref[...], lnw_ref[...], lnb_ref[...])
        wc, _ = _sgu_weights(ws_ref)
        bs = bs_ref[...]
        for g in range(SGU_GROUPS):
            gs = slice(g * LANES, (g + 1) * LANES)
            mixed = _dot(wc[g], vn[:, gs].astype(BF16)) + bs[:, g:g + 1]
            o_ref[:, gs] = (u[:, gs] * mixed).astype(BF16)

    return pl.pallas_call(
        body, name="sgu_fwd", grid=(t // CHUNK,),
        in_specs=[_rows(CHUNK, 2 * SGU_WIDTH, col=col), _full((1, 2 * SGU_WIDTH)), _full((1, SGU_WIDTH)), _full((1, SGU_WIDTH)),
                  _full(ws.shape), _full(bs_t.shape)],
        out_specs=_rows(CHUNK, SGU_WIDTH), out_shape=jax.ShapeDtypeStruct((t, SGU_WIDTH), BF16),
        compiler_params=_params(1),
    )(uv, uvb, lnw, lnb, ws, bs_t)


def _sgu_bwd(dyb, uv, uvb, lnw, lnb, ws, bs_t, e_groups, col=0, into=None):
    t = uv.shape[0]
    more, more_specs, alias = _into(into, 8, 0)

    def body(dyb_ref, uv_ref, uvb_ref, lnw_ref, lnb_ref, ws_ref, bs_ref, e_ref, *rest):
        duv_ref, duvb_ref, dlnw_ref, dlnb_ref, dws_ref, dbs_ref = rest[len(more):]

        @pl.when(pl.program_id(0) == 0)
        def _():
            for r in (duvb_ref, dlnw_ref, dlnb_ref, dws_ref, dbs_ref):
                r[...] = jnp.zeros_like(r)

        (u, vn), act_vjp = jax.vjp(_sgu_act, uv_ref[...], uvb_ref[...], lnw_ref[...], lnb_ref[...])
        wc, causal = _sgu_weights(ws_ref)
        bs = bs_ref[...]
        dyb = dyb_ref[...]
        du, dvn, dmix = [], [], []
        for g in range(SGU_GROUPS):
            gs = slice(g * LANES, (g + 1) * LANES)
            vb = vn[:, gs].astype(BF16)
            mixed = _dot(wc[g], vb) + bs[:, g:g + 1]
            dm = dyb[:, gs] * u[:, gs]
            dmb = dm.astype(BF16)
            du.append(dyb[:, gs] * mixed)
            dvn.append(_dot_tn(wc[g], dmb))
            dws_ref[g] += jnp.where(causal, _dot_nt(dmb, vb), 0.0)
            dmix.append(dm)
        dbs_ref[...] += _dot_split(jnp.concatenate(dmix, axis=1), e_ref[...])
        duv, duvb, dlnw, dlnb = act_vjp((jnp.concatenate(du, axis=1), jnp.concatenate(dvn, axis=1)))
        duv_ref[...] = duv.astype(BF16)
        duvb_ref[...] += duvb
        dlnw_ref[...] += dlnw
        dlnb_ref[...] += dlnb

    return pl.pallas_call(
        body, name="sgu_bwd", grid=(t // CHUNK,),
        in_specs=[_rows(CHUNK, SGU_WIDTH), _rows(CHUNK, 2 * SGU_WIDTH, col=col), _full((1, 2 * SGU_WIDTH)),
                  _full((1, SGU_WIDTH)), _full((1, SGU_WIDTH)), _full(ws.shape), _full(bs_t.shape),
                  _full(e_groups.shape)] + more_specs,
        out_specs=[_rows(CHUNK, 2 * SGU_WIDTH, col=col if into is not None else 0), _full((1, 2 * SGU_WIDTH)),
                   _full((1, SGU_WIDTH)), _full((1, SGU_WIDTH)), _full(ws.shape), _full(bs_t.shape)],
        out_shape=[jax.ShapeDtypeStruct(into.shape if into is not None else (t, 2 * SGU_WIDTH), BF16),
                   jax.ShapeDtypeStruct((1, 2 * SGU_WIDTH), F32),
                   jax.ShapeDtypeStruct((1, SGU_WIDTH), F32), jax.ShapeDtypeStruct((1, SGU_WIDTH), F32),
                   jax.ShapeDtypeStruct(ws.shape, F32), jax.ShapeDtypeStruct(bs_t.shape, F32)],
        compiler_params=_params(1), **alias,
    )(dyb, uv, uvb, lnw, lnb, ws, bs_t, e_groups, *more)


def _merge(gates, pa, pb, bg):
    s = _sigmoid(gates + bg)
    return s[:, :D_MODEL] * pa + s[:, D_MODEL:] * pb


def _merge_fwd(gates, pa, pb, bg, tm=256, col=0):
    t = gates.shape[0]

    def body(g_ref, pa_ref, pb_ref, bg_ref, o_ref):
        o_ref[...] = _merge(g_ref[...], pa_ref[...], pb_ref[...], bg_ref[...]).astype(BF16)

    return pl.pallas_call(
        body, name="merge_fwd", grid=(t // tm,),
        in_specs=[_rows(tm, 2 * D_MODEL, col=col), _rows(tm, D_MODEL), _rows(tm, D_MODEL), _full((1, 2 * D_MODEL))],
        out_specs=_rows(tm, D_MODEL), out_shape=jax.ShapeDtypeStruct((t, D_MODEL), BF16), compiler_params=_params(1),
    )(gates, pa, pb, bg)


def _merge_bwd(dmix, gates, pa, pb, bg, tm=256, col=0, into=None):
    t = gates.shape[0]
    more, more_specs, alias = _into(into, 5, 0)

    def body(d_ref, g_ref, pa_ref, pb_ref, bg_ref, *rest):
        dg_ref, dpa_ref, dpb_ref, dbg_ref = rest[len(more):]

        @pl.when(pl.program_id(0) == 0)
        def _():
            dbg_ref[...] = jnp.zeros_like(dbg_ref)

        _, vjp = jax.vjp(_merge, g_ref[...], pa_ref[...], pb_ref[...], bg_ref[...])
        dg, dpa, dpb, dbg = vjp(d_ref[...])
        dg_ref[...] = dg.astype(BF16)
        dpa_ref[...] = dpa.astype(BF16)
        dpb_ref[...] = dpb.astype(BF16)
        dbg_ref[...] += dbg

    return pl.pallas_call(
        body, name="merge_bwd", grid=(t // tm,),
        in_specs=[_rows(tm, D_MODEL), _rows(tm, 2 * D_MODEL, col=col), _rows(tm, D_MODEL), _rows(tm, D_MODEL),
                  _full((1, 2 * D_MODEL))] + more_specs,
        out_specs=[_rows(tm, 2 * D_MODEL, col=col if into is not None else 0), _rows(tm, D_MODEL),
                   _rows(tm, D_MODEL), _full((1, 2 * D_MODEL))],
        out_shape=[jax.ShapeDtypeStruct(into.shape if into is not None else (t, 2 * D_MODEL), BF16),
                   jax.ShapeDtypeStruct((t, D_MODEL), BF16), jax.ShapeDtypeStruct((t, D_MODEL), BF16),
                   jax.ShapeDtypeStruct((1, 2 * D_MODEL), F32)],
        compiler_params=_params(1), **alias,
    )(dmix, gates, pa, pb, bg, *more)


def _conv_f_fwd(up, cw, cb, tm=128):
    t, c = up.shape

    def body(x_ref, h_ref, w_ref, b_ref, o_ref, y_ref):
        halo = jnp.where(pl.program_id(0) > 0, h_ref[...], 0.0)
        y = _causal_conv(x_ref[...], halo, w_ref[...], b_ref[...])
        y_ref[...] = y
        o_ref[...] = (_silu(y[:, :D_FF]) * y[:, D_FF:]).astype(BF16)

    return pl.pallas_call(
        body, name="conv_f_fwd", grid=(t // tm,),
        in_specs=[_rows(tm, c), _halo(tm, c), _full(cw.shape), _full((1, c))],
        out_specs=[_rows(tm, D_FF), _rows(tm, c)],
        out_shape=[jax.ShapeDtypeStruct((t, D_FF), BF16), jax.ShapeDtypeStruct((t, c), F32)],
        compiler_params=_params(1),
    )(up, up, cw, cb)


def _conv_f_bwd(dact, y, up, cw, tm=128):
    t, c = up.shape
    nt = t // tm

    def body(d_ref, y_ref, x_ref, w_ref, dx_ref, dw_ref, db_ref, nxt_scr):
        @pl.when(pl.program_id(0) == 0)
        def _():
            nxt_scr[...] = jnp.zeros_like(nxt_scr)
            dw_ref[...] = jnp.zeros_like(dw_ref)
            db_ref[...] = jnp.zeros_like(db_ref)

        a, v = y_ref[:, :D_FF], y_ref[:, D_FF:]
        d = d_ref[...]
        dy = jnp.concatenate([d * v * _dsilu(a), d * _silu(a)], axis=1)
        dx, dw = _causal_conv_bwd(dy, nxt_scr[...], x_ref[...], w_ref[...])
        dx_ref[...] = dx.astype(BF16)
        nxt_scr[...] = dy[:8]
        dw_ref[...] += dw
        db_ref[...] += _colsum(dy)

    return pl.pallas_call(
        body, name="conv_f_bwd", grid=(nt,),
        in_specs=[_rows(tm, D_FF, nt, True), _rows(tm, c, nt, True), _rows(tm, c, nt, True), _full(cw.shape)],
        out_specs=[_rows(tm, c, nt, True), _full(cw.shape), _full((1, c))],
        out_shape=[jax.ShapeDtypeStruct((t, c), BF16), jax.ShapeDtypeStruct(cw.shape, F32),
                   jax.ShapeDtypeStruct((1, c), F32)],
        scratch_shapes=[pltpu.VMEM((8, c), F32)], compiler_params=_params(1),
    )(dact, y, up, cw)


def _conv_a_bwd(dxs, db, dc, y, xbc, cw, tm=256, col=0, into=None):
    t, c = xbc.shape[0], cw.shape[1]
    nt = t // tm
    more, more_specs, alias = _into(into, 6, 0)

    def body(dxs_ref, db_ref, dc_ref, y_ref, x_ref, w_ref, *rest):
        dx_ref, dw_ref, dbias_ref, nxt_scr = rest[len(more):]

        @pl.when(pl.program_id(0) == 0)
        def _():
            nxt_scr[...] = jnp.zeros_like(nxt_scr)
            dw_ref[...] = jnp.zeros_like(dw_ref)
            dbias_ref[...] = jnp.zeros_like(dbias_ref)

        dy = jnp.concatenate([dxs_ref[...], db_ref[...], dc_ref[...]], axis=1) * _dsilu(y_ref[...])
        dx, dw = _causal_conv_bwd(dy, nxt_scr[...], x_ref[...], w_ref[...])
        dx_ref[...] = dx.astype(BF16)
        nxt_scr[...] = dy[:8]
        dw_ref[...] += dw
        dbias_ref[...] += _colsum(dy)

    return pl.pallas_call(
        body, name="conv_a_bwd", grid=(nt,),
        in_specs=[_rows(tm, SSD_INNER, nt, True), _rows(tm, SSD_BC, nt, True), _rows(tm, SSD_BC, nt, True),
                  _rows(tm, c, nt, True), _rows(tm, c, nt, True, col), _full(cw.shape)] + more_specs,
        out_specs=[_rows(tm, c, nt, True, col if into is not None else 0), _full(cw.shape), _full((1, c))],
        out_shape=[jax.ShapeDtypeStruct(into.shape if into is not None else (t, c), BF16),
                   jax.ShapeDtypeStruct(cw.shape, F32), jax.ShapeDtypeStruct((1, c), F32)],
        scratch_shapes=[pltpu.VMEM((8, c), F32)], compiler_params=_params(1), **alias,
    )(dxs, db, dc, y, xbc, cw, *more)


def _pad_lanes(v, n=DT_PAD):
    return jnp.pad(v, ((0, 0), (0, n - v.shape[1])))


def _local_step(x, target, w, p, after=None, late_weights=None, on_grad=None, on_small=None):
    dtb, alog, dsk = _pad_lanes(p["dt_bias"]), _pad_lanes(p["a_log"]), _pad_lanes(p["d_skip"])
    bs_t = _pad_lanes(p["b_spatial"].T)
    e_heads = (jnp.arange(SSD_INNER)[:, None] // SSD_HEAD_DIM == jnp.arange(LANES)[None, :]).astype(BF16)
    e_heads_t = (jnp.arange(LANES)[:, None] == jnp.arange(SSD_INNER)[None, :] // SSD_HEAD_DIM).astype(BF16)
    e_groups = (jnp.arange(SGU_WIDTH)[:, None] // LANES == jnp.arange(LANES)[None, :]).astype(BF16)

    n1 = _norm_fwd(x, p["norm1_w"], "norm1_fwd", after=after)
    z = _mm(n1, w["z"], "nt", "proj_z")
    xbc = _mm(n1, w["xbc"], "nt", "proj_xbc")
    dtr = _mm(n1, w["dt"], "nt", "proj_dt")
    uv = _mm(n1, w["uv"], "nt", "proj_uv")
    gates = _mm(n1, w["gates"], "nt", "proj_gates")
    xc, conv_a_out = _conv_a_fwd(xbc, w["conv_a"], p["conv_a_b"])
    y, ya, sprev = _ssd_fwd(xc, dtr, z, dtb, alog, dsk, p["ssd_norm_w"], e_heads_t)
    yb = _sgu_fwd(uv, p["uv_b"], p["v_ln_w"], p["v_ln_b"], p["w_spatial"], bs_t)
    if late_weights is not None:
        w = {**w, **late_weights(ya, yb)}
    pa = _mm(ya, w["branch_a"], "nn", "branch_a")
    pb = _mm(yb, w["branch_b"], "nn", "branch_b")
    mix = _merge_fwd(gates, pa, pb, p["b_gate"])
    wide = [(D_MODEL, F32), (D_MODEL, BF16)]
    h1, n2 = _mm_rows(mix, w["out"], "nn", "out_proj", _residual_norm, rows=[x], fulls=[p["norm2_w"]], row_outs=wide)
    up = _mm(n2, w["up"], "nt", "up_proj")
    act, conv_f_out = _conv_f_fwd(up, w["conv_f"], p["conv_f_b"])
    dh2, dh2b, loss, g_final = _mm_rows(
        act, w["down"], "nn", "down_proj", _loss_and_grad, rows=[h1, target], fulls=[p["final_norm_w"]],
        row_outs=wide, acc_outs=[(8, LANES), (1, D_MODEL)])

    on_grad = on_grad or (lambda name, grads: None)
    g = {"final_norm_w": g_final}
    g["down"] = _wgrad(act, dh2b, "down_wgrad")
    tok = on_grad("w_down", g)
    dact = _mm(dh2b, w["down"], "nt", "down_dgrad", after=tok)
    dup, g["conv_f"], g["conv_f_b"] = _conv_f_bwd(dact, conv_f_out, up, w["conv_f"])
    g["up"] = _wgrad(dup, n2, "up_wgrad")
    tok = on_grad("w_up", g)
    dh1, dh1b, g["norm2_w"] = _mm_rows(
        dup, w["up"], "nn", "up_dgrad", _norm_backward, rows=[h1, dh2], fulls=[p["norm2_w"]], row_outs=wide,
        acc_outs=[(1, D_MODEL)], after=tok)
    g["out"] = _wgrad(mix, dh1b, "out_wgrad")
    tok = on_grad("w_out", g)
    dmix = _mm(dh1b, w["out"], "nt", "out_dgrad", after=tok)
    dgates, dpa, dpb, g["b_gate"] = _merge_bwd(dmix, gates, pa, pb, p["b_gate"])
    g["branch_a"] = _wgrad(ya, dpa, "branch_a_wgrad")
    g["branch_b"] = _wgrad(yb, dpb, "branch_b_wgrad")
    tok = on_grad("w_branch", g)
    dya = _mm(dpa, w["branch_a"], "nt", "branch_a_dgrad", after=tok)
    dyb = _mm(dpb, w["branch_b"], "nt", "branch_b_dgrad", after=tok)
    duv, g["uv_b"], g["v_ln_w"], g["v_ln_b"], g["w_spatial"], dbs_t = _sgu_bwd(
        dyb, uv, p["uv_b"], p["v_ln_w"], p["v_ln_b"], p["w_spatial"], bs_t, e_groups)
    g["b_spatial"] = dbs_t[:, :SGU_GROUPS].T
    dz, dxs, db, dc, ddtr, g["ssd_norm_w"], ddtb, dalog, ddsk = _ssd_bwd(
        dya, y, z, xc, dtr, sprev, dtb, alog, dsk, p["ssd_norm_w"], e_heads, e_heads_t)
    g["dt_bias"], g["a_log"], g["d_skip"] = ddtb, dalog, ddsk
    dxbc, g["conv_a"], g["conv_a_b"] = _conv_a_bwd(dxs, db, dc, conv_a_out, xbc, w["conv_a"])
    tok = on_small(g, loss) if on_small else None
    ddtrb = ddtr.astype(BF16)
    for name, d in (("z", dz), ("xbc", dxbc), ("dt", ddtrb), ("uv", duv), ("gates", dgates)):
        g[name] = _wgrad(d, n1, name + "_wgrad", after=tok)
    tok = on_grad("w_in", g)
    dn1 = _mm(dz, w["z"], "nn", "z_dgrad", after=tok)
    dn1 = _mm(dxbc, w["xbc"], "nn", "xbc_dgrad", acc=dn1)
    dn1 = _mm(ddtrb, w["dt"], "nn", "dt_dgrad", acc=dn1)
    dn1 = _mm(duv, w["uv"], "nn", "uv_dgrad", acc=dn1)
    gx, g["norm1_w"] = _mm_rows(
        dgates, w["gates"], "nn", "gates_dgrad",
        lambda r, so_far, h, dres, w_: tuple(t[:1] for t in _norm_backward(r + so_far, h, dres, w_)),
        rows=[dn1, x, dh1], fulls=[p["norm1_w"]], row_outs=wide[:1], acc_outs=[(1, D_MODEL)])
    return loss, gx, g


def _place():
    return lax.axis_index("x"), lax.axis_index("y"), lax.axis_index("c")


def _other_chips(x, y):
    return [(1 - x, y), (x, 1 - y), (1 - x, 1 - y)]


def _all_gather(shards, name):
    n = len(shards)

    def body(*refs):
        ins, outs = refs[:n], refs[n:2 * n]
        send_sems, recv_sems, local_sems = refs[2 * n:]
        x, y, c = _place()
        me, sibling = (x, y, c), (x, y, 1 - c)
        chips = _other_chips(x, y)

        def copy(a, k, block, to, src=None):
            slot = outs[a].at[4 * block[0] + 2 * block[1] + block[2]]
            return pltpu.make_async_remote_copy(
                src_ref=slot if src is None else src, dst_ref=slot, send_sem=send_sems.at[7 * a + k],
                recv_sem=recv_sems.at[7 * a + k], device_id=to, device_id_type=MESH)

        started = []
        for a in range(n):
            mine = pltpu.make_async_copy(ins[a], outs[a].at[4 * x + 2 * y + c], local_sems.at[a])
            mine.start()
            started.append(mine)
        sends = []
        for a in range(n):
            sends.append(copy(a, 0, me, sibling, src=ins[a]))
            sends += [copy(a, 1 + j, me, (*chip, c), src=ins[a]) for j, chip in enumerate(chips)]
        for cp in sends:
            cp.start()
        for a in range(n):
            for j, chip in enumerate(chips):
                copy(a, 1 + j, (*chip, c), me).wait_recv()
                fwd = copy(a, 4 + j, (*chip, c), sibling)
                fwd.start()
                sends.append(fwd)
        for a in range(n):
            copy(a, 0, sibling, me).wait_recv()
            for j, chip in enumerate(chips):
                copy(a, 4 + j, (*chip, 1 - c), me).wait_recv()
        for cp in sends:
            cp.wait_send()
        for mine in started:
            mine.wait()

    any_spec = pl.BlockSpec(memory_space=pl.ANY)
    return pl.pallas_call(
        body, name=name, in_specs=[any_spec] * n, out_specs=[any_spec] * n,
        out_shape=[jax.ShapeDtypeStruct((N_DEV, *s.shape), s.dtype) for s in shards],
        scratch_shapes=[pltpu.SemaphoreType.DMA((7 * n,)), pltpu.SemaphoreType.DMA((7 * n,)),
                        pltpu.SemaphoreType.DMA((n,))],
    )(*shards)


HBM_SPEC = pl.BlockSpec(memory_space=pltpu.HBM)
SEM_SPEC = pl.BlockSpec(memory_space=pltpu.SEMAPHORE)
ANY_SPEC = pl.BlockSpec(memory_space=pl.ANY)
DATAFLOW = pltpu.SideEffectType.DATAFLOW_SIDE_EFFECTING
N_PEERS = N_DEV - 1


def _peers(x, y, c):
    out = []
    for r in range(1, N_DEV):
        fx, fy, fc = r >> 2 & 1, r >> 1 & 1, r & 1
        out.append(((1 - x) if fx else x, (1 - y) if fy else y, (1 - c) if fc else c))
    return out


def _gather_copies(srcs, lands, send_sems, recv_sems, sending, scatter=False):
    x, y, c = _place()
    copies = []
    for a, (src, land) in enumerate(zip(srcs, lands)):
        for j, (px, py, pc) in enumerate(_peers(x, y, c)):
            mine, theirs = 4 * x + 2 * y + c, 4 * px + 2 * py + pc
            block = src.at[theirs if sending else 0] if scatter else src
            copies.append(pltpu.make_async_remote_copy(
                src_ref=block, dst_ref=land.at[mine if sending else theirs], send_sem=send_sems.at[N_PEERS * a + j],
                recv_sem=recv_sems.at[N_PEERS * a + j], device_id=(px, py, pc), device_id_type=MESH))
    return copies


def _gather_start(shards, after, name, scatter=False):
    n = len(shards)

    def body(*refs):
        srcs, lands = refs[:n], refs[n:2 * n]
        send_sems, recv_sems = refs[2 * n + 1:2 * n + 3]
        token = refs[-1]
        for cp in _gather_copies(srcs, lands, send_sems, recv_sems, sending=True, scatter=scatter):
            cp.start()
        token[...] = jnp.zeros_like(token)

    lands = [lax.empty(s.shape if scatter else (N_DEV, *s.shape), s.dtype) for s in shards]
    hbm = lambda a: pltpu.with_memory_space_constraint(a, pltpu.HBM)
    out = pl.pallas_call(
        body, name=name,
        out_shape=(pltpu.SemaphoreType.DMA((N_PEERS * n,)), pltpu.SemaphoreType.DMA((N_PEERS * n,)),
                   *[pltpu.HBM(a.shape, a.dtype) for a in (*shards, *lands)], jax.ShapeDtypeStruct((8, LANES), F32)),
        in_specs=[HBM_SPEC] * (2 * n) + [ANY_SPEC],
        out_specs=(SEM_SPEC, SEM_SPEC, *[HBM_SPEC] * (2 * n), pl.BlockSpec(memory_space=pltpu.VMEM)),
        input_output_aliases={i: 2 + i for i in range(2 * n)},
        compiler_params=pltpu.CompilerParams(has_side_effects=DATAFLOW),
    )(*[hbm(a) for a in (*shards, *lands)], after)
    return out[0], out[1], out[2:2 + n], out[2 + n:2 + 2 * n], out[-1]


def _gather_wait(send_sems, recv_sems, shards, lands, after, name, scatter=False):
    n = len(shards)
    after = tuple(after)

    def body(*refs):
        srcs, lands_ = refs[:n], refs[n:2 * n]
        send, recv = refs[2 * n:2 * n + 2]
        for cp in _gather_copies(srcs, lands_, send, recv, sending=False, scatter=scatter):
            cp.wait_send()
            cp.wait_recv()

    out = pl.pallas_call(
        body, name=name, out_shape=tuple(pltpu.HBM(a.shape, a.dtype) for a in (*shards, *lands)),
        in_specs=[HBM_SPEC] * (2 * n) + [SEM_SPEC, SEM_SPEC] + [ANY_SPEC] * len(after),
        out_specs=tuple([HBM_SPEC] * (2 * n)), input_output_aliases={i: i for i in range(2 * n)},
        compiler_params=pltpu.CompilerParams(has_side_effects=DATAFLOW),
    )(*shards, *lands, send_sems, recv_sems, *after)
    return out[n:]


def _chip_copies(src, land, send_sems, recv_sems):
    x, y, c = _place()
    return [pltpu.make_async_remote_copy(
        src_ref=src.at[2 * cx + cy], dst_ref=land.at[j], send_sem=send_sems.at[j], recv_sem=recv_sems.at[j],
        device_id=(cx, cy, c), device_id_type=MESH) for j, (cx, cy) in enumerate(_other_chips(x, y))]


def _chips_start(q, name):
    def body(q_ref, land_ref, send_sems, recv_sems, q_thru, land_thru, token):
        for cp in _chip_copies(q_ref, land_ref, send_sems, recv_sems):
            cp.start()
        token[...] = jnp.zeros_like(token)

    land = lax.empty((3, *q.shape[1:]), q.dtype)
    return pl.pallas_call(
        body, name=name,
        out_shape=(pltpu.SemaphoreType.DMA((3,)), pltpu.SemaphoreType.DMA((3,)), pltpu.HBM(q.shape, q.dtype),
                   pltpu.HBM(land.shape, land.dtype), jax.ShapeDtypeStruct((8, LANES), F32)),
        in_specs=[HBM_SPEC, HBM_SPEC],
        out_specs=(SEM_SPEC, SEM_SPEC, HBM_SPEC, HBM_SPEC, pl.BlockSpec(memory_space=pltpu.VMEM)),
        input_output_aliases={0: 2, 1: 3}, compiler_params=pltpu.CompilerParams(has_side_effects=DATAFLOW),
    )(pltpu.with_memory_space_constraint(q, pltpu.HBM), pltpu.with_memory_space_constraint(land, pltpu.HBM))


def _chips_wait(send_sems, recv_sems, q, land, after, name):
    def body(q_ref, land_ref, send, recv, after_ref, q_out, land_out):
        for cp in _chip_copies(q_ref, land_ref, send, recv):
            cp.wait_send()
            cp.wait_recv()

    return pl.pallas_call(
        body, name=name, out_shape=(pltpu.HBM(q.shape, q.dtype), pltpu.HBM(land.shape, land.dtype)),
        in_specs=[HBM_SPEC, HBM_SPEC, SEM_SPEC, SEM_SPEC, ANY_SPEC], out_specs=(HBM_SPEC, HBM_SPEC),
        input_output_aliases={0: 0, 1: 1}, compiler_params=pltpu.CompilerParams(has_side_effects=DATAFLOW),
    )(q, land, send_sems, recv_sems, after)[1]


def _exchange_cores(parts, name):
    n = len(parts)

    def body(*refs):
        ins, outs = refs[:n], refs[n:2 * n]
        send_sems, recv_sems = refs[2 * n:]
        x, y, c = _place()
        copies = []
        for a in range(n):
            for k in range(4):
                copies.append(pltpu.make_async_remote_copy(
                    src_ref=ins[a].at[2 * k + (1 - c)], dst_ref=outs[a].at[k], send_sem=send_sems.at[4 * a + k],
                    recv_sem=recv_sems.at[4 * a + k], device_id=(x, y, 1 - c), device_id_type=MESH))
        for cp in copies:
            cp.start()
        for cp in copies:
            cp.wait()

    any_spec = pl.BlockSpec(memory_space=pl.ANY)
    return pl.pallas_call(
        body, name=name, in_specs=[any_spec] * n, out_specs=[any_spec] * n,
        out_shape=[jax.ShapeDtypeStruct((4, *s.shape[1:]), s.dtype) for s in parts],
        scratch_shapes=[pltpu.SemaphoreType.DMA((4 * n,)), pltpu.SemaphoreType.DMA((4 * n,))],
    )(*parts)


def _chip_sum(part, got, place, name, tr=256):
    _, r, c = part.shape
    tr, tc = _tile2d(r, c, tr)

    def body(place_ref, p_ref, g_ref, q_ref, own_ref):
        s = p_ref[0].astype(F32) + g_ref[0].astype(F32)
        q_ref[0] = s.astype(BF16)

        @pl.when(pl.program_id(2) == place_ref[1])
        def _():
            own_ref[...] = s

    grid_spec = pltpu.PrefetchScalarGridSpec(
        num_scalar_prefetch=1, grid=(r // tr, c // tc, 4),
        in_specs=[pl.BlockSpec((1, tr, tc), lambda i, j, k, pr: (2 * k + pr[0], i, j)),
                  pl.BlockSpec((1, tr, tc), lambda i, j, k, pr: (k, i, j))],
        out_specs=[pl.BlockSpec((1, tr, tc), lambda i, j, k, pr: (k, i, j)),
                   pl.BlockSpec((tr, tc), lambda i, j, k, pr: (i, j))])
    return pl.pallas_call(
        body, name=name, grid_spec=grid_spec,
        out_shape=[jax.ShapeDtypeStruct((4, r, c), BF16), jax.ShapeDtypeStruct((r, c), F32)],
        compiler_params=_params(3),
    )(place, part, got)


def _adamw(w, g, m, v):
    m = ADAM_B1 * m + (1.0 - ADAM_B1) * g
    v = ADAM_B2 * v + (1.0 - ADAM_B2) * jnp.square(g)
    m_hat = m / (1.0 - ADAM_B1 ** ADAM_STEP)
    v_hat = v / (1.0 - ADAM_B2 ** ADAM_STEP)
    return -ADAM_LR * (m_hat / (jnp.sqrt(v_hat) + ADAM_EPS) + ADAM_WD * w), m, v


def _sum_adamw(own, got, w, m, v, name, tr=256):
    r, c = own.shape
    if w.ndim == 3:
        tr, tc = r, 4 * LANES
        wblk = pl.BlockSpec((tr, 1, tc), lambda i, j: (i, 0, j))
    else:
        tr, tc = _tile2d(r, c, tr)
        wblk = pl.BlockSpec((tr, tc), lambda i, j: (i, j))

    def body(own_ref, got_ref, w_ref, m_ref, v_ref, g_ref, d_ref, nm_ref, nv_ref):
        g = own_ref[...]
        for j in range(3):
            g = g + got_ref[j].astype(F32)
        two_d = lambda ref: ref[...].reshape(tr, tc)
        delta, nm, nv = _adamw(two_d(w_ref), g, two_d(m_ref), two_d(v_ref))
        for ref, val in ((g_ref, g), (d_ref, delta), (nm_ref, nm), (nv_ref, nv)):
            ref[...] = val.reshape(ref.shape)

    blk = pl.BlockSpec((tr, tc), lambda i, j: (i, j))
    return pl.pallas_call(
        body, name=name, grid=(r // tr, c // tc),
        in_specs=[blk, pl.BlockSpec((3, tr, tc), lambda i, j: (0, i, j)), wblk, wblk, wblk], out_specs=[wblk] * 4,
        out_shape=[jax.ShapeDtypeStruct(w.shape, F32)] * 4, compiler_params=_params(2),
    )(own, got, w, m, v)


def _sum8_adamw(part, got, place, w, m, v, name, tr=256):
    r, c = w.shape
    tr, tc = _tile2d(r, c, tr)

    def body(place_ref, own_ref, got_ref, w_ref, m_ref, v_ref, g_ref, d_ref, nm_ref, nv_ref):
        dev = 2 * place_ref[1] + place_ref[0]
        g = jnp.zeros((tr, tc), F32)
        for d in range(N_DEV):
            g = g + jnp.where(dev == d, own_ref[0], got_ref[d]).astype(F32)
        g_ref[...] = g
        d_ref[...], nm_ref[...], nv_ref[...] = _adamw(w_ref[...], g, m_ref[...], v_ref[...])

    blk = pl.BlockSpec((tr, tc), lambda i, j, pr: (i, j))
    grid_spec = pltpu.PrefetchScalarGridSpec(
        num_scalar_prefetch=1, grid=(r // tr, c // tc),
        in_specs=[pl.BlockSpec((1, tr, tc), lambda i, j, pr: (2 * pr[1] + pr[0], i, j)),
                  pl.BlockSpec((N_DEV, tr, tc), lambda i, j, pr: (0, i, j)), blk, blk, blk],
        out_specs=[blk] * 4)
    return pl.pallas_call(
        body, name=name, grid_spec=grid_spec, out_shape=[jax.ShapeDtypeStruct((r, c), F32)] * 4,
        compiler_params=_params(2),
    )(place, part, got, w, m, v)


VECTORS = ["norm1_w", "b_gate", "conv_a_b", "dt_bias", "a_log", "d_skip", "ssd_norm_w", "uv_b", "v_ln_w", "v_ln_b",
           "norm2_w", "conv_f_b", "final_norm_w"]
SMALL_ORDER = VECTORS + ["w_spatial", "b_spatial", "conv_a_w", "conv_f_w"]


ROW_VECTORS = VECTORS[1:]


def _small_adamw(gathered, w, m, v):
    sizes = {n: w[n].shape[1] for n in ROW_VECTORS}
    offs, off = {}, 0
    for n in ROW_VECTORS:
        offs[n] = off
        off += -(-sizes[n] // LANES) * LANES
    loss_off = off
    k = len(SMALL_ORDER)
    n_g = len(gathered)

    def body(*refs):
        row_ref, ws_ref, bs_ref, ca_ref, cf_ref, n1_ref = refs[:n_g]
        w_refs, m_refs, v_refs = (dict(zip(SMALL_ORDER, refs[n_g + i * k:n_g + (i + 1) * k])) for i in range(3))
        outs = refs[n_g + 3 * k:]
        x, y, c = _place()
        dev = 4 * x + 2 * y + c

        def total(ref):
            s = ref[0]
            for d in range(1, N_DEV):
                s = s + ref[d]
            return s

        row = total(row_ref)
        grads = {n: row[:, offs[n]:offs[n] + sizes[n]] for n in ROW_VECTORS}
        grads["norm1_w"], grads["w_spatial"], grads["b_spatial"] = total(n1_ref), total(ws_ref), total(bs_ref)
        for n, ref in (("conv_a_w", ca_ref), ("conv_f_w", cf_ref)):
            whole, cols = total(ref), w_refs[n].shape[1]
            mine = whole[:, :cols]
            for d in range(1, N_DEV):
                mine = jnp.where(dev == d, whole[:, d * cols:(d + 1) * cols], mine)
            grads[n] = mine
        for i, n in enumerate(SMALL_ORDER):
            outs[4 * i][...] = grads[n]
            outs[4 * i + 1][...], outs[4 * i + 2][...], outs[4 * i + 3][...] = _adamw(
                w_refs[n][...], grads[n], m_refs[n][...], v_refs[n][...])
        outs[4 * k][...] = row[:, loss_off:loss_off + LANES]

    out = pl.pallas_call(
        body, name="adamw_small",
        out_shape=[jax.ShapeDtypeStruct(w[n].shape, F32) for n in SMALL_ORDER for _ in range(4)]
        + [jax.ShapeDtypeStruct((1, LANES), F32)],
        compiler_params=_params(0),
    )(*gathered, *[t[n] for t in (w, m, v) for n in SMALL_ORDER])
    return [dict(zip(SMALL_ORDER, out[j:4 * k:4])) for j in range(4)] + [out[4 * k]]


SMALL = ["norm1_w", "b_gate", "conv_a_b", "dt_bias", "a_log", "d_skip", "ssd_norm_w", "uv_b", "v_ln_w", "v_ln_b",
         "w_spatial", "b_spatial", "norm2_w", "conv_f_b", "final_norm_w"]
BIG = ["w_in", "w_branch", "w_out", "w_up", "w_down"]
TRANSPOSED = ("w_in", "w_up")
WEIGHTS = ["norm1_w", "w_in", "b_gate", "conv_a_w", "conv_a_b", "dt_bias", "a_log", "d_skip", "ssd_norm_w", "uv_b",
           "v_ln_w", "v_ln_b", "w_spatial", "b_spatial", "w_branch", "w_out", "norm2_w", "w_up", "conv_f_w",
           "conv_f_b", "w_down", "final_norm_w"]
IN_SPLITS = [("z", 0, 2048), ("xbc", 2048, 5120), ("dt", 5120, 5152), ("uv", 5152, 7200), ("gates", 7200, 9248)]


def _columns_from_devices(a):
    return a.transpose(1, 0, 2).reshape(a.shape[1], -1)


def kernel(x, norm1_w, w_in, b_gate, conv_a_w, conv_a_b, dt_bias, a_log, d_skip, ssd_norm_w, uv_b, v_ln_w, v_ln_b, w_spatial, b_spatial, w_branch, w_out, norm2_w, w_up, conv_f_w, conv_f_b, w_down, final_norm_w, loss_target, m_norm1_w, m_w_in, m_b_gate, m_conv_a_w, m_conv_a_b, m_dt_bias, m_a_log, m_d_skip, m_ssd_norm_w, m_uv_b, m_v_ln_w, m_v_ln_b, m_w_spatial, m_b_spatial, m_w_branch, m_w_out, m_norm2_w, m_w_up, m_conv_f_w, m_conv_f_b, m_w_down, m_final_norm_w, v_norm1_w, v_w_in, v_b_gate, v_conv_a_w, v_conv_a_b, v_dt_bias, v_a_log, v_d_skip, v_ssd_norm_w, v_uv_b, v_v_ln_w, v_v_ln_b, v_w_spatial, v_b_spatial, v_w_branch, v_w_out, v_norm2_w, v_w_up, v_conv_f_w, v_conv_f_b, v_w_down, v_final_norm_w):
    args = dict(locals())
    wts = {n: args[n] for n in WEIGHTS}
    mom = {n: args["m_" + n] for n in WEIGHTS}
    var = {n: args["v_" + n] for n in WEIGHTS}
    cx, cy, cc = _place()
    dev = 4 * cx + 2 * cy + cc
    place = jnp.stack([cc, 2 * cx + cy]).astype(jnp.int32)

    def shard2d(n, a):
        return a[0].T if n in TRANSPOSED else a[0]

    def unshard(n, b):
        return (b.T if n in TRANSPOSED else b)[None]

    g_in, g_conv_a, g_conv_f = _all_gather(
        [shard2d("w_in", w_in).astype(BF16), conv_a_w[0], conv_f_w[0]], "gather_w_in")
    late = [shard2d(n, wts[n]).astype(BF16) for n in BIG[1:]]
    send_sems, recv_sems, late, lands, token = _gather_start(late, g_in, "gather_late_start")
    w_in_rows = g_in.reshape(-1, D_MODEL)
    w = {name: w_in_rows[lo:hi] for name, lo, hi in IN_SPLITS}
    w["dt"] = jnp.pad(w["dt"], ((0, DT_PAD - SSD_HEADS), (0, 0)))
    w["conv_a"] = _columns_from_devices(g_conv_a)
    w["conv_f"] = _columns_from_devices(g_conv_f)

    def late_weights(*after):
        got = _gather_wait(send_sems, recv_sems, late, lands, after, "gather_late_wait")
        g_branch, g_out, g_up, g_down = [lax.dynamic_update_index_in_dim(land, mine, dev, 0).reshape(-1, D_MODEL)
                                         for land, mine in zip(got, late)]
        return {"branch_a": g_branch[:SSD_INNER], "branch_b": g_branch[SSD_INNER:], "out": g_out, "up": g_up,
                "down": g_down}

    in_flight = {}

    def on_grad(n, g):
        part = {"w_in": lambda: jnp.concatenate([g[name][:hi - lo] for name, lo, hi in IN_SPLITS], axis=0),
                "w_branch": lambda: jnp.concatenate([g["branch_a"], g["branch_b"]], axis=0),
                "w_out": lambda: g["out"], "w_up": lambda: g["up"], "w_down": lambda: g["down"]}[n]()
        part = part.reshape(N_DEV, -1, D_MODEL)
        if n != "w_in":
            send, recv, (part,), (land,), tok = _gather_start([part], part, f"to_owners_start_{n}", scatter=True)
            in_flight[n] = (part, send, recv, land)
            return tok
        from_core, = _exchange_cores([part], f"to_other_core_{n}")
        q, own = _chip_sum(part, from_core, place, f"chip_sum_{n}")
        send, recv, q, land, tok = _chips_start(q, f"to_other_chips_start_{n}")
        in_flight[n] = (own, send, recv, q, land)
        return tok

    p = {n: wts[n][0] if wts[n].ndim > 2 else wts[n].reshape(1, -1) for n in SMALL}
    small_flight = []

    def on_small(g, loss):
        arrays = [jnp.concatenate([g[n] for n in ROW_VECTORS] + [loss[:1]], axis=1), g["w_spatial"], g["b_spatial"],
                  g["conv_a"], g["conv_f"]]
        *flight, tok = _gather_start(arrays, g["conv_a"], "gather_small_start")
        small_flight.append(flight)
        return tok

    loss, gx, g = _local_step(x[0], loss_target[0], w, p, after=token, late_weights=late_weights, on_grad=on_grad,
                              on_small=on_small)
    *flight, _ = _gather_start([g["norm1_w"]], gx, "gather_norm1_start")
    small_flight.append(flight)

    grads, delta, new_m, new_v = {}, {}, {}, {}

    def big_adamw(n, after):
        if n != "w_in":
            part, send, recv, land = in_flight[n]
            got, = _gather_wait(send, recv, [part], [land], [after], f"to_owners_wait_{n}", scatter=True)
            out = _sum8_adamw(part, got, place, shard2d(n, wts[n]), shard2d(n, mom[n]), shard2d(n, var[n]),
                              f"adamw_{n}")
            grads[n], delta[n], new_m[n], new_v[n] = [unshard(n, o) for o in out]
            return out[1]
        own, send, recv, q, land = in_flight[n]
        got = _chips_wait(send, recv, q, land, after, f"to_other_chips_wait_{n}")
        out = _sum_adamw(own, got, *[t[n].transpose(2, 0, 1) for t in (wts, mom, var)], f"adamw_{n}")
        grads[n], delta[n], new_m[n], new_v[n] = [o.transpose(1, 2, 0) for o in out]
        return out[1]

    after = gx
    for n in ("w_down", "w_up", "w_out", "w_branch", "w_in"):
        after = big_adamw(n, after)
    gathered = []
    for (send, recv, mine, land), name in zip(small_flight, ("gather_small_wait", "gather_norm1_wait")):
        got = _gather_wait(send, recv, mine, land, [after], name)
        gathered += [lax.dynamic_update_index_in_dim(full, own, dev, 0) for full, own in zip(got, mine)]
    small = [{n: t[n][0] if t[n].ndim > 2 else t[n].reshape(1, -1) for n in SMALL_ORDER} for t in (wts, mom, var)]
    *outs, loss = _small_adamw(gathered, *small)
    for tgt, out in zip((grads, delta, new_m, new_v), outs):
        tgt.update({n: out[n].reshape(wts[n].shape) for n in SMALL_ORDER})
    loss = loss[0, 0]

    return (loss, gx[None], *[grads[n] for n in WEIGHTS], *[delta[n] for n in WEIGHTS],
            *[new_m[n] for n in WEIGHTS], *[new_v[n] for n in WEIGHTS])
```

```python
import functools

import jax
import jax.numpy as jnp
from jax import lax
from jax.experimental import pallas as pl
from jax.experimental.pallas import tpu as pltpu

F32, BF16 = jnp.float32, jnp.bfloat16
HIGHEST = lax.Precision.HIGHEST

D_MODEL = 1024
SSD_INNER = 2048
SSD_HEAD_DIM = 64
SSD_HEADS = 32
SSD_GROUPS = 4
SSD_STATE = 128
SSD_BC = SSD_GROUPS * SSD_STATE
SSD_XBC = SSD_INNER + 2 * SSD_BC
SSD_CONV = 4
CHUNK = 128
N_PAIRS = SSD_HEADS // 2
PAIRS_PER_GROUP = N_PAIRS // SSD_GROUPS
SGU_WIDTH = 1024
SGU_GROUPS = 8
D_FF = 2816
FFN_CONV = 3
NORM_EPS = 1e-6
LN_EPS = 1e-5
LANES = 128
DT_PAD = LANES

ADAM_LR, ADAM_B1, ADAM_B2, ADAM_EPS, ADAM_WD, ADAM_STEP = 0.001, 0.9, 0.999, 1e-08, 0.01, 10

N_DEV = 8
VMEM_LIMIT = 56 * 1024 * 1024
MESH = pl.DeviceIdType.MESH


def _params(n_grid, **kw):
    sem = dict(dimension_semantics=("arbitrary",) * n_grid) if n_grid else {}
    return pltpu.CompilerParams(vmem_limit_bytes=VMEM_LIMIT, **sem, **kw)


def _tile(n, pref):
    t = (min(pref, n) // LANES) * LANES
    while n % t:
        t -= LANES
    return t


def _row_tile(r, pref):
    for t in range(min(pref, r) // 16 * 16, 0, -16):
        if r % t == 0:
            return t
    return r


def _tile2d(r, c, rows):
    if r % 16 == 0:
        return _row_tile(r, rows), c
    return r, _tile(c, 2 * LANES)


def _rows(tm, n, nt=None, rev=False, col=0):
    if rev:
        return pl.BlockSpec((tm, n), lambda i: (nt - 1 - i, col))
    return pl.BlockSpec((tm, n), lambda i: (i, col))


def _halo(tm, n, nt=None, rev=False, col=0):
    per = tm // 8
    if rev:
        return pl.BlockSpec((8, n), lambda i: (jnp.maximum((nt - 1 - i) * per - 1, 0), col))
    return pl.BlockSpec((8, n), lambda i: (jnp.maximum(i * per - 1, 0), col))


def _into(into, in_index, out_index):
    if into is None:
        return [], [], {}
    return [into], [pl.BlockSpec(memory_space=pl.ANY)], dict(input_output_aliases={in_index: out_index})


def _full(shape):
    nd = len(shape)
    return pl.BlockSpec(shape, lambda *_: (0,) * nd)


def _rms(x, w, eps=NORM_EPS):
    return x * lax.rsqrt(jnp.mean(x * x, axis=-1, keepdims=True) + eps) * w


def _layer_norm(x, w, b):
    mu = jnp.mean(x, axis=-1, keepdims=True)
    var = jnp.mean(jnp.square(x - mu), axis=-1, keepdims=True)
    return (x - mu) * lax.rsqrt(var + LN_EPS) * w + b


def _sigmoid(x):
    return 1.0 / (1.0 + jnp.exp(-x))


def _silu(x):
    return x * _sigmoid(x)


def _dsilu(x):
    s = _sigmoid(x)
    return s * (1.0 + x * (1.0 - s))


def _softplus(x):
    return jnp.maximum(x, 0.0) + jnp.log(1.0 + jnp.exp(-jnp.abs(x)))


def _gelu(x):
    return jax.nn.gelu(x)


def _dot(a, b):
    return jnp.dot(a, b, preferred_element_type=F32)


def _dot_nt(a, b):
    return lax.dot_general(a, b, (((1,), (1,)), ((), ())), preferred_element_type=F32)


def _dot_tn(a, b):
    return lax.dot_general(a, b, (((0,), (0,)), ((), ())), preferred_element_type=F32)


def _dot_split(p, e):
    hi = p.astype(BF16)
    lo = (p - hi.astype(F32)).astype(BF16)
    return _dot(hi, e) + _dot(lo, e)


def _colsum(x):
    return jnp.sum(x, axis=0, keepdims=True)


def _shift_down(x, halo, j):
    xs = pltpu.roll(x, j, 0)
    hs = pltpu.roll(halo, j, 0)
    r8 = lax.broadcasted_iota(jnp.int32, hs.shape, 0)
    return jnp.concatenate([jnp.where(r8 < j, hs, xs[:8]), xs[8:]], axis=0)


def _shift_up(x, nxt, j):
    n = x.shape[0]
    xs = pltpu.roll(x, n - j, 0)
    ns = pltpu.roll(nxt, 8 - j, 0)
    r8 = lax.broadcasted_iota(jnp.int32, ns.shape, 0)
    return jnp.concatenate([xs[:n - 8], jnp.where(r8 >= 8 - j, ns, xs[n - 8:])], axis=0)


def _causal_conv(x, halo, w, b):
    k = w.shape[0]
    y = b + w[k - 1:k, :] * x
    for j in range(1, k):
        y = y + w[k - 1 - j:k - j, :] * _shift_down(x, halo, j)
    return y


def _causal_conv_bwd(dy, nxt, x, w):
    k = w.shape[0]
    dx = w[k - 1:k, :] * dy
    dw = [_colsum(dy * x)]
    for j in range(1, k):
        dyj = _shift_up(dy, nxt, j)
        dx = dx + w[k - 1 - j:k - j, :] * dyj
        dw.append(_colsum(dyj * x))
    return dx, jnp.concatenate(dw[::-1], axis=0)


MM_TILE_PREF = 1408
MM_VMEM_BUDGET = 40 * 1024 * 1024
MM_WHOLE_K = 6144


def _mm_tiles(m, n, k, out_bytes):
    tm, tn = _tile(m, MM_TILE_PREF), _tile(n, MM_TILE_PREF)
    need = lambda tm, tn: 2 * (2 * k * (tm + tn) + out_bytes * tm * tn)
    while need(tm, tn) > MM_VMEM_BUDGET:
        if tn >= tm and tn > LANES:
            tn = _tile(n, tn - LANES)
        else:
            tm = _tile(m, tm - LANES)
    return tm, tn


def _mm(a, b, dims, name, acc=None, out_dtype=F32, after=None):
    if dims == "tn":
        k, m = a.shape
    else:
        m, k = a.shape
    n = b.shape[0] if dims == "nt" else b.shape[1]
    tm, tn = _mm_tiles(m, n, k, 4 * (2 if acc is not None else 1))
    a_spec = pl.BlockSpec((k, tm), lambda j, i: (0, i)) if dims == "tn" else pl.BlockSpec((tm, k), lambda j, i: (i, 0))
    b_spec = pl.BlockSpec((tn, k), lambda j, i: (j, 0)) if dims == "nt" else pl.BlockSpec((k, tn), lambda j, i: (0, j))
    o_spec = pl.BlockSpec((tm, tn), lambda j, i: (i, j))
    dot = {"nn": _dot, "nt": _dot_nt, "tn": _dot_tn}[dims]

    def body(a_ref, b_ref, *rest):
        r = dot(a_ref[...], b_ref[...])
        if acc is not None:
            r = r + rest[0][...]
        rest[-1][...] = r.astype(out_dtype)

    ins, specs = [a, b], [a_spec, b_spec]
    if acc is not None:
        ins.append(acc)
        specs.append(o_spec)
    if after is not None:
        ins.append(after)
        specs.append(pl.BlockSpec(memory_space=pl.ANY))
    return pl.pallas_call(
        body, name=name, grid=(n // tn, m // tm), in_specs=specs, out_specs=o_spec,
        out_shape=jax.ShapeDtypeStruct((m, n), out_dtype), compiler_params=_params(2),
    )(*ins)


def _mm_rows(a, b, dims, name, fn, rows=(), fulls=(), row_outs=(), acc_outs=(), after=None):
    m, k = a.shape
    n = b.shape[0] if dims == "nt" else b.shape[1]
    tk = k if (dims == "nt" or k <= MM_WHOLE_K) else _tile(k, 1024)
    nk = k // tk
    per_row = (2 * tk + 8 * n + sum(4 * r.shape[1] for r in rows)
               + sum(c * jnp.dtype(d).itemsize for c, d in row_outs))
    tm = _tile(m, 1024)
    while 2 * tm * per_row + 4 * tk * n > MM_VMEM_BUDGET:
        tm = _tile(m, tm - LANES)
    dot = _dot_nt if dims == "nt" else _dot
    n_in = 2 + len(rows) + len(fulls) + (after is not None)
    n_out = len(row_outs) + len(acc_outs)

    def body(*refs):
        ins, outs, scratch = refs[:n_in], refs[n_in:n_in + n_out], refs[n_in + n_out:]
        row_refs, acc_refs = outs[:len(row_outs)], outs[len(row_outs):]
        step = pl.program_id(1)

        @pl.when(jnp.logical_and(pl.program_id(0) == 0, step == 0))
        def _():
            for r in acc_refs:
                r[...] = jnp.zeros_like(r)

        part = dot(ins[0][...], ins[1][...])
        if nk > 1:
            part_ref, = scratch

            @pl.when(step == 0)
            def _():
                part_ref[...] = part

            @pl.when(step > 0)
            def _():
                part_ref[...] += part

        @pl.when(step == nk - 1)
        def _():
            result = part_ref[...] if nk > 1 else part
            new_rows, incs = fn(result, *[r[...] for r in ins[2:2 + len(rows) + len(fulls)]])
            for r, val in zip(row_refs, new_rows):
                r[...] = val.astype(r.dtype)
            for r, inc in zip(acc_refs, incs):
                r[...] += inc

    tile_rows = lambda c: pl.BlockSpec((tm, c), lambda i, s: (i, 0))
    b_spec = pl.BlockSpec((tk, n), lambda i, s: (s, 0)) if dims == "nn" else _full(b.shape)
    extra, extra_specs = ([after], [pl.BlockSpec(memory_space=pl.ANY)]) if after is not None else ([], [])
    return pl.pallas_call(
        body, name=name, grid=(m // tm, nk),
        in_specs=[pl.BlockSpec((tm, tk), lambda i, s: (i, s)), b_spec] + [tile_rows(r.shape[1]) for r in rows]
        + [_full(f.shape) for f in fulls] + extra_specs,
        out_specs=[tile_rows(c) for c, _ in row_outs] + [_full(s) for s in acc_outs],
        out_shape=[jax.ShapeDtypeStruct((m, c), d) for c, d in row_outs]
        + [jax.ShapeDtypeStruct(s, F32) for s in acc_outs],
        scratch_shapes=[pltpu.VMEM((tm, n), F32)] if nk > 1 else [],
        compiler_params=_params(2),
    )(a, b, *rows, *fulls, *extra)


def _residual_norm(o, x, w):
    h = x + o
    return (h, _rms(h, w)), ()


def _norm_backward(dn, h, dres, w):
    _, vjp = jax.vjp(_rms, h, w)
    dh, dw = vjp(dn)
    dh = dh + dres
    return (dh, dh), (dw,)


def _loss_and_grad(dn, h1, target, w):
    yf, vjp = jax.vjp(_rms, h1 + dn, w)
    err = yf - target
    loss = 0.5 * jnp.sum(jnp.mean(err * err, axis=-1, keepdims=True))
    dh, dw = vjp(err * (1.0 / err.shape[-1]))
    return (dh, dh), (jnp.full((8, LANES), loss, F32), dw)


def _wgrad(a, d, name, after=None):
    return _mm(a, d, "tn", name, out_dtype=BF16, after=after)


def _norm_fwd(x, w, name, after=None, tm=512):
    t, d = x.shape

    def body(x_ref, w_ref, *rest):
        rest[-1][...] = _rms(x_ref[...], w_ref[...]).astype(BF16)

    extra, extra_specs = ([after], [_full(after.shape)]) if after is not None else ([], [])
    return pl.pallas_call(
        body, name=name, grid=(t // tm,), in_specs=[_rows(tm, d), _full((1, d))] + extra_specs,
        out_specs=_rows(tm, d), out_shape=jax.ShapeDtypeStruct((t, d), BF16), compiler_params=_params(1),
    )(x, w, *extra)


def _conv_a_fwd(xbc, cw, cb, tm=256, col=0):
    t, c = xbc.shape[0], cw.shape[1]

    def body(x_ref, h_ref, w_ref, b_ref, o_ref, y_ref):
        halo = jnp.where(pl.program_id(0) > 0, h_ref[...], 0.0)
        y = _causal_conv(x_ref[...], halo, w_ref[...], b_ref[...])
        y_ref[...] = y
        o_ref[...] = _silu(y)

    return pl.pallas_call(
        body, name="conv_a_fwd", grid=(t // tm,),
        in_specs=[_rows(tm, c, col=col), _halo(tm, c, col=col), _full(cw.shape), _full((1, c))],
        out_specs=[_rows(tm, c)] * 2, out_shape=[jax.ShapeDtypeStruct((t, c), F32)] * 2, compiler_params=_params(1),
    )(xbc, xbc, cw, cb)


def _ssd_common(dtr, dtb, alog, e_t):
    row = lax.broadcasted_iota(jnp.int32, (CHUNK, CHUNK), 0)
    col = lax.broadcasted_iota(jnp.int32, (CHUNK, CHUNK), 1)
    causal = row >= col
    dt = _softplus(dtr + dtb)
    a = -jnp.exp(alog)
    acum = jnp.dot(causal.astype(F32), dt * a, precision=HIGHEST, preferred_element_type=F32)
    spread = lambda v: _dot_split(v, e_t)
    return dict(dt=dt, a=a, acum=acum, acum_t=acum.T, causal=causal, row=row, col=col, lane_lo=col < SSD_HEAD_DIM,
                dt_x=spread(dt), ecol_x=spread(jnp.exp(acum)), dsr_x=spread(jnp.exp(acum[CHUNK - 1:CHUNK, :] - acum)))


def _head_decay(c, h, transposed=False):
    d = c["acum"][:, h:h + 1] - c["acum_t"][h:h + 1, :]
    if transposed:
        return jnp.exp(jnp.where(c["row"] <= c["col"], -d, -jnp.inf))
    return jnp.exp(jnp.where(c["causal"], d, -jnp.inf))


def _ssd_fwd(xc, dtr, z, dtb, alog, dsk, nw, e_t, z_col=0):
    t = xc.shape[0]
    nc = t // CHUNK

    def body(xs_ref, b_ref, c_ref, dtr_ref, z_ref, dtb_ref, alog_ref, dsk_ref, nw_ref, et_ref,
             y_ref, ya_ref, sp_ref, s_scr):
        @pl.when(pl.program_id(0) == 0)
        def _():
            s_scr[...] = jnp.zeros_like(s_scr)

        c = _ssd_common(dtr_ref[...], dtb_ref[...], alog_ref[...], et_ref[...])
        lane_lo = c["lane_lo"]
        dsk = dsk_ref[...]
        for g in range(SSD_GROUPS):
            gs = slice(g * SSD_STATE, (g + 1) * SSD_STATE)
            bg_t, cg = b_ref[:, gs].T.astype(BF16), c_ref[:, gs].astype(BF16)
            cb = _dot(cg, bg_t)
            for pp in range(PAIRS_PER_GROUP):
                j = g * PAIRS_PER_GROUP + pp
                ps = slice(j * LANES, (j + 1) * LANES)
                x = xs_ref[:, ps]
                ecol, dsr = c["ecol_x"][:, ps], c["dsr_x"][:, ps]
                xdt = x * c["dt_x"][:, ps]
                xb = xdt.astype(BF16)
                zero = jnp.zeros_like(xb)
                yd = (_dot((cb * _head_decay(c, 2 * j)).astype(BF16), jnp.where(lane_lo, xb, zero))
                      + _dot((cb * _head_decay(c, 2 * j + 1)).astype(BF16), jnp.where(lane_lo, zero, xb)))
                sp = s_scr[j]
                yo = ecol * _dot(cg, sp.astype(BF16))
                st = _dot(bg_t, (xdt * dsr).astype(BF16))
                sp_ref[0, j] = sp
                s_scr[j] = ecol[CHUNK - 1:CHUNK] * sp + st
                dskp = jnp.where(lane_lo[0:1], dsk[:, 2 * j:2 * j + 1], dsk[:, 2 * j + 1:2 * j + 2])
                y_ref[:, ps] = yd + yo + dskp * x
        ya_ref[...] = _rms(y_ref[...] * _silu(z_ref[...]), nw_ref[...]).astype(BF16)

    ck = lambda n, col=0: pl.BlockSpec((CHUNK, n), lambda c: (c, col))
    return pl.pallas_call(
        body, name="ssd_fwd", grid=(nc,),
        in_specs=[ck(SSD_INNER), ck(SSD_BC, SSD_INNER // SSD_BC), ck(SSD_BC, SSD_INNER // SSD_BC + 1), ck(DT_PAD),
                  ck(SSD_INNER, z_col), _full((1, DT_PAD)), _full((1, DT_PAD)), _full((1, DT_PAD)),
                  _full((1, SSD_INNER)), _full(e_t.shape)],
        out_specs=[ck(SSD_INNER), ck(SSD_INNER),
                   pl.BlockSpec((1, N_PAIRS, SSD_STATE, LANES), lambda c: (c, 0, 0, 0))],
        out_shape=[jax.ShapeDtypeStruct((t, SSD_INNER), F32), jax.ShapeDtypeStruct((t, SSD_INNER), BF16),
                   jax.ShapeDtypeStruct((nc, N_PAIRS, SSD_STATE, LANES), F32)],
        scratch_shapes=[pltpu.VMEM((N_PAIRS, SSD_STATE, LANES), F32)], compiler_params=_params(1),
    )(xc, xc, xc, dtr, z, dtb, alog, dsk, nw, e_t)


def _ssd_bwd(dya, y, z, xc, dtr, sprev, dtb, alog, dsk, nw, e_heads, e_t, z_col=0, into=None):
    t = xc.shape[0]
    nc = t // CHUNK
    more, more_specs, alias = _into(into, 14, 0)

    def body(dya_ref, y_ref, z_ref, xs_ref, b_ref, c_ref, dtr_ref, sp_ref, dtb_ref, alog_ref, dsk_ref, nw_ref, e_ref,
             et_ref, *rest):
        dz_ref, dxs_ref, db_ref, dc_ref, ddtr_ref, dnw_ref, ddtb_ref, dalog_ref, ddsk_ref, ds_scr = rest[len(more):]

        @pl.when(pl.program_id(0) == 0)
        def _():
            ds_scr[...] = jnp.zeros_like(ds_scr)
            for r in (dnw_ref, ddtb_ref, dalog_ref, ddsk_ref):
                r[...] = jnp.zeros_like(r)

        y = y_ref[...]
        _, gate_vjp = jax.vjp(lambda y_, z_, w_: _rms(y_ * _silu(z_), w_), y, z_ref[...], nw_ref[...])
        dy, dz, dnw = gate_vjp(dya_ref[...])
        dz_ref[...] = dz.astype(BF16)
        dnw_ref[...] += dnw

        dtr = dtr_ref[...]
        c = _ssd_common(dtr, dtb_ref[...], alog_ref[...], et_ref[...])
        dt, a, lane_lo, row, col = c["dt"], c["a"], c["lane_lo"], c["row"], c["col"]
        dsk = dsk_ref[...]
        p_a, p_dt, v_last = [], [], []
        da_cols = jnp.zeros((CHUNK, CHUNK), F32)
        da_rows = jnp.zeros((CHUNK, CHUNK), F32)
        for g in range(SSD_GROUPS):
            gs = slice(g * SSD_STATE, (g + 1) * SSD_STATE)
            bg, cg = b_ref[:, gs].astype(BF16), c_ref[:, gs].astype(BF16)
            bg_t, cg_t = b_ref[:, gs].T.astype(BF16), c_ref[:, gs].T.astype(BF16)
            cb, cb_t = _dot(cg, bg_t), _dot(bg, cg_t)
            dcb = jnp.zeros((CHUNK, CHUNK), F32)
            dbg = jnp.zeros((CHUNK, SSD_STATE), F32)
            dcg = jnp.zeros((CHUNK, SSD_STATE), F32)
            for pp in range(PAIRS_PER_GROUP):
                j = g * PAIRS_PER_GROUP + pp
                ps = slice(j * LANES, (j + 1) * LANES)
                x = xs_ref[:, ps]
                dtp, ecol, dsr = c["dt_x"][:, ps], c["ecol_x"][:, ps], c["dsr_x"][:, ps]
                elast = ecol[CHUNK - 1:CHUNK]
                xdt = x * dtp
                xb = xdt.astype(BF16)
                dskp = jnp.where(lane_lo[0:1], dsk[:, 2 * j:2 * j + 1], dsk[:, 2 * j + 1:2 * j + 2])
                dyp = dy[:, ps]
                dyb = dyp.astype(BF16)
                sp, dsn = sp_ref[0, j], ds_scr[j]
                spb, dsnb = sp.astype(BF16), dsn.astype(BF16)
                y_off = ecol * _dot(cg, spb)
                dw = (dyp * ecol).astype(BF16)
                dcg = dcg + _dot_nt(dw, spb)
                dsp = _dot(cg_t, dw) + elast * dsn
                xd = xdt * dsr
                zd = _dot(bg, dsnb) * dsr
                dbg = dbg + _dot_nt(xd.astype(BF16), dsnb)
                dxdt = zd
                zero = jnp.zeros_like(xb)
                for h, lm in ((2 * j, lane_lo), (2 * j + 1, jnp.logical_not(lane_lo))):
                    le = _head_decay(c, h)
                    dm = _dot_nt(jnp.where(lm, dyb, zero), jnp.where(lm, xb, zero))
                    dcb = dcb + dm * le
                    m = cb * le
                    m_t = (cb_t * _head_decay(c, h, transposed=True)).astype(BF16)
                    dxdt = dxdt + jnp.where(lm, _dot(m_t, dyb), 0.0)
                    q = dm * m
                    da_cols = da_cols + jnp.where(col == h, jnp.sum(q, axis=1, keepdims=True), 0.0)
                    da_rows = da_rows + jnp.where(row == h, _colsum(q), 0.0)
                ds_scr[j] = dsp
                dxs_ref[:, ps] = dxdt * dtp + dskp * dyp
                p_a.append(dyp * y_off - xdt * zd)
                p_dt.append(dxdt * x)
                v_last.append(_colsum(zd * xdt) + elast * _colsum(dsn * sp))
            dcbb = dcb.astype(BF16)
            db_ref[:, gs] = dbg + _dot_tn(dcbb, cg)
            dc_ref[:, gs] = dcg + _dot(dcbb, bg)
        e = e_ref[...]
        rows8 = jnp.concatenate([jnp.concatenate(v_last, axis=1), _colsum(dy * xs_ref[...]),
                                 jnp.zeros((6, SSD_INNER), F32)], axis=0)
        r8 = _dot_split(rows8, e)
        da = (_dot_split(jnp.concatenate(p_a, axis=1), e) + jnp.where(row == CHUNK - 1, r8[0:1], 0.0)
              + da_cols - da_rows.T)
        ddsk_ref[...] += r8[1:2]
        dadt = jnp.dot((row <= col).astype(F32), da, precision=HIGHEST, preferred_element_type=F32)
        ddt = dadt * a + _dot_split(jnp.concatenate(p_dt, axis=1), e)
        dalog_ref[...] += _colsum(dadt * dt) * a
        ddtr = ddt * _sigmoid(dtr + dtb_ref[...])
        ddtr_ref[...] = ddtr
        ddtb_ref[...] += _colsum(ddtr)

    ck = lambda n, col=0: pl.BlockSpec((CHUNK, n), lambda c: (nc - 1 - c, col))
    acc = lambda n: _full((1, n))
    return pl.pallas_call(
        body, name="ssd_bwd", grid=(nc,),
        in_specs=[ck(SSD_INNER), ck(SSD_INNER), ck(SSD_INNER, z_col), ck(SSD_INNER), ck(SSD_BC, SSD_INNER // SSD_BC),
                  ck(SSD_BC, SSD_INNER // SSD_BC + 1), ck(DT_PAD),
                  pl.BlockSpec((1, N_PAIRS, SSD_STATE, LANES), lambda c: (nc - 1 - c, 0, 0, 0)),
                  acc(DT_PAD), acc(DT_PAD), acc(DT_PAD), acc(SSD_INNER), _full((SSD_INNER, LANES)),
                  _full((LANES, SSD_INNER))] + more_specs,
        out_specs=[ck(SSD_INNER, z_col if into is not None else 0), ck(SSD_INNER), ck(SSD_BC), ck(SSD_BC), ck(DT_PAD),
                   acc(SSD_INNER), acc(DT_PAD), acc(DT_PAD), acc(DT_PAD)],
        out_shape=[jax.ShapeDtypeStruct(into.shape if into is not None else (t, SSD_INNER), BF16),
                   jax.ShapeDtypeStruct((t, SSD_INNER), F32),
                   jax.ShapeDtypeStruct((t, SSD_BC), F32), jax.ShapeDtypeStruct((t, SSD_BC), F32),
                   jax.ShapeDtypeStruct((t, DT_PAD), F32), jax.ShapeDtypeStruct((1, SSD_INNER), F32),
                   jax.ShapeDtypeStruct((1, DT_PAD), F32), jax.ShapeDtypeStruct((1, DT_PAD), F32),
                   jax.ShapeDtypeStruct((1, DT_PAD), F32)],
        scratch_shapes=[pltpu.VMEM((N_PAIRS, SSD_STATE, LANES), F32)], compiler_params=_params(1), **alias,
    )(dya, y, z, xc, xc, xc, dtr, sprev, dtb, alog, dsk, nw, e_heads, e_t, *more)


def _sgu_act(uv, uvb, lnw, lnb):
    a = _gelu(uv + uvb)
    return a[:, :SGU_WIDTH], _layer_norm(a[:, SGU_WIDTH:], lnw, lnb)


def _sgu_weights(ws_ref):
    row = lax.broadcasted_iota(jnp.int32, (CHUNK, CHUNK), 0)
    col = lax.broadcasted_iota(jnp.int32, (CHUNK, CHUNK), 1)
    return [jnp.where(row >= col, ws_ref[g], 0.0).astype(BF16) for g in range(SGU_GROUPS)], row >= col


def _sgu_fwd(uv, uvb, lnw, lnb, ws, bs_t, col=0):
    t = uv.shape[0]

    def body(uv_ref, uvb_ref, lnw_ref, lnb_ref, ws_ref, bs_ref, o_ref):
        u, vn = _sgu_act(uv_ref[...], uvb_ref[...], lnw_ref[...], lnb_ref[...])
        wc, _ = _sgu_weights(ws_ref)
        bs = bs_ref[...]
        for g in range(SGU_GROUPS):
            gs = slice(g * LANES, (g + 1) * LANES)
            mixed = _dot(wc[g], vn[:, gs].astype(BF16)) + bs[:, g:g + 1]
            o_ref[:, gs] = (u[:, gs] * mixed).astype(BF16)

    return pl.pallas_call(
        body, name="sgu_fwd", grid=(t // CHUNK,),
        in_specs=[_rows(CHUNK, 2 * SGU_WIDTH, col=col), _full((1, 2 * SGU_WIDTH)), _full((1, SGU_WIDTH)), _full((1, SGU_WIDTH)),
                  _full(ws.shape), _full(bs_t.shape)],
        out_specs=_rows(CHUNK, SGU_WIDTH), out_shape=jax.ShapeDtypeStruct((t, SGU_WIDTH), BF16),
        compiler_params=_params(1),
    )(uv, uvb, lnw, lnb, ws, bs_t)


def _sgu_bwd(dyb, uv, uvb, lnw, lnb, ws, bs_t, e_groups, col=0, into=None):
    t = uv.shape[0]
    more, more_specs, alias = _into(into, 8, 0)

    def body(dyb_ref, uv_ref, uvb_ref, lnw_ref, lnb_ref, ws_ref, bs_ref, e_ref, *rest):
        duv_ref, duvb_ref, dlnw_ref, dlnb_ref, dws_ref, dbs_ref = rest[len(more):]

        @pl.when(pl.program_id(0) == 0)
        def _():
            for r in (duvb_ref, dlnw_ref, dlnb_ref, dws_ref, dbs_ref):
                r[...] = jnp.zeros_like(r)

        (u, vn), act_vjp = jax.vjp(_sgu_act, uv_ref[...], uvb_ref[...], lnw_ref[...], lnb_ref[...])
        wc, causal = _sgu_weights(ws_ref)
        bs = bs_ref[...]
        dyb = dyb_ref[...]
        du, dvn, dmix = [], [], []
        for g in range(SGU_GROUPS):
            gs = slice(g * LANES, (g + 1) * LANES)
            vb = vn[:, gs].astype(BF16)
            mixed = _dot(wc[g], vb) + bs[:, g:g + 1]
            dm = dyb[:, gs] * u[:, gs]
            dmb = dm.astype(BF16)
            du.append(dyb[:, gs] * mixed)
            dvn.append(_dot_tn(wc[g], dmb))
            dws_ref[g] += jnp.where(causal, _dot_nt(dmb, vb), 0.0)
            dmix.append(dm)
        dbs_ref[...] += _dot_split(jnp.concatenate(dmix, axis=1), e_ref[...])
        duv, duvb, dlnw, dlnb = act_vjp((jnp.concatenate(du, axis=1), jnp.concatenate(dvn, axis=1)))
        duv_ref[...] = duv.astype(BF16)
        duvb_ref[...] += duvb
        dlnw_ref[...] += dlnw
        dlnb_ref[...] += dlnb

    return pl.pallas_call(
        body, name="sgu_bwd", grid=(t // CHUNK,),
        in_specs=[_rows(CHUNK, SGU_WIDTH), _rows(CHUNK, 2 * SGU_WIDTH, col=col), _full((1, 2 * SGU_WIDTH)),
                  _full((1, SGU_WIDTH)), _full((1, SGU_WIDTH)), _full(ws.shape), _full(bs_t.shape),
                  _full(e_groups.shape)] + more_specs,
        out_specs=[_rows(CHUNK, 2 * SGU_WIDTH, col=col if into is not None else 0), _full((1, 2 * SGU_WIDTH)),
                   _full((1, SGU_WIDTH)), _full((1, SGU_WIDTH)), _full(ws.shape), _full(bs_t.shape)],
        out_shape=[jax.ShapeDtypeStruct(into.shape if into is not None else (t, 2 * SGU_WIDTH), BF16),
                   jax.ShapeDtypeStruct((1, 2 * SGU_WIDTH), F32),
                   jax.ShapeDtypeStruct((1, SGU_WIDTH), F32), jax.ShapeDtypeStruct((1, SGU_WIDTH), F32),
                   jax.ShapeDtypeStruct(ws.shape, F32), jax.ShapeDtypeStruct(bs_t.shape, F32)],
        compiler_params=_params(1), **alias,
    )(dyb, uv, uvb, lnw, lnb, ws, bs_t, e_groups, *more)


def _merge(gates, pa, pb, bg):
    s = _sigmoid(gates + bg)
    return s[:, :D_MODEL] * pa + s[:, D_MODEL:] * pb


def _merge_fwd(gates, pa, pb, bg, tm=256, col=0):
    t = gates.shape[0]

    def body(g_ref, pa_ref, pb_ref, bg_ref, o_ref):
        o_ref[...] = _merge(g_ref[...], pa_ref[...], pb_ref[...], bg_ref[...]).astype(BF16)

    return pl.pallas_call(
        body, name="merge_fwd", grid=(t // tm,),
        in_specs=[_rows(tm, 2 * D_MODEL, col=col), _rows(tm, D_MODEL), _rows(tm, D_MODEL), _full((1, 2 * D_MODEL))],
        out_specs=_rows(tm, D_MODEL), out_shape=jax.ShapeDtypeStruct((t, D_MODEL), BF16), compiler_params=_params(1),
    )(gates, pa, pb, bg)


def _merge_bwd(dmix, gates, pa, pb, bg, tm=256, col=0, into=None):
    t = gates.shape[0]
    more, more_specs, alias = _into(into, 5, 0)

    def body(d_ref, g_ref, pa_ref, pb_ref, bg_ref, *rest):
        dg_ref, dpa_ref, dpb_ref, dbg_ref = rest[len(more):]

        @pl.when(pl.program_id(0) == 0)
        def _():
            dbg_ref[...] = jnp.zeros_like(dbg_ref)

        _, vjp = jax.vjp(_merge, g_ref[...], pa_ref[...], pb_ref[...], bg_ref[...])
        dg, dpa, dpb, dbg = vjp(d_ref[...])
        dg_ref[...] = dg.astype(BF16)
        dpa_ref[...] = dpa.astype(BF16)
        dpb_ref[...] = dpb.astype(BF16)
        dbg_ref[...] += dbg

    return pl.pallas_call(
        body, name="merge_bwd", grid=(t // tm,),
        in_specs=[_rows(tm, D_MODEL), _rows(tm, 2 * D_MODEL, col=col), _rows(tm, D_MODEL), _rows(tm, D_MODEL),
                  _full((1, 2 * D_MODEL))] + more_specs,
        out_specs=[_rows(tm, 2 * D_MODEL, col=col if into is not None else 0), _rows(tm, D_MODEL),
                   _rows(tm, D_MODEL), _full((1, 2 * D_MODEL))],
        out_shape=[jax.ShapeDtypeStruct(into.shape if into is not None else (t, 2 * D_MODEL), BF16),
                   jax.ShapeDtypeStruct((t, D_MODEL), BF16), jax.ShapeDtypeStruct((t, D_MODEL), BF16),
                   jax.ShapeDtypeStruct((1, 2 * D_MODEL), F32)],
        compiler_params=_params(1), **alias,
    )(dmix, gates, pa, pb, bg, *more)


def _conv_f_fwd(up, cw, cb, tm=128):
    t, c = up.shape

    def body(x_ref, h_ref, w_ref, b_ref, o_ref, y_ref):
        halo = jnp.where(pl.program_id(0) > 0, h_ref[...], 0.0)
        y = _causal_conv(x_ref[...], halo, w_ref[...], b_ref[...])
        y_ref[...] = y
        o_ref[...] = (_silu(y[:, :D_FF]) * y[:, D_FF:]).astype(BF16)

    return pl.pallas_call(
        body, name="conv_f_fwd", grid=(t // tm,),
        in_specs=[_rows(tm, c), _halo(tm, c), _full(cw.shape), _full((1, c))],
        out_specs=[_rows(tm, D_FF), _rows(tm, c)],
        out_shape=[jax.ShapeDtypeStruct((t, D_FF), BF16), jax.ShapeDtypeStruct((t, c), F32)],
        compiler_params=_params(1),
    )(up, up, cw, cb)


def _conv_f_bwd(dact, y, up, cw, tm=128):
    t, c = up.shape
    nt = t // tm

    def body(d_ref, y_ref, x_ref, w_ref, dx_ref, dw_ref, db_ref, nxt_scr):
        @pl.when(pl.program_id(0) == 0)
        def _():
            nxt_scr[...] = jnp.zeros_like(nxt_scr)
            dw_ref[...] = jnp.zeros_like(dw_ref)
            db_ref[...] = jnp.zeros_like(db_ref)

        a, v = y_ref[:, :D_FF], y_ref[:, D_FF:]
        d = d_ref[...]
        dy = jnp.concatenate([d * v * _dsilu(a), d * _silu(a)], axis=1)
        dx, dw = _causal_conv_bwd(dy, nxt_scr[...], x_ref[...], w_ref[...])
        dx_ref[...] = dx.astype(BF16)
        nxt_scr[...] = dy[:8]
        dw_ref[...] += dw
        db_ref[...] += _colsum(dy)

    return pl.pallas_call(
        body, name="conv_f_bwd", grid=(nt,),
        in_specs=[_rows(tm, D_FF, nt, True), _rows(tm, c, nt, True), _rows(tm, c, nt, True), _full(cw.shape)],
        out_specs=[_rows(tm, c, nt, True), _full(cw.shape), _full((1, c))],
        out_shape=[jax.ShapeDtypeStruct((t, c), BF16), jax.ShapeDtypeStruct(cw.shape, F32),
                   jax.ShapeDtypeStruct((1, c), F32)],
        scratch_shapes=[pltpu.VMEM((8, c), F32)], compiler_params=_params(1),
    )(dact, y, up, cw)


def _conv_a_bwd(dxs, db, dc, y, xbc, cw, tm=256, col=0, into=None):
    t, c = xbc.shape[0], cw.shape[1]
    nt = t // tm
    more, more_specs, alias = _into(into, 6, 0)

    def body(dxs_ref, db_ref, dc_ref, y_ref, x_ref, w_ref, *rest):
        dx_ref, dw_ref, dbias_ref, nxt_scr = rest[len(more):]

        @pl.when(pl.program_id(0) == 0)
        def _():
            nxt_scr[...] = jnp.zeros_like(nxt_scr)
            dw_ref[...] = jnp.zeros_like(dw_ref)
            dbias_ref[...] = jnp.zeros_like(dbias_ref)

        dy = jnp.concatenate([dxs_ref[...], db_ref[...], dc_ref[...]], axis=1) * _dsilu(y_ref[...])
        dx, dw = _causal_conv_bwd(dy, nxt_scr[...], x_ref[...], w_ref[...])
        dx_ref[...] = dx.astype(BF16)
        nxt_scr[...] = dy[:8]
        dw_ref[...] += dw
        dbias_ref[...] += _colsum(dy)

    return pl.pallas_call(
        body, name="conv_a_bwd", grid=(nt,),
        in_specs=[_rows(tm, SSD_INNER, nt, True), _rows(tm, SSD_BC, nt, True), _rows(tm, SSD_BC, nt, True),
                  _rows(tm, c, nt, True), _rows(tm, c, nt, True, col), _full(cw.shape)] + more_specs,
        out_specs=[_rows(tm, c, nt, True, col if into is not None else 0), _full(cw.shape), _full((1, c))],
        out_shape=[jax.ShapeDtypeStruct(into.shape if into is not None else (t, c), BF16),
                   jax.ShapeDtypeStruct(cw.shape, F32), jax.ShapeDtypeStruct((1, c), F32)],
        scratch_shapes=[pltpu.VMEM((8, c), F32)], compiler_params=_params(1), **alias,
    )(dxs, db, dc, y, xbc, cw, *more)


def _pad_lanes(v, n=DT_PAD):
    return jnp.pad(v, ((0, 0), (0, n - v.shape[1])))


def _local_step(x, target, w, p, after=None, late_weights=None, on_grad=None, on_small=None):
    dtb, alog, dsk = _pad_lanes(p["dt_bias"]), _pad_lanes(p["a_log"]), _pad_lanes(p["d_skip"])
    bs_t = _pad_lanes(p["b_spatial"].T)
    e_heads = (jnp.arange(SSD_INNER)[:, None] // SSD_HEAD_DIM == jnp.arange(LANES)[None, :]).astype(BF16)
    e_heads_t = (jnp.arange(LANES)[:, None] == jnp.arange(SSD_INNER)[None, :] // SSD_HEAD_DIM).astype(BF16)
    e_groups = (jnp.arange(SGU_WIDTH)[:, None] // LANES == jnp.arange(LANES)[None, :]).astype(BF16)

    n1 = _norm_fwd(x, p["norm1_w"], "norm1_fwd", after=after)
    z = _mm(n1, w["z"], "nt", "proj_z")
    xbc = _mm(n1, w["xbc"], "nt", "proj_xbc")
    dtr = _mm(n1, w["dt"], "nt", "proj_dt")
    uv = _mm(n1, w["uv"], "nt", "proj_uv")
    gates = _mm(n1, w["gates"], "nt", "proj_gates")
    xc, conv_a_out = _conv_a_fwd(xbc, w["conv_a"], p["conv_a_b"])
    y, ya, sprev = _ssd_fwd(xc, dtr, z, dtb, alog, dsk, p["ssd_norm_w"], e_heads_t)
    yb = _sgu_fwd(uv, p["uv_b"], p["v_ln_w"], p["v_ln_b"], p["w_spatial"], bs_t)
    if late_weights is not None:
        w = {**w, **late_weights(ya, yb)}
    pa = _mm(ya, w["branch_a"], "nn", "branch_a")
    pb = _mm(yb, w["branch_b"], "nn", "branch_b")
    mix = _merge_fwd(gates, pa, pb, p["b_gate"])
    wide = [(D_MODEL, F32), (D_MODEL, BF16)]
    h1, n2 = _mm_rows(mix, w["out"], "nn", "out_proj", _residual_norm, rows=[x], fulls=[p["norm2_w"]], row_outs=wide)
    up = _mm(n2, w["up"], "nt", "up_proj")
    act, conv_f_out = _conv_f_fwd(up, w["conv_f"], p["conv_f_b"])
    dh2, dh2b, loss, g_final = _mm_rows(
        act, w["down"], "nn", "down_proj", _loss_and_grad, rows=[h1, target], fulls=[p["final_norm_w"]],
        row_outs=wide, acc_outs=[(8, LANES), (1, D_MODEL)])

    on_grad = on_grad or (lambda name, grads: None)
    g = {"final_norm_w": g_final}
    g["down"] = _wgrad(act, dh2b, "down_wgrad")
    tok = on_grad("w_down", g)
    dact = _mm(dh2b, w["down"], "nt", "down_dgrad", after=tok)
    dup, g["conv_f"], g["conv_f_b"] = _conv_f_bwd(dact, conv_f_out, up, w["conv_f"])
    g["up"] = _wgrad(dup, n2, "up_wgrad")
    tok = on_grad("w_up", g)
    dh1, dh1b, g["norm2_w"] = _mm_rows(
        dup, w["up"], "nn", "up_dgrad", _norm_backward, rows=[h1, dh2], fulls=[p["norm2_w"]], row_outs=wide,
        acc_outs=[(1, D_MODEL)], after=tok)
    g["out"] = _wgrad(mix, dh1b, "out_wgrad")
    tok = on_grad("w_out", g)
    dmix = _mm(dh1b, w["out"], "nt", "out_dgrad", after=tok)
    dgates, dpa, dpb, g["b_gate"] = _merge_bwd(dmix, gates, pa, pb, p["b_gate"])
    g["branch_a"] = _wgrad(ya, dpa, "branch_a_wgrad")
    g["branch_b"] = _wgrad(yb, dpb, "branch_b_wgrad")
    tok = on_grad("w_branch", g)
    dya = _mm(dpa, w["branch_a"], "nt", "branch_a_dgrad", after=tok)
    dyb = _mm(dpb, w["branch_b"], "nt", "branch_b_dgrad", after=tok)
    duv, g["uv_b"], g["v_ln_w"], g["v_ln_b"], g["w_spatial"], dbs_t = _sgu_bwd(
        dyb, uv, p["uv_b"], p["v_ln_w"], p["v_ln_b"], p["w_spatial"], bs_t, e_groups)
    g["b_spatial"] = dbs_t[:, :SGU_GROUPS].T
    dz, dxs, db, dc, ddtr, g["ssd_norm_w"], ddtb, dalog, ddsk = _ssd_bwd(
        dya, y, z, xc, dtr, sprev, dtb, alog, dsk, p["ssd_norm_w"], e_heads, e_heads_t)
    g["dt_bias"], g["a_log"], g["d_skip"] = ddtb, dalog, ddsk
    dxbc, g["conv_a"], g["conv_a_b"] = _conv_a_bwd(dxs, db, dc, conv_a_out, xbc, w["conv_a"])
    tok = on_small(g, loss) if on_small else None
    ddtrb = ddtr.astype(BF16)
    for name, d in (("z", dz), ("xbc", dxbc), ("dt", ddtrb), ("uv", duv), ("gates", dgates)):
        g[name] = _wgrad(d, n1, name + "_wgrad", after=tok)
    tok = on_grad("w_in", g)
    dn1 = _mm(dz, w["z"], "nn", "z_dgrad", after=tok)
    dn1 = _mm(dxbc, w["xbc"], "nn", "xbc_dgrad", acc=dn1)
    dn1 = _mm(ddtrb, w["dt"], "nn", "dt_dgrad", acc=dn1)
    dn1 = _mm(duv, w["uv"], "nn", "uv_dgrad", acc=dn1)
    gx, g["norm1_w"] = _mm_rows(
        dgates, w["gates"], "nn", "gates_dgrad",
        lambda r, so_far, h, dres, w_: tuple(t[:1] for t in _norm_backward(r + so_far, h, dres, w_)),
        rows=[dn1, x, dh1], fulls=[p["norm1_w"]], row_outs=wide[:1], acc_outs=[(1, D_MODEL)])
    return loss, gx, g


def _place():
    return lax.axis_index("x"), lax.axis_index("y"), lax.axis_index("c")


def _other_chips(x, y):
    return [(1 - x, y), (x, 1 - y), (1 - x, 1 - y)]


def _all_gather(shards, name):
    n = len(shards)

    def body(*refs):
        ins, outs = refs[:n], refs[n:2 * n]
        send_sems, recv_sems, local_sems = refs[2 * n:]
        x, y, c = _place()
        me, sibling = (x, y, c), (x, y, 1 - c)
        chips = _other_chips(x, y)

        def copy(a, k, block, to, src=None):
            slot = outs[a].at[4 * block[0] + 2 * block[1] + block[2]]
            return pltpu.make_async_remote_copy(
                src_ref=slot if src is None else src, dst_ref=slot, send_sem=send_sems.at[7 * a + k],
                recv_sem=recv_sems.at[7 * a + k], device_id=to, device_id_type=MESH)

        started = []
        for a in range(n):
            mine = pltpu.make_async_copy(ins[a], outs[a].at[4 * x + 2 * y + c], local_sems.at[a])
            mine.start()
            started.append(mine)
        sends = []
        for a in range(n):
            sends.append(copy(a, 0, me, sibling, src=ins[a]))
            sends += [copy(a, 1 + j, me, (*chip, c), src=ins[a]) for j, chip in enumerate(chips)]
        for cp in sends:
            cp.start()
        for a in range(n):
            for j, chip in enumerate(chips):
                copy(a, 1 + j, (*chip, c), me).wait_recv()
                fwd = copy(a, 4 + j, (*chip, c), sibling)
                fwd.start()
                sends.append(fwd)
        for a in range(n):
            copy(a, 0, sibling, me).wait_recv()
            for j, chip in enumerate(chips):
                copy(a, 4 + j, (*chip, 1 - c), me).wait_recv()
        for cp in sends:
            cp.wait_send()
        for mine in started:
            mine.wait()

    any_spec = pl.BlockSpec(memory_space=pl.ANY)
    return pl.pallas_call(
        body, name=name, in_specs=[any_spec] * n, out_specs=[any_spec] * n,
        out_shape=[jax.ShapeDtypeStruct((N_DEV, *s.shape), s.dtype) for s in shards],
        scratch_shapes=[pltpu.SemaphoreType.DMA((7 * n,)), pltpu.SemaphoreType.DMA((7 * n,)),
                        pltpu.SemaphoreType.DMA((n,))],
    )(*shards)


HBM_SPEC = pl.BlockSpec(memory_space=pltpu.HBM)
SEM_SPEC = pl.BlockSpec(memory_space=pltpu.SEMAPHORE)
ANY_SPEC = pl.BlockSpec(memory_space=pl.ANY)
DATAFLOW = pltpu.SideEffectType.DATAFLOW_SIDE_EFFECTING
N_PEERS = N_DEV - 1


def _peers(x, y, c):
    out = []
    for r in range(1, N_DEV):
        fx, fy, fc = r >> 2 & 1, r >> 1 & 1, r & 1
        out.append(((1 - x) if fx else x, (1 - y) if fy else y, (1 - c) if fc else c))
    return out


def _gather_copies(srcs, lands, send_sems, recv_sems, sending, scatter=False):
    x, y, c = _place()
    copies = []
    for a, (src, land) in enumerate(zip(srcs, lands)):
        for j, (px, py, pc) in enumerate(_peers(x, y, c)):
            mine, theirs = 4 * x + 2 * y + c, 4 * px + 2 * py + pc
            block = src.at[theirs if sending else 0] if scatter else src
            copies.append(pltpu.make_async_remote_copy(
                src_ref=block, dst_ref=land.at[mine if sending else theirs], send_sem=send_sems.at[N_PEERS * a + j],
                recv_sem=recv_sems.at[N_PEERS * a + j], device_id=(px, py, pc), device_id_type=MESH))
    return copies


def _gather_start(shards, after, name, scatter=False):
    n = len(shards)
    after = [] if after is None else [after]

    def body(*refs):
        srcs, lands = refs[:n], refs[n:2 * n]
        send_sems, recv_sems = refs[2 * n + len(after):2 * n + len(after) + 2]
        token = refs[-1]
        for cp in _gather_copies(srcs, lands, send_sems, recv_sems, sending=True, scatter=scatter):
            cp.start()
        token[...] = jnp.zeros_like(token)

    lands = [lax.empty(s.shape if scatter else (N_DEV, *s.shape), s.dtype) for s in shards]
    hbm = lambda a: pltpu.with_memory_space_constraint(a, pltpu.HBM)
    out = pl.pallas_call(
        body, name=name,
        out_shape=(pltpu.SemaphoreType.DMA((N_PEERS * n,)), pltpu.SemaphoreType.DMA((N_PEERS * n,)),
                   *[pltpu.HBM(a.shape, a.dtype) for a in (*shards, *lands)], jax.ShapeDtypeStruct((8, LANES), F32)),
        in_specs=[HBM_SPEC] * (2 * n) + [ANY_SPEC] * len(after),
        out_specs=(SEM_SPEC, SEM_SPEC, *[HBM_SPEC] * (2 * n), pl.BlockSpec(memory_space=pltpu.VMEM)),
        input_output_aliases={i: 2 + i for i in range(2 * n)},
        compiler_params=pltpu.CompilerParams(has_side_effects=DATAFLOW),
    )(*[hbm(a) for a in (*shards, *lands)], *after)
    return out[0], out[1], out[2:2 + n], out[2 + n:2 + 2 * n], out[-1]


def _gather_wait(send_sems, recv_sems, shards, lands, after, name, scatter=False):
    n = len(shards)
    after = tuple(after)

    def body(*refs):
        srcs, lands_ = refs[:n], refs[n:2 * n]
        send, recv = refs[2 * n:2 * n + 2]
        for cp in _gather_copies(srcs, lands_, send, recv, sending=False, scatter=scatter):
            cp.wait_send()
            cp.wait_recv()

    out = pl.pallas_call(
        body, name=name, out_shape=tuple(pltpu.HBM(a.shape, a.dtype) for a in (*shards, *lands)),
        in_specs=[HBM_SPEC] * (2 * n) + [SEM_SPEC, SEM_SPEC] + [ANY_SPEC] * len(after),
        out_specs=tuple([HBM_SPEC] * (2 * n)), input_output_aliases={i: i for i in range(2 * n)},
        compiler_params=pltpu.CompilerParams(has_side_effects=DATAFLOW),
    )(*shards, *lands, send_sems, recv_sems, *after)
    return out[:n], out[n:]


def _chip_copies(src, land, send_sems, recv_sems):
    x, y, c = _place()
    return [pltpu.make_async_remote_copy(
        src_ref=src.at[2 * cx + cy], dst_ref=land.at[j], send_sem=send_sems.at[j], recv_sem=recv_sems.at[j],
        device_id=(cx, cy, c), device_id_type=MESH) for j, (cx, cy) in enumerate(_other_chips(x, y))]


def _chips_start(q, name):
    def body(q_ref, land_ref, send_sems, recv_sems, q_thru, land_thru, token):
        for cp in _chip_copies(q_ref, land_ref, send_sems, recv_sems):
            cp.start()
        token[...] = jnp.zeros_like(token)

    land = lax.empty((3, *q.shape[1:]), q.dtype)
    return pl.pallas_call(
        body, name=name,
        out_shape=(pltpu.SemaphoreType.DMA((3,)), pltpu.SemaphoreType.DMA((3,)), pltpu.HBM(q.shape, q.dtype),
                   pltpu.HBM(land.shape, land.dtype), jax.ShapeDtypeStruct((8, LANES), F32)),
        in_specs=[HBM_SPEC, HBM_SPEC],
        out_specs=(SEM_SPEC, SEM_SPEC, HBM_SPEC, HBM_SPEC, pl.BlockSpec(memory_space=pltpu.VMEM)),
        input_output_aliases={0: 2, 1: 3}, compiler_params=pltpu.CompilerParams(has_side_effects=DATAFLOW),
    )(pltpu.with_memory_space_constraint(q, pltpu.HBM), pltpu.with_memory_space_constraint(land, pltpu.HBM))


def _chips_wait(send_sems, recv_sems, q, land, after, name):
    def body(q_ref, land_ref, send, recv, after_ref, q_out, land_out):
        for cp in _chip_copies(q_ref, land_ref, send, recv):
            cp.wait_send()
            cp.wait_recv()

    return pl.pallas_call(
        body, name=name, out_shape=(pltpu.HBM(q.shape, q.dtype), pltpu.HBM(land.shape, land.dtype)),
        in_specs=[HBM_SPEC, HBM_SPEC, SEM_SPEC, SEM_SPEC, ANY_SPEC], out_specs=(HBM_SPEC, HBM_SPEC),
        input_output_aliases={0: 0, 1: 1}, compiler_params=pltpu.CompilerParams(has_side_effects=DATAFLOW),
    )(q, land, send_sems, recv_sems, after)[1]


def _exchange_cores(parts, name):
    n = len(parts)

    def body(*refs):
        ins, outs = refs[:n], refs[n:2 * n]
        send_sems, recv_sems = refs[2 * n:]
        x, y, c = _place()
        copies = []
        for a in range(n):
            for k in range(4):
                copies.append(pltpu.make_async_remote_copy(
                    src_ref=ins[a].at[2 * k + (1 - c)], dst_ref=outs[a].at[k], send_sem=send_sems.at[4 * a + k],
                    recv_sem=recv_sems.at[4 * a + k], device_id=(x, y, 1 - c), device_id_type=MESH))
        for cp in copies:
            cp.start()
        for cp in copies:
            cp.wait()

    any_spec = pl.BlockSpec(memory_space=pl.ANY)
    return pl.pallas_call(
        body, name=name, in_specs=[any_spec] * n, out_specs=[any_spec] * n,
        out_shape=[jax.ShapeDtypeStruct((4, *s.shape[1:]), s.dtype) for s in parts],
        scratch_shapes=[pltpu.SemaphoreType.DMA((4 * n,)), pltpu.SemaphoreType.DMA((4 * n,))],
    )(*parts)


def _chip_sum(part, got, place, name, tr=256):
    _, r, c = part.shape
    tr, tc = _tile2d(r, c, tr)

    def body(place_ref, p_ref, g_ref, q_ref, own_ref):
        s = p_ref[0].astype(F32) + g_ref[0].astype(F32)
        q_ref[0] = s.astype(BF16)

        @pl.when(pl.program_id(2) == place_ref[1])
        def _():
            own_ref[...] = s

    grid_spec = pltpu.PrefetchScalarGridSpec(
        num_scalar_prefetch=1, grid=(r // tr, c // tc, 4),
        in_specs=[pl.BlockSpec((1, tr, tc), lambda i, j, k, pr: (2 * k + pr[0], i, j)),
                  pl.BlockSpec((1, tr, tc), lambda i, j, k, pr: (k, i, j))],
        out_specs=[pl.BlockSpec((1, tr, tc), lambda i, j, k, pr: (k, i, j)),
                   pl.BlockSpec((tr, tc), lambda i, j, k, pr: (i, j))])
    return pl.pallas_call(
        body, name=name, grid_spec=grid_spec,
        out_shape=[jax.ShapeDtypeStruct((4, r, c), BF16), jax.ShapeDtypeStruct((r, c), F32)],
        compiler_params=_params(3),
    )(place, part, got)


def _adamw(w, g, m, v):
    m = ADAM_B1 * m + (1.0 - ADAM_B1) * g
    v = ADAM_B2 * v + (1.0 - ADAM_B2) * jnp.square(g)
    m_hat = m / (1.0 - ADAM_B1 ** ADAM_STEP)
    v_hat = v / (1.0 - ADAM_B2 ** ADAM_STEP)
    return -ADAM_LR * (m_hat / (jnp.sqrt(v_hat) + ADAM_EPS) + ADAM_WD * w), m, v


def _sum_adamw(own, got, w, m, v, name, tr=256):
    r, c = own.shape
    if w.ndim == 3:
        tr, tc = r, 4 * LANES
        wblk = pl.BlockSpec((tr, 1, tc), lambda i, j: (i, 0, j))
    else:
        tr, tc = _tile2d(r, c, tr)
        wblk = pl.BlockSpec((tr, tc), lambda i, j: (i, j))

    def body(own_ref, got_ref, w_ref, m_ref, v_ref, g_ref, d_ref, nm_ref, nv_ref):
        g = own_ref[...]
        for j in range(3):
            g = g + got_ref[j].astype(F32)
        two_d = lambda ref: ref[...].reshape(tr, tc)
        delta, nm, nv = _adamw(two_d(w_ref), g, two_d(m_ref), two_d(v_ref))
        for ref, val in ((g_ref, g), (d_ref, delta), (nm_ref, nm), (nv_ref, nv)):
            ref[...] = val.reshape(ref.shape)

    blk = pl.BlockSpec((tr, tc), lambda i, j: (i, j))
    return pl.pallas_call(
        body, name=name, grid=(r // tr, c // tc),
        in_specs=[blk, pl.BlockSpec((3, tr, tc), lambda i, j: (0, i, j)), wblk, wblk, wblk], out_specs=[wblk] * 4,
        out_shape=[jax.ShapeDtypeStruct(w.shape, F32)] * 4, compiler_params=_params(2),
    )(own, got, w, m, v)


def _sum8_adamw(part, got, place, w, m, v, name, tr=256):
    r, c = w.shape
    tr, tc = _tile2d(r, c, tr)

    def body(place_ref, own_ref, got_ref, w_ref, m_ref, v_ref, g_ref, d_ref, nm_ref, nv_ref):
        dev = 2 * place_ref[1] + place_ref[0]
        g = jnp.zeros((tr, tc), F32)
        for d in range(N_DEV):
            g = g + jnp.where(dev == d, own_ref[0], got_ref[d]).astype(F32)
        g_ref[...] = g
        d_ref[...], nm_ref[...], nv_ref[...] = _adamw(w_ref[...], g, m_ref[...], v_ref[...])

    blk = pl.BlockSpec((tr, tc), lambda i, j, pr: (i, j))
    grid_spec = pltpu.PrefetchScalarGridSpec(
        num_scalar_prefetch=1, grid=(r // tr, c // tc),
        in_specs=[pl.BlockSpec((1, tr, tc), lambda i, j, pr: (2 * pr[1] + pr[0], i, j)),
                  pl.BlockSpec((N_DEV, tr, tc), lambda i, j, pr: (0, i, j)), blk, blk, blk],
        out_specs=[blk] * 4)
    return pl.pallas_call(
        body, name=name, grid_spec=grid_spec, out_shape=[jax.ShapeDtypeStruct((r, c), F32)] * 4,
        compiler_params=_params(2),
    )(place, part, got, w, m, v)


VECTORS = ["norm1_w", "b_gate", "conv_a_b", "dt_bias", "a_log", "d_skip", "ssd_norm_w", "uv_b", "v_ln_w", "v_ln_b",
           "norm2_w", "conv_f_b", "final_norm_w"]
SMALL_ORDER = VECTORS + ["w_spatial", "b_spatial", "conv_a_w", "conv_f_w"]


ROW_VECTORS = VECTORS[1:]


def _small_adamw(gathered, w, m, v):
    sizes = {n: w[n].shape[1] for n in ROW_VECTORS}
    offs, off = {}, 0
    for n in ROW_VECTORS:
        offs[n] = off
        off += -(-sizes[n] // LANES) * LANES
    loss_off = off
    k = len(SMALL_ORDER)
    n_g = len(gathered)

    def body(*refs):
        row_ref, ws_ref, bs_ref, ca_ref, cf_ref, n1_ref = refs[:n_g]
        w_refs, m_refs, v_refs = (dict(zip(SMALL_ORDER, refs[n_g + i * k:n_g + (i + 1) * k])) for i in range(3))
        outs = refs[n_g + 3 * k:]
        x, y, c = _place()
        dev = 4 * x + 2 * y + c

        def total(ref):
            s = ref[0]
            for d in range(1, N_DEV):
                s = s + ref[d]
            return s

        row = total(row_ref)
        grads = {n: row[:, offs[n]:offs[n] + sizes[n]] for n in ROW_VECTORS}
        grads["norm1_w"], grads["w_spatial"], grads["b_spatial"] = total(n1_ref), total(ws_ref), total(bs_ref)
        for n, ref in (("conv_a_w", ca_ref), ("conv_f_w", cf_ref)):
            whole, cols = total(ref), w_refs[n].shape[1]
            mine = whole[:, :cols]
            for d in range(1, N_DEV):
                mine = jnp.where(dev == d, whole[:, d * cols:(d + 1) * cols], mine)
            grads[n] = mine
        for i, n in enumerate(SMALL_ORDER):
            outs[4 * i][...] = grads[n]
            outs[4 * i + 1][...], outs[4 * i + 2][...], outs[4 * i + 3][...] = _adamw(
                w_refs[n][...], grads[n], m_refs[n][...], v_refs[n][...])
        outs[4 * k][...] = row[:, loss_off:loss_off + LANES]

    out = pl.pallas_call(
        body, name="adamw_small",
        out_shape=[jax.ShapeDtypeStruct(w[n].shape, F32) for n in SMALL_ORDER for _ in range(4)]
        + [jax.ShapeDtypeStruct((1, LANES), F32)],
        compiler_params=_params(0),
    )(*gathered, *[t[n] for t in (w, m, v) for n in SMALL_ORDER])
    return [dict(zip(SMALL_ORDER, out[j:4 * k:4])) for j in range(4)] + [out[4 * k]]


SMALL = ["norm1_w", "b_gate", "conv_a_b", "dt_bias", "a_log", "d_skip", "ssd_norm_w", "uv_b", "v_ln_w", "v_ln_b",
         "w_spatial", "b_spatial", "norm2_w", "conv_f_b", "final_norm_w"]
BIG = ["w_in", "w_branch", "w_out", "w_up", "w_down"]
TRANSPOSED = ("w_in", "w_up")
WEIGHTS = ["norm1_w", "w_in", "b_gate", "conv_a_w", "conv_a_b", "dt_bias", "a_log", "d_skip", "ssd_norm_w", "uv_b",
           "v_ln_w", "v_ln_b", "w_spatial", "b_spatial", "w_branch", "w_out", "norm2_w", "w_up", "conv_f_w",
           "conv_f_b", "w_down", "final_norm_w"]
IN_SPLITS = [("z", 0, 2048), ("xbc", 2048, 5120), ("dt", 5120, 5152), ("uv", 5152, 7200), ("gates", 7200, 9248)]


def _columns_from_devices(a):
    return a.transpose(1, 0, 2).reshape(a.shape[1], -1)


def kernel(x, norm1_w, w_in, b_gate, conv_a_w, conv_a_b, dt_bias, a_log, d_skip, ssd_norm_w, uv_b, v_ln_w, v_ln_b, w_spatial, b_spatial, w_branch, w_out, norm2_w, w_up, conv_f_w, conv_f_b, w_down, final_norm_w, loss_target, m_norm1_w, m_w_in, m_b_gate, m_conv_a_w, m_conv_a_b, m_dt_bias, m_a_log, m_d_skip, m_ssd_norm_w, m_uv_b, m_v_ln_w, m_v_ln_b, m_w_spatial, m_b_spatial, m_w_branch, m_w_out, m_norm2_w, m_w_up, m_conv_f_w, m_conv_f_b, m_w_down, m_final_norm_w, v_norm1_w, v_w_in, v_b_gate, v_conv_a_w, v_conv_a_b, v_dt_bias, v_a_log, v_d_skip, v_ssd_norm_w, v_uv_b, v_v_ln_w, v_v_ln_b, v_w_spatial, v_b_spatial, v_w_branch, v_w_out, v_norm2_w, v_w_up, v_conv_f_w, v_conv_f_b, v_w_down, v_final_norm_w):
    args = dict(locals())
    wts = {n: args[n] for n in WEIGHTS}
    mom = {n: args["m_" + n] for n in WEIGHTS}
    var = {n: args["v_" + n] for n in WEIGHTS}
    cx, cy, cc = _place()
    dev = 4 * cx + 2 * cy + cc
    place = jnp.stack([cc, 2 * cx + cy]).astype(jnp.int32)

    def shard2d(n, a):
        return a[0].T if n in TRANSPOSED else a[0]

    def unshard(n, b):
        return (b.T if n in TRANSPOSED else b)[None]

    g_in, g_conv_a, g_conv_f = _all_gather(
        [shard2d("w_in", w_in).astype(BF16), conv_a_w[0], conv_f_w[0]], "gather_w_in")
    late = [shard2d(n, wts[n]).astype(BF16) for n in BIG[1:]]
    send_sems, recv_sems, late, lands, token = _gather_start(late, g_in, "gather_late_start")
    w_in_rows = g_in.reshape(-1, D_MODEL)
    w = {name: w_in_rows[lo:hi] for name, lo, hi in IN_SPLITS}
    w["dt"] = jnp.pad(w["dt"], ((0, DT_PAD - SSD_HEADS), (0, 0)))
    w["conv_a"] = _columns_from_devices(g_conv_a)
    w["conv_f"] = _columns_from_devices(g_conv_f)

    def late_weights(*after):
        mine, got = _gather_wait(send_sems, recv_sems, late, lands, after, "gather_late_wait")
        g_branch, g_out, g_up, g_down = [lax.dynamic_update_index_in_dim(land, own, dev, 0).reshape(-1, D_MODEL)
                                         for land, own in zip(got, mine)]
        return {"branch_a": g_branch[:SSD_INNER], "branch_b": g_branch[SSD_INNER:], "out": g_out, "up": g_up,
                "down": g_down}

    in_flight = {}

    def on_grad(n, g):
        part = {"w_in": lambda: jnp.concatenate([g[name][:hi - lo] for name, lo, hi in IN_SPLITS], axis=0),
                "w_branch": lambda: jnp.concatenate([g["branch_a"], g["branch_b"]], axis=0),
                "w_out": lambda: g["out"], "w_up": lambda: g["up"], "w_down": lambda: g["down"]}[n]()
        part = part.reshape(N_DEV, -1, D_MODEL)
        if n != "w_in":
            send, recv, (part,), (land,), tok = _gather_start([part], None, f"to_owners_start_{n}", scatter=True)
            in_flight[n] = (part, send, recv, land)
            return tok
        from_core, = _exchange_cores([part], f"to_other_core_{n}")
        q, own = _chip_sum(part, from_core, place, f"chip_sum_{n}")
        send, recv, q, land, tok = _chips_start(q, f"to_other_chips_start_{n}")
        in_flight[n] = (own, send, recv, q, land)
        return tok

    p = {n: wts[n][0] if wts[n].ndim > 2 else wts[n].reshape(1, -1) for n in SMALL}
    small_flight = []

    def on_small(g, loss):
        arrays = [jnp.concatenate([g[n] for n in ROW_VECTORS] + [loss[:1]], axis=1), g["w_spatial"], g["b_spatial"],
                  g["conv_a"], g["conv_f"]]
        *flight, tok = _gather_start(arrays, g["conv_a"], "gather_small_start")
        small_flight.append(flight)
        return tok

    loss, gx, g = _local_step(x[0], loss_target[0], w, p, after=token, late_weights=late_weights, on_grad=on_grad,
                              on_small=on_small)
    *flight, _ = _gather_start([g["norm1_w"]], gx, "gather_norm1_start")
    small_flight.append(flight)

    grads, delta, new_m, new_v = {}, {}, {}, {}

    def big_adamw(n, after):
        if n != "w_in":
            part, send, recv, land = in_flight[n]
            (part,), (got,) = _gather_wait(send, recv, [part], [land], [after], f"to_owners_wait_{n}", scatter=True)
            out = _sum8_adamw(part, got, place, shard2d(n, wts[n]), shard2d(n, mom[n]), shard2d(n, var[n]),
                              f"adamw_{n}")
            grads[n], delta[n], new_m[n], new_v[n] = [unshard(n, o) for o in out]
            return out[1]
        own, send, recv, q, land = in_flight[n]
        got = _chips_wait(send, recv, q, land, after, f"to_other_chips_wait_{n}")
        out = _sum_adamw(own, got, *[t[n].transpose(2, 0, 1) for t in (wts, mom, var)], f"adamw_{n}")
        grads[n], delta[n], new_m[n], new_v[n] = [o.transpose(1, 2, 0) for o in out]
        return out[1]

    after = gx
    for n in ("w_down", "w_up", "w_out", "w_branch", "w_in"):
        after = big_adamw(n, after)
    gathered = []
    for (send, recv, mine, land), name in zip(small_flight, ("gather_small_wait", "gather_norm1_wait")):
        mine, got = _gather_wait(send, recv, mine, land, [after], name)
        gathered += [lax.dynamic_update_index_in_dim(full, own, dev, 0) for full, own in zip(got, mine)]
    small = [{n: t[n][0] if t[n].ndim > 2 else t[n].reshape(1, -1) for n in SMALL_ORDER} for t in (wts, mom, var)]
    *outs, loss = _small_adamw(gathered, *small)
    for tgt, out in zip((grads, delta, new_m, new_v), outs):
        tgt.update({n: out[n].reshape(wts[n].shape) for n in SMALL_ORDER})
    loss = loss[0, 0]

    return (loss, gx[None], *[grads[n] for n in WEIGHTS], *[delta[n] for n in WEIGHTS],
            *[new_m[n] for n in WEIGHTS], *[new_v[n] for n in WEIGHTS])
```

```python
import functools

import jax
import jax.numpy as jnp
from jax import lax
from jax.experimental import pallas as pl
from jax.experimental.pallas import tpu as pltpu

F32, BF16 = jnp.float32, jnp.bfloat16
HIGHEST = lax.Precision.HIGHEST

D_MODEL = 1024
SSD_INNER = 2048
SSD_HEAD_DIM = 64
SSD_HEADS = 32
SSD_GROUPS = 4
SSD_STATE = 128
SSD_BC = SSD_GROUPS * SSD_STATE
SSD_XBC = SSD_INNER + 2 * SSD_BC
SSD_CONV = 4
CHUNK = 128
N_PAIRS = SSD_HEADS // 2
PAIRS_PER_GROUP = N_PAIRS // SSD_GROUPS
SGU_WIDTH = 1024
SGU_GROUPS = 8
D_FF = 2816
FFN_CONV = 3
NORM_EPS = 1e-6
LN_EPS = 1e-5
LANES = 128
DT_PAD = LANES

ADAM_LR, ADAM_B1, ADAM_B2, ADAM_EPS, ADAM_WD, ADAM_STEP = 0.001, 0.9, 0.999, 1e-08, 0.01, 10

N_DEV = 8
VMEM_LIMIT = 56 * 1024 * 1024
MESH = pl.DeviceIdType.MESH


def _params(n_grid, **kw):
    sem = dict(dimension_semantics=("arbitrary",) * n_grid) if n_grid else {}
    return pltpu.CompilerParams(vmem_limit_bytes=VMEM_LIMIT, **sem, **kw)


def _tile(n, pref):
    t = (min(pref, n) // LANES) * LANES
    while n % t:
        t -= LANES
    return t


def _row_tile(r, pref):
    for t in range(min(pref, r) // 16 * 16, 0, -16):
        if r % t == 0:
            return t
    return r


def _tile2d(r, c, rows):
    if r % 16 == 0:
        return _row_tile(r, rows), c
    return r, _tile(c, 2 * LANES)


def _rows(tm, n, nt=None, rev=False, col=0):
    if rev:
        return pl.BlockSpec((tm, n), lambda i: (nt - 1 - i, col))
    return pl.BlockSpec((tm, n), lambda i: (i, col))


def _halo(tm, n, nt=None, rev=False, col=0):
    per = tm // 8
    if rev:
        return pl.BlockSpec((8, n), lambda i: (jnp.maximum((nt - 1 - i) * per - 1, 0), col))
    return pl.BlockSpec((8, n), lambda i: (jnp.maximum(i * per - 1, 0), col))


def _into(into, in_index, out_index):
    if into is None:
        return [], [], {}
    return [into], [pl.BlockSpec(memory_space=pl.ANY)], dict(input_output_aliases={in_index: out_index})


def _full(shape):
    nd = len(shape)
    return pl.BlockSpec(shape, lambda *_: (0,) * nd)


def _rms(x, w, eps=NORM_EPS):
    return x * lax.rsqrt(jnp.mean(x * x, axis=-1, keepdims=True) + eps) * w


def _layer_norm(x, w, b):
    mu = jnp.mean(x, axis=-1, keepdims=True)
    var = jnp.mean(jnp.square(x - mu), axis=-1, keepdims=True)
    return (x - mu) * lax.rsqrt(var + LN_EPS) * w + b


def _sigmoid(x):
    return 1.0 / (1.0 + jnp.exp(-x))


def _silu(x):
    return x * _sigmoid(x)


def _dsilu(x):
    s = _sigmoid(x)
    return s * (1.0 + x * (1.0 - s))


def _softplus(x):
    return jnp.maximum(x, 0.0) + jnp.log(1.0 + jnp.exp(-jnp.abs(x)))


def _gelu(x):
    return jax.nn.gelu(x)


def _dot(a, b):
    return jnp.dot(a, b, preferred_element_type=F32)


def _dot_nt(a, b):
    return lax.dot_general(a, b, (((1,), (1,)), ((), ())), preferred_element_type=F32)


def _dot_tn(a, b):
    return lax.dot_general(a, b, (((0,), (0,)), ((), ())), preferred_element_type=F32)


def _dot_split(p, e):
    hi = p.astype(BF16)
    lo = (p - hi.astype(F32)).astype(BF16)
    return _dot(hi, e) + _dot(lo, e)


def _colsum(x):
    return jnp.sum(x, axis=0, keepdims=True)


def _shift_down(x, halo, j):
    xs = pltpu.roll(x, j, 0)
    hs = pltpu.roll(halo, j, 0)
    r8 = lax.broadcasted_iota(jnp.int32, hs.shape, 0)
    return jnp.concatenate([jnp.where(r8 < j, hs, xs[:8]), xs[8:]], axis=0)


def _shift_up(x, nxt, j):
    n = x.shape[0]
    xs = pltpu.roll(x, n - j, 0)
    ns = pltpu.roll(nxt, 8 - j, 0)
    r8 = lax.broadcasted_iota(jnp.int32, ns.shape, 0)
    return jnp.concatenate([xs[:n - 8], jnp.where(r8 >= 8 - j, ns, xs[n - 8:])], axis=0)


def _causal_conv(x, halo, w, b):
    k = w.shape[0]
    y = b + w[k - 1:k, :] * x
    for j in range(1, k):
        y = y + w[k - 1 - j:k - j, :] * _shift_down(x, halo, j)
    return y


def _causal_conv_bwd(dy, nxt, x, w):
    k = w.shape[0]
    dx = w[k - 1:k, :] * dy
    dw = [_colsum(dy * x)]
    for j in range(1, k):
        dyj = _shift_up(dy, nxt, j)
        dx = dx + w[k - 1 - j:k - j, :] * dyj
        dw.append(_colsum(dyj * x))
    return dx, jnp.concatenate(dw[::-1], axis=0)


MM_TILE_PREF = 1408
MM_VMEM_BUDGET = 40 * 1024 * 1024
MM_WHOLE_K = 6144


def _mm_tiles(m, n, k, out_bytes):
    tm, tn = _tile(m, MM_TILE_PREF), _tile(n, MM_TILE_PREF)
    need = lambda tm, tn: 2 * (2 * k * (tm + tn) + out_bytes * tm * tn)
    while need(tm, tn) > MM_VMEM_BUDGET:
        if tn >= tm and tn > LANES:
            tn = _tile(n, tn - LANES)
        else:
            tm = _tile(m, tm - LANES)
    return tm, tn


def _mm(a, b, dims, name, acc=None, out_dtype=F32, after=None):
    if dims == "tn":
        k, m = a.shape
    else:
        m, k = a.shape
    n = b.shape[0] if dims == "nt" else b.shape[1]
    tm, tn = _mm_tiles(m, n, k, 4 * (2 if acc is not None else 1))
    a_spec = pl.BlockSpec((k, tm), lambda j, i: (0, i)) if dims == "tn" else pl.BlockSpec((tm, k), lambda j, i: (i, 0))
    b_spec = pl.BlockSpec((tn, k), lambda j, i: (j, 0)) if dims == "nt" else pl.BlockSpec((k, tn), lambda j, i: (0, j))
    o_spec = pl.BlockSpec((tm, tn), lambda j, i: (i, j))
    dot = {"nn": _dot, "nt": _dot_nt, "tn": _dot_tn}[dims]

    def body(a_ref, b_ref, *rest):
        r = dot(a_ref[...], b_ref[...])
        if acc is not None:
            r = r + rest[0][...]
        rest[-1][...] = r.astype(out_dtype)

    ins, specs = [a, b], [a_spec, b_spec]
    if acc is not None:
        ins.append(acc)
        specs.append(o_spec)
    if after is not None:
        ins.append(after)
        specs.append(pl.BlockSpec(memory_space=pl.ANY))
    return pl.pallas_call(
        body, name=name, grid=(n // tn, m // tm), in_specs=specs, out_specs=o_spec,
        out_shape=jax.ShapeDtypeStruct((m, n), out_dtype), compiler_params=_params(2),
    )(*ins)


def _mm_rows(a, b, dims, name, fn, rows=(), fulls=(), row_outs=(), acc_outs=(), after=None):
    m, k = a.shape
    n = b.shape[0] if dims == "nt" else b.shape[1]
    tk = k if (dims == "nt" or k <= MM_WHOLE_K) else _tile(k, 1024)
    nk = k // tk
    per_row = (2 * tk + 8 * n + sum(4 * r.shape[1] for r in rows)
               + sum(c * jnp.dtype(d).itemsize for c, d in row_outs))
    tm = _tile(m, 1024)
    while 2 * tm * per_row + 4 * tk * n > MM_VMEM_BUDGET:
        tm = _tile(m, tm - LANES)
    dot = _dot_nt if dims == "nt" else _dot
    n_in = 2 + len(rows) + len(fulls) + (after is not None)
    n_out = len(row_outs) + len(acc_outs)

    def body(*refs):
        ins, outs, scratch = refs[:n_in], refs[n_in:n_in + n_out], refs[n_in + n_out:]
        row_refs, acc_refs = outs[:len(row_outs)], outs[len(row_outs):]
        step = pl.program_id(1)

        @pl.when(jnp.logical_and(pl.program_id(0) == 0, step == 0))
        def _():
            for r in acc_refs:
                r[...] = jnp.zeros_like(r)

        part = dot(ins[0][...], ins[1][...])
        if nk > 1:
            part_ref, = scratch

            @pl.when(step == 0)
            def _():
                part_ref[...] = part

            @pl.when(step > 0)
            def _():
                part_ref[...] += part

        @pl.when(step == nk - 1)
        def _():
            result = part_ref[...] if nk > 1 else part
            new_rows, incs = fn(result, *[r[...] for r in ins[2:2 + len(rows) + len(fulls)]])
            for r, val in zip(row_refs, new_rows):
                r[...] = val.astype(r.dtype)
            for r, inc in zip(acc_refs, incs):
                r[...] += inc

    tile_rows = lambda c: pl.BlockSpec((tm, c), lambda i, s: (i, 0))
    b_spec = pl.BlockSpec((tk, n), lambda i, s: (s, 0)) if dims == "nn" else _full(b.shape)
    extra, extra_specs = ([after], [pl.BlockSpec(memory_space=pl.ANY)]) if after is not None else ([], [])
    return pl.pallas_call(
        body, name=name, grid=(m // tm, nk),
        in_specs=[pl.BlockSpec((tm, tk), lambda i, s: (i, s)), b_spec] + [tile_rows(r.shape[1]) for r in rows]
        + [_full(f.shape) for f in fulls] + extra_specs,
        out_specs=[tile_rows(c) for c, _ in row_outs] + [_full(s) for s in acc_outs],
        out_shape=[jax.ShapeDtypeStruct((m, c), d) for c, d in row_outs]
        + [jax.ShapeDtypeStruct(s, F32) for s in acc_outs],
        scratch_shapes=[pltpu.VMEM((tm, n), F32)] if nk > 1 else [],
        compiler_params=_params(2),
    )(a, b, *rows, *fulls, *extra)


def _residual_norm(o, x, w):
    h = x + o
    return (h, _rms(h, w)), ()


def _norm_backward(dn, h, dres, w):
    _, vjp = jax.vjp(_rms, h, w)
    dh, dw = vjp(dn)
    dh = dh + dres
    return (dh, dh), (dw,)


def _loss_and_grad(dn, h1, target, w):
    yf, vjp = jax.vjp(_rms, h1 + dn, w)
    err = yf - target
    loss = 0.5 * jnp.sum(jnp.mean(err * err, axis=-1, keepdims=True))
    dh, dw = vjp(err * (1.0 / err.shape[-1]))
    return (dh, dh), (jnp.full((8, LANES), loss, F32), dw)


def _wgrad(a, d, name, after=None):
    return _mm(a, d, "tn", name, out_dtype=BF16, after=after)


def _norm_fwd(x, w, name, after=None, tm=512):
    t, d = x.shape

    def body(x_ref, w_ref, *rest):
        rest[-1][...] = _rms(x_ref[...], w_ref[...]).astype(BF16)

    extra, extra_specs = ([after], [_full(after.shape)]) if after is not None else ([], [])
    return pl.pallas_call(
        body, name=name, grid=(t // tm,), in_specs=[_rows(tm, d), _full((1, d))] + extra_specs,
        out_specs=_rows(tm, d), out_shape=jax.ShapeDtypeStruct((t, d), BF16), compiler_params=_params(1),
    )(x, w, *extra)


def _conv_a_fwd(xbc, cw, cb, tm=256, col=0):
    t, c = xbc.shape[0], cw.shape[1]

    def body(x_ref, h_ref, w_ref, b_ref, o_ref, y_ref):
        halo = jnp.where(pl.program_id(0) > 0, h_ref[...], 0.0)
        y = _causal_conv(x_ref[...], halo, w_ref[...], b_ref[...])
        y_ref[...] = y
        o_ref[...] = _silu(y)

    return pl.pallas_call(
        body, name="conv_a_fwd", grid=(t // tm,),
        in_specs=[_rows(tm, c, col=col), _halo(tm, c, col=col), _full(cw.shape), _full((1, c))],
        out_specs=[_rows(tm, c)] * 2, out_shape=[jax.ShapeDtypeStruct((t, c), F32)] * 2, compiler_params=_params(1),
    )(xbc, xbc, cw, cb)


def _ssd_common(dtr, dtb, alog, e_t):
    row = lax.broadcasted_iota(jnp.int32, (CHUNK, CHUNK), 0)
    col = lax.broadcasted_iota(jnp.int32, (CHUNK, CHUNK), 1)
    causal = row >= col
    dt = _softplus(dtr + dtb)
    a = -jnp.exp(alog)
    acum = jnp.dot(causal.astype(F32), dt * a, precision=HIGHEST, preferred_element_type=F32)
    spread = lambda v: _dot_split(v, e_t)
    return dict(dt=dt, a=a, acum=acum, acum_t=acum.T, causal=causal, row=row, col=col, lane_lo=col < SSD_HEAD_DIM,
                dt_x=spread(dt), ecol_x=spread(jnp.exp(acum)), dsr_x=spread(jnp.exp(acum[CHUNK - 1:CHUNK, :] - acum)))


def _head_decay(c, h, transposed=False):
    d = c["acum"][:, h:h + 1] - c["acum_t"][h:h + 1, :]
    if transposed:
        return jnp.exp(jnp.where(c["row"] <= c["col"], -d, -jnp.inf))
    return jnp.exp(jnp.where(c["causal"], d, -jnp.inf))


def _ssd_fwd(xc, dtr, z, dtb, alog, dsk, nw, e_t, z_col=0):
    t = xc.shape[0]
    nc = t // CHUNK

    def body(xs_ref, b_ref, c_ref, dtr_ref, z_ref, dtb_ref, alog_ref, dsk_ref, nw_ref, et_ref,
             y_ref, ya_ref, sp_ref, s_scr):
        @pl.when(pl.program_id(0) == 0)
        def _():
            s_scr[...] = jnp.zeros_like(s_scr)

        c = _ssd_common(dtr_ref[...], dtb_ref[...], alog_ref[...], et_ref[...])
        lane_lo = c["lane_lo"]
        dsk = dsk_ref[...]
        for g in range(SSD_GROUPS):
            gs = slice(g * SSD_STATE, (g + 1) * SSD_STATE)
            bg_t, cg = b_ref[:, gs].T.astype(BF16), c_ref[:, gs].astype(BF16)
            cb = _dot(cg, bg_t)
            for pp in range(PAIRS_PER_GROUP):
                j = g * PAIRS_PER_GROUP + pp
                ps = slice(j * LANES, (j + 1) * LANES)
                x = xs_ref[:, ps]
                ecol, dsr = c["ecol_x"][:, ps], c["dsr_x"][:, ps]
                xdt = x * c["dt_x"][:, ps]
                xb = xdt.astype(BF16)
                zero = jnp.zeros_like(xb)
                yd = (_dot((cb * _head_decay(c, 2 * j)).astype(BF16), jnp.where(lane_lo, xb, zero))
                      + _dot((cb * _head_decay(c, 2 * j + 1)).astype(BF16), jnp.where(lane_lo, zero, xb)))
                sp = s_scr[j]
                yo = ecol * _dot(cg, sp.astype(BF16))
                st = _dot(bg_t, (xdt * dsr).astype(BF16))
                sp_ref[0, j] = sp
                s_scr[j] = ecol[CHUNK - 1:CHUNK] * sp + st
                dskp = jnp.where(lane_lo[0:1], dsk[:, 2 * j:2 * j + 1], dsk[:, 2 * j + 1:2 * j + 2])
                y_ref[:, ps] = yd + yo + dskp * x
        ya_ref[...] = _rms(y_ref[...] * _silu(z_ref[...]), nw_ref[...]).astype(BF16)

    ck = lambda n, col=0: pl.BlockSpec((CHUNK, n), lambda c: (c, col))
    return pl.pallas_call(
        body, name="ssd_fwd", grid=(nc,),
        in_specs=[ck(SSD_INNER), ck(SSD_BC, SSD_INNER // SSD_BC), ck(SSD_BC, SSD_INNER // SSD_BC + 1), ck(DT_PAD),
                  ck(SSD_INNER, z_col), _full((1, DT_PAD)), _full((1, DT_PAD)), _full((1, DT_PAD)),
                  _full((1, SSD_INNER)), _full(e_t.shape)],
        out_specs=[ck(SSD_INNER), ck(SSD_INNER),
                   pl.BlockSpec((1, N_PAIRS, SSD_STATE, LANES), lambda c: (c, 0, 0, 0))],
        out_shape=[jax.ShapeDtypeStruct((t, SSD_INNER), F32), jax.ShapeDtypeStruct((t, SSD_INNER), BF16),
                   jax.ShapeDtypeStruct((nc, N_PAIRS, SSD_STATE, LANES), F32)],
        scratch_shapes=[pltpu.VMEM((N_PAIRS, SSD_STATE, LANES), F32)], compiler_params=_params(1),
    )(xc, xc, xc, dtr, z, dtb, alog, dsk, nw, e_t)


def _ssd_bwd(dya, y, z, xc, dtr, sprev, dtb, alog, dsk, nw, e_heads, e_t, z_col=0, into=None):
    t = xc.shape[0]
    nc = t // CHUNK
    more, more_specs, alias = _into(into, 14, 0)

    def body(dya_ref, y_ref, z_ref, xs_ref, b_ref, c_ref, dtr_ref, sp_ref, dtb_ref, alog_ref, dsk_ref, nw_ref, e_ref,
             et_ref, *rest):
        dz_ref, dxs_ref, db_ref, dc_ref, ddtr_ref, dnw_ref, ddtb_ref, dalog_ref, ddsk_ref, ds_scr = rest[len(more):]

        @pl.when(pl.program_id(0) == 0)
        def _():
            ds_scr[...] = jnp.zeros_like(ds_scr)
            for r in (dnw_ref, ddtb_ref, dalog_ref, ddsk_ref):
                r[...] = jnp.zeros_like(r)

        y = y_ref[...]
        _, gate_vjp = jax.vjp(lambda y_, z_, w_: _rms(y_ * _silu(z_), w_), y, z_ref[...], nw_ref[...])
        dy, dz, dnw = gate_vjp(dya_ref[...])
        dz_ref[...] = dz.astype(BF16)
        dnw_ref[...] += dnw

        dtr = dtr_ref[...]
        c = _ssd_common(dtr, dtb_ref[...], alog_ref[...], et_ref[...])
        dt, a, lane_lo, row, col = c["dt"], c["a"], c["lane_lo"], c["row"], c["col"]
        dsk = dsk_ref[...]
        p_a, p_dt, v_last = [], [], []
        da_cols = jnp.zeros((CHUNK, CHUNK), F32)
        da_rows = jnp.zeros((CHUNK, CHUNK), F32)
        for g in range(SSD_GROUPS):
            gs = slice(g * SSD_STATE, (g + 1) * SSD_STATE)
            bg, cg = b_ref[:, gs].astype(BF16), c_ref[:, gs].astype(BF16)
            bg_t, cg_t = b_ref[:, gs].T.astype(BF16), c_ref[:, gs].T.astype(BF16)
            cb, cb_t = _dot(cg, bg_t), _dot(bg, cg_t)
            dcb = jnp.zeros((CHUNK, CHUNK), F32)
            dbg = jnp.zeros((CHUNK, SSD_STATE), F32)
            dcg = jnp.zeros((CHUNK, SSD_STATE), F32)
            for pp in range(PAIRS_PER_GROUP):
                j = g * PAIRS_PER_GROUP + pp
                ps = slice(j * LANES, (j + 1) * LANES)
                x = xs_ref[:, ps]
                dtp, ecol, dsr = c["dt_x"][:, ps], c["ecol_x"][:, ps], c["dsr_x"][:, ps]
                elast = ecol[CHUNK - 1:CHUNK]
                xdt = x * dtp
                xb = xdt.astype(BF16)
                dskp = jnp.where(lane_lo[0:1], dsk[:, 2 * j:2 * j + 1], dsk[:, 2 * j + 1:2 * j + 2])
                dyp = dy[:, ps]
                dyb = dyp.astype(BF16)
                sp, dsn = sp_ref[0, j], ds_scr[j]
                spb, dsnb = sp.astype(BF16), dsn.astype(BF16)
                y_off = ecol * _dot(cg, spb)
                dw = (dyp * ecol).astype(BF16)
                dcg = dcg + _dot_nt(dw, spb)
                dsp = _dot(cg_t, dw) + elast * dsn
                xd = xdt * dsr
                zd = _dot(bg, dsnb) * dsr
                dbg = dbg + _dot_nt(xd.astype(BF16), dsnb)
                dxdt = zd
                zero = jnp.zeros_like(xb)
                for h, lm in ((2 * j, lane_lo), (2 * j + 1, jnp.logical_not(lane_lo))):
                    le = _head_decay(c, h)
                    dm = _dot_nt(jnp.where(lm, dyb, zero), jnp.where(lm, xb, zero))
                    dcb = dcb + dm * le
                    m = cb * le
                    m_t = (cb_t * _head_decay(c, h, transposed=True)).astype(BF16)
                    dxdt = dxdt + jnp.where(lm, _dot(m_t, dyb), 0.0)
                    q = dm * m
                    da_cols = da_cols + jnp.where(col == h, jnp.sum(q, axis=1, keepdims=True), 0.0)
                    da_rows = da_rows + jnp.where(row == h, _colsum(q), 0.0)
                ds_scr[j] = dsp
                dxs_ref[:, ps] = dxdt * dtp + dskp * dyp
                p_a.append(dyp * y_off - xdt * zd)
                p_dt.append(dxdt * x)
                v_last.append(_colsum(zd * xdt) + elast * _colsum(dsn * sp))
            dcbb = dcb.astype(BF16)
            db_ref[:, gs] = dbg + _dot_tn(dcbb, cg)
            dc_ref[:, gs] = dcg + _dot(dcbb, bg)
        e = e_ref[...]
        rows8 = jnp.concatenate([jnp.concatenate(v_last, axis=1), _colsum(dy * xs_ref[...]),
                                 jnp.zeros((6, SSD_INNER), F32)], axis=0)
        r8 = _dot_split(rows8, e)
        da = (_dot_split(jnp.concatenate(p_a, axis=1), e) + jnp.where(row == CHUNK - 1, r8[0:1], 0.0)
              + da_cols - da_rows.T)
        ddsk_ref[...] += r8[1:2]
        dadt = jnp.dot((row <= col).astype(F32), da, precision=HIGHEST, preferred_element_type=F32)
        ddt = dadt * a + _dot_split(jnp.concatenate(p_dt, axis=1), e)
        dalog_ref[...] += _colsum(dadt * dt) * a
        ddtr = ddt * _sigmoid(dtr + dtb_ref[...])
        ddtr_ref[...] = ddtr
        ddtb_ref[...] += _colsum(ddtr)

    ck = lambda n, col=0: pl.BlockSpec((CHUNK, n), lambda c: (nc - 1 - c, col))
    acc = lambda n: _full((1, n))
    return pl.pallas_call(
        body, name="ssd_bwd", grid=(nc,),
        in_specs=[ck(SSD_INNER), ck(SSD_INNER), ck(SSD_INNER, z_col), ck(SSD_INNER), ck(SSD_BC, SSD_INNER // SSD_BC),
                  ck(SSD_BC, SSD_INNER // SSD_BC + 1), ck(DT_PAD),
                  pl.BlockSpec((1, N_PAIRS, SSD_STATE, LANES), lambda c: (nc - 1 - c, 0, 0, 0)),
                  acc(DT_PAD), acc(DT_PAD), acc(DT_PAD), acc(SSD_INNER), _full((SSD_INNER, LANES)),
                  _full((LANES, SSD_INNER))] + more_specs,
        out_specs=[ck(SSD_INNER, z_col if into is not None else 0), ck(SSD_INNER), ck(SSD_BC), ck(SSD_BC), ck(DT_PAD),
                   acc(SSD_INNER), acc(DT_PAD), acc(DT_PAD), acc(DT_PAD)],
        out_shape=[jax.ShapeDtypeStruct(into.shape if into is not None else (t, SSD_INNER), BF16),
                   jax.ShapeDtypeStruct((t, SSD_INNER), F32),
                   jax.ShapeDtypeStruct((t, SSD_BC), F32), jax.ShapeDtypeStruct((t, SSD_BC), F32),
                   jax.ShapeDtypeStruct((t, DT_PAD), F32), jax.ShapeDtypeStruct((1, SSD_INNER), F32),
                   jax.ShapeDtypeStruct((1, DT_PAD), F32), jax.ShapeDtypeStruct((1, DT_PAD), F32),
                   jax.ShapeDtypeStruct((1, DT_PAD), F32)],
        scratch_shapes=[pltpu.VMEM((N_PAIRS, SSD_STATE, LANES), F32)], compiler_params=_params(1), **alias,
    )(dya, y, z, xc, xc, xc, dtr, sprev, dtb, alog, dsk, nw, e_heads, e_t, *more)


def _sgu_act(uv, uvb, lnw, lnb):
    a = _gelu(uv + uvb)
    return a[:, :SGU_WIDTH], _layer_norm(a[:, SGU_WIDTH:], lnw, lnb)


def _sgu_weights(ws_ref):
    row = lax.broadcasted_iota(jnp.int32, (CHUNK, CHUNK), 0)
    col = lax.broadcasted_iota(jnp.int32, (CHUNK, CHUNK), 1)
    return [jnp.where(row >= col, ws_ref[g], 0.0).astype(BF16) for g in range(SGU_GROUPS)], row >= col


def _sgu_fwd(uv, uvb, lnw, lnb, ws, bs_t, col=0):
    t = uv.shape[0]

    def body(uv_ref, uvb_ref, lnw_ref, lnb_ref, ws_ref, bs_ref, o_ref):
        u, vn = _sgu_act(uv_ref[...], uvb_ref[...], lnw_ref[...], lnb_ref[...])
        wc, _ = _sgu_weights(ws_ref)
        bs = bs_ref[...]
        for g in range(SGU_GROUPS):
            gs = slice(g * LANES, (g + 1) * LANES)
            mixed = _dot(wc[g], vn[:, gs].astype(BF16)) + bs[:, g:g + 1]
            o_ref[:, gs] = (u[:, gs] * mixed).astype(BF16)

    return pl.pallas_call(
        body, name="sgu_fwd", grid=(t // CHUNK,),
        in_specs=[_rows(CHUNK, 2 * SGU_WIDTH, col=col), _full((1, 2 * SGU_WIDTH)), _full((1, SGU_WIDTH)), _full((1, SGU_WIDTH)),
                  _full(ws.shape), _full(bs_t.shape)],
        out_specs=_rows(CHUNK, SGU_WIDTH), out_shape=jax.ShapeDtypeStruct((t, SGU_WIDTH), BF16),
        compiler_params=_params(1),
    )(uv, uvb, lnw, lnb, ws, bs_t)


def _sgu_bwd(dyb, uv, uvb, lnw, lnb, ws, bs_t, e_groups, col=0, into=None):
    t = uv.shape[0]
    more, more_specs, alias = _into(into, 8, 0)

    def body(dyb_ref, uv_ref, uvb_ref, lnw_ref, lnb_ref, ws_ref, bs_ref, e_ref, *rest):
        duv_ref, duvb_ref, dlnw_ref, dlnb_ref, dws_ref, dbs_ref = rest[len(more):]

        @pl.when(pl.program_id(0) == 0)
        def _():
            for r in (duvb_ref, dlnw_ref, dlnb_ref, dws_ref, dbs_ref):
                r[...] = jnp.zeros_like(r)

        (u, vn), act_vjp = jax.vjp(_sgu_act, uv_ref[...], uvb_ref[...], lnw_ref[...], lnb_ref[...])
        wc, causal = _sgu_weights(ws_ref)
        bs = bs_ref[...]
        dyb = dyb_ref[...]
        du, dvn, dmix = [], [], []
        for g in range(SGU_GROUPS):
            gs = slice(g * LANES, (g + 1) * LANES)
            vb = vn[:, gs].astype(BF16)
            mixed = _dot(wc[g], vb) + bs[:, g:g + 1]
            dm = dyb[:, gs] * u[:, gs]
            dmb = dm.astype(BF16)
            du.append(dyb[:, gs] * mixed)
            dvn.append(_dot_tn(wc[g], dmb))
            dws_ref[g] += jnp.where(causal, _dot_nt(dmb, vb), 0.0)
            dmix.append(dm)
        dbs_ref[...] += _dot_split(jnp.concatenate(dmix, axis=1), e_ref[...])
        duv, duvb, dlnw, dlnb = act_vjp((jnp.concatenate(du, axis=1), jnp.concatenate(dvn, axis=1)))
        duv_ref[...] = duv.astype(BF16)
        duvb_ref[...] += duvb
        dlnw_ref[...] += dlnw
        dlnb_ref[...] += dlnb

    return pl.pallas_call(
        body, name="sgu_bwd", grid=(t // CHUNK,),
        in_specs=[_rows(CHUNK, SGU_WIDTH), _rows(CHUNK, 2 * SGU_WIDTH, col=col), _full((1, 2 * SGU_WIDTH)),
                  _full((1, SGU_WIDTH)), _full((1, SGU_WIDTH)), _full(ws.shape), _full(bs_t.shape),
                  _full(e_groups.shape)] + more_specs,
        out_specs=[_rows(CHUNK, 2 * SGU_WIDTH, col=col if into is not None else 0), _full((1, 2 * SGU_WIDTH)),
                   _full((1, SGU_WIDTH)), _full((1, SGU_WIDTH)), _full(ws.shape), _full(bs_t.shape)],
        out_shape=[jax.ShapeDtypeStruct(into.shape if into is not None else (t, 2 * SGU_WIDTH), BF16),
                   jax.ShapeDtypeStruct((1, 2 * SGU_WIDTH), F32),
                   jax.ShapeDtypeStruct((1, SGU_WIDTH), F32), jax.ShapeDtypeStruct((1, SGU_WIDTH), F32),
                   jax.ShapeDtypeStruct(ws.shape, F32), jax.ShapeDtypeStruct(bs_t.shape, F32)],
        compiler_params=_params(1), **alias,
    )(dyb, uv, uvb, lnw, lnb, ws, bs_t, e_groups, *more)


def _merge(gates, pa, pb, bg):
    s = _sigmoid(gates + bg)
    return s[:, :D_MODEL] * pa + s[:, D_MODEL:] * pb


def _merge_fwd(gates, pa, pb, bg, tm=256, col=0):
    t = gates.shape[0]

    def body(g_ref, pa_ref, pb_ref, bg_ref, o_ref):
        o_ref[...] = _merge(g_ref[...], pa_ref[...], pb_ref[...], bg_ref[...]).astype(BF16)

    return pl.pallas_call(
        body, name="merge_fwd", grid=(t // tm,),
        in_specs=[_rows(tm, 2 * D_MODEL, col=col), _rows(tm, D_MODEL), _rows(tm, D_MODEL), _full((1, 2 * D_MODEL))],
        out_specs=_rows(tm, D_MODEL), out_shape=jax.ShapeDtypeStruct((t, D_MODEL), BF16), compiler_params=_params(1),
    )(gates, pa, pb, bg)


def _merge_bwd(dmix, gates, pa, pb, bg, tm=256, col=0, into=None):
    t = gates.shape[0]
    more, more_specs, alias = _into(into, 5, 0)

    def body(d_ref, g_ref, pa_ref, pb_ref, bg_ref, *rest):
        dg_ref, dpa_ref, dpb_ref, dbg_ref = rest[len(more):]

        @pl.when(pl.program_id(0) == 0)
        def _():
            dbg_ref[...] = jnp.zeros_like(dbg_ref)

        _, vjp = jax.vjp(_merge, g_ref[...], pa_ref[...], pb_ref[...], bg_ref[...])
        dg, dpa, dpb, dbg = vjp(d_ref[...])
        dg_ref[...] = dg.astype(BF16)
        dpa_ref[...] = dpa.astype(BF16)
        dpb_ref[...] = dpb.astype(BF16)
        dbg_ref[...] += dbg

    return pl.pallas_call(
        body, name="merge_bwd", grid=(t // tm,),
        in_specs=[_rows(tm, D_MODEL), _rows(tm, 2 * D_MODEL, col=col), _rows(tm, D_MODEL), _rows(tm, D_MODEL),
                  _full((1, 2 * D_MODEL))] + more_specs,
        out_specs=[_rows(tm, 2 * D_MODEL, col=col if into is not None else 0), _rows(tm, D_MODEL),
                   _rows(tm, D_MODEL), _full((1, 2 * D_MODEL))],
        out_shape=[jax.ShapeDtypeStruct(into.shape if into is not None else (t, 2 * D_MODEL), BF16),
                   jax.ShapeDtypeStruct((t, D_MODEL), BF16), jax.ShapeDtypeStruct((t, D_MODEL), BF16),
                   jax.ShapeDtypeStruct((1, 2 * D_MODEL), F32)],
        compiler_params=_params(1), **alias,
    )(dmix, gates, pa, pb, bg, *more)


def _conv_f_fwd(up, cw, cb, tm=128):
    t, c = up.shape

    def body(x_ref, h_ref, w_ref, b_ref, o_ref, y_ref):
        halo = jnp.where(pl.program_id(0) > 0, h_ref[...], 0.0)
        y = _causal_conv(x_ref[...], halo, w_ref[...], b_ref[...])
        y_ref[...] = y
        o_ref[...] = (_silu(y[:, :D_FF]) * y[:, D_FF:]).astype(BF16)

    return pl.pallas_call(
        body, name="conv_f_fwd", grid=(t // tm,),
        in_specs=[_rows(tm, c), _halo(tm, c), _full(cw.shape), _full((1, c))],
        out_specs=[_rows(tm, D_FF), _rows(tm, c)],
        out_shape=[jax.ShapeDtypeStruct((t, D_FF), BF16), jax.ShapeDtypeStruct((t, c), F32)],
        compiler_params=_params(1),
    )(up, up, cw, cb)


def _conv_f_bwd(dact, y, up, cw, tm=128):
    t, c = up.shape
    nt = t // tm

    def body(d_ref, y_ref, x_ref, w_ref, dx_ref, dw_ref, db_ref, nxt_scr):
        @pl.when(pl.program_id(0) == 0)
        def _():
            nxt_scr[...] = jnp.zeros_like(nxt_scr)
            dw_ref[...] = jnp.zeros_like(dw_ref)
            db_ref[...] = jnp.zeros_like(db_ref)

        a, v = y_ref[:, :D_FF], y_ref[:, D_FF:]
        d = d_ref[...]
        dy = jnp.concatenate([d * v * _dsilu(a), d * _silu(a)], axis=1)
        dx, dw = _causal_conv_bwd(dy, nxt_scr[...], x_ref[...], w_ref[...])
        dx_ref[...] = dx.astype(BF16)
        nxt_scr[...] = dy[:8]
        dw_ref[...] += dw
        db_ref[...] += _colsum(dy)

    return pl.pallas_call(
        body, name="conv_f_bwd", grid=(nt,),
        in_specs=[_rows(tm, D_FF, nt, True), _rows(tm, c, nt, True), _rows(tm, c, nt, True), _full(cw.shape)],
        out_specs=[_rows(tm, c, nt, True), _full(cw.shape), _full((1, c))],
        out_shape=[jax.ShapeDtypeStruct((t, c), BF16), jax.ShapeDtypeStruct(cw.shape, F32),
                   jax.ShapeDtypeStruct((1, c), F32)],
        scratch_shapes=[pltpu.VMEM((8, c), F32)], compiler_params=_params(1),
    )(dact, y, up, cw)


def _conv_a_bwd(dxs, db, dc, y, xbc, cw, tm=256, col=0, into=None):
    t, c = xbc.shape[0], cw.shape[1]
    nt = t // tm
    more, more_specs, alias = _into(into, 6, 0)

    def body(dxs_ref, db_ref, dc_ref, y_ref, x_ref, w_ref, *rest):
        dx_ref, dw_ref, dbias_ref, nxt_scr = rest[len(more):]

        @pl.when(pl.program_id(0) == 0)
        def _():
            nxt_scr[...] = jnp.zeros_like(nxt_scr)
            dw_ref[...] = jnp.zeros_like(dw_ref)
            dbias_ref[...] = jnp.zeros_like(dbias_ref)

        dy = jnp.concatenate([dxs_ref[...], db_ref[...], dc_ref[...]], axis=1) * _dsilu(y_ref[...])
        dx, dw = _causal_conv_bwd(dy, nxt_scr[...], x_ref[...], w_ref[...])
        dx_ref[...] = dx.astype(BF16)
        nxt_scr[...] = dy[:8]
        dw_ref[...] += dw
        dbias_ref[...] += _colsum(dy)

    return pl.pallas_call(
        body, name="conv_a_bwd", grid=(nt,),
        in_specs=[_rows(tm, SSD_INNER, nt, True), _rows(tm, SSD_BC, nt, True), _rows(tm, SSD_BC, nt, True),
                  _rows(tm, c, nt, True), _rows(tm, c, nt, True, col), _full(cw.shape)] + more_specs,
        out_specs=[_rows(tm, c, nt, True, col if into is not None else 0), _full(cw.shape), _full((1, c))],
        out_shape=[jax.ShapeDtypeStruct(into.shape if into is not None else (t, c), BF16),
                   jax.ShapeDtypeStruct(cw.shape, F32), jax.ShapeDtypeStruct((1, c), F32)],
        scratch_shapes=[pltpu.VMEM((8, c), F32)], compiler_params=_params(1), **alias,
    )(dxs, db, dc, y, xbc, cw, *more)


def _pad_lanes(v, n=DT_PAD):
    return jnp.pad(v, ((0, 0), (0, n - v.shape[1])))


def _local_step(x, target, w, p, after=None, late_weights=None, on_grad=None, on_small=None):
    dtb, alog, dsk = _pad_lanes(p["dt_bias"]), _pad_lanes(p["a_log"]), _pad_lanes(p["d_skip"])
    bs_t = _pad_lanes(p["b_spatial"].T)
    e_heads = (jnp.arange(SSD_INNER)[:, None] // SSD_HEAD_DIM == jnp.arange(LANES)[None, :]).astype(BF16)
    e_heads_t = (jnp.arange(LANES)[:, None] == jnp.arange(SSD_INNER)[None, :] // SSD_HEAD_DIM).astype(BF16)
    e_groups = (jnp.arange(SGU_WIDTH)[:, None] // LANES == jnp.arange(LANES)[None, :]).astype(BF16)

    n1 = _norm_fwd(x, p["norm1_w"], "norm1_fwd", after=after)
    z = _mm(n1, w["z"], "nt", "proj_z")
    xbc = _mm(n1, w["xbc"], "nt", "proj_xbc")
    dtr = _mm(n1, w["dt"], "nt", "proj_dt")
    uv = _mm(n1, w["uv"], "nt", "proj_uv")
    gates = _mm(n1, w["gates"], "nt", "proj_gates")
    xc, conv_a_out = _conv_a_fwd(xbc, w["conv_a"], p["conv_a_b"])
    y, ya, sprev = _ssd_fwd(xc, dtr, z, dtb, alog, dsk, p["ssd_norm_w"], e_heads_t)
    yb = _sgu_fwd(uv, p["uv_b"], p["v_ln_w"], p["v_ln_b"], p["w_spatial"], bs_t)
    if late_weights is not None:
        w = {**w, **late_weights(ya, yb)}
    pa = _mm(ya, w["branch_a"], "nn", "branch_a")
    pb = _mm(yb, w["branch_b"], "nn", "branch_b")
    mix = _merge_fwd(gates, pa, pb, p["b_gate"])
    wide = [(D_MODEL, F32), (D_MODEL, BF16)]
    h1, n2 = _mm_rows(mix, w["out"], "nn", "out_proj", _residual_norm, rows=[x], fulls=[p["norm2_w"]], row_outs=wide)
    up = _mm(n2, w["up"], "nt", "up_proj")
    act, conv_f_out = _conv_f_fwd(up, w["conv_f"], p["conv_f_b"])
    dh2, dh2b, loss, g_final = _mm_rows(
        act, w["down"], "nn", "down_proj", _loss_and_grad, rows=[h1, target], fulls=[p["final_norm_w"]],
        row_outs=wide, acc_outs=[(8, LANES), (1, D_MODEL)])

    on_grad = on_grad or (lambda name, grads: None)
    g = {"final_norm_w": g_final}
    g["down"] = _wgrad(act, dh2b, "down_wgrad")
    tok = on_grad("w_down", g)
    dact = _mm(dh2b, w["down"], "nt", "down_dgrad", after=tok)
    dup, g["conv_f"], g["conv_f_b"] = _conv_f_bwd(dact, conv_f_out, up, w["conv_f"])
    g["up"] = _wgrad(dup, n2, "up_wgrad")
    tok = on_grad("w_up", g)
    dh1, dh1b, g["norm2_w"] = _mm_rows(
        dup, w["up"], "nn", "up_dgrad", _norm_backward, rows=[h1, dh2], fulls=[p["norm2_w"]], row_outs=wide,
        acc_outs=[(1, D_MODEL)], after=tok)
    g["out"] = _wgrad(mix, dh1b, "out_wgrad")
    tok = on_grad("w_out", g)
    dmix = _mm(dh1b, w["out"], "nt", "out_dgrad", after=tok)
    dgates, dpa, dpb, g["b_gate"] = _merge_bwd(dmix, gates, pa, pb, p["b_gate"])
    g["branch_a"] = _wgrad(ya, dpa, "branch_a_wgrad")
    g["branch_b"] = _wgrad(yb, dpb, "branch_b_wgrad")
    tok = on_grad("w_branch", g)
    dya = _mm(dpa, w["branch_a"], "nt", "branch_a_dgrad", after=tok)
    dyb = _mm(dpb, w["branch_b"], "nt", "branch_b_dgrad", after=tok)
    duv, g["uv_b"], g["v_ln_w"], g["v_ln_b"], g["w_spatial"], dbs_t = _sgu_bwd(
        dyb, uv, p["uv_b"], p["v_ln_w"], p["v_ln_b"], p["w_spatial"], bs_t, e_groups)
    g["b_spatial"] = dbs_t[:, :SGU_GROUPS].T
    dz, dxs, db, dc, ddtr, g["ssd_norm_w"], ddtb, dalog, ddsk = _ssd_bwd(
        dya, y, z, xc, dtr, sprev, dtb, alog, dsk, p["ssd_norm_w"], e_heads, e_heads_t)
    g["dt_bias"], g["a_log"], g["d_skip"] = ddtb, dalog, ddsk
    dxbc, g["conv_a"], g["conv_a_b"] = _conv_a_bwd(dxs, db, dc, conv_a_out, xbc, w["conv_a"])
    tok = on_small(g, loss) if on_small else None
    ddtrb = ddtr.astype(BF16)
    for name, d in (("z", dz), ("xbc", dxbc), ("dt", ddtrb), ("uv", duv), ("gates", dgates)):
        g[name] = _wgrad(d, n1, name + "_wgrad", after=tok)
    tok = on_grad("w_in", g)
    dn1 = _mm(dz, w["z"], "nn", "z_dgrad", after=tok)
    dn1 = _mm(dxbc, w["xbc"], "nn", "xbc_dgrad", acc=dn1)
    dn1 = _mm(ddtrb, w["dt"], "nn", "dt_dgrad", acc=dn1)
    dn1 = _mm(duv, w["uv"], "nn", "uv_dgrad", acc=dn1)
    gx, g["norm1_w"] = _mm_rows(
        dgates, w["gates"], "nn", "gates_dgrad",
        lambda r, so_far, h, dres, w_: tuple(t[:1] for t in _norm_backward(r + so_far, h, dres, w_)),
        rows=[dn1, x, dh1], fulls=[p["norm1_w"]], row_outs=wide[:1], acc_outs=[(1, D_MODEL)])
    return loss, gx, g


def _place():
    return lax.axis_index("x"), lax.axis_index("y"), lax.axis_index("c")


def _other_chips(x, y):
    return [(1 - x, y), (x, 1 - y), (1 - x, 1 - y)]


def _all_gather(shards, name):
    n = len(shards)

    def body(*refs):
        ins, outs = refs[:n], refs[n:2 * n]
        send_sems, recv_sems, local_sems = refs[2 * n:]
        x, y, c = _place()
        me, sibling = (x, y, c), (x, y, 1 - c)
        chips = _other_chips(x, y)

        def copy(a, k, block, to, src=None):
            slot = outs[a].at[4 * block[0] + 2 * block[1] + block[2]]
            return pltpu.make_async_remote_copy(
                src_ref=slot if src is None else src, dst_ref=slot, send_sem=send_sems.at[7 * a + k],
                recv_sem=recv_sems.at[7 * a + k], device_id=to, device_id_type=MESH)

        started = []
        for a in range(n):
            mine = pltpu.make_async_copy(ins[a], outs[a].at[4 * x + 2 * y + c], local_sems.at[a])
            mine.start()
            started.append(mine)
        sends = []
        for a in range(n):
            sends.append(copy(a, 0, me, sibling, src=ins[a]))
            sends += [copy(a, 1 + j, me, (*chip, c), src=ins[a]) for j, chip in enumerate(chips)]
        for cp in sends:
            cp.start()
        for a in range(n):
            for j, chip in enumerate(chips):
                copy(a, 1 + j, (*chip, c), me).wait_recv()
                fwd = copy(a, 4 + j, (*chip, c), sibling)
                fwd.start()
                sends.append(fwd)
        for a in range(n):
            copy(a, 0, sibling, me).wait_recv()
            for j, chip in enumerate(chips):
                copy(a, 4 + j, (*chip, 1 - c), me).wait_recv()
        for cp in sends:
            cp.wait_send()
        for mine in started:
            mine.wait()

    any_spec = pl.BlockSpec(memory_space=pl.ANY)
    return pl.pallas_call(
        body, name=name, in_specs=[any_spec] * n, out_specs=[any_spec] * n,
        out_shape=[jax.ShapeDtypeStruct((N_DEV, *s.shape), s.dtype) for s in shards],
        scratch_shapes=[pltpu.SemaphoreType.DMA((7 * n,)), pltpu.SemaphoreType.DMA((7 * n,)),
                        pltpu.SemaphoreType.DMA((n,))],
    )(*shards)


HBM_SPEC = pl.BlockSpec(memory_space=pltpu.HBM)
SEM_SPEC = pl.BlockSpec(memory_space=pltpu.SEMAPHORE)
ANY_SPEC = pl.BlockSpec(memory_space=pl.ANY)
DATAFLOW = pltpu.SideEffectType.DATAFLOW_SIDE_EFFECTING
N_PEERS = N_DEV - 1


def _peers(x, y, c):
    out = []
    for r in range(1, N_DEV):
        fx, fy, fc = r >> 2 & 1, r >> 1 & 1, r & 1
        out.append(((1 - x) if fx else x, (1 - y) if fy else y, (1 - c) if fc else c))
    return out


def _gather_copies(srcs, lands, send_sems, recv_sems, sending, scatter=False):
    x, y, c = _place()
    copies = []
    for a, (src, land) in enumerate(zip(srcs, lands)):
        for j, (px, py, pc) in enumerate(_peers(x, y, c)):
            mine, theirs = 4 * x + 2 * y + c, 4 * px + 2 * py + pc
            block = src.at[theirs if sending else 0] if scatter else src
            copies.append(pltpu.make_async_remote_copy(
                src_ref=block, dst_ref=land.at[mine if sending else theirs], send_sem=send_sems.at[N_PEERS * a + j],
                recv_sem=recv_sems.at[N_PEERS * a + j], device_id=(px, py, pc), device_id_type=MESH))
    return copies


def _gather_start(shards, after, name, scatter=False):
    n = len(shards)
    after = [] if after is None else [after]

    def body(*refs):
        srcs, lands = refs[:n], refs[n:2 * n]
        send_sems, recv_sems = refs[2 * n + len(after):2 * n + len(after) + 2]
        token = refs[-1]
        for cp in _gather_copies(srcs, lands, send_sems, recv_sems, sending=True, scatter=scatter):
            cp.start()
        token[...] = jnp.zeros_like(token)

    lands = [lax.empty(s.shape if scatter else (N_DEV, *s.shape), s.dtype) for s in shards]
    hbm = lambda a: pltpu.with_memory_space_constraint(a, pltpu.HBM)
    out = pl.pallas_call(
        body, name=name,
        out_shape=(pltpu.SemaphoreType.DMA((N_PEERS * n,)), pltpu.SemaphoreType.DMA((N_PEERS * n,)),
                   *[pltpu.HBM(a.shape, a.dtype) for a in (*shards, *lands)], jax.ShapeDtypeStruct((8, LANES), F32)),
        in_specs=[HBM_SPEC] * (2 * n) + [ANY_SPEC] * len(after),
        out_specs=(SEM_SPEC, SEM_SPEC, *[HBM_SPEC] * (2 * n), pl.BlockSpec(memory_space=pltpu.VMEM)),
        input_output_aliases={i: 2 + i for i in range(2 * n)},
        compiler_params=pltpu.CompilerParams(has_side_effects=DATAFLOW),
    )(*[hbm(a) for a in (*shards, *lands)], *after)
    return out[0], out[1], out[2:2 + n], out[2 + n:2 + 2 * n], out[-1]


def _gather_wait(send_sems, recv_sems, shards, lands, after, name, scatter=False):
    n = len(shards)
    after = tuple(after)

    def body(*refs):
        srcs, lands_ = refs[:n], refs[n:2 * n]
        send, recv = refs[2 * n:2 * n + 2]
        for cp in _gather_copies(srcs, lands_, send, recv, sending=False, scatter=scatter):
            cp.wait_send()
            cp.wait_recv()

    out = pl.pallas_call(
        body, name=name, out_shape=tuple(pltpu.HBM(a.shape, a.dtype) for a in (*shards, *lands)),
        in_specs=[HBM_SPEC] * (2 * n) + [SEM_SPEC, SEM_SPEC] + [ANY_SPEC] * len(after),
        out_specs=tuple([HBM_SPEC] * (2 * n)), input_output_aliases={i: i for i in range(2 * n)},
        compiler_params=pltpu.CompilerParams(has_side_effects=DATAFLOW),
    )(*shards, *lands, send_sems, recv_sems, *after)
    return out[:n], out[n:]


def _chip_copies(src, land, send_sems, recv_sems):
    x, y, c = _place()
    return [pltpu.make_async_remote_copy(
        src_ref=src.at[2 * cx + cy], dst_ref=land.at[j], send_sem=send_sems.at[j], recv_sem=recv_sems.at[j],
        device_id=(cx, cy, c), device_id_type=MESH) for j, (cx, cy) in enumerate(_other_chips(x, y))]


def _chips_start(q, name):
    def body(q_ref, land_ref, send_sems, recv_sems, q_thru, land_thru, token):
        for cp in _chip_copies(q_ref, land_ref, send_sems, recv_sems):
            cp.start()
        token[...] = jnp.zeros_like(token)

    land = lax.empty((3, *q.shape[1:]), q.dtype)
    return pl.pallas_call(
        body, name=name,
        out_shape=(pltpu.SemaphoreType.DMA((3,)), pltpu.SemaphoreType.DMA((3,)), pltpu.HBM(q.shape, q.dtype),
                   pltpu.HBM(land.shape, land.dtype), jax.ShapeDtypeStruct((8, LANES), F32)),
        in_specs=[HBM_SPEC, HBM_SPEC],
        out_specs=(SEM_SPEC, SEM_SPEC, HBM_SPEC, HBM_SPEC, pl.BlockSpec(memory_space=pltpu.VMEM)),
        input_output_aliases={0: 2, 1: 3}, compiler_params=pltpu.CompilerParams(has_side_effects=DATAFLOW),
    )(pltpu.with_memory_space_constraint(q, pltpu.HBM), pltpu.with_memory_space_constraint(land, pltpu.HBM))


def _chips_wait(send_sems, recv_sems, q, land, after, name):
    def body(q_ref, land_ref, send, recv, after_ref, q_out, land_out):
        for cp in _chip_copies(q_ref, land_ref, send, recv):
            cp.wait_send()
            cp.wait_recv()

    return pl.pallas_call(
        body, name=name, out_shape=(pltpu.HBM(q.shape, q.dtype), pltpu.HBM(land.shape, land.dtype)),
        in_specs=[HBM_SPEC, HBM_SPEC, SEM_SPEC, SEM_SPEC, ANY_SPEC], out_specs=(HBM_SPEC, HBM_SPEC),
        input_output_aliases={0: 0, 1: 1}, compiler_params=pltpu.CompilerParams(has_side_effects=DATAFLOW),
    )(q, land, send_sems, recv_sems, after)[1]


def _exchange_cores(parts, name):
    n = len(parts)

    def body(*refs):
        ins, outs = refs[:n], refs[n:2 * n]
        send_sems, recv_sems = refs[2 * n:]
        x, y, c = _place()
        copies = []
        for a in range(n):
            for k in range(4):
                copies.append(pltpu.make_async_remote_copy(
                    src_ref=ins[a].at[2 * k + (1 - c)], dst_ref=outs[a].at[k], send_sem=send_sems.at[4 * a + k],
                    recv_sem=recv_sems.at[4 * a + k], device_id=(x, y, 1 - c), device_id_type=MESH))
        for cp in copies:
            cp.start()
        for cp in copies:
            cp.wait()

    any_spec = pl.BlockSpec(memory_space=pl.ANY)
    return pl.pallas_call(
        body, name=name, in_specs=[any_spec] * n, out_specs=[any_spec] * n,
        out_shape=[jax.ShapeDtypeStruct((4, *s.shape[1:]), s.dtype) for s in parts],
        scratch_shapes=[pltpu.SemaphoreType.DMA((4 * n,)), pltpu.SemaphoreType.DMA((4 * n,))],
    )(*parts)


def _chip_sum(part, got, place, name, tr=256):
    _, r, c = part.shape
    tr, tc = _tile2d(r, c, tr)

    def body(place_ref, p_ref, g_ref, q_ref, own_ref):
        s = p_ref[0].astype(F32) + g_ref[0].astype(F32)
        q_ref[0] = s.astype(BF16)

        @pl.when(pl.program_id(2) == place_ref[1])
        def _():
            own_ref[...] = s

    grid_spec = pltpu.PrefetchScalarGridSpec(
        num_scalar_prefetch=1, grid=(r // tr, c // tc, 4),
        in_specs=[pl.BlockSpec((1, tr, tc), lambda i, j, k, pr: (2 * k + pr[0], i, j)),
                  pl.BlockSpec((1, tr, tc), lambda i, j, k, pr: (k, i, j))],
        out_specs=[pl.BlockSpec((1, tr, tc), lambda i, j, k, pr: (k, i, j)),
                   pl.BlockSpec((tr, tc), lambda i, j, k, pr: (i, j))])
    return pl.pallas_call(
        body, name=name, grid_spec=grid_spec,
        out_shape=[jax.ShapeDtypeStruct((4, r, c), BF16), jax.ShapeDtypeStruct((r, c), F32)],
        compiler_params=_params(3),
    )(place, part, got)


def _adamw(w, g, m, v):
    m = ADAM_B1 * m + (1.0 - ADAM_B1) * g
    v = ADAM_B2 * v + (1.0 - ADAM_B2) * jnp.square(g)
    m_hat = m / (1.0 - ADAM_B1 ** ADAM_STEP)
    v_hat = v / (1.0 - ADAM_B2 ** ADAM_STEP)
    return -ADAM_LR * (m_hat / (jnp.sqrt(v_hat) + ADAM_EPS) + ADAM_WD * w), m, v


def _sum_adamw(own, got, w, m, v, name, tr=256):
    r, c = own.shape
    if w.ndim == 3:
        tr, tc = r, 4 * LANES
        wblk = pl.BlockSpec((tr, 1, tc), lambda i, j: (i, 0, j))
    else:
        tr, tc = _tile2d(r, c, tr)
        wblk = pl.BlockSpec((tr, tc), lambda i, j: (i, j))

    def body(own_ref, got_ref, w_ref, m_ref, v_ref, g_ref, d_ref, nm_ref, nv_ref):
        g = own_ref[...]
        for j in range(3):
            g = g + got_ref[j].astype(F32)
        two_d = lambda ref: ref[...].reshape(tr, tc)
        delta, nm, nv = _adamw(two_d(w_ref), g, two_d(m_ref), two_d(v_ref))
        for ref, val in ((g_ref, g), (d_ref, delta), (nm_ref, nm), (nv_ref, nv)):
            ref[...] = val.reshape(ref.shape)

    blk = pl.BlockSpec((tr, tc), lambda i, j: (i, j))
    return pl.pallas_call(
        body, name=name, grid=(r // tr, c // tc),
        in_specs=[blk, pl.BlockSpec((3, tr, tc), lambda i, j: (0, i, j)), wblk, wblk, wblk], out_specs=[wblk] * 4,
        out_shape=[jax.ShapeDtypeStruct(w.shape, F32)] * 4, compiler_params=_params(2),
    )(own, got, w, m, v)


def _sum8_adamw(part, got, place, w, m, v, name, tr=256):
    _, r, c = part.shape
    if w.ndim == 3:
        tr, tc = r, 2 * LANES
        blk = pl.BlockSpec((tr, 1, tc), lambda i, j, pr: (i, 0, j))
    else:
        tr, tc = _tile2d(r, c, tr)
        blk = pl.BlockSpec((tr, tc), lambda i, j, pr: (i, j))

    def body(place_ref, own_ref, got_ref, w_ref, m_ref, v_ref, g_ref, d_ref, nm_ref, nv_ref):
        dev = 2 * place_ref[1] + place_ref[0]
        g = jnp.zeros((tr, tc), F32)
        for d in range(N_DEV):
            g = g + jnp.where(dev == d, own_ref[0], got_ref[d]).astype(F32)
        two_d = lambda ref: ref[...].reshape(tr, tc)
        delta, nm, nv = _adamw(two_d(w_ref), g, two_d(m_ref), two_d(v_ref))
        for ref, val in ((g_ref, g), (d_ref, delta), (nm_ref, nm), (nv_ref, nv)):
            ref[...] = val.reshape(ref.shape)

    grid_spec = pltpu.PrefetchScalarGridSpec(
        num_scalar_prefetch=1, grid=(r // tr, c // tc),
        in_specs=[pl.BlockSpec((1, tr, tc), lambda i, j, pr: (2 * pr[1] + pr[0], i, j)),
                  pl.BlockSpec((N_DEV, tr, tc), lambda i, j, pr: (0, i, j)), blk, blk, blk],
        out_specs=[blk] * 4)
    return pl.pallas_call(
        body, name=name, grid_spec=grid_spec, out_shape=[jax.ShapeDtypeStruct(w.shape, F32)] * 4,
        compiler_params=_params(2),
    )(place, part, got, w, m, v)


VECTORS = ["norm1_w", "b_gate", "conv_a_b", "dt_bias", "a_log", "d_skip", "ssd_norm_w", "uv_b", "v_ln_w", "v_ln_b",
           "norm2_w", "conv_f_b", "final_norm_w"]
SMALL_ORDER = VECTORS + ["w_spatial", "b_spatial", "conv_a_w", "conv_f_w"]


ROW_VECTORS = VECTORS[1:]


def _small_adamw(gathered, w, m, v):
    sizes = {n: w[n].shape[1] for n in ROW_VECTORS}
    offs, off = {}, 0
    for n in ROW_VECTORS:
        offs[n] = off
        off += -(-sizes[n] // LANES) * LANES
    loss_off = off
    k = len(SMALL_ORDER)
    n_g = len(gathered)

    def body(*refs):
        row_ref, ws_ref, bs_ref, ca_ref, cf_ref, n1_ref = refs[:n_g]
        w_refs, m_refs, v_refs = (dict(zip(SMALL_ORDER, refs[n_g + i * k:n_g + (i + 1) * k])) for i in range(3))
        outs = refs[n_g + 3 * k:]
        x, y, c = _place()
        dev = 4 * x + 2 * y + c

        def total(ref):
            s = ref[0]
            for d in range(1, N_DEV):
                s = s + ref[d]
            return s

        row = total(row_ref)
        grads = {n: row[:, offs[n]:offs[n] + sizes[n]] for n in ROW_VECTORS}
        grads["norm1_w"], grads["w_spatial"], grads["b_spatial"] = total(n1_ref), total(ws_ref), total(bs_ref)
        for n, ref in (("conv_a_w", ca_ref), ("conv_f_w", cf_ref)):
            whole, cols = total(ref), w_refs[n].shape[1]
            mine = whole[:, :cols]
            for d in range(1, N_DEV):
                mine = jnp.where(dev == d, whole[:, d * cols:(d + 1) * cols], mine)
            grads[n] = mine
        for i, n in enumerate(SMALL_ORDER):
            outs[4 * i][...] = grads[n]
            outs[4 * i + 1][...], outs[4 * i + 2][...], outs[4 * i + 3][...] = _adamw(
                w_refs[n][...], grads[n], m_refs[n][...], v_refs[n][...])
        outs[4 * k][...] = row[:, loss_off:loss_off + LANES]

    out = pl.pallas_call(
        body, name="adamw_small",
        out_shape=[jax.ShapeDtypeStruct(w[n].shape, F32) for n in SMALL_ORDER for _ in range(4)]
        + [jax.ShapeDtypeStruct((1, LANES), F32)],
        compiler_params=_params(0),
    )(*gathered, *[t[n] for t in (w, m, v) for n in SMALL_ORDER])
    return [dict(zip(SMALL_ORDER, out[j:4 * k:4])) for j in range(4)] + [out[4 * k]]


SMALL = ["norm1_w", "b_gate", "conv_a_b", "dt_bias", "a_log", "d_skip", "ssd_norm_w", "uv_b", "v_ln_w", "v_ln_b",
         "w_spatial", "b_spatial", "norm2_w", "conv_f_b", "final_norm_w"]
BIG = ["w_in", "w_branch", "w_out", "w_up", "w_down"]
TRANSPOSED = ("w_in", "w_up")
TWO_LEVEL = ()
WEIGHTS = ["norm1_w", "w_in", "b_gate", "conv_a_w", "conv_a_b", "dt_bias", "a_log", "d_skip", "ssd_norm_w", "uv_b",
           "v_ln_w", "v_ln_b", "w_spatial", "b_spatial", "w_branch", "w_out", "norm2_w", "w_up", "conv_f_w",
           "conv_f_b", "w_down", "final_norm_w"]
IN_SPLITS = [("z", 0, 2048), ("xbc", 2048, 5120), ("dt", 5120, 5152), ("uv", 5152, 7200), ("gates", 7200, 9248)]


def _columns_from_devices(a):
    return a.transpose(1, 0, 2).reshape(a.shape[1], -1)


def kernel(x, norm1_w, w_in, b_gate, conv_a_w, conv_a_b, dt_bias, a_log, d_skip, ssd_norm_w, uv_b, v_ln_w, v_ln_b, w_spatial, b_spatial, w_branch, w_out, norm2_w, w_up, conv_f_w, conv_f_b, w_down, final_norm_w, loss_target, m_norm1_w, m_w_in, m_b_gate, m_conv_a_w, m_conv_a_b, m_dt_bias, m_a_log, m_d_skip, m_ssd_norm_w, m_uv_b, m_v_ln_w, m_v_ln_b, m_w_spatial, m_b_spatial, m_w_branch, m_w_out, m_norm2_w, m_w_up, m_conv_f_w, m_conv_f_b, m_w_down, m_final_norm_w, v_norm1_w, v_w_in, v_b_gate, v_conv_a_w, v_conv_a_b, v_dt_bias, v_a_log, v_d_skip, v_ssd_norm_w, v_uv_b, v_v_ln_w, v_v_ln_b, v_w_spatial, v_b_spatial, v_w_branch, v_w_out, v_norm2_w, v_w_up, v_conv_f_w, v_conv_f_b, v_w_down, v_final_norm_w):
    args = dict(locals())
    wts = {n: args[n] for n in WEIGHTS}
    mom = {n: args["m_" + n] for n in WEIGHTS}
    var = {n: args["v_" + n] for n in WEIGHTS}
    cx, cy, cc = _place()
    dev = 4 * cx + 2 * cy + cc
    place = jnp.stack([cc, 2 * cx + cy]).astype(jnp.int32)

    def shard2d(n, a):
        return a[0].T if n in TRANSPOSED else a[0]

    def unshard(n, b):
        return (b.T if n in TRANSPOSED else b)[None]

    g_in, g_conv_a, g_conv_f = _all_gather(
        [shard2d("w_in", w_in).astype(BF16), conv_a_w[0], conv_f_w[0]], "gather_w_in")
    late = [shard2d(n, wts[n]).astype(BF16) for n in BIG[1:]]
    send_sems, recv_sems, late, lands, token = _gather_start(late, g_in, "gather_late_start")
    w_in_rows = g_in.reshape(-1, D_MODEL)
    w = {name: w_in_rows[lo:hi] for name, lo, hi in IN_SPLITS}
    w["dt"] = jnp.pad(w["dt"], ((0, DT_PAD - SSD_HEADS), (0, 0)))
    w["conv_a"] = _columns_from_devices(g_conv_a)
    w["conv_f"] = _columns_from_devices(g_conv_f)

    def late_weights(*after):
        mine, got = _gather_wait(send_sems, recv_sems, late, lands, after, "gather_late_wait")
        g_branch, g_out, g_up, g_down = [lax.dynamic_update_index_in_dim(land, own, dev, 0).reshape(-1, D_MODEL)
                                         for land, own in zip(got, mine)]
        return {"branch_a": g_branch[:SSD_INNER], "branch_b": g_branch[SSD_INNER:], "out": g_out, "up": g_up,
                "down": g_down}

    in_flight = {}

    def on_grad(n, g):
        part = {"w_in": lambda: jnp.concatenate([g[name][:hi - lo] for name, lo, hi in IN_SPLITS], axis=0),
                "w_branch": lambda: jnp.concatenate([g["branch_a"], g["branch_b"]], axis=0),
                "w_out": lambda: g["out"], "w_up": lambda: g["up"], "w_down": lambda: g["down"]}[n]()
        part = part.reshape(N_DEV, -1, D_MODEL)
        if n not in TWO_LEVEL:
            send, recv, (part,), (land,), tok = _gather_start([part], None, f"to_owners_start_{n}", scatter=True)
            in_flight[n] = (part, send, recv, land)
            return tok
        from_core, = _exchange_cores([part], f"to_other_core_{n}")
        q, own = _chip_sum(part, from_core, place, f"chip_sum_{n}")
        send, recv, q, land, tok = _chips_start(q, f"to_other_chips_start_{n}")
        in_flight[n] = (own, send, recv, q, land)
        return tok

    p = {n: wts[n][0] if wts[n].ndim > 2 else wts[n].reshape(1, -1) for n in SMALL}
    small_flight = []

    def on_small(g, loss):
        arrays = [jnp.concatenate([g[n] for n in ROW_VECTORS] + [loss[:1]], axis=1), g["w_spatial"], g["b_spatial"],
                  g["conv_a"], g["conv_f"]]
        *flight, tok = _gather_start(arrays, g["conv_a"], "gather_small_start")
        small_flight.append(flight)
        return tok

    loss, gx, g = _local_step(x[0], loss_target[0], w, p, after=token, late_weights=late_weights, on_grad=on_grad,
                              on_small=on_small)
    *flight, _ = _gather_start([g["norm1_w"]], gx, "gather_norm1_start")
    small_flight.append(flight)

    grads, delta, new_m, new_v = {}, {}, {}, {}

    def big_adamw(n, after):
        view = (lambda a: a.transpose(2, 0, 1)) if n == "w_in" else (lambda a: shard2d(n, a))
        back = (lambda b: b.transpose(1, 2, 0)) if n == "w_in" else (lambda b: unshard(n, b))
        state = [view(t[n]) for t in (wts, mom, var)]
        if n not in TWO_LEVEL:
            part, send, recv, land = in_flight[n]
            (part,), (got,) = _gather_wait(send, recv, [part], [land], [after], f"to_owners_wait_{n}", scatter=True)
            out = _sum8_adamw(part, got, place, *state, f"adamw_{n}")
        else:
            own, send, recv, q, land = in_flight[n]
            got = _chips_wait(send, recv, q, land, after, f"to_other_chips_wait_{n}")
            out = _sum_adamw(own, got, *state, f"adamw_{n}")
        grads[n], delta[n], new_m[n], new_v[n] = [back(o) for o in out]
        return out[1]

    after = gx
    for n in ("w_down", "w_up", "w_out", "w_branch", "w_in"):
        after = big_adamw(n, after)
    gathered = []
    for (send, recv, mine, land), name in zip(small_flight, ("gather_small_wait", "gather_norm1_wait")):
        mine, got = _gather_wait(send, recv, mine, land, [after], name)
        gathered += [lax.dynamic_update_index_in_dim(full, own, dev, 0) for full, own in zip(got, mine)]
    small = [{n: t[n][0] if t[n].ndim > 2 else t[n].reshape(1, -1) for n in SMALL_ORDER} for t in (wts, mom, var)]
    *outs, loss = _small_adamw(gathered, *small)
    for tgt, out in zip((grads, delta, new_m, new_v), outs):
        tgt.update({n: out[n].reshape(wts[n].shape) for n in SMALL_ORDER})
    loss = loss[0, 0]

    return (loss, gx[None], *[grads[n] for n in WEIGHTS], *[delta[n] for n in WEIGHTS],
            *[new_m[n] for n in WEIGHTS], *[new_v[n] for n in WEIGHTS])
```

```python
import functools

import jax
import jax.numpy as jnp
from jax import lax
from jax.experimental import pallas as pl
from jax.experimental.pallas import tpu as pltpu

F32, BF16 = jnp.float32, jnp.bfloat16
HIGHEST = lax.Precision.HIGHEST

D_MODEL = 1024
SSD_INNER = 2048
SSD_HEAD_DIM = 64
SSD_HEADS = 32
SSD_GROUPS = 4
SSD_STATE = 128
SSD_BC = SSD_GROUPS * SSD_STATE
SSD_XBC = SSD_INNER + 2 * SSD_BC
SSD_CONV = 4
CHUNK = 128
N_PAIRS = SSD_HEADS // 2
PAIRS_PER_GROUP = N_PAIRS // SSD_GROUPS
SGU_WIDTH = 1024
SGU_GROUPS = 8
D_FF = 2816
FFN_CONV = 3
NORM_EPS = 1e-6
LN_EPS = 1e-5
LANES = 128
DT_PAD = LANES

ADAM_LR, ADAM_B1, ADAM_B2, ADAM_EPS, ADAM_WD, ADAM_STEP = 0.001, 0.9, 0.999, 1e-08, 0.01, 10

N_DEV = 8
VMEM_LIMIT = 56 * 1024 * 1024
MESH = pl.DeviceIdType.MESH


def _params(n_grid, **kw):
    sem = dict(dimension_semantics=("arbitrary",) * n_grid) if n_grid else {}
    return pltpu.CompilerParams(vmem_limit_bytes=VMEM_LIMIT, **sem, **kw)


def _tile(n, pref):
    t = (min(pref, n) // LANES) * LANES
    while n % t:
        t -= LANES
    return t


def _row_tile(r, pref):
    for t in range(min(pref, r) // 16 * 16, 0, -16):
        if r % t == 0:
            return t
    return r


def _tile2d(r, c, rows):
    if r % 16 == 0:
        return _row_tile(r, rows), c
    return r, _tile(c, 2 * LANES)


def _rows(tm, n, nt=None, rev=False, col=0):
    if rev:
        return pl.BlockSpec((tm, n), lambda i: (nt - 1 - i, col))
    return pl.BlockSpec((tm, n), lambda i: (i, col))


def _halo(tm, n, nt=None, rev=False, col=0):
    per = tm // 8
    if rev:
        return pl.BlockSpec((8, n), lambda i: (jnp.maximum((nt - 1 - i) * per - 1, 0), col))
    return pl.BlockSpec((8, n), lambda i: (jnp.maximum(i * per - 1, 0), col))


def _into(into, in_index, out_index):
    if into is None:
        return [], [], {}
    return [into], [pl.BlockSpec(memory_space=pl.ANY)], dict(input_output_aliases={in_index: out_index})


def _full(shape):
    nd = len(shape)
    return pl.BlockSpec(shape, lambda *_: (0,) * nd)


def _rms(x, w, eps=NORM_EPS):
    return x * lax.rsqrt(jnp.mean(x * x, axis=-1, keepdims=True) + eps) * w


def _layer_norm(x, w, b):
    mu = jnp.mean(x, axis=-1, keepdims=True)
    var = jnp.mean(jnp.square(x - mu), axis=-1, keepdims=True)
    return (x - mu) * lax.rsqrt(var + LN_EPS) * w + b


def _sigmoid(x):
    return 1.0 / (1.0 + jnp.exp(-x))


def _silu(x):
    return x * _sigmoid(x)


def _dsilu(x):
    s = _sigmoid(x)
    return s * (1.0 + x * (1.0 - s))


def _softplus(x):
    return jnp.maximum(x, 0.0) + jnp.log(1.0 + jnp.exp(-jnp.abs(x)))


def _gelu(x):
    return jax.nn.gelu(x)


def _dot(a, b):
    return jnp.dot(a, b, preferred_element_type=F32)


def _dot_nt(a, b):
    return lax.dot_general(a, b, (((1,), (1,)), ((), ())), preferred_element_type=F32)


def _dot_tn(a, b):
    return lax.dot_general(a, b, (((0,), (0,)), ((), ())), preferred_element_type=F32)


def _dot_split(p, e):
    hi = p.astype(BF16)
    lo = (p - hi.astype(F32)).astype(BF16)
    return _dot(hi, e) + _dot(lo, e)


def _colsum(x):
    return jnp.sum(x, axis=0, keepdims=True)


def _shift_down(x, halo, j):
    xs = pltpu.roll(x, j, 0)
    hs = pltpu.roll(halo, j, 0)
    r8 = lax.broadcasted_iota(jnp.int32, hs.shape, 0)
    return jnp.concatenate([jnp.where(r8 < j, hs, xs[:8]), xs[8:]], axis=0)


def _shift_up(x, nxt, j):
    n = x.shape[0]
    xs = pltpu.roll(x, n - j, 0)
    ns = pltpu.roll(nxt, 8 - j, 0)
    r8 = lax.broadcasted_iota(jnp.int32, ns.shape, 0)
    return jnp.concatenate([xs[:n - 8], jnp.where(r8 >= 8 - j, ns, xs[n - 8:])], axis=0)


def _causal_conv(x, halo, w, b):
    k = w.shape[0]
    y = b + w[k - 1:k, :] * x
    for j in range(1, k):
        y = y + w[k - 1 - j:k - j, :] * _shift_down(x, halo, j)
    return y


def _causal_conv_bwd(dy, nxt, x, w):
    k = w.shape[0]
    dx = w[k - 1:k, :] * dy
    dw = [_colsum(dy * x)]
    for j in range(1, k):
        dyj = _shift_up(dy, nxt, j)
        dx = dx + w[k - 1 - j:k - j, :] * dyj
        dw.append(_colsum(dyj * x))
    return dx, jnp.concatenate(dw[::-1], axis=0)


MM_TILE_PREF = 1408
MM_VMEM_BUDGET = 40 * 1024 * 1024
MM_WHOLE_K = 6144


def _mm_tiles(m, n, k, out_bytes):
    tm, tn = _tile(m, MM_TILE_PREF), _tile(n, MM_TILE_PREF)
    need = lambda tm, tn: 2 * (2 * k * (tm + tn) + out_bytes * tm * tn)
    while need(tm, tn) > MM_VMEM_BUDGET:
        if tn >= tm and tn > LANES:
            tn = _tile(n, tn - LANES)
        else:
            tm = _tile(m, tm - LANES)
    return tm, tn


def _mm(a, b, dims, name, acc=None, out_dtype=F32, after=None):
    if dims == "tn":
        k, m = a.shape
    else:
        m, k = a.shape
    n = b.shape[0] if dims == "nt" else b.shape[1]
    tm, tn = _mm_tiles(m, n, k, 4 * (2 if acc is not None else 1))
    a_spec = pl.BlockSpec((k, tm), lambda j, i: (0, i)) if dims == "tn" else pl.BlockSpec((tm, k), lambda j, i: (i, 0))
    b_spec = pl.BlockSpec((tn, k), lambda j, i: (j, 0)) if dims == "nt" else pl.BlockSpec((k, tn), lambda j, i: (0, j))
    o_spec = pl.BlockSpec((tm, tn), lambda j, i: (i, j))
    dot = {"nn": _dot, "nt": _dot_nt, "tn": _dot_tn}[dims]

    def body(a_ref, b_ref, *rest):
        r = dot(a_ref[...], b_ref[...])
        if acc is not None:
            r = r + rest[0][...]
        rest[-1][...] = r.astype(out_dtype)

    ins, specs = [a, b], [a_spec, b_spec]
    if acc is not None:
        ins.append(acc)
        specs.append(o_spec)
    if after is not None:
        ins.append(after)
        specs.append(pl.BlockSpec(memory_space=pl.ANY))
    return pl.pallas_call(
        body, name=name, grid=(n // tn, m // tm), in_specs=specs, out_specs=o_spec,
        out_shape=jax.ShapeDtypeStruct((m, n), out_dtype), compiler_params=_params(2),
    )(*ins)


def _mm_rows(a, b, dims, name, fn, rows=(), fulls=(), row_outs=(), acc_outs=(), after=None):
    m, k = a.shape
    n = b.shape[0] if dims == "nt" else b.shape[1]
    tk = k if (dims == "nt" or k <= MM_WHOLE_K) else _tile(k, 1024)
    nk = k // tk
    per_row = (2 * tk + 8 * n + sum(4 * r.shape[1] for r in rows)
               + sum(c * jnp.dtype(d).itemsize for c, d in row_outs))
    tm = _tile(m, 1024)
    while 2 * tm * per_row + 4 * tk * n > MM_VMEM_BUDGET:
        tm = _tile(m, tm - LANES)
    dot = _dot_nt if dims == "nt" else _dot
    n_in = 2 + len(rows) + len(fulls) + (after is not None)
    n_out = len(row_outs) + len(acc_outs)

    def body(*refs):
        ins, outs, scratch = refs[:n_in], refs[n_in:n_in + n_out], refs[n_in + n_out:]
        row_refs, acc_refs = outs[:len(row_outs)], outs[len(row_outs):]
        step = pl.program_id(1)

        @pl.when(jnp.logical_and(pl.program_id(0) == 0, step == 0))
        def _():
            for r in acc_refs:
                r[...] = jnp.zeros_like(r)

        part = dot(ins[0][...], ins[1][...])
        if nk > 1:
            part_ref, = scratch

            @pl.when(step == 0)
            def _():
                part_ref[...] = part

            @pl.when(step > 0)
            def _():
                part_ref[...] += part

        @pl.when(step == nk - 1)
        def _():
            result = part_ref[...] if nk > 1 else part
            new_rows, incs = fn(result, *[r[...] for r in ins[2:2 + len(rows) + len(fulls)]])
            for r, val in zip(row_refs, new_rows):
                r[...] = val.astype(r.dtype)
            for r, inc in zip(acc_refs, incs):
                r[...] += inc

    tile_rows = lambda c: pl.BlockSpec((tm, c), lambda i, s: (i, 0))
    b_spec = pl.BlockSpec((tk, n), lambda i, s: (s, 0)) if dims == "nn" else _full(b.shape)
    extra, extra_specs = ([after], [pl.BlockSpec(memory_space=pl.ANY)]) if after is not None else ([], [])
    return pl.pallas_call(
        body, name=name, grid=(m // tm, nk),
        in_specs=[pl.BlockSpec((tm, tk), lambda i, s: (i, s)), b_spec] + [tile_rows(r.shape[1]) for r in rows]
        + [_full(f.shape) for f in fulls] + extra_specs,
        out_specs=[tile_rows(c) for c, _ in row_outs] + [_full(s) for s in acc_outs],
        out_shape=[jax.ShapeDtypeStruct((m, c), d) for c, d in row_outs]
        + [jax.ShapeDtypeStruct(s, F32) for s in acc_outs],
        scratch_shapes=[pltpu.VMEM((tm, n), F32)] if nk > 1 else [],
        compiler_params=_params(2),
    )(a, b, *rows, *fulls, *extra)


def _residual_norm(o, x, w):
    h = x + o
    return (h, _rms(h, w)), ()


def _norm_backward(dn, h, dres, w):
    _, vjp = jax.vjp(_rms, h, w)
    dh, dw = vjp(dn)
    dh = dh + dres
    return (dh, dh), (dw,)


def _loss_and_grad(dn, h1, target, w):
    yf, vjp = jax.vjp(_rms, h1 + dn, w)
    err = yf - target
    loss = 0.5 * jnp.sum(jnp.mean(err * err, axis=-1, keepdims=True))
    dh, dw = vjp(err * (1.0 / err.shape[-1]))
    return (dh, dh), (jnp.full((8, LANES), loss, F32), dw)


def _wgrad(a, d, name, after=None):
    return _mm(a, d, "tn", name, out_dtype=BF16, after=after)


def _norm_fwd(x, w, name, after=None, tm=512):
    t, d = x.shape

    def body(x_ref, w_ref, *rest):
        rest[-1][...] = _rms(x_ref[...], w_ref[...]).astype(BF16)

    extra, extra_specs = ([after], [_full(after.shape)]) if after is not None else ([], [])
    return pl.pallas_call(
        body, name=name, grid=(t // tm,), in_specs=[_rows(tm, d), _full((1, d))] + extra_specs,
        out_specs=_rows(tm, d), out_shape=jax.ShapeDtypeStruct((t, d), BF16), compiler_params=_params(1),
    )(x, w, *extra)


def _conv_a_fwd(xbc, cw, cb, tm=256, col=0):
    t, c = xbc.shape[0], cw.shape[1]

    def body(x_ref, h_ref, w_ref, b_ref, o_ref, y_ref):
        halo = jnp.where(pl.program_id(0) > 0, h_ref[...], 0.0)
        y = _causal_conv(x_ref[...], halo, w_ref[...], b_ref[...])
        y_ref[...] = y.astype(BF16)
        o_ref[...] = _silu(y)

    return pl.pallas_call(
        body, name="conv_a_fwd", grid=(t // tm,),
        in_specs=[_rows(tm, c, col=col), _halo(tm, c, col=col), _full(cw.shape), _full((1, c))],
        out_specs=[_rows(tm, c)] * 2,
        out_shape=[jax.ShapeDtypeStruct((t, c), F32), jax.ShapeDtypeStruct((t, c), BF16)], compiler_params=_params(1),
    )(xbc, xbc, cw, cb)


def _ssd_common(dtr, dtb, alog, e_t):
    row = lax.broadcasted_iota(jnp.int32, (CHUNK, CHUNK), 0)
    col = lax.broadcasted_iota(jnp.int32, (CHUNK, CHUNK), 1)
    causal = row >= col
    dt = _softplus(dtr + dtb)
    a = -jnp.exp(alog)
    acum = jnp.dot(causal.astype(F32), dt * a, precision=HIGHEST, preferred_element_type=F32)
    spread = lambda v: _dot_split(v, e_t)
    return dict(dt=dt, a=a, acum=acum, acum_t=acum.T, causal=causal, row=row, col=col, lane_lo=col < SSD_HEAD_DIM,
                dt_x=spread(dt), ecol_x=spread(jnp.exp(acum)), dsr_x=spread(jnp.exp(acum[CHUNK - 1:CHUNK, :] - acum)))


def _head_decay(c, h, transposed=False):
    d = c["acum"][:, h:h + 1] - c["acum_t"][h:h + 1, :]
    if transposed:
        return jnp.exp(jnp.where(c["row"] <= c["col"], -d, -jnp.inf))
    return jnp.exp(jnp.where(c["causal"], d, -jnp.inf))


def _ssd_fwd(xc, dtr, z, dtb, alog, dsk, nw, e_t, z_col=0):
    t = xc.shape[0]
    nc = t // CHUNK

    def body(xs_ref, b_ref, c_ref, dtr_ref, z_ref, dtb_ref, alog_ref, dsk_ref, nw_ref, et_ref,
             y_ref, ya_ref, sp_ref, s_scr):
        @pl.when(pl.program_id(0) == 0)
        def _():
            s_scr[...] = jnp.zeros_like(s_scr)

        c = _ssd_common(dtr_ref[...], dtb_ref[...], alog_ref[...], et_ref[...])
        lane_lo = c["lane_lo"]
        dsk = dsk_ref[...]
        for g in range(SSD_GROUPS):
            gs = slice(g * SSD_STATE, (g + 1) * SSD_STATE)
            bg_t, cg = b_ref[:, gs].T.astype(BF16), c_ref[:, gs].astype(BF16)
            cb = _dot(cg, bg_t)
            for pp in range(PAIRS_PER_GROUP):
                j = g * PAIRS_PER_GROUP + pp
                ps = slice(j * LANES, (j + 1) * LANES)
                x = xs_ref[:, ps]
                ecol, dsr = c["ecol_x"][:, ps], c["dsr_x"][:, ps]
                xdt = x * c["dt_x"][:, ps]
                xb = xdt.astype(BF16)
                zero = jnp.zeros_like(xb)
                yd = (_dot((cb * _head_decay(c, 2 * j)).astype(BF16), jnp.where(lane_lo, xb, zero))
                      + _dot((cb * _head_decay(c, 2 * j + 1)).astype(BF16), jnp.where(lane_lo, zero, xb)))
                sp = s_scr[j]
                yo = ecol * _dot(cg, sp.astype(BF16))
                st = _dot(bg_t, (xdt * dsr).astype(BF16))
                sp_ref[0, j] = sp
                s_scr[j] = ecol[CHUNK - 1:CHUNK] * sp + st
                dskp = jnp.where(lane_lo[0:1], dsk[:, 2 * j:2 * j + 1], dsk[:, 2 * j + 1:2 * j + 2])
                y_ref[:, ps] = yd + yo + dskp * x
        ya_ref[...] = _rms(y_ref[...] * _silu(z_ref[...]), nw_ref[...]).astype(BF16)

    ck = lambda n, col=0: pl.BlockSpec((CHUNK, n), lambda c: (c, col))
    return pl.pallas_call(
        body, name="ssd_fwd", grid=(nc,),
        in_specs=[ck(SSD_INNER), ck(SSD_BC, SSD_INNER // SSD_BC), ck(SSD_BC, SSD_INNER // SSD_BC + 1), ck(DT_PAD),
                  ck(SSD_INNER, z_col), _full((1, DT_PAD)), _full((1, DT_PAD)), _full((1, DT_PAD)),
                  _full((1, SSD_INNER)), _full(e_t.shape)],
        out_specs=[ck(SSD_INNER), ck(SSD_INNER),
                   pl.BlockSpec((1, N_PAIRS, SSD_STATE, LANES), lambda c: (c, 0, 0, 0))],
        out_shape=[jax.ShapeDtypeStruct((t, SSD_INNER), F32), jax.ShapeDtypeStruct((t, SSD_INNER), BF16),
                   jax.ShapeDtypeStruct((nc, N_PAIRS, SSD_STATE, LANES), F32)],
        scratch_shapes=[pltpu.VMEM((N_PAIRS, SSD_STATE, LANES), F32)], compiler_params=_params(1),
    )(xc, xc, xc, dtr, z, dtb, alog, dsk, nw, e_t)


def _ssd_bwd(dya, y, z, xc, dtr, sprev, dtb, alog, dsk, nw, e_heads, e_t, z_col=0, into=None):
    t = xc.shape[0]
    nc = t // CHUNK
    more, more_specs, alias = _into(into, 14, 0)

    def body(dya_ref, y_ref, z_ref, xs_ref, b_ref, c_ref, dtr_ref, sp_ref, dtb_ref, alog_ref, dsk_ref, nw_ref, e_ref,
             et_ref, *rest):
        dz_ref, dxs_ref, db_ref, dc_ref, ddtr_ref, dnw_ref, ddtb_ref, dalog_ref, ddsk_ref, ds_scr = rest[len(more):]

        @pl.when(pl.program_id(0) == 0)
        def _():
            ds_scr[...] = jnp.zeros_like(ds_scr)
            for r in (dnw_ref, ddtb_ref, dalog_ref, ddsk_ref):
                r[...] = jnp.zeros_like(r)

        y = y_ref[...]
        _, gate_vjp = jax.vjp(lambda y_, z_, w_: _rms(y_ * _silu(z_), w_), y, z_ref[...], nw_ref[...])
        dy, dz, dnw = gate_vjp(dya_ref[...])
        dz_ref[...] = dz.astype(BF16)
        dnw_ref[...] += dnw

        dtr = dtr_ref[...]
        c = _ssd_common(dtr, dtb_ref[...], alog_ref[...], et_ref[...])
        dt, a, lane_lo, row, col = c["dt"], c["a"], c["lane_lo"], c["row"], c["col"]
        dsk = dsk_ref[...]
        p_a, p_dt, v_last = [], [], []
        da_cols = jnp.zeros((CHUNK, CHUNK), F32)
        da_rows = jnp.zeros((CHUNK, CHUNK), F32)
        for g in range(SSD_GROUPS):
            gs = slice(g * SSD_STATE, (g + 1) * SSD_STATE)
            bg, cg = b_ref[:, gs].astype(BF16), c_ref[:, gs].astype(BF16)
            bg_t, cg_t = b_ref[:, gs].T.astype(BF16), c_ref[:, gs].T.astype(BF16)
            cb, cb_t = _dot(cg, bg_t), _dot(bg, cg_t)
            dcb = jnp.zeros((CHUNK, CHUNK), F32)
            dbg = jnp.zeros((CHUNK, SSD_STATE), F32)
            dcg = jnp.zeros((CHUNK, SSD_STATE), F32)
            for pp in range(PAIRS_PER_GROUP):
                j = g * PAIRS_PER_GROUP + pp
                ps = slice(j * LANES, (j + 1) * LANES)
                x = xs_ref[:, ps]
                dtp, ecol, dsr = c["dt_x"][:, ps], c["ecol_x"][:, ps], c["dsr_x"][:, ps]
                elast = ecol[CHUNK - 1:CHUNK]
                xdt = x * dtp
                xb = xdt.astype(BF16)
                dskp = jnp.where(lane_lo[0:1], dsk[:, 2 * j:2 * j + 1], dsk[:, 2 * j + 1:2 * j + 2])
                dyp = dy[:, ps]
                dyb = dyp.astype(BF16)
                sp, dsn = sp_ref[0, j], ds_scr[j]
                spb, dsnb = sp.astype(BF16), dsn.astype(BF16)
                y_off = ecol * _dot(cg, spb)
                dw = (dyp * ecol).astype(BF16)
                dcg = dcg + _dot_nt(dw, spb)
                dsp = _dot(cg_t, dw) + elast * dsn
                xd = xdt * dsr
                zd = _dot(bg, dsnb) * dsr
                dbg = dbg + _dot_nt(xd.astype(BF16), dsnb)
                dxdt = zd
                zero = jnp.zeros_like(xb)
                for h, lm in ((2 * j, lane_lo), (2 * j + 1, jnp.logical_not(lane_lo))):
                    le = _head_decay(c, h)
                    dm = _dot_nt(jnp.where(lm, dyb, zero), jnp.where(lm, xb, zero))
                    dcb = dcb + dm * le
                    m = cb * le
                    m_t = (cb_t * _head_decay(c, h, transposed=True)).astype(BF16)
                    dxdt = dxdt + jnp.where(lm, _dot(m_t, dyb), 0.0)
                    q = dm * m
                    da_cols = da_cols + jnp.where(col == h, jnp.sum(q, axis=1, keepdims=True), 0.0)
                    da_rows = da_rows + jnp.where(row == h, _colsum(q), 0.0)
                ds_scr[j] = dsp
                dxs_ref[:, ps] = dxdt * dtp + dskp * dyp
                p_a.append(dyp * y_off - xdt * zd)
                p_dt.append(dxdt * x)
                v_last.append(_colsum(zd * xdt) + elast * _colsum(dsn * sp))
            dcbb = dcb.astype(BF16)
            db_ref[:, gs] = dbg + _dot_tn(dcbb, cg)
            dc_ref[:, gs] = dcg + _dot(dcbb, bg)
        e = e_ref[...]
        rows8 = jnp.concatenate([jnp.concatenate(v_last, axis=1), _colsum(dy * xs_ref[...]),
                                 jnp.zeros((6, SSD_INNER), F32)], axis=0)
        r8 = _dot_split(rows8, e)
        da = (_dot_split(jnp.concatenate(p_a, axis=1), e) + jnp.where(row == CHUNK - 1, r8[0:1], 0.0)
              + da_cols - da_rows.T)
        ddsk_ref[...] += r8[1:2]
        dadt = jnp.dot((row <= col).astype(F32), da, precision=HIGHEST, preferred_element_type=F32)
        ddt = dadt * a + _dot_split(jnp.concatenate(p_dt, axis=1), e)
        dalog_ref[...] += _colsum(dadt * dt) * a
        ddtr = ddt * _sigmoid(dtr + dtb_ref[...])
        ddtr_ref[...] = ddtr
        ddtb_ref[...] += _colsum(ddtr)

    ck = lambda n, col=0: pl.BlockSpec((CHUNK, n), lambda c: (nc - 1 - c, col))
    acc = lambda n: _full((1, n))
    return pl.pallas_call(
        body, name="ssd_bwd", grid=(nc,),
        in_specs=[ck(SSD_INNER), ck(SSD_INNER), ck(SSD_INNER, z_col), ck(SSD_INNER), ck(SSD_BC, SSD_INNER // SSD_BC),
                  ck(SSD_BC, SSD_INNER // SSD_BC + 1), ck(DT_PAD),
                  pl.BlockSpec((1, N_PAIRS, SSD_STATE, LANES), lambda c: (nc - 1 - c, 0, 0, 0)),
                  acc(DT_PAD), acc(DT_PAD), acc(DT_PAD), acc(SSD_INNER), _full((SSD_INNER, LANES)),
                  _full((LANES, SSD_INNER))] + more_specs,
        out_specs=[ck(SSD_INNER, z_col if into is not None else 0), ck(SSD_INNER), ck(SSD_BC), ck(SSD_BC), ck(DT_PAD),
                   acc(SSD_INNER), acc(DT_PAD), acc(DT_PAD), acc(DT_PAD)],
        out_shape=[jax.ShapeDtypeStruct(into.shape if into is not None else (t, SSD_INNER), BF16),
                   jax.ShapeDtypeStruct((t, SSD_INNER), F32),
                   jax.ShapeDtypeStruct((t, SSD_BC), F32), jax.ShapeDtypeStruct((t, SSD_BC), F32),
                   jax.ShapeDtypeStruct((t, DT_PAD), F32), jax.ShapeDtypeStruct((1, SSD_INNER), F32),
                   jax.ShapeDtypeStruct((1, DT_PAD), F32), jax.ShapeDtypeStruct((1, DT_PAD), F32),
                   jax.ShapeDtypeStruct((1, DT_PAD), F32)],
        scratch_shapes=[pltpu.VMEM((N_PAIRS, SSD_STATE, LANES), F32)], compiler_params=_params(1), **alias,
    )(dya, y, z, xc, xc, xc, dtr, sprev, dtb, alog, dsk, nw, e_heads, e_t, *more)


def _sgu_act(uv, uvb, lnw, lnb):
    a = _gelu(uv + uvb)
    return a[:, :SGU_WIDTH], _layer_norm(a[:, SGU_WIDTH:], lnw, lnb)


def _sgu_weights(ws_ref):
    row = lax.broadcasted_iota(jnp.int32, (CHUNK, CHUNK), 0)
    col = lax.broadcasted_iota(jnp.int32, (CHUNK, CHUNK), 1)
    return [jnp.where(row >= col, ws_ref[g], 0.0).astype(BF16) for g in range(SGU_GROUPS)], row >= col


def _sgu_fwd(uv, uvb, lnw, lnb, ws, bs_t, col=0):
    t = uv.shape[0]

    def body(uv_ref, uvb_ref, lnw_ref, lnb_ref, ws_ref, bs_ref, o_ref):
        u, vn = _sgu_act(uv_ref[...], uvb_ref[...], lnw_ref[...], lnb_ref[...])
        wc, _ = _sgu_weights(ws_ref)
        bs = bs_ref[...]
        for g in range(SGU_GROUPS):
            gs = slice(g * LANES, (g + 1) * LANES)
            mixed = _dot(wc[g], vn[:, gs].astype(BF16)) + bs[:, g:g + 1]
            o_ref[:, gs] = (u[:, gs] * mixed).astype(BF16)

    return pl.pallas_call(
        body, name="sgu_fwd", grid=(t // CHUNK,),
        in_specs=[_rows(CHUNK, 2 * SGU_WIDTH, col=col), _full((1, 2 * SGU_WIDTH)), _full((1, SGU_WIDTH)), _full((1, SGU_WIDTH)),
                  _full(ws.shape), _full(bs_t.shape)],
        out_specs=_rows(CHUNK, SGU_WIDTH), out_shape=jax.ShapeDtypeStruct((t, SGU_WIDTH), BF16),
        compiler_params=_params(1),
    )(uv, uvb, lnw, lnb, ws, bs_t)


def _sgu_bwd(dyb, uv, uvb, lnw, lnb, ws, bs_t, e_groups, col=0, into=None):
    t = uv.shape[0]
    more, more_specs, alias = _into(into, 8, 0)

    def body(dyb_ref, uv_ref, uvb_ref, lnw_ref, lnb_ref, ws_ref, bs_ref, e_ref, *rest):
        duv_ref, duvb_ref, dlnw_ref, dlnb_ref, dws_ref, dbs_ref = rest[len(more):]

        @pl.when(pl.program_id(0) == 0)
        def _():
            for r in (duvb_ref, dlnw_ref, dlnb_ref, dws_ref, dbs_ref):
                r[...] = jnp.zeros_like(r)

        (u, vn), act_vjp = jax.vjp(_sgu_act, uv_ref[...], uvb_ref[...], lnw_ref[...], lnb_ref[...])
        wc, causal = _sgu_weights(ws_ref)
        bs = bs_ref[...]
        dyb = dyb_ref[...]
        du, dvn, dmix = [], [], []
        for g in range(SGU_GROUPS):
            gs = slice(g * LANES, (g + 1) * LANES)
            vb = vn[:, gs].astype(BF16)
            mixed = _dot(wc[g], vb) + bs[:, g:g + 1]
            dm = dyb[:, gs] * u[:, gs]
            dmb = dm.astype(BF16)
            du.append(dyb[:, gs] * mixed)
            dvn.append(_dot_tn(wc[g], dmb))
            dws_ref[g] += jnp.where(causal, _dot_nt(dmb, vb), 0.0)
            dmix.append(dm)
        dbs_ref[...] += _dot_split(jnp.concatenate(dmix, axis=1), e_ref[...])
        duv, duvb, dlnw, dlnb = act_vjp((jnp.concatenate(du, axis=1), jnp.concatenate(dvn, axis=1)))
        duv_ref[...] = duv.astype(BF16)
        duvb_ref[...] += duvb
        dlnw_ref[...] += dlnw
        dlnb_ref[...] += dlnb

    return pl.pallas_call(
        body, name="sgu_bwd", grid=(t // CHUNK,),
        in_specs=[_rows(CHUNK, SGU_WIDTH), _rows(CHUNK, 2 * SGU_WIDTH, col=col), _full((1, 2 * SGU_WIDTH)),
                  _full((1, SGU_WIDTH)), _full((1, SGU_WIDTH)), _full(ws.shape), _full(bs_t.shape),
                  _full(e_groups.shape)] + more_specs,
        out_specs=[_rows(CHUNK, 2 * SGU_WIDTH, col=col if into is not None else 0), _full((1, 2 * SGU_WIDTH)),
                   _full((1, SGU_WIDTH)), _full((1, SGU_WIDTH)), _full(ws.shape), _full(bs_t.shape)],
        out_shape=[jax.ShapeDtypeStruct(into.shape if into is not None else (t, 2 * SGU_WIDTH), BF16),
                   jax.ShapeDtypeStruct((1, 2 * SGU_WIDTH), F32),
                   jax.ShapeDtypeStruct((1, SGU_WIDTH), F32), jax.ShapeDtypeStruct((1, SGU_WIDTH), F32),
                   jax.ShapeDtypeStruct(ws.shape, F32), jax.ShapeDtypeStruct(bs_t.shape, F32)],
        compiler_params=_params(1), **alias,
    )(dyb, uv, uvb, lnw, lnb, ws, bs_t, e_groups, *more)


def _merge(gates, pa, pb, bg):
    s = _sigmoid(gates + bg)
    return s[:, :D_MODEL] * pa + s[:, D_MODEL:] * pb


def _merge_fwd(gates, pa, pb, bg, tm=256, col=0):
    t = gates.shape[0]

    def body(g_ref, pa_ref, pb_ref, bg_ref, o_ref):
        o_ref[...] = _merge(g_ref[...], pa_ref[...], pb_ref[...], bg_ref[...]).astype(BF16)

    return pl.pallas_call(
        body, name="merge_fwd", grid=(t // tm,),
        in_specs=[_rows(tm, 2 * D_MODEL, col=col), _rows(tm, D_MODEL), _rows(tm, D_MODEL), _full((1, 2 * D_MODEL))],
        out_specs=_rows(tm, D_MODEL), out_shape=jax.ShapeDtypeStruct((t, D_MODEL), BF16), compiler_params=_params(1),
    )(gates, pa, pb, bg)


def _merge_bwd(dmix, gates, pa, pb, bg, tm=256, col=0, into=None):
    t = gates.shape[0]
    more, more_specs, alias = _into(into, 5, 0)

    def body(d_ref, g_ref, pa_ref, pb_ref, bg_ref, *rest):
        dg_ref, dpa_ref, dpb_ref, dbg_ref = rest[len(more):]

        @pl.when(pl.program_id(0) == 0)
        def _():
            dbg_ref[...] = jnp.zeros_like(dbg_ref)

        _, vjp = jax.vjp(_merge, g_ref[...], pa_ref[...], pb_ref[...], bg_ref[...])
        dg, dpa, dpb, dbg = vjp(d_ref[...])
        dg_ref[...] = dg.astype(BF16)
        dpa_ref[...] = dpa.astype(BF16)
        dpb_ref[...] = dpb.astype(BF16)
        dbg_ref[...] += dbg

    return pl.pallas_call(
        body, name="merge_bwd", grid=(t // tm,),
        in_specs=[_rows(tm, D_MODEL), _rows(tm, 2 * D_MODEL, col=col), _rows(tm, D_MODEL), _rows(tm, D_MODEL),
                  _full((1, 2 * D_MODEL))] + more_specs,
        out_specs=[_rows(tm, 2 * D_MODEL, col=col if into is not None else 0), _rows(tm, D_MODEL),
                   _rows(tm, D_MODEL), _full((1, 2 * D_MODEL))],
        out_shape=[jax.ShapeDtypeStruct(into.shape if into is not None else (t, 2 * D_MODEL), BF16),
                   jax.ShapeDtypeStruct((t, D_MODEL), BF16), jax.ShapeDtypeStruct((t, D_MODEL), BF16),
                   jax.ShapeDtypeStruct((1, 2 * D_MODEL), F32)],
        compiler_params=_params(1), **alias,
    )(dmix, gates, pa, pb, bg, *more)


def _conv_f_fwd(up, cw, cb, tm=128):
    t, c = up.shape

    def body(x_ref, h_ref, w_ref, b_ref, o_ref, y_ref):
        halo = jnp.where(pl.program_id(0) > 0, h_ref[...], 0.0)
        y = _causal_conv(x_ref[...], halo, w_ref[...], b_ref[...])
        y_ref[...] = y.astype(BF16)
        o_ref[...] = (_silu(y[:, :D_FF]) * y[:, D_FF:]).astype(BF16)

    return pl.pallas_call(
        body, name="conv_f_fwd", grid=(t // tm,),
        in_specs=[_rows(tm, c), _halo(tm, c), _full(cw.shape), _full((1, c))],
        out_specs=[_rows(tm, D_FF), _rows(tm, c)],
        out_shape=[jax.ShapeDtypeStruct((t, D_FF), BF16), jax.ShapeDtypeStruct((t, c), BF16)],
        compiler_params=_params(1),
    )(up, up, cw, cb)


def _conv_f_bwd(dact, y, up, cw, tm=128):
    t, c = up.shape
    nt = t // tm

    def body(d_ref, y_ref, x_ref, w_ref, dx_ref, dw_ref, db_ref, nxt_scr):
        @pl.when(pl.program_id(0) == 0)
        def _():
            nxt_scr[...] = jnp.zeros_like(nxt_scr)
            dw_ref[...] = jnp.zeros_like(dw_ref)
            db_ref[...] = jnp.zeros_like(db_ref)

        a, v = y_ref[:, :D_FF].astype(F32), y_ref[:, D_FF:].astype(F32)
        d = d_ref[...]
        dy = jnp.concatenate([d * v * _dsilu(a), d * _silu(a)], axis=1)
        dx, dw = _causal_conv_bwd(dy, nxt_scr[...], x_ref[...], w_ref[...])
        dx_ref[...] = dx.astype(BF16)
        nxt_scr[...] = dy[:8]
        dw_ref[...] += dw
        db_ref[...] += _colsum(dy)

    return pl.pallas_call(
        body, name="conv_f_bwd", grid=(nt,),
        in_specs=[_rows(tm, D_FF, nt, True), _rows(tm, c, nt, True), _rows(tm, c, nt, True), _full(cw.shape)],
        out_specs=[_rows(tm, c, nt, True), _full(cw.shape), _full((1, c))],
        out_shape=[jax.ShapeDtypeStruct((t, c), BF16), jax.ShapeDtypeStruct(cw.shape, F32),
                   jax.ShapeDtypeStruct((1, c), F32)],
        scratch_shapes=[pltpu.VMEM((8, c), F32)], compiler_params=_params(1),
    )(dact, y, up, cw)


def _conv_a_bwd(dxs, db, dc, y, xbc, cw, tm=256, col=0, into=None):
    t, c = xbc.shape[0], cw.shape[1]
    nt = t // tm
    more, more_specs, alias = _into(into, 6, 0)

    def body(dxs_ref, db_ref, dc_ref, y_ref, x_ref, w_ref, *rest):
        dx_ref, dw_ref, dbias_ref, nxt_scr = rest[len(more):]

        @pl.when(pl.program_id(0) == 0)
        def _():
            nxt_scr[...] = jnp.zeros_like(nxt_scr)
            dw_ref[...] = jnp.zeros_like(dw_ref)
            dbias_ref[...] = jnp.zeros_like(dbias_ref)

        dy = jnp.concatenate([dxs_ref[...], db_ref[...], dc_ref[...]], axis=1) * _dsilu(y_ref[...].astype(F32))
        dx, dw = _causal_conv_bwd(dy, nxt_scr[...], x_ref[...], w_ref[...])
        dx_ref[...] = dx.astype(BF16)
        nxt_scr[...] = dy[:8]
        dw_ref[...] += dw
        dbias_ref[...] += _colsum(dy)

    return pl.pallas_call(
        body, name="conv_a_bwd", grid=(nt,),
        in_specs=[_rows(tm, SSD_INNER, nt, True), _rows(tm, SSD_BC, nt, True), _rows(tm, SSD_BC, nt, True),
                  _rows(tm, c, nt, True), _rows(tm, c, nt, True, col), _full(cw.shape)] + more_specs,
        out_specs=[_rows(tm, c, nt, True, col if into is not None else 0), _full(cw.shape), _full((1, c))],
        out_shape=[jax.ShapeDtypeStruct(into.shape if into is not None else (t, c), BF16),
                   jax.ShapeDtypeStruct(cw.shape, F32), jax.ShapeDtypeStruct((1, c), F32)],
        scratch_shapes=[pltpu.VMEM((8, c), F32)], compiler_params=_params(1), **alias,
    )(dxs, db, dc, y, xbc, cw, *more)


def _pad_lanes(v, n=DT_PAD):
    return jnp.pad(v, ((0, 0), (0, n - v.shape[1])))


def _local_step(x, target, w, p, after=None, late_weights=None, on_grad=None, on_small=None):
    dtb, alog, dsk = _pad_lanes(p["dt_bias"]), _pad_lanes(p["a_log"]), _pad_lanes(p["d_skip"])
    bs_t = _pad_lanes(p["b_spatial"].T)
    e_heads = (jnp.arange(SSD_INNER)[:, None] // SSD_HEAD_DIM == jnp.arange(LANES)[None, :]).astype(BF16)
    e_heads_t = (jnp.arange(LANES)[:, None] == jnp.arange(SSD_INNER)[None, :] // SSD_HEAD_DIM).astype(BF16)
    e_groups = (jnp.arange(SGU_WIDTH)[:, None] // LANES == jnp.arange(LANES)[None, :]).astype(BF16)

    n1 = _norm_fwd(x, p["norm1_w"], "norm1_fwd", after=after)
    z = _mm(n1, w["z"], "nt", "proj_z")
    xbc = _mm(n1, w["xbc"], "nt", "proj_xbc")
    dtr = _mm(n1, w["dt"], "nt", "proj_dt")
    uv = _mm(n1, w["uv"], "nt", "proj_uv")
    gates = _mm(n1, w["gates"], "nt", "proj_gates")
    xc, conv_a_out = _conv_a_fwd(xbc, w["conv_a"], p["conv_a_b"])
    y, ya, sprev = _ssd_fwd(xc, dtr, z, dtb, alog, dsk, p["ssd_norm_w"], e_heads_t)
    yb = _sgu_fwd(uv, p["uv_b"], p["v_ln_w"], p["v_ln_b"], p["w_spatial"], bs_t)
    if late_weights is not None:
        w = {**w, **late_weights(ya, yb)}
    pa = _mm(ya, w["branch_a"], "nn", "branch_a")
    pb = _mm(yb, w["branch_b"], "nn", "branch_b")
    mix = _merge_fwd(gates, pa, pb, p["b_gate"])
    wide = [(D_MODEL, F32), (D_MODEL, BF16)]
    h1, n2 = _mm_rows(mix, w["out"], "nn", "out_proj", _residual_norm, rows=[x], fulls=[p["norm2_w"]], row_outs=wide)
    up = _mm(n2, w["up"], "nt", "up_proj")
    act, conv_f_out = _conv_f_fwd(up, w["conv_f"], p["conv_f_b"])
    dh2, dh2b, loss, g_final = _mm_rows(
        act, w["down"], "nn", "down_proj", _loss_and_grad, rows=[h1, target], fulls=[p["final_norm_w"]],
        row_outs=wide, acc_outs=[(8, LANES), (1, D_MODEL)])

    on_grad = on_grad or (lambda name, grads: None)
    g = {"final_norm_w": g_final}
    g["down"] = _wgrad(act, dh2b, "down_wgrad")
    tok = on_grad("w_down", g)
    dact = _mm(dh2b, w["down"], "nt", "down_dgrad", after=tok)
    dup, g["conv_f"], g["conv_f_b"] = _conv_f_bwd(dact, conv_f_out, up, w["conv_f"])
    g["up"] = _wgrad(dup, n2, "up_wgrad")
    tok = on_grad("w_up", g)
    dh1, dh1b, g["norm2_w"] = _mm_rows(
        dup, w["up"], "nn", "up_dgrad", _norm_backward, rows=[h1, dh2], fulls=[p["norm2_w"]], row_outs=wide,
        acc_outs=[(1, D_MODEL)], after=tok)
    g["out"] = _wgrad(mix, dh1b, "out_wgrad")
    tok = on_grad("w_out", g)
    dmix = _mm(dh1b, w["out"], "nt", "out_dgrad", after=tok)
    dgates, dpa, dpb, g["b_gate"] = _merge_bwd(dmix, gates, pa, pb, p["b_gate"])
    g["branch_a"] = _wgrad(ya, dpa, "branch_a_wgrad")
    g["branch_b"] = _wgrad(yb, dpb, "branch_b_wgrad")
    tok = on_grad("w_branch", g)
    dya = _mm(dpa, w["branch_a"], "nt", "branch_a_dgrad", after=tok)
    dyb = _mm(dpb, w["branch_b"], "nt", "branch_b_dgrad", after=tok)
    duv, g["uv_b"], g["v_ln_w"], g["v_ln_b"], g["w_spatial"], dbs_t = _sgu_bwd(
        dyb, uv, p["uv_b"], p["v_ln_w"], p["v_ln_b"], p["w_spatial"], bs_t, e_groups)
    g["b_spatial"] = dbs_t[:, :SGU_GROUPS].T
    dz, dxs, db, dc, ddtr, g["ssd_norm_w"], ddtb, dalog, ddsk = _ssd_bwd(
        dya, y, z, xc, dtr, sprev, dtb, alog, dsk, p["ssd_norm_w"], e_heads, e_heads_t)
    g["dt_bias"], g["a_log"], g["d_skip"] = ddtb, dalog, ddsk
    dxbc, g["conv_a"], g["conv_a_b"] = _conv_a_bwd(dxs, db, dc, conv_a_out, xbc, w["conv_a"])
    tok = on_small(g, loss) if on_small else None
    ddtrb = ddtr.astype(BF16)
    for name, d in (("z", dz), ("xbc", dxbc), ("dt", ddtrb), ("uv", duv), ("gates", dgates)):
        g[name] = _wgrad(d, n1, name + "_wgrad", after=tok)
    tok = on_grad("w_in", g)
    dn1 = _mm(dz, w["z"], "nn", "z_dgrad", after=tok)
    dn1 = _mm(dxbc, w["xbc"], "nn", "xbc_dgrad", acc=dn1)
    dn1 = _mm(ddtrb, w["dt"], "nn", "dt_dgrad", acc=dn1)
    dn1 = _mm(duv, w["uv"], "nn", "uv_dgrad", acc=dn1)
    gx, g["norm1_w"] = _mm_rows(
        dgates, w["gates"], "nn", "gates_dgrad",
        lambda r, so_far, h, dres, w_: tuple(t[:1] for t in _norm_backward(r + so_far, h, dres, w_)),
        rows=[dn1, x, dh1], fulls=[p["norm1_w"]], row_outs=wide[:1], acc_outs=[(1, D_MODEL)])
    return loss, gx, g


def _place():
    return lax.axis_index("x"), lax.axis_index("y"), lax.axis_index("c")


def _other_chips(x, y):
    return [(1 - x, y), (x, 1 - y), (1 - x, 1 - y)]


def _all_gather(shards, name):
    n = len(shards)

    def body(*refs):
        ins, outs = refs[:n], refs[n:2 * n]
        send_sems, recv_sems, local_sems = refs[2 * n:]
        x, y, c = _place()
        me, sibling = (x, y, c), (x, y, 1 - c)
        chips = _other_chips(x, y)

        def copy(a, k, block, to, src=None):
            slot = outs[a].at[4 * block[0] + 2 * block[1] + block[2]]
            return pltpu.make_async_remote_copy(
                src_ref=slot if src is None else src, dst_ref=slot, send_sem=send_sems.at[7 * a + k],
                recv_sem=recv_sems.at[7 * a + k], device_id=to, device_id_type=MESH)

        started = []
        for a in range(n):
            mine = pltpu.make_async_copy(ins[a], outs[a].at[4 * x + 2 * y + c], local_sems.at[a])
            mine.start()
            started.append(mine)
        sends = []
        for a in range(n):
            sends.append(copy(a, 0, me, sibling, src=ins[a]))
            sends += [copy(a, 1 + j, me, (*chip, c), src=ins[a]) for j, chip in enumerate(chips)]
        for cp in sends:
            cp.start()
        for a in range(n):
            for j, chip in enumerate(chips):
                copy(a, 1 + j, (*chip, c), me).wait_recv()
                fwd = copy(a, 4 + j, (*chip, c), sibling)
                fwd.start()
                sends.append(fwd)
        for a in range(n):
            copy(a, 0, sibling, me).wait_recv()
            for j, chip in enumerate(chips):
                copy(a, 4 + j, (*chip, 1 - c), me).wait_recv()
        for cp in sends:
            cp.wait_send()
        for mine in started:
            mine.wait()

    any_spec = pl.BlockSpec(memory_space=pl.ANY)
    return pl.pallas_call(
        body, name=name, in_specs=[any_spec] * n, out_specs=[any_spec] * n,
        out_shape=[jax.ShapeDtypeStruct((N_DEV, *s.shape), s.dtype) for s in shards],
        scratch_shapes=[pltpu.SemaphoreType.DMA((7 * n,)), pltpu.SemaphoreType.DMA((7 * n,)),
                        pltpu.SemaphoreType.DMA((n,))],
    )(*shards)


HBM_SPEC = pl.BlockSpec(memory_space=pltpu.HBM)
SEM_SPEC = pl.BlockSpec(memory_space=pltpu.SEMAPHORE)
ANY_SPEC = pl.BlockSpec(memory_space=pl.ANY)
DATAFLOW = pltpu.SideEffectType.DATAFLOW_SIDE_EFFECTING
N_PEERS = N_DEV - 1


def _peers(x, y, c):
    out = []
    for r in range(1, N_DEV):
        fx, fy, fc = r >> 2 & 1, r >> 1 & 1, r & 1
        out.append(((1 - x) if fx else x, (1 - y) if fy else y, (1 - c) if fc else c))
    return out


def _gather_copies(srcs, lands, send_sems, recv_sems, sending, scatter=False):
    x, y, c = _place()
    copies = []
    for a, (src, land) in enumerate(zip(srcs, lands)):
        for j, (px, py, pc) in enumerate(_peers(x, y, c)):
            mine, theirs = 4 * x + 2 * y + c, 4 * px + 2 * py + pc
            block = src.at[theirs if sending else 0] if scatter else src
            copies.append(pltpu.make_async_remote_copy(
                src_ref=block, dst_ref=land.at[mine if sending else theirs], send_sem=send_sems.at[N_PEERS * a + j],
                recv_sem=recv_sems.at[N_PEERS * a + j], device_id=(px, py, pc), device_id_type=MESH))
    return copies


def _gather_start(shards, after, name, scatter=False):
    n = len(shards)
    after = [] if after is None else [after]

    def body(*refs):
        srcs, lands = refs[:n], refs[n:2 * n]
        send_sems, recv_sems = refs[2 * n + len(after):2 * n + len(after) + 2]
        token = refs[-1]
        for cp in _gather_copies(srcs, lands, send_sems, recv_sems, sending=True, scatter=scatter):
            cp.start()
        token[...] = jnp.zeros_like(token)

    lands = [lax.empty(s.shape if scatter else (N_DEV, *s.shape), s.dtype) for s in shards]
    hbm = lambda a: pltpu.with_memory_space_constraint(a, pltpu.HBM)
    out = pl.pallas_call(
        body, name=name,
        out_shape=(pltpu.SemaphoreType.DMA((N_PEERS * n,)), pltpu.SemaphoreType.DMA((N_PEERS * n,)),
                   *[pltpu.HBM(a.shape, a.dtype) for a in (*shards, *lands)], jax.ShapeDtypeStruct((8, LANES), F32)),
        in_specs=[HBM_SPEC] * (2 * n) + [ANY_SPEC] * len(after),
        out_specs=(SEM_SPEC, SEM_SPEC, *[HBM_SPEC] * (2 * n), pl.BlockSpec(memory_space=pltpu.VMEM)),
        input_output_aliases={i: 2 + i for i in range(2 * n)},
        compiler_params=pltpu.CompilerParams(has_side_effects=DATAFLOW),
    )(*[hbm(a) for a in (*shards, *lands)], *after)
    return out[0], out[1], out[2:2 + n], out[2 + n:2 + 2 * n], out[-1]


def _gather_wait(send_sems, recv_sems, shards, lands, after, name, scatter=False):
    n = len(shards)
    after = tuple(after)

    def body(*refs):
        srcs, lands_ = refs[:n], refs[n:2 * n]
        send, recv = refs[2 * n:2 * n + 2]
        for cp in _gather_copies(srcs, lands_, send, recv, sending=False, scatter=scatter):
            cp.wait_send()
            cp.wait_recv()

    out = pl.pallas_call(
        body, name=name, out_shape=tuple(pltpu.HBM(a.shape, a.dtype) for a in (*shards, *lands)),
        in_specs=[HBM_SPEC] * (2 * n) + [SEM_SPEC, SEM_SPEC] + [ANY_SPEC] * len(after),
        out_specs=tuple([HBM_SPEC] * (2 * n)), input_output_aliases={i: i for i in range(2 * n)},
        compiler_params=pltpu.CompilerParams(has_side_effects=DATAFLOW),
    )(*shards, *lands, send_sems, recv_sems, *after)
    return out[:n], out[n:]


def _chip_copies(src, land, send_sems, recv_sems):
    x, y, c = _place()
    return [pltpu.make_async_remote_copy(
        src_ref=src.at[2 * cx + cy], dst_ref=land.at[j], send_sem=send_sems.at[j], recv_sem=recv_sems.at[j],
        device_id=(cx, cy, c), device_id_type=MESH) for j, (cx, cy) in enumerate(_other_chips(x, y))]


def _chips_start(q, name):
    def body(q_ref, land_ref, send_sems, recv_sems, q_thru, land_thru, token):
        for cp in _chip_copies(q_ref, land_ref, send_sems, recv_sems):
            cp.start()
        token[...] = jnp.zeros_like(token)

    land = lax.empty((3, *q.shape[1:]), q.dtype)
    return pl.pallas_call(
        body, name=name,
        out_shape=(pltpu.SemaphoreType.DMA((3,)), pltpu.SemaphoreType.DMA((3,)), pltpu.HBM(q.shape, q.dtype),
                   pltpu.HBM(land.shape, land.dtype), jax.ShapeDtypeStruct((8, LANES), F32)),
        in_specs=[HBM_SPEC, HBM_SPEC],
        out_specs=(SEM_SPEC, SEM_SPEC, HBM_SPEC, HBM_SPEC, pl.BlockSpec(memory_space=pltpu.VMEM)),
        input_output_aliases={0: 2, 1: 3}, compiler_params=pltpu.CompilerParams(has_side_effects=DATAFLOW),
    )(pltpu.with_memory_space_constraint(q, pltpu.HBM), pltpu.with_memory_space_constraint(land, pltpu.HBM))


def _chips_wait(send_sems, recv_sems, q, land, after, name):
    def body(q_ref, land_ref, send, recv, after_ref, q_out, land_out):
        for cp in _chip_copies(q_ref, land_ref, send, recv):
            cp.wait_send()
            cp.wait_recv()

    return pl.pallas_call(
        body, name=name, out_shape=(pltpu.HBM(q.shape, q.dtype), pltpu.HBM(land.shape, land.dtype)),
        in_specs=[HBM_SPEC, HBM_SPEC, SEM_SPEC, SEM_SPEC, ANY_SPEC], out_specs=(HBM_SPEC, HBM_SPEC),
        input_output_aliases={0: 0, 1: 1}, compiler_params=pltpu.CompilerParams(has_side_effects=DATAFLOW),
    )(q, land, send_sems, recv_sems, after)[1]


def _exchange_cores(parts, name):
    n = len(parts)

    def body(*refs):
        ins, outs = refs[:n], refs[n:2 * n]
        send_sems, recv_sems = refs[2 * n:]
        x, y, c = _place()
        copies = []
        for a in range(n):
            for k in range(4):
                copies.append(pltpu.make_async_remote_copy(
                    src_ref=ins[a].at[2 * k + (1 - c)], dst_ref=outs[a].at[k], send_sem=send_sems.at[4 * a + k],
                    recv_sem=recv_sems.at[4 * a + k], device_id=(x, y, 1 - c), device_id_type=MESH))
        for cp in copies:
            cp.start()
        for cp in copies:
            cp.wait()

    any_spec = pl.BlockSpec(memory_space=pl.ANY)
    return pl.pallas_call(
        body, name=name, in_specs=[any_spec] * n, out_specs=[any_spec] * n,
        out_shape=[jax.ShapeDtypeStruct((4, *s.shape[1:]), s.dtype) for s in parts],
        scratch_shapes=[pltpu.SemaphoreType.DMA((4 * n,)), pltpu.SemaphoreType.DMA((4 * n,))],
    )(*parts)


def _chip_sum(part, got, place, name, tr=256):
    _, r, c = part.shape
    tr, tc = _tile2d(r, c, tr)

    def body(place_ref, p_ref, g_ref, q_ref, own_ref):
        s = p_ref[0].astype(F32) + g_ref[0].astype(F32)
        q_ref[0] = s.astype(BF16)

        @pl.when(pl.program_id(2) == place_ref[1])
        def _():
            own_ref[...] = s

    grid_spec = pltpu.PrefetchScalarGridSpec(
        num_scalar_prefetch=1, grid=(r // tr, c // tc, 4),
        in_specs=[pl.BlockSpec((1, tr, tc), lambda i, j, k, pr: (2 * k + pr[0], i, j)),
                  pl.BlockSpec((1, tr, tc), lambda i, j, k, pr: (k, i, j))],
        out_specs=[pl.BlockSpec((1, tr, tc), lambda i, j, k, pr: (k, i, j)),
                   pl.BlockSpec((tr, tc), lambda i, j, k, pr: (i, j))])
    return pl.pallas_call(
        body, name=name, grid_spec=grid_spec,
        out_shape=[jax.ShapeDtypeStruct((4, r, c), BF16), jax.ShapeDtypeStruct((r, c), F32)],
        compiler_params=_params(3),
    )(place, part, got)


def _adamw(w, g, m, v):
    m = ADAM_B1 * m + (1.0 - ADAM_B1) * g
    v = ADAM_B2 * v + (1.0 - ADAM_B2) * jnp.square(g)
    m_hat = m / (1.0 - ADAM_B1 ** ADAM_STEP)
    v_hat = v / (1.0 - ADAM_B2 ** ADAM_STEP)
    return -ADAM_LR * (m_hat / (jnp.sqrt(v_hat) + ADAM_EPS) + ADAM_WD * w), m, v


def _sum_adamw(own, got, w, m, v, name, tr=256):
    r, c = own.shape
    if w.ndim == 3:
        tr, tc = r, 4 * LANES
        wblk = pl.BlockSpec((tr, 1, tc), lambda i, j: (i, 0, j))
    else:
        tr, tc = _tile2d(r, c, tr)
        wblk = pl.BlockSpec((tr, tc), lambda i, j: (i, j))

    def body(own_ref, got_ref, w_ref, m_ref, v_ref, g_ref, d_ref, nm_ref, nv_ref):
        g = own_ref[...]
        for j in range(3):
            g = g + got_ref[j].astype(F32)
        two_d = lambda ref: ref[...].reshape(tr, tc)
        delta, nm, nv = _adamw(two_d(w_ref), g, two_d(m_ref), two_d(v_ref))
        for ref, val in ((g_ref, g), (d_ref, delta), (nm_ref, nm), (nv_ref, nv)):
            ref[...] = val.reshape(ref.shape)

    blk = pl.BlockSpec((tr, tc), lambda i, j: (i, j))
    return pl.pallas_call(
        body, name=name, grid=(r // tr, c // tc),
        in_specs=[blk, pl.BlockSpec((3, tr, tc), lambda i, j: (0, i, j)), wblk, wblk, wblk], out_specs=[wblk] * 4,
        out_shape=[jax.ShapeDtypeStruct(w.shape, F32)] * 4, compiler_params=_params(2),
    )(own, got, w, m, v)


def _sum8_adamw(part, got, place, w, m, v, name, tr=256):
    _, r, c = part.shape
    if w.ndim == 3:
        tr, tc = r, 2 * LANES
        blk = pl.BlockSpec((tr, 1, tc), lambda i, j, pr: (i, 0, j))
    else:
        tr, tc = _tile2d(r, c, tr)
        blk = pl.BlockSpec((tr, tc), lambda i, j, pr: (i, j))

    def body(place_ref, own_ref, got_ref, w_ref, m_ref, v_ref, g_ref, d_ref, nm_ref, nv_ref):
        dev = 2 * place_ref[1] + place_ref[0]
        g = jnp.zeros((tr, tc), F32)
        for d in range(N_DEV):
            g = g + jnp.where(dev == d, own_ref[0], got_ref[d]).astype(F32)
        two_d = lambda ref: ref[...].reshape(tr, tc)
        delta, nm, nv = _adamw(two_d(w_ref), g, two_d(m_ref), two_d(v_ref))
        for ref, val in ((g_ref, g), (d_ref, delta), (nm_ref, nm), (nv_ref, nv)):
            ref[...] = val.reshape(ref.shape)

    grid_spec = pltpu.PrefetchScalarGridSpec(
        num_scalar_prefetch=1, grid=(r // tr, c // tc),
        in_specs=[pl.BlockSpec((1, tr, tc), lambda i, j, pr: (2 * pr[1] + pr[0], i, j)),
                  pl.BlockSpec((N_DEV, tr, tc), lambda i, j, pr: (0, i, j)), blk, blk, blk],
        out_specs=[blk] * 4)
    return pl.pallas_call(
        body, name=name, grid_spec=grid_spec, out_shape=[jax.ShapeDtypeStruct(w.shape, F32)] * 4,
        compiler_params=_params(2),
    )(place, part, got, w, m, v)


VECTORS = ["norm1_w", "b_gate", "conv_a_b", "dt_bias", "a_log", "d_skip", "ssd_norm_w", "uv_b", "v_ln_w", "v_ln_b",
           "norm2_w", "conv_f_b", "final_norm_w"]
SMALL_ORDER = VECTORS + ["w_spatial", "b_spatial", "conv_a_w", "conv_f_w"]


ROW_VECTORS = VECTORS[1:]


def _small_adamw(gathered, w, m, v):
    sizes = {n: w[n].shape[1] for n in ROW_VECTORS}
    offs, off = {}, 0
    for n in ROW_VECTORS:
        offs[n] = off
        off += -(-sizes[n] // LANES) * LANES
    loss_off = off
    k = len(SMALL_ORDER)
    n_g = len(gathered)

    def body(*refs):
        row_ref, ws_ref, bs_ref, ca_ref, cf_ref, n1_ref = refs[:n_g]
        w_refs, m_refs, v_refs = (dict(zip(SMALL_ORDER, refs[n_g + i * k:n_g + (i + 1) * k])) for i in range(3))
        outs = refs[n_g + 3 * k:]
        x, y, c = _place()
        dev = 4 * x + 2 * y + c

        def total(ref):
            s = ref[0]
            for d in range(1, N_DEV):
                s = s + ref[d]
            return s

        row = total(row_ref)
        grads = {n: row[:, offs[n]:offs[n] + sizes[n]] for n in ROW_VECTORS}
        grads["norm1_w"], grads["w_spatial"], grads["b_spatial"] = total(n1_ref), total(ws_ref), total(bs_ref)
        for n, ref in (("conv_a_w", ca_ref), ("conv_f_w", cf_ref)):
            whole, cols = total(ref), w_refs[n].shape[1]
            mine = whole[:, :cols]
            for d in range(1, N_DEV):
                mine = jnp.where(dev == d, whole[:, d * cols:(d + 1) * cols], mine)
            grads[n] = mine
        for i, n in enumerate(SMALL_ORDER):
            outs[4 * i][...] = grads[n]
            outs[4 * i + 1][...], outs[4 * i + 2][...], outs[4 * i + 3][...] = _adamw(
                w_refs[n][...], grads[n], m_refs[n][...], v_refs[n][...])
        outs[4 * k][...] = row[:, loss_off:loss_off + LANES]

    out = pl.pallas_call(
        body, name="adamw_small",
        out_shape=[jax.ShapeDtypeStruct(w[n].shape, F32) for n in SMALL_ORDER for _ in range(4)]
        + [jax.ShapeDtypeStruct((1, LANES), F32)],
        compiler_params=_params(0),
    )(*gathered, *[t[n] for t in (w, m, v) for n in SMALL_ORDER])
    return [dict(zip(SMALL_ORDER, out[j:4 * k:4])) for j in range(4)] + [out[4 * k]]


SMALL = ["norm1_w", "b_gate", "conv_a_b", "dt_bias", "a_log", "d_skip", "ssd_norm_w", "uv_b", "v_ln_w", "v_ln_b",
         "w_spatial", "b_spatial", "norm2_w", "conv_f_b", "final_norm_w"]
BIG = ["w_in", "w_branch", "w_out", "w_up", "w_down"]
TRANSPOSED = ("w_in", "w_up")
TWO_LEVEL = ()
WEIGHTS = ["norm1_w", "w_in", "b_gate", "conv_a_w", "conv_a_b", "dt_bias", "a_log", "d_skip", "ssd_norm_w", "uv_b",
           "v_ln_w", "v_ln_b", "w_spatial", "b_spatial", "w_branch", "w_out", "norm2_w", "w_up", "conv_f_w",
           "conv_f_b", "w_down", "final_norm_w"]
IN_SPLITS = [("z", 0, 2048), ("xbc", 2048, 5120), ("dt", 5120, 5152), ("uv", 5152, 7200), ("gates", 7200, 9248)]


def _columns_from_devices(a):
    return a.transpose(1, 0, 2).reshape(a.shape[1], -1)


def kernel(x, norm1_w, w_in, b_gate, conv_a_w, conv_a_b, dt_bias, a_log, d_skip, ssd_norm_w, uv_b, v_ln_w, v_ln_b, w_spatial, b_spatial, w_branch, w_out, norm2_w, w_up, conv_f_w, conv_f_b, w_down, final_norm_w, loss_target, m_norm1_w, m_w_in, m_b_gate, m_conv_a_w, m_conv_a_b, m_dt_bias, m_a_log, m_d_skip, m_ssd_norm_w, m_uv_b, m_v_ln_w, m_v_ln_b, m_w_spatial, m_b_spatial, m_w_branch, m_w_out, m_norm2_w, m_w_up, m_conv_f_w, m_conv_f_b, m_w_down, m_final_norm_w, v_norm1_w, v_w_in, v_b_gate, v_conv_a_w, v_conv_a_b, v_dt_bias, v_a_log, v_d_skip, v_ssd_norm_w, v_uv_b, v_v_ln_w, v_v_ln_b, v_w_spatial, v_b_spatial, v_w_branch, v_w_out, v_norm2_w, v_w_up, v_conv_f_w, v_conv_f_b, v_w_down, v_final_norm_w):
    args = dict(locals())
    wts = {n: args[n] for n in WEIGHTS}
    mom = {n: args["m_" + n] for n in WEIGHTS}
    var = {n: args["v_" + n] for n in WEIGHTS}
    cx, cy, cc = _place()
    dev = 4 * cx + 2 * cy + cc
    place = jnp.stack([cc, 2 * cx + cy]).astype(jnp.int32)

    def shard2d(n, a):
        return a[0].T if n in TRANSPOSED else a[0]

    def unshard(n, b):
        return (b.T if n in TRANSPOSED else b)[None]

    g_in, g_conv_a, g_conv_f = _all_gather(
        [shard2d("w_in", w_in).astype(BF16), conv_a_w[0], conv_f_w[0]], "gather_w_in")
    late = [shard2d(n, wts[n]).astype(BF16) for n in BIG[1:]]
    send_sems, recv_sems, late, lands, token = _gather_start(late, g_in, "gather_late_start")
    w_in_rows = g_in.reshape(-1, D_MODEL)
    w = {name: w_in_rows[lo:hi] for name, lo, hi in IN_SPLITS}
    w["dt"] = jnp.pad(w["dt"], ((0, DT_PAD - SSD_HEADS), (0, 0)))
    w["conv_a"] = _columns_from_devices(g_conv_a)
    w["conv_f"] = _columns_from_devices(g_conv_f)

    def late_weights(*after):
        mine, got = _gather_wait(send_sems, recv_sems, late, lands, after, "gather_late_wait")
        g_branch, g_out, g_up, g_down = [lax.dynamic_update_index_in_dim(land, own, dev, 0).reshape(-1, D_MODEL)
                                         for land, own in zip(got, mine)]
        return {"branch_a": g_branch[:SSD_INNER], "branch_b": g_branch[SSD_INNER:], "out": g_out, "up": g_up,
                "down": g_down}

    in_flight = {}

    def on_grad(n, g):
        part = {"w_in": lambda: jnp.concatenate([g[name][:hi - lo] for name, lo, hi in IN_SPLITS], axis=0),
                "w_branch": lambda: jnp.concatenate([g["branch_a"], g["branch_b"]], axis=0),
                "w_out": lambda: g["out"], "w_up": lambda: g["up"], "w_down": lambda: g["down"]}[n]()
        part = part.reshape(N_DEV, -1, D_MODEL)
        if n not in TWO_LEVEL:
            send, recv, (part,), (land,), tok = _gather_start([part], None, f"to_owners_start_{n}", scatter=True)
            in_flight[n] = (part, send, recv, land)
            return tok
        from_core, = _exchange_cores([part], f"to_other_core_{n}")
        q, own = _chip_sum(part, from_core, place, f"chip_sum_{n}")
        send, recv, q, land, tok = _chips_start(q, f"to_other_chips_start_{n}")
        in_flight[n] = (own, send, recv, q, land)
        return tok

    p = {n: wts[n][0] if wts[n].ndim > 2 else wts[n].reshape(1, -1) for n in SMALL}
    small_flight = []

    def on_small(g, loss):
        arrays = [jnp.concatenate([g[n] for n in ROW_VECTORS] + [loss[:1]], axis=1), g["w_spatial"], g["b_spatial"],
                  g["conv_a"], g["conv_f"]]
        *flight, tok = _gather_start(arrays, g["conv_a"], "gather_small_start")
        small_flight.append(flight)
        return tok

    loss, gx, g = _local_step(x[0], loss_target[0], w, p, after=token, late_weights=late_weights, on_grad=on_grad,
                              on_small=on_small)
    *flight, _ = _gather_start([g["norm1_w"]], gx, "gather_norm1_start")
    small_flight.append(flight)

    grads, delta, new_m, new_v = {}, {}, {}, {}

    def big_adamw(n, after):
        view = (lambda a: a.transpose(2, 0, 1)) if n == "w_in" else (lambda a: shard2d(n, a))
        back = (lambda b: b.transpose(1, 2, 0)) if n == "w_in" else (lambda b: unshard(n, b))
        state = [view(t[n]) for t in (wts, mom, var)]
        if n not in TWO_LEVEL:
            part, send, recv, land = in_flight[n]
            (part,), (got,) = _gather_wait(send, recv, [part], [land], [after], f"to_owners_wait_{n}", scatter=True)
            out = _sum8_adamw(part, got, place, *state, f"adamw_{n}")
        else:
            own, send, recv, q, land = in_flight[n]
            got = _chips_wait(send, recv, q, land, after, f"to_other_chips_wait_{n}")
            out = _sum_adamw(own, got, *state, f"adamw_{n}")
        grads[n], delta[n], new_m[n], new_v[n] = [back(o) for o in out]
        return out[1]

    after = gx
    for n in ("w_down", "w_up", "w_out", "w_branch"):
        after = big_adamw(n, after)
    gathered = []
    for (send, recv, mine, land), name in zip(small_flight, ("gather_small_wait", "gather_norm1_wait")):
        mine, got = _gather_wait(send, recv, mine, land, [after], name)
        gathered += [lax.dynamic_update_index_in_dim(full, own, dev, 0) for full, own in zip(got, mine)]
    small = [{n: t[n][0] if t[n].ndim > 2 else t[n].reshape(1, -1) for n in SMALL_ORDER} for t in (wts, mom, var)]
    *outs, loss = _small_adamw(gathered, *small)
    for tgt, out in zip((grads, delta, new_m, new_v), outs):
        tgt.update({n: out[n].reshape(wts[n].shape) for n in SMALL_ORDER})
    big_adamw("w_in", loss)
    loss = loss[0, 0]

    return (loss, gx[None], *[grads[n] for n in WEIGHTS], *[delta[n] for n in WEIGHTS],
            *[new_m[n] for n in WEIGHTS], *[new_v[n] for n in WEIGHTS])
```

```python
import functools

import jax
import jax.numpy as jnp
from jax import lax
from jax.experimental import pallas as pl
from jax.experimental.pallas import tpu as pltpu

F32, BF16 = jnp.float32, jnp.bfloat16
HIGHEST = lax.Precision.HIGHEST

D_MODEL = 1024
SSD_INNER = 2048
SSD_HEAD_DIM = 64
SSD_HEADS = 32
SSD_GROUPS = 4
SSD_STATE = 128
SSD_BC = SSD_GROUPS * SSD_STATE
SSD_XBC = SSD_INNER + 2 * SSD_BC
SSD_CONV = 4
CHUNK = 128
N_PAIRS = SSD_HEADS // 2
PAIRS_PER_GROUP = N_PAIRS // SSD_GROUPS
SGU_WIDTH = 1024
SGU_GROUPS = 8
D_FF = 2816
FFN_CONV = 3
NORM_EPS = 1e-6
LN_EPS = 1e-5
LANES = 128
DT_PAD = LANES

ADAM_LR, ADAM_B1, ADAM_B2, ADAM_EPS, ADAM_WD, ADAM_STEP = 0.001, 0.9, 0.999, 1e-08, 0.01, 10

N_DEV = 8
VMEM_LIMIT = 56 * 1024 * 1024
MESH = pl.DeviceIdType.MESH


def _params(n_grid, **kw):
    sem = dict(dimension_semantics=("arbitrary",) * n_grid) if n_grid else {}
    return pltpu.CompilerParams(vmem_limit_bytes=VMEM_LIMIT, **sem, **kw)


def _tile(n, pref):
    t = (min(pref, n) // LANES) * LANES
    while n % t:
        t -= LANES
    return t


def _row_tile(r, pref):
    for t in range(min(pref, r) // 16 * 16, 0, -16):
        if r % t == 0:
            return t
    return r


def _tile2d(r, c, rows):
    if r % 16 == 0:
        return _row_tile(r, rows), c
    return r, _tile(c, 2 * LANES)


def _rows(tm, n, nt=None, rev=False):
    if rev:
        return pl.BlockSpec((tm, n), lambda i: (nt - 1 - i, 0))
    return pl.BlockSpec((tm, n), lambda i: (i, 0))


def _halo(tm, n):
    per = tm // 8
    return pl.BlockSpec((8, n), lambda i: (jnp.maximum(i * per - 1, 0), 0))


def _full(shape):
    nd = len(shape)
    return pl.BlockSpec(shape, lambda *_: (0,) * nd)


def _rms(x, w, eps=NORM_EPS):
    return x * lax.rsqrt(jnp.mean(x * x, axis=-1, keepdims=True) + eps) * w


def _layer_norm(x, w, b):
    mu = jnp.mean(x, axis=-1, keepdims=True)
    var = jnp.mean(jnp.square(x - mu), axis=-1, keepdims=True)
    return (x - mu) * lax.rsqrt(var + LN_EPS) * w + b


def _sigmoid(x):
    return 1.0 / (1.0 + jnp.exp(-x))


def _silu(x):
    return x * _sigmoid(x)


def _dsilu(x):
    s = _sigmoid(x)
    return s * (1.0 + x * (1.0 - s))


def _softplus(x):
    return jnp.maximum(x, 0.0) + jnp.log(1.0 + jnp.exp(-jnp.abs(x)))


def _gelu(x):
    return jax.nn.gelu(x)


def _dot(a, b):
    return jnp.dot(a, b, preferred_element_type=F32)


def _dot_nt(a, b):
    return lax.dot_general(a, b, (((1,), (1,)), ((), ())), preferred_element_type=F32)


def _dot_tn(a, b):
    return lax.dot_general(a, b, (((0,), (0,)), ((), ())), preferred_element_type=F32)


def _dot_split(p, e):
    hi = p.astype(BF16)
    lo = (p - hi.astype(F32)).astype(BF16)
    return _dot(hi, e) + _dot(lo, e)


def _colsum(x):
    return jnp.sum(x, axis=0, keepdims=True)


def _shift_down(x, halo, j):
    xs = pltpu.roll(x, j, 0)
    hs = pltpu.roll(halo, j, 0)
    r8 = lax.broadcasted_iota(jnp.int32, hs.shape, 0)
    return jnp.concatenate([jnp.where(r8 < j, hs, xs[:8]), xs[8:]], axis=0)


def _shift_up(x, nxt, j):
    n = x.shape[0]
    xs = pltpu.roll(x, n - j, 0)
    ns = pltpu.roll(nxt, 8 - j, 0)
    r8 = lax.broadcasted_iota(jnp.int32, ns.shape, 0)
    return jnp.concatenate([xs[:n - 8], jnp.where(r8 >= 8 - j, ns, xs[n - 8:])], axis=0)


def _causal_conv(x, halo, w, b):
    k = w.shape[0]
    y = b + w[k - 1:k, :] * x
    for j in range(1, k):
        y = y + w[k - 1 - j:k - j, :] * _shift_down(x, halo, j)
    return y


def _causal_conv_bwd(dy, nxt, x, w):
    k = w.shape[0]
    dx = w[k - 1:k, :] * dy
    dw = [_colsum(dy * x)]
    for j in range(1, k):
        dyj = _shift_up(dy, nxt, j)
        dx = dx + w[k - 1 - j:k - j, :] * dyj
        dw.append(_colsum(dyj * x))
    return dx, jnp.concatenate(dw[::-1], axis=0)


MM_TILE_PREF = 1408
MM_VMEM_BUDGET = 40 * 1024 * 1024


def _mm_tiles(m, n, k, out_bytes):
    tm, tn = _tile(m, MM_TILE_PREF), _tile(n, MM_TILE_PREF)
    need = lambda tm, tn: 2 * (2 * k * (tm + tn) + out_bytes * tm * tn)
    while need(tm, tn) > MM_VMEM_BUDGET:
        if tn >= tm and tn > LANES:
            tn = _tile(n, tn - LANES)
        else:
            tm = _tile(m, tm - LANES)
    return tm, tn


def _mm(a, b, dims, name, acc=None, out_dtype=F32, after=None):
    if dims == "tn":
        k, m = a.shape
    else:
        m, k = a.shape
    n = b.shape[0] if dims == "nt" else b.shape[1]
    tm, tn = _mm_tiles(m, n, k, 4 * (2 if acc is not None else 1))
    a_spec = pl.BlockSpec((k, tm), lambda j, i: (0, i)) if dims == "tn" else pl.BlockSpec((tm, k), lambda j, i: (i, 0))
    b_spec = pl.BlockSpec((tn, k), lambda j, i: (j, 0)) if dims == "nt" else pl.BlockSpec((k, tn), lambda j, i: (0, j))
    o_spec = pl.BlockSpec((tm, tn), lambda j, i: (i, j))
    dot = {"nn": _dot, "nt": _dot_nt, "tn": _dot_tn}[dims]

    def body(a_ref, b_ref, *rest):
        r = dot(a_ref[...], b_ref[...])
        if acc is not None:
            r = r + rest[0][...]
        rest[-1][...] = r.astype(out_dtype)

    ins, specs = [a, b], [a_spec, b_spec]
    if acc is not None:
        ins.append(acc)
        specs.append(o_spec)
    if after is not None:
        ins.append(after)
        specs.append(pl.BlockSpec(memory_space=pl.ANY))
    return pl.pallas_call(
        body, name=name, grid=(n // tn, m // tm), in_specs=specs, out_specs=o_spec,
        out_shape=jax.ShapeDtypeStruct((m, n), out_dtype), compiler_params=_params(2),
    )(*ins)


def _mm_rows(a, b, dims, name, fn, rows=(), fulls=(), row_outs=(), acc_outs=(), after=None):
    m, k = a.shape
    n = b.shape[0] if dims == "nt" else b.shape[1]
    per_row = 2 * k + 8 * n + sum(4 * r.shape[1] for r in rows) + sum(c * jnp.dtype(d).itemsize for c, d in row_outs)
    tm = _tile(m, 1024)
    while 2 * tm * per_row + 4 * k * n > MM_VMEM_BUDGET:
        tm = _tile(m, tm - LANES)
    dot = _dot_nt if dims == "nt" else _dot
    n_in = 2 + len(rows) + len(fulls) + (after is not None)

    def body(*refs):
        ins, outs = refs[:n_in], refs[n_in:]
        row_refs, acc_refs = outs[:len(row_outs)], outs[len(row_outs):]

        @pl.when(pl.program_id(0) == 0)
        def _():
            for r in acc_refs:
                r[...] = jnp.zeros_like(r)

        new_rows, incs = fn(dot(ins[0][...], ins[1][...]), *[r[...] for r in ins[2:2 + len(rows) + len(fulls)]])
        for r, val in zip(row_refs, new_rows):
            r[...] = val.astype(r.dtype)
        for r, inc in zip(acc_refs, incs):
            r[...] += inc

    extra, extra_specs = ([after], [pl.BlockSpec(memory_space=pl.ANY)]) if after is not None else ([], [])
    return pl.pallas_call(
        body, name=name, grid=(m // tm,),
        in_specs=[_rows(tm, k), _full(b.shape)] + [_rows(tm, r.shape[1]) for r in rows]
        + [_full(f.shape) for f in fulls] + extra_specs,
        out_specs=[_rows(tm, c) for c, _ in row_outs] + [_full(s) for s in acc_outs],
        out_shape=[jax.ShapeDtypeStruct((m, c), d) for c, d in row_outs]
        + [jax.ShapeDtypeStruct(s, F32) for s in acc_outs],
        compiler_params=_params(1),
    )(a, b, *rows, *fulls, *extra)


def _residual_norm(o, x, w):
    h = x + o
    return (h, _rms(h, w)), ()


def _norm_backward(dn, h, dres, w):
    _, vjp = jax.vjp(_rms, h, w)
    dh, dw = vjp(dn)
    dh = dh + dres
    return (dh, dh), (dw,)


def _loss_and_grad(dn, h1, target, w):
    yf, vjp = jax.vjp(_rms, h1 + dn, w)
    err = yf - target
    loss = 0.5 * jnp.sum(jnp.mean(err * err, axis=-1, keepdims=True))
    dh, dw = vjp(err * (1.0 / err.shape[-1]))
    return (dh, dh), (jnp.full((8, LANES), loss, F32), dw)


def _wgrad(a, d, name, after=None):
    return _mm(a, d, "tn", name, out_dtype=BF16, after=after)


def _norm_fwd(x, w, name, after=None, tm=512):
    t, d = x.shape

    def body(x_ref, w_ref, *rest):
        rest[-1][...] = _rms(x_ref[...], w_ref[...]).astype(BF16)

    extra, extra_specs = ([after], [_full(after.shape)]) if after is not None else ([], [])
    return pl.pallas_call(
        body, name=name, grid=(t // tm,), in_specs=[_rows(tm, d), _full((1, d))] + extra_specs,
        out_specs=_rows(tm, d), out_shape=jax.ShapeDtypeStruct((t, d), BF16), compiler_params=_params(1),
    )(x, w, *extra)


def _conv_a_fwd(xbc, cw, cb, tm=256):
    t, c = xbc.shape

    def body(x_ref, h_ref, w_ref, b_ref, o_ref, y_ref):
        halo = jnp.where(pl.program_id(0) > 0, h_ref[...], 0.0)
        y = _causal_conv(x_ref[...], halo, w_ref[...], b_ref[...])
        y_ref[...] = y.astype(BF16)
        o_ref[...] = _silu(y)

    return pl.pallas_call(
        body, name="conv_a_fwd", grid=(t // tm,),
        in_specs=[_rows(tm, c), _halo(tm, c), _full(cw.shape), _full((1, c))],
        out_specs=[_rows(tm, c)] * 2,
        out_shape=[jax.ShapeDtypeStruct((t, c), F32), jax.ShapeDtypeStruct((t, c), BF16)], compiler_params=_params(1),
    )(xbc, xbc, cw, cb)


def _ssd_common(dtr, dtb, alog, e_t):
    row = lax.broadcasted_iota(jnp.int32, (CHUNK, CHUNK), 0)
    col = lax.broadcasted_iota(jnp.int32, (CHUNK, CHUNK), 1)
    causal = row >= col
    dt = _softplus(dtr + dtb)
    a = -jnp.exp(alog)
    acum = jnp.dot(causal.astype(F32), dt * a, precision=HIGHEST, preferred_element_type=F32)
    spread = lambda v: _dot(v.astype(BF16), e_t)
    elast = jnp.broadcast_to(jnp.exp(acum[CHUNK - 1:CHUNK, :]), (8, LANES))
    return dict(dt=dt, a=a, acum=acum, acum_t=acum.T, causal=causal, row=row, col=col, lane_lo=col < SSD_HEAD_DIM,
                dt_x=_dot_split(dt, e_t), ecol_x=spread(jnp.exp(acum)), elast_x=_dot_split(elast, e_t)[0:1],
                dsr_x=spread(jnp.exp(acum[CHUNK - 1:CHUNK, :] - acum)))


def _head_decay(c, h, transposed=False):
    d = c["acum"][:, h:h + 1] - c["acum_t"][h:h + 1, :]
    if transposed:
        return jnp.exp(jnp.where(c["row"] <= c["col"], -d, -jnp.inf))
    return jnp.exp(jnp.where(c["causal"], d, -jnp.inf))


def _ssd_fwd(xc, dtr, z, dtb, alog, dsk, nw, e_t):
    t = xc.shape[0]
    nc = t // CHUNK

    def body(xs_ref, b_ref, c_ref, dtr_ref, z_ref, dtb_ref, alog_ref, dsk_ref, nw_ref, et_ref,
             y_ref, ya_ref, sp_ref, s_scr):
        @pl.when(pl.program_id(0) == 0)
        def _():
            s_scr[...] = jnp.zeros_like(s_scr)

        c = _ssd_common(dtr_ref[...], dtb_ref[...], alog_ref[...], et_ref[...])
        lane_lo = c["lane_lo"]
        dsk = dsk_ref[...]
        for g in range(SSD_GROUPS):
            gs = slice(g * SSD_STATE, (g + 1) * SSD_STATE)
            bg_t, cg = b_ref[:, gs].T.astype(BF16), c_ref[:, gs].astype(BF16)
            cb = _dot(cg, bg_t)
            for pp in range(PAIRS_PER_GROUP):
                j = g * PAIRS_PER_GROUP + pp
                ps = slice(j * LANES, (j + 1) * LANES)
                x = xs_ref[:, ps]
                ecol, dsr = c["ecol_x"][:, ps], c["dsr_x"][:, ps]
                xdt = x * c["dt_x"][:, ps]
                xb = xdt.astype(BF16)
                zero = jnp.zeros_like(xb)
                yd = (_dot((cb * _head_decay(c, 2 * j)).astype(BF16), jnp.where(lane_lo, xb, zero))
                      + _dot((cb * _head_decay(c, 2 * j + 1)).astype(BF16), jnp.where(lane_lo, zero, xb)))
                sp = s_scr[j]
                yo = ecol * _dot(cg, sp.astype(BF16))
                st = _dot(bg_t, (xdt * dsr).astype(BF16))
                sp_ref[0, j] = sp
                s_scr[j] = c["elast_x"][:, ps] * sp + st
                dskp = jnp.where(lane_lo[0:1], dsk[:, 2 * j:2 * j + 1], dsk[:, 2 * j + 1:2 * j + 2])
                y_ref[:, ps] = yd + yo + dskp * x
        ya_ref[...] = _rms(y_ref[...] * _silu(z_ref[...]), nw_ref[...]).astype(BF16)

    ck = lambda n, col=0: pl.BlockSpec((CHUNK, n), lambda c: (c, col))
    return pl.pallas_call(
        body, name="ssd_fwd", grid=(nc,),
        in_specs=[ck(SSD_INNER), ck(SSD_BC, SSD_INNER // SSD_BC), ck(SSD_BC, SSD_INNER // SSD_BC + 1), ck(DT_PAD),
                  ck(SSD_INNER), _full((1, DT_PAD)), _full((1, DT_PAD)), _full((1, DT_PAD)),
                  _full((1, SSD_INNER)), _full(e_t.shape)],
        out_specs=[ck(SSD_INNER), ck(SSD_INNER),
                   pl.BlockSpec((1, N_PAIRS, SSD_STATE, LANES), lambda c: (c, 0, 0, 0))],
        out_shape=[jax.ShapeDtypeStruct((t, SSD_INNER), F32), jax.ShapeDtypeStruct((t, SSD_INNER), BF16),
                   jax.ShapeDtypeStruct((nc, N_PAIRS, SSD_STATE, LANES), F32)],
        scratch_shapes=[pltpu.VMEM((N_PAIRS, SSD_STATE, LANES), F32)], compiler_params=_params(1),
    )(xc, xc, xc, dtr, z, dtb, alog, dsk, nw, e_t)


def _ssd_bwd(dya, y, z, xc, dtr, sprev, dtb, alog, dsk, nw, e_heads, e_t):
    t = xc.shape[0]
    nc = t // CHUNK

    def body(dya_ref, y_ref, z_ref, xs_ref, b_ref, c_ref, dtr_ref, sp_ref, dtb_ref, alog_ref, dsk_ref, nw_ref, e_ref,
             et_ref, dz_ref, dxs_ref, db_ref, dc_ref, ddtr_ref, dnw_ref, ddtb_ref, dalog_ref, ddsk_ref, ds_scr):
        @pl.when(pl.program_id(0) == 0)
        def _():
            ds_scr[...] = jnp.zeros_like(ds_scr)
            for r in (dnw_ref, ddtb_ref, dalog_ref, ddsk_ref):
                r[...] = jnp.zeros_like(r)

        y = y_ref[...]
        _, gate_vjp = jax.vjp(lambda y_, z_, w_: _rms(y_ * _silu(z_), w_), y, z_ref[...], nw_ref[...])
        dy, dz, dnw = gate_vjp(dya_ref[...])
        dz_ref[...] = dz.astype(BF16)
        dnw_ref[...] += dnw

        dtr = dtr_ref[...]
        c = _ssd_common(dtr, dtb_ref[...], alog_ref[...], et_ref[...])
        dt, a, lane_lo, row, col = c["dt"], c["a"], c["lane_lo"], c["row"], c["col"]
        dsk = dsk_ref[...]
        p_a, p_dt, v_last = [], [], []
        da_cols = jnp.zeros((CHUNK, CHUNK), F32)
        da_rows = jnp.zeros((CHUNK, CHUNK), F32)
        for g in range(SSD_GROUPS):
            gs = slice(g * SSD_STATE, (g + 1) * SSD_STATE)
            bg, cg = b_ref[:, gs].astype(BF16), c_ref[:, gs].astype(BF16)
            bg_t, cg_t = b_ref[:, gs].T.astype(BF16), c_ref[:, gs].T.astype(BF16)
            cb, cb_t = _dot(cg, bg_t), _dot(bg, cg_t)
            dcb = jnp.zeros((CHUNK, CHUNK), F32)
            dbg = jnp.zeros((CHUNK, SSD_STATE), F32)
            dcg = jnp.zeros((CHUNK, SSD_STATE), F32)
            for pp in range(PAIRS_PER_GROUP):
                j = g * PAIRS_PER_GROUP + pp
                ps = slice(j * LANES, (j + 1) * LANES)
                x = xs_ref[:, ps]
                dtp, ecol, dsr = c["dt_x"][:, ps], c["ecol_x"][:, ps], c["dsr_x"][:, ps]
                elast = c["elast_x"][:, ps]
                xdt = x * dtp
                xb = xdt.astype(BF16)
                dskp = jnp.where(lane_lo[0:1], dsk[:, 2 * j:2 * j + 1], dsk[:, 2 * j + 1:2 * j + 2])
                dyp = dy[:, ps]
                dyb = dyp.astype(BF16)
                sp, dsn = sp_ref[0, j], ds_scr[j]
                spb, dsnb = sp.astype(BF16), dsn.astype(BF16)
                y_off = ecol * _dot(cg, spb)
                dw = (dyp * ecol).astype(BF16)
                dcg = dcg + _dot_nt(dw, spb)
                dsp = _dot(cg_t, dw) + elast * dsn
                xd = xdt * dsr
                zd = _dot(bg, dsnb) * dsr
                dbg = dbg + _dot_nt(xd.astype(BF16), dsnb)
                dxdt = zd
                zero = jnp.zeros_like(xb)
                for h, lm in ((2 * j, lane_lo), (2 * j + 1, jnp.logical_not(lane_lo))):
                    le = _head_decay(c, h)
                    dm = _dot_nt(jnp.where(lm, dyb, zero), jnp.where(lm, xb, zero))
                    dcb = dcb + dm * le
                    m = cb * le
                    m_t = (cb_t * _head_decay(c, h, transposed=True)).astype(BF16)
                    dxdt = dxdt + jnp.where(lm, _dot(m_t, dyb), 0.0)
                    q = dm * m
                    da_cols = da_cols + jnp.where(col == h, jnp.sum(q, axis=1, keepdims=True), 0.0)
                    da_rows = da_rows + jnp.where(row == h, _colsum(q), 0.0)
                ds_scr[j] = dsp
                dxs_ref[:, ps] = dxdt * dtp + dskp * dyp
                p_a.append(dyp * y_off - xdt * zd)
                p_dt.append(dxdt * x)
                v_last.append(_colsum(zd * xdt) + elast * _colsum(dsn * sp))
            dcbb = dcb.astype(BF16)
            db_ref[:, gs] = dbg + _dot_tn(dcbb, cg)
            dc_ref[:, gs] = dcg + _dot(dcbb, bg)
        e = e_ref[...]
        rows8 = jnp.concatenate([jnp.concatenate(v_last, axis=1), _colsum(dy * xs_ref[...]),
                                 jnp.zeros((6, SSD_INNER), F32)], axis=0)
        r8 = _dot_split(rows8, e)
        da = (_dot_split(jnp.concatenate(p_a, axis=1), e) + jnp.where(row == CHUNK - 1, r8[0:1], 0.0)
              + da_cols - da_rows.T)
        ddsk_ref[...] += r8[1:2]
        dadt = jnp.dot((row <= col).astype(F32), da, precision=HIGHEST, preferred_element_type=F32)
        ddt = dadt * a + _dot_split(jnp.concatenate(p_dt, axis=1), e)
        dalog_ref[...] += _colsum(dadt * dt) * a
        ddtr = ddt * _sigmoid(dtr + dtb_ref[...])
        ddtr_ref[...] = ddtr
        ddtb_ref[...] += _colsum(ddtr)

    ck = lambda n, col=0: pl.BlockSpec((CHUNK, n), lambda c: (nc - 1 - c, col))
    acc = lambda n: _full((1, n))
    return pl.pallas_call(
        body, name="ssd_bwd", grid=(nc,),
        in_specs=[ck(SSD_INNER), ck(SSD_INNER), ck(SSD_INNER), ck(SSD_INNER), ck(SSD_BC, SSD_INNER // SSD_BC),
                  ck(SSD_BC, SSD_INNER // SSD_BC + 1), ck(DT_PAD),
                  pl.BlockSpec((1, N_PAIRS, SSD_STATE, LANES), lambda c: (nc - 1 - c, 0, 0, 0)),
                  acc(DT_PAD), acc(DT_PAD), acc(DT_PAD), acc(SSD_INNER), _full((SSD_INNER, LANES)),
                  _full((LANES, SSD_INNER))],
        out_specs=[ck(SSD_INNER), ck(SSD_INNER), ck(SSD_BC), ck(SSD_BC), ck(DT_PAD),
                   acc(SSD_INNER), acc(DT_PAD), acc(DT_PAD), acc(DT_PAD)],
        out_shape=[jax.ShapeDtypeStruct((t, SSD_INNER), BF16), jax.ShapeDtypeStruct((t, SSD_INNER), F32),
                   jax.ShapeDtypeStruct((t, SSD_BC), F32), jax.ShapeDtypeStruct((t, SSD_BC), F32),
                   jax.ShapeDtypeStruct((t, DT_PAD), F32), jax.ShapeDtypeStruct((1, SSD_INNER), F32),
                   jax.ShapeDtypeStruct((1, DT_PAD), F32), jax.ShapeDtypeStruct((1, DT_PAD), F32),
                   jax.ShapeDtypeStruct((1, DT_PAD), F32)],
        scratch_shapes=[pltpu.VMEM((N_PAIRS, SSD_STATE, LANES), F32)], compiler_params=_params(1),
    )(dya, y, z, xc, xc, xc, dtr, sprev, dtb, alog, dsk, nw, e_heads, e_t)


def _sgu_act(uv, uvb, lnw, lnb):
    a = _gelu(uv + uvb)
    return a[:, :SGU_WIDTH], _layer_norm(a[:, SGU_WIDTH:], lnw, lnb)


def _sgu_weights(ws_ref):
    row = lax.broadcasted_iota(jnp.int32, (CHUNK, CHUNK), 0)
    col = lax.broadcasted_iota(jnp.int32, (CHUNK, CHUNK), 1)
    return [jnp.where(row >= col, ws_ref[g], 0.0).astype(BF16) for g in range(SGU_GROUPS)], row >= col


def _sgu_fwd(uv, uvb, lnw, lnb, ws, bs_t):
    t = uv.shape[0]

    def body(uv_ref, uvb_ref, lnw_ref, lnb_ref, ws_ref, bs_ref, o_ref):
        u, vn = _sgu_act(uv_ref[...], uvb_ref[...], lnw_ref[...], lnb_ref[...])
        wc, _ = _sgu_weights(ws_ref)
        bs = bs_ref[...]
        for g in range(SGU_GROUPS):
            gs = slice(g * LANES, (g + 1) * LANES)
            mixed = _dot(wc[g], vn[:, gs].astype(BF16)) + bs[:, g:g + 1]
            o_ref[:, gs] = (u[:, gs] * mixed).astype(BF16)

    return pl.pallas_call(
        body, name="sgu_fwd", grid=(t // CHUNK,),
        in_specs=[_rows(CHUNK, 2 * SGU_WIDTH), _full((1, 2 * SGU_WIDTH)), _full((1, SGU_WIDTH)), _full((1, SGU_WIDTH)),
                  _full(ws.shape), _full(bs_t.shape)],
        out_specs=_rows(CHUNK, SGU_WIDTH), out_shape=jax.ShapeDtypeStruct((t, SGU_WIDTH), BF16),
        compiler_params=_params(1),
    )(uv, uvb, lnw, lnb, ws, bs_t)


def _sgu_bwd(dyb, uv, uvb, lnw, lnb, ws, bs_t, e_groups):
    t = uv.shape[0]

    def body(dyb_ref, uv_ref, uvb_ref, lnw_ref, lnb_ref, ws_ref, bs_ref, e_ref,
             duv_ref, duvb_ref, dlnw_ref, dlnb_ref, dws_ref, dbs_ref):
        @pl.when(pl.program_id(0) == 0)
        def _():
            for r in (duvb_ref, dlnw_ref, dlnb_ref, dws_ref, dbs_ref):
                r[...] = jnp.zeros_like(r)

        (u, vn), act_vjp = jax.vjp(_sgu_act, uv_ref[...], uvb_ref[...], lnw_ref[...], lnb_ref[...])
        wc, causal = _sgu_weights(ws_ref)
        bs = bs_ref[...]
        dyb = dyb_ref[...]
        du, dvn, dmix = [], [], []
        for g in range(SGU_GROUPS):
            gs = slice(g * LANES, (g + 1) * LANES)
            vb = vn[:, gs].astype(BF16)
            mixed = _dot(wc[g], vb) + bs[:, g:g + 1]
            dm = dyb[:, gs] * u[:, gs]
            dmb = dm.astype(BF16)
            du.append(dyb[:, gs] * mixed)
            dvn.append(_dot_tn(wc[g], dmb))
            dws_ref[g] += jnp.where(causal, _dot_nt(dmb, vb), 0.0)
            dmix.append(dm)
        dbs_ref[...] += _dot_split(jnp.concatenate(dmix, axis=1), e_ref[...])
        duv, duvb, dlnw, dlnb = act_vjp((jnp.concatenate(du, axis=1), jnp.concatenate(dvn, axis=1)))
        duv_ref[...] = duv.astype(BF16)
        duvb_ref[...] += duvb
        dlnw_ref[...] += dlnw
        dlnb_ref[...] += dlnb

    return pl.pallas_call(
        body, name="sgu_bwd", grid=(t // CHUNK,),
        in_specs=[_rows(CHUNK, SGU_WIDTH), _rows(CHUNK, 2 * SGU_WIDTH), _full((1, 2 * SGU_WIDTH)),
                  _full((1, SGU_WIDTH)), _full((1, SGU_WIDTH)), _full(ws.shape), _full(bs_t.shape),
                  _full(e_groups.shape)],
        out_specs=[_rows(CHUNK, 2 * SGU_WIDTH), _full((1, 2 * SGU_WIDTH)), _full((1, SGU_WIDTH)),
                   _full((1, SGU_WIDTH)), _full(ws.shape), _full(bs_t.shape)],
        out_shape=[jax.ShapeDtypeStruct((t, 2 * SGU_WIDTH), BF16), jax.ShapeDtypeStruct((1, 2 * SGU_WIDTH), F32),
                   jax.ShapeDtypeStruct((1, SGU_WIDTH), F32), jax.ShapeDtypeStruct((1, SGU_WIDTH), F32),
                   jax.ShapeDtypeStruct(ws.shape, F32), jax.ShapeDtypeStruct(bs_t.shape, F32)],
        compiler_params=_params(1),
    )(dyb, uv, uvb, lnw, lnb, ws, bs_t, e_groups)


def _merge(gates, pa, pb, bg):
    s = _sigmoid(gates + bg)
    return s[:, :D_MODEL] * pa + s[:, D_MODEL:] * pb


def _merge_fwd(gates, pa, pb, bg, tm=256):
    t = gates.shape[0]

    def body(g_ref, pa_ref, pb_ref, bg_ref, o_ref):
        o_ref[...] = _merge(g_ref[...], pa_ref[...], pb_ref[...], bg_ref[...]).astype(BF16)

    return pl.pallas_call(
        body, name="merge_fwd", grid=(t // tm,),
        in_specs=[_rows(tm, 2 * D_MODEL), _rows(tm, D_MODEL), _rows(tm, D_MODEL), _full((1, 2 * D_MODEL))],
        out_specs=_rows(tm, D_MODEL), out_shape=jax.ShapeDtypeStruct((t, D_MODEL), BF16), compiler_params=_params(1),
    )(gates, pa, pb, bg)


def _merge_bwd(dmix, gates, pa, pb, bg, tm=256):
    t = gates.shape[0]

    def body(d_ref, g_ref, pa_ref, pb_ref, bg_ref, dg_ref, dpa_ref, dpb_ref, dbg_ref):
        @pl.when(pl.program_id(0) == 0)
        def _():
            dbg_ref[...] = jnp.zeros_like(dbg_ref)

        _, vjp = jax.vjp(_merge, g_ref[...], pa_ref[...], pb_ref[...], bg_ref[...])
        dg, dpa, dpb, dbg = vjp(d_ref[...])
        dg_ref[...] = dg.astype(BF16)
        dpa_ref[...] = dpa.astype(BF16)
        dpb_ref[...] = dpb.astype(BF16)
        dbg_ref[...] += dbg

    return pl.pallas_call(
        body, name="merge_bwd", grid=(t // tm,),
        in_specs=[_rows(tm, D_MODEL), _rows(tm, 2 * D_MODEL), _rows(tm, D_MODEL), _rows(tm, D_MODEL),
                  _full((1, 2 * D_MODEL))],
        out_specs=[_rows(tm, 2 * D_MODEL), _rows(tm, D_MODEL), _rows(tm, D_MODEL), _full((1, 2 * D_MODEL))],
        out_shape=[jax.ShapeDtypeStruct((t, 2 * D_MODEL), BF16), jax.ShapeDtypeStruct((t, D_MODEL), BF16),
                   jax.ShapeDtypeStruct((t, D_MODEL), BF16), jax.ShapeDtypeStruct((1, 2 * D_MODEL), F32)],
        compiler_params=_params(1),
    )(dmix, gates, pa, pb, bg)


def _conv_f_fwd(up, cw, cb, tm=128):
    t, c = up.shape

    def body(x_ref, h_ref, w_ref, b_ref, o_ref, y_ref):
        halo = jnp.where(pl.program_id(0) > 0, h_ref[...], 0.0)
        y = _causal_conv(x_ref[...], halo, w_ref[...], b_ref[...])
        y_ref[...] = y.astype(BF16)
        o_ref[...] = (_silu(y[:, :D_FF]) * y[:, D_FF:]).astype(BF16)

    return pl.pallas_call(
        body, name="conv_f_fwd", grid=(t // tm,),
        in_specs=[_rows(tm, c), _halo(tm, c), _full(cw.shape), _full((1, c))],
        out_specs=[_rows(tm, D_FF), _rows(tm, c)],
        out_shape=[jax.ShapeDtypeStruct((t, D_FF), BF16), jax.ShapeDtypeStruct((t, c), BF16)],
        compiler_params=_params(1),
    )(up, up, cw, cb)


def _conv_f_bwd(dact, y, up, cw, tm=128):
    t, c = up.shape
    nt = t // tm

    def body(d_ref, y_ref, x_ref, w_ref, dx_ref, dw_ref, db_ref, nxt_scr):
        @pl.when(pl.program_id(0) == 0)
        def _():
            nxt_scr[...] = jnp.zeros_like(nxt_scr)
            dw_ref[...] = jnp.zeros_like(dw_ref)
            db_ref[...] = jnp.zeros_like(db_ref)

        a, v = y_ref[:, :D_FF].astype(F32), y_ref[:, D_FF:].astype(F32)
        d = d_ref[...]
        dy = jnp.concatenate([d * v * _dsilu(a), d * _silu(a)], axis=1)
        dx, dw = _causal_conv_bwd(dy, nxt_scr[...], x_ref[...], w_ref[...])
        dx_ref[...] = dx.astype(BF16)
        nxt_scr[...] = dy[:8]
        dw_ref[...] += dw
        db_ref[...] += _colsum(dy)

    return pl.pallas_call(
        body, name="conv_f_bwd", grid=(nt,),
        in_specs=[_rows(tm, D_FF, nt, True), _rows(tm, c, nt, True), _rows(tm, c, nt, True), _full(cw.shape)],
        out_specs=[_rows(tm, c, nt, True), _full(cw.shape), _full((1, c))],
        out_shape=[jax.ShapeDtypeStruct((t, c), BF16), jax.ShapeDtypeStruct(cw.shape, F32),
                   jax.ShapeDtypeStruct((1, c), F32)],
        scratch_shapes=[pltpu.VMEM((8, c), F32)], compiler_params=_params(1),
    )(dact, y, up, cw)


def _conv_a_bwd(dxs, db, dc, y, xbc, cw, tm=256):
    t, c = xbc.shape
    nt = t // tm

    def body(dxs_ref, db_ref, dc_ref, y_ref, x_ref, w_ref, dx_ref, dw_ref, dbias_ref, nxt_scr):
        @pl.when(pl.program_id(0) == 0)
        def _():
            nxt_scr[...] = jnp.zeros_like(nxt_scr)
            dw_ref[...] = jnp.zeros_like(dw_ref)
            dbias_ref[...] = jnp.zeros_like(dbias_ref)

        dy = jnp.concatenate([dxs_ref[...], db_ref[...], dc_ref[...]], axis=1) * _dsilu(y_ref[...].astype(F32))
        dx, dw = _causal_conv_bwd(dy, nxt_scr[...], x_ref[...], w_ref[...])
        dx_ref[...] = dx.astype(BF16)
        nxt_scr[...] = dy[:8]
        dw_ref[...] += dw
        dbias_ref[...] += _colsum(dy)

    return pl.pallas_call(
        body, name="conv_a_bwd", grid=(nt,),
        in_specs=[_rows(tm, SSD_INNER, nt, True), _rows(tm, SSD_BC, nt, True), _rows(tm, SSD_BC, nt, True),
                  _rows(tm, c, nt, True), _rows(tm, c, nt, True), _full(cw.shape)],
        out_specs=[_rows(tm, c, nt, True), _full(cw.shape), _full((1, c))],
        out_shape=[jax.ShapeDtypeStruct((t, c), BF16), jax.ShapeDtypeStruct(cw.shape, F32),
                   jax.ShapeDtypeStruct((1, c), F32)],
        scratch_shapes=[pltpu.VMEM((8, c), F32)], compiler_params=_params(1),
    )(dxs, db, dc, y, xbc, cw)


def _pad_lanes(v, n=DT_PAD):
    return jnp.pad(v, ((0, 0), (0, n - v.shape[1])))


def _local_step(x, target, w, p, after=None, late_weights=None, on_grad=None, on_small=None):
    dtb, alog, dsk = _pad_lanes(p["dt_bias"]), _pad_lanes(p["a_log"]), _pad_lanes(p["d_skip"])
    bs_t = _pad_lanes(p["b_spatial"].T)
    e_heads = (jnp.arange(SSD_INNER)[:, None] // SSD_HEAD_DIM == jnp.arange(LANES)[None, :]).astype(BF16)
    e_heads_t = (jnp.arange(LANES)[:, None] == jnp.arange(SSD_INNER)[None, :] // SSD_HEAD_DIM).astype(BF16)
    e_groups = (jnp.arange(SGU_WIDTH)[:, None] // LANES == jnp.arange(LANES)[None, :]).astype(BF16)

    n1 = _norm_fwd(x, p["norm1_w"], "norm1_fwd", after=after)
    z = _mm(n1, w["z"], "nt", "proj_z")
    xbc = _mm(n1, w["xbc"], "nt", "proj_xbc")
    dtr = _mm(n1, w["dt"], "nt", "proj_dt")
    uv = _mm(n1, w["uv"], "nt", "proj_uv")
    gates = _mm(n1, w["gates"], "nt", "proj_gates")
    xc, conv_a_out = _conv_a_fwd(xbc, w["conv_a"], p["conv_a_b"])
    y, ya, sprev = _ssd_fwd(xc, dtr, z, dtb, alog, dsk, p["ssd_norm_w"], e_heads_t)
    yb = _sgu_fwd(uv, p["uv_b"], p["v_ln_w"], p["v_ln_b"], p["w_spatial"], bs_t)
    if late_weights is not None:
        w = {**w, **late_weights(ya, yb)}
    pa = _mm(ya, w["branch_a"], "nn", "branch_a")
    pb = _mm(yb, w["branch_b"], "nn", "branch_b")
    mix = _merge_fwd(gates, pa, pb, p["b_gate"])
    wide = [(D_MODEL, F32), (D_MODEL, BF16)]
    h1, n2 = _mm_rows(mix, w["out"], "nn", "out_proj", _residual_norm, rows=[x], fulls=[p["norm2_w"]], row_outs=wide)
    up = _mm(n2, w["up"], "nt", "up_proj")
    act, conv_f_out = _conv_f_fwd(up, w["conv_f"], p["conv_f_b"])
    dh2, dh2b, loss, g_final = _mm_rows(
        act, w["down"], "nn", "down_proj", _loss_and_grad, rows=[h1, target], fulls=[p["final_norm_w"]],
        row_outs=wide, acc_outs=[(8, LANES), (1, D_MODEL)])

    on_grad = on_grad or (lambda name, grads: None)
    g = {"final_norm_w": g_final}
    g["down"] = _wgrad(act, dh2b, "down_wgrad")
    tok = on_grad("w_down", g)
    dact = _mm(dh2b, w["down"], "nt", "down_dgrad", after=tok)
    dup, g["conv_f"], g["conv_f_b"] = _conv_f_bwd(dact, conv_f_out, up, w["conv_f"])
    g["up"] = _wgrad(dup, n2, "up_wgrad")
    tok = on_grad("w_up", g)
    dh1, dh1b, g["norm2_w"] = _mm_rows(
        dup, w["up"], "nn", "up_dgrad", _norm_backward, rows=[h1, dh2], fulls=[p["norm2_w"]], row_outs=wide,
        acc_outs=[(1, D_MODEL)], after=tok)
    g["out"] = _wgrad(mix, dh1b, "out_wgrad")
    tok = on_grad("w_out", g)
    dmix = _mm(dh1b, w["out"], "nt", "out_dgrad", after=tok)
    dgates, dpa, dpb, g["b_gate"] = _merge_bwd(dmix, gates, pa, pb, p["b_gate"])
    g["branch_a"] = _wgrad(ya, dpa, "branch_a_wgrad")
    g["branch_b"] = _wgrad(yb, dpb, "branch_b_wgrad")
    tok = on_grad("w_branch", g)
    dya = _mm(dpa, w["branch_a"], "nt", "branch_a_dgrad", after=tok)
    dyb = _mm(dpb, w["branch_b"], "nt", "branch_b_dgrad", after=tok)
    duv, g["uv_b"], g["v_ln_w"], g["v_ln_b"], g["w_spatial"], dbs_t = _sgu_bwd(
        dyb, uv, p["uv_b"], p["v_ln_w"], p["v_ln_b"], p["w_spatial"], bs_t, e_groups)
    g["b_spatial"] = dbs_t[:, :SGU_GROUPS].T
    dz, dxs, db, dc, ddtr, g["ssd_norm_w"], ddtb, dalog, ddsk = _ssd_bwd(
        dya, y, z, xc, dtr, sprev, dtb, alog, dsk, p["ssd_norm_w"], e_heads, e_heads_t)
    g["dt_bias"], g["a_log"], g["d_skip"] = ddtb, dalog, ddsk
    dxbc, g["conv_a"], g["conv_a_b"] = _conv_a_bwd(dxs, db, dc, conv_a_out, xbc, w["conv_a"])
    tok = on_small(g, loss) if on_small else None
    ddtrb = ddtr.astype(BF16)
    for name, d in (("z", dz), ("xbc", dxbc), ("dt", ddtrb), ("uv", duv), ("gates", dgates)):
        g[name] = _wgrad(d, n1, name + "_wgrad", after=tok)
    tok = on_grad("w_in", g)
    dn1 = _mm(dz, w["z"], "nn", "z_dgrad", after=tok)
    dn1 = _mm(dxbc, w["xbc"], "nn", "xbc_dgrad", acc=dn1)
    dn1 = _mm(ddtrb, w["dt"], "nn", "dt_dgrad", acc=dn1)
    dn1 = _mm(duv, w["uv"], "nn", "uv_dgrad", acc=dn1)
    gx, g["norm1_w"] = _mm_rows(
        dgates, w["gates"], "nn", "gates_dgrad",
        lambda r, so_far, h, dres, w_: tuple(t[:1] for t in _norm_backward(r + so_far, h, dres, w_)),
        rows=[dn1, x, dh1], fulls=[p["norm1_w"]], row_outs=wide[:1], acc_outs=[(1, D_MODEL)])
    return loss, gx, g


def _place():
    return lax.axis_index("x"), lax.axis_index("y"), lax.axis_index("c")


def _other_chips(x, y):
    return [(1 - x, y), (x, 1 - y), (1 - x, 1 - y)]


def _all_gather(shards, name):
    n = len(shards)

    def body(*refs):
        ins, outs = refs[:n], refs[n:2 * n]
        send_sems, recv_sems, local_sems = refs[2 * n:]
        x, y, c = _place()
        me, sibling = (x, y, c), (x, y, 1 - c)
        chips = _other_chips(x, y)

        def copy(a, k, block, to, src=None):
            slot = outs[a].at[4 * block[0] + 2 * block[1] + block[2]]
            return pltpu.make_async_remote_copy(
                src_ref=slot if src is None else src, dst_ref=slot, send_sem=send_sems.at[7 * a + k],
                recv_sem=recv_sems.at[7 * a + k], device_id=to, device_id_type=MESH)

        started = []
        for a in range(n):
            mine = pltpu.make_async_copy(ins[a], outs[a].at[4 * x + 2 * y + c], local_sems.at[a])
            mine.start()
            started.append(mine)
        sends = []
        for a in range(n):
            sends.append(copy(a, 0, me, sibling, src=ins[a]))
            sends += [copy(a, 1 + j, me, (*chip, c), src=ins[a]) for j, chip in enumerate(chips)]
        for cp in sends:
            cp.start()
        for a in range(n):
            for j, chip in enumerate(chips):
                copy(a, 1 + j, (*chip, c), me).wait_recv()
                fwd = copy(a, 4 + j, (*chip, c), sibling)
                fwd.start()
                sends.append(fwd)
        for a in range(n):
            copy(a, 0, sibling, me).wait_recv()
            for j, chip in enumerate(chips):
                copy(a, 4 + j, (*chip, 1 - c), me).wait_recv()
        for cp in sends:
            cp.wait_send()
        for mine in started:
            mine.wait()

    any_spec = pl.BlockSpec(memory_space=pl.ANY)
    return pl.pallas_call(
        body, name=name, in_specs=[any_spec] * n, out_specs=[any_spec] * n,
        out_shape=[jax.ShapeDtypeStruct((N_DEV, *s.shape), s.dtype) for s in shards],
        scratch_shapes=[pltpu.SemaphoreType.DMA((7 * n,)), pltpu.SemaphoreType.DMA((7 * n,)),
                        pltpu.SemaphoreType.DMA((n,))],
    )(*shards)


HBM_SPEC = pl.BlockSpec(memory_space=pltpu.HBM)
SEM_SPEC = pl.BlockSpec(memory_space=pltpu.SEMAPHORE)
ANY_SPEC = pl.BlockSpec(memory_space=pl.ANY)
DATAFLOW = pltpu.SideEffectType.DATAFLOW_SIDE_EFFECTING
N_PEERS = N_DEV - 1


def _peers(x, y, c):
    out = []
    for r in range(1, N_DEV):
        fx, fy, fc = r >> 2 & 1, r >> 1 & 1, r & 1
        out.append(((1 - x) if fx else x, (1 - y) if fy else y, (1 - c) if fc else c))
    return out


def _gather_copies(srcs, lands, send_sems, recv_sems, sending, scatter=False):
    x, y, c = _place()
    copies = []
    for a, (src, land) in enumerate(zip(srcs, lands)):
        for j, (px, py, pc) in enumerate(_peers(x, y, c)):
            mine, theirs = 4 * x + 2 * y + c, 4 * px + 2 * py + pc
            block = src.at[theirs if sending else 0] if scatter else src
            copies.append(pltpu.make_async_remote_copy(
                src_ref=block, dst_ref=land.at[mine if sending else theirs], send_sem=send_sems.at[N_PEERS * a + j],
                recv_sem=recv_sems.at[N_PEERS * a + j], device_id=(px, py, pc), device_id_type=MESH))
    return copies


def _gather_start(shards, after, name, scatter=False):
    n = len(shards)
    after = [] if after is None else [after]

    def body(*refs):
        srcs, lands = refs[:n], refs[n:2 * n]
        send_sems, recv_sems = refs[2 * n + len(after):2 * n + len(after) + 2]
        token = refs[-1]
        for cp in _gather_copies(srcs, lands, send_sems, recv_sems, sending=True, scatter=scatter):
            cp.start()
        token[...] = jnp.zeros_like(token)

    lands = [lax.empty(s.shape if scatter else (N_DEV, *s.shape), s.dtype) for s in shards]
    hbm = lambda a: pltpu.with_memory_space_constraint(a, pltpu.HBM)
    out = pl.pallas_call(
        body, name=name,
        out_shape=(pltpu.SemaphoreType.DMA((N_PEERS * n,)), pltpu.SemaphoreType.DMA((N_PEERS * n,)),
                   *[pltpu.HBM(a.shape, a.dtype) for a in (*shards, *lands)], jax.ShapeDtypeStruct((8, LANES), F32)),
        in_specs=[HBM_SPEC] * (2 * n) + [ANY_SPEC] * len(after),
        out_specs=(SEM_SPEC, SEM_SPEC, *[HBM_SPEC] * (2 * n), pl.BlockSpec(memory_space=pltpu.VMEM)),
        input_output_aliases={i: 2 + i for i in range(2 * n)},
        compiler_params=pltpu.CompilerParams(has_side_effects=DATAFLOW),
    )(*[hbm(a) for a in (*shards, *lands)], *after)
    return out[0], out[1], out[2:2 + n], out[2 + n:2 + 2 * n], out[-1]


def _gather_wait(send_sems, recv_sems, shards, lands, after, name, scatter=False):
    n = len(shards)
    after = tuple(after)

    def body(*refs):
        srcs, lands_ = refs[:n], refs[n:2 * n]
        send, recv = refs[2 * n:2 * n + 2]
        for cp in _gather_copies(srcs, lands_, send, recv, sending=False, scatter=scatter):
            cp.wait_send()
            cp.wait_recv()

    out = pl.pallas_call(
        body, name=name, out_shape=tuple(pltpu.HBM(a.shape, a.dtype) for a in (*shards, *lands)),
        in_specs=[HBM_SPEC] * (2 * n) + [SEM_SPEC, SEM_SPEC] + [ANY_SPEC] * len(after),
        out_specs=tuple([HBM_SPEC] * (2 * n)), input_output_aliases={i: i for i in range(2 * n)},
        compiler_params=pltpu.CompilerParams(has_side_effects=DATAFLOW),
    )(*shards, *lands, send_sems, recv_sems, *after)
    return out[:n], out[n:]


def _adamw(w, g, m, v):
    m = ADAM_B1 * m + (1.0 - ADAM_B1) * g
    v = ADAM_B2 * v + (1.0 - ADAM_B2) * jnp.square(g)
    m_hat = m / (1.0 - ADAM_B1 ** ADAM_STEP)
    v_hat = v / (1.0 - ADAM_B2 ** ADAM_STEP)
    return -ADAM_LR * (m_hat / (jnp.sqrt(v_hat) + ADAM_EPS) + ADAM_WD * w), m, v


def _sum8_adamw(part, got, place, w, m, v, name, tr=256):
    _, r, c = part.shape
    if w.ndim == 3:
        tr, tc = r, 2 * LANES
        blk = pl.BlockSpec((tr, 1, tc), lambda i, j, pr: (i, 0, j))
    else:
        tr, tc = _tile2d(r, c, tr)
        blk = pl.BlockSpec((tr, tc), lambda i, j, pr: (i, j))

    def body(place_ref, own_ref, got_ref, w_ref, m_ref, v_ref, g_ref, d_ref, nm_ref, nv_ref):
        dev = 2 * place_ref[1] + place_ref[0]
        g = jnp.zeros((tr, tc), F32)
        for d in range(N_DEV):
            g = g + jnp.where(dev == d, own_ref[0], got_ref[d]).astype(F32)
        two_d = lambda ref: ref[...].reshape(tr, tc)
        delta, nm, nv = _adamw(two_d(w_ref), g, two_d(m_ref), two_d(v_ref))
        for ref, val in ((g_ref, g), (d_ref, delta), (nm_ref, nm), (nv_ref, nv)):
            ref[...] = val.reshape(ref.shape)

    grid_spec = pltpu.PrefetchScalarGridSpec(
        num_scalar_prefetch=1, grid=(r // tr, c // tc),
        in_specs=[pl.BlockSpec((1, tr, tc), lambda i, j, pr: (2 * pr[1] + pr[0], i, j)),
                  pl.BlockSpec((N_DEV, tr, tc), lambda i, j, pr: (0, i, j)), blk, blk, blk],
        out_specs=[blk] * 4)
    return pl.pallas_call(
        body, name=name, grid_spec=grid_spec, out_shape=[jax.ShapeDtypeStruct(w.shape, F32)] * 4,
        compiler_params=_params(2),
    )(place, part, got, w, m, v)


VECTORS = ["norm1_w", "b_gate", "conv_a_b", "dt_bias", "a_log", "d_skip", "ssd_norm_w", "uv_b", "v_ln_w", "v_ln_b",
           "norm2_w", "conv_f_b", "final_norm_w"]
SMALL_ORDER = VECTORS + ["w_spatial", "b_spatial", "conv_a_w", "conv_f_w"]


ROW_VECTORS = VECTORS[1:]


def _small_adamw(gathered, w, m, v):
    sizes = {n: w[n].shape[1] for n in ROW_VECTORS}
    offs, off = {}, 0
    for n in ROW_VECTORS:
        offs[n] = off
        off += -(-sizes[n] // LANES) * LANES
    loss_off = off
    k = len(SMALL_ORDER)
    n_g = len(gathered)

    def body(*refs):
        row_ref, ws_ref, bs_ref, ca_ref, cf_ref, n1_ref = refs[:n_g]
        w_refs, m_refs, v_refs = (dict(zip(SMALL_ORDER, refs[n_g + i * k:n_g + (i + 1) * k])) for i in range(3))
        outs = refs[n_g + 3 * k:]
        x, y, c = _place()
        dev = 4 * x + 2 * y + c

        def total(ref):
            s = ref[0]
            for d in range(1, N_DEV):
                s = s + ref[d]
            return s

        row = total(row_ref)
        grads = {n: row[:, offs[n]:offs[n] + sizes[n]] for n in ROW_VECTORS}
        grads["norm1_w"], grads["w_spatial"], grads["b_spatial"] = total(n1_ref), total(ws_ref), total(bs_ref)
        for n, ref in (("conv_a_w", ca_ref), ("conv_f_w", cf_ref)):
            whole, cols = total(ref), w_refs[n].shape[1]
            mine = whole[:, :cols]
            for d in range(1, N_DEV):
                mine = jnp.where(dev == d, whole[:, d * cols:(d + 1) * cols], mine)
            grads[n] = mine
        for i, n in enumerate(SMALL_ORDER):
            outs[4 * i][...] = grads[n]
            outs[4 * i + 1][...], outs[4 * i + 2][...], outs[4 * i + 3][...] = _adamw(
                w_refs[n][...], grads[n], m_refs[n][...], v_refs[n][...])
        outs[4 * k][...] = row[:, loss_off:loss_off + LANES]

    out = pl.pallas_call(
        body, name="adamw_small",
        out_shape=[jax.ShapeDtypeStruct(w[n].shape, F32) for n in SMALL_ORDER for _ in range(4)]
        + [jax.ShapeDtypeStruct((1, LANES), F32)],
        compiler_params=_params(0),
    )(*gathered, *[t[n] for t in (w, m, v) for n in SMALL_ORDER])
    return [dict(zip(SMALL_ORDER, out[j:4 * k:4])) for j in range(4)] + [out[4 * k]]


SMALL = ["norm1_w", "b_gate", "conv_a_b", "dt_bias", "a_log", "d_skip", "ssd_norm_w", "uv_b", "v_ln_w", "v_ln_b",
         "w_spatial", "b_spatial", "norm2_w", "conv_f_b", "final_norm_w"]
BIG = ["w_in", "w_branch", "w_out", "w_up", "w_down"]
TRANSPOSED = ("w_in", "w_up")
WEIGHTS = ["norm1_w", "w_in", "b_gate", "conv_a_w", "conv_a_b", "dt_bias", "a_log", "d_skip", "ssd_norm_w", "uv_b",
           "v_ln_w", "v_ln_b", "w_spatial", "b_spatial", "w_branch", "w_out", "norm2_w", "w_up", "conv_f_w",
           "conv_f_b", "w_down", "final_norm_w"]
IN_SPLITS = [("z", 0, 2048), ("xbc", 2048, 5120), ("dt", 5120, 5152), ("uv", 5152, 7200), ("gates", 7200, 9248)]


def _columns_from_devices(a):
    return a.transpose(1, 0, 2).reshape(a.shape[1], -1)


def kernel(x, norm1_w, w_in, b_gate, conv_a_w, conv_a_b, dt_bias, a_log, d_skip, ssd_norm_w, uv_b, v_ln_w, v_ln_b, w_spatial, b_spatial, w_branch, w_out, norm2_w, w_up, conv_f_w, conv_f_b, w_down, final_norm_w, loss_target, m_norm1_w, m_w_in, m_b_gate, m_conv_a_w, m_conv_a_b, m_dt_bias, m_a_log, m_d_skip, m_ssd_norm_w, m_uv_b, m_v_ln_w, m_v_ln_b, m_w_spatial, m_b_spatial, m_w_branch, m_w_out, m_norm2_w, m_w_up, m_conv_f_w, m_conv_f_b, m_w_down, m_final_norm_w, v_norm1_w, v_w_in, v_b_gate, v_conv_a_w, v_conv_a_b, v_dt_bias, v_a_log, v_d_skip, v_ssd_norm_w, v_uv_b, v_v_ln_w, v_v_ln_b, v_w_spatial, v_b_spatial, v_w_branch, v_w_out, v_norm2_w, v_w_up, v_conv_f_w, v_conv_f_b, v_w_down, v_final_norm_w):
    args = dict(locals())
    wts = {n: args[n] for n in WEIGHTS}
    mom = {n: args["m_" + n] for n in WEIGHTS}
    var = {n: args["v_" + n] for n in WEIGHTS}
    cx, cy, cc = _place()
    dev = 4 * cx + 2 * cy + cc
    place = jnp.stack([cc, 2 * cx + cy]).astype(jnp.int32)

    def shard2d(n, a):
        return a[0].T if n in TRANSPOSED else a[0]

    def unshard(n, b):
        return (b.T if n in TRANSPOSED else b)[None]

    g_in, g_conv_a, g_conv_f = _all_gather(
        [shard2d("w_in", w_in).astype(BF16), conv_a_w[0], conv_f_w[0]], "gather_w_in")
    late = [shard2d(n, wts[n]).astype(BF16) for n in BIG[1:]]
    send_sems, recv_sems, late, lands, token = _gather_start(late, g_in, "gather_late_start")
    w_in_rows = g_in.reshape(-1, D_MODEL)
    w = {name: w_in_rows[lo:hi] for name, lo, hi in IN_SPLITS}
    w["dt"] = jnp.pad(w["dt"], ((0, DT_PAD - SSD_HEADS), (0, 0)))
    w["conv_a"] = _columns_from_devices(g_conv_a)
    w["conv_f"] = _columns_from_devices(g_conv_f)

    def late_weights(*after):
        mine, got = _gather_wait(send_sems, recv_sems, late, lands, after, "gather_late_wait")
        g_branch, g_out, g_up, g_down = [lax.dynamic_update_index_in_dim(land, own, dev, 0).reshape(-1, D_MODEL)
                                         for land, own in zip(got, mine)]
        return {"branch_a": g_branch[:SSD_INNER], "branch_b": g_branch[SSD_INNER:], "out": g_out, "up": g_up,
                "down": g_down}

    in_flight = {}

    def on_grad(n, g):
        part = {"w_in": lambda: jnp.concatenate([g[name][:hi - lo] for name, lo, hi in IN_SPLITS], axis=0),
                "w_branch": lambda: jnp.concatenate([g["branch_a"], g["branch_b"]], axis=0),
                "w_out": lambda: g["out"], "w_up": lambda: g["up"], "w_down": lambda: g["down"]}[n]()
        part = part.reshape(N_DEV, -1, D_MODEL)
        send, recv, (part,), (land,), tok = _gather_start([part], None, f"to_owners_start_{n}", scatter=True)
        in_flight[n] = (part, send, recv, land)
        return tok

    p = {n: wts[n][0] if wts[n].ndim > 2 else wts[n].reshape(1, -1) for n in SMALL}
    small_flight = []

    def on_small(g, loss):
        arrays = [jnp.concatenate([g[n] for n in ROW_VECTORS] + [loss[:1]], axis=1), g["w_spatial"], g["b_spatial"],
                  g["conv_a"], g["conv_f"]]
        *flight, tok = _gather_start(arrays, g["conv_a"], "gather_small_start")
        small_flight.append(flight)
        return tok

    loss, gx, g = _local_step(x[0], loss_target[0], w, p, after=token, late_weights=late_weights, on_grad=on_grad,
                              on_small=on_small)
    *flight, _ = _gather_start([g["norm1_w"]], gx, "gather_norm1_start")
    small_flight.append(flight)

    grads, delta, new_m, new_v = {}, {}, {}, {}

    def big_adamw(n, after):
        view = (lambda a: a.transpose(2, 0, 1)) if n == "w_in" else (lambda a: shard2d(n, a))
        back = (lambda b: b.transpose(1, 2, 0)) if n == "w_in" else (lambda b: unshard(n, b))
        part, send, recv, land = in_flight[n]
        (part,), (got,) = _gather_wait(send, recv, [part], [land], [after], f"to_owners_wait_{n}", scatter=True)
        out = _sum8_adamw(part, got, place, *[view(t[n]) for t in (wts, mom, var)], f"adamw_{n}")
        grads[n], delta[n], new_m[n], new_v[n] = [back(o) for o in out]
        return out[1]

    after = gx
    for n in ("w_down", "w_up", "w_out", "w_branch"):
        after = big_adamw(n, after)
    gathered = []
    for (send, recv, mine, land), name in zip(small_flight, ("gather_small_wait", "gather_norm1_wait")):
        mine, got = _gather_wait(send, recv, mine, land, [after], name)
        gathered += [lax.dynamic_update_index_in_dim(full, own, dev, 0) for full, own in zip(got, mine)]
    small = [{n: t[n][0] if t[n].ndim > 2 else t[n].reshape(1, -1) for n in SMALL_ORDER} for t in (wts, mom, var)]
    *outs, loss = _small_adamw(gathered, *small)
    for tgt, out in zip((grads, delta, new_m, new_v), outs):
        tgt.update({n: out[n].reshape(wts[n].shape) for n in SMALL_ORDER})
    big_adamw("w_in", loss)
    loss = loss[0, 0]

    return (loss, gx[None], *[grads[n] for n in WEIGHTS], *[delta[n] for n in WEIGHTS],
            *[new_m[n] for n in WEIGHTS], *[new_v[n] for n in WEIGHTS])
```

```python
import functools

import jax
import jax.numpy as jnp
from jax import lax
from jax.experimental import pallas as pl
from jax.experimental.pallas import tpu as pltpu

F32, BF16 = jnp.float32, jnp.bfloat16
HIGHEST = lax.Precision.HIGHEST

D_MODEL = 1024
SSD_INNER = 2048
SSD_HEAD_DIM = 64
SSD_HEADS = 32
SSD_GROUPS = 4
SSD_STATE = 128
SSD_BC = SSD_GROUPS * SSD_STATE
SSD_XBC = SSD_INNER + 2 * SSD_BC
SSD_CONV = 4
CHUNK = 128
N_PAIRS = SSD_HEADS // 2
PAIRS_PER_GROUP = N_PAIRS // SSD_GROUPS
SGU_WIDTH = 1024
SGU_GROUPS = 8
D_FF = 2816
FFN_CONV = 3
NORM_EPS = 1e-6
LN_EPS = 1e-5
LANES = 128
DT_PAD = LANES

ADAM_LR, ADAM_B1, ADAM_B2, ADAM_EPS, ADAM_WD, ADAM_STEP = 0.001, 0.9, 0.999, 1e-08, 0.01, 10

N_DEV = 8
VMEM_LIMIT = 56 * 1024 * 1024
MESH = pl.DeviceIdType.MESH


def _params(n_grid, **kw):
    sem = dict(dimension_semantics=("arbitrary",) * n_grid) if n_grid else {}
    return pltpu.CompilerParams(vmem_limit_bytes=VMEM_LIMIT, **sem, **kw)


def _tile(n, pref):
    t = (min(pref, n) // LANES) * LANES
    while n % t:
        t -= LANES
    return t


def _row_tile(r, pref):
    for t in range(min(pref, r) // 16 * 16, 0, -16):
        if r % t == 0:
            return t
    return r


def _tile2d(r, c, rows):
    if r % 16 == 0:
        return _row_tile(r, rows), c
    return r, _tile(c, 2 * LANES)


def _rows(tm, n, nt=None, rev=False):
    if rev:
        return pl.BlockSpec((tm, n), lambda i: (nt - 1 - i, 0))
    return pl.BlockSpec((tm, n), lambda i: (i, 0))


def _halo(tm, n, rows=8):
    per = tm // rows
    return pl.BlockSpec((rows, n), lambda i: (jnp.maximum(i * per - 1, 0), 0))


def _full(shape):
    nd = len(shape)
    return pl.BlockSpec(shape, lambda *_: (0,) * nd)


def _rms(x, w, eps=NORM_EPS):
    return x * lax.rsqrt(jnp.mean(x * x, axis=-1, keepdims=True) + eps) * w


def _layer_norm(x, w, b):
    mu = jnp.mean(x, axis=-1, keepdims=True)
    var = jnp.mean(jnp.square(x - mu), axis=-1, keepdims=True)
    return (x - mu) * lax.rsqrt(var + LN_EPS) * w + b


def _sigmoid(x):
    return 1.0 / (1.0 + jnp.exp(-x))


def _silu(x):
    return x * _sigmoid(x)


def _dsilu(x):
    s = _sigmoid(x)
    return s * (1.0 + x * (1.0 - s))


def _softplus(x):
    return jnp.maximum(x, 0.0) + jnp.log(1.0 + jnp.exp(-jnp.abs(x)))


def _gelu(x):
    return jax.nn.gelu(x)


def _dot(a, b):
    return jnp.dot(a, b, preferred_element_type=F32)


def _dot_nt(a, b):
    return lax.dot_general(a, b, (((1,), (1,)), ((), ())), preferred_element_type=F32)


def _dot_tn(a, b):
    return lax.dot_general(a, b, (((0,), (0,)), ((), ())), preferred_element_type=F32)


def _dot_split(p, e):
    hi = p.astype(BF16)
    lo = (p - hi.astype(F32)).astype(BF16)
    return _dot(hi, e) + _dot(lo, e)


def _colsum(x):
    return jnp.sum(x, axis=0, keepdims=True)


def _shift_down(x, halo, j):
    xs = pltpu.roll(x, j, 0)
    hs = pltpu.roll(halo, j, 0)
    r8 = lax.broadcasted_iota(jnp.int32, hs.shape, 0)
    return jnp.concatenate([jnp.where(r8 < j, hs, xs[:8]), xs[8:]], axis=0)


def _shift_up(x, nxt, j):
    n = x.shape[0]
    xs = pltpu.roll(x, n - j, 0)
    ns = pltpu.roll(nxt, 8 - j, 0)
    r8 = lax.broadcasted_iota(jnp.int32, ns.shape, 0)
    return jnp.concatenate([xs[:n - 8], jnp.where(r8 >= 8 - j, ns, xs[n - 8:])], axis=0)


def _causal_conv(x, halo, w, b):
    k = w.shape[0]
    y = b + w[k - 1:k, :] * x
    for j in range(1, k):
        y = y + w[k - 1 - j:k - j, :] * _shift_down(x, halo, j)
    return y


def _causal_conv_bwd(dy, nxt, x, w):
    k = w.shape[0]
    dx = w[k - 1:k, :] * dy
    dw = [_colsum(dy * x)]
    for j in range(1, k):
        dyj = _shift_up(dy, nxt, j)
        dx = dx + w[k - 1 - j:k - j, :] * dyj
        dw.append(_colsum(dyj * x))
    return dx, jnp.concatenate(dw[::-1], axis=0)


MM_TILE_PREF = 1408
MM_VMEM_BUDGET = 40 * 1024 * 1024


def _mm_tiles(m, n, k, out_bytes):
    tm, tn = _tile(m, MM_TILE_PREF), _tile(n, MM_TILE_PREF)
    need = lambda tm, tn: 2 * (2 * k * (tm + tn) + out_bytes * tm * tn)
    while need(tm, tn) > MM_VMEM_BUDGET:
        if tn >= tm and tn > LANES:
            tn = _tile(n, tn - LANES)
        else:
            tm = _tile(m, tm - LANES)
    return tm, tn


def _mm(a, b, dims, name, acc=None, out_dtype=F32, after=None):
    if dims == "tn":
        k, m = a.shape
    else:
        m, k = a.shape
    n = b.shape[0] if dims == "nt" else b.shape[1]
    tm, tn = _mm_tiles(m, n, k, 4 * (2 if acc is not None else 1))
    a_spec = pl.BlockSpec((k, tm), lambda j, i: (0, i)) if dims == "tn" else pl.BlockSpec((tm, k), lambda j, i: (i, 0))
    b_spec = pl.BlockSpec((tn, k), lambda j, i: (j, 0)) if dims == "nt" else pl.BlockSpec((k, tn), lambda j, i: (0, j))
    o_spec = pl.BlockSpec((tm, tn), lambda j, i: (i, j))
    dot = {"nn": _dot, "nt": _dot_nt, "tn": _dot_tn}[dims]

    def body(a_ref, b_ref, *rest):
        r = dot(a_ref[...], b_ref[...])
        if acc is not None:
            r = r + rest[0][...]
        rest[-1][...] = r.astype(out_dtype)

    ins, specs = [a, b], [a_spec, b_spec]
    if acc is not None:
        ins.append(acc)
        specs.append(o_spec)
    if after is not None:
        ins.append(after)
        specs.append(pl.BlockSpec(memory_space=pl.ANY))
    return pl.pallas_call(
        body, name=name, grid=(n // tn, m // tm), in_specs=specs, out_specs=o_spec,
        out_shape=jax.ShapeDtypeStruct((m, n), out_dtype), compiler_params=_params(2),
    )(*ins)


def _mm_rows(a, b, dims, name, fn, rows=(), fulls=(), row_outs=(), acc_outs=(), after=None):
    m, k = a.shape
    n = b.shape[0] if dims == "nt" else b.shape[1]
    per_row = 2 * k + 8 * n + sum(4 * r.shape[1] for r in rows) + sum(c * jnp.dtype(d).itemsize for c, d in row_outs)
    tm = _tile(m, 1024)
    while 2 * tm * per_row + 4 * k * n > MM_VMEM_BUDGET:
        tm = _tile(m, tm - LANES)
    dot = _dot_nt if dims == "nt" else _dot
    n_in = 2 + len(rows) + len(fulls) + (after is not None)

    def body(*refs):
        ins, outs = refs[:n_in], refs[n_in:]
        row_refs, acc_refs = outs[:len(row_outs)], outs[len(row_outs):]

        @pl.when(pl.program_id(0) == 0)
        def _():
            for r in acc_refs:
                r[...] = jnp.zeros_like(r)

        new_rows, incs = fn(dot(ins[0][...], ins[1][...]), *[r[...] for r in ins[2:2 + len(rows) + len(fulls)]])
        for r, val in zip(row_refs, new_rows):
            r[...] = val.astype(r.dtype)
        for r, inc in zip(acc_refs, incs):
            r[...] += inc

    extra, extra_specs = ([after], [pl.BlockSpec(memory_space=pl.ANY)]) if after is not None else ([], [])
    return pl.pallas_call(
        body, name=name, grid=(m // tm,),
        in_specs=[_rows(tm, k), _full(b.shape)] + [_rows(tm, r.shape[1]) for r in rows]
        + [_full(f.shape) for f in fulls] + extra_specs,
        out_specs=[_rows(tm, c) for c, _ in row_outs] + [_full(s) for s in acc_outs],
        out_shape=[jax.ShapeDtypeStruct((m, c), d) for c, d in row_outs]
        + [jax.ShapeDtypeStruct(s, F32) for s in acc_outs],
        compiler_params=_params(1),
    )(a, b, *rows, *fulls, *extra)


def _residual_norm(o, x, w):
    h = x + o
    return (h, _rms(h, w)), ()


def _norm_backward(dn, h, dres, w):
    _, vjp = jax.vjp(_rms, h, w)
    dh, dw = vjp(dn)
    dh = dh + dres
    return (dh, dh), (dw,)


def _loss_and_grad(dn, h1, target, w):
    yf, vjp = jax.vjp(_rms, h1 + dn, w)
    err = yf - target
    loss = 0.5 * jnp.sum(jnp.mean(err * err, axis=-1, keepdims=True))
    dh, dw = vjp(err * (1.0 / err.shape[-1]))
    return (dh, dh), (jnp.full((8, LANES), loss, F32), dw)


def _wgrad(a, d, name, after=None):
    return _mm(a, d, "tn", name, out_dtype=BF16, after=after)


def _norm_fwd(x, w, name, after=None, tm=512):
    t, d = x.shape

    def body(x_ref, w_ref, *rest):
        rest[-1][...] = _rms(x_ref[...], w_ref[...]).astype(BF16)

    extra, extra_specs = ([after], [_full(after.shape)]) if after is not None else ([], [])
    return pl.pallas_call(
        body, name=name, grid=(t // tm,), in_specs=[_rows(tm, d), _full((1, d))] + extra_specs,
        out_specs=_rows(tm, d), out_shape=jax.ShapeDtypeStruct((t, d), BF16), compiler_params=_params(1),
    )(x, w, *extra)


def _conv_a_fwd(xbc, cw, cb, tm=256):
    t, c = xbc.shape

    def body(x_ref, h_ref, w_ref, b_ref, o_ref, y_ref):
        halo = jnp.where(pl.program_id(0) > 0, h_ref[...], 0.0)
        y = _causal_conv(x_ref[...], halo, w_ref[...], b_ref[...])
        y_ref[...] = y.astype(BF16)
        o_ref[...] = _silu(y)

    return pl.pallas_call(
        body, name="conv_a_fwd", grid=(t // tm,),
        in_specs=[_rows(tm, c), _halo(tm, c), _full(cw.shape), _full((1, c))],
        out_specs=[_rows(tm, c)] * 2,
        out_shape=[jax.ShapeDtypeStruct((t, c), F32), jax.ShapeDtypeStruct((t, c), BF16)], compiler_params=_params(1),
    )(xbc, xbc, cw, cb)


def _ssd_common(dtr, dtb, alog, e_t):
    row = lax.broadcasted_iota(jnp.int32, (CHUNK, CHUNK), 0)
    col = lax.broadcasted_iota(jnp.int32, (CHUNK, CHUNK), 1)
    causal = row >= col
    dt = _softplus(dtr + dtb)
    a = -jnp.exp(alog)
    acum = jnp.dot(causal.astype(F32), dt * a, precision=HIGHEST, preferred_element_type=F32)
    spread = lambda v: _dot(v.astype(BF16), e_t)
    elast = jnp.broadcast_to(jnp.exp(acum[CHUNK - 1:CHUNK, :]), (8, LANES))
    return dict(dt=dt, a=a, acum=acum, acum_t=acum.T, causal=causal, row=row, col=col, lane_lo=col < SSD_HEAD_DIM,
                dt_x=_dot_split(dt, e_t), ecol_x=spread(jnp.exp(acum)), elast_x=_dot_split(elast, e_t)[0:1],
                dsr_x=spread(jnp.exp(acum[CHUNK - 1:CHUNK, :] - acum)))


def _head_decay(c, h, transposed=False):
    d = c["acum"][:, h:h + 1] - c["acum_t"][h:h + 1, :]
    if transposed:
        return jnp.exp(jnp.where(c["row"] <= c["col"], -d, -jnp.inf))
    return jnp.exp(jnp.where(c["causal"], d, -jnp.inf))


def _ssd_fwd(xc, dtr, z, dtb, alog, dsk, nw, e_t):
    t = xc.shape[0]
    nc = t // CHUNK

    def body(xs_ref, b_ref, c_ref, dtr_ref, z_ref, dtb_ref, alog_ref, dsk_ref, nw_ref, et_ref,
             y_ref, ya_ref, sp_ref, s_scr):
        @pl.when(pl.program_id(0) == 0)
        def _():
            s_scr[...] = jnp.zeros_like(s_scr)

        c = _ssd_common(dtr_ref[...], dtb_ref[...], alog_ref[...], et_ref[...])
        lane_lo = c["lane_lo"]
        dsk = dsk_ref[...]
        for g in range(SSD_GROUPS):
            gs = slice(g * SSD_STATE, (g + 1) * SSD_STATE)
            bg_t, cg = b_ref[:, gs].T.astype(BF16), c_ref[:, gs].astype(BF16)
            cb = _dot(cg, bg_t)
            for pp in range(PAIRS_PER_GROUP):
                j = g * PAIRS_PER_GROUP + pp
                ps = slice(j * LANES, (j + 1) * LANES)
                x = xs_ref[:, ps]
                ecol, dsr = c["ecol_x"][:, ps], c["dsr_x"][:, ps]
                xdt = x * c["dt_x"][:, ps]
                xb = xdt.astype(BF16)
                zero = jnp.zeros_like(xb)
                yd = (_dot((cb * _head_decay(c, 2 * j)).astype(BF16), jnp.where(lane_lo, xb, zero))
                      + _dot((cb * _head_decay(c, 2 * j + 1)).astype(BF16), jnp.where(lane_lo, zero, xb)))
                sp = s_scr[j]
                yo = ecol * _dot(cg, sp.astype(BF16))
                st = _dot(bg_t, (xdt * dsr).astype(BF16))
                sp_ref[0, j] = sp
                s_scr[j] = c["elast_x"][:, ps] * sp + st
                dskp = jnp.where(lane_lo[0:1], dsk[:, 2 * j:2 * j + 1], dsk[:, 2 * j + 1:2 * j + 2])
                y_ref[:, ps] = yd + yo + dskp * x
        ya_ref[...] = _rms(y_ref[...] * _silu(z_ref[...]), nw_ref[...]).astype(BF16)

    ck = lambda n, col=0: pl.BlockSpec((CHUNK, n), lambda c: (c, col))
    return pl.pallas_call(
        body, name="ssd_fwd", grid=(nc,),
        in_specs=[ck(SSD_INNER), ck(SSD_BC, SSD_INNER // SSD_BC), ck(SSD_BC, SSD_INNER // SSD_BC + 1), ck(DT_PAD),
                  ck(SSD_INNER), _full((1, DT_PAD)), _full((1, DT_PAD)), _full((1, DT_PAD)),
                  _full((1, SSD_INNER)), _full(e_t.shape)],
        out_specs=[ck(SSD_INNER), ck(SSD_INNER),
                   pl.BlockSpec((1, N_PAIRS, SSD_STATE, LANES), lambda c: (c, 0, 0, 0))],
        out_shape=[jax.ShapeDtypeStruct((t, SSD_INNER), F32), jax.ShapeDtypeStruct((t, SSD_INNER), BF16),
                   jax.ShapeDtypeStruct((nc, N_PAIRS, SSD_STATE, LANES), F32)],
        scratch_shapes=[pltpu.VMEM((N_PAIRS, SSD_STATE, LANES), F32)], compiler_params=_params(1),
    )(xc, xc, xc, dtr, z, dtb, alog, dsk, nw, e_t)


def _ssd_bwd(dya, y, z, xc, dtr, sprev, dtb, alog, dsk, nw, e_heads, e_t):
    t = xc.shape[0]
    nc = t // CHUNK

    def body(dya_ref, y_ref, z_ref, xs_ref, b_ref, c_ref, dtr_ref, sp_ref, dtb_ref, alog_ref, dsk_ref, nw_ref, e_ref,
             et_ref, dz_ref, dxs_ref, db_ref, dc_ref, ddtr_ref, dnw_ref, ddtb_ref, dalog_ref, ddsk_ref, ds_scr):
        @pl.when(pl.program_id(0) == 0)
        def _():
            ds_scr[...] = jnp.zeros_like(ds_scr)
            for r in (dnw_ref, ddtb_ref, dalog_ref, ddsk_ref):
                r[...] = jnp.zeros_like(r)

        y = y_ref[...]
        _, gate_vjp = jax.vjp(lambda y_, z_, w_: _rms(y_ * _silu(z_), w_), y, z_ref[...], nw_ref[...])
        dy, dz, dnw = gate_vjp(dya_ref[...])
        dz_ref[...] = dz.astype(BF16)
        dnw_ref[...] += dnw

        dtr = dtr_ref[...]
        c = _ssd_common(dtr, dtb_ref[...], alog_ref[...], et_ref[...])
        dt, a, lane_lo, row, col = c["dt"], c["a"], c["lane_lo"], c["row"], c["col"]
        dsk = dsk_ref[...]
        p_a, p_dt, v_last = [], [], []
        da_cols = jnp.zeros((CHUNK, CHUNK), F32)
        da_rows = jnp.zeros((CHUNK, CHUNK), F32)
        for g in range(SSD_GROUPS):
            gs = slice(g * SSD_STATE, (g + 1) * SSD_STATE)
            bg, cg = b_ref[:, gs].astype(BF16), c_ref[:, gs].astype(BF16)
            bg_t, cg_t = b_ref[:, gs].T.astype(BF16), c_ref[:, gs].T.astype(BF16)
            cb, cb_t = _dot(cg, bg_t), _dot(bg, cg_t)
            dcb = jnp.zeros((CHUNK, CHUNK), F32)
            dbg = jnp.zeros((CHUNK, SSD_STATE), F32)
            dcg = jnp.zeros((CHUNK, SSD_STATE), F32)
            for pp in range(PAIRS_PER_GROUP):
                j = g * PAIRS_PER_GROUP + pp
                ps = slice(j * LANES, (j + 1) * LANES)
                x = xs_ref[:, ps]
                dtp, ecol, dsr = c["dt_x"][:, ps], c["ecol_x"][:, ps], c["dsr_x"][:, ps]
                elast = c["elast_x"][:, ps]
                xdt = x * dtp
                xb = xdt.astype(BF16)
                dskp = jnp.where(lane_lo[0:1], dsk[:, 2 * j:2 * j + 1], dsk[:, 2 * j + 1:2 * j + 2])
                dyp = dy[:, ps]
                dyb = dyp.astype(BF16)
                sp, dsn = sp_ref[0, j], ds_scr[j]
                spb, dsnb = sp.astype(BF16), dsn.astype(BF16)
                y_off = ecol * _dot(cg, spb)
                dw = (dyp * ecol).astype(BF16)
                dcg = dcg + _dot_nt(dw, spb)
                dsp = _dot(cg_t, dw) + elast * dsn
                xd = xdt * dsr
                zd = _dot(bg, dsnb) * dsr
                dbg = dbg + _dot_nt(xd.astype(BF16), dsnb)
                dxdt = zd
                zero = jnp.zeros_like(xb)
                for h, lm in ((2 * j, lane_lo), (2 * j + 1, jnp.logical_not(lane_lo))):
                    le = _head_decay(c, h)
                    dm = _dot_nt(jnp.where(lm, dyb, zero), jnp.where(lm, xb, zero))
                    dcb = dcb + dm * le
                    m = cb * le
                    m_t = (cb_t * _head_decay(c, h, transposed=True)).astype(BF16)
                    dxdt = dxdt + jnp.where(lm, _dot(m_t, dyb), 0.0)
                    q = dm * m
                    da_cols = da_cols + jnp.where(col == h, jnp.sum(q, axis=1, keepdims=True), 0.0)
                    da_rows = da_rows + jnp.where(row == h, _colsum(q), 0.0)
                ds_scr[j] = dsp
                dxs_ref[:, ps] = dxdt * dtp + dskp * dyp
                p_a.append(dyp * y_off - xdt * zd)
                p_dt.append(dxdt * x)
                v_last.append(_colsum(zd * xdt) + elast * _colsum(dsn * sp))
            dcbb = dcb.astype(BF16)
            db_ref[:, gs] = dbg + _dot_tn(dcbb, cg)
            dc_ref[:, gs] = dcg + _dot(dcbb, bg)
        e = e_ref[...]
        rows8 = jnp.concatenate([jnp.concatenate(v_last, axis=1), _colsum(dy * xs_ref[...]),
                                 jnp.zeros((6, SSD_INNER), F32)], axis=0)
        r8 = _dot_split(rows8, e)
        da = (_dot_split(jnp.concatenate(p_a, axis=1), e) + jnp.where(row == CHUNK - 1, r8[0:1], 0.0)
              + da_cols - da_rows.T)
        ddsk_ref[...] += r8[1:2]
        dadt = jnp.dot((row <= col).astype(F32), da, precision=HIGHEST, preferred_element_type=F32)
        ddt = dadt * a + _dot_split(jnp.concatenate(p_dt, axis=1), e)
        dalog_ref[...] += _colsum(dadt * dt) * a
        ddtr = ddt * _sigmoid(dtr + dtb_ref[...])
        ddtr_ref[...] = ddtr
        ddtb_ref[...] += _colsum(ddtr)

    ck = lambda n, col=0: pl.BlockSpec((CHUNK, n), lambda c: (nc - 1 - c, col))
    acc = lambda n: _full((1, n))
    return pl.pallas_call(
        body, name="ssd_bwd", grid=(nc,),
        in_specs=[ck(SSD_INNER), ck(SSD_INNER), ck(SSD_INNER), ck(SSD_INNER), ck(SSD_BC, SSD_INNER // SSD_BC),
                  ck(SSD_BC, SSD_INNER // SSD_BC + 1), ck(DT_PAD),
                  pl.BlockSpec((1, N_PAIRS, SSD_STATE, LANES), lambda c: (nc - 1 - c, 0, 0, 0)),
                  acc(DT_PAD), acc(DT_PAD), acc(DT_PAD), acc(SSD_INNER), _full((SSD_INNER, LANES)),
                  _full((LANES, SSD_INNER))],
        out_specs=[ck(SSD_INNER), ck(SSD_INNER), ck(SSD_BC), ck(SSD_BC), ck(DT_PAD),
                   acc(SSD_INNER), acc(DT_PAD), acc(DT_PAD), acc(DT_PAD)],
        out_shape=[jax.ShapeDtypeStruct((t, SSD_INNER), BF16), jax.ShapeDtypeStruct((t, SSD_INNER), F32),
                   jax.ShapeDtypeStruct((t, SSD_BC), F32), jax.ShapeDtypeStruct((t, SSD_BC), F32),
                   jax.ShapeDtypeStruct((t, DT_PAD), F32), jax.ShapeDtypeStruct((1, SSD_INNER), F32),
                   jax.ShapeDtypeStruct((1, DT_PAD), F32), jax.ShapeDtypeStruct((1, DT_PAD), F32),
                   jax.ShapeDtypeStruct((1, DT_PAD), F32)],
        scratch_shapes=[pltpu.VMEM((N_PAIRS, SSD_STATE, LANES), F32)], compiler_params=_params(1),
    )(dya, y, z, xc, xc, xc, dtr, sprev, dtb, alog, dsk, nw, e_heads, e_t)


def _sgu_act(uv, uvb, lnw, lnb):
    a = _gelu(uv + uvb)
    return a[:, :SGU_WIDTH], _layer_norm(a[:, SGU_WIDTH:], lnw, lnb)


def _sgu_weights(ws_ref):
    row = lax.broadcasted_iota(jnp.int32, (CHUNK, CHUNK), 0)
    col = lax.broadcasted_iota(jnp.int32, (CHUNK, CHUNK), 1)
    return [jnp.where(row >= col, ws_ref[g], 0.0).astype(BF16) for g in range(SGU_GROUPS)], row >= col


def _sgu_fwd(uv, uvb, lnw, lnb, ws, bs_t):
    t = uv.shape[0]

    def body(uv_ref, uvb_ref, lnw_ref, lnb_ref, ws_ref, bs_ref, o_ref):
        u, vn = _sgu_act(uv_ref[...], uvb_ref[...], lnw_ref[...], lnb_ref[...])
        wc, _ = _sgu_weights(ws_ref)
        bs = bs_ref[...]
        for g in range(SGU_GROUPS):
            gs = slice(g * LANES, (g + 1) * LANES)
            mixed = _dot(wc[g], vn[:, gs].astype(BF16)) + bs[:, g:g + 1]
            o_ref[:, gs] = (u[:, gs] * mixed).astype(BF16)

    return pl.pallas_call(
        body, name="sgu_fwd", grid=(t // CHUNK,),
        in_specs=[_rows(CHUNK, 2 * SGU_WIDTH), _full((1, 2 * SGU_WIDTH)), _full((1, SGU_WIDTH)), _full((1, SGU_WIDTH)),
                  _full(ws.shape), _full(bs_t.shape)],
        out_specs=_rows(CHUNK, SGU_WIDTH), out_shape=jax.ShapeDtypeStruct((t, SGU_WIDTH), BF16),
        compiler_params=_params(1),
    )(uv, uvb, lnw, lnb, ws, bs_t)


def _sgu_bwd(dyb, uv, uvb, lnw, lnb, ws, bs_t, e_groups):
    t = uv.shape[0]

    def body(dyb_ref, uv_ref, uvb_ref, lnw_ref, lnb_ref, ws_ref, bs_ref, e_ref,
             duv_ref, duvb_ref, dlnw_ref, dlnb_ref, dws_ref, dbs_ref):
        @pl.when(pl.program_id(0) == 0)
        def _():
            for r in (duvb_ref, dlnw_ref, dlnb_ref, dws_ref, dbs_ref):
                r[...] = jnp.zeros_like(r)

        (u, vn), act_vjp = jax.vjp(_sgu_act, uv_ref[...], uvb_ref[...], lnw_ref[...], lnb_ref[...])
        wc, causal = _sgu_weights(ws_ref)
        bs = bs_ref[...]
        dyb = dyb_ref[...]
        du, dvn, dmix = [], [], []
        for g in range(SGU_GROUPS):
            gs = slice(g * LANES, (g + 1) * LANES)
            vb = vn[:, gs].astype(BF16)
            mixed = _dot(wc[g], vb) + bs[:, g:g + 1]
            dm = dyb[:, gs] * u[:, gs]
            dmb = dm.astype(BF16)
            du.append(dyb[:, gs] * mixed)
            dvn.append(_dot_tn(wc[g], dmb))
            dws_ref[g] += jnp.where(causal, _dot_nt(dmb, vb), 0.0)
            dmix.append(dm)
        dbs_ref[...] += _dot_split(jnp.concatenate(dmix, axis=1), e_ref[...])
        duv, duvb, dlnw, dlnb = act_vjp((jnp.concatenate(du, axis=1), jnp.concatenate(dvn, axis=1)))
        duv_ref[...] = duv.astype(BF16)
        duvb_ref[...] += duvb
        dlnw_ref[...] += dlnw
        dlnb_ref[...] += dlnb

    return pl.pallas_call(
        body, name="sgu_bwd", grid=(t // CHUNK,),
        in_specs=[_rows(CHUNK, SGU_WIDTH), _rows(CHUNK, 2 * SGU_WIDTH), _full((1, 2 * SGU_WIDTH)),
                  _full((1, SGU_WIDTH)), _full((1, SGU_WIDTH)), _full(ws.shape), _full(bs_t.shape),
                  _full(e_groups.shape)],
        out_specs=[_rows(CHUNK, 2 * SGU_WIDTH), _full((1, 2 * SGU_WIDTH)), _full((1, SGU_WIDTH)),
                   _full((1, SGU_WIDTH)), _full(ws.shape), _full(bs_t.shape)],
        out_shape=[jax.ShapeDtypeStruct((t, 2 * SGU_WIDTH), BF16), jax.ShapeDtypeStruct((1, 2 * SGU_WIDTH), F32),
                   jax.ShapeDtypeStruct((1, SGU_WIDTH), F32), jax.ShapeDtypeStruct((1, SGU_WIDTH), F32),
                   jax.ShapeDtypeStruct(ws.shape, F32), jax.ShapeDtypeStruct(bs_t.shape, F32)],
        compiler_params=_params(1),
    )(dyb, uv, uvb, lnw, lnb, ws, bs_t, e_groups)


def _merge(gates, pa, pb, bg):
    s = _sigmoid(gates + bg)
    return s[:, :D_MODEL] * pa + s[:, D_MODEL:] * pb


def _merge_fwd(gates, pa, pb, bg, tm=256):
    t = gates.shape[0]

    def body(g_ref, pa_ref, pb_ref, bg_ref, o_ref):
        o_ref[...] = _merge(g_ref[...], pa_ref[...], pb_ref[...], bg_ref[...]).astype(BF16)

    return pl.pallas_call(
        body, name="merge_fwd", grid=(t // tm,),
        in_specs=[_rows(tm, 2 * D_MODEL), _rows(tm, D_MODEL), _rows(tm, D_MODEL), _full((1, 2 * D_MODEL))],
        out_specs=_rows(tm, D_MODEL), out_shape=jax.ShapeDtypeStruct((t, D_MODEL), BF16), compiler_params=_params(1),
    )(gates, pa, pb, bg)


def _merge_bwd(dmix, gates, pa, pb, bg, tm=256):
    t = gates.shape[0]

    def body(d_ref, g_ref, pa_ref, pb_ref, bg_ref, dg_ref, dpa_ref, dpb_ref, dbg_ref):
        @pl.when(pl.program_id(0) == 0)
        def _():
            dbg_ref[...] = jnp.zeros_like(dbg_ref)

        _, vjp = jax.vjp(_merge, g_ref[...], pa_ref[...], pb_ref[...], bg_ref[...])
        dg, dpa, dpb, dbg = vjp(d_ref[...])
        dg_ref[...] = dg.astype(BF16)
        dpa_ref[...] = dpa.astype(BF16)
        dpb_ref[...] = dpb.astype(BF16)
        dbg_ref[...] += dbg

    return pl.pallas_call(
        body, name="merge_bwd", grid=(t // tm,),
        in_specs=[_rows(tm, D_MODEL), _rows(tm, 2 * D_MODEL), _rows(tm, D_MODEL), _rows(tm, D_MODEL),
                  _full((1, 2 * D_MODEL))],
        out_specs=[_rows(tm, 2 * D_MODEL), _rows(tm, D_MODEL), _rows(tm, D_MODEL), _full((1, 2 * D_MODEL))],
        out_shape=[jax.ShapeDtypeStruct((t, 2 * D_MODEL), BF16), jax.ShapeDtypeStruct((t, D_MODEL), BF16),
                   jax.ShapeDtypeStruct((t, D_MODEL), BF16), jax.ShapeDtypeStruct((1, 2 * D_MODEL), F32)],
        compiler_params=_params(1),
    )(dmix, gates, pa, pb, bg)


def _conv_f_fwd(up, cw, cb, tm=128):
    t, c = up.shape

    def body(x_ref, h_ref, w_ref, b_ref, o_ref, y_ref):
        halo = jnp.where(pl.program_id(0) > 0, h_ref[...].astype(F32)[8:], 0.0)
        y = _causal_conv(x_ref[...].astype(F32), halo, w_ref[...], b_ref[...])
        y_ref[...] = y.astype(BF16)
        o_ref[...] = (_silu(y[:, :D_FF]) * y[:, D_FF:]).astype(BF16)

    return pl.pallas_call(
        body, name="conv_f_fwd", grid=(t // tm,),
        in_specs=[_rows(tm, c), _halo(tm, c, rows=16), _full(cw.shape), _full((1, c))],
        out_specs=[_rows(tm, D_FF), _rows(tm, c)],
        out_shape=[jax.ShapeDtypeStruct((t, D_FF), BF16), jax.ShapeDtypeStruct((t, c), BF16)],
        compiler_params=_params(1),
    )(up, up, cw, cb)


def _conv_f_bwd(dact, y, up, cw, tm=128):
    t, c = up.shape
    nt = t // tm

    def body(d_ref, y_ref, x_ref, w_ref, dx_ref, dw_ref, db_ref, nxt_scr):
        @pl.when(pl.program_id(0) == 0)
        def _():
            nxt_scr[...] = jnp.zeros_like(nxt_scr)
            dw_ref[...] = jnp.zeros_like(dw_ref)
            db_ref[...] = jnp.zeros_like(db_ref)

        a, v = y_ref[:, :D_FF].astype(F32), y_ref[:, D_FF:].astype(F32)
        d = d_ref[...].astype(F32)
        dy = jnp.concatenate([d * v * _dsilu(a), d * _silu(a)], axis=1)
        dx, dw = _causal_conv_bwd(dy, nxt_scr[...], x_ref[...].astype(F32), w_ref[...])
        dx_ref[...] = dx.astype(BF16)
        nxt_scr[...] = dy[:8]
        dw_ref[...] += dw
        db_ref[...] += _colsum(dy)

    return pl.pallas_call(
        body, name="conv_f_bwd", grid=(nt,),
        in_specs=[_rows(tm, D_FF, nt, True), _rows(tm, c, nt, True), _rows(tm, c, nt, True), _full(cw.shape)],
        out_specs=[_rows(tm, c, nt, True), _full(cw.shape), _full((1, c))],
        out_shape=[jax.ShapeDtypeStruct((t, c), BF16), jax.ShapeDtypeStruct(cw.shape, F32),
                   jax.ShapeDtypeStruct((1, c), F32)],
        scratch_shapes=[pltpu.VMEM((8, c), F32)], compiler_params=_params(1),
    )(dact, y, up, cw)


def _conv_a_bwd(dxs, db, dc, y, xbc, cw, tm=256):
    t, c = xbc.shape
    nt = t // tm

    def body(dxs_ref, db_ref, dc_ref, y_ref, x_ref, w_ref, dx_ref, dw_ref, dbias_ref, nxt_scr):
        @pl.when(pl.program_id(0) == 0)
        def _():
            nxt_scr[...] = jnp.zeros_like(nxt_scr)
            dw_ref[...] = jnp.zeros_like(dw_ref)
            dbias_ref[...] = jnp.zeros_like(dbias_ref)

        dy = jnp.concatenate([dxs_ref[...], db_ref[...], dc_ref[...]], axis=1) * _dsilu(y_ref[...].astype(F32))
        dx, dw = _causal_conv_bwd(dy, nxt_scr[...], x_ref[...], w_ref[...])
        dx_ref[...] = dx.astype(BF16)
        nxt_scr[...] = dy[:8]
        dw_ref[...] += dw
        dbias_ref[...] += _colsum(dy)

    return pl.pallas_call(
        body, name="conv_a_bwd", grid=(nt,),
        in_specs=[_rows(tm, SSD_INNER, nt, True), _rows(tm, SSD_BC, nt, True), _rows(tm, SSD_BC, nt, True),
                  _rows(tm, c, nt, True), _rows(tm, c, nt, True), _full(cw.shape)],
        out_specs=[_rows(tm, c, nt, True), _full(cw.shape), _full((1, c))],
        out_shape=[jax.ShapeDtypeStruct((t, c), BF16), jax.ShapeDtypeStruct(cw.shape, F32),
                   jax.ShapeDtypeStruct((1, c), F32)],
        scratch_shapes=[pltpu.VMEM((8, c), F32)], compiler_params=_params(1),
    )(dxs, db, dc, y, xbc, cw)


def _pad_lanes(v, n=DT_PAD):
    return jnp.pad(v, ((0, 0), (0, n - v.shape[1])))


def _local_step(x, target, w, p, after=None, late_weights=None, on_grad=None, on_small=None):
    dtb, alog, dsk = _pad_lanes(p["dt_bias"]), _pad_lanes(p["a_log"]), _pad_lanes(p["d_skip"])
    bs_t = _pad_lanes(p["b_spatial"].T)
    e_heads = (jnp.arange(SSD_INNER)[:, None] // SSD_HEAD_DIM == jnp.arange(LANES)[None, :]).astype(BF16)
    e_heads_t = (jnp.arange(LANES)[:, None] == jnp.arange(SSD_INNER)[None, :] // SSD_HEAD_DIM).astype(BF16)
    e_groups = (jnp.arange(SGU_WIDTH)[:, None] // LANES == jnp.arange(LANES)[None, :]).astype(BF16)

    n1 = _norm_fwd(x, p["norm1_w"], "norm1_fwd", after=after)
    z = _mm(n1, w["z"], "nt", "proj_z")
    xbc = _mm(n1, w["xbc"], "nt", "proj_xbc")
    dtr = _mm(n1, w["dt"], "nt", "proj_dt")
    uv = _mm(n1, w["uv"], "nt", "proj_uv")
    gates = _mm(n1, w["gates"], "nt", "proj_gates")
    xc, conv_a_out = _conv_a_fwd(xbc, w["conv_a"], p["conv_a_b"])
    y, ya, sprev = _ssd_fwd(xc, dtr, z, dtb, alog, dsk, p["ssd_norm_w"], e_heads_t)
    yb = _sgu_fwd(uv, p["uv_b"], p["v_ln_w"], p["v_ln_b"], p["w_spatial"], bs_t)
    if late_weights is not None:
        w = {**w, **late_weights(ya, yb)}
    pa = _mm(ya, w["branch_a"], "nn", "branch_a")
    pb = _mm(yb, w["branch_b"], "nn", "branch_b")
    mix = _merge_fwd(gates, pa, pb, p["b_gate"])
    wide = [(D_MODEL, F32), (D_MODEL, BF16)]
    h1, n2 = _mm_rows(mix, w["out"], "nn", "out_proj", _residual_norm, rows=[x], fulls=[p["norm2_w"]], row_outs=wide)
    up = _mm(n2, w["up"], "nt", "up_proj", out_dtype=BF16)
    act, conv_f_out = _conv_f_fwd(up, w["conv_f"], p["conv_f_b"])
    dh2, dh2b, loss, g_final = _mm_rows(
        act, w["down"], "nn", "down_proj", _loss_and_grad, rows=[h1, target], fulls=[p["final_norm_w"]],
        row_outs=wide, acc_outs=[(8, LANES), (1, D_MODEL)])

    on_grad = on_grad or (lambda name, grads: None)
    g = {"final_norm_w": g_final}
    g["down"] = _wgrad(act, dh2b, "down_wgrad")
    tok = on_grad("w_down", g)
    dact = _mm(dh2b, w["down"], "nt", "down_dgrad", out_dtype=BF16, after=tok)
    dup, g["conv_f"], g["conv_f_b"] = _conv_f_bwd(dact, conv_f_out, up, w["conv_f"])
    g["up"] = _wgrad(dup, n2, "up_wgrad")
    tok = on_grad("w_up", g)
    dh1, dh1b, g["norm2_w"] = _mm_rows(
        dup, w["up"], "nn", "up_dgrad", _norm_backward, rows=[h1, dh2], fulls=[p["norm2_w"]], row_outs=wide,
        acc_outs=[(1, D_MODEL)], after=tok)
    g["out"] = _wgrad(mix, dh1b, "out_wgrad")
    tok = on_grad("w_out", g)
    dmix = _mm(dh1b, w["out"], "nt", "out_dgrad", after=tok)
    dgates, dpa, dpb, g["b_gate"] = _merge_bwd(dmix, gates, pa, pb, p["b_gate"])
    g["branch_a"] = _wgrad(ya, dpa, "branch_a_wgrad")
    g["branch_b"] = _wgrad(yb, dpb, "branch_b_wgrad")
    tok = on_grad("w_branch", g)
    dya = _mm(dpa, w["branch_a"], "nt", "branch_a_dgrad", after=tok)
    dyb = _mm(dpb, w["branch_b"], "nt", "branch_b_dgrad", after=tok)
    duv, g["uv_b"], g["v_ln_w"], g["v_ln_b"], g["w_spatial"], dbs_t = _sgu_bwd(
        dyb, uv, p["uv_b"], p["v_ln_w"], p["v_ln_b"], p["w_spatial"], bs_t, e_groups)
    g["b_spatial"] = dbs_t[:, :SGU_GROUPS].T
    dz, dxs, db, dc, ddtr, g["ssd_norm_w"], ddtb, dalog, ddsk = _ssd_bwd(
        dya, y, z, xc, dtr, sprev, dtb, alog, dsk, p["ssd_norm_w"], e_heads, e_heads_t)
    g["dt_bias"], g["a_log"], g["d_skip"] = ddtb, dalog, ddsk
    dxbc, g["conv_a"], g["conv_a_b"] = _conv_a_bwd(dxs, db, dc, conv_a_out, xbc, w["conv_a"])
    tok = on_small(g, loss) if on_small else None
    ddtrb = ddtr.astype(BF16)
    for name, d in (("z", dz), ("xbc", dxbc), ("dt", ddtrb), ("uv", duv), ("gates", dgates)):
        g[name] = _wgrad(d, n1, name + "_wgrad", after=tok)
    tok = on_grad("w_in", g)
    dn1 = _mm(dz, w["z"], "nn", "z_dgrad", after=tok)
    dn1 = _mm(dxbc, w["xbc"], "nn", "xbc_dgrad", acc=dn1)
    dn1 = _mm(ddtrb, w["dt"], "nn", "dt_dgrad", acc=dn1)
    dn1 = _mm(duv, w["uv"], "nn", "uv_dgrad", acc=dn1)
    gx, g["norm1_w"] = _mm_rows(
        dgates, w["gates"], "nn", "gates_dgrad",
        lambda r, so_far, h, dres, w_: tuple(t[:1] for t in _norm_backward(r + so_far, h, dres, w_)),
        rows=[dn1, x, dh1], fulls=[p["norm1_w"]], row_outs=wide[:1], acc_outs=[(1, D_MODEL)])
    return loss, gx, g


def _place():
    return lax.axis_index("x"), lax.axis_index("y"), lax.axis_index("c")


def _other_chips(x, y):
    return [(1 - x, y), (x, 1 - y), (1 - x, 1 - y)]


def _all_gather(shards, name):
    n = len(shards)

    def body(*refs):
        ins, outs = refs[:n], refs[n:2 * n]
        send_sems, recv_sems, local_sems = refs[2 * n:]
        x, y, c = _place()
        me, sibling = (x, y, c), (x, y, 1 - c)
        chips = _other_chips(x, y)

        def copy(a, k, block, to, src=None):
            slot = outs[a].at[4 * block[0] + 2 * block[1] + block[2]]
            return pltpu.make_async_remote_copy(
                src_ref=slot if src is None else src, dst_ref=slot, send_sem=send_sems.at[7 * a + k],
                recv_sem=recv_sems.at[7 * a + k], device_id=to, device_id_type=MESH)

        started = []
        for a in range(n):
            mine = pltpu.make_async_copy(ins[a], outs[a].at[4 * x + 2 * y + c], local_sems.at[a])
            mine.start()
            started.append(mine)
        sends = []
        for a in range(n):
            sends.append(copy(a, 0, me, sibling, src=ins[a]))
            sends += [copy(a, 1 + j, me, (*chip, c), src=ins[a]) for j, chip in enumerate(chips)]
        for cp in sends:
            cp.start()
        for a in range(n):
            for j, chip in enumerate(chips):
                copy(a, 1 + j, (*chip, c), me).wait_recv()
                fwd = copy(a, 4 + j, (*chip, c), sibling)
                fwd.start()
                sends.append(fwd)
        for a in range(n):
            copy(a, 0, sibling, me).wait_recv()
            for j, chip in enumerate(chips):
                copy(a, 4 + j, (*chip, 1 - c), me).wait_recv()
        for cp in sends:
            cp.wait_send()
        for mine in started:
            mine.wait()

    any_spec = pl.BlockSpec(memory_space=pl.ANY)
    return pl.pallas_call(
        body, name=name, in_specs=[any_spec] * n, out_specs=[any_spec] * n,
        out_shape=[jax.ShapeDtypeStruct((N_DEV, *s.shape), s.dtype) for s in shards],
        scratch_shapes=[pltpu.SemaphoreType.DMA((7 * n,)), pltpu.SemaphoreType.DMA((7 * n,)),
                        pltpu.SemaphoreType.DMA((n,))],
    )(*shards)


HBM_SPEC = pl.BlockSpec(memory_space=pltpu.HBM)
SEM_SPEC = pl.BlockSpec(memory_space=pltpu.SEMAPHORE)
ANY_SPEC = pl.BlockSpec(memory_space=pl.ANY)
DATAFLOW = pltpu.SideEffectType.DATAFLOW_SIDE_EFFECTING
N_PEERS = N_DEV - 1


def _peers(x, y, c):
    out = []
    for r in range(1, N_DEV):
        fx, fy, fc = r >> 2 & 1, r >> 1 & 1, r & 1
        out.append(((1 - x) if fx else x, (1 - y) if fy else y, (1 - c) if fc else c))
    return out


def _gather_copies(srcs, lands, send_sems, recv_sems, sending, scatter=False):
    x, y, c = _place()
    copies = []
    for a, (src, land) in enumerate(zip(srcs, lands)):
        for j, (px, py, pc) in enumerate(_peers(x, y, c)):
            mine, theirs = 4 * x + 2 * y + c, 4 * px + 2 * py + pc
            block = src.at[theirs if sending else 0] if scatter else src
            copies.append(pltpu.make_async_remote_copy(
                src_ref=block, dst_ref=land.at[mine if sending else theirs], send_sem=send_sems.at[N_PEERS * a + j],
                recv_sem=recv_sems.at[N_PEERS * a + j], device_id=(px, py, pc), device_id_type=MESH))
    return copies


def _gather_start(shards, after, name, scatter=False):
    n = len(shards)
    after = [] if after is None else [after]

    def body(*refs):
        srcs, lands = refs[:n], refs[n:2 * n]
        send_sems, recv_sems = refs[2 * n + len(after):2 * n + len(after) + 2]
        token = refs[-1]
        for cp in _gather_copies(srcs, lands, send_sems, recv_sems, sending=True, scatter=scatter):
            cp.start()
        token[...] = jnp.zeros_like(token)

    lands = [lax.empty(s.shape if scatter else (N_DEV, *s.shape), s.dtype) for s in shards]
    hbm = lambda a: pltpu.with_memory_space_constraint(a, pltpu.HBM)
    out = pl.pallas_call(
        body, name=name,
        out_shape=(pltpu.SemaphoreType.DMA((N_PEERS * n,)), pltpu.SemaphoreType.DMA((N_PEERS * n,)),
                   *[pltpu.HBM(a.shape, a.dtype) for a in (*shards, *lands)], jax.ShapeDtypeStruct((8, LANES), F32)),
        in_specs=[HBM_SPEC] * (2 * n) + [ANY_SPEC] * len(after),
        out_specs=(SEM_SPEC, SEM_SPEC, *[HBM_SPEC] * (2 * n), pl.BlockSpec(memory_space=pltpu.VMEM)),
        input_output_aliases={i: 2 + i for i in range(2 * n)},
        compiler_params=pltpu.CompilerParams(has_side_effects=DATAFLOW),
    )(*[hbm(a) for a in (*shards, *lands)], *after)
    return out[0], out[1], out[2:2 + n], out[2 + n:2 + 2 * n], out[-1]


def _gather_wait(send_sems, recv_sems, shards, lands, after, name, scatter=False):
    n = len(shards)
    after = tuple(after)

    def body(*refs):
        srcs, lands_ = refs[:n], refs[n:2 * n]
        send, recv = refs[2 * n:2 * n + 2]
        for cp in _gather_copies(srcs, lands_, send, recv, sending=False, scatter=scatter):
            cp.wait_send()
            cp.wait_recv()

    out = pl.pallas_call(
        body, name=name, out_shape=tuple(pltpu.HBM(a.shape, a.dtype) for a in (*shards, *lands)),
        in_specs=[HBM_SPEC] * (2 * n) + [SEM_SPEC, SEM_SPEC] + [ANY_SPEC] * len(after),
        out_specs=tuple([HBM_SPEC] * (2 * n)), input_output_aliases={i: i for i in range(2 * n)},
        compiler_params=pltpu.CompilerParams(has_side_effects=DATAFLOW),
    )(*shards, *lands, send_sems, recv_sems, *after)
    return out[:n], out[n:]


def _adamw(w, g, m, v):
    m = ADAM_B1 * m + (1.0 - ADAM_B1) * g
    v = ADAM_B2 * v + (1.0 - ADAM_B2) * jnp.square(g)
    m_hat = m / (1.0 - ADAM_B1 ** ADAM_STEP)
    v_hat = v / (1.0 - ADAM_B2 ** ADAM_STEP)
    return -ADAM_LR * (m_hat / (jnp.sqrt(v_hat) + ADAM_EPS) + ADAM_WD * w), m, v


def _sum8_adamw(part, got, place, w, m, v, name, tr=256):
    _, r, c = part.shape
    if w.ndim == 3:
        tr, tc = r, 2 * LANES
        blk = pl.BlockSpec((tr, 1, tc), lambda i, j, pr: (i, 0, j))
    else:
        tr, tc = _tile2d(r, c, tr)
        blk = pl.BlockSpec((tr, tc), lambda i, j, pr: (i, j))

    def body(place_ref, own_ref, got_ref, w_ref, m_ref, v_ref, g_ref, d_ref, nm_ref, nv_ref):
        dev = 2 * place_ref[1] + place_ref[0]
        g = jnp.zeros((tr, tc), F32)
        for d in range(N_DEV):
            g = g + jnp.where(dev == d, own_ref[0], got_ref[d]).astype(F32)
        two_d = lambda ref: ref[...].reshape(tr, tc)
        delta, nm, nv = _adamw(two_d(w_ref), g, two_d(m_ref), two_d(v_ref))
        for ref, val in ((g_ref, g), (d_ref, delta), (nm_ref, nm), (nv_ref, nv)):
            ref[...] = val.reshape(ref.shape)

    grid_spec = pltpu.PrefetchScalarGridSpec(
        num_scalar_prefetch=1, grid=(r // tr, c // tc),
        in_specs=[pl.BlockSpec((1, tr, tc), lambda i, j, pr: (2 * pr[1] + pr[0], i, j)),
                  pl.BlockSpec((N_DEV, tr, tc), lambda i, j, pr: (0, i, j)), blk, blk, blk],
        out_specs=[blk] * 4)
    return pl.pallas_call(
        body, name=name, grid_spec=grid_spec, out_shape=[jax.ShapeDtypeStruct(w.shape, F32)] * 4,
        compiler_params=_params(2),
    )(place, part, got, w, m, v)


VECTORS = ["norm1_w", "b_gate", "conv_a_b", "dt_bias", "a_log", "d_skip", "ssd_norm_w", "uv_b", "v_ln_w", "v_ln_b",
           "norm2_w", "conv_f_b", "final_norm_w"]
SMALL_ORDER = VECTORS + ["w_spatial", "b_spatial", "conv_a_w", "conv_f_w"]


ROW_VECTORS = VECTORS[1:]


def _small_adamw(gathered, w, m, v):
    sizes = {n: w[n].shape[1] for n in ROW_VECTORS}
    offs, off = {}, 0
    for n in ROW_VECTORS:
        offs[n] = off
        off += -(-sizes[n] // LANES) * LANES
    loss_off = off
    k = len(SMALL_ORDER)
    n_g = len(gathered)

    def body(*refs):
        row_ref, ws_ref, bs_ref, ca_ref, cf_ref, n1_ref = refs[:n_g]
        w_refs, m_refs, v_refs = (dict(zip(SMALL_ORDER, refs[n_g + i * k:n_g + (i + 1) * k])) for i in range(3))
        outs = refs[n_g + 3 * k:]
        x, y, c = _place()
        dev = 4 * x + 2 * y + c

        def total(ref):
            s = ref[0]
            for d in range(1, N_DEV):
                s = s + ref[d]
            return s

        row = total(row_ref)
        grads = {n: row[:, offs[n]:offs[n] + sizes[n]] for n in ROW_VECTORS}
        grads["norm1_w"], grads["w_spatial"], grads["b_spatial"] = total(n1_ref), total(ws_ref), total(bs_ref)
        for n, ref in (("conv_a_w", ca_ref), ("conv_f_w", cf_ref)):
            whole, cols = total(ref), w_refs[n].shape[1]
            mine = whole[:, :cols]
            for d in range(1, N_DEV):
                mine = jnp.where(dev == d, whole[:, d * cols:(d + 1) * cols], mine)
            grads[n] = mine
        for i, n in enumerate(SMALL_ORDER):
            outs[4 * i][...] = grads[n]
            outs[4 * i + 1][...], outs[4 * i + 2][...], outs[4 * i + 3][...] = _adamw(
                w_refs[n][...], grads[n], m_refs[n][...], v_refs[n][...])
        outs[4 * k][...] = row[:, loss_off:loss_off + LANES]

    out = pl.pallas_call(
        body, name="adamw_small",
        out_shape=[jax.ShapeDtypeStruct(w[n].shape, F32) for n in SMALL_ORDER for _ in range(4)]
        + [jax.ShapeDtypeStruct((1, LANES), F32)],
        compiler_params=_params(0),
    )(*gathered, *[t[n] for t in (w, m, v) for n in SMALL_ORDER])
    return [dict(zip(SMALL_ORDER, out[j:4 * k:4])) for j in range(4)] + [out[4 * k]]


SMALL = ["norm1_w", "b_gate", "conv_a_b", "dt_bias", "a_log", "d_skip", "ssd_norm_w", "uv_b", "v_ln_w", "v_ln_b",
         "w_spatial", "b_spatial", "norm2_w", "conv_f_b", "final_norm_w"]
BIG = ["w_in", "w_branch", "w_out", "w_up", "w_down"]
TRANSPOSED = ("w_in", "w_up")
WEIGHTS = ["norm1_w", "w_in", "b_gate", "conv_a_w", "conv_a_b", "dt_bias", "a_log", "d_skip", "ssd_norm_w", "uv_b",
           "v_ln_w", "v_ln_b", "w_spatial", "b_spatial", "w_branch", "w_out", "norm2_w", "w_up", "conv_f_w",
           "conv_f_b", "w_down", "final_norm_w"]
IN_SPLITS = [("z", 0, 2048), ("xbc", 2048, 5120), ("dt", 5120, 5152), ("uv", 5152, 7200), ("gates", 7200, 9248)]


def _columns_from_devices(a):
    return a.transpose(1, 0, 2).reshape(a.shape[1], -1)


def kernel(x, norm1_w, w_in, b_gate, conv_a_w, conv_a_b, dt_bias, a_log, d_skip, ssd_norm_w, uv_b, v_ln_w, v_ln_b, w_spatial, b_spatial, w_branch, w_out, norm2_w, w_up, conv_f_w, conv_f_b, w_down, final_norm_w, loss_target, m_norm1_w, m_w_in, m_b_gate, m_conv_a_w, m_conv_a_b, m_dt_bias, m_a_log, m_d_skip, m_ssd_norm_w, m_uv_b, m_v_ln_w, m_v_ln_b, m_w_spatial, m_b_spatial, m_w_branch, m_w_out, m_norm2_w, m_w_up, m_conv_f_w, m_conv_f_b, m_w_down, m_final_norm_w, v_norm1_w, v_w_in, v_b_gate, v_conv_a_w, v_conv_a_b, v_dt_bias, v_a_log, v_d_skip, v_ssd_norm_w, v_uv_b, v_v_ln_w, v_v_ln_b, v_w_spatial, v_b_spatial, v_w_branch, v_w_out, v_norm2_w, v_w_up, v_conv_f_w, v_conv_f_b, v_w_down, v_final_norm_w):
    args = dict(locals())
    wts = {n: args[n] for n in WEIGHTS}
    mom = {n: args["m_" + n] for n in WEIGHTS}
    var = {n: args["v_" + n] for n in WEIGHTS}
    cx, cy, cc = _place()
    dev = 4 * cx + 2 * cy + cc
    place = jnp.stack([cc, 2 * cx + cy]).astype(jnp.int32)

    def shard2d(n, a):
        return a[0].T if n in TRANSPOSED else a[0]

    def unshard(n, b):
        return (b.T if n in TRANSPOSED else b)[None]

    g_in, g_conv_a, g_conv_f = _all_gather(
        [shard2d("w_in", w_in).astype(BF16), conv_a_w[0], conv_f_w[0]], "gather_w_in")
    late = [shard2d(n, wts[n]).astype(BF16) for n in BIG[1:]]
    send_sems, recv_sems, late, lands, token = _gather_start(late, g_in, "gather_late_start")
    w_in_rows = g_in.reshape(-1, D_MODEL)
    w = {name: w_in_rows[lo:hi] for name, lo, hi in IN_SPLITS}
    w["dt"] = jnp.pad(w["dt"], ((0, DT_PAD - SSD_HEADS), (0, 0)))
    w["conv_a"] = _columns_from_devices(g_conv_a)
    w["conv_f"] = _columns_from_devices(g_conv_f)

    def late_weights(*after):
        mine, got = _gather_wait(send_sems, recv_sems, late, lands, after, "gather_late_wait")
        g_branch, g_out, g_up, g_down = [lax.dynamic_update_index_in_dim(land, own, dev, 0).reshape(-1, D_MODEL)
                                         for land, own in zip(got, mine)]
        return {"branch_a": g_branch[:SSD_INNER], "branch_b": g_branch[SSD_INNER:], "out": g_out, "up": g_up,
                "down": g_down}

    in_flight = {}

    def on_grad(n, g):
        part = {"w_in": lambda: jnp.concatenate([g[name][:hi - lo] for name, lo, hi in IN_SPLITS], axis=0),
                "w_branch": lambda: jnp.concatenate([g["branch_a"], g["branch_b"]], axis=0),
                "w_out": lambda: g["out"], "w_up": lambda: g["up"], "w_down": lambda: g["down"]}[n]()
        part = part.reshape(N_DEV, -1, D_MODEL)
        send, recv, (part,), (land,), tok = _gather_start([part], None, f"to_owners_start_{n}", scatter=True)
        in_flight[n] = (part, send, recv, land)
        return tok

    p = {n: wts[n][0] if wts[n].ndim > 2 else wts[n].reshape(1, -1) for n in SMALL}
    small_flight = []

    def on_small(g, loss):
        arrays = [jnp.concatenate([g[n] for n in ROW_VECTORS] + [loss[:1]], axis=1), g["w_spatial"], g["b_spatial"],
                  g["conv_a"], g["conv_f"]]
        *flight, tok = _gather_start(arrays, g["conv_a"], "gather_small_start")
        small_flight.append(flight)
        return tok

    loss, gx, g = _local_step(x[0], loss_target[0], w, p, after=token, late_weights=late_weights, on_grad=on_grad,
                              on_small=on_small)
    *flight, _ = _gather_start([g["norm1_w"]], gx, "gather_norm1_start")
    small_flight.append(flight)

    grads, delta, new_m, new_v = {}, {}, {}, {}

    def big_adamw(n, after):
        view = (lambda a: a.transpose(2, 0, 1)) if n == "w_in" else (lambda a: shard2d(n, a))
        back = (lambda b: b.transpose(1, 2, 0)) if n == "w_in" else (lambda b: unshard(n, b))
        part, send, recv, land = in_flight[n]
        (part,), (got,) = _gather_wait(send, recv, [part], [land], [after], f"to_owners_wait_{n}", scatter=True)
        out = _sum8_adamw(part, got, place, *[view(t[n]) for t in (wts, mom, var)], f"adamw_{n}")
        grads[n], delta[n], new_m[n], new_v[n] = [back(o) for o in out]
        return out[1]

    after = gx
    for n in ("w_down", "w_up", "w_out", "w_branch"):
        after = big_adamw(n, after)
    gathered = []
    for (send, recv, mine, land), name in zip(small_flight, ("gather_small_wait", "gather_norm1_wait")):
        mine, got = _gather_wait(send, recv, mine, land, [after], name)
        gathered += [lax.dynamic_update_index_in_dim(full, own, dev, 0) for full, own in zip(got, mine)]
    small = [{n: t[n][0] if t[n].ndim > 2 else t[n].reshape(1, -1) for n in SMALL_ORDER} for t in (wts, mom, var)]
    *outs, loss = _small_adamw(gathered, *small)
    for tgt, out in zip((grads, delta, new_m, new_v), outs):
        tgt.update({n: out[n].reshape(wts[n].shape) for n in SMALL_ORDER})
    big_adamw("w_in", loss)
    loss = loss[0, 0]

    return (loss, gx[None], *[grads[n] for n in WEIGHTS], *[delta[n] for n in WEIGHTS],
            *[new_m[n] for n in WEIGHTS], *[new_v[n] for n in WEIGHTS])
```

```python
import functools

import jax
import jax.numpy as jnp
from jax import lax
from jax.experimental import pallas as pl
from jax.experimental.pallas import tpu as pltpu

F32, BF16 = jnp.float32, jnp.bfloat16
HIGHEST = lax.Precision.HIGHEST

D_MODEL = 1024
SSD_INNER = 2048
SSD_HEAD_DIM = 64
SSD_HEADS = 32
SSD_GROUPS = 4
SSD_STATE = 128
SSD_BC = SSD_GROUPS * SSD_STATE
SSD_XBC = SSD_INNER + 2 * SSD_BC
SSD_CONV = 4
CHUNK = 128
N_PAIRS = SSD_HEADS // 2
PAIRS_PER_GROUP = N_PAIRS // SSD_GROUPS
SGU_WIDTH = 1024
SGU_GROUPS = 8
D_FF = 2816
FFN_CONV = 3
NORM_EPS = 1e-6
LN_EPS = 1e-5
LANES = 128
DT_PAD = LANES

ADAM_LR, ADAM_B1, ADAM_B2, ADAM_EPS, ADAM_WD, ADAM_STEP = 0.001, 0.9, 0.999, 1e-08, 0.01, 10

N_DEV = 8
VMEM_LIMIT = 56 * 1024 * 1024
MESH = pl.DeviceIdType.MESH


def _params(n_grid, **kw):
    sem = dict(dimension_semantics=("arbitrary",) * n_grid) if n_grid else {}
    return pltpu.CompilerParams(vmem_limit_bytes=VMEM_LIMIT, **sem, **kw)


def _tile(n, pref):
    t = (min(pref, n) // LANES) * LANES
    while n % t:
        t -= LANES
    return t


def _row_tile(r, pref):
    for t in range(min(pref, r) // 16 * 16, 0, -16):
        if r % t == 0:
            return t
    return r


def _tile2d(r, c, rows):
    if r % 16 == 0:
        return _row_tile(r, rows), c
    return r, _tile(c, 2 * LANES)


def _rows(tm, n, nt=None, rev=False):
    if rev:
        return pl.BlockSpec((tm, n), lambda i: (nt - 1 - i, 0))
    return pl.BlockSpec((tm, n), lambda i: (i, 0))


def _halo(tm, n, rows=8):
    per = tm // rows
    return pl.BlockSpec((rows, n), lambda i: (jnp.maximum(i * per - 1, 0), 0))


def _full(shape):
    nd = len(shape)
    return pl.BlockSpec(shape, lambda *_: (0,) * nd)


def _rms(x, w, eps=NORM_EPS):
    return x * lax.rsqrt(jnp.mean(x * x, axis=-1, keepdims=True) + eps) * w


def _layer_norm(x, w, b):
    mu = jnp.mean(x, axis=-1, keepdims=True)
    var = jnp.mean(jnp.square(x - mu), axis=-1, keepdims=True)
    return (x - mu) * lax.rsqrt(var + LN_EPS) * w + b


def _sigmoid(x):
    return 1.0 / (1.0 + jnp.exp(-x))


def _silu(x):
    return x * _sigmoid(x)


def _dsilu(x):
    s = _sigmoid(x)
    return s * (1.0 + x * (1.0 - s))


def _softplus(x):
    return jnp.maximum(x, 0.0) + jnp.log(1.0 + jnp.exp(-jnp.abs(x)))


def _gelu(x):
    return jax.nn.gelu(x)


def _dot(a, b):
    return jnp.dot(a, b, preferred_element_type=F32)


def _dot_nt(a, b):
    return lax.dot_general(a, b, (((1,), (1,)), ((), ())), preferred_element_type=F32)


def _dot_tn(a, b):
    return lax.dot_general(a, b, (((0,), (0,)), ((), ())), preferred_element_type=F32)


def _dot_split(p, e):
    hi = p.astype(BF16)
    lo = (p - hi.astype(F32)).astype(BF16)
    return _dot(hi, e) + _dot(lo, e)


def _colsum(x):
    return jnp.sum(x, axis=0, keepdims=True)


def _shift_down(x, halo, j):
    xs = pltpu.roll(x, j, 0)
    hs = pltpu.roll(halo, j, 0)
    r8 = lax.broadcasted_iota(jnp.int32, hs.shape, 0)
    return jnp.concatenate([jnp.where(r8 < j, hs, xs[:8]), xs[8:]], axis=0)


def _shift_up(x, nxt, j):
    n = x.shape[0]
    xs = pltpu.roll(x, n - j, 0)
    ns = pltpu.roll(nxt, 8 - j, 0)
    r8 = lax.broadcasted_iota(jnp.int32, ns.shape, 0)
    return jnp.concatenate([xs[:n - 8], jnp.where(r8 >= 8 - j, ns, xs[n - 8:])], axis=0)


def _causal_conv(x, halo, w, b):
    k = w.shape[0]
    y = b + w[k - 1:k, :] * x
    for j in range(1, k):
        y = y + w[k - 1 - j:k - j, :] * _shift_down(x, halo, j)
    return y


def _causal_conv_bwd(dy, nxt, x, w):
    k = w.shape[0]
    dx = w[k - 1:k, :] * dy
    dw = [_colsum(dy * x)]
    for j in range(1, k):
        dyj = _shift_up(dy, nxt, j)
        dx = dx + w[k - 1 - j:k - j, :] * dyj
        dw.append(_colsum(dyj * x))
    return dx, jnp.concatenate(dw[::-1], axis=0)


MM_TILE_PREF = 1408
MM_VMEM_BUDGET = 40 * 1024 * 1024


def _mm_tiles(m, n, k, out_bytes):
    tm, tn = _tile(m, MM_TILE_PREF), _tile(n, MM_TILE_PREF)
    need = lambda tm, tn: 2 * (2 * k * (tm + tn) + out_bytes * tm * tn)
    while need(tm, tn) > MM_VMEM_BUDGET:
        if tn >= tm and tn > LANES:
            tn = _tile(n, tn - LANES)
        else:
            tm = _tile(m, tm - LANES)
    return tm, tn


def _mm(a, b, dims, name, acc=None, out_dtype=F32, after=None):
    if dims == "tn":
        k, m = a.shape
    else:
        m, k = a.shape
    n = b.shape[0] if dims == "nt" else b.shape[1]
    tm, tn = _mm_tiles(m, n, k, 4 * (2 if acc is not None else 1))
    a_spec = pl.BlockSpec((k, tm), lambda j, i: (0, i)) if dims == "tn" else pl.BlockSpec((tm, k), lambda j, i: (i, 0))
    b_spec = pl.BlockSpec((tn, k), lambda j, i: (j, 0)) if dims == "nt" else pl.BlockSpec((k, tn), lambda j, i: (0, j))
    o_spec = pl.BlockSpec((tm, tn), lambda j, i: (i, j))
    dot = {"nn": _dot, "nt": _dot_nt, "tn": _dot_tn}[dims]

    def body(a_ref, b_ref, *rest):
        r = dot(a_ref[...], b_ref[...])
        if acc is not None:
            r = r + rest[0][...]
        rest[-1][...] = r.astype(out_dtype)

    ins, specs = [a, b], [a_spec, b_spec]
    if acc is not None:
        ins.append(acc)
        specs.append(o_spec)
    if after is not None:
        ins.append(after)
        specs.append(pl.BlockSpec(memory_space=pl.ANY))
    return pl.pallas_call(
        body, name=name, grid=(n // tn, m // tm), in_specs=specs, out_specs=o_spec,
        out_shape=jax.ShapeDtypeStruct((m, n), out_dtype), compiler_params=_params(2),
    )(*ins)


def _mm_rows(a, b, dims, name, fn, rows=(), fulls=(), row_outs=(), acc_outs=(), after=None):
    m, k = a.shape
    n = b.shape[0] if dims == "nt" else b.shape[1]
    per_row = 2 * k + 8 * n + sum(4 * r.shape[1] for r in rows) + sum(c * jnp.dtype(d).itemsize for c, d in row_outs)
    tm = _tile(m, 1024)
    while 2 * tm * per_row + 4 * k * n > MM_VMEM_BUDGET:
        tm = _tile(m, tm - LANES)
    dot = _dot_nt if dims == "nt" else _dot
    n_in = 2 + len(rows) + len(fulls) + (after is not None)

    def body(*refs):
        ins, outs = refs[:n_in], refs[n_in:]
        row_refs, acc_refs = outs[:len(row_outs)], outs[len(row_outs):]

        @pl.when(pl.program_id(0) == 0)
        def _():
            for r in acc_refs:
                r[...] = jnp.zeros_like(r)

        new_rows, incs = fn(dot(ins[0][...], ins[1][...]), *[r[...] for r in ins[2:2 + len(rows) + len(fulls)]])
        for r, val in zip(row_refs, new_rows):
            r[...] = val.astype(r.dtype)
        for r, inc in zip(acc_refs, incs):
            r[...] += inc

    extra, extra_specs = ([after], [pl.BlockSpec(memory_space=pl.ANY)]) if after is not None else ([], [])
    return pl.pallas_call(
        body, name=name, grid=(m // tm,),
        in_specs=[_rows(tm, k), _full(b.shape)] + [_rows(tm, r.shape[1]) for r in rows]
        + [_full(f.shape) for f in fulls] + extra_specs,
        out_specs=[_rows(tm, c) for c, _ in row_outs] + [_full(s) for s in acc_outs],
        out_shape=[jax.ShapeDtypeStruct((m, c), d) for c, d in row_outs]
        + [jax.ShapeDtypeStruct(s, F32) for s in acc_outs],
        compiler_params=_params(1),
    )(a, b, *rows, *fulls, *extra)


def _residual_norm(o, x, w):
    h = x + o
    return (h, _rms(h, w)), ()


def _norm_backward(dn, h, dres, w):
    _, vjp = jax.vjp(_rms, h, w)
    dh, dw = vjp(dn)
    dh = dh + dres
    return (dh, dh), (dw,)


def _loss_and_grad(dn, h1, target, w):
    yf, vjp = jax.vjp(_rms, h1 + dn, w)
    err = yf - target
    loss = 0.5 * jnp.sum(jnp.mean(err * err, axis=-1, keepdims=True))
    dh, dw = vjp(err * (1.0 / err.shape[-1]))
    return (dh, dh), (jnp.full((8, LANES), loss, F32), dw)


def _wgrad(a, d, name, after=None):
    return _mm(a, d, "tn", name, out_dtype=BF16, after=after)


def _norm_fwd(x, w, name, after=None, tm=512):
    t, d = x.shape

    def body(x_ref, w_ref, *rest):
        rest[-1][...] = _rms(x_ref[...], w_ref[...]).astype(BF16)

    extra, extra_specs = ([after], [_full(after.shape)]) if after is not None else ([], [])
    return pl.pallas_call(
        body, name=name, grid=(t // tm,), in_specs=[_rows(tm, d), _full((1, d))] + extra_specs,
        out_specs=_rows(tm, d), out_shape=jax.ShapeDtypeStruct((t, d), BF16), compiler_params=_params(1),
    )(x, w, *extra)


def _conv_a_fwd(xbc, cw, cb, tm=256):
    t, c = xbc.shape

    def body(x_ref, h_ref, w_ref, b_ref, o_ref, y_ref):
        halo = jnp.where(pl.program_id(0) > 0, h_ref[...].astype(F32)[8:], 0.0)
        y = _causal_conv(x_ref[...].astype(F32), halo, w_ref[...], b_ref[...])
        y_ref[...] = y.astype(BF16)
        o_ref[...] = _silu(y)

    return pl.pallas_call(
        body, name="conv_a_fwd", grid=(t // tm,),
        in_specs=[_rows(tm, c), _halo(tm, c, rows=16), _full(cw.shape), _full((1, c))],
        out_specs=[_rows(tm, c)] * 2,
        out_shape=[jax.ShapeDtypeStruct((t, c), F32), jax.ShapeDtypeStruct((t, c), BF16)], compiler_params=_params(1),
    )(xbc, xbc, cw, cb)


def _ssd_common(dtr, dtb, alog, e_t):
    row = lax.broadcasted_iota(jnp.int32, (CHUNK, CHUNK), 0)
    col = lax.broadcasted_iota(jnp.int32, (CHUNK, CHUNK), 1)
    causal = row >= col
    dt = _softplus(dtr + dtb)
    a = -jnp.exp(alog)
    acum = jnp.dot(causal.astype(F32), dt * a, precision=HIGHEST, preferred_element_type=F32)
    spread = lambda v: _dot(v.astype(BF16), e_t)
    elast = jnp.broadcast_to(jnp.exp(acum[CHUNK - 1:CHUNK, :]), (8, LANES))
    return dict(dt=dt, a=a, acum=acum, acum_t=acum.T, causal=causal, row=row, col=col, lane_lo=col < SSD_HEAD_DIM,
                dt_x=_dot_split(dt, e_t), ecol_x=spread(jnp.exp(acum)), elast_x=_dot_split(elast, e_t)[0:1],
                dsr_x=spread(jnp.exp(acum[CHUNK - 1:CHUNK, :] - acum)))


def _head_decay(c, h, transposed=False):
    d = c["acum"][:, h:h + 1] - c["acum_t"][h:h + 1, :]
    if transposed:
        return jnp.exp(jnp.where(c["row"] <= c["col"], -d, -jnp.inf))
    return jnp.exp(jnp.where(c["causal"], d, -jnp.inf))


def _ssd_fwd(xc, dtr, z, dtb, alog, dsk, nw, e_t):
    t = xc.shape[0]
    nc = t // CHUNK

    def body(xs_ref, b_ref, c_ref, dtr_ref, z_ref, dtb_ref, alog_ref, dsk_ref, nw_ref, et_ref,
             y_ref, ya_ref, sp_ref, s_scr):
        @pl.when(pl.program_id(0) == 0)
        def _():
            s_scr[...] = jnp.zeros_like(s_scr)

        c = _ssd_common(dtr_ref[...], dtb_ref[...], alog_ref[...], et_ref[...])
        lane_lo = c["lane_lo"]
        dsk = dsk_ref[...]
        for g in range(SSD_GROUPS):
            gs = slice(g * SSD_STATE, (g + 1) * SSD_STATE)
            bg_t, cg = b_ref[:, gs].T.astype(BF16), c_ref[:, gs].astype(BF16)
            cb = _dot(cg, bg_t)
            for pp in range(PAIRS_PER_GROUP):
                j = g * PAIRS_PER_GROUP + pp
                ps = slice(j * LANES, (j + 1) * LANES)
                x = xs_ref[:, ps]
                ecol, dsr = c["ecol_x"][:, ps], c["dsr_x"][:, ps]
                xdt = x * c["dt_x"][:, ps]
                xb = xdt.astype(BF16)
                zero = jnp.zeros_like(xb)
                yd = (_dot((cb * _head_decay(c, 2 * j)).astype(BF16), jnp.where(lane_lo, xb, zero))
                      + _dot((cb * _head_decay(c, 2 * j + 1)).astype(BF16), jnp.where(lane_lo, zero, xb)))
                sp = s_scr[j]
                yo = ecol * _dot(cg, sp.astype(BF16))
                st = _dot(bg_t, (xdt * dsr).astype(BF16))
                sp_ref[0, j] = sp
                s_scr[j] = c["elast_x"][:, ps] * sp + st
                dskp = jnp.where(lane_lo[0:1], dsk[:, 2 * j:2 * j + 1], dsk[:, 2 * j + 1:2 * j + 2])
                y_ref[:, ps] = yd + yo + dskp * x
        ya_ref[...] = _rms(y_ref[...] * _silu(z_ref[...].astype(F32)), nw_ref[...]).astype(BF16)

    ck = lambda n, col=0: pl.BlockSpec((CHUNK, n), lambda c: (c, col))
    return pl.pallas_call(
        body, name="ssd_fwd", grid=(nc,),
        in_specs=[ck(SSD_INNER), ck(SSD_BC, SSD_INNER // SSD_BC), ck(SSD_BC, SSD_INNER // SSD_BC + 1), ck(DT_PAD),
                  ck(SSD_INNER), _full((1, DT_PAD)), _full((1, DT_PAD)), _full((1, DT_PAD)),
                  _full((1, SSD_INNER)), _full(e_t.shape)],
        out_specs=[ck(SSD_INNER), ck(SSD_INNER),
                   pl.BlockSpec((1, N_PAIRS, SSD_STATE, LANES), lambda c: (c, 0, 0, 0))],
        out_shape=[jax.ShapeDtypeStruct((t, SSD_INNER), F32), jax.ShapeDtypeStruct((t, SSD_INNER), BF16),
                   jax.ShapeDtypeStruct((nc, N_PAIRS, SSD_STATE, LANES), F32)],
        scratch_shapes=[pltpu.VMEM((N_PAIRS, SSD_STATE, LANES), F32)], compiler_params=_params(1),
    )(xc, xc, xc, dtr, z, dtb, alog, dsk, nw, e_t)


def _ssd_bwd(dya, y, z, xc, dtr, sprev, dtb, alog, dsk, nw, e_heads, e_t):
    t = xc.shape[0]
    nc = t // CHUNK

    def body(dya_ref, y_ref, z_ref, xs_ref, b_ref, c_ref, dtr_ref, sp_ref, dtb_ref, alog_ref, dsk_ref, nw_ref, e_ref,
             et_ref, dz_ref, dxs_ref, db_ref, dc_ref, ddtr_ref, dnw_ref, ddtb_ref, dalog_ref, ddsk_ref, ds_scr):
        @pl.when(pl.program_id(0) == 0)
        def _():
            ds_scr[...] = jnp.zeros_like(ds_scr)
            for r in (dnw_ref, ddtb_ref, dalog_ref, ddsk_ref):
                r[...] = jnp.zeros_like(r)

        y = y_ref[...]
        _, gate_vjp = jax.vjp(lambda y_, z_, w_: _rms(y_ * _silu(z_), w_), y, z_ref[...].astype(F32), nw_ref[...])
        dy, dz, dnw = gate_vjp(dya_ref[...])
        dz_ref[...] = dz.astype(BF16)
        dnw_ref[...] += dnw

        dtr = dtr_ref[...]
        c = _ssd_common(dtr, dtb_ref[...], alog_ref[...], et_ref[...])
        dt, a, lane_lo, row, col = c["dt"], c["a"], c["lane_lo"], c["row"], c["col"]
        dsk = dsk_ref[...]
        p_a, p_dt, v_last = [], [], []
        da_cols = jnp.zeros((CHUNK, CHUNK), F32)
        da_rows = jnp.zeros((CHUNK, CHUNK), F32)
        for g in range(SSD_GROUPS):
            gs = slice(g * SSD_STATE, (g + 1) * SSD_STATE)
            bg, cg = b_ref[:, gs].astype(BF16), c_ref[:, gs].astype(BF16)
            bg_t, cg_t = b_ref[:, gs].T.astype(BF16), c_ref[:, gs].T.astype(BF16)
            cb, cb_t = _dot(cg, bg_t), _dot(bg, cg_t)
            dcb = jnp.zeros((CHUNK, CHUNK), F32)
            dbg = jnp.zeros((CHUNK, SSD_STATE), F32)
            dcg = jnp.zeros((CHUNK, SSD_STATE), F32)
            for pp in range(PAIRS_PER_GROUP):
                j = g * PAIRS_PER_GROUP + pp
                ps = slice(j * LANES, (j + 1) * LANES)
                x = xs_ref[:, ps]
                dtp, ecol, dsr = c["dt_x"][:, ps], c["ecol_x"][:, ps], c["dsr_x"][:, ps]
                elast = c["elast_x"][:, ps]
                xdt = x * dtp
                xb = xdt.astype(BF16)
                dskp = jnp.where(lane_lo[0:1], dsk[:, 2 * j:2 * j + 1], dsk[:, 2 * j + 1:2 * j + 2])
                dyp = dy[:, ps]
                dyb = dyp.astype(BF16)
                sp, dsn = sp_ref[0, j], ds_scr[j]
                spb, dsnb = sp.astype(BF16), dsn.astype(BF16)
                y_off = ecol * _dot(cg, spb)
                dw = (dyp * ecol).astype(BF16)
                dcg = dcg + _dot_nt(dw, spb)
                dsp = _dot(cg_t, dw) + elast * dsn
                xd = xdt * dsr
                zd = _dot(bg, dsnb) * dsr
                dbg = dbg + _dot_nt(xd.astype(BF16), dsnb)
                dxdt = zd
                zero = jnp.zeros_like(xb)
                for h, lm in ((2 * j, lane_lo), (2 * j + 1, jnp.logical_not(lane_lo))):
                    le = _head_decay(c, h)
                    dm = _dot_nt(jnp.where(lm, dyb, zero), jnp.where(lm, xb, zero))
                    dcb = dcb + dm * le
                    m = cb * le
                    m_t = (cb_t * _head_decay(c, h, transposed=True)).astype(BF16)
                    dxdt = dxdt + jnp.where(lm, _dot(m_t, dyb), 0.0)
                    q = dm * m
                    da_cols = da_cols + jnp.where(col == h, jnp.sum(q, axis=1, keepdims=True), 0.0)
                    da_rows = da_rows + jnp.where(row == h, _colsum(q), 0.0)
                ds_scr[j] = dsp
                dxs_ref[:, ps] = dxdt * dtp + dskp * dyp
                p_a.append(dyp * y_off - xdt * zd)
                p_dt.append(dxdt * x)
                v_last.append(_colsum(zd * xdt) + elast * _colsum(dsn * sp))
            dcbb = dcb.astype(BF16)
            db_ref[:, gs] = dbg + _dot_tn(dcbb, cg)
            dc_ref[:, gs] = dcg + _dot(dcbb, bg)
        e = e_ref[...]
        rows8 = jnp.concatenate([jnp.concatenate(v_last, axis=1), _colsum(dy * xs_ref[...]),
                                 jnp.zeros((6, SSD_INNER), F32)], axis=0)
        r8 = _dot_split(rows8, e)
        da = (_dot_split(jnp.concatenate(p_a, axis=1), e) + jnp.where(row == CHUNK - 1, r8[0:1], 0.0)
              + da_cols - da_rows.T)
        ddsk_ref[...] += r8[1:2]
        dadt = jnp.dot((row <= col).astype(F32), da, precision=HIGHEST, preferred_element_type=F32)
        ddt = dadt * a + _dot_split(jnp.concatenate(p_dt, axis=1), e)
        dalog_ref[...] += _colsum(dadt * dt) * a
        ddtr = ddt * _sigmoid(dtr + dtb_ref[...])
        ddtr_ref[...] = ddtr
        ddtb_ref[...] += _colsum(ddtr)

    ck = lambda n, col=0: pl.BlockSpec((CHUNK, n), lambda c: (nc - 1 - c, col))
    acc = lambda n: _full((1, n))
    return pl.pallas_call(
        body, name="ssd_bwd", grid=(nc,),
        in_specs=[ck(SSD_INNER), ck(SSD_INNER), ck(SSD_INNER), ck(SSD_INNER), ck(SSD_BC, SSD_INNER // SSD_BC),
                  ck(SSD_BC, SSD_INNER // SSD_BC + 1), ck(DT_PAD),
                  pl.BlockSpec((1, N_PAIRS, SSD_STATE, LANES), lambda c: (nc - 1 - c, 0, 0, 0)),
                  acc(DT_PAD), acc(DT_PAD), acc(DT_PAD), acc(SSD_INNER), _full((SSD_INNER, LANES)),
                  _full((LANES, SSD_INNER))],
        out_specs=[ck(SSD_INNER), ck(SSD_INNER), ck(SSD_BC), ck(SSD_BC), ck(DT_PAD),
                   acc(SSD_INNER), acc(DT_PAD), acc(DT_PAD), acc(DT_PAD)],
        out_shape=[jax.ShapeDtypeStruct((t, SSD_INNER), BF16), jax.ShapeDtypeStruct((t, SSD_INNER), F32),
                   jax.ShapeDtypeStruct((t, SSD_BC), F32), jax.ShapeDtypeStruct((t, SSD_BC), F32),
                   jax.ShapeDtypeStruct((t, DT_PAD), F32), jax.ShapeDtypeStruct((1, SSD_INNER), F32),
                   jax.ShapeDtypeStruct((1, DT_PAD), F32), jax.ShapeDtypeStruct((1, DT_PAD), F32),
                   jax.ShapeDtypeStruct((1, DT_PAD), F32)],
        scratch_shapes=[pltpu.VMEM((N_PAIRS, SSD_STATE, LANES), F32)], compiler_params=_params(1),
    )(dya, y, z, xc, xc, xc, dtr, sprev, dtb, alog, dsk, nw, e_heads, e_t)


def _sgu_act(uv, uvb, lnw, lnb):
    a = _gelu(uv + uvb)
    return a[:, :SGU_WIDTH], _layer_norm(a[:, SGU_WIDTH:], lnw, lnb)


def _sgu_weights(ws_ref):
    row = lax.broadcasted_iota(jnp.int32, (CHUNK, CHUNK), 0)
    col = lax.broadcasted_iota(jnp.int32, (CHUNK, CHUNK), 1)
    return [jnp.where(row >= col, ws_ref[g], 0.0).astype(BF16) for g in range(SGU_GROUPS)], row >= col


def _sgu_fwd(uv, uvb, lnw, lnb, ws, bs_t):
    t = uv.shape[0]

    def body(uv_ref, uvb_ref, lnw_ref, lnb_ref, ws_ref, bs_ref, o_ref):
        u, vn = _sgu_act(uv_ref[...].astype(F32), uvb_ref[...], lnw_ref[...], lnb_ref[...])
        wc, _ = _sgu_weights(ws_ref)
        bs = bs_ref[...]
        for g in range(SGU_GROUPS):
            gs = slice(g * LANES, (g + 1) * LANES)
            mixed = _dot(wc[g], vn[:, gs].astype(BF16)) + bs[:, g:g + 1]
            o_ref[:, gs] = (u[:, gs] * mixed).astype(BF16)

    return pl.pallas_call(
        body, name="sgu_fwd", grid=(t // CHUNK,),
        in_specs=[_rows(CHUNK, 2 * SGU_WIDTH), _full((1, 2 * SGU_WIDTH)), _full((1, SGU_WIDTH)), _full((1, SGU_WIDTH)),
                  _full(ws.shape), _full(bs_t.shape)],
        out_specs=_rows(CHUNK, SGU_WIDTH), out_shape=jax.ShapeDtypeStruct((t, SGU_WIDTH), BF16),
        compiler_params=_params(1),
    )(uv, uvb, lnw, lnb, ws, bs_t)


def _sgu_bwd(dyb, uv, uvb, lnw, lnb, ws, bs_t, e_groups):
    t = uv.shape[0]

    def body(dyb_ref, uv_ref, uvb_ref, lnw_ref, lnb_ref, ws_ref, bs_ref, e_ref,
             duv_ref, duvb_ref, dlnw_ref, dlnb_ref, dws_ref, dbs_ref):
        @pl.when(pl.program_id(0) == 0)
        def _():
            for r in (duvb_ref, dlnw_ref, dlnb_ref, dws_ref, dbs_ref):
                r[...] = jnp.zeros_like(r)

        (u, vn), act_vjp = jax.vjp(_sgu_act, uv_ref[...].astype(F32), uvb_ref[...], lnw_ref[...], lnb_ref[...])
        wc, causal = _sgu_weights(ws_ref)
        bs = bs_ref[...]
        dyb = dyb_ref[...]
        du, dvn, dmix = [], [], []
        for g in range(SGU_GROUPS):
            gs = slice(g * LANES, (g + 1) * LANES)
            vb = vn[:, gs].astype(BF16)
            mixed = _dot(wc[g], vb) + bs[:, g:g + 1]
            dm = dyb[:, gs] * u[:, gs]
            dmb = dm.astype(BF16)
            du.append(dyb[:, gs] * mixed)
            dvn.append(_dot_tn(wc[g], dmb))
            dws_ref[g] += jnp.where(causal, _dot_nt(dmb, vb), 0.0)
            dmix.append(dm)
        dbs_ref[...] += _dot_split(jnp.concatenate(dmix, axis=1), e_ref[...])
        duv, duvb, dlnw, dlnb = act_vjp((jnp.concatenate(du, axis=1), jnp.concatenate(dvn, axis=1)))
        duv_ref[...] = duv.astype(BF16)
        duvb_ref[...] += duvb
        dlnw_ref[...] += dlnw
        dlnb_ref[...] += dlnb

    return pl.pallas_call(
        body, name="sgu_bwd", grid=(t // CHUNK,),
        in_specs=[_rows(CHUNK, SGU_WIDTH), _rows(CHUNK, 2 * SGU_WIDTH), _full((1, 2 * SGU_WIDTH)),
                  _full((1, SGU_WIDTH)), _full((1, SGU_WIDTH)), _full(ws.shape), _full(bs_t.shape),
                  _full(e_groups.shape)],
        out_specs=[_rows(CHUNK, 2 * SGU_WIDTH), _full((1, 2 * SGU_WIDTH)), _full((1, SGU_WIDTH)),
                   _full((1, SGU_WIDTH)), _full(ws.shape), _full(bs_t.shape)],
        out_shape=[jax.ShapeDtypeStruct((t, 2 * SGU_WIDTH), BF16), jax.ShapeDtypeStruct((1, 2 * SGU_WIDTH), F32),
                   jax.ShapeDtypeStruct((1, SGU_WIDTH), F32), jax.ShapeDtypeStruct((1, SGU_WIDTH), F32),
                   jax.ShapeDtypeStruct(ws.shape, F32), jax.ShapeDtypeStruct(bs_t.shape, F32)],
        compiler_params=_params(1),
    )(dyb, uv, uvb, lnw, lnb, ws, bs_t, e_groups)


def _merge(gates, pa, pb, bg):
    s = _sigmoid(gates + bg)
    return s[:, :D_MODEL] * pa + s[:, D_MODEL:] * pb


def _merge_fwd(gates, pa, pb, bg, tm=256):
    t = gates.shape[0]

    def body(g_ref, pa_ref, pb_ref, bg_ref, o_ref):
        o_ref[...] = _merge(g_ref[...].astype(F32), pa_ref[...].astype(F32), pb_ref[...].astype(F32),
                            bg_ref[...]).astype(BF16)

    return pl.pallas_call(
        body, name="merge_fwd", grid=(t // tm,),
        in_specs=[_rows(tm, 2 * D_MODEL), _rows(tm, D_MODEL), _rows(tm, D_MODEL), _full((1, 2 * D_MODEL))],
        out_specs=_rows(tm, D_MODEL), out_shape=jax.ShapeDtypeStruct((t, D_MODEL), BF16), compiler_params=_params(1),
    )(gates, pa, pb, bg)


def _merge_bwd(dmix, gates, pa, pb, bg, tm=256):
    t = gates.shape[0]

    def body(d_ref, g_ref, pa_ref, pb_ref, bg_ref, dg_ref, dpa_ref, dpb_ref, dbg_ref):
        @pl.when(pl.program_id(0) == 0)
        def _():
            dbg_ref[...] = jnp.zeros_like(dbg_ref)

        _, vjp = jax.vjp(_merge, g_ref[...].astype(F32), pa_ref[...].astype(F32), pb_ref[...].astype(F32),
                         bg_ref[...])
        dg, dpa, dpb, dbg = vjp(d_ref[...])
        dg_ref[...] = dg.astype(BF16)
        dpa_ref[...] = dpa.astype(BF16)
        dpb_ref[...] = dpb.astype(BF16)
        dbg_ref[...] += dbg

    return pl.pallas_call(
        body, name="merge_bwd", grid=(t // tm,),
        in_specs=[_rows(tm, D_MODEL), _rows(tm, 2 * D_MODEL), _rows(tm, D_MODEL), _rows(tm, D_MODEL),
                  _full((1, 2 * D_MODEL))],
        out_specs=[_rows(tm, 2 * D_MODEL), _rows(tm, D_MODEL), _rows(tm, D_MODEL), _full((1, 2 * D_MODEL))],
        out_shape=[jax.ShapeDtypeStruct((t, 2 * D_MODEL), BF16), jax.ShapeDtypeStruct((t, D_MODEL), BF16),
                   jax.ShapeDtypeStruct((t, D_MODEL), BF16), jax.ShapeDtypeStruct((1, 2 * D_MODEL), F32)],
        compiler_params=_params(1),
    )(dmix, gates, pa, pb, bg)


def _conv_f_fwd(up, cw, cb, tm=128):
    t, c = up.shape

    def body(x_ref, h_ref, w_ref, b_ref, o_ref, y_ref):
        halo = jnp.where(pl.program_id(0) > 0, h_ref[...].astype(F32)[8:], 0.0)
        y = _causal_conv(x_ref[...].astype(F32), halo, w_ref[...], b_ref[...])
        y_ref[...] = y.astype(BF16)
        o_ref[...] = (_silu(y[:, :D_FF]) * y[:, D_FF:]).astype(BF16)

    return pl.pallas_call(
        body, name="conv_f_fwd", grid=(t // tm,),
        in_specs=[_rows(tm, c), _halo(tm, c, rows=16), _full(cw.shape), _full((1, c))],
        out_specs=[_rows(tm, D_FF), _rows(tm, c)],
        out_shape=[jax.ShapeDtypeStruct((t, D_FF), BF16), jax.ShapeDtypeStruct((t, c), BF16)],
        compiler_params=_params(1),
    )(up, up, cw, cb)


def _conv_f_bwd(dact, y, up, cw, tm=128):
    t, c = up.shape
    nt = t // tm

    def body(d_ref, y_ref, x_ref, w_ref, dx_ref, dw_ref, db_ref, nxt_scr):
        @pl.when(pl.program_id(0) == 0)
        def _():
            nxt_scr[...] = jnp.zeros_like(nxt_scr)
            dw_ref[...] = jnp.zeros_like(dw_ref)
            db_ref[...] = jnp.zeros_like(db_ref)

        a, v = y_ref[:, :D_FF].astype(F32), y_ref[:, D_FF:].astype(F32)
        d = d_ref[...].astype(F32)
        dy = jnp.concatenate([d * v * _dsilu(a), d * _silu(a)], axis=1)
        dx, dw = _causal_conv_bwd(dy, nxt_scr[...], x_ref[...].astype(F32), w_ref[...])
        dx_ref[...] = dx.astype(BF16)
        nxt_scr[...] = dy[:8]
        dw_ref[...] += dw
        db_ref[...] += _colsum(dy)

    return pl.pallas_call(
        body, name="conv_f_bwd", grid=(nt,),
        in_specs=[_rows(tm, D_FF, nt, True), _rows(tm, c, nt, True), _rows(tm, c, nt, True), _full(cw.shape)],
        out_specs=[_rows(tm, c, nt, True), _full(cw.shape), _full((1, c))],
        out_shape=[jax.ShapeDtypeStruct((t, c), BF16), jax.ShapeDtypeStruct(cw.shape, F32),
                   jax.ShapeDtypeStruct((1, c), F32)],
        scratch_shapes=[pltpu.VMEM((8, c), F32)], compiler_params=_params(1),
    )(dact, y, up, cw)


def _conv_a_bwd(dxs, db, dc, y, xbc, cw, tm=256):
    t, c = xbc.shape
    nt = t // tm

    def body(dxs_ref, db_ref, dc_ref, y_ref, x_ref, w_ref, dx_ref, dw_ref, dbias_ref, nxt_scr):
        @pl.when(pl.program_id(0) == 0)
        def _():
            nxt_scr[...] = jnp.zeros_like(nxt_scr)
            dw_ref[...] = jnp.zeros_like(dw_ref)
            dbias_ref[...] = jnp.zeros_like(dbias_ref)

        dy = jnp.concatenate([dxs_ref[...], db_ref[...], dc_ref[...]], axis=1) * _dsilu(y_ref[...].astype(F32))
        dx, dw = _causal_conv_bwd(dy, nxt_scr[...], x_ref[...].astype(F32), w_ref[...])
        dx_ref[...] = dx.astype(BF16)
        nxt_scr[...] = dy[:8]
        dw_ref[...] += dw
        dbias_ref[...] += _colsum(dy)

    return pl.pallas_call(
        body, name="conv_a_bwd", grid=(nt,),
        in_specs=[_rows(tm, SSD_INNER, nt, True), _rows(tm, SSD_BC, nt, True), _rows(tm, SSD_BC, nt, True),
                  _rows(tm, c, nt, True), _rows(tm, c, nt, True), _full(cw.shape)],
        out_specs=[_rows(tm, c, nt, True), _full(cw.shape), _full((1, c))],
        out_shape=[jax.ShapeDtypeStruct((t, c), BF16), jax.ShapeDtypeStruct(cw.shape, F32),
                   jax.ShapeDtypeStruct((1, c), F32)],
        scratch_shapes=[pltpu.VMEM((8, c), F32)], compiler_params=_params(1),
    )(dxs, db, dc, y, xbc, cw)


def _pad_lanes(v, n=DT_PAD):
    return jnp.pad(v, ((0, 0), (0, n - v.shape[1])))


def _local_step(x, target, w, p, after=None, late_weights=None, on_grad=None, on_small=None):
    dtb, alog, dsk = _pad_lanes(p["dt_bias"]), _pad_lanes(p["a_log"]), _pad_lanes(p["d_skip"])
    bs_t = _pad_lanes(p["b_spatial"].T)
    e_heads = (jnp.arange(SSD_INNER)[:, None] // SSD_HEAD_DIM == jnp.arange(LANES)[None, :]).astype(BF16)
    e_heads_t = (jnp.arange(LANES)[:, None] == jnp.arange(SSD_INNER)[None, :] // SSD_HEAD_DIM).astype(BF16)
    e_groups = (jnp.arange(SGU_WIDTH)[:, None] // LANES == jnp.arange(LANES)[None, :]).astype(BF16)

    n1 = _norm_fwd(x, p["norm1_w"], "norm1_fwd", after=after)
    z = _mm(n1, w["z"], "nt", "proj_z", out_dtype=BF16)
    xbc = _mm(n1, w["xbc"], "nt", "proj_xbc", out_dtype=BF16)
    dtr = _mm(n1, w["dt"], "nt", "proj_dt")
    uv = _mm(n1, w["uv"], "nt", "proj_uv", out_dtype=BF16)
    gates = _mm(n1, w["gates"], "nt", "proj_gates", out_dtype=BF16)
    xc, conv_a_out = _conv_a_fwd(xbc, w["conv_a"], p["conv_a_b"])
    y, ya, sprev = _ssd_fwd(xc, dtr, z, dtb, alog, dsk, p["ssd_norm_w"], e_heads_t)
    yb = _sgu_fwd(uv, p["uv_b"], p["v_ln_w"], p["v_ln_b"], p["w_spatial"], bs_t)
    if late_weights is not None:
        w = {**w, **late_weights(ya, yb)}
    pa = _mm(ya, w["branch_a"], "nn", "branch_a", out_dtype=BF16)
    pb = _mm(yb, w["branch_b"], "nn", "branch_b", out_dtype=BF16)
    mix = _merge_fwd(gates, pa, pb, p["b_gate"])
    wide = [(D_MODEL, F32), (D_MODEL, BF16)]
    h1, n2 = _mm_rows(mix, w["out"], "nn", "out_proj", _residual_norm, rows=[x], fulls=[p["norm2_w"]], row_outs=wide)
    up = _mm(n2, w["up"], "nt", "up_proj", out_dtype=BF16)
    act, conv_f_out = _conv_f_fwd(up, w["conv_f"], p["conv_f_b"])
    dh2, dh2b, loss, g_final = _mm_rows(
        act, w["down"], "nn", "down_proj", _loss_and_grad, rows=[h1, target], fulls=[p["final_norm_w"]],
        row_outs=wide, acc_outs=[(8, LANES), (1, D_MODEL)])

    on_grad = on_grad or (lambda name, grads: None)
    g = {"final_norm_w": g_final}
    g["down"] = _wgrad(act, dh2b, "down_wgrad")
    tok = on_grad("w_down", g)
    dact = _mm(dh2b, w["down"], "nt", "down_dgrad", out_dtype=BF16, after=tok)
    dup, g["conv_f"], g["conv_f_b"] = _conv_f_bwd(dact, conv_f_out, up, w["conv_f"])
    g["up"] = _wgrad(dup, n2, "up_wgrad")
    tok = on_grad("w_up", g)
    dh1, dh1b, g["norm2_w"] = _mm_rows(
        dup, w["up"], "nn", "up_dgrad", _norm_backward, rows=[h1, dh2], fulls=[p["norm2_w"]], row_outs=wide,
        acc_outs=[(1, D_MODEL)], after=tok)
    g["out"] = _wgrad(mix, dh1b, "out_wgrad")
    tok = on_grad("w_out", g)
    dmix = _mm(dh1b, w["out"], "nt", "out_dgrad", after=tok)
    dgates, dpa, dpb, g["b_gate"] = _merge_bwd(dmix, gates, pa, pb, p["b_gate"])
    g["branch_a"] = _wgrad(ya, dpa, "branch_a_wgrad")
    g["branch_b"] = _wgrad(yb, dpb, "branch_b_wgrad")
    tok = on_grad("w_branch", g)
    dya = _mm(dpa, w["branch_a"], "nt", "branch_a_dgrad", after=tok)
    dyb = _mm(dpb, w["branch_b"], "nt", "branch_b_dgrad", after=tok)
    duv, g["uv_b"], g["v_ln_w"], g["v_ln_b"], g["w_spatial"], dbs_t = _sgu_bwd(
        dyb, uv, p["uv_b"], p["v_ln_w"], p["v_ln_b"], p["w_spatial"], bs_t, e_groups)
    g["b_spatial"] = dbs_t[:, :SGU_GROUPS].T
    dz, dxs, db, dc, ddtr, g["ssd_norm_w"], ddtb, dalog, ddsk = _ssd_bwd(
        dya, y, z, xc, dtr, sprev, dtb, alog, dsk, p["ssd_norm_w"], e_heads, e_heads_t)
    g["dt_bias"], g["a_log"], g["d_skip"] = ddtb, dalog, ddsk
    dxbc, g["conv_a"], g["conv_a_b"] = _conv_a_bwd(dxs, db, dc, conv_a_out, xbc, w["conv_a"])
    tok = on_small(g, loss) if on_small else None
    ddtrb = ddtr.astype(BF16)
    for name, d in (("z", dz), ("xbc", dxbc), ("dt", ddtrb), ("uv", duv), ("gates", dgates)):
        g[name] = _wgrad(d, n1, name + "_wgrad", after=tok)
    tok = on_grad("w_in", g)
    dn1 = _mm(dz, w["z"], "nn", "z_dgrad", after=tok)
    dn1 = _mm(dxbc, w["xbc"], "nn", "xbc_dgrad", acc=dn1)
    dn1 = _mm(ddtrb, w["dt"], "nn", "dt_dgrad", acc=dn1)
    dn1 = _mm(duv, w["uv"], "nn", "uv_dgrad", acc=dn1)
    gx, g["norm1_w"] = _mm_rows(
        dgates, w["gates"], "nn", "gates_dgrad",
        lambda r, so_far, h, dres, w_: tuple(t[:1] for t in _norm_backward(r + so_far, h, dres, w_)),
        rows=[dn1, x, dh1], fulls=[p["norm1_w"]], row_outs=wide[:1], acc_outs=[(1, D_MODEL)])
    return loss, gx, g


def _place():
    return lax.axis_index("x"), lax.axis_index("y"), lax.axis_index("c")


def _other_chips(x, y):
    return [(1 - x, y), (x, 1 - y), (1 - x, 1 - y)]


def _all_gather(shards, name):
    n = len(shards)

    def body(*refs):
        ins, outs = refs[:n], refs[n:2 * n]
        send_sems, recv_sems, local_sems = refs[2 * n:]
        x, y, c = _place()
        me, sibling = (x, y, c), (x, y, 1 - c)
        chips = _other_chips(x, y)

        def copy(a, k, block, to, src=None):
            slot = outs[a].at[4 * block[0] + 2 * block[1] + block[2]]
            return pltpu.make_async_remote_copy(
                src_ref=slot if src is None else src, dst_ref=slot, send_sem=send_sems.at[7 * a + k],
                recv_sem=recv_sems.at[7 * a + k], device_id=to, device_id_type=MESH)

        started = []
        for a in range(n):
            mine = pltpu.make_async_copy(ins[a], outs[a].at[4 * x + 2 * y + c], local_sems.at[a])
            mine.start()
            started.append(mine)
        sends = []
        for a in range(n):
            sends.append(copy(a, 0, me, sibling, src=ins[a]))
            sends += [copy(a, 1 + j, me, (*chip, c), src=ins[a]) for j, chip in enumerate(chips)]
        for cp in sends:
            cp.start()
        for a in range(n):
            for j, chip in enumerate(chips):
                copy(a, 1 + j, (*chip, c), me).wait_recv()
                fwd = copy(a, 4 + j, (*chip, c), sibling)
                fwd.start()
                sends.append(fwd)
        for a in range(n):
            copy(a, 0, sibling, me).wait_recv()
            for j, chip in enumerate(chips):
                copy(a, 4 + j, (*chip, 1 - c), me).wait_recv()
        for cp in sends:
            cp.wait_send()
        for mine in started:
            mine.wait()

    any_spec = pl.BlockSpec(memory_space=pl.ANY)
    return pl.pallas_call(
        body, name=name, in_specs=[any_spec] * n, out_specs=[any_spec] * n,
        out_shape=[jax.ShapeDtypeStruct((N_DEV, *s.shape), s.dtype) for s in shards],
        scratch_shapes=[pltpu.SemaphoreType.DMA((7 * n,)), pltpu.SemaphoreType.DMA((7 * n,)),
                        pltpu.SemaphoreType.DMA((n,))],
    )(*shards)


HBM_SPEC = pl.BlockSpec(memory_space=pltpu.HBM)
SEM_SPEC = pl.BlockSpec(memory_space=pltpu.SEMAPHORE)
ANY_SPEC = pl.BlockSpec(memory_space=pl.ANY)
DATAFLOW = pltpu.SideEffectType.DATAFLOW_SIDE_EFFECTING
N_PEERS = N_DEV - 1


def _peers(x, y, c):
    out = []
    for r in range(1, N_DEV):
        fx, fy, fc = r >> 2 & 1, r >> 1 & 1, r & 1
        out.append(((1 - x) if fx else x, (1 - y) if fy else y, (1 - c) if fc else c))
    return out


def _gather_copies(srcs, lands, send_sems, recv_sems, sending, scatter=False):
    x, y, c = _place()
    copies = []
    for a, (src, land) in enumerate(zip(srcs, lands)):
        for j, (px, py, pc) in enumerate(_peers(x, y, c)):
            mine, theirs = 4 * x + 2 * y + c, 4 * px + 2 * py + pc
            block = src.at[theirs if sending else 0] if scatter else src
            copies.append(pltpu.make_async_remote_copy(
                src_ref=block, dst_ref=land.at[mine if sending else theirs], send_sem=send_sems.at[N_PEERS * a + j],
                recv_sem=recv_sems.at[N_PEERS * a + j], device_id=(px, py, pc), device_id_type=MESH))
    return copies


def _gather_start(shards, after, name, scatter=False):
    n = len(shards)
    after = [] if after is None else [after]

    def body(*refs):
        srcs, lands = refs[:n], refs[n:2 * n]
        send_sems, recv_sems = refs[2 * n + len(after):2 * n + len(after) + 2]
        token = refs[-1]
        for cp in _gather_copies(srcs, lands, send_sems, recv_sems, sending=True, scatter=scatter):
            cp.start()
        token[...] = jnp.zeros_like(token)

    lands = [lax.empty(s.shape if scatter else (N_DEV, *s.shape), s.dtype) for s in shards]
    hbm = lambda a: pltpu.with_memory_space_constraint(a, pltpu.HBM)
    out = pl.pallas_call(
        body, name=name,
        out_shape=(pltpu.SemaphoreType.DMA((N_PEERS * n,)), pltpu.SemaphoreType.DMA((N_PEERS * n,)),
                   *[pltpu.HBM(a.shape, a.dtype) for a in (*shards, *lands)], jax.ShapeDtypeStruct((8, LANES), F32)),
        in_specs=[HBM_SPEC] * (2 * n) + [ANY_SPEC] * len(after),
        out_specs=(SEM_SPEC, SEM_SPEC, *[HBM_SPEC] * (2 * n), pl.BlockSpec(memory_space=pltpu.VMEM)),
        input_output_aliases={i: 2 + i for i in range(2 * n)},
        compiler_params=pltpu.CompilerParams(has_side_effects=DATAFLOW),
    )(*[hbm(a) for a in (*shards, *lands)], *after)
    return out[0], out[1], out[2:2 + n], out[2 + n:2 + 2 * n], out[-1]


def _gather_wait(send_sems, recv_sems, shards, lands, after, name, scatter=False):
    n = len(shards)
    after = tuple(after)

    def body(*refs):
        srcs, lands_ = refs[:n], refs[n:2 * n]
        send, recv = refs[2 * n:2 * n + 2]
        for cp in _gather_copies(srcs, lands_, send, recv, sending=False, scatter=scatter):
            cp.wait_send()
            cp.wait_recv()

    out = pl.pallas_call(
        body, name=name, out_shape=tuple(pltpu.HBM(a.shape, a.dtype) for a in (*shards, *lands)),
        in_specs=[HBM_SPEC] * (2 * n) + [SEM_SPEC, SEM_SPEC] + [ANY_SPEC] * len(after),
        out_specs=tuple([HBM_SPEC] * (2 * n)), input_output_aliases={i: i for i in range(2 * n)},
        compiler_params=pltpu.CompilerParams(has_side_effects=DATAFLOW),
    )(*shards, *lands, send_sems, recv_sems, *after)
    return out[:n], out[n:]


def _adamw(w, g, m, v):
    m = ADAM_B1 * m + (1.0 - ADAM_B1) * g
    v = ADAM_B2 * v + (1.0 - ADAM_B2) * jnp.square(g)
    m_hat = m / (1.0 - ADAM_B1 ** ADAM_STEP)
    v_hat = v / (1.0 - ADAM_B2 ** ADAM_STEP)
    return -ADAM_LR * (m_hat / (jnp.sqrt(v_hat) + ADAM_EPS) + ADAM_WD * w), m, v


def _sum8_adamw(part, got, place, w, m, v, name, tr=256):
    _, r, c = part.shape
    if w.ndim == 3:
        tr, tc = r, 2 * LANES
        blk = pl.BlockSpec((tr, 1, tc), lambda i, j, pr: (i, 0, j))
    else:
        tr, tc = _tile2d(r, c, tr)
        blk = pl.BlockSpec((tr, tc), lambda i, j, pr: (i, j))

    def body(place_ref, own_ref, got_ref, w_ref, m_ref, v_ref, g_ref, d_ref, nm_ref, nv_ref):
        dev = 2 * place_ref[1] + place_ref[0]
        g = jnp.zeros((tr, tc), F32)
        for d in range(N_DEV):
            g = g + jnp.where(dev == d, own_ref[0], got_ref[d]).astype(F32)
        two_d = lambda ref: ref[...].reshape(tr, tc)
        delta, nm, nv = _adamw(two_d(w_ref), g, two_d(m_ref), two_d(v_ref))
        for ref, val in ((g_ref, g), (d_ref, delta), (nm_ref, nm), (nv_ref, nv)):
            ref[...] = val.reshape(ref.shape)

    grid_spec = pltpu.PrefetchScalarGridSpec(
        num_scalar_prefetch=1, grid=(r // tr, c // tc),
        in_specs=[pl.BlockSpec((1, tr, tc), lambda i, j, pr: (2 * pr[1] + pr[0], i, j)),
                  pl.BlockSpec((N_DEV, tr, tc), lambda i, j, pr: (0, i, j)), blk, blk, blk],
        out_specs=[blk] * 4)
    return pl.pallas_call(
        body, name=name, grid_spec=grid_spec, out_shape=[jax.ShapeDtypeStruct(w.shape, F32)] * 4,
        compiler_params=_params(2),
    )(place, part, got, w, m, v)


VECTORS = ["norm1_w", "b_gate", "conv_a_b", "dt_bias", "a_log", "d_skip", "ssd_norm_w", "uv_b", "v_ln_w", "v_ln_b",
           "norm2_w", "conv_f_b", "final_norm_w"]
SMALL_ORDER = VECTORS + ["w_spatial", "b_spatial", "conv_a_w", "conv_f_w"]


ROW_VECTORS = VECTORS[1:]


def _small_adamw(gathered, w, m, v):
    sizes = {n: w[n].shape[1] for n in ROW_VECTORS}
    offs, off = {}, 0
    for n in ROW_VECTORS:
        offs[n] = off
        off += -(-sizes[n] // LANES) * LANES
    loss_off = off
    k = len(SMALL_ORDER)
    n_g = len(gathered)

    def body(*refs):
        row_ref, ws_ref, bs_ref, ca_ref, cf_ref, n1_ref = refs[:n_g]
        w_refs, m_refs, v_refs = (dict(zip(SMALL_ORDER, refs[n_g + i * k:n_g + (i + 1) * k])) for i in range(3))
        outs = refs[n_g + 3 * k:]
        x, y, c = _place()
        dev = 4 * x + 2 * y + c

        def total(ref):
            s = ref[0]
            for d in range(1, N_DEV):
                s = s + ref[d]
            return s

        row = total(row_ref)
        grads = {n: row[:, offs[n]:offs[n] + sizes[n]] for n in ROW_VECTORS}
        grads["norm1_w"], grads["w_spatial"], grads["b_spatial"] = total(n1_ref), total(ws_ref), total(bs_ref)
        for n, ref in (("conv_a_w", ca_ref), ("conv_f_w", cf_ref)):
            whole, cols = total(ref), w_refs[n].shape[1]
            mine = whole[:, :cols]
            for d in range(1, N_DEV):
                mine = jnp.where(dev == d, whole[:, d * cols:(d + 1) * cols], mine)
            grads[n] = mine
        for i, n in enumerate(SMALL_ORDER):
            outs[4 * i][...] = grads[n]
            outs[4 * i + 1][...], outs[4 * i + 2][...], outs[4 * i + 3][...] = _adamw(
                w_refs[n][...], grads[n], m_refs[n][...], v_refs[n][...])
        outs[4 * k][...] = row[:, loss_off:loss_off + LANES]

    out = pl.pallas_call(
        body, name="adamw_small",
        out_shape=[jax.ShapeDtypeStruct(w[n].shape, F32) for n in SMALL_ORDER for _ in range(4)]
        + [jax.ShapeDtypeStruct((1, LANES), F32)],
        compiler_params=_params(0),
    )(*gathered, *[t[n] for t in (w, m, v) for n in SMALL_ORDER])
    return [dict(zip(SMALL_ORDER, out[j:4 * k:4])) for j in range(4)] + [out[4 * k]]


SMALL = ["norm1_w", "b_gate", "conv_a_b", "dt_bias", "a_log", "d_skip", "ssd_norm_w", "uv_b", "v_ln_w", "v_ln_b",
         "w_spatial", "b_spatial", "norm2_w", "conv_f_b", "final_norm_w"]
BIG = ["w_in", "w_branch", "w_out", "w_up", "w_down"]
TRANSPOSED = ("w_in", "w_up")
WEIGHTS = ["norm1_w", "w_in", "b_gate", "conv_a_w", "conv_a_b", "dt_bias", "a_log", "d_skip", "ssd_norm_w", "uv_b",
           "v_ln_w", "v_ln_b", "w_spatial", "b_spatial", "w_branch", "w_out", "norm2_w", "w_up", "conv_f_w",
           "conv_f_b", "w_down", "final_norm_w"]
IN_SPLITS = [("z", 0, 2048), ("xbc", 2048, 5120), ("dt", 5120, 5152), ("uv", 5152, 7200), ("gates", 7200, 9248)]


def _columns_from_devices(a):
    return a.transpose(1, 0, 2).reshape(a.shape[1], -1)


def kernel(x, norm1_w, w_in, b_gate, conv_a_w, conv_a_b, dt_bias, a_log, d_skip, ssd_norm_w, uv_b, v_ln_w, v_ln_b, w_spatial, b_spatial, w_branch, w_out, norm2_w, w_up, conv_f_w, conv_f_b, w_down, final_norm_w, loss_target, m_norm1_w, m_w_in, m_b_gate, m_conv_a_w, m_conv_a_b, m_dt_bias, m_a_log, m_d_skip, m_ssd_norm_w, m_uv_b, m_v_ln_w, m_v_ln_b, m_w_spatial, m_b_spatial, m_w_branch, m_w_out, m_norm2_w, m_w_up, m_conv_f_w, m_conv_f_b, m_w_down, m_final_norm_w, v_norm1_w, v_w_in, v_b_gate, v_conv_a_w, v_conv_a_b, v_dt_bias, v_a_log, v_d_skip, v_ssd_norm_w, v_uv_b, v_v_ln_w, v_v_ln_b, v_w_spatial, v_b_spatial, v_w_branch, v_w_out, v_norm2_w, v_w_up, v_conv_f_w, v_conv_f_b, v_w_down, v_final_norm_w):
    args = dict(locals())
    wts = {n: args[n] for n in WEIGHTS}
    mom = {n: args["m_" + n] for n in WEIGHTS}
    var = {n: args["v_" + n] for n in WEIGHTS}
    cx, cy, cc = _place()
    dev = 4 * cx + 2 * cy + cc
    place = jnp.stack([cc, 2 * cx + cy]).astype(jnp.int32)

    def shard2d(n, a):
        return a[0].T if n in TRANSPOSED else a[0]

    def unshard(n, b):
        return (b.T if n in TRANSPOSED else b)[None]

    g_in, g_conv_a, g_conv_f = _all_gather(
        [shard2d("w_in", w_in).astype(BF16), conv_a_w[0], conv_f_w[0]], "gather_w_in")
    late = [shard2d(n, wts[n]).astype(BF16) for n in BIG[1:]]
    send_sems, recv_sems, late, lands, token = _gather_start(late, g_in, "gather_late_start")
    w_in_rows = g_in.reshape(-1, D_MODEL)
    w = {name: w_in_rows[lo:hi] for name, lo, hi in IN_SPLITS}
    w["dt"] = jnp.pad(w["dt"], ((0, DT_PAD - SSD_HEADS), (0, 0)))
    w["conv_a"] = _columns_from_devices(g_conv_a)
    w["conv_f"] = _columns_from_devices(g_conv_f)

    def late_weights(*after):
        mine, got = _gather_wait(send_sems, recv_sems, late, lands, after, "gather_late_wait")
        g_branch, g_out, g_up, g_down = [lax.dynamic_update_index_in_dim(land, own, dev, 0).reshape(-1, D_MODEL)
                                         for land, own in zip(got, mine)]
        return {"branch_a": g_branch[:SSD_INNER], "branch_b": g_branch[SSD_INNER:], "out": g_out, "up": g_up,
                "down": g_down}

    in_flight = {}

    def on_grad(n, g):
        part = {"w_in": lambda: jnp.concatenate([g[name][:hi - lo] for name, lo, hi in IN_SPLITS], axis=0),
                "w_branch": lambda: jnp.concatenate([g["branch_a"], g["branch_b"]], axis=0),
                "w_out": lambda: g["out"], "w_up": lambda: g["up"], "w_down": lambda: g["down"]}[n]()
        part = part.reshape(N_DEV, -1, D_MODEL)
        send, recv, (part,), (land,), tok = _gather_start([part], None, f"to_owners_start_{n}", scatter=True)
        in_flight[n] = (part, send, recv, land)
        return tok

    p = {n: wts[n][0] if wts[n].ndim > 2 else wts[n].reshape(1, -1) for n in SMALL}
    small_flight = []

    def on_small(g, loss):
        arrays = [jnp.concatenate([g[n] for n in ROW_VECTORS] + [loss[:1]], axis=1), g["w_spatial"], g["b_spatial"],
                  g["conv_a"], g["conv_f"]]
        *flight, tok = _gather_start(arrays, g["conv_a"], "gather_small_start")
        small_flight.append(flight)
        return tok

    loss, gx, g = _local_step(x[0], loss_target[0], w, p, after=token, late_weights=late_weights, on_grad=on_grad,
                              on_small=on_small)
    *flight, _ = _gather_start([g["norm1_w"]], gx, "gather_norm1_start")
    small_flight.append(flight)

    grads, delta, new_m, new_v = {}, {}, {}, {}

    def big_adamw(n, after):
        view = (lambda a: a.transpose(2, 0, 1)) if n == "w_in" else (lambda a: shard2d(n, a))
        back = (lambda b: b.transpose(1, 2, 0)) if n == "w_in" else (lambda b: unshard(n, b))
        part, send, recv, land = in_flight[n]
        (part,), (got,) = _gather_wait(send, recv, [part], [land], [after], f"to_owners_wait_{n}", scatter=True)
        out = _sum8_adamw(part, got, place, *[view(t[n]) for t in (wts, mom, var)], f"adamw_{n}")
        grads[n], delta[n], new_m[n], new_v[n] = [back(o) for o in out]
        return out[1]

    after = gx
    for n in ("w_down", "w_up", "w_out", "w_branch"):
        after = big_adamw(n, after)
    gathered = []
    for (send, recv, mine, land), name in zip(small_flight, ("gather_small_wait", "gather_norm1_wait")):
        mine, got = _gather_wait(send, recv, mine, land, [after], name)
        gathered += [lax.dynamic_update_index_in_dim(full, own, dev, 0) for full, own in zip(got, mine)]
    small = [{n: t[n][0] if t[n].ndim > 2 else t[n].reshape(1, -1) for n in SMALL_ORDER} for t in (wts, mom, var)]
    *outs, loss = _small_adamw(gathered, *small)
    for tgt, out in zip((grads, delta, new_m, new_v), outs):
        tgt.update({n: out[n].reshape(wts[n].shape) for n in SMALL_ORDER})
    big_adamw("w_in", loss)
    loss = loss[0, 0]

    return (loss, gx[None], *[grads[n] for n in WEIGHTS], *[delta[n] for n in WEIGHTS],
            *[new_m[n] for n in WEIGHTS], *[new_v[n] for n in WEIGHTS])
```

```python
import functools

import jax
import jax.numpy as jnp
from jax import lax
from jax.experimental import pallas as pl
from jax.experimental.pallas import tpu as pltpu

F32, BF16 = jnp.float32, jnp.bfloat16
HIGHEST = lax.Precision.HIGHEST

D_MODEL = 1024
SSD_INNER = 2048
SSD_HEAD_DIM = 64
SSD_HEADS = 32
SSD_GROUPS = 4
SSD_STATE = 128
SSD_BC = SSD_GROUPS * SSD_STATE
SSD_XBC = SSD_INNER + 2 * SSD_BC
SSD_CONV = 4
CHUNK = 128
N_PAIRS = SSD_HEADS // 2
PAIRS_PER_GROUP = N_PAIRS // SSD_GROUPS
SGU_WIDTH = 1024
SGU_GROUPS = 8
D_FF = 2816
FFN_CONV = 3
NORM_EPS = 1e-6
LN_EPS = 1e-5
LANES = 128
DT_PAD = LANES

ADAM_LR, ADAM_B1, ADAM_B2, ADAM_EPS, ADAM_WD, ADAM_STEP = 0.001, 0.9, 0.999, 1e-08, 0.01, 10

N_DEV = 8
VMEM_LIMIT = 56 * 1024 * 1024
MESH = pl.DeviceIdType.MESH


def _params(n_grid, **kw):
    sem = dict(dimension_semantics=("arbitrary",) * n_grid) if n_grid else {}
    return pltpu.CompilerParams(vmem_limit_bytes=VMEM_LIMIT, **sem, **kw)


def _tile(n, pref):
    t = (min(pref, n) // LANES) * LANES
    while n % t:
        t -= LANES
    return t


def _row_tile(r, pref):
    for t in range(min(pref, r) // 16 * 16, 0, -16):
        if r % t == 0:
            return t
    return r


def _tile2d(r, c, rows):
    if r % 16 == 0:
        return _row_tile(r, rows), c
    return r, _tile(c, 2 * LANES)


def _rows(tm, n, nt=None, rev=False):
    if rev:
        return pl.BlockSpec((tm, n), lambda i: (nt - 1 - i, 0))
    return pl.BlockSpec((tm, n), lambda i: (i, 0))


def _halo(tm, n, rows=8):
    per = tm // rows
    return pl.BlockSpec((rows, n), lambda i: (jnp.maximum(i * per - 1, 0), 0))


def _full(shape):
    nd = len(shape)
    return pl.BlockSpec(shape, lambda *_: (0,) * nd)


def _rms(x, w, eps=NORM_EPS):
    return x * lax.rsqrt(jnp.mean(x * x, axis=-1, keepdims=True) + eps) * w


def _layer_norm(x, w, b):
    mu = jnp.mean(x, axis=-1, keepdims=True)
    var = jnp.mean(jnp.square(x - mu), axis=-1, keepdims=True)
    return (x - mu) * lax.rsqrt(var + LN_EPS) * w + b


def _sigmoid(x):
    return 1.0 / (1.0 + jnp.exp(-x))


def _silu(x):
    return x * _sigmoid(x)


def _dsilu(x):
    s = _sigmoid(x)
    return s * (1.0 + x * (1.0 - s))


def _silu_and_grad(x):
    s = _sigmoid(x)
    return x * s, s * (1.0 + x * (1.0 - s))


def _softplus(x):
    return jnp.maximum(x, 0.0) + jnp.log(1.0 + jnp.exp(-jnp.abs(x)))


def _gelu(x):
    return jax.nn.gelu(x)


def _dot(a, b):
    return jnp.dot(a, b, preferred_element_type=F32)


def _dot_nt(a, b):
    return lax.dot_general(a, b, (((1,), (1,)), ((), ())), preferred_element_type=F32)


def _dot_tn(a, b):
    return lax.dot_general(a, b, (((0,), (0,)), ((), ())), preferred_element_type=F32)


def _dot_split(p, e):
    hi = p.astype(BF16)
    lo = (p - hi.astype(F32)).astype(BF16)
    return _dot(hi, e) + _dot(lo, e)


def _colsum(x):
    return jnp.sum(x, axis=0, keepdims=True)


def _shift_down(x, halo, j):
    xs = pltpu.roll(x, j, 0)
    hs = pltpu.roll(halo, j, 0)
    r8 = lax.broadcasted_iota(jnp.int32, hs.shape, 0)
    return jnp.concatenate([jnp.where(r8 < j, hs, xs[:8]), xs[8:]], axis=0)


def _shift_up(x, nxt, j):
    n = x.shape[0]
    xs = pltpu.roll(x, n - j, 0)
    ns = pltpu.roll(nxt, 8 - j, 0)
    r8 = lax.broadcasted_iota(jnp.int32, ns.shape, 0)
    return jnp.concatenate([xs[:n - 8], jnp.where(r8 >= 8 - j, ns, xs[n - 8:])], axis=0)


def _causal_conv(x, halo, w, b):
    k = w.shape[0]
    y = b + w[k - 1:k, :] * x
    for j in range(1, k):
        y = y + w[k - 1 - j:k - j, :] * _shift_down(x, halo, j)
    return y


def _causal_conv_bwd(dy, nxt, x, w):
    k = w.shape[0]
    dx = w[k - 1:k, :] * dy
    dw = [_colsum(dy * x)]
    for j in range(1, k):
        dyj = _shift_up(dy, nxt, j)
        dx = dx + w[k - 1 - j:k - j, :] * dyj
        dw.append(_colsum(dyj * x))
    return dx, jnp.concatenate(dw[::-1], axis=0)


MM_TILE_PREF = 1408
MM_VMEM_BUDGET = 40 * 1024 * 1024


def _mm_tiles(m, n, k, out_bytes):
    tm, tn = _tile(m, MM_TILE_PREF), _tile(n, MM_TILE_PREF)
    need = lambda tm, tn: 2 * (2 * k * (tm + tn) + out_bytes * tm * tn)
    while need(tm, tn) > MM_VMEM_BUDGET:
        if tn >= tm and tn > LANES:
            tn = _tile(n, tn - LANES)
        else:
            tm = _tile(m, tm - LANES)
    return tm, tn


def _mm(a, b, dims, name, acc=None, out_dtype=F32, after=None):
    if dims == "tn":
        k, m = a.shape
    else:
        m, k = a.shape
    n = b.shape[0] if dims == "nt" else b.shape[1]
    tm, tn = _mm_tiles(m, n, k, 4 * (2 if acc is not None else 1))
    a_spec = pl.BlockSpec((k, tm), lambda j, i: (0, i)) if dims == "tn" else pl.BlockSpec((tm, k), lambda j, i: (i, 0))
    b_spec = pl.BlockSpec((tn, k), lambda j, i: (j, 0)) if dims == "nt" else pl.BlockSpec((k, tn), lambda j, i: (0, j))
    o_spec = pl.BlockSpec((tm, tn), lambda j, i: (i, j))
    dot = {"nn": _dot, "nt": _dot_nt, "tn": _dot_tn}[dims]

    def body(a_ref, b_ref, *rest):
        r = dot(a_ref[...], b_ref[...])
        if acc is not None:
            r = r + rest[0][...]
        rest[-1][...] = r.astype(out_dtype)

    ins, specs = [a, b], [a_spec, b_spec]
    if acc is not None:
        ins.append(acc)
        specs.append(o_spec)
    if after is not None:
        ins.append(after)
        specs.append(pl.BlockSpec(memory_space=pl.ANY))
    return pl.pallas_call(
        body, name=name, grid=(n // tn, m // tm), in_specs=specs, out_specs=o_spec,
        out_shape=jax.ShapeDtypeStruct((m, n), out_dtype), compiler_params=_params(2),
    )(*ins)


def _mm_rows(a, b, dims, name, fn, rows=(), fulls=(), row_outs=(), acc_outs=(), after=None):
    m, k = a.shape
    n = b.shape[0] if dims == "nt" else b.shape[1]
    per_row = 2 * k + 8 * n + sum(4 * r.shape[1] for r in rows) + sum(c * jnp.dtype(d).itemsize for c, d in row_outs)
    tm = _tile(m, 1024)
    while 2 * tm * per_row + 4 * k * n > MM_VMEM_BUDGET:
        tm = _tile(m, tm - LANES)
    dot = _dot_nt if dims == "nt" else _dot
    n_in = 2 + len(rows) + len(fulls) + (after is not None)

    def body(*refs):
        ins, outs = refs[:n_in], refs[n_in:]
        row_refs, acc_refs = outs[:len(row_outs)], outs[len(row_outs):]

        @pl.when(pl.program_id(0) == 0)
        def _():
            for r in acc_refs:
                r[...] = jnp.zeros_like(r)

        new_rows, incs = fn(dot(ins[0][...], ins[1][...]), *[r[...] for r in ins[2:2 + len(rows) + len(fulls)]])
        for r, val in zip(row_refs, new_rows):
            r[...] = val.astype(r.dtype)
        for r, inc in zip(acc_refs, incs):
            r[...] += inc

    extra, extra_specs = ([after], [pl.BlockSpec(memory_space=pl.ANY)]) if after is not None else ([], [])
    return pl.pallas_call(
        body, name=name, grid=(m // tm,),
        in_specs=[_rows(tm, k), _full(b.shape)] + [_rows(tm, r.shape[1]) for r in rows]
        + [_full(f.shape) for f in fulls] + extra_specs,
        out_specs=[_rows(tm, c) for c, _ in row_outs] + [_full(s) for s in acc_outs],
        out_shape=[jax.ShapeDtypeStruct((m, c), d) for c, d in row_outs]
        + [jax.ShapeDtypeStruct(s, F32) for s in acc_outs],
        compiler_params=_params(1),
    )(a, b, *rows, *fulls, *extra)


def _residual_norm(o, x, w):
    h = x + o
    return (h, _rms(h, w)), ()


def _norm_backward(dn, h, dres, w):
    _, vjp = jax.vjp(_rms, h, w)
    dh, dw = vjp(dn)
    dh = dh + dres
    return (dh, dh), (dw,)


def _loss_and_grad(dn, h1, target, w):
    yf, vjp = jax.vjp(_rms, h1 + dn, w)
    err = yf - target
    loss = 0.5 * jnp.sum(jnp.mean(err * err, axis=-1, keepdims=True))
    dh, dw = vjp(err * (1.0 / err.shape[-1]))
    return (dh, dh), (jnp.full((8, LANES), loss, F32), dw)


def _wgrad(a, d, name, after=None):
    return _mm(a, d, "tn", name, out_dtype=BF16, after=after)


def _norm_fwd(x, w, name, after=None, tm=512):
    t, d = x.shape

    def body(x_ref, w_ref, *rest):
        rest[-1][...] = _rms(x_ref[...], w_ref[...]).astype(BF16)

    extra, extra_specs = ([after], [_full(after.shape)]) if after is not None else ([], [])
    return pl.pallas_call(
        body, name=name, grid=(t // tm,), in_specs=[_rows(tm, d), _full((1, d))] + extra_specs,
        out_specs=_rows(tm, d), out_shape=jax.ShapeDtypeStruct((t, d), BF16), compiler_params=_params(1),
    )(x, w, *extra)


def _conv_a_fwd(xbc, cw, cb, tm=256):
    t, c = xbc.shape

    def body(x_ref, h_ref, w_ref, b_ref, o_ref, y_ref):
        halo = jnp.where(pl.program_id(0) > 0, h_ref[...].astype(F32)[8:], 0.0)
        y = _causal_conv(x_ref[...].astype(F32), halo, w_ref[...], b_ref[...])
        y_ref[...] = y.astype(BF16)
        o_ref[...] = _silu(y)

    return pl.pallas_call(
        body, name="conv_a_fwd", grid=(t // tm,),
        in_specs=[_rows(tm, c), _halo(tm, c, rows=16), _full(cw.shape), _full((1, c))],
        out_specs=[_rows(tm, c)] * 2,
        out_shape=[jax.ShapeDtypeStruct((t, c), F32), jax.ShapeDtypeStruct((t, c), BF16)], compiler_params=_params(1),
    )(xbc, xbc, cw, cb)


def _ssd_common(dtr, dtb, alog, e_t):
    row = lax.broadcasted_iota(jnp.int32, (CHUNK, CHUNK), 0)
    col = lax.broadcasted_iota(jnp.int32, (CHUNK, CHUNK), 1)
    causal = row >= col
    dt = _softplus(dtr + dtb)
    a = -jnp.exp(alog)
    acum = jnp.dot(causal.astype(F32), dt * a, precision=HIGHEST, preferred_element_type=F32)
    spread = lambda v: _dot(v.astype(BF16), e_t)
    elast = jnp.broadcast_to(jnp.exp(acum[CHUNK - 1:CHUNK, :]), (8, LANES))
    return dict(dt=dt, a=a, acum=acum, acum_t=acum.T, causal=causal, row=row, col=col, lane_lo=col < SSD_HEAD_DIM,
                dt_x=_dot_split(dt, e_t), ecol_x=spread(jnp.exp(acum)), elast_x=_dot_split(elast, e_t)[0:1],
                dsr_x=spread(jnp.exp(acum[CHUNK - 1:CHUNK, :] - acum)))


def _head_decay(c, h, transposed=False):
    d = c["acum"][:, h:h + 1] - c["acum_t"][h:h + 1, :]
    if transposed:
        return jnp.exp(jnp.where(c["row"] <= c["col"], -d, -jnp.inf))
    return jnp.exp(jnp.where(c["causal"], d, -jnp.inf))


def _ssd_fwd(xc, dtr, z, dtb, alog, dsk, nw, e_t):
    t = xc.shape[0]
    nc = t // CHUNK

    def body(xs_ref, b_ref, c_ref, dtr_ref, z_ref, dtb_ref, alog_ref, dsk_ref, nw_ref, et_ref,
             y_ref, ya_ref, sp_ref, s_scr):
        @pl.when(pl.program_id(0) == 0)
        def _():
            s_scr[...] = jnp.zeros_like(s_scr)

        c = _ssd_common(dtr_ref[...], dtb_ref[...], alog_ref[...], et_ref[...])
        lane_lo = c["lane_lo"]
        dsk = dsk_ref[...]
        for g in range(SSD_GROUPS):
            gs = slice(g * SSD_STATE, (g + 1) * SSD_STATE)
            bg_t, cg = b_ref[:, gs].T.astype(BF16), c_ref[:, gs].astype(BF16)
            cb = _dot(cg, bg_t)
            for pp in range(PAIRS_PER_GROUP):
                j = g * PAIRS_PER_GROUP + pp
                ps = slice(j * LANES, (j + 1) * LANES)
                x = xs_ref[:, ps]
                ecol, dsr = c["ecol_x"][:, ps], c["dsr_x"][:, ps]
                xdt = x * c["dt_x"][:, ps]
                xb = xdt.astype(BF16)
                zero = jnp.zeros_like(xb)
                yd = (_dot((cb * _head_decay(c, 2 * j)).astype(BF16), jnp.where(lane_lo, xb, zero))
                      + _dot((cb * _head_decay(c, 2 * j + 1)).astype(BF16), jnp.where(lane_lo, zero, xb)))
                sp = s_scr[j]
                yo = ecol * _dot(cg, sp.astype(BF16))
                st = _dot(bg_t, (xdt * dsr).astype(BF16))
                sp_ref[0, j] = sp
                s_scr[j] = c["elast_x"][:, ps] * sp + st
                dskp = jnp.where(lane_lo[0:1], dsk[:, 2 * j:2 * j + 1], dsk[:, 2 * j + 1:2 * j + 2])
                y_ref[:, ps] = yd + yo + dskp * x
        ya_ref[...] = _rms(y_ref[...] * _silu(z_ref[...].astype(F32)), nw_ref[...]).astype(BF16)

    ck = lambda n, col=0: pl.BlockSpec((CHUNK, n), lambda c: (c, col))
    return pl.pallas_call(
        body, name="ssd_fwd", grid=(nc,),
        in_specs=[ck(SSD_INNER), ck(SSD_BC, SSD_INNER // SSD_BC), ck(SSD_BC, SSD_INNER // SSD_BC + 1), ck(DT_PAD),
                  ck(SSD_INNER), _full((1, DT_PAD)), _full((1, DT_PAD)), _full((1, DT_PAD)),
                  _full((1, SSD_INNER)), _full(e_t.shape)],
        out_specs=[ck(SSD_INNER), ck(SSD_INNER),
                   pl.BlockSpec((1, N_PAIRS, SSD_STATE, LANES), lambda c: (c, 0, 0, 0))],
        out_shape=[jax.ShapeDtypeStruct((t, SSD_INNER), F32), jax.ShapeDtypeStruct((t, SSD_INNER), BF16),
                   jax.ShapeDtypeStruct((nc, N_PAIRS, SSD_STATE, LANES), F32)],
        scratch_shapes=[pltpu.VMEM((N_PAIRS, SSD_STATE, LANES), F32)], compiler_params=_params(1),
    )(xc, xc, xc, dtr, z, dtb, alog, dsk, nw, e_t)


def _ssd_bwd(dya, y, z, xc, dtr, sprev, dtb, alog, dsk, nw, e_heads, e_t):
    t = xc.shape[0]
    nc = t // CHUNK

    def body(dya_ref, y_ref, z_ref, xs_ref, b_ref, c_ref, dtr_ref, sp_ref, dtb_ref, alog_ref, dsk_ref, nw_ref, e_ref,
             et_ref, dz_ref, dxs_ref, db_ref, dc_ref, ddtr_ref, dnw_ref, ddtb_ref, dalog_ref, ddsk_ref, ds_scr):
        @pl.when(pl.program_id(0) == 0)
        def _():
            ds_scr[...] = jnp.zeros_like(ds_scr)
            for r in (dnw_ref, ddtb_ref, dalog_ref, ddsk_ref):
                r[...] = jnp.zeros_like(r)

        y = y_ref[...]
        _, gate_vjp = jax.vjp(lambda y_, z_, w_: _rms(y_ * _silu(z_), w_), y, z_ref[...].astype(F32), nw_ref[...])
        dy, dz, dnw = gate_vjp(dya_ref[...])
        dz_ref[...] = dz.astype(BF16)
        dnw_ref[...] += dnw

        dtr = dtr_ref[...]
        c = _ssd_common(dtr, dtb_ref[...], alog_ref[...], et_ref[...])
        dt, a, lane_lo, row, col = c["dt"], c["a"], c["lane_lo"], c["row"], c["col"]
        dsk = dsk_ref[...]
        p_a, p_dt, v_last = [], [], []
        da_cols = jnp.zeros((CHUNK, CHUNK), F32)
        da_rows = jnp.zeros((CHUNK, CHUNK), F32)
        for g in range(SSD_GROUPS):
            gs = slice(g * SSD_STATE, (g + 1) * SSD_STATE)
            bg, cg = b_ref[:, gs].astype(BF16), c_ref[:, gs].astype(BF16)
            bg_t, cg_t = b_ref[:, gs].T.astype(BF16), c_ref[:, gs].T.astype(BF16)
            cb, cb_t = _dot(cg, bg_t), _dot(bg, cg_t)
            dcb = jnp.zeros((CHUNK, CHUNK), F32)
            dbg = jnp.zeros((CHUNK, SSD_STATE), F32)
            dcg = jnp.zeros((CHUNK, SSD_STATE), F32)
            for pp in range(PAIRS_PER_GROUP):
                j = g * PAIRS_PER_GROUP + pp
                ps = slice(j * LANES, (j + 1) * LANES)
                x = xs_ref[:, ps]
                dtp, ecol, dsr = c["dt_x"][:, ps], c["ecol_x"][:, ps], c["dsr_x"][:, ps]
                elast = c["elast_x"][:, ps]
                xdt = x * dtp
                xb = xdt.astype(BF16)
                dskp = jnp.where(lane_lo[0:1], dsk[:, 2 * j:2 * j + 1], dsk[:, 2 * j + 1:2 * j + 2])
                dyp = dy[:, ps]
                dyb = dyp.astype(BF16)
                sp, dsn = sp_ref[0, j], ds_scr[j]
                spb, dsnb = sp.astype(BF16), dsn.astype(BF16)
                y_off = ecol * _dot(cg, spb)
                dw = (dyp * ecol).astype(BF16)
                dcg = dcg + _dot_nt(dw, spb)
                dsp = _dot(cg_t, dw) + elast * dsn
                xd = xdt * dsr
                zd = _dot(bg, dsnb) * dsr
                dbg = dbg + _dot_nt(xd.astype(BF16), dsnb)
                dxdt = zd
                zero = jnp.zeros_like(xb)
                for h, lm in ((2 * j, lane_lo), (2 * j + 1, jnp.logical_not(lane_lo))):
                    le = _head_decay(c, h)
                    dm = _dot_nt(jnp.where(lm, dyb, zero), jnp.where(lm, xb, zero))
                    dcb = dcb + dm * le
                    m = cb * le
                    m_t = (cb_t * _head_decay(c, h, transposed=True)).astype(BF16)
                    dxdt = dxdt + jnp.where(lm, _dot(m_t, dyb), 0.0)
                    q = dm * m
                    da_cols = da_cols + jnp.where(col == h, jnp.sum(q, axis=1, keepdims=True), 0.0)
                    da_rows = da_rows + jnp.where(row == h, _colsum(q), 0.0)
                ds_scr[j] = dsp
                dxs_ref[:, ps] = dxdt * dtp + dskp * dyp
                p_a.append(dyp * y_off - xdt * zd)
                p_dt.append(dxdt * x)
                v_last.append(_colsum(zd * xdt) + elast * _colsum(dsn * sp))
            dcbb = dcb.astype(BF16)
            db_ref[:, gs] = dbg + _dot_tn(dcbb, cg)
            dc_ref[:, gs] = dcg + _dot(dcbb, bg)
        e = e_ref[...]
        rows8 = jnp.concatenate([jnp.concatenate(v_last, axis=1), _colsum(dy * xs_ref[...]),
                                 jnp.zeros((6, SSD_INNER), F32)], axis=0)
        r8 = _dot_split(rows8, e)
        da = (_dot_split(jnp.concatenate(p_a, axis=1), e) + jnp.where(row == CHUNK - 1, r8[0:1], 0.0)
              + da_cols - da_rows.T)
        ddsk_ref[...] += r8[1:2]
        dadt = jnp.dot((row <= col).astype(F32), da, precision=HIGHEST, preferred_element_type=F32)
        ddt = dadt * a + _dot_split(jnp.concatenate(p_dt, axis=1), e)
        dalog_ref[...] += _colsum(dadt * dt) * a
        ddtr = ddt * _sigmoid(dtr + dtb_ref[...])
        ddtr_ref[...] = ddtr
        ddtb_ref[...] += _colsum(ddtr)

    ck = lambda n, col=0: pl.BlockSpec((CHUNK, n), lambda c: (nc - 1 - c, col))
    acc = lambda n: _full((1, n))
    return pl.pallas_call(
        body, name="ssd_bwd", grid=(nc,),
        in_specs=[ck(SSD_INNER), ck(SSD_INNER), ck(SSD_INNER), ck(SSD_INNER), ck(SSD_BC, SSD_INNER // SSD_BC),
                  ck(SSD_BC, SSD_INNER // SSD_BC + 1), ck(DT_PAD),
                  pl.BlockSpec((1, N_PAIRS, SSD_STATE, LANES), lambda c: (nc - 1 - c, 0, 0, 0)),
                  acc(DT_PAD), acc(DT_PAD), acc(DT_PAD), acc(SSD_INNER), _full((SSD_INNER, LANES)),
                  _full((LANES, SSD_INNER))],
        out_specs=[ck(SSD_INNER), ck(SSD_INNER), ck(SSD_BC), ck(SSD_BC), ck(DT_PAD),
                   acc(SSD_INNER), acc(DT_PAD), acc(DT_PAD), acc(DT_PAD)],
        out_shape=[jax.ShapeDtypeStruct((t, SSD_INNER), BF16), jax.ShapeDtypeStruct((t, SSD_INNER), F32),
                   jax.ShapeDtypeStruct((t, SSD_BC), F32), jax.ShapeDtypeStruct((t, SSD_BC), F32),
                   jax.ShapeDtypeStruct((t, DT_PAD), F32), jax.ShapeDtypeStruct((1, SSD_INNER), F32),
                   jax.ShapeDtypeStruct((1, DT_PAD), F32), jax.ShapeDtypeStruct((1, DT_PAD), F32),
                   jax.ShapeDtypeStruct((1, DT_PAD), F32)],
        scratch_shapes=[pltpu.VMEM((N_PAIRS, SSD_STATE, LANES), F32)], compiler_params=_params(1),
    )(dya, y, z, xc, xc, xc, dtr, sprev, dtb, alog, dsk, nw, e_heads, e_t)


def _sgu_act(uv, uvb, lnw, lnb):
    a = _gelu(uv + uvb)
    return a[:, :SGU_WIDTH], _layer_norm(a[:, SGU_WIDTH:], lnw, lnb)


def _sgu_weights(ws_ref):
    row = lax.broadcasted_iota(jnp.int32, (CHUNK, CHUNK), 0)
    col = lax.broadcasted_iota(jnp.int32, (CHUNK, CHUNK), 1)
    return [jnp.where(row >= col, ws_ref[g], 0.0).astype(BF16) for g in range(SGU_GROUPS)], row >= col


def _sgu_fwd(uv, uvb, lnw, lnb, ws, bs_t):
    t = uv.shape[0]

    def body(uv_ref, uvb_ref, lnw_ref, lnb_ref, ws_ref, bs_ref, o_ref):
        u, vn = _sgu_act(uv_ref[...].astype(F32), uvb_ref[...], lnw_ref[...], lnb_ref[...])
        wc, _ = _sgu_weights(ws_ref)
        bs = bs_ref[...]
        for g in range(SGU_GROUPS):
            gs = slice(g * LANES, (g + 1) * LANES)
            mixed = _dot(wc[g], vn[:, gs].astype(BF16)) + bs[:, g:g + 1]
            o_ref[:, gs] = (u[:, gs] * mixed).astype(BF16)

    return pl.pallas_call(
        body, name="sgu_fwd", grid=(t // CHUNK,),
        in_specs=[_rows(CHUNK, 2 * SGU_WIDTH), _full((1, 2 * SGU_WIDTH)), _full((1, SGU_WIDTH)), _full((1, SGU_WIDTH)),
                  _full(ws.shape), _full(bs_t.shape)],
        out_specs=_rows(CHUNK, SGU_WIDTH), out_shape=jax.ShapeDtypeStruct((t, SGU_WIDTH), BF16),
        compiler_params=_params(1),
    )(uv, uvb, lnw, lnb, ws, bs_t)


def _sgu_bwd(dyb, uv, uvb, lnw, lnb, ws, bs_t, e_groups):
    t = uv.shape[0]

    def body(dyb_ref, uv_ref, uvb_ref, lnw_ref, lnb_ref, ws_ref, bs_ref, e_ref,
             duv_ref, duvb_ref, dlnw_ref, dlnb_ref, dws_ref, dbs_ref):
        @pl.when(pl.program_id(0) == 0)
        def _():
            for r in (duvb_ref, dlnw_ref, dlnb_ref, dws_ref, dbs_ref):
                r[...] = jnp.zeros_like(r)

        (u, vn), act_vjp = jax.vjp(_sgu_act, uv_ref[...].astype(F32), uvb_ref[...], lnw_ref[...], lnb_ref[...])
        wc, causal = _sgu_weights(ws_ref)
        bs = bs_ref[...]
        dyb = dyb_ref[...]
        du, dvn, dmix = [], [], []
        for g in range(SGU_GROUPS):
            gs = slice(g * LANES, (g + 1) * LANES)
            vb = vn[:, gs].astype(BF16)
            mixed = _dot(wc[g], vb) + bs[:, g:g + 1]
            dm = dyb[:, gs] * u[:, gs]
            dmb = dm.astype(BF16)
            du.append(dyb[:, gs] * mixed)
            dvn.append(_dot_tn(wc[g], dmb))
            dws_ref[g] += jnp.where(causal, _dot_nt(dmb, vb), 0.0)
            dmix.append(dm)
        dbs_ref[...] += _dot_split(jnp.concatenate(dmix, axis=1), e_ref[...])
        duv, duvb, dlnw, dlnb = act_vjp((jnp.concatenate(du, axis=1), jnp.concatenate(dvn, axis=1)))
        duv_ref[...] = duv.astype(BF16)
        duvb_ref[...] += duvb
        dlnw_ref[...] += dlnw
        dlnb_ref[...] += dlnb

    return pl.pallas_call(
        body, name="sgu_bwd", grid=(t // CHUNK,),
        in_specs=[_rows(CHUNK, SGU_WIDTH), _rows(CHUNK, 2 * SGU_WIDTH), _full((1, 2 * SGU_WIDTH)),
                  _full((1, SGU_WIDTH)), _full((1, SGU_WIDTH)), _full(ws.shape), _full(bs_t.shape),
                  _full(e_groups.shape)],
        out_specs=[_rows(CHUNK, 2 * SGU_WIDTH), _full((1, 2 * SGU_WIDTH)), _full((1, SGU_WIDTH)),
                   _full((1, SGU_WIDTH)), _full(ws.shape), _full(bs_t.shape)],
        out_shape=[jax.ShapeDtypeStruct((t, 2 * SGU_WIDTH), BF16), jax.ShapeDtypeStruct((1, 2 * SGU_WIDTH), F32),
                   jax.ShapeDtypeStruct((1, SGU_WIDTH), F32), jax.ShapeDtypeStruct((1, SGU_WIDTH), F32),
                   jax.ShapeDtypeStruct(ws.shape, F32), jax.ShapeDtypeStruct(bs_t.shape, F32)],
        compiler_params=_params(1),
    )(dyb, uv, uvb, lnw, lnb, ws, bs_t, e_groups)


def _merge(gates, pa, pb, bg):
    s = _sigmoid(gates + bg)
    return s[:, :D_MODEL] * pa + s[:, D_MODEL:] * pb


def _merge_fwd(gates, pa, pb, bg, tm=256):
    t = gates.shape[0]

    def body(g_ref, pa_ref, pb_ref, bg_ref, o_ref):
        o_ref[...] = _merge(g_ref[...].astype(F32), pa_ref[...].astype(F32), pb_ref[...].astype(F32),
                            bg_ref[...]).astype(BF16)

    return pl.pallas_call(
        body, name="merge_fwd", grid=(t // tm,),
        in_specs=[_rows(tm, 2 * D_MODEL), _rows(tm, D_MODEL), _rows(tm, D_MODEL), _full((1, 2 * D_MODEL))],
        out_specs=_rows(tm, D_MODEL), out_shape=jax.ShapeDtypeStruct((t, D_MODEL), BF16), compiler_params=_params(1),
    )(gates, pa, pb, bg)


def _merge_backward(dmix, gates, pa, pb, bg):
    _, vjp = jax.vjp(_merge, gates.astype(F32), pa.astype(F32), pb.astype(F32), bg)
    dg, dpa, dpb, dbg = vjp(dmix)
    return (dg, dpa, dpb), (dbg,)


def _conv_f_fwd(up, cw, cb, tm=128):
    t, c = up.shape

    def body(x_ref, h_ref, w_ref, b_ref, o_ref, y_ref):
        halo = jnp.where(pl.program_id(0) > 0, h_ref[...].astype(F32)[8:], 0.0)
        y = _causal_conv(x_ref[...].astype(F32), halo, w_ref[...], b_ref[...])
        y_ref[...] = y.astype(BF16)
        o_ref[...] = (_silu(y[:, :D_FF]) * y[:, D_FF:]).astype(BF16)

    return pl.pallas_call(
        body, name="conv_f_fwd", grid=(t // tm,),
        in_specs=[_rows(tm, c), _halo(tm, c, rows=16), _full(cw.shape), _full((1, c))],
        out_specs=[_rows(tm, D_FF), _rows(tm, c)],
        out_shape=[jax.ShapeDtypeStruct((t, D_FF), BF16), jax.ShapeDtypeStruct((t, c), BF16)],
        compiler_params=_params(1),
    )(up, up, cw, cb)


def _conv_f_bwd(dact, y, up, cw, tm=128):
    t, c = up.shape
    nt = t // tm

    def body(d_ref, y_ref, x_ref, w_ref, dx_ref, dw_ref, db_ref, nxt_scr):
        @pl.when(pl.program_id(0) == 0)
        def _():
            nxt_scr[...] = jnp.zeros_like(nxt_scr)
            dw_ref[...] = jnp.zeros_like(dw_ref)
            db_ref[...] = jnp.zeros_like(db_ref)

        a, v = y_ref[:, :D_FF].astype(F32), y_ref[:, D_FF:].astype(F32)
        d = d_ref[...].astype(F32)
        silu_a, dsilu_a = _silu_and_grad(a)
        dy = jnp.concatenate([d * v * dsilu_a, d * silu_a], axis=1)
        dx, dw = _causal_conv_bwd(dy, nxt_scr[...], x_ref[...].astype(F32), w_ref[...])
        dx_ref[...] = dx.astype(BF16)
        nxt_scr[...] = dy[:8]
        dw_ref[...] += dw
        db_ref[...] += _colsum(dy)

    return pl.pallas_call(
        body, name="conv_f_bwd", grid=(nt,),
        in_specs=[_rows(tm, D_FF, nt, True), _rows(tm, c, nt, True), _rows(tm, c, nt, True), _full(cw.shape)],
        out_specs=[_rows(tm, c, nt, True), _full(cw.shape), _full((1, c))],
        out_shape=[jax.ShapeDtypeStruct((t, c), BF16), jax.ShapeDtypeStruct(cw.shape, F32),
                   jax.ShapeDtypeStruct((1, c), F32)],
        scratch_shapes=[pltpu.VMEM((8, c), F32)], compiler_params=_params(1),
    )(dact, y, up, cw)


def _conv_a_bwd(dxs, db, dc, y, xbc, cw, tm=256):
    t, c = xbc.shape
    nt = t // tm

    def body(dxs_ref, db_ref, dc_ref, y_ref, x_ref, w_ref, dx_ref, dw_ref, dbias_ref, nxt_scr):
        @pl.when(pl.program_id(0) == 0)
        def _():
            nxt_scr[...] = jnp.zeros_like(nxt_scr)
            dw_ref[...] = jnp.zeros_like(dw_ref)
            dbias_ref[...] = jnp.zeros_like(dbias_ref)

        dy = jnp.concatenate([dxs_ref[...], db_ref[...], dc_ref[...]], axis=1) * _dsilu(y_ref[...].astype(F32))
        dx, dw = _causal_conv_bwd(dy, nxt_scr[...], x_ref[...].astype(F32), w_ref[...])
        dx_ref[...] = dx.astype(BF16)
        nxt_scr[...] = dy[:8]
        dw_ref[...] += dw
        dbias_ref[...] += _colsum(dy)

    return pl.pallas_call(
        body, name="conv_a_bwd", grid=(nt,),
        in_specs=[_rows(tm, SSD_INNER, nt, True), _rows(tm, SSD_BC, nt, True), _rows(tm, SSD_BC, nt, True),
                  _rows(tm, c, nt, True), _rows(tm, c, nt, True), _full(cw.shape)],
        out_specs=[_rows(tm, c, nt, True), _full(cw.shape), _full((1, c))],
        out_shape=[jax.ShapeDtypeStruct((t, c), BF16), jax.ShapeDtypeStruct(cw.shape, F32),
                   jax.ShapeDtypeStruct((1, c), F32)],
        scratch_shapes=[pltpu.VMEM((8, c), F32)], compiler_params=_params(1),
    )(dxs, db, dc, y, xbc, cw)


def _pad_lanes(v, n=DT_PAD):
    return jnp.pad(v, ((0, 0), (0, n - v.shape[1])))


def _local_step(x, target, w, p, after=None, late_weights=None, on_grad=None, on_small=None):
    dtb, alog, dsk = _pad_lanes(p["dt_bias"]), _pad_lanes(p["a_log"]), _pad_lanes(p["d_skip"])
    bs_t = _pad_lanes(p["b_spatial"].T)
    e_heads = (jnp.arange(SSD_INNER)[:, None] // SSD_HEAD_DIM == jnp.arange(LANES)[None, :]).astype(BF16)
    e_heads_t = (jnp.arange(LANES)[:, None] == jnp.arange(SSD_INNER)[None, :] // SSD_HEAD_DIM).astype(BF16)
    e_groups = (jnp.arange(SGU_WIDTH)[:, None] // LANES == jnp.arange(LANES)[None, :]).astype(BF16)

    n1 = _norm_fwd(x, p["norm1_w"], "norm1_fwd", after=after)
    z = _mm(n1, w["z"], "nt", "proj_z", out_dtype=BF16)
    xbc = _mm(n1, w["xbc"], "nt", "proj_xbc", out_dtype=BF16)
    dtr = _mm(n1, w["dt"], "nt", "proj_dt")
    uv = _mm(n1, w["uv"], "nt", "proj_uv", out_dtype=BF16)
    gates = _mm(n1, w["gates"], "nt", "proj_gates", out_dtype=BF16)
    xc, conv_a_out = _conv_a_fwd(xbc, w["conv_a"], p["conv_a_b"])
    y, ya, sprev = _ssd_fwd(xc, dtr, z, dtb, alog, dsk, p["ssd_norm_w"], e_heads_t)
    yb = _sgu_fwd(uv, p["uv_b"], p["v_ln_w"], p["v_ln_b"], p["w_spatial"], bs_t)
    if late_weights is not None:
        w = {**w, **late_weights(ya, yb)}
    pa = _mm(ya, w["branch_a"], "nn", "branch_a", out_dtype=BF16)
    pb = _mm(yb, w["branch_b"], "nn", "branch_b", out_dtype=BF16)
    mix = _merge_fwd(gates, pa, pb, p["b_gate"])
    wide = [(D_MODEL, F32), (D_MODEL, BF16)]
    h1, n2 = _mm_rows(mix, w["out"], "nn", "out_proj", _residual_norm, rows=[x], fulls=[p["norm2_w"]], row_outs=wide)
    up = _mm(n2, w["up"], "nt", "up_proj", out_dtype=BF16)
    act, conv_f_out = _conv_f_fwd(up, w["conv_f"], p["conv_f_b"])
    dh2, dh2b, loss, g_final = _mm_rows(
        act, w["down"], "nn", "down_proj", _loss_and_grad, rows=[h1, target], fulls=[p["final_norm_w"]],
        row_outs=wide, acc_outs=[(8, LANES), (1, D_MODEL)])

    on_grad = on_grad or (lambda name, grads: None)
    g = {"final_norm_w": g_final}
    g["down"] = _wgrad(act, dh2b, "down_wgrad")
    tok = on_grad("w_down", g)
    dact = _mm(dh2b, w["down"], "nt", "down_dgrad", out_dtype=BF16, after=tok)
    dup, g["conv_f"], g["conv_f_b"] = _conv_f_bwd(dact, conv_f_out, up, w["conv_f"])
    g["up"] = _wgrad(dup, n2, "up_wgrad")
    tok = on_grad("w_up", g)
    dh1, dh1b, g["norm2_w"] = _mm_rows(
        dup, w["up"], "nn", "up_dgrad", _norm_backward, rows=[h1, dh2], fulls=[p["norm2_w"]], row_outs=wide,
        acc_outs=[(1, D_MODEL)], after=tok)
    g["out"] = _wgrad(mix, dh1b, "out_wgrad")
    tok = on_grad("w_out", g)
    dgates, dpa, dpb, g["b_gate"] = _mm_rows(
        dh1b, w["out"], "nt", "out_dgrad", _merge_backward, rows=[gates, pa, pb], fulls=[p["b_gate"]],
        row_outs=[(2 * D_MODEL, BF16), (D_MODEL, BF16), (D_MODEL, BF16)], acc_outs=[(1, 2 * D_MODEL)], after=tok)
    g["branch_a"] = _wgrad(ya, dpa, "branch_a_wgrad")
    g["branch_b"] = _wgrad(yb, dpb, "branch_b_wgrad")
    tok = on_grad("w_branch", g)
    dya = _mm(dpa, w["branch_a"], "nt", "branch_a_dgrad", after=tok)
    dyb = _mm(dpb, w["branch_b"], "nt", "branch_b_dgrad", after=tok)
    duv, g["uv_b"], g["v_ln_w"], g["v_ln_b"], g["w_spatial"], dbs_t = _sgu_bwd(
        dyb, uv, p["uv_b"], p["v_ln_w"], p["v_ln_b"], p["w_spatial"], bs_t, e_groups)
    g["b_spatial"] = dbs_t[:, :SGU_GROUPS].T
    dz, dxs, db, dc, ddtr, g["ssd_norm_w"], ddtb, dalog, ddsk = _ssd_bwd(
        dya, y, z, xc, dtr, sprev, dtb, alog, dsk, p["ssd_norm_w"], e_heads, e_heads_t)
    g["dt_bias"], g["a_log"], g["d_skip"] = ddtb, dalog, ddsk
    dxbc, g["conv_a"], g["conv_a_b"] = _conv_a_bwd(dxs, db, dc, conv_a_out, xbc, w["conv_a"])
    tok = on_small(g, loss) if on_small else None
    ddtrb = ddtr.astype(BF16)
    for name, d in (("z", dz), ("xbc", dxbc), ("dt", ddtrb), ("uv", duv), ("gates", dgates)):
        g[name] = _wgrad(d, n1, name + "_wgrad", after=tok)
    tok = on_grad("w_in", g)
    dn1 = _mm(dz, w["z"], "nn", "z_dgrad", after=tok)
    dn1 = _mm(dxbc, w["xbc"], "nn", "xbc_dgrad", acc=dn1)
    dn1 = _mm(ddtrb, w["dt"], "nn", "dt_dgrad", acc=dn1)
    dn1 = _mm(duv, w["uv"], "nn", "uv_dgrad", acc=dn1)
    gx, g["norm1_w"] = _mm_rows(
        dgates, w["gates"], "nn", "gates_dgrad",
        lambda r, so_far, h, dres, w_: tuple(t[:1] for t in _norm_backward(r + so_far, h, dres, w_)),
        rows=[dn1, x, dh1], fulls=[p["norm1_w"]], row_outs=wide[:1], acc_outs=[(1, D_MODEL)])
    return loss, gx, g


def _place():
    return lax.axis_index("x"), lax.axis_index("y"), lax.axis_index("c")


def _other_chips(x, y):
    return [(1 - x, y), (x, 1 - y), (1 - x, 1 - y)]


def _all_gather(shards, name):
    n = len(shards)

    def body(*refs):
        ins, outs = refs[:n], refs[n:2 * n]
        send_sems, recv_sems, local_sems = refs[2 * n:]
        x, y, c = _place()
        me, sibling = (x, y, c), (x, y, 1 - c)
        chips = _other_chips(x, y)

        def copy(a, k, block, to, src=None):
            slot = outs[a].at[4 * block[0] + 2 * block[1] + block[2]]
            return pltpu.make_async_remote_copy(
                src_ref=slot if src is None else src, dst_ref=slot, send_sem=send_sems.at[7 * a + k],
                recv_sem=recv_sems.at[7 * a + k], device_id=to, device_id_type=MESH)

        started = []
        for a in range(n):
            mine = pltpu.make_async_copy(ins[a], outs[a].at[4 * x + 2 * y + c], local_sems.at[a])
            mine.start()
            started.append(mine)
        sends = []
        for a in range(n):
            sends.append(copy(a, 0, me, sibling, src=ins[a]))
            sends += [copy(a, 1 + j, me, (*chip, c), src=ins[a]) for j, chip in enumerate(chips)]
        for cp in sends:
            cp.start()
        for a in range(n):
            for j, chip in enumerate(chips):
                copy(a, 1 + j, (*chip, c), me).wait_recv()
                fwd = copy(a, 4 + j, (*chip, c), sibling)
                fwd.start()
                sends.append(fwd)
        for a in range(n):
            copy(a, 0, sibling, me).wait_recv()
            for j, chip in enumerate(chips):
                copy(a, 4 + j, (*chip, 1 - c), me).wait_recv()
        for cp in sends:
            cp.wait_send()
        for mine in started:
            mine.wait()

    any_spec = pl.BlockSpec(memory_space=pl.ANY)
    return pl.pallas_call(
        body, name=name, in_specs=[any_spec] * n, out_specs=[any_spec] * n,
        out_shape=[jax.ShapeDtypeStruct((N_DEV, *s.shape), s.dtype) for s in shards],
        scratch_shapes=[pltpu.SemaphoreType.DMA((7 * n,)), pltpu.SemaphoreType.DMA((7 * n,)),
                        pltpu.SemaphoreType.DMA((n,))],
    )(*shards)


HBM_SPEC = pl.BlockSpec(memory_space=pltpu.HBM)
SEM_SPEC = pl.BlockSpec(memory_space=pltpu.SEMAPHORE)
ANY_SPEC = pl.BlockSpec(memory_space=pl.ANY)
DATAFLOW = pltpu.SideEffectType.DATAFLOW_SIDE_EFFECTING
N_PEERS = N_DEV - 1


def _peers(x, y, c):
    out = []
    for r in range(1, N_DEV):
        fx, fy, fc = r >> 2 & 1, r >> 1 & 1, r & 1
        out.append(((1 - x) if fx else x, (1 - y) if fy else y, (1 - c) if fc else c))
    return out


def _gather_copies(srcs, lands, send_sems, recv_sems, sending, scatter=False):
    x, y, c = _place()
    copies = []
    for a, (src, land) in enumerate(zip(srcs, lands)):
        for j, (px, py, pc) in enumerate(_peers(x, y, c)):
            mine, theirs = 4 * x + 2 * y + c, 4 * px + 2 * py + pc
            block = src.at[theirs if sending else 0] if scatter else src
            copies.append(pltpu.make_async_remote_copy(
                src_ref=block, dst_ref=land.at[mine if sending else theirs], send_sem=send_sems.at[N_PEERS * a + j],
                recv_sem=recv_sems.at[N_PEERS * a + j], device_id=(px, py, pc), device_id_type=MESH))
    return copies


def _gather_start(shards, after, name, scatter=False):
    n = len(shards)
    after = [] if after is None else [after]

    def body(*refs):
        srcs, lands = refs[:n], refs[n:2 * n]
        send_sems, recv_sems = refs[2 * n + len(after):2 * n + len(after) + 2]
        token = refs[-1]
        for cp in _gather_copies(srcs, lands, send_sems, recv_sems, sending=True, scatter=scatter):
            cp.start()
        token[...] = jnp.zeros_like(token)

    lands = [lax.empty(s.shape if scatter else (N_DEV, *s.shape), s.dtype) for s in shards]
    hbm = lambda a: pltpu.with_memory_space_constraint(a, pltpu.HBM)
    out = pl.pallas_call(
        body, name=name,
        out_shape=(pltpu.SemaphoreType.DMA((N_PEERS * n,)), pltpu.SemaphoreType.DMA((N_PEERS * n,)),
                   *[pltpu.HBM(a.shape, a.dtype) for a in (*shards, *lands)], jax.ShapeDtypeStruct((8, LANES), F32)),
        in_specs=[HBM_SPEC] * (2 * n) + [ANY_SPEC] * len(after),
        out_specs=(SEM_SPEC, SEM_SPEC, *[HBM_SPEC] * (2 * n), pl.BlockSpec(memory_space=pltpu.VMEM)),
        input_output_aliases={i: 2 + i for i in range(2 * n)},
        compiler_params=pltpu.CompilerParams(has_side_effects=DATAFLOW),
    )(*[hbm(a) for a in (*shards, *lands)], *after)
    return out[0], out[1], out[2:2 + n], out[2 + n:2 + 2 * n], out[-1]


def _gather_wait(send_sems, recv_sems, shards, lands, after, name, scatter=False):
    n = len(shards)
    after = tuple(after)

    def body(*refs):
        srcs, lands_ = refs[:n], refs[n:2 * n]
        send, recv = refs[2 * n:2 * n + 2]
        for cp in _gather_copies(srcs, lands_, send, recv, sending=False, scatter=scatter):
            cp.wait_send()
            cp.wait_recv()

    out = pl.pallas_call(
        body, name=name, out_shape=tuple(pltpu.HBM(a.shape, a.dtype) for a in (*shards, *lands)),
        in_specs=[HBM_SPEC] * (2 * n) + [SEM_SPEC, SEM_SPEC] + [ANY_SPEC] * len(after),
        out_specs=tuple([HBM_SPEC] * (2 * n)), input_output_aliases={i: i for i in range(2 * n)},
        compiler_params=pltpu.CompilerParams(has_side_effects=DATAFLOW),
    )(*shards, *lands, send_sems, recv_sems, *after)
    return out[:n], out[n:]


def _adamw(w, g, m, v):
    m = ADAM_B1 * m + (1.0 - ADAM_B1) * g
    v = ADAM_B2 * v + (1.0 - ADAM_B2) * jnp.square(g)
    m_hat = m / (1.0 - ADAM_B1 ** ADAM_STEP)
    v_hat = v / (1.0 - ADAM_B2 ** ADAM_STEP)
    return -ADAM_LR * (m_hat / (jnp.sqrt(v_hat) + ADAM_EPS) + ADAM_WD * w), m, v


def _sum8_adamw(part, got, place, w, m, v, name, tr=256):
    _, r, c = part.shape
    if w.ndim == 3:
        tr, tc = r, 2 * LANES
        blk = pl.BlockSpec((tr, 1, tc), lambda i, j, pr: (i, 0, j))
    else:
        tr, tc = _tile2d(r, c, tr)
        blk = pl.BlockSpec((tr, tc), lambda i, j, pr: (i, j))

    def body(place_ref, own_ref, got_ref, w_ref, m_ref, v_ref, g_ref, d_ref, nm_ref, nv_ref):
        dev = 2 * place_ref[1] + place_ref[0]
        g = jnp.zeros((tr, tc), F32)
        for d in range(N_DEV):
            g = g + jnp.where(dev == d, own_ref[0], got_ref[d]).astype(F32)
        two_d = lambda ref: ref[...].reshape(tr, tc)
        delta, nm, nv = _adamw(two_d(w_ref), g, two_d(m_ref), two_d(v_ref))
        for ref, val in ((g_ref, g), (d_ref, delta), (nm_ref, nm), (nv_ref, nv)):
            ref[...] = val.reshape(ref.shape)

    grid_spec = pltpu.PrefetchScalarGridSpec(
        num_scalar_prefetch=1, grid=(r // tr, c // tc),
        in_specs=[pl.BlockSpec((1, tr, tc), lambda i, j, pr: (2 * pr[1] + pr[0], i, j)),
                  pl.BlockSpec((N_DEV, tr, tc), lambda i, j, pr: (0, i, j)), blk, blk, blk],
        out_specs=[blk] * 4)
    return pl.pallas_call(
        body, name=name, grid_spec=grid_spec, out_shape=[jax.ShapeDtypeStruct(w.shape, F32)] * 4,
        compiler_params=_params(2),
    )(place, part, got, w, m, v)


VECTORS = ["norm1_w", "b_gate", "conv_a_b", "dt_bias", "a_log", "d_skip", "ssd_norm_w", "uv_b", "v_ln_w", "v_ln_b",
           "norm2_w", "conv_f_b", "final_norm_w"]
SMALL_ORDER = VECTORS + ["w_spatial", "b_spatial", "conv_a_w", "conv_f_w"]


ROW_VECTORS = VECTORS[1:]


def _small_adamw(gathered, w, m, v):
    sizes = {n: w[n].shape[1] for n in ROW_VECTORS}
    offs, off = {}, 0
    for n in ROW_VECTORS:
        offs[n] = off
        off += -(-sizes[n] // LANES) * LANES
    loss_off = off
    k = len(SMALL_ORDER)
    n_g = len(gathered)

    def body(*refs):
        row_ref, ws_ref, bs_ref, ca_ref, cf_ref, n1_ref = refs[:n_g]
        w_refs, m_refs, v_refs = (dict(zip(SMALL_ORDER, refs[n_g + i * k:n_g + (i + 1) * k])) for i in range(3))
        outs = refs[n_g + 3 * k:]
        x, y, c = _place()
        dev = 4 * x + 2 * y + c

        def total(ref):
            s = ref[0]
            for d in range(1, N_DEV):
                s = s + ref[d]
            return s

        row = total(row_ref)
        grads = {n: row[:, offs[n]:offs[n] + sizes[n]] for n in ROW_VECTORS}
        grads["norm1_w"], grads["w_spatial"], grads["b_spatial"] = total(n1_ref), total(ws_ref), total(bs_ref)
        for n, ref in (("conv_a_w", ca_ref), ("conv_f_w", cf_ref)):
            whole, cols = total(ref), w_refs[n].shape[1]
            mine = whole[:, :cols]
            for d in range(1, N_DEV):
                mine = jnp.where(dev == d, whole[:, d * cols:(d + 1) * cols], mine)
            grads[n] = mine
        for i, n in enumerate(SMALL_ORDER):
            outs[4 * i][...] = grads[n]
            outs[4 * i + 1][...], outs[4 * i + 2][...], outs[4 * i + 3][...] = _adamw(
                w_refs[n][...], grads[n], m_refs[n][...], v_refs[n][...])
        outs[4 * k][...] = row[:, loss_off:loss_off + LANES]

    out = pl.pallas_call(
        body, name="adamw_small",
        out_shape=[jax.ShapeDtypeStruct(w[n].shape, F32) for n in SMALL_ORDER for _ in range(4)]
        + [jax.ShapeDtypeStruct((1, LANES), F32)],
        compiler_params=_params(0),
    )(*gathered, *[t[n] for t in (w, m, v) for n in SMALL_ORDER])
    return [dict(zip(SMALL_ORDER, out[j:4 * k:4])) for j in range(4)] + [out[4 * k]]


SMALL = ["norm1_w", "b_gate", "conv_a_b", "dt_bias", "a_log", "d_skip", "ssd_norm_w", "uv_b", "v_ln_w", "v_ln_b",
         "w_spatial", "b_spatial", "norm2_w", "conv_f_b", "final_norm_w"]
BIG = ["w_in", "w_branch", "w_out", "w_up", "w_down"]
TRANSPOSED = ("w_in", "w_up")
WEIGHTS = ["norm1_w", "w_in", "b_gate", "conv_a_w", "conv_a_b", "dt_bias", "a_log", "d_skip", "ssd_norm_w", "uv_b",
           "v_ln_w", "v_ln_b", "w_spatial", "b_spatial", "w_branch", "w_out", "norm2_w", "w_up", "conv_f_w",
           "conv_f_b", "w_down", "final_norm_w"]
IN_SPLITS = [("z", 0, 2048), ("xbc", 2048, 5120), ("dt", 5120, 5152), ("uv", 5152, 7200), ("gates", 7200, 9248)]


def _columns_from_devices(a):
    return a.transpose(1, 0, 2).reshape(a.shape[1], -1)


def kernel(x, norm1_w, w_in, b_gate, conv_a_w, conv_a_b, dt_bias, a_log, d_skip, ssd_norm_w, uv_b, v_ln_w, v_ln_b, w_spatial, b_spatial, w_branch, w_out, norm2_w, w_up, conv_f_w, conv_f_b, w_down, final_norm_w, loss_target, m_norm1_w, m_w_in, m_b_gate, m_conv_a_w, m_conv_a_b, m_dt_bias, m_a_log, m_d_skip, m_ssd_norm_w, m_uv_b, m_v_ln_w, m_v_ln_b, m_w_spatial, m_b_spatial, m_w_branch, m_w_out, m_norm2_w, m_w_up, m_conv_f_w, m_conv_f_b, m_w_down, m_final_norm_w, v_norm1_w, v_w_in, v_b_gate, v_conv_a_w, v_conv_a_b, v_dt_bias, v_a_log, v_d_skip, v_ssd_norm_w, v_uv_b, v_v_ln_w, v_v_ln_b, v_w_spatial, v_b_spatial, v_w_branch, v_w_out, v_norm2_w, v_w_up, v_conv_f_w, v_conv_f_b, v_w_down, v_final_norm_w):
    args = dict(locals())
    wts = {n: args[n] for n in WEIGHTS}
    mom = {n: args["m_" + n] for n in WEIGHTS}
    var = {n: args["v_" + n] for n in WEIGHTS}
    cx, cy, cc = _place()
    dev = 4 * cx + 2 * cy + cc
    place = jnp.stack([cc, 2 * cx + cy]).astype(jnp.int32)

    def shard2d(n, a):
        return a[0].T if n in TRANSPOSED else a[0]

    def unshard(n, b):
        return (b.T if n in TRANSPOSED else b)[None]

    g_in, g_conv_a, g_conv_f = _all_gather(
        [shard2d("w_in", w_in).astype(BF16), conv_a_w[0], conv_f_w[0]], "gather_w_in")
    late = [shard2d(n, wts[n]).astype(BF16) for n in BIG[1:]]
    send_sems, recv_sems, late, lands, token = _gather_start(late, g_in, "gather_late_start")
    w_in_rows = g_in.reshape(-1, D_MODEL)
    w = {name: w_in_rows[lo:hi] for name, lo, hi in IN_SPLITS}
    w["dt"] = jnp.pad(w["dt"], ((0, DT_PAD - SSD_HEADS), (0, 0)))
    w["conv_a"] = _columns_from_devices(g_conv_a)
    w["conv_f"] = _columns_from_devices(g_conv_f)

    def late_weights(*after):
        mine, got = _gather_wait(send_sems, recv_sems, late, lands, after, "gather_late_wait")
        g_branch, g_out, g_up, g_down = [lax.dynamic_update_index_in_dim(land, own, dev, 0).reshape(-1, D_MODEL)
                                         for land, own in zip(got, mine)]
        return {"branch_a": g_branch[:SSD_INNER], "branch_b": g_branch[SSD_INNER:], "out": g_out, "up": g_up,
                "down": g_down}

    in_flight = {}

    def on_grad(n, g):
        part = {"w_in": lambda: jnp.concatenate([g[name][:hi - lo] for name, lo, hi in IN_SPLITS], axis=0),
                "w_branch": lambda: jnp.concatenate([g["branch_a"], g["branch_b"]], axis=0),
                "w_out": lambda: g["out"], "w_up": lambda: g["up"], "w_down": lambda: g["down"]}[n]()
        part = part.reshape(N_DEV, -1, D_MODEL)
        send, recv, (part,), (land,), tok = _gather_start([part], None, f"to_owners_start_{n}", scatter=True)
        in_flight[n] = (part, send, recv, land)
        return tok

    p = {n: wts[n][0] if wts[n].ndim > 2 else wts[n].reshape(1, -1) for n in SMALL}
    small_flight = []

    def on_small(g, loss):
        arrays = [jnp.concatenate([g[n] for n in ROW_VECTORS] + [loss[:1]], axis=1), g["w_spatial"], g["b_spatial"],
                  g["conv_a"], g["conv_f"]]
        *flight, tok = _gather_start(arrays, g["conv_a"], "gather_small_start")
        small_flight.append(flight)
        return tok

    loss, gx, g = _local_step(x[0], loss_target[0], w, p, after=token, late_weights=late_weights, on_grad=on_grad,
                              on_small=on_small)
    *flight, _ = _gather_start([g["norm1_w"]], gx, "gather_norm1_start")
    small_flight.append(flight)

    grads, delta, new_m, new_v = {}, {}, {}, {}

    def big_adamw(n, after):
        view = (lambda a: a.transpose(2, 0, 1)) if n == "w_in" else (lambda a: shard2d(n, a))
        back = (lambda b: b.transpose(1, 2, 0)) if n == "w_in" else (lambda b: unshard(n, b))
        part, send, recv, land = in_flight[n]
        (part,), (got,) = _gather_wait(send, recv, [part], [land], [after], f"to_owners_wait_{n}", scatter=True)
        out = _sum8_adamw(part, got, place, *[view(t[n]) for t in (wts, mom, var)], f"adamw_{n}")
        grads[n], delta[n], new_m[n], new_v[n] = [back(o) for o in out]
        return out[1]

    after = gx
    for n in ("w_down", "w_up", "w_out", "w_branch"):
        after = big_adamw(n, after)
    gathered = []
    for (send, recv, mine, land), name in zip(small_flight, ("gather_small_wait", "gather_norm1_wait")):
        mine, got = _gather_wait(send, recv, mine, land, [after], name)
        gathered += [lax.dynamic_update_index_in_dim(full, own, dev, 0) for full, own in zip(got, mine)]
    small = [{n: t[n][0] if t[n].ndim > 2 else t[n].reshape(1, -1) for n in SMALL_ORDER} for t in (wts, mom, var)]
    *outs, loss = _small_adamw(gathered, *small)
    for tgt, out in zip((grads, delta, new_m, new_v), outs):
        tgt.update({n: out[n].reshape(wts[n].shape) for n in SMALL_ORDER})
    big_adamw("w_in", loss)
    loss = loss[0, 0]

    return (loss, gx[None], *[grads[n] for n in WEIGHTS], *[delta[n] for n in WEIGHTS],
            *[new_m[n] for n in WEIGHTS], *[new_v[n] for n in WEIGHTS])
```

```python
import functools

import jax
import jax.numpy as jnp
from jax import lax
from jax.experimental import pallas as pl
from jax.experimental.pallas import tpu as pltpu

F32, BF16 = jnp.float32, jnp.bfloat16
HIGHEST = lax.Precision.HIGHEST

D_MODEL = 1024
SSD_INNER = 2048
SSD_HEAD_DIM = 64
SSD_HEADS = 32
SSD_GROUPS = 4
SSD_STATE = 128
SSD_BC = SSD_GROUPS * SSD_STATE
SSD_XBC = SSD_INNER + 2 * SSD_BC
SSD_CONV = 4
CHUNK = 128
N_PAIRS = SSD_HEADS // 2
PAIRS_PER_GROUP = N_PAIRS // SSD_GROUPS
SGU_WIDTH = 1024
SGU_GROUPS = 8
D_FF = 2816
FFN_CONV = 3
NORM_EPS = 1e-6
LN_EPS = 1e-5
LANES = 128
DT_PAD = LANES

ADAM_LR, ADAM_B1, ADAM_B2, ADAM_EPS, ADAM_WD, ADAM_STEP = 0.001, 0.9, 0.999, 1e-08, 0.01, 10

N_DEV = 8
VMEM_LIMIT = 56 * 1024 * 1024
MESH = pl.DeviceIdType.MESH


def _params(n_grid, **kw):
    sem = dict(dimension_semantics=("arbitrary",) * n_grid) if n_grid else {}
    return pltpu.CompilerParams(vmem_limit_bytes=VMEM_LIMIT, **sem, **kw)


def _tile(n, pref):
    t = (min(pref, n) // LANES) * LANES
    while n % t:
        t -= LANES
    return t


def _row_tile(r, pref):
    for t in range(min(pref, r) // 16 * 16, 0, -16):
        if r % t == 0:
            return t
    return r


def _tile2d(r, c, rows):
    if r % 16 == 0:
        return _row_tile(r, rows), c
    return r, _tile(c, 2 * LANES)


def _rows(tm, n, nt=None, rev=False):
    if rev:
        return pl.BlockSpec((tm, n), lambda i: (nt - 1 - i, 0))
    return pl.BlockSpec((tm, n), lambda i: (i, 0))


def _halo(tm, n, rows=8):
    per = tm // rows
    return pl.BlockSpec((rows, n), lambda i: (jnp.maximum(i * per - 1, 0), 0))


def _full(shape):
    nd = len(shape)
    return pl.BlockSpec(shape, lambda *_: (0,) * nd)


def _rms(x, w, eps=NORM_EPS):
    return x * lax.rsqrt(jnp.mean(x * x, axis=-1, keepdims=True) + eps) * w


def _layer_norm(x, w, b):
    mu = jnp.mean(x, axis=-1, keepdims=True)
    var = jnp.mean(jnp.square(x - mu), axis=-1, keepdims=True)
    return (x - mu) * lax.rsqrt(var + LN_EPS) * w + b


def _sigmoid(x):
    return 1.0 / (1.0 + jnp.exp(-x))


def _silu(x):
    return x * _sigmoid(x)


def _dsilu(x):
    s = _sigmoid(x)
    return s * (1.0 + x * (1.0 - s))


def _silu_and_grad(x):
    s = _sigmoid(x)
    return x * s, s * (1.0 + x * (1.0 - s))


def _softplus(x):
    return jnp.maximum(x, 0.0) + jnp.log(1.0 + jnp.exp(-jnp.abs(x)))


def _gelu(x):
    return jax.nn.gelu(x)


def _dot(a, b):
    return jnp.dot(a, b, preferred_element_type=F32)


def _dot_nt(a, b):
    return lax.dot_general(a, b, (((1,), (1,)), ((), ())), preferred_element_type=F32)


def _dot_tn(a, b):
    return lax.dot_general(a, b, (((0,), (0,)), ((), ())), preferred_element_type=F32)


def _dot_split(p, e):
    hi = p.astype(BF16)
    lo = (p - hi.astype(F32)).astype(BF16)
    return _dot(hi, e) + _dot(lo, e)


def _colsum(x):
    return jnp.sum(x, axis=0, keepdims=True)


def _shift_down(x, halo, j):
    xs = pltpu.roll(x, j, 0)
    hs = pltpu.roll(halo, j, 0)
    r8 = lax.broadcasted_iota(jnp.int32, hs.shape, 0)
    return jnp.concatenate([jnp.where(r8 < j, hs, xs[:8]), xs[8:]], axis=0)


def _shift_up(x, nxt, j):
    n = x.shape[0]
    xs = pltpu.roll(x, n - j, 0)
    ns = pltpu.roll(nxt, 8 - j, 0)
    r8 = lax.broadcasted_iota(jnp.int32, ns.shape, 0)
    return jnp.concatenate([xs[:n - 8], jnp.where(r8 >= 8 - j, ns, xs[n - 8:])], axis=0)


def _causal_conv(x, halo, w, b):
    k = w.shape[0]
    y = b + w[k - 1:k, :] * x
    for j in range(1, k):
        y = y + w[k - 1 - j:k - j, :] * _shift_down(x, halo, j)
    return y


def _causal_conv_bwd(dy, nxt, x, w):
    k = w.shape[0]
    dx = w[k - 1:k, :] * dy
    dw = [_colsum(dy * x)]
    for j in range(1, k):
        dyj = _shift_up(dy, nxt, j)
        dx = dx + w[k - 1 - j:k - j, :] * dyj
        dw.append(_colsum(dyj * x))
    return dx, jnp.concatenate(dw[::-1], axis=0)


MM_TILE_PREF = 1408
MM_VMEM_BUDGET = 40 * 1024 * 1024


def _mm_tiles(m, n, k, out_bytes):
    tm, tn = _tile(m, MM_TILE_PREF), _tile(n, MM_TILE_PREF)
    need = lambda tm, tn: 2 * (2 * k * (tm + tn) + out_bytes * tm * tn)
    while need(tm, tn) > MM_VMEM_BUDGET:
        if tn >= tm and tn > LANES:
            tn = _tile(n, tn - LANES)
        else:
            tm = _tile(m, tm - LANES)
    return tm, tn


def _mm(a, b, dims, name, acc=None, out_dtype=F32, after=None):
    a_list, b_list = (list(a), list(b)) if isinstance(a, (list, tuple)) else ([a], [b])
    k_axis, m_axis = (0, 1) if dims == "tn" else (1, 0)
    m, ks = a_list[0].shape[m_axis], [x.shape[k_axis] for x in a_list]
    n = b_list[0].shape[0] if dims == "nt" else b_list[0].shape[1]
    tm, tn = _mm_tiles(m, n, sum(ks), 4 * (2 if acc is not None else 1))
    a_specs = [pl.BlockSpec((k, tm), lambda j, i: (0, i)) if dims == "tn" else pl.BlockSpec((tm, k), lambda j, i: (i, 0))
               for k in ks]
    b_specs = [pl.BlockSpec((tn, k), lambda j, i: (j, 0)) if dims == "nt" else pl.BlockSpec((k, tn), lambda j, i: (0, j))
               for k in ks]
    o_spec = pl.BlockSpec((tm, tn), lambda j, i: (i, j))
    dot = {"nn": _dot, "nt": _dot_nt, "tn": _dot_tn}[dims]
    n_pairs = len(ks)

    def body(*refs):
        rest = refs[2 * n_pairs:]
        r = dot(refs[0][...], refs[n_pairs][...])
        for i in range(1, n_pairs):
            r = r + dot(refs[i][...], refs[n_pairs + i][...])
        if acc is not None:
            r = r + rest[0][...]
        rest[-1][...] = r.astype(out_dtype)

    ins, specs = a_list + b_list, a_specs + b_specs
    if acc is not None:
        ins.append(acc)
        specs.append(o_spec)
    if after is not None:
        ins.append(after)
        specs.append(pl.BlockSpec(memory_space=pl.ANY))
    return pl.pallas_call(
        body, name=name, grid=(n // tn, m // tm), in_specs=specs, out_specs=o_spec,
        out_shape=jax.ShapeDtypeStruct((m, n), out_dtype), compiler_params=_params(2),
    )(*ins)


def _mm_rows(a, b, dims, name, fn, rows=(), fulls=(), row_outs=(), acc_outs=(), after=None):
    a_list, b_list = (list(a), list(b)) if isinstance(a, (list, tuple)) else ([a], [b])
    m, ks = a_list[0].shape[0], [x.shape[1] for x in a_list]
    n, k = (b_list[0].shape[0] if dims == "nt" else b_list[0].shape[1]), sum(ks)
    per_row = 2 * k + 8 * n + sum(4 * r.shape[1] for r in rows) + sum(c * jnp.dtype(d).itemsize for c, d in row_outs)
    tm = _tile(m, 1024)
    while 2 * tm * per_row + 4 * k * n > MM_VMEM_BUDGET:
        tm = _tile(m, tm - LANES)
    dot = _dot_nt if dims == "nt" else _dot
    n_pairs = len(ks)
    n_in = 2 * n_pairs + len(rows) + len(fulls) + (after is not None)

    def body(*refs):
        ins, outs = refs[:n_in], refs[n_in:]
        row_refs, acc_refs = outs[:len(row_outs)], outs[len(row_outs):]

        @pl.when(pl.program_id(0) == 0)
        def _():
            for r in acc_refs:
                r[...] = jnp.zeros_like(r)

        result = dot(ins[0][...], ins[n_pairs][...])
        for i in range(1, n_pairs):
            result = result + dot(ins[i][...], ins[n_pairs + i][...])
        new_rows, incs = fn(result, *[r[...] for r in ins[2 * n_pairs:2 * n_pairs + len(rows) + len(fulls)]])
        for r, val in zip(row_refs, new_rows):
            r[...] = val.astype(r.dtype)
        for r, inc in zip(acc_refs, incs):
            r[...] += inc

    extra, extra_specs = ([after], [pl.BlockSpec(memory_space=pl.ANY)]) if after is not None else ([], [])
    return pl.pallas_call(
        body, name=name, grid=(m // tm,),
        in_specs=[_rows(tm, k_i) for k_i in ks] + [_full(x.shape) for x in b_list]
        + [_rows(tm, r.shape[1]) for r in rows] + [_full(f.shape) for f in fulls] + extra_specs,
        out_specs=[_rows(tm, c) for c, _ in row_outs] + [_full(s) for s in acc_outs],
        out_shape=[jax.ShapeDtypeStruct((m, c), d) for c, d in row_outs]
        + [jax.ShapeDtypeStruct(s, F32) for s in acc_outs],
        compiler_params=_params(1),
    )(*a_list, *b_list, *rows, *fulls, *extra)


def _residual_norm(o, x, w):
    h = x + o
    return (h, _rms(h, w)), ()


def _norm_backward(dn, h, dres, w):
    _, vjp = jax.vjp(_rms, h, w)
    dh, dw = vjp(dn)
    dh = dh + dres
    return (dh, dh), (dw,)


def _loss_and_grad(dn, h1, target, w):
    yf, vjp = jax.vjp(_rms, h1 + dn, w)
    err = yf - target
    loss = 0.5 * jnp.sum(jnp.mean(err * err, axis=-1, keepdims=True))
    dh, dw = vjp(err * (1.0 / err.shape[-1]))
    return (dh, dh), (jnp.full((8, LANES), loss, F32), dw)


def _wgrad(a, d, name, after=None):
    return _mm(a, d, "tn", name, out_dtype=BF16, after=after)


def _norm_fwd(x, w, name, after=None, tm=512):
    t, d = x.shape

    def body(x_ref, w_ref, *rest):
        rest[-1][...] = _rms(x_ref[...], w_ref[...]).astype(BF16)

    extra, extra_specs = ([after], [_full(after.shape)]) if after is not None else ([], [])
    return pl.pallas_call(
        body, name=name, grid=(t // tm,), in_specs=[_rows(tm, d), _full((1, d))] + extra_specs,
        out_specs=_rows(tm, d), out_shape=jax.ShapeDtypeStruct((t, d), BF16), compiler_params=_params(1),
    )(x, w, *extra)


def _conv_a_fwd(xbc, cw, cb, tm=256):
    t, c = xbc.shape

    def body(x_ref, h_ref, w_ref, b_ref, o_ref, y_ref):
        halo = jnp.where(pl.program_id(0) > 0, h_ref[...].astype(F32)[8:], 0.0)
        y = _causal_conv(x_ref[...].astype(F32), halo, w_ref[...], b_ref[...])
        y_ref[...] = y.astype(BF16)
        o_ref[...] = _silu(y)

    return pl.pallas_call(
        body, name="conv_a_fwd", grid=(t // tm,),
        in_specs=[_rows(tm, c), _halo(tm, c, rows=16), _full(cw.shape), _full((1, c))],
        out_specs=[_rows(tm, c)] * 2,
        out_shape=[jax.ShapeDtypeStruct((t, c), F32), jax.ShapeDtypeStruct((t, c), BF16)], compiler_params=_params(1),
    )(xbc, xbc, cw, cb)


def _ssd_common(dtr, dtb, alog, e_t):
    row = lax.broadcasted_iota(jnp.int32, (CHUNK, CHUNK), 0)
    col = lax.broadcasted_iota(jnp.int32, (CHUNK, CHUNK), 1)
    causal = row >= col
    dt = _softplus(dtr + dtb)
    a = -jnp.exp(alog)
    acum = jnp.dot(causal.astype(F32), dt * a, precision=HIGHEST, preferred_element_type=F32)
    spread = lambda v: _dot(v.astype(BF16), e_t)
    elast = jnp.broadcast_to(jnp.exp(acum[CHUNK - 1:CHUNK, :]), (8, LANES))
    return dict(dt=dt, a=a, acum=acum, acum_t=acum.T, causal=causal, row=row, col=col, lane_lo=col < SSD_HEAD_DIM,
                dt_x=_dot_split(dt, e_t), ecol_x=spread(jnp.exp(acum)), elast_x=_dot_split(elast, e_t)[0:1],
                dsr_x=spread(jnp.exp(acum[CHUNK - 1:CHUNK, :] - acum)))


def _head_decay(c, h, transposed=False):
    d = c["acum"][:, h:h + 1] - c["acum_t"][h:h + 1, :]
    if transposed:
        return jnp.exp(jnp.where(c["row"] <= c["col"], -d, -jnp.inf))
    return jnp.exp(jnp.where(c["causal"], d, -jnp.inf))


def _ssd_fwd(xc, dtr, z, dtb, alog, dsk, nw, e_t):
    t = xc.shape[0]
    nc = t // CHUNK

    def body(xs_ref, b_ref, c_ref, dtr_ref, z_ref, dtb_ref, alog_ref, dsk_ref, nw_ref, et_ref,
             y_ref, ya_ref, sp_ref, s_scr):
        @pl.when(pl.program_id(0) == 0)
        def _():
            s_scr[...] = jnp.zeros_like(s_scr)

        c = _ssd_common(dtr_ref[...], dtb_ref[...], alog_ref[...], et_ref[...])
        lane_lo = c["lane_lo"]
        dsk = dsk_ref[...]
        for g in range(SSD_GROUPS):
            gs = slice(g * SSD_STATE, (g + 1) * SSD_STATE)
            bg_t, cg = b_ref[:, gs].T.astype(BF16), c_ref[:, gs].astype(BF16)
            cb = _dot(cg, bg_t)
            for pp in range(PAIRS_PER_GROUP):
                j = g * PAIRS_PER_GROUP + pp
                ps = slice(j * LANES, (j + 1) * LANES)
                x = xs_ref[:, ps]
                ecol, dsr = c["ecol_x"][:, ps], c["dsr_x"][:, ps]
                xdt = x * c["dt_x"][:, ps]
                xb = xdt.astype(BF16)
                zero = jnp.zeros_like(xb)
                yd = (_dot((cb * _head_decay(c, 2 * j)).astype(BF16), jnp.where(lane_lo, xb, zero))
                      + _dot((cb * _head_decay(c, 2 * j + 1)).astype(BF16), jnp.where(lane_lo, zero, xb)))
                sp = s_scr[j]
                yo = ecol * _dot(cg, sp.astype(BF16))
                st = _dot(bg_t, (xdt * dsr).astype(BF16))
                sp_ref[0, j] = sp
                s_scr[j] = c["elast_x"][:, ps] * sp + st
                dskp = jnp.where(lane_lo[0:1], dsk[:, 2 * j:2 * j + 1], dsk[:, 2 * j + 1:2 * j + 2])
                y_ref[:, ps] = yd + yo + dskp * x
        ya_ref[...] = _rms(y_ref[...] * _silu(z_ref[...].astype(F32)), nw_ref[...]).astype(BF16)

    ck = lambda n, col=0: pl.BlockSpec((CHUNK, n), lambda c: (c, col))
    return pl.pallas_call(
        body, name="ssd_fwd", grid=(nc,),
        in_specs=[ck(SSD_INNER), ck(SSD_BC, SSD_INNER // SSD_BC), ck(SSD_BC, SSD_INNER // SSD_BC + 1), ck(DT_PAD),
                  ck(SSD_INNER), _full((1, DT_PAD)), _full((1, DT_PAD)), _full((1, DT_PAD)),
                  _full((1, SSD_INNER)), _full(e_t.shape)],
        out_specs=[ck(SSD_INNER), ck(SSD_INNER),
                   pl.BlockSpec((1, N_PAIRS, SSD_STATE, LANES), lambda c: (c, 0, 0, 0))],
        out_shape=[jax.ShapeDtypeStruct((t, SSD_INNER), F32), jax.ShapeDtypeStruct((t, SSD_INNER), BF16),
                   jax.ShapeDtypeStruct((nc, N_PAIRS, SSD_STATE, LANES), F32)],
        scratch_shapes=[pltpu.VMEM((N_PAIRS, SSD_STATE, LANES), F32)], compiler_params=_params(1),
    )(xc, xc, xc, dtr, z, dtb, alog, dsk, nw, e_t)


def _ssd_bwd(dya, y, z, xc, dtr, sprev, dtb, alog, dsk, nw, e_heads, e_t):
    t = xc.shape[0]
    nc = t // CHUNK

    def body(dya_ref, y_ref, z_ref, xs_ref, b_ref, c_ref, dtr_ref, sp_ref, dtb_ref, alog_ref, dsk_ref, nw_ref, e_ref,
             et_ref, dz_ref, dxs_ref, db_ref, dc_ref, ddtr_ref, dnw_ref, ddtb_ref, dalog_ref, ddsk_ref, ds_scr):
        @pl.when(pl.program_id(0) == 0)
        def _():
            ds_scr[...] = jnp.zeros_like(ds_scr)
            for r in (dnw_ref, ddtb_ref, dalog_ref, ddsk_ref):
                r[...] = jnp.zeros_like(r)

        y = y_ref[...]
        _, gate_vjp = jax.vjp(lambda y_, z_, w_: _rms(y_ * _silu(z_), w_), y, z_ref[...].astype(F32), nw_ref[...])
        dy, dz, dnw = gate_vjp(dya_ref[...])
        dz_ref[...] = dz.astype(BF16)
        dnw_ref[...] += dnw

        dtr = dtr_ref[...]
        c = _ssd_common(dtr, dtb_ref[...], alog_ref[...], et_ref[...])
        dt, a, lane_lo, row, col = c["dt"], c["a"], c["lane_lo"], c["row"], c["col"]
        dsk = dsk_ref[...]
        p_a, p_dt, v_last = [], [], []
        da_cols = jnp.zeros((CHUNK, CHUNK), F32)
        da_rows = jnp.zeros((CHUNK, CHUNK), F32)
        for g in range(SSD_GROUPS):
            gs = slice(g * SSD_STATE, (g + 1) * SSD_STATE)
            bg, cg = b_ref[:, gs].astype(BF16), c_ref[:, gs].astype(BF16)
            bg_t, cg_t = b_ref[:, gs].T.astype(BF16), c_ref[:, gs].T.astype(BF16)
            cb, cb_t = _dot(cg, bg_t), _dot(bg, cg_t)
            dcb = jnp.zeros((CHUNK, CHUNK), F32)
            dbg = jnp.zeros((CHUNK, SSD_STATE), F32)
            dcg = jnp.zeros((CHUNK, SSD_STATE), F32)
            for pp in range(PAIRS_PER_GROUP):
                j = g * PAIRS_PER_GROUP + pp
                ps = slice(j * LANES, (j + 1) * LANES)
                x = xs_ref[:, ps]
                dtp, ecol, dsr = c["dt_x"][:, ps], c["ecol_x"][:, ps], c["dsr_x"][:, ps]
                elast = c["elast_x"][:, ps]
                xdt = x * dtp
                xb = xdt.astype(BF16)
                dskp = jnp.where(lane_lo[0:1], dsk[:, 2 * j:2 * j + 1], dsk[:, 2 * j + 1:2 * j + 2])
                dyp = dy[:, ps]
                dyb = dyp.astype(BF16)
                sp, dsn = sp_ref[0, j], ds_scr[j]
                spb, dsnb = sp.astype(BF16), dsn.astype(BF16)
                y_off = ecol * _dot(cg, spb)
                dw = (dyp * ecol).astype(BF16)
                dcg = dcg + _dot_nt(dw, spb)
                dsp = _dot(cg_t, dw) + elast * dsn
                xd = xdt * dsr
                zd = _dot(bg, dsnb) * dsr
                dbg = dbg + _dot_nt(xd.astype(BF16), dsnb)
                dxdt = zd
                zero = jnp.zeros_like(xb)
                for h, lm in ((2 * j, lane_lo), (2 * j + 1, jnp.logical_not(lane_lo))):
                    le = _head_decay(c, h)
                    dm = _dot_nt(jnp.where(lm, dyb, zero), jnp.where(lm, xb, zero))
                    dcb = dcb + dm * le
                    m = cb * le
                    m_t = (cb_t * _head_decay(c, h, transposed=True)).astype(BF16)
                    dxdt = dxdt + jnp.where(lm, _dot(m_t, dyb), 0.0)
                    q = dm * m
                    da_cols = da_cols + jnp.where(col == h, jnp.sum(q, axis=1, keepdims=True), 0.0)
                    da_rows = da_rows + jnp.where(row == h, _colsum(q), 0.0)
                ds_scr[j] = dsp
                dxs_ref[:, ps] = dxdt * dtp + dskp * dyp
                p_a.append(dyp * y_off - xdt * zd)
                p_dt.append(dxdt * x)
                v_last.append(_colsum(zd * xdt) + elast * _colsum(dsn * sp))
            dcbb = dcb.astype(BF16)
            db_ref[:, gs] = dbg + _dot_tn(dcbb, cg)
            dc_ref[:, gs] = dcg + _dot(dcbb, bg)
        e = e_ref[...]
        rows8 = jnp.concatenate([jnp.concatenate(v_last, axis=1), _colsum(dy * xs_ref[...]),
                                 jnp.zeros((6, SSD_INNER), F32)], axis=0)
        r8 = _dot_split(rows8, e)
        da = (_dot_split(jnp.concatenate(p_a, axis=1), e) + jnp.where(row == CHUNK - 1, r8[0:1], 0.0)
              + da_cols - da_rows.T)
        ddsk_ref[...] += r8[1:2]
        dadt = jnp.dot((row <= col).astype(F32), da, precision=HIGHEST, preferred_element_type=F32)
        ddt = dadt * a + _dot_split(jnp.concatenate(p_dt, axis=1), e)
        dalog_ref[...] += _colsum(dadt * dt) * a
        ddtr = ddt * _sigmoid(dtr + dtb_ref[...])
        ddtr_ref[...] = ddtr
        ddtb_ref[...] += _colsum(ddtr)

    ck = lambda n, col=0: pl.BlockSpec((CHUNK, n), lambda c: (nc - 1 - c, col))
    acc = lambda n: _full((1, n))
    return pl.pallas_call(
        body, name="ssd_bwd", grid=(nc,),
        in_specs=[ck(SSD_INNER), ck(SSD_INNER), ck(SSD_INNER), ck(SSD_INNER), ck(SSD_BC, SSD_INNER // SSD_BC),
                  ck(SSD_BC, SSD_INNER // SSD_BC + 1), ck(DT_PAD),
                  pl.BlockSpec((1, N_PAIRS, SSD_STATE, LANES), lambda c: (nc - 1 - c, 0, 0, 0)),
                  acc(DT_PAD), acc(DT_PAD), acc(DT_PAD), acc(SSD_INNER), _full((SSD_INNER, LANES)),
                  _full((LANES, SSD_INNER))],
        out_specs=[ck(SSD_INNER), ck(SSD_INNER), ck(SSD_BC), ck(SSD_BC), ck(DT_PAD),
                   acc(SSD_INNER), acc(DT_PAD), acc(DT_PAD), acc(DT_PAD)],
        out_shape=[jax.ShapeDtypeStruct((t, SSD_INNER), BF16), jax.ShapeDtypeStruct((t, SSD_INNER), F32),
                   jax.ShapeDtypeStruct((t, SSD_BC), F32), jax.ShapeDtypeStruct((t, SSD_BC), F32),
                   jax.ShapeDtypeStruct((t, DT_PAD), F32), jax.ShapeDtypeStruct((1, SSD_INNER), F32),
                   jax.ShapeDtypeStruct((1, DT_PAD), F32), jax.ShapeDtypeStruct((1, DT_PAD), F32),
                   jax.ShapeDtypeStruct((1, DT_PAD), F32)],
        scratch_shapes=[pltpu.VMEM((N_PAIRS, SSD_STATE, LANES), F32)], compiler_params=_params(1),
    )(dya, y, z, xc, xc, xc, dtr, sprev, dtb, alog, dsk, nw, e_heads, e_t)


def _sgu_act(uv, uvb, lnw, lnb):
    a = _gelu(uv + uvb)
    return a[:, :SGU_WIDTH], _layer_norm(a[:, SGU_WIDTH:], lnw, lnb)


def _sgu_weights(ws_ref):
    row = lax.broadcasted_iota(jnp.int32, (CHUNK, CHUNK), 0)
    col = lax.broadcasted_iota(jnp.int32, (CHUNK, CHUNK), 1)
    return [jnp.where(row >= col, ws_ref[g], 0.0).astype(BF16) for g in range(SGU_GROUPS)], row >= col


def _sgu_fwd(uv, uvb, lnw, lnb, ws, bs_t):
    t = uv.shape[0]

    def body(uv_ref, uvb_ref, lnw_ref, lnb_ref, ws_ref, bs_ref, o_ref):
        u, vn = _sgu_act(uv_ref[...].astype(F32), uvb_ref[...], lnw_ref[...], lnb_ref[...])
        wc, _ = _sgu_weights(ws_ref)
        bs = bs_ref[...]
        for g in range(SGU_GROUPS):
            gs = slice(g * LANES, (g + 1) * LANES)
            mixed = _dot(wc[g], vn[:, gs].astype(BF16)) + bs[:, g:g + 1]
            o_ref[:, gs] = (u[:, gs] * mixed).astype(BF16)

    return pl.pallas_call(
        body, name="sgu_fwd", grid=(t // CHUNK,),
        in_specs=[_rows(CHUNK, 2 * SGU_WIDTH), _full((1, 2 * SGU_WIDTH)), _full((1, SGU_WIDTH)), _full((1, SGU_WIDTH)),
                  _full(ws.shape), _full(bs_t.shape)],
        out_specs=_rows(CHUNK, SGU_WIDTH), out_shape=jax.ShapeDtypeStruct((t, SGU_WIDTH), BF16),
        compiler_params=_params(1),
    )(uv, uvb, lnw, lnb, ws, bs_t)


def _sgu_bwd(dyb, uv, uvb, lnw, lnb, ws, bs_t, e_groups):
    t = uv.shape[0]

    def body(dyb_ref, uv_ref, uvb_ref, lnw_ref, lnb_ref, ws_ref, bs_ref, e_ref,
             duv_ref, duvb_ref, dlnw_ref, dlnb_ref, dws_ref, dbs_ref):
        @pl.when(pl.program_id(0) == 0)
        def _():
            for r in (duvb_ref, dlnw_ref, dlnb_ref, dws_ref, dbs_ref):
                r[...] = jnp.zeros_like(r)

        (u, vn), act_vjp = jax.vjp(_sgu_act, uv_ref[...].astype(F32), uvb_ref[...], lnw_ref[...], lnb_ref[...])
        wc, causal = _sgu_weights(ws_ref)
        bs = bs_ref[...]
        dyb = dyb_ref[...]
        du, dvn, dmix = [], [], []
        for g in range(SGU_GROUPS):
            gs = slice(g * LANES, (g + 1) * LANES)
            vb = vn[:, gs].astype(BF16)
            mixed = _dot(wc[g], vb) + bs[:, g:g + 1]
            dm = dyb[:, gs] * u[:, gs]
            dmb = dm.astype(BF16)
            du.append(dyb[:, gs] * mixed)
            dvn.append(_dot_tn(wc[g], dmb))
            dws_ref[g] += jnp.where(causal, _dot_nt(dmb, vb), 0.0)
            dmix.append(dm)
        dbs_ref[...] += _dot_split(jnp.concatenate(dmix, axis=1), e_ref[...])
        duv, duvb, dlnw, dlnb = act_vjp((jnp.concatenate(du, axis=1), jnp.concatenate(dvn, axis=1)))
        duv_ref[...] = duv.astype(BF16)
        duvb_ref[...] += duvb
        dlnw_ref[...] += dlnw
        dlnb_ref[...] += dlnb

    return pl.pallas_call(
        body, name="sgu_bwd", grid=(t // CHUNK,),
        in_specs=[_rows(CHUNK, SGU_WIDTH), _rows(CHUNK, 2 * SGU_WIDTH), _full((1, 2 * SGU_WIDTH)),
                  _full((1, SGU_WIDTH)), _full((1, SGU_WIDTH)), _full(ws.shape), _full(bs_t.shape),
                  _full(e_groups.shape)],
        out_specs=[_rows(CHUNK, 2 * SGU_WIDTH), _full((1, 2 * SGU_WIDTH)), _full((1, SGU_WIDTH)),
                   _full((1, SGU_WIDTH)), _full(ws.shape), _full(bs_t.shape)],
        out_shape=[jax.ShapeDtypeStruct((t, 2 * SGU_WIDTH), BF16), jax.ShapeDtypeStruct((1, 2 * SGU_WIDTH), F32),
                   jax.ShapeDtypeStruct((1, SGU_WIDTH), F32), jax.ShapeDtypeStruct((1, SGU_WIDTH), F32),
                   jax.ShapeDtypeStruct(ws.shape, F32), jax.ShapeDtypeStruct(bs_t.shape, F32)],
        compiler_params=_params(1),
    )(dyb, uv, uvb, lnw, lnb, ws, bs_t, e_groups)


def _merge(gates, pa, pb, bg):
    s = _sigmoid(gates + bg)
    return s[:, :D_MODEL] * pa + s[:, D_MODEL:] * pb


def _merge_fwd(gates, pa, pb, bg, tm=256):
    t = gates.shape[0]

    def body(g_ref, pa_ref, pb_ref, bg_ref, o_ref):
        o_ref[...] = _merge(g_ref[...].astype(F32), pa_ref[...].astype(F32), pb_ref[...].astype(F32),
                            bg_ref[...]).astype(BF16)

    return pl.pallas_call(
        body, name="merge_fwd", grid=(t // tm,),
        in_specs=[_rows(tm, 2 * D_MODEL), _rows(tm, D_MODEL), _rows(tm, D_MODEL), _full((1, 2 * D_MODEL))],
        out_specs=_rows(tm, D_MODEL), out_shape=jax.ShapeDtypeStruct((t, D_MODEL), BF16), compiler_params=_params(1),
    )(gates, pa, pb, bg)


def _merge_backward(dmix, gates, pa, pb, bg):
    _, vjp = jax.vjp(_merge, gates.astype(F32), pa.astype(F32), pb.astype(F32), bg)
    dg, dpa, dpb, dbg = vjp(dmix)
    return (dg, dpa, dpb), (dbg,)


def _conv_f_fwd(up, cw, cb, tm=128):
    t, c = up.shape

    def body(x_ref, h_ref, w_ref, b_ref, o_ref, y_ref):
        halo = jnp.where(pl.program_id(0) > 0, h_ref[...].astype(F32)[8:], 0.0)
        y = _causal_conv(x_ref[...].astype(F32), halo, w_ref[...], b_ref[...])
        y_ref[...] = y.astype(BF16)
        o_ref[...] = (_silu(y[:, :D_FF]) * y[:, D_FF:]).astype(BF16)

    return pl.pallas_call(
        body, name="conv_f_fwd", grid=(t // tm,),
        in_specs=[_rows(tm, c), _halo(tm, c, rows=16), _full(cw.shape), _full((1, c))],
        out_specs=[_rows(tm, D_FF), _rows(tm, c)],
        out_shape=[jax.ShapeDtypeStruct((t, D_FF), BF16), jax.ShapeDtypeStruct((t, c), BF16)],
        compiler_params=_params(1),
    )(up, up, cw, cb)


def _conv_f_bwd(dact, y, up, cw, tm=128):
    t, c = up.shape
    nt = t // tm

    def body(d_ref, y_ref, x_ref, w_ref, dx_ref, dw_ref, db_ref, nxt_scr):
        @pl.when(pl.program_id(0) == 0)
        def _():
            nxt_scr[...] = jnp.zeros_like(nxt_scr)
            dw_ref[...] = jnp.zeros_like(dw_ref)
            db_ref[...] = jnp.zeros_like(db_ref)

        a, v = y_ref[:, :D_FF].astype(F32), y_ref[:, D_FF:].astype(F32)
        d = d_ref[...].astype(F32)
        silu_a, dsilu_a = _silu_and_grad(a)
        dy = jnp.concatenate([d * v * dsilu_a, d * silu_a], axis=1)
        dx, dw = _causal_conv_bwd(dy, nxt_scr[...], x_ref[...].astype(F32), w_ref[...])
        dx_ref[...] = dx.astype(BF16)
        nxt_scr[...] = dy[:8]
        dw_ref[...] += dw
        db_ref[...] += _colsum(dy)

    return pl.pallas_call(
        body, name="conv_f_bwd", grid=(nt,),
        in_specs=[_rows(tm, D_FF, nt, True), _rows(tm, c, nt, True), _rows(tm, c, nt, True), _full(cw.shape)],
        out_specs=[_rows(tm, c, nt, True), _full(cw.shape), _full((1, c))],
        out_shape=[jax.ShapeDtypeStruct((t, c), BF16), jax.ShapeDtypeStruct(cw.shape, F32),
                   jax.ShapeDtypeStruct((1, c), F32)],
        scratch_shapes=[pltpu.VMEM((8, c), F32)], compiler_params=_params(1),
    )(dact, y, up, cw)


def _conv_a_bwd(dxs, db, dc, y, xbc, cw, tm=256):
    t, c = xbc.shape
    nt = t // tm

    def body(dxs_ref, db_ref, dc_ref, y_ref, x_ref, w_ref, dx_ref, dw_ref, dbias_ref, nxt_scr):
        @pl.when(pl.program_id(0) == 0)
        def _():
            nxt_scr[...] = jnp.zeros_like(nxt_scr)
            dw_ref[...] = jnp.zeros_like(dw_ref)
            dbias_ref[...] = jnp.zeros_like(dbias_ref)

        dy = jnp.concatenate([dxs_ref[...], db_ref[...], dc_ref[...]], axis=1) * _dsilu(y_ref[...].astype(F32))
        dx, dw = _causal_conv_bwd(dy, nxt_scr[...], x_ref[...].astype(F32), w_ref[...])
        dx_ref[...] = dx.astype(BF16)
        nxt_scr[...] = dy[:8]
        dw_ref[...] += dw
        dbias_ref[...] += _colsum(dy)

    return pl.pallas_call(
        body, name="conv_a_bwd", grid=(nt,),
        in_specs=[_rows(tm, SSD_INNER, nt, True), _rows(tm, SSD_BC, nt, True), _rows(tm, SSD_BC, nt, True),
                  _rows(tm, c, nt, True), _rows(tm, c, nt, True), _full(cw.shape)],
        out_specs=[_rows(tm, c, nt, True), _full(cw.shape), _full((1, c))],
        out_shape=[jax.ShapeDtypeStruct((t, c), BF16), jax.ShapeDtypeStruct(cw.shape, F32),
                   jax.ShapeDtypeStruct((1, c), F32)],
        scratch_shapes=[pltpu.VMEM((8, c), F32)], compiler_params=_params(1),
    )(dxs, db, dc, y, xbc, cw)


def _pad_lanes(v, n=DT_PAD):
    return jnp.pad(v, ((0, 0), (0, n - v.shape[1])))


def _local_step(x, target, w, p, after=None, late_weights=None, on_grad=None, on_small=None):
    dtb, alog, dsk = _pad_lanes(p["dt_bias"]), _pad_lanes(p["a_log"]), _pad_lanes(p["d_skip"])
    bs_t = _pad_lanes(p["b_spatial"].T)
    e_heads = (jnp.arange(SSD_INNER)[:, None] // SSD_HEAD_DIM == jnp.arange(LANES)[None, :]).astype(BF16)
    e_heads_t = (jnp.arange(LANES)[:, None] == jnp.arange(SSD_INNER)[None, :] // SSD_HEAD_DIM).astype(BF16)
    e_groups = (jnp.arange(SGU_WIDTH)[:, None] // LANES == jnp.arange(LANES)[None, :]).astype(BF16)

    n1 = _norm_fwd(x, p["norm1_w"], "norm1_fwd", after=after)
    z = _mm(n1, w["z"], "nt", "proj_z", out_dtype=BF16)
    xbc = _mm(n1, w["xbc"], "nt", "proj_xbc", out_dtype=BF16)
    dtr = _mm(n1, w["dt"], "nt", "proj_dt")
    uv = _mm(n1, w["uv"], "nt", "proj_uv", out_dtype=BF16)
    gates = _mm(n1, w["gates"], "nt", "proj_gates", out_dtype=BF16)
    xc, conv_a_out = _conv_a_fwd(xbc, w["conv_a"], p["conv_a_b"])
    y, ya, sprev = _ssd_fwd(xc, dtr, z, dtb, alog, dsk, p["ssd_norm_w"], e_heads_t)
    yb = _sgu_fwd(uv, p["uv_b"], p["v_ln_w"], p["v_ln_b"], p["w_spatial"], bs_t)
    if late_weights is not None:
        w = {**w, **late_weights(ya, yb)}
    pa = _mm(ya, w["branch_a"], "nn", "branch_a", out_dtype=BF16)
    pb = _mm(yb, w["branch_b"], "nn", "branch_b", out_dtype=BF16)
    mix = _merge_fwd(gates, pa, pb, p["b_gate"])
    wide = [(D_MODEL, F32), (D_MODEL, BF16)]
    h1, n2 = _mm_rows(mix, w["out"], "nn", "out_proj", _residual_norm, rows=[x], fulls=[p["norm2_w"]], row_outs=wide)
    up = _mm(n2, w["up"], "nt", "up_proj", out_dtype=BF16)
    act, conv_f_out = _conv_f_fwd(up, w["conv_f"], p["conv_f_b"])
    dh2, dh2b, loss, g_final = _mm_rows(
        act, w["down"], "nn", "down_proj", _loss_and_grad, rows=[h1, target], fulls=[p["final_norm_w"]],
        row_outs=wide, acc_outs=[(8, LANES), (1, D_MODEL)])

    on_grad = on_grad or (lambda name, grads: None)
    g = {"final_norm_w": g_final}
    g["down"] = _wgrad(act, dh2b, "down_wgrad")
    tok = on_grad("w_down", g)
    dact = _mm(dh2b, w["down"], "nt", "down_dgrad", out_dtype=BF16, after=tok)
    dup, g["conv_f"], g["conv_f_b"] = _conv_f_bwd(dact, conv_f_out, up, w["conv_f"])
    g["up"] = _wgrad(dup, n2, "up_wgrad")
    tok = on_grad("w_up", g)
    dh1, dh1b, g["norm2_w"] = _mm_rows(
        dup, w["up"], "nn", "up_dgrad", _norm_backward, rows=[h1, dh2], fulls=[p["norm2_w"]], row_outs=wide,
        acc_outs=[(1, D_MODEL)], after=tok)
    g["out"] = _wgrad(mix, dh1b, "out_wgrad")
    tok = on_grad("w_out", g)
    dgates, dpa, dpb, g["b_gate"] = _mm_rows(
        dh1b, w["out"], "nt", "out_dgrad", _merge_backward, rows=[gates, pa, pb], fulls=[p["b_gate"]],
        row_outs=[(2 * D_MODEL, BF16), (D_MODEL, BF16), (D_MODEL, BF16)], acc_outs=[(1, 2 * D_MODEL)], after=tok)
    g["branch_a"] = _wgrad(ya, dpa, "branch_a_wgrad")
    g["branch_b"] = _wgrad(yb, dpb, "branch_b_wgrad")
    tok = on_grad("w_branch", g)
    dya = _mm(dpa, w["branch_a"], "nt", "branch_a_dgrad", after=tok)
    dyb = _mm(dpb, w["branch_b"], "nt", "branch_b_dgrad", after=tok)
    duv, g["uv_b"], g["v_ln_w"], g["v_ln_b"], g["w_spatial"], dbs_t = _sgu_bwd(
        dyb, uv, p["uv_b"], p["v_ln_w"], p["v_ln_b"], p["w_spatial"], bs_t, e_groups)
    g["b_spatial"] = dbs_t[:, :SGU_GROUPS].T
    dz, dxs, db, dc, ddtr, g["ssd_norm_w"], ddtb, dalog, ddsk = _ssd_bwd(
        dya, y, z, xc, dtr, sprev, dtb, alog, dsk, p["ssd_norm_w"], e_heads, e_heads_t)
    g["dt_bias"], g["a_log"], g["d_skip"] = ddtb, dalog, ddsk
    dxbc, g["conv_a"], g["conv_a_b"] = _conv_a_bwd(dxs, db, dc, conv_a_out, xbc, w["conv_a"])
    tok = on_small(g, loss) if on_small else None
    ddtrb = ddtr.astype(BF16)
    for name, d in (("z", dz), ("xbc", dxbc), ("dt", ddtrb), ("uv", duv), ("gates", dgates)):
        g[name] = _wgrad(d, n1, name + "_wgrad", after=tok)
    tok = on_grad("w_in", g)
    dn1 = _mm([dz, dxbc], [w["z"], w["xbc"]], "nn", "ssd_dgrad", after=tok)
    gx, g["norm1_w"] = _mm_rows(
        [duv, dgates, ddtrb], [w["uv"], w["gates"], w["dt"]], "nn", "in_dgrad",
        lambda r, so_far, h, dres, w_: tuple(t[:1] for t in _norm_backward(r + so_far, h, dres, w_)),
        rows=[dn1, x, dh1], fulls=[p["norm1_w"]], row_outs=wide[:1], acc_outs=[(1, D_MODEL)])
    return loss, gx, g


def _place():
    return lax.axis_index("x"), lax.axis_index("y"), lax.axis_index("c")


def _other_chips(x, y):
    return [(1 - x, y), (x, 1 - y), (1 - x, 1 - y)]


def _all_gather(shards, name):
    n = len(shards)

    def body(*refs):
        ins, outs = refs[:n], refs[n:2 * n]
        send_sems, recv_sems, local_sems = refs[2 * n:]
        x, y, c = _place()
        me, sibling = (x, y, c), (x, y, 1 - c)
        chips = _other_chips(x, y)

        def copy(a, k, block, to, src=None):
            slot = outs[a].at[4 * block[0] + 2 * block[1] + block[2]]
            return pltpu.make_async_remote_copy(
                src_ref=slot if src is None else src, dst_ref=slot, send_sem=send_sems.at[7 * a + k],
                recv_sem=recv_sems.at[7 * a + k], device_id=to, device_id_type=MESH)

        started = []
        for a in range(n):
            mine = pltpu.make_async_copy(ins[a], outs[a].at[4 * x + 2 * y + c], local_sems.at[a])
            mine.start()
            started.append(mine)
        sends = []
        for a in range(n):
            sends.append(copy(a, 0, me, sibling, src=ins[a]))
            sends += [copy(a, 1 + j, me, (*chip, c), src=ins[a]) for j, chip in enumerate(chips)]
        for cp in sends:
            cp.start()
        for a in range(n):
            for j, chip in enumerate(chips):
                copy(a, 1 + j, (*chip, c), me).wait_recv()
                fwd = copy(a, 4 + j, (*chip, c), sibling)
                fwd.start()
                sends.append(fwd)
        for a in range(n):
            copy(a, 0, sibling, me).wait_recv()
            for j, chip in enumerate(chips):
                copy(a, 4 + j, (*chip, 1 - c), me).wait_recv()
        for cp in sends:
            cp.wait_send()
        for mine in started:
            mine.wait()

    any_spec = pl.BlockSpec(memory_space=pl.ANY)
    return pl.pallas_call(
        body, name=name, in_specs=[any_spec] * n, out_specs=[any_spec] * n,
        out_shape=[jax.ShapeDtypeStruct((N_DEV, *s.shape), s.dtype) for s in shards],
        scratch_shapes=[pltpu.SemaphoreType.DMA((7 * n,)), pltpu.SemaphoreType.DMA((7 * n,)),
                        pltpu.SemaphoreType.DMA((n,))],
    )(*shards)


HBM_SPEC = pl.BlockSpec(memory_space=pltpu.HBM)
SEM_SPEC = pl.BlockSpec(memory_space=pltpu.SEMAPHORE)
ANY_SPEC = pl.BlockSpec(memory_space=pl.ANY)
DATAFLOW = pltpu.SideEffectType.DATAFLOW_SIDE_EFFECTING
N_PEERS = N_DEV - 1


def _peers(x, y, c):
    out = []
    for r in range(1, N_DEV):
        fx, fy, fc = r >> 2 & 1, r >> 1 & 1, r & 1
        out.append(((1 - x) if fx else x, (1 - y) if fy else y, (1 - c) if fc else c))
    return out


def _gather_copies(srcs, lands, send_sems, recv_sems, sending, scatter=False):
    x, y, c = _place()
    copies = []
    for a, (src, land) in enumerate(zip(srcs, lands)):
        for j, (px, py, pc) in enumerate(_peers(x, y, c)):
            mine, theirs = 4 * x + 2 * y + c, 4 * px + 2 * py + pc
            block = src.at[theirs if sending else 0] if scatter else src
            copies.append(pltpu.make_async_remote_copy(
                src_ref=block, dst_ref=land.at[mine if sending else theirs], send_sem=send_sems.at[N_PEERS * a + j],
                recv_sem=recv_sems.at[N_PEERS * a + j], device_id=(px, py, pc), device_id_type=MESH))
    return copies


def _gather_start(shards, after, name, scatter=False):
    n = len(shards)
    after = [] if after is None else [after]

    def body(*refs):
        srcs, lands = refs[:n], refs[n:2 * n]
        send_sems, recv_sems = refs[2 * n + len(after):2 * n + len(after) + 2]
        token = refs[-1]
        for cp in _gather_copies(srcs, lands, send_sems, recv_sems, sending=True, scatter=scatter):
            cp.start()
        token[...] = jnp.zeros_like(token)

    lands = [lax.empty(s.shape if scatter else (N_DEV, *s.shape), s.dtype) for s in shards]
    hbm = lambda a: pltpu.with_memory_space_constraint(a, pltpu.HBM)
    out = pl.pallas_call(
        body, name=name,
        out_shape=(pltpu.SemaphoreType.DMA((N_PEERS * n,)), pltpu.SemaphoreType.DMA((N_PEERS * n,)),
                   *[pltpu.HBM(a.shape, a.dtype) for a in (*shards, *lands)], jax.ShapeDtypeStruct((8, LANES), F32)),
        in_specs=[HBM_SPEC] * (2 * n) + [ANY_SPEC] * len(after),
        out_specs=(SEM_SPEC, SEM_SPEC, *[HBM_SPEC] * (2 * n), pl.BlockSpec(memory_space=pltpu.VMEM)),
        input_output_aliases={i: 2 + i for i in range(2 * n)},
        compiler_params=pltpu.CompilerParams(has_side_effects=DATAFLOW),
    )(*[hbm(a) for a in (*shards, *lands)], *after)
    return out[0], out[1], out[2:2 + n], out[2 + n:2 + 2 * n], out[-1]


def _gather_wait(send_sems, recv_sems, shards, lands, after, name, scatter=False):
    n = len(shards)
    after = tuple(after)

    def body(*refs):
        srcs, lands_ = refs[:n], refs[n:2 * n]
        send, recv = refs[2 * n:2 * n + 2]
        for cp in _gather_copies(srcs, lands_, send, recv, sending=False, scatter=scatter):
            cp.wait_send()
            cp.wait_recv()

    out = pl.pallas_call(
        body, name=name, out_shape=tuple(pltpu.HBM(a.shape, a.dtype) for a in (*shards, *lands)),
        in_specs=[HBM_SPEC] * (2 * n) + [SEM_SPEC, SEM_SPEC] + [ANY_SPEC] * len(after),
        out_specs=tuple([HBM_SPEC] * (2 * n)), input_output_aliases={i: i for i in range(2 * n)},
        compiler_params=pltpu.CompilerParams(has_side_effects=DATAFLOW),
    )(*shards, *lands, send_sems, recv_sems, *after)
    return out[:n], out[n:]


def _adamw(w, g, m, v):
    m = ADAM_B1 * m + (1.0 - ADAM_B1) * g
    v = ADAM_B2 * v + (1.0 - ADAM_B2) * jnp.square(g)
    m_hat = m / (1.0 - ADAM_B1 ** ADAM_STEP)
    v_hat = v / (1.0 - ADAM_B2 ** ADAM_STEP)
    return -ADAM_LR * (m_hat / (jnp.sqrt(v_hat) + ADAM_EPS) + ADAM_WD * w), m, v


def _sum8_adamw(part, got, place, w, m, v, name, tr=256):
    _, r, c = part.shape
    if w.ndim == 3:
        tr, tc = r, 2 * LANES
        blk = pl.BlockSpec((tr, 1, tc), lambda i, j, pr: (i, 0, j))
    else:
        tr, tc = _tile2d(r, c, tr)
        blk = pl.BlockSpec((tr, tc), lambda i, j, pr: (i, j))

    def body(place_ref, own_ref, got_ref, w_ref, m_ref, v_ref, g_ref, d_ref, nm_ref, nv_ref):
        dev = 2 * place_ref[1] + place_ref[0]
        g = jnp.zeros((tr, tc), F32)
        for d in range(N_DEV):
            g = g + jnp.where(dev == d, own_ref[0], got_ref[d]).astype(F32)
        two_d = lambda ref: ref[...].reshape(tr, tc)
        delta, nm, nv = _adamw(two_d(w_ref), g, two_d(m_ref), two_d(v_ref))
        for ref, val in ((g_ref, g), (d_ref, delta), (nm_ref, nm), (nv_ref, nv)):
            ref[...] = val.reshape(ref.shape)

    grid_spec = pltpu.PrefetchScalarGridSpec(
        num_scalar_prefetch=1, grid=(r // tr, c // tc),
        in_specs=[pl.BlockSpec((1, tr, tc), lambda i, j, pr: (2 * pr[1] + pr[0], i, j)),
                  pl.BlockSpec((N_DEV, tr, tc), lambda i, j, pr: (0, i, j)), blk, blk, blk],
        out_specs=[blk] * 4)
    return pl.pallas_call(
        body, name=name, grid_spec=grid_spec, out_shape=[jax.ShapeDtypeStruct(w.shape, F32)] * 4,
        compiler_params=_params(2),
    )(place, part, got, w, m, v)


VECTORS = ["norm1_w", "b_gate", "conv_a_b", "dt_bias", "a_log", "d_skip", "ssd_norm_w", "uv_b", "v_ln_w", "v_ln_b",
           "norm2_w", "conv_f_b", "final_norm_w"]
SMALL_ORDER = VECTORS + ["w_spatial", "b_spatial", "conv_a_w", "conv_f_w"]


ROW_VECTORS = VECTORS[1:]


def _small_adamw(gathered, w, m, v):
    sizes = {n: w[n].shape[1] for n in ROW_VECTORS}
    offs, off = {}, 0
    for n in ROW_VECTORS:
        offs[n] = off
        off += -(-sizes[n] // LANES) * LANES
    loss_off = off
    k = len(SMALL_ORDER)
    n_g = len(gathered)

    def body(*refs):
        row_ref, ws_ref, bs_ref, ca_ref, cf_ref, n1_ref = refs[:n_g]
        w_refs, m_refs, v_refs = (dict(zip(SMALL_ORDER, refs[n_g + i * k:n_g + (i + 1) * k])) for i in range(3))
        outs = refs[n_g + 3 * k:]
        x, y, c = _place()
        dev = 4 * x + 2 * y + c

        def total(ref):
            s = ref[0]
            for d in range(1, N_DEV):
                s = s + ref[d]
            return s

        row = total(row_ref)
        grads = {n: row[:, offs[n]:offs[n] + sizes[n]] for n in ROW_VECTORS}
        grads["norm1_w"], grads["w_spatial"], grads["b_spatial"] = total(n1_ref), total(ws_ref), total(bs_ref)
        for n, ref in (("conv_a_w", ca_ref), ("conv_f_w", cf_ref)):
            whole, cols = total(ref), w_refs[n].shape[1]
            mine = whole[:, :cols]
            for d in range(1, N_DEV):
                mine = jnp.where(dev == d, whole[:, d * cols:(d + 1) * cols], mine)
            grads[n] = mine
        for i, n in enumerate(SMALL_ORDER):
            outs[4 * i][...] = grads[n]
            outs[4 * i + 1][...], outs[4 * i + 2][...], outs[4 * i + 3][...] = _adamw(
                w_refs[n][...], grads[n], m_refs[n][...], v_refs[n][...])
        outs[4 * k][...] = row[:, loss_off:loss_off + LANES]

    out = pl.pallas_call(
        body, name="adamw_small",
        out_shape=[jax.ShapeDtypeStruct(w[n].shape, F32) for n in SMALL_ORDER for _ in range(4)]
        + [jax.ShapeDtypeStruct((1, LANES), F32)],
        compiler_params=_params(0),
    )(*gathered, *[t[n] for t in (w, m, v) for n in SMALL_ORDER])
    return [dict(zip(SMALL_ORDER, out[j:4 * k:4])) for j in range(4)] + [out[4 * k]]


SMALL = ["norm1_w", "b_gate", "conv_a_b", "dt_bias", "a_log", "d_skip", "ssd_norm_w", "uv_b", "v_ln_w", "v_ln_b",
         "w_spatial", "b_spatial", "norm2_w", "conv_f_b", "final_norm_w"]
BIG = ["w_in", "w_branch", "w_out", "w_up", "w_down"]
TRANSPOSED = ("w_in", "w_up")
WEIGHTS = ["norm1_w", "w_in", "b_gate", "conv_a_w", "conv_a_b", "dt_bias", "a_log", "d_skip", "ssd_norm_w", "uv_b",
           "v_ln_w", "v_ln_b", "w_spatial", "b_spatial", "w_branch", "w_out", "norm2_w", "w_up", "conv_f_w",
           "conv_f_b", "w_down", "final_norm_w"]
IN_SPLITS = [("z", 0, 2048), ("xbc", 2048, 5120), ("dt", 5120, 5152), ("uv", 5152, 7200), ("gates", 7200, 9248)]


def _columns_from_devices(a):
    return a.transpose(1, 0, 2).reshape(a.shape[1], -1)


def kernel(x, norm1_w, w_in, b_gate, conv_a_w, conv_a_b, dt_bias, a_log, d_skip, ssd_norm_w, uv_b, v_ln_w, v_ln_b, w_spatial, b_spatial, w_branch, w_out, norm2_w, w_up, conv_f_w, conv_f_b, w_down, final_norm_w, loss_target, m_norm1_w, m_w_in, m_b_gate, m_conv_a_w, m_conv_a_b, m_dt_bias, m_a_log, m_d_skip, m_ssd_norm_w, m_uv_b, m_v_ln_w, m_v_ln_b, m_w_spatial, m_b_spatial, m_w_branch, m_w_out, m_norm2_w, m_w_up, m_conv_f_w, m_conv_f_b, m_w_down, m_final_norm_w, v_norm1_w, v_w_in, v_b_gate, v_conv_a_w, v_conv_a_b, v_dt_bias, v_a_log, v_d_skip, v_ssd_norm_w, v_uv_b, v_v_ln_w, v_v_ln_b, v_w_spatial, v_b_spatial, v_w_branch, v_w_out, v_norm2_w, v_w_up, v_conv_f_w, v_conv_f_b, v_w_down, v_final_norm_w):
    args = dict(locals())
    wts = {n: args[n] for n in WEIGHTS}
    mom = {n: args["m_" + n] for n in WEIGHTS}
    var = {n: args["v_" + n] for n in WEIGHTS}
    cx, cy, cc = _place()
    dev = 4 * cx + 2 * cy + cc
    place = jnp.stack([cc, 2 * cx + cy]).astype(jnp.int32)

    def shard2d(n, a):
        return a[0].T if n in TRANSPOSED else a[0]

    def unshard(n, b):
        return (b.T if n in TRANSPOSED else b)[None]

    g_in, g_conv_a, g_conv_f = _all_gather(
        [shard2d("w_in", w_in).astype(BF16), conv_a_w[0], conv_f_w[0]], "gather_w_in")
    late = [shard2d(n, wts[n]).astype(BF16) for n in BIG[1:]]
    send_sems, recv_sems, late, lands, token = _gather_start(late, g_in, "gather_late_start")
    w_in_rows = g_in.reshape(-1, D_MODEL)
    w = {name: w_in_rows[lo:hi] for name, lo, hi in IN_SPLITS}
    w["dt"] = jnp.pad(w["dt"], ((0, DT_PAD - SSD_HEADS), (0, 0)))
    w["conv_a"] = _columns_from_devices(g_conv_a)
    w["conv_f"] = _columns_from_devices(g_conv_f)

    def late_weights(*after):
        mine, got = _gather_wait(send_sems, recv_sems, late, lands, after, "gather_late_wait")
        g_branch, g_out, g_up, g_down = [lax.dynamic_update_index_in_dim(land, own, dev, 0).reshape(-1, D_MODEL)
                                         for land, own in zip(got, mine)]
        return {"branch_a": g_branch[:SSD_INNER], "branch_b": g_branch[SSD_INNER:], "out": g_out, "up": g_up,
                "down": g_down}

    in_flight = {}

    def on_grad(n, g):
        part = {"w_in": lambda: jnp.concatenate([g[name][:hi - lo] for name, lo, hi in IN_SPLITS], axis=0),
                "w_branch": lambda: jnp.concatenate([g["branch_a"], g["branch_b"]], axis=0),
                "w_out": lambda: g["out"], "w_up": lambda: g["up"], "w_down": lambda: g["down"]}[n]()
        part = part.reshape(N_DEV, -1, D_MODEL)
        send, recv, (part,), (land,), tok = _gather_start([part], None, f"to_owners_start_{n}", scatter=True)
        in_flight[n] = (part, send, recv, land)
        return tok

    p = {n: wts[n][0] if wts[n].ndim > 2 else wts[n].reshape(1, -1) for n in SMALL}
    small_flight = []

    def on_small(g, loss):
        arrays = [jnp.concatenate([g[n] for n in ROW_VECTORS] + [loss[:1]], axis=1), g["w_spatial"], g["b_spatial"],
                  g["conv_a"], g["conv_f"]]
        *flight, tok = _gather_start(arrays, g["conv_a"], "gather_small_start")
        small_flight.append(flight)
        return tok

    loss, gx, g = _local_step(x[0], loss_target[0], w, p, after=token, late_weights=late_weights, on_grad=on_grad,
                              on_small=on_small)
    *flight, _ = _gather_start([g["norm1_w"]], gx, "gather_norm1_start")
    small_flight.append(flight)

    grads, delta, new_m, new_v = {}, {}, {}, {}

    def big_adamw(n, after):
        view = (lambda a: a.transpose(2, 0, 1)) if n == "w_in" else (lambda a: shard2d(n, a))
        back = (lambda b: b.transpose(1, 2, 0)) if n == "w_in" else (lambda b: unshard(n, b))
        part, send, recv, land = in_flight[n]
        (part,), (got,) = _gather_wait(send, recv, [part], [land], [after], f"to_owners_wait_{n}", scatter=True)
        out = _sum8_adamw(part, got, place, *[view(t[n]) for t in (wts, mom, var)], f"adamw_{n}")
        grads[n], delta[n], new_m[n], new_v[n] = [back(o) for o in out]
        return out[1]

    after = gx
    for n in ("w_down", "w_up", "w_out", "w_branch"):
        after = big_adamw(n, after)
    gathered = []
    for (send, recv, mine, land), name in zip(small_flight, ("gather_small_wait", "gather_norm1_wait")):
        mine, got = _gather_wait(send, recv, mine, land, [after], name)
        gathered += [lax.dynamic_update_index_in_dim(full, own, dev, 0) for full, own in zip(got, mine)]
    small = [{n: t[n][0] if t[n].ndim > 2 else t[n].reshape(1, -1) for n in SMALL_ORDER} for t in (wts, mom, var)]
    *outs, loss = _small_adamw(gathered, *small)
    for tgt, out in zip((grads, delta, new_m, new_v), outs):
        tgt.update({n: out[n].reshape(wts[n].shape) for n in SMALL_ORDER})
    big_adamw("w_in", loss)
    loss = loss[0, 0]

    return (loss, gx[None], *[grads[n] for n in WEIGHTS], *[delta[n] for n in WEIGHTS],
            *[new_m[n] for n in WEIGHTS], *[new_v[n] for n in WEIGHTS])
```

```python
import functools

import jax
import jax.numpy as jnp
from jax import lax
from jax.experimental import pallas as pl
from jax.experimental.pallas import tpu as pltpu

F32, BF16 = jnp.float32, jnp.bfloat16
HIGHEST = lax.Precision.HIGHEST

D_MODEL = 1024
SSD_INNER = 2048
SSD_HEAD_DIM = 64
SSD_HEADS = 32
SSD_GROUPS = 4
SSD_STATE = 128
SSD_BC = SSD_GROUPS * SSD_STATE
SSD_XBC = SSD_INNER + 2 * SSD_BC
SSD_CONV = 4
CHUNK = 128
N_PAIRS = SSD_HEADS // 2
PAIRS_PER_GROUP = N_PAIRS // SSD_GROUPS
SGU_WIDTH = 1024
SGU_GROUPS = 8
D_FF = 2816
FFN_CONV = 3
NORM_EPS = 1e-6
LN_EPS = 1e-5
LANES = 128
DT_PAD = LANES

ADAM_LR, ADAM_B1, ADAM_B2, ADAM_EPS, ADAM_WD, ADAM_STEP = 0.001, 0.9, 0.999, 1e-08, 0.01, 10

N_DEV = 8
VMEM_LIMIT = 56 * 1024 * 1024
MESH = pl.DeviceIdType.MESH


def _params(n_grid, **kw):
    sem = dict(dimension_semantics=("arbitrary",) * n_grid) if n_grid else {}
    return pltpu.CompilerParams(vmem_limit_bytes=VMEM_LIMIT, **sem, **kw)


def _tile(n, pref):
    t = (min(pref, n) // LANES) * LANES
    while n % t:
        t -= LANES
    return t


def _row_tile(r, pref):
    for t in range(min(pref, r) // 16 * 16, 0, -16):
        if r % t == 0:
            return t
    return r


def _tile2d(r, c, rows):
    if r % 16 == 0:
        return _row_tile(r, rows), c
    return r, _tile(c, 2 * LANES)


def _rows(tm, n, nt=None, rev=False):
    if rev:
        return pl.BlockSpec((tm, n), lambda i: (nt - 1 - i, 0))
    return pl.BlockSpec((tm, n), lambda i: (i, 0))


def _halo(tm, n, rows=8):
    per = tm // rows
    return pl.BlockSpec((rows, n), lambda i: (jnp.maximum(i * per - 1, 0), 0))


def _full(shape):
    nd = len(shape)
    return pl.BlockSpec(shape, lambda *_: (0,) * nd)


def _rms(x, w, eps=NORM_EPS):
    return x * lax.rsqrt(jnp.mean(x * x, axis=-1, keepdims=True) + eps) * w


def _layer_norm(x, w, b):
    mu = jnp.mean(x, axis=-1, keepdims=True)
    var = jnp.mean(jnp.square(x - mu), axis=-1, keepdims=True)
    return (x - mu) * lax.rsqrt(var + LN_EPS) * w + b


def _sigmoid(x):
    return 1.0 / (1.0 + jnp.exp(-x))


def _silu(x):
    return x * _sigmoid(x)


def _dsilu(x):
    s = _sigmoid(x)
    return s * (1.0 + x * (1.0 - s))


def _silu_and_grad(x):
    s = _sigmoid(x)
    return x * s, s * (1.0 + x * (1.0 - s))


def _softplus(x):
    return jnp.maximum(x, 0.0) + jnp.log(1.0 + jnp.exp(-jnp.abs(x)))


def _gelu(x):
    return jax.nn.gelu(x)


def _dot(a, b):
    return jnp.dot(a, b, preferred_element_type=F32)


def _dot_nt(a, b):
    return lax.dot_general(a, b, (((1,), (1,)), ((), ())), preferred_element_type=F32)


def _dot_tn(a, b):
    return lax.dot_general(a, b, (((0,), (0,)), ((), ())), preferred_element_type=F32)


def _dot_split(p, e):
    hi = p.astype(BF16)
    lo = (p - hi.astype(F32)).astype(BF16)
    return _dot(hi, e) + _dot(lo, e)


def _colsum(x):
    return jnp.sum(x, axis=0, keepdims=True)


def _shift_down(x, halo, j):
    xs = pltpu.roll(x, j, 0)
    hs = pltpu.roll(halo, j, 0)
    r8 = lax.broadcasted_iota(jnp.int32, hs.shape, 0)
    return jnp.concatenate([jnp.where(r8 < j, hs, xs[:8]), xs[8:]], axis=0)


def _shift_up(x, nxt, j):
    n = x.shape[0]
    xs = pltpu.roll(x, n - j, 0)
    ns = pltpu.roll(nxt, 8 - j, 0)
    r8 = lax.broadcasted_iota(jnp.int32, ns.shape, 0)
    return jnp.concatenate([xs[:n - 8], jnp.where(r8 >= 8 - j, ns, xs[n - 8:])], axis=0)


def _causal_conv(x, halo, w, b):
    k = w.shape[0]
    y = b + w[k - 1:k, :] * x
    for j in range(1, k):
        y = y + w[k - 1 - j:k - j, :] * _shift_down(x, halo, j)
    return y


def _causal_conv_bwd(dy, nxt, x, w):
    k = w.shape[0]
    dx = w[k - 1:k, :] * dy
    dw = [_colsum(dy * x)]
    for j in range(1, k):
        dyj = _shift_up(dy, nxt, j)
        dx = dx + w[k - 1 - j:k - j, :] * dyj
        dw.append(_colsum(dyj * x))
    return dx, jnp.concatenate(dw[::-1], axis=0)


MM_TILE_PREF = 1408
MM_VMEM_BUDGET = 40 * 1024 * 1024


def _mm_tiles(m, n, k, out_bytes):
    tm, tn = _tile(m, MM_TILE_PREF), _tile(n, MM_TILE_PREF)
    need = lambda tm, tn: 2 * (2 * k * (tm + tn) + out_bytes * tm * tn)
    while need(tm, tn) > MM_VMEM_BUDGET:
        if tn >= tm and tn > LANES:
            tn = _tile(n, tn - LANES)
        else:
            tm = _tile(m, tm - LANES)
    return tm, tn


def _mm(a, b, dims, name, acc=None, out_dtype=F32, after=None):
    a_list, b_list = (list(a), list(b)) if isinstance(a, (list, tuple)) else ([a], [b])
    k_axis, m_axis = (0, 1) if dims == "tn" else (1, 0)
    m, ks = a_list[0].shape[m_axis], [x.shape[k_axis] for x in a_list]
    n = b_list[0].shape[0] if dims == "nt" else b_list[0].shape[1]
    tm, tn = _mm_tiles(m, n, sum(ks), 4 * (2 if acc is not None else 1))
    a_specs = [pl.BlockSpec((k, tm), lambda j, i: (0, i)) if dims == "tn" else pl.BlockSpec((tm, k), lambda j, i: (i, 0))
               for k in ks]
    b_specs = [pl.BlockSpec((tn, k), lambda j, i: (j, 0)) if dims == "nt" else pl.BlockSpec((k, tn), lambda j, i: (0, j))
               for k in ks]
    o_spec = pl.BlockSpec((tm, tn), lambda j, i: (i, j))
    dot = {"nn": _dot, "nt": _dot_nt, "tn": _dot_tn}[dims]
    n_pairs = len(ks)

    def body(*refs):
        rest = refs[2 * n_pairs:]
        r = dot(refs[0][...], refs[n_pairs][...])
        for i in range(1, n_pairs):
            r = r + dot(refs[i][...], refs[n_pairs + i][...])
        if acc is not None:
            r = r + rest[0][...]
        rest[-1][...] = r.astype(out_dtype)

    ins, specs = a_list + b_list, a_specs + b_specs
    if acc is not None:
        ins.append(acc)
        specs.append(o_spec)
    if after is not None:
        ins.append(after)
        specs.append(pl.BlockSpec(memory_space=pl.ANY))
    return pl.pallas_call(
        body, name=name, grid=(n // tn, m // tm), in_specs=specs, out_specs=o_spec,
        out_shape=jax.ShapeDtypeStruct((m, n), out_dtype), compiler_params=_params(2),
    )(*ins)


def _mm_rows(a, b, dims, name, fn, rows=(), fulls=(), row_outs=(), acc_outs=(), after=None):
    a_list, b_list = (list(a), list(b)) if isinstance(a, (list, tuple)) else ([a], [b])
    m, ks = a_list[0].shape[0], [x.shape[1] for x in a_list]
    n, k = (b_list[0].shape[0] if dims == "nt" else b_list[0].shape[1]), sum(ks)
    per_row = 2 * k + 8 * n + sum(4 * r.shape[1] for r in rows) + sum(c * jnp.dtype(d).itemsize for c, d in row_outs)
    tm = _tile(m, 1024)
    while 2 * tm * per_row + 4 * k * n > MM_VMEM_BUDGET:
        tm = _tile(m, tm - LANES)
    dot = _dot_nt if dims == "nt" else _dot
    n_pairs = len(ks)
    n_in = 2 * n_pairs + len(rows) + len(fulls) + (after is not None)

    def body(*refs):
        ins, outs = refs[:n_in], refs[n_in:]
        row_refs, acc_refs = outs[:len(row_outs)], outs[len(row_outs):]

        @pl.when(pl.program_id(0) == 0)
        def _():
            for r in acc_refs:
                r[...] = jnp.zeros_like(r)

        result = dot(ins[0][...], ins[n_pairs][...])
        for i in range(1, n_pairs):
            result = result + dot(ins[i][...], ins[n_pairs + i][...])
        new_rows, incs = fn(result, *[r[...] for r in ins[2 * n_pairs:2 * n_pairs + len(rows) + len(fulls)]])
        for r, val in zip(row_refs, new_rows):
            r[...] = val.astype(r.dtype)
        for r, inc in zip(acc_refs, incs):
            r[...] += inc

    extra, extra_specs = ([after], [pl.BlockSpec(memory_space=pl.ANY)]) if after is not None else ([], [])
    return pl.pallas_call(
        body, name=name, grid=(m // tm,),
        in_specs=[_rows(tm, k_i) for k_i in ks] + [_full(x.shape) for x in b_list]
        + [_rows(tm, r.shape[1]) for r in rows] + [_full(f.shape) for f in fulls] + extra_specs,
        out_specs=[_rows(tm, c) for c, _ in row_outs] + [_full(s) for s in acc_outs],
        out_shape=[jax.ShapeDtypeStruct((m, c), d) for c, d in row_outs]
        + [jax.ShapeDtypeStruct(s, F32) for s in acc_outs],
        compiler_params=_params(1),
    )(*a_list, *b_list, *rows, *fulls, *extra)


def _residual_norm(o, x, w):
    h = x + o
    return (h, _rms(h, w)), ()


def _norm_backward(dn, h, dres, w):
    _, vjp = jax.vjp(_rms, h, w)
    dh, dw = vjp(dn)
    dh = dh + dres
    return (dh, dh), (dw,)


def _loss_and_grad(dn, h1, target, w):
    yf, vjp = jax.vjp(_rms, h1 + dn, w)
    err = yf - target
    loss = 0.5 * jnp.sum(jnp.mean(err * err, axis=-1, keepdims=True))
    dh, dw = vjp(err * (1.0 / err.shape[-1]))
    return (dh, dh), (jnp.full((8, LANES), loss, F32), dw)


def _wgrad(a, d, name, after=None):
    return _mm(a, d, "tn", name, out_dtype=BF16, after=after)


def _norm_fwd(x, w, name, after=None, tm=512):
    t, d = x.shape

    def body(x_ref, w_ref, *rest):
        rest[-1][...] = _rms(x_ref[...], w_ref[...]).astype(BF16)

    extra, extra_specs = ([after], [_full(after.shape)]) if after is not None else ([], [])
    return pl.pallas_call(
        body, name=name, grid=(t // tm,), in_specs=[_rows(tm, d), _full((1, d))] + extra_specs,
        out_specs=_rows(tm, d), out_shape=jax.ShapeDtypeStruct((t, d), BF16), compiler_params=_params(1),
    )(x, w, *extra)


def _conv_a_fwd(xbc, cw, cb, tm=256):
    t, c = xbc.shape

    def body(x_ref, h_ref, w_ref, b_ref, o_ref, y_ref):
        halo = jnp.where(pl.program_id(0) > 0, h_ref[...].astype(F32)[8:], 0.0)
        y = _causal_conv(x_ref[...].astype(F32), halo, w_ref[...], b_ref[...])
        y_ref[...] = y.astype(BF16)
        o_ref[...] = _silu(y)

    return pl.pallas_call(
        body, name="conv_a_fwd", grid=(t // tm,),
        in_specs=[_rows(tm, c), _halo(tm, c, rows=16), _full(cw.shape), _full((1, c))],
        out_specs=[_rows(tm, c)] * 2,
        out_shape=[jax.ShapeDtypeStruct((t, c), F32), jax.ShapeDtypeStruct((t, c), BF16)], compiler_params=_params(1),
    )(xbc, xbc, cw, cb)


def _ssd_common(dtr, dtb, alog, e_t):
    row = lax.broadcasted_iota(jnp.int32, (CHUNK, CHUNK), 0)
    col = lax.broadcasted_iota(jnp.int32, (CHUNK, CHUNK), 1)
    causal = row >= col
    dt = _softplus(dtr + dtb)
    a = -jnp.exp(alog)
    acum = jnp.dot(causal.astype(F32), dt * a, precision=HIGHEST, preferred_element_type=F32)
    spread = lambda v: _dot(v.astype(BF16), e_t)
    elast = jnp.broadcast_to(jnp.exp(acum[CHUNK - 1:CHUNK, :]), (8, LANES))
    return dict(dt=dt, a=a, acum=acum, acum_t=acum.T, causal=causal, row=row, col=col, lane_lo=col < SSD_HEAD_DIM,
                dt_x=_dot_split(dt, e_t), ecol_x=spread(jnp.exp(acum)), elast_x=_dot_split(elast, e_t)[0:1],
                dsr_x=spread(jnp.exp(acum[CHUNK - 1:CHUNK, :] - acum)))


def _head_decay(c, h, transposed=False):
    d = c["acum"][:, h:h + 1] - c["acum_t"][h:h + 1, :]
    if transposed:
        return jnp.exp(jnp.where(c["row"] <= c["col"], -d, -jnp.inf))
    return jnp.exp(jnp.where(c["causal"], d, -jnp.inf))


def _ssd_fwd(xc, dtr, z, dtb, alog, dsk, nw, e_t):
    t = xc.shape[0]
    nc = t // CHUNK

    def body(xs_ref, b_ref, c_ref, dtr_ref, z_ref, dtb_ref, alog_ref, dsk_ref, nw_ref, et_ref,
             y_ref, ya_ref, sp_ref, s_scr):
        @pl.when(pl.program_id(0) == 0)
        def _():
            s_scr[...] = jnp.zeros_like(s_scr)

        c = _ssd_common(dtr_ref[...], dtb_ref[...], alog_ref[...], et_ref[...])
        lane_lo = c["lane_lo"]
        dsk = dsk_ref[...]
        for g in range(SSD_GROUPS):
            gs = slice(g * SSD_STATE, (g + 1) * SSD_STATE)
            bg_t, cg = b_ref[:, gs].T.astype(BF16), c_ref[:, gs].astype(BF16)
            cb = _dot(cg, bg_t)
            for pp in range(PAIRS_PER_GROUP):
                j = g * PAIRS_PER_GROUP + pp
                ps = slice(j * LANES, (j + 1) * LANES)
                x = xs_ref[:, ps]
                ecol, dsr = c["ecol_x"][:, ps], c["dsr_x"][:, ps]
                xdt = x * c["dt_x"][:, ps]
                xb = xdt.astype(BF16)
                zero = jnp.zeros_like(xb)
                yd = (_dot((cb * _head_decay(c, 2 * j)).astype(BF16), jnp.where(lane_lo, xb, zero))
                      + _dot((cb * _head_decay(c, 2 * j + 1)).astype(BF16), jnp.where(lane_lo, zero, xb)))
                sp = s_scr[j]
                yo = ecol * _dot(cg, sp.astype(BF16))
                st = _dot(bg_t, (xdt * dsr).astype(BF16))
                sp_ref[0, j] = sp
                s_scr[j] = c["elast_x"][:, ps] * sp + st
                dskp = jnp.where(lane_lo[0:1], dsk[:, 2 * j:2 * j + 1], dsk[:, 2 * j + 1:2 * j + 2])
                y_ref[:, ps] = yd + yo + dskp * x
        ya_ref[...] = _rms(y_ref[...] * _silu(z_ref[...].astype(F32)), nw_ref[...]).astype(BF16)

    ck = lambda n, col=0: pl.BlockSpec((CHUNK, n), lambda c: (c, col))
    return pl.pallas_call(
        body, name="ssd_fwd", grid=(nc,),
        in_specs=[ck(SSD_INNER), ck(SSD_BC, SSD_INNER // SSD_BC), ck(SSD_BC, SSD_INNER // SSD_BC + 1), ck(DT_PAD),
                  ck(SSD_INNER), _full((1, DT_PAD)), _full((1, DT_PAD)), _full((1, DT_PAD)),
                  _full((1, SSD_INNER)), _full(e_t.shape)],
        out_specs=[ck(SSD_INNER), ck(SSD_INNER),
                   pl.BlockSpec((1, N_PAIRS, SSD_STATE, LANES), lambda c: (c, 0, 0, 0))],
        out_shape=[jax.ShapeDtypeStruct((t, SSD_INNER), F32), jax.ShapeDtypeStruct((t, SSD_INNER), BF16),
                   jax.ShapeDtypeStruct((nc, N_PAIRS, SSD_STATE, LANES), F32)],
        scratch_shapes=[pltpu.VMEM((N_PAIRS, SSD_STATE, LANES), F32)], compiler_params=_params(1),
    )(xc, xc, xc, dtr, z, dtb, alog, dsk, nw, e_t)


def _ssd_bwd(dya, y, z, xc, dtr, sprev, dtb, alog, dsk, nw, e_heads, e_t):
    t = xc.shape[0]
    nc = t // CHUNK

    def body(dya_ref, y_ref, z_ref, xs_ref, b_ref, c_ref, dtr_ref, sp_ref, dtb_ref, alog_ref, dsk_ref, nw_ref, e_ref,
             et_ref, dz_ref, dxs_ref, db_ref, dc_ref, ddtr_ref, dnw_ref, ddtb_ref, dalog_ref, ddsk_ref, ds_scr):
        @pl.when(pl.program_id(0) == 0)
        def _():
            ds_scr[...] = jnp.zeros_like(ds_scr)
            for r in (dnw_ref, ddtb_ref, dalog_ref, ddsk_ref):
                r[...] = jnp.zeros_like(r)

        y = y_ref[...]
        _, gate_vjp = jax.vjp(lambda y_, z_, w_: _rms(y_ * _silu(z_), w_), y, z_ref[...].astype(F32), nw_ref[...])
        dy, dz, dnw = gate_vjp(dya_ref[...])
        dz_ref[...] = dz.astype(BF16)
        dnw_ref[...] += dnw

        dtr = dtr_ref[...]
        c = _ssd_common(dtr, dtb_ref[...], alog_ref[...], et_ref[...])
        dt, a, lane_lo, row, col = c["dt"], c["a"], c["lane_lo"], c["row"], c["col"]
        dsk = dsk_ref[...]
        p_a, p_dt, v_last = [], [], []
        da_cols = jnp.zeros((CHUNK, CHUNK), F32)
        da_rows = jnp.zeros((CHUNK, CHUNK), F32)
        for g in range(SSD_GROUPS):
            gs = slice(g * SSD_STATE, (g + 1) * SSD_STATE)
            bg, cg = b_ref[:, gs].astype(BF16), c_ref[:, gs].astype(BF16)
            bg_t, cg_t = b_ref[:, gs].T.astype(BF16), c_ref[:, gs].T.astype(BF16)
            cb, cb_t = _dot(cg, bg_t), _dot(bg, cg_t)
            dcb = jnp.zeros((CHUNK, CHUNK), F32)
            dbg = jnp.zeros((CHUNK, SSD_STATE), F32)
            dcg = jnp.zeros((CHUNK, SSD_STATE), F32)
            for pp in range(PAIRS_PER_GROUP):
                j = g * PAIRS_PER_GROUP + pp
                ps = slice(j * LANES, (j + 1) * LANES)
                x = xs_ref[:, ps]
                dtp, ecol, dsr = c["dt_x"][:, ps], c["ecol_x"][:, ps], c["dsr_x"][:, ps]
                elast = c["elast_x"][:, ps]
                xdt = x * dtp
                xb = xdt.astype(BF16)
                dskp = jnp.where(lane_lo[0:1], dsk[:, 2 * j:2 * j + 1], dsk[:, 2 * j + 1:2 * j + 2])
                dyp = dy[:, ps]
                dyb = dyp.astype(BF16)
                sp, dsn = sp_ref[0, j], ds_scr[j]
                spb, dsnb = sp.astype(BF16), dsn.astype(BF16)
                y_off = ecol * _dot(cg, spb)
                dw = (dyp * ecol).astype(BF16)
                dcg = dcg + _dot_nt(dw, spb)
                dsp = _dot(cg_t, dw) + elast * dsn
                xd = xdt * dsr
                zd = _dot(bg, dsnb) * dsr
                dbg = dbg + _dot_nt(xd.astype(BF16), dsnb)
                dxdt = zd
                zero = jnp.zeros_like(xb)
                for h, lm in ((2 * j, lane_lo), (2 * j + 1, jnp.logical_not(lane_lo))):
                    le = _head_decay(c, h)
                    dm = _dot_nt(jnp.where(lm, dyb, zero), jnp.where(lm, xb, zero))
                    dcb = dcb + dm * le
                    m = cb * le
                    m_t = (cb_t * _head_decay(c, h, transposed=True)).astype(BF16)
                    dxdt = dxdt + jnp.where(lm, _dot(m_t, dyb), 0.0)
                    q = dm * m
                    da_cols = da_cols + jnp.where(col == h, jnp.sum(q, axis=1, keepdims=True), 0.0)
                    da_rows = da_rows + jnp.where(row == h, _colsum(q), 0.0)
                ds_scr[j] = dsp
                dxs_ref[:, ps] = dxdt * dtp + dskp * dyp
                p_a.append(dyp * y_off - xdt * zd)
                p_dt.append(dxdt * x)
                v_last.append(_colsum(zd * xdt) + elast * _colsum(dsn * sp))
            dcbb = dcb.astype(BF16)
            db_ref[:, gs] = dbg + _dot_tn(dcbb, cg)
            dc_ref[:, gs] = dcg + _dot(dcbb, bg)
        e = e_ref[...]
        rows8 = jnp.concatenate([jnp.concatenate(v_last, axis=1), _colsum(dy * xs_ref[...]),
                                 jnp.zeros((6, SSD_INNER), F32)], axis=0)
        r8 = _dot_split(rows8, e)
        da = (_dot_split(jnp.concatenate(p_a, axis=1), e) + jnp.where(row == CHUNK - 1, r8[0:1], 0.0)
              + da_cols - da_rows.T)
        ddsk_ref[...] += r8[1:2]
        dadt = jnp.dot((row <= col).astype(F32), da, precision=HIGHEST, preferred_element_type=F32)
        ddt = dadt * a + _dot_split(jnp.concatenate(p_dt, axis=1), e)
        dalog_ref[...] += _colsum(dadt * dt) * a
        ddtr = ddt * _sigmoid(dtr + dtb_ref[...])
        ddtr_ref[...] = ddtr
        ddtb_ref[...] += _colsum(ddtr)

    ck = lambda n, col=0: pl.BlockSpec((CHUNK, n), lambda c: (nc - 1 - c, col))
    acc = lambda n: _full((1, n))
    return pl.pallas_call(
        body, name="ssd_bwd", grid=(nc,),
        in_specs=[ck(SSD_INNER), ck(SSD_INNER), ck(SSD_INNER), ck(SSD_INNER), ck(SSD_BC, SSD_INNER // SSD_BC),
                  ck(SSD_BC, SSD_INNER // SSD_BC + 1), ck(DT_PAD),
                  pl.BlockSpec((1, N_PAIRS, SSD_STATE, LANES), lambda c: (nc - 1 - c, 0, 0, 0)),
                  acc(DT_PAD), acc(DT_PAD), acc(DT_PAD), acc(SSD_INNER), _full((SSD_INNER, LANES)),
                  _full((LANES, SSD_INNER))],
        out_specs=[ck(SSD_INNER), ck(SSD_INNER), ck(SSD_BC), ck(SSD_BC), ck(DT_PAD),
                   acc(SSD_INNER), acc(DT_PAD), acc(DT_PAD), acc(DT_PAD)],
        out_shape=[jax.ShapeDtypeStruct((t, SSD_INNER), BF16), jax.ShapeDtypeStruct((t, SSD_INNER), F32),
                   jax.ShapeDtypeStruct((t, SSD_BC), F32), jax.ShapeDtypeStruct((t, SSD_BC), F32),
                   jax.ShapeDtypeStruct((t, DT_PAD), F32), jax.ShapeDtypeStruct((1, SSD_INNER), F32),
                   jax.ShapeDtypeStruct((1, DT_PAD), F32), jax.ShapeDtypeStruct((1, DT_PAD), F32),
                   jax.ShapeDtypeStruct((1, DT_PAD), F32)],
        scratch_shapes=[pltpu.VMEM((N_PAIRS, SSD_STATE, LANES), F32)], compiler_params=_params(1),
    )(dya, y, z, xc, xc, xc, dtr, sprev, dtb, alog, dsk, nw, e_heads, e_t)


def _sgu_act(uv, uvb, lnw, lnb):
    a = _gelu(uv + uvb)
    return a[:, :SGU_WIDTH], _layer_norm(a[:, SGU_WIDTH:], lnw, lnb)


def _sgu_weights(ws_ref):
    row = lax.broadcasted_iota(jnp.int32, (CHUNK, CHUNK), 0)
    col = lax.broadcasted_iota(jnp.int32, (CHUNK, CHUNK), 1)
    return [jnp.where(row >= col, ws_ref[g], 0.0).astype(BF16) for g in range(SGU_GROUPS)], row >= col


def _sgu_fwd(uv, uvb, lnw, lnb, ws, bs_t):
    t = uv.shape[0]

    def body(uv_ref, uvb_ref, lnw_ref, lnb_ref, ws_ref, bs_ref, o_ref):
        u, vn = _sgu_act(uv_ref[...].astype(F32), uvb_ref[...], lnw_ref[...], lnb_ref[...])
        wc, _ = _sgu_weights(ws_ref)
        bs = bs_ref[...]
        for g in range(SGU_GROUPS):
            gs = slice(g * LANES, (g + 1) * LANES)
            mixed = _dot(wc[g], vn[:, gs].astype(BF16)) + bs[:, g:g + 1]
            o_ref[:, gs] = (u[:, gs] * mixed).astype(BF16)

    return pl.pallas_call(
        body, name="sgu_fwd", grid=(t // CHUNK,),
        in_specs=[_rows(CHUNK, 2 * SGU_WIDTH), _full((1, 2 * SGU_WIDTH)), _full((1, SGU_WIDTH)), _full((1, SGU_WIDTH)),
                  _full(ws.shape), _full(bs_t.shape)],
        out_specs=_rows(CHUNK, SGU_WIDTH), out_shape=jax.ShapeDtypeStruct((t, SGU_WIDTH), BF16),
        compiler_params=_params(1),
    )(uv, uvb, lnw, lnb, ws, bs_t)


def _sgu_bwd(dyb, uv, uvb, lnw, lnb, ws, bs_t, e_groups):
    t = uv.shape[0]

    def body(dyb_ref, uv_ref, uvb_ref, lnw_ref, lnb_ref, ws_ref, bs_ref, e_ref,
             duv_ref, duvb_ref, dlnw_ref, dlnb_ref, dws_ref, dbs_ref):
        @pl.when(pl.program_id(0) == 0)
        def _():
            for r in (duvb_ref, dlnw_ref, dlnb_ref, dws_ref, dbs_ref):
                r[...] = jnp.zeros_like(r)

        (u, vn), act_vjp = jax.vjp(_sgu_act, uv_ref[...].astype(F32), uvb_ref[...], lnw_ref[...], lnb_ref[...])
        wc, causal = _sgu_weights(ws_ref)
        bs = bs_ref[...]
        dyb = dyb_ref[...]
        du, dvn, dmix = [], [], []
        for g in range(SGU_GROUPS):
            gs = slice(g * LANES, (g + 1) * LANES)
            vb = vn[:, gs].astype(BF16)
            mixed = _dot(wc[g], vb) + bs[:, g:g + 1]
            dm = dyb[:, gs] * u[:, gs]
            dmb = dm.astype(BF16)
            du.append(dyb[:, gs] * mixed)
            dvn.append(_dot_tn(wc[g], dmb))
            dws_ref[g] += jnp.where(causal, _dot_nt(dmb, vb), 0.0)
            dmix.append(dm)
        dbs_ref[...] += _dot_split(jnp.concatenate(dmix, axis=1), e_ref[...])
        duv, duvb, dlnw, dlnb = act_vjp((jnp.concatenate(du, axis=1), jnp.concatenate(dvn, axis=1)))
        duv_ref[...] = duv.astype(BF16)
        duvb_ref[...] += duvb
        dlnw_ref[...] += dlnw
        dlnb_ref[...] += dlnb

    return pl.pallas_call(
        body, name="sgu_bwd", grid=(t // CHUNK,),
        in_specs=[_rows(CHUNK, SGU_WIDTH), _rows(CHUNK, 2 * SGU_WIDTH), _full((1, 2 * SGU_WIDTH)),
                  _full((1, SGU_WIDTH)), _full((1, SGU_WIDTH)), _full(ws.shape), _full(bs_t.shape),
                  _full(e_groups.shape)],
        out_specs=[_rows(CHUNK, 2 * SGU_WIDTH), _full((1, 2 * SGU_WIDTH)), _full((1, SGU_WIDTH)),
                   _full((1, SGU_WIDTH)), _full(ws.shape), _full(bs_t.shape)],
        out_shape=[jax.ShapeDtypeStruct((t, 2 * SGU_WIDTH), BF16), jax.ShapeDtypeStruct((1, 2 * SGU_WIDTH), F32),
                   jax.ShapeDtypeStruct((1, SGU_WIDTH), F32), jax.ShapeDtypeStruct((1, SGU_WIDTH), F32),
                   jax.ShapeDtypeStruct(ws.shape, F32), jax.ShapeDtypeStruct(bs_t.shape, F32)],
        compiler_params=_params(1),
    )(dyb, uv, uvb, lnw, lnb, ws, bs_t, e_groups)


def _merge(gates, pa, pb, bg):
    s = _sigmoid(gates + bg)
    return s[:, :D_MODEL] * pa + s[:, D_MODEL:] * pb


def _merge_fwd(gates, pa, pb, bg, tm=256):
    t = gates.shape[0]

    def body(g_ref, pa_ref, pb_ref, bg_ref, o_ref):
        o_ref[...] = _merge(g_ref[...].astype(F32), pa_ref[...].astype(F32), pb_ref[...].astype(F32),
                            bg_ref[...]).astype(BF16)

    return pl.pallas_call(
        body, name="merge_fwd", grid=(t // tm,),
        in_specs=[_rows(tm, 2 * D_MODEL), _rows(tm, D_MODEL), _rows(tm, D_MODEL), _full((1, 2 * D_MODEL))],
        out_specs=_rows(tm, D_MODEL), out_shape=jax.ShapeDtypeStruct((t, D_MODEL), BF16), compiler_params=_params(1),
    )(gates, pa, pb, bg)


def _merge_backward(dmix, gates, pa, pb, bg):
    _, vjp = jax.vjp(_merge, gates.astype(F32), pa.astype(F32), pb.astype(F32), bg)
    dg, dpa, dpb, dbg = vjp(dmix)
    return (dg, dpa, dpb), (dbg,)


def _conv_f_fwd(up, cw, cb, tm=128):
    t, c = up.shape

    def body(x_ref, h_ref, w_ref, b_ref, o_ref, y_ref):
        halo = jnp.where(pl.program_id(0) > 0, h_ref[...].astype(F32)[8:], 0.0)
        y = _causal_conv(x_ref[...].astype(F32), halo, w_ref[...], b_ref[...])
        y_ref[...] = y.astype(BF16)
        o_ref[...] = (_silu(y[:, :D_FF]) * y[:, D_FF:]).astype(BF16)

    return pl.pallas_call(
        body, name="conv_f_fwd", grid=(t // tm,),
        in_specs=[_rows(tm, c), _halo(tm, c, rows=16), _full(cw.shape), _full((1, c))],
        out_specs=[_rows(tm, D_FF), _rows(tm, c)],
        out_shape=[jax.ShapeDtypeStruct((t, D_FF), BF16), jax.ShapeDtypeStruct((t, c), BF16)],
        compiler_params=_params(1),
    )(up, up, cw, cb)


def _conv_f_bwd(dact, y, up, cw, tm=128):
    t, c = up.shape
    nt = t // tm

    def body(d_ref, y_ref, x_ref, w_ref, dx_ref, dw_ref, db_ref, nxt_scr):
        @pl.when(pl.program_id(0) == 0)
        def _():
            nxt_scr[...] = jnp.zeros_like(nxt_scr)
            dw_ref[...] = jnp.zeros_like(dw_ref)
            db_ref[...] = jnp.zeros_like(db_ref)

        a, v = y_ref[:, :D_FF].astype(F32), y_ref[:, D_FF:].astype(F32)
        d = d_ref[...].astype(F32)
        silu_a, dsilu_a = _silu_and_grad(a)
        dy = jnp.concatenate([d * v * dsilu_a, d * silu_a], axis=1)
        dx, dw = _causal_conv_bwd(dy, nxt_scr[...], x_ref[...].astype(F32), w_ref[...])
        dx_ref[...] = dx.astype(BF16)
        nxt_scr[...] = dy[:8]
        dw_ref[...] += dw
        db_ref[...] += _colsum(dy)

    return pl.pallas_call(
        body, name="conv_f_bwd", grid=(nt,),
        in_specs=[_rows(tm, D_FF, nt, True), _rows(tm, c, nt, True), _rows(tm, c, nt, True), _full(cw.shape)],
        out_specs=[_rows(tm, c, nt, True), _full(cw.shape), _full((1, c))],
        out_shape=[jax.ShapeDtypeStruct((t, c), BF16), jax.ShapeDtypeStruct(cw.shape, F32),
                   jax.ShapeDtypeStruct((1, c), F32)],
        scratch_shapes=[pltpu.VMEM((8, c), F32)], compiler_params=_params(1),
    )(dact, y, up, cw)


def _conv_a_bwd(dxs, db, dc, y, xbc, cw, tm=256):
    t, c = xbc.shape
    nt = t // tm

    def body(dxs_ref, db_ref, dc_ref, y_ref, x_ref, w_ref, dx_ref, dw_ref, dbias_ref, nxt_scr):
        @pl.when(pl.program_id(0) == 0)
        def _():
            nxt_scr[...] = jnp.zeros_like(nxt_scr)
            dw_ref[...] = jnp.zeros_like(dw_ref)
            dbias_ref[...] = jnp.zeros_like(dbias_ref)

        dy = jnp.concatenate([dxs_ref[...], db_ref[...], dc_ref[...]], axis=1) * _dsilu(y_ref[...].astype(F32))
        dx, dw = _causal_conv_bwd(dy, nxt_scr[...], x_ref[...].astype(F32), w_ref[...])
        dx_ref[...] = dx.astype(BF16)
        nxt_scr[...] = dy[:8]
        dw_ref[...] += dw
        dbias_ref[...] += _colsum(dy)

    return pl.pallas_call(
        body, name="conv_a_bwd", grid=(nt,),
        in_specs=[_rows(tm, SSD_INNER, nt, True), _rows(tm, SSD_BC, nt, True), _rows(tm, SSD_BC, nt, True),
                  _rows(tm, c, nt, True), _rows(tm, c, nt, True), _full(cw.shape)],
        out_specs=[_rows(tm, c, nt, True), _full(cw.shape), _full((1, c))],
        out_shape=[jax.ShapeDtypeStruct((t, c), BF16), jax.ShapeDtypeStruct(cw.shape, F32),
                   jax.ShapeDtypeStruct((1, c), F32)],
        scratch_shapes=[pltpu.VMEM((8, c), F32)], compiler_params=_params(1),
    )(dxs, db, dc, y, xbc, cw)


def _pad_lanes(v, n=DT_PAD):
    return jnp.pad(v, ((0, 0), (0, n - v.shape[1])))


def _local_step(x, target, w, p, after=None, late_weights=None, on_grad=None, on_small=None):
    dtb, alog, dsk = _pad_lanes(p["dt_bias"]), _pad_lanes(p["a_log"]), _pad_lanes(p["d_skip"])
    bs_t = _pad_lanes(p["b_spatial"].T)
    e_heads = (jnp.arange(SSD_INNER)[:, None] // SSD_HEAD_DIM == jnp.arange(LANES)[None, :]).astype(BF16)
    e_heads_t = (jnp.arange(LANES)[:, None] == jnp.arange(SSD_INNER)[None, :] // SSD_HEAD_DIM).astype(BF16)
    e_groups = (jnp.arange(SGU_WIDTH)[:, None] // LANES == jnp.arange(LANES)[None, :]).astype(BF16)

    n1 = _norm_fwd(x, p["norm1_w"], "norm1_fwd", after=after)
    z = _mm(n1, w["z"], "nt", "proj_z", out_dtype=BF16)
    xbc = _mm(n1, w["xbc"], "nt", "proj_xbc", out_dtype=BF16)
    dtr = _mm(n1, w["dt"], "nt", "proj_dt")
    uv = _mm(n1, w["uv"], "nt", "proj_uv", out_dtype=BF16)
    gates = _mm(n1, w["gates"], "nt", "proj_gates", out_dtype=BF16)
    xc, conv_a_out = _conv_a_fwd(xbc, w["conv_a"], p["conv_a_b"])
    y, ya, sprev = _ssd_fwd(xc, dtr, z, dtb, alog, dsk, p["ssd_norm_w"], e_heads_t)
    yb = _sgu_fwd(uv, p["uv_b"], p["v_ln_w"], p["v_ln_b"], p["w_spatial"], bs_t)
    if late_weights is not None:
        w = {**w, **late_weights(ya, yb)}
    pa = _mm(ya, w["branch_a"], "nn", "branch_a", out_dtype=BF16)
    pb = _mm(yb, w["branch_b"], "nn", "branch_b", out_dtype=BF16)
    mix = _merge_fwd(gates, pa, pb, p["b_gate"])
    wide = [(D_MODEL, F32), (D_MODEL, BF16)]
    h1, n2 = _mm_rows(mix, w["out"], "nn", "out_proj", _residual_norm, rows=[x], fulls=[p["norm2_w"]], row_outs=wide)
    up = _mm(n2, w["up"], "nt", "up_proj", out_dtype=BF16)
    act, conv_f_out = _conv_f_fwd(up, w["conv_f"], p["conv_f_b"])
    dh2, dh2b, loss, g_final = _mm_rows(
        act, w["down"], "nn", "down_proj", _loss_and_grad, rows=[h1, target], fulls=[p["final_norm_w"]],
        row_outs=wide, acc_outs=[(8, LANES), (1, D_MODEL)])

    on_grad = on_grad or (lambda name, grads: None)
    g = {"final_norm_w": g_final}
    g["down"] = _wgrad(act, dh2b, "down_wgrad")
    tok = on_grad("w_down", g)
    dact = _mm(dh2b, w["down"], "nt", "down_dgrad", out_dtype=BF16, after=tok)
    dup, g["conv_f"], g["conv_f_b"] = _conv_f_bwd(dact, conv_f_out, up, w["conv_f"])
    g["up"] = _wgrad(dup, n2, "up_wgrad")
    tok = on_grad("w_up", g)
    dh1, dh1b, g["norm2_w"] = _mm_rows(
        dup, w["up"], "nn", "up_dgrad", _norm_backward, rows=[h1, dh2], fulls=[p["norm2_w"]], row_outs=wide,
        acc_outs=[(1, D_MODEL)], after=tok)
    g["out"] = _wgrad(mix, dh1b, "out_wgrad")
    tok = on_grad("w_out", g)
    dgates, dpa, dpb, g["b_gate"] = _mm_rows(
        dh1b, w["out"], "nt", "out_dgrad", _merge_backward, rows=[gates, pa, pb], fulls=[p["b_gate"]],
        row_outs=[(2 * D_MODEL, BF16), (D_MODEL, BF16), (D_MODEL, BF16)], acc_outs=[(1, 2 * D_MODEL)], after=tok)
    g["branch_a"] = _wgrad(ya, dpa, "branch_a_wgrad")
    g["branch_b"] = _wgrad(yb, dpb, "branch_b_wgrad")
    tok = on_grad("w_branch", g)
    dya = _mm(dpa, w["branch_a"], "nt", "branch_a_dgrad", after=tok)
    dyb = _mm(dpb, w["branch_b"], "nt", "branch_b_dgrad", after=tok)
    duv, g["uv_b"], g["v_ln_w"], g["v_ln_b"], g["w_spatial"], dbs_t = _sgu_bwd(
        dyb, uv, p["uv_b"], p["v_ln_w"], p["v_ln_b"], p["w_spatial"], bs_t, e_groups)
    g["b_spatial"] = dbs_t[:, :SGU_GROUPS].T
    dz, dxs, db, dc, ddtr, g["ssd_norm_w"], ddtb, dalog, ddsk = _ssd_bwd(
        dya, y, z, xc, dtr, sprev, dtb, alog, dsk, p["ssd_norm_w"], e_heads, e_heads_t)
    g["dt_bias"], g["a_log"], g["d_skip"] = ddtb, dalog, ddsk
    dxbc, g["conv_a"], g["conv_a_b"] = _conv_a_bwd(dxs, db, dc, conv_a_out, xbc, w["conv_a"])
    tok = on_small(g, loss) if on_small else None
    ddtrb = ddtr.astype(BF16)
    for name, d in (("z", dz), ("xbc", dxbc), ("dt", ddtrb), ("uv", duv), ("gates", dgates)):
        g[name] = _wgrad(d, n1, name + "_wgrad", after=tok)
    tok = on_grad("w_in", g)
    dn1 = _mm([dz, dxbc], [w["z"], w["xbc"]], "nn", "ssd_dgrad", after=tok)
    gx, g["norm1_w"] = _mm_rows(
        [duv, dgates, ddtrb], [w["uv"], w["gates"], w["dt"]], "nn", "in_dgrad",
        lambda r, so_far, h, dres, w_: tuple(t[:1] for t in _norm_backward(r + so_far, h, dres, w_)),
        rows=[dn1, x, dh1], fulls=[p["norm1_w"]], row_outs=wide[:1], acc_outs=[(1, D_MODEL)])
    return loss, gx, g


def _place():
    return lax.axis_index("x"), lax.axis_index("y"), lax.axis_index("c")


def _other_chips(x, y):
    return [(1 - x, y), (x, 1 - y), (1 - x, 1 - y)]


def _all_gather(shards, name):
    n = len(shards)

    def body(*refs):
        ins, outs = refs[:n], refs[n:2 * n]
        send_sems, recv_sems, local_sems = refs[2 * n:]
        x, y, c = _place()
        me, sibling = (x, y, c), (x, y, 1 - c)
        chips = _other_chips(x, y)

        def copy(a, k, block, to, src=None):
            slot = outs[a].at[4 * block[0] + 2 * block[1] + block[2]]
            return pltpu.make_async_remote_copy(
                src_ref=slot if src is None else src, dst_ref=slot, send_sem=send_sems.at[7 * a + k],
                recv_sem=recv_sems.at[7 * a + k], device_id=to, device_id_type=MESH)

        started = []
        for a in range(n):
            mine = pltpu.make_async_copy(ins[a], outs[a].at[4 * x + 2 * y + c], local_sems.at[a])
            mine.start()
            started.append(mine)
        sends = []
        for a in range(n):
            sends.append(copy(a, 0, me, sibling, src=ins[a]))
            sends += [copy(a, 1 + j, me, (*chip, c), src=ins[a]) for j, chip in enumerate(chips)]
        for cp in sends:
            cp.start()
        for a in range(n):
            for j, chip in enumerate(chips):
                copy(a, 1 + j, (*chip, c), me).wait_recv()
                fwd = copy(a, 4 + j, (*chip, c), sibling)
                fwd.start()
                sends.append(fwd)
        for a in range(n):
            copy(a, 0, sibling, me).wait_recv()
            for j, chip in enumerate(chips):
                copy(a, 4 + j, (*chip, 1 - c), me).wait_recv()
        for cp in sends:
            cp.wait_send()
        for mine in started:
            mine.wait()

    any_spec = pl.BlockSpec(memory_space=pl.ANY)
    return pl.pallas_call(
        body, name=name, in_specs=[any_spec] * n, out_specs=[any_spec] * n,
        out_shape=[jax.ShapeDtypeStruct((N_DEV, *s.shape), s.dtype) for s in shards],
        scratch_shapes=[pltpu.SemaphoreType.DMA((7 * n,)), pltpu.SemaphoreType.DMA((7 * n,)),
                        pltpu.SemaphoreType.DMA((n,))],
    )(*shards)


HBM_SPEC = pl.BlockSpec(memory_space=pltpu.HBM)
SEM_SPEC = pl.BlockSpec(memory_space=pltpu.SEMAPHORE)
ANY_SPEC = pl.BlockSpec(memory_space=pl.ANY)
DATAFLOW = pltpu.SideEffectType.DATAFLOW_SIDE_EFFECTING
N_PEERS = N_DEV - 1


def _peers(x, y, c):
    out = []
    for r in range(1, N_DEV):
        fx, fy, fc = r >> 2 & 1, r >> 1 & 1, r & 1
        out.append(((1 - x) if fx else x, (1 - y) if fy else y, (1 - c) if fc else c))
    return out


def _gather_copies(srcs, lands, send_sems, recv_sems, sending, scatter=False):
    x, y, c = _place()
    copies = []
    for a, (src, land) in enumerate(zip(srcs, lands)):
        for j, (px, py, pc) in enumerate(_peers(x, y, c)):
            mine, theirs = 4 * x + 2 * y + c, 4 * px + 2 * py + pc
            block = src.at[theirs if sending else 0] if scatter else src
            copies.append(pltpu.make_async_remote_copy(
                src_ref=block, dst_ref=land.at[mine if sending else theirs], send_sem=send_sems.at[N_PEERS * a + j],
                recv_sem=recv_sems.at[N_PEERS * a + j], device_id=(px, py, pc), device_id_type=MESH))
    return copies


def _gather_start(shards, after, name, scatter=False):
    n = len(shards)
    after = [] if after is None else [after]

    def body(*refs):
        srcs, lands = refs[:n], refs[n:2 * n]
        send_sems, recv_sems = refs[2 * n + len(after):2 * n + len(after) + 2]
        token = refs[-1]
        for cp in _gather_copies(srcs, lands, send_sems, recv_sems, sending=True, scatter=scatter):
            cp.start()
        token[...] = jnp.zeros_like(token)

    lands = [lax.empty(s.shape if scatter else (N_DEV, *s.shape), s.dtype) for s in shards]
    hbm = lambda a: pltpu.with_memory_space_constraint(a, pltpu.HBM)
    out = pl.pallas_call(
        body, name=name,
        out_shape=(pltpu.SemaphoreType.DMA((N_PEERS * n,)), pltpu.SemaphoreType.DMA((N_PEERS * n,)),
                   *[pltpu.HBM(a.shape, a.dtype) for a in (*shards, *lands)], jax.ShapeDtypeStruct((8, LANES), F32)),
        in_specs=[HBM_SPEC] * (2 * n) + [ANY_SPEC] * len(after),
        out_specs=(SEM_SPEC, SEM_SPEC, *[HBM_SPEC] * (2 * n), pl.BlockSpec(memory_space=pltpu.VMEM)),
        input_output_aliases={i: 2 + i for i in range(2 * n)},
        compiler_params=pltpu.CompilerParams(has_side_effects=DATAFLOW),
    )(*[hbm(a) for a in (*shards, *lands)], *after)
    return out[0], out[1], out[2:2 + n], out[2 + n:2 + 2 * n], out[-1]


def _gather_wait(send_sems, recv_sems, shards, lands, after, name, scatter=False):
    n = len(shards)
    after = tuple(after)

    def body(*refs):
        srcs, lands_ = refs[:n], refs[n:2 * n]
        send, recv = refs[2 * n:2 * n + 2]
        for cp in _gather_copies(srcs, lands_, send, recv, sending=False, scatter=scatter):
            cp.wait_send()
            cp.wait_recv()

    out = pl.pallas_call(
        body, name=name, out_shape=tuple(pltpu.HBM(a.shape, a.dtype) for a in (*shards, *lands)),
        in_specs=[HBM_SPEC] * (2 * n) + [SEM_SPEC, SEM_SPEC] + [ANY_SPEC] * len(after),
        out_specs=tuple([HBM_SPEC] * (2 * n)), input_output_aliases={i: i for i in range(2 * n)},
        compiler_params=pltpu.CompilerParams(has_side_effects=DATAFLOW),
    )(*shards, *lands, send_sems, recv_sems, *after)
    return out[:n], out[n:]


def _chip_copies(src, land, send_sems, recv_sems):
    x, y, c = _place()
    return [pltpu.make_async_remote_copy(
        src_ref=src.at[2 * cx + cy], dst_ref=land.at[j], send_sem=send_sems.at[j], recv_sem=recv_sems.at[j],
        device_id=(cx, cy, c), device_id_type=MESH) for j, (cx, cy) in enumerate(_other_chips(x, y))]


def _chips_start(q, name):
    def body(q_ref, land_ref, send_sems, recv_sems, q_thru, land_thru, token):
        for cp in _chip_copies(q_ref, land_ref, send_sems, recv_sems):
            cp.start()
        token[...] = jnp.zeros_like(token)

    land = lax.empty((3, *q.shape[1:]), q.dtype)
    return pl.pallas_call(
        body, name=name,
        out_shape=(pltpu.SemaphoreType.DMA((3,)), pltpu.SemaphoreType.DMA((3,)), pltpu.HBM(q.shape, q.dtype),
                   pltpu.HBM(land.shape, land.dtype), jax.ShapeDtypeStruct((8, LANES), F32)),
        in_specs=[HBM_SPEC, HBM_SPEC],
        out_specs=(SEM_SPEC, SEM_SPEC, HBM_SPEC, HBM_SPEC, pl.BlockSpec(memory_space=pltpu.VMEM)),
        input_output_aliases={0: 2, 1: 3}, compiler_params=pltpu.CompilerParams(has_side_effects=DATAFLOW),
    )(pltpu.with_memory_space_constraint(q, pltpu.HBM), pltpu.with_memory_space_constraint(land, pltpu.HBM))


def _chips_wait(send_sems, recv_sems, q, land, after, name):
    def body(q_ref, land_ref, send, recv, after_ref, q_out, land_out):
        for cp in _chip_copies(q_ref, land_ref, send, recv):
            cp.wait_send()
            cp.wait_recv()

    return pl.pallas_call(
        body, name=name, out_shape=(pltpu.HBM(q.shape, q.dtype), pltpu.HBM(land.shape, land.dtype)),
        in_specs=[HBM_SPEC, HBM_SPEC, SEM_SPEC, SEM_SPEC, ANY_SPEC], out_specs=(HBM_SPEC, HBM_SPEC),
        input_output_aliases={0: 0, 1: 1}, compiler_params=pltpu.CompilerParams(has_side_effects=DATAFLOW),
    )(q, land, send_sems, recv_sems, after)[1]


def _exchange_cores(part, name):
    def body(in_ref, out_ref, send_sems, recv_sems):
        x, y, c = _place()
        copies = [pltpu.make_async_remote_copy(
            src_ref=in_ref.at[2 * k + (1 - c)], dst_ref=out_ref.at[k], send_sem=send_sems.at[k],
            recv_sem=recv_sems.at[k], device_id=(x, y, 1 - c), device_id_type=MESH) for k in range(4)]
        for cp in copies:
            cp.start()
        for cp in copies:
            cp.wait()

    return pl.pallas_call(
        body, name=name, in_specs=[ANY_SPEC], out_specs=ANY_SPEC,
        out_shape=jax.ShapeDtypeStruct((4, *part.shape[1:]), part.dtype),
        scratch_shapes=[pltpu.SemaphoreType.DMA((4,)), pltpu.SemaphoreType.DMA((4,))],
    )(part)


def _chip_sum(part, got, place, name, tr=256):
    _, r, c = part.shape
    tr, tc = _tile2d(r, c, tr)

    def body(place_ref, p_ref, g_ref, q_ref, own_ref):
        s = p_ref[0].astype(F32) + g_ref[0].astype(F32)
        q_ref[0] = s.astype(BF16)

        @pl.when(pl.program_id(2) == place_ref[1])
        def _():
            own_ref[...] = s

    grid_spec = pltpu.PrefetchScalarGridSpec(
        num_scalar_prefetch=1, grid=(r // tr, c // tc, 4),
        in_specs=[pl.BlockSpec((1, tr, tc), lambda i, j, k, pr: (2 * k + pr[0], i, j)),
                  pl.BlockSpec((1, tr, tc), lambda i, j, k, pr: (k, i, j))],
        out_specs=[pl.BlockSpec((1, tr, tc), lambda i, j, k, pr: (k, i, j)),
                   pl.BlockSpec((tr, tc), lambda i, j, k, pr: (i, j))])
    return pl.pallas_call(
        body, name=name, grid_spec=grid_spec,
        out_shape=[jax.ShapeDtypeStruct((4, r, c), BF16), jax.ShapeDtypeStruct((r, c), F32)],
        compiler_params=_params(3),
    )(place, part, got)


def _sum_adamw(own, got, w, m, v, name):
    r, c = own.shape
    tc = 4 * LANES

    def body(own_ref, got_ref, w_ref, m_ref, v_ref, g_ref, d_ref, nm_ref, nv_ref):
        g = own_ref[...]
        for j in range(3):
            g = g + got_ref[j].astype(F32)
        two_d = lambda ref: ref[...].reshape(r, tc)
        delta, nm, nv = _adamw(two_d(w_ref), g, two_d(m_ref), two_d(v_ref))
        for ref, val in ((g_ref, g), (d_ref, delta), (nm_ref, nm), (nv_ref, nv)):
            ref[...] = val.reshape(ref.shape)

    wblk = pl.BlockSpec((r, 1, tc), lambda j: (0, 0, j))
    return pl.pallas_call(
        body, name=name, grid=(c // tc,),
        in_specs=[pl.BlockSpec((r, tc), lambda j: (0, j)), pl.BlockSpec((3, r, tc), lambda j: (0, 0, j)),
                  wblk, wblk, wblk],
        out_specs=[wblk] * 4, out_shape=[jax.ShapeDtypeStruct(w.shape, F32)] * 4, compiler_params=_params(1),
    )(own, got, w, m, v)


def _adamw(w, g, m, v):
    m = ADAM_B1 * m + (1.0 - ADAM_B1) * g
    v = ADAM_B2 * v + (1.0 - ADAM_B2) * jnp.square(g)
    m_hat = m / (1.0 - ADAM_B1 ** ADAM_STEP)
    v_hat = v / (1.0 - ADAM_B2 ** ADAM_STEP)
    return -ADAM_LR * (m_hat / (jnp.sqrt(v_hat) + ADAM_EPS) + ADAM_WD * w), m, v


def _sum8_adamw(part, got, place, w, m, v, name, tr=256):
    r, c = w.shape
    tr, tc = _tile2d(r, c, tr)
    blk = pl.BlockSpec((tr, tc), lambda i, j, pr: (i, j))

    def body(place_ref, own_ref, got_ref, w_ref, m_ref, v_ref, g_ref, d_ref, nm_ref, nv_ref):
        dev = 2 * place_ref[1] + place_ref[0]
        g = jnp.zeros((tr, tc), F32)
        for d in range(N_DEV):
            g = g + jnp.where(dev == d, own_ref[0], got_ref[d]).astype(F32)
        g_ref[...] = g
        d_ref[...], nm_ref[...], nv_ref[...] = _adamw(w_ref[...], g, m_ref[...], v_ref[...])

    grid_spec = pltpu.PrefetchScalarGridSpec(
        num_scalar_prefetch=1, grid=(r // tr, c // tc),
        in_specs=[pl.BlockSpec((1, tr, tc), lambda i, j, pr: (2 * pr[1] + pr[0], i, j)),
                  pl.BlockSpec((N_DEV, tr, tc), lambda i, j, pr: (0, i, j)), blk, blk, blk],
        out_specs=[blk] * 4)
    return pl.pallas_call(
        body, name=name, grid_spec=grid_spec, out_shape=[jax.ShapeDtypeStruct(w.shape, F32)] * 4,
        compiler_params=_params(2),
    )(place, part, got, w, m, v)


VECTORS = ["norm1_w", "b_gate", "conv_a_b", "dt_bias", "a_log", "d_skip", "ssd_norm_w", "uv_b", "v_ln_w", "v_ln_b",
           "norm2_w", "conv_f_b", "final_norm_w"]
SMALL_ORDER = VECTORS + ["w_spatial", "b_spatial", "conv_a_w", "conv_f_w"]


ROW_VECTORS = VECTORS[1:]


def _small_adamw(gathered, w, m, v):
    sizes = {n: w[n].shape[1] for n in ROW_VECTORS}
    offs, off = {}, 0
    for n in ROW_VECTORS:
        offs[n] = off
        off += -(-sizes[n] // LANES) * LANES
    loss_off = off
    k = len(SMALL_ORDER)
    n_g = len(gathered)

    def body(*refs):
        row_ref, ws_ref, bs_ref, ca_ref, cf_ref, n1_ref = refs[:n_g]
        w_refs, m_refs, v_refs = (dict(zip(SMALL_ORDER, refs[n_g + i * k:n_g + (i + 1) * k])) for i in range(3))
        outs = refs[n_g + 3 * k:]
        x, y, c = _place()
        dev = 4 * x + 2 * y + c

        def total(ref):
            s = ref[0]
            for d in range(1, N_DEV):
                s = s + ref[d]
            return s

        row = total(row_ref)
        grads = {n: row[:, offs[n]:offs[n] + sizes[n]] for n in ROW_VECTORS}
        grads["norm1_w"], grads["w_spatial"], grads["b_spatial"] = total(n1_ref), total(ws_ref), total(bs_ref)
        for n, ref in (("conv_a_w", ca_ref), ("conv_f_w", cf_ref)):
            whole, cols = total(ref), w_refs[n].shape[1]
            mine = whole[:, :cols]
            for d in range(1, N_DEV):
                mine = jnp.where(dev == d, whole[:, d * cols:(d + 1) * cols], mine)
            grads[n] = mine
        for i, n in enumerate(SMALL_ORDER):
            outs[4 * i][...] = grads[n]
            outs[4 * i + 1][...], outs[4 * i + 2][...], outs[4 * i + 3][...] = _adamw(
                w_refs[n][...], grads[n], m_refs[n][...], v_refs[n][...])
        outs[4 * k][...] = row[:, loss_off:loss_off + LANES]

    out = pl.pallas_call(
        body, name="adamw_small",
        out_shape=[jax.ShapeDtypeStruct(w[n].shape, F32) for n in SMALL_ORDER for _ in range(4)]
        + [jax.ShapeDtypeStruct((1, LANES), F32)],
        compiler_params=_params(0),
    )(*gathered, *[t[n] for t in (w, m, v) for n in SMALL_ORDER])
    return [dict(zip(SMALL_ORDER, out[j:4 * k:4])) for j in range(4)] + [out[4 * k]]


SMALL = ["norm1_w", "b_gate", "conv_a_b", "dt_bias", "a_log", "d_skip", "ssd_norm_w", "uv_b", "v_ln_w", "v_ln_b",
         "w_spatial", "b_spatial", "norm2_w", "conv_f_b", "final_norm_w"]
BIG = ["w_in", "w_branch", "w_out", "w_up", "w_down"]
TRANSPOSED = ("w_in", "w_up")
WEIGHTS = ["norm1_w", "w_in", "b_gate", "conv_a_w", "conv_a_b", "dt_bias", "a_log", "d_skip", "ssd_norm_w", "uv_b",
           "v_ln_w", "v_ln_b", "w_spatial", "b_spatial", "w_branch", "w_out", "norm2_w", "w_up", "conv_f_w",
           "conv_f_b", "w_down", "final_norm_w"]
IN_SPLITS = [("z", 0, 2048), ("xbc", 2048, 5120), ("dt", 5120, 5152), ("uv", 5152, 7200), ("gates", 7200, 9248)]


def _columns_from_devices(a):
    return a.transpose(1, 0, 2).reshape(a.shape[1], -1)


def kernel(x, norm1_w, w_in, b_gate, conv_a_w, conv_a_b, dt_bias, a_log, d_skip, ssd_norm_w, uv_b, v_ln_w, v_ln_b, w_spatial, b_spatial, w_branch, w_out, norm2_w, w_up, conv_f_w, conv_f_b, w_down, final_norm_w, loss_target, m_norm1_w, m_w_in, m_b_gate, m_conv_a_w, m_conv_a_b, m_dt_bias, m_a_log, m_d_skip, m_ssd_norm_w, m_uv_b, m_v_ln_w, m_v_ln_b, m_w_spatial, m_b_spatial, m_w_branch, m_w_out, m_norm2_w, m_w_up, m_conv_f_w, m_conv_f_b, m_w_down, m_final_norm_w, v_norm1_w, v_w_in, v_b_gate, v_conv_a_w, v_conv_a_b, v_dt_bias, v_a_log, v_d_skip, v_ssd_norm_w, v_uv_b, v_v_ln_w, v_v_ln_b, v_w_spatial, v_b_spatial, v_w_branch, v_w_out, v_norm2_w, v_w_up, v_conv_f_w, v_conv_f_b, v_w_down, v_final_norm_w):
    args = dict(locals())
    wts = {n: args[n] for n in WEIGHTS}
    mom = {n: args["m_" + n] for n in WEIGHTS}
    var = {n: args["v_" + n] for n in WEIGHTS}
    cx, cy, cc = _place()
    dev = 4 * cx + 2 * cy + cc
    place = jnp.stack([cc, 2 * cx + cy]).astype(jnp.int32)

    def shard2d(n, a):
        return a[0].T if n in TRANSPOSED else a[0]

    def unshard(n, b):
        return (b.T if n in TRANSPOSED else b)[None]

    g_in, g_conv_a, g_conv_f = _all_gather(
        [shard2d("w_in", w_in).astype(BF16), conv_a_w[0], conv_f_w[0]], "gather_w_in")
    late = [shard2d(n, wts[n]).astype(BF16) for n in BIG[1:]]
    send_sems, recv_sems, late, lands, token = _gather_start(late, g_in, "gather_late_start")
    w_in_rows = g_in.reshape(-1, D_MODEL)
    w = {name: w_in_rows[lo:hi] for name, lo, hi in IN_SPLITS}
    w["dt"] = jnp.pad(w["dt"], ((0, DT_PAD - SSD_HEADS), (0, 0)))
    w["conv_a"] = _columns_from_devices(g_conv_a)
    w["conv_f"] = _columns_from_devices(g_conv_f)

    def late_weights(*after):
        mine, got = _gather_wait(send_sems, recv_sems, late, lands, after, "gather_late_wait")
        g_branch, g_out, g_up, g_down = [lax.dynamic_update_index_in_dim(land, own, dev, 0).reshape(-1, D_MODEL)
                                         for land, own in zip(got, mine)]
        return {"branch_a": g_branch[:SSD_INNER], "branch_b": g_branch[SSD_INNER:], "out": g_out, "up": g_up,
                "down": g_down}

    in_flight = {}

    def on_grad(n, g):
        part = {"w_in": lambda: jnp.concatenate([g[name][:hi - lo] for name, lo, hi in IN_SPLITS], axis=0),
                "w_branch": lambda: jnp.concatenate([g["branch_a"], g["branch_b"]], axis=0),
                "w_out": lambda: g["out"], "w_up": lambda: g["up"], "w_down": lambda: g["down"]}[n]()
        part = part.reshape(N_DEV, -1, D_MODEL)
        if n == "w_in":
            q, own = _chip_sum(part, _exchange_cores(part, "to_other_core_w_in"), place, "chip_sum_w_in")
            send, recv, q, land, tok = _chips_start(q, "to_other_chips_start_w_in")
            in_flight[n] = (own, send, recv, q, land)
            return tok
        send, recv, (part,), (land,), tok = _gather_start([part], None, f"to_owners_start_{n}", scatter=True)
        in_flight[n] = (part, send, recv, land)
        return tok

    p = {n: wts[n][0] if wts[n].ndim > 2 else wts[n].reshape(1, -1) for n in SMALL}
    small_flight = []

    def on_small(g, loss):
        arrays = [jnp.concatenate([g[n] for n in ROW_VECTORS] + [loss[:1]], axis=1), g["w_spatial"], g["b_spatial"],
                  g["conv_a"], g["conv_f"]]
        *flight, tok = _gather_start(arrays, g["conv_a"], "gather_small_start")
        small_flight.append(flight)
        return tok

    loss, gx, g = _local_step(x[0], loss_target[0], w, p, after=token, late_weights=late_weights, on_grad=on_grad,
                              on_small=on_small)
    *flight, _ = _gather_start([g["norm1_w"]], gx, "gather_norm1_start")
    small_flight.append(flight)

    grads, delta, new_m, new_v = {}, {}, {}, {}

    def big_adamw(n, after):
        if n == "w_in":
            own, send, recv, q, land = in_flight[n]
            got = _chips_wait(send, recv, q, land, after, "to_other_chips_wait_w_in")
            out = _sum_adamw(own, got, *[t[n].transpose(2, 0, 1) for t in (wts, mom, var)], "adamw_w_in")
            grads[n], delta[n], new_m[n], new_v[n] = [o.transpose(1, 2, 0) for o in out]
            return out[1]
        part, send, recv, land = in_flight[n]
        (part,), (got,) = _gather_wait(send, recv, [part], [land], [after], f"to_owners_wait_{n}", scatter=True)
        out = _sum8_adamw(part, got, place, *[shard2d(n, t[n]) for t in (wts, mom, var)], f"adamw_{n}")
        grads[n], delta[n], new_m[n], new_v[n] = [unshard(n, o) for o in out]
        return out[1]

    after = gx
    for n in ("w_down", "w_up", "w_out", "w_branch"):
        after = big_adamw(n, after)
    gathered = []
    for (send, recv, mine, land), name in zip(small_flight, ("gather_small_wait", "gather_norm1_wait")):
        mine, got = _gather_wait(send, recv, mine, land, [after], name)
        gathered += [lax.dynamic_update_index_in_dim(full, own, dev, 0) for full, own in zip(got, mine)]
    small = [{n: t[n][0] if t[n].ndim > 2 else t[n].reshape(1, -1) for n in SMALL_ORDER} for t in (wts, mom, var)]
    *outs, loss = _small_adamw(gathered, *small)
    for tgt, out in zip((grads, delta, new_m, new_v), outs):
        tgt.update({n: out[n].reshape(wts[n].shape) for n in SMALL_ORDER})
    big_adamw("w_in", loss)
    loss = loss[0, 0]

    return (loss, gx[None], *[grads[n] for n in WEIGHTS], *[delta[n] for n in WEIGHTS],
            *[new_m[n] for n in WEIGHTS], *[new_v[n] for n in WEIGHTS])
```

```python
import functools

import jax
import jax.numpy as jnp
from jax import lax
from jax.experimental import pallas as pl
from jax.experimental.pallas import tpu as pltpu

F32, BF16 = jnp.float32, jnp.bfloat16
HIGHEST = lax.Precision.HIGHEST

D_MODEL = 1024
SSD_INNER = 2048
SSD_HEAD_DIM = 64
SSD_HEADS = 32
SSD_GROUPS = 4
SSD_STATE = 128
SSD_BC = SSD_GROUPS * SSD_STATE
SSD_XBC = SSD_INNER + 2 * SSD_BC
SSD_CONV = 4
CHUNK = 128
N_PAIRS = SSD_HEADS // 2
PAIRS_PER_GROUP = N_PAIRS // SSD_GROUPS
SGU_WIDTH = 1024
SGU_GROUPS = 8
D_FF = 2816
FFN_CONV = 3
NORM_EPS = 1e-6
LN_EPS = 1e-5
LANES = 128
DT_PAD = LANES

ADAM_LR, ADAM_B1, ADAM_B2, ADAM_EPS, ADAM_WD, ADAM_STEP = 0.001, 0.9, 0.999, 1e-08, 0.01, 10

N_DEV = 8
VMEM_LIMIT = 56 * 1024 * 1024
MESH = pl.DeviceIdType.MESH


def _params(n_grid, **kw):
    sem = dict(dimension_semantics=("arbitrary",) * n_grid) if n_grid else {}
    return pltpu.CompilerParams(vmem_limit_bytes=VMEM_LIMIT, **sem, **kw)


def _tile(n, pref):
    t = (min(pref, n) // LANES) * LANES
    while n % t:
        t -= LANES
    return t


def _row_tile(r, pref):
    for t in range(min(pref, r) // 16 * 16, 0, -16):
        if r % t == 0:
            return t
    return r


def _tile2d(r, c, rows):
    if r % 16 == 0:
        return _row_tile(r, rows), c
    return r, _tile(c, 2 * LANES)


def _rows(tm, n, nt=None, rev=False):
    if rev:
        return pl.BlockSpec((tm, n), lambda i: (nt - 1 - i, 0))
    return pl.BlockSpec((tm, n), lambda i: (i, 0))


def _halo(tm, n, rows=8):
    per = tm // rows
    return pl.BlockSpec((rows, n), lambda i: (jnp.maximum(i * per - 1, 0), 0))


def _full(shape):
    nd = len(shape)
    return pl.BlockSpec(shape, lambda *_: (0,) * nd)


def _rms(x, w, eps=NORM_EPS):
    return x * lax.rsqrt(jnp.mean(x * x, axis=-1, keepdims=True) + eps) * w


def _layer_norm(x, w, b):
    mu = jnp.mean(x, axis=-1, keepdims=True)
    var = jnp.mean(jnp.square(x - mu), axis=-1, keepdims=True)
    return (x - mu) * lax.rsqrt(var + LN_EPS) * w + b


def _sigmoid(x):
    return 1.0 / (1.0 + jnp.exp(-x))


def _silu(x):
    return x * _sigmoid(x)


def _dsilu(x):
    s = _sigmoid(x)
    return s * (1.0 + x * (1.0 - s))


def _silu_and_grad(x):
    s = _sigmoid(x)
    return x * s, s * (1.0 + x * (1.0 - s))


def _softplus(x):
    return jnp.maximum(x, 0.0) + jnp.log(1.0 + jnp.exp(-jnp.abs(x)))


def _gelu(x):
    return jax.nn.gelu(x)


def _dot(a, b):
    return jnp.dot(a, b, preferred_element_type=F32)


def _dot_nt(a, b):
    return lax.dot_general(a, b, (((1,), (1,)), ((), ())), preferred_element_type=F32)


def _dot_tn(a, b):
    return lax.dot_general(a, b, (((0,), (0,)), ((), ())), preferred_element_type=F32)


def _dot_split(p, e):
    hi = p.astype(BF16)
    lo = (p - hi.astype(F32)).astype(BF16)
    return _dot(hi, e) + _dot(lo, e)


def _colsum(x):
    return jnp.sum(x, axis=0, keepdims=True)


def _shift_down(x, halo, j):
    xs = pltpu.roll(x, j, 0)
    hs = pltpu.roll(halo, j, 0)
    r8 = lax.broadcasted_iota(jnp.int32, hs.shape, 0)
    return jnp.concatenate([jnp.where(r8 < j, hs, xs[:8]), xs[8:]], axis=0)


def _shift_up(x, nxt, j):
    n = x.shape[0]
    xs = pltpu.roll(x, n - j, 0)
    ns = pltpu.roll(nxt, 8 - j, 0)
    r8 = lax.broadcasted_iota(jnp.int32, ns.shape, 0)
    return jnp.concatenate([xs[:n - 8], jnp.where(r8 >= 8 - j, ns, xs[n - 8:])], axis=0)


def _causal_conv(x, halo, w, b):
    k = w.shape[0]
    y = b + w[k - 1:k, :] * x
    for j in range(1, k):
        y = y + w[k - 1 - j:k - j, :] * _shift_down(x, halo, j)
    return y


def _causal_conv_bwd(dy, nxt, x, w):
    k = w.shape[0]
    dx = w[k - 1:k, :] * dy
    dw = [_colsum(dy * x)]
    for j in range(1, k):
        dyj = _shift_up(dy, nxt, j)
        dx = dx + w[k - 1 - j:k - j, :] * dyj
        dw.append(_colsum(dyj * x))
    return dx, jnp.concatenate(dw[::-1], axis=0)


MM_TILE_PREF = 1408
MM_VMEM_BUDGET = 40 * 1024 * 1024


def _mm_tiles(m, n, k, out_bytes):
    tm, tn = _tile(m, MM_TILE_PREF), _tile(n, MM_TILE_PREF)
    need = lambda tm, tn: 2 * (2 * k * (tm + tn) + out_bytes * tm * tn)
    while need(tm, tn) > MM_VMEM_BUDGET:
        if tn >= tm and tn > LANES:
            tn = _tile(n, tn - LANES)
        else:
            tm = _tile(m, tm - LANES)
    return tm, tn


def _mm(a, b, dims, name, acc=None, out_dtype=F32, after=None):
    a_list, b_list = (list(a), list(b)) if isinstance(a, (list, tuple)) else ([a], [b])
    k_axis, m_axis = (0, 1) if dims == "tn" else (1, 0)
    m, ks = a_list[0].shape[m_axis], [x.shape[k_axis] for x in a_list]
    n = b_list[0].shape[0] if dims == "nt" else b_list[0].shape[1]
    tm, tn = _mm_tiles(m, n, sum(ks), 4 * (2 if acc is not None else 1))
    a_specs = [pl.BlockSpec((k, tm), lambda j, i: (0, i)) if dims == "tn" else pl.BlockSpec((tm, k), lambda j, i: (i, 0))
               for k in ks]
    b_specs = [pl.BlockSpec((tn, k), lambda j, i: (j, 0)) if dims == "nt" else pl.BlockSpec((k, tn), lambda j, i: (0, j))
               for k in ks]
    o_spec = pl.BlockSpec((tm, tn), lambda j, i: (i, j))
    dot = {"nn": _dot, "nt": _dot_nt, "tn": _dot_tn}[dims]
    n_pairs = len(ks)

    def body(*refs):
        rest = refs[2 * n_pairs:]
        r = dot(refs[0][...], refs[n_pairs][...])
        for i in range(1, n_pairs):
            r = r + dot(refs[i][...], refs[n_pairs + i][...])
        if acc is not None:
            r = r + rest[0][...]
        rest[-1][...] = r.astype(out_dtype)

    ins, specs = a_list + b_list, a_specs + b_specs
    if acc is not None:
        ins.append(acc)
        specs.append(o_spec)
    if after is not None:
        ins.append(after)
        specs.append(pl.BlockSpec(memory_space=pl.ANY))
    return pl.pallas_call(
        body, name=name, grid=(n // tn, m // tm), in_specs=specs, out_specs=o_spec,
        out_shape=jax.ShapeDtypeStruct((m, n), out_dtype), compiler_params=_params(2),
    )(*ins)


def _mm_rows(a, b, dims, name, fn, rows=(), fulls=(), row_outs=(), acc_outs=(), after=None, summed=True):
    a_list, b_list = (list(a), list(b)) if isinstance(a, (list, tuple)) else ([a], [b])
    m, ks = a_list[0].shape[0], [x.shape[1] for x in a_list]
    n, k = (b_list[0].shape[0] if dims == "nt" else b_list[0].shape[1]), sum(ks)
    per_row = 2 * k + 8 * n + sum(4 * r.shape[1] for r in rows) + sum(c * jnp.dtype(d).itemsize for c, d in row_outs)
    tm = _tile(m, 1024)
    while 2 * tm * per_row + 4 * k * n > MM_VMEM_BUDGET:
        tm = _tile(m, tm - LANES)
    dot = _dot_nt if dims == "nt" else _dot
    n_pairs = len(ks)
    n_in = 2 * n_pairs + len(rows) + len(fulls) + (after is not None)

    def body(*refs):
        ins, outs = refs[:n_in], refs[n_in:]
        row_refs, acc_refs = outs[:len(row_outs)], outs[len(row_outs):]

        @pl.when(pl.program_id(0) == 0)
        def _():
            for r in acc_refs:
                r[...] = jnp.zeros_like(r)

        products = [dot(ins[i][...], ins[n_pairs + i][...]) for i in range(n_pairs)]
        result = functools.reduce(lambda p, q: p + q, products) if summed else products
        new_rows, incs = fn(result, *[r[...] for r in ins[2 * n_pairs:2 * n_pairs + len(rows) + len(fulls)]])
        for r, val in zip(row_refs, new_rows):
            r[...] = val.astype(r.dtype)
        for r, inc in zip(acc_refs, incs):
            r[...] += inc

    extra, extra_specs = ([after], [pl.BlockSpec(memory_space=pl.ANY)]) if after is not None else ([], [])
    return pl.pallas_call(
        body, name=name, grid=(m // tm,),
        in_specs=[_rows(tm, k_i) for k_i in ks] + [_full(x.shape) for x in b_list]
        + [_rows(tm, r.shape[1]) for r in rows] + [_full(f.shape) for f in fulls] + extra_specs,
        out_specs=[_rows(tm, c) for c, _ in row_outs] + [_full(s) for s in acc_outs],
        out_shape=[jax.ShapeDtypeStruct((m, c), d) for c, d in row_outs]
        + [jax.ShapeDtypeStruct(s, F32) for s in acc_outs],
        compiler_params=_params(1),
    )(*a_list, *b_list, *rows, *fulls, *extra)


def _residual_norm(o, x, w):
    h = x + o
    return (h, _rms(h, w)), ()


def _norm_backward(dn, h, dres, w):
    _, vjp = jax.vjp(_rms, h, w)
    dh, dw = vjp(dn)
    dh = dh + dres
    return (dh, dh), (dw,)


def _loss_and_grad(dn, h1, target, w):
    yf, vjp = jax.vjp(_rms, h1 + dn, w)
    err = yf - target
    loss = 0.5 * jnp.sum(jnp.mean(err * err, axis=-1, keepdims=True))
    dh, dw = vjp(err * (1.0 / err.shape[-1]))
    return (dh, dh), (jnp.full((8, LANES), loss, F32), dw)


def _wgrad(a, d, name, after=None):
    return _mm(a, d, "tn", name, out_dtype=BF16, after=after)


def _norm_fwd(x, w, name, after=None, tm=512):
    t, d = x.shape

    def body(x_ref, w_ref, *rest):
        rest[-1][...] = _rms(x_ref[...], w_ref[...]).astype(BF16)

    extra, extra_specs = ([after], [_full(after.shape)]) if after is not None else ([], [])
    return pl.pallas_call(
        body, name=name, grid=(t // tm,), in_specs=[_rows(tm, d), _full((1, d))] + extra_specs,
        out_specs=_rows(tm, d), out_shape=jax.ShapeDtypeStruct((t, d), BF16), compiler_params=_params(1),
    )(x, w, *extra)


def _conv_a_fwd(xbc, cw, cb, tm=256):
    t, c = xbc.shape

    def body(x_ref, h_ref, w_ref, b_ref, o_ref, y_ref):
        halo = jnp.where(pl.program_id(0) > 0, h_ref[...].astype(F32)[8:], 0.0)
        y = _causal_conv(x_ref[...].astype(F32), halo, w_ref[...], b_ref[...])
        y_ref[...] = y.astype(BF16)
        o_ref[...] = _silu(y)

    return pl.pallas_call(
        body, name="conv_a_fwd", grid=(t // tm,),
        in_specs=[_rows(tm, c), _halo(tm, c, rows=16), _full(cw.shape), _full((1, c))],
        out_specs=[_rows(tm, c)] * 2,
        out_shape=[jax.ShapeDtypeStruct((t, c), F32), jax.ShapeDtypeStruct((t, c), BF16)], compiler_params=_params(1),
    )(xbc, xbc, cw, cb)


def _ssd_common(dtr, dtb, alog, e_t):
    row = lax.broadcasted_iota(jnp.int32, (CHUNK, CHUNK), 0)
    col = lax.broadcasted_iota(jnp.int32, (CHUNK, CHUNK), 1)
    causal = row >= col
    dt = _softplus(dtr + dtb)
    a = -jnp.exp(alog)
    acum = jnp.dot(causal.astype(F32), dt * a, precision=HIGHEST, preferred_element_type=F32)
    spread = lambda v: _dot(v.astype(BF16), e_t)
    elast = jnp.broadcast_to(jnp.exp(acum[CHUNK - 1:CHUNK, :]), (8, LANES))
    return dict(dt=dt, a=a, acum=acum, acum_t=acum.T, causal=causal, row=row, col=col, lane_lo=col < SSD_HEAD_DIM,
                dt_x=_dot_split(dt, e_t), ecol_x=spread(jnp.exp(acum)), elast_x=_dot_split(elast, e_t)[0:1],
                dsr_x=spread(jnp.exp(acum[CHUNK - 1:CHUNK, :] - acum)))


def _head_decay(c, h, transposed=False):
    d = c["acum"][:, h:h + 1] - c["acum_t"][h:h + 1, :]
    if transposed:
        return jnp.exp(jnp.where(c["row"] <= c["col"], -d, -jnp.inf))
    return jnp.exp(jnp.where(c["causal"], d, -jnp.inf))


def _ssd_fwd(xc, dtr, z, dtb, alog, dsk, nw, e_t):
    t = xc.shape[0]
    nc = t // CHUNK

    def body(xs_ref, b_ref, c_ref, dtr_ref, z_ref, dtb_ref, alog_ref, dsk_ref, nw_ref, et_ref,
             y_ref, ya_ref, sp_ref, s_scr):
        @pl.when(pl.program_id(0) == 0)
        def _():
            s_scr[...] = jnp.zeros_like(s_scr)

        c = _ssd_common(dtr_ref[...], dtb_ref[...], alog_ref[...], et_ref[...])
        lane_lo = c["lane_lo"]
        dsk = dsk_ref[...]
        for g in range(SSD_GROUPS):
            gs = slice(g * SSD_STATE, (g + 1) * SSD_STATE)
            bg_t, cg = b_ref[:, gs].T.astype(BF16), c_ref[:, gs].astype(BF16)
            cb = _dot(cg, bg_t)
            for pp in range(PAIRS_PER_GROUP):
                j = g * PAIRS_PER_GROUP + pp
                ps = slice(j * LANES, (j + 1) * LANES)
                x = xs_ref[:, ps]
                ecol, dsr = c["ecol_x"][:, ps], c["dsr_x"][:, ps]
                xdt = x * c["dt_x"][:, ps]
                xb = xdt.astype(BF16)
                zero = jnp.zeros_like(xb)
                yd = (_dot((cb * _head_decay(c, 2 * j)).astype(BF16), jnp.where(lane_lo, xb, zero))
                      + _dot((cb * _head_decay(c, 2 * j + 1)).astype(BF16), jnp.where(lane_lo, zero, xb)))
                sp = s_scr[j]
                yo = ecol * _dot(cg, sp.astype(BF16))
                st = _dot(bg_t, (xdt * dsr).astype(BF16))
                sp_ref[0, j] = sp
                s_scr[j] = c["elast_x"][:, ps] * sp + st
                dskp = jnp.where(lane_lo[0:1], dsk[:, 2 * j:2 * j + 1], dsk[:, 2 * j + 1:2 * j + 2])
                y_ref[:, ps] = yd + yo + dskp * x
        ya_ref[...] = _rms(y_ref[...] * _silu(z_ref[...].astype(F32)), nw_ref[...]).astype(BF16)

    ck = lambda n, col=0: pl.BlockSpec((CHUNK, n), lambda c: (c, col))
    return pl.pallas_call(
        body, name="ssd_fwd", grid=(nc,),
        in_specs=[ck(SSD_INNER), ck(SSD_BC, SSD_INNER // SSD_BC), ck(SSD_BC, SSD_INNER // SSD_BC + 1), ck(DT_PAD),
                  ck(SSD_INNER), _full((1, DT_PAD)), _full((1, DT_PAD)), _full((1, DT_PAD)),
                  _full((1, SSD_INNER)), _full(e_t.shape)],
        out_specs=[ck(SSD_INNER), ck(SSD_INNER),
                   pl.BlockSpec((1, N_PAIRS, SSD_STATE, LANES), lambda c: (c, 0, 0, 0))],
        out_shape=[jax.ShapeDtypeStruct((t, SSD_INNER), F32), jax.ShapeDtypeStruct((t, SSD_INNER), BF16),
                   jax.ShapeDtypeStruct((nc, N_PAIRS, SSD_STATE, LANES), F32)],
        scratch_shapes=[pltpu.VMEM((N_PAIRS, SSD_STATE, LANES), F32)], compiler_params=_params(1),
    )(xc, xc, xc, dtr, z, dtb, alog, dsk, nw, e_t)


def _ssd_bwd(dya, y, z, xc, dtr, sprev, dtb, alog, dsk, nw, e_heads, e_t):
    t = xc.shape[0]
    nc = t // CHUNK

    def body(dya_ref, y_ref, z_ref, xs_ref, b_ref, c_ref, dtr_ref, sp_ref, dtb_ref, alog_ref, dsk_ref, nw_ref, e_ref,
             et_ref, dz_ref, dxs_ref, db_ref, dc_ref, ddtr_ref, dnw_ref, ddtb_ref, dalog_ref, ddsk_ref, ds_scr):
        @pl.when(pl.program_id(0) == 0)
        def _():
            ds_scr[...] = jnp.zeros_like(ds_scr)
            for r in (dnw_ref, ddtb_ref, dalog_ref, ddsk_ref):
                r[...] = jnp.zeros_like(r)

        y = y_ref[...]
        _, gate_vjp = jax.vjp(lambda y_, z_, w_: _rms(y_ * _silu(z_), w_), y, z_ref[...].astype(F32), nw_ref[...])
        dy, dz, dnw = gate_vjp(dya_ref[...])
        dz_ref[...] = dz.astype(BF16)
        dnw_ref[...] += dnw

        dtr = dtr_ref[...]
        c = _ssd_common(dtr, dtb_ref[...], alog_ref[...], et_ref[...])
        dt, a, lane_lo, row, col = c["dt"], c["a"], c["lane_lo"], c["row"], c["col"]
        dsk = dsk_ref[...]
        p_a, p_dt, v_last = [], [], []
        da_cols = jnp.zeros((CHUNK, CHUNK), F32)
        da_rows = jnp.zeros((CHUNK, CHUNK), F32)
        for g in range(SSD_GROUPS):
            gs = slice(g * SSD_STATE, (g + 1) * SSD_STATE)
            bg, cg = b_ref[:, gs].astype(BF16), c_ref[:, gs].astype(BF16)
            bg_t, cg_t = b_ref[:, gs].T.astype(BF16), c_ref[:, gs].T.astype(BF16)
            cb, cb_t = _dot(cg, bg_t), _dot(bg, cg_t)
            dcb = jnp.zeros((CHUNK, CHUNK), F32)
            dbg = jnp.zeros((CHUNK, SSD_STATE), F32)
            dcg = jnp.zeros((CHUNK, SSD_STATE), F32)
            for pp in range(PAIRS_PER_GROUP):
                j = g * PAIRS_PER_GROUP + pp
                ps = slice(j * LANES, (j + 1) * LANES)
                x = xs_ref[:, ps]
                dtp, ecol, dsr = c["dt_x"][:, ps], c["ecol_x"][:, ps], c["dsr_x"][:, ps]
                elast = c["elast_x"][:, ps]
                xdt = x * dtp
                xb = xdt.astype(BF16)
                dskp = jnp.where(lane_lo[0:1], dsk[:, 2 * j:2 * j + 1], dsk[:, 2 * j + 1:2 * j + 2])
                dyp = dy[:, ps]
                dyb = dyp.astype(BF16)
                sp, dsn = sp_ref[0, j], ds_scr[j]
                spb, dsnb = sp.astype(BF16), dsn.astype(BF16)
                y_off = ecol * _dot(cg, spb)
                dw = (dyp * ecol).astype(BF16)
                dcg = dcg + _dot_nt(dw, spb)
                dsp = _dot(cg_t, dw) + elast * dsn
                xd = xdt * dsr
                zd = _dot(bg, dsnb) * dsr
                dbg = dbg + _dot_nt(xd.astype(BF16), dsnb)
                dxdt = zd
                zero = jnp.zeros_like(xb)
                for h, lm in ((2 * j, lane_lo), (2 * j + 1, jnp.logical_not(lane_lo))):
                    le = _head_decay(c, h)
                    dm = _dot_nt(jnp.where(lm, dyb, zero), jnp.where(lm, xb, zero))
                    dcb = dcb + dm * le
                    m = cb * le
                    m_t = (cb_t * _head_decay(c, h, transposed=True)).astype(BF16)
                    dxdt = dxdt + jnp.where(lm, _dot(m_t, dyb), 0.0)
                    q = dm * m
                    da_cols = da_cols + jnp.where(col == h, jnp.sum(q, axis=1, keepdims=True), 0.0)
                    da_rows = da_rows + jnp.where(row == h, _colsum(q), 0.0)
                ds_scr[j] = dsp
                dxs_ref[:, ps] = dxdt * dtp + dskp * dyp
                p_a.append(dyp * y_off - xdt * zd)
                p_dt.append(dxdt * x)
                v_last.append(_colsum(zd * xdt) + elast * _colsum(dsn * sp))
            dcbb = dcb.astype(BF16)
            db_ref[:, gs] = dbg + _dot_tn(dcbb, cg)
            dc_ref[:, gs] = dcg + _dot(dcbb, bg)
        e = e_ref[...]
        rows8 = jnp.concatenate([jnp.concatenate(v_last, axis=1), _colsum(dy * xs_ref[...]),
                                 jnp.zeros((6, SSD_INNER), F32)], axis=0)
        r8 = _dot_split(rows8, e)
        da = (_dot_split(jnp.concatenate(p_a, axis=1), e) + jnp.where(row == CHUNK - 1, r8[0:1], 0.0)
              + da_cols - da_rows.T)
        ddsk_ref[...] += r8[1:2]
        dadt = jnp.dot((row <= col).astype(F32), da, precision=HIGHEST, preferred_element_type=F32)
        ddt = dadt * a + _dot_split(jnp.concatenate(p_dt, axis=1), e)
        dalog_ref[...] += _colsum(dadt * dt) * a
        ddtr = ddt * _sigmoid(dtr + dtb_ref[...])
        ddtr_ref[...] = ddtr
        ddtb_ref[...] += _colsum(ddtr)

    ck = lambda n, col=0: pl.BlockSpec((CHUNK, n), lambda c: (nc - 1 - c, col))
    acc = lambda n: _full((1, n))
    return pl.pallas_call(
        body, name="ssd_bwd", grid=(nc,),
        in_specs=[ck(SSD_INNER), ck(SSD_INNER), ck(SSD_INNER), ck(SSD_INNER), ck(SSD_BC, SSD_INNER // SSD_BC),
                  ck(SSD_BC, SSD_INNER // SSD_BC + 1), ck(DT_PAD),
                  pl.BlockSpec((1, N_PAIRS, SSD_STATE, LANES), lambda c: (nc - 1 - c, 0, 0, 0)),
                  acc(DT_PAD), acc(DT_PAD), acc(DT_PAD), acc(SSD_INNER), _full((SSD_INNER, LANES)),
                  _full((LANES, SSD_INNER))],
        out_specs=[ck(SSD_INNER), ck(SSD_INNER), ck(SSD_BC), ck(SSD_BC), ck(DT_PAD),
                   acc(SSD_INNER), acc(DT_PAD), acc(DT_PAD), acc(DT_PAD)],
        out_shape=[jax.ShapeDtypeStruct((t, SSD_INNER), BF16), jax.ShapeDtypeStruct((t, SSD_INNER), F32),
                   jax.ShapeDtypeStruct((t, SSD_BC), F32), jax.ShapeDtypeStruct((t, SSD_BC), F32),
                   jax.ShapeDtypeStruct((t, DT_PAD), F32), jax.ShapeDtypeStruct((1, SSD_INNER), F32),
                   jax.ShapeDtypeStruct((1, DT_PAD), F32), jax.ShapeDtypeStruct((1, DT_PAD), F32),
                   jax.ShapeDtypeStruct((1, DT_PAD), F32)],
        scratch_shapes=[pltpu.VMEM((N_PAIRS, SSD_STATE, LANES), F32)], compiler_params=_params(1),
    )(dya, y, z, xc, xc, xc, dtr, sprev, dtb, alog, dsk, nw, e_heads, e_t)


def _sgu_act(uv, uvb, lnw, lnb):
    a = _gelu(uv + uvb)
    return a[:, :SGU_WIDTH], _layer_norm(a[:, SGU_WIDTH:], lnw, lnb)


def _sgu_weights(ws_ref):
    row = lax.broadcasted_iota(jnp.int32, (CHUNK, CHUNK), 0)
    col = lax.broadcasted_iota(jnp.int32, (CHUNK, CHUNK), 1)
    return [jnp.where(row >= col, ws_ref[g], 0.0).astype(BF16) for g in range(SGU_GROUPS)], row >= col


def _sgu_fwd(uv, uvb, lnw, lnb, ws, bs_t):
    t = uv.shape[0]

    def body(uv_ref, uvb_ref, lnw_ref, lnb_ref, ws_ref, bs_ref, o_ref):
        u, vn = _sgu_act(uv_ref[...].astype(F32), uvb_ref[...], lnw_ref[...], lnb_ref[...])
        wc, _ = _sgu_weights(ws_ref)
        bs = bs_ref[...]
        for g in range(SGU_GROUPS):
            gs = slice(g * LANES, (g + 1) * LANES)
            mixed = _dot(wc[g], vn[:, gs].astype(BF16)) + bs[:, g:g + 1]
            o_ref[:, gs] = (u[:, gs] * mixed).astype(BF16)

    return pl.pallas_call(
        body, name="sgu_fwd", grid=(t // CHUNK,),
        in_specs=[_rows(CHUNK, 2 * SGU_WIDTH), _full((1, 2 * SGU_WIDTH)), _full((1, SGU_WIDTH)), _full((1, SGU_WIDTH)),
                  _full(ws.shape), _full(bs_t.shape)],
        out_specs=_rows(CHUNK, SGU_WIDTH), out_shape=jax.ShapeDtypeStruct((t, SGU_WIDTH), BF16),
        compiler_params=_params(1),
    )(uv, uvb, lnw, lnb, ws, bs_t)


def _sgu_bwd(dyb, uv, uvb, lnw, lnb, ws, bs_t, e_groups):
    t = uv.shape[0]

    def body(dyb_ref, uv_ref, uvb_ref, lnw_ref, lnb_ref, ws_ref, bs_ref, e_ref,
             duv_ref, duvb_ref, dlnw_ref, dlnb_ref, dws_ref, dbs_ref):
        @pl.when(pl.program_id(0) == 0)
        def _():
            for r in (duvb_ref, dlnw_ref, dlnb_ref, dws_ref, dbs_ref):
                r[...] = jnp.zeros_like(r)

        (u, vn), act_vjp = jax.vjp(_sgu_act, uv_ref[...].astype(F32), uvb_ref[...], lnw_ref[...], lnb_ref[...])
        wc, causal = _sgu_weights(ws_ref)
        bs = bs_ref[...]
        dyb = dyb_ref[...]
        du, dvn, dmix = [], [], []
        for g in range(SGU_GROUPS):
            gs = slice(g * LANES, (g + 1) * LANES)
            vb = vn[:, gs].astype(BF16)
            mixed = _dot(wc[g], vb) + bs[:, g:g + 1]
            dm = dyb[:, gs] * u[:, gs]
            dmb = dm.astype(BF16)
            du.append(dyb[:, gs] * mixed)
            dvn.append(_dot_tn(wc[g], dmb))
            dws_ref[g] += jnp.where(causal, _dot_nt(dmb, vb), 0.0)
            dmix.append(dm)
        dbs_ref[...] += _dot_split(jnp.concatenate(dmix, axis=1), e_ref[...])
        duv, duvb, dlnw, dlnb = act_vjp((jnp.concatenate(du, axis=1), jnp.concatenate(dvn, axis=1)))
        duv_ref[...] = duv.astype(BF16)
        duvb_ref[...] += duvb
        dlnw_ref[...] += dlnw
        dlnb_ref[...] += dlnb

    return pl.pallas_call(
        body, name="sgu_bwd", grid=(t // CHUNK,),
        in_specs=[_rows(CHUNK, SGU_WIDTH), _rows(CHUNK, 2 * SGU_WIDTH), _full((1, 2 * SGU_WIDTH)),
                  _full((1, SGU_WIDTH)), _full((1, SGU_WIDTH)), _full(ws.shape), _full(bs_t.shape),
                  _full(e_groups.shape)],
        out_specs=[_rows(CHUNK, 2 * SGU_WIDTH), _full((1, 2 * SGU_WIDTH)), _full((1, SGU_WIDTH)),
                   _full((1, SGU_WIDTH)), _full(ws.shape), _full(bs_t.shape)],
        out_shape=[jax.ShapeDtypeStruct((t, 2 * SGU_WIDTH), BF16), jax.ShapeDtypeStruct((1, 2 * SGU_WIDTH), F32),
                   jax.ShapeDtypeStruct((1, SGU_WIDTH), F32), jax.ShapeDtypeStruct((1, SGU_WIDTH), F32),
                   jax.ShapeDtypeStruct(ws.shape, F32), jax.ShapeDtypeStruct(bs_t.shape, F32)],
        compiler_params=_params(1),
    )(dyb, uv, uvb, lnw, lnb, ws, bs_t, e_groups)


def _merge(gates, pa, pb, bg):
    s = _sigmoid(gates + bg)
    return s[:, :D_MODEL] * pa + s[:, D_MODEL:] * pb


def _branches_merge(branches, gates, bg):
    pa, pb = branches
    return (pa, pb, _merge(gates.astype(F32), pa, pb, bg)), ()


def _merge_backward(dmix, gates, pa, pb, bg):
    _, vjp = jax.vjp(_merge, gates.astype(F32), pa.astype(F32), pb.astype(F32), bg)
    dg, dpa, dpb, dbg = vjp(dmix)
    return (dg, dpa, dpb), (dbg,)


def _conv_f_fwd(up, cw, cb, tm=128):
    t, c = up.shape

    def body(x_ref, h_ref, w_ref, b_ref, o_ref, y_ref):
        halo = jnp.where(pl.program_id(0) > 0, h_ref[...].astype(F32)[8:], 0.0)
        y = _causal_conv(x_ref[...].astype(F32), halo, w_ref[...], b_ref[...])
        y_ref[...] = y.astype(BF16)
        o_ref[...] = (_silu(y[:, :D_FF]) * y[:, D_FF:]).astype(BF16)

    return pl.pallas_call(
        body, name="conv_f_fwd", grid=(t // tm,),
        in_specs=[_rows(tm, c), _halo(tm, c, rows=16), _full(cw.shape), _full((1, c))],
        out_specs=[_rows(tm, D_FF), _rows(tm, c)],
        out_shape=[jax.ShapeDtypeStruct((t, D_FF), BF16), jax.ShapeDtypeStruct((t, c), BF16)],
        compiler_params=_params(1),
    )(up, up, cw, cb)


def _conv_f_bwd(dact, y, up, cw, tm=128):
    t, c = up.shape
    nt = t // tm

    def body(d_ref, y_ref, x_ref, w_ref, dx_ref, dw_ref, db_ref, nxt_scr):
        @pl.when(pl.program_id(0) == 0)
        def _():
            nxt_scr[...] = jnp.zeros_like(nxt_scr)
            dw_ref[...] = jnp.zeros_like(dw_ref)
            db_ref[...] = jnp.zeros_like(db_ref)

        a, v = y_ref[:, :D_FF].astype(F32), y_ref[:, D_FF:].astype(F32)
        d = d_ref[...].astype(F32)
        silu_a, dsilu_a = _silu_and_grad(a)
        dy = jnp.concatenate([d * v * dsilu_a, d * silu_a], axis=1)
        dx, dw = _causal_conv_bwd(dy, nxt_scr[...], x_ref[...].astype(F32), w_ref[...])
        dx_ref[...] = dx.astype(BF16)
        nxt_scr[...] = dy[:8]
        dw_ref[...] += dw
        db_ref[...] += _colsum(dy)

    return pl.pallas_call(
        body, name="conv_f_bwd", grid=(nt,),
        in_specs=[_rows(tm, D_FF, nt, True), _rows(tm, c, nt, True), _rows(tm, c, nt, True), _full(cw.shape)],
        out_specs=[_rows(tm, c, nt, True), _full(cw.shape), _full((1, c))],
        out_shape=[jax.ShapeDtypeStruct((t, c), BF16), jax.ShapeDtypeStruct(cw.shape, F32),
                   jax.ShapeDtypeStruct((1, c), F32)],
        scratch_shapes=[pltpu.VMEM((8, c), F32)], compiler_params=_params(1),
    )(dact, y, up, cw)


def _conv_a_bwd(dxs, db, dc, y, xbc, cw, tm=256):
    t, c = xbc.shape
    nt = t // tm

    def body(dxs_ref, db_ref, dc_ref, y_ref, x_ref, w_ref, dx_ref, dw_ref, dbias_ref, nxt_scr):
        @pl.when(pl.program_id(0) == 0)
        def _():
            nxt_scr[...] = jnp.zeros_like(nxt_scr)
            dw_ref[...] = jnp.zeros_like(dw_ref)
            dbias_ref[...] = jnp.zeros_like(dbias_ref)

        dy = jnp.concatenate([dxs_ref[...], db_ref[...], dc_ref[...]], axis=1) * _dsilu(y_ref[...].astype(F32))
        dx, dw = _causal_conv_bwd(dy, nxt_scr[...], x_ref[...].astype(F32), w_ref[...])
        dx_ref[...] = dx.astype(BF16)
        nxt_scr[...] = dy[:8]
        dw_ref[...] += dw
        dbias_ref[...] += _colsum(dy)

    return pl.pallas_call(
        body, name="conv_a_bwd", grid=(nt,),
        in_specs=[_rows(tm, SSD_INNER, nt, True), _rows(tm, SSD_BC, nt, True), _rows(tm, SSD_BC, nt, True),
                  _rows(tm, c, nt, True), _rows(tm, c, nt, True), _full(cw.shape)],
        out_specs=[_rows(tm, c, nt, True), _full(cw.shape), _full((1, c))],
        out_shape=[jax.ShapeDtypeStruct((t, c), BF16), jax.ShapeDtypeStruct(cw.shape, F32),
                   jax.ShapeDtypeStruct((1, c), F32)],
        scratch_shapes=[pltpu.VMEM((8, c), F32)], compiler_params=_params(1),
    )(dxs, db, dc, y, xbc, cw)


def _pad_lanes(v, n=DT_PAD):
    return jnp.pad(v, ((0, 0), (0, n - v.shape[1])))


def _local_step(x, target, w, p, after=None, late_weights=None, on_grad=None, on_small=None):
    dtb, alog, dsk = _pad_lanes(p["dt_bias"]), _pad_lanes(p["a_log"]), _pad_lanes(p["d_skip"])
    bs_t = _pad_lanes(p["b_spatial"].T)
    e_heads = (jnp.arange(SSD_INNER)[:, None] // SSD_HEAD_DIM == jnp.arange(LANES)[None, :]).astype(BF16)
    e_heads_t = (jnp.arange(LANES)[:, None] == jnp.arange(SSD_INNER)[None, :] // SSD_HEAD_DIM).astype(BF16)
    e_groups = (jnp.arange(SGU_WIDTH)[:, None] // LANES == jnp.arange(LANES)[None, :]).astype(BF16)

    n1 = _norm_fwd(x, p["norm1_w"], "norm1_fwd", after=after)
    z = _mm(n1, w["z"], "nt", "proj_z", out_dtype=BF16)
    xbc = _mm(n1, w["xbc"], "nt", "proj_xbc", out_dtype=BF16)
    dtr = _mm(n1, w["dt"], "nt", "proj_dt")
    uv = _mm(n1, w["uv"], "nt", "proj_uv", out_dtype=BF16)
    gates = _mm(n1, w["gates"], "nt", "proj_gates", out_dtype=BF16)
    xc, conv_a_out = _conv_a_fwd(xbc, w["conv_a"], p["conv_a_b"])
    y, ya, sprev = _ssd_fwd(xc, dtr, z, dtb, alog, dsk, p["ssd_norm_w"], e_heads_t)
    yb = _sgu_fwd(uv, p["uv_b"], p["v_ln_w"], p["v_ln_b"], p["w_spatial"], bs_t)
    if late_weights is not None:
        w = {**w, **late_weights(ya, yb)}
    narrow = (D_MODEL, BF16)
    pa, pb, mix = _mm_rows(
        [ya, yb], [w["branch_a"], w["branch_b"]], "nn", "branches", _branches_merge, rows=[gates],
        fulls=[p["b_gate"]], row_outs=[narrow] * 3, summed=False)
    wide = [(D_MODEL, F32), (D_MODEL, BF16)]
    h1, n2 = _mm_rows(mix, w["out"], "nn", "out_proj", _residual_norm, rows=[x], fulls=[p["norm2_w"]], row_outs=wide)
    up = _mm(n2, w["up"], "nt", "up_proj", out_dtype=BF16)
    act, conv_f_out = _conv_f_fwd(up, w["conv_f"], p["conv_f_b"])
    dh2, dh2b, loss, g_final = _mm_rows(
        act, w["down"], "nn", "down_proj", _loss_and_grad, rows=[h1, target], fulls=[p["final_norm_w"]],
        row_outs=wide, acc_outs=[(8, LANES), (1, D_MODEL)])

    on_grad = on_grad or (lambda name, grads: None)
    g = {"final_norm_w": g_final}
    g["down"] = _wgrad(act, dh2b, "down_wgrad")
    tok = on_grad("w_down", g)
    dact = _mm(dh2b, w["down"], "nt", "down_dgrad", out_dtype=BF16, after=tok)
    dup, g["conv_f"], g["conv_f_b"] = _conv_f_bwd(dact, conv_f_out, up, w["conv_f"])
    g["up"] = _wgrad(dup, n2, "up_wgrad")
    tok = on_grad("w_up", g)
    dh1, dh1b, g["norm2_w"] = _mm_rows(
        dup, w["up"], "nn", "up_dgrad", _norm_backward, rows=[h1, dh2], fulls=[p["norm2_w"]], row_outs=wide,
        acc_outs=[(1, D_MODEL)], after=tok)
    g["out"] = _wgrad(mix, dh1b, "out_wgrad")
    tok = on_grad("w_out", g)
    dgates, dpa, dpb, g["b_gate"] = _mm_rows(
        dh1b, w["out"], "nt", "out_dgrad", _merge_backward, rows=[gates, pa, pb], fulls=[p["b_gate"]],
        row_outs=[(2 * D_MODEL, BF16), (D_MODEL, BF16), (D_MODEL, BF16)], acc_outs=[(1, 2 * D_MODEL)], after=tok)
    g["branch_a"] = _wgrad(ya, dpa, "branch_a_wgrad")
    g["branch_b"] = _wgrad(yb, dpb, "branch_b_wgrad")
    tok = on_grad("w_branch", g)
    dya = _mm(dpa, w["branch_a"], "nt", "branch_a_dgrad", after=tok)
    dyb = _mm(dpb, w["branch_b"], "nt", "branch_b_dgrad", after=tok)
    duv, g["uv_b"], g["v_ln_w"], g["v_ln_b"], g["w_spatial"], dbs_t = _sgu_bwd(
        dyb, uv, p["uv_b"], p["v_ln_w"], p["v_ln_b"], p["w_spatial"], bs_t, e_groups)
    g["b_spatial"] = dbs_t[:, :SGU_GROUPS].T
    dz, dxs, db, dc, ddtr, g["ssd_norm_w"], ddtb, dalog, ddsk = _ssd_bwd(
        dya, y, z, xc, dtr, sprev, dtb, alog, dsk, p["ssd_norm_w"], e_heads, e_heads_t)
    g["dt_bias"], g["a_log"], g["d_skip"] = ddtb, dalog, ddsk
    dxbc, g["conv_a"], g["conv_a_b"] = _conv_a_bwd(dxs, db, dc, conv_a_out, xbc, w["conv_a"])
    tok = on_small(g, loss) if on_small else None
    ddtrb = ddtr.astype(BF16)
    for name, d in (("z", dz), ("xbc", dxbc), ("dt", ddtrb), ("uv", duv), ("gates", dgates)):
        g[name] = _wgrad(d, n1, name + "_wgrad", after=tok)
    tok = on_grad("w_in", g)
    dn1 = _mm([dz, dxbc], [w["z"], w["xbc"]], "nn", "ssd_dgrad", after=tok)
    gx, g["norm1_w"] = _mm_rows(
        [duv, dgates, ddtrb], [w["uv"], w["gates"], w["dt"]], "nn", "in_dgrad",
        lambda r, so_far, h, dres, w_: tuple(t[:1] for t in _norm_backward(r + so_far, h, dres, w_)),
        rows=[dn1, x, dh1], fulls=[p["norm1_w"]], row_outs=wide[:1], acc_outs=[(1, D_MODEL)])
    return loss, gx, g


def _place():
    return lax.axis_index("x"), lax.axis_index("y"), lax.axis_index("c")


def _other_chips(x, y):
    return [(1 - x, y), (x, 1 - y), (1 - x, 1 - y)]


def _all_gather(shards, name):
    n = len(shards)

    def body(*refs):
        ins, outs = refs[:n], refs[n:2 * n]
        send_sems, recv_sems, local_sems = refs[2 * n:]
        x, y, c = _place()
        me, sibling = (x, y, c), (x, y, 1 - c)
        chips = _other_chips(x, y)

        def copy(a, k, block, to, src=None):
            slot = outs[a].at[4 * block[0] + 2 * block[1] + block[2]]
            return pltpu.make_async_remote_copy(
                src_ref=slot if src is None else src, dst_ref=slot, send_sem=send_sems.at[7 * a + k],
                recv_sem=recv_sems.at[7 * a + k], device_id=to, device_id_type=MESH)

        started = []
        for a in range(n):
            mine = pltpu.make_async_copy(ins[a], outs[a].at[4 * x + 2 * y + c], local_sems.at[a])
            mine.start()
            started.append(mine)
        sends = []
        for a in range(n):
            sends.append(copy(a, 0, me, sibling, src=ins[a]))
            sends += [copy(a, 1 + j, me, (*chip, c), src=ins[a]) for j, chip in enumerate(chips)]
        for cp in sends:
            cp.start()
        for a in range(n):
            for j, chip in enumerate(chips):
                copy(a, 1 + j, (*chip, c), me).wait_recv()
                fwd = copy(a, 4 + j, (*chip, c), sibling)
                fwd.start()
                sends.append(fwd)
        for a in range(n):
            copy(a, 0, sibling, me).wait_recv()
            for j, chip in enumerate(chips):
                copy(a, 4 + j, (*chip, 1 - c), me).wait_recv()
        for cp in sends:
            cp.wait_send()
        for mine in started:
            mine.wait()

    any_spec = pl.BlockSpec(memory_space=pl.ANY)
    return pl.pallas_call(
        body, name=name, in_specs=[any_spec] * n, out_specs=[any_spec] * n,
        out_shape=[jax.ShapeDtypeStruct((N_DEV, *s.shape), s.dtype) for s in shards],
        scratch_shapes=[pltpu.SemaphoreType.DMA((7 * n,)), pltpu.SemaphoreType.DMA((7 * n,)),
                        pltpu.SemaphoreType.DMA((n,))],
    )(*shards)


HBM_SPEC = pl.BlockSpec(memory_space=pltpu.HBM)
SEM_SPEC = pl.BlockSpec(memory_space=pltpu.SEMAPHORE)
ANY_SPEC = pl.BlockSpec(memory_space=pl.ANY)
DATAFLOW = pltpu.SideEffectType.DATAFLOW_SIDE_EFFECTING
N_PEERS = N_DEV - 1


def _peers(x, y, c):
    out = []
    for r in range(1, N_DEV):
        fx, fy, fc = r >> 2 & 1, r >> 1 & 1, r & 1
        out.append(((1 - x) if fx else x, (1 - y) if fy else y, (1 - c) if fc else c))
    return out


def _gather_copies(srcs, lands, send_sems, recv_sems, sending, scatter=False):
    x, y, c = _place()
    copies = []
    for a, (src, land) in enumerate(zip(srcs, lands)):
        for j, (px, py, pc) in enumerate(_peers(x, y, c)):
            mine, theirs = 4 * x + 2 * y + c, 4 * px + 2 * py + pc
            block = src.at[theirs if sending else 0] if scatter else src
            copies.append(pltpu.make_async_remote_copy(
                src_ref=block, dst_ref=land.at[mine if sending else theirs], send_sem=send_sems.at[N_PEERS * a + j],
                recv_sem=recv_sems.at[N_PEERS * a + j], device_id=(px, py, pc), device_id_type=MESH))
    return copies


def _gather_start(shards, after, name, scatter=False):
    n = len(shards)
    after = [] if after is None else [after]

    def body(*refs):
        srcs, lands = refs[:n], refs[n:2 * n]
        send_sems, recv_sems = refs[2 * n + len(after):2 * n + len(after) + 2]
        token = refs[-1]
        for cp in _gather_copies(srcs, lands, send_sems, recv_sems, sending=True, scatter=scatter):
            cp.start()
        token[...] = jnp.zeros_like(token)

    lands = [lax.empty(s.shape if scatter else (N_DEV, *s.shape), s.dtype) for s in shards]
    hbm = lambda a: pltpu.with_memory_space_constraint(a, pltpu.HBM)
    out = pl.pallas_call(
        body, name=name,
        out_shape=(pltpu.SemaphoreType.DMA((N_PEERS * n,)), pltpu.SemaphoreType.DMA((N_PEERS * n,)),
                   *[pltpu.HBM(a.shape, a.dtype) for a in (*shards, *lands)], jax.ShapeDtypeStruct((8, LANES), F32)),
        in_specs=[HBM_SPEC] * (2 * n) + [ANY_SPEC] * len(after),
        out_specs=(SEM_SPEC, SEM_SPEC, *[HBM_SPEC] * (2 * n), pl.BlockSpec(memory_space=pltpu.VMEM)),
        input_output_aliases={i: 2 + i for i in range(2 * n)},
        compiler_params=pltpu.CompilerParams(has_side_effects=DATAFLOW),
    )(*[hbm(a) for a in (*shards, *lands)], *after)
    return out[0], out[1], out[2:2 + n], out[2 + n:2 + 2 * n], out[-1]


def _gather_wait(send_sems, recv_sems, shards, lands, after, name, scatter=False):
    n = len(shards)
    after = tuple(after)

    def body(*refs):
        srcs, lands_ = refs[:n], refs[n:2 * n]
        send, recv = refs[2 * n:2 * n + 2]
        for cp in _gather_copies(srcs, lands_, send, recv, sending=False, scatter=scatter):
            cp.wait_send()
            cp.wait_recv()

    out = pl.pallas_call(
        body, name=name, out_shape=tuple(pltpu.HBM(a.shape, a.dtype) for a in (*shards, *lands)),
        in_specs=[HBM_SPEC] * (2 * n) + [SEM_SPEC, SEM_SPEC] + [ANY_SPEC] * len(after),
        out_specs=tuple([HBM_SPEC] * (2 * n)), input_output_aliases={i: i for i in range(2 * n)},
        compiler_params=pltpu.CompilerParams(has_side_effects=DATAFLOW),
    )(*shards, *lands, send_sems, recv_sems, *after)
    return out[:n], out[n:]


def _chip_copies(src, land, send_sems, recv_sems):
    x, y, c = _place()
    return [pltpu.make_async_remote_copy(
        src_ref=src.at[2 * cx + cy], dst_ref=land.at[j], send_sem=send_sems.at[j], recv_sem=recv_sems.at[j],
        device_id=(cx, cy, c), device_id_type=MESH) for j, (cx, cy) in enumerate(_other_chips(x, y))]


def _chips_start(q, name):
    def body(q_ref, land_ref, send_sems, recv_sems, q_thru, land_thru, token):
        for cp in _chip_copies(q_ref, land_ref, send_sems, recv_sems):
            cp.start()
        token[...] = jnp.zeros_like(token)

    land = lax.empty((3, *q.shape[1:]), q.dtype)
    return pl.pallas_call(
        body, name=name,
        out_shape=(pltpu.SemaphoreType.DMA((3,)), pltpu.SemaphoreType.DMA((3,)), pltpu.HBM(q.shape, q.dtype),
                   pltpu.HBM(land.shape, land.dtype), jax.ShapeDtypeStruct((8, LANES), F32)),
        in_specs=[HBM_SPEC, HBM_SPEC],
        out_specs=(SEM_SPEC, SEM_SPEC, HBM_SPEC, HBM_SPEC, pl.BlockSpec(memory_space=pltpu.VMEM)),
        input_output_aliases={0: 2, 1: 3}, compiler_params=pltpu.CompilerParams(has_side_effects=DATAFLOW),
    )(pltpu.with_memory_space_constraint(q, pltpu.HBM), pltpu.with_memory_space_constraint(land, pltpu.HBM))


def _chips_wait(send_sems, recv_sems, q, land, after, name):
    def body(q_ref, land_ref, send, recv, after_ref, q_out, land_out):
        for cp in _chip_copies(q_ref, land_ref, send, recv):
            cp.wait_send()
            cp.wait_recv()

    return pl.pallas_call(
        body, name=name, out_shape=(pltpu.HBM(q.shape, q.dtype), pltpu.HBM(land.shape, land.dtype)),
        in_specs=[HBM_SPEC, HBM_SPEC, SEM_SPEC, SEM_SPEC, ANY_SPEC], out_specs=(HBM_SPEC, HBM_SPEC),
        input_output_aliases={0: 0, 1: 1}, compiler_params=pltpu.CompilerParams(has_side_effects=DATAFLOW),
    )(q, land, send_sems, recv_sems, after)[1]


def _exchange_cores(part, name):
    def body(in_ref, out_ref, send_sems, recv_sems):
        x, y, c = _place()
        copies = [pltpu.make_async_remote_copy(
            src_ref=in_ref.at[2 * k + (1 - c)], dst_ref=out_ref.at[k], send_sem=send_sems.at[k],
            recv_sem=recv_sems.at[k], device_id=(x, y, 1 - c), device_id_type=MESH) for k in range(4)]
        for cp in copies:
            cp.start()
        for cp in copies:
            cp.wait()

    return pl.pallas_call(
        body, name=name, in_specs=[ANY_SPEC], out_specs=ANY_SPEC,
        out_shape=jax.ShapeDtypeStruct((4, *part.shape[1:]), part.dtype),
        scratch_shapes=[pltpu.SemaphoreType.DMA((4,)), pltpu.SemaphoreType.DMA((4,))],
    )(part)


def _chip_sum(part, got, place, name, tr=256):
    _, r, c = part.shape
    tr, tc = _tile2d(r, c, tr)

    def body(place_ref, p_ref, g_ref, q_ref, own_ref):
        s = p_ref[0].astype(F32) + g_ref[0].astype(F32)
        q_ref[0] = s.astype(BF16)

        @pl.when(pl.program_id(2) == place_ref[1])
        def _():
            own_ref[...] = s

    grid_spec = pltpu.PrefetchScalarGridSpec(
        num_scalar_prefetch=1, grid=(r // tr, c // tc, 4),
        in_specs=[pl.BlockSpec((1, tr, tc), lambda i, j, k, pr: (2 * k + pr[0], i, j)),
                  pl.BlockSpec((1, tr, tc), lambda i, j, k, pr: (k, i, j))],
        out_specs=[pl.BlockSpec((1, tr, tc), lambda i, j, k, pr: (k, i, j)),
                   pl.BlockSpec((tr, tc), lambda i, j, k, pr: (i, j))])
    return pl.pallas_call(
        body, name=name, grid_spec=grid_spec,
        out_shape=[jax.ShapeDtypeStruct((4, r, c), BF16), jax.ShapeDtypeStruct((r, c), F32)],
        compiler_params=_params(3),
    )(place, part, got)


def _sum_adamw(own, got, w, m, v, name):
    r, c = own.shape
    tc = 4 * LANES

    def body(own_ref, got_ref, w_ref, m_ref, v_ref, g_ref, d_ref, nm_ref, nv_ref):
        g = own_ref[...]
        for j in range(3):
            g = g + got_ref[j].astype(F32)
        two_d = lambda ref: ref[...].reshape(r, tc)
        delta, nm, nv = _adamw(two_d(w_ref), g, two_d(m_ref), two_d(v_ref))
        for ref, val in ((g_ref, g), (d_ref, delta), (nm_ref, nm), (nv_ref, nv)):
            ref[...] = val.reshape(ref.shape)

    wblk = pl.BlockSpec((r, 1, tc), lambda j: (0, 0, j))
    return pl.pallas_call(
        body, name=name, grid=(c // tc,),
        in_specs=[pl.BlockSpec((r, tc), lambda j: (0, j)), pl.BlockSpec((3, r, tc), lambda j: (0, 0, j)),
                  wblk, wblk, wblk],
        out_specs=[wblk] * 4, out_shape=[jax.ShapeDtypeStruct(w.shape, F32)] * 4, compiler_params=_params(1),
    )(own, got, w, m, v)


def _adamw(w, g, m, v):
    m = ADAM_B1 * m + (1.0 - ADAM_B1) * g
    v = ADAM_B2 * v + (1.0 - ADAM_B2) * jnp.square(g)
    m_hat = m / (1.0 - ADAM_B1 ** ADAM_STEP)
    v_hat = v / (1.0 - ADAM_B2 ** ADAM_STEP)
    return -ADAM_LR * (m_hat / (jnp.sqrt(v_hat) + ADAM_EPS) + ADAM_WD * w), m, v


def _sum8_adamw(part, got, place, w, m, v, name, tr=256):
    r, c = w.shape
    tr, tc = _tile2d(r, c, tr)
    blk = pl.BlockSpec((tr, tc), lambda i, j, pr: (i, j))

    def body(place_ref, own_ref, got_ref, w_ref, m_ref, v_ref, g_ref, d_ref, nm_ref, nv_ref):
        dev = 2 * place_ref[1] + place_ref[0]
        g = jnp.zeros((tr, tc), F32)
        for d in range(N_DEV):
            g = g + jnp.where(dev == d, own_ref[0], got_ref[d]).astype(F32)
        g_ref[...] = g
        d_ref[...], nm_ref[...], nv_ref[...] = _adamw(w_ref[...], g, m_ref[...], v_ref[...])

    grid_spec = pltpu.PrefetchScalarGridSpec(
        num_scalar_prefetch=1, grid=(r // tr, c // tc),
        in_specs=[pl.BlockSpec((1, tr, tc), lambda i, j, pr: (2 * pr[1] + pr[0], i, j)),
                  pl.BlockSpec((N_DEV, tr, tc), lambda i, j, pr: (0, i, j)), blk, blk, blk],
        out_specs=[blk] * 4)
    return pl.pallas_call(
        body, name=name, grid_spec=grid_spec, out_shape=[jax.ShapeDtypeStruct(w.shape, F32)] * 4,
        compiler_params=_params(2),
    )(place, part, got, w, m, v)


VECTORS = ["norm1_w", "b_gate", "conv_a_b", "dt_bias", "a_log", "d_skip", "ssd_norm_w", "uv_b", "v_ln_w", "v_ln_b",
           "norm2_w", "conv_f_b", "final_norm_w"]
SMALL_ORDER = VECTORS + ["w_spatial", "b_spatial", "conv_a_w", "conv_f_w"]


ROW_VECTORS = VECTORS[1:]


def _small_adamw(gathered, w, m, v):
    sizes = {n: w[n].shape[1] for n in ROW_VECTORS}
    offs, off = {}, 0
    for n in ROW_VECTORS:
        offs[n] = off
        off += -(-sizes[n] // LANES) * LANES
    loss_off = off
    k = len(SMALL_ORDER)
    n_g = len(gathered)

    def body(*refs):
        row_ref, ws_ref, bs_ref, ca_ref, cf_ref, n1_ref = refs[:n_g]
        w_refs, m_refs, v_refs = (dict(zip(SMALL_ORDER, refs[n_g + i * k:n_g + (i + 1) * k])) for i in range(3))
        outs = refs[n_g + 3 * k:]
        x, y, c = _place()
        dev = 4 * x + 2 * y + c

        def total(ref):
            s = ref[0]
            for d in range(1, N_DEV):
                s = s + ref[d]
            return s

        row = total(row_ref)
        grads = {n: row[:, offs[n]:offs[n] + sizes[n]] for n in ROW_VECTORS}
        grads["norm1_w"], grads["w_spatial"], grads["b_spatial"] = total(n1_ref), total(ws_ref), total(bs_ref)
        for n, ref in (("conv_a_w", ca_ref), ("conv_f_w", cf_ref)):
            whole, cols = total(ref), w_refs[n].shape[1]
            mine = whole[:, :cols]
            for d in range(1, N_DEV):
                mine = jnp.where(dev == d, whole[:, d * cols:(d + 1) * cols], mine)
            grads[n] = mine
        for i, n in enumerate(SMALL_ORDER):
            outs[4 * i][...] = grads[n]
            outs[4 * i + 1][...], outs[4 * i + 2][...], outs[4 * i + 3][...] = _adamw(
                w_refs[n][...], grads[n], m_refs[n][...], v_refs[n][...])
        outs[4 * k][...] = row[:, loss_off:loss_off + LANES]

    out = pl.pallas_call(
        body, name="adamw_small",
        out_shape=[jax.ShapeDtypeStruct(w[n].shape, F32) for n in SMALL_ORDER for _ in range(4)]
        + [jax.ShapeDtypeStruct((1, LANES), F32)],
        compiler_params=_params(0),
    )(*gathered, *[t[n] for t in (w, m, v) for n in SMALL_ORDER])
    return [dict(zip(SMALL_ORDER, out[j:4 * k:4])) for j in range(4)] + [out[4 * k]]


SMALL = ["norm1_w", "b_gate", "conv_a_b", "dt_bias", "a_log", "d_skip", "ssd_norm_w", "uv_b", "v_ln_w", "v_ln_b",
         "w_spatial", "b_spatial", "norm2_w", "conv_f_b", "final_norm_w"]
BIG = ["w_in", "w_branch", "w_out", "w_up", "w_down"]
TRANSPOSED = ("w_in", "w_up")
WEIGHTS = ["norm1_w", "w_in", "b_gate", "conv_a_w", "conv_a_b", "dt_bias", "a_log", "d_skip", "ssd_norm_w", "uv_b",
           "v_ln_w", "v_ln_b", "w_spatial", "b_spatial", "w_branch", "w_out", "norm2_w", "w_up", "conv_f_w",
           "conv_f_b", "w_down", "final_norm_w"]
IN_SPLITS = [("z", 0, 2048), ("xbc", 2048, 5120), ("dt", 5120, 5152), ("uv", 5152, 7200), ("gates", 7200, 9248)]


def _columns_from_devices(a):
    return a.transpose(1, 0, 2).reshape(a.shape[1], -1)


def kernel(x, norm1_w, w_in, b_gate, conv_a_w, conv_a_b, dt_bias, a_log, d_skip, ssd_norm_w, uv_b, v_ln_w, v_ln_b, w_spatial, b_spatial, w_branch, w_out, norm2_w, w_up, conv_f_w, conv_f_b, w_down, final_norm_w, loss_target, m_norm1_w, m_w_in, m_b_gate, m_conv_a_w, m_conv_a_b, m_dt_bias, m_a_log, m_d_skip, m_ssd_norm_w, m_uv_b, m_v_ln_w, m_v_ln_b, m_w_spatial, m_b_spatial, m_w_branch, m_w_out, m_norm2_w, m_w_up, m_conv_f_w, m_conv_f_b, m_w_down, m_final_norm_w, v_norm1_w, v_w_in, v_b_gate, v_conv_a_w, v_conv_a_b, v_dt_bias, v_a_log, v_d_skip, v_ssd_norm_w, v_uv_b, v_v_ln_w, v_v_ln_b, v_w_spatial, v_b_spatial, v_w_branch, v_w_out, v_norm2_w, v_w_up, v_conv_f_w, v_conv_f_b, v_w_down, v_final_norm_w):
    args = dict(locals())
    wts = {n: args[n] for n in WEIGHTS}
    mom = {n: args["m_" + n] for n in WEIGHTS}
    var = {n: args["v_" + n] for n in WEIGHTS}
    cx, cy, cc = _place()
    dev = 4 * cx + 2 * cy + cc
    place = jnp.stack([cc, 2 * cx + cy]).astype(jnp.int32)

    def shard2d(n, a):
        return a[0].T if n in TRANSPOSED else a[0]

    def unshard(n, b):
        return (b.T if n in TRANSPOSED else b)[None]

    g_in, g_conv_a, g_conv_f = _all_gather(
        [shard2d("w_in", w_in).astype(BF16), conv_a_w[0], conv_f_w[0]], "gather_w_in")
    late = [shard2d(n, wts[n]).astype(BF16) for n in BIG[1:]]
    send_sems, recv_sems, late, lands, token = _gather_start(late, g_in, "gather_late_start")
    w_in_rows = g_in.reshape(-1, D_MODEL)
    w = {name: w_in_rows[lo:hi] for name, lo, hi in IN_SPLITS}
    w["dt"] = jnp.pad(w["dt"], ((0, DT_PAD - SSD_HEADS), (0, 0)))
    w["conv_a"] = _columns_from_devices(g_conv_a)
    w["conv_f"] = _columns_from_devices(g_conv_f)

    def late_weights(*after):
        mine, got = _gather_wait(send_sems, recv_sems, late, lands, after, "gather_late_wait")
        g_branch, g_out, g_up, g_down = [lax.dynamic_update_index_in_dim(land, own, dev, 0).reshape(-1, D_MODEL)
                                         for land, own in zip(got, mine)]
        return {"branch_a": g_branch[:SSD_INNER], "branch_b": g_branch[SSD_INNER:], "out": g_out, "up": g_up,
                "down": g_down}

    in_flight = {}

    def on_grad(n, g):
        part = {"w_in": lambda: jnp.concatenate([g[name][:hi - lo] for name, lo, hi in IN_SPLITS], axis=0),
                "w_branch": lambda: jnp.concatenate([g["branch_a"], g["branch_b"]], axis=0),
                "w_out": lambda: g["out"], "w_up": lambda: g["up"], "w_down": lambda: g["down"]}[n]()
        part = part.reshape(N_DEV, -1, D_MODEL)
        if n == "w_in":
            q, own = _chip_sum(part, _exchange_cores(part, "to_other_core_w_in"), place, "chip_sum_w_in")
            send, recv, q, land, tok = _chips_start(q, "to_other_chips_start_w_in")
            in_flight[n] = (own, send, recv, q, land)
            return tok
        send, recv, (part,), (land,), tok = _gather_start([part], None, f"to_owners_start_{n}", scatter=True)
        in_flight[n] = (part, send, recv, land)
        return tok

    p = {n: wts[n][0] if wts[n].ndim > 2 else wts[n].reshape(1, -1) for n in SMALL}
    small_flight = []

    def on_small(g, loss):
        arrays = [jnp.concatenate([g[n] for n in ROW_VECTORS] + [loss[:1]], axis=1), g["w_spatial"], g["b_spatial"],
                  g["conv_a"], g["conv_f"]]
        *flight, tok = _gather_start(arrays, g["conv_a"], "gather_small_start")
        small_flight.append(flight)
        return tok

    loss, gx, g = _local_step(x[0], loss_target[0], w, p, after=token, late_weights=late_weights, on_grad=on_grad,
                              on_small=on_small)
    *flight, _ = _gather_start([g["norm1_w"]], gx, "gather_norm1_start")
    small_flight.append(flight)

    grads, delta, new_m, new_v = {}, {}, {}, {}

    def big_adamw(n, after):
        if n == "w_in":
            own, send, recv, q, land = in_flight[n]
            got = _chips_wait(send, recv, q, land, after, "to_other_chips_wait_w_in")
            out = _sum_adamw(own, got, *[t[n].transpose(2, 0, 1) for t in (wts, mom, var)], "adamw_w_in")
            grads[n], delta[n], new_m[n], new_v[n] = [o.transpose(1, 2, 0) for o in out]
            return out[1]
        part, send, recv, land = in_flight[n]
        (part,), (got,) = _gather_wait(send, recv, [part], [land], [after], f"to_owners_wait_{n}", scatter=True)
        out = _sum8_adamw(part, got, place, *[shard2d(n, t[n]) for t in (wts, mom, var)], f"adamw_{n}")
        grads[n], delta[n], new_m[n], new_v[n] = [unshard(n, o) for o in out]
        return out[1]

    after = gx
    for n in ("w_down", "w_up", "w_out", "w_branch"):
        after = big_adamw(n, after)
    gathered = []
    for (send, recv, mine, land), name in zip(small_flight, ("gather_small_wait", "gather_norm1_wait")):
        mine, got = _gather_wait(send, recv, mine, land, [after], name)
        gathered += [lax.dynamic_update_index_in_dim(full, own, dev, 0) for full, own in zip(got, mine)]
    small = [{n: t[n][0] if t[n].ndim > 2 else t[n].reshape(1, -1) for n in SMALL_ORDER} for t in (wts, mom, var)]
    *outs, loss = _small_adamw(gathered, *small)
    for tgt, out in zip((grads, delta, new_m, new_v), outs):
        tgt.update({n: out[n].reshape(wts[n].shape) for n in SMALL_ORDER})
    big_adamw("w_in", loss)
    loss = loss[0, 0]

    return (loss, gx[None], *[grads[n] for n in WEIGHTS], *[delta[n] for n in WEIGHTS],
            *[new_m[n] for n in WEIGHTS], *[new_v[n] for n in WEIGHTS])
```

```python
import functools

import jax
import jax.numpy as jnp
from jax import lax
from jax.experimental import pallas as pl
from jax.experimental.pallas import tpu as pltpu

F32, BF16 = jnp.float32, jnp.bfloat16
HIGHEST = lax.Precision.HIGHEST

D_MODEL = 1024
SSD_INNER = 2048
SSD_HEAD_DIM = 64
SSD_HEADS = 32
SSD_GROUPS = 4
SSD_STATE = 128
SSD_BC = SSD_GROUPS * SSD_STATE
SSD_XBC = SSD_INNER + 2 * SSD_BC
SSD_CONV = 4
CHUNK = 128
N_PAIRS = SSD_HEADS // 2
PAIRS_PER_GROUP = N_PAIRS // SSD_GROUPS
SGU_WIDTH = 1024
SGU_GROUPS = 8
D_FF = 2816
FFN_CONV = 3
NORM_EPS = 1e-6
LN_EPS = 1e-5
LANES = 128
DT_PAD = LANES

ADAM_LR, ADAM_B1, ADAM_B2, ADAM_EPS, ADAM_WD, ADAM_STEP = 0.001, 0.9, 0.999, 1e-08, 0.01, 10

N_DEV = 8
VMEM_LIMIT = 56 * 1024 * 1024
MESH = pl.DeviceIdType.MESH


def _params(n_grid, **kw):
    sem = dict(dimension_semantics=("arbitrary",) * n_grid) if n_grid else {}
    return pltpu.CompilerParams(vmem_limit_bytes=VMEM_LIMIT, **sem, **kw)


def _tile(n, pref):
    t = (min(pref, n) // LANES) * LANES
    while n % t:
        t -= LANES
    return t


def _row_tile(r, pref):
    for t in range(min(pref, r) // 16 * 16, 0, -16):
        if r % t == 0:
            return t
    return r


def _tile2d(r, c, rows):
    if r % 16 == 0:
        return _row_tile(r, rows), c
    return r, _tile(c, 2 * LANES)


def _rows(tm, n, nt=None, rev=False):
    if rev:
        return pl.BlockSpec((tm, n), lambda i: (nt - 1 - i, 0))
    return pl.BlockSpec((tm, n), lambda i: (i, 0))


def _halo(tm, n, rows=8):
    per = tm // rows
    return pl.BlockSpec((rows, n), lambda i: (jnp.maximum(i * per - 1, 0), 0))


def _full(shape):
    nd = len(shape)
    return pl.BlockSpec(shape, lambda *_: (0,) * nd)


def _rms(x, w, eps=NORM_EPS):
    return x * lax.rsqrt(jnp.mean(x * x, axis=-1, keepdims=True) + eps) * w


def _layer_norm(x, w, b):
    mu = jnp.mean(x, axis=-1, keepdims=True)
    var = jnp.mean(jnp.square(x - mu), axis=-1, keepdims=True)
    return (x - mu) * lax.rsqrt(var + LN_EPS) * w + b


def _sigmoid(x):
    return 1.0 / (1.0 + jnp.exp(-x))


def _silu(x):
    return x * _sigmoid(x)


def _dsilu(x):
    s = _sigmoid(x)
    return s * (1.0 + x * (1.0 - s))


def _silu_and_grad(x):
    s = _sigmoid(x)
    return x * s, s * (1.0 + x * (1.0 - s))


def _softplus(x):
    return jnp.maximum(x, 0.0) + jnp.log(1.0 + jnp.exp(-jnp.abs(x)))


def _gelu(x):
    return jax.nn.gelu(x)


def _dot(a, b):
    return jnp.dot(a, b, preferred_element_type=F32)


def _dot_nt(a, b):
    return lax.dot_general(a, b, (((1,), (1,)), ((), ())), preferred_element_type=F32)


def _dot_tn(a, b):
    return lax.dot_general(a, b, (((0,), (0,)), ((), ())), preferred_element_type=F32)


def _dot_split(p, e):
    hi = p.astype(BF16)
    lo = (p - hi.astype(F32)).astype(BF16)
    return _dot(hi, e) + _dot(lo, e)


def _colsum(x):
    return jnp.sum(x, axis=0, keepdims=True)


def _shift_down(x, halo, j):
    xs = pltpu.roll(x, j, 0)
    hs = pltpu.roll(halo, j, 0)
    r8 = lax.broadcasted_iota(jnp.int32, hs.shape, 0)
    return jnp.concatenate([jnp.where(r8 < j, hs, xs[:8]), xs[8:]], axis=0)


def _shift_up(x, nxt, j):
    n = x.shape[0]
    xs = pltpu.roll(x, n - j, 0)
    ns = pltpu.roll(nxt, 8 - j, 0)
    r8 = lax.broadcasted_iota(jnp.int32, ns.shape, 0)
    return jnp.concatenate([xs[:n - 8], jnp.where(r8 >= 8 - j, ns, xs[n - 8:])], axis=0)


def _causal_conv(x, halo, w, b):
    k = w.shape[0]
    y = b + w[k - 1:k, :] * x
    for j in range(1, k):
        y = y + w[k - 1 - j:k - j, :] * _shift_down(x, halo, j)
    return y


def _causal_conv_bwd(dy, nxt, x, w):
    k = w.shape[0]
    dx = w[k - 1:k, :] * dy
    dw = [_colsum(dy * x)]
    for j in range(1, k):
        dyj = _shift_up(dy, nxt, j)
        dx = dx + w[k - 1 - j:k - j, :] * dyj
        dw.append(_colsum(dyj * x))
    return dx, jnp.concatenate(dw[::-1], axis=0)


MM_TILE_PREF = 1408
MM_VMEM_BUDGET = 40 * 1024 * 1024


def _mm_tiles(m, n, k, out_bytes):
    tm, tn = _tile(m, MM_TILE_PREF), _tile(n, MM_TILE_PREF)
    need = lambda tm, tn: 2 * (2 * k * (tm + tn) + out_bytes * tm * tn)
    while need(tm, tn) > MM_VMEM_BUDGET:
        if tn >= tm and tn > LANES:
            tn = _tile(n, tn - LANES)
        else:
            tm = _tile(m, tm - LANES)
    return tm, tn


def _mm(a, b, dims, name, acc=None, out_dtype=F32, after=None):
    a_list, b_list = (list(a), list(b)) if isinstance(a, (list, tuple)) else ([a], [b])
    k_axis, m_axis = (0, 1) if dims == "tn" else (1, 0)
    m, ks = a_list[0].shape[m_axis], [x.shape[k_axis] for x in a_list]
    n = b_list[0].shape[0] if dims == "nt" else b_list[0].shape[1]
    tm, tn = _mm_tiles(m, n, sum(ks), 4 * (2 if acc is not None else 1))
    a_specs = [pl.BlockSpec((k, tm), lambda j, i: (0, i)) if dims == "tn" else pl.BlockSpec((tm, k), lambda j, i: (i, 0))
               for k in ks]
    b_specs = [pl.BlockSpec((tn, k), lambda j, i: (j, 0)) if dims == "nt" else pl.BlockSpec((k, tn), lambda j, i: (0, j))
               for k in ks]
    o_spec = pl.BlockSpec((tm, tn), lambda j, i: (i, j))
    dot = {"nn": _dot, "nt": _dot_nt, "tn": _dot_tn}[dims]
    n_pairs = len(ks)

    def body(*refs):
        rest = refs[2 * n_pairs:]
        r = dot(refs[0][...], refs[n_pairs][...])
        for i in range(1, n_pairs):
            r = r + dot(refs[i][...], refs[n_pairs + i][...])
        if acc is not None:
            r = r + rest[0][...]
        rest[-1][...] = r.astype(out_dtype)

    ins, specs = a_list + b_list, a_specs + b_specs
    if acc is not None:
        ins.append(acc)
        specs.append(o_spec)
    if after is not None:
        ins.append(after)
        specs.append(pl.BlockSpec(memory_space=pl.ANY))
    return pl.pallas_call(
        body, name=name, grid=(n // tn, m // tm), in_specs=specs, out_specs=o_spec,
        out_shape=jax.ShapeDtypeStruct((m, n), out_dtype), compiler_params=_params(2),
    )(*ins)


def _mm_rows(a, b, dims, name, fn, rows=(), fulls=(), row_outs=(), acc_outs=(), after=None, summed=True):
    a_list, b_list = (list(a), list(b)) if isinstance(a, (list, tuple)) else ([a], [b])
    m, ks = a_list[0].shape[0], [x.shape[1] for x in a_list]
    n, k = (b_list[0].shape[0] if dims == "nt" else b_list[0].shape[1]), sum(ks)
    per_row = 2 * k + 8 * n + sum(4 * r.shape[1] for r in rows) + sum(c * jnp.dtype(d).itemsize for c, d in row_outs)
    tm = _tile(m, 1024)
    while 2 * tm * per_row + 4 * k * n > MM_VMEM_BUDGET:
        tm = _tile(m, tm - LANES)
    dot = _dot_nt if dims == "nt" else _dot
    n_pairs = len(ks)
    n_in = 2 * n_pairs + len(rows) + len(fulls) + (after is not None)

    def body(*refs):
        ins, outs = refs[:n_in], refs[n_in:]
        row_refs, acc_refs = outs[:len(row_outs)], outs[len(row_outs):]

        @pl.when(pl.program_id(0) == 0)
        def _():
            for r in acc_refs:
                r[...] = jnp.zeros_like(r)

        products = [dot(ins[i][...], ins[n_pairs + i][...]) for i in range(n_pairs)]
        result = functools.reduce(lambda p, q: p + q, products) if summed else products
        new_rows, incs = fn(result, *[r[...] for r in ins[2 * n_pairs:2 * n_pairs + len(rows) + len(fulls)]])
        for r, val in zip(row_refs, new_rows):
            r[...] = val.astype(r.dtype)
        for r, inc in zip(acc_refs, incs):
            r[...] += inc

    extra, extra_specs = ([after], [pl.BlockSpec(memory_space=pl.ANY)]) if after is not None else ([], [])
    return pl.pallas_call(
        body, name=name, grid=(m // tm,),
        in_specs=[_rows(tm, k_i) for k_i in ks] + [_full(x.shape) for x in b_list]
        + [_rows(tm, r.shape[1]) for r in rows] + [_full(f.shape) for f in fulls] + extra_specs,
        out_specs=[_rows(tm, c) for c, _ in row_outs] + [_full(s) for s in acc_outs],
        out_shape=[jax.ShapeDtypeStruct((m, c), d) for c, d in row_outs]
        + [jax.ShapeDtypeStruct(s, F32) for s in acc_outs],
        compiler_params=_params(1),
    )(*a_list, *b_list, *rows, *fulls, *extra)


def _residual_norm(o, x, w):
    h = x + o
    return (h, _rms(h, w)), ()


def _norm_backward(dn, h, dres, w):
    _, vjp = jax.vjp(_rms, h, w)
    dh, dw = vjp(dn)
    dh = dh + dres
    return (dh, dh), (dw,)


def _loss_and_grad(dn, h1, target, w):
    yf, vjp = jax.vjp(_rms, h1 + dn, w)
    err = yf - target
    loss = 0.5 * jnp.sum(jnp.mean(err * err, axis=-1, keepdims=True))
    dh, dw = vjp(err * (1.0 / err.shape[-1]))
    return (dh, dh), (jnp.full((8, LANES), loss, F32), dw)


def _wgrad(a, d, name, after=None):
    return _mm(a, d, "tn", name, out_dtype=BF16, after=after)


def _norm_fwd(x, w, name, after=None, tm=512):
    t, d = x.shape

    def body(x_ref, w_ref, *rest):
        rest[-1][...] = _rms(x_ref[...], w_ref[...]).astype(BF16)

    extra, extra_specs = ([after], [_full(after.shape)]) if after is not None else ([], [])
    return pl.pallas_call(
        body, name=name, grid=(t // tm,), in_specs=[_rows(tm, d), _full((1, d))] + extra_specs,
        out_specs=_rows(tm, d), out_shape=jax.ShapeDtypeStruct((t, d), BF16), compiler_params=_params(1),
    )(x, w, *extra)


def _conv_a_fwd(xbc, cw, cb, tm=256):
    t, c = xbc.shape

    def body(x_ref, h_ref, w_ref, b_ref, o_ref, y_ref):
        halo = jnp.where(pl.program_id(0) > 0, h_ref[...].astype(F32)[8:], 0.0)
        y = _causal_conv(x_ref[...].astype(F32), halo, w_ref[...], b_ref[...])
        y_ref[...] = y.astype(BF16)
        o_ref[...] = _silu(y)

    return pl.pallas_call(
        body, name="conv_a_fwd", grid=(t // tm,),
        in_specs=[_rows(tm, c), _halo(tm, c, rows=16), _full(cw.shape), _full((1, c))],
        out_specs=[_rows(tm, c)] * 2,
        out_shape=[jax.ShapeDtypeStruct((t, c), F32), jax.ShapeDtypeStruct((t, c), BF16)], compiler_params=_params(1),
    )(xbc, xbc, cw, cb)


def _ssd_common(dtr, dtb, alog, e_t):
    row = lax.broadcasted_iota(jnp.int32, (CHUNK, CHUNK), 0)
    col = lax.broadcasted_iota(jnp.int32, (CHUNK, CHUNK), 1)
    causal = row >= col
    dt = _softplus(dtr + dtb)
    a = -jnp.exp(alog)
    acum = jnp.dot(causal.astype(F32), dt * a, precision=HIGHEST, preferred_element_type=F32)
    spread = lambda v: _dot(v.astype(BF16), e_t)
    elast = jnp.broadcast_to(jnp.exp(acum[CHUNK - 1:CHUNK, :]), (8, LANES))
    return dict(dt=dt, a=a, acum=acum, acum_t=acum.T, causal=causal, row=row, col=col, lane_lo=col < SSD_HEAD_DIM,
                dt_x=_dot_split(dt, e_t), ecol_x=spread(jnp.exp(acum)), elast_x=_dot_split(elast, e_t)[0:1],
                dsr_x=spread(jnp.exp(acum[CHUNK - 1:CHUNK, :] - acum)))


def _head_decay(c, h, transposed=False):
    d = c["acum"][:, h:h + 1] - c["acum_t"][h:h + 1, :]
    if transposed:
        return jnp.exp(jnp.where(c["row"] <= c["col"], -d, -jnp.inf))
    return jnp.exp(jnp.where(c["causal"], d, -jnp.inf))


def _ssd_fwd(xc, dtr, z, dtb, alog, dsk, nw, e_t):
    t = xc.shape[0]
    nc = t // CHUNK

    def body(xs_ref, b_ref, c_ref, dtr_ref, z_ref, dtb_ref, alog_ref, dsk_ref, nw_ref, et_ref,
             y_ref, ya_ref, sp_ref, s_scr):
        @pl.when(pl.program_id(0) == 0)
        def _():
            s_scr[...] = jnp.zeros_like(s_scr)

        c = _ssd_common(dtr_ref[...], dtb_ref[...], alog_ref[...], et_ref[...])
        lane_lo = c["lane_lo"]
        dsk = dsk_ref[...]
        for g in range(SSD_GROUPS):
            gs = slice(g * SSD_STATE, (g + 1) * SSD_STATE)
            bg_t, cg = b_ref[:, gs].T.astype(BF16), c_ref[:, gs].astype(BF16)
            cb = _dot(cg, bg_t)
            for pp in range(PAIRS_PER_GROUP):
                j = g * PAIRS_PER_GROUP + pp
                ps = slice(j * LANES, (j + 1) * LANES)
                x = xs_ref[:, ps]
                ecol, dsr = c["ecol_x"][:, ps], c["dsr_x"][:, ps]
                xdt = x * c["dt_x"][:, ps]
                xb = xdt.astype(BF16)
                zero = jnp.zeros_like(xb)
                yd = (_dot((cb * _head_decay(c, 2 * j)).astype(BF16), jnp.where(lane_lo, xb, zero))
                      + _dot((cb * _head_decay(c, 2 * j + 1)).astype(BF16), jnp.where(lane_lo, zero, xb)))
                sp = s_scr[j]
                yo = ecol * _dot(cg, sp.astype(BF16))
                st = _dot(bg_t, (xdt * dsr).astype(BF16))
                sp_ref[0, j] = sp
                s_scr[j] = c["elast_x"][:, ps] * sp + st
                dskp = jnp.where(lane_lo[0:1], dsk[:, 2 * j:2 * j + 1], dsk[:, 2 * j + 1:2 * j + 2])
                y_ref[:, ps] = yd + yo + dskp * x
        ya_ref[...] = _rms(y_ref[...] * _silu(z_ref[...].astype(F32)), nw_ref[...]).astype(BF16)

    ck = lambda n, col=0: pl.BlockSpec((CHUNK, n), lambda c: (c, col))
    return pl.pallas_call(
        body, name="ssd_fwd", grid=(nc,),
        in_specs=[ck(SSD_INNER), ck(SSD_BC, SSD_INNER // SSD_BC), ck(SSD_BC, SSD_INNER // SSD_BC + 1), ck(DT_PAD),
                  ck(SSD_INNER), _full((1, DT_PAD)), _full((1, DT_PAD)), _full((1, DT_PAD)),
                  _full((1, SSD_INNER)), _full(e_t.shape)],
        out_specs=[ck(SSD_INNER), ck(SSD_INNER),
                   pl.BlockSpec((1, N_PAIRS, SSD_STATE, LANES), lambda c: (c, 0, 0, 0))],
        out_shape=[jax.ShapeDtypeStruct((t, SSD_INNER), F32), jax.ShapeDtypeStruct((t, SSD_INNER), BF16),
                   jax.ShapeDtypeStruct((nc, N_PAIRS, SSD_STATE, LANES), F32)],
        scratch_shapes=[pltpu.VMEM((N_PAIRS, SSD_STATE, LANES), F32)], compiler_params=_params(1),
    )(xc, xc, xc, dtr, z, dtb, alog, dsk, nw, e_t)


def _ssd_bwd(dya, y, z, xc, dtr, sprev, dtb, alog, dsk, nw, e_heads, e_t):
    t = xc.shape[0]
    nc = t // CHUNK

    def body(dya_ref, y_ref, z_ref, xs_ref, b_ref, c_ref, dtr_ref, sp_ref, dtb_ref, alog_ref, dsk_ref, nw_ref, e_ref,
             et_ref, dz_ref, dxs_ref, db_ref, dc_ref, ddtr_ref, dnw_ref, ddtb_ref, dalog_ref, ddsk_ref, ds_scr):
        @pl.when(pl.program_id(0) == 0)
        def _():
            ds_scr[...] = jnp.zeros_like(ds_scr)
            for r in (dnw_ref, ddtb_ref, dalog_ref, ddsk_ref):
                r[...] = jnp.zeros_like(r)

        y = y_ref[...]
        _, gate_vjp = jax.vjp(lambda y_, z_, w_: _rms(y_ * _silu(z_), w_), y, z_ref[...].astype(F32), nw_ref[...])
        dy, dz, dnw = gate_vjp(dya_ref[...])
        dz_ref[...] = dz.astype(BF16)
        dnw_ref[...] += dnw

        dtr = dtr_ref[...]
        c = _ssd_common(dtr, dtb_ref[...], alog_ref[...], et_ref[...])
        dt, a, lane_lo, row, col = c["dt"], c["a"], c["lane_lo"], c["row"], c["col"]
        dsk = dsk_ref[...]
        p_a, p_dt, v_last = [], [], []
        da_cols = jnp.zeros((CHUNK, CHUNK), F32)
        da_rows = jnp.zeros((CHUNK, CHUNK), F32)
        for g in range(SSD_GROUPS):
            gs = slice(g * SSD_STATE, (g + 1) * SSD_STATE)
            bg, cg = b_ref[:, gs].astype(BF16), c_ref[:, gs].astype(BF16)
            bg_t, cg_t = b_ref[:, gs].T.astype(BF16), c_ref[:, gs].T.astype(BF16)
            cb, cb_t = _dot(cg, bg_t), _dot(bg, cg_t)
            dcb = jnp.zeros((CHUNK, CHUNK), F32)
            dbg = jnp.zeros((CHUNK, SSD_STATE), F32)
            dcg = jnp.zeros((CHUNK, SSD_STATE), F32)
            for pp in range(PAIRS_PER_GROUP):
                j = g * PAIRS_PER_GROUP + pp
                ps = slice(j * LANES, (j + 1) * LANES)
                x = xs_ref[:, ps]
                dtp, ecol, dsr = c["dt_x"][:, ps], c["ecol_x"][:, ps], c["dsr_x"][:, ps]
                elast = c["elast_x"][:, ps]
                xdt = x * dtp
                xb = xdt.astype(BF16)
                dskp = jnp.where(lane_lo[0:1], dsk[:, 2 * j:2 * j + 1], dsk[:, 2 * j + 1:2 * j + 2])
                dyp = dy[:, ps]
                dyb = dyp.astype(BF16)
                sp, dsn = sp_ref[0, j], ds_scr[j]
                spb, dsnb = sp.astype(BF16), dsn.astype(BF16)
                y_off = ecol * _dot(cg, spb)
                dw = (dyp * ecol).astype(BF16)
                dcg = dcg + _dot_nt(dw, spb)
                dsp = _dot(cg_t, dw) + elast * dsn
                xd = xdt * dsr
                zd = _dot(bg, dsnb) * dsr
                dbg = dbg + _dot_nt(xd.astype(BF16), dsnb)
                dxdt = zd
                zero = jnp.zeros_like(xb)
                for h, lm in ((2 * j, lane_lo), (2 * j + 1, jnp.logical_not(lane_lo))):
                    le = _head_decay(c, h)
                    dm = _dot_nt(jnp.where(lm, dyb, zero), jnp.where(lm, xb, zero))
                    dcb = dcb + dm * le
                    m = cb * le
                    m_t = (cb_t * _head_decay(c, h, transposed=True)).astype(BF16)
                    dxdt = dxdt + jnp.where(lm, _dot(m_t, dyb), 0.0)
                    q = dm * m
                    da_cols = da_cols + jnp.where(col == h, jnp.sum(q, axis=1, keepdims=True), 0.0)
                    da_rows = da_rows + jnp.where(row == h, _colsum(q), 0.0)
                ds_scr[j] = dsp
                dxs_ref[:, ps] = dxdt * dtp + dskp * dyp
                p_a.append(dyp * y_off - xdt * zd)
                p_dt.append(dxdt * x)
                v_last.append(_colsum(zd * xdt) + elast * _colsum(dsn * sp))
            dcbb = dcb.astype(BF16)
            db_ref[:, gs] = dbg + _dot_tn(dcbb, cg)
            dc_ref[:, gs] = dcg + _dot(dcbb, bg)
        e = e_ref[...]
        rows8 = jnp.concatenate([jnp.concatenate(v_last, axis=1), _colsum(dy * xs_ref[...]),
                                 jnp.zeros((6, SSD_INNER), F32)], axis=0)
        r8 = _dot_split(rows8, e)
        da = (_dot_split(jnp.concatenate(p_a, axis=1), e) + jnp.where(row == CHUNK - 1, r8[0:1], 0.0)
              + da_cols - da_rows.T)
        ddsk_ref[...] += r8[1:2]
        dadt = jnp.dot((row <= col).astype(F32), da, precision=HIGHEST, preferred_element_type=F32)
        ddt = dadt * a + _dot_split(jnp.concatenate(p_dt, axis=1), e)
        dalog_ref[...] += _colsum(dadt * dt) * a
        ddtr = ddt * _sigmoid(dtr + dtb_ref[...])
        ddtr_ref[...] = ddtr
        ddtb_ref[...] += _colsum(ddtr)

    ck = lambda n, col=0: pl.BlockSpec((CHUNK, n), lambda c: (nc - 1 - c, col))
    acc = lambda n: _full((1, n))
    return pl.pallas_call(
        body, name="ssd_bwd", grid=(nc,),
        in_specs=[ck(SSD_INNER), ck(SSD_INNER), ck(SSD_INNER), ck(SSD_INNER), ck(SSD_BC, SSD_INNER // SSD_BC),
                  ck(SSD_BC, SSD_INNER // SSD_BC + 1), ck(DT_PAD),
                  pl.BlockSpec((1, N_PAIRS, SSD_STATE, LANES), lambda c: (nc - 1 - c, 0, 0, 0)),
                  acc(DT_PAD), acc(DT_PAD), acc(DT_PAD), acc(SSD_INNER), _full((SSD_INNER, LANES)),
                  _full((LANES, SSD_INNER))],
        out_specs=[ck(SSD_INNER), ck(SSD_INNER), ck(SSD_BC), ck(SSD_BC), ck(DT_PAD),
                   acc(SSD_INNER), acc(DT_PAD), acc(DT_PAD), acc(DT_PAD)],
        out_shape=[jax.ShapeDtypeStruct((t, SSD_INNER), BF16), jax.ShapeDtypeStruct((t, SSD_INNER), F32),
                   jax.ShapeDtypeStruct((t, SSD_BC), F32), jax.ShapeDtypeStruct((t, SSD_BC), F32),
                   jax.ShapeDtypeStruct((t, DT_PAD), F32), jax.ShapeDtypeStruct((1, SSD_INNER), F32),
                   jax.ShapeDtypeStruct((1, DT_PAD), F32), jax.ShapeDtypeStruct((1, DT_PAD), F32),
                   jax.ShapeDtypeStruct((1, DT_PAD), F32)],
        scratch_shapes=[pltpu.VMEM((N_PAIRS, SSD_STATE, LANES), F32)], compiler_params=_params(1),
    )(dya, y, z, xc, xc, xc, dtr, sprev, dtb, alog, dsk, nw, e_heads, e_t)


def _sgu_act(uv, uvb, lnw, lnb):
    a = _gelu(uv + uvb)
    return a[:, :SGU_WIDTH], _layer_norm(a[:, SGU_WIDTH:], lnw, lnb)


def _sgu_weights(ws_ref):
    row = lax.broadcasted_iota(jnp.int32, (CHUNK, CHUNK), 0)
    col = lax.broadcasted_iota(jnp.int32, (CHUNK, CHUNK), 1)
    return [jnp.where(row >= col, ws_ref[g], 0.0).astype(BF16) for g in range(SGU_GROUPS)], row >= col


def _sgu_fwd(uv, uvb, lnw, lnb, ws, bs_t):
    t = uv.shape[0]

    def body(uv_ref, uvb_ref, lnw_ref, lnb_ref, ws_ref, bs_ref, o_ref):
        u, vn = _sgu_act(uv_ref[...].astype(F32), uvb_ref[...], lnw_ref[...], lnb_ref[...])
        wc, _ = _sgu_weights(ws_ref)
        bs = bs_ref[...]
        for g in range(SGU_GROUPS):
            gs = slice(g * LANES, (g + 1) * LANES)
            mixed = _dot(wc[g], vn[:, gs].astype(BF16)) + bs[:, g:g + 1]
            o_ref[:, gs] = (u[:, gs] * mixed).astype(BF16)

    return pl.pallas_call(
        body, name="sgu_fwd", grid=(t // CHUNK,),
        in_specs=[_rows(CHUNK, 2 * SGU_WIDTH), _full((1, 2 * SGU_WIDTH)), _full((1, SGU_WIDTH)), _full((1, SGU_WIDTH)),
                  _full(ws.shape), _full(bs_t.shape)],
        out_specs=_rows(CHUNK, SGU_WIDTH), out_shape=jax.ShapeDtypeStruct((t, SGU_WIDTH), BF16),
        compiler_params=_params(1),
    )(uv, uvb, lnw, lnb, ws, bs_t)


def _sgu_bwd(dyb, uv, uvb, lnw, lnb, ws, bs_t, e_groups):
    t = uv.shape[0]

    def body(dyb_ref, uv_ref, uvb_ref, lnw_ref, lnb_ref, ws_ref, bs_ref, e_ref,
             duv_ref, duvb_ref, dlnw_ref, dlnb_ref, dws_ref, dbs_ref):
        @pl.when(pl.program_id(0) == 0)
        def _():
            for r in (duvb_ref, dlnw_ref, dlnb_ref, dws_ref, dbs_ref):
                r[...] = jnp.zeros_like(r)

        (u, vn), act_vjp = jax.vjp(_sgu_act, uv_ref[...].astype(F32), uvb_ref[...], lnw_ref[...], lnb_ref[...])
        wc, causal = _sgu_weights(ws_ref)
        bs = bs_ref[...]
        dyb = dyb_ref[...]
        du, dvn, dmix = [], [], []
        for g in range(SGU_GROUPS):
            gs = slice(g * LANES, (g + 1) * LANES)
            vb = vn[:, gs].astype(BF16)
            mixed = _dot(wc[g], vb) + bs[:, g:g + 1]
            dm = dyb[:, gs] * u[:, gs]
            dmb = dm.astype(BF16)
            du.append(dyb[:, gs] * mixed)
            dvn.append(_dot_tn(wc[g], dmb))
            dws_ref[g] += jnp.where(causal, _dot_nt(dmb, vb), 0.0)
            dmix.append(dm)
        dbs_ref[...] += _dot_split(jnp.concatenate(dmix, axis=1), e_ref[...])
        duv, duvb, dlnw, dlnb = act_vjp((jnp.concatenate(du, axis=1), jnp.concatenate(dvn, axis=1)))
        duv_ref[...] = duv.astype(BF16)
        duvb_ref[...] += duvb
        dlnw_ref[...] += dlnw
        dlnb_ref[...] += dlnb

    return pl.pallas_call(
        body, name="sgu_bwd", grid=(t // CHUNK,),
        in_specs=[_rows(CHUNK, SGU_WIDTH), _rows(CHUNK, 2 * SGU_WIDTH), _full((1, 2 * SGU_WIDTH)),
                  _full((1, SGU_WIDTH)), _full((1, SGU_WIDTH)), _full(ws.shape), _full(bs_t.shape),
                  _full(e_groups.shape)],
        out_specs=[_rows(CHUNK, 2 * SGU_WIDTH), _full((1, 2 * SGU_WIDTH)), _full((1, SGU_WIDTH)),
                   _full((1, SGU_WIDTH)), _full(ws.shape), _full(bs_t.shape)],
        out_shape=[jax.ShapeDtypeStruct((t, 2 * SGU_WIDTH), BF16), jax.ShapeDtypeStruct((1, 2 * SGU_WIDTH), F32),
                   jax.ShapeDtypeStruct((1, SGU_WIDTH), F32), jax.ShapeDtypeStruct((1, SGU_WIDTH), F32),
                   jax.ShapeDtypeStruct(ws.shape, F32), jax.ShapeDtypeStruct(bs_t.shape, F32)],
        compiler_params=_params(1),
    )(dyb, uv, uvb, lnw, lnb, ws, bs_t, e_groups)


def _merge(gates, pa, pb, bg):
    s = _sigmoid(gates + bg)
    return s[:, :D_MODEL] * pa + s[:, D_MODEL:] * pb


def _branches_merge(branches, gates, bg):
    pa, pb = branches
    return (pa, pb, _merge(gates.astype(F32), pa, pb, bg)), ()


def _merge_backward(dmix, gates, pa, pb, bg):
    _, vjp = jax.vjp(_merge, gates.astype(F32), pa.astype(F32), pb.astype(F32), bg)
    dg, dpa, dpb, dbg = vjp(dmix)
    return (dg, dpa, dpb), (dbg,)


def _conv_f_fwd(up, cw, cb, tm=128):
    t, c = up.shape

    def body(x_ref, h_ref, w_ref, b_ref, o_ref, y_ref):
        halo = jnp.where(pl.program_id(0) > 0, h_ref[...].astype(F32)[8:], 0.0)
        y = _causal_conv(x_ref[...].astype(F32), halo, w_ref[...], b_ref[...])
        y_ref[...] = y.astype(BF16)
        o_ref[...] = (_silu(y[:, :D_FF]) * y[:, D_FF:]).astype(BF16)

    return pl.pallas_call(
        body, name="conv_f_fwd", grid=(t // tm,),
        in_specs=[_rows(tm, c), _halo(tm, c, rows=16), _full(cw.shape), _full((1, c))],
        out_specs=[_rows(tm, D_FF), _rows(tm, c)],
        out_shape=[jax.ShapeDtypeStruct((t, D_FF), BF16), jax.ShapeDtypeStruct((t, c), BF16)],
        compiler_params=_params(1),
    )(up, up, cw, cb)


def _conv_f_bwd(dact, y, up, cw, tm=128):
    t, c = up.shape
    nt = t // tm

    def body(d_ref, y_ref, x_ref, w_ref, dx_ref, dw_ref, db_ref, nxt_scr):
        @pl.when(pl.program_id(0) == 0)
        def _():
            nxt_scr[...] = jnp.zeros_like(nxt_scr)
            dw_ref[...] = jnp.zeros_like(dw_ref)
            db_ref[...] = jnp.zeros_like(db_ref)

        a, v = y_ref[:, :D_FF].astype(F32), y_ref[:, D_FF:].astype(F32)
        d = d_ref[...].astype(F32)
        silu_a, dsilu_a = _silu_and_grad(a)
        dy = jnp.concatenate([d * v * dsilu_a, d * silu_a], axis=1)
        dx, dw = _causal_conv_bwd(dy, nxt_scr[...], x_ref[...].astype(F32), w_ref[...])
        dx_ref[...] = dx.astype(BF16)
        nxt_scr[...] = dy[:8]
        dw_ref[...] += dw
        db_ref[...] += _colsum(dy)

    return pl.pallas_call(
        body, name="conv_f_bwd", grid=(nt,),
        in_specs=[_rows(tm, D_FF, nt, True), _rows(tm, c, nt, True), _rows(tm, c, nt, True), _full(cw.shape)],
        out_specs=[_rows(tm, c, nt, True), _full(cw.shape), _full((1, c))],
        out_shape=[jax.ShapeDtypeStruct((t, c), BF16), jax.ShapeDtypeStruct(cw.shape, F32),
                   jax.ShapeDtypeStruct((1, c), F32)],
        scratch_shapes=[pltpu.VMEM((8, c), F32)], compiler_params=_params(1),
    )(dact, y, up, cw)


def _conv_a_bwd(dxs, db, dc, y, xbc, cw, tm=256):
    t, c = xbc.shape
    nt = t // tm

    def body(dxs_ref, db_ref, dc_ref, y_ref, x_ref, w_ref, dx_ref, dw_ref, dbias_ref, nxt_scr):
        @pl.when(pl.program_id(0) == 0)
        def _():
            nxt_scr[...] = jnp.zeros_like(nxt_scr)
            dw_ref[...] = jnp.zeros_like(dw_ref)
            dbias_ref[...] = jnp.zeros_like(dbias_ref)

        dy = jnp.concatenate([dxs_ref[...], db_ref[...], dc_ref[...]], axis=1) * _dsilu(y_ref[...].astype(F32))
        dx, dw = _causal_conv_bwd(dy, nxt_scr[...], x_ref[...].astype(F32), w_ref[...])
        dx_ref[...] = dx.astype(BF16)
        nxt_scr[...] = dy[:8]
        dw_ref[...] += dw
        dbias_ref[...] += _colsum(dy)

    return pl.pallas_call(
        body, name="conv_a_bwd", grid=(nt,),
        in_specs=[_rows(tm, SSD_INNER, nt, True), _rows(tm, SSD_BC, nt, True), _rows(tm, SSD_BC, nt, True),
                  _rows(tm, c, nt, True), _rows(tm, c, nt, True), _full(cw.shape)],
        out_specs=[_rows(tm, c, nt, True), _full(cw.shape), _full((1, c))],
        out_shape=[jax.ShapeDtypeStruct((t, c), BF16), jax.ShapeDtypeStruct(cw.shape, F32),
                   jax.ShapeDtypeStruct((1, c), F32)],
        scratch_shapes=[pltpu.VMEM((8, c), F32)], compiler_params=_params(1),
    )(dxs, db, dc, y, xbc, cw)


def _pad_lanes(v, n=DT_PAD):
    return jnp.pad(v, ((0, 0), (0, n - v.shape[1])))


def _local_step(x, target, w, p, after=None, late_weights=None, on_grad=None, on_small=None):
    dtb, alog, dsk = _pad_lanes(p["dt_bias"]), _pad_lanes(p["a_log"]), _pad_lanes(p["d_skip"])
    bs_t = _pad_lanes(p["b_spatial"].T)
    e_heads = (jnp.arange(SSD_INNER)[:, None] // SSD_HEAD_DIM == jnp.arange(LANES)[None, :]).astype(BF16)
    e_heads_t = (jnp.arange(LANES)[:, None] == jnp.arange(SSD_INNER)[None, :] // SSD_HEAD_DIM).astype(BF16)
    e_groups = (jnp.arange(SGU_WIDTH)[:, None] // LANES == jnp.arange(LANES)[None, :]).astype(BF16)

    n1 = _norm_fwd(x, p["norm1_w"], "norm1_fwd", after=after)
    z = _mm(n1, w["z"], "nt", "proj_z", out_dtype=BF16)
    xbc = _mm(n1, w["xbc"], "nt", "proj_xbc", out_dtype=BF16)
    dtr = _mm(n1, w["dt"], "nt", "proj_dt")
    uv = _mm(n1, w["uv"], "nt", "proj_uv", out_dtype=BF16)
    gates = _mm(n1, w["gates"], "nt", "proj_gates", out_dtype=BF16)
    xc, conv_a_out = _conv_a_fwd(xbc, w["conv_a"], p["conv_a_b"])
    y, ya, sprev = _ssd_fwd(xc, dtr, z, dtb, alog, dsk, p["ssd_norm_w"], e_heads_t)
    yb = _sgu_fwd(uv, p["uv_b"], p["v_ln_w"], p["v_ln_b"], p["w_spatial"], bs_t)
    if late_weights is not None:
        w = {**w, **late_weights(ya, yb)}
    narrow = (D_MODEL, BF16)
    pa, pb, mix = _mm_rows(
        [ya, yb], [w["branch_a"], w["branch_b"]], "nn", "branches", _branches_merge, rows=[gates],
        fulls=[p["b_gate"]], row_outs=[narrow] * 3, summed=False)
    wide = [(D_MODEL, F32), (D_MODEL, BF16)]
    h1, n2 = _mm_rows(mix, w["out"], "nn", "out_proj", _residual_norm, rows=[x], fulls=[p["norm2_w"]], row_outs=wide)
    up = _mm(n2, w["up"], "nt", "up_proj", out_dtype=BF16)
    act, conv_f_out = _conv_f_fwd(up, w["conv_f"], p["conv_f_b"])
    dh2, dh2b, loss, g_final = _mm_rows(
        act, w["down"], "nn", "down_proj", _loss_and_grad, rows=[h1, target], fulls=[p["final_norm_w"]],
        row_outs=wide, acc_outs=[(8, LANES), (1, D_MODEL)])

    on_grad = on_grad or (lambda name, grads: None)
    g = {"final_norm_w": g_final}
    g["down"] = _wgrad(act, dh2b, "down_wgrad")
    tok = on_grad("w_down", g)
    dact = _mm(dh2b, w["down"], "nt", "down_dgrad", out_dtype=BF16, after=tok)
    dup, g["conv_f"], g["conv_f_b"] = _conv_f_bwd(dact, conv_f_out, up, w["conv_f"])
    g["up"] = _wgrad(dup, n2, "up_wgrad")
    tok = on_grad("w_up", g)
    dh1, dh1b, g["norm2_w"] = _mm_rows(
        dup, w["up"], "nn", "up_dgrad", _norm_backward, rows=[h1, dh2], fulls=[p["norm2_w"]], row_outs=wide,
        acc_outs=[(1, D_MODEL)], after=tok)
    g["out"] = _wgrad(mix, dh1b, "out_wgrad")
    tok = on_grad("w_out", g)
    dgates, dpa, dpb, g["b_gate"] = _mm_rows(
        dh1b, w["out"], "nt", "out_dgrad", _merge_backward, rows=[gates, pa, pb], fulls=[p["b_gate"]],
        row_outs=[(2 * D_MODEL, BF16), (D_MODEL, BF16), (D_MODEL, BF16)], acc_outs=[(1, 2 * D_MODEL)], after=tok)
    g["branch_a"] = _wgrad(ya, dpa, "branch_a_wgrad")
    g["branch_b"] = _wgrad(yb, dpb, "branch_b_wgrad")
    tok = on_grad("w_branch", g)
    dya, dyb = _mm_rows(
        [dpa, dpb], [w["branch_a"], w["branch_b"]], "nt", "branches_dgrad", lambda products: (tuple(products), ()),
        row_outs=[(SSD_INNER, F32), (SGU_WIDTH, F32)], after=tok, summed=False)
    duv, g["uv_b"], g["v_ln_w"], g["v_ln_b"], g["w_spatial"], dbs_t = _sgu_bwd(
        dyb, uv, p["uv_b"], p["v_ln_w"], p["v_ln_b"], p["w_spatial"], bs_t, e_groups)
    g["b_spatial"] = dbs_t[:, :SGU_GROUPS].T
    dz, dxs, db, dc, ddtr, g["ssd_norm_w"], ddtb, dalog, ddsk = _ssd_bwd(
        dya, y, z, xc, dtr, sprev, dtb, alog, dsk, p["ssd_norm_w"], e_heads, e_heads_t)
    g["dt_bias"], g["a_log"], g["d_skip"] = ddtb, dalog, ddsk
    dxbc, g["conv_a"], g["conv_a_b"] = _conv_a_bwd(dxs, db, dc, conv_a_out, xbc, w["conv_a"])
    tok = on_small(g, loss) if on_small else None
    ddtrb = ddtr.astype(BF16)
    for name, d in (("z", dz), ("xbc", dxbc), ("dt", ddtrb), ("uv", duv), ("gates", dgates)):
        g[name] = _wgrad(d, n1, name + "_wgrad", after=tok)
    tok = on_grad("w_in", g)
    dn1 = _mm([dz, dxbc], [w["z"], w["xbc"]], "nn", "ssd_dgrad", after=tok)
    gx, g["norm1_w"] = _mm_rows(
        [duv, dgates, ddtrb], [w["uv"], w["gates"], w["dt"]], "nn", "in_dgrad",
        lambda r, so_far, h, dres, w_: tuple(t[:1] for t in _norm_backward(r + so_far, h, dres, w_)),
        rows=[dn1, x, dh1], fulls=[p["norm1_w"]], row_outs=wide[:1], acc_outs=[(1, D_MODEL)])
    return loss, gx, g


def _place():
    return lax.axis_index("x"), lax.axis_index("y"), lax.axis_index("c")


def _other_chips(x, y):
    return [(1 - x, y), (x, 1 - y), (1 - x, 1 - y)]


def _all_gather(shards, name):
    n = len(shards)

    def body(*refs):
        ins, outs = refs[:n], refs[n:2 * n]
        send_sems, recv_sems, local_sems = refs[2 * n:]
        x, y, c = _place()
        me, sibling = (x, y, c), (x, y, 1 - c)
        chips = _other_chips(x, y)

        def copy(a, k, block, to, src=None):
            slot = outs[a].at[4 * block[0] + 2 * block[1] + block[2]]
            return pltpu.make_async_remote_copy(
                src_ref=slot if src is None else src, dst_ref=slot, send_sem=send_sems.at[7 * a + k],
                recv_sem=recv_sems.at[7 * a + k], device_id=to, device_id_type=MESH)

        started = []
        for a in range(n):
            mine = pltpu.make_async_copy(ins[a], outs[a].at[4 * x + 2 * y + c], local_sems.at[a])
            mine.start()
            started.append(mine)
        sends = []
        for a in range(n):
            sends.append(copy(a, 0, me, sibling, src=ins[a]))
            sends += [copy(a, 1 + j, me, (*chip, c), src=ins[a]) for j, chip in enumerate(chips)]
        for cp in sends:
            cp.start()
        for a in range(n):
            for j, chip in enumerate(chips):
                copy(a, 1 + j, (*chip, c), me).wait_recv()
                fwd = copy(a, 4 + j, (*chip, c), sibling)
                fwd.start()
                sends.append(fwd)
        for a in range(n):
            copy(a, 0, sibling, me).wait_recv()
            for j, chip in enumerate(chips):
                copy(a, 4 + j, (*chip, 1 - c), me).wait_recv()
        for cp in sends:
            cp.wait_send()
        for mine in started:
            mine.wait()

    any_spec = pl.BlockSpec(memory_space=pl.ANY)
    return pl.pallas_call(
        body, name=name, in_specs=[any_spec] * n, out_specs=[any_spec] * n,
        out_shape=[jax.ShapeDtypeStruct((N_DEV, *s.shape), s.dtype) for s in shards],
        scratch_shapes=[pltpu.SemaphoreType.DMA((7 * n,)), pltpu.SemaphoreType.DMA((7 * n,)),
                        pltpu.SemaphoreType.DMA((n,))],
    )(*shards)


HBM_SPEC = pl.BlockSpec(memory_space=pltpu.HBM)
SEM_SPEC = pl.BlockSpec(memory_space=pltpu.SEMAPHORE)
ANY_SPEC = pl.BlockSpec(memory_space=pl.ANY)
DATAFLOW = pltpu.SideEffectType.DATAFLOW_SIDE_EFFECTING
N_PEERS = N_DEV - 1


def _peers(x, y, c):
    out = []
    for r in range(1, N_DEV):
        fx, fy, fc = r >> 2 & 1, r >> 1 & 1, r & 1
        out.append(((1 - x) if fx else x, (1 - y) if fy else y, (1 - c) if fc else c))
    return out


def _gather_copies(srcs, lands, send_sems, recv_sems, sending, scatter=False):
    x, y, c = _place()
    copies = []
    for a, (src, land) in enumerate(zip(srcs, lands)):
        for j, (px, py, pc) in enumerate(_peers(x, y, c)):
            mine, theirs = 4 * x + 2 * y + c, 4 * px + 2 * py + pc
            block = src.at[theirs if sending else 0] if scatter else src
            copies.append(pltpu.make_async_remote_copy(
                src_ref=block, dst_ref=land.at[mine if sending else theirs], send_sem=send_sems.at[N_PEERS * a + j],
                recv_sem=recv_sems.at[N_PEERS * a + j], device_id=(px, py, pc), device_id_type=MESH))
    return copies


def _gather_start(shards, after, name, scatter=False):
    n = len(shards)
    after = [] if after is None else [after]

    def body(*refs):
        srcs, lands = refs[:n], refs[n:2 * n]
        send_sems, recv_sems = refs[2 * n + len(after):2 * n + len(after) + 2]
        token = refs[-1]
        for cp in _gather_copies(srcs, lands, send_sems, recv_sems, sending=True, scatter=scatter):
            cp.start()
        token[...] = jnp.zeros_like(token)

    lands = [lax.empty(s.shape if scatter else (N_DEV, *s.shape), s.dtype) for s in shards]
    hbm = lambda a: pltpu.with_memory_space_constraint(a, pltpu.HBM)
    out = pl.pallas_call(
        body, name=name,
        out_shape=(pltpu.SemaphoreType.DMA((N_PEERS * n,)), pltpu.SemaphoreType.DMA((N_PEERS * n,)),
                   *[pltpu.HBM(a.shape, a.dtype) for a in (*shards, *lands)], jax.ShapeDtypeStruct((8, LANES), F32)),
        in_specs=[HBM_SPEC] * (2 * n) + [ANY_SPEC] * len(after),
        out_specs=(SEM_SPEC, SEM_SPEC, *[HBM_SPEC] * (2 * n), pl.BlockSpec(memory_space=pltpu.VMEM)),
        input_output_aliases={i: 2 + i for i in range(2 * n)},
        compiler_params=pltpu.CompilerParams(has_side_effects=DATAFLOW),
    )(*[hbm(a) for a in (*shards, *lands)], *after)
    return out[0], out[1], out[2:2 + n], out[2 + n:2 + 2 * n], out[-1]


def _gather_wait(send_sems, recv_sems, shards, lands, after, name, scatter=False):
    n = len(shards)
    after = tuple(after)

    def body(*refs):
        srcs, lands_ = refs[:n], refs[n:2 * n]
        send, recv = refs[2 * n:2 * n + 2]
        for cp in _gather_copies(srcs, lands_, send, recv, sending=False, scatter=scatter):
            cp.wait_send()
            cp.wait_recv()

    out = pl.pallas_call(
        body, name=name, out_shape=tuple(pltpu.HBM(a.shape, a.dtype) for a in (*shards, *lands)),
        in_specs=[HBM_SPEC] * (2 * n) + [SEM_SPEC, SEM_SPEC] + [ANY_SPEC] * len(after),
        out_specs=tuple([HBM_SPEC] * (2 * n)), input_output_aliases={i: i for i in range(2 * n)},
        compiler_params=pltpu.CompilerParams(has_side_effects=DATAFLOW),
    )(*shards, *lands, send_sems, recv_sems, *after)
    return out[:n], out[n:]


def _chip_copies(src, land, send_sems, recv_sems):
    x, y, c = _place()
    return [pltpu.make_async_remote_copy(
        src_ref=src.at[2 * cx + cy], dst_ref=land.at[j], send_sem=send_sems.at[j], recv_sem=recv_sems.at[j],
        device_id=(cx, cy, c), device_id_type=MESH) for j, (cx, cy) in enumerate(_other_chips(x, y))]


def _chips_start(q, name):
    def body(q_ref, land_ref, send_sems, recv_sems, q_thru, land_thru, token):
        for cp in _chip_copies(q_ref, land_ref, send_sems, recv_sems):
            cp.start()
        token[...] = jnp.zeros_like(token)

    land = lax.empty((3, *q.shape[1:]), q.dtype)
    return pl.pallas_call(
        body, name=name,
        out_shape=(pltpu.SemaphoreType.DMA((3,)), pltpu.SemaphoreType.DMA((3,)), pltpu.HBM(q.shape, q.dtype),
                   pltpu.HBM(land.shape, land.dtype), jax.ShapeDtypeStruct((8, LANES), F32)),
        in_specs=[HBM_SPEC, HBM_SPEC],
        out_specs=(SEM_SPEC, SEM_SPEC, HBM_SPEC, HBM_SPEC, pl.BlockSpec(memory_space=pltpu.VMEM)),
        input_output_aliases={0: 2, 1: 3}, compiler_params=pltpu.CompilerParams(has_side_effects=DATAFLOW),
    )(pltpu.with_memory_space_constraint(q, pltpu.HBM), pltpu.with_memory_space_constraint(land, pltpu.HBM))


def _chips_wait(send_sems, recv_sems, q, land, after, name):
    def body(q_ref, land_ref, send, recv, after_ref, q_out, land_out):
        for cp in _chip_copies(q_ref, land_ref, send, recv):
            cp.wait_send()
            cp.wait_recv()

    return pl.pallas_call(
        body, name=name, out_shape=(pltpu.HBM(q.shape, q.dtype), pltpu.HBM(land.shape, land.dtype)),
        in_specs=[HBM_SPEC, HBM_SPEC, SEM_SPEC, SEM_SPEC, ANY_SPEC], out_specs=(HBM_SPEC, HBM_SPEC),
        input_output_aliases={0: 0, 1: 1}, compiler_params=pltpu.CompilerParams(has_side_effects=DATAFLOW),
    )(q, land, send_sems, recv_sems, after)[1]


def _exchange_cores(part, name):
    def body(in_ref, out_ref, send_sems, recv_sems):
        x, y, c = _place()
        copies = [pltpu.make_async_remote_copy(
            src_ref=in_ref.at[2 * k + (1 - c)], dst_ref=out_ref.at[k], send_sem=send_sems.at[k],
            recv_sem=recv_sems.at[k], device_id=(x, y, 1 - c), device_id_type=MESH) for k in range(4)]
        for cp in copies:
            cp.start()
        for cp in copies:
            cp.wait()

    return pl.pallas_call(
        body, name=name, in_specs=[ANY_SPEC], out_specs=ANY_SPEC,
        out_shape=jax.ShapeDtypeStruct((4, *part.shape[1:]), part.dtype),
        scratch_shapes=[pltpu.SemaphoreType.DMA((4,)), pltpu.SemaphoreType.DMA((4,))],
    )(part)


def _chip_sum(part, got, place, name, tr=256):
    _, r, c = part.shape
    tr, tc = _tile2d(r, c, tr)

    def body(place_ref, p_ref, g_ref, q_ref, own_ref):
        s = p_ref[0].astype(F32) + g_ref[0].astype(F32)
        q_ref[0] = s.astype(BF16)

        @pl.when(pl.program_id(2) == place_ref[1])
        def _():
            own_ref[...] = s

    grid_spec = pltpu.PrefetchScalarGridSpec(
        num_scalar_prefetch=1, grid=(r // tr, c // tc, 4),
        in_specs=[pl.BlockSpec((1, tr, tc), lambda i, j, k, pr: (2 * k + pr[0], i, j)),
                  pl.BlockSpec((1, tr, tc), lambda i, j, k, pr: (k, i, j))],
        out_specs=[pl.BlockSpec((1, tr, tc), lambda i, j, k, pr: (k, i, j)),
                   pl.BlockSpec((tr, tc), lambda i, j, k, pr: (i, j))])
    return pl.pallas_call(
        body, name=name, grid_spec=grid_spec,
        out_shape=[jax.ShapeDtypeStruct((4, r, c), BF16), jax.ShapeDtypeStruct((r, c), F32)],
        compiler_params=_params(3),
    )(place, part, got)


def _sum_adamw(own, got, w, m, v, name):
    r, c = own.shape
    tc = 4 * LANES

    def body(own_ref, got_ref, w_ref, m_ref, v_ref, g_ref, d_ref, nm_ref, nv_ref):
        g = own_ref[...]
        for j in range(3):
            g = g + got_ref[j].astype(F32)
        two_d = lambda ref: ref[...].reshape(r, tc)
        delta, nm, nv = _adamw(two_d(w_ref), g, two_d(m_ref), two_d(v_ref))
        for ref, val in ((g_ref, g), (d_ref, delta), (nm_ref, nm), (nv_ref, nv)):
            ref[...] = val.reshape(ref.shape)

    wblk = pl.BlockSpec((r, 1, tc), lambda j: (0, 0, j))
    return pl.pallas_call(
        body, name=name, grid=(c // tc,),
        in_specs=[pl.BlockSpec((r, tc), lambda j: (0, j)), pl.BlockSpec((3, r, tc), lambda j: (0, 0, j)),
                  wblk, wblk, wblk],
        out_specs=[wblk] * 4, out_shape=[jax.ShapeDtypeStruct(w.shape, F32)] * 4, compiler_params=_params(1),
    )(own, got, w, m, v)


def _adamw(w, g, m, v):
    m = ADAM_B1 * m + (1.0 - ADAM_B1) * g
    v = ADAM_B2 * v + (1.0 - ADAM_B2) * jnp.square(g)
    m_hat = m / (1.0 - ADAM_B1 ** ADAM_STEP)
    v_hat = v / (1.0 - ADAM_B2 ** ADAM_STEP)
    return -ADAM_LR * (m_hat / (jnp.sqrt(v_hat) + ADAM_EPS) + ADAM_WD * w), m, v


def _sum8_adamw(part, got, place, w, m, v, name, tr=256):
    r, c = w.shape
    tr, tc = _tile2d(r, c, tr)
    blk = pl.BlockSpec((tr, tc), lambda i, j, pr: (i, j))

    def body(place_ref, own_ref, got_ref, w_ref, m_ref, v_ref, g_ref, d_ref, nm_ref, nv_ref):
        dev = 2 * place_ref[1] + place_ref[0]
        g = jnp.zeros((tr, tc), F32)
        for d in range(N_DEV):
            g = g + jnp.where(dev == d, own_ref[0], got_ref[d]).astype(F32)
        g_ref[...] = g
        d_ref[...], nm_ref[...], nv_ref[...] = _adamw(w_ref[...], g, m_ref[...], v_ref[...])

    grid_spec = pltpu.PrefetchScalarGridSpec(
        num_scalar_prefetch=1, grid=(r // tr, c // tc),
        in_specs=[pl.BlockSpec((1, tr, tc), lambda i, j, pr: (2 * pr[1] + pr[0], i, j)),
                  pl.BlockSpec((N_DEV, tr, tc), lambda i, j, pr: (0, i, j)), blk, blk, blk],
        out_specs=[blk] * 4)
    return pl.pallas_call(
        body, name=name, grid_spec=grid_spec, out_shape=[jax.ShapeDtypeStruct(w.shape, F32)] * 4,
        compiler_params=_params(2),
    )(place, part, got, w, m, v)


VECTORS = ["norm1_w", "b_gate", "conv_a_b", "dt_bias", "a_log", "d_skip", "ssd_norm_w", "uv_b", "v_ln_w", "v_ln_b",
           "norm2_w", "conv_f_b", "final_norm_w"]
SMALL_ORDER = VECTORS + ["w_spatial", "b_spatial", "conv_a_w", "conv_f_w"]


ROW_VECTORS = VECTORS[1:]


def _small_adamw(gathered, w, m, v):
    sizes = {n: w[n].shape[1] for n in ROW_VECTORS}
    offs, off = {}, 0
    for n in ROW_VECTORS:
        offs[n] = off
        off += -(-sizes[n] // LANES) * LANES
    loss_off = off
    k = len(SMALL_ORDER)
    n_g = len(gathered)

    def body(*refs):
        row_ref, ws_ref, bs_ref, ca_ref, cf_ref, n1_ref = refs[:n_g]
        w_refs, m_refs, v_refs = (dict(zip(SMALL_ORDER, refs[n_g + i * k:n_g + (i + 1) * k])) for i in range(3))
        outs = refs[n_g + 3 * k:]
        x, y, c = _place()
        dev = 4 * x + 2 * y + c

        def total(ref):
            s = ref[0]
            for d in range(1, N_DEV):
                s = s + ref[d]
            return s

        row = total(row_ref)
        grads = {n: row[:, offs[n]:offs[n] + sizes[n]] for n in ROW_VECTORS}
        grads["norm1_w"], grads["w_spatial"], grads["b_spatial"] = total(n1_ref), total(ws_ref), total(bs_ref)
        for n, ref in (("conv_a_w", ca_ref), ("conv_f_w", cf_ref)):
            whole, cols = total(ref), w_refs[n].shape[1]
            mine = whole[:, :cols]
            for d in range(1, N_DEV):
                mine = jnp.where(dev == d, whole[:, d * cols:(d + 1) * cols], mine)
            grads[n] = mine
        for i, n in enumerate(SMALL_ORDER):
            outs[4 * i][...] = grads[n]
            outs[4 * i + 1][...], outs[4 * i + 2][...], outs[4 * i + 3][...] = _adamw(
                w_refs[n][...], grads[n], m_refs[n][...], v_refs[n][...])
        outs[4 * k][...] = row[:, loss_off:loss_off + LANES]

    out = pl.pallas_call(
        body, name="adamw_small",
        out_shape=[jax.ShapeDtypeStruct(w[n].shape, F32) for n in SMALL_ORDER for _ in range(4)]
        + [jax.ShapeDtypeStruct((1, LANES), F32)],
        compiler_params=_params(0),
    )(*gathered, *[t[n] for t in (w, m, v) for n in SMALL_ORDER])
    return [dict(zip(SMALL_ORDER, out[j:4 * k:4])) for j in range(4)] + [out[4 * k]]


SMALL = ["norm1_w", "b_gate", "conv_a_b", "dt_bias", "a_log", "d_skip", "ssd_norm_w", "uv_b", "v_ln_w", "v_ln_b",
         "w_spatial", "b_spatial", "norm2_w", "conv_f_b", "final_norm_w"]
BIG = ["w_in", "w_branch", "w_out", "w_up", "w_down"]
TRANSPOSED = ("w_in", "w_up")
WEIGHTS = ["norm1_w", "w_in", "b_gate", "conv_a_w", "conv_a_b", "dt_bias", "a_log", "d_skip", "ssd_norm_w", "uv_b",
           "v_ln_w", "v_ln_b", "w_spatial", "b_spatial", "w_branch", "w_out", "norm2_w", "w_up", "conv_f_w",
           "conv_f_b", "w_down", "final_norm_w"]
IN_SPLITS = [("z", 0, 2048), ("xbc", 2048, 5120), ("dt", 5120, 5152), ("uv", 5152, 7200), ("gates", 7200, 9248)]


def _columns_from_devices(a):
    return a.transpose(1, 0, 2).reshape(a.shape[1], -1)


def kernel(x, norm1_w, w_in, b_gate, conv_a_w, conv_a_b, dt_bias, a_log, d_skip, ssd_norm_w, uv_b, v_ln_w, v_ln_b, w_spatial, b_spatial, w_branch, w_out, norm2_w, w_up, conv_f_w, conv_f_b, w_down, final_norm_w, loss_target, m_norm1_w, m_w_in, m_b_gate, m_conv_a_w, m_conv_a_b, m_dt_bias, m_a_log, m_d_skip, m_ssd_norm_w, m_uv_b, m_v_ln_w, m_v_ln_b, m_w_spatial, m_b_spatial, m_w_branch, m_w_out, m_norm2_w, m_w_up, m_conv_f_w, m_conv_f_b, m_w_down, m_final_norm_w, v_norm1_w, v_w_in, v_b_gate, v_conv_a_w, v_conv_a_b, v_dt_bias, v_a_log, v_d_skip, v_ssd_norm_w, v_uv_b, v_v_ln_w, v_v_ln_b, v_w_spatial, v_b_spatial, v_w_branch, v_w_out, v_norm2_w, v_w_up, v_conv_f_w, v_conv_f_b, v_w_down, v_final_norm_w):
    args = dict(locals())
    wts = {n: args[n] for n in WEIGHTS}
    mom = {n: args["m_" + n] for n in WEIGHTS}
    var = {n: args["v_" + n] for n in WEIGHTS}
    cx, cy, cc = _place()
    dev = 4 * cx + 2 * cy + cc
    place = jnp.stack([cc, 2 * cx + cy]).astype(jnp.int32)

    def shard2d(n, a):
        return a[0].T if n in TRANSPOSED else a[0]

    def unshard(n, b):
        return (b.T if n in TRANSPOSED else b)[None]

    g_in, g_conv_a, g_conv_f = _all_gather(
        [shard2d("w_in", w_in).astype(BF16), conv_a_w[0], conv_f_w[0]], "gather_w_in")
    late = [shard2d(n, wts[n]).astype(BF16) for n in BIG[1:]]
    send_sems, recv_sems, late, lands, token = _gather_start(late, g_in, "gather_late_start")
    w_in_rows = g_in.reshape(-1, D_MODEL)
    w = {name: w_in_rows[lo:hi] for name, lo, hi in IN_SPLITS}
    w["dt"] = jnp.pad(w["dt"], ((0, DT_PAD - SSD_HEADS), (0, 0)))
    w["conv_a"] = _columns_from_devices(g_conv_a)
    w["conv_f"] = _columns_from_devices(g_conv_f)

    def late_weights(*after):
        mine, got = _gather_wait(send_sems, recv_sems, late, lands, after, "gather_late_wait")
        g_branch, g_out, g_up, g_down = [lax.dynamic_update_index_in_dim(land, own, dev, 0).reshape(-1, D_MODEL)
                                         for land, own in zip(got, mine)]
        return {"branch_a": g_branch[:SSD_INNER], "branch_b": g_branch[SSD_INNER:], "out": g_out, "up": g_up,
                "down": g_down}

    in_flight = {}

    def on_grad(n, g):
        part = {"w_in": lambda: jnp.concatenate([g[name][:hi - lo] for name, lo, hi in IN_SPLITS], axis=0),
                "w_branch": lambda: jnp.concatenate([g["branch_a"], g["branch_b"]], axis=0),
                "w_out": lambda: g["out"], "w_up": lambda: g["up"], "w_down": lambda: g["down"]}[n]()
        part = part.reshape(N_DEV, -1, D_MODEL)
        if n == "w_in":
            q, own = _chip_sum(part, _exchange_cores(part, "to_other_core_w_in"), place, "chip_sum_w_in")
            send, recv, q, land, tok = _chips_start(q, "to_other_chips_start_w_in")
            in_flight[n] = (own, send, recv, q, land)
            return tok
        send, recv, (part,), (land,), tok = _gather_start([part], None, f"to_owners_start_{n}", scatter=True)
        in_flight[n] = (part, send, recv, land)
        return tok

    p = {n: wts[n][0] if wts[n].ndim > 2 else wts[n].reshape(1, -1) for n in SMALL}
    small_flight = []

    def on_small(g, loss):
        arrays = [jnp.concatenate([g[n] for n in ROW_VECTORS] + [loss[:1]], axis=1), g["w_spatial"], g["b_spatial"],
                  g["conv_a"], g["conv_f"]]
        *flight, tok = _gather_start(arrays, g["conv_a"], "gather_small_start")
        small_flight.append(flight)
        return tok

    loss, gx, g = _local_step(x[0], loss_target[0], w, p, after=token, late_weights=late_weights, on_grad=on_grad,
                              on_small=on_small)
    *flight, _ = _gather_start([g["norm1_w"]], gx, "gather_norm1_start")
    small_flight.append(flight)

    grads, delta, new_m, new_v = {}, {}, {}, {}

    def big_adamw(n, after):
        if n == "w_in":
            own, send, recv, q, land = in_flight[n]
            got = _chips_wait(send, recv, q, land, after, "to_other_chips_wait_w_in")
            out = _sum_adamw(own, got, *[t[n].transpose(2, 0, 1) for t in (wts, mom, var)], "adamw_w_in")
            grads[n], delta[n], new_m[n], new_v[n] = [o.transpose(1, 2, 0) for o in out]
            return out[1]
        part, send, recv, land = in_flight[n]
        (part,), (got,) = _gather_wait(send, recv, [part], [land], [after], f"to_owners_wait_{n}", scatter=True)
        out = _sum8_adamw(part, got, place, *[shard2d(n, t[n]) for t in (wts, mom, var)], f"adamw_{n}")
        grads[n], delta[n], new_m[n], new_v[n] = [unshard(n, o) for o in out]
        return out[1]

    after = gx
    for n in ("w_down", "w_up", "w_out", "w_branch"):
        after = big_adamw(n, after)
    gathered = []
    for (send, recv, mine, land), name in zip(small_flight, ("gather_small_wait", "gather_norm1_wait")):
        mine, got = _gather_wait(send, recv, mine, land, [after], name)
        gathered += [lax.dynamic_update_index_in_dim(full, own, dev, 0) for full, own in zip(got, mine)]
    small = [{n: t[n][0] if t[n].ndim > 2 else t[n].reshape(1, -1) for n in SMALL_ORDER} for t in (wts, mom, var)]
    *outs, loss = _small_adamw(gathered, *small)
    for tgt, out in zip((grads, delta, new_m, new_v), outs):
        tgt.update({n: out[n].reshape(wts[n].shape) for n in SMALL_ORDER})
    big_adamw("w_in", loss)
    loss = loss[0, 0]

    return (loss, gx[None], *[grads[n] for n in WEIGHTS], *[delta[n] for n in WEIGHTS],
            *[new_m[n] for n in WEIGHTS], *[new_v[n] for n in WEIGHTS])
```

```python
import functools

import jax
import jax.numpy as jnp
from jax import lax
from jax.experimental import pallas as pl
from jax.experimental.pallas import tpu as pltpu

F32, BF16 = jnp.float32, jnp.bfloat16
HIGHEST = lax.Precision.HIGHEST

D_MODEL = 1024
SSD_INNER = 2048
SSD_HEAD_DIM = 64
SSD_HEADS = 32
SSD_GROUPS = 4
SSD_STATE = 128
SSD_BC = SSD_GROUPS * SSD_STATE
SSD_XBC = SSD_INNER + 2 * SSD_BC
SSD_CONV = 4
CHUNK = 128
N_PAIRS = SSD_HEADS // 2
PAIRS_PER_GROUP = N_PAIRS // SSD_GROUPS
SGU_WIDTH = 1024
SGU_GROUPS = 8
D_FF = 2816
FFN_CONV = 3
NORM_EPS = 1e-6
LN_EPS = 1e-5
LANES = 128
DT_PAD = LANES

ADAM_LR, ADAM_B1, ADAM_B2, ADAM_EPS, ADAM_WD, ADAM_STEP = 0.001, 0.9, 0.999, 1e-08, 0.01, 10

N_DEV = 8
VMEM_LIMIT = 56 * 1024 * 1024
MESH = pl.DeviceIdType.MESH


def _params(n_grid, **kw):
    sem = dict(dimension_semantics=("arbitrary",) * n_grid) if n_grid else {}
    return pltpu.CompilerParams(vmem_limit_bytes=VMEM_LIMIT, **sem, **kw)


def _tile(n, pref):
    t = (min(pref, n) // LANES) * LANES
    while n % t:
        t -= LANES
    return t


def _row_tile(r, pref):
    for t in range(min(pref, r) // 16 * 16, 0, -16):
        if r % t == 0:
            return t
    return r


def _tile2d(r, c, rows):
    if r % 16 == 0:
        return _row_tile(r, rows), c
    return r, _tile(c, 2 * LANES)


def _rows(tm, n, nt=None, rev=False):
    if rev:
        return pl.BlockSpec((tm, n), lambda i: (nt - 1 - i, 0))
    return pl.BlockSpec((tm, n), lambda i: (i, 0))


def _halo(tm, n, rows=8):
    per = tm // rows
    return pl.BlockSpec((rows, n), lambda i: (jnp.maximum(i * per - 1, 0), 0))


def _full(shape):
    nd = len(shape)
    return pl.BlockSpec(shape, lambda *_: (0,) * nd)


def _rms(x, w, eps=NORM_EPS):
    return x * lax.rsqrt(jnp.mean(x * x, axis=-1, keepdims=True) + eps) * w


def _layer_norm(x, w, b):
    mu = jnp.mean(x, axis=-1, keepdims=True)
    var = jnp.mean(jnp.square(x - mu), axis=-1, keepdims=True)
    return (x - mu) * lax.rsqrt(var + LN_EPS) * w + b


def _sigmoid(x):
    return 1.0 / (1.0 + jnp.exp(-x))


def _silu(x):
    return x * _sigmoid(x)


def _dsilu(x):
    s = _sigmoid(x)
    return s * (1.0 + x * (1.0 - s))


def _silu_and_grad(x):
    s = _sigmoid(x)
    return x * s, s * (1.0 + x * (1.0 - s))


def _softplus(x):
    return jnp.maximum(x, 0.0) + jnp.log(1.0 + jnp.exp(-jnp.abs(x)))


def _gelu(x):
    return jax.nn.gelu(x)


def _dot(a, b):
    return jnp.dot(a, b, preferred_element_type=F32)


def _dot_nt(a, b):
    return lax.dot_general(a, b, (((1,), (1,)), ((), ())), preferred_element_type=F32)


def _dot_tn(a, b):
    return lax.dot_general(a, b, (((0,), (0,)), ((), ())), preferred_element_type=F32)


def _dot_split(p, e):
    hi = p.astype(BF16)
    lo = (p - hi.astype(F32)).astype(BF16)
    return _dot(hi, e) + _dot(lo, e)


def _colsum(x):
    return jnp.sum(x, axis=0, keepdims=True)


def _shift_down(x, halo, j):
    xs = pltpu.roll(x, j, 0)
    hs = pltpu.roll(halo, j, 0)
    r8 = lax.broadcasted_iota(jnp.int32, hs.shape, 0)
    return jnp.concatenate([jnp.where(r8 < j, hs, xs[:8]), xs[8:]], axis=0)


def _shift_up(x, nxt, j):
    n = x.shape[0]
    xs = pltpu.roll(x, n - j, 0)
    ns = pltpu.roll(nxt, 8 - j, 0)
    r8 = lax.broadcasted_iota(jnp.int32, ns.shape, 0)
    return jnp.concatenate([xs[:n - 8], jnp.where(r8 >= 8 - j, ns, xs[n - 8:])], axis=0)


def _causal_conv(x, halo, w, b):
    k = w.shape[0]
    y = b + w[k - 1:k, :] * x
    for j in range(1, k):
        y = y + w[k - 1 - j:k - j, :] * _shift_down(x, halo, j)
    return y


def _causal_conv_bwd(dy, nxt, x, w):
    k = w.shape[0]
    dx = w[k - 1:k, :] * dy
    dw = [_colsum(dy * x)]
    for j in range(1, k):
        dyj = _shift_up(dy, nxt, j)
        dx = dx + w[k - 1 - j:k - j, :] * dyj
        dw.append(_colsum(dyj * x))
    return dx, jnp.concatenate(dw[::-1], axis=0)


MM_TILE_PREF = 1408
MM_VMEM_BUDGET = 40 * 1024 * 1024


def _mm_tiles(m, n, k, out_bytes):
    tm, tn = _tile(m, MM_TILE_PREF), _tile(n, MM_TILE_PREF)
    need = lambda tm, tn: 2 * (2 * k * (tm + tn) + out_bytes * tm * tn)
    while need(tm, tn) > MM_VMEM_BUDGET:
        if tn >= tm and tn > LANES:
            tn = _tile(n, tn - LANES)
        else:
            tm = _tile(m, tm - LANES)
    return tm, tn


def _mm(a, b, dims, name, acc=None, out_dtype=F32, after=None):
    a_list, b_list = (list(a), list(b)) if isinstance(a, (list, tuple)) else ([a], [b])
    k_axis, m_axis = (0, 1) if dims == "tn" else (1, 0)
    m, ks = a_list[0].shape[m_axis], [x.shape[k_axis] for x in a_list]
    n = b_list[0].shape[0] if dims == "nt" else b_list[0].shape[1]
    tm, tn = _mm_tiles(m, n, sum(ks), 4 * (2 if acc is not None else 1))
    a_specs = [pl.BlockSpec((k, tm), lambda j, i: (0, i)) if dims == "tn" else pl.BlockSpec((tm, k), lambda j, i: (i, 0))
               for k in ks]
    b_specs = [pl.BlockSpec((tn, k), lambda j, i: (j, 0)) if dims == "nt" else pl.BlockSpec((k, tn), lambda j, i: (0, j))
               for k in ks]
    o_spec = pl.BlockSpec((tm, tn), lambda j, i: (i, j))
    dot = {"nn": _dot, "nt": _dot_nt, "tn": _dot_tn}[dims]
    n_pairs = len(ks)

    def body(*refs):
        rest = refs[2 * n_pairs:]
        r = dot(refs[0][...], refs[n_pairs][...])
        for i in range(1, n_pairs):
            r = r + dot(refs[i][...], refs[n_pairs + i][...])
        if acc is not None:
            r = r + rest[0][...]
        rest[-1][...] = r.astype(out_dtype)

    ins, specs = a_list + b_list, a_specs + b_specs
    if acc is not None:
        ins.append(acc)
        specs.append(o_spec)
    if after is not None:
        ins.append(after)
        specs.append(pl.BlockSpec(memory_space=pl.ANY))
    return pl.pallas_call(
        body, name=name, grid=(n // tn, m // tm), in_specs=specs, out_specs=o_spec,
        out_shape=jax.ShapeDtypeStruct((m, n), out_dtype), compiler_params=_params(2),
    )(*ins)


def _mm_rows(a, b, dims, name, fn, rows=(), fulls=(), row_outs=(), acc_outs=(), after=None, summed=True):
    a_list, b_list = (list(a), list(b)) if isinstance(a, (list, tuple)) else ([a], [b])
    m, ks = a_list[0].shape[0], [x.shape[1] for x in a_list]
    n, k = (b_list[0].shape[0] if dims == "nt" else b_list[0].shape[1]), sum(ks)
    per_row = 2 * k + 8 * n + sum(4 * r.shape[1] for r in rows) + sum(c * jnp.dtype(d).itemsize for c, d in row_outs)
    tm = _tile(m, 1024)
    while 2 * tm * per_row + 4 * k * n > MM_VMEM_BUDGET:
        tm = _tile(m, tm - LANES)
    dot = _dot_nt if dims == "nt" else _dot
    n_pairs = len(ks)
    n_in = 2 * n_pairs + len(rows) + len(fulls) + (after is not None)

    def body(*refs):
        ins, outs = refs[:n_in], refs[n_in:]
        row_refs, acc_refs = outs[:len(row_outs)], outs[len(row_outs):]

        @pl.when(pl.program_id(0) == 0)
        def _():
            for r in acc_refs:
                r[...] = jnp.zeros_like(r)

        products = [dot(ins[i][...], ins[n_pairs + i][...]) for i in range(n_pairs)]
        result = functools.reduce(lambda p, q: p + q, products) if summed else products
        new_rows, incs = fn(result, *[r[...] for r in ins[2 * n_pairs:2 * n_pairs + len(rows) + len(fulls)]])
        for r, val in zip(row_refs, new_rows):
            r[...] = val.astype(r.dtype)
        for r, inc in zip(acc_refs, incs):
            r[...] += inc

    extra, extra_specs = ([after], [pl.BlockSpec(memory_space=pl.ANY)]) if after is not None else ([], [])
    return pl.pallas_call(
        body, name=name, grid=(m // tm,),
        in_specs=[_rows(tm, k_i) for k_i in ks] + [_full(x.shape) for x in b_list]
        + [_rows(tm, r.shape[1]) for r in rows] + [_full(f.shape) for f in fulls] + extra_specs,
        out_specs=[_rows(tm, c) for c, _ in row_outs] + [_full(s) for s in acc_outs],
        out_shape=[jax.ShapeDtypeStruct((m, c), d) for c, d in row_outs]
        + [jax.ShapeDtypeStruct(s, F32) for s in acc_outs],
        compiler_params=_params(1),
    )(*a_list, *b_list, *rows, *fulls, *extra)


def _residual_norm(o, x, w):
    h = x + o
    return (h, _rms(h, w)), ()


def _norm_backward(dn, h, dres, w):
    _, vjp = jax.vjp(_rms, h, w)
    dh, dw = vjp(dn)
    dh = dh + dres
    return (dh, dh), (dw,)


def _loss_and_grad(dn, h1, target, w):
    yf, vjp = jax.vjp(_rms, h1 + dn, w)
    err = yf - target
    loss = 0.5 * jnp.sum(jnp.mean(err * err, axis=-1, keepdims=True))
    dh, dw = vjp(err * (1.0 / err.shape[-1]))
    return (dh, dh), (jnp.full((8, LANES), loss, F32), dw)


def _wgrad(a, d, name, after=None):
    return _mm(a, d, "tn", name, out_dtype=BF16, after=after)


def _norm_fwd(x, w, name, after=None, tm=512):
    t, d = x.shape

    def body(x_ref, w_ref, *rest):
        rest[-1][...] = _rms(x_ref[...], w_ref[...]).astype(BF16)

    extra, extra_specs = ([after], [_full(after.shape)]) if after is not None else ([], [])
    return pl.pallas_call(
        body, name=name, grid=(t // tm,), in_specs=[_rows(tm, d), _full((1, d))] + extra_specs,
        out_specs=_rows(tm, d), out_shape=jax.ShapeDtypeStruct((t, d), BF16), compiler_params=_params(1),
    )(x, w, *extra)


def _conv_a_fwd(xbc, cw, cb, tm=256):
    t, c = xbc.shape

    def body(x_ref, h_ref, w_ref, b_ref, o_ref, y_ref):
        halo = jnp.where(pl.program_id(0) > 0, h_ref[...].astype(F32)[8:], 0.0)
        y = _causal_conv(x_ref[...].astype(F32), halo, w_ref[...], b_ref[...])
        y_ref[...] = y.astype(BF16)
        o_ref[...] = _silu(y)

    return pl.pallas_call(
        body, name="conv_a_fwd", grid=(t // tm,),
        in_specs=[_rows(tm, c), _halo(tm, c, rows=16), _full(cw.shape), _full((1, c))],
        out_specs=[_rows(tm, c)] * 2,
        out_shape=[jax.ShapeDtypeStruct((t, c), F32), jax.ShapeDtypeStruct((t, c), BF16)], compiler_params=_params(1),
    )(xbc, xbc, cw, cb)


def _ssd_common(dtr, dtb, alog, e_t):
    row = lax.broadcasted_iota(jnp.int32, (CHUNK, CHUNK), 0)
    col = lax.broadcasted_iota(jnp.int32, (CHUNK, CHUNK), 1)
    causal = row >= col
    dt = _softplus(dtr + dtb)
    a = -jnp.exp(alog)
    acum = jnp.dot(causal.astype(F32), dt * a, precision=HIGHEST, preferred_element_type=F32)
    spread = lambda v: _dot(v.astype(BF16), e_t)
    elast = jnp.broadcast_to(jnp.exp(acum[CHUNK - 1:CHUNK, :]), (8, LANES))
    return dict(dt=dt, a=a, acum=acum, acum_t=acum.T, causal=causal, row=row, col=col, lane_lo=col < SSD_HEAD_DIM,
                dt_x=_dot_split(dt, e_t), ecol_x=spread(jnp.exp(acum)), elast_x=_dot_split(elast, e_t)[0:1],
                dsr_x=spread(jnp.exp(acum[CHUNK - 1:CHUNK, :] - acum)))


def _head_decay(c, h, transposed=False):
    d = c["acum"][:, h:h + 1] - c["acum_t"][h:h + 1, :]
    if transposed:
        return jnp.exp(jnp.where(c["row"] <= c["col"], -d, -jnp.inf))
    return jnp.exp(jnp.where(c["causal"], d, -jnp.inf))


def _ssd_fwd(xc, dtr, z, dtb, alog, dsk, nw, e_t):
    t = xc.shape[0]
    nc = t // CHUNK

    def body(xs_ref, b_ref, c_ref, dtr_ref, z_ref, dtb_ref, alog_ref, dsk_ref, nw_ref, et_ref,
             y_ref, ya_ref, sp_ref, s_scr):
        @pl.when(pl.program_id(0) == 0)
        def _():
            s_scr[...] = jnp.zeros_like(s_scr)

        c = _ssd_common(dtr_ref[...], dtb_ref[...], alog_ref[...], et_ref[...])
        lane_lo = c["lane_lo"]
        dsk = dsk_ref[...]
        for g in range(SSD_GROUPS):
            gs = slice(g * SSD_STATE, (g + 1) * SSD_STATE)
            bg_t, cg = b_ref[:, gs].T.astype(BF16), c_ref[:, gs].astype(BF16)
            cb = _dot(cg, bg_t)
            for pp in range(PAIRS_PER_GROUP):
                j = g * PAIRS_PER_GROUP + pp
                ps = slice(j * LANES, (j + 1) * LANES)
                x = xs_ref[:, ps]
                ecol, dsr = c["ecol_x"][:, ps], c["dsr_x"][:, ps]
                xdt = x * c["dt_x"][:, ps]
                xb = xdt.astype(BF16)
                zero = jnp.zeros_like(xb)
                yd = (_dot((cb * _head_decay(c, 2 * j)).astype(BF16), jnp.where(lane_lo, xb, zero))
                      + _dot((cb * _head_decay(c, 2 * j + 1)).astype(BF16), jnp.where(lane_lo, zero, xb)))
                sp = s_scr[j]
                yo = ecol * _dot(cg, sp.astype(BF16))
                st = _dot(bg_t, (xdt * dsr).astype(BF16))
                sp_ref[0, j] = sp
                s_scr[j] = c["elast_x"][:, ps] * sp + st
                dskp = jnp.where(lane_lo[0:1], dsk[:, 2 * j:2 * j + 1], dsk[:, 2 * j + 1:2 * j + 2])
                y_ref[:, ps] = yd + yo + dskp * x
        ya_ref[...] = _rms(y_ref[...] * _silu(z_ref[...].astype(F32)), nw_ref[...]).astype(BF16)

    ck = lambda n, col=0: pl.BlockSpec((CHUNK, n), lambda c: (c, col))
    return pl.pallas_call(
        body, name="ssd_fwd", grid=(nc,),
        in_specs=[ck(SSD_INNER), ck(SSD_BC, SSD_INNER // SSD_BC), ck(SSD_BC, SSD_INNER // SSD_BC + 1), ck(DT_PAD),
                  ck(SSD_INNER), _full((1, DT_PAD)), _full((1, DT_PAD)), _full((1, DT_PAD)),
                  _full((1, SSD_INNER)), _full(e_t.shape)],
        out_specs=[ck(SSD_INNER), ck(SSD_INNER),
                   pl.BlockSpec((1, N_PAIRS, SSD_STATE, LANES), lambda c: (c, 0, 0, 0))],
        out_shape=[jax.ShapeDtypeStruct((t, SSD_INNER), F32), jax.ShapeDtypeStruct((t, SSD_INNER), BF16),
                   jax.ShapeDtypeStruct((nc, N_PAIRS, SSD_STATE, LANES), F32)],
        scratch_shapes=[pltpu.VMEM((N_PAIRS, SSD_STATE, LANES), F32)], compiler_params=_params(1),
    )(xc, xc, xc, dtr, z, dtb, alog, dsk, nw, e_t)


def _ssd_bwd(dya, y, z, xc, dtr, sprev, dtb, alog, dsk, nw, e_heads, e_t):
    t = xc.shape[0]
    nc = t // CHUNK

    def body(dya_ref, y_ref, z_ref, xs_ref, b_ref, c_ref, dtr_ref, sp_ref, dtb_ref, alog_ref, dsk_ref, nw_ref, e_ref,
             et_ref, dz_ref, dxs_ref, db_ref, dc_ref, ddtr_ref, dnw_ref, ddtb_ref, dalog_ref, ddsk_ref, ds_scr):
        @pl.when(pl.program_id(0) == 0)
        def _():
            ds_scr[...] = jnp.zeros_like(ds_scr)
            for r in (dnw_ref, ddtb_ref, dalog_ref, ddsk_ref):
                r[...] = jnp.zeros_like(r)

        y = y_ref[...]
        _, gate_vjp = jax.vjp(lambda y_, z_, w_: _rms(y_ * _silu(z_), w_), y, z_ref[...].astype(F32), nw_ref[...])
        dy, dz, dnw = gate_vjp(dya_ref[...])
        dz_ref[...] = dz.astype(BF16)
        dnw_ref[...] += dnw

        dtr = dtr_ref[...]
        c = _ssd_common(dtr, dtb_ref[...], alog_ref[...], et_ref[...])
        dt, a, lane_lo, row, col = c["dt"], c["a"], c["lane_lo"], c["row"], c["col"]
        dsk = dsk_ref[...]
        p_a, p_dt, v_last = [], [], []
        da_cols = jnp.zeros((CHUNK, CHUNK), F32)
        da_rows = jnp.zeros((CHUNK, CHUNK), F32)
        for g in range(SSD_GROUPS):
            gs = slice(g * SSD_STATE, (g + 1) * SSD_STATE)
            bg, cg = b_ref[:, gs].astype(BF16), c_ref[:, gs].astype(BF16)
            bg_t, cg_t = b_ref[:, gs].T.astype(BF16), c_ref[:, gs].T.astype(BF16)
            cb, cb_t = _dot(cg, bg_t), _dot(bg, cg_t)
            dcb = jnp.zeros((CHUNK, CHUNK), F32)
            dbg = jnp.zeros((CHUNK, SSD_STATE), F32)
            dcg = jnp.zeros((CHUNK, SSD_STATE), F32)
            for pp in range(PAIRS_PER_GROUP):
                j = g * PAIRS_PER_GROUP + pp
                ps = slice(j * LANES, (j + 1) * LANES)
                x = xs_ref[:, ps]
                dtp, ecol, dsr = c["dt_x"][:, ps], c["ecol_x"][:, ps], c["dsr_x"][:, ps]
                elast = c["elast_x"][:, ps]
                xdt = x * dtp
                xb = xdt.astype(BF16)
                dskp = jnp.where(lane_lo[0:1], dsk[:, 2 * j:2 * j + 1], dsk[:, 2 * j + 1:2 * j + 2])
                dyp = dy[:, ps]
                dyb = dyp.astype(BF16)
                sp, dsn = sp_ref[0, j], ds_scr[j]
                spb, dsnb = sp.astype(BF16), dsn.astype(BF16)
                y_off = ecol * _dot(cg, spb)
                dw = (dyp * ecol).astype(BF16)
                dcg = dcg + _dot_nt(dw, spb)
                dsp = _dot(cg_t, dw) + elast * dsn
                xd = xdt * dsr
                zd = _dot(bg, dsnb) * dsr
                dbg = dbg + _dot_nt(xd.astype(BF16), dsnb)
                dxdt = zd
                zero = jnp.zeros_like(xb)
                for h, lm in ((2 * j, lane_lo), (2 * j + 1, jnp.logical_not(lane_lo))):
                    le = _head_decay(c, h)
                    dm = _dot_nt(jnp.where(lm, dyb, zero), jnp.where(lm, xb, zero))
                    dcb = dcb + dm * le
                    m = cb * le
                    m_t = (cb_t * _head_decay(c, h, transposed=True)).astype(BF16)
                    dxdt = dxdt + jnp.where(lm, _dot(m_t, dyb), 0.0)
                    q = dm * m
                    da_cols = da_cols + jnp.where(col == h, jnp.sum(q, axis=1, keepdims=True), 0.0)
                    da_rows = da_rows + jnp.where(row == h, _colsum(q), 0.0)
                ds_scr[j] = dsp
                dxs_ref[:, ps] = dxdt * dtp + dskp * dyp
                p_a.append(dyp * y_off - xdt * zd)
                p_dt.append(dxdt * x)
                v_last.append(_colsum(zd * xdt) + elast * _colsum(dsn * sp))
            dcbb = dcb.astype(BF16)
            db_ref[:, gs] = dbg + _dot_tn(dcbb, cg)
            dc_ref[:, gs] = dcg + _dot(dcbb, bg)
        e = e_ref[...]
        rows8 = jnp.concatenate([jnp.concatenate(v_last, axis=1), _colsum(dy * xs_ref[...]),
                                 jnp.zeros((6, SSD_INNER), F32)], axis=0)
        r8 = _dot_split(rows8, e)
        da = (_dot_split(jnp.concatenate(p_a, axis=1), e) + jnp.where(row == CHUNK - 1, r8[0:1], 0.0)
              + da_cols - da_rows.T)
        ddsk_ref[...] += r8[1:2]
        dadt = jnp.dot((row <= col).astype(F32), da, precision=HIGHEST, preferred_element_type=F32)
        ddt = dadt * a + _dot_split(jnp.concatenate(p_dt, axis=1), e)
        dalog_ref[...] += _colsum(dadt * dt) * a
        ddtr = ddt * _sigmoid(dtr + dtb_ref[...])
        ddtr_ref[...] = ddtr
        ddtb_ref[...] += _colsum(ddtr)

    ck = lambda n, col=0: pl.BlockSpec((CHUNK, n), lambda c: (nc - 1 - c, col))
    acc = lambda n: _full((1, n))
    return pl.pallas_call(
        body, name="ssd_bwd", grid=(nc,),
        in_specs=[ck(SSD_INNER), ck(SSD_INNER), ck(SSD_INNER), ck(SSD_INNER), ck(SSD_BC, SSD_INNER // SSD_BC),
                  ck(SSD_BC, SSD_INNER // SSD_BC + 1), ck(DT_PAD),
                  pl.BlockSpec((1, N_PAIRS, SSD_STATE, LANES), lambda c: (nc - 1 - c, 0, 0, 0)),
                  acc(DT_PAD), acc(DT_PAD), acc(DT_PAD), acc(SSD_INNER), _full((SSD_INNER, LANES)),
                  _full((LANES, SSD_INNER))],
        out_specs=[ck(SSD_INNER), ck(SSD_INNER), ck(SSD_BC), ck(SSD_BC), ck(DT_PAD),
                   acc(SSD_INNER), acc(DT_PAD), acc(DT_PAD), acc(DT_PAD)],
        out_shape=[jax.ShapeDtypeStruct((t, SSD_INNER), BF16), jax.ShapeDtypeStruct((t, SSD_INNER), F32),
                   jax.ShapeDtypeStruct((t, SSD_BC), F32), jax.ShapeDtypeStruct((t, SSD_BC), F32),
                   jax.ShapeDtypeStruct((t, DT_PAD), F32), jax.ShapeDtypeStruct((1, SSD_INNER), F32),
                   jax.ShapeDtypeStruct((1, DT_PAD), F32), jax.ShapeDtypeStruct((1, DT_PAD), F32),
                   jax.ShapeDtypeStruct((1, DT_PAD), F32)],
        scratch_shapes=[pltpu.VMEM((N_PAIRS, SSD_STATE, LANES), F32)], compiler_params=_params(1),
    )(dya, y, z, xc, xc, xc, dtr, sprev, dtb, alog, dsk, nw, e_heads, e_t)


def _sgu_act(uv, uvb, lnw, lnb):
    a = _gelu(uv + uvb)
    return a[:, :SGU_WIDTH], _layer_norm(a[:, SGU_WIDTH:], lnw, lnb)


def _sgu_weights(ws_ref):
    row = lax.broadcasted_iota(jnp.int32, (CHUNK, CHUNK), 0)
    col = lax.broadcasted_iota(jnp.int32, (CHUNK, CHUNK), 1)
    return [jnp.where(row >= col, ws_ref[g], 0.0).astype(BF16) for g in range(SGU_GROUPS)], row >= col


def _sgu_fwd(uv, uvb, lnw, lnb, ws, bs_t):
    t = uv.shape[0]

    def body(uv_ref, uvb_ref, lnw_ref, lnb_ref, ws_ref, bs_ref, o_ref):
        u, vn = _sgu_act(uv_ref[...].astype(F32), uvb_ref[...], lnw_ref[...], lnb_ref[...])
        wc, _ = _sgu_weights(ws_ref)
        bs = bs_ref[...]
        for g in range(SGU_GROUPS):
            gs = slice(g * LANES, (g + 1) * LANES)
            mixed = _dot(wc[g], vn[:, gs].astype(BF16)) + bs[:, g:g + 1]
            o_ref[:, gs] = (u[:, gs] * mixed).astype(BF16)

    return pl.pallas_call(
        body, name="sgu_fwd", grid=(t // CHUNK,),
        in_specs=[_rows(CHUNK, 2 * SGU_WIDTH), _full((1, 2 * SGU_WIDTH)), _full((1, SGU_WIDTH)), _full((1, SGU_WIDTH)),
                  _full(ws.shape), _full(bs_t.shape)],
        out_specs=_rows(CHUNK, SGU_WIDTH), out_shape=jax.ShapeDtypeStruct((t, SGU_WIDTH), BF16),
        compiler_params=_params(1),
    )(uv, uvb, lnw, lnb, ws, bs_t)


def _sgu_bwd(dyb, uv, uvb, lnw, lnb, ws, bs_t, e_groups):
    t = uv.shape[0]

    def body(dyb_ref, uv_ref, uvb_ref, lnw_ref, lnb_ref, ws_ref, bs_ref, e_ref,
             duv_ref, duvb_ref, dlnw_ref, dlnb_ref, dws_ref, dbs_ref):
        @pl.when(pl.program_id(0) == 0)
        def _():
            for r in (duvb_ref, dlnw_ref, dlnb_ref, dws_ref, dbs_ref):
                r[...] = jnp.zeros_like(r)

        (u, vn), act_vjp = jax.vjp(_sgu_act, uv_ref[...].astype(F32), uvb_ref[...], lnw_ref[...], lnb_ref[...])
        wc, causal = _sgu_weights(ws_ref)
        bs = bs_ref[...]
        dyb = dyb_ref[...]
        du, dvn, dmix = [], [], []
        for g in range(SGU_GROUPS):
            gs = slice(g * LANES, (g + 1) * LANES)
            vb = vn[:, gs].astype(BF16)
            mixed = _dot(wc[g], vb) + bs[:, g:g + 1]
            dm = dyb[:, gs] * u[:, gs]
            dmb = dm.astype(BF16)
            du.append(dyb[:, gs] * mixed)
            dvn.append(_dot_tn(wc[g], dmb))
            dws_ref[g] += jnp.where(causal, _dot_nt(dmb, vb), 0.0)
            dmix.append(dm)
        dbs_ref[...] += _dot_split(jnp.concatenate(dmix, axis=1), e_ref[...])
        duv, duvb, dlnw, dlnb = act_vjp((jnp.concatenate(du, axis=1), jnp.concatenate(dvn, axis=1)))
        duv_ref[...] = duv.astype(BF16)
        duvb_ref[...] += duvb
        dlnw_ref[...] += dlnw
        dlnb_ref[...] += dlnb

    return pl.pallas_call(
        body, name="sgu_bwd", grid=(t // CHUNK,),
        in_specs=[_rows(CHUNK, SGU_WIDTH), _rows(CHUNK, 2 * SGU_WIDTH), _full((1, 2 * SGU_WIDTH)),
                  _full((1, SGU_WIDTH)), _full((1, SGU_WIDTH)), _full(ws.shape), _full(bs_t.shape),
                  _full(e_groups.shape)],
        out_specs=[_rows(CHUNK, 2 * SGU_WIDTH), _full((1, 2 * SGU_WIDTH)), _full((1, SGU_WIDTH)),
                   _full((1, SGU_WIDTH)), _full(ws.shape), _full(bs_t.shape)],
        out_shape=[jax.ShapeDtypeStruct((t, 2 * SGU_WIDTH), BF16), jax.ShapeDtypeStruct((1, 2 * SGU_WIDTH), F32),
                   jax.ShapeDtypeStruct((1, SGU_WIDTH), F32), jax.ShapeDtypeStruct((1, SGU_WIDTH), F32),
                   jax.ShapeDtypeStruct(ws.shape, F32), jax.ShapeDtypeStruct(bs_t.shape, F32)],
        compiler_params=_params(1),
    )(dyb, uv, uvb, lnw, lnb, ws, bs_t, e_groups)


def _merge(gates, pa, pb, bg):
    s = _sigmoid(gates + bg)
    return s[:, :D_MODEL] * pa + s[:, D_MODEL:] * pb


def _branches_merge(branches, gates, bg):
    pa, pb = branches
    return (pa, pb, _merge(gates.astype(F32), pa, pb, bg)), ()


def _merge_backward(dmix, gates, pa, pb, bg):
    _, vjp = jax.vjp(_merge, gates.astype(F32), pa.astype(F32), pb.astype(F32), bg)
    dg, dpa, dpb, dbg = vjp(dmix)
    return (dg, dpa, dpb), (dbg,)


def _conv_f_fwd(up, cw, cb, tm=256):
    t, c = up.shape

    def body(x_ref, h_ref, w_ref, b_ref, o_ref, y_ref):
        halo = jnp.where(pl.program_id(0) > 0, h_ref[...].astype(F32)[8:], 0.0)
        y = _causal_conv(x_ref[...].astype(F32), halo, w_ref[...], b_ref[...])
        y_ref[...] = y.astype(BF16)
        o_ref[...] = (_silu(y[:, :D_FF]) * y[:, D_FF:]).astype(BF16)

    return pl.pallas_call(
        body, name="conv_f_fwd", grid=(t // tm,),
        in_specs=[_rows(tm, c), _halo(tm, c, rows=16), _full(cw.shape), _full((1, c))],
        out_specs=[_rows(tm, D_FF), _rows(tm, c)],
        out_shape=[jax.ShapeDtypeStruct((t, D_FF), BF16), jax.ShapeDtypeStruct((t, c), BF16)],
        compiler_params=_params(1),
    )(up, up, cw, cb)


def _conv_f_bwd(dact, y, up, cw, tm=128):
    t, c = up.shape
    nt = t // tm

    def body(d_ref, y_ref, x_ref, w_ref, dx_ref, dw_ref, db_ref, nxt_scr):
        @pl.when(pl.program_id(0) == 0)
        def _():
            nxt_scr[...] = jnp.zeros_like(nxt_scr)
            dw_ref[...] = jnp.zeros_like(dw_ref)
            db_ref[...] = jnp.zeros_like(db_ref)

        a, v = y_ref[:, :D_FF].astype(F32), y_ref[:, D_FF:].astype(F32)
        d = d_ref[...].astype(F32)
        silu_a, dsilu_a = _silu_and_grad(a)
        dy = jnp.concatenate([d * v * dsilu_a, d * silu_a], axis=1)
        dx, dw = _causal_conv_bwd(dy, nxt_scr[...], x_ref[...].astype(F32), w_ref[...])
        dx_ref[...] = dx.astype(BF16)
        nxt_scr[...] = dy[:8]
        dw_ref[...] += dw
        db_ref[...] += _colsum(dy)

    return pl.pallas_call(
        body, name="conv_f_bwd", grid=(nt,),
        in_specs=[_rows(tm, D_FF, nt, True), _rows(tm, c, nt, True), _rows(tm, c, nt, True), _full(cw.shape)],
        out_specs=[_rows(tm, c, nt, True), _full(cw.shape), _full((1, c))],
        out_shape=[jax.ShapeDtypeStruct((t, c), BF16), jax.ShapeDtypeStruct(cw.shape, F32),
                   jax.ShapeDtypeStruct((1, c), F32)],
        scratch_shapes=[pltpu.VMEM((8, c), F32)], compiler_params=_params(1),
    )(dact, y, up, cw)


def _conv_a_bwd(dxs, db, dc, y, xbc, cw, tm=256):
    t, c = xbc.shape
    nt = t // tm

    def body(dxs_ref, db_ref, dc_ref, y_ref, x_ref, w_ref, dx_ref, dw_ref, dbias_ref, nxt_scr):
        @pl.when(pl.program_id(0) == 0)
        def _():
            nxt_scr[...] = jnp.zeros_like(nxt_scr)
            dw_ref[...] = jnp.zeros_like(dw_ref)
            dbias_ref[...] = jnp.zeros_like(dbias_ref)

        dy = jnp.concatenate([dxs_ref[...], db_ref[...], dc_ref[...]], axis=1) * _dsilu(y_ref[...].astype(F32))
        dx, dw = _causal_conv_bwd(dy, nxt_scr[...], x_ref[...].astype(F32), w_ref[...])
        dx_ref[...] = dx.astype(BF16)
        nxt_scr[...] = dy[:8]
        dw_ref[...] += dw
        dbias_ref[...] += _colsum(dy)

    return pl.pallas_call(
        body, name="conv_a_bwd", grid=(nt,),
        in_specs=[_rows(tm, SSD_INNER, nt, True), _rows(tm, SSD_BC, nt, True), _rows(tm, SSD_BC, nt, True),
                  _rows(tm, c, nt, True), _rows(tm, c, nt, True), _full(cw.shape)],
        out_specs=[_rows(tm, c, nt, True), _full(cw.shape), _full((1, c))],
        out_shape=[jax.ShapeDtypeStruct((t, c), BF16), jax.ShapeDtypeStruct(cw.shape, F32),
                   jax.ShapeDtypeStruct((1, c), F32)],
        scratch_shapes=[pltpu.VMEM((8, c), F32)], compiler_params=_params(1),
    )(dxs, db, dc, y, xbc, cw)


def _pad_lanes(v, n=DT_PAD):
    return jnp.pad(v, ((0, 0), (0, n - v.shape[1])))


def _local_step(x, target, w, p, after=None, late_weights=None, on_grad=None, on_small=None):
    dtb, alog, dsk = _pad_lanes(p["dt_bias"]), _pad_lanes(p["a_log"]), _pad_lanes(p["d_skip"])
    bs_t = _pad_lanes(p["b_spatial"].T)
    e_heads = (jnp.arange(SSD_INNER)[:, None] // SSD_HEAD_DIM == jnp.arange(LANES)[None, :]).astype(BF16)
    e_heads_t = (jnp.arange(LANES)[:, None] == jnp.arange(SSD_INNER)[None, :] // SSD_HEAD_DIM).astype(BF16)
    e_groups = (jnp.arange(SGU_WIDTH)[:, None] // LANES == jnp.arange(LANES)[None, :]).astype(BF16)

    n1 = _norm_fwd(x, p["norm1_w"], "norm1_fwd", after=after)
    z = _mm(n1, w["z"], "nt", "proj_z", out_dtype=BF16)
    xbc = _mm(n1, w["xbc"], "nt", "proj_xbc", out_dtype=BF16)
    dtr = _mm(n1, w["dt"], "nt", "proj_dt")
    uv = _mm(n1, w["uv"], "nt", "proj_uv", out_dtype=BF16)
    gates = _mm(n1, w["gates"], "nt", "proj_gates", out_dtype=BF16)
    xc, conv_a_out = _conv_a_fwd(xbc, w["conv_a"], p["conv_a_b"])
    y, ya, sprev = _ssd_fwd(xc, dtr, z, dtb, alog, dsk, p["ssd_norm_w"], e_heads_t)
    yb = _sgu_fwd(uv, p["uv_b"], p["v_ln_w"], p["v_ln_b"], p["w_spatial"], bs_t)
    if late_weights is not None:
        w = {**w, **late_weights(ya, yb)}
    narrow = (D_MODEL, BF16)
    pa, pb, mix = _mm_rows(
        [ya, yb], [w["branch_a"], w["branch_b"]], "nn", "branches", _branches_merge, rows=[gates],
        fulls=[p["b_gate"]], row_outs=[narrow] * 3, summed=False)
    wide = [(D_MODEL, F32), (D_MODEL, BF16)]
    h1, n2 = _mm_rows(mix, w["out"], "nn", "out_proj", _residual_norm, rows=[x], fulls=[p["norm2_w"]], row_outs=wide)
    up = _mm(n2, w["up"], "nt", "up_proj", out_dtype=BF16)
    act, conv_f_out = _conv_f_fwd(up, w["conv_f"], p["conv_f_b"])
    dh2, dh2b, loss, g_final = _mm_rows(
        act, w["down"], "nn", "down_proj", _loss_and_grad, rows=[h1, target], fulls=[p["final_norm_w"]],
        row_outs=wide, acc_outs=[(8, LANES), (1, D_MODEL)])

    on_grad = on_grad or (lambda name, grads: None)
    g = {"final_norm_w": g_final}
    g["down"] = _wgrad(act, dh2b, "down_wgrad")
    tok = on_grad("w_down", g)
    dact = _mm(dh2b, w["down"], "nt", "down_dgrad", out_dtype=BF16, after=tok)
    dup, g["conv_f"], g["conv_f_b"] = _conv_f_bwd(dact, conv_f_out, up, w["conv_f"])
    g["up"] = _wgrad(dup, n2, "up_wgrad")
    tok = on_grad("w_up", g)
    dh1, dh1b, g["norm2_w"] = _mm_rows(
        dup, w["up"], "nn", "up_dgrad", _norm_backward, rows=[h1, dh2], fulls=[p["norm2_w"]], row_outs=wide,
        acc_outs=[(1, D_MODEL)], after=tok)
    g["out"] = _wgrad(mix, dh1b, "out_wgrad")
    tok = on_grad("w_out", g)
    dgates, dpa, dpb, g["b_gate"] = _mm_rows(
        dh1b, w["out"], "nt", "out_dgrad", _merge_backward, rows=[gates, pa, pb], fulls=[p["b_gate"]],
        row_outs=[(2 * D_MODEL, BF16), (D_MODEL, BF16), (D_MODEL, BF16)], acc_outs=[(1, 2 * D_MODEL)], after=tok)
    g["branch_a"] = _wgrad(ya, dpa, "branch_a_wgrad")
    g["branch_b"] = _wgrad(yb, dpb, "branch_b_wgrad")
    tok = on_grad("w_branch", g)
    dya, dyb = _mm_rows(
        [dpa, dpb], [w["branch_a"], w["branch_b"]], "nt", "branches_dgrad", lambda products: (tuple(products), ()),
        row_outs=[(SSD_INNER, F32), (SGU_WIDTH, F32)], after=tok, summed=False)
    duv, g["uv_b"], g["v_ln_w"], g["v_ln_b"], g["w_spatial"], dbs_t = _sgu_bwd(
        dyb, uv, p["uv_b"], p["v_ln_w"], p["v_ln_b"], p["w_spatial"], bs_t, e_groups)
    g["b_spatial"] = dbs_t[:, :SGU_GROUPS].T
    dz, dxs, db, dc, ddtr, g["ssd_norm_w"], ddtb, dalog, ddsk = _ssd_bwd(
        dya, y, z, xc, dtr, sprev, dtb, alog, dsk, p["ssd_norm_w"], e_heads, e_heads_t)
    g["dt_bias"], g["a_log"], g["d_skip"] = ddtb, dalog, ddsk
    dxbc, g["conv_a"], g["conv_a_b"] = _conv_a_bwd(dxs, db, dc, conv_a_out, xbc, w["conv_a"])
    tok = on_small(g, loss) if on_small else None
    ddtrb = ddtr.astype(BF16)
    for name, d in (("z", dz), ("xbc", dxbc), ("dt", ddtrb), ("uv", duv), ("gates", dgates)):
        g[name] = _wgrad(d, n1, name + "_wgrad", after=tok)
    tok = on_grad("w_in", g)
    dn1 = _mm([dz, dxbc], [w["z"], w["xbc"]], "nn", "ssd_dgrad", after=tok)
    gx, g["norm1_w"] = _mm_rows(
        [duv, dgates, ddtrb], [w["uv"], w["gates"], w["dt"]], "nn", "in_dgrad",
        lambda r, so_far, h, dres, w_: tuple(t[:1] for t in _norm_backward(r + so_far, h, dres, w_)),
        rows=[dn1, x, dh1], fulls=[p["norm1_w"]], row_outs=wide[:1], acc_outs=[(1, D_MODEL)])
    return loss, gx, g


def _place():
    return lax.axis_index("x"), lax.axis_index("y"), lax.axis_index("c")


def _other_chips(x, y):
    return [(1 - x, y), (x, 1 - y), (1 - x, 1 - y)]


def _all_gather(shards, name):
    n = len(shards)

    def body(*refs):
        ins, outs = refs[:n], refs[n:2 * n]
        send_sems, recv_sems, local_sems = refs[2 * n:]
        x, y, c = _place()
        me, sibling = (x, y, c), (x, y, 1 - c)
        chips = _other_chips(x, y)

        def copy(a, k, block, to, src=None):
            slot = outs[a].at[4 * block[0] + 2 * block[1] + block[2]]
            return pltpu.make_async_remote_copy(
                src_ref=slot if src is None else src, dst_ref=slot, send_sem=send_sems.at[7 * a + k],
                recv_sem=recv_sems.at[7 * a + k], device_id=to, device_id_type=MESH)

        started = []
        for a in range(n):
            mine = pltpu.make_async_copy(ins[a], outs[a].at[4 * x + 2 * y + c], local_sems.at[a])
            mine.start()
            started.append(mine)
        sends = []
        for a in range(n):
            sends.append(copy(a, 0, me, sibling, src=ins[a]))
            sends += [copy(a, 1 + j, me, (*chip, c), src=ins[a]) for j, chip in enumerate(chips)]
        for cp in sends:
            cp.start()
        for a in range(n):
            for j, chip in enumerate(chips):
                copy(a, 1 + j, (*chip, c), me).wait_recv()
                fwd = copy(a, 4 + j, (*chip, c), sibling)
                fwd.start()
                sends.append(fwd)
        for a in range(n):
            copy(a, 0, sibling, me).wait_recv()
            for j, chip in enumerate(chips):
                copy(a, 4 + j, (*chip, 1 - c), me).wait_recv()
        for cp in sends:
            cp.wait_send()
        for mine in started:
            mine.wait()

    any_spec = pl.BlockSpec(memory_space=pl.ANY)
    return pl.pallas_call(
        body, name=name, in_specs=[any_spec] * n, out_specs=[any_spec] * n,
        out_shape=[jax.ShapeDtypeStruct((N_DEV, *s.shape), s.dtype) for s in shards],
        scratch_shapes=[pltpu.SemaphoreType.DMA((7 * n,)), pltpu.SemaphoreType.DMA((7 * n,)),
                        pltpu.SemaphoreType.DMA((n,))],
    )(*shards)


HBM_SPEC = pl.BlockSpec(memory_space=pltpu.HBM)
SEM_SPEC = pl.BlockSpec(memory_space=pltpu.SEMAPHORE)
ANY_SPEC = pl.BlockSpec(memory_space=pl.ANY)
DATAFLOW = pltpu.SideEffectType.DATAFLOW_SIDE_EFFECTING
N_PEERS = N_DEV - 1


def _peers(x, y, c):
    out = []
    for r in range(1, N_DEV):
        fx, fy, fc = r >> 2 & 1, r >> 1 & 1, r & 1
        out.append(((1 - x) if fx else x, (1 - y) if fy else y, (1 - c) if fc else c))
    return out


def _gather_copies(srcs, lands, send_sems, recv_sems, sending, scatter=False):
    x, y, c = _place()
    copies = []
    for a, (src, land) in enumerate(zip(srcs, lands)):
        for j, (px, py, pc) in enumerate(_peers(x, y, c)):
            mine, theirs = 4 * x + 2 * y + c, 4 * px + 2 * py + pc
            block = src.at[theirs if sending else 0] if scatter else src
            copies.append(pltpu.make_async_remote_copy(
                src_ref=block, dst_ref=land.at[mine if sending else theirs], send_sem=send_sems.at[N_PEERS * a + j],
                recv_sem=recv_sems.at[N_PEERS * a + j], device_id=(px, py, pc), device_id_type=MESH))
    return copies


def _gather_start(shards, after, name, scatter=False):
    n = len(shards)
    after = [] if after is None else [after]

    def body(*refs):
        srcs, lands = refs[:n], refs[n:2 * n]
        send_sems, recv_sems = refs[2 * n + len(after):2 * n + len(after) + 2]
        token = refs[-1]
        for cp in _gather_copies(srcs, lands, send_sems, recv_sems, sending=True, scatter=scatter):
            cp.start()
        token[...] = jnp.zeros_like(token)

    lands = [lax.empty(s.shape if scatter else (N_DEV, *s.shape), s.dtype) for s in shards]
    hbm = lambda a: pltpu.with_memory_space_constraint(a, pltpu.HBM)
    out = pl.pallas_call(
        body, name=name,
        out_shape=(pltpu.SemaphoreType.DMA((N_PEERS * n,)), pltpu.SemaphoreType.DMA((N_PEERS * n,)),
                   *[pltpu.HBM(a.shape, a.dtype) for a in (*shards, *lands)], jax.ShapeDtypeStruct((8, LANES), F32)),
        in_specs=[HBM_SPEC] * (2 * n) + [ANY_SPEC] * len(after),
        out_specs=(SEM_SPEC, SEM_SPEC, *[HBM_SPEC] * (2 * n), pl.BlockSpec(memory_space=pltpu.VMEM)),
        input_output_aliases={i: 2 + i for i in range(2 * n)},
        compiler_params=pltpu.CompilerParams(has_side_effects=DATAFLOW),
    )(*[hbm(a) for a in (*shards, *lands)], *after)
    return out[0], out[1], out[2:2 + n], out[2 + n:2 + 2 * n], out[-1]


def _gather_wait(send_sems, recv_sems, shards, lands, after, name, scatter=False):
    n = len(shards)
    after = tuple(after)

    def body(*refs):
        srcs, lands_ = refs[:n], refs[n:2 * n]
        send, recv = refs[2 * n:2 * n + 2]
        for cp in _gather_copies(srcs, lands_, send, recv, sending=False, scatter=scatter):
            cp.wait_send()
            cp.wait_recv()

    out = pl.pallas_call(
        body, name=name, out_shape=tuple(pltpu.HBM(a.shape, a.dtype) for a in (*shards, *lands)),
        in_specs=[HBM_SPEC] * (2 * n) + [SEM_SPEC, SEM_SPEC] + [ANY_SPEC] * len(after),
        out_specs=tuple([HBM_SPEC] * (2 * n)), input_output_aliases={i: i for i in range(2 * n)},
        compiler_params=pltpu.CompilerParams(has_side_effects=DATAFLOW),
    )(*shards, *lands, send_sems, recv_sems, *after)
    return out[:n], out[n:]


def _chip_copies(src, land, send_sems, recv_sems):
    x, y, c = _place()
    return [pltpu.make_async_remote_copy(
        src_ref=src.at[2 * cx + cy], dst_ref=land.at[j], send_sem=send_sems.at[j], recv_sem=recv_sems.at[j],
        device_id=(cx, cy, c), device_id_type=MESH) for j, (cx, cy) in enumerate(_other_chips(x, y))]


def _chips_start(q, name):
    def body(q_ref, land_ref, send_sems, recv_sems, q_thru, land_thru, token):
        for cp in _chip_copies(q_ref, land_ref, send_sems, recv_sems):
            cp.start()
        token[...] = jnp.zeros_like(token)

    land = lax.empty((3, *q.shape[1:]), q.dtype)
    return pl.pallas_call(
        body, name=name,
        out_shape=(pltpu.SemaphoreType.DMA((3,)), pltpu.SemaphoreType.DMA((3,)), pltpu.HBM(q.shape, q.dtype),
                   pltpu.HBM(land.shape, land.dtype), jax.ShapeDtypeStruct((8, LANES), F32)),
        in_specs=[HBM_SPEC, HBM_SPEC],
        out_specs=(SEM_SPEC, SEM_SPEC, HBM_SPEC, HBM_SPEC, pl.BlockSpec(memory_space=pltpu.VMEM)),
        input_output_aliases={0: 2, 1: 3}, compiler_params=pltpu.CompilerParams(has_side_effects=DATAFLOW),
    )(pltpu.with_memory_space_constraint(q, pltpu.HBM), pltpu.with_memory_space_constraint(land, pltpu.HBM))


def _chips_wait(send_sems, recv_sems, q, land, after, name):
    def body(q_ref, land_ref, send, recv, after_ref, q_out, land_out):
        for cp in _chip_copies(q_ref, land_ref, send, recv):
            cp.wait_send()
            cp.wait_recv()

    return pl.pallas_call(
        body, name=name, out_shape=(pltpu.HBM(q.shape, q.dtype), pltpu.HBM(land.shape, land.dtype)),
        in_specs=[HBM_SPEC, HBM_SPEC, SEM_SPEC, SEM_SPEC, ANY_SPEC], out_specs=(HBM_SPEC, HBM_SPEC),
        input_output_aliases={0: 0, 1: 1}, compiler_params=pltpu.CompilerParams(has_side_effects=DATAFLOW),
    )(q, land, send_sems, recv_sems, after)[1]


def _exchange_cores(part, name):
    def body(in_ref, out_ref, send_sems, recv_sems):
        x, y, c = _place()
        copies = [pltpu.make_async_remote_copy(
            src_ref=in_ref.at[2 * k + (1 - c)], dst_ref=out_ref.at[k], send_sem=send_sems.at[k],
            recv_sem=recv_sems.at[k], device_id=(x, y, 1 - c), device_id_type=MESH) for k in range(4)]
        for cp in copies:
            cp.start()
        for cp in copies:
            cp.wait()

    return pl.pallas_call(
        body, name=name, in_specs=[ANY_SPEC], out_specs=ANY_SPEC,
        out_shape=jax.ShapeDtypeStruct((4, *part.shape[1:]), part.dtype),
        scratch_shapes=[pltpu.SemaphoreType.DMA((4,)), pltpu.SemaphoreType.DMA((4,))],
    )(part)


def _chip_sum(part, got, place, name, tr=256):
    _, r, c = part.shape
    tr, tc = _tile2d(r, c, tr)

    def body(place_ref, p_ref, g_ref, q_ref, own_ref):
        s = p_ref[0].astype(F32) + g_ref[0].astype(F32)
        q_ref[0] = s.astype(BF16)

        @pl.when(pl.program_id(2) == place_ref[1])
        def _():
            own_ref[...] = s

    grid_spec = pltpu.PrefetchScalarGridSpec(
        num_scalar_prefetch=1, grid=(r // tr, c // tc, 4),
        in_specs=[pl.BlockSpec((1, tr, tc), lambda i, j, k, pr: (2 * k + pr[0], i, j)),
                  pl.BlockSpec((1, tr, tc), lambda i, j, k, pr: (k, i, j))],
        out_specs=[pl.BlockSpec((1, tr, tc), lambda i, j, k, pr: (k, i, j)),
                   pl.BlockSpec((tr, tc), lambda i, j, k, pr: (i, j))])
    return pl.pallas_call(
        body, name=name, grid_spec=grid_spec,
        out_shape=[jax.ShapeDtypeStruct((4, r, c), BF16), jax.ShapeDtypeStruct((r, c), F32)],
        compiler_params=_params(3),
    )(place, part, got)


def _sum_adamw(own, got, w, m, v, name):
    r, c = own.shape
    tc = 4 * LANES

    def body(own_ref, got_ref, w_ref, m_ref, v_ref, g_ref, d_ref, nm_ref, nv_ref):
        g = own_ref[...]
        for j in range(3):
            g = g + got_ref[j].astype(F32)
        two_d = lambda ref: ref[...].reshape(r, tc)
        delta, nm, nv = _adamw(two_d(w_ref), g, two_d(m_ref), two_d(v_ref))
        for ref, val in ((g_ref, g), (d_ref, delta), (nm_ref, nm), (nv_ref, nv)):
            ref[...] = val.reshape(ref.shape)

    wblk = pl.BlockSpec((r, 1, tc), lambda j: (0, 0, j))
    return pl.pallas_call(
        body, name=name, grid=(c // tc,),
        in_specs=[pl.BlockSpec((r, tc), lambda j: (0, j)), pl.BlockSpec((3, r, tc), lambda j: (0, 0, j)),
                  wblk, wblk, wblk],
        out_specs=[wblk] * 4, out_shape=[jax.ShapeDtypeStruct(w.shape, F32)] * 4, compiler_params=_params(1),
    )(own, got, w, m, v)


def _adamw(w, g, m, v):
    m = ADAM_B1 * m + (1.0 - ADAM_B1) * g
    v = ADAM_B2 * v + (1.0 - ADAM_B2) * jnp.square(g)
    m_hat = m / (1.0 - ADAM_B1 ** ADAM_STEP)
    v_hat = v / (1.0 - ADAM_B2 ** ADAM_STEP)
    return -ADAM_LR * (m_hat / (jnp.sqrt(v_hat) + ADAM_EPS) + ADAM_WD * w), m, v


def _sum8_adamw(part, got, place, w, m, v, name, tr=256):
    r, c = w.shape
    tr, tc = _tile2d(r, c, tr)
    blk = pl.BlockSpec((tr, tc), lambda i, j, pr: (i, j))

    def body(place_ref, own_ref, got_ref, w_ref, m_ref, v_ref, g_ref, d_ref, nm_ref, nv_ref):
        dev = 2 * place_ref[1] + place_ref[0]
        g = jnp.zeros((tr, tc), F32)
        for d in range(N_DEV):
            g = g + jnp.where(dev == d, own_ref[0], got_ref[d]).astype(F32)
        g_ref[...] = g
        d_ref[...], nm_ref[...], nv_ref[...] = _adamw(w_ref[...], g, m_ref[...], v_ref[...])

    grid_spec = pltpu.PrefetchScalarGridSpec(
        num_scalar_prefetch=1, grid=(r // tr, c // tc),
        in_specs=[pl.BlockSpec((1, tr, tc), lambda i, j, pr: (2 * pr[1] + pr[0], i, j)),
                  pl.BlockSpec((N_DEV, tr, tc), lambda i, j, pr: (0, i, j)), blk, blk, blk],
        out_specs=[blk] * 4)
    return pl.pallas_call(
        body, name=name, grid_spec=grid_spec, out_shape=[jax.ShapeDtypeStruct(w.shape, F32)] * 4,
        compiler_params=_params(2),
    )(place, part, got, w, m, v)


VECTORS = ["norm1_w", "b_gate", "conv_a_b", "dt_bias", "a_log", "d_skip", "ssd_norm_w", "uv_b", "v_ln_w", "v_ln_b",
           "norm2_w", "conv_f_b", "final_norm_w"]
SMALL_ORDER = VECTORS + ["w_spatial", "b_spatial", "conv_a_w", "conv_f_w"]


ROW_VECTORS = VECTORS[1:]


def _small_adamw(gathered, w, m, v):
    sizes = {n: w[n].shape[1] for n in ROW_VECTORS}
    offs, off = {}, 0
    for n in ROW_VECTORS:
        offs[n] = off
        off += -(-sizes[n] // LANES) * LANES
    loss_off = off
    k = len(SMALL_ORDER)
    n_g = len(gathered)

    def body(*refs):
        row_ref, ws_ref, bs_ref, ca_ref, cf_ref, n1_ref = refs[:n_g]
        w_refs, m_refs, v_refs = (dict(zip(SMALL_ORDER, refs[n_g + i * k:n_g + (i + 1) * k])) for i in range(3))
        outs = refs[n_g + 3 * k:]
        x, y, c = _place()
        dev = 4 * x + 2 * y + c

        def total(ref):
            s = ref[0]
            for d in range(1, N_DEV):
                s = s + ref[d]
            return s

        row = total(row_ref)
        grads = {n: row[:, offs[n]:offs[n] + sizes[n]] for n in ROW_VECTORS}
        grads["norm1_w"], grads["w_spatial"], grads["b_spatial"] = total(n1_ref), total(ws_ref), total(bs_ref)
        for n, ref in (("conv_a_w", ca_ref), ("conv_f_w", cf_ref)):
            whole, cols = total(ref), w_refs[n].shape[1]
            mine = whole[:, :cols]
            for d in range(1, N_DEV):
                mine = jnp.where(dev == d, whole[:, d * cols:(d + 1) * cols], mine)
            grads[n] = mine
        for i, n in enumerate(SMALL_ORDER):
            outs[4 * i][...] = grads[n]
            outs[4 * i + 1][...], outs[4 * i + 2][...], outs[4 * i + 3][...] = _adamw(
                w_refs[n][...], grads[n], m_refs[n][...], v_refs[n][...])
        outs[4 * k][...] = row[:, loss_off:loss_off + LANES]

    out = pl.pallas_call(
        body, name="adamw_small",
        out_shape=[jax.ShapeDtypeStruct(w[n].shape, F32) for n in SMALL_ORDER for _ in range(4)]
        + [jax.ShapeDtypeStruct((1, LANES), F32)],
        compiler_params=_params(0),
    )(*gathered, *[t[n] for t in (w, m, v) for n in SMALL_ORDER])
    return [dict(zip(SMALL_ORDER, out[j:4 * k:4])) for j in range(4)] + [out[4 * k]]


SMALL = ["norm1_w", "b_gate", "conv_a_b", "dt_bias", "a_log", "d_skip", "ssd_norm_w", "uv_b", "v_ln_w", "v_ln_b",
         "w_spatial", "b_spatial", "norm2_w", "conv_f_b", "final_norm_w"]
BIG = ["w_in", "w_branch", "w_out", "w_up", "w_down"]
TRANSPOSED = ("w_in", "w_up")
WEIGHTS = ["norm1_w", "w_in", "b_gate", "conv_a_w", "conv_a_b", "dt_bias", "a_log", "d_skip", "ssd_norm_w", "uv_b",
           "v_ln_w", "v_ln_b", "w_spatial", "b_spatial", "w_branch", "w_out", "norm2_w", "w_up", "conv_f_w",
           "conv_f_b", "w_down", "final_norm_w"]
IN_SPLITS = [("z", 0, 2048), ("xbc", 2048, 5120), ("dt", 5120, 5152), ("uv", 5152, 7200), ("gates", 7200, 9248)]


def _columns_from_devices(a):
    return a.transpose(1, 0, 2).reshape(a.shape[1], -1)


def kernel(x, norm1_w, w_in, b_gate, conv_a_w, conv_a_b, dt_bias, a_log, d_skip, ssd_norm_w, uv_b, v_ln_w, v_ln_b, w_spatial, b_spatial, w_branch, w_out, norm2_w, w_up, conv_f_w, conv_f_b, w_down, final_norm_w, loss_target, m_norm1_w, m_w_in, m_b_gate, m_conv_a_w, m_conv_a_b, m_dt_bias, m_a_log, m_d_skip, m_ssd_norm_w, m_uv_b, m_v_ln_w, m_v_ln_b, m_w_spatial, m_b_spatial, m_w_branch, m_w_out, m_norm2_w, m_w_up, m_conv_f_w, m_conv_f_b, m_w_down, m_final_norm_w, v_norm1_w, v_w_in, v_b_gate, v_conv_a_w, v_conv_a_b, v_dt_bias, v_a_log, v_d_skip, v_ssd_norm_w, v_uv_b, v_v_ln_w, v_v_ln_b, v_w_spatial, v_b_spatial, v_w_branch, v_w_out, v_norm2_w, v_w_up, v_conv_f_w, v_conv_f_b, v_w_down, v_final_norm_w):
    args = dict(locals())
    wts = {n: args[n] for n in WEIGHTS}
    mom = {n: args["m_" + n] for n in WEIGHTS}
    var = {n: args["v_" + n] for n in WEIGHTS}
    cx, cy, cc = _place()
    dev = 4 * cx + 2 * cy + cc
    place = jnp.stack([cc, 2 * cx + cy]).astype(jnp.int32)

    def shard2d(n, a):
        return a[0].T if n in TRANSPOSED else a[0]

    def unshard(n, b):
        return (b.T if n in TRANSPOSED else b)[None]

    g_in, g_conv_a, g_conv_f = _all_gather(
        [shard2d("w_in", w_in).astype(BF16), conv_a_w[0], conv_f_w[0]], "gather_w_in")
    late = [shard2d(n, wts[n]).astype(BF16) for n in BIG[1:]]
    send_sems, recv_sems, late, lands, token = _gather_start(late, g_in, "gather_late_start")
    w_in_rows = g_in.reshape(-1, D_MODEL)
    w = {name: w_in_rows[lo:hi] for name, lo, hi in IN_SPLITS}
    w["dt"] = jnp.pad(w["dt"], ((0, DT_PAD - SSD_HEADS), (0, 0)))
    w["conv_a"] = _columns_from_devices(g_conv_a)
    w["conv_f"] = _columns_from_devices(g_conv_f)

    def late_weights(*after):
        mine, got = _gather_wait(send_sems, recv_sems, late, lands, after, "gather_late_wait")
        g_branch, g_out, g_up, g_down = [lax.dynamic_update_index_in_dim(land, own, dev, 0).reshape(-1, D_MODEL)
                                         for land, own in zip(got, mine)]
        return {"branch_a": g_branch[:SSD_INNER], "branch_b": g_branch[SSD_INNER:], "out": g_out, "up": g_up,
                "down": g_down}

    in_flight = {}

    def on_grad(n, g):
        part = {"w_in": lambda: jnp.concatenate([g[name][:hi - lo] for name, lo, hi in IN_SPLITS], axis=0),
                "w_branch": lambda: jnp.concatenate([g["branch_a"], g["branch_b"]], axis=0),
                "w_out": lambda: g["out"], "w_up": lambda: g["up"], "w_down": lambda: g["down"]}[n]()
        part = part.reshape(N_DEV, -1, D_MODEL)
        if n == "w_in":
            q, own = _chip_sum(part, _exchange_cores(part, "to_other_core_w_in"), place, "chip_sum_w_in")
            send, recv, q, land, tok = _chips_start(q, "to_other_chips_start_w_in")
            in_flight[n] = (own, send, recv, q, land)
            return tok
        send, recv, (part,), (land,), tok = _gather_start([part], None, f"to_owners_start_{n}", scatter=True)
        in_flight[n] = (part, send, recv, land)
        return tok

    p = {n: wts[n][0] if wts[n].ndim > 2 else wts[n].reshape(1, -1) for n in SMALL}
    small_flight = []

    def on_small(g, loss):
        arrays = [jnp.concatenate([g[n] for n in ROW_VECTORS] + [loss[:1]], axis=1), g["w_spatial"], g["b_spatial"],
                  g["conv_a"], g["conv_f"]]
        *flight, tok = _gather_start(arrays, g["conv_a"], "gather_small_start")
        small_flight.append(flight)
        return tok

    loss, gx, g = _local_step(x[0], loss_target[0], w, p, after=token, late_weights=late_weights, on_grad=on_grad,
                              on_small=on_small)
    *flight, _ = _gather_start([g["norm1_w"]], gx, "gather_norm1_start")
    small_flight.append(flight)

    grads, delta, new_m, new_v = {}, {}, {}, {}

    def big_adamw(n, after):
        if n == "w_in":
            own, send, recv, q, land = in_flight[n]
            got = _chips_wait(send, recv, q, land, after, "to_other_chips_wait_w_in")
            out = _sum_adamw(own, got, *[t[n].transpose(2, 0, 1) for t in (wts, mom, var)], "adamw_w_in")
            grads[n], delta[n], new_m[n], new_v[n] = [o.transpose(1, 2, 0) for o in out]
            return out[1]
        part, send, recv, land = in_flight[n]
        (part,), (got,) = _gather_wait(send, recv, [part], [land], [after], f"to_owners_wait_{n}", scatter=True)
        out = _sum8_adamw(part, got, place, *[shard2d(n, t[n]) for t in (wts, mom, var)], f"adamw_{n}")
        grads[n], delta[n], new_m[n], new_v[n] = [unshard(n, o) for o in out]
        return out[1]

    after = gx
    for n in ("w_down", "w_up", "w_out", "w_branch"):
        after = big_adamw(n, after)
    gathered = []
    for (send, recv, mine, land), name in zip(small_flight, ("gather_small_wait", "gather_norm1_wait")):
        mine, got = _gather_wait(send, recv, mine, land, [after], name)
        gathered += [lax.dynamic_update_index_in_dim(full, own, dev, 0) for full, own in zip(got, mine)]
    small = [{n: t[n][0] if t[n].ndim > 2 else t[n].reshape(1, -1) for n in SMALL_ORDER} for t in (wts, mom, var)]
    *outs, loss = _small_adamw(gathered, *small)
    for tgt, out in zip((grads, delta, new_m, new_v), outs):
        tgt.update({n: out[n].reshape(wts[n].shape) for n in SMALL_ORDER})
    big_adamw("w_in", loss)
    loss = loss[0, 0]

    return (loss, gx[None], *[grads[n] for n in WEIGHTS], *[delta[n] for n in WEIGHTS],
            *[new_m[n] for n in WEIGHTS], *[new_v[n] for n in WEIGHTS])
```

```python
import functools

import jax
import jax.numpy as jnp
from jax import lax
from jax.experimental import pallas as pl
from jax.experimental.pallas import tpu as pltpu

F32, BF16 = jnp.float32, jnp.bfloat16
HIGHEST = lax.Precision.HIGHEST

D_MODEL = 1024
SSD_INNER = 2048
SSD_HEAD_DIM = 64
SSD_HEADS = 32
SSD_GROUPS = 4
SSD_STATE = 128
SSD_BC = SSD_GROUPS * SSD_STATE
SSD_XBC = SSD_INNER + 2 * SSD_BC
SSD_CONV = 4
CHUNK = 128
N_PAIRS = SSD_HEADS // 2
PAIRS_PER_GROUP = N_PAIRS // SSD_GROUPS
SGU_WIDTH = 1024
SGU_GROUPS = 8
SGU_TILE = 256
D_FF = 2816
FFN_CONV = 3
NORM_EPS = 1e-6
LN_EPS = 1e-5
LANES = 128
DT_PAD = LANES

ADAM_LR, ADAM_B1, ADAM_B2, ADAM_EPS, ADAM_WD, ADAM_STEP = 0.001, 0.9, 0.999, 1e-08, 0.01, 10

N_DEV = 8
VMEM_LIMIT = 56 * 1024 * 1024
MESH = pl.DeviceIdType.MESH


def _params(n_grid, **kw):
    sem = dict(dimension_semantics=("arbitrary",) * n_grid) if n_grid else {}
    return pltpu.CompilerParams(vmem_limit_bytes=VMEM_LIMIT, **sem, **kw)


def _tile(n, pref):
    t = (min(pref, n) // LANES) * LANES
    while n % t:
        t -= LANES
    return t


def _row_tile(r, pref):
    for t in range(min(pref, r) // 16 * 16, 0, -16):
        if r % t == 0:
            return t
    return r


def _tile2d(r, c, rows):
    if r % 16 == 0:
        return _row_tile(r, rows), c
    return r, _tile(c, 2 * LANES)


def _rows(tm, n, nt=None, rev=False):
    if rev:
        return pl.BlockSpec((tm, n), lambda i: (nt - 1 - i, 0))
    return pl.BlockSpec((tm, n), lambda i: (i, 0))


def _halo(tm, n, rows=8):
    per = tm // rows
    return pl.BlockSpec((rows, n), lambda i: (jnp.maximum(i * per - 1, 0), 0))


def _full(shape):
    nd = len(shape)
    return pl.BlockSpec(shape, lambda *_: (0,) * nd)


def _rms(x, w, eps=NORM_EPS):
    return x * lax.rsqrt(jnp.mean(x * x, axis=-1, keepdims=True) + eps) * w


def _layer_norm(x, w, b):
    mu = jnp.mean(x, axis=-1, keepdims=True)
    var = jnp.mean(jnp.square(x - mu), axis=-1, keepdims=True)
    return (x - mu) * lax.rsqrt(var + LN_EPS) * w + b


def _sigmoid(x):
    return 1.0 / (1.0 + jnp.exp(-x))


def _silu(x):
    return x * _sigmoid(x)


def _dsilu(x):
    s = _sigmoid(x)
    return s * (1.0 + x * (1.0 - s))


def _silu_and_grad(x):
    s = _sigmoid(x)
    return x * s, s * (1.0 + x * (1.0 - s))


def _softplus(x):
    return jnp.maximum(x, 0.0) + jnp.log(1.0 + jnp.exp(-jnp.abs(x)))


def _gelu(x):
    return jax.nn.gelu(x)


def _dot(a, b):
    return jnp.dot(a, b, preferred_element_type=F32)


def _dot_nt(a, b):
    return lax.dot_general(a, b, (((1,), (1,)), ((), ())), preferred_element_type=F32)


def _dot_tn(a, b):
    return lax.dot_general(a, b, (((0,), (0,)), ((), ())), preferred_element_type=F32)


def _dot_split(p, e):
    hi = p.astype(BF16)
    lo = (p - hi.astype(F32)).astype(BF16)
    return _dot(hi, e) + _dot(lo, e)


def _colsum(x):
    return jnp.sum(x, axis=0, keepdims=True)


def _shift_down(x, halo, j):
    xs = pltpu.roll(x, j, 0)
    hs = pltpu.roll(halo, j, 0)
    r8 = lax.broadcasted_iota(jnp.int32, hs.shape, 0)
    return jnp.concatenate([jnp.where(r8 < j, hs, xs[:8]), xs[8:]], axis=0)


def _shift_up(x, nxt, j):
    n = x.shape[0]
    xs = pltpu.roll(x, n - j, 0)
    ns = pltpu.roll(nxt, 8 - j, 0)
    r8 = lax.broadcasted_iota(jnp.int32, ns.shape, 0)
    return jnp.concatenate([xs[:n - 8], jnp.where(r8 >= 8 - j, ns, xs[n - 8:])], axis=0)


def _causal_conv(x, halo, w, b):
    k = w.shape[0]
    y = b + w[k - 1:k, :] * x
    for j in range(1, k):
        y = y + w[k - 1 - j:k - j, :] * _shift_down(x, halo, j)
    return y


def _causal_conv_bwd(dy, nxt, x, w):
    k = w.shape[0]
    dx = w[k - 1:k, :] * dy
    dw = [_colsum(dy * x)]
    for j in range(1, k):
        dyj = _shift_up(dy, nxt, j)
        dx = dx + w[k - 1 - j:k - j, :] * dyj
        dw.append(_colsum(dyj * x))
    return dx, jnp.concatenate(dw[::-1], axis=0)


MM_TILE_PREF = 1408
MM_VMEM_BUDGET = 40 * 1024 * 1024


def _mm_tiles(m, n, k, out_bytes):
    tm, tn = _tile(m, MM_TILE_PREF), _tile(n, MM_TILE_PREF)
    need = lambda tm, tn: 2 * (2 * k * (tm + tn) + out_bytes * tm * tn)
    while need(tm, tn) > MM_VMEM_BUDGET:
        if tn >= tm and tn > LANES:
            tn = _tile(n, tn - LANES)
        else:
            tm = _tile(m, tm - LANES)
    return tm, tn


def _mm(a, b, dims, name, acc=None, out_dtype=F32, after=None):
    a_list, b_list = (list(a), list(b)) if isinstance(a, (list, tuple)) else ([a], [b])
    k_axis, m_axis = (0, 1) if dims == "tn" else (1, 0)
    m, ks = a_list[0].shape[m_axis], [x.shape[k_axis] for x in a_list]
    n = b_list[0].shape[0] if dims == "nt" else b_list[0].shape[1]
    tm, tn = _mm_tiles(m, n, sum(ks), 4 * (2 if acc is not None else 1))
    a_specs = [pl.BlockSpec((k, tm), lambda j, i: (0, i)) if dims == "tn" else pl.BlockSpec((tm, k), lambda j, i: (i, 0))
               for k in ks]
    b_specs = [pl.BlockSpec((tn, k), lambda j, i: (j, 0)) if dims == "nt" else pl.BlockSpec((k, tn), lambda j, i: (0, j))
               for k in ks]
    o_spec = pl.BlockSpec((tm, tn), lambda j, i: (i, j))
    dot = {"nn": _dot, "nt": _dot_nt, "tn": _dot_tn}[dims]
    n_pairs = len(ks)

    def body(*refs):
        rest = refs[2 * n_pairs:]
        r = dot(refs[0][...], refs[n_pairs][...])
        for i in range(1, n_pairs):
            r = r + dot(refs[i][...], refs[n_pairs + i][...])
        if acc is not None:
            r = r + rest[0][...]
        rest[-1][...] = r.astype(out_dtype)

    ins, specs = a_list + b_list, a_specs + b_specs
    if acc is not None:
        ins.append(acc)
        specs.append(o_spec)
    if after is not None:
        ins.append(after)
        specs.append(pl.BlockSpec(memory_space=pl.ANY))
    return pl.pallas_call(
        body, name=name, grid=(n // tn, m // tm), in_specs=specs, out_specs=o_spec,
        out_shape=jax.ShapeDtypeStruct((m, n), out_dtype), compiler_params=_params(2),
    )(*ins)


def _mm_rows(a, b, dims, name, fn, rows=(), fulls=(), row_outs=(), acc_outs=(), after=None, summed=True):
    a_list, b_list = (list(a), list(b)) if isinstance(a, (list, tuple)) else ([a], [b])
    m, ks = a_list[0].shape[0], [x.shape[1] for x in a_list]
    n, k = (b_list[0].shape[0] if dims == "nt" else b_list[0].shape[1]), sum(ks)
    per_row = 2 * k + 8 * n + sum(4 * r.shape[1] for r in rows) + sum(c * jnp.dtype(d).itemsize for c, d in row_outs)
    tm = _tile(m, 1024)
    while 2 * tm * per_row + 4 * k * n > MM_VMEM_BUDGET:
        tm = _tile(m, tm - LANES)
    dot = _dot_nt if dims == "nt" else _dot
    n_pairs = len(ks)
    n_in = 2 * n_pairs + len(rows) + len(fulls) + (after is not None)

    def body(*refs):
        ins, outs = refs[:n_in], refs[n_in:]
        row_refs, acc_refs = outs[:len(row_outs)], outs[len(row_outs):]

        @pl.when(pl.program_id(0) == 0)
        def _():
            for r in acc_refs:
                r[...] = jnp.zeros_like(r)

        products = [dot(ins[i][...], ins[n_pairs + i][...]) for i in range(n_pairs)]
        result = functools.reduce(lambda p, q: p + q, products) if summed else products
        new_rows, incs = fn(result, *[r[...] for r in ins[2 * n_pairs:2 * n_pairs + len(rows) + len(fulls)]])
        for r, val in zip(row_refs, new_rows):
            r[...] = val.astype(r.dtype)
        for r, inc in zip(acc_refs, incs):
            r[...] += inc

    extra, extra_specs = ([after], [pl.BlockSpec(memory_space=pl.ANY)]) if after is not None else ([], [])
    return pl.pallas_call(
        body, name=name, grid=(m // tm,),
        in_specs=[_rows(tm, k_i) for k_i in ks] + [_full(x.shape) for x in b_list]
        + [_rows(tm, r.shape[1]) for r in rows] + [_full(f.shape) for f in fulls] + extra_specs,
        out_specs=[_rows(tm, c) for c, _ in row_outs] + [_full(s) for s in acc_outs],
        out_shape=[jax.ShapeDtypeStruct((m, c), d) for c, d in row_outs]
        + [jax.ShapeDtypeStruct(s, F32) for s in acc_outs],
        compiler_params=_params(1),
    )(*a_list, *b_list, *rows, *fulls, *extra)


def _residual_norm(o, x, w):
    h = x + o
    return (h, _rms(h, w)), ()


def _norm_backward(dn, h, dres, w):
    _, vjp = jax.vjp(_rms, h, w)
    dh, dw = vjp(dn)
    dh = dh + dres
    return (dh, dh), (dw,)


def _loss_and_grad(dn, h1, target, w):
    yf, vjp = jax.vjp(_rms, h1 + dn, w)
    err = yf - target
    loss = 0.5 * jnp.sum(jnp.mean(err * err, axis=-1, keepdims=True))
    dh, dw = vjp(err * (1.0 / err.shape[-1]))
    return (dh, dh), (jnp.full((8, LANES), loss, F32), dw)


def _wgrad(a, d, name, after=None):
    return _mm(a, d, "tn", name, out_dtype=BF16, after=after)


def _norm_fwd(x, w, name, after=None, tm=512):
    t, d = x.shape

    def body(x_ref, w_ref, *rest):
        rest[-1][...] = _rms(x_ref[...], w_ref[...]).astype(BF16)

    extra, extra_specs = ([after], [_full(after.shape)]) if after is not None else ([], [])
    return pl.pallas_call(
        body, name=name, grid=(t // tm,), in_specs=[_rows(tm, d), _full((1, d))] + extra_specs,
        out_specs=_rows(tm, d), out_shape=jax.ShapeDtypeStruct((t, d), BF16), compiler_params=_params(1),
    )(x, w, *extra)


def _conv_a_fwd(xbc, cw, cb, tm=256):
    t, c = xbc.shape

    def body(x_ref, h_ref, w_ref, b_ref, o_ref, y_ref):
        halo = jnp.where(pl.program_id(0) > 0, h_ref[...].astype(F32)[8:], 0.0)
        y = _causal_conv(x_ref[...].astype(F32), halo, w_ref[...], b_ref[...])
        y_ref[...] = y.astype(BF16)
        o_ref[...] = _silu(y)

    return pl.pallas_call(
        body, name="conv_a_fwd", grid=(t // tm,),
        in_specs=[_rows(tm, c), _halo(tm, c, rows=16), _full(cw.shape), _full((1, c))],
        out_specs=[_rows(tm, c)] * 2,
        out_shape=[jax.ShapeDtypeStruct((t, c), F32), jax.ShapeDtypeStruct((t, c), BF16)], compiler_params=_params(1),
    )(xbc, xbc, cw, cb)


def _ssd_common(dtr, dtb, alog, e_t):
    row = lax.broadcasted_iota(jnp.int32, (CHUNK, CHUNK), 0)
    col = lax.broadcasted_iota(jnp.int32, (CHUNK, CHUNK), 1)
    causal = row >= col
    dt = _softplus(dtr + dtb)
    a = -jnp.exp(alog)
    acum = jnp.dot(causal.astype(F32), dt * a, precision=HIGHEST, preferred_element_type=F32)
    spread = lambda v: _dot(v.astype(BF16), e_t)
    elast = jnp.broadcast_to(jnp.exp(acum[CHUNK - 1:CHUNK, :]), (8, LANES))
    return dict(dt=dt, a=a, acum=acum, acum_t=acum.T, causal=causal, row=row, col=col, lane_lo=col < SSD_HEAD_DIM,
                dt_x=_dot_split(dt, e_t), ecol_x=spread(jnp.exp(acum)), elast_x=_dot_split(elast, e_t)[0:1],
                dsr_x=spread(jnp.exp(acum[CHUNK - 1:CHUNK, :] - acum)))


def _head_decay(c, h, transposed=False):
    d = c["acum"][:, h:h + 1] - c["acum_t"][h:h + 1, :]
    if transposed:
        return jnp.exp(jnp.where(c["row"] <= c["col"], -d, -jnp.inf))
    return jnp.exp(jnp.where(c["causal"], d, -jnp.inf))


def _ssd_fwd(xc, dtr, z, dtb, alog, dsk, nw, e_t):
    t = xc.shape[0]
    nc = t // CHUNK

    def body(xs_ref, b_ref, c_ref, dtr_ref, z_ref, dtb_ref, alog_ref, dsk_ref, nw_ref, et_ref,
             y_ref, ya_ref, sp_ref, s_scr):
        @pl.when(pl.program_id(0) == 0)
        def _():
            s_scr[...] = jnp.zeros_like(s_scr)

        c = _ssd_common(dtr_ref[...], dtb_ref[...], alog_ref[...], et_ref[...])
        lane_lo = c["lane_lo"]
        dsk = dsk_ref[...]
        for g in range(SSD_GROUPS):
            gs = slice(g * SSD_STATE, (g + 1) * SSD_STATE)
            bg_t, cg = b_ref[:, gs].T.astype(BF16), c_ref[:, gs].astype(BF16)
            cb = _dot(cg, bg_t)
            for pp in range(PAIRS_PER_GROUP):
                j = g * PAIRS_PER_GROUP + pp
                ps = slice(j * LANES, (j + 1) * LANES)
                x = xs_ref[:, ps]
                ecol, dsr = c["ecol_x"][:, ps], c["dsr_x"][:, ps]
                xdt = x * c["dt_x"][:, ps]
                xb = xdt.astype(BF16)
                zero = jnp.zeros_like(xb)
                yd = (_dot((cb * _head_decay(c, 2 * j)).astype(BF16), jnp.where(lane_lo, xb, zero))
                      + _dot((cb * _head_decay(c, 2 * j + 1)).astype(BF16), jnp.where(lane_lo, zero, xb)))
                sp = s_scr[j]
                yo = ecol * _dot(cg, sp.astype(BF16))
                st = _dot(bg_t, (xdt * dsr).astype(BF16))
                sp_ref[0, j] = sp
                s_scr[j] = c["elast_x"][:, ps] * sp + st
                dskp = jnp.where(lane_lo[0:1], dsk[:, 2 * j:2 * j + 1], dsk[:, 2 * j + 1:2 * j + 2])
                y_ref[:, ps] = yd + yo + dskp * x
        ya_ref[...] = _rms(y_ref[...] * _silu(z_ref[...].astype(F32)), nw_ref[...]).astype(BF16)

    ck = lambda n, col=0: pl.BlockSpec((CHUNK, n), lambda c: (c, col))
    return pl.pallas_call(
        body, name="ssd_fwd", grid=(nc,),
        in_specs=[ck(SSD_INNER), ck(SSD_BC, SSD_INNER // SSD_BC), ck(SSD_BC, SSD_INNER // SSD_BC + 1), ck(DT_PAD),
                  ck(SSD_INNER), _full((1, DT_PAD)), _full((1, DT_PAD)), _full((1, DT_PAD)),
                  _full((1, SSD_INNER)), _full(e_t.shape)],
        out_specs=[ck(SSD_INNER), ck(SSD_INNER),
                   pl.BlockSpec((1, N_PAIRS, SSD_STATE, LANES), lambda c: (c, 0, 0, 0))],
        out_shape=[jax.ShapeDtypeStruct((t, SSD_INNER), F32), jax.ShapeDtypeStruct((t, SSD_INNER), BF16),
                   jax.ShapeDtypeStruct((nc, N_PAIRS, SSD_STATE, LANES), F32)],
        scratch_shapes=[pltpu.VMEM((N_PAIRS, SSD_STATE, LANES), F32)], compiler_params=_params(1),
    )(xc, xc, xc, dtr, z, dtb, alog, dsk, nw, e_t)


def _ssd_bwd(dya, y, z, xc, dtr, sprev, dtb, alog, dsk, nw, e_heads, e_t):
    t = xc.shape[0]
    nc = t // CHUNK

    def body(dya_ref, y_ref, z_ref, xs_ref, b_ref, c_ref, dtr_ref, sp_ref, dtb_ref, alog_ref, dsk_ref, nw_ref, e_ref,
             et_ref, dz_ref, dxs_ref, db_ref, dc_ref, ddtr_ref, dnw_ref, ddtb_ref, dalog_ref, ddsk_ref, ds_scr):
        @pl.when(pl.program_id(0) == 0)
        def _():
            ds_scr[...] = jnp.zeros_like(ds_scr)
            for r in (dnw_ref, ddtb_ref, dalog_ref, ddsk_ref):
                r[...] = jnp.zeros_like(r)

        y = y_ref[...]
        _, gate_vjp = jax.vjp(lambda y_, z_, w_: _rms(y_ * _silu(z_), w_), y, z_ref[...].astype(F32), nw_ref[...])
        dy, dz, dnw = gate_vjp(dya_ref[...])
        dz_ref[...] = dz.astype(BF16)
        dnw_ref[...] += dnw

        dtr = dtr_ref[...]
        c = _ssd_common(dtr, dtb_ref[...], alog_ref[...], et_ref[...])
        dt, a, lane_lo, row, col = c["dt"], c["a"], c["lane_lo"], c["row"], c["col"]
        dsk = dsk_ref[...]
        p_a, p_dt, v_last = [], [], []
        da_cols = jnp.zeros((CHUNK, CHUNK), F32)
        da_rows = jnp.zeros((CHUNK, CHUNK), F32)
        for g in range(SSD_GROUPS):
            gs = slice(g * SSD_STATE, (g + 1) * SSD_STATE)
            bg, cg = b_ref[:, gs].astype(BF16), c_ref[:, gs].astype(BF16)
            bg_t, cg_t = b_ref[:, gs].T.astype(BF16), c_ref[:, gs].T.astype(BF16)
            cb, cb_t = _dot(cg, bg_t), _dot(bg, cg_t)
            dcb = jnp.zeros((CHUNK, CHUNK), F32)
            dbg = jnp.zeros((CHUNK, SSD_STATE), F32)
            dcg = jnp.zeros((CHUNK, SSD_STATE), F32)
            for pp in range(PAIRS_PER_GROUP):
                j = g * PAIRS_PER_GROUP + pp
                ps = slice(j * LANES, (j + 1) * LANES)
                x = xs_ref[:, ps]
                dtp, ecol, dsr = c["dt_x"][:, ps], c["ecol_x"][:, ps], c["dsr_x"][:, ps]
                elast = c["elast_x"][:, ps]
                xdt = x * dtp
                xb = xdt.astype(BF16)
                dskp = jnp.where(lane_lo[0:1], dsk[:, 2 * j:2 * j + 1], dsk[:, 2 * j + 1:2 * j + 2])
                dyp = dy[:, ps]
                dyb = dyp.astype(BF16)
                sp, dsn = sp_ref[0, j], ds_scr[j]
                spb, dsnb = sp.astype(BF16), dsn.astype(BF16)
                y_off = ecol * _dot(cg, spb)
                dw = (dyp * ecol).astype(BF16)
                dcg = dcg + _dot_nt(dw, spb)
                dsp = _dot(cg_t, dw) + elast * dsn
                xd = xdt * dsr
                zd = _dot(bg, dsnb) * dsr
                dbg = dbg + _dot_nt(xd.astype(BF16), dsnb)
                dxdt = zd
                zero = jnp.zeros_like(xb)
                for h, lm in ((2 * j, lane_lo), (2 * j + 1, jnp.logical_not(lane_lo))):
                    le = _head_decay(c, h)
                    dm = _dot_nt(jnp.where(lm, dyb, zero), jnp.where(lm, xb, zero))
                    dcb = dcb + dm * le
                    m = cb * le
                    m_t = (cb_t * _head_decay(c, h, transposed=True)).astype(BF16)
                    dxdt = dxdt + jnp.where(lm, _dot(m_t, dyb), 0.0)
                    q = dm * m
                    da_cols = da_cols + jnp.where(col == h, jnp.sum(q, axis=1, keepdims=True), 0.0)
                    da_rows = da_rows + jnp.where(row == h, _colsum(q), 0.0)
                ds_scr[j] = dsp
                dxs_ref[:, ps] = dxdt * dtp + dskp * dyp
                p_a.append(dyp * y_off - xdt * zd)
                p_dt.append(dxdt * x)
                v_last.append(_colsum(zd * xdt) + elast * _colsum(dsn * sp))
            dcbb = dcb.astype(BF16)
            db_ref[:, gs] = dbg + _dot_tn(dcbb, cg)
            dc_ref[:, gs] = dcg + _dot(dcbb, bg)
        e = e_ref[...]
        rows8 = jnp.concatenate([jnp.concatenate(v_last, axis=1), _colsum(dy * xs_ref[...]),
                                 jnp.zeros((6, SSD_INNER), F32)], axis=0)
        r8 = _dot_split(rows8, e)
        da = (_dot_split(jnp.concatenate(p_a, axis=1), e) + jnp.where(row == CHUNK - 1, r8[0:1], 0.0)
              + da_cols - da_rows.T)
        ddsk_ref[...] += r8[1:2]
        dadt = jnp.dot((row <= col).astype(F32), da, precision=HIGHEST, preferred_element_type=F32)
        ddt = dadt * a + _dot_split(jnp.concatenate(p_dt, axis=1), e)
        dalog_ref[...] += _colsum(dadt * dt) * a
        ddtr = ddt * _sigmoid(dtr + dtb_ref[...])
        ddtr_ref[...] = ddtr
        ddtb_ref[...] += _colsum(ddtr)

    ck = lambda n, col=0: pl.BlockSpec((CHUNK, n), lambda c: (nc - 1 - c, col))
    acc = lambda n: _full((1, n))
    return pl.pallas_call(
        body, name="ssd_bwd", grid=(nc,),
        in_specs=[ck(SSD_INNER), ck(SSD_INNER), ck(SSD_INNER), ck(SSD_INNER), ck(SSD_BC, SSD_INNER // SSD_BC),
                  ck(SSD_BC, SSD_INNER // SSD_BC + 1), ck(DT_PAD),
                  pl.BlockSpec((1, N_PAIRS, SSD_STATE, LANES), lambda c: (nc - 1 - c, 0, 0, 0)),
                  acc(DT_PAD), acc(DT_PAD), acc(DT_PAD), acc(SSD_INNER), _full((SSD_INNER, LANES)),
                  _full((LANES, SSD_INNER))],
        out_specs=[ck(SSD_INNER), ck(SSD_INNER), ck(SSD_BC), ck(SSD_BC), ck(DT_PAD),
                   acc(SSD_INNER), acc(DT_PAD), acc(DT_PAD), acc(DT_PAD)],
        out_shape=[jax.ShapeDtypeStruct((t, SSD_INNER), BF16), jax.ShapeDtypeStruct((t, SSD_INNER), F32),
                   jax.ShapeDtypeStruct((t, SSD_BC), F32), jax.ShapeDtypeStruct((t, SSD_BC), F32),
                   jax.ShapeDtypeStruct((t, DT_PAD), F32), jax.ShapeDtypeStruct((1, SSD_INNER), F32),
                   jax.ShapeDtypeStruct((1, DT_PAD), F32), jax.ShapeDtypeStruct((1, DT_PAD), F32),
                   jax.ShapeDtypeStruct((1, DT_PAD), F32)],
        scratch_shapes=[pltpu.VMEM((N_PAIRS, SSD_STATE, LANES), F32)], compiler_params=_params(1),
    )(dya, y, z, xc, xc, xc, dtr, sprev, dtb, alog, dsk, nw, e_heads, e_t)


def _sgu_act(uv, uvb, lnw, lnb):
    a = _gelu(uv + uvb)
    return a[:, :SGU_WIDTH], _layer_norm(a[:, SGU_WIDTH:], lnw, lnb)


def _sgu_weights(ws_ref):
    row = lax.broadcasted_iota(jnp.int32, (CHUNK, CHUNK), 0)
    col = lax.broadcasted_iota(jnp.int32, (CHUNK, CHUNK), 1)
    return [jnp.where(row >= col, ws_ref[g], 0.0).astype(BF16) for g in range(SGU_GROUPS)], row >= col


def _sgu_fwd(uv, uvb, lnw, lnb, ws, bs_t):
    t = uv.shape[0]

    def body(uv_ref, uvb_ref, lnw_ref, lnb_ref, ws_ref, bs_ref, o_ref):
        u, vn = _sgu_act(uv_ref[...].astype(F32), uvb_ref[...], lnw_ref[...], lnb_ref[...])
        wc, _ = _sgu_weights(ws_ref)
        bs = bs_ref[...]
        for ck in range(SGU_TILE // CHUNK):
            rs = slice(ck * CHUNK, (ck + 1) * CHUNK)
            for g in range(SGU_GROUPS):
                gs = slice(g * LANES, (g + 1) * LANES)
                mixed = _dot(wc[g], vn[rs, gs].astype(BF16)) + bs[:, g:g + 1]
                o_ref[rs, gs] = (u[rs, gs] * mixed).astype(BF16)

    return pl.pallas_call(
        body, name="sgu_fwd", grid=(t // SGU_TILE,),
        in_specs=[_rows(SGU_TILE, 2 * SGU_WIDTH), _full((1, 2 * SGU_WIDTH)), _full((1, SGU_WIDTH)),
                  _full((1, SGU_WIDTH)), _full(ws.shape), _full(bs_t.shape)],
        out_specs=_rows(SGU_TILE, SGU_WIDTH), out_shape=jax.ShapeDtypeStruct((t, SGU_WIDTH), BF16),
        compiler_params=_params(1),
    )(uv, uvb, lnw, lnb, ws, bs_t)


def _sgu_bwd(dyb, uv, uvb, lnw, lnb, ws, bs_t, e_groups):
    t = uv.shape[0]

    def body(dyb_ref, uv_ref, uvb_ref, lnw_ref, lnb_ref, ws_ref, bs_ref, e_ref,
             duv_ref, duvb_ref, dlnw_ref, dlnb_ref, dws_ref, dbs_ref):
        @pl.when(pl.program_id(0) == 0)
        def _():
            for r in (duvb_ref, dlnw_ref, dlnb_ref, dws_ref, dbs_ref):
                r[...] = jnp.zeros_like(r)

        (u, vn), act_vjp = jax.vjp(_sgu_act, uv_ref[...].astype(F32), uvb_ref[...], lnw_ref[...], lnb_ref[...])
        wc, causal = _sgu_weights(ws_ref)
        bs = bs_ref[...]
        dyb = dyb_ref[...]
        du_rows, dvn_rows = [], []
        for ck in range(SGU_TILE // CHUNK):
            rs = slice(ck * CHUNK, (ck + 1) * CHUNK)
            du, dvn, dmix = [], [], []
            for g in range(SGU_GROUPS):
                gs = slice(g * LANES, (g + 1) * LANES)
                vb = vn[rs, gs].astype(BF16)
                mixed = _dot(wc[g], vb) + bs[:, g:g + 1]
                dm = dyb[rs, gs] * u[rs, gs]
                dmb = dm.astype(BF16)
                du.append(dyb[rs, gs] * mixed)
                dvn.append(_dot_tn(wc[g], dmb))
                dws_ref[g] += jnp.where(causal, _dot_nt(dmb, vb), 0.0)
                dmix.append(dm)
            dbs_ref[...] += _dot_split(jnp.concatenate(dmix, axis=1), e_ref[...])
            du_rows.append(jnp.concatenate(du, axis=1))
            dvn_rows.append(jnp.concatenate(dvn, axis=1))
        duv, duvb, dlnw, dlnb = act_vjp((jnp.concatenate(du_rows, axis=0), jnp.concatenate(dvn_rows, axis=0)))
        duv_ref[...] = duv.astype(BF16)
        duvb_ref[...] += duvb
        dlnw_ref[...] += dlnw
        dlnb_ref[...] += dlnb

    return pl.pallas_call(
        body, name="sgu_bwd", grid=(t // SGU_TILE,),
        in_specs=[_rows(SGU_TILE, SGU_WIDTH), _rows(SGU_TILE, 2 * SGU_WIDTH), _full((1, 2 * SGU_WIDTH)),
                  _full((1, SGU_WIDTH)), _full((1, SGU_WIDTH)), _full(ws.shape), _full(bs_t.shape),
                  _full(e_groups.shape)],
        out_specs=[_rows(SGU_TILE, 2 * SGU_WIDTH), _full((1, 2 * SGU_WIDTH)), _full((1, SGU_WIDTH)),
                   _full((1, SGU_WIDTH)), _full(ws.shape), _full(bs_t.shape)],
        out_shape=[jax.ShapeDtypeStruct((t, 2 * SGU_WIDTH), BF16), jax.ShapeDtypeStruct((1, 2 * SGU_WIDTH), F32),
                   jax.ShapeDtypeStruct((1, SGU_WIDTH), F32), jax.ShapeDtypeStruct((1, SGU_WIDTH), F32),
                   jax.ShapeDtypeStruct(ws.shape, F32), jax.ShapeDtypeStruct(bs_t.shape, F32)],
        compiler_params=_params(1),
    )(dyb, uv, uvb, lnw, lnb, ws, bs_t, e_groups)


def _merge(gates, pa, pb, bg):
    s = _sigmoid(gates + bg)
    return s[:, :D_MODEL] * pa + s[:, D_MODEL:] * pb


def _branches_merge(branches, gates, bg):
    pa, pb = branches
    return (pa, pb, _merge(gates.astype(F32), pa, pb, bg)), ()


def _merge_backward(dmix, gates, pa, pb, bg):
    _, vjp = jax.vjp(_merge, gates.astype(F32), pa.astype(F32), pb.astype(F32), bg)
    dg, dpa, dpb, dbg = vjp(dmix)
    return (dg, dpa, dpb), (dbg,)


def _conv_f_fwd(up, cw, cb, tm=256):
    t, c = up.shape

    def body(x_ref, h_ref, w_ref, b_ref, o_ref, y_ref):
        halo = jnp.where(pl.program_id(0) > 0, h_ref[...].astype(F32)[8:], 0.0)
        y = _causal_conv(x_ref[...].astype(F32), halo, w_ref[...], b_ref[...])
        y_ref[...] = y.astype(BF16)
        o_ref[...] = (_silu(y[:, :D_FF]) * y[:, D_FF:]).astype(BF16)

    return pl.pallas_call(
        body, name="conv_f_fwd", grid=(t // tm,),
        in_specs=[_rows(tm, c), _halo(tm, c, rows=16), _full(cw.shape), _full((1, c))],
        out_specs=[_rows(tm, D_FF), _rows(tm, c)],
        out_shape=[jax.ShapeDtypeStruct((t, D_FF), BF16), jax.ShapeDtypeStruct((t, c), BF16)],
        compiler_params=_params(1),
    )(up, up, cw, cb)


def _conv_f_bwd(dact, y, up, cw, tm=128):
    t, c = up.shape
    nt = t // tm

    def body(d_ref, y_ref, x_ref, w_ref, dx_ref, dw_ref, db_ref, nxt_scr):
        @pl.when(pl.program_id(0) == 0)
        def _():
            nxt_scr[...] = jnp.zeros_like(nxt_scr)
            dw_ref[...] = jnp.zeros_like(dw_ref)
            db_ref[...] = jnp.zeros_like(db_ref)

        a, v = y_ref[:, :D_FF].astype(F32), y_ref[:, D_FF:].astype(F32)
        d = d_ref[...].astype(F32)
        silu_a, dsilu_a = _silu_and_grad(a)
        dy = jnp.concatenate([d * v * dsilu_a, d * silu_a], axis=1)
        dx, dw = _causal_conv_bwd(dy, nxt_scr[...], x_ref[...].astype(F32), w_ref[...])
        dx_ref[...] = dx.astype(BF16)
        nxt_scr[...] = dy[:8]
        dw_ref[...] += dw
        db_ref[...] += _colsum(dy)

    return pl.pallas_call(
        body, name="conv_f_bwd", grid=(nt,),
        in_specs=[_rows(tm, D_FF, nt, True), _rows(tm, c, nt, True), _rows(tm, c, nt, True), _full(cw.shape)],
        out_specs=[_rows(tm, c, nt, True), _full(cw.shape), _full((1, c))],
        out_shape=[jax.ShapeDtypeStruct((t, c), BF16), jax.ShapeDtypeStruct(cw.shape, F32),
                   jax.ShapeDtypeStruct((1, c), F32)],
        scratch_shapes=[pltpu.VMEM((8, c), F32)], compiler_params=_params(1),
    )(dact, y, up, cw)


def _conv_a_bwd(dxs, db, dc, y, xbc, cw, tm=256):
    t, c = xbc.shape
    nt = t // tm

    def body(dxs_ref, db_ref, dc_ref, y_ref, x_ref, w_ref, dx_ref, dw_ref, dbias_ref, nxt_scr):
        @pl.when(pl.program_id(0) == 0)
        def _():
            nxt_scr[...] = jnp.zeros_like(nxt_scr)
            dw_ref[...] = jnp.zeros_like(dw_ref)
            dbias_ref[...] = jnp.zeros_like(dbias_ref)

        dy = jnp.concatenate([dxs_ref[...], db_ref[...], dc_ref[...]], axis=1) * _dsilu(y_ref[...].astype(F32))
        dx, dw = _causal_conv_bwd(dy, nxt_scr[...], x_ref[...].astype(F32), w_ref[...])
        dx_ref[...] = dx.astype(BF16)
        nxt_scr[...] = dy[:8]
        dw_ref[...] += dw
        dbias_ref[...] += _colsum(dy)

    return pl.pallas_call(
        body, name="conv_a_bwd", grid=(nt,),
        in_specs=[_rows(tm, SSD_INNER, nt, True), _rows(tm, SSD_BC, nt, True), _rows(tm, SSD_BC, nt, True),
                  _rows(tm, c, nt, True), _rows(tm, c, nt, True), _full(cw.shape)],
        out_specs=[_rows(tm, c, nt, True), _full(cw.shape), _full((1, c))],
        out_shape=[jax.ShapeDtypeStruct((t, c), BF16), jax.ShapeDtypeStruct(cw.shape, F32),
                   jax.ShapeDtypeStruct((1, c), F32)],
        scratch_shapes=[pltpu.VMEM((8, c), F32)], compiler_params=_params(1),
    )(dxs, db, dc, y, xbc, cw)


def _pad_lanes(v, n=DT_PAD):
    return jnp.pad(v, ((0, 0), (0, n - v.shape[1])))


def _local_step(x, target, w, p, after=None, late_weights=None, on_grad=None, on_small=None):
    dtb, alog, dsk = _pad_lanes(p["dt_bias"]), _pad_lanes(p["a_log"]), _pad_lanes(p["d_skip"])
    bs_t = _pad_lanes(p["b_spatial"].T)
    e_heads = (jnp.arange(SSD_INNER)[:, None] // SSD_HEAD_DIM == jnp.arange(LANES)[None, :]).astype(BF16)
    e_heads_t = (jnp.arange(LANES)[:, None] == jnp.arange(SSD_INNER)[None, :] // SSD_HEAD_DIM).astype(BF16)
    e_groups = (jnp.arange(SGU_WIDTH)[:, None] // LANES == jnp.arange(LANES)[None, :]).astype(BF16)

    n1 = _norm_fwd(x, p["norm1_w"], "norm1_fwd", after=after)
    z = _mm(n1, w["z"], "nt", "proj_z", out_dtype=BF16)
    xbc = _mm(n1, w["xbc"], "nt", "proj_xbc", out_dtype=BF16)
    dtr = _mm(n1, w["dt"], "nt", "proj_dt")
    uv = _mm(n1, w["uv"], "nt", "proj_uv", out_dtype=BF16)
    gates = _mm(n1, w["gates"], "nt", "proj_gates", out_dtype=BF16)
    xc, conv_a_out = _conv_a_fwd(xbc, w["conv_a"], p["conv_a_b"])
    y, ya, sprev = _ssd_fwd(xc, dtr, z, dtb, alog, dsk, p["ssd_norm_w"], e_heads_t)
    yb = _sgu_fwd(uv, p["uv_b"], p["v_ln_w"], p["v_ln_b"], p["w_spatial"], bs_t)
    if late_weights is not None:
        w = {**w, **late_weights(ya, yb)}
    narrow = (D_MODEL, BF16)
    pa, pb, mix = _mm_rows(
        [ya, yb], [w["branch_a"], w["branch_b"]], "nn", "branches", _branches_merge, rows=[gates],
        fulls=[p["b_gate"]], row_outs=[narrow] * 3, summed=False)
    wide = [(D_MODEL, F32), (D_MODEL, BF16)]
    h1, n2 = _mm_rows(mix, w["out"], "nn", "out_proj", _residual_norm, rows=[x], fulls=[p["norm2_w"]], row_outs=wide)
    up = _mm(n2, w["up"], "nt", "up_proj", out_dtype=BF16)
    act, conv_f_out = _conv_f_fwd(up, w["conv_f"], p["conv_f_b"])
    dh2, dh2b, loss, g_final = _mm_rows(
        act, w["down"], "nn", "down_proj", _loss_and_grad, rows=[h1, target], fulls=[p["final_norm_w"]],
        row_outs=wide, acc_outs=[(8, LANES), (1, D_MODEL)])

    on_grad = on_grad or (lambda name, grads: None)
    g = {"final_norm_w": g_final}
    g["down"] = _wgrad(act, dh2b, "down_wgrad")
    tok = on_grad("w_down", g)
    dact = _mm(dh2b, w["down"], "nt", "down_dgrad", out_dtype=BF16, after=tok)
    dup, g["conv_f"], g["conv_f_b"] = _conv_f_bwd(dact, conv_f_out, up, w["conv_f"])
    g["up"] = _wgrad(dup, n2, "up_wgrad")
    tok = on_grad("w_up", g)
    dh1, dh1b, g["norm2_w"] = _mm_rows(
        dup, w["up"], "nn", "up_dgrad", _norm_backward, rows=[h1, dh2], fulls=[p["norm2_w"]], row_outs=wide,
        acc_outs=[(1, D_MODEL)], after=tok)
    g["out"] = _wgrad(mix, dh1b, "out_wgrad")
    tok = on_grad("w_out", g)
    dgates, dpa, dpb, g["b_gate"] = _mm_rows(
        dh1b, w["out"], "nt", "out_dgrad", _merge_backward, rows=[gates, pa, pb], fulls=[p["b_gate"]],
        row_outs=[(2 * D_MODEL, BF16), (D_MODEL, BF16), (D_MODEL, BF16)], acc_outs=[(1, 2 * D_MODEL)], after=tok)
    g["branch_a"] = _wgrad(ya, dpa, "branch_a_wgrad")
    g["branch_b"] = _wgrad(yb, dpb, "branch_b_wgrad")
    tok = on_grad("w_branch", g)
    dya, dyb = _mm_rows(
        [dpa, dpb], [w["branch_a"], w["branch_b"]], "nt", "branches_dgrad", lambda products: (tuple(products), ()),
        row_outs=[(SSD_INNER, F32), (SGU_WIDTH, F32)], after=tok, summed=False)
    duv, g["uv_b"], g["v_ln_w"], g["v_ln_b"], g["w_spatial"], dbs_t = _sgu_bwd(
        dyb, uv, p["uv_b"], p["v_ln_w"], p["v_ln_b"], p["w_spatial"], bs_t, e_groups)
    g["b_spatial"] = dbs_t[:, :SGU_GROUPS].T
    dz, dxs, db, dc, ddtr, g["ssd_norm_w"], ddtb, dalog, ddsk = _ssd_bwd(
        dya, y, z, xc, dtr, sprev, dtb, alog, dsk, p["ssd_norm_w"], e_heads, e_heads_t)
    g["dt_bias"], g["a_log"], g["d_skip"] = ddtb, dalog, ddsk
    dxbc, g["conv_a"], g["conv_a_b"] = _conv_a_bwd(dxs, db, dc, conv_a_out, xbc, w["conv_a"])
    tok = on_small(g, loss) if on_small else None
    ddtrb = ddtr.astype(BF16)
    for name, d in (("z", dz), ("xbc", dxbc), ("dt", ddtrb), ("uv", duv), ("gates", dgates)):
        g[name] = _wgrad(d, n1, name + "_wgrad", after=tok)
    tok = on_grad("w_in", g)
    dn1 = _mm([dz, dxbc], [w["z"], w["xbc"]], "nn", "ssd_dgrad", after=tok)
    gx, g["norm1_w"] = _mm_rows(
        [duv, dgates, ddtrb], [w["uv"], w["gates"], w["dt"]], "nn", "in_dgrad",
        lambda r, so_far, h, dres, w_: tuple(t[:1] for t in _norm_backward(r + so_far, h, dres, w_)),
        rows=[dn1, x, dh1], fulls=[p["norm1_w"]], row_outs=wide[:1], acc_outs=[(1, D_MODEL)])
    return loss, gx, g


def _place():
    return lax.axis_index("x"), lax.axis_index("y"), lax.axis_index("c")


def _other_chips(x, y):
    return [(1 - x, y), (x, 1 - y), (1 - x, 1 - y)]


def _all_gather(shards, name):
    n = len(shards)

    def body(*refs):
        ins, outs = refs[:n], refs[n:2 * n]
        send_sems, recv_sems, local_sems = refs[2 * n:]
        x, y, c = _place()
        me, sibling = (x, y, c), (x, y, 1 - c)
        chips = _other_chips(x, y)

        def copy(a, k, block, to, src=None):
            slot = outs[a].at[4 * block[0] + 2 * block[1] + block[2]]
            return pltpu.make_async_remote_copy(
                src_ref=slot if src is None else src, dst_ref=slot, send_sem=send_sems.at[7 * a + k],
                recv_sem=recv_sems.at[7 * a + k], device_id=to, device_id_type=MESH)

        started = []
        for a in range(n):
            mine = pltpu.make_async_copy(ins[a], outs[a].at[4 * x + 2 * y + c], local_sems.at[a])
            mine.start()
            started.append(mine)
        sends = []
        for a in range(n):
            sends.append(copy(a, 0, me, sibling, src=ins[a]))
            sends += [copy(a, 1 + j, me, (*chip, c), src=ins[a]) for j, chip in enumerate(chips)]
        for cp in sends:
            cp.start()
        for a in range(n):
            for j, chip in enumerate(chips):
                copy(a, 1 + j, (*chip, c), me).wait_recv()
                fwd = copy(a, 4 + j, (*chip, c), sibling)
                fwd.start()
                sends.append(fwd)
        for a in range(n):
            copy(a, 0, sibling, me).wait_recv()
            for j, chip in enumerate(chips):
                copy(a, 4 + j, (*chip, 1 - c), me).wait_recv()
        for cp in sends:
            cp.wait_send()
        for mine in started:
            mine.wait()

    any_spec = pl.BlockSpec(memory_space=pl.ANY)
    return pl.pallas_call(
        body, name=name, in_specs=[any_spec] * n, out_specs=[any_spec] * n,
        out_shape=[jax.ShapeDtypeStruct((N_DEV, *s.shape), s.dtype) for s in shards],
        scratch_shapes=[pltpu.SemaphoreType.DMA((7 * n,)), pltpu.SemaphoreType.DMA((7 * n,)),
                        pltpu.SemaphoreType.DMA((n,))],
    )(*shards)


HBM_SPEC = pl.BlockSpec(memory_space=pltpu.HBM)
SEM_SPEC = pl.BlockSpec(memory_space=pltpu.SEMAPHORE)
ANY_SPEC = pl.BlockSpec(memory_space=pl.ANY)
DATAFLOW = pltpu.SideEffectType.DATAFLOW_SIDE_EFFECTING
N_PEERS = N_DEV - 1


def _peers(x, y, c):
    out = []
    for r in range(1, N_DEV):
        fx, fy, fc = r >> 2 & 1, r >> 1 & 1, r & 1
        out.append(((1 - x) if fx else x, (1 - y) if fy else y, (1 - c) if fc else c))
    return out


def _gather_copies(srcs, lands, send_sems, recv_sems, sending, scatter=False):
    x, y, c = _place()
    copies = []
    for a, (src, land) in enumerate(zip(srcs, lands)):
        for j, (px, py, pc) in enumerate(_peers(x, y, c)):
            mine, theirs = 4 * x + 2 * y + c, 4 * px + 2 * py + pc
            block = src.at[theirs if sending else 0] if scatter else src
            copies.append(pltpu.make_async_remote_copy(
                src_ref=block, dst_ref=land.at[mine if sending else theirs], send_sem=send_sems.at[N_PEERS * a + j],
                recv_sem=recv_sems.at[N_PEERS * a + j], device_id=(px, py, pc), device_id_type=MESH))
    return copies


def _gather_start(shards, after, name, scatter=False):
    n = len(shards)
    after = [] if after is None else [after]

    def body(*refs):
        srcs, lands = refs[:n], refs[n:2 * n]
        send_sems, recv_sems = refs[2 * n + len(after):2 * n + len(after) + 2]
        token = refs[-1]
        for cp in _gather_copies(srcs, lands, send_sems, recv_sems, sending=True, scatter=scatter):
            cp.start()
        token[...] = jnp.zeros_like(token)

    lands = [lax.empty(s.shape if scatter else (N_DEV, *s.shape), s.dtype) for s in shards]
    hbm = lambda a: pltpu.with_memory_space_constraint(a, pltpu.HBM)
    out = pl.pallas_call(
        body, name=name,
        out_shape=(pltpu.SemaphoreType.DMA((N_PEERS * n,)), pltpu.SemaphoreType.DMA((N_PEERS * n,)),
                   *[pltpu.HBM(a.shape, a.dtype) for a in (*shards, *lands)], jax.ShapeDtypeStruct((8, LANES), F32)),
        in_specs=[HBM_SPEC] * (2 * n) + [ANY_SPEC] * len(after),
        out_specs=(SEM_SPEC, SEM_SPEC, *[HBM_SPEC] * (2 * n), pl.BlockSpec(memory_space=pltpu.VMEM)),
        input_output_aliases={i: 2 + i for i in range(2 * n)},
        compiler_params=pltpu.CompilerParams(has_side_effects=DATAFLOW),
    )(*[hbm(a) for a in (*shards, *lands)], *after)
    return out[0], out[1], out[2:2 + n], out[2 + n:2 + 2 * n], out[-1]


def _gather_wait(send_sems, recv_sems, shards, lands, after, name, scatter=False):
    n = len(shards)
    after = tuple(after)

    def body(*refs):
        srcs, lands_ = refs[:n], refs[n:2 * n]
        send, recv = refs[2 * n:2 * n + 2]
        for cp in _gather_copies(srcs, lands_, send, recv, sending=False, scatter=scatter):
            cp.wait_send()
            cp.wait_recv()

    out = pl.pallas_call(
        body, name=name, out_shape=tuple(pltpu.HBM(a.shape, a.dtype) for a in (*shards, *lands)),
        in_specs=[HBM_SPEC] * (2 * n) + [SEM_SPEC, SEM_SPEC] + [ANY_SPEC] * len(after),
        out_specs=tuple([HBM_SPEC] * (2 * n)), input_output_aliases={i: i for i in range(2 * n)},
        compiler_params=pltpu.CompilerParams(has_side_effects=DATAFLOW),
    )(*shards, *lands, send_sems, recv_sems, *after)
    return out[:n], out[n:]


def _chip_copies(src, land, send_sems, recv_sems):
    x, y, c = _place()
    return [pltpu.make_async_remote_copy(
        src_ref=src.at[2 * cx + cy], dst_ref=land.at[j], send_sem=send_sems.at[j], recv_sem=recv_sems.at[j],
        device_id=(cx, cy, c), device_id_type=MESH) for j, (cx, cy) in enumerate(_other_chips(x, y))]


def _chips_start(q, name):
    def body(q_ref, land_ref, send_sems, recv_sems, q_thru, land_thru, token):
        for cp in _chip_copies(q_ref, land_ref, send_sems, recv_sems):
            cp.start()
        token[...] = jnp.zeros_like(token)

    land = lax.empty((3, *q.shape[1:]), q.dtype)
    return pl.pallas_call(
        body, name=name,
        out_shape=(pltpu.SemaphoreType.DMA((3,)), pltpu.SemaphoreType.DMA((3,)), pltpu.HBM(q.shape, q.dtype),
                   pltpu.HBM(land.shape, land.dtype), jax.ShapeDtypeStruct((8, LANES), F32)),
        in_specs=[HBM_SPEC, HBM_SPEC],
        out_specs=(SEM_SPEC, SEM_SPEC, HBM_SPEC, HBM_SPEC, pl.BlockSpec(memory_space=pltpu.VMEM)),
        input_output_aliases={0: 2, 1: 3}, compiler_params=pltpu.CompilerParams(has_side_effects=DATAFLOW),
    )(pltpu.with_memory_space_constraint(q, pltpu.HBM), pltpu.with_memory_space_constraint(land, pltpu.HBM))


def _chips_wait(send_sems, recv_sems, q, land, after, name):
    def body(q_ref, land_ref, send, recv, after_ref, q_out, land_out):
        for cp in _chip_copies(q_ref, land_ref, send, recv):
            cp.wait_send()
            cp.wait_recv()

    return pl.pallas_call(
        body, name=name, out_shape=(pltpu.HBM(q.shape, q.dtype), pltpu.HBM(land.shape, land.dtype)),
        in_specs=[HBM_SPEC, HBM_SPEC, SEM_SPEC, SEM_SPEC, ANY_SPEC], out_specs=(HBM_SPEC, HBM_SPEC),
        input_output_aliases={0: 0, 1: 1}, compiler_params=pltpu.CompilerParams(has_side_effects=DATAFLOW),
    )(q, land, send_sems, recv_sems, after)[1]


def _exchange_cores(part, name):
    def body(in_ref, out_ref, send_sems, recv_sems):
        x, y, c = _place()
        copies = [pltpu.make_async_remote_copy(
            src_ref=in_ref.at[2 * k + (1 - c)], dst_ref=out_ref.at[k], send_sem=send_sems.at[k],
            recv_sem=recv_sems.at[k], device_id=(x, y, 1 - c), device_id_type=MESH) for k in range(4)]
        for cp in copies:
            cp.start()
        for cp in copies:
            cp.wait()

    return pl.pallas_call(
        body, name=name, in_specs=[ANY_SPEC], out_specs=ANY_SPEC,
        out_shape=jax.ShapeDtypeStruct((4, *part.shape[1:]), part.dtype),
        scratch_shapes=[pltpu.SemaphoreType.DMA((4,)), pltpu.SemaphoreType.DMA((4,))],
    )(part)


def _chip_sum(part, got, place, name, tr=256):
    _, r, c = part.shape
    tr, tc = _tile2d(r, c, tr)

    def body(place_ref, p_ref, g_ref, q_ref, own_ref):
        s = p_ref[0].astype(F32) + g_ref[0].astype(F32)
        q_ref[0] = s.astype(BF16)

        @pl.when(pl.program_id(2) == place_ref[1])
        def _():
            own_ref[...] = s

    grid_spec = pltpu.PrefetchScalarGridSpec(
        num_scalar_prefetch=1, grid=(r // tr, c // tc, 4),
        in_specs=[pl.BlockSpec((1, tr, tc), lambda i, j, k, pr: (2 * k + pr[0], i, j)),
                  pl.BlockSpec((1, tr, tc), lambda i, j, k, pr: (k, i, j))],
        out_specs=[pl.BlockSpec((1, tr, tc), lambda i, j, k, pr: (k, i, j)),
                   pl.BlockSpec((tr, tc), lambda i, j, k, pr: (i, j))])
    return pl.pallas_call(
        body, name=name, grid_spec=grid_spec,
        out_shape=[jax.ShapeDtypeStruct((4, r, c), BF16), jax.ShapeDtypeStruct((r, c), F32)],
        compiler_params=_params(3),
    )(place, part, got)


def _sum_adamw(own, got, w, m, v, name):
    r, c = own.shape
    tc = 4 * LANES

    def body(own_ref, got_ref, w_ref, m_ref, v_ref, g_ref, d_ref, nm_ref, nv_ref):
        g = own_ref[...]
        for j in range(3):
            g = g + got_ref[j].astype(F32)
        two_d = lambda ref: ref[...].reshape(r, tc)
        delta, nm, nv = _adamw(two_d(w_ref), g, two_d(m_ref), two_d(v_ref))
        for ref, val in ((g_ref, g), (d_ref, delta), (nm_ref, nm), (nv_ref, nv)):
            ref[...] = val.reshape(ref.shape)

    wblk = pl.BlockSpec((r, 1, tc), lambda j: (0, 0, j))
    return pl.pallas_call(
        body, name=name, grid=(c // tc,),
        in_specs=[pl.BlockSpec((r, tc), lambda j: (0, j)), pl.BlockSpec((3, r, tc), lambda j: (0, 0, j)),
                  wblk, wblk, wblk],
        out_specs=[wblk] * 4, out_shape=[jax.ShapeDtypeStruct(w.shape, F32)] * 4, compiler_params=_params(1),
    )(own, got, w, m, v)


def _adamw(w, g, m, v):
    m = ADAM_B1 * m + (1.0 - ADAM_B1) * g
    v = ADAM_B2 * v + (1.0 - ADAM_B2) * jnp.square(g)
    m_hat = m / (1.0 - ADAM_B1 ** ADAM_STEP)
    v_hat = v / (1.0 - ADAM_B2 ** ADAM_STEP)
    return -ADAM_LR * (m_hat / (jnp.sqrt(v_hat) + ADAM_EPS) + ADAM_WD * w), m, v


def _sum8_adamw(part, got, place, w, m, v, name, tr=256):
    r, c = w.shape
    tr, tc = _tile2d(r, c, tr)
    blk = pl.BlockSpec((tr, tc), lambda i, j, pr: (i, j))

    def body(place_ref, own_ref, got_ref, w_ref, m_ref, v_ref, g_ref, d_ref, nm_ref, nv_ref):
        dev = 2 * place_ref[1] + place_ref[0]
        g = jnp.zeros((tr, tc), F32)
        for d in range(N_DEV):
            g = g + jnp.where(dev == d, own_ref[0], got_ref[d]).astype(F32)
        g_ref[...] = g
        d_ref[...], nm_ref[...], nv_ref[...] = _adamw(w_ref[...], g, m_ref[...], v_ref[...])

    grid_spec = pltpu.PrefetchScalarGridSpec(
        num_scalar_prefetch=1, grid=(r // tr, c // tc),
        in_specs=[pl.BlockSpec((1, tr, tc), lambda i, j, pr: (2 * pr[1] + pr[0], i, j)),
                  pl.BlockSpec((N_DEV, tr, tc), lambda i, j, pr: (0, i, j)), blk, blk, blk],
        out_specs=[blk] * 4)
    return pl.pallas_call(
        body, name=name, grid_spec=grid_spec, out_shape=[jax.ShapeDtypeStruct(w.shape, F32)] * 4,
        compiler_params=_params(2),
    )(place, part, got, w, m, v)


VECTORS = ["norm1_w", "b_gate", "conv_a_b", "dt_bias", "a_log", "d_skip", "ssd_norm_w", "uv_b", "v_ln_w", "v_ln_b",
           "norm2_w", "conv_f_b", "final_norm_w"]
SMALL_ORDER = VECTORS + ["w_spatial", "b_spatial", "conv_a_w", "conv_f_w"]


ROW_VECTORS = VECTORS[1:]


def _small_adamw(gathered, w, m, v):
    sizes = {n: w[n].shape[1] for n in ROW_VECTORS}
    offs, off = {}, 0
    for n in ROW_VECTORS:
        offs[n] = off
        off += -(-sizes[n] // LANES) * LANES
    loss_off = off
    k = len(SMALL_ORDER)
    n_g = len(gathered)

    def body(*refs):
        row_ref, ws_ref, bs_ref, ca_ref, cf_ref, n1_ref = refs[:n_g]
        w_refs, m_refs, v_refs = (dict(zip(SMALL_ORDER, refs[n_g + i * k:n_g + (i + 1) * k])) for i in range(3))
        outs = refs[n_g + 3 * k:]
        x, y, c = _place()
        dev = 4 * x + 2 * y + c

        def total(ref):
            s = ref[0]
            for d in range(1, N_DEV):
                s = s + ref[d]
            return s

        row = total(row_ref)
        grads = {n: row[:, offs[n]:offs[n] + sizes[n]] for n in ROW_VECTORS}
        grads["norm1_w"], grads["w_spatial"], grads["b_spatial"] = total(n1_ref), total(ws_ref), total(bs_ref)
        for n, ref in (("conv_a_w", ca_ref), ("conv_f_w", cf_ref)):
            whole, cols = total(ref), w_refs[n].shape[1]
            mine = whole[:, :cols]
            for d in range(1, N_DEV):
                mine = jnp.where(dev == d, whole[:, d * cols:(d + 1) * cols], mine)
            grads[n] = mine
        for i, n in enumerate(SMALL_ORDER):
            outs[4 * i][...] = grads[n]
            outs[4 * i + 1][...], outs[4 * i + 2][...], outs[4 * i + 3][...] = _adamw(
                w_refs[n][...], grads[n], m_refs[n][...], v_refs[n][...])
        outs[4 * k][...] = row[:, loss_off:loss_off + LANES]

    out = pl.pallas_call(
        body, name="adamw_small",
        out_shape=[jax.ShapeDtypeStruct(w[n].shape, F32) for n in SMALL_ORDER for _ in range(4)]
        + [jax.ShapeDtypeStruct((1, LANES), F32)],
        compiler_params=_params(0),
    )(*gathered, *[t[n] for t in (w, m, v) for n in SMALL_ORDER])
    return [dict(zip(SMALL_ORDER, out[j:4 * k:4])) for j in range(4)] + [out[4 * k]]


SMALL = ["norm1_w", "b_gate", "conv_a_b", "dt_bias", "a_log", "d_skip", "ssd_norm_w", "uv_b", "v_ln_w", "v_ln_b",
         "w_spatial", "b_spatial", "norm2_w", "conv_f_b", "final_norm_w"]
BIG = ["w_in", "w_branch", "w_out", "w_up", "w_down"]
TRANSPOSED = ("w_in", "w_up")
WEIGHTS = ["norm1_w", "w_in", "b_gate", "conv_a_w", "conv_a_b", "dt_bias", "a_log", "d_skip", "ssd_norm_w", "uv_b",
           "v_ln_w", "v_ln_b", "w_spatial", "b_spatial", "w_branch", "w_out", "norm2_w", "w_up", "conv_f_w",
           "conv_f_b", "w_down", "final_norm_w"]
IN_SPLITS = [("z", 0, 2048), ("xbc", 2048, 5120), ("dt", 5120, 5152), ("uv", 5152, 7200), ("gates", 7200, 9248)]


def _columns_from_devices(a):
    return a.transpose(1, 0, 2).reshape(a.shape[1], -1)


def kernel(x, norm1_w, w_in, b_gate, conv_a_w, conv_a_b, dt_bias, a_log, d_skip, ssd_norm_w, uv_b, v_ln_w, v_ln_b, w_spatial, b_spatial, w_branch, w_out, norm2_w, w_up, conv_f_w, conv_f_b, w_down, final_norm_w, loss_target, m_norm1_w, m_w_in, m_b_gate, m_conv_a_w, m_conv_a_b, m_dt_bias, m_a_log, m_d_skip, m_ssd_norm_w, m_uv_b, m_v_ln_w, m_v_ln_b, m_w_spatial, m_b_spatial, m_w_branch, m_w_out, m_norm2_w, m_w_up, m_conv_f_w, m_conv_f_b, m_w_down, m_final_norm_w, v_norm1_w, v_w_in, v_b_gate, v_conv_a_w, v_conv_a_b, v_dt_bias, v_a_log, v_d_skip, v_ssd_norm_w, v_uv_b, v_v_ln_w, v_v_ln_b, v_w_spatial, v_b_spatial, v_w_branch, v_w_out, v_norm2_w, v_w_up, v_conv_f_w, v_conv_f_b, v_w_down, v_final_norm_w):
    args = dict(locals())
    wts = {n: args[n] for n in WEIGHTS}
    mom = {n: args["m_" + n] for n in WEIGHTS}
    var = {n: args["v_" + n] for n in WEIGHTS}
    cx, cy, cc = _place()
    dev = 4 * cx + 2 * cy + cc
    place = jnp.stack([cc, 2 * cx + cy]).astype(jnp.int32)

    def shard2d(n, a):
        return a[0].T if n in TRANSPOSED else a[0]

    def unshard(n, b):
        return (b.T if n in TRANSPOSED else b)[None]

    g_in, g_conv_a, g_conv_f = _all_gather(
        [shard2d("w_in", w_in).astype(BF16), conv_a_w[0], conv_f_w[0]], "gather_w_in")
    late = [shard2d(n, wts[n]).astype(BF16) for n in BIG[1:]]
    send_sems, recv_sems, late, lands, token = _gather_start(late, g_in, "gather_late_start")
    w_in_rows = g_in.reshape(-1, D_MODEL)
    w = {name: w_in_rows[lo:hi] for name, lo, hi in IN_SPLITS}
    w["dt"] = jnp.pad(w["dt"], ((0, DT_PAD - SSD_HEADS), (0, 0)))
    w["conv_a"] = _columns_from_devices(g_conv_a)
    w["conv_f"] = _columns_from_devices(g_conv_f)

    def late_weights(*after):
        mine, got = _gather_wait(send_sems, recv_sems, late, lands, after, "gather_late_wait")
        g_branch, g_out, g_up, g_down = [lax.dynamic_update_index_in_dim(land, own, dev, 0).reshape(-1, D_MODEL)
                                         for land, own in zip(got, mine)]
        return {"branch_a": g_branch[:SSD_INNER], "branch_b": g_branch[SSD_INNER:], "out": g_out, "up": g_up,
                "down": g_down}

    in_flight = {}

    def on_grad(n, g):
        part = {"w_in": lambda: jnp.concatenate([g[name][:hi - lo] for name, lo, hi in IN_SPLITS], axis=0),
                "w_branch": lambda: jnp.concatenate([g["branch_a"], g["branch_b"]], axis=0),
                "w_out": lambda: g["out"], "w_up": lambda: g["up"], "w_down": lambda: g["down"]}[n]()
        part = part.reshape(N_DEV, -1, D_MODEL)
        if n == "w_in":
            q, own = _chip_sum(part, _exchange_cores(part, "to_other_core_w_in"), place, "chip_sum_w_in")
            send, recv, q, land, tok = _chips_start(q, "to_other_chips_start_w_in")
            in_flight[n] = (own, send, recv, q, land)
            return tok
        send, recv, (part,), (land,), tok = _gather_start([part], None, f"to_owners_start_{n}", scatter=True)
        in_flight[n] = (part, send, recv, land)
        return tok

    p = {n: wts[n][0] if wts[n].ndim > 2 else wts[n].reshape(1, -1) for n in SMALL}
    small_flight = []

    def on_small(g, loss):
        arrays = [jnp.concatenate([g[n] for n in ROW_VECTORS] + [loss[:1]], axis=1), g["w_spatial"], g["b_spatial"],
                  g["conv_a"], g["conv_f"]]
        *flight, tok = _gather_start(arrays, g["conv_a"], "gather_small_start")
        small_flight.append(flight)
        return tok

    loss, gx, g = _local_step(x[0], loss_target[0], w, p, after=token, late_weights=late_weights, on_grad=on_grad,
                              on_small=on_small)
    *flight, _ = _gather_start([g["norm1_w"]], gx, "gather_norm1_start")
    small_flight.append(flight)

    grads, delta, new_m, new_v = {}, {}, {}, {}

    def big_adamw(n, after):
        if n == "w_in":
            own, send, recv, q, land = in_flight[n]
            got = _chips_wait(send, recv, q, land, after, "to_other_chips_wait_w_in")
            out = _sum_adamw(own, got, *[t[n].transpose(2, 0, 1) for t in (wts, mom, var)], "adamw_w_in")
            grads[n], delta[n], new_m[n], new_v[n] = [o.transpose(1, 2, 0) for o in out]
            return out[1]
        part, send, recv, land = in_flight[n]
        (part,), (got,) = _gather_wait(send, recv, [part], [land], [after], f"to_owners_wait_{n}", scatter=True)
        out = _sum8_adamw(part, got, place, *[shard2d(n, t[n]) for t in (wts, mom, var)], f"adamw_{n}")
        grads[n], delta[n], new_m[n], new_v[n] = [unshard(n, o) for o in out]
        return out[1]

    after = gx
    for n in ("w_down", "w_up", "w_out", "w_branch"):
        after = big_adamw(n, after)
    gathered = []
    for (send, recv, mine, land), name in zip(small_flight, ("gather_small_wait", "gather_norm1_wait")):
        mine, got = _gather_wait(send, recv, mine, land, [after], name)
        gathered += [lax.dynamic_update_index_in_dim(full, own, dev, 0) for full, own in zip(got, mine)]
    small = [{n: t[n][0] if t[n].ndim > 2 else t[n].reshape(1, -1) for n in SMALL_ORDER} for t in (wts, mom, var)]
    *outs, loss = _small_adamw(gathered, *small)
    for tgt, out in zip((grads, delta, new_m, new_v), outs):
        tgt.update({n: out[n].reshape(wts[n].shape) for n in SMALL_ORDER})
    big_adamw("w_in", loss)
    loss = loss[0, 0]

    return (loss, gx[None], *[grads[n] for n in WEIGHTS], *[delta[n] for n in WEIGHTS],
            *[new_m[n] for n in WEIGHTS], *[new_v[n] for n in WEIGHTS])
```

```python
import functools

import jax
import jax.numpy as jnp
from jax import lax
from jax.experimental import pallas as pl
from jax.experimental.pallas import tpu as pltpu

F32, BF16 = jnp.float32, jnp.bfloat16
HIGHEST = lax.Precision.HIGHEST

D_MODEL = 1024
SSD_INNER = 2048
SSD_HEAD_DIM = 64
SSD_HEADS = 32
SSD_GROUPS = 4
SSD_STATE = 128
SSD_BC = SSD_GROUPS * SSD_STATE
SSD_XBC = SSD_INNER + 2 * SSD_BC
SSD_CONV = 4
CHUNK = 128
N_PAIRS = SSD_HEADS // 2
PAIRS_PER_GROUP = N_PAIRS // SSD_GROUPS
SGU_WIDTH = 1024
SGU_GROUPS = 8
SGU_TILE = 512
D_FF = 2816
FFN_CONV = 3
NORM_EPS = 1e-6
LN_EPS = 1e-5
LANES = 128
DT_PAD = LANES

ADAM_LR, ADAM_B1, ADAM_B2, ADAM_EPS, ADAM_WD, ADAM_STEP = 0.001, 0.9, 0.999, 1e-08, 0.01, 10

N_DEV = 8
VMEM_LIMIT = 56 * 1024 * 1024
MESH = pl.DeviceIdType.MESH


def _params(n_grid, **kw):
    sem = dict(dimension_semantics=("arbitrary",) * n_grid) if n_grid else {}
    return pltpu.CompilerParams(vmem_limit_bytes=VMEM_LIMIT, **sem, **kw)


def _tile(n, pref):
    t = (min(pref, n) // LANES) * LANES
    while n % t:
        t -= LANES
    return t


def _row_tile(r, pref):
    for t in range(min(pref, r) // 16 * 16, 0, -16):
        if r % t == 0:
            return t
    return r


def _tile2d(r, c, rows):
    if r % 16 == 0:
        return _row_tile(r, rows), c
    return r, _tile(c, 2 * LANES)


def _rows(tm, n, nt=None, rev=False):
    if rev:
        return pl.BlockSpec((tm, n), lambda i: (nt - 1 - i, 0))
    return pl.BlockSpec((tm, n), lambda i: (i, 0))


def _halo(tm, n, rows=8):
    per = tm // rows
    return pl.BlockSpec((rows, n), lambda i: (jnp.maximum(i * per - 1, 0), 0))


def _full(shape):
    nd = len(shape)
    return pl.BlockSpec(shape, lambda *_: (0,) * nd)


def _rms(x, w, eps=NORM_EPS):
    return x * lax.rsqrt(jnp.mean(x * x, axis=-1, keepdims=True) + eps) * w


def _layer_norm(x, w, b):
    mu = jnp.mean(x, axis=-1, keepdims=True)
    var = jnp.mean(jnp.square(x - mu), axis=-1, keepdims=True)
    return (x - mu) * lax.rsqrt(var + LN_EPS) * w + b


def _sigmoid(x):
    return 1.0 / (1.0 + jnp.exp(-x))


def _silu(x):
    return x * _sigmoid(x)


def _dsilu(x):
    s = _sigmoid(x)
    return s * (1.0 + x * (1.0 - s))


def _silu_and_grad(x):
    s = _sigmoid(x)
    return x * s, s * (1.0 + x * (1.0 - s))


def _softplus(x):
    return jnp.maximum(x, 0.0) + jnp.log(1.0 + jnp.exp(-jnp.abs(x)))


def _gelu(x):
    return jax.nn.gelu(x)


def _dot(a, b):
    return jnp.dot(a, b, preferred_element_type=F32)


def _dot_nt(a, b):
    return lax.dot_general(a, b, (((1,), (1,)), ((), ())), preferred_element_type=F32)


def _dot_tn(a, b):
    return lax.dot_general(a, b, (((0,), (0,)), ((), ())), preferred_element_type=F32)


def _dot_split(p, e):
    hi = p.astype(BF16)
    lo = (p - hi.astype(F32)).astype(BF16)
    return _dot(hi, e) + _dot(lo, e)


def _colsum(x):
    return jnp.sum(x, axis=0, keepdims=True)


def _shift_down(x, halo, j):
    xs = pltpu.roll(x, j, 0)
    hs = pltpu.roll(halo, j, 0)
    r8 = lax.broadcasted_iota(jnp.int32, hs.shape, 0)
    return jnp.concatenate([jnp.where(r8 < j, hs, xs[:8]), xs[8:]], axis=0)


def _shift_up(x, nxt, j):
    n = x.shape[0]
    xs = pltpu.roll(x, n - j, 0)
    ns = pltpu.roll(nxt, 8 - j, 0)
    r8 = lax.broadcasted_iota(jnp.int32, ns.shape, 0)
    return jnp.concatenate([xs[:n - 8], jnp.where(r8 >= 8 - j, ns, xs[n - 8:])], axis=0)


def _causal_conv(x, halo, w, b):
    k = w.shape[0]
    y = b + w[k - 1:k, :] * x
    for j in range(1, k):
        y = y + w[k - 1 - j:k - j, :] * _shift_down(x, halo, j)
    return y


def _causal_conv_bwd(dy, nxt, x, w):
    k = w.shape[0]
    dx = w[k - 1:k, :] * dy
    dw = [_colsum(dy * x)]
    for j in range(1, k):
        dyj = _shift_up(dy, nxt, j)
        dx = dx + w[k - 1 - j:k - j, :] * dyj
        dw.append(_colsum(dyj * x))
    return dx, jnp.concatenate(dw[::-1], axis=0)


MM_TILE_PREF = 1408
MM_VMEM_BUDGET = 40 * 1024 * 1024


def _mm_tiles(m, n, k, out_bytes):
    tm, tn = _tile(m, MM_TILE_PREF), _tile(n, MM_TILE_PREF)
    need = lambda tm, tn: 2 * (2 * k * (tm + tn) + out_bytes * tm * tn)
    while need(tm, tn) > MM_VMEM_BUDGET:
        if tn >= tm and tn > LANES:
            tn = _tile(n, tn - LANES)
        else:
            tm = _tile(m, tm - LANES)
    return tm, tn


def _mm(a, b, dims, name, acc=None, out_dtype=F32, after=None):
    a_list, b_list = (list(a), list(b)) if isinstance(a, (list, tuple)) else ([a], [b])
    k_axis, m_axis = (0, 1) if dims == "tn" else (1, 0)
    m, ks = a_list[0].shape[m_axis], [x.shape[k_axis] for x in a_list]
    n = b_list[0].shape[0] if dims == "nt" else b_list[0].shape[1]
    tm, tn = _mm_tiles(m, n, sum(ks), 4 * (2 if acc is not None else 1))
    a_specs = [pl.BlockSpec((k, tm), lambda j, i: (0, i)) if dims == "tn" else pl.BlockSpec((tm, k), lambda j, i: (i, 0))
               for k in ks]
    b_specs = [pl.BlockSpec((tn, k), lambda j, i: (j, 0)) if dims == "nt" else pl.BlockSpec((k, tn), lambda j, i: (0, j))
               for k in ks]
    o_spec = pl.BlockSpec((tm, tn), lambda j, i: (i, j))
    dot = {"nn": _dot, "nt": _dot_nt, "tn": _dot_tn}[dims]
    n_pairs = len(ks)

    def body(*refs):
        rest = refs[2 * n_pairs:]
        r = dot(refs[0][...], refs[n_pairs][...])
        for i in range(1, n_pairs):
            r = r + dot(refs[i][...], refs[n_pairs + i][...])
        if acc is not None:
            r = r + rest[0][...]
        rest[-1][...] = r.astype(out_dtype)

    ins, specs = a_list + b_list, a_specs + b_specs
    if acc is not None:
        ins.append(acc)
        specs.append(o_spec)
    if after is not None:
        ins.append(after)
        specs.append(pl.BlockSpec(memory_space=pl.ANY))
    return pl.pallas_call(
        body, name=name, grid=(n // tn, m // tm), in_specs=specs, out_specs=o_spec,
        out_shape=jax.ShapeDtypeStruct((m, n), out_dtype), compiler_params=_params(2),
    )(*ins)


def _mm_rows(a, b, dims, name, fn, rows=(), fulls=(), row_outs=(), acc_outs=(), after=None, summed=True):
    a_list, b_list = (list(a), list(b)) if isinstance(a, (list, tuple)) else ([a], [b])
    m, ks = a_list[0].shape[0], [x.shape[1] for x in a_list]
    n, k = (b_list[0].shape[0] if dims == "nt" else b_list[0].shape[1]), sum(ks)
    per_row = 2 * k + 8 * n + sum(4 * r.shape[1] for r in rows) + sum(c * jnp.dtype(d).itemsize for c, d in row_outs)
    tm = _tile(m, 1024)
    while 2 * tm * per_row + 4 * k * n > MM_VMEM_BUDGET:
        tm = _tile(m, tm - LANES)
    dot = _dot_nt if dims == "nt" else _dot
    n_pairs = len(ks)
    n_in = 2 * n_pairs + len(rows) + len(fulls) + (after is not None)

    def body(*refs):
        ins, outs = refs[:n_in], refs[n_in:]
        row_refs, acc_refs = outs[:len(row_outs)], outs[len(row_outs):]

        @pl.when(pl.program_id(0) == 0)
        def _():
            for r in acc_refs:
                r[...] = jnp.zeros_like(r)

        products = [dot(ins[i][...], ins[n_pairs + i][...]) for i in range(n_pairs)]
        result = functools.reduce(lambda p, q: p + q, products) if summed else products
        new_rows, incs = fn(result, *[r[...] for r in ins[2 * n_pairs:2 * n_pairs + len(rows) + len(fulls)]])
        for r, val in zip(row_refs, new_rows):
            r[...] = val.astype(r.dtype)
        for r, inc in zip(acc_refs, incs):
            r[...] += inc

    extra, extra_specs = ([after], [pl.BlockSpec(memory_space=pl.ANY)]) if after is not None else ([], [])
    return pl.pallas_call(
        body, name=name, grid=(m // tm,),
        in_specs=[_rows(tm, k_i) for k_i in ks] + [_full(x.shape) for x in b_list]
        + [_rows(tm, r.shape[1]) for r in rows] + [_full(f.shape) for f in fulls] + extra_specs,
        out_specs=[_rows(tm, c) for c, _ in row_outs] + [_full(s) for s in acc_outs],
        out_shape=[jax.ShapeDtypeStruct((m, c), d) for c, d in row_outs]
        + [jax.ShapeDtypeStruct(s, F32) for s in acc_outs],
        compiler_params=_params(1),
    )(*a_list, *b_list, *rows, *fulls, *extra)


def _residual_norm(o, x, w):
    h = x + o
    return (h, _rms(h, w)), ()


def _norm_backward(dn, h, dres, w):
    _, vjp = jax.vjp(_rms, h, w)
    dh, dw = vjp(dn)
    dh = dh + dres
    return (dh, dh), (dw,)


def _loss_and_grad(dn, h1, target, w):
    yf, vjp = jax.vjp(_rms, h1 + dn, w)
    err = yf - target
    loss = 0.5 * jnp.sum(jnp.mean(err * err, axis=-1, keepdims=True))
    dh, dw = vjp(err * (1.0 / err.shape[-1]))
    return (dh, dh), (jnp.full((8, LANES), loss, F32), dw)


def _wgrad(a, d, name, after=None):
    return _mm(a, d, "tn", name, out_dtype=BF16, after=after)


def _norm_fwd(x, w, name, after=None, tm=512):
    t, d = x.shape

    def body(x_ref, w_ref, *rest):
        rest[-1][...] = _rms(x_ref[...], w_ref[...]).astype(BF16)

    extra, extra_specs = ([after], [_full(after.shape)]) if after is not None else ([], [])
    return pl.pallas_call(
        body, name=name, grid=(t // tm,), in_specs=[_rows(tm, d), _full((1, d))] + extra_specs,
        out_specs=_rows(tm, d), out_shape=jax.ShapeDtypeStruct((t, d), BF16), compiler_params=_params(1),
    )(x, w, *extra)


def _conv_a_fwd(xbc, cw, cb, tm=256):
    t, c = xbc.shape

    def body(x_ref, h_ref, w_ref, b_ref, o_ref, y_ref):
        halo = jnp.where(pl.program_id(0) > 0, h_ref[...].astype(F32)[8:], 0.0)
        y = _causal_conv(x_ref[...].astype(F32), halo, w_ref[...], b_ref[...])
        y_ref[...] = y.astype(BF16)
        o_ref[...] = _silu(y)

    return pl.pallas_call(
        body, name="conv_a_fwd", grid=(t // tm,),
        in_specs=[_rows(tm, c), _halo(tm, c, rows=16), _full(cw.shape), _full((1, c))],
        out_specs=[_rows(tm, c)] * 2,
        out_shape=[jax.ShapeDtypeStruct((t, c), F32), jax.ShapeDtypeStruct((t, c), BF16)], compiler_params=_params(1),
    )(xbc, xbc, cw, cb)


def _ssd_common(dtr, dtb, alog, e_t):
    row = lax.broadcasted_iota(jnp.int32, (CHUNK, CHUNK), 0)
    col = lax.broadcasted_iota(jnp.int32, (CHUNK, CHUNK), 1)
    causal = row >= col
    dt = _softplus(dtr + dtb)
    a = -jnp.exp(alog)
    acum = jnp.dot(causal.astype(F32), dt * a, precision=HIGHEST, preferred_element_type=F32)
    spread = lambda v: _dot(v.astype(BF16), e_t)
    elast = jnp.broadcast_to(jnp.exp(acum[CHUNK - 1:CHUNK, :]), (8, LANES))
    return dict(dt=dt, a=a, acum=acum, acum_t=acum.T, causal=causal, row=row, col=col, lane_lo=col < SSD_HEAD_DIM,
                dt_x=_dot_split(dt, e_t), ecol_x=spread(jnp.exp(acum)), elast_x=_dot_split(elast, e_t)[0:1],
                dsr_x=spread(jnp.exp(acum[CHUNK - 1:CHUNK, :] - acum)))


def _head_decay(c, h, transposed=False):
    d = c["acum"][:, h:h + 1] - c["acum_t"][h:h + 1, :]
    if transposed:
        return jnp.exp(jnp.where(c["row"] <= c["col"], -d, -jnp.inf))
    return jnp.exp(jnp.where(c["causal"], d, -jnp.inf))


def _ssd_fwd(xc, dtr, z, dtb, alog, dsk, nw, e_t):
    t = xc.shape[0]
    nc = t // CHUNK

    def body(xs_ref, b_ref, c_ref, dtr_ref, z_ref, dtb_ref, alog_ref, dsk_ref, nw_ref, et_ref,
             y_ref, ya_ref, sp_ref, s_scr):
        @pl.when(pl.program_id(0) == 0)
        def _():
            s_scr[...] = jnp.zeros_like(s_scr)

        c = _ssd_common(dtr_ref[...], dtb_ref[...], alog_ref[...], et_ref[...])
        lane_lo = c["lane_lo"]
        dsk = dsk_ref[...]
        for g in range(SSD_GROUPS):
            gs = slice(g * SSD_STATE, (g + 1) * SSD_STATE)
            bg_t, cg = b_ref[:, gs].T.astype(BF16), c_ref[:, gs].astype(BF16)
            cb = _dot(cg, bg_t)
            for pp in range(PAIRS_PER_GROUP):
                j = g * PAIRS_PER_GROUP + pp
                ps = slice(j * LANES, (j + 1) * LANES)
                x = xs_ref[:, ps]
                ecol, dsr = c["ecol_x"][:, ps], c["dsr_x"][:, ps]
                xdt = x * c["dt_x"][:, ps]
                xb = xdt.astype(BF16)
                zero = jnp.zeros_like(xb)
                yd = (_dot((cb * _head_decay(c, 2 * j)).astype(BF16), jnp.where(lane_lo, xb, zero))
                      + _dot((cb * _head_decay(c, 2 * j + 1)).astype(BF16), jnp.where(lane_lo, zero, xb)))
                sp = s_scr[j]
                yo = ecol * _dot(cg, sp.astype(BF16))
                st = _dot(bg_t, (xdt * dsr).astype(BF16))
                sp_ref[0, j] = sp
                s_scr[j] = c["elast_x"][:, ps] * sp + st
                dskp = jnp.where(lane_lo[0:1], dsk[:, 2 * j:2 * j + 1], dsk[:, 2 * j + 1:2 * j + 2])
                y_ref[:, ps] = yd + yo + dskp * x
        ya_ref[...] = _rms(y_ref[...] * _silu(z_ref[...].astype(F32)), nw_ref[...]).astype(BF16)

    ck = lambda n, col=0: pl.BlockSpec((CHUNK, n), lambda c: (c, col))
    return pl.pallas_call(
        body, name="ssd_fwd", grid=(nc,),
        in_specs=[ck(SSD_INNER), ck(SSD_BC, SSD_INNER // SSD_BC), ck(SSD_BC, SSD_INNER // SSD_BC + 1), ck(DT_PAD),
                  ck(SSD_INNER), _full((1, DT_PAD)), _full((1, DT_PAD)), _full((1, DT_PAD)),
                  _full((1, SSD_INNER)), _full(e_t.shape)],
        out_specs=[ck(SSD_INNER), ck(SSD_INNER),
                   pl.BlockSpec((1, N_PAIRS, SSD_STATE, LANES), lambda c: (c, 0, 0, 0))],
        out_shape=[jax.ShapeDtypeStruct((t, SSD_INNER), F32), jax.ShapeDtypeStruct((t, SSD_INNER), BF16),
                   jax.ShapeDtypeStruct((nc, N_PAIRS, SSD_STATE, LANES), F32)],
        scratch_shapes=[pltpu.VMEM((N_PAIRS, SSD_STATE, LANES), F32)], compiler_params=_params(1),
    )(xc, xc, xc, dtr, z, dtb, alog, dsk, nw, e_t)


def _ssd_bwd(dya, y, z, xc, dtr, sprev, dtb, alog, dsk, nw, e_heads, e_t):
    t = xc.shape[0]
    nc = t // CHUNK

    def body(dya_ref, y_ref, z_ref, xs_ref, b_ref, c_ref, dtr_ref, sp_ref, dtb_ref, alog_ref, dsk_ref, nw_ref, e_ref,
             et_ref, dz_ref, dxs_ref, db_ref, dc_ref, ddtr_ref, dnw_ref, ddtb_ref, dalog_ref, ddsk_ref, ds_scr):
        @pl.when(pl.program_id(0) == 0)
        def _():
            ds_scr[...] = jnp.zeros_like(ds_scr)
            for r in (dnw_ref, ddtb_ref, dalog_ref, ddsk_ref):
                r[...] = jnp.zeros_like(r)

        y = y_ref[...]
        _, gate_vjp = jax.vjp(lambda y_, z_, w_: _rms(y_ * _silu(z_), w_), y, z_ref[...].astype(F32), nw_ref[...])
        dy, dz, dnw = gate_vjp(dya_ref[...])
        dz_ref[...] = dz.astype(BF16)
        dnw_ref[...] += dnw

        dtr = dtr_ref[...]
        c = _ssd_common(dtr, dtb_ref[...], alog_ref[...], et_ref[...])
        dt, a, lane_lo, row, col = c["dt"], c["a"], c["lane_lo"], c["row"], c["col"]
        dsk = dsk_ref[...]
        p_a, p_dt, v_last = [], [], []
        da_cols = jnp.zeros((CHUNK, CHUNK), F32)
        da_rows = jnp.zeros((CHUNK, CHUNK), F32)
        for g in range(SSD_GROUPS):
            gs = slice(g * SSD_STATE, (g + 1) * SSD_STATE)
            bg, cg = b_ref[:, gs].astype(BF16), c_ref[:, gs].astype(BF16)
            bg_t, cg_t = b_ref[:, gs].T.astype(BF16), c_ref[:, gs].T.astype(BF16)
            cb, cb_t = _dot(cg, bg_t), _dot(bg, cg_t)
            dcb = jnp.zeros((CHUNK, CHUNK), F32)
            dbg = jnp.zeros((CHUNK, SSD_STATE), F32)
            dcg = jnp.zeros((CHUNK, SSD_STATE), F32)
            for pp in range(PAIRS_PER_GROUP):
                j = g * PAIRS_PER_GROUP + pp
                ps = slice(j * LANES, (j + 1) * LANES)
                x = xs_ref[:, ps]
                dtp, ecol, dsr = c["dt_x"][:, ps], c["ecol_x"][:, ps], c["dsr_x"][:, ps]
                elast = c["elast_x"][:, ps]
                xdt = x * dtp
                xb = xdt.astype(BF16)
                dskp = jnp.where(lane_lo[0:1], dsk[:, 2 * j:2 * j + 1], dsk[:, 2 * j + 1:2 * j + 2])
                dyp = dy[:, ps]
                dyb = dyp.astype(BF16)
                sp, dsn = sp_ref[0, j], ds_scr[j]
                spb, dsnb = sp.astype(BF16), dsn.astype(BF16)
                y_off = ecol * _dot(cg, spb)
                dw = (dyp * ecol).astype(BF16)
                dcg = dcg + _dot_nt(dw, spb)
                dsp = _dot(cg_t, dw) + elast * dsn
                xd = xdt * dsr
                zd = _dot(bg, dsnb) * dsr
                dbg = dbg + _dot_nt(xd.astype(BF16), dsnb)
                dxdt = zd
                zero = jnp.zeros_like(xb)
                for h, lm in ((2 * j, lane_lo), (2 * j + 1, jnp.logical_not(lane_lo))):
                    le = _head_decay(c, h)
                    dm = _dot_nt(jnp.where(lm, dyb, zero), jnp.where(lm, xb, zero))
                    dcb = dcb + dm * le
                    m = cb * le
                    m_t = (cb_t * _head_decay(c, h, transposed=True)).astype(BF16)
                    dxdt = dxdt + jnp.where(lm, _dot(m_t, dyb), 0.0)
                    q = dm * m
                    da_cols = da_cols + jnp.where(col == h, jnp.sum(q, axis=1, keepdims=True), 0.0)
                    da_rows = da_rows + jnp.where(row == h, _colsum(q), 0.0)
                ds_scr[j] = dsp
                dxs_ref[:, ps] = dxdt * dtp + dskp * dyp
                p_a.append(dyp * y_off - xdt * zd)
                p_dt.append(dxdt * x)
                v_last.append(_colsum(zd * xdt) + elast * _colsum(dsn * sp))
            dcbb = dcb.astype(BF16)
            db_ref[:, gs] = dbg + _dot_tn(dcbb, cg)
            dc_ref[:, gs] = dcg + _dot(dcbb, bg)
        e = e_ref[...]
        rows8 = jnp.concatenate([jnp.concatenate(v_last, axis=1), _colsum(dy * xs_ref[...]),
                                 jnp.zeros((6, SSD_INNER), F32)], axis=0)
        r8 = _dot_split(rows8, e)
        da = (_dot_split(jnp.concatenate(p_a, axis=1), e) + jnp.where(row == CHUNK - 1, r8[0:1], 0.0)
              + da_cols - da_rows.T)
        ddsk_ref[...] += r8[1:2]
        dadt = jnp.dot((row <= col).astype(F32), da, precision=HIGHEST, preferred_element_type=F32)
        ddt = dadt * a + _dot_split(jnp.concatenate(p_dt, axis=1), e)
        dalog_ref[...] += _colsum(dadt * dt) * a
        ddtr = ddt * _sigmoid(dtr + dtb_ref[...])
        ddtr_ref[...] = ddtr
        ddtb_ref[...] += _colsum(ddtr)

    ck = lambda n, col=0: pl.BlockSpec((CHUNK, n), lambda c: (nc - 1 - c, col))
    acc = lambda n: _full((1, n))
    return pl.pallas_call(
        body, name="ssd_bwd", grid=(nc,),
        in_specs=[ck(SSD_INNER), ck(SSD_INNER), ck(SSD_INNER), ck(SSD_INNER), ck(SSD_BC, SSD_INNER // SSD_BC),
                  ck(SSD_BC, SSD_INNER // SSD_BC + 1), ck(DT_PAD),
                  pl.BlockSpec((1, N_PAIRS, SSD_STATE, LANES), lambda c: (nc - 1 - c, 0, 0, 0)),
                  acc(DT_PAD), acc(DT_PAD), acc(DT_PAD), acc(SSD_INNER), _full((SSD_INNER, LANES)),
                  _full((LANES, SSD_INNER))],
        out_specs=[ck(SSD_INNER), ck(SSD_INNER), ck(SSD_BC), ck(SSD_BC), ck(DT_PAD),
                   acc(SSD_INNER), acc(DT_PAD), acc(DT_PAD), acc(DT_PAD)],
        out_shape=[jax.ShapeDtypeStruct((t, SSD_INNER), BF16), jax.ShapeDtypeStruct((t, SSD_INNER), F32),
                   jax.ShapeDtypeStruct((t, SSD_BC), F32), jax.ShapeDtypeStruct((t, SSD_BC), F32),
                   jax.ShapeDtypeStruct((t, DT_PAD), F32), jax.ShapeDtypeStruct((1, SSD_INNER), F32),
                   jax.ShapeDtypeStruct((1, DT_PAD), F32), jax.ShapeDtypeStruct((1, DT_PAD), F32),
                   jax.ShapeDtypeStruct((1, DT_PAD), F32)],
        scratch_shapes=[pltpu.VMEM((N_PAIRS, SSD_STATE, LANES), F32)], compiler_params=_params(1),
    )(dya, y, z, xc, xc, xc, dtr, sprev, dtb, alog, dsk, nw, e_heads, e_t)


def _sgu_act(uv, uvb, lnw, lnb):
    a = _gelu(uv + uvb)
    return a[:, :SGU_WIDTH], _layer_norm(a[:, SGU_WIDTH:], lnw, lnb)


def _sgu_weights(ws_ref):
    row = lax.broadcasted_iota(jnp.int32, (CHUNK, CHUNK), 0)
    col = lax.broadcasted_iota(jnp.int32, (CHUNK, CHUNK), 1)
    return [jnp.where(row >= col, ws_ref[g], 0.0).astype(BF16) for g in range(SGU_GROUPS)], row >= col


def _sgu_fwd(uv, uvb, lnw, lnb, ws, bs_t):
    t = uv.shape[0]

    def body(uv_ref, uvb_ref, lnw_ref, lnb_ref, ws_ref, bs_ref, o_ref):
        u, vn = _sgu_act(uv_ref[...].astype(F32), uvb_ref[...], lnw_ref[...], lnb_ref[...])
        wc, _ = _sgu_weights(ws_ref)
        bs = bs_ref[...]
        for ck in range(SGU_TILE // CHUNK):
            rs = slice(ck * CHUNK, (ck + 1) * CHUNK)
            for g in range(SGU_GROUPS):
                gs = slice(g * LANES, (g + 1) * LANES)
                mixed = _dot(wc[g], vn[rs, gs].astype(BF16)) + bs[:, g:g + 1]
                o_ref[rs, gs] = (u[rs, gs] * mixed).astype(BF16)

    return pl.pallas_call(
        body, name="sgu_fwd", grid=(t // SGU_TILE,),
        in_specs=[_rows(SGU_TILE, 2 * SGU_WIDTH), _full((1, 2 * SGU_WIDTH)), _full((1, SGU_WIDTH)),
                  _full((1, SGU_WIDTH)), _full(ws.shape), _full(bs_t.shape)],
        out_specs=_rows(SGU_TILE, SGU_WIDTH), out_shape=jax.ShapeDtypeStruct((t, SGU_WIDTH), BF16),
        compiler_params=_params(1),
    )(uv, uvb, lnw, lnb, ws, bs_t)


def _sgu_bwd(dyb, uv, uvb, lnw, lnb, ws, bs_t, e_groups):
    t = uv.shape[0]

    def body(dyb_ref, uv_ref, uvb_ref, lnw_ref, lnb_ref, ws_ref, bs_ref, e_ref,
             duv_ref, duvb_ref, dlnw_ref, dlnb_ref, dws_ref, dbs_ref):
        @pl.when(pl.program_id(0) == 0)
        def _():
            for r in (duvb_ref, dlnw_ref, dlnb_ref, dws_ref, dbs_ref):
                r[...] = jnp.zeros_like(r)

        (u, vn), act_vjp = jax.vjp(_sgu_act, uv_ref[...].astype(F32), uvb_ref[...], lnw_ref[...], lnb_ref[...])
        wc, causal = _sgu_weights(ws_ref)
        bs = bs_ref[...]
        dyb = dyb_ref[...]
        du_rows, dvn_rows = [], []
        for ck in range(SGU_TILE // CHUNK):
            rs = slice(ck * CHUNK, (ck + 1) * CHUNK)
            du, dvn, dmix = [], [], []
            for g in range(SGU_GROUPS):
                gs = slice(g * LANES, (g + 1) * LANES)
                vb = vn[rs, gs].astype(BF16)
                mixed = _dot(wc[g], vb) + bs[:, g:g + 1]
                dm = dyb[rs, gs] * u[rs, gs]
                dmb = dm.astype(BF16)
                du.append(dyb[rs, gs] * mixed)
                dvn.append(_dot_tn(wc[g], dmb))
                dws_ref[g] += jnp.where(causal, _dot_nt(dmb, vb), 0.0)
                dmix.append(dm)
            dbs_ref[...] += _dot_split(jnp.concatenate(dmix, axis=1), e_ref[...])
            du_rows.append(jnp.concatenate(du, axis=1))
            dvn_rows.append(jnp.concatenate(dvn, axis=1))
        duv, duvb, dlnw, dlnb = act_vjp((jnp.concatenate(du_rows, axis=0), jnp.concatenate(dvn_rows, axis=0)))
        duv_ref[...] = duv.astype(BF16)
        duvb_ref[...] += duvb
        dlnw_ref[...] += dlnw
        dlnb_ref[...] += dlnb

    return pl.pallas_call(
        body, name="sgu_bwd", grid=(t // SGU_TILE,),
        in_specs=[_rows(SGU_TILE, SGU_WIDTH), _rows(SGU_TILE, 2 * SGU_WIDTH), _full((1, 2 * SGU_WIDTH)),
                  _full((1, SGU_WIDTH)), _full((1, SGU_WIDTH)), _full(ws.shape), _full(bs_t.shape),
                  _full(e_groups.shape)],
        out_specs=[_rows(SGU_TILE, 2 * SGU_WIDTH), _full((1, 2 * SGU_WIDTH)), _full((1, SGU_WIDTH)),
                   _full((1, SGU_WIDTH)), _full(ws.shape), _full(bs_t.shape)],
        out_shape=[jax.ShapeDtypeStruct((t, 2 * SGU_WIDTH), BF16), jax.ShapeDtypeStruct((1, 2 * SGU_WIDTH), F32),
                   jax.ShapeDtypeStruct((1, SGU_WIDTH), F32), jax.ShapeDtypeStruct((1, SGU_WIDTH), F32),
                   jax.ShapeDtypeStruct(ws.shape, F32), jax.ShapeDtypeStruct(bs_t.shape, F32)],
        compiler_params=_params(1),
    )(dyb, uv, uvb, lnw, lnb, ws, bs_t, e_groups)


def _merge(gates, pa, pb, bg):
    s = _sigmoid(gates + bg)
    return s[:, :D_MODEL] * pa + s[:, D_MODEL:] * pb


def _branches_merge(branches, gates, bg):
    pa, pb = branches
    return (pa, pb, _merge(gates.astype(F32), pa, pb, bg)), ()


def _merge_backward(dmix, gates, pa, pb, bg):
    _, vjp = jax.vjp(_merge, gates.astype(F32), pa.astype(F32), pb.astype(F32), bg)
    dg, dpa, dpb, dbg = vjp(dmix)
    return (dg, dpa, dpb), (dbg,)


def _conv_f_fwd(up, cw, cb, tm=256):
    t, c = up.shape

    def body(x_ref, h_ref, w_ref, b_ref, o_ref, y_ref):
        halo = jnp.where(pl.program_id(0) > 0, h_ref[...].astype(F32)[8:], 0.0)
        y = _causal_conv(x_ref[...].astype(F32), halo, w_ref[...], b_ref[...])
        y_ref[...] = y.astype(BF16)
        o_ref[...] = (_silu(y[:, :D_FF]) * y[:, D_FF:]).astype(BF16)

    return pl.pallas_call(
        body, name="conv_f_fwd", grid=(t // tm,),
        in_specs=[_rows(tm, c), _halo(tm, c, rows=16), _full(cw.shape), _full((1, c))],
        out_specs=[_rows(tm, D_FF), _rows(tm, c)],
        out_shape=[jax.ShapeDtypeStruct((t, D_FF), BF16), jax.ShapeDtypeStruct((t, c), BF16)],
        compiler_params=_params(1),
    )(up, up, cw, cb)


def _conv_f_bwd(dact, y, up, cw, tm=128):
    t, c = up.shape
    nt = t // tm

    def body(d_ref, y_ref, x_ref, w_ref, dx_ref, dw_ref, db_ref, nxt_scr):
        @pl.when(pl.program_id(0) == 0)
        def _():
            nxt_scr[...] = jnp.zeros_like(nxt_scr)
            dw_ref[...] = jnp.zeros_like(dw_ref)
            db_ref[...] = jnp.zeros_like(db_ref)

        a, v = y_ref[:, :D_FF].astype(F32), y_ref[:, D_FF:].astype(F32)
        d = d_ref[...].astype(F32)
        silu_a, dsilu_a = _silu_and_grad(a)
        dy = jnp.concatenate([d * v * dsilu_a, d * silu_a], axis=1)
        dx, dw = _causal_conv_bwd(dy, nxt_scr[...], x_ref[...].astype(F32), w_ref[...])
        dx_ref[...] = dx.astype(BF16)
        nxt_scr[...] = dy[:8]
        dw_ref[...] += dw
        db_ref[...] += _colsum(dy)

    return pl.pallas_call(
        body, name="conv_f_bwd", grid=(nt,),
        in_specs=[_rows(tm, D_FF, nt, True), _rows(tm, c, nt, True), _rows(tm, c, nt, True), _full(cw.shape)],
        out_specs=[_rows(tm, c, nt, True), _full(cw.shape), _full((1, c))],
        out_shape=[jax.ShapeDtypeStruct((t, c), BF16), jax.ShapeDtypeStruct(cw.shape, F32),
                   jax.ShapeDtypeStruct((1, c), F32)],
        scratch_shapes=[pltpu.VMEM((8, c), F32)], compiler_params=_params(1),
    )(dact, y, up, cw)


def _conv_a_bwd(dxs, db, dc, y, xbc, cw, tm=256):
    t, c = xbc.shape
    nt = t // tm

    def body(dxs_ref, db_ref, dc_ref, y_ref, x_ref, w_ref, dx_ref, dw_ref, dbias_ref, nxt_scr):
        @pl.when(pl.program_id(0) == 0)
        def _():
            nxt_scr[...] = jnp.zeros_like(nxt_scr)
            dw_ref[...] = jnp.zeros_like(dw_ref)
            dbias_ref[...] = jnp.zeros_like(dbias_ref)

        dy = jnp.concatenate([dxs_ref[...], db_ref[...], dc_ref[...]], axis=1) * _dsilu(y_ref[...].astype(F32))
        dx, dw = _causal_conv_bwd(dy, nxt_scr[...], x_ref[...].astype(F32), w_ref[...])
        dx_ref[...] = dx.astype(BF16)
        nxt_scr[...] = dy[:8]
        dw_ref[...] += dw
        dbias_ref[...] += _colsum(dy)

    return pl.pallas_call(
        body, name="conv_a_bwd", grid=(nt,),
        in_specs=[_rows(tm, SSD_INNER, nt, True), _rows(tm, SSD_BC, nt, True), _rows(tm, SSD_BC, nt, True),
                  _rows(tm, c, nt, True), _rows(tm, c, nt, True), _full(cw.shape)],
        out_specs=[_rows(tm, c, nt, True), _full(cw.shape), _full((1, c))],
        out_shape=[jax.ShapeDtypeStruct((t, c), BF16), jax.ShapeDtypeStruct(cw.shape, F32),
                   jax.ShapeDtypeStruct((1, c), F32)],
        scratch_shapes=[pltpu.VMEM((8, c), F32)], compiler_params=_params(1),
    )(dxs, db, dc, y, xbc, cw)


def _pad_lanes(v, n=DT_PAD):
    return jnp.pad(v, ((0, 0), (0, n - v.shape[1])))


def _local_step(x, target, w, p, after=None, late_weights=None, on_grad=None, on_small=None):
    dtb, alog, dsk = _pad_lanes(p["dt_bias"]), _pad_lanes(p["a_log"]), _pad_lanes(p["d_skip"])
    bs_t = _pad_lanes(p["b_spatial"].T)
    e_heads = (jnp.arange(SSD_INNER)[:, None] // SSD_HEAD_DIM == jnp.arange(LANES)[None, :]).astype(BF16)
    e_heads_t = (jnp.arange(LANES)[:, None] == jnp.arange(SSD_INNER)[None, :] // SSD_HEAD_DIM).astype(BF16)
    e_groups = (jnp.arange(SGU_WIDTH)[:, None] // LANES == jnp.arange(LANES)[None, :]).astype(BF16)

    n1 = _norm_fwd(x, p["norm1_w"], "norm1_fwd", after=after)
    z = _mm(n1, w["z"], "nt", "proj_z", out_dtype=BF16)
    xbc = _mm(n1, w["xbc"], "nt", "proj_xbc", out_dtype=BF16)
    dtr = _mm(n1, w["dt"], "nt", "proj_dt")
    uv = _mm(n1, w["uv"], "nt", "proj_uv", out_dtype=BF16)
    gates = _mm(n1, w["gates"], "nt", "proj_gates", out_dtype=BF16)
    xc, conv_a_out = _conv_a_fwd(xbc, w["conv_a"], p["conv_a_b"])
    y, ya, sprev = _ssd_fwd(xc, dtr, z, dtb, alog, dsk, p["ssd_norm_w"], e_heads_t)
    yb = _sgu_fwd(uv, p["uv_b"], p["v_ln_w"], p["v_ln_b"], p["w_spatial"], bs_t)
    if late_weights is not None:
        w = {**w, **late_weights(ya, yb)}
    narrow = (D_MODEL, BF16)
    pa, pb, mix = _mm_rows(
        [ya, yb], [w["branch_a"], w["branch_b"]], "nn", "branches", _branches_merge, rows=[gates],
        fulls=[p["b_gate"]], row_outs=[narrow] * 3, summed=False)
    wide = [(D_MODEL, F32), (D_MODEL, BF16)]
    h1, n2 = _mm_rows(mix, w["out"], "nn", "out_proj", _residual_norm, rows=[x], fulls=[p["norm2_w"]], row_outs=wide)
    up = _mm(n2, w["up"], "nt", "up_proj", out_dtype=BF16)
    act, conv_f_out = _conv_f_fwd(up, w["conv_f"], p["conv_f_b"])
    dh2, dh2b, loss, g_final = _mm_rows(
        act, w["down"], "nn", "down_proj", _loss_and_grad, rows=[h1, target], fulls=[p["final_norm_w"]],
        row_outs=wide, acc_outs=[(8, LANES), (1, D_MODEL)])

    on_grad = on_grad or (lambda name, grads: None)
    g = {"final_norm_w": g_final}
    g["down"] = _wgrad(act, dh2b, "down_wgrad")
    tok = on_grad("w_down", g)
    dact = _mm(dh2b, w["down"], "nt", "down_dgrad", out_dtype=BF16, after=tok)
    dup, g["conv_f"], g["conv_f_b"] = _conv_f_bwd(dact, conv_f_out, up, w["conv_f"])
    g["up"] = _wgrad(dup, n2, "up_wgrad")
    tok = on_grad("w_up", g)
    dh1, dh1b, g["norm2_w"] = _mm_rows(
        dup, w["up"], "nn", "up_dgrad", _norm_backward, rows=[h1, dh2], fulls=[p["norm2_w"]], row_outs=wide,
        acc_outs=[(1, D_MODEL)], after=tok)
    g["out"] = _wgrad(mix, dh1b, "out_wgrad")
    tok = on_grad("w_out", g)
    dgates, dpa, dpb, g["b_gate"] = _mm_rows(
        dh1b, w["out"], "nt", "out_dgrad", _merge_backward, rows=[gates, pa, pb], fulls=[p["b_gate"]],
        row_outs=[(2 * D_MODEL, BF16), (D_MODEL, BF16), (D_MODEL, BF16)], acc_outs=[(1, 2 * D_MODEL)], after=tok)
    g["branch_a"] = _wgrad(ya, dpa, "branch_a_wgrad")
    g["branch_b"] = _wgrad(yb, dpb, "branch_b_wgrad")
    tok = on_grad("w_branch", g)
    dya, dyb = _mm_rows(
        [dpa, dpb], [w["branch_a"], w["branch_b"]], "nt", "branches_dgrad", lambda products: (tuple(products), ()),
        row_outs=[(SSD_INNER, F32), (SGU_WIDTH, F32)], after=tok, summed=False)
    duv, g["uv_b"], g["v_ln_w"], g["v_ln_b"], g["w_spatial"], dbs_t = _sgu_bwd(
        dyb, uv, p["uv_b"], p["v_ln_w"], p["v_ln_b"], p["w_spatial"], bs_t, e_groups)
    g["b_spatial"] = dbs_t[:, :SGU_GROUPS].T
    dz, dxs, db, dc, ddtr, g["ssd_norm_w"], ddtb, dalog, ddsk = _ssd_bwd(
        dya, y, z, xc, dtr, sprev, dtb, alog, dsk, p["ssd_norm_w"], e_heads, e_heads_t)
    g["dt_bias"], g["a_log"], g["d_skip"] = ddtb, dalog, ddsk
    dxbc, g["conv_a"], g["conv_a_b"] = _conv_a_bwd(dxs, db, dc, conv_a_out, xbc, w["conv_a"])
    tok = on_small(g, loss) if on_small else None
    ddtrb = ddtr.astype(BF16)
    for name, d in (("z", dz), ("xbc", dxbc), ("dt", ddtrb), ("uv", duv), ("gates", dgates)):
        g[name] = _wgrad(d, n1, name + "_wgrad", after=tok)
    tok = on_grad("w_in", g)
    dn1 = _mm([dz, dxbc], [w["z"], w["xbc"]], "nn", "ssd_dgrad", after=tok)
    gx, g["norm1_w"] = _mm_rows(
        [duv, dgates, ddtrb], [w["uv"], w["gates"], w["dt"]], "nn", "in_dgrad",
        lambda r, so_far, h, dres, w_: tuple(t[:1] for t in _norm_backward(r + so_far, h, dres, w_)),
        rows=[dn1, x, dh1], fulls=[p["norm1_w"]], row_outs=wide[:1], acc_outs=[(1, D_MODEL)])
    return loss, gx, g


def _place():
    return lax.axis_index("x"), lax.axis_index("y"), lax.axis_index("c")


def _other_chips(x, y):
    return [(1 - x, y), (x, 1 - y), (1 - x, 1 - y)]


def _all_gather(shards, name):
    n = len(shards)

    def body(*refs):
        ins, outs = refs[:n], refs[n:2 * n]
        send_sems, recv_sems, local_sems = refs[2 * n:]
        x, y, c = _place()
        me, sibling = (x, y, c), (x, y, 1 - c)
        chips = _other_chips(x, y)

        def copy(a, k, block, to, src=None):
            slot = outs[a].at[4 * block[0] + 2 * block[1] + block[2]]
            return pltpu.make_async_remote_copy(
                src_ref=slot if src is None else src, dst_ref=slot, send_sem=send_sems.at[7 * a + k],
                recv_sem=recv_sems.at[7 * a + k], device_id=to, device_id_type=MESH)

        started = []
        for a in range(n):
            mine = pltpu.make_async_copy(ins[a], outs[a].at[4 * x + 2 * y + c], local_sems.at[a])
            mine.start()
            started.append(mine)
        sends = []
        for a in range(n):
            sends.append(copy(a, 0, me, sibling, src=ins[a]))
            sends += [copy(a, 1 + j, me, (*chip, c), src=ins[a]) for j, chip in enumerate(chips)]
        for cp in sends:
            cp.start()
        for a in range(n):
            for j, chip in enumerate(chips):
                copy(a, 1 + j, (*chip, c), me).wait_recv()
                fwd = copy(a, 4 + j, (*chip, c), sibling)
                fwd.start()
                sends.append(fwd)
        for a in range(n):
            copy(a, 0, sibling, me).wait_recv()
            for j, chip in enumerate(chips):
                copy(a, 4 + j, (*chip, 1 - c), me).wait_recv()
        for cp in sends:
            cp.wait_send()
        for mine in started:
            mine.wait()

    any_spec = pl.BlockSpec(memory_space=pl.ANY)
    return pl.pallas_call(
        body, name=name, in_specs=[any_spec] * n, out_specs=[any_spec] * n,
        out_shape=[jax.ShapeDtypeStruct((N_DEV, *s.shape), s.dtype) for s in shards],
        scratch_shapes=[pltpu.SemaphoreType.DMA((7 * n,)), pltpu.SemaphoreType.DMA((7 * n,)),
                        pltpu.SemaphoreType.DMA((n,))],
    )(*shards)


HBM_SPEC = pl.BlockSpec(memory_space=pltpu.HBM)
SEM_SPEC = pl.BlockSpec(memory_space=pltpu.SEMAPHORE)
ANY_SPEC = pl.BlockSpec(memory_space=pl.ANY)
DATAFLOW = pltpu.SideEffectType.DATAFLOW_SIDE_EFFECTING
N_PEERS = N_DEV - 1


def _peers(x, y, c):
    out = []
    for r in range(1, N_DEV):
        fx, fy, fc = r >> 2 & 1, r >> 1 & 1, r & 1
        out.append(((1 - x) if fx else x, (1 - y) if fy else y, (1 - c) if fc else c))
    return out


def _gather_copies(srcs, lands, send_sems, recv_sems, sending, scatter=False):
    x, y, c = _place()
    copies = []
    for a, (src, land) in enumerate(zip(srcs, lands)):
        for j, (px, py, pc) in enumerate(_peers(x, y, c)):
            mine, theirs = 4 * x + 2 * y + c, 4 * px + 2 * py + pc
            block = src.at[theirs if sending else 0] if scatter else src
            copies.append(pltpu.make_async_remote_copy(
                src_ref=block, dst_ref=land.at[mine if sending else theirs], send_sem=send_sems.at[N_PEERS * a + j],
                recv_sem=recv_sems.at[N_PEERS * a + j], device_id=(px, py, pc), device_id_type=MESH))
    return copies


def _gather_start(shards, after, name, scatter=False):
    n = len(shards)
    after = [] if after is None else [after]

    def body(*refs):
        srcs, lands = refs[:n], refs[n:2 * n]
        send_sems, recv_sems = refs[2 * n + len(after):2 * n + len(after) + 2]
        token = refs[-1]
        for cp in _gather_copies(srcs, lands, send_sems, recv_sems, sending=True, scatter=scatter):
            cp.start()
        token[...] = jnp.zeros_like(token)

    lands = [lax.empty(s.shape if scatter else (N_DEV, *s.shape), s.dtype) for s in shards]
    hbm = lambda a: pltpu.with_memory_space_constraint(a, pltpu.HBM)
    out = pl.pallas_call(
        body, name=name,
        out_shape=(pltpu.SemaphoreType.DMA((N_PEERS * n,)), pltpu.SemaphoreType.DMA((N_PEERS * n,)),
                   *[pltpu.HBM(a.shape, a.dtype) for a in (*shards, *lands)], jax.ShapeDtypeStruct((8, LANES), F32)),
        in_specs=[HBM_SPEC] * (2 * n) + [ANY_SPEC] * len(after),
        out_specs=(SEM_SPEC, SEM_SPEC, *[HBM_SPEC] * (2 * n), pl.BlockSpec(memory_space=pltpu.VMEM)),
        input_output_aliases={i: 2 + i for i in range(2 * n)},
        compiler_params=pltpu.CompilerParams(has_side_effects=DATAFLOW),
    )(*[hbm(a) for a in (*shards, *lands)], *after)
    return out[0], out[1], out[2:2 + n], out[2 + n:2 + 2 * n], out[-1]


def _gather_wait(send_sems, recv_sems, shards, lands, after, name, scatter=False):
    n = len(shards)
    after = tuple(after)

    def body(*refs):
        srcs, lands_ = refs[:n], refs[n:2 * n]
        send, recv = refs[2 * n:2 * n + 2]
        for cp in _gather_copies(srcs, lands_, send, recv, sending=False, scatter=scatter):
            cp.wait_send()
            cp.wait_recv()

    out = pl.pallas_call(
        body, name=name, out_shape=tuple(pltpu.HBM(a.shape, a.dtype) for a in (*shards, *lands)),
        in_specs=[HBM_SPEC] * (2 * n) + [SEM_SPEC, SEM_SPEC] + [ANY_SPEC] * len(after),
        out_specs=tuple([HBM_SPEC] * (2 * n)), input_output_aliases={i: i for i in range(2 * n)},
        compiler_params=pltpu.CompilerParams(has_side_effects=DATAFLOW),
    )(*shards, *lands, send_sems, recv_sems, *after)
    return out[:n], out[n:]


def _chip_copies(src, land, send_sems, recv_sems):
    x, y, c = _place()
    return [pltpu.make_async_remote_copy(
        src_ref=src.at[2 * cx + cy], dst_ref=land.at[j], send_sem=send_sems.at[j], recv_sem=recv_sems.at[j],
        device_id=(cx, cy, c), device_id_type=MESH) for j, (cx, cy) in enumerate(_other_chips(x, y))]


def _chips_start(q, name):
    def body(q_ref, land_ref, send_sems, recv_sems, q_thru, land_thru, token):
        for cp in _chip_copies(q_ref, land_ref, send_sems, recv_sems):
            cp.start()
        token[...] = jnp.zeros_like(token)

    land = lax.empty((3, *q.shape[1:]), q.dtype)
    return pl.pallas_call(
        body, name=name,
        out_shape=(pltpu.SemaphoreType.DMA((3,)), pltpu.SemaphoreType.DMA((3,)), pltpu.HBM(q.shape, q.dtype),
                   pltpu.HBM(land.shape, land.dtype), jax.ShapeDtypeStruct((8, LANES), F32)),
        in_specs=[HBM_SPEC, HBM_SPEC],
        out_specs=(SEM_SPEC, SEM_SPEC, HBM_SPEC, HBM_SPEC, pl.BlockSpec(memory_space=pltpu.VMEM)),
        input_output_aliases={0: 2, 1: 3}, compiler_params=pltpu.CompilerParams(has_side_effects=DATAFLOW),
    )(pltpu.with_memory_space_constraint(q, pltpu.HBM), pltpu.with_memory_space_constraint(land, pltpu.HBM))


def _chips_wait(send_sems, recv_sems, q, land, after, name):
    def body(q_ref, land_ref, send, recv, after_ref, q_out, land_out):
        for cp in _chip_copies(q_ref, land_ref, send, recv):
            cp.wait_send()
            cp.wait_recv()

    return pl.pallas_call(
        body, name=name, out_shape=(pltpu.HBM(q.shape, q.dtype), pltpu.HBM(land.shape, land.dtype)),
        in_specs=[HBM_SPEC, HBM_SPEC, SEM_SPEC, SEM_SPEC, ANY_SPEC], out_specs=(HBM_SPEC, HBM_SPEC),
        input_output_aliases={0: 0, 1: 1}, compiler_params=pltpu.CompilerParams(has_side_effects=DATAFLOW),
    )(q, land, send_sems, recv_sems, after)[1]


def _exchange_cores(part, name):
    def body(in_ref, out_ref, send_sems, recv_sems):
        x, y, c = _place()
        copies = [pltpu.make_async_remote_copy(
            src_ref=in_ref.at[2 * k + (1 - c)], dst_ref=out_ref.at[k], send_sem=send_sems.at[k],
            recv_sem=recv_sems.at[k], device_id=(x, y, 1 - c), device_id_type=MESH) for k in range(4)]
        for cp in copies:
            cp.start()
        for cp in copies:
            cp.wait()

    return pl.pallas_call(
        body, name=name, in_specs=[ANY_SPEC], out_specs=ANY_SPEC,
        out_shape=jax.ShapeDtypeStruct((4, *part.shape[1:]), part.dtype),
        scratch_shapes=[pltpu.SemaphoreType.DMA((4,)), pltpu.SemaphoreType.DMA((4,))],
    )(part)


def _chip_sum(part, got, place, name, tr=256):
    _, r, c = part.shape
    tr, tc = _tile2d(r, c, tr)

    def body(place_ref, p_ref, g_ref, q_ref, own_ref):
        s = p_ref[0].astype(F32) + g_ref[0].astype(F32)
        q_ref[0] = s.astype(BF16)

        @pl.when(pl.program_id(2) == place_ref[1])
        def _():
            own_ref[...] = s

    grid_spec = pltpu.PrefetchScalarGridSpec(
        num_scalar_prefetch=1, grid=(r // tr, c // tc, 4),
        in_specs=[pl.BlockSpec((1, tr, tc), lambda i, j, k, pr: (2 * k + pr[0], i, j)),
                  pl.BlockSpec((1, tr, tc), lambda i, j, k, pr: (k, i, j))],
        out_specs=[pl.BlockSpec((1, tr, tc), lambda i, j, k, pr: (k, i, j)),
                   pl.BlockSpec((tr, tc), lambda i, j, k, pr: (i, j))])
    return pl.pallas_call(
        body, name=name, grid_spec=grid_spec,
        out_shape=[jax.ShapeDtypeStruct((4, r, c), BF16), jax.ShapeDtypeStruct((r, c), F32)],
        compiler_params=_params(3),
    )(place, part, got)


def _sum_adamw(own, got, w, m, v, name):
    r, c = own.shape
    tc = 4 * LANES

    def body(own_ref, got_ref, w_ref, m_ref, v_ref, g_ref, d_ref, nm_ref, nv_ref):
        g = own_ref[...]
        for j in range(3):
            g = g + got_ref[j].astype(F32)
        two_d = lambda ref: ref[...].reshape(r, tc)
        delta, nm, nv = _adamw(two_d(w_ref), g, two_d(m_ref), two_d(v_ref))
        for ref, val in ((g_ref, g), (d_ref, delta), (nm_ref, nm), (nv_ref, nv)):
            ref[...] = val.reshape(ref.shape)

    wblk = pl.BlockSpec((r, 1, tc), lambda j: (0, 0, j))
    return pl.pallas_call(
        body, name=name, grid=(c // tc,),
        in_specs=[pl.BlockSpec((r, tc), lambda j: (0, j)), pl.BlockSpec((3, r, tc), lambda j: (0, 0, j)),
                  wblk, wblk, wblk],
        out_specs=[wblk] * 4, out_shape=[jax.ShapeDtypeStruct(w.shape, F32)] * 4, compiler_params=_params(1),
    )(own, got, w, m, v)


def _adamw(w, g, m, v):
    m = ADAM_B1 * m + (1.0 - ADAM_B1) * g
    v = ADAM_B2 * v + (1.0 - ADAM_B2) * jnp.square(g)
    m_hat = m / (1.0 - ADAM_B1 ** ADAM_STEP)
    v_hat = v / (1.0 - ADAM_B2 ** ADAM_STEP)
    return -ADAM_LR * (m_hat / (jnp.sqrt(v_hat) + ADAM_EPS) + ADAM_WD * w), m, v


def _sum8_adamw(part, got, place, w, m, v, name, tr=256):
    r, c = w.shape
    tr, tc = _tile2d(r, c, tr)
    blk = pl.BlockSpec((tr, tc), lambda i, j, pr: (i, j))

    def body(place_ref, own_ref, got_ref, w_ref, m_ref, v_ref, g_ref, d_ref, nm_ref, nv_ref):
        dev = 2 * place_ref[1] + place_ref[0]
        g = jnp.zeros((tr, tc), F32)
        for d in range(N_DEV):
            g = g + jnp.where(dev == d, own_ref[0], got_ref[d]).astype(F32)
        g_ref[...] = g
        d_ref[...], nm_ref[...], nv_ref[...] = _adamw(w_ref[...], g, m_ref[...], v_ref[...])

    grid_spec = pltpu.PrefetchScalarGridSpec(
        num_scalar_prefetch=1, grid=(r // tr, c // tc),
        in_specs=[pl.BlockSpec((1, tr, tc), lambda i, j, pr: (2 * pr[1] + pr[0], i, j)),
                  pl.BlockSpec((N_DEV, tr, tc), lambda i, j, pr: (0, i, j)), blk, blk, blk],
        out_specs=[blk] * 4)
    return pl.pallas_call(
        body, name=name, grid_spec=grid_spec, out_shape=[jax.ShapeDtypeStruct(w.shape, F32)] * 4,
        compiler_params=_params(2),
    )(place, part, got, w, m, v)


VECTORS = ["norm1_w", "b_gate", "conv_a_b", "dt_bias", "a_log", "d_skip", "ssd_norm_w", "uv_b", "v_ln_w", "v_ln_b",
           "norm2_w", "conv_f_b", "final_norm_w"]
SMALL_ORDER = VECTORS + ["w_spatial", "b_spatial", "conv_a_w", "conv_f_w"]


ROW_VECTORS = VECTORS[1:]


def _small_adamw(gathered, w, m, v):
    sizes = {n: w[n].shape[1] for n in ROW_VECTORS}
    offs, off = {}, 0
    for n in ROW_VECTORS:
        offs[n] = off
        off += -(-sizes[n] // LANES) * LANES
    loss_off = off
    k = len(SMALL_ORDER)
    n_g = len(gathered)

    def body(*refs):
        row_ref, ws_ref, bs_ref, ca_ref, cf_ref, n1_ref = refs[:n_g]
        w_refs, m_refs, v_refs = (dict(zip(SMALL_ORDER, refs[n_g + i * k:n_g + (i + 1) * k])) for i in range(3))
        outs = refs[n_g + 3 * k:]
        x, y, c = _place()
        dev = 4 * x + 2 * y + c

        def total(ref):
            s = ref[0]
            for d in range(1, N_DEV):
                s = s + ref[d]
            return s

        row = total(row_ref)
        grads = {n: row[:, offs[n]:offs[n] + sizes[n]] for n in ROW_VECTORS}
        grads["norm1_w"], grads["w_spatial"], grads["b_spatial"] = total(n1_ref), total(ws_ref), total(bs_ref)
        for n, ref in (("conv_a_w", ca_ref), ("conv_f_w", cf_ref)):
            whole, cols = total(ref), w_refs[n].shape[1]
            mine = whole[:, :cols]
            for d in range(1, N_DEV):
                mine = jnp.where(dev == d, whole[:, d * cols:(d + 1) * cols], mine)
            grads[n] = mine
        for i, n in enumerate(SMALL_ORDER):
            outs[4 * i][...] = grads[n]
            outs[4 * i + 1][...], outs[4 * i + 2][...], outs[4 * i + 3][...] = _adamw(
                w_refs[n][...], grads[n], m_refs[n][...], v_refs[n][...])
        outs[4 * k][...] = row[:, loss_off:loss_off + LANES]

    out = pl.pallas_call(
        body, name="adamw_small",
        out_shape=[jax.ShapeDtypeStruct(w[n].shape, F32) for n in SMALL_ORDER for _ in range(4)]
        + [jax.ShapeDtypeStruct((1, LANES), F32)],
        compiler_params=_params(0),
    )(*gathered, *[t[n] for t in (w, m, v) for n in SMALL_ORDER])
    return [dict(zip(SMALL_ORDER, out[j:4 * k:4])) for j in range(4)] + [out[4 * k]]


SMALL = ["norm1_w", "b_gate", "conv_a_b", "dt_bias", "a_log", "d_skip", "ssd_norm_w", "uv_b", "v_ln_w", "v_ln_b",
         "w_spatial", "b_spatial", "norm2_w", "conv_f_b", "final_norm_w"]
BIG = ["w_in", "w_branch", "w_out", "w_up", "w_down"]
TRANSPOSED = ("w_in", "w_up")
WEIGHTS = ["norm1_w", "w_in", "b_gate", "conv_a_w", "conv_a_b", "dt_bias", "a_log", "d_skip", "ssd_norm_w", "uv_b",
           "v_ln_w", "v_ln_b", "w_spatial", "b_spatial", "w_branch", "w_out", "norm2_w", "w_up", "conv_f_w",
           "conv_f_b", "w_down", "final_norm_w"]
IN_SPLITS = [("z", 0, 2048), ("xbc", 2048, 5120), ("dt", 5120, 5152), ("uv", 5152, 7200), ("gates", 7200, 9248)]


def _columns_from_devices(a):
    return a.transpose(1, 0, 2).reshape(a.shape[1], -1)


def kernel(x, norm1_w, w_in, b_gate, conv_a_w, conv_a_b, dt_bias, a_log, d_skip, ssd_norm_w, uv_b, v_ln_w, v_ln_b, w_spatial, b_spatial, w_branch, w_out, norm2_w, w_up, conv_f_w, conv_f_b, w_down, final_norm_w, loss_target, m_norm1_w, m_w_in, m_b_gate, m_conv_a_w, m_conv_a_b, m_dt_bias, m_a_log, m_d_skip, m_ssd_norm_w, m_uv_b, m_v_ln_w, m_v_ln_b, m_w_spatial, m_b_spatial, m_w_branch, m_w_out, m_norm2_w, m_w_up, m_conv_f_w, m_conv_f_b, m_w_down, m_final_norm_w, v_norm1_w, v_w_in, v_b_gate, v_conv_a_w, v_conv_a_b, v_dt_bias, v_a_log, v_d_skip, v_ssd_norm_w, v_uv_b, v_v_ln_w, v_v_ln_b, v_w_spatial, v_b_spatial, v_w_branch, v_w_out, v_norm2_w, v_w_up, v_conv_f_w, v_conv_f_b, v_w_down, v_final_norm_w):
    args = dict(locals())
    wts = {n: args[n] for n in WEIGHTS}
    mom = {n: args["m_" + n] for n in WEIGHTS}
    var = {n: args["v_" + n] for n in WEIGHTS}
    cx, cy, cc = _place()
    dev = 4 * cx + 2 * cy + cc
    place = jnp.stack([cc, 2 * cx + cy]).astype(jnp.int32)

    def shard2d(n, a):
        return a[0].T if n in TRANSPOSED else a[0]

    def unshard(n, b):
        return (b.T if n in TRANSPOSED else b)[None]

    g_in, g_conv_a, g_conv_f = _all_gather(
        [shard2d("w_in", w_in).astype(BF16), conv_a_w[0], conv_f_w[0]], "gather_w_in")
    late = [shard2d(n, wts[n]).astype(BF16) for n in BIG[1:]]
    send_sems, recv_sems, late, lands, token = _gather_start(late, g_in, "gather_late_start")
    w_in_rows = g_in.reshape(-1, D_MODEL)
    w = {name: w_in_rows[lo:hi] for name, lo, hi in IN_SPLITS}
    w["dt"] = jnp.pad(w["dt"], ((0, DT_PAD - SSD_HEADS), (0, 0)))
    w["conv_a"] = _columns_from_devices(g_conv_a)
    w["conv_f"] = _columns_from_devices(g_conv_f)

    def late_weights(*after):
        mine, got = _gather_wait(send_sems, recv_sems, late, lands, after, "gather_late_wait")
        g_branch, g_out, g_up, g_down = [lax.dynamic_update_index_in_dim(land, own, dev, 0).reshape(-1, D_MODEL)
                                         for land, own in zip(got, mine)]
        return {"branch_a": g_branch[:SSD_INNER], "branch_b": g_branch[SSD_INNER:], "out": g_out, "up": g_up,
                "down": g_down}

    in_flight = {}

    def on_grad(n, g):
        part = {"w_in": lambda: jnp.concatenate([g[name][:hi - lo] for name, lo, hi in IN_SPLITS], axis=0),
                "w_branch": lambda: jnp.concatenate([g["branch_a"], g["branch_b"]], axis=0),
                "w_out": lambda: g["out"], "w_up": lambda: g["up"], "w_down": lambda: g["down"]}[n]()
        part = part.reshape(N_DEV, -1, D_MODEL)
        if n == "w_in":
            q, own = _chip_sum(part, _exchange_cores(part, "to_other_core_w_in"), place, "chip_sum_w_in")
            send, recv, q, land, tok = _chips_start(q, "to_other_chips_start_w_in")
            in_flight[n] = (own, send, recv, q, land)
            return tok
        send, recv, (part,), (land,), tok = _gather_start([part], None, f"to_owners_start_{n}", scatter=True)
        in_flight[n] = (part, send, recv, land)
        return tok

    p = {n: wts[n][0] if wts[n].ndim > 2 else wts[n].reshape(1, -1) for n in SMALL}
    small_flight = []

    def on_small(g, loss):
        arrays = [jnp.concatenate([g[n] for n in ROW_VECTORS] + [loss[:1]], axis=1), g["w_spatial"], g["b_spatial"],
                  g["conv_a"], g["conv_f"]]
        *flight, tok = _gather_start(arrays, g["conv_a"], "gather_small_start")
        small_flight.append(flight)
        return tok

    loss, gx, g = _local_step(x[0], loss_target[0], w, p, after=token, late_weights=late_weights, on_grad=on_grad,
                              on_small=on_small)
    *flight, _ = _gather_start([g["norm1_w"]], gx, "gather_norm1_start")
    small_flight.append(flight)

    grads, delta, new_m, new_v = {}, {}, {}, {}

    def big_adamw(n, after):
        if n == "w_in":
            own, send, recv, q, land = in_flight[n]
            got = _chips_wait(send, recv, q, land, after, "to_other_chips_wait_w_in")
            out = _sum_adamw(own, got, *[t[n].transpose(2, 0, 1) for t in (wts, mom, var)], "adamw_w_in")
            grads[n], delta[n], new_m[n], new_v[n] = [o.transpose(1, 2, 0) for o in out]
            return out[1]
        part, send, recv, land = in_flight[n]
        (part,), (got,) = _gather_wait(send, recv, [part], [land], [after], f"to_owners_wait_{n}", scatter=True)
        out = _sum8_adamw(part, got, place, *[shard2d(n, t[n]) for t in (wts, mom, var)], f"adamw_{n}")
        grads[n], delta[n], new_m[n], new_v[n] = [unshard(n, o) for o in out]
        return out[1]

    after = gx
    for n in ("w_down", "w_up", "w_out", "w_branch"):
        after = big_adamw(n, after)
    gathered = []
    for (send, recv, mine, land), name in zip(small_flight, ("gather_small_wait", "gather_norm1_wait")):
        mine, got = _gather_wait(send, recv, mine, land, [after], name)
        gathered += [lax.dynamic_update_index_in_dim(full, own, dev, 0) for full, own in zip(got, mine)]
    small = [{n: t[n][0] if t[n].ndim > 2 else t[n].reshape(1, -1) for n in SMALL_ORDER} for t in (wts, mom, var)]
    *outs, loss = _small_adamw(gathered, *small)
    for tgt, out in zip((grads, delta, new_m, new_v), outs):
        tgt.update({n: out[n].reshape(wts[n].shape) for n in SMALL_ORDER})
    big_adamw("w_in", loss)
    loss = loss[0, 0]

    return (loss, gx[None], *[grads[n] for n in WEIGHTS], *[delta[n] for n in WEIGHTS],
            *[new_m[n] for n in WEIGHTS], *[new_v[n] for n in WEIGHTS])
```

```python
import functools

import jax
import jax.numpy as jnp
from jax import lax
from jax.experimental import pallas as pl
from jax.experimental.pallas import tpu as pltpu

F32, BF16 = jnp.float32, jnp.bfloat16
HIGHEST = lax.Precision.HIGHEST

D_MODEL = 1024
SSD_INNER = 2048
SSD_HEAD_DIM = 64
SSD_HEADS = 32
SSD_GROUPS = 4
SSD_STATE = 128
SSD_BC = SSD_GROUPS * SSD_STATE
SSD_XBC = SSD_INNER + 2 * SSD_BC
SSD_CONV = 4
CHUNK = 128
N_PAIRS = SSD_HEADS // 2
PAIRS_PER_GROUP = N_PAIRS // SSD_GROUPS
SGU_WIDTH = 1024
SGU_GROUPS = 8
SGU_TILE = 256
D_FF = 2816
FFN_CONV = 3
NORM_EPS = 1e-6
LN_EPS = 1e-5
LANES = 128
DT_PAD = LANES

ADAM_LR, ADAM_B1, ADAM_B2, ADAM_EPS, ADAM_WD, ADAM_STEP = 0.001, 0.9, 0.999, 1e-08, 0.01, 10

N_DEV = 8
VMEM_LIMIT = 56 * 1024 * 1024
MESH = pl.DeviceIdType.MESH


def _params(n_grid, **kw):
    sem = dict(dimension_semantics=("arbitrary",) * n_grid) if n_grid else {}
    return pltpu.CompilerParams(vmem_limit_bytes=VMEM_LIMIT, **sem, **kw)


def _tile(n, pref):
    t = (min(pref, n) // LANES) * LANES
    while n % t:
        t -= LANES
    return t


def _row_tile(r, pref):
    for t in range(min(pref, r) // 16 * 16, 0, -16):
        if r % t == 0:
            return t
    return r


def _tile2d(r, c, rows):
    if r % 16 == 0:
        return _row_tile(r, rows), c
    return r, _tile(c, 2 * LANES)


def _rows(tm, n, nt=None, rev=False):
    if rev:
        return pl.BlockSpec((tm, n), lambda i: (nt - 1 - i, 0))
    return pl.BlockSpec((tm, n), lambda i: (i, 0))


def _halo(tm, n, rows=8):
    per = tm // rows
    return pl.BlockSpec((rows, n), lambda i: (jnp.maximum(i * per - 1, 0), 0))


def _full(shape):
    nd = len(shape)
    return pl.BlockSpec(shape, lambda *_: (0,) * nd)


def _rms(x, w, eps=NORM_EPS):
    return x * lax.rsqrt(jnp.mean(x * x, axis=-1, keepdims=True) + eps) * w


def _layer_norm(x, w, b):
    mu = jnp.mean(x, axis=-1, keepdims=True)
    var = jnp.mean(jnp.square(x - mu), axis=-1, keepdims=True)
    return (x - mu) * lax.rsqrt(var + LN_EPS) * w + b


def _sigmoid(x):
    return 1.0 / (1.0 + jnp.exp(-x))


def _silu(x):
    return x * _sigmoid(x)


def _dsilu(x):
    s = _sigmoid(x)
    return s * (1.0 + x * (1.0 - s))


def _silu_and_grad(x):
    s = _sigmoid(x)
    return x * s, s * (1.0 + x * (1.0 - s))


def _softplus(x):
    return jnp.maximum(x, 0.0) + jnp.log(1.0 + jnp.exp(-jnp.abs(x)))


def _gelu(x):
    return jax.nn.gelu(x)


def _dot(a, b):
    return jnp.dot(a, b, preferred_element_type=F32)


def _dot_nt(a, b):
    return lax.dot_general(a, b, (((1,), (1,)), ((), ())), preferred_element_type=F32)


def _dot_tn(a, b):
    return lax.dot_general(a, b, (((0,), (0,)), ((), ())), preferred_element_type=F32)


def _dot_split(p, e):
    hi = p.astype(BF16)
    lo = (p - hi.astype(F32)).astype(BF16)
    return _dot(hi, e) + _dot(lo, e)


def _colsum(x):
    return jnp.sum(x, axis=0, keepdims=True)


def _shift_down(x, halo, j):
    xs = pltpu.roll(x, j, 0)
    hs = pltpu.roll(halo, j, 0)
    r8 = lax.broadcasted_iota(jnp.int32, hs.shape, 0)
    return jnp.concatenate([jnp.where(r8 < j, hs, xs[:8]), xs[8:]], axis=0)


def _shift_up(x, nxt, j):
    n = x.shape[0]
    xs = pltpu.roll(x, n - j, 0)
    ns = pltpu.roll(nxt, 8 - j, 0)
    r8 = lax.broadcasted_iota(jnp.int32, ns.shape, 0)
    return jnp.concatenate([xs[:n - 8], jnp.where(r8 >= 8 - j, ns, xs[n - 8:])], axis=0)


def _causal_conv(x, halo, w, b):
    k = w.shape[0]
    y = b + w[k - 1:k, :] * x
    for j in range(1, k):
        y = y + w[k - 1 - j:k - j, :] * _shift_down(x, halo, j)
    return y


def _causal_conv_bwd(dy, nxt, x, w):
    k = w.shape[0]
    dx = w[k - 1:k, :] * dy
    dw = [_colsum(dy * x)]
    for j in range(1, k):
        dyj = _shift_up(dy, nxt, j)
        dx = dx + w[k - 1 - j:k - j, :] * dyj
        dw.append(_colsum(dyj * x))
    return dx, jnp.concatenate(dw[::-1], axis=0)


MM_TILE_PREF = 1408
MM_VMEM_BUDGET = 40 * 1024 * 1024


def _mm_tiles(m, n, k, out_bytes):
    tm, tn = _tile(m, MM_TILE_PREF), _tile(n, MM_TILE_PREF)
    need = lambda tm, tn: 2 * (2 * k * (tm + tn) + out_bytes * tm * tn)
    while need(tm, tn) > MM_VMEM_BUDGET:
        if tn >= tm and tn > LANES:
            tn = _tile(n, tn - LANES)
        else:
            tm = _tile(m, tm - LANES)
    return tm, tn


def _mm(a, b, dims, name, acc=None, out_dtype=F32, after=None):
    a_list, b_list = (list(a), list(b)) if isinstance(a, (list, tuple)) else ([a], [b])
    k_axis, m_axis = (0, 1) if dims == "tn" else (1, 0)
    m, ks = a_list[0].shape[m_axis], [x.shape[k_axis] for x in a_list]
    n = b_list[0].shape[0] if dims == "nt" else b_list[0].shape[1]
    tm, tn = _mm_tiles(m, n, sum(ks), 4 * (2 if acc is not None else 1))
    a_specs = [pl.BlockSpec((k, tm), lambda j, i: (0, i)) if dims == "tn" else pl.BlockSpec((tm, k), lambda j, i: (i, 0))
               for k in ks]
    b_specs = [pl.BlockSpec((tn, k), lambda j, i: (j, 0)) if dims == "nt" else pl.BlockSpec((k, tn), lambda j, i: (0, j))
               for k in ks]
    o_spec = pl.BlockSpec((tm, tn), lambda j, i: (i, j))
    dot = {"nn": _dot, "nt": _dot_nt, "tn": _dot_tn}[dims]
    n_pairs = len(ks)

    def body(*refs):
        rest = refs[2 * n_pairs:]
        r = dot(refs[0][...], refs[n_pairs][...])
        for i in range(1, n_pairs):
            r = r + dot(refs[i][...], refs[n_pairs + i][...])
        if acc is not None:
            r = r + rest[0][...]
        rest[-1][...] = r.astype(out_dtype)

    ins, specs = a_list + b_list, a_specs + b_specs
    if acc is not None:
        ins.append(acc)
        specs.append(o_spec)
    if after is not None:
        ins.append(after)
        specs.append(pl.BlockSpec(memory_space=pl.ANY))
    return pl.pallas_call(
        body, name=name, grid=(n // tn, m // tm), in_specs=specs, out_specs=o_spec,
        out_shape=jax.ShapeDtypeStruct((m, n), out_dtype), compiler_params=_params(2),
    )(*ins)


def _mm_rows(a, b, dims, name, fn, rows=(), fulls=(), row_outs=(), acc_outs=(), after=None, summed=True):
    a_list, b_list = (list(a), list(b)) if isinstance(a, (list, tuple)) else ([a], [b])
    m, ks = a_list[0].shape[0], [x.shape[1] for x in a_list]
    n, k = (b_list[0].shape[0] if dims == "nt" else b_list[0].shape[1]), sum(ks)
    per_row = 2 * k + 8 * n + sum(4 * r.shape[1] for r in rows) + sum(c * jnp.dtype(d).itemsize for c, d in row_outs)
    tm = _tile(m, 1024)
    while 2 * tm * per_row + 4 * k * n > MM_VMEM_BUDGET:
        tm = _tile(m, tm - LANES)
    dot = _dot_nt if dims == "nt" else _dot
    n_pairs = len(ks)
    n_in = 2 * n_pairs + len(rows) + len(fulls) + (after is not None)

    def body(*refs):
        ins, outs = refs[:n_in], refs[n_in:]
        row_refs, acc_refs = outs[:len(row_outs)], outs[len(row_outs):]

        @pl.when(pl.program_id(0) == 0)
        def _():
            for r in acc_refs:
                r[...] = jnp.zeros_like(r)

        products = [dot(ins[i][...], ins[n_pairs + i][...]) for i in range(n_pairs)]
        result = functools.reduce(lambda p, q: p + q, products) if summed else products
        new_rows, incs = fn(result, *[r[...] for r in ins[2 * n_pairs:2 * n_pairs + len(rows) + len(fulls)]])
        for r, val in zip(row_refs, new_rows):
            r[...] = val.astype(r.dtype)
        for r, inc in zip(acc_refs, incs):
            r[...] += inc

    extra, extra_specs = ([after], [pl.BlockSpec(memory_space=pl.ANY)]) if after is not None else ([], [])
    return pl.pallas_call(
        body, name=name, grid=(m // tm,),
        in_specs=[_rows(tm, k_i) for k_i in ks] + [_full(x.shape) for x in b_list]
        + [_rows(tm, r.shape[1]) for r in rows] + [_full(f.shape) for f in fulls] + extra_specs,
        out_specs=[_rows(tm, c) for c, _ in row_outs] + [_full(s) for s in acc_outs],
        out_shape=[jax.ShapeDtypeStruct((m, c), d) for c, d in row_outs]
        + [jax.ShapeDtypeStruct(s, F32) for s in acc_outs],
        compiler_params=_params(1),
    )(*a_list, *b_list, *rows, *fulls, *extra)


def _residual_norm(o, x, w):
    h = x + o
    return (h, _rms(h, w)), ()


def _norm_backward(dn, h, dres, w):
    _, vjp = jax.vjp(_rms, h, w)
    dh, dw = vjp(dn)
    dh = dh + dres
    return (dh, dh), (dw,)


def _loss_and_grad(dn, h1, target, w):
    yf, vjp = jax.vjp(_rms, h1 + dn, w)
    err = yf - target
    loss = 0.5 * jnp.sum(jnp.mean(err * err, axis=-1, keepdims=True))
    dh, dw = vjp(err * (1.0 / err.shape[-1]))
    return (dh, dh), (jnp.full((8, LANES), loss, F32), dw)


def _wgrad(a, d, name, after=None):
    return _mm(a, d, "tn", name, out_dtype=BF16, after=after)


def _norm_fwd(x, w, name, after=None, tm=512):
    t, d = x.shape

    def body(x_ref, w_ref, *rest):
        rest[-1][...] = _rms(x_ref[...], w_ref[...]).astype(BF16)

    extra, extra_specs = ([after], [_full(after.shape)]) if after is not None else ([], [])
    return pl.pallas_call(
        body, name=name, grid=(t // tm,), in_specs=[_rows(tm, d), _full((1, d))] + extra_specs,
        out_specs=_rows(tm, d), out_shape=jax.ShapeDtypeStruct((t, d), BF16), compiler_params=_params(1),
    )(x, w, *extra)


def _conv_a_fwd(xbc, cw, cb, tm=512):
    t, c = xbc.shape

    def body(x_ref, h_ref, w_ref, b_ref, o_ref, y_ref):
        halo = jnp.where(pl.program_id(0) > 0, h_ref[...].astype(F32)[8:], 0.0)
        y = _causal_conv(x_ref[...].astype(F32), halo, w_ref[...], b_ref[...])
        y_ref[...] = y.astype(BF16)
        o_ref[...] = _silu(y)

    return pl.pallas_call(
        body, name="conv_a_fwd", grid=(t // tm,),
        in_specs=[_rows(tm, c), _halo(tm, c, rows=16), _full(cw.shape), _full((1, c))],
        out_specs=[_rows(tm, c)] * 2,
        out_shape=[jax.ShapeDtypeStruct((t, c), F32), jax.ShapeDtypeStruct((t, c), BF16)], compiler_params=_params(1),
    )(xbc, xbc, cw, cb)


def _ssd_common(dtr, dtb, alog, e_t):
    row = lax.broadcasted_iota(jnp.int32, (CHUNK, CHUNK), 0)
    col = lax.broadcasted_iota(jnp.int32, (CHUNK, CHUNK), 1)
    causal = row >= col
    dt = _softplus(dtr + dtb)
    a = -jnp.exp(alog)
    acum = jnp.dot(causal.astype(F32), dt * a, precision=HIGHEST, preferred_element_type=F32)
    spread = lambda v: _dot(v.astype(BF16), e_t)
    elast = jnp.broadcast_to(jnp.exp(acum[CHUNK - 1:CHUNK, :]), (8, LANES))
    return dict(dt=dt, a=a, acum=acum, acum_t=acum.T, causal=causal, row=row, col=col, lane_lo=col < SSD_HEAD_DIM,
                dt_x=_dot_split(dt, e_t), ecol_x=spread(jnp.exp(acum)), elast_x=_dot_split(elast, e_t)[0:1],
                dsr_x=spread(jnp.exp(acum[CHUNK - 1:CHUNK, :] - acum)))


def _head_decay(c, h, transposed=False):
    d = c["acum"][:, h:h + 1] - c["acum_t"][h:h + 1, :]
    if transposed:
        return jnp.exp(jnp.where(c["row"] <= c["col"], -d, -jnp.inf))
    return jnp.exp(jnp.where(c["causal"], d, -jnp.inf))


def _ssd_fwd(xc, dtr, z, dtb, alog, dsk, nw, e_t):
    t = xc.shape[0]
    nc = t // CHUNK

    def body(xs_ref, b_ref, c_ref, dtr_ref, z_ref, dtb_ref, alog_ref, dsk_ref, nw_ref, et_ref,
             y_ref, ya_ref, sp_ref, s_scr):
        @pl.when(pl.program_id(0) == 0)
        def _():
            s_scr[...] = jnp.zeros_like(s_scr)

        c = _ssd_common(dtr_ref[...], dtb_ref[...], alog_ref[...], et_ref[...])
        lane_lo = c["lane_lo"]
        dsk = dsk_ref[...]
        for g in range(SSD_GROUPS):
            gs = slice(g * SSD_STATE, (g + 1) * SSD_STATE)
            bg_t, cg = b_ref[:, gs].T.astype(BF16), c_ref[:, gs].astype(BF16)
            cb = _dot(cg, bg_t)
            for pp in range(PAIRS_PER_GROUP):
                j = g * PAIRS_PER_GROUP + pp
                ps = slice(j * LANES, (j + 1) * LANES)
                x = xs_ref[:, ps]
                ecol, dsr = c["ecol_x"][:, ps], c["dsr_x"][:, ps]
                xdt = x * c["dt_x"][:, ps]
                xb = xdt.astype(BF16)
                zero = jnp.zeros_like(xb)
                yd = (_dot((cb * _head_decay(c, 2 * j)).astype(BF16), jnp.where(lane_lo, xb, zero))
                      + _dot((cb * _head_decay(c, 2 * j + 1)).astype(BF16), jnp.where(lane_lo, zero, xb)))
                sp = s_scr[j]
                yo = ecol * _dot(cg, sp.astype(BF16))
                st = _dot(bg_t, (xdt * dsr).astype(BF16))
                sp_ref[0, j] = sp
                s_scr[j] = c["elast_x"][:, ps] * sp + st
                dskp = jnp.where(lane_lo[0:1], dsk[:, 2 * j:2 * j + 1], dsk[:, 2 * j + 1:2 * j + 2])
                y_ref[:, ps] = yd + yo + dskp * x
        ya_ref[...] = _rms(y_ref[...] * _silu(z_ref[...].astype(F32)), nw_ref[...]).astype(BF16)

    ck = lambda n, col=0: pl.BlockSpec((CHUNK, n), lambda c: (c, col))
    return pl.pallas_call(
        body, name="ssd_fwd", grid=(nc,),
        in_specs=[ck(SSD_INNER), ck(SSD_BC, SSD_INNER // SSD_BC), ck(SSD_BC, SSD_INNER // SSD_BC + 1), ck(DT_PAD),
                  ck(SSD_INNER), _full((1, DT_PAD)), _full((1, DT_PAD)), _full((1, DT_PAD)),
                  _full((1, SSD_INNER)), _full(e_t.shape)],
        out_specs=[ck(SSD_INNER), ck(SSD_INNER),
                   pl.BlockSpec((1, N_PAIRS, SSD_STATE, LANES), lambda c: (c, 0, 0, 0))],
        out_shape=[jax.ShapeDtypeStruct((t, SSD_INNER), F32), jax.ShapeDtypeStruct((t, SSD_INNER), BF16),
                   jax.ShapeDtypeStruct((nc, N_PAIRS, SSD_STATE, LANES), F32)],
        scratch_shapes=[pltpu.VMEM((N_PAIRS, SSD_STATE, LANES), F32)], compiler_params=_params(1),
    )(xc, xc, xc, dtr, z, dtb, alog, dsk, nw, e_t)


def _ssd_bwd(dya, y, z, xc, dtr, sprev, dtb, alog, dsk, nw, e_heads, e_t):
    t = xc.shape[0]
    nc = t // CHUNK

    def body(dya_ref, y_ref, z_ref, xs_ref, b_ref, c_ref, dtr_ref, sp_ref, dtb_ref, alog_ref, dsk_ref, nw_ref, e_ref,
             et_ref, dz_ref, dxs_ref, db_ref, dc_ref, ddtr_ref, dnw_ref, ddtb_ref, dalog_ref, ddsk_ref, ds_scr):
        @pl.when(pl.program_id(0) == 0)
        def _():
            ds_scr[...] = jnp.zeros_like(ds_scr)
            for r in (dnw_ref, ddtb_ref, dalog_ref, ddsk_ref):
                r[...] = jnp.zeros_like(r)

        y = y_ref[...]
        _, gate_vjp = jax.vjp(lambda y_, z_, w_: _rms(y_ * _silu(z_), w_), y, z_ref[...].astype(F32), nw_ref[...])
        dy, dz, dnw = gate_vjp(dya_ref[...])
        dz_ref[...] = dz.astype(BF16)
        dnw_ref[...] += dnw

        dtr = dtr_ref[...]
        c = _ssd_common(dtr, dtb_ref[...], alog_ref[...], et_ref[...])
        dt, a, lane_lo, row, col = c["dt"], c["a"], c["lane_lo"], c["row"], c["col"]
        dsk = dsk_ref[...]
        p_a, p_dt, v_last = [], [], []
        da_cols = jnp.zeros((CHUNK, CHUNK), F32)
        da_rows = jnp.zeros((CHUNK, CHUNK), F32)
        for g in range(SSD_GROUPS):
            gs = slice(g * SSD_STATE, (g + 1) * SSD_STATE)
            bg, cg = b_ref[:, gs].astype(BF16), c_ref[:, gs].astype(BF16)
            bg_t, cg_t = b_ref[:, gs].T.astype(BF16), c_ref[:, gs].T.astype(BF16)
            cb, cb_t = _dot(cg, bg_t), _dot(bg, cg_t)
            dcb = jnp.zeros((CHUNK, CHUNK), F32)
            dbg = jnp.zeros((CHUNK, SSD_STATE), F32)
            dcg = jnp.zeros((CHUNK, SSD_STATE), F32)
            for pp in range(PAIRS_PER_GROUP):
                j = g * PAIRS_PER_GROUP + pp
                ps = slice(j * LANES, (j + 1) * LANES)
                x = xs_ref[:, ps]
                dtp, ecol, dsr = c["dt_x"][:, ps], c["ecol_x"][:, ps], c["dsr_x"][:, ps]
                elast = c["elast_x"][:, ps]
                xdt = x * dtp
                xb = xdt.astype(BF16)
                dskp = jnp.where(lane_lo[0:1], dsk[:, 2 * j:2 * j + 1], dsk[:, 2 * j + 1:2 * j + 2])
                dyp = dy[:, ps]
                dyb = dyp.astype(BF16)
                sp, dsn = sp_ref[0, j], ds_scr[j]
                spb, dsnb = sp.astype(BF16), dsn.astype(BF16)
                y_off = ecol * _dot(cg, spb)
                dw = (dyp * ecol).astype(BF16)
                dcg = dcg + _dot_nt(dw, spb)
                dsp = _dot(cg_t, dw) + elast * dsn
                xd = xdt * dsr
                zd = _dot(bg, dsnb) * dsr
                dbg = dbg + _dot_nt(xd.astype(BF16), dsnb)
                dxdt = zd
                zero = jnp.zeros_like(xb)
                for h, lm in ((2 * j, lane_lo), (2 * j + 1, jnp.logical_not(lane_lo))):
                    le = _head_decay(c, h)
                    dm = _dot_nt(jnp.where(lm, dyb, zero), jnp.where(lm, xb, zero))
                    dcb = dcb + dm * le
                    m = cb * le
                    m_t = (cb_t * _head_decay(c, h, transposed=True)).astype(BF16)
                    dxdt = dxdt + jnp.where(lm, _dot(m_t, dyb), 0.0)
                    q = dm * m
                    da_cols = da_cols + jnp.where(col == h, jnp.sum(q, axis=1, keepdims=True), 0.0)
                    da_rows = da_rows + jnp.where(row == h, _colsum(q), 0.0)
                ds_scr[j] = dsp
                dxs_ref[:, ps] = dxdt * dtp + dskp * dyp
                p_a.append(dyp * y_off - xdt * zd)
                p_dt.append(dxdt * x)
                v_last.append(_colsum(zd * xdt) + elast * _colsum(dsn * sp))
            dcbb = dcb.astype(BF16)
            db_ref[:, gs] = dbg + _dot_tn(dcbb, cg)
            dc_ref[:, gs] = dcg + _dot(dcbb, bg)
        e = e_ref[...]
        rows8 = jnp.concatenate([jnp.concatenate(v_last, axis=1), _colsum(dy * xs_ref[...]),
                                 jnp.zeros((6, SSD_INNER), F32)], axis=0)
        r8 = _dot_split(rows8, e)
        da = (_dot_split(jnp.concatenate(p_a, axis=1), e) + jnp.where(row == CHUNK - 1, r8[0:1], 0.0)
              + da_cols - da_rows.T)
        ddsk_ref[...] += r8[1:2]
        dadt = jnp.dot((row <= col).astype(F32), da, precision=HIGHEST, preferred_element_type=F32)
        ddt = dadt * a + _dot_split(jnp.concatenate(p_dt, axis=1), e)
        dalog_ref[...] += _colsum(dadt * dt) * a
        ddtr = ddt * _sigmoid(dtr + dtb_ref[...])
        ddtr_ref[...] = ddtr
        ddtb_ref[...] += _colsum(ddtr)

    ck = lambda n, col=0: pl.BlockSpec((CHUNK, n), lambda c: (nc - 1 - c, col))
    acc = lambda n: _full((1, n))
    return pl.pallas_call(
        body, name="ssd_bwd", grid=(nc,),
        in_specs=[ck(SSD_INNER), ck(SSD_INNER), ck(SSD_INNER), ck(SSD_INNER), ck(SSD_BC, SSD_INNER // SSD_BC),
                  ck(SSD_BC, SSD_INNER // SSD_BC + 1), ck(DT_PAD),
                  pl.BlockSpec((1, N_PAIRS, SSD_STATE, LANES), lambda c: (nc - 1 - c, 0, 0, 0)),
                  acc(DT_PAD), acc(DT_PAD), acc(DT_PAD), acc(SSD_INNER), _full((SSD_INNER, LANES)),
                  _full((LANES, SSD_INNER))],
        out_specs=[ck(SSD_INNER), ck(SSD_INNER), ck(SSD_BC), ck(SSD_BC), ck(DT_PAD),
                   acc(SSD_INNER), acc(DT_PAD), acc(DT_PAD), acc(DT_PAD)],
        out_shape=[jax.ShapeDtypeStruct((t, SSD_INNER), BF16), jax.ShapeDtypeStruct((t, SSD_INNER), F32),
                   jax.ShapeDtypeStruct((t, SSD_BC), F32), jax.ShapeDtypeStruct((t, SSD_BC), F32),
                   jax.ShapeDtypeStruct((t, DT_PAD), F32), jax.ShapeDtypeStruct((1, SSD_INNER), F32),
                   jax.ShapeDtypeStruct((1, DT_PAD), F32), jax.ShapeDtypeStruct((1, DT_PAD), F32),
                   jax.ShapeDtypeStruct((1, DT_PAD), F32)],
        scratch_shapes=[pltpu.VMEM((N_PAIRS, SSD_STATE, LANES), F32)], compiler_params=_params(1),
    )(dya, y, z, xc, xc, xc, dtr, sprev, dtb, alog, dsk, nw, e_heads, e_t)


def _sgu_act(uv, uvb, lnw, lnb):
    a = _gelu(uv + uvb)
    return a[:, :SGU_WIDTH], _layer_norm(a[:, SGU_WIDTH:], lnw, lnb)


def _sgu_weights(ws_ref):
    row = lax.broadcasted_iota(jnp.int32, (CHUNK, CHUNK), 0)
    col = lax.broadcasted_iota(jnp.int32, (CHUNK, CHUNK), 1)
    return [jnp.where(row >= col, ws_ref[g], 0.0).astype(BF16) for g in range(SGU_GROUPS)], row >= col


def _sgu_fwd(uv, uvb, lnw, lnb, ws, bs_t):
    t = uv.shape[0]

    def body(uv_ref, uvb_ref, lnw_ref, lnb_ref, ws_ref, bs_ref, o_ref):
        u, vn = _sgu_act(uv_ref[...].astype(F32), uvb_ref[...], lnw_ref[...], lnb_ref[...])
        wc, _ = _sgu_weights(ws_ref)
        bs = bs_ref[...]
        for ck in range(SGU_TILE // CHUNK):
            rs = slice(ck * CHUNK, (ck + 1) * CHUNK)
            for g in range(SGU_GROUPS):
                gs = slice(g * LANES, (g + 1) * LANES)
                mixed = _dot(wc[g], vn[rs, gs].astype(BF16)) + bs[:, g:g + 1]
                o_ref[rs, gs] = (u[rs, gs] * mixed).astype(BF16)

    return pl.pallas_call(
        body, name="sgu_fwd", grid=(t // SGU_TILE,),
        in_specs=[_rows(SGU_TILE, 2 * SGU_WIDTH), _full((1, 2 * SGU_WIDTH)), _full((1, SGU_WIDTH)),
                  _full((1, SGU_WIDTH)), _full(ws.shape), _full(bs_t.shape)],
        out_specs=_rows(SGU_TILE, SGU_WIDTH), out_shape=jax.ShapeDtypeStruct((t, SGU_WIDTH), BF16),
        compiler_params=_params(1),
    )(uv, uvb, lnw, lnb, ws, bs_t)


def _sgu_bwd(dyb, uv, uvb, lnw, lnb, ws, bs_t, e_groups):
    t = uv.shape[0]

    def body(dyb_ref, uv_ref, uvb_ref, lnw_ref, lnb_ref, ws_ref, bs_ref, e_ref,
             duv_ref, duvb_ref, dlnw_ref, dlnb_ref, dws_ref, dbs_ref):
        @pl.when(pl.program_id(0) == 0)
        def _():
            for r in (duvb_ref, dlnw_ref, dlnb_ref, dws_ref, dbs_ref):
                r[...] = jnp.zeros_like(r)

        (u, vn), act_vjp = jax.vjp(_sgu_act, uv_ref[...].astype(F32), uvb_ref[...], lnw_ref[...], lnb_ref[...])
        wc, causal = _sgu_weights(ws_ref)
        bs = bs_ref[...]
        dyb = dyb_ref[...]
        du_rows, dvn_rows = [], []
        for ck in range(SGU_TILE // CHUNK):
            rs = slice(ck * CHUNK, (ck + 1) * CHUNK)
            du, dvn, dmix = [], [], []
            for g in range(SGU_GROUPS):
                gs = slice(g * LANES, (g + 1) * LANES)
                vb = vn[rs, gs].astype(BF16)
                mixed = _dot(wc[g], vb) + bs[:, g:g + 1]
                dm = dyb[rs, gs] * u[rs, gs]
                dmb = dm.astype(BF16)
                du.append(dyb[rs, gs] * mixed)
                dvn.append(_dot_tn(wc[g], dmb))
                dws_ref[g] += jnp.where(causal, _dot_nt(dmb, vb), 0.0)
                dmix.append(dm)
            dbs_ref[...] += _dot_split(jnp.concatenate(dmix, axis=1), e_ref[...])
            du_rows.append(jnp.concatenate(du, axis=1))
            dvn_rows.append(jnp.concatenate(dvn, axis=1))
        duv, duvb, dlnw, dlnb = act_vjp((jnp.concatenate(du_rows, axis=0), jnp.concatenate(dvn_rows, axis=0)))
        duv_ref[...] = duv.astype(BF16)
        duvb_ref[...] += duvb
        dlnw_ref[...] += dlnw
        dlnb_ref[...] += dlnb

    return pl.pallas_call(
        body, name="sgu_bwd", grid=(t // SGU_TILE,),
        in_specs=[_rows(SGU_TILE, SGU_WIDTH), _rows(SGU_TILE, 2 * SGU_WIDTH), _full((1, 2 * SGU_WIDTH)),
                  _full((1, SGU_WIDTH)), _full((1, SGU_WIDTH)), _full(ws.shape), _full(bs_t.shape),
                  _full(e_groups.shape)],
        out_specs=[_rows(SGU_TILE, 2 * SGU_WIDTH), _full((1, 2 * SGU_WIDTH)), _full((1, SGU_WIDTH)),
                   _full((1, SGU_WIDTH)), _full(ws.shape), _full(bs_t.shape)],
        out_shape=[jax.ShapeDtypeStruct((t, 2 * SGU_WIDTH), BF16), jax.ShapeDtypeStruct((1, 2 * SGU_WIDTH), F32),
                   jax.ShapeDtypeStruct((1, SGU_WIDTH), F32), jax.ShapeDtypeStruct((1, SGU_WIDTH), F32),
                   jax.ShapeDtypeStruct(ws.shape, F32), jax.ShapeDtypeStruct(bs_t.shape, F32)],
        compiler_params=_params(1),
    )(dyb, uv, uvb, lnw, lnb, ws, bs_t, e_groups)


def _merge(gates, pa, pb, bg):
    s = _sigmoid(gates + bg)
    return s[:, :D_MODEL] * pa + s[:, D_MODEL:] * pb


def _branches_merge(branches, gates, bg):
    pa, pb = branches
    return (pa, pb, _merge(gates.astype(F32), pa, pb, bg)), ()


def _merge_backward(dmix, gates, pa, pb, bg):
    _, vjp = jax.vjp(_merge, gates.astype(F32), pa.astype(F32), pb.astype(F32), bg)
    dg, dpa, dpb, dbg = vjp(dmix)
    return (dg, dpa, dpb), (dbg,)


def _conv_f_fwd(up, cw, cb, tm=512):
    t, c = up.shape

    def body(x_ref, h_ref, w_ref, b_ref, o_ref, y_ref):
        halo = jnp.where(pl.program_id(0) > 0, h_ref[...].astype(F32)[8:], 0.0)
        y = _causal_conv(x_ref[...].astype(F32), halo, w_ref[...], b_ref[...])
        y_ref[...] = y.astype(BF16)
        o_ref[...] = (_silu(y[:, :D_FF]) * y[:, D_FF:]).astype(BF16)

    return pl.pallas_call(
        body, name="conv_f_fwd", grid=(t // tm,),
        in_specs=[_rows(tm, c), _halo(tm, c, rows=16), _full(cw.shape), _full((1, c))],
        out_specs=[_rows(tm, D_FF), _rows(tm, c)],
        out_shape=[jax.ShapeDtypeStruct((t, D_FF), BF16), jax.ShapeDtypeStruct((t, c), BF16)],
        compiler_params=_params(1),
    )(up, up, cw, cb)


def _conv_f_bwd(dact, y, up, cw, tm=128):
    t, c = up.shape
    nt = t // tm

    def body(d_ref, y_ref, x_ref, w_ref, dx_ref, dw_ref, db_ref, nxt_scr):
        @pl.when(pl.program_id(0) == 0)
        def _():
            nxt_scr[...] = jnp.zeros_like(nxt_scr)
            dw_ref[...] = jnp.zeros_like(dw_ref)
            db_ref[...] = jnp.zeros_like(db_ref)

        a, v = y_ref[:, :D_FF].astype(F32), y_ref[:, D_FF:].astype(F32)
        d = d_ref[...].astype(F32)
        silu_a, dsilu_a = _silu_and_grad(a)
        dy = jnp.concatenate([d * v * dsilu_a, d * silu_a], axis=1)
        dx, dw = _causal_conv_bwd(dy, nxt_scr[...], x_ref[...].astype(F32), w_ref[...])
        dx_ref[...] = dx.astype(BF16)
        nxt_scr[...] = dy[:8]
        dw_ref[...] += dw
        db_ref[...] += _colsum(dy)

    return pl.pallas_call(
        body, name="conv_f_bwd", grid=(nt,),
        in_specs=[_rows(tm, D_FF, nt, True), _rows(tm, c, nt, True), _rows(tm, c, nt, True), _full(cw.shape)],
        out_specs=[_rows(tm, c, nt, True), _full(cw.shape), _full((1, c))],
        out_shape=[jax.ShapeDtypeStruct((t, c), BF16), jax.ShapeDtypeStruct(cw.shape, F32),
                   jax.ShapeDtypeStruct((1, c), F32)],
        scratch_shapes=[pltpu.VMEM((8, c), F32)], compiler_params=_params(1),
    )(dact, y, up, cw)


def _conv_a_bwd(dxs, db, dc, y, xbc, cw, tm=256):
    t, c = xbc.shape
    nt = t // tm

    def body(dxs_ref, db_ref, dc_ref, y_ref, x_ref, w_ref, dx_ref, dw_ref, dbias_ref, nxt_scr):
        @pl.when(pl.program_id(0) == 0)
        def _():
            nxt_scr[...] = jnp.zeros_like(nxt_scr)
            dw_ref[...] = jnp.zeros_like(dw_ref)
            dbias_ref[...] = jnp.zeros_like(dbias_ref)

        dy = jnp.concatenate([dxs_ref[...], db_ref[...], dc_ref[...]], axis=1) * _dsilu(y_ref[...].astype(F32))
        dx, dw = _causal_conv_bwd(dy, nxt_scr[...], x_ref[...].astype(F32), w_ref[...])
        dx_ref[...] = dx.astype(BF16)
        nxt_scr[...] = dy[:8]
        dw_ref[...] += dw
        dbias_ref[...] += _colsum(dy)

    return pl.pallas_call(
        body, name="conv_a_bwd", grid=(nt,),
        in_specs=[_rows(tm, SSD_INNER, nt, True), _rows(tm, SSD_BC, nt, True), _rows(tm, SSD_BC, nt, True),
                  _rows(tm, c, nt, True), _rows(tm, c, nt, True), _full(cw.shape)],
        out_specs=[_rows(tm, c, nt, True), _full(cw.shape), _full((1, c))],
        out_shape=[jax.ShapeDtypeStruct((t, c), BF16), jax.ShapeDtypeStruct(cw.shape, F32),
                   jax.ShapeDtypeStruct((1, c), F32)],
        scratch_shapes=[pltpu.VMEM((8, c), F32)], compiler_params=_params(1),
    )(dxs, db, dc, y, xbc, cw)


def _pad_lanes(v, n=DT_PAD):
    return jnp.pad(v, ((0, 0), (0, n - v.shape[1])))


def _local_step(x, target, w, p, after=None, late_weights=None, on_grad=None, on_small=None):
    dtb, alog, dsk = _pad_lanes(p["dt_bias"]), _pad_lanes(p["a_log"]), _pad_lanes(p["d_skip"])
    bs_t = _pad_lanes(p["b_spatial"].T)
    e_heads = (jnp.arange(SSD_INNER)[:, None] // SSD_HEAD_DIM == jnp.arange(LANES)[None, :]).astype(BF16)
    e_heads_t = (jnp.arange(LANES)[:, None] == jnp.arange(SSD_INNER)[None, :] // SSD_HEAD_DIM).astype(BF16)
    e_groups = (jnp.arange(SGU_WIDTH)[:, None] // LANES == jnp.arange(LANES)[None, :]).astype(BF16)

    n1 = _norm_fwd(x, p["norm1_w"], "norm1_fwd", after=after)
    z = _mm(n1, w["z"], "nt", "proj_z", out_dtype=BF16)
    xbc = _mm(n1, w["xbc"], "nt", "proj_xbc", out_dtype=BF16)
    dtr = _mm(n1, w["dt"], "nt", "proj_dt")
    uv = _mm(n1, w["uv"], "nt", "proj_uv", out_dtype=BF16)
    gates = _mm(n1, w["gates"], "nt", "proj_gates", out_dtype=BF16)
    xc, conv_a_out = _conv_a_fwd(xbc, w["conv_a"], p["conv_a_b"])
    y, ya, sprev = _ssd_fwd(xc, dtr, z, dtb, alog, dsk, p["ssd_norm_w"], e_heads_t)
    yb = _sgu_fwd(uv, p["uv_b"], p["v_ln_w"], p["v_ln_b"], p["w_spatial"], bs_t)
    if late_weights is not None:
        w = {**w, **late_weights(ya, yb)}
    narrow = (D_MODEL, BF16)
    pa, pb, mix = _mm_rows(
        [ya, yb], [w["branch_a"], w["branch_b"]], "nn", "branches", _branches_merge, rows=[gates],
        fulls=[p["b_gate"]], row_outs=[narrow] * 3, summed=False)
    wide = [(D_MODEL, F32), (D_MODEL, BF16)]
    h1, n2 = _mm_rows(mix, w["out"], "nn", "out_proj", _residual_norm, rows=[x], fulls=[p["norm2_w"]], row_outs=wide)
    up = _mm(n2, w["up"], "nt", "up_proj", out_dtype=BF16)
    act, conv_f_out = _conv_f_fwd(up, w["conv_f"], p["conv_f_b"])
    dh2, dh2b, loss, g_final = _mm_rows(
        act, w["down"], "nn", "down_proj", _loss_and_grad, rows=[h1, target], fulls=[p["final_norm_w"]],
        row_outs=wide, acc_outs=[(8, LANES), (1, D_MODEL)])

    on_grad = on_grad or (lambda name, grads: None)
    g = {"final_norm_w": g_final}
    g["down"] = _wgrad(act, dh2b, "down_wgrad")
    tok = on_grad("w_down", g)
    dact = _mm(dh2b, w["down"], "nt", "down_dgrad", out_dtype=BF16, after=tok)
    dup, g["conv_f"], g["conv_f_b"] = _conv_f_bwd(dact, conv_f_out, up, w["conv_f"])
    g["up"] = _wgrad(dup, n2, "up_wgrad")
    tok = on_grad("w_up", g)
    dh1, dh1b, g["norm2_w"] = _mm_rows(
        dup, w["up"], "nn", "up_dgrad", _norm_backward, rows=[h1, dh2], fulls=[p["norm2_w"]], row_outs=wide,
        acc_outs=[(1, D_MODEL)], after=tok)
    g["out"] = _wgrad(mix, dh1b, "out_wgrad")
    tok = on_grad("w_out", g)
    dgates, dpa, dpb, g["b_gate"] = _mm_rows(
        dh1b, w["out"], "nt", "out_dgrad", _merge_backward, rows=[gates, pa, pb], fulls=[p["b_gate"]],
        row_outs=[(2 * D_MODEL, BF16), (D_MODEL, BF16), (D_MODEL, BF16)], acc_outs=[(1, 2 * D_MODEL)], after=tok)
    g["branch_a"] = _wgrad(ya, dpa, "branch_a_wgrad")
    g["branch_b"] = _wgrad(yb, dpb, "branch_b_wgrad")
    tok = on_grad("w_branch", g)
    dya, dyb = _mm_rows(
        [dpa, dpb], [w["branch_a"], w["branch_b"]], "nt", "branches_dgrad", lambda products: (tuple(products), ()),
        row_outs=[(SSD_INNER, F32), (SGU_WIDTH, F32)], after=tok, summed=False)
    duv, g["uv_b"], g["v_ln_w"], g["v_ln_b"], g["w_spatial"], dbs_t = _sgu_bwd(
        dyb, uv, p["uv_b"], p["v_ln_w"], p["v_ln_b"], p["w_spatial"], bs_t, e_groups)
    g["b_spatial"] = dbs_t[:, :SGU_GROUPS].T
    dz, dxs, db, dc, ddtr, g["ssd_norm_w"], ddtb, dalog, ddsk = _ssd_bwd(
        dya, y, z, xc, dtr, sprev, dtb, alog, dsk, p["ssd_norm_w"], e_heads, e_heads_t)
    g["dt_bias"], g["a_log"], g["d_skip"] = ddtb, dalog, ddsk
    dxbc, g["conv_a"], g["conv_a_b"] = _conv_a_bwd(dxs, db, dc, conv_a_out, xbc, w["conv_a"])
    tok = on_small(g, loss) if on_small else None
    ddtrb = ddtr.astype(BF16)
    for name, d in (("z", dz), ("xbc", dxbc), ("dt", ddtrb), ("uv", duv), ("gates", dgates)):
        g[name] = _wgrad(d, n1, name + "_wgrad", after=tok)
    tok = on_grad("w_in", g)
    dn1 = _mm([dz, dxbc], [w["z"], w["xbc"]], "nn", "ssd_dgrad", after=tok)
    gx, g["norm1_w"] = _mm_rows(
        [duv, dgates, ddtrb], [w["uv"], w["gates"], w["dt"]], "nn", "in_dgrad",
        lambda r, so_far, h, dres, w_: tuple(t[:1] for t in _norm_backward(r + so_far, h, dres, w_)),
        rows=[dn1, x, dh1], fulls=[p["norm1_w"]], row_outs=wide[:1], acc_outs=[(1, D_MODEL)])
    return loss, gx, g


def _place():
    return lax.axis_index("x"), lax.axis_index("y"), lax.axis_index("c")


def _other_chips(x, y):
    return [(1 - x, y), (x, 1 - y), (1 - x, 1 - y)]


def _all_gather(shards, name):
    n = len(shards)

    def body(*refs):
        ins, outs = refs[:n], refs[n:2 * n]
        send_sems, recv_sems, local_sems = refs[2 * n:]
        x, y, c = _place()
        me, sibling = (x, y, c), (x, y, 1 - c)
        chips = _other_chips(x, y)

        def copy(a, k, block, to, src=None):
            slot = outs[a].at[4 * block[0] + 2 * block[1] + block[2]]
            return pltpu.make_async_remote_copy(
                src_ref=slot if src is None else src, dst_ref=slot, send_sem=send_sems.at[7 * a + k],
                recv_sem=recv_sems.at[7 * a + k], device_id=to, device_id_type=MESH)

        started = []
        for a in range(n):
            mine = pltpu.make_async_copy(ins[a], outs[a].at[4 * x + 2 * y + c], local_sems.at[a])
            mine.start()
            started.append(mine)
        sends = []
        for a in range(n):
            sends.append(copy(a, 0, me, sibling, src=ins[a]))
            sends += [copy(a, 1 + j, me, (*chip, c), src=ins[a]) for j, chip in enumerate(chips)]
        for cp in sends:
            cp.start()
        for a in range(n):
            for j, chip in enumerate(chips):
                copy(a, 1 + j, (*chip, c), me).wait_recv()
                fwd = copy(a, 4 + j, (*chip, c), sibling)
                fwd.start()
                sends.append(fwd)
        for a in range(n):
            copy(a, 0, sibling, me).wait_recv()
            for j, chip in enumerate(chips):
                copy(a, 4 + j, (*chip, 1 - c), me).wait_recv()
        for cp in sends:
            cp.wait_send()
        for mine in started:
            mine.wait()

    any_spec = pl.BlockSpec(memory_space=pl.ANY)
    return pl.pallas_call(
        body, name=name, in_specs=[any_spec] * n, out_specs=[any_spec] * n,
        out_shape=[jax.ShapeDtypeStruct((N_DEV, *s.shape), s.dtype) for s in shards],
        scratch_shapes=[pltpu.SemaphoreType.DMA((7 * n,)), pltpu.SemaphoreType.DMA((7 * n,)),
                        pltpu.SemaphoreType.DMA((n,))],
    )(*shards)


HBM_SPEC = pl.BlockSpec(memory_space=pltpu.HBM)
SEM_SPEC = pl.BlockSpec(memory_space=pltpu.SEMAPHORE)
ANY_SPEC = pl.BlockSpec(memory_space=pl.ANY)
DATAFLOW = pltpu.SideEffectType.DATAFLOW_SIDE_EFFECTING
N_PEERS = N_DEV - 1


def _peers(x, y, c):
    out = []
    for r in range(1, N_DEV):
        fx, fy, fc = r >> 2 & 1, r >> 1 & 1, r & 1
        out.append(((1 - x) if fx else x, (1 - y) if fy else y, (1 - c) if fc else c))
    return out


def _gather_copies(srcs, lands, send_sems, recv_sems, sending, scatter=False):
    x, y, c = _place()
    copies = []
    for a, (src, land) in enumerate(zip(srcs, lands)):
        for j, (px, py, pc) in enumerate(_peers(x, y, c)):
            mine, theirs = 4 * x + 2 * y + c, 4 * px + 2 * py + pc
            block = src.at[theirs if sending else 0] if scatter else src
            copies.append(pltpu.make_async_remote_copy(
                src_ref=block, dst_ref=land.at[mine if sending else theirs], send_sem=send_sems.at[N_PEERS * a + j],
                recv_sem=recv_sems.at[N_PEERS * a + j], device_id=(px, py, pc), device_id_type=MESH))
    return copies


def _gather_start(shards, after, name, scatter=False):
    n = len(shards)
    after = [] if after is None else [after]

    def body(*refs):
        srcs, lands = refs[:n], refs[n:2 * n]
        send_sems, recv_sems = refs[2 * n + len(after):2 * n + len(after) + 2]
        token = refs[-1]
        for cp in _gather_copies(srcs, lands, send_sems, recv_sems, sending=True, scatter=scatter):
            cp.start()
        token[...] = jnp.zeros_like(token)

    lands = [lax.empty(s.shape if scatter else (N_DEV, *s.shape), s.dtype) for s in shards]
    hbm = lambda a: pltpu.with_memory_space_constraint(a, pltpu.HBM)
    out = pl.pallas_call(
        body, name=name,
        out_shape=(pltpu.SemaphoreType.DMA((N_PEERS * n,)), pltpu.SemaphoreType.DMA((N_PEERS * n,)),
                   *[pltpu.HBM(a.shape, a.dtype) for a in (*shards, *lands)], jax.ShapeDtypeStruct((8, LANES), F32)),
        in_specs=[HBM_SPEC] * (2 * n) + [ANY_SPEC] * len(after),
        out_specs=(SEM_SPEC, SEM_SPEC, *[HBM_SPEC] * (2 * n), pl.BlockSpec(memory_space=pltpu.VMEM)),
        input_output_aliases={i: 2 + i for i in range(2 * n)},
        compiler_params=pltpu.CompilerParams(has_side_effects=DATAFLOW),
    )(*[hbm(a) for a in (*shards, *lands)], *after)
    return out[0], out[1], out[2:2 + n], out[2 + n:2 + 2 * n], out[-1]


def _gather_wait(send_sems, recv_sems, shards, lands, after, name, scatter=False):
    n = len(shards)
    after = tuple(after)

    def body(*refs):
        srcs, lands_ = refs[:n], refs[n:2 * n]
        send, recv = refs[2 * n:2 * n + 2]
        for cp in _gather_copies(srcs, lands_, send, recv, sending=False, scatter=scatter):
            cp.wait_send()
            cp.wait_recv()

    out = pl.pallas_call(
        body, name=name, out_shape=tuple(pltpu.HBM(a.shape, a.dtype) for a in (*shards, *lands)),
        in_specs=[HBM_SPEC] * (2 * n) + [SEM_SPEC, SEM_SPEC] + [ANY_SPEC] * len(after),
        out_specs=tuple([HBM_SPEC] * (2 * n)), input_output_aliases={i: i for i in range(2 * n)},
        compiler_params=pltpu.CompilerParams(has_side_effects=DATAFLOW),
    )(*shards, *lands, send_sems, recv_sems, *after)
    return out[:n], out[n:]


def _chip_copies(src, land, send_sems, recv_sems):
    x, y, c = _place()
    return [pltpu.make_async_remote_copy(
        src_ref=src.at[2 * cx + cy], dst_ref=land.at[j], send_sem=send_sems.at[j], recv_sem=recv_sems.at[j],
        device_id=(cx, cy, c), device_id_type=MESH) for j, (cx, cy) in enumerate(_other_chips(x, y))]


def _chips_start(q, name):
    def body(q_ref, land_ref, send_sems, recv_sems, q_thru, land_thru, token):
        for cp in _chip_copies(q_ref, land_ref, send_sems, recv_sems):
            cp.start()
        token[...] = jnp.zeros_like(token)

    land = lax.empty((3, *q.shape[1:]), q.dtype)
    return pl.pallas_call(
        body, name=name,
        out_shape=(pltpu.SemaphoreType.DMA((3,)), pltpu.SemaphoreType.DMA((3,)), pltpu.HBM(q.shape, q.dtype),
                   pltpu.HBM(land.shape, land.dtype), jax.ShapeDtypeStruct((8, LANES), F32)),
        in_specs=[HBM_SPEC, HBM_SPEC],
        out_specs=(SEM_SPEC, SEM_SPEC, HBM_SPEC, HBM_SPEC, pl.BlockSpec(memory_space=pltpu.VMEM)),
        input_output_aliases={0: 2, 1: 3}, compiler_params=pltpu.CompilerParams(has_side_effects=DATAFLOW),
    )(pltpu.with_memory_space_constraint(q, pltpu.HBM), pltpu.with_memory_space_constraint(land, pltpu.HBM))


def _chips_wait(send_sems, recv_sems, q, land, after, name):
    def body(q_ref, land_ref, send, recv, after_ref, q_out, land_out):
        for cp in _chip_copies(q_ref, land_ref, send, recv):
            cp.wait_send()
            cp.wait_recv()

    return pl.pallas_call(
        body, name=name, out_shape=(pltpu.HBM(q.shape, q.dtype), pltpu.HBM(land.shape, land.dtype)),
        in_specs=[HBM_SPEC, HBM_SPEC, SEM_SPEC, SEM_SPEC, ANY_SPEC], out_specs=(HBM_SPEC, HBM_SPEC),
        input_output_aliases={0: 0, 1: 1}, compiler_params=pltpu.CompilerParams(has_side_effects=DATAFLOW),
    )(q, land, send_sems, recv_sems, after)[1]


def _exchange_cores(part, name):
    def body(in_ref, out_ref, send_sems, recv_sems):
        x, y, c = _place()
        copies = [pltpu.make_async_remote_copy(
            src_ref=in_ref.at[2 * k + (1 - c)], dst_ref=out_ref.at[k], send_sem=send_sems.at[k],
            recv_sem=recv_sems.at[k], device_id=(x, y, 1 - c), device_id_type=MESH) for k in range(4)]
        for cp in copies:
            cp.start()
        for cp in copies:
            cp.wait()

    return pl.pallas_call(
        body, name=name, in_specs=[ANY_SPEC], out_specs=ANY_SPEC,
        out_shape=jax.ShapeDtypeStruct((4, *part.shape[1:]), part.dtype),
        scratch_shapes=[pltpu.SemaphoreType.DMA((4,)), pltpu.SemaphoreType.DMA((4,))],
    )(part)


def _chip_sum(part, got, place, name, tr=256):
    _, r, c = part.shape
    tr, tc = _tile2d(r, c, tr)

    def body(place_ref, p_ref, g_ref, q_ref, own_ref):
        s = p_ref[0].astype(F32) + g_ref[0].astype(F32)
        q_ref[0] = s.astype(BF16)

        @pl.when(pl.program_id(2) == place_ref[1])
        def _():
            own_ref[...] = s

    grid_spec = pltpu.PrefetchScalarGridSpec(
        num_scalar_prefetch=1, grid=(r // tr, c // tc, 4),
        in_specs=[pl.BlockSpec((1, tr, tc), lambda i, j, k, pr: (2 * k + pr[0], i, j)),
                  pl.BlockSpec((1, tr, tc), lambda i, j, k, pr: (k, i, j))],
        out_specs=[pl.BlockSpec((1, tr, tc), lambda i, j, k, pr: (k, i, j)),
                   pl.BlockSpec((tr, tc), lambda i, j, k, pr: (i, j))])
    return pl.pallas_call(
        body, name=name, grid_spec=grid_spec,
        out_shape=[jax.ShapeDtypeStruct((4, r, c), BF16), jax.ShapeDtypeStruct((r, c), F32)],
        compiler_params=_params(3),
    )(place, part, got)


def _sum_adamw(own, got, w, m, v, name):
    r, c = own.shape
    tc = 4 * LANES

    def body(own_ref, got_ref, w_ref, m_ref, v_ref, g_ref, d_ref, nm_ref, nv_ref):
        g = own_ref[...]
        for j in range(3):
            g = g + got_ref[j].astype(F32)
        two_d = lambda ref: ref[...].reshape(r, tc)
        delta, nm, nv = _adamw(two_d(w_ref), g, two_d(m_ref), two_d(v_ref))
        for ref, val in ((g_ref, g), (d_ref, delta), (nm_ref, nm), (nv_ref, nv)):
            ref[...] = val.reshape(ref.shape)

    wblk = pl.BlockSpec((r, 1, tc), lambda j: (0, 0, j))
    return pl.pallas_call(
        body, name=name, grid=(c // tc,),
        in_specs=[pl.BlockSpec((r, tc), lambda j: (0, j)), pl.BlockSpec((3, r, tc), lambda j: (0, 0, j)),
                  wblk, wblk, wblk],
        out_specs=[wblk] * 4, out_shape=[jax.ShapeDtypeStruct(w.shape, F32)] * 4, compiler_params=_params(1),
    )(own, got, w, m, v)


def _adamw(w, g, m, v):
    m = ADAM_B1 * m + (1.0 - ADAM_B1) * g
    v = ADAM_B2 * v + (1.0 - ADAM_B2) * jnp.square(g)
    m_hat = m / (1.0 - ADAM_B1 ** ADAM_STEP)
    v_hat = v / (1.0 - ADAM_B2 ** ADAM_STEP)
    return -ADAM_LR * (m_hat / (jnp.sqrt(v_hat) + ADAM_EPS) + ADAM_WD * w), m, v


def _sum8_adamw(part, got, place, w, m, v, name, tr=256):
    r, c = w.shape
    tr, tc = _tile2d(r, c, tr)
    blk = pl.BlockSpec((tr, tc), lambda i, j, pr: (i, j))

    def body(place_ref, own_ref, got_ref, w_ref, m_ref, v_ref, g_ref, d_ref, nm_ref, nv_ref):
        dev = 2 * place_ref[1] + place_ref[0]
        g = jnp.zeros((tr, tc), F32)
        for d in range(N_DEV):
            g = g + jnp.where(dev == d, own_ref[0], got_ref[d]).astype(F32)
        g_ref[...] = g
        d_ref[...], nm_ref[...], nv_ref[...] = _adamw(w_ref[...], g, m_ref[...], v_ref[...])

    grid_spec = pltpu.PrefetchScalarGridSpec(
        num_scalar_prefetch=1, grid=(r // tr, c // tc),
        in_specs=[pl.BlockSpec((1, tr, tc), lambda i, j, pr: (2 * pr[1] + pr[0], i, j)),
                  pl.BlockSpec((N_DEV, tr, tc), lambda i, j, pr: (0, i, j)), blk, blk, blk],
        out_specs=[blk] * 4)
    return pl.pallas_call(
        body, name=name, grid_spec=grid_spec, out_shape=[jax.ShapeDtypeStruct(w.shape, F32)] * 4,
        compiler_params=_params(2),
    )(place, part, got, w, m, v)


VECTORS = ["norm1_w", "b_gate", "conv_a_b", "dt_bias", "a_log", "d_skip", "ssd_norm_w", "uv_b", "v_ln_w", "v_ln_b",
           "norm2_w", "conv_f_b", "final_norm_w"]
SMALL_ORDER = VECTORS + ["w_spatial", "b_spatial", "conv_a_w", "conv_f_w"]


ROW_VECTORS = VECTORS[1:]


def _small_adamw(gathered, w, m, v):
    sizes = {n: w[n].shape[1] for n in ROW_VECTORS}
    offs, off = {}, 0
    for n in ROW_VECTORS:
        offs[n] = off
        off += -(-sizes[n] // LANES) * LANES
    loss_off = off
    k = len(SMALL_ORDER)
    n_g = len(gathered)

    def body(*refs):
        row_ref, ws_ref, bs_ref, ca_ref, cf_ref, n1_ref = refs[:n_g]
        w_refs, m_refs, v_refs = (dict(zip(SMALL_ORDER, refs[n_g + i * k:n_g + (i + 1) * k])) for i in range(3))
        outs = refs[n_g + 3 * k:]
        x, y, c = _place()
        dev = 4 * x + 2 * y + c

        def total(ref):
            s = ref[0]
            for d in range(1, N_DEV):
                s = s + ref[d]
            return s

        row = total(row_ref)
        grads = {n: row[:, offs[n]:offs[n] + sizes[n]] for n in ROW_VECTORS}
        grads["norm1_w"], grads["w_spatial"], grads["b_spatial"] = total(n1_ref), total(ws_ref), total(bs_ref)
        for n, ref in (("conv_a_w", ca_ref), ("conv_f_w", cf_ref)):
            whole, cols = total(ref), w_refs[n].shape[1]
            mine = whole[:, :cols]
            for d in range(1, N_DEV):
                mine = jnp.where(dev == d, whole[:, d * cols:(d + 1) * cols], mine)
            grads[n] = mine
        for i, n in enumerate(SMALL_ORDER):
            outs[4 * i][...] = grads[n]
            outs[4 * i + 1][...], outs[4 * i + 2][...], outs[4 * i + 3][...] = _adamw(
                w_refs[n][...], grads[n], m_refs[n][...], v_refs[n][...])
        outs[4 * k][...] = row[:, loss_off:loss_off + LANES]

    out = pl.pallas_call(
        body, name="adamw_small",
        out_shape=[jax.ShapeDtypeStruct(w[n].shape, F32) for n in SMALL_ORDER for _ in range(4)]
        + [jax.ShapeDtypeStruct((1, LANES), F32)],
        compiler_params=_params(0),
    )(*gathered, *[t[n] for t in (w, m, v) for n in SMALL_ORDER])
    return [dict(zip(SMALL_ORDER, out[j:4 * k:4])) for j in range(4)] + [out[4 * k]]


SMALL = ["norm1_w", "b_gate", "conv_a_b", "dt_bias", "a_log", "d_skip", "ssd_norm_w", "uv_b", "v_ln_w", "v_ln_b",
         "w_spatial", "b_spatial", "norm2_w", "conv_f_b", "final_norm_w"]
BIG = ["w_in", "w_branch", "w_out", "w_up", "w_down"]
TRANSPOSED = ("w_in", "w_up")
WEIGHTS = ["norm1_w", "w_in", "b_gate", "conv_a_w", "conv_a_b", "dt_bias", "a_log", "d_skip", "ssd_norm_w", "uv_b",
           "v_ln_w", "v_ln_b", "w_spatial", "b_spatial", "w_branch", "w_out", "norm2_w", "w_up", "conv_f_w",
           "conv_f_b", "w_down", "final_norm_w"]
IN_SPLITS = [("z", 0, 2048), ("xbc", 2048, 5120), ("dt", 5120, 5152), ("uv", 5152, 7200), ("gates", 7200, 9248)]


def _columns_from_devices(a):
    return a.transpose(1, 0, 2).reshape(a.shape[1], -1)


def kernel(x, norm1_w, w_in, b_gate, conv_a_w, conv_a_b, dt_bias, a_log, d_skip, ssd_norm_w, uv_b, v_ln_w, v_ln_b, w_spatial, b_spatial, w_branch, w_out, norm2_w, w_up, conv_f_w, conv_f_b, w_down, final_norm_w, loss_target, m_norm1_w, m_w_in, m_b_gate, m_conv_a_w, m_conv_a_b, m_dt_bias, m_a_log, m_d_skip, m_ssd_norm_w, m_uv_b, m_v_ln_w, m_v_ln_b, m_w_spatial, m_b_spatial, m_w_branch, m_w_out, m_norm2_w, m_w_up, m_conv_f_w, m_conv_f_b, m_w_down, m_final_norm_w, v_norm1_w, v_w_in, v_b_gate, v_conv_a_w, v_conv_a_b, v_dt_bias, v_a_log, v_d_skip, v_ssd_norm_w, v_uv_b, v_v_ln_w, v_v_ln_b, v_w_spatial, v_b_spatial, v_w_branch, v_w_out, v_norm2_w, v_w_up, v_conv_f_w, v_conv_f_b, v_w_down, v_final_norm_w):
    args = dict(locals())
    wts = {n: args[n] for n in WEIGHTS}
    mom = {n: args["m_" + n] for n in WEIGHTS}
    var = {n: args["v_" + n] for n in WEIGHTS}
    cx, cy, cc = _place()
    dev = 4 * cx + 2 * cy + cc
    place = jnp.stack([cc, 2 * cx + cy]).astype(jnp.int32)

    def shard2d(n, a):
        return a[0].T if n in TRANSPOSED else a[0]

    def unshard(n, b):
        return (b.T if n in TRANSPOSED else b)[None]

    g_in, g_conv_a, g_conv_f = _all_gather(
        [shard2d("w_in", w_in).astype(BF16), conv_a_w[0], conv_f_w[0]], "gather_w_in")
    late = [shard2d(n, wts[n]).astype(BF16) for n in BIG[1:]]
    send_sems, recv_sems, late, lands, token = _gather_start(late, g_in, "gather_late_start")
    w_in_rows = g_in.reshape(-1, D_MODEL)
    w = {name: w_in_rows[lo:hi] for name, lo, hi in IN_SPLITS}
    w["dt"] = jnp.pad(w["dt"], ((0, DT_PAD - SSD_HEADS), (0, 0)))
    w["conv_a"] = _columns_from_devices(g_conv_a)
    w["conv_f"] = _columns_from_devices(g_conv_f)

    def late_weights(*after):
        mine, got = _gather_wait(send_sems, recv_sems, late, lands, after, "gather_late_wait")
        g_branch, g_out, g_up, g_down = [lax.dynamic_update_index_in_dim(land, own, dev, 0).reshape(-1, D_MODEL)
                                         for land, own in zip(got, mine)]
        return {"branch_a": g_branch[:SSD_INNER], "branch_b": g_branch[SSD_INNER:], "out": g_out, "up": g_up,
                "down": g_down}

    in_flight = {}

    def on_grad(n, g):
        part = {"w_in": lambda: jnp.concatenate([g[name][:hi - lo] for name, lo, hi in IN_SPLITS], axis=0),
                "w_branch": lambda: jnp.concatenate([g["branch_a"], g["branch_b"]], axis=0),
                "w_out": lambda: g["out"], "w_up": lambda: g["up"], "w_down": lambda: g["down"]}[n]()
        part = part.reshape(N_DEV, -1, D_MODEL)
        if n == "w_in":
            q, own = _chip_sum(part, _exchange_cores(part, "to_other_core_w_in"), place, "chip_sum_w_in")
            send, recv, q, land, tok = _chips_start(q, "to_other_chips_start_w_in")
            in_flight[n] = (own, send, recv, q, land)
            return tok
        send, recv, (part,), (land,), tok = _gather_start([part], None, f"to_owners_start_{n}", scatter=True)
        in_flight[n] = (part, send, recv, land)
        return tok

    p = {n: wts[n][0] if wts[n].ndim > 2 else wts[n].reshape(1, -1) for n in SMALL}
    small_flight = []

    def on_small(g, loss):
        arrays = [jnp.concatenate([g[n] for n in ROW_VECTORS] + [loss[:1]], axis=1), g["w_spatial"], g["b_spatial"],
                  g["conv_a"], g["conv_f"]]
        *flight, tok = _gather_start(arrays, g["conv_a"], "gather_small_start")
        small_flight.append(flight)
        return tok

    loss, gx, g = _local_step(x[0], loss_target[0], w, p, after=token, late_weights=late_weights, on_grad=on_grad,
                              on_small=on_small)
    *flight, _ = _gather_start([g["norm1_w"]], gx, "gather_norm1_start")
    small_flight.append(flight)

    grads, delta, new_m, new_v = {}, {}, {}, {}

    def big_adamw(n, after):
        if n == "w_in":
            own, send, recv, q, land = in_flight[n]
            got = _chips_wait(send, recv, q, land, after, "to_other_chips_wait_w_in")
            out = _sum_adamw(own, got, *[t[n].transpose(2, 0, 1) for t in (wts, mom, var)], "adamw_w_in")
            grads[n], delta[n], new_m[n], new_v[n] = [o.transpose(1, 2, 0) for o in out]
            return out[1]
        part, send, recv, land = in_flight[n]
        (part,), (got,) = _gather_wait(send, recv, [part], [land], [after], f"to_owners_wait_{n}", scatter=True)
        out = _sum8_adamw(part, got, place, *[shard2d(n, t[n]) for t in (wts, mom, var)], f"adamw_{n}")
        grads[n], delta[n], new_m[n], new_v[n] = [unshard(n, o) for o in out]
        return out[1]

    after = gx
    for n in ("w_down", "w_up", "w_out", "w_branch"):
        after = big_adamw(n, after)
    gathered = []
    for (send, recv, mine, land), name in zip(small_flight, ("gather_small_wait", "gather_norm1_wait")):
        mine, got = _gather_wait(send, recv, mine, land, [after], name)
        gathered += [lax.dynamic_update_index_in_dim(full, own, dev, 0) for full, own in zip(got, mine)]
    small = [{n: t[n][0] if t[n].ndim > 2 else t[n].reshape(1, -1) for n in SMALL_ORDER} for t in (wts, mom, var)]
    *outs, loss = _small_adamw(gathered, *small)
    for tgt, out in zip((grads, delta, new_m, new_v), outs):
        tgt.update({n: out[n].reshape(wts[n].shape) for n in SMALL_ORDER})
    big_adamw("w_in", loss)
    loss = loss[0, 0]

    return (loss, gx[None], *[grads[n] for n in WEIGHTS], *[delta[n] for n in WEIGHTS],
            *[new_m[n] for n in WEIGHTS], *[new_v[n] for n in WEIGHTS])
```

```python
import functools

import jax
import jax.numpy as jnp
from jax import lax
from jax.experimental import pallas as pl
from jax.experimental.pallas import tpu as pltpu

F32, BF16 = jnp.float32, jnp.bfloat16
HIGHEST = lax.Precision.HIGHEST

D_MODEL = 1024
SSD_INNER = 2048
SSD_HEAD_DIM = 64
SSD_HEADS = 32
SSD_GROUPS = 4
SSD_STATE = 128
SSD_BC = SSD_GROUPS * SSD_STATE
SSD_XBC = SSD_INNER + 2 * SSD_BC
SSD_CONV = 4
CHUNK = 128
N_PAIRS = SSD_HEADS // 2
PAIRS_PER_GROUP = N_PAIRS // SSD_GROUPS
SGU_WIDTH = 1024
SGU_GROUPS = 8
SGU_TILE = 256
D_FF = 2816
FFN_CONV = 3
NORM_EPS = 1e-6
LN_EPS = 1e-5
LANES = 128
DT_PAD = LANES

ADAM_LR, ADAM_B1, ADAM_B2, ADAM_EPS, ADAM_WD, ADAM_STEP = 0.001, 0.9, 0.999, 1e-08, 0.01, 10

N_DEV = 8
VMEM_LIMIT = 56 * 1024 * 1024
MESH = pl.DeviceIdType.MESH


def _params(n_grid, **kw):
    sem = dict(dimension_semantics=("arbitrary",) * n_grid) if n_grid else {}
    return pltpu.CompilerParams(vmem_limit_bytes=VMEM_LIMIT, **sem, **kw)


def _tile(n, pref):
    t = (min(pref, n) // LANES) * LANES
    while n % t:
        t -= LANES
    return t


def _row_tile(r, pref):
    for t in range(min(pref, r) // 16 * 16, 0, -16):
        if r % t == 0:
            return t
    return r


def _tile2d(r, c, rows):
    if r % 16 == 0:
        return _row_tile(r, rows), c
    return r, _tile(c, 2 * LANES)


def _rows(tm, n, nt=None, rev=False):
    if rev:
        return pl.BlockSpec((tm, n), lambda i: (nt - 1 - i, 0))
    return pl.BlockSpec((tm, n), lambda i: (i, 0))


def _halo(tm, n, rows=8):
    per = tm // rows
    return pl.BlockSpec((rows, n), lambda i: (jnp.maximum(i * per - 1, 0), 0))


def _full(shape):
    nd = len(shape)
    return pl.BlockSpec(shape, lambda *_: (0,) * nd)


def _rms(x, w, eps=NORM_EPS):
    return x * lax.rsqrt(jnp.mean(x * x, axis=-1, keepdims=True) + eps) * w


def _layer_norm(x, w, b):
    mu = jnp.mean(x, axis=-1, keepdims=True)
    var = jnp.mean(jnp.square(x - mu), axis=-1, keepdims=True)
    return (x - mu) * lax.rsqrt(var + LN_EPS) * w + b


def _sigmoid(x):
    return 1.0 / (1.0 + jnp.exp(-x))


def _silu(x):
    return x * _sigmoid(x)


def _dsilu(x):
    s = _sigmoid(x)
    return s * (1.0 + x * (1.0 - s))


def _silu_and_grad(x):
    s = _sigmoid(x)
    return x * s, s * (1.0 + x * (1.0 - s))


def _softplus(x):
    return jnp.maximum(x, 0.0) + jnp.log(1.0 + jnp.exp(-jnp.abs(x)))


def _gelu(x):
    return jax.nn.gelu(x)


def _dot(a, b):
    return jnp.dot(a, b, preferred_element_type=F32)


def _dot_nt(a, b):
    return lax.dot_general(a, b, (((1,), (1,)), ((), ())), preferred_element_type=F32)


def _dot_tn(a, b):
    return lax.dot_general(a, b, (((0,), (0,)), ((), ())), preferred_element_type=F32)


def _dot_split(p, e):
    hi = p.astype(BF16)
    lo = (p - hi.astype(F32)).astype(BF16)
    return _dot(hi, e) + _dot(lo, e)


def _colsum(x):
    return jnp.sum(x, axis=0, keepdims=True)


def _shift_down(x, halo, j):
    xs = pltpu.roll(x, j, 0)
    hs = pltpu.roll(halo, j, 0)
    r8 = lax.broadcasted_iota(jnp.int32, hs.shape, 0)
    return jnp.concatenate([jnp.where(r8 < j, hs, xs[:8]), xs[8:]], axis=0)


def _shift_up(x, nxt, j):
    n = x.shape[0]
    xs = pltpu.roll(x, n - j, 0)
    ns = pltpu.roll(nxt, 8 - j, 0)
    r8 = lax.broadcasted_iota(jnp.int32, ns.shape, 0)
    return jnp.concatenate([xs[:n - 8], jnp.where(r8 >= 8 - j, ns, xs[n - 8:])], axis=0)


def _causal_conv(x, halo, w, b):
    k = w.shape[0]
    y = b + w[k - 1:k, :] * x
    for j in range(1, k):
        y = y + w[k - 1 - j:k - j, :] * _shift_down(x, halo, j)
    return y


def _causal_conv_bwd(dy, nxt, x, w):
    k = w.shape[0]
    dx = w[k - 1:k, :] * dy
    dw = [_colsum(dy * x)]
    for j in range(1, k):
        dyj = _shift_up(dy, nxt, j)
        dx = dx + w[k - 1 - j:k - j, :] * dyj
        dw.append(_colsum(dyj * x))
    return dx, jnp.concatenate(dw[::-1], axis=0)


MM_TILE_PREF = 1408
MM_VMEM_BUDGET = 40 * 1024 * 1024


def _mm_tiles(m, n, k, out_bytes):
    tm, tn = _tile(m, MM_TILE_PREF), _tile(n, MM_TILE_PREF)
    need = lambda tm, tn: 2 * (2 * k * (tm + tn) + out_bytes * tm * tn)
    while need(tm, tn) > MM_VMEM_BUDGET:
        if tn >= tm and tn > LANES:
            tn = _tile(n, tn - LANES)
        else:
            tm = _tile(m, tm - LANES)
    return tm, tn


def _mm(a, b, dims, name, acc=None, out_dtype=F32, after=None):
    a_list, b_list = (list(a), list(b)) if isinstance(a, (list, tuple)) else ([a], [b])
    k_axis, m_axis = (0, 1) if dims == "tn" else (1, 0)
    m, ks = a_list[0].shape[m_axis], [x.shape[k_axis] for x in a_list]
    n = b_list[0].shape[0] if dims == "nt" else b_list[0].shape[1]
    tm, tn = _mm_tiles(m, n, sum(ks), 4 * (2 if acc is not None else 1))
    a_specs = [pl.BlockSpec((k, tm), lambda j, i: (0, i)) if dims == "tn" else pl.BlockSpec((tm, k), lambda j, i: (i, 0))
               for k in ks]
    b_specs = [pl.BlockSpec((tn, k), lambda j, i: (j, 0)) if dims == "nt" else pl.BlockSpec((k, tn), lambda j, i: (0, j))
               for k in ks]
    o_spec = pl.BlockSpec((tm, tn), lambda j, i: (i, j))
    dot = {"nn": _dot, "nt": _dot_nt, "tn": _dot_tn}[dims]
    n_pairs = len(ks)

    def body(*refs):
        rest = refs[2 * n_pairs:]
        r = dot(refs[0][...], refs[n_pairs][...])
        for i in range(1, n_pairs):
            r = r + dot(refs[i][...], refs[n_pairs + i][...])
        if acc is not None:
            r = r + rest[0][...]
        rest[-1][...] = r.astype(out_dtype)

    ins, specs = a_list + b_list, a_specs + b_specs
    if acc is not None:
        ins.append(acc)
        specs.append(o_spec)
    if after is not None:
        ins.append(after)
        specs.append(pl.BlockSpec(memory_space=pl.ANY))
    return pl.pallas_call(
        body, name=name, grid=(n // tn, m // tm), in_specs=specs, out_specs=o_spec,
        out_shape=jax.ShapeDtypeStruct((m, n), out_dtype), compiler_params=_params(2),
    )(*ins)


def _mm_rows(a, b, dims, name, fn, rows=(), fulls=(), row_outs=(), acc_outs=(), after=None, summed=True):
    a_list, b_list = (list(a), list(b)) if isinstance(a, (list, tuple)) else ([a], [b])
    m, ks = a_list[0].shape[0], [x.shape[1] for x in a_list]
    n, k = (b_list[0].shape[0] if dims == "nt" else b_list[0].shape[1]), sum(ks)
    per_row = 2 * k + 8 * n + sum(4 * r.shape[1] for r in rows) + sum(c * jnp.dtype(d).itemsize for c, d in row_outs)
    tm = _tile(m, 1024)
    while 2 * tm * per_row + 4 * k * n > MM_VMEM_BUDGET:
        tm = _tile(m, tm - LANES)
    dot = _dot_nt if dims == "nt" else _dot
    n_pairs = len(ks)
    n_in = 2 * n_pairs + len(rows) + len(fulls) + (after is not None)

    def body(*refs):
        ins, outs = refs[:n_in], refs[n_in:]
        row_refs, acc_refs = outs[:len(row_outs)], outs[len(row_outs):]

        @pl.when(pl.program_id(0) == 0)
        def _():
            for r in acc_refs:
                r[...] = jnp.zeros_like(r)

        products = [dot(ins[i][...], ins[n_pairs + i][...]) for i in range(n_pairs)]
        result = functools.reduce(lambda p, q: p + q, products) if summed else products
        new_rows, incs = fn(result, *[r[...] for r in ins[2 * n_pairs:2 * n_pairs + len(rows) + len(fulls)]])
        for r, val in zip(row_refs, new_rows):
            r[...] = val.astype(r.dtype)
        for r, inc in zip(acc_refs, incs):
            r[...] += inc

    extra, extra_specs = ([after], [pl.BlockSpec(memory_space=pl.ANY)]) if after is not None else ([], [])
    return pl.pallas_call(
        body, name=name, grid=(m // tm,),
        in_specs=[_rows(tm, k_i) for k_i in ks] + [_full(x.shape) for x in b_list]
        + [_rows(tm, r.shape[1]) for r in rows] + [_full(f.shape) for f in fulls] + extra_specs,
        out_specs=[_rows(tm, c) for c, _ in row_outs] + [_full(s) for s in acc_outs],
        out_shape=[jax.ShapeDtypeStruct((m, c), d) for c, d in row_outs]
        + [jax.ShapeDtypeStruct(s, F32) for s in acc_outs],
        compiler_params=_params(1),
    )(*a_list, *b_list, *rows, *fulls, *extra)


def _residual_norm(o, x, w):
    h = x + o
    return (h, _rms(h, w)), ()


def _norm_backward(dn, h, dres, w):
    _, vjp = jax.vjp(_rms, h, w)
    dh, dw = vjp(dn)
    dh = dh + dres
    return (dh, dh), (dw,)


def _loss_and_grad(dn, h1, target, w):
    yf, vjp = jax.vjp(_rms, h1 + dn, w)
    err = yf - target
    loss = 0.5 * jnp.sum(jnp.mean(err * err, axis=-1, keepdims=True))
    dh, dw = vjp(err * (1.0 / err.shape[-1]))
    return (dh, dh), (jnp.full((8, LANES), loss, F32), dw)


def _wgrad(a, d, name, after=None):
    return _mm(a, d, "tn", name, out_dtype=BF16, after=after)


def _norm_fwd(x, w, name, after=None, tm=512):
    t, d = x.shape

    def body(x_ref, w_ref, *rest):
        rest[-1][...] = _rms(x_ref[...], w_ref[...]).astype(BF16)

    extra, extra_specs = ([after], [_full(after.shape)]) if after is not None else ([], [])
    return pl.pallas_call(
        body, name=name, grid=(t // tm,), in_specs=[_rows(tm, d), _full((1, d))] + extra_specs,
        out_specs=_rows(tm, d), out_shape=jax.ShapeDtypeStruct((t, d), BF16), compiler_params=_params(1),
    )(x, w, *extra)


def _conv_a_fwd(xbc, cw, cb, tm=256):
    t, c = xbc.shape

    def body(x_ref, h_ref, w_ref, b_ref, o_ref, y_ref):
        halo = jnp.where(pl.program_id(0) > 0, h_ref[...].astype(F32)[8:], 0.0)
        y = _causal_conv(x_ref[...].astype(F32), halo, w_ref[...], b_ref[...])
        y_ref[...] = y.astype(BF16)
        o_ref[...] = _silu(y)

    return pl.pallas_call(
        body, name="conv_a_fwd", grid=(t // tm,),
        in_specs=[_rows(tm, c), _halo(tm, c, rows=16), _full(cw.shape), _full((1, c))],
        out_specs=[_rows(tm, c)] * 2,
        out_shape=[jax.ShapeDtypeStruct((t, c), F32), jax.ShapeDtypeStruct((t, c), BF16)], compiler_params=_params(1),
    )(xbc, xbc, cw, cb)


def _ssd_common(dtr, dtb, alog, e_t):
    row = lax.broadcasted_iota(jnp.int32, (CHUNK, CHUNK), 0)
    col = lax.broadcasted_iota(jnp.int32, (CHUNK, CHUNK), 1)
    causal = row >= col
    dt = _softplus(dtr + dtb)
    a = -jnp.exp(alog)
    acum = jnp.dot(causal.astype(F32), dt * a, precision=HIGHEST, preferred_element_type=F32)
    spread = lambda v: _dot(v.astype(BF16), e_t)
    elast = jnp.broadcast_to(jnp.exp(acum[CHUNK - 1:CHUNK, :]), (8, LANES))
    return dict(dt=dt, a=a, acum=acum, acum_t=acum.T, causal=causal, row=row, col=col, lane_lo=col < SSD_HEAD_DIM,
                dt_x=_dot_split(dt, e_t), ecol_x=spread(jnp.exp(acum)), elast_x=_dot_split(elast, e_t)[0:1],
                dsr_x=spread(jnp.exp(acum[CHUNK - 1:CHUNK, :] - acum)))


def _head_decay(c, h, transposed=False):
    d = c["acum"][:, h:h + 1] - c["acum_t"][h:h + 1, :]
    if transposed:
        return jnp.exp(jnp.where(c["row"] <= c["col"], -d, -jnp.inf))
    return jnp.exp(jnp.where(c["causal"], d, -jnp.inf))


def _ssd_fwd(xc, dtr, z, dtb, alog, dsk, nw, e_t):
    t = xc.shape[0]
    nc = t // CHUNK

    def body(xs_ref, b_ref, c_ref, dtr_ref, z_ref, dtb_ref, alog_ref, dsk_ref, nw_ref, et_ref,
             y_ref, ya_ref, sp_ref, s_scr):
        @pl.when(pl.program_id(0) == 0)
        def _():
            s_scr[...] = jnp.zeros_like(s_scr)

        c = _ssd_common(dtr_ref[...], dtb_ref[...], alog_ref[...], et_ref[...])
        lane_lo = c["lane_lo"]
        dsk = dsk_ref[...]
        for g in range(SSD_GROUPS):
            gs = slice(g * SSD_STATE, (g + 1) * SSD_STATE)
            bg_t, cg = b_ref[:, gs].T.astype(BF16), c_ref[:, gs].astype(BF16)
            cb = _dot(cg, bg_t)
            for pp in range(PAIRS_PER_GROUP):
                j = g * PAIRS_PER_GROUP + pp
                ps = slice(j * LANES, (j + 1) * LANES)
                x = xs_ref[:, ps]
                ecol, dsr = c["ecol_x"][:, ps], c["dsr_x"][:, ps]
                xdt = x * c["dt_x"][:, ps]
                xb = xdt.astype(BF16)
                zero = jnp.zeros_like(xb)
                yd = (_dot((cb * _head_decay(c, 2 * j)).astype(BF16), jnp.where(lane_lo, xb, zero))
                      + _dot((cb * _head_decay(c, 2 * j + 1)).astype(BF16), jnp.where(lane_lo, zero, xb)))
                sp = s_scr[j]
                yo = ecol * _dot(cg, sp.astype(BF16))
                st = _dot(bg_t, (xdt * dsr).astype(BF16))
                sp_ref[0, j] = sp
                s_scr[j] = c["elast_x"][:, ps] * sp + st
                dskp = jnp.where(lane_lo[0:1], dsk[:, 2 * j:2 * j + 1], dsk[:, 2 * j + 1:2 * j + 2])
                y_ref[:, ps] = yd + yo + dskp * x
        ya_ref[...] = _rms(y_ref[...] * _silu(z_ref[...].astype(F32)), nw_ref[...]).astype(BF16)

    ck = lambda n, col=0: pl.BlockSpec((CHUNK, n), lambda c: (c, col))
    return pl.pallas_call(
        body, name="ssd_fwd", grid=(nc,),
        in_specs=[ck(SSD_INNER), ck(SSD_BC, SSD_INNER // SSD_BC), ck(SSD_BC, SSD_INNER // SSD_BC + 1), ck(DT_PAD),
                  ck(SSD_INNER), _full((1, DT_PAD)), _full((1, DT_PAD)), _full((1, DT_PAD)),
                  _full((1, SSD_INNER)), _full(e_t.shape)],
        out_specs=[ck(SSD_INNER), ck(SSD_INNER),
                   pl.BlockSpec((1, N_PAIRS, SSD_STATE, LANES), lambda c: (c, 0, 0, 0))],
        out_shape=[jax.ShapeDtypeStruct((t, SSD_INNER), F32), jax.ShapeDtypeStruct((t, SSD_INNER), BF16),
                   jax.ShapeDtypeStruct((nc, N_PAIRS, SSD_STATE, LANES), F32)],
        scratch_shapes=[pltpu.VMEM((N_PAIRS, SSD_STATE, LANES), F32)], compiler_params=_params(1),
    )(xc, xc, xc, dtr, z, dtb, alog, dsk, nw, e_t)


def _ssd_bwd(dya, y, z, xc, dtr, sprev, dtb, alog, dsk, nw, e_heads, e_t):
    t = xc.shape[0]
    nc = t // CHUNK

    def body(dya_ref, y_ref, z_ref, xs_ref, b_ref, c_ref, dtr_ref, sp_ref, dtb_ref, alog_ref, dsk_ref, nw_ref, e_ref,
             et_ref, dz_ref, dxs_ref, db_ref, dc_ref, ddtr_ref, dnw_ref, ddtb_ref, dalog_ref, ddsk_ref, ds_scr):
        @pl.when(pl.program_id(0) == 0)
        def _():
            ds_scr[...] = jnp.zeros_like(ds_scr)
            for r in (dnw_ref, ddtb_ref, dalog_ref, ddsk_ref):
                r[...] = jnp.zeros_like(r)

        y = y_ref[...]
        _, gate_vjp = jax.vjp(lambda y_, z_, w_: _rms(y_ * _silu(z_), w_), y, z_ref[...].astype(F32), nw_ref[...])
        dy, dz, dnw = gate_vjp(dya_ref[...])
        dz_ref[...] = dz.astype(BF16)
        dnw_ref[...] += dnw

        dtr = dtr_ref[...]
        c = _ssd_common(dtr, dtb_ref[...], alog_ref[...], et_ref[...])
        dt, a, lane_lo, row, col = c["dt"], c["a"], c["lane_lo"], c["row"], c["col"]
        dsk = dsk_ref[...]
        p_a, p_dt, v_last = [], [], []
        da_cols = jnp.zeros((CHUNK, CHUNK), F32)
        da_rows = jnp.zeros((CHUNK, CHUNK), F32)
        for g in range(SSD_GROUPS):
            gs = slice(g * SSD_STATE, (g + 1) * SSD_STATE)
            bg, cg = b_ref[:, gs].astype(BF16), c_ref[:, gs].astype(BF16)
            bg_t, cg_t = b_ref[:, gs].T.astype(BF16), c_ref[:, gs].T.astype(BF16)
            cb, cb_t = _dot(cg, bg_t), _dot(bg, cg_t)
            dcb = jnp.zeros((CHUNK, CHUNK), F32)
            dbg = jnp.zeros((CHUNK, SSD_STATE), F32)
            dcg = jnp.zeros((CHUNK, SSD_STATE), F32)
            for pp in range(PAIRS_PER_GROUP):
                j = g * PAIRS_PER_GROUP + pp
                ps = slice(j * LANES, (j + 1) * LANES)
                x = xs_ref[:, ps]
                dtp, ecol, dsr = c["dt_x"][:, ps], c["ecol_x"][:, ps], c["dsr_x"][:, ps]
                elast = c["elast_x"][:, ps]
                xdt = x * dtp
                xb = xdt.astype(BF16)
                dskp = jnp.where(lane_lo[0:1], dsk[:, 2 * j:2 * j + 1], dsk[:, 2 * j + 1:2 * j + 2])
                dyp = dy[:, ps]
                dyb = dyp.astype(BF16)
                sp, dsn = sp_ref[0, j], ds_scr[j]
                spb, dsnb = sp.astype(BF16), dsn.astype(BF16)
                y_off = ecol * _dot(cg, spb)
                dw = (dyp * ecol).astype(BF16)
                dcg = dcg + _dot_nt(dw, spb)
                dsp = _dot(cg_t, dw) + elast * dsn
                xd = xdt * dsr
                zd = _dot(bg, dsnb) * dsr
                dbg = dbg + _dot_nt(xd.astype(BF16), dsnb)
                dxdt = zd
                zero = jnp.zeros_like(xb)
                for h, lm in ((2 * j, lane_lo), (2 * j + 1, jnp.logical_not(lane_lo))):
                    le = _head_decay(c, h)
                    dm = _dot_nt(jnp.where(lm, dyb, zero), jnp.where(lm, xb, zero))
                    dcb = dcb + dm * le
                    m = cb * le
                    m_t = (cb_t * _head_decay(c, h, transposed=True)).astype(BF16)
                    dxdt = dxdt + jnp.where(lm, _dot(m_t, dyb), 0.0)
                    q = dm * m
                    da_cols = da_cols + jnp.where(col == h, jnp.sum(q, axis=1, keepdims=True), 0.0)
                    da_rows = da_rows + jnp.where(row == h, _colsum(q), 0.0)
                ds_scr[j] = dsp
                dxs_ref[:, ps] = dxdt * dtp + dskp * dyp
                p_a.append(dyp * y_off - xdt * zd)
                p_dt.append(dxdt * x)
                v_last.append(_colsum(zd * xdt) + elast * _colsum(dsn * sp))
            dcbb = dcb.astype(BF16)
            db_ref[:, gs] = dbg + _dot_tn(dcbb, cg)
            dc_ref[:, gs] = dcg + _dot(dcbb, bg)
        e = e_ref[...]
        rows8 = jnp.concatenate([jnp.concatenate(v_last, axis=1), _colsum(dy * xs_ref[...]),
                                 jnp.zeros((6, SSD_INNER), F32)], axis=0)
        r8 = _dot_split(rows8, e)
        da = (_dot_split(jnp.concatenate(p_a, axis=1), e) + jnp.where(row == CHUNK - 1, r8[0:1], 0.0)
              + da_cols - da_rows.T)
        ddsk_ref[...] += r8[1:2]
        dadt = jnp.dot((row <= col).astype(F32), da, precision=HIGHEST, preferred_element_type=F32)
        ddt = dadt * a + _dot_split(jnp.concatenate(p_dt, axis=1), e)
        dalog_ref[...] += _colsum(dadt * dt) * a
        ddtr = ddt * _sigmoid(dtr + dtb_ref[...])
        ddtr_ref[...] = ddtr
        ddtb_ref[...] += _colsum(ddtr)

    ck = lambda n, col=0: pl.BlockSpec((CHUNK, n), lambda c: (nc - 1 - c, col))
    acc = lambda n: _full((1, n))
    return pl.pallas_call(
        body, name="ssd_bwd", grid=(nc,),
        in_specs=[ck(SSD_INNER), ck(SSD_INNER), ck(SSD_INNER), ck(SSD_INNER), ck(SSD_BC, SSD_INNER // SSD_BC),
                  ck(SSD_BC, SSD_INNER // SSD_BC + 1), ck(DT_PAD),
                  pl.BlockSpec((1, N_PAIRS, SSD_STATE, LANES), lambda c: (nc - 1 - c, 0, 0, 0)),
                  acc(DT_PAD), acc(DT_PAD), acc(DT_PAD), acc(SSD_INNER), _full((SSD_INNER, LANES)),
                  _full((LANES, SSD_INNER))],
        out_specs=[ck(SSD_INNER), ck(SSD_INNER), ck(SSD_BC), ck(SSD_BC), ck(DT_PAD),
                   acc(SSD_INNER), acc(DT_PAD), acc(DT_PAD), acc(DT_PAD)],
        out_shape=[jax.ShapeDtypeStruct((t, SSD_INNER), BF16), jax.ShapeDtypeStruct((t, SSD_INNER), F32),
                   jax.ShapeDtypeStruct((t, SSD_BC), F32), jax.ShapeDtypeStruct((t, SSD_BC), F32),
                   jax.ShapeDtypeStruct((t, DT_PAD), F32), jax.ShapeDtypeStruct((1, SSD_INNER), F32),
                   jax.ShapeDtypeStruct((1, DT_PAD), F32), jax.ShapeDtypeStruct((1, DT_PAD), F32),
                   jax.ShapeDtypeStruct((1, DT_PAD), F32)],
        scratch_shapes=[pltpu.VMEM((N_PAIRS, SSD_STATE, LANES), F32)], compiler_params=_params(1),
    )(dya, y, z, xc, xc, xc, dtr, sprev, dtb, alog, dsk, nw, e_heads, e_t)


def _sgu_act(uv, uvb, lnw, lnb):
    a = _gelu(uv + uvb)
    return a[:, :SGU_WIDTH], _layer_norm(a[:, SGU_WIDTH:], lnw, lnb)


def _sgu_weights(ws_ref):
    row = lax.broadcasted_iota(jnp.int32, (CHUNK, CHUNK), 0)
    col = lax.broadcasted_iota(jnp.int32, (CHUNK, CHUNK), 1)
    return [jnp.where(row >= col, ws_ref[g], 0.0).astype(BF16) for g in range(SGU_GROUPS)], row >= col


def _sgu_fwd(uv, uvb, lnw, lnb, ws, bs_t):
    t = uv.shape[0]

    def body(uv_ref, uvb_ref, lnw_ref, lnb_ref, ws_ref, bs_ref, o_ref):
        u, vn = _sgu_act(uv_ref[...].astype(F32), uvb_ref[...], lnw_ref[...], lnb_ref[...])
        wc, _ = _sgu_weights(ws_ref)
        bs = bs_ref[...]
        for ck in range(SGU_TILE // CHUNK):
            rs = slice(ck * CHUNK, (ck + 1) * CHUNK)
            for g in range(SGU_GROUPS):
                gs = slice(g * LANES, (g + 1) * LANES)
                mixed = _dot(wc[g], vn[rs, gs].astype(BF16)) + bs[:, g:g + 1]
                o_ref[rs, gs] = (u[rs, gs] * mixed).astype(BF16)

    return pl.pallas_call(
        body, name="sgu_fwd", grid=(t // SGU_TILE,),
        in_specs=[_rows(SGU_TILE, 2 * SGU_WIDTH), _full((1, 2 * SGU_WIDTH)), _full((1, SGU_WIDTH)),
                  _full((1, SGU_WIDTH)), _full(ws.shape), _full(bs_t.shape)],
        out_specs=_rows(SGU_TILE, SGU_WIDTH), out_shape=jax.ShapeDtypeStruct((t, SGU_WIDTH), BF16),
        compiler_params=_params(1),
    )(uv, uvb, lnw, lnb, ws, bs_t)


def _sgu_bwd(dyb, uv, uvb, lnw, lnb, ws, bs_t, e_groups):
    t = uv.shape[0]

    def body(dyb_ref, uv_ref, uvb_ref, lnw_ref, lnb_ref, ws_ref, bs_ref, e_ref,
             duv_ref, duvb_ref, dlnw_ref, dlnb_ref, dws_ref, dbs_ref):
        @pl.when(pl.program_id(0) == 0)
        def _():
            for r in (duvb_ref, dlnw_ref, dlnb_ref, dws_ref, dbs_ref):
                r[...] = jnp.zeros_like(r)

        (u, vn), act_vjp = jax.vjp(_sgu_act, uv_ref[...].astype(F32), uvb_ref[...], lnw_ref[...], lnb_ref[...])
        wc, causal = _sgu_weights(ws_ref)
        bs = bs_ref[...]
        dyb = dyb_ref[...]
        du_rows, dvn_rows = [], []
        for ck in range(SGU_TILE // CHUNK):
            rs = slice(ck * CHUNK, (ck + 1) * CHUNK)
            du, dvn, dmix = [], [], []
            for g in range(SGU_GROUPS):
                gs = slice(g * LANES, (g + 1) * LANES)
                vb = vn[rs, gs].astype(BF16)
                mixed = _dot(wc[g], vb) + bs[:, g:g + 1]
                dm = dyb[rs, gs] * u[rs, gs]
                dmb = dm.astype(BF16)
                du.append(dyb[rs, gs] * mixed)
                dvn.append(_dot_tn(wc[g], dmb))
                dws_ref[g] += jnp.where(causal, _dot_nt(dmb, vb), 0.0)
                dmix.append(dm)
            dbs_ref[...] += _dot_split(jnp.concatenate(dmix, axis=1), e_ref[...])
            du_rows.append(jnp.concatenate(du, axis=1))
            dvn_rows.append(jnp.concatenate(dvn, axis=1))
        duv, duvb, dlnw, dlnb = act_vjp((jnp.concatenate(du_rows, axis=0), jnp.concatenate(dvn_rows, axis=0)))
        duv_ref[...] = duv.astype(BF16)
        duvb_ref[...] += duvb
        dlnw_ref[...] += dlnw
        dlnb_ref[...] += dlnb

    return pl.pallas_call(
        body, name="sgu_bwd", grid=(t // SGU_TILE,),
        in_specs=[_rows(SGU_TILE, SGU_WIDTH), _rows(SGU_TILE, 2 * SGU_WIDTH), _full((1, 2 * SGU_WIDTH)),
                  _full((1, SGU_WIDTH)), _full((1, SGU_WIDTH)), _full(ws.shape), _full(bs_t.shape),
                  _full(e_groups.shape)],
        out_specs=[_rows(SGU_TILE, 2 * SGU_WIDTH), _full((1, 2 * SGU_WIDTH)), _full((1, SGU_WIDTH)),
                   _full((1, SGU_WIDTH)), _full(ws.shape), _full(bs_t.shape)],
        out_shape=[jax.ShapeDtypeStruct((t, 2 * SGU_WIDTH), BF16), jax.ShapeDtypeStruct((1, 2 * SGU_WIDTH), F32),
                   jax.ShapeDtypeStruct((1, SGU_WIDTH), F32), jax.ShapeDtypeStruct((1, SGU_WIDTH), F32),
                   jax.ShapeDtypeStruct(ws.shape, F32), jax.ShapeDtypeStruct(bs_t.shape, F32)],
        compiler_params=_params(1),
    )(dyb, uv, uvb, lnw, lnb, ws, bs_t, e_groups)


def _merge(gates, pa, pb, bg):
    s = _sigmoid(gates + bg)
    return s[:, :D_MODEL] * pa + s[:, D_MODEL:] * pb


def _branches_merge(branches, gates, bg):
    pa, pb = branches
    return (pa, pb, _merge(gates.astype(F32), pa, pb, bg)), ()


def _merge_backward(dmix, gates, pa, pb, bg):
    _, vjp = jax.vjp(_merge, gates.astype(F32), pa.astype(F32), pb.astype(F32), bg)
    dg, dpa, dpb, dbg = vjp(dmix)
    return (dg, dpa, dpb), (dbg,)


def _conv_f_fwd(up, cw, cb, tm=256):
    t, c = up.shape

    def body(x_ref, h_ref, w_ref, b_ref, o_ref, y_ref):
        halo = jnp.where(pl.program_id(0) > 0, h_ref[...].astype(F32)[8:], 0.0)
        y = _causal_conv(x_ref[...].astype(F32), halo, w_ref[...], b_ref[...])
        y_ref[...] = y.astype(BF16)
        o_ref[...] = (_silu(y[:, :D_FF]) * y[:, D_FF:]).astype(BF16)

    return pl.pallas_call(
        body, name="conv_f_fwd", grid=(t // tm,),
        in_specs=[_rows(tm, c), _halo(tm, c, rows=16), _full(cw.shape), _full((1, c))],
        out_specs=[_rows(tm, D_FF), _rows(tm, c)],
        out_shape=[jax.ShapeDtypeStruct((t, D_FF), BF16), jax.ShapeDtypeStruct((t, c), BF16)],
        compiler_params=_params(1),
    )(up, up, cw, cb)


def _conv_f_bwd(dact, y, up, cw, tm=128):
    t, c = up.shape
    nt = t // tm

    def body(d_ref, y_ref, x_ref, w_ref, dx_ref, dw_ref, db_ref, nxt_scr):
        @pl.when(pl.program_id(0) == 0)
        def _():
            nxt_scr[...] = jnp.zeros_like(nxt_scr)
            dw_ref[...] = jnp.zeros_like(dw_ref)
            db_ref[...] = jnp.zeros_like(db_ref)

        cb = D_FF // 2
        for k in range(D_FF // cb):
            cols_a, cols_v = slice(k * cb, (k + 1) * cb), slice(D_FF + k * cb, D_FF + (k + 1) * cb)
            a, v = y_ref[:, cols_a].astype(F32), y_ref[:, cols_v].astype(F32)
            d = d_ref[:, cols_a].astype(F32)
            silu_a, dsilu_a = _silu_and_grad(a)
            for cols, dy in ((cols_a, d * v * dsilu_a), (cols_v, d * silu_a)):
                dx, dw = _causal_conv_bwd(dy, nxt_scr[:, cols], x_ref[:, cols].astype(F32), w_ref[:, cols])
                dx_ref[:, cols] = dx.astype(BF16)
                nxt_scr[:, cols] = dy[:8]
                dw_ref[:, cols] += dw
                db_ref[:, cols] += _colsum(dy)

    return pl.pallas_call(
        body, name="conv_f_bwd", grid=(nt,),
        in_specs=[_rows(tm, D_FF, nt, True), _rows(tm, c, nt, True), _rows(tm, c, nt, True), _full(cw.shape)],
        out_specs=[_rows(tm, c, nt, True), _full(cw.shape), _full((1, c))],
        out_shape=[jax.ShapeDtypeStruct((t, c), BF16), jax.ShapeDtypeStruct(cw.shape, F32),
                   jax.ShapeDtypeStruct((1, c), F32)],
        scratch_shapes=[pltpu.VMEM((8, c), F32)], compiler_params=_params(1),
    )(dact, y, up, cw)


def _conv_a_bwd(dxs, db, dc, y, xbc, cw, tm=256):
    t, c = xbc.shape
    nt = t // tm

    def body(dxs_ref, db_ref, dc_ref, y_ref, x_ref, w_ref, dx_ref, dw_ref, dbias_ref, nxt_scr):
        @pl.when(pl.program_id(0) == 0)
        def _():
            nxt_scr[...] = jnp.zeros_like(nxt_scr)
            dw_ref[...] = jnp.zeros_like(dw_ref)
            dbias_ref[...] = jnp.zeros_like(dbias_ref)

        dy = jnp.concatenate([dxs_ref[...], db_ref[...], dc_ref[...]], axis=1) * _dsilu(y_ref[...].astype(F32))
        dx, dw = _causal_conv_bwd(dy, nxt_scr[...], x_ref[...].astype(F32), w_ref[...])
        dx_ref[...] = dx.astype(BF16)
        nxt_scr[...] = dy[:8]
        dw_ref[...] += dw
        dbias_ref[...] += _colsum(dy)

    return pl.pallas_call(
        body, name="conv_a_bwd", grid=(nt,),
        in_specs=[_rows(tm, SSD_INNER, nt, True), _rows(tm, SSD_BC, nt, True), _rows(tm, SSD_BC, nt, True),
                  _rows(tm, c, nt, True), _rows(tm, c, nt, True), _full(cw.shape)],
        out_specs=[_rows(tm, c, nt, True), _full(cw.shape), _full((1, c))],
        out_shape=[jax.ShapeDtypeStruct((t, c), BF16), jax.ShapeDtypeStruct(cw.shape, F32),
                   jax.ShapeDtypeStruct((1, c), F32)],
        scratch_shapes=[pltpu.VMEM((8, c), F32)], compiler_params=_params(1),
    )(dxs, db, dc, y, xbc, cw)


def _pad_lanes(v, n=DT_PAD):
    return jnp.pad(v, ((0, 0), (0, n - v.shape[1])))


def _local_step(x, target, w, p, after=None, late_weights=None, on_grad=None, on_small=None):
    dtb, alog, dsk = _pad_lanes(p["dt_bias"]), _pad_lanes(p["a_log"]), _pad_lanes(p["d_skip"])
    bs_t = _pad_lanes(p["b_spatial"].T)
    e_heads = (jnp.arange(SSD_INNER)[:, None] // SSD_HEAD_DIM == jnp.arange(LANES)[None, :]).astype(BF16)
    e_heads_t = (jnp.arange(LANES)[:, None] == jnp.arange(SSD_INNER)[None, :] // SSD_HEAD_DIM).astype(BF16)
    e_groups = (jnp.arange(SGU_WIDTH)[:, None] // LANES == jnp.arange(LANES)[None, :]).astype(BF16)

    n1 = _norm_fwd(x, p["norm1_w"], "norm1_fwd", after=after)
    z = _mm(n1, w["z"], "nt", "proj_z", out_dtype=BF16)
    xbc = _mm(n1, w["xbc"], "nt", "proj_xbc", out_dtype=BF16)
    dtr = _mm(n1, w["dt"], "nt", "proj_dt")
    uv = _mm(n1, w["uv"], "nt", "proj_uv", out_dtype=BF16)
    gates = _mm(n1, w["gates"], "nt", "proj_gates", out_dtype=BF16)
    xc, conv_a_out = _conv_a_fwd(xbc, w["conv_a"], p["conv_a_b"])
    y, ya, sprev = _ssd_fwd(xc, dtr, z, dtb, alog, dsk, p["ssd_norm_w"], e_heads_t)
    yb = _sgu_fwd(uv, p["uv_b"], p["v_ln_w"], p["v_ln_b"], p["w_spatial"], bs_t)
    if late_weights is not None:
        w = {**w, **late_weights(ya, yb)}
    narrow = (D_MODEL, BF16)
    pa, pb, mix = _mm_rows(
        [ya, yb], [w["branch_a"], w["branch_b"]], "nn", "branches", _branches_merge, rows=[gates],
        fulls=[p["b_gate"]], row_outs=[narrow] * 3, summed=False)
    wide = [(D_MODEL, F32), (D_MODEL, BF16)]
    h1, n2 = _mm_rows(mix, w["out"], "nn", "out_proj", _residual_norm, rows=[x], fulls=[p["norm2_w"]], row_outs=wide)
    up = _mm(n2, w["up"], "nt", "up_proj", out_dtype=BF16)
    act, conv_f_out = _conv_f_fwd(up, w["conv_f"], p["conv_f_b"])
    dh2, dh2b, loss, g_final = _mm_rows(
        act, w["down"], "nn", "down_proj", _loss_and_grad, rows=[h1, target], fulls=[p["final_norm_w"]],
        row_outs=wide, acc_outs=[(8, LANES), (1, D_MODEL)])

    on_grad = on_grad or (lambda name, grads: None)
    g = {"final_norm_w": g_final}
    g["down"] = _wgrad(act, dh2b, "down_wgrad")
    tok = on_grad("w_down", g)
    dact = _mm(dh2b, w["down"], "nt", "down_dgrad", out_dtype=BF16, after=tok)
    dup, g["conv_f"], g["conv_f_b"] = _conv_f_bwd(dact, conv_f_out, up, w["conv_f"])
    g["up"] = _wgrad(dup, n2, "up_wgrad")
    tok = on_grad("w_up", g)
    dh1, dh1b, g["norm2_w"] = _mm_rows(
        dup, w["up"], "nn", "up_dgrad", _norm_backward, rows=[h1, dh2], fulls=[p["norm2_w"]], row_outs=wide,
        acc_outs=[(1, D_MODEL)], after=tok)
    g["out"] = _wgrad(mix, dh1b, "out_wgrad")
    tok = on_grad("w_out", g)
    dgates, dpa, dpb, g["b_gate"] = _mm_rows(
        dh1b, w["out"], "nt", "out_dgrad", _merge_backward, rows=[gates, pa, pb], fulls=[p["b_gate"]],
        row_outs=[(2 * D_MODEL, BF16), (D_MODEL, BF16), (D_MODEL, BF16)], acc_outs=[(1, 2 * D_MODEL)], after=tok)
    g["branch_a"] = _wgrad(ya, dpa, "branch_a_wgrad")
    g["branch_b"] = _wgrad(yb, dpb, "branch_b_wgrad")
    tok = on_grad("w_branch", g)
    dya, dyb = _mm_rows(
        [dpa, dpb], [w["branch_a"], w["branch_b"]], "nt", "branches_dgrad", lambda products: (tuple(products), ()),
        row_outs=[(SSD_INNER, F32), (SGU_WIDTH, F32)], after=tok, summed=False)
    duv, g["uv_b"], g["v_ln_w"], g["v_ln_b"], g["w_spatial"], dbs_t = _sgu_bwd(
        dyb, uv, p["uv_b"], p["v_ln_w"], p["v_ln_b"], p["w_spatial"], bs_t, e_groups)
    g["b_spatial"] = dbs_t[:, :SGU_GROUPS].T
    dz, dxs, db, dc, ddtr, g["ssd_norm_w"], ddtb, dalog, ddsk = _ssd_bwd(
        dya, y, z, xc, dtr, sprev, dtb, alog, dsk, p["ssd_norm_w"], e_heads, e_heads_t)
    g["dt_bias"], g["a_log"], g["d_skip"] = ddtb, dalog, ddsk
    dxbc, g["conv_a"], g["conv_a_b"] = _conv_a_bwd(dxs, db, dc, conv_a_out, xbc, w["conv_a"])
    tok = on_small(g, loss) if on_small else None
    ddtrb = ddtr.astype(BF16)
    for name, d in (("z", dz), ("xbc", dxbc), ("dt", ddtrb), ("uv", duv), ("gates", dgates)):
        g[name] = _wgrad(d, n1, name + "_wgrad", after=tok)
    tok = on_grad("w_in", g)
    dn1 = _mm([dz, dxbc], [w["z"], w["xbc"]], "nn", "ssd_dgrad", after=tok)
    gx, g["norm1_w"] = _mm_rows(
        [duv, dgates, ddtrb], [w["uv"], w["gates"], w["dt"]], "nn", "in_dgrad",
        lambda r, so_far, h, dres, w_: tuple(t[:1] for t in _norm_backward(r + so_far, h, dres, w_)),
        rows=[dn1, x, dh1], fulls=[p["norm1_w"]], row_outs=wide[:1], acc_outs=[(1, D_MODEL)])
    return loss, gx, g


def _place():
    return lax.axis_index("x"), lax.axis_index("y"), lax.axis_index("c")


def _other_chips(x, y):
    return [(1 - x, y), (x, 1 - y), (1 - x, 1 - y)]


def _all_gather(shards, name):
    n = len(shards)

    def body(*refs):
        ins, outs = refs[:n], refs[n:2 * n]
        send_sems, recv_sems, local_sems = refs[2 * n:]
        x, y, c = _place()
        me, sibling = (x, y, c), (x, y, 1 - c)
        chips = _other_chips(x, y)

        def copy(a, k, block, to, src=None):
            slot = outs[a].at[4 * block[0] + 2 * block[1] + block[2]]
            return pltpu.make_async_remote_copy(
                src_ref=slot if src is None else src, dst_ref=slot, send_sem=send_sems.at[7 * a + k],
                recv_sem=recv_sems.at[7 * a + k], device_id=to, device_id_type=MESH)

        started = []
        for a in range(n):
            mine = pltpu.make_async_copy(ins[a], outs[a].at[4 * x + 2 * y + c], local_sems.at[a])
            mine.start()
            started.append(mine)
        sends = []
        for a in range(n):
            sends.append(copy(a, 0, me, sibling, src=ins[a]))
            sends += [copy(a, 1 + j, me, (*chip, c), src=ins[a]) for j, chip in enumerate(chips)]
        for cp in sends:
            cp.start()
        for a in range(n):
            for j, chip in enumerate(chips):
                copy(a, 1 + j, (*chip, c), me).wait_recv()
                fwd = copy(a, 4 + j, (*chip, c), sibling)
                fwd.start()
                sends.append(fwd)
        for a in range(n):
            copy(a, 0, sibling, me).wait_recv()
            for j, chip in enumerate(chips):
                copy(a, 4 + j, (*chip, 1 - c), me).wait_recv()
        for cp in sends:
            cp.wait_send()
        for mine in started:
            mine.wait()

    any_spec = pl.BlockSpec(memory_space=pl.ANY)
    return pl.pallas_call(
        body, name=name, in_specs=[any_spec] * n, out_specs=[any_spec] * n,
        out_shape=[jax.ShapeDtypeStruct((N_DEV, *s.shape), s.dtype) for s in shards],
        scratch_shapes=[pltpu.SemaphoreType.DMA((7 * n,)), pltpu.SemaphoreType.DMA((7 * n,)),
                        pltpu.SemaphoreType.DMA((n,))],
    )(*shards)


HBM_SPEC = pl.BlockSpec(memory_space=pltpu.HBM)
SEM_SPEC = pl.BlockSpec(memory_space=pltpu.SEMAPHORE)
ANY_SPEC = pl.BlockSpec(memory_space=pl.ANY)
DATAFLOW = pltpu.SideEffectType.DATAFLOW_SIDE_EFFECTING
N_PEERS = N_DEV - 1


def _peers(x, y, c):
    out = []
    for r in range(1, N_DEV):
        fx, fy, fc = r >> 2 & 1, r >> 1 & 1, r & 1
        out.append(((1 - x) if fx else x, (1 - y) if fy else y, (1 - c) if fc else c))
    return out


def _gather_copies(srcs, lands, send_sems, recv_sems, sending, scatter=False):
    x, y, c = _place()
    copies = []
    for a, (src, land) in enumerate(zip(srcs, lands)):
        for j, (px, py, pc) in enumerate(_peers(x, y, c)):
            mine, theirs = 4 * x + 2 * y + c, 4 * px + 2 * py + pc
            block = src.at[theirs if sending else 0] if scatter else src
            copies.append(pltpu.make_async_remote_copy(
                src_ref=block, dst_ref=land.at[mine if sending else theirs], send_sem=send_sems.at[N_PEERS * a + j],
                recv_sem=recv_sems.at[N_PEERS * a + j], device_id=(px, py, pc), device_id_type=MESH))
    return copies


def _gather_start(shards, after, name, scatter=False):
    n = len(shards)
    after = [] if after is None else [after]

    def body(*refs):
        srcs, lands = refs[:n], refs[n:2 * n]
        send_sems, recv_sems = refs[2 * n + len(after):2 * n + len(after) + 2]
        token = refs[-1]
        for cp in _gather_copies(srcs, lands, send_sems, recv_sems, sending=True, scatter=scatter):
            cp.start()
        token[...] = jnp.zeros_like(token)

    lands = [lax.empty(s.shape if scatter else (N_DEV, *s.shape), s.dtype) for s in shards]
    hbm = lambda a: pltpu.with_memory_space_constraint(a, pltpu.HBM)
    out = pl.pallas_call(
        body, name=name,
        out_shape=(pltpu.SemaphoreType.DMA((N_PEERS * n,)), pltpu.SemaphoreType.DMA((N_PEERS * n,)),
                   *[pltpu.HBM(a.shape, a.dtype) for a in (*shards, *lands)], jax.ShapeDtypeStruct((8, LANES), F32)),
        in_specs=[HBM_SPEC] * (2 * n) + [ANY_SPEC] * len(after),
        out_specs=(SEM_SPEC, SEM_SPEC, *[HBM_SPEC] * (2 * n), pl.BlockSpec(memory_space=pltpu.VMEM)),
        input_output_aliases={i: 2 + i for i in range(2 * n)},
        compiler_params=pltpu.CompilerParams(has_side_effects=DATAFLOW),
    )(*[hbm(a) for a in (*shards, *lands)], *after)
    return out[0], out[1], out[2:2 + n], out[2 + n:2 + 2 * n], out[-1]


def _gather_wait(send_sems, recv_sems, shards, lands, after, name, scatter=False):
    n = len(shards)
    after = tuple(after)

    def body(*refs):
        srcs, lands_ = refs[:n], refs[n:2 * n]
        send, recv = refs[2 * n:2 * n + 2]
        for cp in _gather_copies(srcs, lands_, send, recv, sending=False, scatter=scatter):
            cp.wait_send()
            cp.wait_recv()

    out = pl.pallas_call(
        body, name=name, out_shape=tuple(pltpu.HBM(a.shape, a.dtype) for a in (*shards, *lands)),
        in_specs=[HBM_SPEC] * (2 * n) + [SEM_SPEC, SEM_SPEC] + [ANY_SPEC] * len(after),
        out_specs=tuple([HBM_SPEC] * (2 * n)), input_output_aliases={i: i for i in range(2 * n)},
        compiler_params=pltpu.CompilerParams(has_side_effects=DATAFLOW),
    )(*shards, *lands, send_sems, recv_sems, *after)
    return out[:n], out[n:]


def _chip_copies(src, land, send_sems, recv_sems):
    x, y, c = _place()
    return [pltpu.make_async_remote_copy(
        src_ref=src.at[2 * cx + cy], dst_ref=land.at[j], send_sem=send_sems.at[j], recv_sem=recv_sems.at[j],
        device_id=(cx, cy, c), device_id_type=MESH) for j, (cx, cy) in enumerate(_other_chips(x, y))]


def _chips_start(q, name):
    def body(q_ref, land_ref, send_sems, recv_sems, q_thru, land_thru, token):
        for cp in _chip_copies(q_ref, land_ref, send_sems, recv_sems):
            cp.start()
        token[...] = jnp.zeros_like(token)

    land = lax.empty((3, *q.shape[1:]), q.dtype)
    return pl.pallas_call(
        body, name=name,
        out_shape=(pltpu.SemaphoreType.DMA((3,)), pltpu.SemaphoreType.DMA((3,)), pltpu.HBM(q.shape, q.dtype),
                   pltpu.HBM(land.shape, land.dtype), jax.ShapeDtypeStruct((8, LANES), F32)),
        in_specs=[HBM_SPEC, HBM_SPEC],
        out_specs=(SEM_SPEC, SEM_SPEC, HBM_SPEC, HBM_SPEC, pl.BlockSpec(memory_space=pltpu.VMEM)),
        input_output_aliases={0: 2, 1: 3}, compiler_params=pltpu.CompilerParams(has_side_effects=DATAFLOW),
    )(pltpu.with_memory_space_constraint(q, pltpu.HBM), pltpu.with_memory_space_constraint(land, pltpu.HBM))


def _chips_wait(send_sems, recv_sems, q, land, after, name):
    def body(q_ref, land_ref, send, recv, after_ref, q_out, land_out):
        for cp in _chip_copies(q_ref, land_ref, send, recv):
            cp.wait_send()
            cp.wait_recv()

    return pl.pallas_call(
        body, name=name, out_shape=(pltpu.HBM(q.shape, q.dtype), pltpu.HBM(land.shape, land.dtype)),
        in_specs=[HBM_SPEC, HBM_SPEC, SEM_SPEC, SEM_SPEC, ANY_SPEC], out_specs=(HBM_SPEC, HBM_SPEC),
        input_output_aliases={0: 0, 1: 1}, compiler_params=pltpu.CompilerParams(has_side_effects=DATAFLOW),
    )(q, land, send_sems, recv_sems, after)[1]


def _exchange_cores(part, name):
    def body(in_ref, out_ref, send_sems, recv_sems):
        x, y, c = _place()
        copies = [pltpu.make_async_remote_copy(
            src_ref=in_ref.at[2 * k + (1 - c)], dst_ref=out_ref.at[k], send_sem=send_sems.at[k],
            recv_sem=recv_sems.at[k], device_id=(x, y, 1 - c), device_id_type=MESH) for k in range(4)]
        for cp in copies:
            cp.start()
        for cp in copies:
            cp.wait()

    return pl.pallas_call(
        body, name=name, in_specs=[ANY_SPEC], out_specs=ANY_SPEC,
        out_shape=jax.ShapeDtypeStruct((4, *part.shape[1:]), part.dtype),
        scratch_shapes=[pltpu.SemaphoreType.DMA((4,)), pltpu.SemaphoreType.DMA((4,))],
    )(part)


def _chip_sum(part, got, place, name, tr=256):
    _, r, c = part.shape
    tr, tc = _tile2d(r, c, tr)

    def body(place_ref, p_ref, g_ref, q_ref, own_ref):
        s = p_ref[0].astype(F32) + g_ref[0].astype(F32)
        q_ref[0] = s.astype(BF16)

        @pl.when(pl.program_id(2) == place_ref[1])
        def _():
            own_ref[...] = s

    grid_spec = pltpu.PrefetchScalarGridSpec(
        num_scalar_prefetch=1, grid=(r // tr, c // tc, 4),
        in_specs=[pl.BlockSpec((1, tr, tc), lambda i, j, k, pr: (2 * k + pr[0], i, j)),
                  pl.BlockSpec((1, tr, tc), lambda i, j, k, pr: (k, i, j))],
        out_specs=[pl.BlockSpec((1, tr, tc), lambda i, j, k, pr: (k, i, j)),
                   pl.BlockSpec((tr, tc), lambda i, j, k, pr: (i, j))])
    return pl.pallas_call(
        body, name=name, grid_spec=grid_spec,
        out_shape=[jax.ShapeDtypeStruct((4, r, c), BF16), jax.ShapeDtypeStruct((r, c), F32)],
        compiler_params=_params(3),
    )(place, part, got)


def _sum_adamw(own, got, w, m, v, name):
    r, c = own.shape
    tc = 4 * LANES

    def body(own_ref, got_ref, w_ref, m_ref, v_ref, g_ref, d_ref, nm_ref, nv_ref):
        g = own_ref[...]
        for j in range(3):
            g = g + got_ref[j].astype(F32)
        two_d = lambda ref: ref[...].reshape(r, tc)
        delta, nm, nv = _adamw(two_d(w_ref), g, two_d(m_ref), two_d(v_ref))
        for ref, val in ((g_ref, g), (d_ref, delta), (nm_ref, nm), (nv_ref, nv)):
            ref[...] = val.reshape(ref.shape)

    wblk = pl.BlockSpec((r, 1, tc), lambda j: (0, 0, j))
    return pl.pallas_call(
        body, name=name, grid=(c // tc,),
        in_specs=[pl.BlockSpec((r, tc), lambda j: (0, j)), pl.BlockSpec((3, r, tc), lambda j: (0, 0, j)),
                  wblk, wblk, wblk],
        out_specs=[wblk] * 4, out_shape=[jax.ShapeDtypeStruct(w.shape, F32)] * 4, compiler_params=_params(1),
    )(own, got, w, m, v)


def _adamw(w, g, m, v):
    m = ADAM_B1 * m + (1.0 - ADAM_B1) * g
    v = ADAM_B2 * v + (1.0 - ADAM_B2) * jnp.square(g)
    m_hat = m / (1.0 - ADAM_B1 ** ADAM_STEP)
    v_hat = v / (1.0 - ADAM_B2 ** ADAM_STEP)
    return -ADAM_LR * (m_hat / (jnp.sqrt(v_hat) + ADAM_EPS) + ADAM_WD * w), m, v


def _sum8_adamw(part, got, place, w, m, v, name, tr=256):
    r, c = w.shape
    tr, tc = _tile2d(r, c, tr)
    blk = pl.BlockSpec((tr, tc), lambda i, j, pr: (i, j))

    def body(place_ref, own_ref, got_ref, w_ref, m_ref, v_ref, g_ref, d_ref, nm_ref, nv_ref):
        dev = 2 * place_ref[1] + place_ref[0]
        g = jnp.zeros((tr, tc), F32)
        for d in range(N_DEV):
            g = g + jnp.where(dev == d, own_ref[0], got_ref[d]).astype(F32)
        g_ref[...] = g
        d_ref[...], nm_ref[...], nv_ref[...] = _adamw(w_ref[...], g, m_ref[...], v_ref[...])

    grid_spec = pltpu.PrefetchScalarGridSpec(
        num_scalar_prefetch=1, grid=(r // tr, c // tc),
        in_specs=[pl.BlockSpec((1, tr, tc), lambda i, j, pr: (2 * pr[1] + pr[0], i, j)),
                  pl.BlockSpec((N_DEV, tr, tc), lambda i, j, pr: (0, i, j)), blk, blk, blk],
        out_specs=[blk] * 4)
    return pl.pallas_call(
        body, name=name, grid_spec=grid_spec, out_shape=[jax.ShapeDtypeStruct(w.shape, F32)] * 4,
        compiler_params=_params(2),
    )(place, part, got, w, m, v)


VECTORS = ["norm1_w", "b_gate", "conv_a_b", "dt_bias", "a_log", "d_skip", "ssd_norm_w", "uv_b", "v_ln_w", "v_ln_b",
           "norm2_w", "conv_f_b", "final_norm_w"]
SMALL_ORDER = VECTORS + ["w_spatial", "b_spatial", "conv_a_w", "conv_f_w"]


ROW_VECTORS = VECTORS[1:]


def _small_adamw(gathered, w, m, v):
    sizes = {n: w[n].shape[1] for n in ROW_VECTORS}
    offs, off = {}, 0
    for n in ROW_VECTORS:
        offs[n] = off
        off += -(-sizes[n] // LANES) * LANES
    loss_off = off
    k = len(SMALL_ORDER)
    n_g = len(gathered)

    def body(*refs):
        row_ref, ws_ref, bs_ref, ca_ref, cf_ref, n1_ref = refs[:n_g]
        w_refs, m_refs, v_refs = (dict(zip(SMALL_ORDER, refs[n_g + i * k:n_g + (i + 1) * k])) for i in range(3))
        outs = refs[n_g + 3 * k:]
        x, y, c = _place()
        dev = 4 * x + 2 * y + c

        def total(ref):
            s = ref[0]
            for d in range(1, N_DEV):
                s = s + ref[d]
            return s

        row = total(row_ref)
        grads = {n: row[:, offs[n]:offs[n] + sizes[n]] for n in ROW_VECTORS}
        grads["norm1_w"], grads["w_spatial"], grads["b_spatial"] = total(n1_ref), total(ws_ref), total(bs_ref)
        for n, ref in (("conv_a_w", ca_ref), ("conv_f_w", cf_ref)):
            whole, cols = total(ref), w_refs[n].shape[1]
            mine = whole[:, :cols]
            for d in range(1, N_DEV):
                mine = jnp.where(dev == d, whole[:, d * cols:(d + 1) * cols], mine)
            grads[n] = mine
        for i, n in enumerate(SMALL_ORDER):
            outs[4 * i][...] = grads[n]
            outs[4 * i + 1][...], outs[4 * i + 2][...], outs[4 * i + 3][...] = _adamw(
                w_refs[n][...], grads[n], m_refs[n][...], v_refs[n][...])
        outs[4 * k][...] = row[:, loss_off:loss_off + LANES]

    out = pl.pallas_call(
        body, name="adamw_small",
        out_shape=[jax.ShapeDtypeStruct(w[n].shape, F32) for n in SMALL_ORDER for _ in range(4)]
        + [jax.ShapeDtypeStruct((1, LANES), F32)],
        compiler_params=_params(0),
    )(*gathered, *[t[n] for t in (w, m, v) for n in SMALL_ORDER])
    return [dict(zip(SMALL_ORDER, out[j:4 * k:4])) for j in range(4)] + [out[4 * k]]


SMALL = ["norm1_w", "b_gate", "conv_a_b", "dt_bias", "a_log", "d_skip", "ssd_norm_w", "uv_b", "v_ln_w", "v_ln_b",
         "w_spatial", "b_spatial", "norm2_w", "conv_f_b", "final_norm_w"]
BIG = ["w_in", "w_branch", "w_out", "w_up", "w_down"]
TRANSPOSED = ("w_in", "w_up")
WEIGHTS = ["norm1_w", "w_in", "b_gate", "conv_a_w", "conv_a_b", "dt_bias", "a_log", "d_skip", "ssd_norm_w", "uv_b",
           "v_ln_w", "v_ln_b", "w_spatial", "b_spatial", "w_branch", "w_out", "norm2_w", "w_up", "conv_f_w",
           "conv_f_b", "w_down", "final_norm_w"]
IN_SPLITS = [("z", 0, 2048), ("xbc", 2048, 5120), ("dt", 5120, 5152), ("uv", 5152, 7200), ("gates", 7200, 9248)]


def _columns_from_devices(a):
    return a.transpose(1, 0, 2).reshape(a.shape[1], -1)


def kernel(x, norm1_w, w_in, b_gate, conv_a_w, conv_a_b, dt_bias, a_log, d_skip, ssd_norm_w, uv_b, v_ln_w, v_ln_b, w_spatial, b_spatial, w_branch, w_out, norm2_w, w_up, conv_f_w, conv_f_b, w_down, final_norm_w, loss_target, m_norm1_w, m_w_in, m_b_gate, m_conv_a_w, m_conv_a_b, m_dt_bias, m_a_log, m_d_skip, m_ssd_norm_w, m_uv_b, m_v_ln_w, m_v_ln_b, m_w_spatial, m_b_spatial, m_w_branch, m_w_out, m_norm2_w, m_w_up, m_conv_f_w, m_conv_f_b, m_w_down, m_final_norm_w, v_norm1_w, v_w_in, v_b_gate, v_conv_a_w, v_conv_a_b, v_dt_bias, v_a_log, v_d_skip, v_ssd_norm_w, v_uv_b, v_v_ln_w, v_v_ln_b, v_w_spatial, v_b_spatial, v_w_branch, v_w_out, v_norm2_w, v_w_up, v_conv_f_w, v_conv_f_b, v_w_down, v_final_norm_w):
    args = dict(locals())
    wts = {n: args[n] for n in WEIGHTS}
    mom = {n: args["m_" + n] for n in WEIGHTS}
    var = {n: args["v_" + n] for n in WEIGHTS}
    cx, cy, cc = _place()
    dev = 4 * cx + 2 * cy + cc
    place = jnp.stack([cc, 2 * cx + cy]).astype(jnp.int32)

    def shard2d(n, a):
        return a[0].T if n in TRANSPOSED else a[0]

    def unshard(n, b):
        return (b.T if n in TRANSPOSED else b)[None]

    g_in, g_conv_a, g_conv_f = _all_gather(
        [shard2d("w_in", w_in).astype(BF16), conv_a_w[0], conv_f_w[0]], "gather_w_in")
    late = [shard2d(n, wts[n]).astype(BF16) for n in BIG[1:]]
    send_sems, recv_sems, late, lands, token = _gather_start(late, g_in, "gather_late_start")
    w_in_rows = g_in.reshape(-1, D_MODEL)
    w = {name: w_in_rows[lo:hi] for name, lo, hi in IN_SPLITS}
    w["dt"] = jnp.pad(w["dt"], ((0, DT_PAD - SSD_HEADS), (0, 0)))
    w["conv_a"] = _columns_from_devices(g_conv_a)
    w["conv_f"] = _columns_from_devices(g_conv_f)

    def late_weights(*after):
        mine, got = _gather_wait(send_sems, recv_sems, late, lands, after, "gather_late_wait")
        g_branch, g_out, g_up, g_down = [lax.dynamic_update_index_in_dim(land, own, dev, 0).reshape(-1, D_MODEL)
                                         for land, own in zip(got, mine)]
        return {"branch_a": g_branch[:SSD_INNER], "branch_b": g_branch[SSD_INNER:], "out": g_out, "up": g_up,
                "down": g_down}

    in_flight = {}

    def on_grad(n, g):
        part = {"w_in": lambda: jnp.concatenate([g[name][:hi - lo] for name, lo, hi in IN_SPLITS], axis=0),
                "w_branch": lambda: jnp.concatenate([g["branch_a"], g["branch_b"]], axis=0),
                "w_out": lambda: g["out"], "w_up": lambda: g["up"], "w_down": lambda: g["down"]}[n]()
        part = part.reshape(N_DEV, -1, D_MODEL)
        if n == "w_in":
            q, own = _chip_sum(part, _exchange_cores(part, "to_other_core_w_in"), place, "chip_sum_w_in")
            send, recv, q, land, tok = _chips_start(q, "to_other_chips_start_w_in")
            in_flight[n] = (own, send, recv, q, land)
            return tok
        send, recv, (part,), (land,), tok = _gather_start([part], None, f"to_owners_start_{n}", scatter=True)
        in_flight[n] = (part, send, recv, land)
        return tok

    p = {n: wts[n][0] if wts[n].ndim > 2 else wts[n].reshape(1, -1) for n in SMALL}
    small_flight = []

    def on_small(g, loss):
        arrays = [jnp.concatenate([g[n] for n in ROW_VECTORS] + [loss[:1]], axis=1), g["w_spatial"], g["b_spatial"],
                  g["conv_a"], g["conv_f"]]
        *flight, tok = _gather_start(arrays, g["conv_a"], "gather_small_start")
        small_flight.append(flight)
        return tok

    loss, gx, g = _local_step(x[0], loss_target[0], w, p, after=token, late_weights=late_weights, on_grad=on_grad,
                              on_small=on_small)
    *flight, _ = _gather_start([g["norm1_w"]], gx, "gather_norm1_start")
    small_flight.append(flight)

    grads, delta, new_m, new_v = {}, {}, {}, {}

    def big_adamw(n, after):
        if n == "w_in":
            own, send, recv, q, land = in_flight[n]
            got = _chips_wait(send, recv, q, land, after, "to_other_chips_wait_w_in")
            out = _sum_adamw(own, got, *[t[n].transpose(2, 0, 1) for t in (wts, mom, var)], "adamw_w_in")
            grads[n], delta[n], new_m[n], new_v[n] = [o.transpose(1, 2, 0) for o in out]
            return out[1]
        part, send, recv, land = in_flight[n]
        (part,), (got,) = _gather_wait(send, recv, [part], [land], [after], f"to_owners_wait_{n}", scatter=True)
        out = _sum8_adamw(part, got, place, *[shard2d(n, t[n]) for t in (wts, mom, var)], f"adamw_{n}")
        grads[n], delta[n], new_m[n], new_v[n] = [unshard(n, o) for o in out]
        return out[1]

    after = gx
    for n in ("w_down", "w_up", "w_out", "w_branch"):
        after = big_adamw(n, after)
    gathered = []
    for (send, recv, mine, land), name in zip(small_flight, ("gather_small_wait", "gather_norm1_wait")):
        mine, got = _gather_wait(send, recv, mine, land, [after], name)
        gathered += [lax.dynamic_update_index_in_dim(full, own, dev, 0) for full, own in zip(got, mine)]
    small = [{n: t[n][0] if t[n].ndim > 2 else t[n].reshape(1, -1) for n in SMALL_ORDER} for t in (wts, mom, var)]
    *outs, loss = _small_adamw(gathered, *small)
    for tgt, out in zip((grads, delta, new_m, new_v), outs):
        tgt.update({n: out[n].reshape(wts[n].shape) for n in SMALL_ORDER})
    big_adamw("w_in", loss)
    loss = loss[0, 0]

    return (loss, gx[None], *[grads[n] for n in WEIGHTS], *[delta[n] for n in WEIGHTS],
            *[new_m[n] for n in WEIGHTS], *[new_v[n] for n in WEIGHTS])
```
